```python
import math
import jax, jax.numpy as jnp
from jax import lax
import numpy as np

D_MODEL = 1024
BATCH = 16
SEQ = 2048
DEPTH = 1

D_MIX = D_MODEL
CONV_CH = D_MIX // 2
CONV_GROUPS = 8
CONV_K = 31
FOX_HEADS = 8
FOX_HEAD_DIM = 64
FOX_W = FOX_HEADS * FOX_HEAD_DIM
Q_BLOCK = 128
MEM_LEN = 256
MEM_HEADS = 4
MEM_HEAD_DIM = D_MODEL // MEM_HEADS
D_FF = ((8 * D_MODEL // 3 + 255) // 256) * 256
EPS = 1e-6

OFF_U = 0
OFF_G = OFF_U + CONV_CH
OFF_Q = OFF_G + CONV_CH
OFF_K = OFF_Q + FOX_W
OFF_V = OFF_K + FOX_W
OFF_F = OFF_V + FOX_W
D_IN = OFF_F + FOX_HEADS

kernel_name = "hybrid_conformer_fox_memory_block"


def rmsnorm(x, g):
    xf = x.astype(jnp.float32)
    y = xf * lax.rsqrt(jnp.mean(xf * xf, axis=-1, keepdims=True) + EPS)
    return (y * g.astype(jnp.float32)).astype(x.dtype)


def layernorm(x, g, b):
    xf = x.astype(jnp.float32)
    mu = jnp.mean(xf, axis=-1, keepdims=True)
    xc = xf - mu
    y = xc * lax.rsqrt(jnp.mean(xc * xc, axis=-1, keepdims=True) + EPS)
    return (y * g.astype(jnp.float32) + b.astype(jnp.float32)).astype(x.dtype)


def conformer_conv(u, gate, conv_w, conv_b, ln_g, ln_b):
    a = u * jax.nn.sigmoid(gate)
    y = lax.conv_general_dilated(
        a, conv_w[:, None, :].astype(a.dtype),
        window_strides=(1,), padding=[(CONV_K - 1, 0)],
        dimension_numbers=("NWC", "WIO", "NWC"),
        feature_group_count=CONV_CH) + conv_b.astype(a.dtype)
    return jax.nn.silu(layernorm(y, ln_g, ln_b))


def forgetting_attention(q, k, v, logf):
    b, s, h, dh = q.shape
    scale = 1.0 / math.sqrt(dh)
    qh = jnp.transpose(q, (0, 2, 1, 3))
    kh = jnp.transpose(k, (0, 2, 1, 3))
    vh = jnp.transpose(v, (0, 2, 1, 3))
    c = jnp.transpose(jnp.cumsum(logf, axis=1), (0, 2, 1))
    outs = []
    for i in range(s // Q_BLOCK):
        q0, end = i * Q_BLOCK, (i + 1) * Q_BLOCK
        logits = jnp.einsum("bhqd,bhkd->bhqk", qh[:, :, q0:end], kh[:, :, :end],
                            preferred_element_type=jnp.float32) * scale
        logits = logits + (c[:, :, q0:end, None] - c[:, :, None, :end])
        causal = jnp.arange(end)[None, :] <= (q0 + jnp.arange(Q_BLOCK))[:, None]
        logits = jnp.where(causal[None, None], logits, -jnp.inf)
        p = jax.nn.softmax(logits, axis=-1)
        outs.append(jnp.einsum("bhqk,bhkd->bhqd", p.astype(vh.dtype), vh[:, :, :end]))
    o = jnp.concatenate(outs, axis=2)
    return jnp.transpose(o, (0, 2, 1, 3)).reshape(b, s, h * dh)


def memory_cross_attention(hx, mem_n, w_mq, w_mkv, w_mo):
    b, s, _ = hx.shape
    m = mem_n.shape[1]
    q = (hx @ w_mq).reshape(b, s, MEM_HEADS, MEM_HEAD_DIM)
    kv = mem_n @ w_mkv
    k = kv[..., :D_MODEL].reshape(b, m, MEM_HEADS, MEM_HEAD_DIM)
    v = kv[..., D_MODEL:].reshape(b, m, MEM_HEADS, MEM_HEAD_DIM)
    logits = jnp.einsum("bshd,bmhd->bhsm", q, k,
                        preferred_element_type=jnp.float32) / math.sqrt(MEM_HEAD_DIM)
    p = jax.nn.softmax(logits, axis=-1)
    o = jnp.einsum("bhsm,bmhd->bshd", p.astype(v.dtype), v).reshape(b, s, D_MODEL)
    return o @ w_mo


def _fwd_setup_inputs(seed: int = 0) -> dict:
    key = jax.random.key(seed)
    ks = jax.random.split(key, 24)
    f32 = jnp.float32

    def nrm(k, shape, fan_in):
        return jax.random.normal(k, shape, f32) * (fan_in ** -0.5)

    def gain(k, shape):
        return 1.0 + 0.02 * jax.random.normal(k, shape, f32)

    def small(k, shape, s=0.02):
        return s * jax.random.normal(k, shape, f32)

    return {
        "x": jax.random.normal(ks[0], (BATCH, SEQ, D_MODEL), f32),
        "mem": jax.random.normal(ks[1], (BATCH, MEM_LEN, D_MODEL), f32),
        "g_mix": gain(ks[2], (DEPTH, D_MODEL)),
        "w_in": nrm(ks[3], (DEPTH, D_MODEL, D_IN), D_MODEL),
        "b_f": 2.0 + small(ks[4], (DEPTH, FOX_HEADS), 0.5),
        "conv_w": nrm(ks[5], (DEPTH, CONV_K, CONV_CH), CONV_K),
        "conv_b": small(ks[6], (DEPTH, CONV_CH)),
        "ln_g": gain(ks[7], (DEPTH, CONV_CH)),
        "ln_b": small(ks[8], (DEPTH, CONV_CH)),
        "w_out": nrm(ks[9], (DEPTH, D_MIX, D_MODEL), D_MIX),
        "g_x": gain(ks[10], (DEPTH, D_MODEL)),
        "g_mem": gain(ks[11], (D_MODEL,)),
        "w_mq": nrm(ks[12], (DEPTH, D_MODEL, D_MODEL), D_MODEL),
        "w_mkv": nrm(ks[13], (DEPTH, D_MODEL, 2 * D_MODEL), D_MODEL),
        "w_mo": nrm(ks[14], (DEPTH, D_MODEL, D_MODEL), D_MODEL),
        "g_ffn": gain(ks[15], (DEPTH, D_MODEL)),
        "w_gu": nrm(ks[16], (DEPTH, D_MODEL, 2 * D_FF), D_MODEL),
        "w_down": nrm(ks[17], (DEPTH, D_FF, D_MODEL), D_FF),
        "g_final": gain(ks[18], (D_MODEL,)),
    }


def _fwd_reference(x, mem, g_mix, w_in, b_f, conv_w, conv_b, ln_g, ln_b, w_out,
              g_x, g_mem, w_mq, w_mkv, w_mo, g_ffn, w_gu, w_down, g_final):
    b, s, _ = x.shape
    mem_n = rmsnorm(mem, g_mem)
    for l in range(DEPTH):
        h = rmsnorm(x, g_mix[l])
        z = h @ w_in[l]
        conv_out = conformer_conv(z[..., OFF_U:OFF_G], z[..., OFF_G:OFF_Q],
                                  conv_w[l], conv_b[l], ln_g[l], ln_b[l])
        q = z[..., OFF_Q:OFF_K].reshape(b, s, FOX_HEADS, FOX_HEAD_DIM)
        k = z[..., OFF_K:OFF_V].reshape(b, s, FOX_HEADS, FOX_HEAD_DIM)
        v = z[..., OFF_V:OFF_F].reshape(b, s, FOX_HEADS, FOX_HEAD_DIM)
        logf = jax.nn.log_sigmoid((z[..., OFF_F:] + b_f[l]).astype(jnp.float32))
        att_out = forgetting_attention(q, k, v, logf)
        x = x + jnp.concatenate([conv_out, att_out], axis=-1) @ w_out[l]
        x = x + memory_cross_attention(rmsnorm(x, g_x[l]), mem_n, w_mq[l], w_mkv[l], w_mo[l])
        gu = rmsnorm(x, g_ffn[l]) @ w_gu[l]
        x = x + (jax.nn.silu(gu[..., :D_FF]) * gu[..., D_FF:]) @ w_down[l]
    return rmsnorm(x, g_final)


import jax as _jax
import jax.numpy as _jnp

TWIN_FORMAT = 'train_step'
FWD_PARAMS = ['x', 'mem', 'g_mix', 'w_in', 'b_f', 'conv_w', 'conv_b', 'ln_g', 'ln_b', 'w_out', 'g_x', 'g_mem', 'w_mq', 'w_mkv', 'w_mo', 'g_ffn', 'w_gu', 'w_down', 'g_final']
TWIN_WEIGHTS = ['g_mix', 'w_in', 'b_f', 'conv_w', 'conv_b', 'ln_g', 'ln_b', 'w_out', 'g_x', 'g_mem', 'w_mq', 'w_mkv', 'w_mo', 'g_ffn', 'w_gu', 'w_down', 'g_final']
TWIN_DIFF_INPUT = 'x'
TWIN_INPUTS = ['x', 'mem', 'g_mix', 'w_in', 'b_f', 'conv_w', 'conv_b', 'ln_g', 'ln_b', 'w_out', 'g_x', 'g_mem', 'w_mq', 'w_mkv', 'w_mo', 'g_ffn', 'w_gu', 'w_down', 'g_final', 'loss_target', 'm_g_mix', 'm_w_in', 'm_b_f', 'm_conv_w', 'm_conv_b', 'm_ln_g', 'm_ln_b', 'm_w_out', 'm_g_x', 'm_g_mem', 'm_w_mq', 'm_w_mkv', 'm_w_mo', 'm_g_ffn', 'm_w_gu', 'm_w_down', 'm_g_final', 'v_g_mix', 'v_w_in', 'v_b_f', 'v_conv_w', 'v_conv_b', 'v_ln_g', 'v_ln_b', 'v_w_out', 'v_g_x', 'v_g_mem', 'v_w_mq', 'v_w_mkv', 'v_w_mo', 'v_g_ffn', 'v_w_gu', 'v_w_down', 'v_g_final']
TWIN_OUTPUTS = ['loss', 'grad_x', 'grad_g_mix', 'grad_w_in', 'grad_b_f', 'grad_conv_w', 'grad_conv_b', 'grad_ln_g', 'grad_ln_b', 'grad_w_out', 'grad_g_x', 'grad_g_mem', 'grad_w_mq', 'grad_w_mkv', 'grad_w_mo', 'grad_g_ffn', 'grad_w_gu', 'grad_w_down', 'grad_g_final', 'delta_g_mix', 'delta_w_in', 'delta_b_f', 'delta_conv_w', 'delta_conv_b', 'delta_ln_g', 'delta_ln_b', 'delta_w_out', 'delta_g_x', 'delta_g_mem', 'delta_w_mq', 'delta_w_mkv', 'delta_w_mo', 'delta_g_ffn', 'delta_w_gu', 'delta_w_down', 'delta_g_final', 'new_m_g_mix', 'new_m_w_in', 'new_m_b_f', 'new_m_conv_w', 'new_m_conv_b', 'new_m_ln_g', 'new_m_ln_b', 'new_m_w_out', 'new_m_g_x', 'new_m_g_mem', 'new_m_w_mq', 'new_m_w_mkv', 'new_m_w_mo', 'new_m_g_ffn', 'new_m_w_gu', 'new_m_w_down', 'new_m_g_final', 'new_v_g_mix', 'new_v_w_in', 'new_v_b_f', 'new_v_conv_w', 'new_v_conv_b', 'new_v_ln_g', 'new_v_ln_b', 'new_v_w_out', 'new_v_g_x', 'new_v_g_mem', 'new_v_w_mq', 'new_v_w_mkv', 'new_v_w_mo', 'new_v_g_ffn', 'new_v_w_gu', 'new_v_w_down', 'new_v_g_final']
TWIN_LEAF_KINDS = {'loss': 'loss', 'grad_x': 'grad_x', 'grad_g_mix': 'grad_w', 'grad_w_in': 'grad_w', 'grad_b_f': 'grad_w', 'grad_conv_w': 'grad_w', 'grad_conv_b': 'grad_w', 'grad_ln_g': 'grad_w', 'grad_ln_b': 'grad_w', 'grad_w_out': 'grad_w', 'grad_g_x': 'grad_w', 'grad_g_mem': 'grad_w', 'grad_w_mq': 'grad_w', 'grad_w_mkv': 'grad_w', 'grad_w_mo': 'grad_w', 'grad_g_ffn': 'grad_w', 'grad_w_gu': 'grad_w', 'grad_w_down': 'grad_w', 'grad_g_final': 'grad_w', 'delta_g_mix': 'delta_w', 'delta_w_in': 'delta_w', 'delta_b_f': 'delta_w', 'delta_conv_w': 'delta_w', 'delta_conv_b': 'delta_w', 'delta_ln_g': 'delta_w', 'delta_ln_b': 'delta_w', 'delta_w_out': 'delta_w', 'delta_g_x': 'delta_w', 'delta_g_mem': 'delta_w', 'delta_w_mq': 'delta_w', 'delta_w_mkv': 'delta_w', 'delta_w_mo': 'delta_w', 'delta_g_ffn': 'delta_w', 'delta_w_gu': 'delta_w', 'delta_w_down': 'delta_w', 'delta_g_final': 'delta_w', 'new_m_g_mix': 'new_m', 'new_m_w_in': 'new_m', 'new_m_b_f': 'new_m', 'new_m_conv_w': 'new_m', 'new_m_conv_b': 'new_m', 'new_m_ln_g': 'new_m', 'new_m_ln_b': 'new_m', 'new_m_w_out': 'new_m', 'new_m_g_x': 'new_m', 'new_m_g_mem': 'new_m', 'new_m_w_mq': 'new_m', 'new_m_w_mkv': 'new_m', 'new_m_w_mo': 'new_m', 'new_m_g_ffn': 'new_m', 'new_m_w_gu': 'new_m', 'new_m_w_down': 'new_m', 'new_m_g_final': 'new_m', 'new_v_g_mix': 'new_v', 'new_v_w_in': 'new_v', 'new_v_b_f': 'new_v', 'new_v_conv_w': 'new_v', 'new_v_conv_b': 'new_v', 'new_v_ln_g': 'new_v', 'new_v_ln_b': 'new_v', 'new_v_w_out': 'new_v', 'new_v_g_x': 'new_v', 'new_v_g_mem': 'new_v', 'new_v_w_mq': 'new_v', 'new_v_w_mkv': 'new_v', 'new_v_w_mo': 'new_v', 'new_v_g_ffn': 'new_v', 'new_v_w_gu': 'new_v', 'new_v_w_down': 'new_v', 'new_v_g_final': 'new_v'}


def _forward(args):
    return _fwd_reference(*[args[k] for k in FWD_PARAMS])


def _output_shape():
    out = _jax.eval_shape(lambda: _forward(_fwd_setup_inputs(0)))
    return out.shape, out.dtype

N_MICROBATCH = 1
ADAM_LR = 0.001
ADAM_B1 = 0.9
ADAM_B2 = 0.999
ADAM_EPS = 1e-08
ADAM_WD = 0.01
ADAM_STEP = 10
PER_EXAMPLE_BATCH_AXIS = {'x': 0, 'mem': 0, 'loss_target': 0}
SHARED_INPUTS = []
_WEIGHT_DTYPES = {'g_mix': _jnp.float32, 'w_in': _jnp.float32, 'b_f': _jnp.float32, 'conv_w': _jnp.float32, 'conv_b': _jnp.float32, 'ln_g': _jnp.float32, 'ln_b': _jnp.float32, 'w_out': _jnp.float32, 'g_x': _jnp.float32, 'g_mem': _jnp.float32, 'w_mq': _jnp.float32, 'w_mkv': _jnp.float32, 'w_mo': _jnp.float32, 'g_ffn': _jnp.float32, 'w_gu': _jnp.float32, 'w_down': _jnp.float32, 'g_final': _jnp.float32}
MOMENT_SCALE = {'g_mix': 1.121417e-01, 'w_in': 7.129382e-02, 'b_f': 3.988392e-01, 'conv_w': 1.095560e-01, 'conv_b': 2.232427e-01, 'ln_g': 1.296340e-01, 'ln_b': 1.075607e-01, 'w_out': 9.123175e-02, 'g_x': 1.779143e-02, 'g_mem': 2.675411e-02, 'w_mq': 1.825998e-02, 'w_mkv': 1.861103e-02, 'w_mo': 1.858433e-02, 'g_ffn': 1.231162e-01, 'w_gu': 5.065154e-02, 'w_down': 8.260648e-02, 'g_final': 3.198092e+01}


def _to_microbatches(a, axis):
    t = _jnp.moveaxis(a, axis, 0)
    t = t.reshape((N_MICROBATCH, t.shape[0] // N_MICROBATCH) + t.shape[1:])
    return _jnp.moveaxis(t, 1, axis + 1)


def setup_inputs(seed: int = 0) -> dict:
    inp = _fwd_setup_inputs(seed)
    key = _jax.random.fold_in(_jax.random.key(seed), 7919)
    shape, _ = _output_shape()
    out = dict(inp)
    out["loss_target"] = _jax.random.normal(_jax.random.fold_in(key, 0), shape, _jnp.float32)
    for i, name in enumerate(TWIN_WEIGHTS):
        w = inp[name].astype(_jnp.float32)
        if MOMENT_SCALE is None:
            s = _jnp.sqrt(_jnp.mean(_jnp.square(w)) + 1e-30)
        else:
            s = MOMENT_SCALE[name]
        km, kv = _jax.random.split(_jax.random.fold_in(key, i + 1))
        out[name] = w
        out["m_" + name] = s * _jax.random.normal(km, w.shape, _jnp.float32)
        out["v_" + name] = (s * s) * _jax.random.uniform(kv, w.shape, _jnp.float32, 0.5, 1.5)
    if N_MICROBATCH > 1:
        for name, axis in PER_EXAMPLE_BATCH_AXIS.items():
            out[name] = _to_microbatches(out[name], axis)
    return {'x': out['x'], 'mem': out['mem'], 'g_mix': out['g_mix'], 'w_in': out['w_in'], 'b_f': out['b_f'], 'conv_w': out['conv_w'], 'conv_b': out['conv_b'], 'ln_g': out['ln_g'], 'ln_b': out['ln_b'], 'w_out': out['w_out'], 'g_x': out['g_x'], 'g_mem': out['g_mem'], 'w_mq': out['w_mq'], 'w_mkv': out['w_mkv'], 'w_mo': out['w_mo'], 'g_ffn': out['g_ffn'], 'w_gu': out['w_gu'], 'w_down': out['w_down'], 'g_final': out['g_final'], 'loss_target': out['loss_target'], 'm_g_mix': out['m_g_mix'], 'm_w_in': out['m_w_in'], 'm_b_f': out['m_b_f'], 'm_conv_w': out['m_conv_w'], 'm_conv_b': out['m_conv_b'], 'm_ln_g': out['m_ln_g'], 'm_ln_b': out['m_ln_b'], 'm_w_out': out['m_w_out'], 'm_g_x': out['m_g_x'], 'm_g_mem': out['m_g_mem'], 'm_w_mq': out['m_w_mq'], 'm_w_mkv': out['m_w_mkv'], 'm_w_mo': out['m_w_mo'], 'm_g_ffn': out['m_g_ffn'], 'm_w_gu': out['m_w_gu'], 'm_w_down': out['m_w_down'], 'm_g_final': out['m_g_final'], 'v_g_mix': out['v_g_mix'], 'v_w_in': out['v_w_in'], 'v_b_f': out['v_b_f'], 'v_conv_w': out['v_conv_w'], 'v_conv_b': out['v_conv_b'], 'v_ln_g': out['v_ln_g'], 'v_ln_b': out['v_ln_b'], 'v_w_out': out['v_w_out'], 'v_g_x': out['v_g_x'], 'v_g_mem': out['v_g_mem'], 'v_w_mq': out['v_w_mq'], 'v_w_mkv': out['v_w_mkv'], 'v_w_mo': out['v_w_mo'], 'v_g_ffn': out['v_g_ffn'], 'v_w_gu': out['v_w_gu'], 'v_w_down': out['v_w_down'], 'v_g_final': out['v_g_final']}


def _loss(weights, diff, rest, loss_target):
    with _jax.named_scope("forward"):
        args = {**rest, TWIN_DIFF_INPUT: diff, **{k: w.astype(_WEIGHT_DTYPES[k]) for k, w in weights.items()}}
        y = _forward(args)
    with _jax.named_scope("loss_head"):
        err = _jnp.square(y.astype(_jnp.float32) - loss_target)
        return 0.5 * _jnp.sum(_jnp.mean(err, axis=-1)) if err.ndim else 0.5 * err


def _adamw(w, g, m, v):
    m = ADAM_B1 * m + (1.0 - ADAM_B1) * g
    v = ADAM_B2 * v + (1.0 - ADAM_B2) * _jnp.square(g)
    m_hat = m / (1.0 - ADAM_B1 ** ADAM_STEP)
    v_hat = v / (1.0 - ADAM_B2 ** ADAM_STEP)
    delta = -ADAM_LR * (m_hat / (_jnp.sqrt(v_hat) + ADAM_EPS) + ADAM_WD * w)
    return delta, m, v


def reference(x, mem, g_mix, w_in, b_f, conv_w, conv_b, ln_g, ln_b, w_out, g_x, g_mem, w_mq, w_mkv, w_mo, g_ffn, w_gu, w_down, g_final, loss_target, m_g_mix, m_w_in, m_b_f, m_conv_w, m_conv_b, m_ln_g, m_ln_b, m_w_out, m_g_x, m_g_mem, m_w_mq, m_w_mkv, m_w_mo, m_g_ffn, m_w_gu, m_w_down, m_g_final, v_g_mix, v_w_in, v_b_f, v_conv_w, v_conv_b, v_ln_g, v_ln_b, v_w_out, v_g_x, v_g_mem, v_w_mq, v_w_mkv, v_w_mo, v_g_ffn, v_w_gu, v_w_down, v_g_final):
    given = dict(x=x, mem=mem, g_mix=g_mix, w_in=w_in, b_f=b_f, conv_w=conv_w, conv_b=conv_b, ln_g=ln_g, ln_b=ln_b, w_out=w_out, g_x=g_x, g_mem=g_mem, w_mq=w_mq, w_mkv=w_mkv, w_mo=w_mo, g_ffn=g_ffn, w_gu=w_gu, w_down=w_down, g_final=g_final, loss_target=loss_target, m_g_mix=m_g_mix, m_w_in=m_w_in, m_b_f=m_b_f, m_conv_w=m_conv_w, m_conv_b=m_conv_b, m_ln_g=m_ln_g, m_ln_b=m_ln_b, m_w_out=m_w_out, m_g_x=m_g_x, m_g_mem=m_g_mem, m_w_mq=m_w_mq, m_w_mkv=m_w_mkv, m_w_mo=m_w_mo, m_g_ffn=m_g_ffn, m_w_gu=m_w_gu, m_w_down=m_w_down, m_g_final=m_g_final, v_g_mix=v_g_mix, v_w_in=v_w_in, v_b_f=v_b_f, v_conv_w=v_conv_w, v_conv_b=v_conv_b, v_ln_g=v_ln_g, v_ln_b=v_ln_b, v_w_out=v_w_out, v_g_x=v_g_x, v_g_mem=v_g_mem, v_w_mq=v_w_mq, v_w_mkv=v_w_mkv, v_w_mo=v_w_mo, v_g_ffn=v_g_ffn, v_w_gu=v_w_gu, v_w_down=v_w_down, v_g_final=v_g_final)
    weights = {n: given[n] for n in TWIN_WEIGHTS}
    shared = {n: given[n] for n in SHARED_INPUTS}
    per_example = {n: given[n] for n in ['x', 'mem']}
    grad_fn = _jax.value_and_grad(_loss, argnums=(0, 1))

    def one_microbatch(ex, loss_target):
        ex = dict(ex)
        diff = ex.pop(TWIN_DIFF_INPUT)
        return grad_fn(weights, diff, {**shared, **ex}, loss_target)

    if N_MICROBATCH == 1:
        loss, (grad_w, grad_x) = one_microbatch(per_example, given["loss_target"])
    else:
        def body(carry, xs):
            loss_sum, grad_sum = carry
            l_k, (gw_k, gx_k) = one_microbatch(xs[0], xs[1])
            with _jax.named_scope("update"):
                return (loss_sum + l_k, _jax.tree.map(_jnp.add, grad_sum, gw_k)), gx_k

        init = (_jnp.zeros((), _jnp.float32), _jax.tree.map(_jnp.zeros_like, weights))
        (loss, grad_w), grad_x = _jax.lax.scan(body, init, (per_example, given["loss_target"]))
    with _jax.named_scope("update"):
        delta_w, new_m, new_v = {}, {}, {}
        for n in TWIN_WEIGHTS:
            delta_w[n], new_m[n], new_v[n] = _adamw(weights[n], grad_w[n], given["m_" + n], given["v_" + n])
    return (loss, grad_x, *[grad_w[n] for n in TWIN_WEIGHTS], *[delta_w[n] for n in TWIN_WEIGHTS],
            *[new_m[n] for n in TWIN_WEIGHTS], *[new_v[n] for n in TWIN_WEIGHTS])
```

```python
import functools
import math

import jax
import jax.numpy as jnp
from jax import lax
from jax.experimental import pallas as pl
from jax.experimental.pallas import tpu as pltpu

F32 = jnp.float32
BF16 = jnp.bfloat16
EPS = 1e-6
N_DEV = 8
CONV_CH = 512
CONV_K = 31
FOX_HEADS = 8
FOX_HEAD_DIM = 64
FOX_W = 512
MEM_HEADS = 4
MEM_HEAD_DIM = 256
HALO = 32
LANES = 128
ADAM_LR, ADAM_B1, ADAM_B2, ADAM_EPS, ADAM_WD, ADAM_STEP = 0.001, 0.9, 0.999, 1e-08, 0.01, 10
NEG = -1e30
VMEM_CAP = 60 * 1024 * 1024
MESH = pl.DeviceIdType.MESH


def _call(body, **kw):
    return pl.pallas_call(body, **kw)


def _params(sem=None, vmem=None):
    kw = {}
    if sem is not None:
        kw["dimension_semantics"] = sem
    if vmem is not None:
        kw["vmem_limit_bytes"] = int(min(VMEM_CAP, vmem))
    return pltpu.CompilerParams(**kw)


def _nbytes(shape, dtype):
    return math.prod(shape) * jnp.dtype(dtype).itemsize


def _pick(n, target):
    best = None
    for d in range(LANES, min(n, target) + 1, LANES):
        if n % d == 0:
            best = d
    return n if best is None else best


def matmul(a, b, *, ta=False, tb=False, out_dtype, res=None, tm=512, tn=512, tk=None, name):
    M, K = (a.shape[1], a.shape[0]) if ta else a.shape
    N = b.shape[0] if tb else b.shape[1]
    assert (b.shape[1] if tb else b.shape[0]) == K
    tm, tn = _pick(M, tm), _pick(N, tn)
    tk = K if tk is None else _pick(K, tk)
    assert M % tm == 0 and N % tn == 0 and K % tk == 0, (name, M, N, K, tm, tn, tk)
    nk = K // tk
    dn = (((0 if ta else 1,), (1 if tb else 0,)), ((), ()))

    def body(*refs):
        if res is not None:
            a_ref, b_ref, r_ref, o_ref = refs[:4]
        else:
            a_ref, b_ref, o_ref = refs[:3]
        p = lax.dot_general(a_ref[...].astype(BF16), b_ref[...].astype(BF16), dn,
                            preferred_element_type=F32)

        def finish(acc):
            if res is not None:
                acc = acc + r_ref[...].astype(F32)
            o_ref[...] = acc.astype(out_dtype)

        if nk == 1:
            finish(p)
        else:
            acc_ref = refs[-1]
            k = pl.program_id(2)

            @pl.when(k == 0)
            def _():
                acc_ref[...] = p

            @pl.when(k > 0)
            def _():
                acc_ref[...] += p

            @pl.when(k == nk - 1)
            def _():
                finish(acc_ref[...])

    a_spec = pl.BlockSpec((tk, tm), lambda i, j, k: (k, i)) if ta else pl.BlockSpec((tm, tk), lambda i, j, k: (i, k))
    b_spec = pl.BlockSpec((tn, tk), lambda i, j, k: (j, k)) if tb else pl.BlockSpec((tk, tn), lambda i, j, k: (k, j))
    o_spec = pl.BlockSpec((tm, tn), lambda i, j, k: (i, j))
    in_specs, args = [a_spec, b_spec], [a, b]
    est = 2 * (_nbytes((tm, tk), a.dtype) + _nbytes((tk, tn), b.dtype) + _nbytes((tm, tn), out_dtype))
    est += _nbytes((tm, tk), BF16) + _nbytes((tk, tn), BF16) + 3 * _nbytes((tm, tn), F32)
    if res is not None:
        in_specs.append(o_spec)
        args.append(res)
        est += 2 * _nbytes((tm, tn), res.dtype)
    return _call(
        body, name=name, grid=(M // tm, N // tn, nk),
        in_specs=in_specs, out_specs=o_spec,
        out_shape=jax.ShapeDtypeStruct((M, N), out_dtype),
        scratch_shapes=[] if nk == 1 else [pltpu.VMEM((tm, tn), F32)],
        compiler_params=_params(("parallel", "parallel", "arbitrary"), est + (8 << 20)),
    )(*args)


def _rms_scale(x):
    return lax.rsqrt(jnp.mean(x * x, axis=-1, keepdims=True) + EPS)


def rmsnorm_fwd(x, g, *, name, tm=512):
    T, D = x.shape
    tm = min(tm, T)

    def body(x_ref, g_ref, o_ref):
        xv = x_ref[...]
        o_ref[...] = (xv * _rms_scale(xv) * g_ref[...]).astype(BF16)

    return _call(
        body, name=name, grid=(T // tm,),
        in_specs=[pl.BlockSpec((tm, D), lambda i: (i, 0)), pl.BlockSpec((1, D), lambda i: (0, 0))],
        out_specs=pl.BlockSpec((tm, D), lambda i: (i, 0)),
        out_shape=jax.ShapeDtypeStruct((T, D), BF16),
        compiler_params=_params(("parallel",)),
    )(x, g)


def _rms_bwd_math(xv, gv, dh):
    r = _rms_scale(xv)
    xh = xv * r
    dg = jnp.sum(dh * xh, axis=0, keepdims=True)
    dxh = dh * gv
    dx = r * (dxh - xh * jnp.mean(dxh * xh, axis=-1, keepdims=True))
    return dx, dg


def rmsnorm_bwd(x, g, dh, dres, *, name, tm=256):
    T, D = x.shape
    tm = min(tm, T)

    def body(*refs):
        if dres is not None:
            x_ref, g_ref, dh_ref, dr_ref, dx_ref, dg_ref = refs
        else:
            x_ref, g_ref, dh_ref, dx_ref, dg_ref = refs
        dx, dg = _rms_bwd_math(x_ref[...], g_ref[...], dh_ref[...].astype(F32))
        if dres is not None:
            dx = dx + dr_ref[...]
        dx_ref[...] = dx

        @pl.when(pl.program_id(0) == 0)
        def _():
            dg_ref[...] = jnp.zeros_like(dg_ref)

        dg_ref[...] += dg

    row = pl.BlockSpec((tm, D), lambda i: (i, 0))
    vec = pl.BlockSpec((1, D), lambda i: (0, 0))
    ins, args = [row, vec, row], [x, g, dh]
    if dres is not None:
        ins.append(row)
        args.append(dres)
    return _call(
        body, name=name, grid=(T // tm,), in_specs=ins, out_specs=[row, vec],
        out_shape=[jax.ShapeDtypeStruct((T, D), F32), jax.ShapeDtypeStruct((1, D), F32)],
        compiler_params=_params(("arbitrary",)),
    )(*args)


def final_loss_bwd(x, g, target, *, name, tm=256):
    T, D = x.shape
    tm = min(tm, T)

    def body(x_ref, g_ref, t_ref, dx_ref, dg_ref, l_ref):
        xv, gv = x_ref[...], g_ref[...]
        e = xv * _rms_scale(xv) * gv - t_ref[...]
        part = 0.5 * jnp.sum(jnp.mean(e * e, axis=-1, keepdims=True), axis=0, keepdims=True)
        dx, dg = _rms_bwd_math(xv, gv, e * (1.0 / D))
        dx_ref[...] = dx

        @pl.when(pl.program_id(0) == 0)
        def _():
            dg_ref[...] = jnp.zeros_like(dg_ref)
            l_ref[...] = jnp.zeros_like(l_ref)

        dg_ref[...] += dg
        l_ref[...] += jnp.broadcast_to(part, l_ref.shape)

    row = pl.BlockSpec((tm, D), lambda i: (i, 0))
    vec = pl.BlockSpec((1, D), lambda i: (0, 0))
    return _call(
        body, name=name, grid=(T // tm,), in_specs=[row, vec, row],
        out_specs=[row, vec, pl.BlockSpec((1, LANES), lambda i: (0, 0))],
        out_shape=[jax.ShapeDtypeStruct((T, D), F32), jax.ShapeDtypeStruct((1, D), F32),
                   jax.ShapeDtypeStruct((1, LANES), F32)],
        compiler_params=_params(("arbitrary",)),
    )(x, g, target)


def _sigmoid(v):
    return 1.0 / (1.0 + jnp.exp(-v))


def _glu(blk):
    u = blk[:, :CONV_CH].astype(F32)
    gt = blk[:, CONV_CH:].astype(F32)
    return u * _sigmoid(gt)


def _fill_causal_ext(ext, cur_ref, halo_ref, s, ts):
    ext[pl.ds(HALO, ts), :] = _glu(cur_ref[0])
    hal = _glu(halo_ref[0])
    ext[pl.ds(0, HALO), :] = jnp.where(s > 0, hal, 0.0)


def _causal_conv(ext, w_ref, ts):
    acc = jnp.zeros((ts, CONV_CH), F32)
    for j in range(CONV_K):
        acc = acc + ext[pl.ds(HALO - (CONV_K - 1) + j, ts), :] * w_ref[pl.ds(j, 1), :]
    return acc


def _ln_stats(y):
    mu = jnp.mean(y, axis=-1, keepdims=True)
    yc = y - mu
    rstd = lax.rsqrt(jnp.mean(yc * yc, axis=-1, keepdims=True) + EPS)
    return yc * rstd, rstd


def _conv_specs(ts, S):
    nh = ts // HALO
    cur = pl.BlockSpec((1, ts, 2 * CONV_CH), lambda b, s: (b, s, 0))
    halo = pl.BlockSpec((1, HALO, 2 * CONV_CH), lambda b, s: (b, jnp.maximum(s * nh - 1, 0), 0))
    w = pl.BlockSpec((HALO, CONV_CH), lambda b, s: (0, 0))
    vec = pl.BlockSpec((1, CONV_CH), lambda b, s: (0, 0))
    return cur, halo, w, vec


def conv_branch_fwd(ug, conv_w, conv_b, ln_g, ln_b, *, name, ts=256):
    B, S, _ = ug.shape
    ts = min(ts, S)
    cur, halo, w, vec = _conv_specs(ts, S)

    def body(cur_ref, halo_ref, w_ref, cb_ref, lg_ref, lb_ref, o_ref, ext):
        _fill_causal_ext(ext, cur_ref, halo_ref, pl.program_id(1), ts)
        y = _causal_conv(ext, w_ref, ts) + cb_ref[...]
        yh, _ = _ln_stats(y)
        ln = yh * lg_ref[...] + lb_ref[...]
        o_ref[0] = (ln * _sigmoid(ln)).astype(BF16)

    return _call(
        body, name=name, grid=(B, S // ts), in_specs=[cur, halo, w, vec, vec, vec],
        out_specs=pl.BlockSpec((1, ts, CONV_CH), lambda b, s: (b, s, 0)),
        out_shape=jax.ShapeDtypeStruct((B, S, CONV_CH), BF16),
        scratch_shapes=[pltpu.VMEM((ts + HALO, CONV_CH), F32)],
        compiler_params=_params(("parallel", "parallel")),
    )(ug, ug, conv_w, conv_b, ln_g, ln_b)


def conv_branch_bwd_a(ug, dcat, conv_w, conv_b, ln_g, ln_b, *, name, ts=256):
    B, S, _ = ug.shape
    ts = min(ts, S)
    cur, halo, w, vec = _conv_specs(ts, S)

    def body(cur_ref, halo_ref, d_ref, w_ref, cb_ref, lg_ref, lb_ref, dy_ref, dw_ref, dv_ref, ext):
        _fill_causal_ext(ext, cur_ref, halo_ref, pl.program_id(1), ts)
        y = _causal_conv(ext, w_ref, ts) + cb_ref[...]
        yh, rstd = _ln_stats(y)
        lg = lg_ref[...]
        ln = yh * lg + lb_ref[...]
        sg = _sigmoid(ln)
        dln = d_ref[0].astype(F32) * (sg * (1.0 + ln * (1.0 - sg)))
        dyh = dln * lg
        dy = rstd * (dyh - jnp.mean(dyh, axis=-1, keepdims=True)
                     - yh * jnp.mean(dyh * yh, axis=-1, keepdims=True))
        dy_ref[0] = dy

        @pl.when((pl.program_id(0) == 0) & (pl.program_id(1) == 0))
        def _():
            dw_ref[...] = jnp.zeros_like(dw_ref)
            dv_ref[...] = jnp.zeros_like(dv_ref)

        dv_ref[pl.ds(0, 1), :] += jnp.sum(dy, axis=0, keepdims=True)
        dv_ref[pl.ds(1, 1), :] += jnp.sum(dln * yh, axis=0, keepdims=True)
        dv_ref[pl.ds(2, 1), :] += jnp.sum(dln, axis=0, keepdims=True)
        for j in range(CONV_K):
            tap = ext[pl.ds(HALO - (CONV_K - 1) + j, ts), :]
            dw_ref[pl.ds(j, 1), :] += jnp.sum(dy * tap, axis=0, keepdims=True)

    return _call(
        body, name=name, grid=(B, S // ts),
        in_specs=[cur, halo, pl.BlockSpec((1, ts, CONV_CH), lambda b, s: (b, s, 0)), w, vec, vec, vec],
        out_specs=[pl.BlockSpec((1, ts, CONV_CH), lambda b, s: (b, s, 0)),
                   pl.BlockSpec((HALO, CONV_CH), lambda b, s: (0, 0)),
                   pl.BlockSpec((8, CONV_CH), lambda b, s: (0, 0))],
        out_shape=[jax.ShapeDtypeStruct((B, S, CONV_CH), F32),
                   jax.ShapeDtypeStruct((HALO, CONV_CH), F32),
                   jax.ShapeDtypeStruct((8, CONV_CH), F32)],
        scratch_shapes=[pltpu.VMEM((ts + HALO, CONV_CH), F32)],
        compiler_params=_params(("arbitrary", "arbitrary")),
    )(ug, ug, dcat, conv_w, conv_b, ln_g, ln_b)


def conv_branch_bwd_b(ug, dy, conv_w, *, name, ts=256):
    B, S, _ = ug.shape
    ts = min(ts, S)
    nh, n_halo = ts // HALO, S // HALO

    def body(cur_ref, dy_ref, nxt_ref, w_ref, o_ref, ext):
        last = pl.program_id(1) == pl.num_programs(1) - 1
        ext[pl.ds(0, ts), :] = dy_ref[0]
        ext[pl.ds(ts, HALO), :] = jnp.where(last, 0.0, nxt_ref[0])
        da = jnp.zeros((ts, CONV_CH), F32)
        for j in range(CONV_K):
            da = da + ext[pl.ds(CONV_K - 1 - j, ts), :] * w_ref[pl.ds(j, 1), :]
        blk = cur_ref[0]
        u = blk[:, :CONV_CH].astype(F32)
        sg = _sigmoid(blk[:, CONV_CH:].astype(F32))
        o_ref[0, :, :CONV_CH] = (da * sg).astype(BF16)
        o_ref[0, :, CONV_CH:] = (da * u * sg * (1.0 - sg)).astype(BF16)

    return _call(
        body, name=name, grid=(B, S // ts),
        in_specs=[pl.BlockSpec((1, ts, 2 * CONV_CH), lambda b, s: (b, s, 0)),
                  pl.BlockSpec((1, ts, CONV_CH), lambda b, s: (b, s, 0)),
                  pl.BlockSpec((1, HALO, CONV_CH), lambda b, s: (b, jnp.minimum((s + 1) * nh, n_halo - 1), 0)),
                  pl.BlockSpec((HALO, CONV_CH), lambda b, s: (0, 0))],
        out_specs=pl.BlockSpec((1, ts, 2 * CONV_CH), lambda b, s: (b, s, 0)),
        out_shape=jax.ShapeDtypeStruct((B, S, 2 * CONV_CH), BF16),
        scratch_shapes=[pltpu.VMEM((ts + HALO, CONV_CH), F32)],
        compiler_params=_params(("parallel", "parallel")),
    )(ug, dy, dy, conv_w)


def _tri(n, lower):
    r = lax.broadcasted_iota(jnp.int32, (n, n), 0)
    c = lax.broadcasted_iota(jnp.int32, (n, n), 1)
    return ((r >= c) if lower else (r <= c)).astype(F32)


def _eye(n):
    r = lax.broadcasted_iota(jnp.int32, (n, n), 0)
    c = lax.broadcasted_iota(jnp.int32, (n, n), 1)
    return (r == c).astype(F32)


def _dot_hi(a, b, dn):
    return lax.dot_general(a, b, dn, precision=lax.Precision.HIGHEST, preferred_element_type=F32)


NN = (((1,), (0,)), ((), ()))
NT = (((1,), (1,)), ((), ()))
TN = (((0,), (0,)), ((), ()))


def _log_sigmoid(v):
    e = jnp.exp(-jnp.abs(v))
    log1p_e = jnp.where(e < 1e-3, e * (1.0 - 0.5 * e), jnp.log(1.0 + e))
    return jnp.minimum(v, 0.0) - log1p_e


def fgate_fwd(h, w_f, b_f, *, name, ts=256):
    B, S, D = h.shape
    ts = min(ts, S)

    def body(h_ref, w_ref, b_ref, f_ref, cc_ref, cr_ref, carry):
        @pl.when(pl.program_id(1) == 0)
        def _():
            carry[...] = jnp.zeros_like(carry)

        f = jnp.dot(h_ref[0], w_ref[...], preferred_element_type=F32)
        f_ref[0] = f
        logf = _log_sigmoid(f + b_ref[...])
        c = _dot_hi(_tri(ts, True), logf, NN) + carry[pl.ds(0, 1), :]
        cc_ref[0] = c
        carry[pl.ds(0, 1), :] = c[ts - 1:ts, :]
        cr_ref[0] = _dot_hi(_eye(LANES), c, NT)

    return _call(
        body, name=name, grid=(B, S // ts),
        in_specs=[pl.BlockSpec((1, ts, D), lambda b, s: (b, s, 0)),
                  pl.BlockSpec((D, LANES), lambda b, s: (0, 0)),
                  pl.BlockSpec((1, LANES), lambda b, s: (0, 0))],
        out_specs=[pl.BlockSpec((1, ts, LANES), lambda b, s: (b, s, 0)),
                   pl.BlockSpec((1, ts, LANES), lambda b, s: (b, s, 0)),
                   pl.BlockSpec((1, LANES, ts), lambda b, s: (b, 0, s))],
        out_shape=[jax.ShapeDtypeStruct((B, S, LANES), F32), jax.ShapeDtypeStruct((B, S, LANES), F32),
                   jax.ShapeDtypeStruct((B, LANES, S), F32)],
        scratch_shapes=[pltpu.VMEM((8, LANES), F32)],
        compiler_params=_params(("parallel", "arbitrary")),
    )(h, w_f, b_f)


def fgate_bwd(dc_row, f, b_f, *, name, ts=256):
    B, S, _ = f.shape
    ts = min(ts, S)
    ns = S // ts

    def body(dc_ref, f_ref, b_ref, df_ref, db_ref, carry):
        @pl.when(pl.program_id(1) == 0)
        def _():
            carry[...] = jnp.zeros_like(carry)

        @pl.when((pl.program_id(0) == 0) & (pl.program_id(1) == 0))
        def _():
            db_ref[...] = jnp.zeros_like(db_ref)

        dlogf = _dot_hi(_tri(ts, False), dc_ref[0], NT) + carry[pl.ds(0, 1), :]
        carry[pl.ds(0, 1), :] = dlogf[0:1, :]
        df = dlogf * _sigmoid(-(f_ref[0] + b_ref[...]))
        df_ref[0] = df.astype(BF16)
        db_ref[...] += jnp.sum(df, axis=0, keepdims=True)

    return _call(
        body, name=name, grid=(B, ns),
        in_specs=[pl.BlockSpec((1, LANES, ts), lambda b, s: (b, 0, ns - 1 - s)),
                  pl.BlockSpec((1, ts, LANES), lambda b, s: (b, ns - 1 - s, 0)),
                  pl.BlockSpec((1, LANES), lambda b, s: (0, 0))],
        out_specs=[pl.BlockSpec((1, ts, LANES), lambda b, s: (b, ns - 1 - s, 0)),
                   pl.BlockSpec((1, LANES), lambda b, s: (0, 0))],
        out_shape=[jax.ShapeDtypeStruct((B, S, LANES), BF16), jax.ShapeDtypeStruct((1, LANES), F32)],
        scratch_shapes=[pltpu.VMEM((8, LANES), F32)],
        compiler_params=_params(("arbitrary", "arbitrary")),
    )(dc_row, f, b_f)


def _lane_pick(tile, idx):
    lane = lax.broadcasted_iota(jnp.int32, tile.shape, 1)
    return jnp.sum(jnp.where(lane == idx, tile, 0.0), axis=-1, keepdims=True)


def fox_fwd(qkv, c_col, c_row, *, name, tq=256):
    B, S, _ = qkv.shape
    tq = min(tq, S)
    scale = 1.0 / math.sqrt(FOX_HEAD_DIM)
    npair = FOX_HEADS // 2

    def body(q_ref, k_ref, v_ref, cc_ref, cr_ref, o_ref, l_ref):
        p, qi = pl.program_id(1), pl.program_id(2)
        q = q_ref[0]
        cc_tile = cc_ref[0]
        lane = lax.broadcasted_iota(jnp.int32, (tq, LANES), 1)
        row_g = qi * tq + lax.broadcasted_iota(jnp.int32, (tq, tq), 0)
        col_l = lax.broadcasted_iota(jnp.int32, (tq, tq), 1)
        out = jnp.zeros((tq, LANES), F32)
        lse = jnp.zeros((tq, LANES), F32)
        for hh in range(2):
            hmask = (lane < FOX_HEAD_DIM) if hh == 0 else (lane >= FOX_HEAD_DIM)
            head = 2 * p + hh
            qh = jnp.where(hmask, q, jnp.zeros_like(q))
            cc = _lane_pick(cc_tile, head)

            def step(kb, carry, qh=qh, cc=cc, hmask=hmask, head=head):
                m, l, acc = carry
                k0 = pl.multiple_of(kb * tq, tq)
                k = k_ref[0, pl.ds(k0, tq), :]
                v = v_ref[0, pl.ds(k0, tq), :]
                vh = jnp.where(hmask, v, jnp.zeros_like(v))
                s = lax.dot_general(qh, k, NT, preferred_element_type=F32) * scale
                s = s + (cc - cr_ref[0, pl.ds(head, 1), pl.ds(k0, tq)])
                s = jnp.where(k0 + col_l <= row_g, s, NEG)
                m_new = jnp.maximum(m, jnp.max(s, axis=-1, keepdims=True))
                alpha = jnp.exp(m - m_new)
                pr = jnp.exp(s - m_new)
                l = alpha * l + jnp.sum(pr, axis=-1, keepdims=True)
                acc = alpha * acc + jnp.dot(pr.astype(BF16), vh, preferred_element_type=F32)
                return m_new, l, acc

            init = (jnp.full((tq, 1), NEG, F32), jnp.zeros((tq, 1), F32), jnp.zeros((tq, LANES), F32))
            m, l, acc = lax.fori_loop(0, qi + 1, step, init)
            out = out + acc / l
            lse = jnp.where(hmask, m + jnp.log(l), lse)
        o_ref[0] = out.astype(BF16)
        l_ref[0, 0] = lse

    return _call(
        body, name=name, grid=(B, npair, S // tq),
        in_specs=[pl.BlockSpec((1, tq, LANES), lambda b, p, i: (b, i, p)),
                  pl.BlockSpec((1, S, LANES), lambda b, p, i: (b, 0, npair + p)),
                  pl.BlockSpec((1, S, LANES), lambda b, p, i: (b, 0, 2 * npair + p)),
                  pl.BlockSpec((1, tq, LANES), lambda b, p, i: (b, i, 0)),
                  pl.BlockSpec((1, 8, S), lambda b, p, i: (b, 0, 0))],
        out_specs=[pl.BlockSpec((1, tq, LANES), lambda b, p, i: (b, i, p)),
                   pl.BlockSpec((1, 1, tq, LANES), lambda b, p, i: (b, p, i, 0))],
        out_shape=[jax.ShapeDtypeStruct((B, S, FOX_W), BF16),
                   jax.ShapeDtypeStruct((B, npair, S, LANES), F32)],
        compiler_params=_params(("parallel", "parallel", "parallel")),
    )(qkv, qkv, qkv, c_col, c_row)


def fox_delta(qkv, dcat, lse, c_col, c_row, *, name, tq=256):
    B, S, _ = qkv.shape
    tq = min(tq, S)
    scale = 1.0 / math.sqrt(FOX_HEAD_DIM)
    npair = FOX_HEADS // 2

    def body(q_ref, k_ref, v_ref, do_ref, l_ref, cc_ref, cr_ref, d_ref):
        p, qi = pl.program_id(1), pl.program_id(2)
        q = q_ref[0]
        do_b = do_ref[0].astype(BF16)
        cc_tile = cc_ref[0]
        lse_t = l_ref[0, 0]
        lane = lax.broadcasted_iota(jnp.int32, (tq, LANES), 1)
        row_g = qi * tq + lax.broadcasted_iota(jnp.int32, (tq, tq), 0)
        col_l = lax.broadcasted_iota(jnp.int32, (tq, tq), 1)
        delta = jnp.zeros((tq, LANES), F32)
        for hh in range(2):
            hmask = (lane < FOX_HEAD_DIM) if hh == 0 else (lane >= FOX_HEAD_DIM)
            head = 2 * p + hh
            qh = jnp.where(hmask, q, jnp.zeros_like(q))
            doh = jnp.where(hmask, do_b, jnp.zeros_like(do_b))
            cc = _lane_pick(cc_tile, head)
            lse_h = _lane_pick(lse_t, hh * FOX_HEAD_DIM)

            def step(kb, acc, qh=qh, doh=doh, cc=cc, lse_h=lse_h, head=head):
                k0 = pl.multiple_of(kb * tq, tq)
                k = k_ref[0, pl.ds(k0, tq), :]
                v = v_ref[0, pl.ds(k0, tq), :]
                s = lax.dot_general(qh, k, NT, preferred_element_type=F32) * scale
                s = s + (cc - cr_ref[0, pl.ds(head, 1), pl.ds(k0, tq)])
                pr = jnp.where(k0 + col_l <= row_g, jnp.exp(s - lse_h), 0.0)
                dp = lax.dot_general(doh, v, NT, preferred_element_type=F32)
                return acc + jnp.sum(pr * dp, axis=-1, keepdims=True)

            dl = lax.fori_loop(0, qi + 1, step, jnp.zeros((tq, 1), F32))
            delta = jnp.where(hmask, dl, delta)
        d_ref[0, 0] = delta

    return _call(
        body, name=name, grid=(B, npair, S // tq),
        in_specs=[pl.BlockSpec((1, tq, LANES), lambda b, p, i: (b, i, p)),
                  pl.BlockSpec((1, S, LANES), lambda b, p, i: (b, 0, npair + p)),
                  pl.BlockSpec((1, S, LANES), lambda b, p, i: (b, 0, 2 * npair + p)),
                  pl.BlockSpec((1, tq, LANES), lambda b, p, i: (b, i, npair + p)),
                  pl.BlockSpec((1, 1, tq, LANES), lambda b, p, i: (b, p, i, 0)),
                  pl.BlockSpec((1, tq, LANES), lambda b, p, i: (b, i, 0)),
                  pl.BlockSpec((1, 8, S), lambda b, p, i: (b, 0, 0))],
        out_specs=pl.BlockSpec((1, 1, tq, LANES), lambda b, p, i: (b, p, i, 0)),
        out_shape=jax.ShapeDtypeStruct((B, npair, S, LANES), F32),
        compiler_params=_params(("parallel", "parallel", "parallel")),
    )(qkv, qkv, qkv, dcat, lse, c_col, c_row)


def fox_bwd(qkv, dcat, dlt, lse, c_col, c_row, *, name, tq=256):
    B, S, _ = qkv.shape
    tq = min(tq, S)
    nq = S // tq
    scale = 1.0 / math.sqrt(FOX_HEAD_DIM)
    npair = FOX_HEADS // 2

    def body(q_ref, k_ref, v_ref, do_ref, dl_ref, l_ref, cc_ref, cr_ref, dq_ref, dk_ref, dv_ref, dc_ref, dq_acc):
        p, kt = pl.program_id(1), pl.program_id(2)

        @pl.when(kt == 0)
        def _():
            dq_acc[...] = jnp.zeros_like(dq_acc)

        k = k_ref[0]
        v = v_ref[0]
        lane = lax.broadcasted_iota(jnp.int32, (tq, LANES), 1)
        row_l = lax.broadcasted_iota(jnp.int32, (tq, tq), 0)
        col_g = kt * tq + lax.broadcasted_iota(jnp.int32, (tq, tq), 1)
        masks = [lane < FOX_HEAD_DIM, lane >= FOX_HEAD_DIM]
        crs = [cr_ref[0, pl.ds(2 * p + hh, 1), :] for hh in range(2)]

        def step(qb, carry):
            dk, dv, dc0, dc1 = carry
            q0 = pl.multiple_of(qb * tq, tq)
            q = q_ref[0, pl.ds(q0, tq), :]
            do_b = do_ref[0, pl.ds(q0, tq), :].astype(BF16)
            dl_t = dl_ref[0, 0, pl.ds(q0, tq), :]
            lse_t = l_ref[0, 0, pl.ds(q0, tq), :]
            cc_t = cc_ref[0, pl.ds(q0, tq), :]
            causal = col_g <= q0 + row_l
            dq_t = jnp.zeros((tq, LANES), F32)
            dcs = []
            for hh in range(2):
                hm = masks[hh]
                qh = jnp.where(hm, q, jnp.zeros_like(q))
                kh = jnp.where(hm, k, jnp.zeros_like(k))
                doh = jnp.where(hm, do_b, jnp.zeros_like(do_b))
                s = lax.dot_general(qh, k, NT, preferred_element_type=F32) * scale
                s = s + (_lane_pick(cc_t, 2 * p + hh) - crs[hh])
                lse_h = _lane_pick(lse_t, hh * FOX_HEAD_DIM)
                pr = jnp.where(causal, jnp.exp(s - lse_h), 0.0)
                dp = lax.dot_general(doh, v, NT, preferred_element_type=F32)
                ds = pr * (dp - _lane_pick(dl_t, hh * FOX_HEAD_DIM))
                ds_b = ds.astype(BF16)
                dv = dv + lax.dot_general(pr.astype(BF16), doh, TN, preferred_element_type=F32)
                dk = dk + lax.dot_general(ds_b, qh, TN, preferred_element_type=F32) * scale
                dq_t = dq_t + jnp.dot(ds_b, kh, preferred_element_type=F32) * scale
                dcs.append(jnp.sum(ds, axis=0, keepdims=True))
            dq_acc[pl.ds(q0, tq), :] += dq_t
            return dk, dv, dc0 - dcs[0], dc1 - dcs[1]

        z = jnp.zeros((tq, LANES), F32)
        zr = jnp.zeros((1, tq), F32)
        dk, dv, dc0, dc1 = lax.fori_loop(kt, nq, step, (z, z, zr, zr))
        dk_ref[0] = dk.astype(BF16)
        dv_ref[0] = dv.astype(BF16)
        r8 = lax.broadcasted_iota(jnp.int32, (8, tq), 0)
        dc_ref[0, 0] = jnp.where(r8 == 0, dc0, jnp.where(r8 == 1, dc1, 0.0))

        @pl.when(kt == nq - 1)
        def _():
            dq_ref[0] = dq_acc[...].astype(BF16)

    full = lambda col: pl.BlockSpec((1, S, LANES), col)
    tile_out = lambda: pl.BlockSpec((1, tq, LANES), lambda b, p, t: (b, t, p))
    return _call(
        body, name=name, grid=(B, npair, nq),
        in_specs=[full(lambda b, p, t: (b, 0, p)),
                  pl.BlockSpec((1, tq, LANES), lambda b, p, t: (b, t, npair + p)),
                  pl.BlockSpec((1, tq, LANES), lambda b, p, t: (b, t, 2 * npair + p)),
                  full(lambda b, p, t: (b, 0, npair + p)),
                  pl.BlockSpec((1, 1, S, LANES), lambda b, p, t: (b, p, 0, 0)),
                  pl.BlockSpec((1, 1, S, LANES), lambda b, p, t: (b, p, 0, 0)),
                  full(lambda b, p, t: (b, 0, 0)),
                  pl.BlockSpec((1, 8, tq), lambda b, p, t: (b, 0, t))],
        out_specs=[full(lambda b, p, t: (b, 0, p)), tile_out(), tile_out(),
                   pl.BlockSpec((1, 1, 8, tq), lambda b, p, t: (b, p, 0, t))],
        out_shape=[jax.ShapeDtypeStruct((B, S, FOX_W), BF16)] * 3
        + [jax.ShapeDtypeStruct((B, npair, 8, S), F32)],
        scratch_shapes=[pltpu.VMEM((S, LANES), F32)],
        compiler_params=_params(("parallel", "parallel", "arbitrary"), 48 << 20),
    )(qkv, qkv, qkv, dcat, dlt, lse, c_col, c_row)


def xattn_fwd(qm, kv, *, name, tq=256):
    B, S, D = qm.shape
    M = kv.shape[1]
    tq = min(tq, S)
    inv = 1.0 / math.sqrt(MEM_HEAD_DIM)

    def body(q_ref, kv_ref, o_ref):
        for h in range(MEM_HEADS):
            c0 = h * MEM_HEAD_DIM
            qh = q_ref[0, :, c0:c0 + MEM_HEAD_DIM]
            kh = kv_ref[0, :, c0:c0 + MEM_HEAD_DIM]
            vh = kv_ref[0, :, D + c0:D + c0 + MEM_HEAD_DIM]
            s = lax.dot_general(qh, kh, NT, preferred_element_type=F32) * inv
            e = jnp.exp(s - jnp.max(s, axis=-1, keepdims=True))
            o = jnp.dot(e.astype(BF16), vh, preferred_element_type=F32) / jnp.sum(e, axis=-1, keepdims=True)
            o_ref[0, :, c0:c0 + MEM_HEAD_DIM] = o.astype(BF16)

    return _call(
        body, name=name, grid=(B, S // tq),
        in_specs=[pl.BlockSpec((1, tq, D), lambda b, i: (b, i, 0)),
                  pl.BlockSpec((1, M, 2 * D), lambda b, i: (b, 0, 0))],
        out_specs=pl.BlockSpec((1, tq, D), lambda b, i: (b, i, 0)),
        out_shape=jax.ShapeDtypeStruct((B, S, D), BF16),
        compiler_params=_params(("parallel", "parallel")),
    )(qm, kv)


def xattn_bwd(qm, kv, do, *, name, tq=256):
    B, S, D = qm.shape
    M = kv.shape[1]
    tq = min(tq, S)
    inv = 1.0 / math.sqrt(MEM_HEAD_DIM)

    def body(q_ref, kv_ref, do_ref, dq_ref, dkv_ref):
        @pl.when(pl.program_id(1) == 0)
        def _():
            dkv_ref[...] = jnp.zeros_like(dkv_ref)

        for h in range(MEM_HEADS):
            c0 = h * MEM_HEAD_DIM
            qh = q_ref[0, :, c0:c0 + MEM_HEAD_DIM]
            kh = kv_ref[0, :, c0:c0 + MEM_HEAD_DIM]
            vh = kv_ref[0, :, D + c0:D + c0 + MEM_HEAD_DIM]
            doh = do_ref[0, :, c0:c0 + MEM_HEAD_DIM]
            s = lax.dot_general(qh, kh, NT, preferred_element_type=F32) * inv
            e = jnp.exp(s - jnp.max(s, axis=-1, keepdims=True))
            pr = e / jnp.sum(e, axis=-1, keepdims=True)
            dp = lax.dot_general(doh, vh, NT, preferred_element_type=F32)
            ds = pr * (dp - jnp.sum(pr * dp, axis=-1, keepdims=True))
            ds_b = ds.astype(BF16)
            dq_ref[0, :, c0:c0 + MEM_HEAD_DIM] = (jnp.dot(ds_b, kh, preferred_element_type=F32) * inv).astype(BF16)
            dkv_ref[0, :, c0:c0 + MEM_HEAD_DIM] += lax.dot_general(ds_b, qh, TN, preferred_element_type=F32) * inv
            dkv_ref[0, :, D + c0:D + c0 + MEM_HEAD_DIM] += lax.dot_general(
                pr.astype(BF16), doh, TN, preferred_element_type=F32)

    row = pl.BlockSpec((1, tq, D), lambda b, i: (b, i, 0))
    kvs = pl.BlockSpec((1, M, 2 * D), lambda b, i: (b, 0, 0))
    return _call(
        body, name=name, grid=(B, S // tq), in_specs=[row, kvs, row], out_specs=[row, kvs],
        out_shape=[jax.ShapeDtypeStruct((B, S, D), BF16), jax.ShapeDtypeStruct((B, M, 2 * D), F32)],
        compiler_params=_params(("parallel", "arbitrary")),
    )(qm, kv, do)


def swiglu_fwd(gu, *, name, tm=256):
    T, F2 = gu.shape
    Fh = F2 // 2
    tm = min(tm, T)

    def body(gu_ref, o_ref):
        g = gu_ref[:, :Fh].astype(F32)
        u = gu_ref[:, Fh:].astype(F32)
        o_ref[...] = (g * _sigmoid(g) * u).astype(BF16)

    return _call(
        body, name=name, grid=(T // tm,),
        in_specs=[pl.BlockSpec((tm, F2), lambda i: (i, 0))],
        out_specs=pl.BlockSpec((tm, Fh), lambda i: (i, 0)),
        out_shape=jax.ShapeDtypeStruct((T, Fh), BF16),
        compiler_params=_params(("parallel",)),
    )(gu)


def swiglu_bwd(gu, dact, *, name, tm=256):
    T, F2 = gu.shape
    Fh = F2 // 2
    tm = min(tm, T)

    def body(gu_ref, d_ref, o_ref):
        g = gu_ref[:, :Fh].astype(F32)
        u = gu_ref[:, Fh:].astype(F32)
        d = d_ref[...].astype(F32)
        sg = _sigmoid(g)
        o_ref[:, :Fh] = (d * u * (sg * (1.0 + g * (1.0 - sg)))).astype(BF16)
        o_ref[:, Fh:] = (d * g * sg).astype(BF16)

    return _call(
        body, name=name, grid=(T // tm,),
        in_specs=[pl.BlockSpec((tm, F2), lambda i: (i, 0)), pl.BlockSpec((tm, Fh), lambda i: (i, 0))],
        out_specs=pl.BlockSpec((tm, F2), lambda i: (i, 0)),
        out_shape=jax.ShapeDtypeStruct((T, F2), BF16),
        compiler_params=_params(("parallel",)),
    )(gu, dact)


def local_step(x, mem, target, sp, wf):
    B, S, D = x.shape
    T = B * S
    M = mem.shape[1]
    row = lambda v: v.reshape(1, -1).astype(F32)
    g_mix, g_x, g_mem, g_ffn, g_final = (row(sp[k]) for k in ("g_mix", "g_x", "g_mem", "g_ffn", "g_final"))
    conv_b, ln_g, ln_b = row(sp["conv_b"]), row(sp["ln_g"]), row(sp["ln_b"])
    conv_w = jnp.pad(sp["conv_w"].astype(F32), ((0, HALO - CONV_K), (0, 0)))
    b_f = jnp.pad(row(sp["b_f"]), ((0, 0), (0, LANES - FOX_HEADS)))
    n_main = 2 * CONV_CH + 3 * FOX_W
    w_main = wf["w_in"][:, :n_main]
    w_f = jnp.pad(wf["w_in"][:, n_main:], ((0, 0), (0, LANES - FOX_HEADS)))

    x2d = x.reshape(T, D)
    h = rmsnorm_fwd(x2d, g_mix, name="rms_mix")
    z = matmul(h, w_main, out_dtype=BF16, name="mm_in")
    z3 = z.reshape(B, S, n_main)
    ug, qkv = z3[:, :, :2 * CONV_CH], z3[:, :, 2 * CONV_CH:]
    conv_out = conv_branch_fwd(ug, conv_w, conv_b, ln_g, ln_b, name="conv_fwd")
    f_raw, c_col, c_row = fgate_fwd(h.reshape(B, S, D), w_f, b_f, name="fgate_fwd")
    att, lse = fox_fwd(qkv, c_col, c_row, name="fox_fwd")
    cat = jnp.concatenate([conv_out, att], axis=-1).reshape(T, D)
    x1 = matmul(cat, wf["w_out"], out_dtype=F32, res=x2d, name="mm_out")
    hx = rmsnorm_fwd(x1, g_x, name="rms_x")
    qm = matmul(hx, wf["w_mq"], out_dtype=BF16, name="mm_mq")
    mem2d = mem.reshape(B * M, D)
    mem_n = rmsnorm_fwd(mem2d, g_mem, name="rms_mem")
    kv = matmul(mem_n, wf["w_mkv"], out_dtype=BF16, name="mm_mkv").reshape(B, M, 2 * D)
    o = xattn_fwd(qm.reshape(B, S, D), kv, name="xattn_fwd").reshape(T, D)
    x2 = matmul(o, wf["w_mo"], out_dtype=F32, res=x1, name="mm_mo")
    hf = rmsnorm_fwd(x2, g_ffn, name="rms_ffn")
    gu = matmul(hf, wf["w_gu"], out_dtype=BF16, name="mm_gu")
    act = swiglu_fwd(gu, name="swiglu_fwd")
    x3 = matmul(act, wf["w_down"], out_dtype=F32, res=x2, name="mm_down")
    dx3, dg_final, loss = final_loss_bwd(x3, g_final, target.reshape(T, D), name="loss_bwd")
    gw = {}
    gw["w_down"] = matmul(act, dx3, ta=True, out_dtype=BF16, tk=512, name="dw_down")
    dact = matmul(dx3, wf["w_down"], tb=True, out_dtype=BF16, name="dx_down")
    dgu = swiglu_bwd(gu, dact, name="swiglu_bwd")
    gw["w_gu"] = matmul(hf, dgu, ta=True, out_dtype=BF16, tk=512, name="dw_gu")
    dhf = matmul(dgu, wf["w_gu"], tb=True, out_dtype=BF16, tk=2816, name="dx_gu")
    dx2, dg_ffn = rmsnorm_bwd(x2, g_ffn, dhf, dx3, name="rms_ffn_bwd")
    gw["w_mo"] = matmul(o, dx2, ta=True, out_dtype=BF16, tk=512, name="dw_mo")
    do = matmul(dx2, wf["w_mo"], tb=True, out_dtype=BF16, name="dx_mo")
    dqm, dkv = xattn_bwd(qm.reshape(B, S, D), kv, do.reshape(B, S, D), name="xattn_bwd")
    dqm = dqm.reshape(T, D)
    dkv = dkv.reshape(B * M, 2 * D)
    gw["w_mq"] = matmul(hx, dqm, ta=True, out_dtype=BF16, tk=512, name="dw_mq")
    dhx = matmul(dqm, wf["w_mq"], tb=True, out_dtype=BF16, name="dx_mq")
    gw["w_mkv"] = matmul(mem_n, dkv, ta=True, out_dtype=BF16, tk=512, name="dw_mkv")
    dmem_n = matmul(dkv, wf["w_mkv"], tb=True, out_dtype=BF16, name="dx_mkv")
    _, dg_mem = rmsnorm_bwd(mem2d, g_mem, dmem_n, None, name="rms_mem_bwd")
    dx1, dg_x = rmsnorm_bwd(x1, g_x, dhx, dx2, name="rms_x_bwd")
    gw["w_out"] = matmul(cat, dx1, ta=True, out_dtype=BF16, tk=512, name="dw_out")
    dcat = matmul(dx1, wf["w_out"], tb=True, out_dtype=BF16, name="dx_out").reshape(B, S, D)
    dy, dconv_w, dvec = conv_branch_bwd_a(ug, dcat, conv_w, conv_b, ln_g, ln_b, name="conv_bwd_a")
    dug = conv_branch_bwd_b(ug, dy, conv_w, name="conv_bwd_b")
    dlt = fox_delta(qkv, dcat, lse, c_col, c_row, name="fox_delta")
    dq, dk, dv, dc4 = fox_bwd(qkv, dcat, dlt, lse, c_col, c_row, name="fox_bwd")
    dc_row = jnp.pad(dc4[:, :, :2, :].reshape(B, FOX_HEADS, S), ((0, 0), (0, LANES - FOX_HEADS), (0, 0)))
    df, db_f = fgate_bwd(dc_row, f_raw, b_f, name="fgate_bwd")
    dz = jnp.concatenate([dug, dq, dk, dv], axis=-1).reshape(T, n_main)
    df2 = df.reshape(T, LANES)
    dw_main = matmul(h, dz, ta=True, out_dtype=BF16, tk=512, name="dw_in")
    dw_f = matmul(h, df2, ta=True, out_dtype=BF16, tk=512, name="dw_f")
    gw["w_in"] = jnp.concatenate([dw_main, dw_f[:, :FOX_HEADS]], axis=-1)
    dh_f = matmul(df2, w_f, tb=True, out_dtype=F32, name="dx_f")
    dh = matmul(dz, w_main, tb=True, out_dtype=F32, res=dh_f, tk=1280, name="dx_in")
    dx, dg_mix = rmsnorm_bwd(x2d, g_mix, dh, dx1, name="rms_mix_bwd")
    gs = dict(g_mix=dg_mix, b_f=db_f[:, :FOX_HEADS], conv_w=dconv_w[:CONV_K], conv_b=dvec[0:1],
              ln_g=dvec[1:2], ln_b=dvec[2:3], g_x=dg_x, g_mem=dg_mem, g_ffn=dg_ffn, g_final=dg_final)
    return loss, dx.reshape(B, S, D), gs, gw


def _me():
    return lax.axis_index("x"), lax.axis_index("y"), lax.axis_index("c")


def all_gather(xs, *, name):
    shape, dtype = xs.shape, xs.dtype

    def body(x_ref, out_ref, send_sems, recv_sems, local_sem):
        x, y, c = _me()
        me, sibling = (x, y, c), (x, y, 1 - c)
        chips = [(1 - x, y), (x, 1 - y), (1 - x, 1 - y)]

        def slot(px, py, pc):
            return out_ref.at[4 * px + 2 * py + pc]

        def copy(k, block, to, src=None):
            return pltpu.make_async_remote_copy(
                src_ref=slot(*block) if src is None else src, dst_ref=slot(*block),
                send_sem=send_sems.at[k], recv_sem=recv_sems.at[k], device_id=to, device_id_type=MESH)

        mine = pltpu.make_async_copy(x_ref, slot(*me), local_sem)
        mine.start()
        first = [copy(0, me, sibling, src=x_ref)]
        first += [copy(1 + j, me, (*chip, c), src=x_ref) for j, chip in enumerate(chips)]
        for cp in first:
            cp.start()
        passed = [copy(4 + j, (*chip, c), sibling) for j, chip in enumerate(chips)]
        for j, chip in enumerate(chips):
            copy(1 + j, (*chip, c), me).wait_recv()
            passed[j].start()
        copy(0, sibling, me).wait_recv()
        for j, chip in enumerate(chips):
            copy(4 + j, (*chip, 1 - c), me).wait_recv()
        for cp in first + passed:
            cp.wait_send()
        mine.wait()

    return _call(
        body, name=name,
        in_specs=[pl.BlockSpec(memory_space=pl.ANY)], out_specs=pl.BlockSpec(memory_space=pl.ANY),
        out_shape=jax.ShapeDtypeStruct((N_DEV,) + shape, dtype),
        scratch_shapes=[pltpu.SemaphoreType.DMA((7,)), pltpu.SemaphoreType.DMA((7,)), pltpu.SemaphoreType.DMA],
    )(xs)


def sibling_exchange(g, *, name):
    _, _, R, C = g.shape

    def body(g_ref, out_ref, send_sem, recv_sem):
        x, y, c = _me()
        cp = pltpu.make_async_remote_copy(
            src_ref=g_ref.at[:, 1 - c], dst_ref=out_ref, send_sem=send_sem, recv_sem=recv_sem,
            device_id=(x, y, 1 - c), device_id_type=MESH)
        cp.start()
        cp.wait()

    return _call(
        body, name=name,
        in_specs=[pl.BlockSpec(memory_space=pl.ANY)], out_specs=pl.BlockSpec(memory_space=pl.ANY),
        out_shape=jax.ShapeDtypeStruct((4, R, C), g.dtype),
        scratch_shapes=[pltpu.SemaphoreType.DMA, pltpu.SemaphoreType.DMA],
    )(g)


def chip_exchange(p, *, name):
    _, R, C = p.shape

    def body(p_ref, out_ref, send_sems, recv_sems):
        x, y, c = _me()
        my_chip = 2 * x + y
        cps = []
        for k in range(1, 4):
            px, py = x ^ (k >> 1), y ^ (k & 1)
            cps.append(pltpu.make_async_remote_copy(
                src_ref=p_ref.at[2 * px + py], dst_ref=out_ref.at[my_chip],
                send_sem=send_sems.at[k - 1], recv_sem=recv_sems.at[k - 1],
                device_id=(px, py, c), device_id_type=MESH))
        for cp in cps:
            cp.start()
        for k in range(1, 4):
            px, py = x ^ (k >> 1), y ^ (k & 1)
            pltpu.make_async_remote_copy(
                src_ref=p_ref.at[my_chip], dst_ref=out_ref.at[2 * px + py],
                send_sem=send_sems.at[k - 1], recv_sem=recv_sems.at[k - 1],
                device_id=(px, py, c), device_id_type=MESH).wait_recv()
        for cp in cps:
            cp.wait_send()

    return _call(
        body, name=name,
        in_specs=[pl.BlockSpec(memory_space=pl.ANY)], out_specs=pl.BlockSpec(memory_space=pl.ANY),
        out_shape=jax.ShapeDtypeStruct((4, R, C), p.dtype),
        scratch_shapes=[pltpu.SemaphoreType.DMA((3,)), pltpu.SemaphoreType.DMA((3,))],
    )(p)


def pair_sum(g, got, *, name, tr=128):
    _, _, R, C = g.shape
    tr = min(tr, R)
    assert R % tr == 0, (name, R, tr)

    def body(g_ref, got_ref, o_ref):
        mine = jnp.where(lax.axis_index("c") == 0, g_ref[:, 0], g_ref[:, 1])
        o_ref[...] = (mine.astype(F32) + got_ref[...].astype(F32)).astype(o_ref.dtype)

    return _call(
        body, name=name, grid=(R // tr,),
        in_specs=[pl.BlockSpec((4, 2, tr, C), lambda i: (0, 0, i, 0)), pl.BlockSpec((4, tr, C), lambda i: (0, i, 0))],
        out_specs=pl.BlockSpec((4, tr, C), lambda i: (0, i, 0)),
        out_shape=jax.ShapeDtypeStruct((4, R, C), g.dtype),
        compiler_params=_params(("parallel",)),
    )(g, got)


def chip_sum(p, got, *, name, tr=128):
    _, R, C = p.shape
    tr = min(tr, R)
    assert R % tr == 0, (name, R, tr)

    def body(p_ref, got_ref, o_ref):
        my_chip = 2 * lax.axis_index("x") + lax.axis_index("y")
        acc = jnp.zeros((tr, C), F32)
        for j in range(4):
            acc = acc + jnp.where(my_chip == j, p_ref[j], got_ref[j]).astype(F32)
        o_ref[...] = acc

    spec = pl.BlockSpec((4, tr, C), lambda i: (0, i, 0))
    return _call(
        body, name=name, grid=(R // tr,), in_specs=[spec, spec],
        out_specs=pl.BlockSpec((tr, C), lambda i: (i, 0)),
        out_shape=jax.ShapeDtypeStruct((R, C), F32),
        compiler_params=_params(("parallel",)),
    )(p, got)


def rows_sum(g8, *, name):
    _, R, C = g8.shape

    def body(g_ref, o_ref):
        acc = g_ref[0]
        for j in range(1, N_DEV):
            acc = acc + g_ref[j]
        o_ref[...] = acc

    return _call(body, name=name, out_shape=jax.ShapeDtypeStruct((R, C), F32))(g8)


def _adamw_math(w, g, m, v):
    m = ADAM_B1 * m + (1.0 - ADAM_B1) * g
    v = ADAM_B2 * v + (1.0 - ADAM_B2) * (g * g)
    m_hat = m / (1.0 - ADAM_B1 ** ADAM_STEP)
    v_hat = v / (1.0 - ADAM_B2 ** ADAM_STEP)
    delta = -ADAM_LR * (m_hat / (jnp.sqrt(v_hat) + ADAM_EPS) + ADAM_WD * w)
    return delta, m, v


def adamw(w, g, m, v, *, name, tr=256):
    R, C = w.shape
    tr = tr if R % tr == 0 else R

    def body(w_ref, g_ref, m_ref, v_ref, d_ref, mo_ref, vo_ref):
        d, mn, vn = _adamw_math(w_ref[...], g_ref[...], m_ref[...], v_ref[...])
        d_ref[...] = d
        mo_ref[...] = mn
        vo_ref[...] = vn

    spec = pl.BlockSpec((tr, C), lambda i: (i, 0))
    return _call(
        body, name=name, grid=(R // tr,), in_specs=[spec] * 4, out_specs=[spec] * 3,
        out_shape=[jax.ShapeDtypeStruct((R, C), F32)] * 3,
        compiler_params=_params(("parallel",)),
    )(w, g, m, v)


BIG = ("w_in", "w_out", "w_mq", "w_mkv", "w_mo", "w_gu", "w_down")
COL_SHARDED = ("w_in", "w_mkv", "w_gu")
SMALL = ("g_mix", "b_f", "conv_w", "conv_b", "ln_g", "ln_b", "g_x", "g_mem", "g_ffn", "g_final")
PACK_C = 1024
PACK_TILE = 128


def _pack_rows(shard_shapes):
    rows, r0 = {}, 0
    for n in BIG:
        r = math.prod(shard_shapes[n]) // PACK_C
        rows[n] = (r0, r)
        r0 += r
    return rows, -(-r0 // PACK_TILE) * PACK_TILE


def _full_from_gathered(n, blk, shard_shape):
    if n in COL_SHARDED:
        rr, cc = shard_shape
        return blk.reshape(N_DEV, rr, cc).transpose(1, 0, 2).reshape(rr, N_DEV * cc)
    return blk.reshape(N_DEV * shard_shape[0], shard_shape[1])


def _shards_from_full(n, g, shard_shape):
    rr, cc = shard_shape
    if n in COL_SHARDED:
        g = g.reshape(rr, N_DEV, cc).transpose(1, 0, 2)
    return g.reshape(N_DEV, (rr * cc) // PACK_C, PACK_C)


def _small_layout():
    sizes = dict(g_mix=1024, b_f=8, conv_w=CONV_K * CONV_CH, conv_b=512, ln_g=512, ln_b=512, g_x=1024,
                 g_mem=1024, g_ffn=1024, g_final=1024, loss=1)
    lay, r0 = {}, 0
    for n, sz in sizes.items():
        r = -(-sz // LANES)
        lay[n] = (r0, r, sz)
        r0 += r
    return lay, -(-r0 // 8) * 8


def kernel(x, mem, g_mix, w_in, b_f, conv_w, conv_b, ln_g, ln_b, w_out, g_x, g_mem, w_mq, w_mkv, w_mo, g_ffn, w_gu, w_down, g_final, loss_target, m_g_mix, m_w_in, m_b_f, m_conv_w, m_conv_b, m_ln_g, m_ln_b, m_w_out, m_g_x, m_g_mem, m_w_mq, m_w_mkv, m_w_mo, m_g_ffn, m_w_gu, m_w_down, m_g_final, v_g_mix, v_w_in, v_b_f, v_conv_w, v_conv_b, v_ln_g, v_ln_b, v_w_out, v_g_x, v_g_mem, v_w_mq, v_w_mkv, v_w_mo, v_g_ffn, v_w_gu, v_w_down, v_g_final):
    names = ["g_mix", "w_in", "b_f", "conv_w", "conv_b", "ln_g", "ln_b", "w_out", "g_x", "g_mem", "w_mq",
             "w_mkv", "w_mo", "g_ffn", "w_gu", "w_down", "g_final"]
    W = dict(zip(names, [g_mix, w_in, b_f, conv_w, conv_b, ln_g, ln_b, w_out, g_x, g_mem, w_mq, w_mkv, w_mo,
                         g_ffn, w_gu, w_down, g_final]))
    Mo = dict(zip(names, [m_g_mix, m_w_in, m_b_f, m_conv_w, m_conv_b, m_ln_g, m_ln_b, m_w_out, m_g_x, m_g_mem,
                          m_w_mq, m_w_mkv, m_w_mo, m_g_ffn, m_w_gu, m_w_down, m_g_final]))
    Vo = dict(zip(names, [v_g_mix, v_w_in, v_b_f, v_conv_w, v_conv_b, v_ln_g, v_ln_b, v_w_out, v_g_x, v_g_mem,
                          v_w_mq, v_w_mkv, v_w_mo, v_g_ffn, v_w_gu, v_w_down, v_g_final]))
    dev = 4 * lax.axis_index("x") + 2 * lax.axis_index("y") + lax.axis_index("c")

    shard2d = {n: W[n].reshape(W[n].shape[-2:]) for n in BIG}
    shard_shapes = {n: shard2d[n].shape for n in BIG}
    rows, rp = _pack_rows(shard_shapes)
    pack = jnp.concatenate([shard2d[n].astype(BF16).reshape(-1, PACK_C) for n in BIG], axis=0)
    pack = jnp.pad(pack, ((0, rp - pack.shape[0]), (0, 0)))
    gathered = all_gather(pack, name="ag_weights")
    wf = {n: _full_from_gathered(n, gathered[:, rows[n][0]:rows[n][0] + rows[n][1]], shard_shapes[n])
          for n in BIG}
    cw_shard = jnp.pad(conv_w.reshape(conv_w.shape[-2:]), ((0, HALO - CONV_K), (0, 0)))
    cw8 = all_gather(cw_shard, name="ag_conv_w")
    cw_full = cw8.transpose(1, 0, 2).reshape(HALO, -1)[:CONV_K]

    sp = dict(g_mix=g_mix, b_f=b_f, conv_w=cw_full, conv_b=conv_b, ln_g=ln_g, ln_b=ln_b, g_x=g_x, g_mem=g_mem,
              g_ffn=g_ffn, g_final=g_final)
    loss_blk, grad_x, gs, gw = local_step(x, mem, loss_target, sp, wf)

    gpack = jnp.concatenate([_shards_from_full(n, gw[n], shard_shapes[n]) for n in BIG], axis=1)
    gpack = jnp.pad(gpack, ((0, 0), (0, rp - gpack.shape[1]), (0, 0)))
    g42 = gpack.reshape(4, 2, rp, PACK_C)
    got1 = sibling_exchange(g42, name="rs_sibling")
    part = pair_sum(g42, got1, name="rs_pair_sum")
    got2 = chip_exchange(part, name="rs_chips")
    gsum = chip_sum(part, got2, name="rs_chip_sum")

    lay, rs = _small_layout()
    small = {**{n: gs[n] for n in SMALL}, "loss": loss_blk[:, :1]}
    parts = []
    for n, (r0, r, sz) in lay.items():
        flat = small[n].reshape(-1).astype(F32)
        parts.append(jnp.pad(flat, (0, r * LANES - sz)).reshape(r, LANES))
    spack = jnp.concatenate(parts, axis=0)
    spack = jnp.pad(spack, ((0, rs - spack.shape[0]), (0, 0)))
    ssum = rows_sum(all_gather(spack, name="ag_small"), name="small_sum")
    gsmall = {n: ssum[r0:r0 + r].reshape(-1)[:sz] for n, (r0, r, sz) in lay.items()}
    loss = gsmall["loss"].reshape(())

    grads = {}
    for n in BIG:
        r0, r = rows[n]
        grads[n] = gsum[r0:r0 + r].reshape(W[n].shape)
    for n in SMALL:
        if n == "conv_w":
            full = gsmall[n].reshape(CONV_K, CONV_CH)
            ncol = conv_w.shape[-1]
            grads[n] = lax.dynamic_slice(full, (0, dev * ncol), (CONV_K, ncol)).reshape(conv_w.shape)
        else:
            grads[n] = gsmall[n].reshape(W[n].shape)
    delta, new_m, new_v = {}, {}, {}
    for n in names:
        shp = W[n].shape
        two = (lambda a: a.reshape(-1, shp[-1]))
        d, mn, vn = adamw(two(W[n]), two(grads[n]), two(Mo[n]), two(Vo[n]), name="adamw_" + n)
        delta[n], new_m[n], new_v[n] = d.reshape(shp), mn.reshape(shp), vn.reshape(shp)
    return (loss, grad_x, *[grads[n] for n in names], *[delta[n] for n in names],
            *[new_m[n] for n in names], *[new_v[n] for n in names])
```

```python
import functools
import math

import jax
import jax.numpy as jnp
from jax import lax
from jax.experimental import pallas as pl
from jax.experimental.pallas import tpu as pltpu

F32 = jnp.float32
BF16 = jnp.bfloat16
EPS = 1e-6
N_DEV = 8
CONV_CH = 512
CONV_K = 31
FOX_HEADS = 8
FOX_HEAD_DIM = 64
FOX_W = 512
MEM_HEADS = 4
MEM_HEAD_DIM = 256
HALO = 32
LANES = 128
ADAM_LR, ADAM_B1, ADAM_B2, ADAM_EPS, ADAM_WD, ADAM_STEP = 0.001, 0.9, 0.999, 1e-08, 0.01, 10
NEG = -1e30
VMEM_CAP = 60 * 1024 * 1024
MESH = pl.DeviceIdType.MESH


def _call(body, **kw):
    return pl.pallas_call(body, **kw)


def _params(sem=None, vmem=None):
    kw = {}
    if sem is not None:
        kw["dimension_semantics"] = sem
    if vmem is not None:
        kw["vmem_limit_bytes"] = int(min(VMEM_CAP, vmem))
    return pltpu.CompilerParams(**kw)


def _nbytes(shape, dtype):
    return math.prod(shape) * jnp.dtype(dtype).itemsize


def _pick(n, target):
    best = None
    for d in range(LANES, min(n, target) + 1, LANES):
        if n % d == 0:
            best = d
    return n if best is None else best


def matmul(a, b, *, tb=False, out_dtype, res=None, tm=512, tn=512, tk=None, name):
    M, K = a.shape
    N = b.shape[0] if tb else b.shape[1]
    assert (b.shape[1] if tb else b.shape[0]) == K
    tm, tn = _pick(M, tm), _pick(N, tn)
    tk = K if tk is None else _pick(K, tk)
    assert M % tm == 0 and N % tn == 0 and K % tk == 0, (name, M, N, K, tm, tn, tk)
    nk = K // tk
    dn = (((1,), (1 if tb else 0,)), ((), ()))

    def body(*refs):
        if res is not None:
            a_ref, b_ref, r_ref, o_ref = refs[:4]
        else:
            a_ref, b_ref, o_ref = refs[:3]
        p = lax.dot_general(a_ref[...].astype(BF16), b_ref[...].astype(BF16), dn,
                            preferred_element_type=F32)

        def finish(acc):
            if res is not None:
                acc = acc + r_ref[...].astype(F32)
            o_ref[...] = acc.astype(out_dtype)

        if nk == 1:
            finish(p)
        else:
            acc_ref = refs[-1]
            k = pl.program_id(2)

            @pl.when(k == 0)
            def _():
                acc_ref[...] = p

            @pl.when(k > 0)
            def _():
                acc_ref[...] += p

            @pl.when(k == nk - 1)
            def _():
                finish(acc_ref[...])

    a_spec = pl.BlockSpec((tm, tk), lambda i, j, k: (i, k))
    b_spec = pl.BlockSpec((tn, tk), lambda i, j, k: (j, k)) if tb else pl.BlockSpec((tk, tn), lambda i, j, k: (k, j))
    o_spec = pl.BlockSpec((tm, tn), lambda i, j, k: (i, j))
    in_specs, args = [a_spec, b_spec], [a, b]
    est = 2 * (_nbytes((tm, tk), a.dtype) + _nbytes((tk, tn), b.dtype) + _nbytes((tm, tn), out_dtype))
    est += (a.dtype != BF16) * _nbytes((tm, tk), BF16) + (b.dtype != BF16) * _nbytes((tk, tn), BF16)
    est += 2 * _nbytes((tm, tn), F32)
    if res is not None:
        in_specs.append(o_spec)
        args.append(res)
        est += 2 * _nbytes((tm, tn), res.dtype)
    return _call(
        body, name=name, grid=(M // tm, N // tn, nk),
        in_specs=in_specs, out_specs=o_spec,
        out_shape=jax.ShapeDtypeStruct((M, N), out_dtype),
        scratch_shapes=[] if nk == 1 else [pltpu.VMEM((tm, tn), F32)],
        compiler_params=_params(("parallel", "parallel", "arbitrary"), est + (8 << 20)),
    )(*args)


def _rms_scale(x):
    return lax.rsqrt(jnp.mean(x * x, axis=-1, keepdims=True) + EPS)


def rmsnorm_fwd(x, g, *, name, tm=512):
    T, D = x.shape
    tm = min(tm, T)

    def body(x_ref, g_ref, o_ref):
        xv = x_ref[...]
        o_ref[...] = (xv * _rms_scale(xv) * g_ref[...]).astype(BF16)

    return _call(
        body, name=name, grid=(T // tm,),
        in_specs=[pl.BlockSpec((tm, D), lambda i: (i, 0)), pl.BlockSpec((1, D), lambda i: (0, 0))],
        out_specs=pl.BlockSpec((tm, D), lambda i: (i, 0)),
        out_shape=jax.ShapeDtypeStruct((T, D), BF16),
        compiler_params=_params(("parallel",)),
    )(x, g)


def _rms_bwd_math(xv, gv, dh):
    r = _rms_scale(xv)
    xh = xv * r
    dg = jnp.sum(dh * xh, axis=0, keepdims=True)
    dxh = dh * gv
    dx = r * (dxh - xh * jnp.mean(dxh * xh, axis=-1, keepdims=True))
    return dx, dg


def rmsnorm_bwd(x, g, dh, dres, *, name, tm=256):
    T, D = x.shape
    tm = min(tm, T)

    def body(*refs):
        if dres is not None:
            x_ref, g_ref, dh_ref, dr_ref, dx_ref, dg_ref = refs
        else:
            x_ref, g_ref, dh_ref, dx_ref, dg_ref = refs
        dx, dg = _rms_bwd_math(x_ref[...], g_ref[...], dh_ref[...].astype(F32))
        if dres is not None:
            dx = dx + dr_ref[...]
        dx_ref[...] = dx

        @pl.when(pl.program_id(0) == 0)
        def _():
            dg_ref[...] = jnp.zeros_like(dg_ref)

        dg_ref[...] += dg

    row = pl.BlockSpec((tm, D), lambda i: (i, 0))
    vec = pl.BlockSpec((1, D), lambda i: (0, 0))
    ins, args = [row, vec, row], [x, g, dh]
    if dres is not None:
        ins.append(row)
        args.append(dres)
    return _call(
        body, name=name, grid=(T // tm,), in_specs=ins, out_specs=[row, vec],
        out_shape=[jax.ShapeDtypeStruct((T, D), F32), jax.ShapeDtypeStruct((1, D), F32)],
        compiler_params=_params(("arbitrary",)),
    )(*args)


def final_loss_bwd(x, g, target, *, name, tm=256):
    T, D = x.shape
    tm = min(tm, T)

    def body(x_ref, g_ref, t_ref, dx_ref, dg_ref, l_ref):
        xv, gv = x_ref[...], g_ref[...]
        e = xv * _rms_scale(xv) * gv - t_ref[...]
        part = 0.5 * jnp.sum(jnp.mean(e * e, axis=-1, keepdims=True), axis=0, keepdims=True)
        dx, dg = _rms_bwd_math(xv, gv, e * (1.0 / D))
        dx_ref[...] = dx

        @pl.when(pl.program_id(0) == 0)
        def _():
            dg_ref[...] = jnp.zeros_like(dg_ref)
            l_ref[...] = jnp.zeros_like(l_ref)

        dg_ref[...] += dg
        l_ref[...] += jnp.broadcast_to(part, l_ref.shape)

    row = pl.BlockSpec((tm, D), lambda i: (i, 0))
    vec = pl.BlockSpec((1, D), lambda i: (0, 0))
    return _call(
        body, name=name, grid=(T // tm,), in_specs=[row, vec, row],
        out_specs=[row, vec, pl.BlockSpec((1, LANES), lambda i: (0, 0))],
        out_shape=[jax.ShapeDtypeStruct((T, D), F32), jax.ShapeDtypeStruct((1, D), F32),
                   jax.ShapeDtypeStruct((1, LANES), F32)],
        compiler_params=_params(("arbitrary",)),
    )(x, g, target)


def _sigmoid(v):
    return 1.0 / (1.0 + jnp.exp(-v))


def _glu(blk):
    u = blk[:, :CONV_CH].astype(F32)
    gt = blk[:, CONV_CH:].astype(F32)
    return u * _sigmoid(gt)


def _fill_causal_ext(ext, cur_ref, halo_ref, s, ts):
    ext[pl.ds(HALO, ts), :] = _glu(cur_ref[0])
    hal = _glu(halo_ref[0])
    ext[pl.ds(0, HALO), :] = jnp.where(s > 0, hal, 0.0)


def _causal_conv(ext, w_ref, ts):
    acc = jnp.zeros((ts, CONV_CH), F32)
    for j in range(CONV_K):
        acc = acc + ext[pl.ds(HALO - (CONV_K - 1) + j, ts), :] * w_ref[pl.ds(j, 1), :]
    return acc


def _ln_stats(y):
    mu = jnp.mean(y, axis=-1, keepdims=True)
    yc = y - mu
    rstd = lax.rsqrt(jnp.mean(yc * yc, axis=-1, keepdims=True) + EPS)
    return yc * rstd, rstd


def _conv_specs(ts, S):
    nh = ts // HALO
    cur = pl.BlockSpec((1, ts, 2 * CONV_CH), lambda b, s: (b, s, 0))
    halo = pl.BlockSpec((1, HALO, 2 * CONV_CH), lambda b, s: (b, jnp.maximum(s * nh - 1, 0), 0))
    w = pl.BlockSpec((HALO, CONV_CH), lambda b, s: (0, 0))
    vec = pl.BlockSpec((1, CONV_CH), lambda b, s: (0, 0))
    return cur, halo, w, vec


def conv_branch_fwd(ug, conv_w, conv_b, ln_g, ln_b, *, name, ts=256):
    B, S, _ = ug.shape
    ts = min(ts, S)
    cur, halo, w, vec = _conv_specs(ts, S)

    def body(cur_ref, halo_ref, w_ref, cb_ref, lg_ref, lb_ref, o_ref, ext):
        _fill_causal_ext(ext, cur_ref, halo_ref, pl.program_id(1), ts)
        y = _causal_conv(ext, w_ref, ts) + cb_ref[...]
        yh, _ = _ln_stats(y)
        ln = yh * lg_ref[...] + lb_ref[...]
        o_ref[0] = (ln * _sigmoid(ln)).astype(BF16)

    return _call(
        body, name=name, grid=(B, S // ts), in_specs=[cur, halo, w, vec, vec, vec],
        out_specs=pl.BlockSpec((1, ts, CONV_CH), lambda b, s: (b, s, 0)),
        out_shape=jax.ShapeDtypeStruct((B, S, CONV_CH), BF16),
        scratch_shapes=[pltpu.VMEM((ts + HALO, CONV_CH), F32)],
        compiler_params=_params(("parallel", "parallel")),
    )(ug, ug, conv_w, conv_b, ln_g, ln_b)


def conv_branch_bwd_a(ug, dcat, conv_w, conv_b, ln_g, ln_b, *, name, ts=256):
    B, S, _ = ug.shape
    ts = min(ts, S)
    cur, halo, w, vec = _conv_specs(ts, S)

    def body(cur_ref, halo_ref, d_ref, w_ref, cb_ref, lg_ref, lb_ref, dy_ref, dw_ref, dv_ref, ext):
        _fill_causal_ext(ext, cur_ref, halo_ref, pl.program_id(1), ts)
        y = _causal_conv(ext, w_ref, ts) + cb_ref[...]
        yh, rstd = _ln_stats(y)
        lg = lg_ref[...]
        ln = yh * lg + lb_ref[...]
        sg = _sigmoid(ln)
        dln = d_ref[0].astype(F32) * (sg * (1.0 + ln * (1.0 - sg)))
        dyh = dln * lg
        dy = rstd * (dyh - jnp.mean(dyh, axis=-1, keepdims=True)
                     - yh * jnp.mean(dyh * yh, axis=-1, keepdims=True))
        dy_ref[0] = dy

        @pl.when((pl.program_id(0) == 0) & (pl.program_id(1) == 0))
        def _():
            dw_ref[...] = jnp.zeros_like(dw_ref)
            dv_ref[...] = jnp.zeros_like(dv_ref)

        dv_ref[pl.ds(0, 1), :] += jnp.sum(dy, axis=0, keepdims=True)
        dv_ref[pl.ds(1, 1), :] += jnp.sum(dln * yh, axis=0, keepdims=True)
        dv_ref[pl.ds(2, 1), :] += jnp.sum(dln, axis=0, keepdims=True)
        for j in range(CONV_K):
            tap = ext[pl.ds(HALO - (CONV_K - 1) + j, ts), :]
            dw_ref[pl.ds(j, 1), :] += jnp.sum(dy * tap, axis=0, keepdims=True)

    return _call(
        body, name=name, grid=(B, S // ts),
        in_specs=[cur, halo, pl.BlockSpec((1, ts, CONV_CH), lambda b, s: (b, s, 0)), w, vec, vec, vec],
        out_specs=[pl.BlockSpec((1, ts, CONV_CH), lambda b, s: (b, s, 0)),
                   pl.BlockSpec((HALO, CONV_CH), lambda b, s: (0, 0)),
                   pl.BlockSpec((8, CONV_CH), lambda b, s: (0, 0))],
        out_shape=[jax.ShapeDtypeStruct((B, S, CONV_CH), F32),
                   jax.ShapeDtypeStruct((HALO, CONV_CH), F32),
                   jax.ShapeDtypeStruct((8, CONV_CH), F32)],
        scratch_shapes=[pltpu.VMEM((ts + HALO, CONV_CH), F32)],
        compiler_params=_params(("arbitrary", "arbitrary")),
    )(ug, ug, dcat, conv_w, conv_b, ln_g, ln_b)


def conv_branch_bwd_b(ug, dy, conv_w, *, name, ts=256):
    B, S, _ = ug.shape
    ts = min(ts, S)
    nh, n_halo = ts // HALO, S // HALO

    def body(cur_ref, dy_ref, nxt_ref, w_ref, o_ref, ext):
        last = pl.program_id(1) == pl.num_programs(1) - 1
        ext[pl.ds(0, ts), :] = dy_ref[0]
        ext[pl.ds(ts, HALO), :] = jnp.where(last, 0.0, nxt_ref[0])
        da = jnp.zeros((ts, CONV_CH), F32)
        for j in range(CONV_K):
            da = da + ext[pl.ds(CONV_K - 1 - j, ts), :] * w_ref[pl.ds(j, 1), :]
        blk = cur_ref[0]
        u = blk[:, :CONV_CH].astype(F32)
        sg = _sigmoid(blk[:, CONV_CH:].astype(F32))
        o_ref[0, :, :CONV_CH] = (da * sg).astype(BF16)
        o_ref[0, :, CONV_CH:] = (da * u * sg * (1.0 - sg)).astype(BF16)

    return _call(
        body, name=name, grid=(B, S // ts),
        in_specs=[pl.BlockSpec((1, ts, 2 * CONV_CH), lambda b, s: (b, s, 0)),
                  pl.BlockSpec((1, ts, CONV_CH), lambda b, s: (b, s, 0)),
                  pl.BlockSpec((1, HALO, CONV_CH), lambda b, s: (b, jnp.minimum((s + 1) * nh, n_halo - 1), 0)),
                  pl.BlockSpec((HALO, CONV_CH), lambda b, s: (0, 0))],
        out_specs=pl.BlockSpec((1, ts, 2 * CONV_CH), lambda b, s: (b, s, 0)),
        out_shape=jax.ShapeDtypeStruct((B, S, 2 * CONV_CH), BF16),
        scratch_shapes=[pltpu.VMEM((ts + HALO, CONV_CH), F32)],
        compiler_params=_params(("parallel", "parallel")),
    )(ug, dy, dy, conv_w)


def _tri(n, lower):
    r = lax.broadcasted_iota(jnp.int32, (n, n), 0)
    c = lax.broadcasted_iota(jnp.int32, (n, n), 1)
    return ((r >= c) if lower else (r <= c)).astype(F32)


def _eye(n):
    r = lax.broadcasted_iota(jnp.int32, (n, n), 0)
    c = lax.broadcasted_iota(jnp.int32, (n, n), 1)
    return (r == c).astype(F32)


def _dot_hi(a, b, dn):
    return lax.dot_general(a, b, dn, precision=lax.Precision.HIGHEST, preferred_element_type=F32)


NN = (((1,), (0,)), ((), ()))
NT = (((1,), (1,)), ((), ()))
TN = (((0,), (0,)), ((), ()))


def _log_sigmoid(v):
    e = jnp.exp(-jnp.abs(v))
    log1p_e = jnp.where(e < 1e-3, e * (1.0 - 0.5 * e), jnp.log(1.0 + e))
    return jnp.minimum(v, 0.0) - log1p_e


def fgate_fwd(h, w_f, b_f, *, name, ts=256):
    B, S, D = h.shape
    ts = min(ts, S)

    def body(h_ref, w_ref, b_ref, f_ref, cc_ref, cr_ref, carry):
        @pl.when(pl.program_id(1) == 0)
        def _():
            carry[...] = jnp.zeros_like(carry)

        f = jnp.dot(h_ref[0], w_ref[...], preferred_element_type=F32)
        f_ref[0] = f
        logf = _log_sigmoid(f + b_ref[...])
        c = _dot_hi(_tri(ts, True), logf, NN) + carry[pl.ds(0, 1), :]
        cc_ref[0] = c
        carry[pl.ds(0, 1), :] = c[ts - 1:ts, :]
        cr_ref[0] = _dot_hi(_eye(LANES), c, NT)

    return _call(
        body, name=name, grid=(B, S // ts),
        in_specs=[pl.BlockSpec((1, ts, D), lambda b, s: (b, s, 0)),
                  pl.BlockSpec((D, LANES), lambda b, s: (0, 0)),
                  pl.BlockSpec((1, LANES), lambda b, s: (0, 0))],
        out_specs=[pl.BlockSpec((1, ts, LANES), lambda b, s: (b, s, 0)),
                   pl.BlockSpec((1, ts, LANES), lambda b, s: (b, s, 0)),
                   pl.BlockSpec((1, LANES, ts), lambda b, s: (b, 0, s))],
        out_shape=[jax.ShapeDtypeStruct((B, S, LANES), F32), jax.ShapeDtypeStruct((B, S, LANES), F32),
                   jax.ShapeDtypeStruct((B, LANES, S), F32)],
        scratch_shapes=[pltpu.VMEM((8, LANES), F32)],
        compiler_params=_params(("parallel", "arbitrary")),
    )(h, w_f, b_f)


def fgate_bwd(dc_row, f, b_f, *, name, ts=256):
    B, S, _ = f.shape
    ts = min(ts, S)
    ns = S // ts

    def body(dc_ref, f_ref, b_ref, df_ref, db_ref, carry):
        @pl.when(pl.program_id(1) == 0)
        def _():
            carry[...] = jnp.zeros_like(carry)

        @pl.when((pl.program_id(0) == 0) & (pl.program_id(1) == 0))
        def _():
            db_ref[...] = jnp.zeros_like(db_ref)

        dlogf = _dot_hi(_tri(ts, False), dc_ref[0], NT) + carry[pl.ds(0, 1), :]
        carry[pl.ds(0, 1), :] = dlogf[0:1, :]
        df = dlogf * _sigmoid(-(f_ref[0] + b_ref[...]))
        df_ref[0] = df.astype(BF16)
        db_ref[...] += jnp.sum(df, axis=0, keepdims=True)

    return _call(
        body, name=name, grid=(B, ns),
        in_specs=[pl.BlockSpec((1, LANES, ts), lambda b, s: (b, 0, ns - 1 - s)),
                  pl.BlockSpec((1, ts, LANES), lambda b, s: (b, ns - 1 - s, 0)),
                  pl.BlockSpec((1, LANES), lambda b, s: (0, 0))],
        out_specs=[pl.BlockSpec((1, ts, LANES), lambda b, s: (b, ns - 1 - s, 0)),
                   pl.BlockSpec((1, LANES), lambda b, s: (0, 0))],
        out_shape=[jax.ShapeDtypeStruct((B, S, LANES), BF16), jax.ShapeDtypeStruct((1, LANES), F32)],
        scratch_shapes=[pltpu.VMEM((8, LANES), F32)],
        compiler_params=_params(("arbitrary", "arbitrary")),
    )(dc_row, f, b_f)


def _lane_pick(tile, idx):
    lane = lax.broadcasted_iota(jnp.int32, tile.shape, 1)
    return jnp.sum(jnp.where(lane == idx, tile, 0.0), axis=-1, keepdims=True)


def fox_fwd(qkv, c_col, c_row, *, name, tq=256):
    B, S, _ = qkv.shape
    tq = min(tq, S)
    scale = 1.0 / math.sqrt(FOX_HEAD_DIM)
    npair = FOX_HEADS // 2

    def body(q_ref, k_ref, v_ref, cc_ref, cr_ref, o_ref, l_ref):
        p, qi = pl.program_id(1), pl.program_id(2)
        q = q_ref[0]
        cc_tile = cc_ref[0]
        lane = lax.broadcasted_iota(jnp.int32, (tq, LANES), 1)
        row_g = qi * tq + lax.broadcasted_iota(jnp.int32, (tq, tq), 0)
        col_l = lax.broadcasted_iota(jnp.int32, (tq, tq), 1)
        out = jnp.zeros((tq, LANES), F32)
        lse = jnp.zeros((tq, LANES), F32)
        for hh in range(2):
            hmask = (lane < FOX_HEAD_DIM) if hh == 0 else (lane >= FOX_HEAD_DIM)
            head = 2 * p + hh
            qh = jnp.where(hmask, q, jnp.zeros_like(q))
            cc = _lane_pick(cc_tile, head)

            def step(kb, carry, qh=qh, cc=cc, hmask=hmask, head=head):
                m, l, acc = carry
                k0 = pl.multiple_of(kb * tq, tq)
                k = k_ref[0, pl.ds(k0, tq), :]
                v = v_ref[0, pl.ds(k0, tq), :]
                vh = jnp.where(hmask, v, jnp.zeros_like(v))
                s = lax.dot_general(qh, k, NT, preferred_element_type=F32) * scale
                s = s + (cc - cr_ref[0, pl.ds(head, 1), pl.ds(k0, tq)])
                s = jnp.where(k0 + col_l <= row_g, s, NEG)
                m_new = jnp.maximum(m, jnp.max(s, axis=-1, keepdims=True))
                alpha = jnp.exp(m - m_new)
                pr = jnp.exp(s - m_new)
                l = alpha * l + jnp.sum(pr, axis=-1, keepdims=True)
                acc = alpha * acc + jnp.dot(pr.astype(BF16), vh, preferred_element_type=F32)
                return m_new, l, acc

            init = (jnp.full((tq, 1), NEG, F32), jnp.zeros((tq, 1), F32), jnp.zeros((tq, LANES), F32))
            m, l, acc = lax.fori_loop(0, qi + 1, step, init)
            out = out + acc / l
            lse = jnp.where(hmask, m + jnp.log(l), lse)
        o_ref[0] = out.astype(BF16)
        l_ref[0, 0] = lse

    return _call(
        body, name=name, grid=(B, npair, S // tq),
        in_specs=[pl.BlockSpec((1, tq, LANES), lambda b, p, i: (b, i, p)),
                  pl.BlockSpec((1, S, LANES), lambda b, p, i: (b, 0, npair + p)),
                  pl.BlockSpec((1, S, LANES), lambda b, p, i: (b, 0, 2 * npair + p)),
                  pl.BlockSpec((1, tq, LANES), lambda b, p, i: (b, i, 0)),
                  pl.BlockSpec((1, 8, S), lambda b, p, i: (b, 0, 0))],
        out_specs=[pl.BlockSpec((1, tq, LANES), lambda b, p, i: (b, i, p)),
                   pl.BlockSpec((1, 1, tq, LANES), lambda b, p, i: (b, p, i, 0))],
        out_shape=[jax.ShapeDtypeStruct((B, S, FOX_W), BF16),
                   jax.ShapeDtypeStruct((B, npair, S, LANES), F32)],
        compiler_params=_params(("parallel", "parallel", "parallel")),
    )(qkv, qkv, qkv, c_col, c_row)


def fox_delta(qkv, dcat, lse, c_col, c_row, *, name, tq=256):
    B, S, _ = qkv.shape
    tq = min(tq, S)
    scale = 1.0 / math.sqrt(FOX_HEAD_DIM)
    npair = FOX_HEADS // 2

    def body(q_ref, k_ref, v_ref, do_ref, l_ref, cc_ref, cr_ref, d_ref):
        p, qi = pl.program_id(1), pl.program_id(2)
        q = q_ref[0]
        do_b = do_ref[0].astype(BF16)
        cc_tile = cc_ref[0]
        lse_t = l_ref[0, 0]
        lane = lax.broadcasted_iota(jnp.int32, (tq, LANES), 1)
        row_g = qi * tq + lax.broadcasted_iota(jnp.int32, (tq, tq), 0)
        col_l = lax.broadcasted_iota(jnp.int32, (tq, tq), 1)
        delta = jnp.zeros((tq, LANES), F32)
        for hh in range(2):
            hmask = (lane < FOX_HEAD_DIM) if hh == 0 else (lane >= FOX_HEAD_DIM)
            head = 2 * p + hh
            qh = jnp.where(hmask, q, jnp.zeros_like(q))
            doh = jnp.where(hmask, do_b, jnp.zeros_like(do_b))
            cc = _lane_pick(cc_tile, head)
            lse_h = _lane_pick(lse_t, hh * FOX_HEAD_DIM)

            def step(kb, acc, qh=qh, doh=doh, cc=cc, lse_h=lse_h, head=head):
                k0 = pl.multiple_of(kb * tq, tq)
                k = k_ref[0, pl.ds(k0, tq), :]
                v = v_ref[0, pl.ds(k0, tq), :]
                s = lax.dot_general(qh, k, NT, preferred_element_type=F32) * scale
                s = s + (cc - cr_ref[0, pl.ds(head, 1), pl.ds(k0, tq)])
                pr = jnp.where(k0 + col_l <= row_g, jnp.exp(s - lse_h), 0.0)
                dp = lax.dot_general(doh, v, NT, preferred_element_type=F32)
                return acc + jnp.sum(pr * dp, axis=-1, keepdims=True)

            dl = lax.fori_loop(0, qi + 1, step, jnp.zeros((tq, 1), F32))
            delta = jnp.where(hmask, dl, delta)
        d_ref[0, 0] = delta

    return _call(
        body, name=name, grid=(B, npair, S // tq),
        in_specs=[pl.BlockSpec((1, tq, LANES), lambda b, p, i: (b, i, p)),
                  pl.BlockSpec((1, S, LANES), lambda b, p, i: (b, 0, npair + p)),
                  pl.BlockSpec((1, S, LANES), lambda b, p, i: (b, 0, 2 * npair + p)),
                  pl.BlockSpec((1, tq, LANES), lambda b, p, i: (b, i, npair + p)),
                  pl.BlockSpec((1, 1, tq, LANES), lambda b, p, i: (b, p, i, 0)),
                  pl.BlockSpec((1, tq, LANES), lambda b, p, i: (b, i, 0)),
                  pl.BlockSpec((1, 8, S), lambda b, p, i: (b, 0, 0))],
        out_specs=pl.BlockSpec((1, 1, tq, LANES), lambda b, p, i: (b, p, i, 0)),
        out_shape=jax.ShapeDtypeStruct((B, npair, S, LANES), F32),
        compiler_params=_params(("parallel", "parallel", "parallel")),
    )(qkv, qkv, qkv, dcat, lse, c_col, c_row)


def fox_bwd(qkv, dcat, dlt, lse, c_col, c_row, *, name, tq=256):
    B, S, _ = qkv.shape
    tq = min(tq, S)
    nq = S // tq
    scale = 1.0 / math.sqrt(FOX_HEAD_DIM)
    npair = FOX_HEADS // 2

    def body(q_ref, k_ref, v_ref, do_ref, dl_ref, l_ref, cc_ref, cr_ref, dq_ref, dk_ref, dv_ref, dc_ref, dq_acc):
        p, kt = pl.program_id(1), pl.program_id(2)

        @pl.when(kt == 0)
        def _():
            dq_acc[...] = jnp.zeros_like(dq_acc)

        k = k_ref[0]
        v = v_ref[0]
        lane = lax.broadcasted_iota(jnp.int32, (tq, LANES), 1)
        row_l = lax.broadcasted_iota(jnp.int32, (tq, tq), 0)
        col_g = kt * tq + lax.broadcasted_iota(jnp.int32, (tq, tq), 1)
        masks = [lane < FOX_HEAD_DIM, lane >= FOX_HEAD_DIM]
        crs = [cr_ref[0, pl.ds(2 * p + hh, 1), :] for hh in range(2)]

        def step(qb, carry):
            dk, dv, dc0, dc1 = carry
            q0 = pl.multiple_of(qb * tq, tq)
            q = q_ref[0, pl.ds(q0, tq), :]
            do_b = do_ref[0, pl.ds(q0, tq), :].astype(BF16)
            dl_t = dl_ref[0, 0, pl.ds(q0, tq), :]
            lse_t = l_ref[0, 0, pl.ds(q0, tq), :]
            cc_t = cc_ref[0, pl.ds(q0, tq), :]
            causal = col_g <= q0 + row_l
            dq_t = jnp.zeros((tq, LANES), F32)
            dcs = []
            for hh in range(2):
                hm = masks[hh]
                qh = jnp.where(hm, q, jnp.zeros_like(q))
                kh = jnp.where(hm, k, jnp.zeros_like(k))
                doh = jnp.where(hm, do_b, jnp.zeros_like(do_b))
                s = lax.dot_general(qh, k, NT, preferred_element_type=F32) * scale
                s = s + (_lane_pick(cc_t, 2 * p + hh) - crs[hh])
                lse_h = _lane_pick(lse_t, hh * FOX_HEAD_DIM)
                pr = jnp.where(causal, jnp.exp(s - lse_h), 0.0)
                dp = lax.dot_general(doh, v, NT, preferred_element_type=F32)
                ds = pr * (dp - _lane_pick(dl_t, hh * FOX_HEAD_DIM))
                ds_b = ds.astype(BF16)
                dv = dv + lax.dot_general(pr.astype(BF16), doh, TN, preferred_element_type=F32)
                dk = dk + lax.dot_general(ds_b, qh, TN, preferred_element_type=F32) * scale
                dq_t = dq_t + jnp.dot(ds_b, kh, preferred_element_type=F32) * scale
                dcs.append(jnp.sum(ds, axis=0, keepdims=True))
            dq_acc[pl.ds(q0, tq), :] += dq_t
            return dk, dv, dc0 - dcs[0], dc1 - dcs[1]

        z = jnp.zeros((tq, LANES), F32)
        zr = jnp.zeros((1, tq), F32)
        dk, dv, dc0, dc1 = lax.fori_loop(kt, nq, step, (z, z, zr, zr))
        dk_ref[0] = dk.astype(BF16)
        dv_ref[0] = dv.astype(BF16)
        r8 = lax.broadcasted_iota(jnp.int32, (8, tq), 0)
        dc_ref[0, 0] = jnp.where(r8 == 0, dc0, jnp.where(r8 == 1, dc1, 0.0))

        @pl.when(kt == nq - 1)
        def _():
            dq_ref[0] = dq_acc[...].astype(BF16)

    full = lambda col: pl.BlockSpec((1, S, LANES), col)
    tile_out = lambda: pl.BlockSpec((1, tq, LANES), lambda b, p, t: (b, t, p))
    return _call(
        body, name=name, grid=(B, npair, nq),
        in_specs=[full(lambda b, p, t: (b, 0, p)),
                  pl.BlockSpec((1, tq, LANES), lambda b, p, t: (b, t, npair + p)),
                  pl.BlockSpec((1, tq, LANES), lambda b, p, t: (b, t, 2 * npair + p)),
                  full(lambda b, p, t: (b, 0, npair + p)),
                  pl.BlockSpec((1, 1, S, LANES), lambda b, p, t: (b, p, 0, 0)),
                  pl.BlockSpec((1, 1, S, LANES), lambda b, p, t: (b, p, 0, 0)),
                  full(lambda b, p, t: (b, 0, 0)),
                  pl.BlockSpec((1, 8, tq), lambda b, p, t: (b, 0, t))],
        out_specs=[full(lambda b, p, t: (b, 0, p)), tile_out(), tile_out(),
                   pl.BlockSpec((1, 1, 8, tq), lambda b, p, t: (b, p, 0, t))],
        out_shape=[jax.ShapeDtypeStruct((B, S, FOX_W), BF16)] * 3
        + [jax.ShapeDtypeStruct((B, npair, 8, S), F32)],
        scratch_shapes=[pltpu.VMEM((S, LANES), F32)],
        compiler_params=_params(("parallel", "parallel", "arbitrary"), 48 << 20),
    )(qkv, qkv, qkv, dcat, dlt, lse, c_col, c_row)


def xattn_fwd(qm, kv, *, name, tq=256):
    B, S, D = qm.shape
    M = kv.shape[1]
    tq = min(tq, S)
    inv = 1.0 / math.sqrt(MEM_HEAD_DIM)

    def body(q_ref, kv_ref, o_ref):
        for h in range(MEM_HEADS):
            c0 = h * MEM_HEAD_DIM
            qh = q_ref[0, :, c0:c0 + MEM_HEAD_DIM]
            kh = kv_ref[0, :, c0:c0 + MEM_HEAD_DIM]
            vh = kv_ref[0, :, D + c0:D + c0 + MEM_HEAD_DIM]
            s = lax.dot_general(qh, kh, NT, preferred_element_type=F32) * inv
            e = jnp.exp(s - jnp.max(s, axis=-1, keepdims=True))
            o = jnp.dot(e.astype(BF16), vh, preferred_element_type=F32) / jnp.sum(e, axis=-1, keepdims=True)
            o_ref[0, :, c0:c0 + MEM_HEAD_DIM] = o.astype(BF16)

    return _call(
        body, name=name, grid=(B, S // tq),
        in_specs=[pl.BlockSpec((1, tq, D), lambda b, i: (b, i, 0)),
                  pl.BlockSpec((1, M, 2 * D), lambda b, i: (b, 0, 0))],
        out_specs=pl.BlockSpec((1, tq, D), lambda b, i: (b, i, 0)),
        out_shape=jax.ShapeDtypeStruct((B, S, D), BF16),
        compiler_params=_params(("parallel", "parallel")),
    )(qm, kv)


def xattn_bwd(qm, kv, do, *, name, tq=256):
    B, S, D = qm.shape
    M = kv.shape[1]
    tq = min(tq, S)
    inv = 1.0 / math.sqrt(MEM_HEAD_DIM)

    def body(q_ref, kv_ref, do_ref, dq_ref, dkv_ref):
        @pl.when(pl.program_id(1) == 0)
        def _():
            dkv_ref[...] = jnp.zeros_like(dkv_ref)

        for h in range(MEM_HEADS):
            c0 = h * MEM_HEAD_DIM
            qh = q_ref[0, :, c0:c0 + MEM_HEAD_DIM]
            kh = kv_ref[0, :, c0:c0 + MEM_HEAD_DIM]
            vh = kv_ref[0, :, D + c0:D + c0 + MEM_HEAD_DIM]
            doh = do_ref[0, :, c0:c0 + MEM_HEAD_DIM]
            s = lax.dot_general(qh, kh, NT, preferred_element_type=F32) * inv
            e = jnp.exp(s - jnp.max(s, axis=-1, keepdims=True))
            pr = e / jnp.sum(e, axis=-1, keepdims=True)
            dp = lax.dot_general(doh, vh, NT, preferred_element_type=F32)
            ds = pr * (dp - jnp.sum(pr * dp, axis=-1, keepdims=True))
            ds_b = ds.astype(BF16)
            dq_ref[0, :, c0:c0 + MEM_HEAD_DIM] = (jnp.dot(ds_b, kh, preferred_element_type=F32) * inv).astype(BF16)
            dkv_ref[0, :, c0:c0 + MEM_HEAD_DIM] += lax.dot_general(ds_b, qh, TN, preferred_element_type=F32) * inv
            dkv_ref[0, :, D + c0:D + c0 + MEM_HEAD_DIM] += lax.dot_general(
                pr.astype(BF16), doh, TN, preferred_element_type=F32)

    row = pl.BlockSpec((1, tq, D), lambda b, i: (b, i, 0))
    kvs = pl.BlockSpec((1, M, 2 * D), lambda b, i: (b, 0, 0))
    return _call(
        body, name=name, grid=(B, S // tq), in_specs=[row, kvs, row], out_specs=[row, kvs],
        out_shape=[jax.ShapeDtypeStruct((B, S, D), BF16), jax.ShapeDtypeStruct((B, M, 2 * D), F32)],
        compiler_params=_params(("parallel", "arbitrary")),
    )(qm, kv, do)


def swiglu_fwd(gu, *, name, tm=256):
    T, F2 = gu.shape
    Fh = F2 // 2
    tm = min(tm, T)

    def body(gu_ref, o_ref):
        g = gu_ref[:, :Fh].astype(F32)
        u = gu_ref[:, Fh:].astype(F32)
        o_ref[...] = (g * _sigmoid(g) * u).astype(BF16)

    return _call(
        body, name=name, grid=(T // tm,),
        in_specs=[pl.BlockSpec((tm, F2), lambda i: (i, 0))],
        out_specs=pl.BlockSpec((tm, Fh), lambda i: (i, 0)),
        out_shape=jax.ShapeDtypeStruct((T, Fh), BF16),
        compiler_params=_params(("parallel",)),
    )(gu)


def swiglu_bwd(gu, dact, *, name, tm=256):
    T, F2 = gu.shape
    Fh = F2 // 2
    tm = min(tm, T)

    def body(gu_ref, d_ref, o_ref):
        g = gu_ref[:, :Fh].astype(F32)
        u = gu_ref[:, Fh:].astype(F32)
        d = d_ref[...].astype(F32)
        sg = _sigmoid(g)
        o_ref[:, :Fh] = (d * u * (sg * (1.0 + g * (1.0 - sg)))).astype(BF16)
        o_ref[:, Fh:] = (d * g * sg).astype(BF16)

    return _call(
        body, name=name, grid=(T // tm,),
        in_specs=[pl.BlockSpec((tm, F2), lambda i: (i, 0)), pl.BlockSpec((tm, Fh), lambda i: (i, 0))],
        out_specs=pl.BlockSpec((tm, F2), lambda i: (i, 0)),
        out_shape=jax.ShapeDtypeStruct((T, F2), BF16),
        compiler_params=_params(("parallel",)),
    )(gu, dact)


def local_step(x, mem, target, sp, wf):
    B, S, D = x.shape
    T = B * S
    M = mem.shape[1]
    row = lambda v: v.reshape(1, -1).astype(F32)
    g_mix, g_x, g_mem, g_ffn, g_final = (row(sp[k]) for k in ("g_mix", "g_x", "g_mem", "g_ffn", "g_final"))
    conv_b, ln_g, ln_b = row(sp["conv_b"]), row(sp["ln_g"]), row(sp["ln_b"])
    conv_w = jnp.pad(sp["conv_w"].astype(F32), ((0, HALO - CONV_K), (0, 0)))
    b_f = jnp.pad(row(sp["b_f"]), ((0, 0), (0, LANES - FOX_HEADS)))
    n_main = 2 * CONV_CH + 3 * FOX_W
    w_main = wf["w_in"][:, :n_main]
    w_f = jnp.pad(wf["w_in"][:, n_main:], ((0, 0), (0, LANES - FOX_HEADS)))

    x2d = x.reshape(T, D)
    h = rmsnorm_fwd(x2d, g_mix, name="rms_mix")
    z = matmul(h, w_main, out_dtype=BF16, tn=n_main, name="mm_in")
    z3 = z.reshape(B, S, n_main)
    ug, qkv = z3[:, :, :2 * CONV_CH], z3[:, :, 2 * CONV_CH:]
    conv_out = conv_branch_fwd(ug, conv_w, conv_b, ln_g, ln_b, name="conv_fwd")
    f_raw, c_col, c_row = fgate_fwd(h.reshape(B, S, D), w_f, b_f, name="fgate_fwd")
    att, lse = fox_fwd(qkv, c_col, c_row, name="fox_fwd")
    cat = jnp.concatenate([conv_out, att], axis=-1).reshape(T, D)
    x1 = matmul(cat, wf["w_out"], out_dtype=F32, res=x2d, tn=D, name="mm_out")
    hx = rmsnorm_fwd(x1, g_x, name="rms_x")
    qm = matmul(hx, wf["w_mq"], out_dtype=BF16, tn=D, name="mm_mq")
    mem2d = mem.reshape(B * M, D)
    mem_n = rmsnorm_fwd(mem2d, g_mem, name="rms_mem")
    kv = matmul(mem_n, wf["w_mkv"], out_dtype=BF16, tn=2 * D, name="mm_mkv").reshape(B, M, 2 * D)
    o = xattn_fwd(qm.reshape(B, S, D), kv, name="xattn_fwd").reshape(T, D)
    x2 = matmul(o, wf["w_mo"], out_dtype=F32, res=x1, tn=D, name="mm_mo")
    hf = rmsnorm_fwd(x2, g_ffn, name="rms_ffn")
    gu = matmul(hf, wf["w_gu"], out_dtype=BF16, tn=2816, name="mm_gu")
    act = swiglu_fwd(gu, name="swiglu_fwd")
    x3 = matmul(act, wf["w_down"], out_dtype=F32, res=x2, tn=D, name="mm_down")
    dx3, dg_final, loss = final_loss_bwd(x3, g_final, target.reshape(T, D), name="loss_bwd")
    gw = {}
    gw["w_down"] = matmul(act.T, dx3, out_dtype=BF16, tm=1408, tn=256, name="dw_down")
    dact = matmul(dx3, wf["w_down"], tb=True, out_dtype=BF16, tn=2816, name="dx_down")
    dgu = swiglu_bwd(gu, dact, name="swiglu_bwd")
    gw["w_gu"] = matmul(hf.T, dgu, out_dtype=BF16, tn=1408, name="dw_gu")
    dhf = matmul(dgu, wf["w_gu"], tb=True, out_dtype=BF16, tm=256, tn=D, name="dx_gu")
    dx2, dg_ffn = rmsnorm_bwd(x2, g_ffn, dhf, dx3, name="rms_ffn_bwd")
    gw["w_mo"] = matmul(o.T, dx2, out_dtype=BF16, name="dw_mo")
    do = matmul(dx2, wf["w_mo"], tb=True, out_dtype=BF16, tn=D, name="dx_mo")
    dqm, dkv = xattn_bwd(qm.reshape(B, S, D), kv, do.reshape(B, S, D), name="xattn_bwd")
    dqm = dqm.reshape(T, D)
    dkv = dkv.reshape(B * M, 2 * D)
    gw["w_mq"] = matmul(hx.T, dqm, out_dtype=BF16, tn=D, name="dw_mq")
    dhx = matmul(dqm, wf["w_mq"], tb=True, out_dtype=BF16, tn=D, name="dx_mq")
    gw["w_mkv"] = matmul(mem_n.T, dkv, out_dtype=BF16, tn=D, name="dw_mkv")
    dmem_n = matmul(dkv, wf["w_mkv"], tb=True, out_dtype=BF16, tn=D, name="dx_mkv")
    _, dg_mem = rmsnorm_bwd(mem2d, g_mem, dmem_n, None, name="rms_mem_bwd")
    dx1, dg_x = rmsnorm_bwd(x1, g_x, dhx, dx2, name="rms_x_bwd")
    gw["w_out"] = matmul(cat.T, dx1, out_dtype=BF16, name="dw_out")
    dcat = matmul(dx1, wf["w_out"], tb=True, out_dtype=BF16, tn=D, name="dx_out").reshape(B, S, D)
    dy, dconv_w, dvec = conv_branch_bwd_a(ug, dcat, conv_w, conv_b, ln_g, ln_b, name="conv_bwd_a")
    dug = conv_branch_bwd_b(ug, dy, conv_w, name="conv_bwd_b")
    dlt = fox_delta(qkv, dcat, lse, c_col, c_row, name="fox_delta")
    dq, dk, dv, dc4 = fox_bwd(qkv, dcat, dlt, lse, c_col, c_row, name="fox_bwd")
    dc_row = jnp.pad(dc4[:, :, :2, :].reshape(B, FOX_HEADS, S), ((0, 0), (0, LANES - FOX_HEADS), (0, 0)))
    df, db_f = fgate_bwd(dc_row, f_raw, b_f, name="fgate_bwd")
    dz = jnp.concatenate([dug, dq, dk, dv], axis=-1).reshape(T, n_main)
    df2 = df.reshape(T, LANES)
    h_t = h.T
    dw_main = matmul(h_t, dz, out_dtype=BF16, tn=1280, name="dw_in")
    dw_f = matmul(h_t, df2, out_dtype=BF16, name="dw_f")
    gw["w_in"] = jnp.concatenate([dw_main, dw_f[:, :FOX_HEADS]], axis=-1)
    dh_f = matmul(df2, w_f, tb=True, out_dtype=F32, tn=D, name="dx_f")
    dh = matmul(dz, w_main, tb=True, out_dtype=F32, res=dh_f, tn=D, name="dx_in")
    dx, dg_mix = rmsnorm_bwd(x2d, g_mix, dh, dx1, name="rms_mix_bwd")
    gs = dict(g_mix=dg_mix, b_f=db_f[:, :FOX_HEADS], conv_w=dconv_w[:CONV_K], conv_b=dvec[0:1],
              ln_g=dvec[1:2], ln_b=dvec[2:3], g_x=dg_x, g_mem=dg_mem, g_ffn=dg_ffn, g_final=dg_final)
    return loss, dx.reshape(B, S, D), gs, gw


def _me():
    return lax.axis_index("x"), lax.axis_index("y"), lax.axis_index("c")


def _any_specs(n):
    return [pl.BlockSpec(memory_space=pl.ANY)] * n


def all_gather(xs, *, name):
    n = len(xs)

    def body(*refs):
        x_refs, out_refs = refs[:n], refs[n:2 * n]
        send_sems, recv_sems, local_sems = refs[2 * n:]
        x, y, c = _me()
        me, sibling = (x, y, c), (x, y, 1 - c)
        chips = [(1 - x, y), (x, 1 - y), (1 - x, 1 - y)]

        def slot(a, px, py, pc):
            return out_refs[a].at[4 * px + 2 * py + pc]

        def copy(a, k, block, to, own=False):
            return pltpu.make_async_remote_copy(
                src_ref=x_refs[a] if own else slot(a, *block), dst_ref=slot(a, *block),
                send_sem=send_sems.at[k, a], recv_sem=recv_sems.at[k, a], device_id=to, device_id_type=MESH)

        mine = [pltpu.make_async_copy(x_refs[a], slot(a, *me), local_sems.at[a]) for a in range(n)]
        first = [copy(a, 0, me, sibling, own=True) for a in range(n)]
        first += [copy(a, 1 + j, me, (*chip, c), own=True) for j, chip in enumerate(chips) for a in range(n)]
        for cp in mine + first:
            cp.start()
        passed = []
        for j, chip in enumerate(chips):
            for a in range(n):
                copy(a, 1 + j, (*chip, c), me).wait_recv()
                passed.append(copy(a, 4 + j, (*chip, c), sibling))
                passed[-1].start()
        for a in range(n):
            copy(a, 0, sibling, me).wait_recv()
            for j, chip in enumerate(chips):
                copy(a, 4 + j, (*chip, 1 - c), me).wait_recv()
        for cp in first + passed:
            cp.wait_send()
        for cp in mine:
            cp.wait()

    return _call(
        body, name=name, in_specs=_any_specs(n), out_specs=_any_specs(n),
        out_shape=[jax.ShapeDtypeStruct((N_DEV,) + v.shape, v.dtype) for v in xs],
        scratch_shapes=[pltpu.SemaphoreType.DMA((7, n)), pltpu.SemaphoreType.DMA((7, n)),
                        pltpu.SemaphoreType.DMA((n,))],
    )(*xs)


def sibling_exchange(gs, *, name):
    n = len(gs)

    def body(*refs):
        g_refs, out_refs = refs[:n], refs[n:2 * n]
        send_sems, recv_sems = refs[2 * n:]
        x, y, c = _me()
        cps = [pltpu.make_async_remote_copy(
            src_ref=g_refs[a].at[:, 1 - c], dst_ref=out_refs[a], send_sem=send_sems.at[a],
            recv_sem=recv_sems.at[a], device_id=(x, y, 1 - c), device_id_type=MESH) for a in range(n)]
        for cp in cps:
            cp.start()
        for cp in cps:
            cp.wait()

    return _call(
        body, name=name, in_specs=_any_specs(n), out_specs=_any_specs(n),
        out_shape=[jax.ShapeDtypeStruct((4,) + g.shape[2:], g.dtype) for g in gs],
        scratch_shapes=[pltpu.SemaphoreType.DMA((n,)), pltpu.SemaphoreType.DMA((n,))],
    )(*gs)


def chip_exchange(ps, *, name):
    n = len(ps)

    def body(*refs):
        p_refs, out_refs = refs[:n], refs[n:2 * n]
        send_sems, recv_sems = refs[2 * n:]
        x, y, c = _me()
        my_chip = 2 * x + y
        peers = [(x ^ (k >> 1), y ^ (k & 1)) for k in range(1, 4)]

        def copy(a, k, src_chip, dst_chip, to):
            return pltpu.make_async_remote_copy(
                src_ref=p_refs[a].at[src_chip], dst_ref=out_refs[a].at[dst_chip],
                send_sem=send_sems.at[k, a], recv_sem=recv_sems.at[k, a], device_id=to, device_id_type=MESH)

        sends = [copy(a, k, 2 * px + py, my_chip, (px, py, c)) for k, (px, py) in enumerate(peers) for a in range(n)]
        for cp in sends:
            cp.start()
        for k, (px, py) in enumerate(peers):
            for a in range(n):
                copy(a, k, my_chip, 2 * px + py, (px, py, c)).wait_recv()
        for cp in sends:
            cp.wait_send()

    return _call(
        body, name=name, in_specs=_any_specs(n), out_specs=_any_specs(n),
        out_shape=[jax.ShapeDtypeStruct(p.shape, p.dtype) for p in ps],
        scratch_shapes=[pltpu.SemaphoreType.DMA((3, n)), pltpu.SemaphoreType.DMA((3, n))],
    )(*ps)


def _pick_rows(r, target=256):
    best = None
    for d in range(16, min(r, target) + 1, 16):
        if r % d == 0:
            best = d
    return r if best is None else best


def pair_sum(g, got, *, name):
    _, _, R, C = g.shape
    tr = _pick_rows(R)

    def body(g_ref, got_ref, o_ref):
        mine = jnp.where(lax.axis_index("c") == 0, g_ref[:, 0], g_ref[:, 1])
        o_ref[...] = (mine.astype(F32) + got_ref[...].astype(F32)).astype(o_ref.dtype)

    return _call(
        body, name=name, grid=(R // tr,),
        in_specs=[pl.BlockSpec((4, 2, tr, C), lambda i: (0, 0, i, 0)), pl.BlockSpec((4, tr, C), lambda i: (0, i, 0))],
        out_specs=pl.BlockSpec((4, tr, C), lambda i: (0, i, 0)),
        out_shape=jax.ShapeDtypeStruct((4, R, C), g.dtype),
        compiler_params=_params(("parallel",)),
    )(g, got)


def chip_sum_adamw(p, got, w, m, v, *, name):
    _, R, C = p.shape
    tr = _pick_rows(R)

    def body(p_ref, got_ref, w_ref, m_ref, v_ref, g_ref, d_ref, mo_ref, vo_ref):
        my_chip = 2 * lax.axis_index("x") + lax.axis_index("y")
        g = jnp.zeros((tr, C), F32)
        for j in range(4):
            g = g + jnp.where(my_chip == j, p_ref[j], got_ref[j]).astype(F32)
        g_ref[...] = g
        d_ref[...], mo_ref[...], vo_ref[...] = _adamw_math(w_ref[...], g, m_ref[...], v_ref[...])

    part = pl.BlockSpec((4, tr, C), lambda i: (0, i, 0))
    spec = pl.BlockSpec((tr, C), lambda i: (i, 0))
    return _call(
        body, name=name, grid=(R // tr,), in_specs=[part, part, spec, spec, spec], out_specs=[spec] * 4,
        out_shape=[jax.ShapeDtypeStruct((R, C), F32)] * 4,
        compiler_params=_params(("parallel",)),
    )(p, got, w, m, v)


def rows_sum(g8, *, name):
    _, R, C = g8.shape

    def body(g_ref, o_ref):
        acc = g_ref[0]
        for j in range(1, N_DEV):
            acc = acc + g_ref[j]
        o_ref[...] = acc

    return _call(body, name=name, out_shape=jax.ShapeDtypeStruct((R, C), F32))(g8)


def _adamw_math(w, g, m, v):
    m = ADAM_B1 * m + (1.0 - ADAM_B1) * g
    v = ADAM_B2 * v + (1.0 - ADAM_B2) * (g * g)
    m_hat = m / (1.0 - ADAM_B1 ** ADAM_STEP)
    v_hat = v / (1.0 - ADAM_B2 ** ADAM_STEP)
    delta = -ADAM_LR * (m_hat / (jnp.sqrt(v_hat) + ADAM_EPS) + ADAM_WD * w)
    return delta, m, v


def adamw_small(wgmv, *, name):
    n = len(wgmv)

    def body(*refs):
        ins, outs = refs[:4 * n], refs[4 * n:]
        for a in range(n):
            w_ref, g_ref, m_ref, v_ref = ins[4 * a:4 * a + 4]
            d, mn, vn = _adamw_math(w_ref[...], g_ref[...], m_ref[...], v_ref[...])
            outs[3 * a][...] = d
            outs[3 * a + 1][...] = mn
            outs[3 * a + 2][...] = vn

    flat = [t for tup in wgmv for t in tup]
    res = _call(
        body, name=name,
        out_shape=[jax.ShapeDtypeStruct(tup[0].shape, F32) for tup in wgmv for _ in range(3)],
    )(*flat)
    return [tuple(res[3 * a:3 * a + 3]) for a in range(n)]


BIG = ("w_in", "w_out", "w_mq", "w_mkv", "w_mo", "w_gu", "w_down")
COL_SHARDED = ("w_in", "w_mkv", "w_gu")
SMALL = ("g_mix", "b_f", "conv_w", "conv_b", "ln_g", "ln_b", "g_x", "g_mem", "g_ffn", "g_final")


def _full_from_gathered(n, blk):
    _, rr, cc = blk.shape
    if n in COL_SHARDED:
        return blk.transpose(1, 0, 2).reshape(rr, N_DEV * cc)
    return blk.reshape(N_DEV * rr, cc)


def _shards_from_full(n, g):
    rr, cc = g.shape
    if n in COL_SHARDED:
        return g.reshape(rr, 4, 2, cc // N_DEV).transpose(1, 2, 0, 3)
    return g.reshape(4, 2, rr // N_DEV, cc)


def _small_layout():
    sizes = dict(g_mix=1024, b_f=8, conv_w=CONV_K * CONV_CH, conv_b=512, ln_g=512, ln_b=512, g_x=1024,
                 g_mem=1024, g_ffn=1024, g_final=1024, loss=1)
    lay, r0 = {}, 0
    for n, sz in sizes.items():
        r = -(-sz // LANES)
        lay[n] = (r0, r, sz)
        r0 += r
    return lay, -(-r0 // 8) * 8


def kernel(x, mem, g_mix, w_in, b_f, conv_w, conv_b, ln_g, ln_b, w_out, g_x, g_mem, w_mq, w_mkv, w_mo, g_ffn, w_gu, w_down, g_final, loss_target, m_g_mix, m_w_in, m_b_f, m_conv_w, m_conv_b, m_ln_g, m_ln_b, m_w_out, m_g_x, m_g_mem, m_w_mq, m_w_mkv, m_w_mo, m_g_ffn, m_w_gu, m_w_down, m_g_final, v_g_mix, v_w_in, v_b_f, v_conv_w, v_conv_b, v_ln_g, v_ln_b, v_w_out, v_g_x, v_g_mem, v_w_mq, v_w_mkv, v_w_mo, v_g_ffn, v_w_gu, v_w_down, v_g_final):
    names = ["g_mix", "w_in", "b_f", "conv_w", "conv_b", "ln_g", "ln_b", "w_out", "g_x", "g_mem", "w_mq",
             "w_mkv", "w_mo", "g_ffn", "w_gu", "w_down", "g_final"]
    W = dict(zip(names, [g_mix, w_in, b_f, conv_w, conv_b, ln_g, ln_b, w_out, g_x, g_mem, w_mq, w_mkv, w_mo,
                         g_ffn, w_gu, w_down, g_final]))
    Mo = dict(zip(names, [m_g_mix, m_w_in, m_b_f, m_conv_w, m_conv_b, m_ln_g, m_ln_b, m_w_out, m_g_x, m_g_mem,
                          m_w_mq, m_w_mkv, m_w_mo, m_g_ffn, m_w_gu, m_w_down, m_g_final]))
    Vo = dict(zip(names, [v_g_mix, v_w_in, v_b_f, v_conv_w, v_conv_b, v_ln_g, v_ln_b, v_w_out, v_g_x, v_g_mem,
                          v_w_mq, v_w_mkv, v_w_mo, v_g_ffn, v_w_gu, v_w_down, v_g_final]))
    dev = 4 * lax.axis_index("x") + 2 * lax.axis_index("y") + lax.axis_index("c")

    two = lambda a: a.reshape(-1, a.shape[-1])
    cw_shard = jnp.pad(two(conv_w), ((0, HALO - CONV_K), (0, 0)))
    gathered = all_gather([two(W[n]).astype(BF16) for n in BIG] + [cw_shard], name="ag_weights")
    wf = {n: _full_from_gathered(n, blk) for n, blk in zip(BIG, gathered)}
    cw_full = gathered[-1].transpose(1, 0, 2).reshape(HALO, -1)[:CONV_K]

    sp = dict(g_mix=g_mix, b_f=b_f, conv_w=cw_full, conv_b=conv_b, ln_g=ln_g, ln_b=ln_b, g_x=g_x, g_mem=g_mem,
              g_ffn=g_ffn, g_final=g_final)
    loss_blk, grad_x, gs, gw = local_step(x, mem, loss_target, sp, wf)

    g42 = [_shards_from_full(n, gw[n]) for n in BIG]
    got1 = sibling_exchange(g42, name="rs_sibling")
    part = [pair_sum(g, o, name="rs_pair_sum_" + n) for n, g, o in zip(BIG, g42, got1)]
    got2 = chip_exchange(part, name="rs_chips")

    lay, rs = _small_layout()
    small = {**{n: gs[n] for n in SMALL}, "loss": loss_blk[:, :1]}
    parts = []
    for n, (r0, r, sz) in lay.items():
        flat = small[n].reshape(-1).astype(F32)
        parts.append(jnp.pad(flat, (0, r * LANES - sz)).reshape(r, LANES))
    spack = jnp.concatenate(parts, axis=0)
    spack = jnp.pad(spack, ((0, rs - spack.shape[0]), (0, 0)))
    ssum = rows_sum(all_gather([spack], name="ag_small")[0], name="small_sum")
    gsmall = {n: ssum[r0:r0 + r].reshape(-1)[:sz] for n, (r0, r, sz) in lay.items()}
    loss = gsmall["loss"].reshape(())

    grads, delta, new_m, new_v = {}, {}, {}, {}
    for n, p, o in zip(BIG, part, got2):
        shp = W[n].shape
        g, d, mn, vn = chip_sum_adamw(p, o, two(W[n]), two(Mo[n]), two(Vo[n]), name="adamw_" + n)
        grads[n], delta[n], new_m[n], new_v[n] = g.reshape(shp), d.reshape(shp), mn.reshape(shp), vn.reshape(shp)
    for n in SMALL:
        if n == "conv_w":
            full = gsmall[n].reshape(CONV_K, CONV_CH)
            ncol = conv_w.shape[-1]
            grads[n] = lax.dynamic_slice(full, (0, dev * ncol), (CONV_K, ncol)).reshape(conv_w.shape)
        else:
            grads[n] = gsmall[n].reshape(W[n].shape)
    upd = adamw_small([(two(W[n]), two(grads[n]), two(Mo[n]), two(Vo[n])) for n in SMALL], name="adamw_small")
    for n, (d, mn, vn) in zip(SMALL, upd):
        shp = W[n].shape
        delta[n], new_m[n], new_v[n] = d.reshape(shp), mn.reshape(shp), vn.reshape(shp)
    return (loss, grad_x, *[grads[n] for n in names], *[delta[n] for n in names],
            *[new_m[n] for n in names], *[new_v[n] for n in names])
```

```python
import functools
import math

import jax
import jax.numpy as jnp
from jax import lax
from jax.experimental import pallas as pl
from jax.experimental.pallas import tpu as pltpu

F32 = jnp.float32
BF16 = jnp.bfloat16
EPS = 1e-6
N_DEV = 8
CONV_CH = 512
CONV_K = 31
FOX_HEADS = 8
FOX_HEAD_DIM = 64
FOX_W = 512
MEM_HEADS = 4
MEM_HEAD_DIM = 256
HALO = 32
LANES = 128
ADAM_LR, ADAM_B1, ADAM_B2, ADAM_EPS, ADAM_WD, ADAM_STEP = 0.001, 0.9, 0.999, 1e-08, 0.01, 10
NEG = -1e30
VMEM_CAP = 60 * 1024 * 1024
MESH = pl.DeviceIdType.MESH


def _call(body, **kw):
    return pl.pallas_call(body, **kw)


def _params(sem=None, vmem=None):
    kw = {}
    if sem is not None:
        kw["dimension_semantics"] = sem
    if vmem is not None:
        kw["vmem_limit_bytes"] = int(min(VMEM_CAP, vmem))
    return pltpu.CompilerParams(**kw)


def _nbytes(shape, dtype):
    return math.prod(shape) * jnp.dtype(dtype).itemsize


def _pick(n, target):
    best = None
    for d in range(LANES, min(n, target) + 1, LANES):
        if n % d == 0:
            best = d
    return n if best is None else best


def matmul(a, b, *, tb=False, out_dtype, res=None, tm=512, tn=512, tk=None, name):
    M, K = a.shape
    N = b.shape[0] if tb else b.shape[1]
    assert (b.shape[1] if tb else b.shape[0]) == K
    tm, tn = _pick(M, tm), _pick(N, tn)
    tk = K if tk is None else _pick(K, tk)
    assert M % tm == 0 and N % tn == 0 and K % tk == 0, (name, M, N, K, tm, tn, tk)
    nk = K // tk
    dn = (((1,), (1 if tb else 0,)), ((), ()))

    def body(*refs):
        if res is not None:
            a_ref, b_ref, r_ref, o_ref = refs[:4]
        else:
            a_ref, b_ref, o_ref = refs[:3]
        p = lax.dot_general(a_ref[...].astype(BF16), b_ref[...].astype(BF16), dn,
                            preferred_element_type=F32)

        def finish(acc):
            if res is not None:
                acc = acc + r_ref[...].astype(F32)
            o_ref[...] = acc.astype(out_dtype)

        if nk == 1:
            finish(p)
        else:
            acc_ref = refs[-1]
            k = pl.program_id(2)

            @pl.when(k == 0)
            def _():
                acc_ref[...] = p

            @pl.when(k > 0)
            def _():
                acc_ref[...] += p

            @pl.when(k == nk - 1)
            def _():
                finish(acc_ref[...])

    a_spec = pl.BlockSpec((tm, tk), lambda i, j, k: (i, k))
    b_spec = pl.BlockSpec((tn, tk), lambda i, j, k: (j, k)) if tb else pl.BlockSpec((tk, tn), lambda i, j, k: (k, j))
    o_spec = pl.BlockSpec((tm, tn), lambda i, j, k: (i, j))
    in_specs, args = [a_spec, b_spec], [a, b]
    est = 2 * (_nbytes((tm, tk), a.dtype) + _nbytes((tk, tn), b.dtype) + _nbytes((tm, tn), out_dtype))
    est += (a.dtype != BF16) * _nbytes((tm, tk), BF16) + (b.dtype != BF16) * _nbytes((tk, tn), BF16)
    est += 2 * _nbytes((tm, tn), F32)
    if res is not None:
        in_specs.append(o_spec)
        args.append(res)
        est += 2 * _nbytes((tm, tn), res.dtype)
    return _call(
        body, name=name, grid=(M // tm, N // tn, nk),
        in_specs=in_specs, out_specs=o_spec,
        out_shape=jax.ShapeDtypeStruct((M, N), out_dtype),
        scratch_shapes=[] if nk == 1 else [pltpu.VMEM((tm, tn), F32)],
        compiler_params=_params(("parallel", "parallel", "arbitrary"), est + (8 << 20)),
    )(*args)


def _rms_scale(x):
    return lax.rsqrt(jnp.mean(x * x, axis=-1, keepdims=True) + EPS)


def rmsnorm_fwd(x, g, *, name, tm=512):
    T, D = x.shape
    tm = min(tm, T)

    def body(x_ref, g_ref, o_ref):
        xv = x_ref[...]
        o_ref[...] = (xv * _rms_scale(xv) * g_ref[...]).astype(BF16)

    return _call(
        body, name=name, grid=(T // tm,),
        in_specs=[pl.BlockSpec((tm, D), lambda i: (i, 0)), pl.BlockSpec((1, D), lambda i: (0, 0))],
        out_specs=pl.BlockSpec((tm, D), lambda i: (i, 0)),
        out_shape=jax.ShapeDtypeStruct((T, D), BF16),
        compiler_params=_params(("parallel",)),
    )(x, g)


def _rms_bwd_math(xv, gv, dh):
    r = _rms_scale(xv)
    xh = xv * r
    dg = jnp.sum(dh * xh, axis=0, keepdims=True)
    dxh = dh * gv
    dx = r * (dxh - xh * jnp.mean(dxh * xh, axis=-1, keepdims=True))
    return dx, dg


def rmsnorm_bwd(x, g, dh, dres, *, name, tm=256):
    T, D = x.shape
    tm = min(tm, T)

    def body(*refs):
        if dres is not None:
            x_ref, g_ref, dh_ref, dr_ref, dx_ref, dg_ref = refs
        else:
            x_ref, g_ref, dh_ref, dx_ref, dg_ref = refs
        dx, dg = _rms_bwd_math(x_ref[...], g_ref[...], dh_ref[...].astype(F32))
        if dres is not None:
            dx = dx + dr_ref[...]
        dx_ref[...] = dx

        @pl.when(pl.program_id(0) == 0)
        def _():
            dg_ref[...] = jnp.zeros_like(dg_ref)

        dg_ref[...] += dg

    row = pl.BlockSpec((tm, D), lambda i: (i, 0))
    vec = pl.BlockSpec((1, D), lambda i: (0, 0))
    ins, args = [row, vec, row], [x, g, dh]
    if dres is not None:
        ins.append(row)
        args.append(dres)
    return _call(
        body, name=name, grid=(T // tm,), in_specs=ins, out_specs=[row, vec],
        out_shape=[jax.ShapeDtypeStruct((T, D), F32), jax.ShapeDtypeStruct((1, D), F32)],
        compiler_params=_params(("arbitrary",)),
    )(*args)


def final_loss_bwd(x, g, target, *, name, tm=256):
    T, D = x.shape
    tm = min(tm, T)

    def body(x_ref, g_ref, t_ref, dx_ref, dg_ref, l_ref):
        xv, gv = x_ref[...], g_ref[...]
        e = xv * _rms_scale(xv) * gv - t_ref[...]
        part = 0.5 * jnp.sum(jnp.mean(e * e, axis=-1, keepdims=True), axis=0, keepdims=True)
        dx, dg = _rms_bwd_math(xv, gv, e * (1.0 / D))
        dx_ref[...] = dx

        @pl.when(pl.program_id(0) == 0)
        def _():
            dg_ref[...] = jnp.zeros_like(dg_ref)
            l_ref[...] = jnp.zeros_like(l_ref)

        dg_ref[...] += dg
        l_ref[...] += jnp.broadcast_to(part, l_ref.shape)

    row = pl.BlockSpec((tm, D), lambda i: (i, 0))
    vec = pl.BlockSpec((1, D), lambda i: (0, 0))
    return _call(
        body, name=name, grid=(T // tm,), in_specs=[row, vec, row],
        out_specs=[row, vec, pl.BlockSpec((1, LANES), lambda i: (0, 0))],
        out_shape=[jax.ShapeDtypeStruct((T, D), F32), jax.ShapeDtypeStruct((1, D), F32),
                   jax.ShapeDtypeStruct((1, LANES), F32)],
        compiler_params=_params(("arbitrary",)),
    )(x, g, target)


def _sigmoid(v):
    return 1.0 / (1.0 + jnp.exp(-v))


def _glu(blk):
    u = blk[:, :CONV_CH].astype(F32)
    gt = blk[:, CONV_CH:].astype(F32)
    return u * _sigmoid(gt)


def _fill_causal_ext(ext, cur_ref, halo_ref, s, ts):
    ext[pl.ds(HALO, ts), :] = _glu(cur_ref[0])
    hal = _glu(halo_ref[0])
    ext[pl.ds(0, HALO), :] = jnp.where(s > 0, hal, 0.0)


def _causal_conv(ext, w_ref, ts):
    acc = jnp.zeros((ts, CONV_CH), F32)
    for j in range(CONV_K):
        acc = acc + ext[pl.ds(HALO - (CONV_K - 1) + j, ts), :] * w_ref[pl.ds(j, 1), :]
    return acc


def _ln_stats(y):
    mu = jnp.mean(y, axis=-1, keepdims=True)
    yc = y - mu
    rstd = lax.rsqrt(jnp.mean(yc * yc, axis=-1, keepdims=True) + EPS)
    return yc * rstd, rstd


def _conv_specs(ts, S):
    nh = ts // HALO
    cur = pl.BlockSpec((1, ts, 2 * CONV_CH), lambda b, s: (b, s, 0))
    halo = pl.BlockSpec((1, HALO, 2 * CONV_CH), lambda b, s: (b, jnp.maximum(s * nh - 1, 0), 0))
    w = pl.BlockSpec((HALO, CONV_CH), lambda b, s: (0, 0))
    vec = pl.BlockSpec((1, CONV_CH), lambda b, s: (0, 0))
    return cur, halo, w, vec


def conv_branch_fwd(ug, conv_w, conv_b, ln_g, ln_b, *, name, ts=256):
    B, S, _ = ug.shape
    ts = min(ts, S)
    cur, halo, w, vec = _conv_specs(ts, S)

    def body(cur_ref, halo_ref, w_ref, cb_ref, lg_ref, lb_ref, o_ref, ext):
        _fill_causal_ext(ext, cur_ref, halo_ref, pl.program_id(1), ts)
        y = _causal_conv(ext, w_ref, ts) + cb_ref[...]
        yh, _ = _ln_stats(y)
        ln = yh * lg_ref[...] + lb_ref[...]
        o_ref[0] = (ln * _sigmoid(ln)).astype(BF16)

    return _call(
        body, name=name, grid=(B, S // ts), in_specs=[cur, halo, w, vec, vec, vec],
        out_specs=pl.BlockSpec((1, ts, CONV_CH), lambda b, s: (b, s, 0)),
        out_shape=jax.ShapeDtypeStruct((B, S, CONV_CH), BF16),
        scratch_shapes=[pltpu.VMEM((ts + HALO, CONV_CH), F32)],
        compiler_params=_params(("parallel", "parallel")),
    )(ug, ug, conv_w, conv_b, ln_g, ln_b)


def conv_branch_bwd_a(ug, dcat, conv_w, conv_b, ln_g, ln_b, *, name, ts=256):
    B, S, _ = ug.shape
    ts = min(ts, S)
    cur, halo, w, vec = _conv_specs(ts, S)

    def body(cur_ref, halo_ref, d_ref, w_ref, cb_ref, lg_ref, lb_ref, dy_ref, dw_ref, dv_ref, ext):
        _fill_causal_ext(ext, cur_ref, halo_ref, pl.program_id(1), ts)
        y = _causal_conv(ext, w_ref, ts) + cb_ref[...]
        yh, rstd = _ln_stats(y)
        lg = lg_ref[...]
        ln = yh * lg + lb_ref[...]
        sg = _sigmoid(ln)
        dln = d_ref[0].astype(F32) * (sg * (1.0 + ln * (1.0 - sg)))
        dyh = dln * lg
        dy = rstd * (dyh - jnp.mean(dyh, axis=-1, keepdims=True)
                     - yh * jnp.mean(dyh * yh, axis=-1, keepdims=True))
        dy_ref[0] = dy

        @pl.when((pl.program_id(0) == 0) & (pl.program_id(1) == 0))
        def _():
            dw_ref[...] = jnp.zeros_like(dw_ref)
            dv_ref[...] = jnp.zeros_like(dv_ref)

        dv_ref[pl.ds(0, 1), :] += jnp.sum(dy, axis=0, keepdims=True)
        dv_ref[pl.ds(1, 1), :] += jnp.sum(dln * yh, axis=0, keepdims=True)
        dv_ref[pl.ds(2, 1), :] += jnp.sum(dln, axis=0, keepdims=True)
        for j in range(CONV_K):
            tap = ext[pl.ds(HALO - (CONV_K - 1) + j, ts), :]
            dw_ref[pl.ds(j, 1), :] += jnp.sum(dy * tap, axis=0, keepdims=True)

    return _call(
        body, name=name, grid=(B, S // ts),
        in_specs=[cur, halo, pl.BlockSpec((1, ts, CONV_CH), lambda b, s: (b, s, 0)), w, vec, vec, vec],
        out_specs=[pl.BlockSpec((1, ts, CONV_CH), lambda b, s: (b, s, 0)),
                   pl.BlockSpec((HALO, CONV_CH), lambda b, s: (0, 0)),
                   pl.BlockSpec((8, CONV_CH), lambda b, s: (0, 0))],
        out_shape=[jax.ShapeDtypeStruct((B, S, CONV_CH), F32),
                   jax.ShapeDtypeStruct((HALO, CONV_CH), F32),
                   jax.ShapeDtypeStruct((8, CONV_CH), F32)],
        scratch_shapes=[pltpu.VMEM((ts + HALO, CONV_CH), F32)],
        compiler_params=_params(("arbitrary", "arbitrary")),
    )(ug, ug, dcat, conv_w, conv_b, ln_g, ln_b)


def conv_branch_bwd_b(ug, dy, conv_w, *, name, ts=256):
    B, S, _ = ug.shape
    ts = min(ts, S)
    nh, n_halo = ts // HALO, S // HALO

    def body(cur_ref, dy_ref, nxt_ref, w_ref, o_ref, ext):
        last = pl.program_id(1) == pl.num_programs(1) - 1
        ext[pl.ds(0, ts), :] = dy_ref[0]
        ext[pl.ds(ts, HALO), :] = jnp.where(last, 0.0, nxt_ref[0])
        da = jnp.zeros((ts, CONV_CH), F32)
        for j in range(CONV_K):
            da = da + ext[pl.ds(CONV_K - 1 - j, ts), :] * w_ref[pl.ds(j, 1), :]
        blk = cur_ref[0]
        u = blk[:, :CONV_CH].astype(F32)
        sg = _sigmoid(blk[:, CONV_CH:].astype(F32))
        o_ref[0, :, :CONV_CH] = (da * sg).astype(BF16)
        o_ref[0, :, CONV_CH:] = (da * u * sg * (1.0 - sg)).astype(BF16)

    return _call(
        body, name=name, grid=(B, S // ts),
        in_specs=[pl.BlockSpec((1, ts, 2 * CONV_CH), lambda b, s: (b, s, 0)),
                  pl.BlockSpec((1, ts, CONV_CH), lambda b, s: (b, s, 0)),
                  pl.BlockSpec((1, HALO, CONV_CH), lambda b, s: (b, jnp.minimum((s + 1) * nh, n_halo - 1), 0)),
                  pl.BlockSpec((HALO, CONV_CH), lambda b, s: (0, 0))],
        out_specs=pl.BlockSpec((1, ts, 2 * CONV_CH), lambda b, s: (b, s, 0)),
        out_shape=jax.ShapeDtypeStruct((B, S, 2 * CONV_CH), BF16),
        scratch_shapes=[pltpu.VMEM((ts + HALO, CONV_CH), F32)],
        compiler_params=_params(("parallel", "parallel")),
    )(ug, dy, dy, conv_w)


def _tri(n, lower):
    r = lax.broadcasted_iota(jnp.int32, (n, n), 0)
    c = lax.broadcasted_iota(jnp.int32, (n, n), 1)
    return ((r >= c) if lower else (r <= c)).astype(F32)


def _eye(n):
    r = lax.broadcasted_iota(jnp.int32, (n, n), 0)
    c = lax.broadcasted_iota(jnp.int32, (n, n), 1)
    return (r == c).astype(F32)


def _dot_hi(a, b, dn):
    return lax.dot_general(a, b, dn, precision=lax.Precision.HIGHEST, preferred_element_type=F32)


NN = (((1,), (0,)), ((), ()))
NT = (((1,), (1,)), ((), ()))
TN = (((0,), (0,)), ((), ()))


def _log_sigmoid(v):
    e = jnp.exp(-jnp.abs(v))
    log1p_e = jnp.where(e < 1e-3, e * (1.0 - 0.5 * e), jnp.log(1.0 + e))
    return jnp.minimum(v, 0.0) - log1p_e


def fgate_fwd(h, w_f, b_f, *, name, ts=256):
    B, S, D = h.shape
    ts = min(ts, S)

    def body(h_ref, w_ref, b_ref, f_ref, cc_ref, cr_ref, carry):
        @pl.when(pl.program_id(1) == 0)
        def _():
            carry[...] = jnp.zeros_like(carry)

        f = jnp.dot(h_ref[0], w_ref[...], preferred_element_type=F32)
        f_ref[0] = f
        logf = _log_sigmoid(f + b_ref[...])
        c = _dot_hi(_tri(ts, True), logf, NN) + carry[pl.ds(0, 1), :]
        cc_ref[0] = c
        carry[pl.ds(0, 1), :] = c[ts - 1:ts, :]
        cr_ref[0] = _dot_hi(_eye(LANES), c, NT)

    return _call(
        body, name=name, grid=(B, S // ts),
        in_specs=[pl.BlockSpec((1, ts, D), lambda b, s: (b, s, 0)),
                  pl.BlockSpec((D, LANES), lambda b, s: (0, 0)),
                  pl.BlockSpec((1, LANES), lambda b, s: (0, 0))],
        out_specs=[pl.BlockSpec((1, ts, LANES), lambda b, s: (b, s, 0)),
                   pl.BlockSpec((1, ts, LANES), lambda b, s: (b, s, 0)),
                   pl.BlockSpec((1, LANES, ts), lambda b, s: (b, 0, s))],
        out_shape=[jax.ShapeDtypeStruct((B, S, LANES), F32), jax.ShapeDtypeStruct((B, S, LANES), F32),
                   jax.ShapeDtypeStruct((B, LANES, S), F32)],
        scratch_shapes=[pltpu.VMEM((8, LANES), F32)],
        compiler_params=_params(("parallel", "arbitrary")),
    )(h, w_f, b_f)


def fgate_bwd(dc, f, b_f, *, name, ts=256):
    B, S, _ = f.shape
    P = dc.shape[1]
    ts = min(ts, S)
    ns = S // ts

    def body(dc_ref, f_ref, b_ref, df_ref, db_ref, carry):
        @pl.when(pl.program_id(1) == 0)
        def _():
            carry[...] = jnp.zeros_like(carry)

        @pl.when((pl.program_id(0) == 0) & (pl.program_id(1) == 0))
        def _():
            db_ref[...] = jnp.zeros_like(db_ref)

        dc_t = dc_ref[0, 0]
        for j in range(1, P):
            dc_t = dc_t + dc_ref[0, j]
        dlogf = _dot_hi(_tri(ts, False), dc_t, NN) + carry[pl.ds(0, 1), :]
        carry[pl.ds(0, 1), :] = dlogf[0:1, :]
        df = dlogf * _sigmoid(-(f_ref[0] + b_ref[...]))
        df_ref[0] = df.astype(BF16)
        db_ref[...] += jnp.sum(df, axis=0, keepdims=True)

    return _call(
        body, name=name, grid=(B, ns),
        in_specs=[pl.BlockSpec((1, P, ts, LANES), lambda b, s: (b, 0, ns - 1 - s, 0)),
                  pl.BlockSpec((1, ts, LANES), lambda b, s: (b, ns - 1 - s, 0)),
                  pl.BlockSpec((1, LANES), lambda b, s: (0, 0))],
        out_specs=[pl.BlockSpec((1, ts, LANES), lambda b, s: (b, ns - 1 - s, 0)),
                   pl.BlockSpec((1, LANES), lambda b, s: (0, 0))],
        out_shape=[jax.ShapeDtypeStruct((B, S, LANES), BF16), jax.ShapeDtypeStruct((1, LANES), F32)],
        scratch_shapes=[pltpu.VMEM((8, LANES), F32)],
        compiler_params=_params(("arbitrary", "arbitrary")),
    )(dc, f, b_f)


def _lane_pick(tile, idx):
    lane = lax.broadcasted_iota(jnp.int32, tile.shape, 1)
    return jnp.sum(jnp.where(lane == idx, tile, 0.0), axis=-1, keepdims=True)


FOX_T = 256


def _fox_heads(q, cc_ref, p):
    lane = lax.broadcasted_iota(jnp.int32, q.shape, 1)
    qs = q * (1.0 / math.sqrt(FOX_HEAD_DIM))
    qhs = [jnp.where((lane < FOX_HEAD_DIM) == (hh == 0), qs, jnp.zeros_like(qs)) for hh in range(2)]
    crefs = [_lane_pick(cc_ref[0, pl.ds(0, 1), :], 2 * p + hh) for hh in range(2)]
    return qhs, crefs


def _causal(t, transposed):
    r = lax.broadcasted_iota(jnp.int32, (t, t), 0)
    c = lax.broadcasted_iota(jnp.int32, (t, t), 1)
    return (r <= c) if transposed else (c <= r)


def fox_fwd(qkv, c_col, c_row, *, name):
    B, S, _ = qkv.shape
    assert S % FOX_T == 0
    tq, nq = FOX_T, S // FOX_T
    npair = FOX_HEADS // 2

    def body(q_ref, k_ref, v_ref, cc_ref, cr_ref, o_ref, l_ref, m_scr, l_scr, acc_scr):
        p, qi = pl.program_id(1), pl.program_id(2)
        qhs, crefs = _fox_heads(q_ref[0], cc_ref, p)
        for hh in range(2):
            m_scr[hh] = jnp.full((tq, 1), NEG, F32)
            l_scr[hh] = jnp.zeros((tq, 1), F32)
            acc_scr[hh] = jnp.zeros((tq, LANES), F32)

        def tile(kb, diagonal):
            k0 = pl.multiple_of(kb * tq, tq)
            k = k_ref[0, pl.ds(k0, tq), :]
            v = v_ref[0, pl.ds(k0, tq), :]
            for hh in range(2):
                s = lax.dot_general(qhs[hh], k, NT, preferred_element_type=F32)
                s = s + (crefs[hh] - cr_ref[0, pl.ds(2 * p + hh, 1), pl.ds(k0, tq)])
                if diagonal:
                    s = jnp.where(_causal(tq, False), s, NEG)
                m_old = m_scr[hh]
                m_new = jnp.maximum(m_old, jnp.max(s, axis=-1, keepdims=True))
                alpha = jnp.exp(m_old - m_new)
                pr = jnp.exp(s - m_new)
                m_scr[hh] = m_new
                l_scr[hh] = alpha * l_scr[hh] + jnp.sum(pr, axis=-1, keepdims=True)
                acc_scr[hh] = alpha * acc_scr[hh] + jnp.dot(pr.astype(BF16), v, preferred_element_type=F32)

        def off_diagonal(kb, carry):
            tile(kb, False)
            return carry

        lax.fori_loop(0, qi, off_diagonal, 0)
        tile(qi, True)
        first = lax.broadcasted_iota(jnp.int32, (tq, LANES), 1) < FOX_HEAD_DIM
        o_ref[0] = jnp.where(first, acc_scr[0] / l_scr[0], acc_scr[1] / l_scr[1]).astype(BF16)
        l_ref[0, 0] = jnp.where(first, m_scr[0] + jnp.log(l_scr[0]), m_scr[1] + jnp.log(l_scr[1]))

    return _call(
        body, name=name, grid=(B, npair, nq),
        in_specs=[pl.BlockSpec((1, tq, LANES), lambda b, p, i: (b, i, p)),
                  pl.BlockSpec((1, S, LANES), lambda b, p, i: (b, 0, npair + p)),
                  pl.BlockSpec((1, S, LANES), lambda b, p, i: (b, 0, 2 * npair + p)),
                  pl.BlockSpec((1, tq, LANES), lambda b, p, i: (b, i, 0)),
                  pl.BlockSpec((1, 8, S), lambda b, p, i: (b, 0, 0))],
        out_specs=[pl.BlockSpec((1, tq, LANES), lambda b, p, i: (b, i, p)),
                   pl.BlockSpec((1, 1, tq, LANES), lambda b, p, i: (b, p, i, 0))],
        out_shape=[jax.ShapeDtypeStruct((B, S, FOX_W), BF16),
                   jax.ShapeDtypeStruct((B, npair, S, LANES), F32)],
        scratch_shapes=[pltpu.VMEM((2, tq, 1), F32), pltpu.VMEM((2, tq, 1), F32),
                        pltpu.VMEM((2, tq, LANES), F32)],
        compiler_params=_params(("parallel", "parallel", "parallel")),
    )(qkv, qkv, qkv, c_col, c_row)


def fox_bwd_dq(qkv, dcat, lse, c_col, c_row, *, name):
    B, S, _ = qkv.shape
    tq, nq = FOX_T, S // FOX_T
    npair = FOX_HEADS // 2

    def body(q_ref, k_ref, v_ref, do_ref, l_ref, cc_ref, cr_ref, dq_ref, st_ref, p_scr, dp_scr, dl_scr):
        p, qi = pl.program_id(1), pl.program_id(2)
        qhs, crefs = _fox_heads(q_ref[0], cc_ref, p)
        lane = lax.broadcasted_iota(jnp.int32, (tq, LANES), 1)
        do_b = do_ref[0].astype(BF16)
        dohs = [jnp.where((lane < FOX_HEAD_DIM) == (hh == 0), do_b, jnp.zeros_like(do_b)) for hh in range(2)]
        lses = [_lane_pick(l_ref[0, 0], hh * FOX_HEAD_DIM) for hh in range(2)]
        for hh in range(2):
            dl_scr[hh] = jnp.zeros((tq, 1), F32)

        def probs(kb, diagonal):
            k0 = pl.multiple_of(kb * tq, tq)
            k = k_ref[0, pl.ds(k0, tq), :]
            v = v_ref[0, pl.ds(k0, tq), :]
            for hh in range(2):
                s = lax.dot_general(qhs[hh], k, NT, preferred_element_type=F32)
                s = s + (crefs[hh] - cr_ref[0, pl.ds(2 * p + hh, 1), pl.ds(k0, tq)])
                pr = jnp.exp(s - lses[hh])
                if diagonal:
                    pr = jnp.where(_causal(tq, False), pr, 0.0)
                dp = lax.dot_general(dohs[hh], v, NT, preferred_element_type=F32)
                dl_scr[hh] += jnp.sum(pr * dp, axis=-1, keepdims=True)
                p_scr[hh, kb] = pr
                dp_scr[hh, kb] = dp

        def first_pass(kb, carry):
            probs(kb, False)
            return carry

        lax.fori_loop(0, qi, first_pass, 0)
        probs(qi, True)

        def second_pass(kb, dq):
            k0 = pl.multiple_of(kb * tq, tq)
            k = k_ref[0, pl.ds(k0, tq), :]
            for hh in range(2):
                ds = p_scr[hh, kb] * (dp_scr[hh, kb] - dl_scr[hh])
                kh = jnp.where((lane < FOX_HEAD_DIM) == (hh == 0), k, jnp.zeros_like(k))
                dq = dq + jnp.dot(ds.astype(BF16), kh, preferred_element_type=F32)
            return dq

        dq = lax.fori_loop(0, qi + 1, second_pass, jnp.zeros((tq, LANES), F32))
        dq_ref[0] = (dq * (1.0 / math.sqrt(FOX_HEAD_DIM))).astype(BF16)
        cols = jnp.zeros((tq, LANES), F32)
        for j, col in enumerate([crefs[0] - lses[0], crefs[1] - lses[1], dl_scr[0], dl_scr[1]]):
            cols = jnp.where(lane == j, col, cols)
        st_ref[0, 0] = _dot_hi(_eye(LANES), cols, NT)[:8]

    return _call(
        body, name=name, grid=(B, npair, nq),
        in_specs=[pl.BlockSpec((1, tq, LANES), lambda b, p, i: (b, i, p)),
                  pl.BlockSpec((1, S, LANES), lambda b, p, i: (b, 0, npair + p)),
                  pl.BlockSpec((1, S, LANES), lambda b, p, i: (b, 0, 2 * npair + p)),
                  pl.BlockSpec((1, tq, LANES), lambda b, p, i: (b, i, npair + p)),
                  pl.BlockSpec((1, 1, tq, LANES), lambda b, p, i: (b, p, i, 0)),
                  pl.BlockSpec((1, tq, LANES), lambda b, p, i: (b, i, 0)),
                  pl.BlockSpec((1, 8, S), lambda b, p, i: (b, 0, 0))],
        out_specs=[pl.BlockSpec((1, tq, LANES), lambda b, p, i: (b, i, p)),
                   pl.BlockSpec((1, 1, 8, tq), lambda b, p, i: (b, p, 0, i))],
        out_shape=[jax.ShapeDtypeStruct((B, S, FOX_W), BF16), jax.ShapeDtypeStruct((B, npair, 8, S), F32)],
        scratch_shapes=[pltpu.VMEM((2, nq, tq, tq), F32), pltpu.VMEM((2, nq, tq, tq), F32),
                        pltpu.VMEM((2, tq, 1), F32)],
        compiler_params=_params(("parallel", "parallel", "parallel"), 40 << 20),
    )(qkv, qkv, qkv, dcat, lse, c_col, c_row)


def fox_bwd_dkdv(qkv, dcat, stats, c_col, *, name):
    B, S, _ = qkv.shape
    tk, nq = FOX_T, S // FOX_T
    npair = FOX_HEADS // 2
    inv = 1.0 / math.sqrt(FOX_HEAD_DIM)

    def body(q_ref, k_ref, v_ref, do_ref, st_ref, cc_ref, dk_ref, dv_ref, dc_ref, dk_scr, dv_scr, dc_scr):
        p, kt = pl.program_id(1), pl.program_id(2)
        lane = lax.broadcasted_iota(jnp.int32, (tk, LANES), 1)
        masks = [(lane < FOX_HEAD_DIM) == (hh == 0) for hh in range(2)]
        k = k_ref[0]
        v = v_ref[0]
        khs = [jnp.where(masks[hh], k, jnp.zeros_like(k)) for hh in range(2)]
        vhs = [jnp.where(masks[hh], v, jnp.zeros_like(v)) for hh in range(2)]
        ccols = [_lane_pick(cc_ref[0], 2 * p + hh) for hh in range(2)]
        dk_scr[...] = jnp.zeros_like(dk_scr)
        dv_scr[...] = jnp.zeros_like(dv_scr)
        dc_scr[...] = jnp.zeros_like(dc_scr)

        def tile(qb, diagonal):
            q0 = pl.multiple_of(qb * tk, tk)
            qs = q_ref[0, pl.ds(q0, tk), :] * inv
            do_b = do_ref[0, pl.ds(q0, tk), :].astype(BF16)
            for hh in range(2):
                st = lax.dot_general(khs[hh], qs, NT, preferred_element_type=F32)
                pr = jnp.exp(st - ccols[hh] + st_ref[0, 0, pl.ds(hh, 1), pl.ds(q0, tk)])
                if diagonal:
                    pr = jnp.where(_causal(tk, True), pr, 0.0)
                dp = lax.dot_general(vhs[hh], do_b, NT, preferred_element_type=F32)
                ds = pr * (dp - st_ref[0, 0, pl.ds(2 + hh, 1), pl.ds(q0, tk)])
                dv_scr[...] += jnp.dot(pr.astype(BF16), jnp.where(masks[hh], do_b, jnp.zeros_like(do_b)),
                                       preferred_element_type=F32)
                dk_scr[...] += jnp.dot(ds.astype(BF16), jnp.where(masks[hh], qs, jnp.zeros_like(qs)),
                                       preferred_element_type=F32)
                dc_scr[hh] -= jnp.sum(ds, axis=-1, keepdims=True)

        def later(qb, carry):
            tile(qb, False)
            return carry

        tile(kt, True)
        lax.fori_loop(kt + 1, nq, later, 0)
        dk_ref[0] = dk_scr[...].astype(BF16)
        dv_ref[0] = dv_scr[...].astype(BF16)
        dc_ref[0, 0] = jnp.where(lane == 2 * p, dc_scr[0], jnp.where(lane == 2 * p + 1, dc_scr[1], 0.0))

    full = lambda col: pl.BlockSpec((1, S, LANES), col)
    tile_spec = lambda col: pl.BlockSpec((1, tk, LANES), col)
    return _call(
        body, name=name, grid=(B, npair, nq),
        in_specs=[full(lambda b, p, t: (b, 0, p)),
                  tile_spec(lambda b, p, t: (b, t, npair + p)),
                  tile_spec(lambda b, p, t: (b, t, 2 * npair + p)),
                  full(lambda b, p, t: (b, 0, npair + p)),
                  pl.BlockSpec((1, 1, 8, S), lambda b, p, t: (b, p, 0, 0)),
                  tile_spec(lambda b, p, t: (b, t, 0))],
        out_specs=[tile_spec(lambda b, p, t: (b, t, p)), tile_spec(lambda b, p, t: (b, t, p)),
                   pl.BlockSpec((1, 1, tk, LANES), lambda b, p, t: (b, p, t, 0))],
        out_shape=[jax.ShapeDtypeStruct((B, S, FOX_W), BF16)] * 2
        + [jax.ShapeDtypeStruct((B, npair, S, LANES), F32)],
        scratch_shapes=[pltpu.VMEM((tk, LANES), F32), pltpu.VMEM((tk, LANES), F32),
                        pltpu.VMEM((2, tk, 1), F32)],
        compiler_params=_params(("parallel", "parallel", "parallel")),
    )(qkv, qkv, qkv, dcat, stats, c_col)


def xattn_fwd(qm, kv, *, name, tq=256):
    B, S, D = qm.shape
    M = kv.shape[1]
    tq = min(tq, S)
    inv = 1.0 / math.sqrt(MEM_HEAD_DIM)

    def body(q_ref, kv_ref, o_ref):
        for h in range(MEM_HEADS):
            c0 = h * MEM_HEAD_DIM
            qh = q_ref[0, :, c0:c0 + MEM_HEAD_DIM]
            kh = kv_ref[0, :, c0:c0 + MEM_HEAD_DIM]
            vh = kv_ref[0, :, D + c0:D + c0 + MEM_HEAD_DIM]
            s = lax.dot_general(qh, kh, NT, preferred_element_type=F32) * inv
            e = jnp.exp(s - jnp.max(s, axis=-1, keepdims=True))
            o = jnp.dot(e.astype(BF16), vh, preferred_element_type=F32) / jnp.sum(e, axis=-1, keepdims=True)
            o_ref[0, :, c0:c0 + MEM_HEAD_DIM] = o.astype(BF16)

    return _call(
        body, name=name, grid=(B, S // tq),
        in_specs=[pl.BlockSpec((1, tq, D), lambda b, i: (b, i, 0)),
                  pl.BlockSpec((1, M, 2 * D), lambda b, i: (b, 0, 0))],
        out_specs=pl.BlockSpec((1, tq, D), lambda b, i: (b, i, 0)),
        out_shape=jax.ShapeDtypeStruct((B, S, D), BF16),
        compiler_params=_params(("parallel", "parallel")),
    )(qm, kv)


def xattn_bwd(qm, kv, do, *, name, tq=256):
    B, S, D = qm.shape
    M = kv.shape[1]
    tq = min(tq, S)
    inv = 1.0 / math.sqrt(MEM_HEAD_DIM)

    def body(q_ref, kv_ref, do_ref, dq_ref, dkv_ref):
        @pl.when(pl.program_id(1) == 0)
        def _():
            dkv_ref[...] = jnp.zeros_like(dkv_ref)

        for h in range(MEM_HEADS):
            c0 = h * MEM_HEAD_DIM
            qh = q_ref[0, :, c0:c0 + MEM_HEAD_DIM]
            kh = kv_ref[0, :, c0:c0 + MEM_HEAD_DIM]
            vh = kv_ref[0, :, D + c0:D + c0 + MEM_HEAD_DIM]
            doh = do_ref[0, :, c0:c0 + MEM_HEAD_DIM]
            s = lax.dot_general(qh, kh, NT, preferred_element_type=F32) * inv
            e = jnp.exp(s - jnp.max(s, axis=-1, keepdims=True))
            pr = e / jnp.sum(e, axis=-1, keepdims=True)
            dp = lax.dot_general(doh, vh, NT, preferred_element_type=F32)
            ds = pr * (dp - jnp.sum(pr * dp, axis=-1, keepdims=True))
            ds_b = ds.astype(BF16)
            dq_ref[0, :, c0:c0 + MEM_HEAD_DIM] = (jnp.dot(ds_b, kh, preferred_element_type=F32) * inv).astype(BF16)
            dkv_ref[0, :, c0:c0 + MEM_HEAD_DIM] += lax.dot_general(ds_b, qh, TN, preferred_element_type=F32) * inv
            dkv_ref[0, :, D + c0:D + c0 + MEM_HEAD_DIM] += lax.dot_general(
                pr.astype(BF16), doh, TN, preferred_element_type=F32)

    row = pl.BlockSpec((1, tq, D), lambda b, i: (b, i, 0))
    kvs = pl.BlockSpec((1, M, 2 * D), lambda b, i: (b, 0, 0))
    return _call(
        body, name=name, grid=(B, S // tq), in_specs=[row, kvs, row], out_specs=[row, kvs],
        out_shape=[jax.ShapeDtypeStruct((B, S, D), BF16), jax.ShapeDtypeStruct((B, M, 2 * D), F32)],
        compiler_params=_params(("parallel", "arbitrary")),
    )(qm, kv, do)


def swiglu_fwd(gu, *, name, tm=256):
    T, F2 = gu.shape
    Fh = F2 // 2
    tm = min(tm, T)

    def body(gu_ref, o_ref):
        g = gu_ref[:, :Fh].astype(F32)
        u = gu_ref[:, Fh:].astype(F32)
        o_ref[...] = (g * _sigmoid(g) * u).astype(BF16)

    return _call(
        body, name=name, grid=(T // tm,),
        in_specs=[pl.BlockSpec((tm, F2), lambda i: (i, 0))],
        out_specs=pl.BlockSpec((tm, Fh), lambda i: (i, 0)),
        out_shape=jax.ShapeDtypeStruct((T, Fh), BF16),
        compiler_params=_params(("parallel",)),
    )(gu)


def swiglu_bwd(gu, dact, *, name, tm=256):
    T, F2 = gu.shape
    Fh = F2 // 2
    tm = min(tm, T)

    def body(gu_ref, d_ref, o_ref):
        g = gu_ref[:, :Fh].astype(F32)
        u = gu_ref[:, Fh:].astype(F32)
        d = d_ref[...].astype(F32)
        sg = _sigmoid(g)
        o_ref[:, :Fh] = (d * u * (sg * (1.0 + g * (1.0 - sg)))).astype(BF16)
        o_ref[:, Fh:] = (d * g * sg).astype(BF16)

    return _call(
        body, name=name, grid=(T // tm,),
        in_specs=[pl.BlockSpec((tm, F2), lambda i: (i, 0)), pl.BlockSpec((tm, Fh), lambda i: (i, 0))],
        out_specs=pl.BlockSpec((tm, F2), lambda i: (i, 0)),
        out_shape=jax.ShapeDtypeStruct((T, F2), BF16),
        compiler_params=_params(("parallel",)),
    )(gu, dact)


def local_step(x, mem, target, sp, wf):
    B, S, D = x.shape
    T = B * S
    M = mem.shape[1]
    row = lambda v: v.reshape(1, -1).astype(F32)
    g_mix, g_x, g_mem, g_ffn, g_final = (row(sp[k]) for k in ("g_mix", "g_x", "g_mem", "g_ffn", "g_final"))
    conv_b, ln_g, ln_b = row(sp["conv_b"]), row(sp["ln_g"]), row(sp["ln_b"])
    conv_w = jnp.pad(sp["conv_w"].astype(F32), ((0, HALO - CONV_K), (0, 0)))
    b_f = jnp.pad(row(sp["b_f"]), ((0, 0), (0, LANES - FOX_HEADS)))
    n_main = 2 * CONV_CH + 3 * FOX_W
    w_main = wf["w_in"][:, :n_main]
    w_f = jnp.pad(wf["w_in"][:, n_main:], ((0, 0), (0, LANES - FOX_HEADS)))

    x2d = x.reshape(T, D)
    h = rmsnorm_fwd(x2d, g_mix, name="rms_mix")
    z = matmul(h, w_main, out_dtype=BF16, tn=n_main, name="mm_in")
    z3 = z.reshape(B, S, n_main)
    ug, qkv = z3[:, :, :2 * CONV_CH], z3[:, :, 2 * CONV_CH:]
    conv_out = conv_branch_fwd(ug, conv_w, conv_b, ln_g, ln_b, name="conv_fwd")
    f_raw, c_col, c_row = fgate_fwd(h.reshape(B, S, D), w_f, b_f, name="fgate_fwd")
    att, lse = fox_fwd(qkv, c_col, c_row, name="fox_fwd")
    cat = jnp.concatenate([conv_out, att], axis=-1).reshape(T, D)
    x1 = matmul(cat, wf["w_out"], out_dtype=F32, res=x2d, tn=D, name="mm_out")
    hx = rmsnorm_fwd(x1, g_x, name="rms_x")
    qm = matmul(hx, wf["w_mq"], out_dtype=BF16, tn=D, name="mm_mq")
    mem2d = mem.reshape(B * M, D)
    mem_n = rmsnorm_fwd(mem2d, g_mem, name="rms_mem")
    kv = matmul(mem_n, wf["w_mkv"], out_dtype=BF16, tn=2 * D, name="mm_mkv").reshape(B, M, 2 * D)
    o = xattn_fwd(qm.reshape(B, S, D), kv, name="xattn_fwd").reshape(T, D)
    x2 = matmul(o, wf["w_mo"], out_dtype=F32, res=x1, tn=D, name="mm_mo")
    hf = rmsnorm_fwd(x2, g_ffn, name="rms_ffn")
    gu = matmul(hf, wf["w_gu"], out_dtype=BF16, tn=2816, name="mm_gu")
    act = swiglu_fwd(gu, name="swiglu_fwd")
    x3 = matmul(act, wf["w_down"], out_dtype=F32, res=x2, tn=D, name="mm_down")
    dx3, dg_final, loss = final_loss_bwd(x3, g_final, target.reshape(T, D), name="loss_bwd")
    gw = {}
    gw["w_down"] = matmul(act.T, dx3, out_dtype=BF16, tm=1408, tn=256, name="dw_down")
    dact = matmul(dx3, wf["w_down"], tb=True, out_dtype=BF16, tn=2816, name="dx_down")
    dgu = swiglu_bwd(gu, dact, name="swiglu_bwd")
    gw["w_gu"] = matmul(hf.T, dgu, out_dtype=BF16, tn=1408, name="dw_gu")
    dhf = matmul(dgu, wf["w_gu"], tb=True, out_dtype=BF16, tm=256, tn=D, name="dx_gu")
    dx2, dg_ffn = rmsnorm_bwd(x2, g_ffn, dhf, dx3, name="rms_ffn_bwd")
    gw["w_mo"] = matmul(o.T, dx2, out_dtype=BF16, name="dw_mo")
    do = matmul(dx2, wf["w_mo"], tb=True, out_dtype=BF16, tn=D, name="dx_mo")
    dqm, dkv = xattn_bwd(qm.reshape(B, S, D), kv, do.reshape(B, S, D), name="xattn_bwd")
    dqm = dqm.reshape(T, D)
    dkv = dkv.reshape(B * M, 2 * D)
    gw["w_mq"] = matmul(hx.T, dqm, out_dtype=BF16, tn=D, name="dw_mq")
    dhx = matmul(dqm, wf["w_mq"], tb=True, out_dtype=BF16, tn=D, name="dx_mq")
    gw["w_mkv"] = matmul(mem_n.T, dkv, out_dtype=BF16, tn=D, name="dw_mkv")
    dmem_n = matmul(dkv, wf["w_mkv"], tb=True, out_dtype=BF16, tn=D, name="dx_mkv")
    _, dg_mem = rmsnorm_bwd(mem2d, g_mem, dmem_n, None, name="rms_mem_bwd")
    dx1, dg_x = rmsnorm_bwd(x1, g_x, dhx, dx2, name="rms_x_bwd")
    gw["w_out"] = matmul(cat.T, dx1, out_dtype=BF16, name="dw_out")
    dcat = matmul(dx1, wf["w_out"], tb=True, out_dtype=BF16, tn=D, name="dx_out").reshape(B, S, D)
    dy, dconv_w, dvec = conv_branch_bwd_a(ug, dcat, conv_w, conv_b, ln_g, ln_b, name="conv_bwd_a")
    dug = conv_branch_bwd_b(ug, dy, conv_w, name="conv_bwd_b")
    dq, stats = fox_bwd_dq(qkv, dcat, lse, c_col, c_row, name="fox_bwd_dq")
    dk, dv, dc = fox_bwd_dkdv(qkv, dcat, stats, c_col, name="fox_bwd_dkdv")
    df, db_f = fgate_bwd(dc, f_raw, b_f, name="fgate_bwd")
    dz = jnp.concatenate([dug, dq, dk, dv], axis=-1).reshape(T, n_main)
    df2 = df.reshape(T, LANES)
    h_t = h.T
    dw_main = matmul(h_t, dz, out_dtype=BF16, tn=1280, name="dw_in")
    dw_f = matmul(h_t, df2, out_dtype=BF16, name="dw_f")
    gw["w_in"] = jnp.concatenate([dw_main, dw_f[:, :FOX_HEADS]], axis=-1)
    dh_f = matmul(df2, w_f, tb=True, out_dtype=F32, tn=D, name="dx_f")
    dh = matmul(dz, w_main, tb=True, out_dtype=F32, res=dh_f, tn=D, name="dx_in")
    dx, dg_mix = rmsnorm_bwd(x2d, g_mix, dh, dx1, name="rms_mix_bwd")
    gs = dict(g_mix=dg_mix, b_f=db_f[:, :FOX_HEADS], conv_w=dconv_w[:CONV_K], conv_b=dvec[0:1],
              ln_g=dvec[1:2], ln_b=dvec[2:3], g_x=dg_x, g_mem=dg_mem, g_ffn=dg_ffn, g_final=dg_final)
    return loss, dx.reshape(B, S, D), gs, gw


def _me():
    return lax.axis_index("x"), lax.axis_index("y"), lax.axis_index("c")


def _any_specs(n):
    return [pl.BlockSpec(memory_space=pl.ANY)] * n


def all_gather(xs, *, name):
    n = len(xs)

    def body(*refs):
        x_refs, out_refs = refs[:n], refs[n:2 * n]
        send_sems, recv_sems, local_sems = refs[2 * n:]
        x, y, c = _me()
        me, sibling = (x, y, c), (x, y, 1 - c)
        chips = [(1 - x, y), (x, 1 - y), (1 - x, 1 - y)]

        def slot(a, px, py, pc):
            return out_refs[a].at[4 * px + 2 * py + pc]

        def copy(a, k, block, to, own=False):
            return pltpu.make_async_remote_copy(
                src_ref=x_refs[a] if own else slot(a, *block), dst_ref=slot(a, *block),
                send_sem=send_sems.at[k, a], recv_sem=recv_sems.at[k, a], device_id=to, device_id_type=MESH)

        mine = [pltpu.make_async_copy(x_refs[a], slot(a, *me), local_sems.at[a]) for a in range(n)]
        first = [copy(a, 0, me, sibling, own=True) for a in range(n)]
        first += [copy(a, 1 + j, me, (*chip, c), own=True) for j, chip in enumerate(chips) for a in range(n)]
        for cp in mine + first:
            cp.start()
        passed = []
        for j, chip in enumerate(chips):
            for a in range(n):
                copy(a, 1 + j, (*chip, c), me).wait_recv()
                passed.append(copy(a, 4 + j, (*chip, c), sibling))
                passed[-1].start()
        for a in range(n):
            copy(a, 0, sibling, me).wait_recv()
            for j, chip in enumerate(chips):
                copy(a, 4 + j, (*chip, 1 - c), me).wait_recv()
        for cp in first + passed:
            cp.wait_send()
        for cp in mine:
            cp.wait()

    return _call(
        body, name=name, in_specs=_any_specs(n), out_specs=_any_specs(n),
        out_shape=[jax.ShapeDtypeStruct((N_DEV,) + v.shape, v.dtype) for v in xs],
        scratch_shapes=[pltpu.SemaphoreType.DMA((7, n)), pltpu.SemaphoreType.DMA((7, n)),
                        pltpu.SemaphoreType.DMA((n,))],
    )(*xs)


def sibling_exchange(gs, *, name):
    n = len(gs)

    def body(*refs):
        g_refs, out_refs = refs[:n], refs[n:2 * n]
        send_sems, recv_sems = refs[2 * n:]
        x, y, c = _me()
        cps = [pltpu.make_async_remote_copy(
            src_ref=g_refs[a].at[:, 1 - c], dst_ref=out_refs[a], send_sem=send_sems.at[a],
            recv_sem=recv_sems.at[a], device_id=(x, y, 1 - c), device_id_type=MESH) for a in range(n)]
        for cp in cps:
            cp.start()
        for cp in cps:
            cp.wait()

    return _call(
        body, name=name, in_specs=_any_specs(n), out_specs=_any_specs(n),
        out_shape=[jax.ShapeDtypeStruct((4,) + g.shape[2:], g.dtype) for g in gs],
        scratch_shapes=[pltpu.SemaphoreType.DMA((n,)), pltpu.SemaphoreType.DMA((n,))],
    )(*gs)


def chip_exchange(ps, *, name):
    n = len(ps)

    def body(*refs):
        p_refs, out_refs = refs[:n], refs[n:2 * n]
        send_sems, recv_sems = refs[2 * n:]
        x, y, c = _me()
        my_chip = 2 * x + y
        peers = [(x ^ (k >> 1), y ^ (k & 1)) for k in range(1, 4)]

        def copy(a, k, src_chip, dst_chip, to):
            return pltpu.make_async_remote_copy(
                src_ref=p_refs[a].at[src_chip], dst_ref=out_refs[a].at[dst_chip],
                send_sem=send_sems.at[k, a], recv_sem=recv_sems.at[k, a], device_id=to, device_id_type=MESH)

        sends = [copy(a, k, 2 * px + py, my_chip, (px, py, c)) for k, (px, py) in enumerate(peers) for a in range(n)]
        for cp in sends:
            cp.start()
        for k, (px, py) in enumerate(peers):
            for a in range(n):
                copy(a, k, my_chip, 2 * px + py, (px, py, c)).wait_recv()
        for cp in sends:
            cp.wait_send()

    return _call(
        body, name=name, in_specs=_any_specs(n), out_specs=_any_specs(n),
        out_shape=[jax.ShapeDtypeStruct(p.shape, p.dtype) for p in ps],
        scratch_shapes=[pltpu.SemaphoreType.DMA((3, n)), pltpu.SemaphoreType.DMA((3, n))],
    )(*ps)


def _pick_rows(r, target=256):
    best = None
    for d in range(16, min(r, target) + 1, 16):
        if r % d == 0:
            best = d
    return r if best is None else best


def pair_sum(g, got, *, name):
    _, _, R, C = g.shape
    tr = _pick_rows(R)

    def body(g_ref, got_ref, o_ref):
        mine = jnp.where(lax.axis_index("c") == 0, g_ref[:, 0], g_ref[:, 1])
        o_ref[...] = (mine.astype(F32) + got_ref[...].astype(F32)).astype(o_ref.dtype)

    return _call(
        body, name=name, grid=(R // tr,),
        in_specs=[pl.BlockSpec((4, 2, tr, C), lambda i: (0, 0, i, 0)), pl.BlockSpec((4, tr, C), lambda i: (0, i, 0))],
        out_specs=pl.BlockSpec((4, tr, C), lambda i: (0, i, 0)),
        out_shape=jax.ShapeDtypeStruct((4, R, C), g.dtype),
        compiler_params=_params(("parallel",)),
    )(g, got)


def chip_sum_adamw(p, got, w, m, v, *, name):
    _, R, C = p.shape
    tr = _pick_rows(R)

    def body(p_ref, got_ref, w_ref, m_ref, v_ref, g_ref, d_ref, mo_ref, vo_ref):
        my_chip = 2 * lax.axis_index("x") + lax.axis_index("y")
        g = jnp.zeros((tr, C), F32)
        for j in range(4):
            g = g + jnp.where(my_chip == j, p_ref[j], got_ref[j]).astype(F32)
        g_ref[...] = g
        d_ref[...], mo_ref[...], vo_ref[...] = _adamw_math(w_ref[...], g, m_ref[...], v_ref[...])

    part = pl.BlockSpec((4, tr, C), lambda i: (0, i, 0))
    spec = pl.BlockSpec((tr, C), lambda i: (i, 0))
    return _call(
        body, name=name, grid=(R // tr,), in_specs=[part, part, spec, spec, spec], out_specs=[spec] * 4,
        out_shape=[jax.ShapeDtypeStruct((R, C), F32)] * 4,
        compiler_params=_params(("parallel",)),
    )(p, got, w, m, v)


def rows_sum(g8, *, name):
    _, R, C = g8.shape

    def body(g_ref, o_ref):
        acc = g_ref[0]
        for j in range(1, N_DEV):
            acc = acc + g_ref[j]
        o_ref[...] = acc

    return _call(body, name=name, out_shape=jax.ShapeDtypeStruct((R, C), F32))(g8)


def _adamw_math(w, g, m, v):
    m = ADAM_B1 * m + (1.0 - ADAM_B1) * g
    v = ADAM_B2 * v + (1.0 - ADAM_B2) * (g * g)
    m_hat = m / (1.0 - ADAM_B1 ** ADAM_STEP)
    v_hat = v / (1.0 - ADAM_B2 ** ADAM_STEP)
    delta = -ADAM_LR * (m_hat / (jnp.sqrt(v_hat) + ADAM_EPS) + ADAM_WD * w)
    return delta, m, v


def adamw_small(wgmv, *, name):
    n = len(wgmv)

    def body(*refs):
        ins, outs = refs[:4 * n], refs[4 * n:]
        for a in range(n):
            w_ref, g_ref, m_ref, v_ref = ins[4 * a:4 * a + 4]
            d, mn, vn = _adamw_math(w_ref[...], g_ref[...], m_ref[...], v_ref[...])
            outs[3 * a][...] = d
            outs[3 * a + 1][...] = mn
            outs[3 * a + 2][...] = vn

    flat = [t for tup in wgmv for t in tup]
    res = _call(
        body, name=name,
        out_shape=[jax.ShapeDtypeStruct(tup[0].shape, F32) for tup in wgmv for _ in range(3)],
    )(*flat)
    return [tuple(res[3 * a:3 * a + 3]) for a in range(n)]


BIG = ("w_in", "w_out", "w_mq", "w_mkv", "w_mo", "w_gu", "w_down")
COL_SHARDED = ("w_in", "w_mkv", "w_gu")
SMALL = ("g_mix", "b_f", "conv_w", "conv_b", "ln_g", "ln_b", "g_x", "g_mem", "g_ffn", "g_final")


def _full_from_gathered(n, blk):
    _, rr, cc = blk.shape
    if n in COL_SHARDED:
        return blk.transpose(1, 0, 2).reshape(rr, N_DEV * cc)
    return blk.reshape(N_DEV * rr, cc)


def _shards_from_full(n, g):
    rr, cc = g.shape
    if n in COL_SHARDED:
        return g.reshape(rr, 4, 2, cc // N_DEV).transpose(1, 2, 0, 3)
    return g.reshape(4, 2, rr // N_DEV, cc)


def _small_layout():
    sizes = dict(g_mix=1024, b_f=8, conv_w=CONV_K * CONV_CH, conv_b=512, ln_g=512, ln_b=512, g_x=1024,
                 g_mem=1024, g_ffn=1024, g_final=1024, loss=1)
    lay, r0 = {}, 0
    for n, sz in sizes.items():
        r = -(-sz // LANES)
        lay[n] = (r0, r, sz)
        r0 += r
    return lay, -(-r0 // 8) * 8


def kernel(x, mem, g_mix, w_in, b_f, conv_w, conv_b, ln_g, ln_b, w_out, g_x, g_mem, w_mq, w_mkv, w_mo, g_ffn, w_gu, w_down, g_final, loss_target, m_g_mix, m_w_in, m_b_f, m_conv_w, m_conv_b, m_ln_g, m_ln_b, m_w_out, m_g_x, m_g_mem, m_w_mq, m_w_mkv, m_w_mo, m_g_ffn, m_w_gu, m_w_down, m_g_final, v_g_mix, v_w_in, v_b_f, v_conv_w, v_conv_b, v_ln_g, v_ln_b, v_w_out, v_g_x, v_g_mem, v_w_mq, v_w_mkv, v_w_mo, v_g_ffn, v_w_gu, v_w_down, v_g_final):
    names = ["g_mix", "w_in", "b_f", "conv_w", "conv_b", "ln_g", "ln_b", "w_out", "g_x", "g_mem", "w_mq",
             "w_mkv", "w_mo", "g_ffn", "w_gu", "w_down", "g_final"]
    W = dict(zip(names, [g_mix, w_in, b_f, conv_w, conv_b, ln_g, ln_b, w_out, g_x, g_mem, w_mq, w_mkv, w_mo,
                         g_ffn, w_gu, w_down, g_final]))
    Mo = dict(zip(names, [m_g_mix, m_w_in, m_b_f, m_conv_w, m_conv_b, m_ln_g, m_ln_b, m_w_out, m_g_x, m_g_mem,
                          m_w_mq, m_w_mkv, m_w_mo, m_g_ffn, m_w_gu, m_w_down, m_g_final]))
    Vo = dict(zip(names, [v_g_mix, v_w_in, v_b_f, v_conv_w, v_conv_b, v_ln_g, v_ln_b, v_w_out, v_g_x, v_g_mem,
                          v_w_mq, v_w_mkv, v_w_mo, v_g_ffn, v_w_gu, v_w_down, v_g_final]))
    dev = 4 * lax.axis_index("x") + 2 * lax.axis_index("y") + lax.axis_index("c")

    two = lambda a: a.reshape(-1, a.shape[-1])
    cw_shard = jnp.pad(two(conv_w), ((0, HALO - CONV_K), (0, 0)))
    gathered = all_gather([two(W[n]).astype(BF16) for n in BIG] + [cw_shard], name="ag_weights")
    wf = {n: _full_from_gathered(n, blk) for n, blk in zip(BIG, gathered)}
    cw_full = gathered[-1].transpose(1, 0, 2).reshape(HALO, -1)[:CONV_K]

    sp = dict(g_mix=g_mix, b_f=b_f, conv_w=cw_full, conv_b=conv_b, ln_g=ln_g, ln_b=ln_b, g_x=g_x, g_mem=g_mem,
              g_ffn=g_ffn, g_final=g_final)
    loss_blk, grad_x, gs, gw = local_step(x, mem, loss_target, sp, wf)

    g42 = [_shards_from_full(n, gw[n]) for n in BIG]
    got1 = sibling_exchange(g42, name="rs_sibling")
    part = [pair_sum(g, o, name="rs_pair_sum_" + n) for n, g, o in zip(BIG, g42, got1)]
    got2 = chip_exchange(part, name="rs_chips")

    lay, rs = _small_layout()
    small = {**{n: gs[n] for n in SMALL}, "loss": loss_blk[:, :1]}
    parts = []
    for n, (r0, r, sz) in lay.items():
        flat = small[n].reshape(-1).astype(F32)
        parts.append(jnp.pad(flat, (0, r * LANES - sz)).reshape(r, LANES))
    spack = jnp.concatenate(parts, axis=0)
    spack = jnp.pad(spack, ((0, rs - spack.shape[0]), (0, 0)))
    ssum = rows_sum(all_gather([spack], name="ag_small")[0], name="small_sum")
    gsmall = {n: ssum[r0:r0 + r].reshape(-1)[:sz] for n, (r0, r, sz) in lay.items()}
    loss = gsmall["loss"].reshape(())

    grads, delta, new_m, new_v = {}, {}, {}, {}
    for n, p, o in zip(BIG, part, got2):
        shp = W[n].shape
        g, d, mn, vn = chip_sum_adamw(p, o, two(W[n]), two(Mo[n]), two(Vo[n]), name="adamw_" + n)
        grads[n], delta[n], new_m[n], new_v[n] = g.reshape(shp), d.reshape(shp), mn.reshape(shp), vn.reshape(shp)
    for n in SMALL:
        if n == "conv_w":
            full = gsmall[n].reshape(CONV_K, CONV_CH)
            ncol = conv_w.shape[-1]
            grads[n] = lax.dynamic_slice(full, (0, dev * ncol), (CONV_K, ncol)).reshape(conv_w.shape)
        else:
            grads[n] = gsmall[n].reshape(W[n].shape)
    upd = adamw_small([(two(W[n]), two(grads[n]), two(Mo[n]), two(Vo[n])) for n in SMALL], name="adamw_small")
    for n, (d, mn, vn) in zip(SMALL, upd):
        shp = W[n].shape
        delta[n], new_m[n], new_v[n] = d.reshape(shp), mn.reshape(shp), vn.reshape(shp)
    return (loss, grad_x, *[grads[n] for n in names], *[delta[n] for n in names],
            *[new_m[n] for n in names], *[new_v[n] for n in names])
```

```python
import functools
import math

import jax
import jax.numpy as jnp
from jax import lax
from jax.experimental import pallas as pl
from jax.experimental.pallas import tpu as pltpu

F32 = jnp.float32
BF16 = jnp.bfloat16
EPS = 1e-6
N_DEV = 8
CONV_CH = 512
CONV_K = 31
FOX_HEADS = 8
FOX_HEAD_DIM = 64
FOX_W = 512
MEM_HEADS = 4
MEM_HEAD_DIM = 256
HALO = 32
LANES = 128
ADAM_LR, ADAM_B1, ADAM_B2, ADAM_EPS, ADAM_WD, ADAM_STEP = 0.001, 0.9, 0.999, 1e-08, 0.01, 10
NEG = -1e30
VMEM_CAP = 60 * 1024 * 1024
MESH = pl.DeviceIdType.MESH


def _call(body, **kw):
    return pl.pallas_call(body, **kw)


def _params(sem=None, vmem=None):
    kw = {}
    if sem is not None:
        kw["dimension_semantics"] = sem
    if vmem is not None:
        kw["vmem_limit_bytes"] = int(min(VMEM_CAP, vmem))
    return pltpu.CompilerParams(**kw)


def _nbytes(shape, dtype):
    return math.prod(shape) * jnp.dtype(dtype).itemsize


def _pick(n, target):
    best = None
    for d in range(LANES, min(n, target) + 1, LANES):
        if n % d == 0:
            best = d
    return n if best is None else best


def matmul(a, b, *, tb=False, out_dtype, res=None, tm=512, tn=512, tk=None, name):
    M, K = a.shape
    N = b.shape[0] if tb else b.shape[1]
    assert (b.shape[1] if tb else b.shape[0]) == K
    tm, tn = _pick(M, tm), _pick(N, tn)
    tk = K if tk is None else _pick(K, tk)
    assert M % tm == 0 and N % tn == 0 and K % tk == 0, (name, M, N, K, tm, tn, tk)
    nk = K // tk
    dn = (((1,), (1 if tb else 0,)), ((), ()))

    def body(*refs):
        if res is not None:
            a_ref, b_ref, r_ref, o_ref = refs[:4]
        else:
            a_ref, b_ref, o_ref = refs[:3]
        p = lax.dot_general(a_ref[...].astype(BF16), b_ref[...].astype(BF16), dn,
                            preferred_element_type=F32)

        def finish(acc):
            if res is not None:
                acc = acc + r_ref[...].astype(F32)
            o_ref[...] = acc.astype(out_dtype)

        if nk == 1:
            finish(p)
        else:
            acc_ref = refs[-1]
            k = pl.program_id(2)

            @pl.when(k == 0)
            def _():
                acc_ref[...] = p

            @pl.when(k > 0)
            def _():
                acc_ref[...] += p

            @pl.when(k == nk - 1)
            def _():
                finish(acc_ref[...])

    a_spec = pl.BlockSpec((tm, tk), lambda i, j, k: (i, k))
    b_spec = pl.BlockSpec((tn, tk), lambda i, j, k: (j, k)) if tb else pl.BlockSpec((tk, tn), lambda i, j, k: (k, j))
    o_spec = pl.BlockSpec((tm, tn), lambda i, j, k: (i, j))
    in_specs, args = [a_spec, b_spec], [a, b]
    est = 2 * (_nbytes((tm, tk), a.dtype) + _nbytes((tk, tn), b.dtype) + _nbytes((tm, tn), out_dtype))
    est += (a.dtype != BF16) * _nbytes((tm, tk), BF16) + (b.dtype != BF16) * _nbytes((tk, tn), BF16)
    est += 2 * _nbytes((tm, tn), F32)
    if res is not None:
        in_specs.append(o_spec)
        args.append(res)
        est += 2 * _nbytes((tm, tn), res.dtype)
    return _call(
        body, name=name, grid=(M // tm, N // tn, nk),
        in_specs=in_specs, out_specs=o_spec,
        out_shape=jax.ShapeDtypeStruct((M, N), out_dtype),
        scratch_shapes=[] if nk == 1 else [pltpu.VMEM((tm, tn), F32)],
        compiler_params=_params(("parallel", "parallel", "arbitrary"), est + (8 << 20)),
    )(*args)


def _rms_scale(x):
    return lax.rsqrt(jnp.mean(x * x, axis=-1, keepdims=True) + EPS)


def rmsnorm_fwd(x, g, *, name, tm=512):
    T, D = x.shape
    tm = min(tm, T)

    def body(x_ref, g_ref, o_ref):
        xv = x_ref[...]
        o_ref[...] = (xv * _rms_scale(xv) * g_ref[...]).astype(BF16)

    return _call(
        body, name=name, grid=(T // tm,),
        in_specs=[pl.BlockSpec((tm, D), lambda i: (i, 0)), pl.BlockSpec((1, D), lambda i: (0, 0))],
        out_specs=pl.BlockSpec((tm, D), lambda i: (i, 0)),
        out_shape=jax.ShapeDtypeStruct((T, D), BF16),
        compiler_params=_params(("parallel",)),
    )(x, g)


def _rms_bwd_math(xv, gv, dh):
    r = _rms_scale(xv)
    xh = xv * r
    dg = jnp.sum(dh * xh, axis=0, keepdims=True)
    dxh = dh * gv
    dx = r * (dxh - xh * jnp.mean(dxh * xh, axis=-1, keepdims=True))
    return dx, dg


def rmsnorm_bwd(x, g, dh, dres, *, name, tm=256):
    T, D = x.shape
    tm = min(tm, T)

    def body(*refs):
        if dres is not None:
            x_ref, g_ref, dh_ref, dr_ref, dx_ref, dg_ref = refs
        else:
            x_ref, g_ref, dh_ref, dx_ref, dg_ref = refs
        dx, dg = _rms_bwd_math(x_ref[...], g_ref[...], dh_ref[...].astype(F32))
        if dres is not None:
            dx = dx + dr_ref[...]
        dx_ref[...] = dx

        @pl.when(pl.program_id(0) == 0)
        def _():
            dg_ref[...] = jnp.zeros_like(dg_ref)

        dg_ref[...] += dg

    row = pl.BlockSpec((tm, D), lambda i: (i, 0))
    vec = pl.BlockSpec((1, D), lambda i: (0, 0))
    ins, args = [row, vec, row], [x, g, dh]
    if dres is not None:
        ins.append(row)
        args.append(dres)
    return _call(
        body, name=name, grid=(T // tm,), in_specs=ins, out_specs=[row, vec],
        out_shape=[jax.ShapeDtypeStruct((T, D), F32), jax.ShapeDtypeStruct((1, D), F32)],
        compiler_params=_params(("arbitrary",)),
    )(*args)


def final_loss_bwd(x, g, target, *, name, tm=256):
    T, D = x.shape
    tm = min(tm, T)

    def body(x_ref, g_ref, t_ref, dx_ref, dg_ref, l_ref):
        xv, gv = x_ref[...], g_ref[...]
        e = xv * _rms_scale(xv) * gv - t_ref[...]
        part = 0.5 * jnp.sum(jnp.mean(e * e, axis=-1, keepdims=True), axis=0, keepdims=True)
        dx, dg = _rms_bwd_math(xv, gv, e * (1.0 / D))
        dx_ref[...] = dx

        @pl.when(pl.program_id(0) == 0)
        def _():
            dg_ref[...] = jnp.zeros_like(dg_ref)
            l_ref[...] = jnp.zeros_like(l_ref)

        dg_ref[...] += dg
        l_ref[...] += jnp.broadcast_to(part, l_ref.shape)

    row = pl.BlockSpec((tm, D), lambda i: (i, 0))
    vec = pl.BlockSpec((1, D), lambda i: (0, 0))
    return _call(
        body, name=name, grid=(T // tm,), in_specs=[row, vec, row],
        out_specs=[row, vec, pl.BlockSpec((1, LANES), lambda i: (0, 0))],
        out_shape=[jax.ShapeDtypeStruct((T, D), F32), jax.ShapeDtypeStruct((1, D), F32),
                   jax.ShapeDtypeStruct((1, LANES), F32)],
        compiler_params=_params(("arbitrary",)),
    )(x, g, target)


def _sigmoid(v):
    return 1.0 / (1.0 + jnp.exp(-v))


def _glu(blk):
    u = blk[:, :CONV_CH].astype(F32)
    gt = blk[:, CONV_CH:].astype(F32)
    return u * _sigmoid(gt)


def _fill_causal_ext(ext, cur_ref, halo_ref, s, ts):
    ext[pl.ds(HALO, ts), :] = _glu(cur_ref[0])
    hal = _glu(halo_ref[0])
    ext[pl.ds(0, HALO), :] = jnp.where(s > 0, hal, 0.0)


def _causal_conv(ext, w_ref, ts):
    acc = jnp.zeros((ts, CONV_CH), F32)
    for j in range(CONV_K):
        acc = acc + ext[pl.ds(HALO - (CONV_K - 1) + j, ts), :] * w_ref[pl.ds(j, 1), :]
    return acc


def _ln_stats(y):
    mu = jnp.mean(y, axis=-1, keepdims=True)
    yc = y - mu
    rstd = lax.rsqrt(jnp.mean(yc * yc, axis=-1, keepdims=True) + EPS)
    return yc * rstd, rstd


def _conv_specs(ts, S):
    nh = ts // HALO
    cur = pl.BlockSpec((1, ts, 2 * CONV_CH), lambda b, s: (b, s, 0))
    halo = pl.BlockSpec((1, HALO, 2 * CONV_CH), lambda b, s: (b, jnp.maximum(s * nh - 1, 0), 0))
    w = pl.BlockSpec((HALO, CONV_CH), lambda b, s: (0, 0))
    vec = pl.BlockSpec((1, CONV_CH), lambda b, s: (0, 0))
    return cur, halo, w, vec


def conv_branch_fwd(ug, conv_w, conv_b, ln_g, ln_b, *, name, ts=256):
    B, S, _ = ug.shape
    ts = min(ts, S)
    cur, halo, w, vec = _conv_specs(ts, S)

    def body(cur_ref, halo_ref, w_ref, cb_ref, lg_ref, lb_ref, o_ref, ext):
        _fill_causal_ext(ext, cur_ref, halo_ref, pl.program_id(1), ts)
        y = _causal_conv(ext, w_ref, ts) + cb_ref[...]
        yh, _ = _ln_stats(y)
        ln = yh * lg_ref[...] + lb_ref[...]
        o_ref[0] = (ln * _sigmoid(ln)).astype(BF16)

    return _call(
        body, name=name, grid=(B, S // ts), in_specs=[cur, halo, w, vec, vec, vec],
        out_specs=pl.BlockSpec((1, ts, CONV_CH), lambda b, s: (b, s, 0)),
        out_shape=jax.ShapeDtypeStruct((B, S, CONV_CH), BF16),
        scratch_shapes=[pltpu.VMEM((ts + HALO, CONV_CH), F32)],
        compiler_params=_params(("parallel", "parallel")),
    )(ug, ug, conv_w, conv_b, ln_g, ln_b)


def conv_branch_bwd_a(ug, dcat, conv_w, conv_b, ln_g, ln_b, *, name, ts=256):
    B, S, _ = ug.shape
    ts = min(ts, S)
    cur, halo, w, vec = _conv_specs(ts, S)

    def body(cur_ref, halo_ref, d_ref, w_ref, cb_ref, lg_ref, lb_ref, dy_ref, dw_ref, dv_ref, ext):
        _fill_causal_ext(ext, cur_ref, halo_ref, pl.program_id(1), ts)
        y = _causal_conv(ext, w_ref, ts) + cb_ref[...]
        yh, rstd = _ln_stats(y)
        lg = lg_ref[...]
        ln = yh * lg + lb_ref[...]
        sg = _sigmoid(ln)
        dln = d_ref[0].astype(F32) * (sg * (1.0 + ln * (1.0 - sg)))
        dyh = dln * lg
        dy = rstd * (dyh - jnp.mean(dyh, axis=-1, keepdims=True)
                     - yh * jnp.mean(dyh * yh, axis=-1, keepdims=True))
        dy_ref[0] = dy

        @pl.when((pl.program_id(0) == 0) & (pl.program_id(1) == 0))
        def _():
            dw_ref[...] = jnp.zeros_like(dw_ref)
            dv_ref[...] = jnp.zeros_like(dv_ref)

        dv_ref[pl.ds(0, 1), :] += jnp.sum(dy, axis=0, keepdims=True)
        dv_ref[pl.ds(1, 1), :] += jnp.sum(dln * yh, axis=0, keepdims=True)
        dv_ref[pl.ds(2, 1), :] += jnp.sum(dln, axis=0, keepdims=True)
        for j in range(CONV_K):
            tap = ext[pl.ds(HALO - (CONV_K - 1) + j, ts), :]
            dw_ref[pl.ds(j, 1), :] += jnp.sum(dy * tap, axis=0, keepdims=True)

    return _call(
        body, name=name, grid=(B, S // ts),
        in_specs=[cur, halo, pl.BlockSpec((1, ts, CONV_CH), lambda b, s: (b, s, 0)), w, vec, vec, vec],
        out_specs=[pl.BlockSpec((1, ts, CONV_CH), lambda b, s: (b, s, 0)),
                   pl.BlockSpec((HALO, CONV_CH), lambda b, s: (0, 0)),
                   pl.BlockSpec((8, CONV_CH), lambda b, s: (0, 0))],
        out_shape=[jax.ShapeDtypeStruct((B, S, CONV_CH), F32),
                   jax.ShapeDtypeStruct((HALO, CONV_CH), F32),
                   jax.ShapeDtypeStruct((8, CONV_CH), F32)],
        scratch_shapes=[pltpu.VMEM((ts + HALO, CONV_CH), F32)],
        compiler_params=_params(("arbitrary", "arbitrary")),
    )(ug, ug, dcat, conv_w, conv_b, ln_g, ln_b)


def conv_branch_bwd_b(ug, dy, conv_w, *, name, ts=256):
    B, S, _ = ug.shape
    ts = min(ts, S)
    nh, n_halo = ts // HALO, S // HALO

    def body(cur_ref, dy_ref, nxt_ref, w_ref, o_ref, ext):
        last = pl.program_id(1) == pl.num_programs(1) - 1
        ext[pl.ds(0, ts), :] = dy_ref[0]
        ext[pl.ds(ts, HALO), :] = jnp.where(last, 0.0, nxt_ref[0])
        da = jnp.zeros((ts, CONV_CH), F32)
        for j in range(CONV_K):
            da = da + ext[pl.ds(CONV_K - 1 - j, ts), :] * w_ref[pl.ds(j, 1), :]
        blk = cur_ref[0]
        u = blk[:, :CONV_CH].astype(F32)
        sg = _sigmoid(blk[:, CONV_CH:].astype(F32))
        o_ref[0, :, :CONV_CH] = (da * sg).astype(BF16)
        o_ref[0, :, CONV_CH:] = (da * u * sg * (1.0 - sg)).astype(BF16)

    return _call(
        body, name=name, grid=(B, S // ts),
        in_specs=[pl.BlockSpec((1, ts, 2 * CONV_CH), lambda b, s: (b, s, 0)),
                  pl.BlockSpec((1, ts, CONV_CH), lambda b, s: (b, s, 0)),
                  pl.BlockSpec((1, HALO, CONV_CH), lambda b, s: (b, jnp.minimum((s + 1) * nh, n_halo - 1), 0)),
                  pl.BlockSpec((HALO, CONV_CH), lambda b, s: (0, 0))],
        out_specs=pl.BlockSpec((1, ts, 2 * CONV_CH), lambda b, s: (b, s, 0)),
        out_shape=jax.ShapeDtypeStruct((B, S, 2 * CONV_CH), BF16),
        scratch_shapes=[pltpu.VMEM((ts + HALO, CONV_CH), F32)],
        compiler_params=_params(("parallel", "parallel")),
    )(ug, dy, dy, conv_w)


def _tri(n, lower):
    r = lax.broadcasted_iota(jnp.int32, (n, n), 0)
    c = lax.broadcasted_iota(jnp.int32, (n, n), 1)
    return ((r >= c) if lower else (r <= c)).astype(F32)


def _eye(n):
    r = lax.broadcasted_iota(jnp.int32, (n, n), 0)
    c = lax.broadcasted_iota(jnp.int32, (n, n), 1)
    return (r == c).astype(F32)


def _dot_hi(a, b, dn):
    return lax.dot_general(a, b, dn, precision=lax.Precision.HIGHEST, preferred_element_type=F32)


NN = (((1,), (0,)), ((), ()))
NT = (((1,), (1,)), ((), ()))
TN = (((0,), (0,)), ((), ()))


def _log_sigmoid(v):
    e = jnp.exp(-jnp.abs(v))
    log1p_e = jnp.where(e < 1e-3, e * (1.0 - 0.5 * e), jnp.log(1.0 + e))
    return jnp.minimum(v, 0.0) - log1p_e


def fgate_fwd(h, w_f, b_f, *, name, ts=256):
    B, S, D = h.shape
    ts = min(ts, S)

    def body(h_ref, w_ref, b_ref, f_ref, cc_ref, cr_ref, carry):
        @pl.when(pl.program_id(1) == 0)
        def _():
            carry[...] = jnp.zeros_like(carry)

        f = jnp.dot(h_ref[0], w_ref[...], preferred_element_type=F32)
        f_ref[0] = f
        logf = _log_sigmoid(f + b_ref[...])
        c = _dot_hi(_tri(ts, True), logf, NN) + carry[pl.ds(0, 1), :]
        cc_ref[0] = c
        carry[pl.ds(0, 1), :] = c[ts - 1:ts, :]
        cr_ref[0] = _dot_hi(_eye(LANES), c, NT)

    return _call(
        body, name=name, grid=(B, S // ts),
        in_specs=[pl.BlockSpec((1, ts, D), lambda b, s: (b, s, 0)),
                  pl.BlockSpec((D, LANES), lambda b, s: (0, 0)),
                  pl.BlockSpec((1, LANES), lambda b, s: (0, 0))],
        out_specs=[pl.BlockSpec((1, ts, LANES), lambda b, s: (b, s, 0)),
                   pl.BlockSpec((1, ts, LANES), lambda b, s: (b, s, 0)),
                   pl.BlockSpec((1, LANES, ts), lambda b, s: (b, 0, s))],
        out_shape=[jax.ShapeDtypeStruct((B, S, LANES), F32), jax.ShapeDtypeStruct((B, S, LANES), F32),
                   jax.ShapeDtypeStruct((B, LANES, S), F32)],
        scratch_shapes=[pltpu.VMEM((8, LANES), F32)],
        compiler_params=_params(("parallel", "arbitrary")),
    )(h, w_f, b_f)


def fgate_bwd(dc, f, b_f, *, name, ts=256):
    B, S, _ = f.shape
    P = dc.shape[1]
    ts = min(ts, S)
    ns = S // ts

    def body(dc_ref, f_ref, b_ref, df_ref, db_ref, carry):
        @pl.when(pl.program_id(1) == 0)
        def _():
            carry[...] = jnp.zeros_like(carry)

        @pl.when((pl.program_id(0) == 0) & (pl.program_id(1) == 0))
        def _():
            db_ref[...] = jnp.zeros_like(db_ref)

        dc_t = dc_ref[0, 0]
        for j in range(1, P):
            dc_t = dc_t + dc_ref[0, j]
        dlogf = _dot_hi(_tri(ts, False), dc_t, NN) + carry[pl.ds(0, 1), :]
        carry[pl.ds(0, 1), :] = dlogf[0:1, :]
        df = dlogf * _sigmoid(-(f_ref[0] + b_ref[...]))
        df_ref[0] = df.astype(BF16)
        db_ref[...] += jnp.sum(df, axis=0, keepdims=True)

    return _call(
        body, name=name, grid=(B, ns),
        in_specs=[pl.BlockSpec((1, P, ts, LANES), lambda b, s: (b, 0, ns - 1 - s, 0)),
                  pl.BlockSpec((1, ts, LANES), lambda b, s: (b, ns - 1 - s, 0)),
                  pl.BlockSpec((1, LANES), lambda b, s: (0, 0))],
        out_specs=[pl.BlockSpec((1, ts, LANES), lambda b, s: (b, ns - 1 - s, 0)),
                   pl.BlockSpec((1, LANES), lambda b, s: (0, 0))],
        out_shape=[jax.ShapeDtypeStruct((B, S, LANES), BF16), jax.ShapeDtypeStruct((1, LANES), F32)],
        scratch_shapes=[pltpu.VMEM((8, LANES), F32)],
        compiler_params=_params(("arbitrary", "arbitrary")),
    )(dc, f, b_f)


def _lane_pick(tile, idx):
    lane = lax.broadcasted_iota(jnp.int32, tile.shape, 1)
    return jnp.sum(jnp.where(lane == idx, tile, 0.0), axis=-1, keepdims=True)


FOX_T = 256


def _fox_heads(q, cc_ref, p):
    lane = lax.broadcasted_iota(jnp.int32, q.shape, 1)
    qs = q * (1.0 / math.sqrt(FOX_HEAD_DIM))
    qhs = [jnp.where((lane < FOX_HEAD_DIM) == (hh == 0), qs, jnp.zeros_like(qs)) for hh in range(2)]
    crefs = [_lane_pick(cc_ref[0, pl.ds(0, 1), :], 2 * p + hh) for hh in range(2)]
    return qhs, crefs


def _causal(t, transposed):
    r = lax.broadcasted_iota(jnp.int32, (t, t), 0)
    c = lax.broadcasted_iota(jnp.int32, (t, t), 1)
    return (r <= c) if transposed else (c <= r)


def fox_fwd(qkv, c_col, c_row, *, name, rider=None):
    B, S, _ = qkv.shape
    assert S % FOX_T == 0
    tq, nq = FOX_T, S // FOX_T
    npair = FOX_HEADS // 2

    def body(q_ref, k_ref, v_ref, cc_ref, cr_ref, o_ref, l_ref, m_scr, l_scr, acc_scr):
        p, qi = pl.program_id(1), pl.program_id(2)
        qhs, crefs = _fox_heads(q_ref[0], cc_ref, p)
        for hh in range(2):
            m_scr[hh] = jnp.full((tq, 1), NEG, F32)
            l_scr[hh] = jnp.zeros((tq, 1), F32)
            acc_scr[hh] = jnp.zeros((tq, LANES), F32)

        def tile(kb, diagonal):
            k0 = pl.multiple_of(kb * tq, tq)
            k = k_ref[0, pl.ds(k0, tq), :]
            v = v_ref[0, pl.ds(k0, tq), :]
            for hh in range(2):
                s = lax.dot_general(qhs[hh], k, NT, preferred_element_type=F32)
                s = s + (crefs[hh] - cr_ref[0, pl.ds(2 * p + hh, 1), pl.ds(k0, tq)])
                if diagonal:
                    s = jnp.where(_causal(tq, False), s, NEG)
                m_old = m_scr[hh]
                m_new = jnp.maximum(m_old, jnp.max(s, axis=-1, keepdims=True))
                alpha = jnp.exp(m_old - m_new)
                pr = jnp.exp(s - m_new)
                m_scr[hh] = m_new
                l_scr[hh] = alpha * l_scr[hh] + jnp.sum(pr, axis=-1, keepdims=True)
                acc_scr[hh] = alpha * acc_scr[hh] + jnp.dot(pr.astype(BF16), v, preferred_element_type=F32)

        def off_diagonal(kb, carry):
            tile(kb, False)
            return carry

        lax.fori_loop(0, qi, off_diagonal, 0)
        tile(qi, True)
        first = lax.broadcasted_iota(jnp.int32, (tq, LANES), 1) < FOX_HEAD_DIM
        o_ref[0] = jnp.where(first, acc_scr[0] / l_scr[0], acc_scr[1] / l_scr[1]).astype(BF16)
        l_ref[0, 0] = jnp.where(first, m_scr[0] + jnp.log(l_scr[0]), m_scr[1] + jnp.log(l_scr[1]))

    return hosted_call(
        body, rider, name=name, grid=(B, npair, nq),
        in_specs=[pl.BlockSpec((1, tq, LANES), lambda b, p, i: (b, i, p)),
                  pl.BlockSpec((1, S, LANES), lambda b, p, i: (b, 0, npair + p)),
                  pl.BlockSpec((1, S, LANES), lambda b, p, i: (b, 0, 2 * npair + p)),
                  pl.BlockSpec((1, tq, LANES), lambda b, p, i: (b, i, 0)),
                  pl.BlockSpec((1, 8, S), lambda b, p, i: (b, 0, 0))],
        out_specs=[pl.BlockSpec((1, tq, LANES), lambda b, p, i: (b, i, p)),
                   pl.BlockSpec((1, 1, tq, LANES), lambda b, p, i: (b, p, i, 0))],
        out_shape=[jax.ShapeDtypeStruct((B, S, FOX_W), BF16),
                   jax.ShapeDtypeStruct((B, npair, S, LANES), F32)],
        scratch_shapes=[pltpu.VMEM((2, tq, 1), F32), pltpu.VMEM((2, tq, 1), F32),
                        pltpu.VMEM((2, tq, LANES), F32)],
        args=(qkv, qkv, qkv, c_col, c_row),
    )


def fox_bwd_dq(qkv, dcat, lse, c_col, c_row, *, name, rider=None):
    B, S, _ = qkv.shape
    tq, nq = FOX_T, S // FOX_T
    npair = FOX_HEADS // 2

    def body(q_ref, k_ref, v_ref, do_ref, l_ref, cc_ref, cr_ref, dq_ref, st_ref, p_scr, dp_scr, dl_scr):
        p, qi = pl.program_id(1), pl.program_id(2)
        qhs, crefs = _fox_heads(q_ref[0], cc_ref, p)
        lane = lax.broadcasted_iota(jnp.int32, (tq, LANES), 1)
        do_b = do_ref[0].astype(BF16)
        dohs = [jnp.where((lane < FOX_HEAD_DIM) == (hh == 0), do_b, jnp.zeros_like(do_b)) for hh in range(2)]
        lses = [_lane_pick(l_ref[0, 0], hh * FOX_HEAD_DIM) for hh in range(2)]
        for hh in range(2):
            dl_scr[hh] = jnp.zeros((tq, 1), F32)

        def probs(kb, diagonal):
            k0 = pl.multiple_of(kb * tq, tq)
            k = k_ref[0, pl.ds(k0, tq), :]
            v = v_ref[0, pl.ds(k0, tq), :]
            for hh in range(2):
                s = lax.dot_general(qhs[hh], k, NT, preferred_element_type=F32)
                s = s + (crefs[hh] - cr_ref[0, pl.ds(2 * p + hh, 1), pl.ds(k0, tq)])
                pr = jnp.exp(s - lses[hh])
                if diagonal:
                    pr = jnp.where(_causal(tq, False), pr, 0.0)
                dp = lax.dot_general(dohs[hh], v, NT, preferred_element_type=F32)
                dl_scr[hh] += jnp.sum(pr * dp, axis=-1, keepdims=True)
                p_scr[hh, kb] = pr
                dp_scr[hh, kb] = dp

        def first_pass(kb, carry):
            probs(kb, False)
            return carry

        lax.fori_loop(0, qi, first_pass, 0)
        probs(qi, True)

        def second_pass(kb, dq):
            k0 = pl.multiple_of(kb * tq, tq)
            k = k_ref[0, pl.ds(k0, tq), :]
            for hh in range(2):
                ds = p_scr[hh, kb] * (dp_scr[hh, kb] - dl_scr[hh])
                kh = jnp.where((lane < FOX_HEAD_DIM) == (hh == 0), k, jnp.zeros_like(k))
                dq = dq + jnp.dot(ds.astype(BF16), kh, preferred_element_type=F32)
            return dq

        dq = lax.fori_loop(0, qi + 1, second_pass, jnp.zeros((tq, LANES), F32))
        dq_ref[0] = (dq * (1.0 / math.sqrt(FOX_HEAD_DIM))).astype(BF16)
        cols = jnp.zeros((tq, LANES), F32)
        for j, col in enumerate([crefs[0] - lses[0], crefs[1] - lses[1], dl_scr[0], dl_scr[1]]):
            cols = jnp.where(lane == j, col, cols)
        st_ref[0, 0] = _dot_hi(_eye(LANES), cols, NT)[:8]

    return hosted_call(
        body, rider, name=name, grid=(B, npair, nq),
        in_specs=[pl.BlockSpec((1, tq, LANES), lambda b, p, i: (b, i, p)),
                  pl.BlockSpec((1, S, LANES), lambda b, p, i: (b, 0, npair + p)),
                  pl.BlockSpec((1, S, LANES), lambda b, p, i: (b, 0, 2 * npair + p)),
                  pl.BlockSpec((1, tq, LANES), lambda b, p, i: (b, i, npair + p)),
                  pl.BlockSpec((1, 1, tq, LANES), lambda b, p, i: (b, p, i, 0)),
                  pl.BlockSpec((1, tq, LANES), lambda b, p, i: (b, i, 0)),
                  pl.BlockSpec((1, 8, S), lambda b, p, i: (b, 0, 0))],
        out_specs=[pl.BlockSpec((1, tq, LANES), lambda b, p, i: (b, i, p)),
                   pl.BlockSpec((1, 1, 8, tq), lambda b, p, i: (b, p, 0, i))],
        out_shape=[jax.ShapeDtypeStruct((B, S, FOX_W), BF16), jax.ShapeDtypeStruct((B, npair, 8, S), F32)],
        scratch_shapes=[pltpu.VMEM((2, nq, tq, tq), F32), pltpu.VMEM((2, nq, tq, tq), F32),
                        pltpu.VMEM((2, tq, 1), F32)],
        args=(qkv, qkv, qkv, dcat, lse, c_col, c_row), vmem=40 << 20,
    )


def fox_bwd_dkdv(qkv, dcat, stats, c_col, *, name, rider=None):
    B, S, _ = qkv.shape
    tk, nq = FOX_T, S // FOX_T
    npair = FOX_HEADS // 2
    inv = 1.0 / math.sqrt(FOX_HEAD_DIM)

    def body(q_ref, k_ref, v_ref, do_ref, st_ref, cc_ref, dk_ref, dv_ref, dc_ref, dk_scr, dv_scr, dc_scr):
        p, kt = pl.program_id(1), pl.program_id(2)
        lane = lax.broadcasted_iota(jnp.int32, (tk, LANES), 1)
        masks = [(lane < FOX_HEAD_DIM) == (hh == 0) for hh in range(2)]
        k = k_ref[0]
        v = v_ref[0]
        khs = [jnp.where(masks[hh], k, jnp.zeros_like(k)) for hh in range(2)]
        vhs = [jnp.where(masks[hh], v, jnp.zeros_like(v)) for hh in range(2)]
        ccols = [_lane_pick(cc_ref[0], 2 * p + hh) for hh in range(2)]
        dk_scr[...] = jnp.zeros_like(dk_scr)
        dv_scr[...] = jnp.zeros_like(dv_scr)
        dc_scr[...] = jnp.zeros_like(dc_scr)

        def tile(qb, diagonal):
            q0 = pl.multiple_of(qb * tk, tk)
            qs = q_ref[0, pl.ds(q0, tk), :] * inv
            do_b = do_ref[0, pl.ds(q0, tk), :].astype(BF16)
            for hh in range(2):
                st = lax.dot_general(khs[hh], qs, NT, preferred_element_type=F32)
                pr = jnp.exp(st - ccols[hh] + st_ref[0, 0, pl.ds(hh, 1), pl.ds(q0, tk)])
                if diagonal:
                    pr = jnp.where(_causal(tk, True), pr, 0.0)
                dp = lax.dot_general(vhs[hh], do_b, NT, preferred_element_type=F32)
                ds = pr * (dp - st_ref[0, 0, pl.ds(2 + hh, 1), pl.ds(q0, tk)])
                dv_scr[...] += jnp.dot(pr.astype(BF16), jnp.where(masks[hh], do_b, jnp.zeros_like(do_b)),
                                       preferred_element_type=F32)
                dk_scr[...] += jnp.dot(ds.astype(BF16), jnp.where(masks[hh], qs, jnp.zeros_like(qs)),
                                       preferred_element_type=F32)
                dc_scr[hh] -= jnp.sum(ds, axis=-1, keepdims=True)

        def later(qb, carry):
            tile(qb, False)
            return carry

        tile(kt, True)
        lax.fori_loop(kt + 1, nq, later, 0)
        dk_ref[0] = dk_scr[...].astype(BF16)
        dv_ref[0] = dv_scr[...].astype(BF16)
        dc_ref[0, 0] = jnp.where(lane == 2 * p, dc_scr[0], jnp.where(lane == 2 * p + 1, dc_scr[1], 0.0))

    full = lambda col: pl.BlockSpec((1, S, LANES), col)
    tile_spec = lambda col: pl.BlockSpec((1, tk, LANES), col)
    return hosted_call(
        body, rider, name=name, grid=(B, npair, nq),
        in_specs=[full(lambda b, p, t: (b, 0, p)),
                  tile_spec(lambda b, p, t: (b, t, npair + p)),
                  tile_spec(lambda b, p, t: (b, t, 2 * npair + p)),
                  full(lambda b, p, t: (b, 0, npair + p)),
                  pl.BlockSpec((1, 1, 8, S), lambda b, p, t: (b, p, 0, 0)),
                  tile_spec(lambda b, p, t: (b, t, 0))],
        out_specs=[tile_spec(lambda b, p, t: (b, t, p)), tile_spec(lambda b, p, t: (b, t, p)),
                   pl.BlockSpec((1, 1, tk, LANES), lambda b, p, t: (b, p, t, 0))],
        out_shape=[jax.ShapeDtypeStruct((B, S, FOX_W), BF16)] * 2
        + [jax.ShapeDtypeStruct((B, npair, S, LANES), F32)],
        scratch_shapes=[pltpu.VMEM((tk, LANES), F32), pltpu.VMEM((tk, LANES), F32),
                        pltpu.VMEM((2, tk, 1), F32)],
        args=(qkv, qkv, qkv, dcat, stats, c_col),
    )


def xattn_fwd(qm, kv, *, name, tq=256):
    B, S, D = qm.shape
    M = kv.shape[1]
    tq = min(tq, S)
    inv = 1.0 / math.sqrt(MEM_HEAD_DIM)

    def body(q_ref, kv_ref, o_ref):
        for h in range(MEM_HEADS):
            c0 = h * MEM_HEAD_DIM
            qh = q_ref[0, :, c0:c0 + MEM_HEAD_DIM]
            kh = kv_ref[0, :, c0:c0 + MEM_HEAD_DIM]
            vh = kv_ref[0, :, D + c0:D + c0 + MEM_HEAD_DIM]
            s = lax.dot_general(qh, kh, NT, preferred_element_type=F32) * inv
            e = jnp.exp(s - jnp.max(s, axis=-1, keepdims=True))
            o = jnp.dot(e.astype(BF16), vh, preferred_element_type=F32) / jnp.sum(e, axis=-1, keepdims=True)
            o_ref[0, :, c0:c0 + MEM_HEAD_DIM] = o.astype(BF16)

    return _call(
        body, name=name, grid=(B, S // tq),
        in_specs=[pl.BlockSpec((1, tq, D), lambda b, i: (b, i, 0)),
                  pl.BlockSpec((1, M, 2 * D), lambda b, i: (b, 0, 0))],
        out_specs=pl.BlockSpec((1, tq, D), lambda b, i: (b, i, 0)),
        out_shape=jax.ShapeDtypeStruct((B, S, D), BF16),
        compiler_params=_params(("parallel", "parallel")),
    )(qm, kv)


def xattn_bwd(qm, kv, do, *, name, tq=256):
    B, S, D = qm.shape
    M = kv.shape[1]
    tq = min(tq, S)
    inv = 1.0 / math.sqrt(MEM_HEAD_DIM)

    def body(q_ref, kv_ref, do_ref, dq_ref, dkv_ref):
        @pl.when(pl.program_id(1) == 0)
        def _():
            dkv_ref[...] = jnp.zeros_like(dkv_ref)

        for h in range(MEM_HEADS):
            c0 = h * MEM_HEAD_DIM
            qh = q_ref[0, :, c0:c0 + MEM_HEAD_DIM]
            kh = kv_ref[0, :, c0:c0 + MEM_HEAD_DIM]
            vh = kv_ref[0, :, D + c0:D + c0 + MEM_HEAD_DIM]
            doh = do_ref[0, :, c0:c0 + MEM_HEAD_DIM]
            s = lax.dot_general(qh, kh, NT, preferred_element_type=F32) * inv
            e = jnp.exp(s - jnp.max(s, axis=-1, keepdims=True))
            pr = e / jnp.sum(e, axis=-1, keepdims=True)
            dp = lax.dot_general(doh, vh, NT, preferred_element_type=F32)
            ds = pr * (dp - jnp.sum(pr * dp, axis=-1, keepdims=True))
            ds_b = ds.astype(BF16)
            dq_ref[0, :, c0:c0 + MEM_HEAD_DIM] = (jnp.dot(ds_b, kh, preferred_element_type=F32) * inv).astype(BF16)
            dkv_ref[0, :, c0:c0 + MEM_HEAD_DIM] += lax.dot_general(ds_b, qh, TN, preferred_element_type=F32) * inv
            dkv_ref[0, :, D + c0:D + c0 + MEM_HEAD_DIM] += lax.dot_general(
                pr.astype(BF16), doh, TN, preferred_element_type=F32)

    row = pl.BlockSpec((1, tq, D), lambda b, i: (b, i, 0))
    kvs = pl.BlockSpec((1, M, 2 * D), lambda b, i: (b, 0, 0))
    return _call(
        body, name=name, grid=(B, S // tq), in_specs=[row, kvs, row], out_specs=[row, kvs],
        out_shape=[jax.ShapeDtypeStruct((B, S, D), BF16), jax.ShapeDtypeStruct((B, M, 2 * D), F32)],
        compiler_params=_params(("parallel", "arbitrary")),
    )(qm, kv, do)


def swiglu_fwd(gu, *, name, tm=256):
    T, F2 = gu.shape
    Fh = F2 // 2
    tm = min(tm, T)

    def body(gu_ref, o_ref):
        g = gu_ref[:, :Fh].astype(F32)
        u = gu_ref[:, Fh:].astype(F32)
        o_ref[...] = (g * _sigmoid(g) * u).astype(BF16)

    return _call(
        body, name=name, grid=(T // tm,),
        in_specs=[pl.BlockSpec((tm, F2), lambda i: (i, 0))],
        out_specs=pl.BlockSpec((tm, Fh), lambda i: (i, 0)),
        out_shape=jax.ShapeDtypeStruct((T, Fh), BF16),
        compiler_params=_params(("parallel",)),
    )(gu)


def swiglu_bwd(gu, dact, *, name, tm=256):
    T, F2 = gu.shape
    Fh = F2 // 2
    tm = min(tm, T)

    def body(gu_ref, d_ref, o_ref):
        g = gu_ref[:, :Fh].astype(F32)
        u = gu_ref[:, Fh:].astype(F32)
        d = d_ref[...].astype(F32)
        sg = _sigmoid(g)
        o_ref[:, :Fh] = (d * u * (sg * (1.0 + g * (1.0 - sg)))).astype(BF16)
        o_ref[:, Fh:] = (d * g * sg).astype(BF16)

    return _call(
        body, name=name, grid=(T // tm,),
        in_specs=[pl.BlockSpec((tm, F2), lambda i: (i, 0)), pl.BlockSpec((tm, Fh), lambda i: (i, 0))],
        out_specs=pl.BlockSpec((tm, F2), lambda i: (i, 0)),
        out_shape=jax.ShapeDtypeStruct((T, F2), BF16),
        compiler_params=_params(("parallel",)),
    )(gu, dact)


LATE = ("w_out", "w_mq", "w_mkv", "w_mo", "w_gu", "w_down")
RS_GROUPS = (("w_gu", "w_down"), ("w_out", "w_mq", "w_mkv", "w_mo"), ("w_in",))


def reduce_to_chips(names, gw, *, tag):
    g42 = [_shards_from_full(n, gw[n]) for n in names]
    got = sibling_exchange(g42, name="rs_sibling_" + tag)
    return [pair_sum(g, o, name="rs_pair_sum_" + n) for n, g, o in zip(names, g42, got)]


def local_step(x, mem, target, sp, w_in_full, late_shards):
    B, S, D = x.shape
    T = B * S
    M = mem.shape[1]
    row = lambda v: v.reshape(1, -1).astype(F32)
    g_mix, g_x, g_mem, g_ffn, g_final = (row(sp[k]) for k in ("g_mix", "g_x", "g_mem", "g_ffn", "g_final"))
    conv_b, ln_g, ln_b = row(sp["conv_b"]), row(sp["ln_g"]), row(sp["ln_b"])
    conv_w = jnp.pad(sp["conv_w"].astype(F32), ((0, HALO - CONV_K), (0, 0)))
    b_f = jnp.pad(row(sp["b_f"]), ((0, 0), (0, LANES - FOX_HEADS)))
    n_main = 2 * CONV_CH + 3 * FOX_W
    w_main = w_in_full[:, :n_main]
    w_f = jnp.pad(w_in_full[:, n_main:], ((0, 0), (0, LANES - FOX_HEADS)))

    x2d = x.reshape(T, D)
    h = rmsnorm_fwd(x2d, g_mix, name="rms_mix")
    z = matmul(h, w_main, out_dtype=BF16, tn=n_main, name="mm_in")
    z3 = z.reshape(B, S, n_main)
    ug, qkv = z3[:, :, :2 * CONV_CH], z3[:, :, 2 * CONV_CH:]
    conv_out = conv_branch_fwd(ug, conv_w, conv_b, ln_g, ln_b, name="conv_fwd")
    f_raw, c_col, c_row = fgate_fwd(h.reshape(B, S, D), w_f, b_f, name="fgate_fwd")
    (att, lse), partly = fox_fwd(qkv, c_col, c_row, name="fox_fwd", rider=AllGatherStage1(late_shards))
    gathered = all_gather_stage2(partly, name="ag_late_stage2")
    wf = {n: _full_from_gathered(n, blk) for n, blk in zip(LATE, gathered)}
    cat =jnp.concatenate([conv_out, att], axis=-1).reshape(T, D)
    x1 = matmul(cat, wf["w_out"], out_dtype=F32, res=x2d, tn=D, name="mm_out")
    hx = rmsnorm_fwd(x1, g_x, name="rms_x")
    qm = matmul(hx, wf["w_mq"], out_dtype=BF16, tn=D, name="mm_mq")
    mem2d = mem.reshape(B * M, D)
    mem_n = rmsnorm_fwd(mem2d, g_mem, name="rms_mem")
    kv = matmul(mem_n, wf["w_mkv"], out_dtype=BF16, tn=2 * D, name="mm_mkv").reshape(B, M, 2 * D)
    o = xattn_fwd(qm.reshape(B, S, D), kv, name="xattn_fwd").reshape(T, D)
    x2 = matmul(o, wf["w_mo"], out_dtype=F32, res=x1, tn=D, name="mm_mo")
    hf = rmsnorm_fwd(x2, g_ffn, name="rms_ffn")
    gu = matmul(hf, wf["w_gu"], out_dtype=BF16, tn=2816, name="mm_gu")
    act = swiglu_fwd(gu, name="swiglu_fwd")
    x3 = matmul(act, wf["w_down"], out_dtype=F32, res=x2, tn=D, name="mm_down")
    dx3, dg_final, loss = final_loss_bwd(x3, g_final, target.reshape(T, D), name="loss_bwd")
    gw = {}
    gw["w_down"] = matmul(act.T, dx3, out_dtype=BF16, tm=1408, tn=256, name="dw_down")
    dact = matmul(dx3, wf["w_down"], tb=True, out_dtype=BF16, tn=2816, name="dx_down")
    dgu = swiglu_bwd(gu, dact, name="swiglu_bwd")
    gw["w_gu"] = matmul(hf.T, dgu, out_dtype=BF16, tn=1408, name="dw_gu")
    dhf = matmul(dgu, wf["w_gu"], tb=True, out_dtype=BF16, tm=256, tn=D, name="dx_gu")
    dx2, dg_ffn = rmsnorm_bwd(x2, g_ffn, dhf, dx3, name="rms_ffn_bwd")
    gw["w_mo"] = matmul(o.T, dx2, out_dtype=BF16, name="dw_mo")
    do = matmul(dx2, wf["w_mo"], tb=True, out_dtype=BF16, tn=D, name="dx_mo")
    dqm, dkv = xattn_bwd(qm.reshape(B, S, D), kv, do.reshape(B, S, D), name="xattn_bwd")
    dqm = dqm.reshape(T, D)
    dkv = dkv.reshape(B * M, 2 * D)
    gw["w_mq"] = matmul(hx.T, dqm, out_dtype=BF16, tn=D, name="dw_mq")
    dhx = matmul(dqm, wf["w_mq"], tb=True, out_dtype=BF16, tn=D, name="dx_mq")
    gw["w_mkv"] = matmul(mem_n.T, dkv, out_dtype=BF16, tn=D, name="dw_mkv")
    dmem_n = matmul(dkv, wf["w_mkv"], tb=True, out_dtype=BF16, tn=D, name="dx_mkv")
    _, dg_mem = rmsnorm_bwd(mem2d, g_mem, dmem_n, None, name="rms_mem_bwd")
    dx1, dg_x = rmsnorm_bwd(x1, g_x, dhx, dx2, name="rms_x_bwd")
    gw["w_out"] = matmul(cat.T, dx1, out_dtype=BF16, name="dw_out")
    dcat = matmul(dx1, wf["w_out"], tb=True, out_dtype=BF16, tn=D, name="dx_out").reshape(B, S, D)
    dy, dconv_w, dvec = conv_branch_bwd_a(ug, dcat, conv_w, conv_b, ln_g, ln_b, name="conv_bwd_a")
    dug = conv_branch_bwd_b(ug, dy, conv_w, name="conv_bwd_b")
    parts, gots = {}, {}
    for n, p in zip(RS_GROUPS[0], reduce_to_chips(RS_GROUPS[0], gw, tag="ffn")):
        parts[n] = p
    for n, p in zip(RS_GROUPS[1], reduce_to_chips(RS_GROUPS[1], gw, tag="mid")):
        parts[n] = p
    (dq, stats), got = fox_bwd_dq(qkv, dcat, lse, c_col, c_row, name="fox_bwd_dq",
                                  rider=ChipExchange([parts[n] for n in RS_GROUPS[0]]))
    gots.update(zip(RS_GROUPS[0], got))
    (dk, dv, dc), got = fox_bwd_dkdv(qkv, dcat, stats, c_col, name="fox_bwd_dkdv",
                                     rider=ChipExchange([parts[n] for n in RS_GROUPS[1]]))
    gots.update(zip(RS_GROUPS[1], got))
    df, db_f = fgate_bwd(dc, f_raw, b_f, name="fgate_bwd")
    dz = jnp.concatenate([dug, dq, dk, dv], axis=-1).reshape(T, n_main)
    df2 = df.reshape(T, LANES)
    h_t = h.T
    dw_main = matmul(h_t, dz, out_dtype=BF16, tn=1280, name="dw_in")
    dw_f = matmul(h_t, df2, out_dtype=BF16, name="dw_f")
    gw["w_in"] = jnp.concatenate([dw_main, dw_f[:, :FOX_HEADS]], axis=-1)
    dh_f = matmul(df2, w_f, tb=True, out_dtype=F32, tn=D, name="dx_f")
    dh = matmul(dz, w_main, tb=True, out_dtype=F32, res=dh_f, tn=D, name="dx_in")
    parts["w_in"] = reduce_to_chips(RS_GROUPS[2], gw, tag="in")[0]
    gots["w_in"] = chip_exchange([parts["w_in"]], name="rs_chips_in")[0]
    dx, dg_mix = rmsnorm_bwd(x2d, g_mix, dh, dx1, name="rms_mix_bwd")
    gs = dict(g_mix=dg_mix, b_f=db_f[:, :FOX_HEADS], conv_w=dconv_w[:CONV_K], conv_b=dvec[0:1],
              ln_g=dvec[1:2], ln_b=dvec[2:3], g_x=dg_x, g_mem=dg_mem, g_ffn=dg_ffn, g_final=dg_final)
    return loss, dx.reshape(B, S, D), gs, {n: (parts[n], gots[n]) for n in BIG}


def _me():
    return lax.axis_index("x"), lax.axis_index("y"), lax.axis_index("c")


def _any_specs(n):
    return [pl.BlockSpec(memory_space=pl.ANY)] * n


def all_gather(xs, *, name):
    n = len(xs)

    def body(*refs):
        x_refs, out_refs = refs[:n], refs[n:2 * n]
        send_sems, recv_sems, local_sems = refs[2 * n:]
        x, y, c = _me()
        me, sibling = (x, y, c), (x, y, 1 - c)
        chips = [(1 - x, y), (x, 1 - y), (1 - x, 1 - y)]

        def slot(a, px, py, pc):
            return out_refs[a].at[4 * px + 2 * py + pc]

        def copy(a, k, block, to, own=False):
            return pltpu.make_async_remote_copy(
                src_ref=x_refs[a] if own else slot(a, *block), dst_ref=slot(a, *block),
                send_sem=send_sems.at[k, a], recv_sem=recv_sems.at[k, a], device_id=to, device_id_type=MESH)

        mine = [pltpu.make_async_copy(x_refs[a], slot(a, *me), local_sems.at[a]) for a in range(n)]
        first = [copy(a, 0, me, sibling, own=True) for a in range(n)]
        first += [copy(a, 1 + j, me, (*chip, c), own=True) for j, chip in enumerate(chips) for a in range(n)]
        for cp in mine + first:
            cp.start()
        passed = []
        for j, chip in enumerate(chips):
            for a in range(n):
                copy(a, 1 + j, (*chip, c), me).wait_recv()
                passed.append(copy(a, 4 + j, (*chip, c), sibling))
                passed[-1].start()
        for a in range(n):
            copy(a, 0, sibling, me).wait_recv()
            for j, chip in enumerate(chips):
                copy(a, 4 + j, (*chip, 1 - c), me).wait_recv()
        for cp in first + passed:
            cp.wait_send()
        for cp in mine:
            cp.wait()

    return _call(
        body, name=name, in_specs=_any_specs(n), out_specs=_any_specs(n),
        out_shape=[jax.ShapeDtypeStruct((N_DEV,) + v.shape, v.dtype) for v in xs],
        scratch_shapes=[pltpu.SemaphoreType.DMA((7, n)), pltpu.SemaphoreType.DMA((7, n)),
                        pltpu.SemaphoreType.DMA((n,))],
    )(*xs)


def sibling_exchange(gs, *, name):
    n = len(gs)

    def body(*refs):
        g_refs, out_refs = refs[:n], refs[n:2 * n]
        send_sems, recv_sems = refs[2 * n:]
        x, y, c = _me()
        cps = [pltpu.make_async_remote_copy(
            src_ref=g_refs[a].at[:, 1 - c], dst_ref=out_refs[a], send_sem=send_sems.at[a],
            recv_sem=recv_sems.at[a], device_id=(x, y, 1 - c), device_id_type=MESH) for a in range(n)]
        for cp in cps:
            cp.start()
        for cp in cps:
            cp.wait()

    return _call(
        body, name=name, in_specs=_any_specs(n), out_specs=_any_specs(n),
        out_shape=[jax.ShapeDtypeStruct((4,) + g.shape[2:], g.dtype) for g in gs],
        scratch_shapes=[pltpu.SemaphoreType.DMA((n,)), pltpu.SemaphoreType.DMA((n,))],
    )(*gs)


def chip_exchange(ps, *, name):
    return hosted_call(None, ChipExchange(ps), name=name, grid=(), in_specs=[], out_specs=[], out_shape=[],
                       scratch_shapes=[], args=[])[1]


class ChipExchange:
    def __init__(self, ps):
        n = len(ps)
        self.n, self.inputs = n, list(ps)
        self.out_shape = [jax.ShapeDtypeStruct(p.shape, p.dtype) for p in ps]
        self.scratch = [pltpu.SemaphoreType.DMA((3, n)), pltpu.SemaphoreType.DMA((3, n))]

    def _copies(self, p_refs, out_refs, sems, outgoing):
        send_sems, recv_sems = sems
        x, y, c = _me()
        my_chip = 2 * x + y
        cps = []
        for k in range(3):
            px, py = x ^ ((k + 1) >> 1), y ^ ((k + 1) & 1)
            src, dst = (2 * px + py, my_chip) if outgoing else (my_chip, 2 * px + py)
            for a in range(self.n):
                cps.append(pltpu.make_async_remote_copy(
                    src_ref=p_refs[a].at[src], dst_ref=out_refs[a].at[dst], send_sem=send_sems.at[k, a],
                    recv_sem=recv_sems.at[k, a], device_id=(px, py, c), device_id_type=MESH))
        return cps

    def start(self, in_refs, out_refs, sems):
        for cp in self._copies(in_refs, out_refs, sems, True):
            cp.start()

    def finish(self, in_refs, out_refs, sems):
        for cp in self._copies(in_refs, out_refs, sems, False):
            cp.wait_recv()
        for cp in self._copies(in_refs, out_refs, sems, True):
            cp.wait_send()


class AllGatherStage1:
    def __init__(self, xs):
        n = len(xs)
        self.n, self.inputs = n, list(xs)
        self.out_shape = [jax.ShapeDtypeStruct((N_DEV,) + v.shape, v.dtype) for v in xs]
        self.scratch = [pltpu.SemaphoreType.DMA((4, n)), pltpu.SemaphoreType.DMA((4, n)),
                        pltpu.SemaphoreType.DMA((n,))]

    def _copies(self, x_refs, out_refs, sems, kind):
        send_sems, recv_sems, local_sems = sems
        x, y, c = _me()
        slot = lambda a, d: out_refs[a].at[4 * d[0] + 2 * d[1] + d[2]]
        if kind == "local":
            return [pltpu.make_async_copy(x_refs[a], slot(a, (x, y, c)), local_sems.at[a]) for a in range(self.n)]
        cps = []
        for k, peer in enumerate([(x, y, 1 - c), (1 - x, y, c), (x, 1 - y, c), (1 - x, 1 - y, c)]):
            for a in range(self.n):
                cps.append(pltpu.make_async_remote_copy(
                    src_ref=x_refs[a], dst_ref=slot(a, (x, y, c) if kind == "out" else peer),
                    send_sem=send_sems.at[k, a], recv_sem=recv_sems.at[k, a], device_id=peer, device_id_type=MESH))
        return cps

    def start(self, in_refs, out_refs, sems):
        for cp in self._copies(in_refs, out_refs, sems, "local") + self._copies(in_refs, out_refs, sems, "out"):
            cp.start()

    def finish(self, in_refs, out_refs, sems):
        for cp in self._copies(in_refs, out_refs, sems, "in"):
            cp.wait_recv()
        for cp in self._copies(in_refs, out_refs, sems, "out"):
            cp.wait_send()
        for cp in self._copies(in_refs, out_refs, sems, "local"):
            cp.wait()


def all_gather_stage2(outs, *, name):
    n = len(outs)

    def body(*refs):
        out_refs = refs[n:2 * n]
        send_sems, recv_sems = refs[2 * n:]
        x, y, c = _me()
        sends, recvs = [], []
        for k, (px, py) in enumerate([(1 - x, y), (x, 1 - y), (1 - x, 1 - y)]):
            for a in range(n):
                mk = lambda pc: pltpu.make_async_remote_copy(
                    src_ref=out_refs[a].at[4 * px + 2 * py + c], dst_ref=out_refs[a].at[4 * px + 2 * py + pc],
                    send_sem=send_sems.at[k, a], recv_sem=recv_sems.at[k, a], device_id=(x, y, 1 - c),
                    device_id_type=MESH)
                sends.append(mk(c))
                recvs.append(mk(1 - c))
        for cp in sends:
            cp.start()
        for cp in recvs:
            cp.wait_recv()
        for cp in sends:
            cp.wait_send()

    return _call(
        body, name=name, in_specs=_any_specs(n), out_specs=_any_specs(n),
        out_shape=[jax.ShapeDtypeStruct(o.shape, o.dtype) for o in outs],
        input_output_aliases={a: a for a in range(n)},
        scratch_shapes=[pltpu.SemaphoreType.DMA((3, n)), pltpu.SemaphoreType.DMA((3, n))],
    )(*outs)


def hosted_call(body, rider, *, name, grid, in_specs, out_specs, out_shape, scratch_shapes, args, vmem=None):
    n_in, n_out, n_scr = len(in_specs), len(out_specs), len(scratch_shapes)
    r_in, r_out = (len(rider.inputs), len(rider.out_shape)) if rider is not None else (0, 0)

    def wrapped(*refs):
        ins, refs = refs[:n_in], refs[n_in:]
        rins, refs = refs[:r_in], refs[r_in:]
        outs, refs = refs[:n_out], refs[n_out:]
        routs, refs = refs[:r_out], refs[r_out:]
        scr, rscr = refs[:n_scr], refs[n_scr:]
        ids = [pl.program_id(d) for d in range(len(grid))]
        first = functools.reduce(jnp.logical_and, [i == 0 for i in ids], True)
        last = functools.reduce(jnp.logical_and, [i == g - 1 for i, g in zip(ids, grid)], True)
        if rider is not None and grid:
            pl.when(first)(lambda: rider.start(rins, routs, rscr))
        elif rider is not None:
            rider.start(rins, routs, rscr)
        if body is not None:
            body(*ins, *outs, *scr)
        if rider is not None and grid:
            pl.when(last)(lambda: rider.finish(rins, routs, rscr))
        elif rider is not None:
            rider.finish(rins, routs, rscr)

    kw = dict(grid=grid) if grid else {}
    if grid or vmem is not None:
        kw["compiler_params"] = _params(("arbitrary",) * len(grid) if grid else None, vmem)
    res = _call(
        wrapped, name=name, in_specs=list(in_specs) + _any_specs(r_in), out_specs=list(out_specs) + _any_specs(r_out),
        out_shape=list(out_shape) + (rider.out_shape if rider is not None else []),
        scratch_shapes=list(scratch_shapes) + (rider.scratch if rider is not None else []), **kw,
    )(*args, *(rider.inputs if rider is not None else []))
    return list(res[:n_out]), list(res[n_out:])


def _pick_rows(r, target=256):
    best = None
    for d in range(16, min(r, target) + 1, 16):
        if r % d == 0:
            best = d
    return r if best is None else best


def pair_sum(g, got, *, name):
    _, _, R, C = g.shape
    tr = _pick_rows(R)

    def body(g_ref, got_ref, o_ref):
        mine = jnp.where(lax.axis_index("c") == 0, g_ref[:, 0], g_ref[:, 1])
        o_ref[...] = (mine.astype(F32) + got_ref[...].astype(F32)).astype(o_ref.dtype)

    return _call(
        body, name=name, grid=(R // tr,),
        in_specs=[pl.BlockSpec((4, 2, tr, C), lambda i: (0, 0, i, 0)), pl.BlockSpec((4, tr, C), lambda i: (0, i, 0))],
        out_specs=pl.BlockSpec((4, tr, C), lambda i: (0, i, 0)),
        out_shape=jax.ShapeDtypeStruct((4, R, C), g.dtype),
        compiler_params=_params(("parallel",)),
    )(g, got)


def chip_sum_adamw(p, got, w, m, v, *, name):
    _, R, C = p.shape
    tr = _pick_rows(R)

    def body(p_ref, got_ref, w_ref, m_ref, v_ref, g_ref, d_ref, mo_ref, vo_ref):
        my_chip = 2 * lax.axis_index("x") + lax.axis_index("y")
        g = jnp.zeros((tr, C), F32)
        for j in range(4):
            g = g + jnp.where(my_chip == j, p_ref[j], got_ref[j]).astype(F32)
        g_ref[...] = g
        d_ref[...], mo_ref[...], vo_ref[...] = _adamw_math(w_ref[...], g, m_ref[...], v_ref[...])

    part = pl.BlockSpec((4, tr, C), lambda i: (0, i, 0))
    spec = pl.BlockSpec((tr, C), lambda i: (i, 0))
    return _call(
        body, name=name, grid=(R // tr,), in_specs=[part, part, spec, spec, spec], out_specs=[spec] * 4,
        out_shape=[jax.ShapeDtypeStruct((R, C), F32)] * 4,
        compiler_params=_params(("parallel",)),
    )(p, got, w, m, v)


def rows_sum(g8, *, name):
    _, R, C = g8.shape

    def body(g_ref, o_ref):
        acc = g_ref[0]
        for j in range(1, N_DEV):
            acc = acc + g_ref[j]
        o_ref[...] = acc

    return _call(body, name=name, out_shape=jax.ShapeDtypeStruct((R, C), F32))(g8)


def _adamw_math(w, g, m, v):
    m = ADAM_B1 * m + (1.0 - ADAM_B1) * g
    v = ADAM_B2 * v + (1.0 - ADAM_B2) * (g * g)
    m_hat = m / (1.0 - ADAM_B1 ** ADAM_STEP)
    v_hat = v / (1.0 - ADAM_B2 ** ADAM_STEP)
    delta = -ADAM_LR * (m_hat / (jnp.sqrt(v_hat) + ADAM_EPS) + ADAM_WD * w)
    return delta, m, v


def adamw_small(wgmv, *, name):
    n = len(wgmv)

    def body(*refs):
        ins, outs = refs[:4 * n], refs[4 * n:]
        for a in range(n):
            w_ref, g_ref, m_ref, v_ref = ins[4 * a:4 * a + 4]
            d, mn, vn = _adamw_math(w_ref[...], g_ref[...], m_ref[...], v_ref[...])
            outs[3 * a][...] = d
            outs[3 * a + 1][...] = mn
            outs[3 * a + 2][...] = vn

    flat = [t for tup in wgmv for t in tup]
    res = _call(
        body, name=name,
        out_shape=[jax.ShapeDtypeStruct(tup[0].shape, F32) for tup in wgmv for _ in range(3)],
    )(*flat)
    return [tuple(res[3 * a:3 * a + 3]) for a in range(n)]


BIG = ("w_in", "w_out", "w_mq", "w_mkv", "w_mo", "w_gu", "w_down")
COL_SHARDED = ("w_in", "w_mkv", "w_gu")
SMALL = ("g_mix", "b_f", "conv_w", "conv_b", "ln_g", "ln_b", "g_x", "g_mem", "g_ffn", "g_final")


def _full_from_gathered(n, blk):
    _, rr, cc = blk.shape
    if n in COL_SHARDED:
        return blk.transpose(1, 0, 2).reshape(rr, N_DEV * cc)
    return blk.reshape(N_DEV * rr, cc)


def _shards_from_full(n, g):
    rr, cc = g.shape
    if n in COL_SHARDED:
        return g.reshape(rr, 4, 2, cc // N_DEV).transpose(1, 2, 0, 3)
    return g.reshape(4, 2, rr // N_DEV, cc)


def _small_layout():
    sizes = dict(g_mix=1024, b_f=8, conv_w=CONV_K * CONV_CH, conv_b=512, ln_g=512, ln_b=512, g_x=1024,
                 g_mem=1024, g_ffn=1024, g_final=1024, loss=1)
    lay, r0 = {}, 0
    for n, sz in sizes.items():
        r = -(-sz // LANES)
        lay[n] = (r0, r, sz)
        r0 += r
    return lay, -(-r0 // 8) * 8


def kernel(x, mem, g_mix, w_in, b_f, conv_w, conv_b, ln_g, ln_b, w_out, g_x, g_mem, w_mq, w_mkv, w_mo, g_ffn, w_gu, w_down, g_final, loss_target, m_g_mix, m_w_in, m_b_f, m_conv_w, m_conv_b, m_ln_g, m_ln_b, m_w_out, m_g_x, m_g_mem, m_w_mq, m_w_mkv, m_w_mo, m_g_ffn, m_w_gu, m_w_down, m_g_final, v_g_mix, v_w_in, v_b_f, v_conv_w, v_conv_b, v_ln_g, v_ln_b, v_w_out, v_g_x, v_g_mem, v_w_mq, v_w_mkv, v_w_mo, v_g_ffn, v_w_gu, v_w_down, v_g_final):
    names = ["g_mix", "w_in", "b_f", "conv_w", "conv_b", "ln_g", "ln_b", "w_out", "g_x", "g_mem", "w_mq",
             "w_mkv", "w_mo", "g_ffn", "w_gu", "w_down", "g_final"]
    W = dict(zip(names, [g_mix, w_in, b_f, conv_w, conv_b, ln_g, ln_b, w_out, g_x, g_mem, w_mq, w_mkv, w_mo,
                         g_ffn, w_gu, w_down, g_final]))
    Mo = dict(zip(names, [m_g_mix, m_w_in, m_b_f, m_conv_w, m_conv_b, m_ln_g, m_ln_b, m_w_out, m_g_x, m_g_mem,
                          m_w_mq, m_w_mkv, m_w_mo, m_g_ffn, m_w_gu, m_w_down, m_g_final]))
    Vo = dict(zip(names, [v_g_mix, v_w_in, v_b_f, v_conv_w, v_conv_b, v_ln_g, v_ln_b, v_w_out, v_g_x, v_g_mem,
                          v_w_mq, v_w_mkv, v_w_mo, v_g_ffn, v_w_gu, v_w_down, v_g_final]))
    dev = 4 * lax.axis_index("x") + 2 * lax.axis_index("y") + lax.axis_index("c")

    two = lambda a: a.reshape(-1, a.shape[-1])
    cw_shard = jnp.pad(two(conv_w), ((0, HALO - CONV_K), (0, 0)))
    w_in8, cw8 = all_gather([two(w_in).astype(BF16), cw_shard], name="ag_first")
    cw_full = cw8.transpose(1, 0, 2).reshape(HALO, -1)[:CONV_K]

    sp = dict(g_mix=g_mix, b_f=b_f, conv_w=cw_full, conv_b=conv_b, ln_g=ln_g, ln_b=ln_b, g_x=g_x, g_mem=g_mem,
              g_ffn=g_ffn, g_final=g_final)
    loss_blk, grad_x, gs, reduced = local_step(x, mem, loss_target, sp, _full_from_gathered("w_in", w_in8),
                                               [two(W[n]).astype(BF16) for n in LATE])

    lay, rs = _small_layout()
    small = {**{n: gs[n] for n in SMALL}, "loss": loss_blk[:, :1]}
    parts = []
    for n, (r0, r, sz) in lay.items():
        flat = small[n].reshape(-1).astype(F32)
        parts.append(jnp.pad(flat, (0, r * LANES - sz)).reshape(r, LANES))
    spack = jnp.concatenate(parts, axis=0)
    spack = jnp.pad(spack, ((0, rs - spack.shape[0]), (0, 0)))
    ssum = rows_sum(all_gather([spack], name="ag_small")[0], name="small_sum")
    gsmall = {n: ssum[r0:r0 + r].reshape(-1)[:sz] for n, (r0, r, sz) in lay.items()}
    loss = gsmall["loss"].reshape(())

    grads, delta, new_m, new_v = {}, {}, {}, {}
    for n in BIG:
        p, o = reduced[n]
        shp = W[n].shape
        g, d, mn, vn = chip_sum_adamw(p, o, two(W[n]), two(Mo[n]), two(Vo[n]), name="adamw_" + n)
        grads[n], delta[n], new_m[n], new_v[n] = g.reshape(shp), d.reshape(shp), mn.reshape(shp), vn.reshape(shp)
    for n in SMALL:
        if n == "conv_w":
            full = gsmall[n].reshape(CONV_K, CONV_CH)
            ncol = conv_w.shape[-1]
            grads[n] = lax.dynamic_slice(full, (0, dev * ncol), (CONV_K, ncol)).reshape(conv_w.shape)
        else:
            grads[n] = gsmall[n].reshape(W[n].shape)
    upd = adamw_small([(two(W[n]), two(grads[n]), two(Mo[n]), two(Vo[n])) for n in SMALL], name="adamw_small")
    for n, (d, mn, vn) in zip(SMALL, upd):
        shp = W[n].shape
        delta[n], new_m[n], new_v[n] = d.reshape(shp), mn.reshape(shp), vn.reshape(shp)
    return (loss, grad_x, *[grads[n] for n in names], *[delta[n] for n in names],
            *[new_m[n] for n in names], *[new_v[n] for n in names])
```

```python
import functools
import math

import jax
import jax.numpy as jnp
from jax import lax
from jax.experimental import pallas as pl
from jax.experimental.pallas import tpu as pltpu

F32 = jnp.float32
BF16 = jnp.bfloat16
EPS = 1e-6
N_DEV = 8
CONV_CH = 512
CONV_K = 31
FOX_HEADS = 8
FOX_HEAD_DIM = 64
FOX_W = 512
MEM_HEADS = 4
MEM_HEAD_DIM = 256
HALO = 32
LANES = 128
ADAM_LR, ADAM_B1, ADAM_B2, ADAM_EPS, ADAM_WD, ADAM_STEP = 0.001, 0.9, 0.999, 1e-08, 0.01, 10
NEG = -1e30
VMEM_CAP = 60 * 1024 * 1024
MESH = pl.DeviceIdType.MESH


def _call(body, **kw):
    return pl.pallas_call(body, **kw)


def _params(sem=None, vmem=None):
    kw = {}
    if sem is not None:
        kw["dimension_semantics"] = sem
    if vmem is not None:
        kw["vmem_limit_bytes"] = int(min(VMEM_CAP, vmem))
    return pltpu.CompilerParams(**kw)


def _nbytes(shape, dtype):
    return math.prod(shape) * jnp.dtype(dtype).itemsize


def _pick(n, target):
    best = None
    for d in range(LANES, min(n, target) + 1, LANES):
        if n % d == 0:
            best = d
    return n if best is None else best


def matmul(a, b, *, tb=False, out_dtype, res=None, tm=512, tn=512, tk=None, name):
    M, K = a.shape
    N = b.shape[0] if tb else b.shape[1]
    assert (b.shape[1] if tb else b.shape[0]) == K
    tm, tn = _pick(M, tm), _pick(N, tn)
    tk = K if tk is None else _pick(K, tk)
    assert M % tm == 0 and N % tn == 0 and K % tk == 0, (name, M, N, K, tm, tn, tk)
    nk = K // tk
    dn = (((1,), (1 if tb else 0,)), ((), ()))

    def body(*refs):
        if res is not None:
            a_ref, b_ref, r_ref, o_ref = refs[:4]
        else:
            a_ref, b_ref, o_ref = refs[:3]
        p = lax.dot_general(a_ref[...].astype(BF16), b_ref[...].astype(BF16), dn,
                            preferred_element_type=F32)

        def finish(acc):
            if res is not None:
                acc = acc + r_ref[...].astype(F32)
            o_ref[...] = acc.astype(out_dtype)

        if nk == 1:
            finish(p)
        else:
            acc_ref = refs[-1]
            k = pl.program_id(2)

            @pl.when(k == 0)
            def _():
                acc_ref[...] = p

            @pl.when(k > 0)
            def _():
                acc_ref[...] += p

            @pl.when(k == nk - 1)
            def _():
                finish(acc_ref[...])

    a_spec = pl.BlockSpec((tm, tk), lambda i, j, k: (i, k))
    b_spec = pl.BlockSpec((tn, tk), lambda i, j, k: (j, k)) if tb else pl.BlockSpec((tk, tn), lambda i, j, k: (k, j))
    o_spec = pl.BlockSpec((tm, tn), lambda i, j, k: (i, j))
    in_specs, args = [a_spec, b_spec], [a, b]
    est = 2 * (_nbytes((tm, tk), a.dtype) + _nbytes((tk, tn), b.dtype) + _nbytes((tm, tn), out_dtype))
    est += (a.dtype != BF16) * _nbytes((tm, tk), BF16) + (b.dtype != BF16) * _nbytes((tk, tn), BF16)
    est += 2 * _nbytes((tm, tn), F32)
    if res is not None:
        in_specs.append(o_spec)
        args.append(res)
        est += 2 * _nbytes((tm, tn), res.dtype)
    return _call(
        body, name=name, grid=(M // tm, N // tn, nk),
        in_specs=in_specs, out_specs=o_spec,
        out_shape=jax.ShapeDtypeStruct((M, N), out_dtype),
        scratch_shapes=[] if nk == 1 else [pltpu.VMEM((tm, tn), F32)],
        compiler_params=_params(("parallel", "parallel", "arbitrary"), est + (8 << 20)),
    )(*args)


def _rms_scale(x):
    return lax.rsqrt(jnp.mean(x * x, axis=-1, keepdims=True) + EPS)


def rmsnorm_fwd(x, g, *, name, tm=512):
    T, D = x.shape
    tm = min(tm, T)

    def body(x_ref, g_ref, o_ref):
        xv = x_ref[...]
        o_ref[...] = (xv * _rms_scale(xv) * g_ref[...]).astype(BF16)

    return _call(
        body, name=name, grid=(T // tm,),
        in_specs=[pl.BlockSpec((tm, D), lambda i: (i, 0)), pl.BlockSpec((1, D), lambda i: (0, 0))],
        out_specs=pl.BlockSpec((tm, D), lambda i: (i, 0)),
        out_shape=jax.ShapeDtypeStruct((T, D), BF16),
        compiler_params=_params(("parallel",)),
    )(x, g)


def _rms_bwd_math(xv, gv, dh):
    r = _rms_scale(xv)
    xh = xv * r
    dg = jnp.sum(dh * xh, axis=0, keepdims=True)
    dxh = dh * gv
    dx = r * (dxh - xh * jnp.mean(dxh * xh, axis=-1, keepdims=True))
    return dx, dg


def rmsnorm_bwd(x, g, dh, dres, *, name, tm=256):
    T, D = x.shape
    tm = min(tm, T)

    def body(*refs):
        if dres is not None:
            x_ref, g_ref, dh_ref, dr_ref, dx_ref, dg_ref = refs
        else:
            x_ref, g_ref, dh_ref, dx_ref, dg_ref = refs
        dx, dg = _rms_bwd_math(x_ref[...], g_ref[...], dh_ref[...].astype(F32))
        if dres is not None:
            dx = dx + dr_ref[...]
        dx_ref[...] = dx

        @pl.when(pl.program_id(0) == 0)
        def _():
            dg_ref[...] = jnp.zeros_like(dg_ref)

        dg_ref[...] += dg

    row = pl.BlockSpec((tm, D), lambda i: (i, 0))
    vec = pl.BlockSpec((1, D), lambda i: (0, 0))
    ins, args = [row, vec, row], [x, g, dh]
    if dres is not None:
        ins.append(row)
        args.append(dres)
    return _call(
        body, name=name, grid=(T // tm,), in_specs=ins, out_specs=[row, vec],
        out_shape=[jax.ShapeDtypeStruct((T, D), F32), jax.ShapeDtypeStruct((1, D), F32)],
        compiler_params=_params(("arbitrary",)),
    )(*args)


def final_loss_bwd(x, g, target, *, name, tm=256):
    T, D = x.shape
    tm = min(tm, T)

    def body(x_ref, g_ref, t_ref, dx_ref, dg_ref, l_ref):
        xv, gv = x_ref[...], g_ref[...]
        e = xv * _rms_scale(xv) * gv - t_ref[...]
        part = 0.5 * jnp.sum(jnp.mean(e * e, axis=-1, keepdims=True), axis=0, keepdims=True)
        dx, dg = _rms_bwd_math(xv, gv, e * (1.0 / D))
        dx_ref[...] = dx

        @pl.when(pl.program_id(0) == 0)
        def _():
            dg_ref[...] = jnp.zeros_like(dg_ref)
            l_ref[...] = jnp.zeros_like(l_ref)

        dg_ref[...] += dg
        l_ref[...] += jnp.broadcast_to(part, l_ref.shape)

    row = pl.BlockSpec((tm, D), lambda i: (i, 0))
    vec = pl.BlockSpec((1, D), lambda i: (0, 0))
    return _call(
        body, name=name, grid=(T // tm,), in_specs=[row, vec, row],
        out_specs=[row, vec, pl.BlockSpec((1, LANES), lambda i: (0, 0))],
        out_shape=[jax.ShapeDtypeStruct((T, D), F32), jax.ShapeDtypeStruct((1, D), F32),
                   jax.ShapeDtypeStruct((1, LANES), F32)],
        compiler_params=_params(("arbitrary",)),
    )(x, g, target)


def _sigmoid(v):
    return 1.0 / (1.0 + jnp.exp(-v))


def _glu(blk):
    u = blk[:, :CONV_CH].astype(F32)
    gt = blk[:, CONV_CH:].astype(F32)
    return u * _sigmoid(gt)


def _fill_causal_ext(ext, cur_ref, halo_ref, s, ts):
    ext[pl.ds(HALO, ts), :] = _glu(cur_ref[0])
    hal = _glu(halo_ref[0])
    ext[pl.ds(0, HALO), :] = jnp.where(s > 0, hal, 0.0)


def _causal_conv(ext, w_ref, ts):
    acc = jnp.zeros((ts, CONV_CH), F32)
    for j in range(CONV_K):
        acc = acc + ext[pl.ds(HALO - (CONV_K - 1) + j, ts), :] * w_ref[pl.ds(j, 1), :]
    return acc


def _ln_stats(y):
    mu = jnp.mean(y, axis=-1, keepdims=True)
    yc = y - mu
    rstd = lax.rsqrt(jnp.mean(yc * yc, axis=-1, keepdims=True) + EPS)
    return yc * rstd, rstd


def _conv_specs(ts, S):
    nh = ts // HALO
    cur = pl.BlockSpec((1, ts, 2 * CONV_CH), lambda b, s: (b, s, 0))
    halo = pl.BlockSpec((1, HALO, 2 * CONV_CH), lambda b, s: (b, jnp.maximum(s * nh - 1, 0), 0))
    w = pl.BlockSpec((HALO, CONV_CH), lambda b, s: (0, 0))
    vec = pl.BlockSpec((1, CONV_CH), lambda b, s: (0, 0))
    return cur, halo, w, vec


def conv_branch_fwd(ug, conv_w, conv_b, ln_g, ln_b, *, name, ts=256):
    B, S, _ = ug.shape
    ts = min(ts, S)
    cur, halo, w, vec = _conv_specs(ts, S)

    def body(cur_ref, halo_ref, w_ref, cb_ref, lg_ref, lb_ref, o_ref, ext):
        _fill_causal_ext(ext, cur_ref, halo_ref, pl.program_id(1), ts)
        y = _causal_conv(ext, w_ref, ts) + cb_ref[...]
        yh, _ = _ln_stats(y)
        ln = yh * lg_ref[...] + lb_ref[...]
        o_ref[0] = (ln * _sigmoid(ln)).astype(BF16)

    return _call(
        body, name=name, grid=(B, S // ts), in_specs=[cur, halo, w, vec, vec, vec],
        out_specs=pl.BlockSpec((1, ts, CONV_CH), lambda b, s: (b, s, 0)),
        out_shape=jax.ShapeDtypeStruct((B, S, CONV_CH), BF16),
        scratch_shapes=[pltpu.VMEM((ts + HALO, CONV_CH), F32)],
        compiler_params=_params(("parallel", "parallel")),
    )(ug, ug, conv_w, conv_b, ln_g, ln_b)


def conv_branch_bwd_a(ug, dcat, conv_w, conv_b, ln_g, ln_b, *, name, ts=256):
    B, S, _ = ug.shape
    ts = min(ts, S)
    cur, halo, w, vec = _conv_specs(ts, S)

    def body(cur_ref, halo_ref, d_ref, w_ref, cb_ref, lg_ref, lb_ref, dy_ref, dw_ref, dv_ref, ext):
        _fill_causal_ext(ext, cur_ref, halo_ref, pl.program_id(1), ts)
        y = _causal_conv(ext, w_ref, ts) + cb_ref[...]
        yh, rstd = _ln_stats(y)
        lg = lg_ref[...]
        ln = yh * lg + lb_ref[...]
        sg = _sigmoid(ln)
        dln = d_ref[0].astype(F32) * (sg * (1.0 + ln * (1.0 - sg)))
        dyh = dln * lg
        dy = rstd * (dyh - jnp.mean(dyh, axis=-1, keepdims=True)
                     - yh * jnp.mean(dyh * yh, axis=-1, keepdims=True))
        dy_ref[0] = dy

        @pl.when((pl.program_id(0) == 0) & (pl.program_id(1) == 0))
        def _():
            dw_ref[...] = jnp.zeros_like(dw_ref)
            dv_ref[...] = jnp.zeros_like(dv_ref)

        dv_ref[pl.ds(0, 1), :] += jnp.sum(dy, axis=0, keepdims=True)
        dv_ref[pl.ds(1, 1), :] += jnp.sum(dln * yh, axis=0, keepdims=True)
        dv_ref[pl.ds(2, 1), :] += jnp.sum(dln, axis=0, keepdims=True)
        for j in range(CONV_K):
            tap = ext[pl.ds(HALO - (CONV_K - 1) + j, ts), :]
            dw_ref[pl.ds(j, 1), :] += jnp.sum(dy * tap, axis=0, keepdims=True)

    return _call(
        body, name=name, grid=(B, S // ts),
        in_specs=[cur, halo, pl.BlockSpec((1, ts, CONV_CH), lambda b, s: (b, s, 0)), w, vec, vec, vec],
        out_specs=[pl.BlockSpec((1, ts, CONV_CH), lambda b, s: (b, s, 0)),
                   pl.BlockSpec((HALO, CONV_CH), lambda b, s: (0, 0)),
                   pl.BlockSpec((8, CONV_CH), lambda b, s: (0, 0))],
        out_shape=[jax.ShapeDtypeStruct((B, S, CONV_CH), F32),
                   jax.ShapeDtypeStruct((HALO, CONV_CH), F32),
                   jax.ShapeDtypeStruct((8, CONV_CH), F32)],
        scratch_shapes=[pltpu.VMEM((ts + HALO, CONV_CH), F32)],
        compiler_params=_params(("arbitrary", "arbitrary")),
    )(ug, ug, dcat, conv_w, conv_b, ln_g, ln_b)


def conv_branch_bwd_b(ug, dy, conv_w, *, name, ts=256):
    B, S, _ = ug.shape
    ts = min(ts, S)
    nh, n_halo = ts // HALO, S // HALO

    def body(cur_ref, dy_ref, nxt_ref, w_ref, o_ref, ext):
        last = pl.program_id(1) == pl.num_programs(1) - 1
        ext[pl.ds(0, ts), :] = dy_ref[0]
        ext[pl.ds(ts, HALO), :] = jnp.where(last, 0.0, nxt_ref[0])
        da = jnp.zeros((ts, CONV_CH), F32)
        for j in range(CONV_K):
            da = da + ext[pl.ds(CONV_K - 1 - j, ts), :] * w_ref[pl.ds(j, 1), :]
        blk = cur_ref[0]
        u = blk[:, :CONV_CH].astype(F32)
        sg = _sigmoid(blk[:, CONV_CH:].astype(F32))
        o_ref[0, :, :CONV_CH] = (da * sg).astype(BF16)
        o_ref[0, :, CONV_CH:] = (da * u * sg * (1.0 - sg)).astype(BF16)

    return _call(
        body, name=name, grid=(B, S // ts),
        in_specs=[pl.BlockSpec((1, ts, 2 * CONV_CH), lambda b, s: (b, s, 0)),
                  pl.BlockSpec((1, ts, CONV_CH), lambda b, s: (b, s, 0)),
                  pl.BlockSpec((1, HALO, CONV_CH), lambda b, s: (b, jnp.minimum((s + 1) * nh, n_halo - 1), 0)),
                  pl.BlockSpec((HALO, CONV_CH), lambda b, s: (0, 0))],
        out_specs=pl.BlockSpec((1, ts, 2 * CONV_CH), lambda b, s: (b, s, 0)),
        out_shape=jax.ShapeDtypeStruct((B, S, 2 * CONV_CH), BF16),
        scratch_shapes=[pltpu.VMEM((ts + HALO, CONV_CH), F32)],
        compiler_params=_params(("parallel", "parallel")),
    )(ug, dy, dy, conv_w)


def _tri(n, lower):
    r = lax.broadcasted_iota(jnp.int32, (n, n), 0)
    c = lax.broadcasted_iota(jnp.int32, (n, n), 1)
    return ((r >= c) if lower else (r <= c)).astype(F32)


def _eye(n):
    r = lax.broadcasted_iota(jnp.int32, (n, n), 0)
    c = lax.broadcasted_iota(jnp.int32, (n, n), 1)
    return (r == c).astype(F32)


def _dot_hi(a, b, dn):
    return lax.dot_general(a, b, dn, precision=lax.Precision.HIGHEST, preferred_element_type=F32)


NN = (((1,), (0,)), ((), ()))
NT = (((1,), (1,)), ((), ()))
TN = (((0,), (0,)), ((), ()))


def _log_sigmoid(v):
    e = jnp.exp(-jnp.abs(v))
    log1p_e = jnp.where(e < 1e-3, e * (1.0 - 0.5 * e), jnp.log(1.0 + e))
    return jnp.minimum(v, 0.0) - log1p_e


def fgate_fwd(h, w_f, b_f, *, name, ts=256):
    B, S, D = h.shape
    ts = min(ts, S)

    def body(h_ref, w_ref, b_ref, f_ref, cc_ref, cr_ref, carry):
        @pl.when(pl.program_id(1) == 0)
        def _():
            carry[...] = jnp.zeros_like(carry)

        f = jnp.dot(h_ref[0], w_ref[...], preferred_element_type=F32)
        f_ref[0] = f
        logf = _log_sigmoid(f + b_ref[...])
        c = _dot_hi(_tri(ts, True), logf, NN) + carry[pl.ds(0, 1), :]
        cc_ref[0] = c
        carry[pl.ds(0, 1), :] = c[ts - 1:ts, :]
        cr_ref[0] = _dot_hi(_eye(LANES), c, NT)

    return _call(
        body, name=name, grid=(B, S // ts),
        in_specs=[pl.BlockSpec((1, ts, D), lambda b, s: (b, s, 0)),
                  pl.BlockSpec((D, LANES), lambda b, s: (0, 0)),
                  pl.BlockSpec((1, LANES), lambda b, s: (0, 0))],
        out_specs=[pl.BlockSpec((1, ts, LANES), lambda b, s: (b, s, 0)),
                   pl.BlockSpec((1, ts, LANES), lambda b, s: (b, s, 0)),
                   pl.BlockSpec((1, LANES, ts), lambda b, s: (b, 0, s))],
        out_shape=[jax.ShapeDtypeStruct((B, S, LANES), F32), jax.ShapeDtypeStruct((B, S, LANES), F32),
                   jax.ShapeDtypeStruct((B, LANES, S), F32)],
        scratch_shapes=[pltpu.VMEM((8, LANES), F32)],
        compiler_params=_params(("parallel", "arbitrary")),
    )(h, w_f, b_f)


def fgate_bwd(dc, f, b_f, *, name, ts=256):
    B, S, _ = f.shape
    P = dc.shape[1]
    ts = min(ts, S)
    ns = S // ts

    def body(dc_ref, f_ref, b_ref, df_ref, db_ref, carry):
        @pl.when(pl.program_id(1) == 0)
        def _():
            carry[...] = jnp.zeros_like(carry)

        @pl.when((pl.program_id(0) == 0) & (pl.program_id(1) == 0))
        def _():
            db_ref[...] = jnp.zeros_like(db_ref)

        dc_t = dc_ref[0, 0]
        for j in range(1, P):
            dc_t = dc_t + dc_ref[0, j]
        dlogf = _dot_hi(_tri(ts, False), dc_t, NN) + carry[pl.ds(0, 1), :]
        carry[pl.ds(0, 1), :] = dlogf[0:1, :]
        df = dlogf * _sigmoid(-(f_ref[0] + b_ref[...]))
        df_ref[0] = df.astype(BF16)
        db_ref[...] += jnp.sum(df, axis=0, keepdims=True)

    return _call(
        body, name=name, grid=(B, ns),
        in_specs=[pl.BlockSpec((1, P, ts, LANES), lambda b, s: (b, 0, ns - 1 - s, 0)),
                  pl.BlockSpec((1, ts, LANES), lambda b, s: (b, ns - 1 - s, 0)),
                  pl.BlockSpec((1, LANES), lambda b, s: (0, 0))],
        out_specs=[pl.BlockSpec((1, ts, LANES), lambda b, s: (b, ns - 1 - s, 0)),
                   pl.BlockSpec((1, LANES), lambda b, s: (0, 0))],
        out_shape=[jax.ShapeDtypeStruct((B, S, LANES), BF16), jax.ShapeDtypeStruct((1, LANES), F32)],
        scratch_shapes=[pltpu.VMEM((8, LANES), F32)],
        compiler_params=_params(("arbitrary", "arbitrary")),
    )(dc, f, b_f)


def _lane_pick(tile, idx):
    lane = lax.broadcasted_iota(jnp.int32, tile.shape, 1)
    return jnp.sum(jnp.where(lane == idx, tile, 0.0), axis=-1, keepdims=True)


FOX_T = 256


def _fox_heads(q, cc_ref, p):
    lane = lax.broadcasted_iota(jnp.int32, q.shape, 1)
    qs = q * (1.0 / math.sqrt(FOX_HEAD_DIM))
    qhs = [jnp.where((lane < FOX_HEAD_DIM) == (hh == 0), qs, jnp.zeros_like(qs)) for hh in range(2)]
    crefs = [_lane_pick(cc_ref[0, pl.ds(0, 1), :], 2 * p + hh) for hh in range(2)]
    return qhs, crefs


def _causal(t, transposed):
    r = lax.broadcasted_iota(jnp.int32, (t, t), 0)
    c = lax.broadcasted_iota(jnp.int32, (t, t), 1)
    return (r <= c) if transposed else (c <= r)


def fox_fwd(qkv, c_col, c_row, *, name, rider=None):
    B, S, _ = qkv.shape
    assert S % FOX_T == 0
    tq, nq = FOX_T, S // FOX_T
    npair = FOX_HEADS // 2

    def body(q_ref, k_ref, v_ref, cc_ref, cr_ref, o_ref, l_ref, s_scr, m_scr, acc_scr):
        p, qi = pl.program_id(1), pl.program_id(2)
        qhs, crefs = _fox_heads(q_ref[0], cc_ref, p)
        lane = lax.broadcasted_iota(jnp.int32, (tq, LANES), 1)
        first = lane < FOX_HEAD_DIM
        for hh in range(2):
            m_scr[hh] = jnp.full((tq, LANES), NEG, F32)
            acc_scr[hh] = jnp.zeros((tq, LANES), F32)

        def logits(kb, diagonal):
            k0 = pl.multiple_of(kb * tq, tq)
            k = k_ref[0, pl.ds(k0, tq), :]
            for hh in range(2):
                s = lax.dot_general(qhs[hh], k, NT, preferred_element_type=F32)
                s = s + (crefs[hh] - cr_ref[0, pl.ds(2 * p + hh, 1), pl.ds(k0, tq)])
                if diagonal:
                    s = jnp.where(_causal(tq, False), s, NEG)
                s_scr[hh, kb] = s
                m_scr[hh] = jnp.maximum(m_scr[hh], jnp.maximum(s[:, :LANES], s[:, LANES:]))

        def sweep1(kb, carry):
            logits(kb, False)
            return carry

        lax.fori_loop(0, qi, sweep1, 0)
        logits(qi, True)
        ms = [jnp.max(m_scr[hh], axis=-1, keepdims=True) for hh in range(2)]
        mbs = [jnp.broadcast_to(ms[hh], (tq, tq)) for hh in range(2)]

        for hh in range(2):
            m_scr[hh] = jnp.zeros((tq, LANES), F32)

        def weigh(kb, carry):
            k0 = pl.multiple_of(kb * tq, tq)
            v = v_ref[0, pl.ds(k0, tq), :]
            for hh in range(2):
                pr = jnp.exp(s_scr[hh, kb] - mbs[hh])
                m_scr[hh] += pr[:, :LANES] + pr[:, LANES:]
                acc_scr[hh] += jnp.dot(pr.astype(BF16), v, preferred_element_type=F32)
            return carry

        lax.fori_loop(0, qi + 1, weigh, 0)
        accs = [acc_scr[hh] for hh in range(2)]
        ls = [jnp.sum(m_scr[hh], axis=-1, keepdims=True) for hh in range(2)]
        o_ref[0] = jnp.where(first, accs[0] / ls[0], accs[1] / ls[1]).astype(BF16)
        l_ref[0, 0] = jnp.where(first, ms[0] + jnp.log(ls[0]), ms[1] + jnp.log(ls[1]))

    return hosted_call(
        body, rider, name=name, grid=(B, npair, nq),
        in_specs=[pl.BlockSpec((1, tq, LANES), lambda b, p, i: (b, i, p)),
                  pl.BlockSpec((1, S, LANES), lambda b, p, i: (b, 0, npair + p)),
                  pl.BlockSpec((1, S, LANES), lambda b, p, i: (b, 0, 2 * npair + p)),
                  pl.BlockSpec((1, tq, LANES), lambda b, p, i: (b, i, 0)),
                  pl.BlockSpec((1, 8, S), lambda b, p, i: (b, 0, 0))],
        out_specs=[pl.BlockSpec((1, tq, LANES), lambda b, p, i: (b, i, p)),
                   pl.BlockSpec((1, 1, tq, LANES), lambda b, p, i: (b, p, i, 0))],
        out_shape=[jax.ShapeDtypeStruct((B, S, FOX_W), BF16),
                   jax.ShapeDtypeStruct((B, npair, S, LANES), F32)],
        scratch_shapes=[pltpu.VMEM((2, nq, tq, tq), F32), pltpu.VMEM((2, tq, LANES), F32),
                        pltpu.VMEM((2, tq, LANES), F32)],
        args=(qkv, qkv, qkv, c_col, c_row),
    )


def fox_bwd_dq(qkv, dcat, lse, c_col, c_row, *, name, rider=None):
    B, S, _ = qkv.shape
    tq, nq = FOX_T, S // FOX_T
    npair = FOX_HEADS // 2

    def body(q_ref, k_ref, v_ref, do_ref, l_ref, cc_ref, cr_ref, dq_ref, st_ref, p_scr, dp_scr, dl_scr):
        p, qi = pl.program_id(1), pl.program_id(2)
        qhs, crefs = _fox_heads(q_ref[0], cc_ref, p)
        lane = lax.broadcasted_iota(jnp.int32, (tq, LANES), 1)
        do_b = do_ref[0].astype(BF16)
        dohs = [jnp.where((lane < FOX_HEAD_DIM) == (hh == 0), do_b, jnp.zeros_like(do_b)) for hh in range(2)]
        lses = [_lane_pick(l_ref[0, 0], hh * FOX_HEAD_DIM) for hh in range(2)]
        lbs = [jnp.broadcast_to(lses[hh], (tq, tq)) for hh in range(2)]
        for hh in range(2):
            dl_scr[hh] = jnp.zeros((tq, LANES), F32)

        def probs(kb, diagonal):
            k0 = pl.multiple_of(kb * tq, tq)
            k = k_ref[0, pl.ds(k0, tq), :]
            v = v_ref[0, pl.ds(k0, tq), :]
            for hh in range(2):
                s = lax.dot_general(qhs[hh], k, NT, preferred_element_type=F32)
                s = s + (crefs[hh] - cr_ref[0, pl.ds(2 * p + hh, 1), pl.ds(k0, tq)])
                pr = jnp.exp(s - lbs[hh])
                if diagonal:
                    pr = jnp.where(_causal(tq, False), pr, 0.0)
                dp = lax.dot_general(dohs[hh], v, NT, preferred_element_type=F32)
                pdp = pr * dp
                dl_scr[hh] += pdp[:, :LANES] + pdp[:, LANES:]
                p_scr[hh, kb] = pr
                dp_scr[hh, kb] = dp

        def first_pass(kb, carry):
            probs(kb, False)
            return carry

        lax.fori_loop(0, qi, first_pass, 0)
        probs(qi, True)

        dls = [jnp.sum(dl_scr[hh], axis=-1, keepdims=True) for hh in range(2)]
        dlbs = [jnp.broadcast_to(dls[hh], (tq, tq)) for hh in range(2)]

        def second_pass(kb, dq):
            k0 = pl.multiple_of(kb * tq, tq)
            k = k_ref[0, pl.ds(k0, tq), :]
            for hh in range(2):
                ds = p_scr[hh, kb] * (dp_scr[hh, kb] - dlbs[hh])
                kh = jnp.where((lane < FOX_HEAD_DIM) == (hh == 0), k, jnp.zeros_like(k))
                dq = dq + jnp.dot(ds.astype(BF16), kh, preferred_element_type=F32)
            return dq

        dq = lax.fori_loop(0, qi + 1, second_pass, jnp.zeros((tq, LANES), F32))
        dq_ref[0] = (dq * (1.0 / math.sqrt(FOX_HEAD_DIM))).astype(BF16)
        cols = jnp.zeros((tq, LANES), F32)
        for j, col in enumerate([crefs[0] - lses[0], crefs[1] - lses[1], dls[0], dls[1]]):
            cols = jnp.where(lane == j, col, cols)
        st_ref[0, 0] = _dot_hi(_eye(LANES), cols, NT)[:8]

    return hosted_call(
        body, rider, name=name, grid=(B, npair, nq),
        in_specs=[pl.BlockSpec((1, tq, LANES), lambda b, p, i: (b, i, p)),
                  pl.BlockSpec((1, S, LANES), lambda b, p, i: (b, 0, npair + p)),
                  pl.BlockSpec((1, S, LANES), lambda b, p, i: (b, 0, 2 * npair + p)),
                  pl.BlockSpec((1, tq, LANES), lambda b, p, i: (b, i, npair + p)),
                  pl.BlockSpec((1, 1, tq, LANES), lambda b, p, i: (b, p, i, 0)),
                  pl.BlockSpec((1, tq, LANES), lambda b, p, i: (b, i, 0)),
                  pl.BlockSpec((1, 8, S), lambda b, p, i: (b, 0, 0))],
        out_specs=[pl.BlockSpec((1, tq, LANES), lambda b, p, i: (b, i, p)),
                   pl.BlockSpec((1, 1, 8, tq), lambda b, p, i: (b, p, 0, i))],
        out_shape=[jax.ShapeDtypeStruct((B, S, FOX_W), BF16), jax.ShapeDtypeStruct((B, npair, 8, S), F32)],
        scratch_shapes=[pltpu.VMEM((2, nq, tq, tq), F32), pltpu.VMEM((2, nq, tq, tq), F32),
                        pltpu.VMEM((2, tq, LANES), F32)],
        args=(qkv, qkv, qkv, dcat, lse, c_col, c_row), vmem=40 << 20,
    )


def fox_bwd_dkdv(qkv, dcat, stats, c_col, *, name, rider=None):
    B, S, _ = qkv.shape
    tk, nq = FOX_T, S // FOX_T
    npair = FOX_HEADS // 2
    inv = 1.0 / math.sqrt(FOX_HEAD_DIM)

    def body(q_ref, k_ref, v_ref, do_ref, st_ref, cc_ref, dk_ref, dv_ref, dc_ref, dk_scr, dv_scr, dc_scr):
        p, kt = pl.program_id(1), pl.program_id(2)
        lane = lax.broadcasted_iota(jnp.int32, (tk, LANES), 1)
        masks = [(lane < FOX_HEAD_DIM) == (hh == 0) for hh in range(2)]
        k = k_ref[0]
        v = v_ref[0]
        khs = [jnp.where(masks[hh], k, jnp.zeros_like(k)) for hh in range(2)]
        vhs = [jnp.where(masks[hh], v, jnp.zeros_like(v)) for hh in range(2)]
        ccbs = [jnp.broadcast_to(_lane_pick(cc_ref[0], 2 * p + hh), (tk, tk)) for hh in range(2)]
        dk_scr[...] = jnp.zeros_like(dk_scr)
        dv_scr[...] = jnp.zeros_like(dv_scr)
        dc_scr[...] = jnp.zeros_like(dc_scr)

        def tile(qb, diagonal):
            q0 = pl.multiple_of(qb * tk, tk)
            qs = q_ref[0, pl.ds(q0, tk), :] * inv
            do_b = do_ref[0, pl.ds(q0, tk), :].astype(BF16)
            for hh in range(2):
                st = lax.dot_general(khs[hh], qs, NT, preferred_element_type=F32)
                pr = jnp.exp(st - ccbs[hh] + st_ref[0, 0, pl.ds(hh, 1), pl.ds(q0, tk)])
                if diagonal:
                    pr = jnp.where(_causal(tk, True), pr, 0.0)
                dp = lax.dot_general(vhs[hh], do_b, NT, preferred_element_type=F32)
                ds = pr * (dp - st_ref[0, 0, pl.ds(2 + hh, 1), pl.ds(q0, tk)])
                dv_scr[...] += jnp.dot(pr.astype(BF16), jnp.where(masks[hh], do_b, jnp.zeros_like(do_b)),
                                       preferred_element_type=F32)
                dk_scr[...] += jnp.dot(ds.astype(BF16), jnp.where(masks[hh], qs, jnp.zeros_like(qs)),
                                       preferred_element_type=F32)
                dc_scr[hh] -= ds[:, :LANES] + ds[:, LANES:]

        def later(qb, carry):
            tile(qb, False)
            return carry

        tile(kt, True)
        lax.fori_loop(kt + 1, nq, later, 0)
        dk_ref[0] = dk_scr[...].astype(BF16)
        dv_ref[0] = dv_scr[...].astype(BF16)
        dcs = [jnp.sum(dc_scr[hh], axis=-1, keepdims=True) for hh in range(2)]
        dc_ref[0, 0] = jnp.where(lane == 2 * p, dcs[0], jnp.where(lane == 2 * p + 1, dcs[1], 0.0))

    full = lambda col: pl.BlockSpec((1, S, LANES), col)
    tile_spec = lambda col: pl.BlockSpec((1, tk, LANES), col)
    return hosted_call(
        body, rider, name=name, grid=(B, npair, nq),
        in_specs=[full(lambda b, p, t: (b, 0, p)),
                  tile_spec(lambda b, p, t: (b, t, npair + p)),
                  tile_spec(lambda b, p, t: (b, t, 2 * npair + p)),
                  full(lambda b, p, t: (b, 0, npair + p)),
                  pl.BlockSpec((1, 1, 8, S), lambda b, p, t: (b, p, 0, 0)),
                  tile_spec(lambda b, p, t: (b, t, 0))],
        out_specs=[tile_spec(lambda b, p, t: (b, t, p)), tile_spec(lambda b, p, t: (b, t, p)),
                   pl.BlockSpec((1, 1, tk, LANES), lambda b, p, t: (b, p, t, 0))],
        out_shape=[jax.ShapeDtypeStruct((B, S, FOX_W), BF16)] * 2
        + [jax.ShapeDtypeStruct((B, npair, S, LANES), F32)],
        scratch_shapes=[pltpu.VMEM((tk, LANES), F32), pltpu.VMEM((tk, LANES), F32),
                        pltpu.VMEM((2, tk, LANES), F32)],
        args=(qkv, qkv, qkv, dcat, stats, c_col),
    )


def xattn_fwd(qm, kv, *, name, tq=256):
    B, S, D = qm.shape
    M = kv.shape[1]
    tq = min(tq, S)
    inv = 1.0 / math.sqrt(MEM_HEAD_DIM)

    def body(q_ref, kv_ref, o_ref):
        for h in range(MEM_HEADS):
            c0 = h * MEM_HEAD_DIM
            qh = q_ref[0, :, c0:c0 + MEM_HEAD_DIM]
            kh = kv_ref[0, :, c0:c0 + MEM_HEAD_DIM]
            vh = kv_ref[0, :, D + c0:D + c0 + MEM_HEAD_DIM]
            s = lax.dot_general(qh, kh, NT, preferred_element_type=F32) * inv
            e = jnp.exp(s - jnp.max(s, axis=-1, keepdims=True))
            o = jnp.dot(e.astype(BF16), vh, preferred_element_type=F32) / jnp.sum(e, axis=-1, keepdims=True)
            o_ref[0, :, c0:c0 + MEM_HEAD_DIM] = o.astype(BF16)

    return _call(
        body, name=name, grid=(B, S // tq),
        in_specs=[pl.BlockSpec((1, tq, D), lambda b, i: (b, i, 0)),
                  pl.BlockSpec((1, M, 2 * D), lambda b, i: (b, 0, 0))],
        out_specs=pl.BlockSpec((1, tq, D), lambda b, i: (b, i, 0)),
        out_shape=jax.ShapeDtypeStruct((B, S, D), BF16),
        compiler_params=_params(("parallel", "parallel")),
    )(qm, kv)


def xattn_bwd(qm, kv, do, *, name, tq=256):
    B, S, D = qm.shape
    M = kv.shape[1]
    tq = min(tq, S)
    inv = 1.0 / math.sqrt(MEM_HEAD_DIM)

    def body(q_ref, kv_ref, do_ref, dq_ref, dkv_ref):
        @pl.when(pl.program_id(1) == 0)
        def _():
            dkv_ref[...] = jnp.zeros_like(dkv_ref)

        for h in range(MEM_HEADS):
            c0 = h * MEM_HEAD_DIM
            qh = q_ref[0, :, c0:c0 + MEM_HEAD_DIM]
            kh = kv_ref[0, :, c0:c0 + MEM_HEAD_DIM]
            vh = kv_ref[0, :, D + c0:D + c0 + MEM_HEAD_DIM]
            doh = do_ref[0, :, c0:c0 + MEM_HEAD_DIM]
            s = lax.dot_general(qh, kh, NT, preferred_element_type=F32) * inv
            e = jnp.exp(s - jnp.max(s, axis=-1, keepdims=True))
            pr = e / jnp.sum(e, axis=-1, keepdims=True)
            dp = lax.dot_general(doh, vh, NT, preferred_element_type=F32)
            ds = pr * (dp - jnp.sum(pr * dp, axis=-1, keepdims=True))
            ds_b = ds.astype(BF16)
            dq_ref[0, :, c0:c0 + MEM_HEAD_DIM] = (jnp.dot(ds_b, kh, preferred_element_type=F32) * inv).astype(BF16)
            dkv_ref[0, :, c0:c0 + MEM_HEAD_DIM] += lax.dot_general(ds_b, qh, TN, preferred_element_type=F32) * inv
            dkv_ref[0, :, D + c0:D + c0 + MEM_HEAD_DIM] += lax.dot_general(
                pr.astype(BF16), doh, TN, preferred_element_type=F32)

    row = pl.BlockSpec((1, tq, D), lambda b, i: (b, i, 0))
    kvs = pl.BlockSpec((1, M, 2 * D), lambda b, i: (b, 0, 0))
    return _call(
        body, name=name, grid=(B, S // tq), in_specs=[row, kvs, row], out_specs=[row, kvs],
        out_shape=[jax.ShapeDtypeStruct((B, S, D), BF16), jax.ShapeDtypeStruct((B, M, 2 * D), F32)],
        compiler_params=_params(("parallel", "arbitrary")),
    )(qm, kv, do)


def swiglu_fwd(gu, *, name, tm=256):
    T, F2 = gu.shape
    Fh = F2 // 2
    tm = min(tm, T)

    def body(gu_ref, o_ref):
        g = gu_ref[:, :Fh].astype(F32)
        u = gu_ref[:, Fh:].astype(F32)
        o_ref[...] = (g * _sigmoid(g) * u).astype(BF16)

    return _call(
        body, name=name, grid=(T // tm,),
        in_specs=[pl.BlockSpec((tm, F2), lambda i: (i, 0))],
        out_specs=pl.BlockSpec((tm, Fh), lambda i: (i, 0)),
        out_shape=jax.ShapeDtypeStruct((T, Fh), BF16),
        compiler_params=_params(("parallel",)),
    )(gu)


def swiglu_bwd(gu, dact, *, name, tm=256):
    T, F2 = gu.shape
    Fh = F2 // 2
    tm = min(tm, T)

    def body(gu_ref, d_ref, o_ref):
        g = gu_ref[:, :Fh].astype(F32)
        u = gu_ref[:, Fh:].astype(F32)
        d = d_ref[...].astype(F32)
        sg = _sigmoid(g)
        o_ref[:, :Fh] = (d * u * (sg * (1.0 + g * (1.0 - sg)))).astype(BF16)
        o_ref[:, Fh:] = (d * g * sg).astype(BF16)

    return _call(
        body, name=name, grid=(T // tm,),
        in_specs=[pl.BlockSpec((tm, F2), lambda i: (i, 0)), pl.BlockSpec((tm, Fh), lambda i: (i, 0))],
        out_specs=pl.BlockSpec((tm, F2), lambda i: (i, 0)),
        out_shape=jax.ShapeDtypeStruct((T, F2), BF16),
        compiler_params=_params(("parallel",)),
    )(gu, dact)


LATE = ("w_out", "w_mq", "w_mkv", "w_mo", "w_gu", "w_down")
RS_GROUPS = (("w_gu", "w_down"), ("w_out", "w_mq", "w_mkv", "w_mo"), ("w_in",))


def reduce_to_chips(names, gw, *, tag):
    g42 = [_shards_from_full(n, gw[n]) for n in names]
    got = sibling_exchange(g42, name="rs_sibling_" + tag)
    return [pair_sum(g, o, name="rs_pair_sum_" + n) for n, g, o in zip(names, g42, got)]


def local_step(x, mem, target, sp, w_in_full, late_shards):
    B, S, D = x.shape
    T = B * S
    M = mem.shape[1]
    row = lambda v: v.reshape(1, -1).astype(F32)
    g_mix, g_x, g_mem, g_ffn, g_final = (row(sp[k]) for k in ("g_mix", "g_x", "g_mem", "g_ffn", "g_final"))
    conv_b, ln_g, ln_b = row(sp["conv_b"]), row(sp["ln_g"]), row(sp["ln_b"])
    conv_w = jnp.pad(sp["conv_w"].astype(F32), ((0, HALO - CONV_K), (0, 0)))
    b_f = jnp.pad(row(sp["b_f"]), ((0, 0), (0, LANES - FOX_HEADS)))
    n_main = 2 * CONV_CH + 3 * FOX_W
    w_main = w_in_full[:, :n_main]
    w_f = jnp.pad(w_in_full[:, n_main:], ((0, 0), (0, LANES - FOX_HEADS)))

    x2d = x.reshape(T, D)
    h = rmsnorm_fwd(x2d, g_mix, name="rms_mix")
    z = matmul(h, w_main, out_dtype=BF16, tn=n_main, name="mm_in")
    z3 = z.reshape(B, S, n_main)
    ug, qkv = z3[:, :, :2 * CONV_CH], z3[:, :, 2 * CONV_CH:]
    conv_out = conv_branch_fwd(ug, conv_w, conv_b, ln_g, ln_b, name="conv_fwd")
    f_raw, c_col, c_row = fgate_fwd(h.reshape(B, S, D), w_f, b_f, name="fgate_fwd")
    (att, lse), partly = fox_fwd(qkv, c_col, c_row, name="fox_fwd", rider=AllGatherStage1(late_shards))
    gathered = all_gather_stage2(partly, name="ag_late_stage2")
    wf = {n: _full_from_gathered(n, blk) for n, blk in zip(LATE, gathered)}
    cat =jnp.concatenate([conv_out, att], axis=-1).reshape(T, D)
    x1 = matmul(cat, wf["w_out"], out_dtype=F32, res=x2d, tn=D, name="mm_out")
    hx = rmsnorm_fwd(x1, g_x, name="rms_x")
    qm = matmul(hx, wf["w_mq"], out_dtype=BF16, tn=D, name="mm_mq")
    mem2d = mem.reshape(B * M, D)
    mem_n = rmsnorm_fwd(mem2d, g_mem, name="rms_mem")
    kv = matmul(mem_n, wf["w_mkv"], out_dtype=BF16, tn=2 * D, name="mm_mkv").reshape(B, M, 2 * D)
    o = xattn_fwd(qm.reshape(B, S, D), kv, name="xattn_fwd").reshape(T, D)
    x2 = matmul(o, wf["w_mo"], out_dtype=F32, res=x1, tn=D, name="mm_mo")
    hf = rmsnorm_fwd(x2, g_ffn, name="rms_ffn")
    gu = matmul(hf, wf["w_gu"], out_dtype=BF16, tn=2816, name="mm_gu")
    act = swiglu_fwd(gu, name="swiglu_fwd")
    x3 = matmul(act, wf["w_down"], out_dtype=F32, res=x2, tn=D, name="mm_down")
    dx3, dg_final, loss = final_loss_bwd(x3, g_final, target.reshape(T, D), name="loss_bwd")
    gw = {}
    gw["w_down"] = matmul(act.T, dx3, out_dtype=BF16, tm=1408, tn=256, name="dw_down")
    dact = matmul(dx3, wf["w_down"], tb=True, out_dtype=BF16, tn=2816, name="dx_down")
    dgu = swiglu_bwd(gu, dact, name="swiglu_bwd")
    gw["w_gu"] = matmul(hf.T, dgu, out_dtype=BF16, tn=1408, name="dw_gu")
    dhf = matmul(dgu, wf["w_gu"], tb=True, out_dtype=BF16, tm=256, tn=D, name="dx_gu")
    dx2, dg_ffn = rmsnorm_bwd(x2, g_ffn, dhf, dx3, name="rms_ffn_bwd")
    gw["w_mo"] = matmul(o.T, dx2, out_dtype=BF16, name="dw_mo")
    do = matmul(dx2, wf["w_mo"], tb=True, out_dtype=BF16, tn=D, name="dx_mo")
    dqm, dkv = xattn_bwd(qm.reshape(B, S, D), kv, do.reshape(B, S, D), name="xattn_bwd")
    dqm = dqm.reshape(T, D)
    dkv = dkv.reshape(B * M, 2 * D)
    gw["w_mq"] = matmul(hx.T, dqm, out_dtype=BF16, tn=D, name="dw_mq")
    dhx = matmul(dqm, wf["w_mq"], tb=True, out_dtype=BF16, tn=D, name="dx_mq")
    gw["w_mkv"] = matmul(mem_n.T, dkv, out_dtype=BF16, tn=D, name="dw_mkv")
    dmem_n = matmul(dkv, wf["w_mkv"], tb=True, out_dtype=BF16, tn=D, name="dx_mkv")
    _, dg_mem = rmsnorm_bwd(mem2d, g_mem, dmem_n, None, name="rms_mem_bwd")
    dx1, dg_x = rmsnorm_bwd(x1, g_x, dhx, dx2, name="rms_x_bwd")
    gw["w_out"] = matmul(cat.T, dx1, out_dtype=BF16, name="dw_out")
    dcat = matmul(dx1, wf["w_out"], tb=True, out_dtype=BF16, tn=D, name="dx_out").reshape(B, S, D)
    dy, dconv_w, dvec = conv_branch_bwd_a(ug, dcat, conv_w, conv_b, ln_g, ln_b, name="conv_bwd_a")
    dug = conv_branch_bwd_b(ug, dy, conv_w, name="conv_bwd_b")
    parts, gots = {}, {}
    for n, p in zip(RS_GROUPS[0], reduce_to_chips(RS_GROUPS[0], gw, tag="ffn")):
        parts[n] = p
    for n, p in zip(RS_GROUPS[1], reduce_to_chips(RS_GROUPS[1], gw, tag="mid")):
        parts[n] = p
    (dq, stats), got = fox_bwd_dq(qkv, dcat, lse, c_col, c_row, name="fox_bwd_dq",
                                  rider=ChipExchange([parts[n] for n in RS_GROUPS[0]]))
    gots.update(zip(RS_GROUPS[0], got))
    (dk, dv, dc), got = fox_bwd_dkdv(qkv, dcat, stats, c_col, name="fox_bwd_dkdv",
                                     rider=ChipExchange([parts[n] for n in RS_GROUPS[1]]))
    gots.update(zip(RS_GROUPS[1], got))
    df, db_f = fgate_bwd(dc, f_raw, b_f, name="fgate_bwd")
    dz = jnp.concatenate([dug, dq, dk, dv], axis=-1).reshape(T, n_main)
    df2 = df.reshape(T, LANES)
    h_t = h.T
    dw_main = matmul(h_t, dz, out_dtype=BF16, tn=1280, name="dw_in")
    dw_f = matmul(h_t, df2, out_dtype=BF16, name="dw_f")
    gw["w_in"] = jnp.concatenate([dw_main, dw_f[:, :FOX_HEADS]], axis=-1)
    dh_f = matmul(df2, w_f, tb=True, out_dtype=F32, tn=D, name="dx_f")
    dh = matmul(dz, w_main, tb=True, out_dtype=F32, res=dh_f, tn=D, name="dx_in")
    parts["w_in"] = reduce_to_chips(RS_GROUPS[2], gw, tag="in")[0]
    gots["w_in"] = chip_exchange([parts["w_in"]], name="rs_chips_in")[0]
    dx, dg_mix = rmsnorm_bwd(x2d, g_mix, dh, dx1, name="rms_mix_bwd")
    gs = dict(g_mix=dg_mix, b_f=db_f[:, :FOX_HEADS], conv_w=dconv_w[:CONV_K], conv_b=dvec[0:1],
              ln_g=dvec[1:2], ln_b=dvec[2:3], g_x=dg_x, g_mem=dg_mem, g_ffn=dg_ffn, g_final=dg_final)
    return loss, dx.reshape(B, S, D), gs, {n: (parts[n], gots[n]) for n in BIG}


def _me():
    return lax.axis_index("x"), lax.axis_index("y"), lax.axis_index("c")


def _any_specs(n):
    return [pl.BlockSpec(memory_space=pl.ANY)] * n


def all_gather(xs, *, name):
    n = len(xs)

    def body(*refs):
        x_refs, out_refs = refs[:n], refs[n:2 * n]
        send_sems, recv_sems, local_sems = refs[2 * n:]
        x, y, c = _me()
        me, sibling = (x, y, c), (x, y, 1 - c)
        chips = [(1 - x, y), (x, 1 - y), (1 - x, 1 - y)]

        def slot(a, px, py, pc):
            return out_refs[a].at[4 * px + 2 * py + pc]

        def copy(a, k, block, to, own=False):
            return pltpu.make_async_remote_copy(
                src_ref=x_refs[a] if own else slot(a, *block), dst_ref=slot(a, *block),
                send_sem=send_sems.at[k, a], recv_sem=recv_sems.at[k, a], device_id=to, device_id_type=MESH)

        mine = [pltpu.make_async_copy(x_refs[a], slot(a, *me), local_sems.at[a]) for a in range(n)]
        first = [copy(a, 0, me, sibling, own=True) for a in range(n)]
        first += [copy(a, 1 + j, me, (*chip, c), own=True) for j, chip in enumerate(chips) for a in range(n)]
        for cp in mine + first:
            cp.start()
        passed = []
        for j, chip in enumerate(chips):
            for a in range(n):
                copy(a, 1 + j, (*chip, c), me).wait_recv()
                passed.append(copy(a, 4 + j, (*chip, c), sibling))
                passed[-1].start()
        for a in range(n):
            copy(a, 0, sibling, me).wait_recv()
            for j, chip in enumerate(chips):
                copy(a, 4 + j, (*chip, 1 - c), me).wait_recv()
        for cp in first + passed:
            cp.wait_send()
        for cp in mine:
            cp.wait()

    return _call(
        body, name=name, in_specs=_any_specs(n), out_specs=_any_specs(n),
        out_shape=[jax.ShapeDtypeStruct((N_DEV,) + v.shape, v.dtype) for v in xs],
        scratch_shapes=[pltpu.SemaphoreType.DMA((7, n)), pltpu.SemaphoreType.DMA((7, n)),
                        pltpu.SemaphoreType.DMA((n,))],
    )(*xs)


def sibling_exchange(gs, *, name):
    n = len(gs)

    def body(*refs):
        g_refs, out_refs = refs[:n], refs[n:2 * n]
        send_sems, recv_sems = refs[2 * n:]
        x, y, c = _me()
        cps = [pltpu.make_async_remote_copy(
            src_ref=g_refs[a].at[:, 1 - c], dst_ref=out_refs[a], send_sem=send_sems.at[a],
            recv_sem=recv_sems.at[a], device_id=(x, y, 1 - c), device_id_type=MESH) for a in range(n)]
        for cp in cps:
            cp.start()
        for cp in cps:
            cp.wait()

    return _call(
        body, name=name, in_specs=_any_specs(n), out_specs=_any_specs(n),
        out_shape=[jax.ShapeDtypeStruct((4,) + g.shape[2:], g.dtype) for g in gs],
        scratch_shapes=[pltpu.SemaphoreType.DMA((n,)), pltpu.SemaphoreType.DMA((n,))],
    )(*gs)


def chip_exchange(ps, *, name):
    return hosted_call(None, ChipExchange(ps), name=name, grid=(), in_specs=[], out_specs=[], out_shape=[],
                       scratch_shapes=[], args=[])[1]


class ChipExchange:
    def __init__(self, ps):
        n = len(ps)
        self.n, self.inputs = n, list(ps)
        self.out_shape = [jax.ShapeDtypeStruct(p.shape, p.dtype) for p in ps]
        self.scratch = [pltpu.SemaphoreType.DMA((3, n)), pltpu.SemaphoreType.DMA((3, n))]

    def _copies(self, p_refs, out_refs, sems, outgoing):
        send_sems, recv_sems = sems
        x, y, c = _me()
        my_chip = 2 * x + y
        cps = []
        for k in range(3):
            px, py = x ^ ((k + 1) >> 1), y ^ ((k + 1) & 1)
            src, dst = (2 * px + py, my_chip) if outgoing else (my_chip, 2 * px + py)
            for a in range(self.n):
                cps.append(pltpu.make_async_remote_copy(
                    src_ref=p_refs[a].at[src], dst_ref=out_refs[a].at[dst], send_sem=send_sems.at[k, a],
                    recv_sem=recv_sems.at[k, a], device_id=(px, py, c), device_id_type=MESH))
        return cps

    def start(self, in_refs, out_refs, sems):
        for cp in self._copies(in_refs, out_refs, sems, True):
            cp.start()

    def finish(self, in_refs, out_refs, sems):
        for cp in self._copies(in_refs, out_refs, sems, False):
            cp.wait_recv()
        for cp in self._copies(in_refs, out_refs, sems, True):
            cp.wait_send()


class AllGatherStage1:
    def __init__(self, xs):
        n = len(xs)
        self.n, self.inputs = n, list(xs)
        self.out_shape = [jax.ShapeDtypeStruct((N_DEV,) + v.shape, v.dtype) for v in xs]
        self.scratch = [pltpu.SemaphoreType.DMA((4, n)), pltpu.SemaphoreType.DMA((4, n)),
                        pltpu.SemaphoreType.DMA((n,))]

    def _copies(self, x_refs, out_refs, sems, kind):
        send_sems, recv_sems, local_sems = sems
        x, y, c = _me()
        slot = lambda a, d: out_refs[a].at[4 * d[0] + 2 * d[1] + d[2]]
        if kind == "local":
            return [pltpu.make_async_copy(x_refs[a], slot(a, (x, y, c)), local_sems.at[a]) for a in range(self.n)]
        cps = []
        for k, peer in enumerate([(x, y, 1 - c), (1 - x, y, c), (x, 1 - y, c), (1 - x, 1 - y, c)]):
            for a in range(self.n):
                cps.append(pltpu.make_async_remote_copy(
                    src_ref=x_refs[a], dst_ref=slot(a, (x, y, c) if kind == "out" else peer),
                    send_sem=send_sems.at[k, a], recv_sem=recv_sems.at[k, a], device_id=peer, device_id_type=MESH))
        return cps

    def start(self, in_refs, out_refs, sems):
        for cp in self._copies(in_refs, out_refs, sems, "local") + self._copies(in_refs, out_refs, sems, "out"):
            cp.start()

    def finish(self, in_refs, out_refs, sems):
        for cp in self._copies(in_refs, out_refs, sems, "in"):
            cp.wait_recv()
        for cp in self._copies(in_refs, out_refs, sems, "out"):
            cp.wait_send()
        for cp in self._copies(in_refs, out_refs, sems, "local"):
            cp.wait()


def all_gather_stage2(outs, *, name):
    n = len(outs)

    def body(*refs):
        out_refs = refs[n:2 * n]
        send_sems, recv_sems = refs[2 * n:]
        x, y, c = _me()
        sends, recvs = [], []
        for k, (px, py) in enumerate([(1 - x, y), (x, 1 - y), (1 - x, 1 - y)]):
            for a in range(n):
                mk = lambda pc: pltpu.make_async_remote_copy(
                    src_ref=out_refs[a].at[4 * px + 2 * py + c], dst_ref=out_refs[a].at[4 * px + 2 * py + pc],
                    send_sem=send_sems.at[k, a], recv_sem=recv_sems.at[k, a], device_id=(x, y, 1 - c),
                    device_id_type=MESH)
                sends.append(mk(c))
                recvs.append(mk(1 - c))
        for cp in sends:
            cp.start()
        for cp in recvs:
            cp.wait_recv()
        for cp in sends:
            cp.wait_send()

    return _call(
        body, name=name, in_specs=_any_specs(n), out_specs=_any_specs(n),
        out_shape=[jax.ShapeDtypeStruct(o.shape, o.dtype) for o in outs],
        input_output_aliases={a: a for a in range(n)},
        scratch_shapes=[pltpu.SemaphoreType.DMA((3, n)), pltpu.SemaphoreType.DMA((3, n))],
    )(*outs)


def hosted_call(body, rider, *, name, grid, in_specs, out_specs, out_shape, scratch_shapes, args, vmem=None):
    n_in, n_out, n_scr = len(in_specs), len(out_specs), len(scratch_shapes)
    r_in, r_out = (len(rider.inputs), len(rider.out_shape)) if rider is not None else (0, 0)

    def wrapped(*refs):
        ins, refs = refs[:n_in], refs[n_in:]
        rins, refs = refs[:r_in], refs[r_in:]
        outs, refs = refs[:n_out], refs[n_out:]
        routs, refs = refs[:r_out], refs[r_out:]
        scr, rscr = refs[:n_scr], refs[n_scr:]
        ids = [pl.program_id(d) for d in range(len(grid))]
        first = functools.reduce(jnp.logical_and, [i == 0 for i in ids], True)
        last = functools.reduce(jnp.logical_and, [i == g - 1 for i, g in zip(ids, grid)], True)
        if rider is not None and grid:
            pl.when(first)(lambda: rider.start(rins, routs, rscr))
        elif rider is not None:
            rider.start(rins, routs, rscr)
        if body is not None:
            body(*ins, *outs, *scr)
        if rider is not None and grid:
            pl.when(last)(lambda: rider.finish(rins, routs, rscr))
        elif rider is not None:
            rider.finish(rins, routs, rscr)

    kw = dict(grid=grid) if grid else {}
    if grid or vmem is not None:
        kw["compiler_params"] = _params(("arbitrary",) * len(grid) if grid else None, vmem)
    res = _call(
        wrapped, name=name, in_specs=list(in_specs) + _any_specs(r_in), out_specs=list(out_specs) + _any_specs(r_out),
        out_shape=list(out_shape) + (rider.out_shape if rider is not None else []),
        scratch_shapes=list(scratch_shapes) + (rider.scratch if rider is not None else []), **kw,
    )(*args, *(rider.inputs if rider is not None else []))
    return list(res[:n_out]), list(res[n_out:])


def _pick_rows(r, target=256):
    best = None
    for d in range(16, min(r, target) + 1, 16):
        if r % d == 0:
            best = d
    return r if best is None else best


def pair_sum(g, got, *, name):
    _, _, R, C = g.shape
    tr = _pick_rows(R)

    def body(g_ref, got_ref, o_ref):
        mine = jnp.where(lax.axis_index("c") == 0, g_ref[:, 0], g_ref[:, 1])
        o_ref[...] = (mine.astype(F32) + got_ref[...].astype(F32)).astype(o_ref.dtype)

    return _call(
        body, name=name, grid=(R // tr,),
        in_specs=[pl.BlockSpec((4, 2, tr, C), lambda i: (0, 0, i, 0)), pl.BlockSpec((4, tr, C), lambda i: (0, i, 0))],
        out_specs=pl.BlockSpec((4, tr, C), lambda i: (0, i, 0)),
        out_shape=jax.ShapeDtypeStruct((4, R, C), g.dtype),
        compiler_params=_params(("parallel",)),
    )(g, got)


def chip_sum_adamw(p, got, w, m, v, *, name):
    _, R, C = p.shape
    tr = _pick_rows(R)

    def body(p_ref, got_ref, w_ref, m_ref, v_ref, g_ref, d_ref, mo_ref, vo_ref):
        my_chip = 2 * lax.axis_index("x") + lax.axis_index("y")
        g = jnp.zeros((tr, C), F32)
        for j in range(4):
            g = g + jnp.where(my_chip == j, p_ref[j], got_ref[j]).astype(F32)
        g_ref[...] = g
        d_ref[...], mo_ref[...], vo_ref[...] = _adamw_math(w_ref[...], g, m_ref[...], v_ref[...])

    part = pl.BlockSpec((4, tr, C), lambda i: (0, i, 0))
    spec = pl.BlockSpec((tr, C), lambda i: (i, 0))
    return _call(
        body, name=name, grid=(R // tr,), in_specs=[part, part, spec, spec, spec], out_specs=[spec] * 4,
        out_shape=[jax.ShapeDtypeStruct((R, C), F32)] * 4,
        compiler_params=_params(("parallel",)),
    )(p, got, w, m, v)


def rows_sum(g8, *, name):
    _, R, C = g8.shape

    def body(g_ref, o_ref):
        acc = g_ref[0]
        for j in range(1, N_DEV):
            acc = acc + g_ref[j]
        o_ref[...] = acc

    return _call(body, name=name, out_shape=jax.ShapeDtypeStruct((R, C), F32))(g8)


def _adamw_math(w, g, m, v):
    m = ADAM_B1 * m + (1.0 - ADAM_B1) * g
    v = ADAM_B2 * v + (1.0 - ADAM_B2) * (g * g)
    m_hat = m / (1.0 - ADAM_B1 ** ADAM_STEP)
    v_hat = v / (1.0 - ADAM_B2 ** ADAM_STEP)
    delta = -ADAM_LR * (m_hat / (jnp.sqrt(v_hat) + ADAM_EPS) + ADAM_WD * w)
    return delta, m, v


def adamw_small(wgmv, *, name):
    n = len(wgmv)

    def body(*refs):
        ins, outs = refs[:4 * n], refs[4 * n:]
        for a in range(n):
            w_ref, g_ref, m_ref, v_ref = ins[4 * a:4 * a + 4]
            d, mn, vn = _adamw_math(w_ref[...], g_ref[...], m_ref[...], v_ref[...])
            outs[3 * a][...] = d
            outs[3 * a + 1][...] = mn
            outs[3 * a + 2][...] = vn

    flat = [t for tup in wgmv for t in tup]
    res = _call(
        body, name=name,
        out_shape=[jax.ShapeDtypeStruct(tup[0].shape, F32) for tup in wgmv for _ in range(3)],
    )(*flat)
    return [tuple(res[3 * a:3 * a + 3]) for a in range(n)]


BIG = ("w_in", "w_out", "w_mq", "w_mkv", "w_mo", "w_gu", "w_down")
COL_SHARDED = ("w_in", "w_mkv", "w_gu")
SMALL = ("g_mix", "b_f", "conv_w", "conv_b", "ln_g", "ln_b", "g_x", "g_mem", "g_ffn", "g_final")


def _full_from_gathered(n, blk):
    _, rr, cc = blk.shape
    if n in COL_SHARDED:
        return blk.transpose(1, 0, 2).reshape(rr, N_DEV * cc)
    return blk.reshape(N_DEV * rr, cc)


def _shards_from_full(n, g):
    rr, cc = g.shape
    if n in COL_SHARDED:
        return g.reshape(rr, 4, 2, cc // N_DEV).transpose(1, 2, 0, 3)
    return g.reshape(4, 2, rr // N_DEV, cc)


def _small_layout():
    sizes = dict(g_mix=1024, b_f=8, conv_w=CONV_K * CONV_CH, conv_b=512, ln_g=512, ln_b=512, g_x=1024,
                 g_mem=1024, g_ffn=1024, g_final=1024, loss=1)
    lay, r0 = {}, 0
    for n, sz in sizes.items():
        r = -(-sz // LANES)
        lay[n] = (r0, r, sz)
        r0 += r
    return lay, -(-r0 // 8) * 8


def kernel(x, mem, g_mix, w_in, b_f, conv_w, conv_b, ln_g, ln_b, w_out, g_x, g_mem, w_mq, w_mkv, w_mo, g_ffn, w_gu, w_down, g_final, loss_target, m_g_mix, m_w_in, m_b_f, m_conv_w, m_conv_b, m_ln_g, m_ln_b, m_w_out, m_g_x, m_g_mem, m_w_mq, m_w_mkv, m_w_mo, m_g_ffn, m_w_gu, m_w_down, m_g_final, v_g_mix, v_w_in, v_b_f, v_conv_w, v_conv_b, v_ln_g, v_ln_b, v_w_out, v_g_x, v_g_mem, v_w_mq, v_w_mkv, v_w_mo, v_g_ffn, v_w_gu, v_w_down, v_g_final):
    names = ["g_mix", "w_in", "b_f", "conv_w", "conv_b", "ln_g", "ln_b", "w_out", "g_x", "g_mem", "w_mq",
             "w_mkv", "w_mo", "g_ffn", "w_gu", "w_down", "g_final"]
    W = dict(zip(names, [g_mix, w_in, b_f, conv_w, conv_b, ln_g, ln_b, w_out, g_x, g_mem, w_mq, w_mkv, w_mo,
                         g_ffn, w_gu, w_down, g_final]))
    Mo = dict(zip(names, [m_g_mix, m_w_in, m_b_f, m_conv_w, m_conv_b, m_ln_g, m_ln_b, m_w_out, m_g_x, m_g_mem,
                          m_w_mq, m_w_mkv, m_w_mo, m_g_ffn, m_w_gu, m_w_down, m_g_final]))
    Vo = dict(zip(names, [v_g_mix, v_w_in, v_b_f, v_conv_w, v_conv_b, v_ln_g, v_ln_b, v_w_out, v_g_x, v_g_mem,
                          v_w_mq, v_w_mkv, v_w_mo, v_g_ffn, v_w_gu, v_w_down, v_g_final]))
    dev = 4 * lax.axis_index("x") + 2 * lax.axis_index("y") + lax.axis_index("c")

    two = lambda a: a.reshape(-1, a.shape[-1])
    cw_shard = jnp.pad(two(conv_w), ((0, HALO - CONV_K), (0, 0)))
    w_in8, cw8 = all_gather([two(w_in).astype(BF16), cw_shard], name="ag_first")
    cw_full = cw8.transpose(1, 0, 2).reshape(HALO, -1)[:CONV_K]

    sp = dict(g_mix=g_mix, b_f=b_f, conv_w=cw_full, conv_b=conv_b, ln_g=ln_g, ln_b=ln_b, g_x=g_x, g_mem=g_mem,
              g_ffn=g_ffn, g_final=g_final)
    loss_blk, grad_x, gs, reduced = local_step(x, mem, loss_target, sp, _full_from_gathered("w_in", w_in8),
                                               [two(W[n]).astype(BF16) for n in LATE])

    lay, rs = _small_layout()
    small = {**{n: gs[n] for n in SMALL}, "loss": loss_blk[:, :1]}
    parts = []
    for n, (r0, r, sz) in lay.items():
        flat = small[n].reshape(-1).astype(F32)
        parts.append(jnp.pad(flat, (0, r * LANES - sz)).reshape(r, LANES))
    spack = jnp.concatenate(parts, axis=0)
    spack = jnp.pad(spack, ((0, rs - spack.shape[0]), (0, 0)))
    ssum = rows_sum(all_gather([spack], name="ag_small")[0], name="small_sum")
    gsmall = {n: ssum[r0:r0 + r].reshape(-1)[:sz] for n, (r0, r, sz) in lay.items()}
    loss = gsmall["loss"].reshape(())

    grads, delta, new_m, new_v = {}, {}, {}, {}
    for n in BIG:
        p, o = reduced[n]
        shp = W[n].shape
        g, d, mn, vn = chip_sum_adamw(p, o, two(W[n]), two(Mo[n]), two(Vo[n]), name="adamw_" + n)
        grads[n], delta[n], new_m[n], new_v[n] = g.reshape(shp), d.reshape(shp), mn.reshape(shp), vn.reshape(shp)
    for n in SMALL:
        if n == "conv_w":
            full = gsmall[n].reshape(CONV_K, CONV_CH)
            ncol = conv_w.shape[-1]
            grads[n] = lax.dynamic_slice(full, (0, dev * ncol), (CONV_K, ncol)).reshape(conv_w.shape)
        else:
            grads[n] = gsmall[n].reshape(W[n].shape)
    upd = adamw_small([(two(W[n]), two(grads[n]), two(Mo[n]), two(Vo[n])) for n in SMALL], name="adamw_small")
    for n, (d, mn, vn) in zip(SMALL, upd):
        shp = W[n].shape
        delta[n], new_m[n], new_v[n] = d.reshape(shp), mn.reshape(shp), vn.reshape(shp)
    return (loss, grad_x, *[grads[n] for n in names], *[delta[n] for n in names],
            *[new_m[n] for n in names], *[new_v[n] for n in names])
```

```python
import functools
import math

import jax
import jax.numpy as jnp
from jax import lax
from jax.experimental import pallas as pl
from jax.experimental.pallas import tpu as pltpu

F32 = jnp.float32
BF16 = jnp.bfloat16
EPS = 1e-6
N_DEV = 8
CONV_CH = 512
CONV_K = 31
FOX_HEADS = 8
FOX_HEAD_DIM = 64
FOX_W = 512
MEM_HEADS = 4
MEM_HEAD_DIM = 256
HALO = 32
LANES = 128
ADAM_LR, ADAM_B1, ADAM_B2, ADAM_EPS, ADAM_WD, ADAM_STEP = 0.001, 0.9, 0.999, 1e-08, 0.01, 10
NEG = -1e30
VMEM_CAP = 60 * 1024 * 1024
MESH = pl.DeviceIdType.MESH


def _call(body, **kw):
    call = pl.pallas_call(body, **kw)
    return lambda *args: call(*[pltpu.with_memory_space_constraint(a, pltpu.HBM) for a in args])


def _params(sem=None, vmem=None):
    kw = {}
    if sem is not None:
        kw["dimension_semantics"] = sem
    if vmem is not None:
        kw["vmem_limit_bytes"] = int(min(VMEM_CAP, vmem))
    return pltpu.CompilerParams(**kw)


def _nbytes(shape, dtype):
    return math.prod(shape) * jnp.dtype(dtype).itemsize


def _pick(n, target):
    best = None
    for d in range(LANES, min(n, target) + 1, LANES):
        if n % d == 0:
            best = d
    return n if best is None else best


def matmul(a, b, *, tb=False, out_dtype, res=None, tm=512, tn=512, tk=None, name):
    M, K = a.shape
    N = b.shape[0] if tb else b.shape[1]
    assert (b.shape[1] if tb else b.shape[0]) == K
    tm, tn = _pick(M, tm), _pick(N, tn)
    tk = K if tk is None else _pick(K, tk)
    assert M % tm == 0 and N % tn == 0 and K % tk == 0, (name, M, N, K, tm, tn, tk)
    nk = K // tk
    dn = (((1,), (1 if tb else 0,)), ((), ()))

    def body(*refs):
        if res is not None:
            a_ref, b_ref, r_ref, o_ref = refs[:4]
        else:
            a_ref, b_ref, o_ref = refs[:3]
        p = lax.dot_general(a_ref[...].astype(BF16), b_ref[...].astype(BF16), dn,
                            preferred_element_type=F32)

        def finish(acc):
            if res is not None:
                acc = acc + r_ref[...].astype(F32)
            o_ref[...] = acc.astype(out_dtype)

        if nk == 1:
            finish(p)
        else:
            acc_ref = refs[-1]
            k = pl.program_id(2)

            @pl.when(k == 0)
            def _():
                acc_ref[...] = p

            @pl.when(k > 0)
            def _():
                acc_ref[...] += p

            @pl.when(k == nk - 1)
            def _():
                finish(acc_ref[...])

    a_spec = pl.BlockSpec((tm, tk), lambda i, j, k: (i, k))
    b_spec = pl.BlockSpec((tn, tk), lambda i, j, k: (j, k)) if tb else pl.BlockSpec((tk, tn), lambda i, j, k: (k, j))
    o_spec = pl.BlockSpec((tm, tn), lambda i, j, k: (i, j))
    in_specs, args = [a_spec, b_spec], [a, b]
    est = 2 * (_nbytes((tm, tk), a.dtype) + _nbytes((tk, tn), b.dtype) + _nbytes((tm, tn), out_dtype))
    est += (a.dtype != BF16) * _nbytes((tm, tk), BF16) + (b.dtype != BF16) * _nbytes((tk, tn), BF16)
    est += 2 * _nbytes((tm, tn), F32)
    if res is not None:
        in_specs.append(o_spec)
        args.append(res)
        est += 2 * _nbytes((tm, tn), res.dtype)
    return _call(
        body, name=name, grid=(M // tm, N // tn, nk),
        in_specs=in_specs, out_specs=o_spec,
        out_shape=jax.ShapeDtypeStruct((M, N), out_dtype),
        scratch_shapes=[] if nk == 1 else [pltpu.VMEM((tm, tn), F32)],
        compiler_params=_params(("parallel", "parallel", "arbitrary"), est + (8 << 20)),
    )(*args)


def _rms_scale(x):
    return lax.rsqrt(jnp.mean(x * x, axis=-1, keepdims=True) + EPS)


def rmsnorm_fwd(x, g, *, name, tm=512):
    T, D = x.shape
    tm = min(tm, T)

    def body(x_ref, g_ref, o_ref, ot_ref):
        xv = x_ref[...]
        h = xv * _rms_scale(xv) * g_ref[...]
        o_ref[...] = h.astype(BF16)
        ot_ref[...] = h.T.astype(BF16)

    return _call(
        body, name=name, grid=(T // tm,),
        in_specs=[pl.BlockSpec((tm, D), lambda i: (i, 0)), pl.BlockSpec((1, D), lambda i: (0, 0))],
        out_specs=[pl.BlockSpec((tm, D), lambda i: (i, 0)), pl.BlockSpec((D, tm), lambda i: (0, i))],
        out_shape=[jax.ShapeDtypeStruct((T, D), BF16), jax.ShapeDtypeStruct((D, T), BF16)],
        compiler_params=_params(("parallel",)),
    )(x, g)


def _rms_bwd_math(xv, gv, dh):
    r = _rms_scale(xv)
    xh = xv * r
    dg = jnp.sum(dh * xh, axis=0, keepdims=True)
    dxh = dh * gv
    dx = r * (dxh - xh * jnp.mean(dxh * xh, axis=-1, keepdims=True))
    return dx, dg


def rmsnorm_bwd(x, g, dh, dres, *, name, tm=256):
    T, D = x.shape
    tm = min(tm, T)

    def body(*refs):
        if dres is not None:
            x_ref, g_ref, dh_ref, dr_ref, dx_ref, dg_ref = refs
        else:
            x_ref, g_ref, dh_ref, dx_ref, dg_ref = refs
        dx, dg = _rms_bwd_math(x_ref[...], g_ref[...], dh_ref[...].astype(F32))
        if dres is not None:
            dx = dx + dr_ref[...]
        dx_ref[...] = dx

        @pl.when(pl.program_id(0) == 0)
        def _():
            dg_ref[...] = jnp.zeros_like(dg_ref)

        dg_ref[...] += dg

    row = pl.BlockSpec((tm, D), lambda i: (i, 0))
    vec = pl.BlockSpec((1, D), lambda i: (0, 0))
    ins, args = [row, vec, row], [x, g, dh]
    if dres is not None:
        ins.append(row)
        args.append(dres)
    return _call(
        body, name=name, grid=(T // tm,), in_specs=ins, out_specs=[row, vec],
        out_shape=[jax.ShapeDtypeStruct((T, D), F32), jax.ShapeDtypeStruct((1, D), F32)],
        compiler_params=_params(("arbitrary",)),
    )(*args)


def final_loss_bwd(x, g, target, *, name, tm=256):
    T, D = x.shape
    tm = min(tm, T)

    def body(x_ref, g_ref, t_ref, dx_ref, dg_ref, l_ref):
        xv, gv = x_ref[...], g_ref[...]
        e = xv * _rms_scale(xv) * gv - t_ref[...]
        part = 0.5 * jnp.sum(jnp.mean(e * e, axis=-1, keepdims=True), axis=0, keepdims=True)
        dx, dg = _rms_bwd_math(xv, gv, e * (1.0 / D))
        dx_ref[...] = dx

        @pl.when(pl.program_id(0) == 0)
        def _():
            dg_ref[...] = jnp.zeros_like(dg_ref)
            l_ref[...] = jnp.zeros_like(l_ref)

        dg_ref[...] += dg
        l_ref[...] += jnp.broadcast_to(part, l_ref.shape)

    row = pl.BlockSpec((tm, D), lambda i: (i, 0))
    vec = pl.BlockSpec((1, D), lambda i: (0, 0))
    return _call(
        body, name=name, grid=(T // tm,), in_specs=[row, vec, row],
        out_specs=[row, vec, pl.BlockSpec((1, LANES), lambda i: (0, 0))],
        out_shape=[jax.ShapeDtypeStruct((T, D), F32), jax.ShapeDtypeStruct((1, D), F32),
                   jax.ShapeDtypeStruct((1, LANES), F32)],
        compiler_params=_params(("arbitrary",)),
    )(x, g, target)


def _sigmoid(v):
    return 1.0 / (1.0 + jnp.exp(-v))


def _glu(blk):
    u = blk[:, :CONV_CH].astype(F32)
    gt = blk[:, CONV_CH:].astype(F32)
    return u * _sigmoid(gt)


def _fill_causal_ext(ext, cur_ref, halo_ref, s, ts):
    ext[pl.ds(HALO, ts), :] = _glu(cur_ref[0])
    hal = _glu(halo_ref[0])
    ext[pl.ds(0, HALO), :] = jnp.where(s > 0, hal, 0.0)


def _causal_conv(ext, w_ref, ts):
    acc = jnp.zeros((ts, CONV_CH), F32)
    for j in range(CONV_K):
        acc = acc + ext[pl.ds(HALO - (CONV_K - 1) + j, ts), :] * w_ref[pl.ds(j, 1), :]
    return acc


def _ln_stats(y):
    mu = jnp.mean(y, axis=-1, keepdims=True)
    yc = y - mu
    rstd = lax.rsqrt(jnp.mean(yc * yc, axis=-1, keepdims=True) + EPS)
    return yc * rstd, rstd


def _conv_specs(ts, S):
    nh = ts // HALO
    cur = pl.BlockSpec((1, ts, 2 * CONV_CH), lambda b, s: (b, s, 0))
    halo = pl.BlockSpec((1, HALO, 2 * CONV_CH), lambda b, s: (b, jnp.maximum(s * nh - 1, 0), 0))
    w = pl.BlockSpec((HALO, CONV_CH), lambda b, s: (0, 0))
    vec = pl.BlockSpec((1, CONV_CH), lambda b, s: (0, 0))
    return cur, halo, w, vec


def conv_branch_fwd(ug, conv_w, conv_b, ln_g, ln_b, *, name, ts=256):
    B, S, _ = ug.shape
    ts = min(ts, S)
    ns = S // ts
    cur, halo, w, vec = _conv_specs(ts, S)

    def body(cur_ref, halo_ref, w_ref, cb_ref, lg_ref, lb_ref, o_ref, ot_ref, ext):
        _fill_causal_ext(ext, cur_ref, halo_ref, pl.program_id(1), ts)
        y = _causal_conv(ext, w_ref, ts) + cb_ref[...]
        yh, _ = _ln_stats(y)
        ln = yh * lg_ref[...] + lb_ref[...]
        out = ln * _sigmoid(ln)
        o_ref[0] = out.astype(BF16)
        ot_ref[...] = out.T.astype(BF16)

    return _call(
        body, name=name, grid=(B, ns), in_specs=[cur, halo, w, vec, vec, vec],
        out_specs=[pl.BlockSpec((1, ts, CONV_CH), lambda b, s: (b, s, 0)),
                   pl.BlockSpec((CONV_CH, ts), lambda b, s: (0, b * ns + s))],
        out_shape=[jax.ShapeDtypeStruct((B, S, CONV_CH), BF16), jax.ShapeDtypeStruct((CONV_CH, B * S), BF16)],
        scratch_shapes=[pltpu.VMEM((ts + HALO, CONV_CH), F32)],
        compiler_params=_params(("parallel", "parallel")),
    )(ug, ug, conv_w, conv_b, ln_g, ln_b)


def conv_branch_bwd_a(ug, dcat, conv_w, conv_b, ln_g, ln_b, *, name, ts=256):
    B, S, _ = ug.shape
    ts = min(ts, S)
    cur, halo, w, vec = _conv_specs(ts, S)

    def body(cur_ref, halo_ref, d_ref, w_ref, cb_ref, lg_ref, lb_ref, dy_ref, dw_ref, dv_ref, ext):
        _fill_causal_ext(ext, cur_ref, halo_ref, pl.program_id(1), ts)
        y = _causal_conv(ext, w_ref, ts) + cb_ref[...]
        yh, rstd = _ln_stats(y)
        lg = lg_ref[...]
        ln = yh * lg + lb_ref[...]
        sg = _sigmoid(ln)
        dln = d_ref[0].astype(F32) * (sg * (1.0 + ln * (1.0 - sg)))
        dyh = dln * lg
        dy = rstd * (dyh - jnp.mean(dyh, axis=-1, keepdims=True)
                     - yh * jnp.mean(dyh * yh, axis=-1, keepdims=True))
        dy_ref[0] = dy

        @pl.when((pl.program_id(0) == 0) & (pl.program_id(1) == 0))
        def _():
            dw_ref[...] = jnp.zeros_like(dw_ref)
            dv_ref[...] = jnp.zeros_like(dv_ref)

        dv_ref[pl.ds(0, 1), :] += jnp.sum(dy, axis=0, keepdims=True)
        dv_ref[pl.ds(1, 1), :] += jnp.sum(dln * yh, axis=0, keepdims=True)
        dv_ref[pl.ds(2, 1), :] += jnp.sum(dln, axis=0, keepdims=True)
        for j in range(CONV_K):
            tap = ext[pl.ds(HALO - (CONV_K - 1) + j, ts), :]
            dw_ref[pl.ds(j, 1), :] += jnp.sum(dy * tap, axis=0, keepdims=True)

    return _call(
        body, name=name, grid=(B, S // ts),
        in_specs=[cur, halo, pl.BlockSpec((1, ts, CONV_CH), lambda b, s: (b, s, 0)), w, vec, vec, vec],
        out_specs=[pl.BlockSpec((1, ts, CONV_CH), lambda b, s: (b, s, 0)),
                   pl.BlockSpec((HALO, CONV_CH), lambda b, s: (0, 0)),
                   pl.BlockSpec((8, CONV_CH), lambda b, s: (0, 0))],
        out_shape=[jax.ShapeDtypeStruct((B, S, CONV_CH), F32),
                   jax.ShapeDtypeStruct((HALO, CONV_CH), F32),
                   jax.ShapeDtypeStruct((8, CONV_CH), F32)],
        scratch_shapes=[pltpu.VMEM((ts + HALO, CONV_CH), F32)],
        compiler_params=_params(("arbitrary", "arbitrary")),
    )(ug, ug, dcat, conv_w, conv_b, ln_g, ln_b)


def conv_branch_bwd_b(ug, dy, conv_w, *, name, ts=256):
    B, S, _ = ug.shape
    ts = min(ts, S)
    nh, n_halo = ts // HALO, S // HALO

    def body(cur_ref, dy_ref, nxt_ref, w_ref, o_ref, ext):
        last = pl.program_id(1) == pl.num_programs(1) - 1
        ext[pl.ds(0, ts), :] = dy_ref[0]
        ext[pl.ds(ts, HALO), :] = jnp.where(last, 0.0, nxt_ref[0])
        da = jnp.zeros((ts, CONV_CH), F32)
        for j in range(CONV_K):
            da = da + ext[pl.ds(CONV_K - 1 - j, ts), :] * w_ref[pl.ds(j, 1), :]
        blk = cur_ref[0]
        u = blk[:, :CONV_CH].astype(F32)
        sg = _sigmoid(blk[:, CONV_CH:].astype(F32))
        o_ref[0, :, :CONV_CH] = (da * sg).astype(BF16)
        o_ref[0, :, CONV_CH:] = (da * u * sg * (1.0 - sg)).astype(BF16)

    return _call(
        body, name=name, grid=(B, S // ts),
        in_specs=[pl.BlockSpec((1, ts, 2 * CONV_CH), lambda b, s: (b, s, 0)),
                  pl.BlockSpec((1, ts, CONV_CH), lambda b, s: (b, s, 0)),
                  pl.BlockSpec((1, HALO, CONV_CH), lambda b, s: (b, jnp.minimum((s + 1) * nh, n_halo - 1), 0)),
                  pl.BlockSpec((HALO, CONV_CH), lambda b, s: (0, 0))],
        out_specs=pl.BlockSpec((1, ts, 2 * CONV_CH), lambda b, s: (b, s, 0)),
        out_shape=jax.ShapeDtypeStruct((B, S, 2 * CONV_CH), BF16),
        scratch_shapes=[pltpu.VMEM((ts + HALO, CONV_CH), F32)],
        compiler_params=_params(("parallel", "parallel")),
    )(ug, dy, dy, conv_w)


def _tri(n, lower):
    r = lax.broadcasted_iota(jnp.int32, (n, n), 0)
    c = lax.broadcasted_iota(jnp.int32, (n, n), 1)
    return ((r >= c) if lower else (r <= c)).astype(F32)


def _eye(n):
    r = lax.broadcasted_iota(jnp.int32, (n, n), 0)
    c = lax.broadcasted_iota(jnp.int32, (n, n), 1)
    return (r == c).astype(F32)


def _dot_hi(a, b, dn):
    return lax.dot_general(a, b, dn, precision=lax.Precision.HIGHEST, preferred_element_type=F32)


NN = (((1,), (0,)), ((), ()))
NT = (((1,), (1,)), ((), ()))
TN = (((0,), (0,)), ((), ()))


def _log_sigmoid(v):
    e = jnp.exp(-jnp.abs(v))
    log1p_e = jnp.where(e < 1e-3, e * (1.0 - 0.5 * e), jnp.log(1.0 + e))
    return jnp.minimum(v, 0.0) - log1p_e


def fgate_fwd(h, w_f, b_f, *, name, ts=256):
    B, S, D = h.shape
    ts = min(ts, S)

    def body(h_ref, w_ref, b_ref, f_ref, cc_ref, cr_ref, carry):
        @pl.when(pl.program_id(1) == 0)
        def _():
            carry[...] = jnp.zeros_like(carry)

        f = jnp.dot(h_ref[0], w_ref[...], preferred_element_type=F32)
        f_ref[0] = f
        logf = _log_sigmoid(f + b_ref[...])
        c = _dot_hi(_tri(ts, True), logf, NN) + carry[pl.ds(0, 1), :]
        cc_ref[0] = c
        carry[pl.ds(0, 1), :] = c[ts - 1:ts, :]
        cr_ref[0] = _dot_hi(_eye(LANES), c, NT)

    return _call(
        body, name=name, grid=(B, S // ts),
        in_specs=[pl.BlockSpec((1, ts, D), lambda b, s: (b, s, 0)),
                  pl.BlockSpec((D, LANES), lambda b, s: (0, 0)),
                  pl.BlockSpec((1, LANES), lambda b, s: (0, 0))],
        out_specs=[pl.BlockSpec((1, ts, LANES), lambda b, s: (b, s, 0)),
                   pl.BlockSpec((1, ts, LANES), lambda b, s: (b, s, 0)),
                   pl.BlockSpec((1, LANES, ts), lambda b, s: (b, 0, s))],
        out_shape=[jax.ShapeDtypeStruct((B, S, LANES), F32), jax.ShapeDtypeStruct((B, S, LANES), F32),
                   jax.ShapeDtypeStruct((B, LANES, S), F32)],
        scratch_shapes=[pltpu.VMEM((8, LANES), F32)],
        compiler_params=_params(("parallel", "arbitrary")),
    )(h, w_f, b_f)


def fgate_bwd(dc, f, b_f, *, name, ts=256):
    B, S, _ = f.shape
    P = dc.shape[1]
    ts = min(ts, S)
    ns = S // ts

    def body(dc_ref, f_ref, b_ref, df_ref, db_ref, carry):
        @pl.when(pl.program_id(1) == 0)
        def _():
            carry[...] = jnp.zeros_like(carry)

        @pl.when((pl.program_id(0) == 0) & (pl.program_id(1) == 0))
        def _():
            db_ref[...] = jnp.zeros_like(db_ref)

        dc_t = dc_ref[0, 0]
        for j in range(1, P):
            dc_t = dc_t + dc_ref[0, j]
        dlogf = _dot_hi(_tri(ts, False), dc_t, NN) + carry[pl.ds(0, 1), :]
        carry[pl.ds(0, 1), :] = dlogf[0:1, :]
        df = dlogf * _sigmoid(-(f_ref[0] + b_ref[...]))
        df_ref[0] = df.astype(BF16)
        db_ref[...] += jnp.sum(df, axis=0, keepdims=True)

    return _call(
        body, name=name, grid=(B, ns),
        in_specs=[pl.BlockSpec((1, P, ts, LANES), lambda b, s: (b, 0, ns - 1 - s, 0)),
                  pl.BlockSpec((1, ts, LANES), lambda b, s: (b, ns - 1 - s, 0)),
                  pl.BlockSpec((1, LANES), lambda b, s: (0, 0))],
        out_specs=[pl.BlockSpec((1, ts, LANES), lambda b, s: (b, ns - 1 - s, 0)),
                   pl.BlockSpec((1, LANES), lambda b, s: (0, 0))],
        out_shape=[jax.ShapeDtypeStruct((B, S, LANES), BF16), jax.ShapeDtypeStruct((1, LANES), F32)],
        scratch_shapes=[pltpu.VMEM((8, LANES), F32)],
        compiler_params=_params(("arbitrary", "arbitrary")),
    )(dc, f, b_f)


def _lane_pick(tile, idx):
    lane = lax.broadcasted_iota(jnp.int32, tile.shape, 1)
    return jnp.sum(jnp.where(lane == idx, tile, 0.0), axis=-1, keepdims=True)


FOX_T = 256


def _fox_heads(q, cc_ref, p):
    lane = lax.broadcasted_iota(jnp.int32, q.shape, 1)
    qs = q * (1.0 / math.sqrt(FOX_HEAD_DIM))
    qhs = [jnp.where((lane < FOX_HEAD_DIM) == (hh == 0), qs, jnp.zeros_like(qs)) for hh in range(2)]
    crefs = [_lane_pick(cc_ref[0, pl.ds(0, 1), :], 2 * p + hh) for hh in range(2)]
    return qhs, crefs


def _causal(t, transposed):
    r = lax.broadcasted_iota(jnp.int32, (t, t), 0)
    c = lax.broadcasted_iota(jnp.int32, (t, t), 1)
    return (r <= c) if transposed else (c <= r)


QKV0 = 8


def fox_fwd(z, c_col, c_row, *, name, rider=None):
    B, S, _ = z.shape
    assert S % FOX_T == 0
    tq, nq = FOX_T, S // FOX_T
    npair = FOX_HEADS // 2

    def body(q_ref, k_ref, v_ref, cc_ref, cr_ref, o_ref, l_ref, ot_ref, s_scr, m_scr, acc_scr):
        p, qi = pl.program_id(1), pl.program_id(2)
        qhs, crefs = _fox_heads(q_ref[0], cc_ref, p)
        lane = lax.broadcasted_iota(jnp.int32, (tq, LANES), 1)
        first = lane < FOX_HEAD_DIM
        for hh in range(2):
            m_scr[hh] = jnp.full((tq, LANES), NEG, F32)
            acc_scr[hh] = jnp.zeros((tq, LANES), F32)

        def logits(kb, diagonal):
            k0 = pl.multiple_of(kb * tq, tq)
            k = k_ref[0, pl.ds(k0, tq), :]
            for hh in range(2):
                s = lax.dot_general(qhs[hh], k, NT, preferred_element_type=F32)
                s = s + (crefs[hh] - cr_ref[0, pl.ds(2 * p + hh, 1), pl.ds(k0, tq)])
                if diagonal:
                    s = jnp.where(_causal(tq, False), s, NEG)
                s_scr[hh, kb] = s
                m_scr[hh] = jnp.maximum(m_scr[hh], jnp.maximum(s[:, :LANES], s[:, LANES:]))

        def sweep1(kb, carry):
            logits(kb, False)
            return carry

        lax.fori_loop(0, qi, sweep1, 0)
        logits(qi, True)
        ms = [jnp.max(m_scr[hh], axis=-1, keepdims=True) for hh in range(2)]
        mbs = [jnp.broadcast_to(ms[hh], (tq, tq)) for hh in range(2)]

        for hh in range(2):
            m_scr[hh] = jnp.zeros((tq, LANES), F32)

        def weigh(kb, carry):
            k0 = pl.multiple_of(kb * tq, tq)
            v = v_ref[0, pl.ds(k0, tq), :]
            for hh in range(2):
                pr = jnp.exp(s_scr[hh, kb] - mbs[hh])
                m_scr[hh] += pr[:, :LANES] + pr[:, LANES:]
                acc_scr[hh] += jnp.dot(pr.astype(BF16), v, preferred_element_type=F32)
            return carry

        lax.fori_loop(0, qi + 1, weigh, 0)
        accs = [acc_scr[hh] for hh in range(2)]
        ls = [jnp.sum(m_scr[hh], axis=-1, keepdims=True) for hh in range(2)]
        out = jnp.where(first, accs[0] / ls[0], accs[1] / ls[1])
        o_ref[0] = out.astype(BF16)
        ot_ref[...] = out.T.astype(BF16)
        l_ref[0, 0] = jnp.where(first, ms[0] + jnp.log(ls[0]), ms[1] + jnp.log(ls[1]))

    return hosted_call(
        body, rider, name=name, grid=(B, npair, nq),
        in_specs=[pl.BlockSpec((1, tq, LANES), lambda b, p, i: (b, i, QKV0 + p)),
                  pl.BlockSpec((1, S, LANES), lambda b, p, i: (b, 0, QKV0 + npair + p)),
                  pl.BlockSpec((1, S, LANES), lambda b, p, i: (b, 0, QKV0 + 2 * npair + p)),
                  pl.BlockSpec((1, tq, LANES), lambda b, p, i: (b, i, 0)),
                  pl.BlockSpec((1, 8, S), lambda b, p, i: (b, 0, 0))],
        out_specs=[pl.BlockSpec((1, tq, LANES), lambda b, p, i: (b, i, p)),
                   pl.BlockSpec((1, 1, tq, LANES), lambda b, p, i: (b, p, i, 0)),
                   pl.BlockSpec((LANES, tq), lambda b, p, i: (p, b * nq + i))],
        out_shape=[jax.ShapeDtypeStruct((B, S, FOX_W), BF16),
                   jax.ShapeDtypeStruct((B, npair, S, LANES), F32),
                   jax.ShapeDtypeStruct((FOX_W, B * S), BF16)],
        scratch_shapes=[pltpu.VMEM((2, nq, tq, tq), F32), pltpu.VMEM((2, tq, LANES), F32),
                        pltpu.VMEM((2, tq, LANES), F32)],
        args=(z, z, z, c_col, c_row),
    )


def fox_bwd_dq(z, dcat, lse, c_col, c_row, *, name, rider=None):
    B, S, _ = z.shape
    tq, nq = FOX_T, S // FOX_T
    npair = FOX_HEADS // 2

    def body(q_ref, k_ref, v_ref, do_ref, l_ref, cc_ref, cr_ref, dq_ref, st_ref, p_scr, dp_scr, dl_scr):
        p, qi = pl.program_id(1), pl.program_id(2)
        qhs, crefs = _fox_heads(q_ref[0], cc_ref, p)
        lane = lax.broadcasted_iota(jnp.int32, (tq, LANES), 1)
        do_b = do_ref[0].astype(BF16)
        dohs = [jnp.where((lane < FOX_HEAD_DIM) == (hh == 0), do_b, jnp.zeros_like(do_b)) for hh in range(2)]
        lses = [_lane_pick(l_ref[0, 0], hh * FOX_HEAD_DIM) for hh in range(2)]
        lbs = [jnp.broadcast_to(lses[hh], (tq, tq)) for hh in range(2)]
        for hh in range(2):
            dl_scr[hh] = jnp.zeros((tq, LANES), F32)

        def probs(kb, diagonal):
            k0 = pl.multiple_of(kb * tq, tq)
            k = k_ref[0, pl.ds(k0, tq), :]
            v = v_ref[0, pl.ds(k0, tq), :]
            for hh in range(2):
                s = lax.dot_general(qhs[hh], k, NT, preferred_element_type=F32)
                s = s + (crefs[hh] - cr_ref[0, pl.ds(2 * p + hh, 1), pl.ds(k0, tq)])
                pr = jnp.exp(s - lbs[hh])
                if diagonal:
                    pr = jnp.where(_causal(tq, False), pr, 0.0)
                dp = lax.dot_general(dohs[hh], v, NT, preferred_element_type=F32)
                pdp = pr * dp
                dl_scr[hh] += pdp[:, :LANES] + pdp[:, LANES:]
                p_scr[hh, kb] = pr
                dp_scr[hh, kb] = dp

        def first_pass(kb, carry):
            probs(kb, False)
            return carry

        lax.fori_loop(0, qi, first_pass, 0)
        probs(qi, True)

        dls = [jnp.sum(dl_scr[hh], axis=-1, keepdims=True) for hh in range(2)]
        dlbs = [jnp.broadcast_to(dls[hh], (tq, tq)) for hh in range(2)]

        def second_pass(kb, dq):
            k0 = pl.multiple_of(kb * tq, tq)
            k = k_ref[0, pl.ds(k0, tq), :]
            for hh in range(2):
                ds = p_scr[hh, kb] * (dp_scr[hh, kb] - dlbs[hh])
                kh = jnp.where((lane < FOX_HEAD_DIM) == (hh == 0), k, jnp.zeros_like(k))
                dq = dq + jnp.dot(ds.astype(BF16), kh, preferred_element_type=F32)
            return dq

        dq = lax.fori_loop(0, qi + 1, second_pass, jnp.zeros((tq, LANES), F32))
        dq_ref[0] = (dq * (1.0 / math.sqrt(FOX_HEAD_DIM))).astype(BF16)
        cols = jnp.zeros((tq, LANES), F32)
        for j, col in enumerate([crefs[0] - lses[0], crefs[1] - lses[1], dls[0], dls[1]]):
            cols = jnp.where(lane == j, col, cols)
        st_ref[0, 0] = _dot_hi(_eye(LANES), cols, NT)[:8]

    return hosted_call(
        body, rider, name=name, grid=(B, npair, nq),
        in_specs=[pl.BlockSpec((1, tq, LANES), lambda b, p, i: (b, i, QKV0 + p)),
                  pl.BlockSpec((1, S, LANES), lambda b, p, i: (b, 0, QKV0 + npair + p)),
                  pl.BlockSpec((1, S, LANES), lambda b, p, i: (b, 0, QKV0 + 2 * npair + p)),
                  pl.BlockSpec((1, tq, LANES), lambda b, p, i: (b, i, npair + p)),
                  pl.BlockSpec((1, 1, tq, LANES), lambda b, p, i: (b, p, i, 0)),
                  pl.BlockSpec((1, tq, LANES), lambda b, p, i: (b, i, 0)),
                  pl.BlockSpec((1, 8, S), lambda b, p, i: (b, 0, 0))],
        out_specs=[pl.BlockSpec((1, tq, LANES), lambda b, p, i: (b, i, p)),
                   pl.BlockSpec((1, 1, 8, tq), lambda b, p, i: (b, p, 0, i))],
        out_shape=[jax.ShapeDtypeStruct((B, S, FOX_W), BF16), jax.ShapeDtypeStruct((B, npair, 8, S), F32)],
        scratch_shapes=[pltpu.VMEM((2, nq, tq, tq), F32), pltpu.VMEM((2, nq, tq, tq), F32),
                        pltpu.VMEM((2, tq, LANES), F32)],
        args=(z, z, z, dcat, lse, c_col, c_row), vmem=40 << 20,
    )


def fox_bwd_dkdv(z, dcat, stats, c_col, *, name, rider=None):
    B, S, _ = z.shape
    tk, nq = FOX_T, S // FOX_T
    npair = FOX_HEADS // 2
    inv = 1.0 / math.sqrt(FOX_HEAD_DIM)

    def body(q_ref, k_ref, v_ref, do_ref, st_ref, cc_ref, dk_ref, dv_ref, dc_ref, dk_scr, dv_scr, dc_scr):
        p, kt = pl.program_id(1), pl.program_id(2)
        lane = lax.broadcasted_iota(jnp.int32, (tk, LANES), 1)
        masks = [(lane < FOX_HEAD_DIM) == (hh == 0) for hh in range(2)]
        k = k_ref[0]
        v = v_ref[0]
        khs = [jnp.where(masks[hh], k, jnp.zeros_like(k)) for hh in range(2)]
        vhs = [jnp.where(masks[hh], v, jnp.zeros_like(v)) for hh in range(2)]
        ccbs = [jnp.broadcast_to(_lane_pick(cc_ref[0], 2 * p + hh), (tk, tk)) for hh in range(2)]
        dk_scr[...] = jnp.zeros_like(dk_scr)
        dv_scr[...] = jnp.zeros_like(dv_scr)
        dc_scr[...] = jnp.zeros_like(dc_scr)

        def tile(qb, diagonal):
            q0 = pl.multiple_of(qb * tk, tk)
            qs = q_ref[0, pl.ds(q0, tk), :] * inv
            do_b = do_ref[0, pl.ds(q0, tk), :].astype(BF16)
            for hh in range(2):
                st = lax.dot_general(khs[hh], qs, NT, preferred_element_type=F32)
                pr = jnp.exp(st - ccbs[hh] + st_ref[0, 0, pl.ds(hh, 1), pl.ds(q0, tk)])
                if diagonal:
                    pr = jnp.where(_causal(tk, True), pr, 0.0)
                dp = lax.dot_general(vhs[hh], do_b, NT, preferred_element_type=F32)
                ds = pr * (dp - st_ref[0, 0, pl.ds(2 + hh, 1), pl.ds(q0, tk)])
                dv_scr[...] += jnp.dot(pr.astype(BF16), jnp.where(masks[hh], do_b, jnp.zeros_like(do_b)),
                                       preferred_element_type=F32)
                dk_scr[...] += jnp.dot(ds.astype(BF16), jnp.where(masks[hh], qs, jnp.zeros_like(qs)),
                                       preferred_element_type=F32)
                dc_scr[hh] -= ds[:, :LANES] + ds[:, LANES:]

        def later(qb, carry):
            tile(qb, False)
            return carry

        tile(kt, True)
        lax.fori_loop(kt + 1, nq, later, 0)
        dk_ref[0] = dk_scr[...].astype(BF16)
        dv_ref[0] = dv_scr[...].astype(BF16)
        dcs = [jnp.sum(dc_scr[hh], axis=-1, keepdims=True) for hh in range(2)]
        dc_ref[0, 0] = jnp.where(lane == 2 * p, dcs[0], jnp.where(lane == 2 * p + 1, dcs[1], 0.0))

    full = lambda col: pl.BlockSpec((1, S, LANES), col)
    tile_spec = lambda col: pl.BlockSpec((1, tk, LANES), col)
    return hosted_call(
        body, rider, name=name, grid=(B, npair, nq),
        in_specs=[full(lambda b, p, t: (b, 0, QKV0 + p)),
                  tile_spec(lambda b, p, t: (b, t, QKV0 + npair + p)),
                  tile_spec(lambda b, p, t: (b, t, QKV0 + 2 * npair + p)),
                  full(lambda b, p, t: (b, 0, npair + p)),
                  pl.BlockSpec((1, 1, 8, S), lambda b, p, t: (b, p, 0, 0)),
                  tile_spec(lambda b, p, t: (b, t, 0))],
        out_specs=[tile_spec(lambda b, p, t: (b, t, p)), tile_spec(lambda b, p, t: (b, t, p)),
                   pl.BlockSpec((1, 1, tk, LANES), lambda b, p, t: (b, p, t, 0))],
        out_shape=[jax.ShapeDtypeStruct((B, S, FOX_W), BF16)] * 2
        + [jax.ShapeDtypeStruct((B, npair, S, LANES), F32)],
        scratch_shapes=[pltpu.VMEM((tk, LANES), F32), pltpu.VMEM((tk, LANES), F32),
                        pltpu.VMEM((2, tk, LANES), F32)],
        args=(z, z, z, dcat, stats, c_col),
    )


def xattn_fwd(qm, kv, *, name, tq=256):
    B, S, D = qm.shape
    M = kv.shape[1]
    tq = min(tq, S)
    inv = 1.0 / math.sqrt(MEM_HEAD_DIM)

    nq = S // tq

    def body(q_ref, kv_ref, o_ref, ot_ref):
        for h in range(MEM_HEADS):
            c0 = h * MEM_HEAD_DIM
            qh = q_ref[0, :, c0:c0 + MEM_HEAD_DIM]
            kh = kv_ref[0, :, c0:c0 + MEM_HEAD_DIM]
            vh = kv_ref[0, :, D + c0:D + c0 + MEM_HEAD_DIM]
            s = lax.dot_general(qh, kh, NT, preferred_element_type=F32) * inv
            e = jnp.exp(s - jnp.max(s, axis=-1, keepdims=True))
            o = jnp.dot(e.astype(BF16), vh, preferred_element_type=F32) / jnp.sum(e, axis=-1, keepdims=True)
            o_ref[0, :, c0:c0 + MEM_HEAD_DIM] = o.astype(BF16)
            ot_ref[c0:c0 + MEM_HEAD_DIM, :] = o.T.astype(BF16)

    return _call(
        body, name=name, grid=(B, nq),
        in_specs=[pl.BlockSpec((1, tq, D), lambda b, i: (b, i, 0)),
                  pl.BlockSpec((1, M, 2 * D), lambda b, i: (b, 0, 0))],
        out_specs=[pl.BlockSpec((1, tq, D), lambda b, i: (b, i, 0)),
                   pl.BlockSpec((D, tq), lambda b, i: (0, b * nq + i))],
        out_shape=[jax.ShapeDtypeStruct((B, S, D), BF16), jax.ShapeDtypeStruct((D, B * S), BF16)],
        compiler_params=_params(("parallel", "parallel")),
    )(qm, kv)


def xattn_bwd(qm, kv, do, *, name, tq=256):
    B, S, D = qm.shape
    M = kv.shape[1]
    tq = min(tq, S)
    inv = 1.0 / math.sqrt(MEM_HEAD_DIM)

    def body(q_ref, kv_ref, do_ref, dq_ref, dkv_ref):
        @pl.when(pl.program_id(1) == 0)
        def _():
            dkv_ref[...] = jnp.zeros_like(dkv_ref)

        for h in range(MEM_HEADS):
            c0 = h * MEM_HEAD_DIM
            qh = q_ref[0, :, c0:c0 + MEM_HEAD_DIM]
            kh = kv_ref[0, :, c0:c0 + MEM_HEAD_DIM]
            vh = kv_ref[0, :, D + c0:D + c0 + MEM_HEAD_DIM]
            doh = do_ref[0, :, c0:c0 + MEM_HEAD_DIM]
            s = lax.dot_general(qh, kh, NT, preferred_element_type=F32) * inv
            e = jnp.exp(s - jnp.max(s, axis=-1, keepdims=True))
            pr = e / jnp.sum(e, axis=-1, keepdims=True)
            dp = lax.dot_general(doh, vh, NT, preferred_element_type=F32)
            ds = pr * (dp - jnp.sum(pr * dp, axis=-1, keepdims=True))
            ds_b = ds.astype(BF16)
            dq_ref[0, :, c0:c0 + MEM_HEAD_DIM] = (jnp.dot(ds_b, kh, preferred_element_type=F32) * inv).astype(BF16)
            dkv_ref[0, :, c0:c0 + MEM_HEAD_DIM] += lax.dot_general(ds_b, qh, TN, preferred_element_type=F32) * inv
            dkv_ref[0, :, D + c0:D + c0 + MEM_HEAD_DIM] += lax.dot_general(
                pr.astype(BF16), doh, TN, preferred_element_type=F32)

    row = pl.BlockSpec((1, tq, D), lambda b, i: (b, i, 0))
    kvs = pl.BlockSpec((1, M, 2 * D), lambda b, i: (b, 0, 0))
    return _call(
        body, name=name, grid=(B, S // tq), in_specs=[row, kvs, row], out_specs=[row, kvs],
        out_shape=[jax.ShapeDtypeStruct((B, S, D), BF16), jax.ShapeDtypeStruct((B, M, 2 * D), F32)],
        compiler_params=_params(("parallel", "arbitrary")),
    )(qm, kv, do)


def swiglu_fwd(gu, *, name, tm=256):
    T, F2 = gu.shape
    Fh = F2 // 2
    tm = min(tm, T)

    def body(gu_ref, o_ref, ot_ref):
        g = gu_ref[:, :Fh].astype(F32)
        u = gu_ref[:, Fh:].astype(F32)
        act = g * _sigmoid(g) * u
        o_ref[...] = act.astype(BF16)
        ot_ref[...] = act.T.astype(BF16)

    return _call(
        body, name=name, grid=(T // tm,),
        in_specs=[pl.BlockSpec((tm, F2), lambda i: (i, 0))],
        out_specs=[pl.BlockSpec((tm, Fh), lambda i: (i, 0)), pl.BlockSpec((Fh, tm), lambda i: (0, i))],
        out_shape=[jax.ShapeDtypeStruct((T, Fh), BF16), jax.ShapeDtypeStruct((Fh, T), BF16)],
        compiler_params=_params(("parallel",)),
    )(gu)


def swiglu_bwd(gu, dact, *, name, tm=256):
    T, F2 = gu.shape
    Fh = F2 // 2
    tm = min(tm, T)

    def body(gu_ref, d_ref, o_ref):
        g = gu_ref[:, :Fh].astype(F32)
        u = gu_ref[:, Fh:].astype(F32)
        d = d_ref[...].astype(F32)
        sg = _sigmoid(g)
        o_ref[:, :Fh] = (d * u * (sg * (1.0 + g * (1.0 - sg)))).astype(BF16)
        o_ref[:, Fh:] = (d * g * sg).astype(BF16)

    return _call(
        body, name=name, grid=(T // tm,),
        in_specs=[pl.BlockSpec((tm, F2), lambda i: (i, 0)), pl.BlockSpec((tm, Fh), lambda i: (i, 0))],
        out_specs=pl.BlockSpec((tm, F2), lambda i: (i, 0)),
        out_shape=jax.ShapeDtypeStruct((T, F2), BF16),
        compiler_params=_params(("parallel",)),
    )(gu, dact)


LATE = ("w_out", "w_mq", "w_mkv", "w_mo", "w_gu", "w_down")
RS_GROUPS = (("w_gu", "w_down"), ("w_out", "w_mq", "w_mkv", "w_mo"), ("w_in",))


def reduce_to_chips(names, gw, *, tag):
    g42 = [_shards_from_full(n, gw[n]) for n in names]
    got = sibling_exchange(g42, name="rs_sibling_" + tag)
    return [pair_sum(g, o, name="rs_pair_sum_" + n) for n, g, o in zip(names, g42, got)]


def local_step(x, mem, target, sp, w_in_full, late_shards):
    B, S, D = x.shape
    T = B * S
    M = mem.shape[1]
    row = lambda v: v.reshape(1, -1).astype(F32)
    g_mix, g_x, g_mem, g_ffn, g_final = (row(sp[k]) for k in ("g_mix", "g_x", "g_mem", "g_ffn", "g_final"))
    conv_b, ln_g, ln_b = row(sp["conv_b"]), row(sp["ln_g"]), row(sp["ln_b"])
    conv_w = jnp.pad(sp["conv_w"].astype(F32), ((0, HALO - CONV_K), (0, 0)))
    b_f = jnp.pad(row(sp["b_f"]), ((0, 0), (0, LANES - FOX_HEADS)))
    n_main = 2 * CONV_CH + 3 * FOX_W
    w_main = w_in_full[:, :n_main]
    w_f = jnp.pad(w_in_full[:, n_main:], ((0, 0), (0, LANES - FOX_HEADS)))

    x2d = x.reshape(T, D)
    h, h_t = rmsnorm_fwd(x2d, g_mix, name="rms_mix")
    z = matmul(h, w_main, out_dtype=BF16, tn=n_main, name="mm_in")
    z3 = z.reshape(B, S, n_main)
    conv_out, conv_t = conv_branch_fwd(z3, conv_w, conv_b, ln_g, ln_b, name="conv_fwd")
    f_raw, c_col, c_row = fgate_fwd(h.reshape(B, S, D), w_f, b_f, name="fgate_fwd")
    (att, lse, att_t), partly = fox_fwd(z3, c_col, c_row, name="fox_fwd", rider=AllGatherStage1(late_shards))
    gathered = all_gather_stage2(partly, name="ag_late_stage2")
    wf = {n: _full_from_gathered(n, blk) for n, blk in zip(LATE, gathered)}
    cat =jnp.concatenate([conv_out, att], axis=-1).reshape(T, D)
    x1 = matmul(cat, wf["w_out"], out_dtype=F32, res=x2d, tn=D, name="mm_out")
    hx, hx_t = rmsnorm_fwd(x1, g_x, name="rms_x")
    qm = matmul(hx, wf["w_mq"], out_dtype=BF16, tn=D, name="mm_mq")
    mem2d = mem.reshape(B * M, D)
    mem_n, mem_n_t = rmsnorm_fwd(mem2d, g_mem, name="rms_mem")
    kv = matmul(mem_n, wf["w_mkv"], out_dtype=BF16, tn=2 * D, name="mm_mkv").reshape(B, M, 2 * D)
    o, o_t = xattn_fwd(qm.reshape(B, S, D), kv, name="xattn_fwd")
    o = o.reshape(T, D)
    x2 = matmul(o, wf["w_mo"], out_dtype=F32, res=x1, tn=D, name="mm_mo")
    hf, hf_t = rmsnorm_fwd(x2, g_ffn, name="rms_ffn")
    gu = matmul(hf, wf["w_gu"], out_dtype=BF16, tn=2816, name="mm_gu")
    act, act_t = swiglu_fwd(gu, name="swiglu_fwd")
    x3 = matmul(act, wf["w_down"], out_dtype=F32, res=x2, tn=D, name="mm_down")
    dx3, dg_final, loss = final_loss_bwd(x3, g_final, target.reshape(T, D), name="loss_bwd")
    gw = {}
    gw["w_down"] = matmul(act_t, dx3, out_dtype=BF16, tm=1408, tn=256, name="dw_down")
    dact = matmul(dx3, wf["w_down"], tb=True, out_dtype=BF16, tn=2816, name="dx_down")
    dgu = swiglu_bwd(gu, dact, name="swiglu_bwd")
    gw["w_gu"] = matmul(hf_t, dgu, out_dtype=BF16, tn=1408, name="dw_gu")
    dhf = matmul(dgu, wf["w_gu"], tb=True, out_dtype=BF16, tm=256, tn=D, name="dx_gu")
    dx2, dg_ffn = rmsnorm_bwd(x2, g_ffn, dhf, dx3, name="rms_ffn_bwd")
    gw["w_mo"] = matmul(o_t, dx2, out_dtype=BF16, name="dw_mo")
    do = matmul(dx2, wf["w_mo"], tb=True, out_dtype=BF16, tn=D, name="dx_mo")
    dqm, dkv = xattn_bwd(qm.reshape(B, S, D), kv, do.reshape(B, S, D), name="xattn_bwd")
    dqm = dqm.reshape(T, D)
    dkv = dkv.reshape(B * M, 2 * D)
    gw["w_mq"] = matmul(hx_t, dqm, out_dtype=BF16, tn=D, name="dw_mq")
    dhx = matmul(dqm, wf["w_mq"], tb=True, out_dtype=BF16, tn=D, name="dx_mq")
    gw["w_mkv"] = matmul(mem_n_t, dkv, out_dtype=BF16, tn=D, name="dw_mkv")
    dmem_n = matmul(dkv, wf["w_mkv"], tb=True, out_dtype=BF16, tn=D, name="dx_mkv")
    _, dg_mem = rmsnorm_bwd(mem2d, g_mem, dmem_n, None, name="rms_mem_bwd")
    dx1, dg_x = rmsnorm_bwd(x1, g_x, dhx, dx2, name="rms_x_bwd")
    gw["w_out"] = jnp.concatenate([matmul(conv_t, dx1, out_dtype=BF16, name="dw_out_conv"),
                                   matmul(att_t, dx1, out_dtype=BF16, name="dw_out_att")], axis=0)
    dcat = matmul(dx1, wf["w_out"], tb=True, out_dtype=BF16, tn=D, name="dx_out").reshape(B, S, D)
    dy, dconv_w, dvec = conv_branch_bwd_a(z3, dcat, conv_w, conv_b, ln_g, ln_b, name="conv_bwd_a")
    dug = conv_branch_bwd_b(z3, dy, conv_w, name="conv_bwd_b")
    parts, gots = {}, {}
    for n, p in zip(RS_GROUPS[0], reduce_to_chips(RS_GROUPS[0], gw, tag="ffn")):
        parts[n] = p
    for n, p in zip(RS_GROUPS[1], reduce_to_chips(RS_GROUPS[1], gw, tag="mid")):
        parts[n] = p
    (dq, stats), got = fox_bwd_dq(z3, dcat, lse, c_col, c_row, name="fox_bwd_dq",
                                  rider=ChipExchange([parts[n] for n in RS_GROUPS[0]]))
    gots.update(zip(RS_GROUPS[0], got))
    (dk, dv, dc), got = fox_bwd_dkdv(z3, dcat, stats, c_col, name="fox_bwd_dkdv",
                                     rider=ChipExchange([parts[n] for n in RS_GROUPS[1]]))
    gots.update(zip(RS_GROUPS[1], got))
    df, db_f = fgate_bwd(dc, f_raw, b_f, name="fgate_bwd")
    dz = jnp.concatenate([dug, dq, dk, dv], axis=-1).reshape(T, n_main)
    df2 = df.reshape(T, LANES)
    dw_main = matmul(h_t, dz, out_dtype=BF16, tn=1280, name="dw_in")
    dw_f = matmul(h_t, df2, out_dtype=BF16, name="dw_f")
    gw["w_in"] = jnp.concatenate([dw_main, dw_f[:, :FOX_HEADS]], axis=-1)
    dh_f = matmul(df2, w_f, tb=True, out_dtype=F32, tn=D, name="dx_f")
    dh = matmul(dz, w_main, tb=True, out_dtype=F32, res=dh_f, tn=D, name="dx_in")
    parts["w_in"] = reduce_to_chips(RS_GROUPS[2], gw, tag="in")[0]
    gots["w_in"] = chip_exchange([parts["w_in"]], name="rs_chips_in")[0]
    dx, dg_mix = rmsnorm_bwd(x2d, g_mix, dh, dx1, name="rms_mix_bwd")
    gs = dict(g_mix=dg_mix, b_f=db_f[:, :FOX_HEADS], conv_w=dconv_w[:CONV_K], conv_b=dvec[0:1],
              ln_g=dvec[1:2], ln_b=dvec[2:3], g_x=dg_x, g_mem=dg_mem, g_ffn=dg_ffn, g_final=dg_final)
    return loss, dx.reshape(B, S, D), gs, {n: (parts[n], gots[n]) for n in BIG}


def _me():
    return lax.axis_index("x"), lax.axis_index("y"), lax.axis_index("c")


def _any_specs(n):
    return [pl.BlockSpec(memory_space=pl.ANY)] * n


def all_gather(xs, *, name):
    n = len(xs)

    def body(*refs):
        x_refs, out_refs = refs[:n], refs[n:2 * n]
        send_sems, recv_sems, local_sems = refs[2 * n:]
        x, y, c = _me()
        me, sibling = (x, y, c), (x, y, 1 - c)
        chips = [(1 - x, y), (x, 1 - y), (1 - x, 1 - y)]

        def slot(a, px, py, pc):
            return out_refs[a].at[4 * px + 2 * py + pc]

        def copy(a, k, block, to, own=False):
            return pltpu.make_async_remote_copy(
                src_ref=x_refs[a] if own else slot(a, *block), dst_ref=slot(a, *block),
                send_sem=send_sems.at[k, a], recv_sem=recv_sems.at[k, a], device_id=to, device_id_type=MESH)

        mine = [pltpu.make_async_copy(x_refs[a], slot(a, *me), local_sems.at[a]) for a in range(n)]
        first = [copy(a, 0, me, sibling, own=True) for a in range(n)]
        first += [copy(a, 1 + j, me, (*chip, c), own=True) for j, chip in enumerate(chips) for a in range(n)]
        for cp in mine + first:
            cp.start()
        passed = []
        for j, chip in enumerate(chips):
            for a in range(n):
                copy(a, 1 + j, (*chip, c), me).wait_recv()
                passed.append(copy(a, 4 + j, (*chip, c), sibling))
                passed[-1].start()
        for a in range(n):
            copy(a, 0, sibling, me).wait_recv()
            for j, chip in enumerate(chips):
                copy(a, 4 + j, (*chip, 1 - c), me).wait_recv()
        for cp in first + passed:
            cp.wait_send()
        for cp in mine:
            cp.wait()

    return _call(
        body, name=name, in_specs=_any_specs(n), out_specs=_any_specs(n),
        out_shape=[jax.ShapeDtypeStruct((N_DEV,) + v.shape, v.dtype) for v in xs],
        scratch_shapes=[pltpu.SemaphoreType.DMA((7, n)), pltpu.SemaphoreType.DMA((7, n)),
                        pltpu.SemaphoreType.DMA((n,))],
    )(*xs)


def sibling_exchange(gs, *, name):
    n = len(gs)

    def body(*refs):
        g_refs, out_refs = refs[:n], refs[n:2 * n]
        send_sems, recv_sems = refs[2 * n:]
        x, y, c = _me()
        cps = [pltpu.make_async_remote_copy(
            src_ref=g_refs[a].at[:, 1 - c], dst_ref=out_refs[a], send_sem=send_sems.at[a],
            recv_sem=recv_sems.at[a], device_id=(x, y, 1 - c), device_id_type=MESH) for a in range(n)]
        for cp in cps:
            cp.start()
        for cp in cps:
            cp.wait()

    return _call(
        body, name=name, in_specs=_any_specs(n), out_specs=_any_specs(n),
        out_shape=[jax.ShapeDtypeStruct((4,) + g.shape[2:], g.dtype) for g in gs],
        scratch_shapes=[pltpu.SemaphoreType.DMA((n,)), pltpu.SemaphoreType.DMA((n,))],
    )(*gs)


def chip_exchange(ps, *, name):
    return hosted_call(None, ChipExchange(ps), name=name, grid=(), in_specs=[], out_specs=[], out_shape=[],
                       scratch_shapes=[], args=[])[1]


class ChipExchange:
    def __init__(self, ps):
        n = len(ps)
        self.n, self.inputs = n, list(ps)
        self.out_shape = [jax.ShapeDtypeStruct(p.shape, p.dtype) for p in ps]
        self.scratch = [pltpu.SemaphoreType.DMA((3, n)), pltpu.SemaphoreType.DMA((3, n))]

    def _copies(self, p_refs, out_refs, sems, outgoing):
        send_sems, recv_sems = sems
        x, y, c = _me()
        my_chip = 2 * x + y
        cps = []
        for k in range(3):
            px, py = x ^ ((k + 1) >> 1), y ^ ((k + 1) & 1)
            src, dst = (2 * px + py, my_chip) if outgoing else (my_chip, 2 * px + py)
            for a in range(self.n):
                cps.append(pltpu.make_async_remote_copy(
                    src_ref=p_refs[a].at[src], dst_ref=out_refs[a].at[dst], send_sem=send_sems.at[k, a],
                    recv_sem=recv_sems.at[k, a], device_id=(px, py, c), device_id_type=MESH))
        return cps

    def start(self, in_refs, out_refs, sems):
        for cp in self._copies(in_refs, out_refs, sems, True):
            cp.start()

    def finish(self, in_refs, out_refs, sems):
        for cp in self._copies(in_refs, out_refs, sems, False):
            cp.wait_recv()
        for cp in self._copies(in_refs, out_refs, sems, True):
            cp.wait_send()


class AllGatherStage1:
    def __init__(self, xs):
        n = len(xs)
        self.n, self.inputs = n, list(xs)
        self.out_shape = [jax.ShapeDtypeStruct((N_DEV,) + v.shape, v.dtype) for v in xs]
        self.scratch = [pltpu.SemaphoreType.DMA((4, n)), pltpu.SemaphoreType.DMA((4, n)),
                        pltpu.SemaphoreType.DMA((n,))]

    def _copies(self, x_refs, out_refs, sems, kind):
        send_sems, recv_sems, local_sems = sems
        x, y, c = _me()
        slot = lambda a, d: out_refs[a].at[4 * d[0] + 2 * d[1] + d[2]]
        if kind == "local":
            return [pltpu.make_async_copy(x_refs[a], slot(a, (x, y, c)), local_sems.at[a]) for a in range(self.n)]
        cps = []
        for k, peer in enumerate([(x, y, 1 - c), (1 - x, y, c), (x, 1 - y, c), (1 - x, 1 - y, c)]):
            for a in range(self.n):
                cps.append(pltpu.make_async_remote_copy(
                    src_ref=x_refs[a], dst_ref=slot(a, (x, y, c) if kind == "out" else peer),
                    send_sem=send_sems.at[k, a], recv_sem=recv_sems.at[k, a], device_id=peer, device_id_type=MESH))
        return cps

    def start(self, in_refs, out_refs, sems):
        for cp in self._copies(in_refs, out_refs, sems, "local") + self._copies(in_refs, out_refs, sems, "out"):
            cp.start()

    def finish(self, in_refs, out_refs, sems):
        for cp in self._copies(in_refs, out_refs, sems, "in"):
            cp.wait_recv()
        for cp in self._copies(in_refs, out_refs, sems, "out"):
            cp.wait_send()
        for cp in self._copies(in_refs, out_refs, sems, "local"):
            cp.wait()


def all_gather_stage2(outs, *, name):
    n = len(outs)

    def body(*refs):
        out_refs = refs[n:2 * n]
        send_sems, recv_sems = refs[2 * n:]
        x, y, c = _me()
        sends, recvs = [], []
        for k, (px, py) in enumerate([(1 - x, y), (x, 1 - y), (1 - x, 1 - y)]):
            for a in range(n):
                mk = lambda pc: pltpu.make_async_remote_copy(
                    src_ref=out_refs[a].at[4 * px + 2 * py + c], dst_ref=out_refs[a].at[4 * px + 2 * py + pc],
                    send_sem=send_sems.at[k, a], recv_sem=recv_sems.at[k, a], device_id=(x, y, 1 - c),
                    device_id_type=MESH)
                sends.append(mk(c))
                recvs.append(mk(1 - c))
        for cp in sends:
            cp.start()
        for cp in recvs:
            cp.wait_recv()
        for cp in sends:
            cp.wait_send()

    return _call(
        body, name=name, in_specs=_any_specs(n), out_specs=_any_specs(n),
        out_shape=[jax.ShapeDtypeStruct(o.shape, o.dtype) for o in outs],
        input_output_aliases={a: a for a in range(n)},
        scratch_shapes=[pltpu.SemaphoreType.DMA((3, n)), pltpu.SemaphoreType.DMA((3, n))],
    )(*outs)


def hosted_call(body, rider, *, name, grid, in_specs, out_specs, out_shape, scratch_shapes, args, vmem=None):
    n_in, n_out, n_scr = len(in_specs), len(out_specs), len(scratch_shapes)
    r_in, r_out = (len(rider.inputs), len(rider.out_shape)) if rider is not None else (0, 0)

    def wrapped(*refs):
        ins, refs = refs[:n_in], refs[n_in:]
        rins, refs = refs[:r_in], refs[r_in:]
        outs, refs = refs[:n_out], refs[n_out:]
        routs, refs = refs[:r_out], refs[r_out:]
        scr, rscr = refs[:n_scr], refs[n_scr:]
        ids = [pl.program_id(d) for d in range(len(grid))]
        first = functools.reduce(jnp.logical_and, [i == 0 for i in ids], True)
        last = functools.reduce(jnp.logical_and, [i == g - 1 for i, g in zip(ids, grid)], True)
        if rider is not None and grid:
            pl.when(first)(lambda: rider.start(rins, routs, rscr))
        elif rider is not None:
            rider.start(rins, routs, rscr)
        if body is not None:
            body(*ins, *outs, *scr)
        if rider is not None and grid:
            pl.when(last)(lambda: rider.finish(rins, routs, rscr))
        elif rider is not None:
            rider.finish(rins, routs, rscr)

    kw = dict(grid=grid) if grid else {}
    if grid or vmem is not None:
        kw["compiler_params"] = _params(("arbitrary",) * len(grid) if grid else None, vmem)
    res = _call(
        wrapped, name=name, in_specs=list(in_specs) + _any_specs(r_in), out_specs=list(out_specs) + _any_specs(r_out),
        out_shape=list(out_shape) + (rider.out_shape if rider is not None else []),
        scratch_shapes=list(scratch_shapes) + (rider.scratch if rider is not None else []), **kw,
    )(*args, *(rider.inputs if rider is not None else []))
    return list(res[:n_out]), list(res[n_out:])


def _pick_rows(r, target=256):
    best = None
    for d in range(16, min(r, target) + 1, 16):
        if r % d == 0:
            best = d
    return r if best is None else best


def pair_sum(g, got, *, name):
    _, _, R, C = g.shape
    tr = _pick_rows(R)

    def body(g_ref, got_ref, o_ref):
        mine = jnp.where(lax.axis_index("c") == 0, g_ref[:, 0], g_ref[:, 1])
        o_ref[...] = (mine.astype(F32) + got_ref[...].astype(F32)).astype(o_ref.dtype)

    return _call(
        body, name=name, grid=(R // tr,),
        in_specs=[pl.BlockSpec((4, 2, tr, C), lambda i: (0, 0, i, 0)), pl.BlockSpec((4, tr, C), lambda i: (0, i, 0))],
        out_specs=pl.BlockSpec((4, tr, C), lambda i: (0, i, 0)),
        out_shape=jax.ShapeDtypeStruct((4, R, C), g.dtype),
        compiler_params=_params(("parallel",)),
    )(g, got)


def chip_sum_adamw(p, got, w, m, v, *, name):
    _, R, C = p.shape
    tr = _pick_rows(R)

    def body(p_ref, got_ref, w_ref, m_ref, v_ref, g_ref, d_ref, mo_ref, vo_ref):
        my_chip = 2 * lax.axis_index("x") + lax.axis_index("y")
        g = jnp.zeros((tr, C), F32)
        for j in range(4):
            g = g + jnp.where(my_chip == j, p_ref[j], got_ref[j]).astype(F32)
        g_ref[...] = g
        d_ref[...], mo_ref[...], vo_ref[...] = _adamw_math(w_ref[...], g, m_ref[...], v_ref[...])

    part = pl.BlockSpec((4, tr, C), lambda i: (0, i, 0))
    spec = pl.BlockSpec((tr, C), lambda i: (i, 0))
    return _call(
        body, name=name, grid=(R // tr,), in_specs=[part, part, spec, spec, spec], out_specs=[spec] * 4,
        out_shape=[jax.ShapeDtypeStruct((R, C), F32)] * 4,
        compiler_params=_params(("parallel",)),
    )(p, got, w, m, v)


def rows_sum(g8, *, name):
    _, R, C = g8.shape

    def body(g_ref, o_ref):
        acc = g_ref[0]
        for j in range(1, N_DEV):
            acc = acc + g_ref[j]
        o_ref[...] = acc

    return _call(body, name=name, out_shape=jax.ShapeDtypeStruct((R, C), F32))(g8)


def _adamw_math(w, g, m, v):
    m = ADAM_B1 * m + (1.0 - ADAM_B1) * g
    v = ADAM_B2 * v + (1.0 - ADAM_B2) * (g * g)
    m_hat = m / (1.0 - ADAM_B1 ** ADAM_STEP)
    v_hat = v / (1.0 - ADAM_B2 ** ADAM_STEP)
    delta = -ADAM_LR * (m_hat / (jnp.sqrt(v_hat) + ADAM_EPS) + ADAM_WD * w)
    return delta, m, v


def adamw_small(wgmv, *, name):
    n = len(wgmv)

    def body(*refs):
        ins, outs = refs[:4 * n], refs[4 * n:]
        for a in range(n):
            w_ref, g_ref, m_ref, v_ref = ins[4 * a:4 * a + 4]
            d, mn, vn = _adamw_math(w_ref[...], g_ref[...], m_ref[...], v_ref[...])
            outs[3 * a][...] = d
            outs[3 * a + 1][...] = mn
            outs[3 * a + 2][...] = vn

    flat = [t for tup in wgmv for t in tup]
    res = _call(
        body, name=name,
        out_shape=[jax.ShapeDtypeStruct(tup[0].shape, F32) for tup in wgmv for _ in range(3)],
    )(*flat)
    return [tuple(res[3 * a:3 * a + 3]) for a in range(n)]


BIG = ("w_in", "w_out", "w_mq", "w_mkv", "w_mo", "w_gu", "w_down")
COL_SHARDED = ("w_in", "w_mkv", "w_gu")
SMALL = ("g_mix", "b_f", "conv_w", "conv_b", "ln_g", "ln_b", "g_x", "g_mem", "g_ffn", "g_final")


def _full_from_gathered(n, blk):
    _, rr, cc = blk.shape
    if n in COL_SHARDED:
        return jnp.concatenate([blk[k] for k in range(N_DEV)], axis=1)
    return blk.reshape(N_DEV * rr, cc)


def _shards_from_full(n, g):
    rr, cc = g.shape
    if n in COL_SHARDED:
        w = cc // N_DEV
        return jnp.stack([g[:, k * w:(k + 1) * w] for k in range(N_DEV)]).reshape(4, 2, rr, w)
    return g.reshape(4, 2, rr // N_DEV, cc)


def _small_layout():
    sizes = dict(g_mix=1024, b_f=8, conv_w=CONV_K * CONV_CH, conv_b=512, ln_g=512, ln_b=512, g_x=1024,
                 g_mem=1024, g_ffn=1024, g_final=1024, loss=1)
    lay, r0 = {}, 0
    for n, sz in sizes.items():
        r = -(-sz // LANES)
        lay[n] = (r0, r, sz)
        r0 += r
    return lay, -(-r0 // 8) * 8


def kernel(x, mem, g_mix, w_in, b_f, conv_w, conv_b, ln_g, ln_b, w_out, g_x, g_mem, w_mq, w_mkv, w_mo, g_ffn, w_gu, w_down, g_final, loss_target, m_g_mix, m_w_in, m_b_f, m_conv_w, m_conv_b, m_ln_g, m_ln_b, m_w_out, m_g_x, m_g_mem, m_w_mq, m_w_mkv, m_w_mo, m_g_ffn, m_w_gu, m_w_down, m_g_final, v_g_mix, v_w_in, v_b_f, v_conv_w, v_conv_b, v_ln_g, v_ln_b, v_w_out, v_g_x, v_g_mem, v_w_mq, v_w_mkv, v_w_mo, v_g_ffn, v_w_gu, v_w_down, v_g_final):
    names = ["g_mix", "w_in", "b_f", "conv_w", "conv_b", "ln_g", "ln_b", "w_out", "g_x", "g_mem", "w_mq",
             "w_mkv", "w_mo", "g_ffn", "w_gu", "w_down", "g_final"]
    W = dict(zip(names, [g_mix, w_in, b_f, conv_w, conv_b, ln_g, ln_b, w_out, g_x, g_mem, w_mq, w_mkv, w_mo,
                         g_ffn, w_gu, w_down, g_final]))
    Mo = dict(zip(names, [m_g_mix, m_w_in, m_b_f, m_conv_w, m_conv_b, m_ln_g, m_ln_b, m_w_out, m_g_x, m_g_mem,
                          m_w_mq, m_w_mkv, m_w_mo, m_g_ffn, m_w_gu, m_w_down, m_g_final]))
    Vo = dict(zip(names, [v_g_mix, v_w_in, v_b_f, v_conv_w, v_conv_b, v_ln_g, v_ln_b, v_w_out, v_g_x, v_g_mem,
                          v_w_mq, v_w_mkv, v_w_mo, v_g_ffn, v_w_gu, v_w_down, v_g_final]))
    dev = 4 * lax.axis_index("x") + 2 * lax.axis_index("y") + lax.axis_index("c")

    two = lambda a: a.reshape(-1, a.shape[-1])
    cw_shard = jnp.pad(two(conv_w), ((0, HALO - CONV_K), (0, 0)))
    w_in8, cw8 = all_gather([two(w_in).astype(BF16), cw_shard], name="ag_first")
    cw_full = cw8.transpose(1, 0, 2).reshape(HALO, -1)[:CONV_K]

    sp = dict(g_mix=g_mix, b_f=b_f, conv_w=cw_full, conv_b=conv_b, ln_g=ln_g, ln_b=ln_b, g_x=g_x, g_mem=g_mem,
              g_ffn=g_ffn, g_final=g_final)
    loss_blk, grad_x, gs, reduced = local_step(x, mem, loss_target, sp, _full_from_gathered("w_in", w_in8),
                                               [two(W[n]).astype(BF16) for n in LATE])

    lay, rs = _small_layout()
    small = {**{n: gs[n] for n in SMALL}, "loss": loss_blk[:, :1]}
    parts = []
    for n, (r0, r, sz) in lay.items():
        flat = small[n].reshape(-1).astype(F32)
        parts.append(jnp.pad(flat, (0, r * LANES - sz)).reshape(r, LANES))
    spack = jnp.concatenate(parts, axis=0)
    spack = jnp.pad(spack, ((0, rs - spack.shape[0]), (0, 0)))
    ssum = rows_sum(all_gather([spack], name="ag_small")[0], name="small_sum")
    gsmall = {n: ssum[r0:r0 + r].reshape(-1)[:sz] for n, (r0, r, sz) in lay.items()}
    loss = gsmall["loss"].reshape(())

    grads, delta, new_m, new_v = {}, {}, {}, {}
    for n in BIG:
        p, o = reduced[n]
        shp = W[n].shape
        g, d, mn, vn = chip_sum_adamw(p, o, two(W[n]), two(Mo[n]), two(Vo[n]), name="adamw_" + n)
        grads[n], delta[n], new_m[n], new_v[n] = g.reshape(shp), d.reshape(shp), mn.reshape(shp), vn.reshape(shp)
    for n in SMALL:
        if n == "conv_w":
            full = gsmall[n].reshape(CONV_K, CONV_CH)
            ncol = conv_w.shape[-1]
            grads[n] = lax.dynamic_slice(full, (0, dev * ncol), (CONV_K, ncol)).reshape(conv_w.shape)
        else:
            grads[n] = gsmall[n].reshape(W[n].shape)
    upd = adamw_small([(two(W[n]), two(grads[n]), two(Mo[n]), two(Vo[n])) for n in SMALL], name="adamw_small")
    for n, (d, mn, vn) in zip(SMALL, upd):
        shp = W[n].shape
        delta[n], new_m[n], new_v[n] = d.reshape(shp), mn.reshape(shp), vn.reshape(shp)
    return (loss, grad_x, *[grads[n] for n in names], *[delta[n] for n in names],
            *[new_m[n] for n in names], *[new_v[n] for n in names])
```

```python
import functools
import math

import jax
import jax.numpy as jnp
from jax import lax
from jax.experimental import pallas as pl
from jax.experimental.pallas import tpu as pltpu

F32 = jnp.float32
BF16 = jnp.bfloat16
EPS = 1e-6
N_DEV = 8
CONV_CH = 512
CONV_K = 31
FOX_HEADS = 8
FOX_HEAD_DIM = 64
FOX_W = 512
MEM_HEADS = 4
MEM_HEAD_DIM = 256
HALO = 32
LANES = 128
ADAM_LR, ADAM_B1, ADAM_B2, ADAM_EPS, ADAM_WD, ADAM_STEP = 0.001, 0.9, 0.999, 1e-08, 0.01, 10
NEG = -1e30
VMEM_CAP = 60 * 1024 * 1024
MESH = pl.DeviceIdType.MESH


def _call(body, **kw):
    call = pl.pallas_call(body, **kw)
    return lambda *args: call(*[pltpu.with_memory_space_constraint(a, pltpu.HBM) for a in args])


def _params(sem=None, vmem=None):
    kw = {}
    if sem is not None:
        kw["dimension_semantics"] = sem
    if vmem is not None:
        kw["vmem_limit_bytes"] = int(min(VMEM_CAP, vmem))
    return pltpu.CompilerParams(**kw)


def _nbytes(shape, dtype):
    return math.prod(shape) * jnp.dtype(dtype).itemsize


def _pick(n, target):
    best = None
    for d in range(LANES, min(n, target) + 1, LANES):
        if n % d == 0:
            best = d
    return n if best is None else best


def matmul(a, b, *, tb=False, out_dtype, res=None, tm=512, tn=512, tk=None, name, rider=None):
    M, K = a.shape
    N = b.shape[0] if tb else b.shape[1]
    assert (b.shape[1] if tb else b.shape[0]) == K
    tm, tn = _pick(M, tm), _pick(N, tn)
    tk = K if tk is None else _pick(K, tk)
    assert M % tm == 0 and N % tn == 0 and K % tk == 0, (name, M, N, K, tm, tn, tk)
    nk = K // tk
    dn = (((1,), (1 if tb else 0,)), ((), ()))

    def body(*refs):
        if res is not None:
            a_ref, b_ref, r_ref, o_ref = refs[:4]
        else:
            a_ref, b_ref, o_ref = refs[:3]
        p = lax.dot_general(a_ref[...].astype(BF16), b_ref[...].astype(BF16), dn,
                            preferred_element_type=F32)

        def finish(acc):
            if res is not None:
                acc = acc + r_ref[...].astype(F32)
            o_ref[...] = acc.astype(out_dtype)

        if nk == 1:
            finish(p)
        else:
            acc_ref = refs[-1]
            k = pl.program_id(2)

            @pl.when(k == 0)
            def _():
                acc_ref[...] = p

            @pl.when(k > 0)
            def _():
                acc_ref[...] += p

            @pl.when(k == nk - 1)
            def _():
                finish(acc_ref[...])

    a_spec = pl.BlockSpec((tm, tk), lambda i, j, k: (i, k))
    b_spec = pl.BlockSpec((tn, tk), lambda i, j, k: (j, k)) if tb else pl.BlockSpec((tk, tn), lambda i, j, k: (k, j))
    o_spec = pl.BlockSpec((tm, tn), lambda i, j, k: (i, j))
    in_specs, args = [a_spec, b_spec], [a, b]
    est = 2 * (_nbytes((tm, tk), a.dtype) + _nbytes((tk, tn), b.dtype) + _nbytes((tm, tn), out_dtype))
    est += (a.dtype != BF16) * _nbytes((tm, tk), BF16) + (b.dtype != BF16) * _nbytes((tk, tn), BF16)
    est += 2 * _nbytes((tm, tn), F32)
    if res is not None:
        in_specs.append(o_spec)
        args.append(res)
        est += 2 * _nbytes((tm, tn), res.dtype)
    (out,), rode = hosted_call(
        body, rider, name=name, grid=(M // tm, N // tn, nk),
        in_specs=in_specs, out_specs=[o_spec],
        out_shape=[jax.ShapeDtypeStruct((M, N), out_dtype)],
        scratch_shapes=[] if nk == 1 else [pltpu.VMEM((tm, tn), F32)],
        args=args, vmem=est + (8 << 20),
    )
    return out if rider is None else (out, rode)


def _rms_scale(x):
    return lax.rsqrt(jnp.mean(x * x, axis=-1, keepdims=True) + EPS)


def rmsnorm_fwd(x, g, *, name, tm=512):
    T, D = x.shape
    tm = min(tm, T)

    def body(x_ref, g_ref, o_ref, ot_ref):
        xv = x_ref[...]
        h = xv * _rms_scale(xv) * g_ref[...]
        o_ref[...] = h.astype(BF16)
        ot_ref[...] = h.T.astype(BF16)

    return _call(
        body, name=name, grid=(T // tm,),
        in_specs=[pl.BlockSpec((tm, D), lambda i: (i, 0)), pl.BlockSpec((1, D), lambda i: (0, 0))],
        out_specs=[pl.BlockSpec((tm, D), lambda i: (i, 0)), pl.BlockSpec((D, tm), lambda i: (0, i))],
        out_shape=[jax.ShapeDtypeStruct((T, D), BF16), jax.ShapeDtypeStruct((D, T), BF16)],
        compiler_params=_params(("parallel",)),
    )(x, g)


def _rms_bwd_math(xv, gv, dh):
    r = _rms_scale(xv)
    xh = xv * r
    dg = jnp.sum(dh * xh, axis=0, keepdims=True)
    dxh = dh * gv
    dx = r * (dxh - xh * jnp.mean(dxh * xh, axis=-1, keepdims=True))
    return dx, dg


def rmsnorm_bwd(x, g, dh, dres, *, name, tm=256):
    T, D = x.shape
    tm = min(tm, T)

    def body(*refs):
        if dres is not None:
            x_ref, g_ref, dh_ref, dr_ref, dx_ref, dg_ref = refs
        else:
            x_ref, g_ref, dh_ref, dx_ref, dg_ref = refs
        dx, dg = _rms_bwd_math(x_ref[...], g_ref[...], dh_ref[...].astype(F32))
        if dres is not None:
            dx = dx + dr_ref[...]
        dx_ref[...] = dx

        @pl.when(pl.program_id(0) == 0)
        def _():
            dg_ref[...] = jnp.zeros_like(dg_ref)

        dg_ref[...] += dg

    row = pl.BlockSpec((tm, D), lambda i: (i, 0))
    vec = pl.BlockSpec((1, D), lambda i: (0, 0))
    ins, args = [row, vec, row], [x, g, dh]
    if dres is not None:
        ins.append(row)
        args.append(dres)
    return _call(
        body, name=name, grid=(T // tm,), in_specs=ins, out_specs=[row, vec],
        out_shape=[jax.ShapeDtypeStruct((T, D), F32), jax.ShapeDtypeStruct((1, D), F32)],
        compiler_params=_params(("arbitrary",)),
    )(*args)


def final_loss_bwd(x, g, target, *, name, tm=256):
    T, D = x.shape
    tm = min(tm, T)

    def body(x_ref, g_ref, t_ref, dx_ref, dg_ref, l_ref):
        xv, gv = x_ref[...], g_ref[...]
        e = xv * _rms_scale(xv) * gv - t_ref[...]
        part = 0.5 * jnp.sum(jnp.mean(e * e, axis=-1, keepdims=True), axis=0, keepdims=True)
        dx, dg = _rms_bwd_math(xv, gv, e * (1.0 / D))
        dx_ref[...] = dx

        @pl.when(pl.program_id(0) == 0)
        def _():
            dg_ref[...] = jnp.zeros_like(dg_ref)
            l_ref[...] = jnp.zeros_like(l_ref)

        dg_ref[...] += dg
        l_ref[...] += jnp.broadcast_to(part, l_ref.shape)

    row = pl.BlockSpec((tm, D), lambda i: (i, 0))
    vec = pl.BlockSpec((1, D), lambda i: (0, 0))
    return _call(
        body, name=name, grid=(T // tm,), in_specs=[row, vec, row],
        out_specs=[row, vec, pl.BlockSpec((1, LANES), lambda i: (0, 0))],
        out_shape=[jax.ShapeDtypeStruct((T, D), F32), jax.ShapeDtypeStruct((1, D), F32),
                   jax.ShapeDtypeStruct((1, LANES), F32)],
        compiler_params=_params(("arbitrary",)),
    )(x, g, target)


def _sigmoid(v):
    return 1.0 / (1.0 + jnp.exp(-v))


def _glu(blk):
    u = blk[:, :CONV_CH].astype(F32)
    gt = blk[:, CONV_CH:].astype(F32)
    return u * _sigmoid(gt)


def _fill_causal_ext(ext, cur_ref, halo_ref, s, ts):
    ext[pl.ds(HALO, ts), :] = _glu(cur_ref[0])
    hal = _glu(halo_ref[0])
    ext[pl.ds(0, HALO), :] = jnp.where(s > 0, hal, 0.0)


SUBLANES = 8


def _make_shifted(ext, sh):
    n = ext.shape[0]
    full = ext[...]
    for r in range(1, SUBLANES):
        sh[r - 1] = pltpu.roll(full, n - r, 0)


def _tap(ext, sh, off, ts):
    r = off % SUBLANES
    return ext[pl.ds(off, ts), :] if r == 0 else sh[r - 1, pl.ds(off - r, ts), :]


def _causal_conv(ext, sh, w_ref, ts):
    acc = jnp.zeros((ts, CONV_CH), F32)
    for j in range(CONV_K):
        acc = acc + _tap(ext, sh, HALO - (CONV_K - 1) + j, ts) * w_ref[pl.ds(j, 1), :]
    return acc


def _ln_stats(y):
    mu = jnp.mean(y, axis=-1, keepdims=True)
    yc = y - mu
    rstd = lax.rsqrt(jnp.mean(yc * yc, axis=-1, keepdims=True) + EPS)
    return yc * rstd, rstd


def _conv_specs(ts, S):
    nh = ts // HALO
    cur = pl.BlockSpec((1, ts, 2 * CONV_CH), lambda b, s: (b, s, 0))
    halo = pl.BlockSpec((1, HALO, 2 * CONV_CH), lambda b, s: (b, jnp.maximum(s * nh - 1, 0), 0))
    w = pl.BlockSpec((HALO, CONV_CH), lambda b, s: (0, 0))
    vec = pl.BlockSpec((1, CONV_CH), lambda b, s: (0, 0))
    return cur, halo, w, vec


def conv_branch_fwd(ug, conv_w, conv_b, ln_g, ln_b, *, name, ts=256):
    B, S, _ = ug.shape
    ts = min(ts, S)
    ns = S // ts
    cur, halo, w, vec = _conv_specs(ts, S)

    def body(cur_ref, halo_ref, w_ref, cb_ref, lg_ref, lb_ref, o_ref, ot_ref, ext, sh):
        _fill_causal_ext(ext, cur_ref, halo_ref, pl.program_id(1), ts)
        _make_shifted(ext, sh)
        y = _causal_conv(ext, sh, w_ref, ts) + cb_ref[...]
        yh, _ = _ln_stats(y)
        ln = yh * lg_ref[...] + lb_ref[...]
        out = ln * _sigmoid(ln)
        o_ref[0] = out.astype(BF16)
        ot_ref[...] = out.T.astype(BF16)

    return _call(
        body, name=name, grid=(B, ns), in_specs=[cur, halo, w, vec, vec, vec],
        out_specs=[pl.BlockSpec((1, ts, CONV_CH), lambda b, s: (b, s, 0)),
                   pl.BlockSpec((CONV_CH, ts), lambda b, s: (0, b * ns + s))],
        out_shape=[jax.ShapeDtypeStruct((B, S, CONV_CH), BF16), jax.ShapeDtypeStruct((CONV_CH, B * S), BF16)],
        scratch_shapes=[pltpu.VMEM((ts + HALO, CONV_CH), F32),
                        pltpu.VMEM((SUBLANES - 1, ts + HALO, CONV_CH), F32)],
        compiler_params=_params(("parallel", "parallel")),
    )(ug, ug, conv_w, conv_b, ln_g, ln_b)


def conv_branch_bwd_a(ug, dcat, conv_w, conv_b, ln_g, ln_b, *, name, ts=256):
    B, S, _ = ug.shape
    ts = min(ts, S)
    cur, halo, w, vec = _conv_specs(ts, S)

    def body(cur_ref, halo_ref, d_ref, w_ref, cb_ref, lg_ref, lb_ref, dy_ref, dw_ref, dv_ref, ext, sh):
        _fill_causal_ext(ext, cur_ref, halo_ref, pl.program_id(1), ts)
        _make_shifted(ext, sh)
        y = _causal_conv(ext, sh, w_ref, ts) + cb_ref[...]
        yh, rstd = _ln_stats(y)
        lg = lg_ref[...]
        ln = yh * lg + lb_ref[...]
        sg = _sigmoid(ln)
        dln = d_ref[0].astype(F32) * (sg * (1.0 + ln * (1.0 - sg)))
        dyh = dln * lg
        dy = rstd * (dyh - jnp.mean(dyh, axis=-1, keepdims=True)
                     - yh * jnp.mean(dyh * yh, axis=-1, keepdims=True))
        dy_ref[0] = dy

        @pl.when((pl.program_id(0) == 0) & (pl.program_id(1) == 0))
        def _():
            dw_ref[...] = jnp.zeros_like(dw_ref)
            dv_ref[...] = jnp.zeros_like(dv_ref)

        dv_ref[pl.ds(0, 1), :] += jnp.sum(dy, axis=0, keepdims=True)
        dv_ref[pl.ds(1, 1), :] += jnp.sum(dln * yh, axis=0, keepdims=True)
        dv_ref[pl.ds(2, 1), :] += jnp.sum(dln, axis=0, keepdims=True)
        for j in range(CONV_K):
            tap = _tap(ext, sh, HALO - (CONV_K - 1) + j, ts)
            dw_ref[pl.ds(j, 1), :] += jnp.sum(dy * tap, axis=0, keepdims=True)

    return _call(
        body, name=name, grid=(B, S // ts),
        in_specs=[cur, halo, pl.BlockSpec((1, ts, CONV_CH), lambda b, s: (b, s, 0)), w, vec, vec, vec],
        out_specs=[pl.BlockSpec((1, ts, CONV_CH), lambda b, s: (b, s, 0)),
                   pl.BlockSpec((HALO, CONV_CH), lambda b, s: (0, 0)),
                   pl.BlockSpec((8, CONV_CH), lambda b, s: (0, 0))],
        out_shape=[jax.ShapeDtypeStruct((B, S, CONV_CH), F32),
                   jax.ShapeDtypeStruct((HALO, CONV_CH), F32),
                   jax.ShapeDtypeStruct((8, CONV_CH), F32)],
        scratch_shapes=[pltpu.VMEM((ts + HALO, CONV_CH), F32),
                        pltpu.VMEM((SUBLANES - 1, ts + HALO, CONV_CH), F32)],
        compiler_params=_params(("arbitrary", "arbitrary")),
    )(ug, ug, dcat, conv_w, conv_b, ln_g, ln_b)


def conv_branch_bwd_b(ug, dy, conv_w, *, name, ts=256):
    B, S, _ = ug.shape
    ts = min(ts, S)
    nh, n_halo = ts // HALO, S // HALO

    def body(cur_ref, dy_ref, nxt_ref, w_ref, o_ref, ext, sh):
        last = pl.program_id(1) == pl.num_programs(1) - 1
        ext[pl.ds(0, ts), :] = dy_ref[0]
        ext[pl.ds(ts, HALO), :] = jnp.where(last, 0.0, nxt_ref[0])
        _make_shifted(ext, sh)
        da = jnp.zeros((ts, CONV_CH), F32)
        for j in range(CONV_K):
            da = da + _tap(ext, sh, CONV_K - 1 - j, ts) * w_ref[pl.ds(j, 1), :]
        blk = cur_ref[0]
        u = blk[:, :CONV_CH].astype(F32)
        sg = _sigmoid(blk[:, CONV_CH:].astype(F32))
        o_ref[0, :, :CONV_CH] = (da * sg).astype(BF16)
        o_ref[0, :, CONV_CH:] = (da * u * sg * (1.0 - sg)).astype(BF16)

    return _call(
        body, name=name, grid=(B, S // ts),
        in_specs=[pl.BlockSpec((1, ts, 2 * CONV_CH), lambda b, s: (b, s, 0)),
                  pl.BlockSpec((1, ts, CONV_CH), lambda b, s: (b, s, 0)),
                  pl.BlockSpec((1, HALO, CONV_CH), lambda b, s: (b, jnp.minimum((s + 1) * nh, n_halo - 1), 0)),
                  pl.BlockSpec((HALO, CONV_CH), lambda b, s: (0, 0))],
        out_specs=pl.BlockSpec((1, ts, 2 * CONV_CH), lambda b, s: (b, s, 0)),
        out_shape=jax.ShapeDtypeStruct((B, S, 2 * CONV_CH), BF16),
        scratch_shapes=[pltpu.VMEM((ts + HALO, CONV_CH), F32),
                        pltpu.VMEM((SUBLANES - 1, ts + HALO, CONV_CH), F32)],
        compiler_params=_params(("parallel", "parallel")),
    )(ug, dy, dy, conv_w)


def _tri(n, lower):
    r = lax.broadcasted_iota(jnp.int32, (n, n), 0)
    c = lax.broadcasted_iota(jnp.int32, (n, n), 1)
    return ((r >= c) if lower else (r <= c)).astype(F32)


def _eye(n):
    r = lax.broadcasted_iota(jnp.int32, (n, n), 0)
    c = lax.broadcasted_iota(jnp.int32, (n, n), 1)
    return (r == c).astype(F32)


def _dot_hi(a, b, dn):
    return lax.dot_general(a, b, dn, precision=lax.Precision.HIGHEST, preferred_element_type=F32)


NN = (((1,), (0,)), ((), ()))
NT = (((1,), (1,)), ((), ()))
TN = (((0,), (0,)), ((), ()))


def _log_sigmoid(v):
    e = jnp.exp(-jnp.abs(v))
    log1p_e = jnp.where(e < 1e-3, e * (1.0 - 0.5 * e), jnp.log(1.0 + e))
    return jnp.minimum(v, 0.0) - log1p_e


def fgate_fwd(h, w_f, b_f, *, name, ts=256):
    B, S, D = h.shape
    ts = min(ts, S)

    def body(h_ref, w_ref, b_ref, f_ref, cc_ref, cr_ref, carry):
        @pl.when(pl.program_id(1) == 0)
        def _():
            carry[...] = jnp.zeros_like(carry)

        f = jnp.dot(h_ref[0], w_ref[...], preferred_element_type=F32)
        f_ref[0] = f
        logf = _log_sigmoid(f + b_ref[...])
        c = _dot_hi(_tri(ts, True), logf, NN) + carry[pl.ds(0, 1), :]
        cc_ref[0] = c
        carry[pl.ds(0, 1), :] = c[ts - 1:ts, :]
        cr_ref[0] = _dot_hi(_eye(LANES), c, NT)

    return _call(
        body, name=name, grid=(B, S // ts),
        in_specs=[pl.BlockSpec((1, ts, D), lambda b, s: (b, s, 0)),
                  pl.BlockSpec((D, LANES), lambda b, s: (0, 0)),
                  pl.BlockSpec((1, LANES), lambda b, s: (0, 0))],
        out_specs=[pl.BlockSpec((1, ts, LANES), lambda b, s: (b, s, 0)),
                   pl.BlockSpec((1, ts, LANES), lambda b, s: (b, s, 0)),
                   pl.BlockSpec((1, LANES, ts), lambda b, s: (b, 0, s))],
        out_shape=[jax.ShapeDtypeStruct((B, S, LANES), F32), jax.ShapeDtypeStruct((B, S, LANES), F32),
                   jax.ShapeDtypeStruct((B, LANES, S), F32)],
        scratch_shapes=[pltpu.VMEM((8, LANES), F32)],
        compiler_params=_params(("parallel", "arbitrary")),
    )(h, w_f, b_f)


def fgate_bwd(dc, f, b_f, *, name, ts=256):
    B, S, _ = f.shape
    P = dc.shape[1]
    ts = min(ts, S)
    ns = S // ts

    def body(dc_ref, f_ref, b_ref, df_ref, db_ref, carry):
        @pl.when(pl.program_id(1) == 0)
        def _():
            carry[...] = jnp.zeros_like(carry)

        @pl.when((pl.program_id(0) == 0) & (pl.program_id(1) == 0))
        def _():
            db_ref[...] = jnp.zeros_like(db_ref)

        dc_t = dc_ref[0, 0]
        for j in range(1, P):
            dc_t = dc_t + dc_ref[0, j]
        dlogf = _dot_hi(_tri(ts, False), dc_t, NN) + carry[pl.ds(0, 1), :]
        carry[pl.ds(0, 1), :] = dlogf[0:1, :]
        df = dlogf * _sigmoid(-(f_ref[0] + b_ref[...]))
        df_ref[0] = df.astype(BF16)
        db_ref[...] += jnp.sum(df, axis=0, keepdims=True)

    return _call(
        body, name=name, grid=(B, ns),
        in_specs=[pl.BlockSpec((1, P, ts, LANES), lambda b, s: (b, 0, ns - 1 - s, 0)),
                  pl.BlockSpec((1, ts, LANES), lambda b, s: (b, ns - 1 - s, 0)),
                  pl.BlockSpec((1, LANES), lambda b, s: (0, 0))],
        out_specs=[pl.BlockSpec((1, ts, LANES), lambda b, s: (b, ns - 1 - s, 0)),
                   pl.BlockSpec((1, LANES), lambda b, s: (0, 0))],
        out_shape=[jax.ShapeDtypeStruct((B, S, LANES), BF16), jax.ShapeDtypeStruct((1, LANES), F32)],
        scratch_shapes=[pltpu.VMEM((8, LANES), F32)],
        compiler_params=_params(("arbitrary", "arbitrary")),
    )(dc, f, b_f)


def _lane_pick(tile, idx):
    lane = lax.broadcasted_iota(jnp.int32, tile.shape, 1)
    return jnp.sum(jnp.where(lane == idx, tile, 0.0), axis=-1, keepdims=True)


FOX_T = 256


def _fox_heads(q, cc_ref, p):
    lane = lax.broadcasted_iota(jnp.int32, q.shape, 1)
    qs = q * (1.0 / math.sqrt(FOX_HEAD_DIM))
    qhs = [jnp.where((lane < FOX_HEAD_DIM) == (hh == 0), qs, jnp.zeros_like(qs)) for hh in range(2)]
    crefs = [_lane_pick(cc_ref[0, pl.ds(0, 1), :], 2 * p + hh) for hh in range(2)]
    return qhs, crefs


def _causal(t, transposed):
    r = lax.broadcasted_iota(jnp.int32, (t, t), 0)
    c = lax.broadcasted_iota(jnp.int32, (t, t), 1)
    return (r <= c) if transposed else (c <= r)


QKV0 = 8


def fox_fwd(z, c_col, c_row, *, name, rider=None):
    B, S, _ = z.shape
    assert S % FOX_T == 0
    tq, nq = FOX_T, S // FOX_T
    npair = FOX_HEADS // 2

    def body(q_ref, k_ref, v_ref, cc_ref, cr_ref, o_ref, l_ref, ot_ref, s_scr, m_scr, acc_scr):
        p, qi = pl.program_id(1), pl.program_id(2)
        qhs, crefs = _fox_heads(q_ref[0], cc_ref, p)
        lane = lax.broadcasted_iota(jnp.int32, (tq, LANES), 1)
        first = lane < FOX_HEAD_DIM
        for hh in range(2):
            m_scr[hh] = jnp.full((tq, LANES), NEG, F32)
            acc_scr[hh] = jnp.zeros((tq, LANES), F32)

        def logits(kb, diagonal):
            k0 = pl.multiple_of(kb * tq, tq)
            k = k_ref[0, pl.ds(k0, tq), :]
            for hh in range(2):
                s = lax.dot_general(qhs[hh], k, NT, preferred_element_type=F32)
                s = s + (crefs[hh] - cr_ref[0, pl.ds(2 * p + hh, 1), pl.ds(k0, tq)])
                if diagonal:
                    s = jnp.where(_causal(tq, False), s, NEG)
                s_scr[hh, kb] = s
                m_scr[hh] = jnp.maximum(m_scr[hh], jnp.maximum(s[:, :LANES], s[:, LANES:]))

        def sweep1(kb, carry):
            logits(kb, False)
            return carry

        lax.fori_loop(0, qi, sweep1, 0)
        logits(qi, True)
        ms = [jnp.max(m_scr[hh], axis=-1, keepdims=True) for hh in range(2)]
        mbs = [jnp.broadcast_to(ms[hh], (tq, tq)) for hh in range(2)]

        for hh in range(2):
            m_scr[hh] = jnp.zeros((tq, LANES), F32)

        def weigh(kb, carry):
            k0 = pl.multiple_of(kb * tq, tq)
            v = v_ref[0, pl.ds(k0, tq), :]
            for hh in range(2):
                pr = jnp.exp(s_scr[hh, kb] - mbs[hh])
                m_scr[hh] += pr[:, :LANES] + pr[:, LANES:]
                acc_scr[hh] += jnp.dot(pr.astype(BF16), v, preferred_element_type=F32)
            return carry

        lax.fori_loop(0, qi + 1, weigh, 0)
        accs = [acc_scr[hh] for hh in range(2)]
        ls = [jnp.sum(m_scr[hh], axis=-1, keepdims=True) for hh in range(2)]
        out = jnp.where(first, accs[0] / ls[0], accs[1] / ls[1])
        o_ref[0] = out.astype(BF16)
        ot_ref[...] = out.T.astype(BF16)
        l_ref[0, 0] = jnp.where(first, ms[0] + jnp.log(ls[0]), ms[1] + jnp.log(ls[1]))

    return hosted_call(
        body, rider, name=name, grid=(B, npair, nq),
        in_specs=[pl.BlockSpec((1, tq, LANES), lambda b, p, i: (b, i, QKV0 + p)),
                  pl.BlockSpec((1, S, LANES), lambda b, p, i: (b, 0, QKV0 + npair + p)),
                  pl.BlockSpec((1, S, LANES), lambda b, p, i: (b, 0, QKV0 + 2 * npair + p)),
                  pl.BlockSpec((1, tq, LANES), lambda b, p, i: (b, i, 0)),
                  pl.BlockSpec((1, 8, S), lambda b, p, i: (b, 0, 0))],
        out_specs=[pl.BlockSpec((1, tq, LANES), lambda b, p, i: (b, i, p)),
                   pl.BlockSpec((1, 1, tq, LANES), lambda b, p, i: (b, p, i, 0)),
                   pl.BlockSpec((LANES, tq), lambda b, p, i: (p, b * nq + i))],
        out_shape=[jax.ShapeDtypeStruct((B, S, FOX_W), BF16),
                   jax.ShapeDtypeStruct((B, npair, S, LANES), F32),
                   jax.ShapeDtypeStruct((FOX_W, B * S), BF16)],
        scratch_shapes=[pltpu.VMEM((2, nq, tq, tq), F32), pltpu.VMEM((2, tq, LANES), F32),
                        pltpu.VMEM((2, tq, LANES), F32)],
        args=(z, z, z, c_col, c_row),
    )


def fox_bwd_dq(z, dcat, lse, c_col, c_row, *, name, rider=None):
    B, S, _ = z.shape
    tq, nq = FOX_T, S // FOX_T
    npair = FOX_HEADS // 2

    def body(q_ref, k_ref, v_ref, do_ref, l_ref, cc_ref, cr_ref, dq_ref, st_ref, p_scr, dp_scr, dl_scr):
        p, qi = pl.program_id(1), pl.program_id(2)
        qhs, crefs = _fox_heads(q_ref[0], cc_ref, p)
        lane = lax.broadcasted_iota(jnp.int32, (tq, LANES), 1)
        do_b = do_ref[0].astype(BF16)
        dohs = [jnp.where((lane < FOX_HEAD_DIM) == (hh == 0), do_b, jnp.zeros_like(do_b)) for hh in range(2)]
        lses = [_lane_pick(l_ref[0, 0], hh * FOX_HEAD_DIM) for hh in range(2)]
        lbs = [jnp.broadcast_to(lses[hh], (tq, tq)) for hh in range(2)]
        for hh in range(2):
            dl_scr[hh] = jnp.zeros((tq, LANES), F32)

        def probs(kb, diagonal):
            k0 = pl.multiple_of(kb * tq, tq)
            k = k_ref[0, pl.ds(k0, tq), :]
            v = v_ref[0, pl.ds(k0, tq), :]
            for hh in range(2):
                s = lax.dot_general(qhs[hh], k, NT, preferred_element_type=F32)
                s = s + (crefs[hh] - cr_ref[0, pl.ds(2 * p + hh, 1), pl.ds(k0, tq)])
                pr = jnp.exp(s - lbs[hh])
                if diagonal:
                    pr = jnp.where(_causal(tq, False), pr, 0.0)
                dp = lax.dot_general(dohs[hh], v, NT, preferred_element_type=F32)
                pdp = pr * dp
                dl_scr[hh] += pdp[:, :LANES] + pdp[:, LANES:]
                p_scr[hh, kb] = pr
                dp_scr[hh, kb] = dp

        def first_pass(kb, carry):
            probs(kb, False)
            return carry

        lax.fori_loop(0, qi, first_pass, 0)
        probs(qi, True)

        dls = [jnp.sum(dl_scr[hh], axis=-1, keepdims=True) for hh in range(2)]
        dlbs = [jnp.broadcast_to(dls[hh], (tq, tq)) for hh in range(2)]

        def second_pass(kb, dq):
            k0 = pl.multiple_of(kb * tq, tq)
            k = k_ref[0, pl.ds(k0, tq), :]
            for hh in range(2):
                ds = p_scr[hh, kb] * (dp_scr[hh, kb] - dlbs[hh])
                kh = jnp.where((lane < FOX_HEAD_DIM) == (hh == 0), k, jnp.zeros_like(k))
                dq = dq + jnp.dot(ds.astype(BF16), kh, preferred_element_type=F32)
            return dq

        dq = lax.fori_loop(0, qi + 1, second_pass, jnp.zeros((tq, LANES), F32))
        dq_ref[0] = (dq * (1.0 / math.sqrt(FOX_HEAD_DIM))).astype(BF16)
        cols = jnp.zeros((tq, LANES), F32)
        for j, col in enumerate([crefs[0] - lses[0], crefs[1] - lses[1], dls[0], dls[1]]):
            cols = jnp.where(lane == j, col, cols)
        st_ref[0, 0] = _dot_hi(_eye(LANES), cols, NT)[:8]

    return hosted_call(
        body, rider, name=name, grid=(B, npair, nq),
        in_specs=[pl.BlockSpec((1, tq, LANES), lambda b, p, i: (b, i, QKV0 + p)),
                  pl.BlockSpec((1, S, LANES), lambda b, p, i: (b, 0, QKV0 + npair + p)),
                  pl.BlockSpec((1, S, LANES), lambda b, p, i: (b, 0, QKV0 + 2 * npair + p)),
                  pl.BlockSpec((1, tq, LANES), lambda b, p, i: (b, i, npair + p)),
                  pl.BlockSpec((1, 1, tq, LANES), lambda b, p, i: (b, p, i, 0)),
                  pl.BlockSpec((1, tq, LANES), lambda b, p, i: (b, i, 0)),
                  pl.BlockSpec((1, 8, S), lambda b, p, i: (b, 0, 0))],
        out_specs=[pl.BlockSpec((1, tq, LANES), lambda b, p, i: (b, i, p)),
                   pl.BlockSpec((1, 1, 8, tq), lambda b, p, i: (b, p, 0, i))],
        out_shape=[jax.ShapeDtypeStruct((B, S, FOX_W), BF16), jax.ShapeDtypeStruct((B, npair, 8, S), F32)],
        scratch_shapes=[pltpu.VMEM((2, nq, tq, tq), F32), pltpu.VMEM((2, nq, tq, tq), F32),
                        pltpu.VMEM((2, tq, LANES), F32)],
        args=(z, z, z, dcat, lse, c_col, c_row), vmem=40 << 20,
    )


def fox_bwd_dkdv(z, dcat, stats, c_col, *, name, rider=None):
    B, S, _ = z.shape
    tk, nq = FOX_T, S // FOX_T
    npair = FOX_HEADS // 2
    inv = 1.0 / math.sqrt(FOX_HEAD_DIM)

    def body(q_ref, k_ref, v_ref, do_ref, st_ref, cc_ref, dk_ref, dv_ref, dc_ref, dk_scr, dv_scr, dc_scr):
        p, kt = pl.program_id(1), pl.program_id(2)
        lane = lax.broadcasted_iota(jnp.int32, (tk, LANES), 1)
        masks = [(lane < FOX_HEAD_DIM) == (hh == 0) for hh in range(2)]
        k = k_ref[0]
        v = v_ref[0]
        khs = [jnp.where(masks[hh], k, jnp.zeros_like(k)) for hh in range(2)]
        vhs = [jnp.where(masks[hh], v, jnp.zeros_like(v)) for hh in range(2)]
        ccbs = [jnp.broadcast_to(_lane_pick(cc_ref[0], 2 * p + hh), (tk, tk)) for hh in range(2)]
        dk_scr[...] = jnp.zeros_like(dk_scr)
        dv_scr[...] = jnp.zeros_like(dv_scr)
        dc_scr[...] = jnp.zeros_like(dc_scr)

        def tile(qb, diagonal):
            q0 = pl.multiple_of(qb * tk, tk)
            qs = q_ref[0, pl.ds(q0, tk), :] * inv
            do_b = do_ref[0, pl.ds(q0, tk), :].astype(BF16)
            for hh in range(2):
                st = lax.dot_general(khs[hh], qs, NT, preferred_element_type=F32)
                pr = jnp.exp(st - ccbs[hh] + st_ref[0, 0, pl.ds(hh, 1), pl.ds(q0, tk)])
                if diagonal:
                    pr = jnp.where(_causal(tk, True), pr, 0.0)
                dp = lax.dot_general(vhs[hh], do_b, NT, preferred_element_type=F32)
                ds = pr * (dp - st_ref[0, 0, pl.ds(2 + hh, 1), pl.ds(q0, tk)])
                dv_scr[...] += jnp.dot(pr.astype(BF16), jnp.where(masks[hh], do_b, jnp.zeros_like(do_b)),
                                       preferred_element_type=F32)
                dk_scr[...] += jnp.dot(ds.astype(BF16), jnp.where(masks[hh], qs, jnp.zeros_like(qs)),
                                       preferred_element_type=F32)
                dc_scr[hh] -= ds[:, :LANES] + ds[:, LANES:]

        def later(qb, carry):
            tile(qb, False)
            return carry

        tile(kt, True)
        lax.fori_loop(kt + 1, nq, later, 0)
        dk_ref[0] = dk_scr[...].astype(BF16)
        dv_ref[0] = dv_scr[...].astype(BF16)
        dcs = [jnp.sum(dc_scr[hh], axis=-1, keepdims=True) for hh in range(2)]
        dc_ref[0, 0] = jnp.where(lane == 2 * p, dcs[0], jnp.where(lane == 2 * p + 1, dcs[1], 0.0))

    full = lambda col: pl.BlockSpec((1, S, LANES), col)
    tile_spec = lambda col: pl.BlockSpec((1, tk, LANES), col)
    return hosted_call(
        body, rider, name=name, grid=(B, npair, nq),
        in_specs=[full(lambda b, p, t: (b, 0, QKV0 + p)),
                  tile_spec(lambda b, p, t: (b, t, QKV0 + npair + p)),
                  tile_spec(lambda b, p, t: (b, t, QKV0 + 2 * npair + p)),
                  full(lambda b, p, t: (b, 0, npair + p)),
                  pl.BlockSpec((1, 1, 8, S), lambda b, p, t: (b, p, 0, 0)),
                  tile_spec(lambda b, p, t: (b, t, 0))],
        out_specs=[tile_spec(lambda b, p, t: (b, t, p)), tile_spec(lambda b, p, t: (b, t, p)),
                   pl.BlockSpec((1, 1, tk, LANES), lambda b, p, t: (b, p, t, 0))],
        out_shape=[jax.ShapeDtypeStruct((B, S, FOX_W), BF16)] * 2
        + [jax.ShapeDtypeStruct((B, npair, S, LANES), F32)],
        scratch_shapes=[pltpu.VMEM((tk, LANES), F32), pltpu.VMEM((tk, LANES), F32),
                        pltpu.VMEM((2, tk, LANES), F32)],
        args=(z, z, z, dcat, stats, c_col),
    )


def xattn_fwd(qm, kv, *, name, tq=256):
    B, S, D = qm.shape
    M = kv.shape[1]
    tq = min(tq, S)
    inv = 1.0 / math.sqrt(MEM_HEAD_DIM)

    nq = S // tq

    def body(q_ref, kv_ref, o_ref, ot_ref):
        for h in range(MEM_HEADS):
            c0 = h * MEM_HEAD_DIM
            qh = q_ref[0, :, c0:c0 + MEM_HEAD_DIM]
            kh = kv_ref[0, :, c0:c0 + MEM_HEAD_DIM]
            vh = kv_ref[0, :, D + c0:D + c0 + MEM_HEAD_DIM]
            s = lax.dot_general(qh, kh, NT, preferred_element_type=F32) * inv
            e = jnp.exp(s - jnp.max(s, axis=-1, keepdims=True))
            o = jnp.dot(e.astype(BF16), vh, preferred_element_type=F32) / jnp.sum(e, axis=-1, keepdims=True)
            o_ref[0, :, c0:c0 + MEM_HEAD_DIM] = o.astype(BF16)
            ot_ref[c0:c0 + MEM_HEAD_DIM, :] = o.T.astype(BF16)

    return _call(
        body, name=name, grid=(B, nq),
        in_specs=[pl.BlockSpec((1, tq, D), lambda b, i: (b, i, 0)),
                  pl.BlockSpec((1, M, 2 * D), lambda b, i: (b, 0, 0))],
        out_specs=[pl.BlockSpec((1, tq, D), lambda b, i: (b, i, 0)),
                   pl.BlockSpec((D, tq), lambda b, i: (0, b * nq + i))],
        out_shape=[jax.ShapeDtypeStruct((B, S, D), BF16), jax.ShapeDtypeStruct((D, B * S), BF16)],
        compiler_params=_params(("parallel", "parallel")),
    )(qm, kv)


def xattn_bwd(qm, kv, do, *, name, tq=256):
    B, S, D = qm.shape
    M = kv.shape[1]
    tq = min(tq, S)
    inv = 1.0 / math.sqrt(MEM_HEAD_DIM)

    def body(q_ref, kv_ref, do_ref, dq_ref, dkv_ref):
        @pl.when(pl.program_id(1) == 0)
        def _():
            dkv_ref[...] = jnp.zeros_like(dkv_ref)

        for h in range(MEM_HEADS):
            c0 = h * MEM_HEAD_DIM
            qh = q_ref[0, :, c0:c0 + MEM_HEAD_DIM]
            kh = kv_ref[0, :, c0:c0 + MEM_HEAD_DIM]
            vh = kv_ref[0, :, D + c0:D + c0 + MEM_HEAD_DIM]
            doh = do_ref[0, :, c0:c0 + MEM_HEAD_DIM]
            s = lax.dot_general(qh, kh, NT, preferred_element_type=F32) * inv
            e = jnp.exp(s - jnp.max(s, axis=-1, keepdims=True))
            pr = e / jnp.sum(e, axis=-1, keepdims=True)
            dp = lax.dot_general(doh, vh, NT, preferred_element_type=F32)
            ds = pr * (dp - jnp.sum(pr * dp, axis=-1, keepdims=True))
            ds_b = ds.astype(BF16)
            dq_ref[0, :, c0:c0 + MEM_HEAD_DIM] = (jnp.dot(ds_b, kh, preferred_element_type=F32) * inv).astype(BF16)
            dkv_ref[0, :, c0:c0 + MEM_HEAD_DIM] += lax.dot_general(ds_b, qh, TN, preferred_element_type=F32) * inv
            dkv_ref[0, :, D + c0:D + c0 + MEM_HEAD_DIM] += lax.dot_general(
                pr.astype(BF16), doh, TN, preferred_element_type=F32)

    row = pl.BlockSpec((1, tq, D), lambda b, i: (b, i, 0))
    kvs = pl.BlockSpec((1, M, 2 * D), lambda b, i: (b, 0, 0))
    return _call(
        body, name=name, grid=(B, S // tq), in_specs=[row, kvs, row], out_specs=[row, kvs],
        out_shape=[jax.ShapeDtypeStruct((B, S, D), BF16), jax.ShapeDtypeStruct((B, M, 2 * D), F32)],
        compiler_params=_params(("parallel", "arbitrary")),
    )(qm, kv, do)


def swiglu_fwd(gu, *, name, tm=256):
    T, F2 = gu.shape
    Fh = F2 // 2
    tm = min(tm, T)

    def body(gu_ref, o_ref, ot_ref):
        g = gu_ref[:, :Fh].astype(F32)
        u = gu_ref[:, Fh:].astype(F32)
        act = g * _sigmoid(g) * u
        o_ref[...] = act.astype(BF16)
        ot_ref[...] = act.T.astype(BF16)

    return _call(
        body, name=name, grid=(T // tm,),
        in_specs=[pl.BlockSpec((tm, F2), lambda i: (i, 0))],
        out_specs=[pl.BlockSpec((tm, Fh), lambda i: (i, 0)), pl.BlockSpec((Fh, tm), lambda i: (0, i))],
        out_shape=[jax.ShapeDtypeStruct((T, Fh), BF16), jax.ShapeDtypeStruct((Fh, T), BF16)],
        compiler_params=_params(("parallel",)),
    )(gu)


def swiglu_bwd(gu, dact, *, name, tm=256):
    T, F2 = gu.shape
    Fh = F2 // 2
    tm = min(tm, T)

    def body(gu_ref, d_ref, o_ref):
        g = gu_ref[:, :Fh].astype(F32)
        u = gu_ref[:, Fh:].astype(F32)
        d = d_ref[...].astype(F32)
        sg = _sigmoid(g)
        o_ref[:, :Fh] = (d * u * (sg * (1.0 + g * (1.0 - sg)))).astype(BF16)
        o_ref[:, Fh:] = (d * g * sg).astype(BF16)

    return _call(
        body, name=name, grid=(T // tm,),
        in_specs=[pl.BlockSpec((tm, F2), lambda i: (i, 0)), pl.BlockSpec((tm, Fh), lambda i: (i, 0))],
        out_specs=pl.BlockSpec((tm, F2), lambda i: (i, 0)),
        out_shape=jax.ShapeDtypeStruct((T, F2), BF16),
        compiler_params=_params(("parallel",)),
    )(gu, dact)


LATE = ("w_out", "w_mq", "w_mkv", "w_mo", "w_gu", "w_down")
RS_GROUPS = (("w_gu", "w_down"), ("w_out", "w_mq", "w_mkv", "w_mo"), ("w_in",))


def reduce_to_chips(names, gw, *, tag):
    g42 = [_shards_from_full(n, gw[n]) for n in names]
    got = sibling_exchange(g42, name="rs_sibling_" + tag)
    return [pair_sum(g, o, name="rs_pair_sum_" + n) for n, g, o in zip(names, g42, got)]


def local_step(x, mem, target, sp, w_in_full, late_shards):
    B, S, D = x.shape
    T = B * S
    M = mem.shape[1]
    row = lambda v: v.reshape(1, -1).astype(F32)
    g_mix, g_x, g_mem, g_ffn, g_final = (row(sp[k]) for k in ("g_mix", "g_x", "g_mem", "g_ffn", "g_final"))
    conv_b, ln_g, ln_b = row(sp["conv_b"]), row(sp["ln_g"]), row(sp["ln_b"])
    conv_w = jnp.pad(sp["conv_w"].astype(F32), ((0, HALO - CONV_K), (0, 0)))
    b_f = jnp.pad(row(sp["b_f"]), ((0, 0), (0, LANES - FOX_HEADS)))
    n_main = 2 * CONV_CH + 3 * FOX_W
    w_main = w_in_full[:, :n_main]
    w_f = jnp.pad(w_in_full[:, n_main:], ((0, 0), (0, LANES - FOX_HEADS)))

    x2d = x.reshape(T, D)
    h, h_t = rmsnorm_fwd(x2d, g_mix, name="rms_mix")
    z = matmul(h, w_main, out_dtype=BF16, tn=n_main, name="mm_in")
    z3 = z.reshape(B, S, n_main)
    conv_out, conv_t = conv_branch_fwd(z3, conv_w, conv_b, ln_g, ln_b, name="conv_fwd")
    f_raw, c_col, c_row = fgate_fwd(h.reshape(B, S, D), w_f, b_f, name="fgate_fwd")
    (att, lse, att_t), partly = fox_fwd(z3, c_col, c_row, name="fox_fwd", rider=AllGatherStage1(late_shards))
    gathered = all_gather_stage2(partly, name="ag_late_stage2")
    wf = {n: _full_from_gathered(n, blk) for n, blk in zip(LATE, gathered)}
    cat =jnp.concatenate([conv_out, att], axis=-1).reshape(T, D)
    x1 = matmul(cat, wf["w_out"], out_dtype=F32, res=x2d, tn=D, name="mm_out")
    hx, hx_t = rmsnorm_fwd(x1, g_x, name="rms_x")
    qm = matmul(hx, wf["w_mq"], out_dtype=BF16, tn=D, name="mm_mq")
    mem2d = mem.reshape(B * M, D)
    mem_n, mem_n_t = rmsnorm_fwd(mem2d, g_mem, name="rms_mem")
    kv = matmul(mem_n, wf["w_mkv"], out_dtype=BF16, tn=2 * D, name="mm_mkv").reshape(B, M, 2 * D)
    o, o_t = xattn_fwd(qm.reshape(B, S, D), kv, name="xattn_fwd")
    o = o.reshape(T, D)
    x2 = matmul(o, wf["w_mo"], out_dtype=F32, res=x1, tn=D, name="mm_mo")
    hf, hf_t = rmsnorm_fwd(x2, g_ffn, name="rms_ffn")
    gu = matmul(hf, wf["w_gu"], out_dtype=BF16, tn=2816, name="mm_gu")
    act, act_t = swiglu_fwd(gu, name="swiglu_fwd")
    x3 = matmul(act, wf["w_down"], out_dtype=F32, res=x2, tn=D, name="mm_down")
    dx3, dg_final, loss = final_loss_bwd(x3, g_final, target.reshape(T, D), name="loss_bwd")
    gw = {}
    gw["w_down"] = matmul(act_t, dx3, out_dtype=BF16, tm=1408, tn=256, name="dw_down")
    dact = matmul(dx3, wf["w_down"], tb=True, out_dtype=BF16, tn=2816, name="dx_down")
    dgu = swiglu_bwd(gu, dact, name="swiglu_bwd")
    gw["w_gu"] = matmul(hf_t, dgu, out_dtype=BF16, tn=1408, name="dw_gu")
    dhf = matmul(dgu, wf["w_gu"], tb=True, out_dtype=BF16, tm=256, tn=D, name="dx_gu")
    dx2, dg_ffn = rmsnorm_bwd(x2, g_ffn, dhf, dx3, name="rms_ffn_bwd")
    gw["w_mo"] = matmul(o_t, dx2, out_dtype=BF16, name="dw_mo")
    do = matmul(dx2, wf["w_mo"], tb=True, out_dtype=BF16, tn=D, name="dx_mo")
    dqm, dkv = xattn_bwd(qm.reshape(B, S, D), kv, do.reshape(B, S, D), name="xattn_bwd")
    dqm = dqm.reshape(T, D)
    dkv = dkv.reshape(B * M, 2 * D)
    gw["w_mq"] = matmul(hx_t, dqm, out_dtype=BF16, tn=D, name="dw_mq")
    dhx = matmul(dqm, wf["w_mq"], tb=True, out_dtype=BF16, tn=D, name="dx_mq")
    gw["w_mkv"] = matmul(mem_n_t, dkv, out_dtype=BF16, tn=D, name="dw_mkv")
    dmem_n = matmul(dkv, wf["w_mkv"], tb=True, out_dtype=BF16, tn=D, name="dx_mkv")
    _, dg_mem = rmsnorm_bwd(mem2d, g_mem, dmem_n, None, name="rms_mem_bwd")
    dx1, dg_x = rmsnorm_bwd(x1, g_x, dhx, dx2, name="rms_x_bwd")
    gw["w_out"] = jnp.concatenate([matmul(conv_t, dx1, out_dtype=BF16, name="dw_out_conv"),
                                   matmul(att_t, dx1, out_dtype=BF16, name="dw_out_att")], axis=0)
    dcat = matmul(dx1, wf["w_out"], tb=True, out_dtype=BF16, tn=D, name="dx_out").reshape(B, S, D)
    dy, dconv_w, dvec = conv_branch_bwd_a(z3, dcat, conv_w, conv_b, ln_g, ln_b, name="conv_bwd_a")
    dug = conv_branch_bwd_b(z3, dy, conv_w, name="conv_bwd_b")
    parts, gots = {}, {}
    for n, p in zip(RS_GROUPS[0], reduce_to_chips(RS_GROUPS[0], gw, tag="ffn")):
        parts[n] = p
    for n, p in zip(RS_GROUPS[1], reduce_to_chips(RS_GROUPS[1], gw, tag="mid")):
        parts[n] = p
    (dq, stats), got = fox_bwd_dq(z3, dcat, lse, c_col, c_row, name="fox_bwd_dq",
                                  rider=ChipExchange([parts[n] for n in RS_GROUPS[0]]))
    gots.update(zip(RS_GROUPS[0], got))
    (dk, dv, dc), got = fox_bwd_dkdv(z3, dcat, stats, c_col, name="fox_bwd_dkdv",
                                     rider=ChipExchange([parts[n] for n in RS_GROUPS[1]]))
    gots.update(zip(RS_GROUPS[1], got))
    df, db_f = fgate_bwd(dc, f_raw, b_f, name="fgate_bwd")
    dz = jnp.concatenate([dug, dq, dk, dv], axis=-1).reshape(T, n_main)
    df2 = df.reshape(T, LANES)
    dw_main = matmul(h_t, dz, out_dtype=BF16, tn=1280, name="dw_in")
    dw_f = matmul(h_t, df2, out_dtype=BF16, name="dw_f")
    gw["w_in"] = jnp.concatenate([dw_main, dw_f[:, :FOX_HEADS]], axis=-1)
    dh_f = matmul(df2, w_f, tb=True, out_dtype=F32, tn=D, name="dx_f")
    parts["w_in"] = reduce_to_chips(RS_GROUPS[2], gw, tag="in")[0]
    dh, (gots["w_in"],) = matmul(dz, w_main, tb=True, out_dtype=F32, res=dh_f, tn=D, name="dx_in",
                                 rider=ChipExchange([parts["w_in"]]))
    dx, dg_mix = rmsnorm_bwd(x2d, g_mix, dh, dx1, name="rms_mix_bwd")
    gs = dict(g_mix=dg_mix, b_f=db_f[:, :FOX_HEADS], conv_w=dconv_w[:CONV_K], conv_b=dvec[0:1],
              ln_g=dvec[1:2], ln_b=dvec[2:3], g_x=dg_x, g_mem=dg_mem, g_ffn=dg_ffn, g_final=dg_final)
    return loss, dx.reshape(B, S, D), gs, {n: (parts[n], gots[n]) for n in BIG}


def _me():
    return lax.axis_index("x"), lax.axis_index("y"), lax.axis_index("c")


def _any_specs(n):
    return [pl.BlockSpec(memory_space=pl.ANY)] * n


def all_gather(xs, *, name):
    n = len(xs)

    def body(*refs):
        x_refs, out_refs = refs[:n], refs[n:2 * n]
        send_sems, recv_sems, local_sems = refs[2 * n:]
        x, y, c = _me()
        me, sibling = (x, y, c), (x, y, 1 - c)
        chips = [(1 - x, y), (x, 1 - y), (1 - x, 1 - y)]

        def slot(a, px, py, pc):
            return out_refs[a].at[4 * px + 2 * py + pc]

        def copy(a, k, block, to, own=False):
            return pltpu.make_async_remote_copy(
                src_ref=x_refs[a] if own else slot(a, *block), dst_ref=slot(a, *block),
                send_sem=send_sems.at[k, a], recv_sem=recv_sems.at[k, a], device_id=to, device_id_type=MESH)

        mine = [pltpu.make_async_copy(x_refs[a], slot(a, *me), local_sems.at[a]) for a in range(n)]
        first = [copy(a, 0, me, sibling, own=True) for a in range(n)]
        first += [copy(a, 1 + j, me, (*chip, c), own=True) for j, chip in enumerate(chips) for a in range(n)]
        for cp in mine + first:
            cp.start()
        passed = []
        for j, chip in enumerate(chips):
            for a in range(n):
                copy(a, 1 + j, (*chip, c), me).wait_recv()
                passed.append(copy(a, 4 + j, (*chip, c), sibling))
                passed[-1].start()
        for a in range(n):
            copy(a, 0, sibling, me).wait_recv()
            for j, chip in enumerate(chips):
                copy(a, 4 + j, (*chip, 1 - c), me).wait_recv()
        for cp in first + passed:
            cp.wait_send()
        for cp in mine:
            cp.wait()

    return _call(
        body, name=name, in_specs=_any_specs(n), out_specs=_any_specs(n),
        out_shape=[jax.ShapeDtypeStruct((N_DEV,) + v.shape, v.dtype) for v in xs],
        scratch_shapes=[pltpu.SemaphoreType.DMA((7, n)), pltpu.SemaphoreType.DMA((7, n)),
                        pltpu.SemaphoreType.DMA((n,))],
    )(*xs)


def sibling_exchange(gs, *, name):
    n = len(gs)

    def body(*refs):
        g_refs, out_refs = refs[:n], refs[n:2 * n]
        send_sems, recv_sems = refs[2 * n:]
        x, y, c = _me()
        cps = [pltpu.make_async_remote_copy(
            src_ref=g_refs[a].at[:, 1 - c], dst_ref=out_refs[a], send_sem=send_sems.at[a],
            recv_sem=recv_sems.at[a], device_id=(x, y, 1 - c), device_id_type=MESH) for a in range(n)]
        for cp in cps:
            cp.start()
        for cp in cps:
            cp.wait()

    return _call(
        body, name=name, in_specs=_any_specs(n), out_specs=_any_specs(n),
        out_shape=[jax.ShapeDtypeStruct((4,) + g.shape[2:], g.dtype) for g in gs],
        scratch_shapes=[pltpu.SemaphoreType.DMA((n,)), pltpu.SemaphoreType.DMA((n,))],
    )(*gs)


class ChipExchange:
    def __init__(self, ps):
        n = len(ps)
        self.n, self.inputs = n, list(ps)
        self.out_shape = [jax.ShapeDtypeStruct(p.shape, p.dtype) for p in ps]
        self.scratch = [pltpu.SemaphoreType.DMA((3, n)), pltpu.SemaphoreType.DMA((3, n))]

    def _copies(self, p_refs, out_refs, sems, outgoing):
        send_sems, recv_sems = sems
        x, y, c = _me()
        my_chip = 2 * x + y
        cps = []
        for k in range(3):
            px, py = x ^ ((k + 1) >> 1), y ^ ((k + 1) & 1)
            src, dst = (2 * px + py, my_chip) if outgoing else (my_chip, 2 * px + py)
            for a in range(self.n):
                cps.append(pltpu.make_async_remote_copy(
                    src_ref=p_refs[a].at[src], dst_ref=out_refs[a].at[dst], send_sem=send_sems.at[k, a],
                    recv_sem=recv_sems.at[k, a], device_id=(px, py, c), device_id_type=MESH))
        return cps

    def start(self, in_refs, out_refs, sems):
        for cp in self._copies(in_refs, out_refs, sems, True):
            cp.start()

    def finish(self, in_refs, out_refs, sems):
        for cp in self._copies(in_refs, out_refs, sems, False):
            cp.wait_recv()
        for cp in self._copies(in_refs, out_refs, sems, True):
            cp.wait_send()


class AllGatherStage1:
    def __init__(self, xs):
        n = len(xs)
        self.n, self.inputs = n, list(xs)
        self.out_shape = [jax.ShapeDtypeStruct((N_DEV,) + v.shape, v.dtype) for v in xs]
        self.scratch = [pltpu.SemaphoreType.DMA((4, n)), pltpu.SemaphoreType.DMA((4, n)),
                        pltpu.SemaphoreType.DMA((n,))]

    def _copies(self, x_refs, out_refs, sems, kind):
        send_sems, recv_sems, local_sems = sems
        x, y, c = _me()
        slot = lambda a, d: out_refs[a].at[4 * d[0] + 2 * d[1] + d[2]]
        if kind == "local":
            return [pltpu.make_async_copy(x_refs[a], slot(a, (x, y, c)), local_sems.at[a]) for a in range(self.n)]
        cps = []
        for k, peer in enumerate([(x, y, 1 - c), (1 - x, y, c), (x, 1 - y, c), (1 - x, 1 - y, c)]):
            for a in range(self.n):
                cps.append(pltpu.make_async_remote_copy(
                    src_ref=x_refs[a], dst_ref=slot(a, (x, y, c) if kind == "out" else peer),
                    send_sem=send_sems.at[k, a], recv_sem=recv_sems.at[k, a], device_id=peer, device_id_type=MESH))
        return cps

    def start(self, in_refs, out_refs, sems):
        for cp in self._copies(in_refs, out_refs, sems, "local") + self._copies(in_refs, out_refs, sems, "out"):
            cp.start()

    def finish(self, in_refs, out_refs, sems):
        for cp in self._copies(in_refs, out_refs, sems, "in"):
            cp.wait_recv()
        for cp in self._copies(in_refs, out_refs, sems, "out"):
            cp.wait_send()
        for cp in self._copies(in_refs, out_refs, sems, "local"):
            cp.wait()


def all_gather_stage2(outs, *, name):
    n = len(outs)

    def body(*refs):
        out_refs = refs[n:2 * n]
        send_sems, recv_sems = refs[2 * n:]
        x, y, c = _me()
        sends, recvs = [], []
        for k, (px, py) in enumerate([(1 - x, y), (x, 1 - y), (1 - x, 1 - y)]):
            for a in range(n):
                mk = lambda pc: pltpu.make_async_remote_copy(
                    src_ref=out_refs[a].at[4 * px + 2 * py + c], dst_ref=out_refs[a].at[4 * px + 2 * py + pc],
                    send_sem=send_sems.at[k, a], recv_sem=recv_sems.at[k, a], device_id=(x, y, 1 - c),
                    device_id_type=MESH)
                sends.append(mk(c))
                recvs.append(mk(1 - c))
        for cp in sends:
            cp.start()
        for cp in recvs:
            cp.wait_recv()
        for cp in sends:
            cp.wait_send()

    return _call(
        body, name=name, in_specs=_any_specs(n), out_specs=_any_specs(n),
        out_shape=[jax.ShapeDtypeStruct(o.shape, o.dtype) for o in outs],
        input_output_aliases={a: a for a in range(n)},
        scratch_shapes=[pltpu.SemaphoreType.DMA((3, n)), pltpu.SemaphoreType.DMA((3, n))],
    )(*outs)


def hosted_call(body, rider, *, name, grid, in_specs, out_specs, out_shape, scratch_shapes, args, vmem=None):
    n_in, n_out, n_scr = len(in_specs), len(out_specs), len(scratch_shapes)
    r_in, r_out = (len(rider.inputs), len(rider.out_shape)) if rider is not None else (0, 0)

    def wrapped(*refs):
        ins, refs = refs[:n_in], refs[n_in:]
        rins, refs = refs[:r_in], refs[r_in:]
        outs, refs = refs[:n_out], refs[n_out:]
        routs, refs = refs[:r_out], refs[r_out:]
        scr, rscr = refs[:n_scr], refs[n_scr:]
        ids = [pl.program_id(d) for d in range(len(grid))]
        first = functools.reduce(jnp.logical_and, [i == 0 for i in ids], True)
        last = functools.reduce(jnp.logical_and, [i == g - 1 for i, g in zip(ids, grid)], True)
        if rider is not None and grid:
            pl.when(first)(lambda: rider.start(rins, routs, rscr))
        elif rider is not None:
            rider.start(rins, routs, rscr)
        if body is not None:
            body(*ins, *outs, *scr)
        if rider is not None and grid:
            pl.when(last)(lambda: rider.finish(rins, routs, rscr))
        elif rider is not None:
            rider.finish(rins, routs, rscr)

    kw = dict(grid=grid) if grid else {}
    if grid or vmem is not None:
        kw["compiler_params"] = _params(("arbitrary",) * len(grid) if grid else None, vmem)
    res = _call(
        wrapped, name=name, in_specs=list(in_specs) + _any_specs(r_in), out_specs=list(out_specs) + _any_specs(r_out),
        out_shape=list(out_shape) + (rider.out_shape if rider is not None else []),
        scratch_shapes=list(scratch_shapes) + (rider.scratch if rider is not None else []), **kw,
    )(*args, *(rider.inputs if rider is not None else []))
    return list(res[:n_out]), list(res[n_out:])


def _pick_rows(r, target=256):
    best = None
    for d in range(16, min(r, target) + 1, 16):
        if r % d == 0:
            best = d
    return r if best is None else best


def pair_sum(g, got, *, name):
    _, _, R, C = g.shape
    tr = _pick_rows(R)

    def body(g_ref, got_ref, o_ref):
        mine = jnp.where(lax.axis_index("c") == 0, g_ref[:, 0], g_ref[:, 1])
        o_ref[...] = (mine.astype(F32) + got_ref[...].astype(F32)).astype(o_ref.dtype)

    return _call(
        body, name=name, grid=(R // tr,),
        in_specs=[pl.BlockSpec((4, 2, tr, C), lambda i: (0, 0, i, 0)), pl.BlockSpec((4, tr, C), lambda i: (0, i, 0))],
        out_specs=pl.BlockSpec((4, tr, C), lambda i: (0, i, 0)),
        out_shape=jax.ShapeDtypeStruct((4, R, C), g.dtype),
        compiler_params=_params(("parallel",)),
    )(g, got)


def chip_sum_adamw(p, got, w, m, v, *, name):
    _, R, C = p.shape
    tr = _pick_rows(R)

    def body(p_ref, got_ref, w_ref, m_ref, v_ref, g_ref, d_ref, mo_ref, vo_ref):
        my_chip = 2 * lax.axis_index("x") + lax.axis_index("y")
        g = jnp.zeros((tr, C), F32)
        for j in range(4):
            g = g + jnp.where(my_chip == j, p_ref[j], got_ref[j]).astype(F32)
        g_ref[...] = g
        d_ref[...], mo_ref[...], vo_ref[...] = _adamw_math(w_ref[...], g, m_ref[...], v_ref[...])

    part = pl.BlockSpec((4, tr, C), lambda i: (0, i, 0))
    spec = pl.BlockSpec((tr, C), lambda i: (i, 0))
    return _call(
        body, name=name, grid=(R // tr,), in_specs=[part, part, spec, spec, spec], out_specs=[spec] * 4,
        out_shape=[jax.ShapeDtypeStruct((R, C), F32)] * 4,
        compiler_params=_params(("parallel",)),
    )(p, got, w, m, v)


def rows_sum(g8, *, name):
    _, R, C = g8.shape

    def body(g_ref, o_ref):
        acc = g_ref[0]
        for j in range(1, N_DEV):
            acc = acc + g_ref[j]
        o_ref[...] = acc

    return _call(body, name=name, out_shape=jax.ShapeDtypeStruct((R, C), F32))(g8)


def _adamw_math(w, g, m, v):
    m = ADAM_B1 * m + (1.0 - ADAM_B1) * g
    v = ADAM_B2 * v + (1.0 - ADAM_B2) * (g * g)
    m_hat = m / (1.0 - ADAM_B1 ** ADAM_STEP)
    v_hat = v / (1.0 - ADAM_B2 ** ADAM_STEP)
    delta = -ADAM_LR * (m_hat / (jnp.sqrt(v_hat) + ADAM_EPS) + ADAM_WD * w)
    return delta, m, v


def adamw_small(wgmv, *, name):
    n = len(wgmv)

    def body(*refs):
        ins, outs = refs[:4 * n], refs[4 * n:]
        for a in range(n):
            w_ref, g_ref, m_ref, v_ref = ins[4 * a:4 * a + 4]
            d, mn, vn = _adamw_math(w_ref[...], g_ref[...], m_ref[...], v_ref[...])
            outs[3 * a][...] = d
            outs[3 * a + 1][...] = mn
            outs[3 * a + 2][...] = vn

    flat = [t for tup in wgmv for t in tup]
    res = _call(
        body, name=name,
        out_shape=[jax.ShapeDtypeStruct(tup[0].shape, F32) for tup in wgmv for _ in range(3)],
    )(*flat)
    return [tuple(res[3 * a:3 * a + 3]) for a in range(n)]


BIG = ("w_in", "w_out", "w_mq", "w_mkv", "w_mo", "w_gu", "w_down")
COL_SHARDED = ("w_in", "w_mkv", "w_gu")
SMALL = ("g_mix", "b_f", "conv_w", "conv_b", "ln_g", "ln_b", "g_x", "g_mem", "g_ffn", "g_final")


def _full_from_gathered(n, blk):
    _, rr, cc = blk.shape
    if n in COL_SHARDED:
        return jnp.concatenate([blk[k] for k in range(N_DEV)], axis=1)
    return blk.reshape(N_DEV * rr, cc)


def _shards_from_full(n, g):
    rr, cc = g.shape
    if n in COL_SHARDED:
        w = cc // N_DEV
        return jnp.stack([g[:, k * w:(k + 1) * w] for k in range(N_DEV)]).reshape(4, 2, rr, w)
    return g.reshape(4, 2, rr // N_DEV, cc)


def _small_layout():
    sizes = dict(g_mix=1024, b_f=8, conv_w=CONV_K * CONV_CH, conv_b=512, ln_g=512, ln_b=512, g_x=1024,
                 g_mem=1024, g_ffn=1024, g_final=1024, loss=1)
    lay, r0 = {}, 0
    for n, sz in sizes.items():
        r = -(-sz // LANES)
        lay[n] = (r0, r, sz)
        r0 += r
    return lay, -(-r0 // 8) * 8


def kernel(x, mem, g_mix, w_in, b_f, conv_w, conv_b, ln_g, ln_b, w_out, g_x, g_mem, w_mq, w_mkv, w_mo, g_ffn, w_gu, w_down, g_final, loss_target, m_g_mix, m_w_in, m_b_f, m_conv_w, m_conv_b, m_ln_g, m_ln_b, m_w_out, m_g_x, m_g_mem, m_w_mq, m_w_mkv, m_w_mo, m_g_ffn, m_w_gu, m_w_down, m_g_final, v_g_mix, v_w_in, v_b_f, v_conv_w, v_conv_b, v_ln_g, v_ln_b, v_w_out, v_g_x, v_g_mem, v_w_mq, v_w_mkv, v_w_mo, v_g_ffn, v_w_gu, v_w_down, v_g_final):
    names = ["g_mix", "w_in", "b_f", "conv_w", "conv_b", "ln_g", "ln_b", "w_out", "g_x", "g_mem", "w_mq",
             "w_mkv", "w_mo", "g_ffn", "w_gu", "w_down", "g_final"]
    W = dict(zip(names, [g_mix, w_in, b_f, conv_w, conv_b, ln_g, ln_b, w_out, g_x, g_mem, w_mq, w_mkv, w_mo,
                         g_ffn, w_gu, w_down, g_final]))
    Mo = dict(zip(names, [m_g_mix, m_w_in, m_b_f, m_conv_w, m_conv_b, m_ln_g, m_ln_b, m_w_out, m_g_x, m_g_mem,
                          m_w_mq, m_w_mkv, m_w_mo, m_g_ffn, m_w_gu, m_w_down, m_g_final]))
    Vo = dict(zip(names, [v_g_mix, v_w_in, v_b_f, v_conv_w, v_conv_b, v_ln_g, v_ln_b, v_w_out, v_g_x, v_g_mem,
                          v_w_mq, v_w_mkv, v_w_mo, v_g_ffn, v_w_gu, v_w_down, v_g_final]))
    dev = 4 * lax.axis_index("x") + 2 * lax.axis_index("y") + lax.axis_index("c")

    two = lambda a: a.reshape(-1, a.shape[-1])
    cw_shard = jnp.pad(two(conv_w), ((0, HALO - CONV_K), (0, 0)))
    w_in8, cw8 = all_gather([two(w_in).astype(BF16), cw_shard], name="ag_first")
    cw_full = cw8.transpose(1, 0, 2).reshape(HALO, -1)[:CONV_K]

    sp = dict(g_mix=g_mix, b_f=b_f, conv_w=cw_full, conv_b=conv_b, ln_g=ln_g, ln_b=ln_b, g_x=g_x, g_mem=g_mem,
              g_ffn=g_ffn, g_final=g_final)
    loss_blk, grad_x, gs, reduced = local_step(x, mem, loss_target, sp, _full_from_gathered("w_in", w_in8),
                                               [two(W[n]).astype(BF16) for n in LATE])

    lay, rs = _small_layout()
    small = {**{n: gs[n] for n in SMALL}, "loss": loss_blk[:, :1]}
    parts = []
    for n, (r0, r, sz) in lay.items():
        flat = small[n].reshape(-1).astype(F32)
        parts.append(jnp.pad(flat, (0, r * LANES - sz)).reshape(r, LANES))
    spack = jnp.concatenate(parts, axis=0)
    spack = jnp.pad(spack, ((0, rs - spack.shape[0]), (0, 0)))
    ssum = rows_sum(all_gather([spack], name="ag_small")[0], name="small_sum")
    gsmall = {n: ssum[r0:r0 + r].reshape(-1)[:sz] for n, (r0, r, sz) in lay.items()}
    loss = gsmall["loss"].reshape(())

    grads, delta, new_m, new_v = {}, {}, {}, {}
    for n in BIG:
        p, o = reduced[n]
        shp = W[n].shape
        g, d, mn, vn = chip_sum_adamw(p, o, two(W[n]), two(Mo[n]), two(Vo[n]), name="adamw_" + n)
        grads[n], delta[n], new_m[n], new_v[n] = g.reshape(shp), d.reshape(shp), mn.reshape(shp), vn.reshape(shp)
    for n in SMALL:
        if n == "conv_w":
            full = gsmall[n].reshape(CONV_K, CONV_CH)
            ncol = conv_w.shape[-1]
            grads[n] = lax.dynamic_slice(full, (0, dev * ncol), (CONV_K, ncol)).reshape(conv_w.shape)
        else:
            grads[n] = gsmall[n].reshape(W[n].shape)
    upd = adamw_small([(two(W[n]), two(grads[n]), two(Mo[n]), two(Vo[n])) for n in SMALL], name="adamw_small")
    for n, (d, mn, vn) in zip(SMALL, upd):
        shp = W[n].shape
        delta[n], new_m[n], new_v[n] = d.reshape(shp), mn.reshape(shp), vn.reshape(shp)
    return (loss, grad_x, *[grads[n] for n in names], *[delta[n] for n in names],
            *[new_m[n] for n in names], *[new_v[n] for n in names])
```

```python
import functools
import math

import jax
import jax.numpy as jnp
from jax import lax
from jax.experimental import pallas as pl
from jax.experimental.pallas import tpu as pltpu

F32 = jnp.float32
BF16 = jnp.bfloat16
EPS = 1e-6
N_DEV = 8
CONV_CH = 512
CONV_K = 31
FOX_HEADS = 8
FOX_HEAD_DIM = 64
FOX_W = 512
MEM_HEADS = 4
MEM_HEAD_DIM = 256
HALO = 32
LANES = 128
ADAM_LR, ADAM_B1, ADAM_B2, ADAM_EPS, ADAM_WD, ADAM_STEP = 0.001, 0.9, 0.999, 1e-08, 0.01, 10
NEG = -1e30
VMEM_CAP = 60 * 1024 * 1024
MESH = pl.DeviceIdType.MESH


def _call(body, **kw):
    call = pl.pallas_call(body, **kw)
    return lambda *args: call(*[pltpu.with_memory_space_constraint(a, pltpu.HBM) for a in args])


def _params(sem=None, vmem=None):
    kw = {}
    if sem is not None:
        kw["dimension_semantics"] = sem
    if vmem is not None:
        kw["vmem_limit_bytes"] = int(min(VMEM_CAP, vmem))
    return pltpu.CompilerParams(**kw)


def _nbytes(shape, dtype):
    return math.prod(shape) * jnp.dtype(dtype).itemsize


def _pick(n, target):
    best = None
    for d in range(LANES, min(n, target) + 1, LANES):
        if n % d == 0:
            best = d
    return n if best is None else best


def matmul(a, b, *, tb=False, out_dtype, res=None, tm=512, tn=512, tk=None, name, rider=None):
    M, K = a.shape
    N = b.shape[0] if tb else b.shape[1]
    assert (b.shape[1] if tb else b.shape[0]) == K
    tm, tn = _pick(M, tm), _pick(N, tn)
    tk = K if tk is None else _pick(K, tk)
    assert M % tm == 0 and N % tn == 0 and K % tk == 0, (name, M, N, K, tm, tn, tk)
    nk = K // tk
    dn = (((1,), (1 if tb else 0,)), ((), ()))

    def body(*refs):
        if res is not None:
            a_ref, b_ref, r_ref, o_ref = refs[:4]
        else:
            a_ref, b_ref, o_ref = refs[:3]
        p = lax.dot_general(a_ref[...].astype(BF16), b_ref[...].astype(BF16), dn,
                            preferred_element_type=F32)

        def finish(acc):
            if res is not None:
                acc = acc + r_ref[...].astype(F32)
            o_ref[...] = acc.astype(out_dtype)

        if nk == 1:
            finish(p)
        else:
            acc_ref = refs[-1]
            k = pl.program_id(2)

            @pl.when(k == 0)
            def _():
                acc_ref[...] = p

            @pl.when(k > 0)
            def _():
                acc_ref[...] += p

            @pl.when(k == nk - 1)
            def _():
                finish(acc_ref[...])

    a_spec = pl.BlockSpec((tm, tk), lambda i, j, k: (i, k))
    b_spec = pl.BlockSpec((tn, tk), lambda i, j, k: (j, k)) if tb else pl.BlockSpec((tk, tn), lambda i, j, k: (k, j))
    o_spec = pl.BlockSpec((tm, tn), lambda i, j, k: (i, j))
    in_specs, args = [a_spec, b_spec], [a, b]
    est = 2 * (_nbytes((tm, tk), a.dtype) + _nbytes((tk, tn), b.dtype) + _nbytes((tm, tn), out_dtype))
    est += (a.dtype != BF16) * _nbytes((tm, tk), BF16) + (b.dtype != BF16) * _nbytes((tk, tn), BF16)
    est += 2 * _nbytes((tm, tn), F32)
    if res is not None:
        in_specs.append(o_spec)
        args.append(res)
        est += 2 * _nbytes((tm, tn), res.dtype)
    (out,), rode = hosted_call(
        body, rider, name=name, grid=(M // tm, N // tn, nk),
        in_specs=in_specs, out_specs=[o_spec],
        out_shape=[jax.ShapeDtypeStruct((M, N), out_dtype)],
        scratch_shapes=[] if nk == 1 else [pltpu.VMEM((tm, tn), F32)],
        args=args, vmem=est + (8 << 20),
    )
    return out if rider is None else (out, rode)


def _rms_scale(x):
    return lax.rsqrt(jnp.mean(x * x, axis=-1, keepdims=True) + EPS)


def rmsnorm_fwd(x, g, *, name, tm=512):
    T, D = x.shape
    tm = min(tm, T)

    def body(x_ref, g_ref, o_ref, ot_ref):
        xv = x_ref[...]
        h = xv * _rms_scale(xv) * g_ref[...]
        o_ref[...] = h.astype(BF16)
        ot_ref[...] = h.T.astype(BF16)

    return _call(
        body, name=name, grid=(T // tm,),
        in_specs=[pl.BlockSpec((tm, D), lambda i: (i, 0)), pl.BlockSpec((1, D), lambda i: (0, 0))],
        out_specs=[pl.BlockSpec((tm, D), lambda i: (i, 0)), pl.BlockSpec((D, tm), lambda i: (0, i))],
        out_shape=[jax.ShapeDtypeStruct((T, D), BF16), jax.ShapeDtypeStruct((D, T), BF16)],
        compiler_params=_params(("parallel",)),
    )(x, g)


def _rms_bwd_math(xv, gv, dh):
    r = _rms_scale(xv)
    xh = xv * r
    dg = jnp.sum(dh * xh, axis=0, keepdims=True)
    dxh = dh * gv
    dx = r * (dxh - xh * jnp.mean(dxh * xh, axis=-1, keepdims=True))
    return dx, dg


def rmsnorm_bwd(x, g, dh, dres, *, name, tm=256):
    T, D = x.shape
    tm = min(tm, T)

    def body(*refs):
        if dres is not None:
            x_ref, g_ref, dh_ref, dr_ref, dx_ref, dg_ref = refs
        else:
            x_ref, g_ref, dh_ref, dx_ref, dg_ref = refs
        dx, dg = _rms_bwd_math(x_ref[...], g_ref[...], dh_ref[...].astype(F32))
        if dres is not None:
            dx = dx + dr_ref[...]
        dx_ref[...] = dx

        @pl.when(pl.program_id(0) == 0)
        def _():
            dg_ref[...] = jnp.zeros_like(dg_ref)

        dg_ref[...] += dg

    row = pl.BlockSpec((tm, D), lambda i: (i, 0))
    vec = pl.BlockSpec((1, D), lambda i: (0, 0))
    ins, args = [row, vec, row], [x, g, dh]
    if dres is not None:
        ins.append(row)
        args.append(dres)
    return _call(
        body, name=name, grid=(T // tm,), in_specs=ins, out_specs=[row, vec],
        out_shape=[jax.ShapeDtypeStruct((T, D), F32), jax.ShapeDtypeStruct((1, D), F32)],
        compiler_params=_params(("arbitrary",)),
    )(*args)


def final_loss_bwd(x, g, target, *, name, tm=256):
    T, D = x.shape
    tm = min(tm, T)

    def body(x_ref, g_ref, t_ref, dx_ref, dg_ref, l_ref):
        xv, gv = x_ref[...], g_ref[...]
        e = xv * _rms_scale(xv) * gv - t_ref[...]
        part = 0.5 * jnp.sum(jnp.mean(e * e, axis=-1, keepdims=True), axis=0, keepdims=True)
        dx, dg = _rms_bwd_math(xv, gv, e * (1.0 / D))
        dx_ref[...] = dx

        @pl.when(pl.program_id(0) == 0)
        def _():
            dg_ref[...] = jnp.zeros_like(dg_ref)
            l_ref[...] = jnp.zeros_like(l_ref)

        dg_ref[...] += dg
        l_ref[...] += jnp.broadcast_to(part, l_ref.shape)

    row = pl.BlockSpec((tm, D), lambda i: (i, 0))
    vec = pl.BlockSpec((1, D), lambda i: (0, 0))
    return _call(
        body, name=name, grid=(T // tm,), in_specs=[row, vec, row],
        out_specs=[row, vec, pl.BlockSpec((1, LANES), lambda i: (0, 0))],
        out_shape=[jax.ShapeDtypeStruct((T, D), F32), jax.ShapeDtypeStruct((1, D), F32),
                   jax.ShapeDtypeStruct((1, LANES), F32)],
        compiler_params=_params(("arbitrary",)),
    )(x, g, target)


def _sigmoid(v):
    return 1.0 / (1.0 + jnp.exp(-v))


def _glu(blk):
    u = blk[:, :CONV_CH].astype(F32)
    gt = blk[:, CONV_CH:].astype(F32)
    return u * _sigmoid(gt)


def _fill_causal_ext(ext, cur_ref, halo_ref, s, ts):
    ext[pl.ds(HALO, ts), :] = _glu(cur_ref[0])
    hal = _glu(halo_ref[0])
    ext[pl.ds(0, HALO), :] = jnp.where(s > 0, hal, 0.0)


SUBLANES = 8


def _make_shifted(ext, sh):
    n = ext.shape[0]
    full = ext[...]
    for r in range(1, SUBLANES):
        sh[r - 1] = pltpu.roll(full, n - r, 0)


def _tap(ext, sh, off, ts):
    r = off % SUBLANES
    return ext[pl.ds(off, ts), :] if r == 0 else sh[r - 1, pl.ds(off - r, ts), :]


def _causal_conv(ext, sh, w_ref, ts):
    acc = jnp.zeros((ts, CONV_CH), F32)
    for j in range(CONV_K):
        acc = acc + _tap(ext, sh, HALO - (CONV_K - 1) + j, ts) * w_ref[pl.ds(j, 1), :]
    return acc


def _ln_stats(y):
    mu = jnp.mean(y, axis=-1, keepdims=True)
    yc = y - mu
    rstd = lax.rsqrt(jnp.mean(yc * yc, axis=-1, keepdims=True) + EPS)
    return yc * rstd, rstd


def _conv_specs(ts, S):
    nh = ts // HALO
    cur = pl.BlockSpec((1, ts, 2 * CONV_CH), lambda b, s: (b, s, 0))
    halo = pl.BlockSpec((1, HALO, 2 * CONV_CH), lambda b, s: (b, jnp.maximum(s * nh - 1, 0), 0))
    w = pl.BlockSpec((HALO, CONV_CH), lambda b, s: (0, 0))
    vec = pl.BlockSpec((1, CONV_CH), lambda b, s: (0, 0))
    return cur, halo, w, vec


def conv_branch_fwd(ug, conv_w, conv_b, ln_g, ln_b, *, name, ts=256):
    B, S, _ = ug.shape
    ts = min(ts, S)
    ns = S // ts
    cur, halo, w, vec = _conv_specs(ts, S)

    def body(cur_ref, halo_ref, w_ref, cb_ref, lg_ref, lb_ref, o_ref, ot_ref, ext, sh):
        _fill_causal_ext(ext, cur_ref, halo_ref, pl.program_id(1), ts)
        _make_shifted(ext, sh)
        y = _causal_conv(ext, sh, w_ref, ts) + cb_ref[...]
        yh, _ = _ln_stats(y)
        ln = yh * lg_ref[...] + lb_ref[...]
        out = ln * _sigmoid(ln)
        o_ref[0] = out.astype(BF16)
        ot_ref[...] = out.T.astype(BF16)

    return _call(
        body, name=name, grid=(B, ns), in_specs=[cur, halo, w, vec, vec, vec],
        out_specs=[pl.BlockSpec((1, ts, CONV_CH), lambda b, s: (b, s, 0)),
                   pl.BlockSpec((CONV_CH, ts), lambda b, s: (0, b * ns + s))],
        out_shape=[jax.ShapeDtypeStruct((B, S, CONV_CH), BF16), jax.ShapeDtypeStruct((CONV_CH, B * S), BF16)],
        scratch_shapes=[pltpu.VMEM((ts + HALO, CONV_CH), F32),
                        pltpu.VMEM((SUBLANES - 1, ts + HALO, CONV_CH), F32)],
        compiler_params=_params(("parallel", "parallel")),
    )(ug, ug, conv_w, conv_b, ln_g, ln_b)


def conv_branch_bwd_a(ug, dcat, conv_w, conv_b, ln_g, ln_b, *, name, ts=256):
    B, S, _ = ug.shape
    ts = min(ts, S)
    cur, halo, w, vec = _conv_specs(ts, S)

    def body(cur_ref, halo_ref, d_ref, w_ref, cb_ref, lg_ref, lb_ref, dy_ref, dw_ref, dv_ref, ext, sh):
        _fill_causal_ext(ext, cur_ref, halo_ref, pl.program_id(1), ts)
        _make_shifted(ext, sh)
        y = _causal_conv(ext, sh, w_ref, ts) + cb_ref[...]
        yh, rstd = _ln_stats(y)
        lg = lg_ref[...]
        ln = yh * lg + lb_ref[...]
        sg = _sigmoid(ln)
        dln = d_ref[0].astype(F32) * (sg * (1.0 + ln * (1.0 - sg)))
        dyh = dln * lg
        dy = rstd * (dyh - jnp.mean(dyh, axis=-1, keepdims=True)
                     - yh * jnp.mean(dyh * yh, axis=-1, keepdims=True))
        dy_ref[0] = dy

        @pl.when((pl.program_id(0) == 0) & (pl.program_id(1) == 0))
        def _():
            dw_ref[...] = jnp.zeros_like(dw_ref)
            dv_ref[...] = jnp.zeros_like(dv_ref)

        dv_ref[pl.ds(0, 1), :] += jnp.sum(dy, axis=0, keepdims=True)
        dv_ref[pl.ds(1, 1), :] += jnp.sum(dln * yh, axis=0, keepdims=True)
        dv_ref[pl.ds(2, 1), :] += jnp.sum(dln, axis=0, keepdims=True)
        for j in range(CONV_K):
            tap = _tap(ext, sh, HALO - (CONV_K - 1) + j, ts)
            dw_ref[pl.ds(j, 1), :] += jnp.sum(dy * tap, axis=0, keepdims=True)

    return _call(
        body, name=name, grid=(B, S // ts),
        in_specs=[cur, halo, pl.BlockSpec((1, ts, CONV_CH), lambda b, s: (b, s, 0)), w, vec, vec, vec],
        out_specs=[pl.BlockSpec((1, ts, CONV_CH), lambda b, s: (b, s, 0)),
                   pl.BlockSpec((HALO, CONV_CH), lambda b, s: (0, 0)),
                   pl.BlockSpec((8, CONV_CH), lambda b, s: (0, 0))],
        out_shape=[jax.ShapeDtypeStruct((B, S, CONV_CH), F32),
                   jax.ShapeDtypeStruct((HALO, CONV_CH), F32),
                   jax.ShapeDtypeStruct((8, CONV_CH), F32)],
        scratch_shapes=[pltpu.VMEM((ts + HALO, CONV_CH), F32),
                        pltpu.VMEM((SUBLANES - 1, ts + HALO, CONV_CH), F32)],
        compiler_params=_params(("arbitrary", "arbitrary")),
    )(ug, ug, dcat, conv_w, conv_b, ln_g, ln_b)


def conv_branch_bwd_b(ug, dy, conv_w, *, name, ts=256):
    B, S, _ = ug.shape
    ts = min(ts, S)
    nh, n_halo = ts // HALO, S // HALO

    def body(cur_ref, dy_ref, nxt_ref, w_ref, o_ref, ext, sh):
        last = pl.program_id(1) == pl.num_programs(1) - 1
        ext[pl.ds(0, ts), :] = dy_ref[0]
        ext[pl.ds(ts, HALO), :] = jnp.where(last, 0.0, nxt_ref[0])
        _make_shifted(ext, sh)
        da = jnp.zeros((ts, CONV_CH), F32)
        for j in range(CONV_K):
            da = da + _tap(ext, sh, CONV_K - 1 - j, ts) * w_ref[pl.ds(j, 1), :]
        blk = cur_ref[0]
        u = blk[:, :CONV_CH].astype(F32)
        sg = _sigmoid(blk[:, CONV_CH:].astype(F32))
        o_ref[0, :, :CONV_CH] = (da * sg).astype(BF16)
        o_ref[0, :, CONV_CH:] = (da * u * sg * (1.0 - sg)).astype(BF16)

    return _call(
        body, name=name, grid=(B, S // ts),
        in_specs=[pl.BlockSpec((1, ts, 2 * CONV_CH), lambda b, s: (b, s, 0)),
                  pl.BlockSpec((1, ts, CONV_CH), lambda b, s: (b, s, 0)),
                  pl.BlockSpec((1, HALO, CONV_CH), lambda b, s: (b, jnp.minimum((s + 1) * nh, n_halo - 1), 0)),
                  pl.BlockSpec((HALO, CONV_CH), lambda b, s: (0, 0))],
        out_specs=pl.BlockSpec((1, ts, 2 * CONV_CH), lambda b, s: (b, s, 0)),
        out_shape=jax.ShapeDtypeStruct((B, S, 2 * CONV_CH), BF16),
        scratch_shapes=[pltpu.VMEM((ts + HALO, CONV_CH), F32),
                        pltpu.VMEM((SUBLANES - 1, ts + HALO, CONV_CH), F32)],
        compiler_params=_params(("parallel", "parallel")),
    )(ug, dy, dy, conv_w)


def _tri(n, lower):
    r = lax.broadcasted_iota(jnp.int32, (n, n), 0)
    c = lax.broadcasted_iota(jnp.int32, (n, n), 1)
    return ((r >= c) if lower else (r <= c)).astype(F32)


def _eye(n):
    r = lax.broadcasted_iota(jnp.int32, (n, n), 0)
    c = lax.broadcasted_iota(jnp.int32, (n, n), 1)
    return (r == c).astype(F32)


def _dot_hi(a, b, dn):
    return lax.dot_general(a, b, dn, precision=lax.Precision.HIGHEST, preferred_element_type=F32)


NN = (((1,), (0,)), ((), ()))
NT = (((1,), (1,)), ((), ()))
TN = (((0,), (0,)), ((), ()))


def _log_sigmoid(v):
    e = jnp.exp(-jnp.abs(v))
    log1p_e = jnp.where(e < 1e-3, e * (1.0 - 0.5 * e), jnp.log(1.0 + e))
    return jnp.minimum(v, 0.0) - log1p_e


def fgate_fwd(h, w_f, b_f, *, name, ts=256):
    B, S, D = h.shape
    ts = min(ts, S)

    def body(h_ref, w_ref, b_ref, f_ref, cc_ref, cr_ref, carry):
        @pl.when(pl.program_id(1) == 0)
        def _():
            carry[...] = jnp.zeros_like(carry)

        f = jnp.dot(h_ref[0], w_ref[...], preferred_element_type=F32)
        f_ref[0] = f
        logf = _log_sigmoid(f + b_ref[...])
        c = _dot_hi(_tri(ts, True), logf, NN) + carry[pl.ds(0, 1), :]
        cc_ref[0] = c
        carry[pl.ds(0, 1), :] = c[ts - 1:ts, :]
        cr_ref[0] = _dot_hi(_eye(LANES), c, NT)

    return _call(
        body, name=name, grid=(B, S // ts),
        in_specs=[pl.BlockSpec((1, ts, D), lambda b, s: (b, s, 0)),
                  pl.BlockSpec((D, LANES), lambda b, s: (0, 0)),
                  pl.BlockSpec((1, LANES), lambda b, s: (0, 0))],
        out_specs=[pl.BlockSpec((1, ts, LANES), lambda b, s: (b, s, 0)),
                   pl.BlockSpec((1, ts, LANES), lambda b, s: (b, s, 0)),
                   pl.BlockSpec((1, LANES, ts), lambda b, s: (b, 0, s))],
        out_shape=[jax.ShapeDtypeStruct((B, S, LANES), F32), jax.ShapeDtypeStruct((B, S, LANES), F32),
                   jax.ShapeDtypeStruct((B, LANES, S), F32)],
        scratch_shapes=[pltpu.VMEM((8, LANES), F32)],
        compiler_params=_params(("parallel", "arbitrary")),
    )(h, w_f, b_f)


def fgate_bwd(dc, f, b_f, *, name, ts=256):
    B, S, _ = f.shape
    P = dc.shape[1]
    ts = min(ts, S)
    ns = S // ts

    def body(dc_ref, f_ref, b_ref, df_ref, db_ref, carry):
        @pl.when(pl.program_id(1) == 0)
        def _():
            carry[...] = jnp.zeros_like(carry)

        @pl.when((pl.program_id(0) == 0) & (pl.program_id(1) == 0))
        def _():
            db_ref[...] = jnp.zeros_like(db_ref)

        dc_t = dc_ref[0, 0]
        for j in range(1, P):
            dc_t = dc_t + dc_ref[0, j]
        dlogf = _dot_hi(_tri(ts, False), dc_t, NN) + carry[pl.ds(0, 1), :]
        carry[pl.ds(0, 1), :] = dlogf[0:1, :]
        df = dlogf * _sigmoid(-(f_ref[0] + b_ref[...]))
        df_ref[0] = df.astype(BF16)
        db_ref[...] += jnp.sum(df, axis=0, keepdims=True)

    return _call(
        body, name=name, grid=(B, ns),
        in_specs=[pl.BlockSpec((1, P, ts, LANES), lambda b, s: (b, 0, ns - 1 - s, 0)),
                  pl.BlockSpec((1, ts, LANES), lambda b, s: (b, ns - 1 - s, 0)),
                  pl.BlockSpec((1, LANES), lambda b, s: (0, 0))],
        out_specs=[pl.BlockSpec((1, ts, LANES), lambda b, s: (b, ns - 1 - s, 0)),
                   pl.BlockSpec((1, LANES), lambda b, s: (0, 0))],
        out_shape=[jax.ShapeDtypeStruct((B, S, LANES), BF16), jax.ShapeDtypeStruct((1, LANES), F32)],
        scratch_shapes=[pltpu.VMEM((8, LANES), F32)],
        compiler_params=_params(("arbitrary", "arbitrary")),
    )(dc, f, b_f)


def _lane_pick(tile, idx):
    lane = lax.broadcasted_iota(jnp.int32, tile.shape, 1)
    return jnp.sum(jnp.where(lane == idx, tile, 0.0), axis=-1, keepdims=True)


FOX_T = 512


def _fox_heads(q, cc_ref, p):
    lane = lax.broadcasted_iota(jnp.int32, q.shape, 1)
    qs = q * (1.0 / math.sqrt(FOX_HEAD_DIM))
    qhs = [jnp.where((lane < FOX_HEAD_DIM) == (hh == 0), qs, jnp.zeros_like(qs)) for hh in range(2)]
    crefs = [_lane_pick(cc_ref[0, pl.ds(0, 1), :], 2 * p + hh) for hh in range(2)]
    return qhs, crefs


def _fold_lanes(x, op):
    out = x[:, :LANES]
    for j in range(1, x.shape[1] // LANES):
        out = op(out, x[:, j * LANES:(j + 1) * LANES])
    return out


def _causal(t, transposed):
    r = lax.broadcasted_iota(jnp.int32, (t, t), 0)
    c = lax.broadcasted_iota(jnp.int32, (t, t), 1)
    return (r <= c) if transposed else (c <= r)


QKV0 = 8


def fox_fwd(z, c_col, c_row, *, name, rider=None):
    B, S, _ = z.shape
    assert S % FOX_T == 0
    tq, nq = FOX_T, S // FOX_T
    npair = FOX_HEADS // 2

    def body(q_ref, k_ref, v_ref, cc_ref, cr_ref, o_ref, l_ref, ot_ref, s_scr, m_scr, acc_scr):
        p, qi = pl.program_id(1), pl.program_id(2)
        qhs, crefs = _fox_heads(q_ref[0], cc_ref, p)
        lane = lax.broadcasted_iota(jnp.int32, (tq, LANES), 1)
        first = lane < FOX_HEAD_DIM
        for hh in range(2):
            m_scr[hh] = jnp.full((tq, LANES), NEG, F32)
            acc_scr[hh] = jnp.zeros((tq, LANES), F32)

        def logits(kb, diagonal):
            k0 = pl.multiple_of(kb * tq, tq)
            k = k_ref[0, pl.ds(k0, tq), :]
            for hh in range(2):
                s = lax.dot_general(qhs[hh], k, NT, preferred_element_type=F32)
                s = s + (crefs[hh] - cr_ref[0, pl.ds(2 * p + hh, 1), pl.ds(k0, tq)])
                if diagonal:
                    s = jnp.where(_causal(tq, False), s, NEG)
                s_scr[hh, kb] = s
                m_scr[hh] = jnp.maximum(m_scr[hh], _fold_lanes(s, jnp.maximum))

        def sweep1(kb, carry):
            logits(kb, False)
            return carry

        lax.fori_loop(0, qi, sweep1, 0)
        logits(qi, True)
        ms = [jnp.max(m_scr[hh], axis=-1, keepdims=True) for hh in range(2)]
        mbs = [jnp.broadcast_to(ms[hh], (tq, tq)) for hh in range(2)]

        for hh in range(2):
            m_scr[hh] = jnp.zeros((tq, LANES), F32)

        def weigh(kb, carry):
            k0 = pl.multiple_of(kb * tq, tq)
            v = v_ref[0, pl.ds(k0, tq), :]
            for hh in range(2):
                pr = jnp.exp(s_scr[hh, kb] - mbs[hh])
                m_scr[hh] += _fold_lanes(pr, jnp.add)
                acc_scr[hh] += jnp.dot(pr.astype(BF16), v, preferred_element_type=F32)
            return carry

        lax.fori_loop(0, qi + 1, weigh, 0)
        accs = [acc_scr[hh] for hh in range(2)]
        ls = [jnp.sum(m_scr[hh], axis=-1, keepdims=True) for hh in range(2)]
        out = jnp.where(first, accs[0] / ls[0], accs[1] / ls[1])
        o_ref[0] = out.astype(BF16)
        ot_ref[...] = out.T.astype(BF16)
        l_ref[0, 0] = jnp.where(first, ms[0] + jnp.log(ls[0]), ms[1] + jnp.log(ls[1]))

    return hosted_call(
        body, rider, name=name, grid=(B, npair, nq),
        in_specs=[pl.BlockSpec((1, tq, LANES), lambda b, p, i: (b, i, QKV0 + p)),
                  pl.BlockSpec((1, S, LANES), lambda b, p, i: (b, 0, QKV0 + npair + p)),
                  pl.BlockSpec((1, S, LANES), lambda b, p, i: (b, 0, QKV0 + 2 * npair + p)),
                  pl.BlockSpec((1, tq, LANES), lambda b, p, i: (b, i, 0)),
                  pl.BlockSpec((1, 8, S), lambda b, p, i: (b, 0, 0))],
        out_specs=[pl.BlockSpec((1, tq, LANES), lambda b, p, i: (b, i, p)),
                   pl.BlockSpec((1, 1, tq, LANES), lambda b, p, i: (b, p, i, 0)),
                   pl.BlockSpec((LANES, tq), lambda b, p, i: (p, b * nq + i))],
        out_shape=[jax.ShapeDtypeStruct((B, S, FOX_W), BF16),
                   jax.ShapeDtypeStruct((B, npair, S, LANES), F32),
                   jax.ShapeDtypeStruct((FOX_W, B * S), BF16)],
        scratch_shapes=[pltpu.VMEM((2, nq, tq, tq), F32), pltpu.VMEM((2, tq, LANES), F32),
                        pltpu.VMEM((2, tq, LANES), F32)],
        args=(z, z, z, c_col, c_row),
    )


def fox_bwd_dq(z, dcat, lse, c_col, c_row, *, name, rider=None):
    B, S, _ = z.shape
    tq, nq = FOX_T, S // FOX_T
    npair = FOX_HEADS // 2

    def body(q_ref, k_ref, v_ref, do_ref, l_ref, cc_ref, cr_ref, dq_ref, st_ref, p_scr, dp_scr, dl_scr):
        p, qi = pl.program_id(1), pl.program_id(2)
        qhs, crefs = _fox_heads(q_ref[0], cc_ref, p)
        lane = lax.broadcasted_iota(jnp.int32, (tq, LANES), 1)
        do_b = do_ref[0].astype(BF16)
        dohs = [jnp.where((lane < FOX_HEAD_DIM) == (hh == 0), do_b, jnp.zeros_like(do_b)) for hh in range(2)]
        lses = [_lane_pick(l_ref[0, 0], hh * FOX_HEAD_DIM) for hh in range(2)]
        lbs = [jnp.broadcast_to(lses[hh], (tq, tq)) for hh in range(2)]
        for hh in range(2):
            dl_scr[hh] = jnp.zeros((tq, LANES), F32)

        def probs(kb, diagonal):
            k0 = pl.multiple_of(kb * tq, tq)
            k = k_ref[0, pl.ds(k0, tq), :]
            v = v_ref[0, pl.ds(k0, tq), :]
            for hh in range(2):
                s = lax.dot_general(qhs[hh], k, NT, preferred_element_type=F32)
                s = s + (crefs[hh] - cr_ref[0, pl.ds(2 * p + hh, 1), pl.ds(k0, tq)])
                pr = jnp.exp(s - lbs[hh])
                if diagonal:
                    pr = jnp.where(_causal(tq, False), pr, 0.0)
                dp = lax.dot_general(dohs[hh], v, NT, preferred_element_type=F32)
                pdp = pr * dp
                dl_scr[hh] += _fold_lanes(pdp, jnp.add)
                p_scr[hh, kb] = pr
                dp_scr[hh, kb] = dp

        def first_pass(kb, carry):
            probs(kb, False)
            return carry

        lax.fori_loop(0, qi, first_pass, 0)
        probs(qi, True)

        dls = [jnp.sum(dl_scr[hh], axis=-1, keepdims=True) for hh in range(2)]
        dlbs = [jnp.broadcast_to(dls[hh], (tq, tq)) for hh in range(2)]

        def second_pass(kb, dq):
            k0 = pl.multiple_of(kb * tq, tq)
            k = k_ref[0, pl.ds(k0, tq), :]
            for hh in range(2):
                ds = p_scr[hh, kb] * (dp_scr[hh, kb] - dlbs[hh])
                kh = jnp.where((lane < FOX_HEAD_DIM) == (hh == 0), k, jnp.zeros_like(k))
                dq = dq + jnp.dot(ds.astype(BF16), kh, preferred_element_type=F32)
            return dq

        dq = lax.fori_loop(0, qi + 1, second_pass, jnp.zeros((tq, LANES), F32))
        dq_ref[0] = (dq * (1.0 / math.sqrt(FOX_HEAD_DIM))).astype(BF16)
        cols = jnp.zeros((tq, LANES), F32)
        for j, col in enumerate([crefs[0] - lses[0], crefs[1] - lses[1], dls[0], dls[1]]):
            cols = jnp.where(lane == j, col, cols)
        st_ref[0, 0] = _dot_hi(_eye(LANES), cols, NT)[:8]

    return hosted_call(
        body, rider, name=name, grid=(B, npair, nq),
        in_specs=[pl.BlockSpec((1, tq, LANES), lambda b, p, i: (b, i, QKV0 + p)),
                  pl.BlockSpec((1, S, LANES), lambda b, p, i: (b, 0, QKV0 + npair + p)),
                  pl.BlockSpec((1, S, LANES), lambda b, p, i: (b, 0, QKV0 + 2 * npair + p)),
                  pl.BlockSpec((1, tq, LANES), lambda b, p, i: (b, i, npair + p)),
                  pl.BlockSpec((1, 1, tq, LANES), lambda b, p, i: (b, p, i, 0)),
                  pl.BlockSpec((1, tq, LANES), lambda b, p, i: (b, i, 0)),
                  pl.BlockSpec((1, 8, S), lambda b, p, i: (b, 0, 0))],
        out_specs=[pl.BlockSpec((1, tq, LANES), lambda b, p, i: (b, i, p)),
                   pl.BlockSpec((1, 1, 8, tq), lambda b, p, i: (b, p, 0, i))],
        out_shape=[jax.ShapeDtypeStruct((B, S, FOX_W), BF16), jax.ShapeDtypeStruct((B, npair, 8, S), F32)],
        scratch_shapes=[pltpu.VMEM((2, nq, tq, tq), F32), pltpu.VMEM((2, nq, tq, tq), F32),
                        pltpu.VMEM((2, tq, LANES), F32)],
        args=(z, z, z, dcat, lse, c_col, c_row), vmem=56 << 20,
    )


def fox_bwd_dkdv(z, dcat, stats, c_col, *, name, rider=None):
    B, S, _ = z.shape
    tk, nq = FOX_T, S // FOX_T
    npair = FOX_HEADS // 2
    inv = 1.0 / math.sqrt(FOX_HEAD_DIM)

    def body(q_ref, k_ref, v_ref, do_ref, st_ref, cc_ref, dk_ref, dv_ref, dc_ref, dk_scr, dv_scr, dc_scr):
        p, kt = pl.program_id(1), pl.program_id(2)
        lane = lax.broadcasted_iota(jnp.int32, (tk, LANES), 1)
        masks = [(lane < FOX_HEAD_DIM) == (hh == 0) for hh in range(2)]
        k = k_ref[0]
        v = v_ref[0]
        khs = [jnp.where(masks[hh], k, jnp.zeros_like(k)) for hh in range(2)]
        vhs = [jnp.where(masks[hh], v, jnp.zeros_like(v)) for hh in range(2)]
        ccbs = [jnp.broadcast_to(_lane_pick(cc_ref[0], 2 * p + hh), (tk, tk)) for hh in range(2)]
        dk_scr[...] = jnp.zeros_like(dk_scr)
        dv_scr[...] = jnp.zeros_like(dv_scr)
        dc_scr[...] = jnp.zeros_like(dc_scr)

        def tile(qb, diagonal):
            q0 = pl.multiple_of(qb * tk, tk)
            qs = q_ref[0, pl.ds(q0, tk), :] * inv
            do_b = do_ref[0, pl.ds(q0, tk), :].astype(BF16)
            for hh in range(2):
                st = lax.dot_general(khs[hh], qs, NT, preferred_element_type=F32)
                pr = jnp.exp(st - ccbs[hh] + st_ref[0, 0, pl.ds(hh, 1), pl.ds(q0, tk)])
                if diagonal:
                    pr = jnp.where(_causal(tk, True), pr, 0.0)
                dp = lax.dot_general(vhs[hh], do_b, NT, preferred_element_type=F32)
                ds = pr * (dp - st_ref[0, 0, pl.ds(2 + hh, 1), pl.ds(q0, tk)])
                dv_scr[...] += jnp.dot(pr.astype(BF16), jnp.where(masks[hh], do_b, jnp.zeros_like(do_b)),
                                       preferred_element_type=F32)
                dk_scr[...] += jnp.dot(ds.astype(BF16), jnp.where(masks[hh], qs, jnp.zeros_like(qs)),
                                       preferred_element_type=F32)
                dc_scr[hh] -= _fold_lanes(ds, jnp.add)

        def later(qb, carry):
            tile(qb, False)
            return carry

        tile(kt, True)
        lax.fori_loop(kt + 1, nq, later, 0)
        dk_ref[0] = dk_scr[...].astype(BF16)
        dv_ref[0] = dv_scr[...].astype(BF16)
        dcs = [jnp.sum(dc_scr[hh], axis=-1, keepdims=True) for hh in range(2)]
        dc_ref[0, 0] = jnp.where(lane == 2 * p, dcs[0], jnp.where(lane == 2 * p + 1, dcs[1], 0.0))

    full = lambda col: pl.BlockSpec((1, S, LANES), col)
    tile_spec = lambda col: pl.BlockSpec((1, tk, LANES), col)
    return hosted_call(
        body, rider, name=name, grid=(B, npair, nq),
        in_specs=[full(lambda b, p, t: (b, 0, QKV0 + p)),
                  tile_spec(lambda b, p, t: (b, t, QKV0 + npair + p)),
                  tile_spec(lambda b, p, t: (b, t, QKV0 + 2 * npair + p)),
                  full(lambda b, p, t: (b, 0, npair + p)),
                  pl.BlockSpec((1, 1, 8, S), lambda b, p, t: (b, p, 0, 0)),
                  tile_spec(lambda b, p, t: (b, t, 0))],
        out_specs=[tile_spec(lambda b, p, t: (b, t, p)), tile_spec(lambda b, p, t: (b, t, p)),
                   pl.BlockSpec((1, 1, tk, LANES), lambda b, p, t: (b, p, t, 0))],
        out_shape=[jax.ShapeDtypeStruct((B, S, FOX_W), BF16)] * 2
        + [jax.ShapeDtypeStruct((B, npair, S, LANES), F32)],
        scratch_shapes=[pltpu.VMEM((tk, LANES), F32), pltpu.VMEM((tk, LANES), F32),
                        pltpu.VMEM((2, tk, LANES), F32)],
        args=(z, z, z, dcat, stats, c_col),
    )


def xattn_fwd(qm, kv, *, name, tq=256):
    B, S, D = qm.shape
    M = kv.shape[1]
    tq = min(tq, S)
    inv = 1.0 / math.sqrt(MEM_HEAD_DIM)

    nq = S // tq

    def body(q_ref, kv_ref, o_ref, ot_ref):
        for h in range(MEM_HEADS):
            c0 = h * MEM_HEAD_DIM
            qh = q_ref[0, :, c0:c0 + MEM_HEAD_DIM]
            kh = kv_ref[0, :, c0:c0 + MEM_HEAD_DIM]
            vh = kv_ref[0, :, D + c0:D + c0 + MEM_HEAD_DIM]
            s = lax.dot_general(qh, kh, NT, preferred_element_type=F32) * inv
            e = jnp.exp(s - jnp.max(s, axis=-1, keepdims=True))
            o = jnp.dot(e.astype(BF16), vh, preferred_element_type=F32) / jnp.sum(e, axis=-1, keepdims=True)
            o_ref[0, :, c0:c0 + MEM_HEAD_DIM] = o.astype(BF16)
            ot_ref[c0:c0 + MEM_HEAD_DIM, :] = o.T.astype(BF16)

    return _call(
        body, name=name, grid=(B, nq),
        in_specs=[pl.BlockSpec((1, tq, D), lambda b, i: (b, i, 0)),
                  pl.BlockSpec((1, M, 2 * D), lambda b, i: (b, 0, 0))],
        out_specs=[pl.BlockSpec((1, tq, D), lambda b, i: (b, i, 0)),
                   pl.BlockSpec((D, tq), lambda b, i: (0, b * nq + i))],
        out_shape=[jax.ShapeDtypeStruct((B, S, D), BF16), jax.ShapeDtypeStruct((D, B * S), BF16)],
        compiler_params=_params(("parallel", "parallel")),
    )(qm, kv)


def xattn_bwd(qm, kv, do, *, name, tq=256):
    B, S, D = qm.shape
    M = kv.shape[1]
    tq = min(tq, S)
    inv = 1.0 / math.sqrt(MEM_HEAD_DIM)

    def body(q_ref, kv_ref, do_ref, dq_ref, dkv_ref):
        @pl.when(pl.program_id(1) == 0)
        def _():
            dkv_ref[...] = jnp.zeros_like(dkv_ref)

        for h in range(MEM_HEADS):
            c0 = h * MEM_HEAD_DIM
            qh = q_ref[0, :, c0:c0 + MEM_HEAD_DIM]
            kh = kv_ref[0, :, c0:c0 + MEM_HEAD_DIM]
            vh = kv_ref[0, :, D + c0:D + c0 + MEM_HEAD_DIM]
            doh = do_ref[0, :, c0:c0 + MEM_HEAD_DIM]
            s = lax.dot_general(qh, kh, NT, preferred_element_type=F32) * inv
            e = jnp.exp(s - jnp.max(s, axis=-1, keepdims=True))
            pr = e / jnp.sum(e, axis=-1, keepdims=True)
            dp = lax.dot_general(doh, vh, NT, preferred_element_type=F32)
            ds = pr * (dp - jnp.sum(pr * dp, axis=-1, keepdims=True))
            ds_b = ds.astype(BF16)
            dq_ref[0, :, c0:c0 + MEM_HEAD_DIM] = (jnp.dot(ds_b, kh, preferred_element_type=F32) * inv).astype(BF16)
            dkv_ref[0, :, c0:c0 + MEM_HEAD_DIM] += lax.dot_general(ds_b, qh, TN, preferred_element_type=F32) * inv
            dkv_ref[0, :, D + c0:D + c0 + MEM_HEAD_DIM] += lax.dot_general(
                pr.astype(BF16), doh, TN, preferred_element_type=F32)

    row = pl.BlockSpec((1, tq, D), lambda b, i: (b, i, 0))
    kvs = pl.BlockSpec((1, M, 2 * D), lambda b, i: (b, 0, 0))
    return _call(
        body, name=name, grid=(B, S // tq), in_specs=[row, kvs, row], out_specs=[row, kvs],
        out_shape=[jax.ShapeDtypeStruct((B, S, D), BF16), jax.ShapeDtypeStruct((B, M, 2 * D), F32)],
        compiler_params=_params(("parallel", "arbitrary")),
    )(qm, kv, do)


def swiglu_fwd(gu, *, name, tm=256):
    T, F2 = gu.shape
    Fh = F2 // 2
    tm = min(tm, T)

    def body(gu_ref, o_ref, ot_ref):
        g = gu_ref[:, :Fh].astype(F32)
        u = gu_ref[:, Fh:].astype(F32)
        act = g * _sigmoid(g) * u
        o_ref[...] = act.astype(BF16)
        ot_ref[...] = act.T.astype(BF16)

    return _call(
        body, name=name, grid=(T // tm,),
        in_specs=[pl.BlockSpec((tm, F2), lambda i: (i, 0))],
        out_specs=[pl.BlockSpec((tm, Fh), lambda i: (i, 0)), pl.BlockSpec((Fh, tm), lambda i: (0, i))],
        out_shape=[jax.ShapeDtypeStruct((T, Fh), BF16), jax.ShapeDtypeStruct((Fh, T), BF16)],
        compiler_params=_params(("parallel",)),
    )(gu)


def swiglu_bwd(gu, dact, *, name, tm=256):
    T, F2 = gu.shape
    Fh = F2 // 2
    tm = min(tm, T)

    def body(gu_ref, d_ref, o_ref):
        g = gu_ref[:, :Fh].astype(F32)
        u = gu_ref[:, Fh:].astype(F32)
        d = d_ref[...].astype(F32)
        sg = _sigmoid(g)
        o_ref[:, :Fh] = (d * u * (sg * (1.0 + g * (1.0 - sg)))).astype(BF16)
        o_ref[:, Fh:] = (d * g * sg).astype(BF16)

    return _call(
        body, name=name, grid=(T // tm,),
        in_specs=[pl.BlockSpec((tm, F2), lambda i: (i, 0)), pl.BlockSpec((tm, Fh), lambda i: (i, 0))],
        out_specs=pl.BlockSpec((tm, F2), lambda i: (i, 0)),
        out_shape=jax.ShapeDtypeStruct((T, F2), BF16),
        compiler_params=_params(("parallel",)),
    )(gu, dact)


LATE = ("w_out", "w_mq", "w_mkv", "w_mo", "w_gu", "w_down")
RS_GROUPS = (("w_gu", "w_down"), ("w_out", "w_mq", "w_mkv", "w_mo"), ("w_in",))


def reduce_to_chips(names, gw, *, tag):
    g42 = [_shards_from_full(n, gw[n]) for n in names]
    got = sibling_exchange(g42, name="rs_sibling_" + tag)
    return [pair_sum(g, o, name="rs_pair_sum_" + n) for n, g, o in zip(names, g42, got)]


def local_step(x, mem, target, sp, w_in_full, late_shards):
    B, S, D = x.shape
    T = B * S
    M = mem.shape[1]
    row = lambda v: v.reshape(1, -1).astype(F32)
    g_mix, g_x, g_mem, g_ffn, g_final = (row(sp[k]) for k in ("g_mix", "g_x", "g_mem", "g_ffn", "g_final"))
    conv_b, ln_g, ln_b = row(sp["conv_b"]), row(sp["ln_g"]), row(sp["ln_b"])
    conv_w = jnp.pad(sp["conv_w"].astype(F32), ((0, HALO - CONV_K), (0, 0)))
    b_f = jnp.pad(row(sp["b_f"]), ((0, 0), (0, LANES - FOX_HEADS)))
    n_main = 2 * CONV_CH + 3 * FOX_W
    w_main = w_in_full[:, :n_main]
    w_f = jnp.pad(w_in_full[:, n_main:], ((0, 0), (0, LANES - FOX_HEADS)))

    x2d = x.reshape(T, D)
    h, h_t = rmsnorm_fwd(x2d, g_mix, name="rms_mix")
    z = matmul(h, w_main, out_dtype=BF16, tn=n_main, name="mm_in")
    z3 = z.reshape(B, S, n_main)
    conv_out, conv_t = conv_branch_fwd(z3, conv_w, conv_b, ln_g, ln_b, name="conv_fwd")
    f_raw, c_col, c_row = fgate_fwd(h.reshape(B, S, D), w_f, b_f, name="fgate_fwd")
    (att, lse, att_t), partly = fox_fwd(z3, c_col, c_row, name="fox_fwd", rider=AllGatherStage1(late_shards))
    gathered = all_gather_stage2(partly, name="ag_late_stage2")
    wf = {n: _full_from_gathered(n, blk) for n, blk in zip(LATE, gathered)}
    cat =jnp.concatenate([conv_out, att], axis=-1).reshape(T, D)
    x1 = matmul(cat, wf["w_out"], out_dtype=F32, res=x2d, tn=D, name="mm_out")
    hx, hx_t = rmsnorm_fwd(x1, g_x, name="rms_x")
    qm = matmul(hx, wf["w_mq"], out_dtype=BF16, tn=D, name="mm_mq")
    mem2d = mem.reshape(B * M, D)
    mem_n, mem_n_t = rmsnorm_fwd(mem2d, g_mem, name="rms_mem")
    kv = matmul(mem_n, wf["w_mkv"], out_dtype=BF16, tn=2 * D, name="mm_mkv").reshape(B, M, 2 * D)
    o, o_t = xattn_fwd(qm.reshape(B, S, D), kv, name="xattn_fwd")
    o = o.reshape(T, D)
    x2 = matmul(o, wf["w_mo"], out_dtype=F32, res=x1, tn=D, name="mm_mo")
    hf, hf_t = rmsnorm_fwd(x2, g_ffn, name="rms_ffn")
    gu = matmul(hf, wf["w_gu"], out_dtype=BF16, tn=2816, name="mm_gu")
    act, act_t = swiglu_fwd(gu, name="swiglu_fwd")
    x3 = matmul(act, wf["w_down"], out_dtype=F32, res=x2, tn=D, name="mm_down")
    dx3, dg_final, loss = final_loss_bwd(x3, g_final, target.reshape(T, D), name="loss_bwd")
    gw = {}
    gw["w_down"] = matmul(act_t, dx3, out_dtype=BF16, tm=1408, tn=256, name="dw_down")
    dact = matmul(dx3, wf["w_down"], tb=True, out_dtype=BF16, tn=2816, name="dx_down")
    dgu = swiglu_bwd(gu, dact, name="swiglu_bwd")
    gw["w_gu"] = matmul(hf_t, dgu, out_dtype=BF16, tn=1408, name="dw_gu")
    dhf = matmul(dgu, wf["w_gu"], tb=True, out_dtype=BF16, tm=256, tn=D, name="dx_gu")
    dx2, dg_ffn = rmsnorm_bwd(x2, g_ffn, dhf, dx3, name="rms_ffn_bwd")
    gw["w_mo"] = matmul(o_t, dx2, out_dtype=BF16, name="dw_mo")
    do = matmul(dx2, wf["w_mo"], tb=True, out_dtype=BF16, tn=D, name="dx_mo")
    dqm, dkv = xattn_bwd(qm.reshape(B, S, D), kv, do.reshape(B, S, D), name="xattn_bwd")
    dqm = dqm.reshape(T, D)
    dkv = dkv.reshape(B * M, 2 * D)
    gw["w_mq"] = matmul(hx_t, dqm, out_dtype=BF16, tn=D, name="dw_mq")
    dhx = matmul(dqm, wf["w_mq"], tb=True, out_dtype=BF16, tn=D, name="dx_mq")
    gw["w_mkv"] = matmul(mem_n_t, dkv, out_dtype=BF16, tn=D, name="dw_mkv")
    dmem_n = matmul(dkv, wf["w_mkv"], tb=True, out_dtype=BF16, tn=D, name="dx_mkv")
    _, dg_mem = rmsnorm_bwd(mem2d, g_mem, dmem_n, None, name="rms_mem_bwd")
    dx1, dg_x = rmsnorm_bwd(x1, g_x, dhx, dx2, name="rms_x_bwd")
    gw["w_out"] = jnp.concatenate([matmul(conv_t, dx1, out_dtype=BF16, name="dw_out_conv"),
                                   matmul(att_t, dx1, out_dtype=BF16, name="dw_out_att")], axis=0)
    dcat = matmul(dx1, wf["w_out"], tb=True, out_dtype=BF16, tn=D, name="dx_out").reshape(B, S, D)
    dy, dconv_w, dvec = conv_branch_bwd_a(z3, dcat, conv_w, conv_b, ln_g, ln_b, name="conv_bwd_a")
    dug = conv_branch_bwd_b(z3, dy, conv_w, name="conv_bwd_b")
    parts, gots = {}, {}
    for n, p in zip(RS_GROUPS[0], reduce_to_chips(RS_GROUPS[0], gw, tag="ffn")):
        parts[n] = p
    for n, p in zip(RS_GROUPS[1], reduce_to_chips(RS_GROUPS[1], gw, tag="mid")):
        parts[n] = p
    (dq, stats), got = fox_bwd_dq(z3, dcat, lse, c_col, c_row, name="fox_bwd_dq",
                                  rider=ChipExchange([parts[n] for n in RS_GROUPS[0]]))
    gots.update(zip(RS_GROUPS[0], got))
    (dk, dv, dc), got = fox_bwd_dkdv(z3, dcat, stats, c_col, name="fox_bwd_dkdv",
                                     rider=ChipExchange([parts[n] for n in RS_GROUPS[1]]))
    gots.update(zip(RS_GROUPS[1], got))
    df, db_f = fgate_bwd(dc, f_raw, b_f, name="fgate_bwd")
    dz = jnp.concatenate([dug, dq, dk, dv], axis=-1).reshape(T, n_main)
    df2 = df.reshape(T, LANES)
    dw_main = matmul(h_t, dz, out_dtype=BF16, tn=1280, name="dw_in")
    dw_f = matmul(h_t, df2, out_dtype=BF16, name="dw_f")
    gw["w_in"] = jnp.concatenate([dw_main, dw_f[:, :FOX_HEADS]], axis=-1)
    dh_f = matmul(df2, w_f, tb=True, out_dtype=F32, tn=D, name="dx_f")
    parts["w_in"] = reduce_to_chips(RS_GROUPS[2], gw, tag="in")[0]
    dh, (gots["w_in"],) = matmul(dz, w_main, tb=True, out_dtype=F32, res=dh_f, tn=D, name="dx_in",
                                 rider=ChipExchange([parts["w_in"]]))
    dx, dg_mix = rmsnorm_bwd(x2d, g_mix, dh, dx1, name="rms_mix_bwd")
    gs = dict(g_mix=dg_mix, b_f=db_f[:, :FOX_HEADS], conv_w=dconv_w[:CONV_K], conv_b=dvec[0:1],
              ln_g=dvec[1:2], ln_b=dvec[2:3], g_x=dg_x, g_mem=dg_mem, g_ffn=dg_ffn, g_final=dg_final)
    return loss, dx.reshape(B, S, D), gs, {n: (parts[n], gots[n]) for n in BIG}


def _me():
    return lax.axis_index("x"), lax.axis_index("y"), lax.axis_index("c")


def _any_specs(n):
    return [pl.BlockSpec(memory_space=pl.ANY)] * n


def all_gather(xs, *, name):
    n = len(xs)

    def body(*refs):
        x_refs, out_refs = refs[:n], refs[n:2 * n]
        send_sems, recv_sems, local_sems = refs[2 * n:]
        x, y, c = _me()
        me, sibling = (x, y, c), (x, y, 1 - c)
        chips = [(1 - x, y), (x, 1 - y), (1 - x, 1 - y)]

        def slot(a, px, py, pc):
            return out_refs[a].at[4 * px + 2 * py + pc]

        def copy(a, k, block, to, own=False):
            return pltpu.make_async_remote_copy(
                src_ref=x_refs[a] if own else slot(a, *block), dst_ref=slot(a, *block),
                send_sem=send_sems.at[k, a], recv_sem=recv_sems.at[k, a], device_id=to, device_id_type=MESH)

        mine = [pltpu.make_async_copy(x_refs[a], slot(a, *me), local_sems.at[a]) for a in range(n)]
        first = [copy(a, 0, me, sibling, own=True) for a in range(n)]
        first += [copy(a, 1 + j, me, (*chip, c), own=True) for j, chip in enumerate(chips) for a in range(n)]
        for cp in mine + first:
            cp.start()
        passed = []
        for j, chip in enumerate(chips):
            for a in range(n):
                copy(a, 1 + j, (*chip, c), me).wait_recv()
                passed.append(copy(a, 4 + j, (*chip, c), sibling))
                passed[-1].start()
        for a in range(n):
            copy(a, 0, sibling, me).wait_recv()
            for j, chip in enumerate(chips):
                copy(a, 4 + j, (*chip, 1 - c), me).wait_recv()
        for cp in first + passed:
            cp.wait_send()
        for cp in mine:
            cp.wait()

    return _call(
        body, name=name, in_specs=_any_specs(n), out_specs=_any_specs(n),
        out_shape=[jax.ShapeDtypeStruct((N_DEV,) + v.shape, v.dtype) for v in xs],
        scratch_shapes=[pltpu.SemaphoreType.DMA((7, n)), pltpu.SemaphoreType.DMA((7, n)),
                        pltpu.SemaphoreType.DMA((n,))],
    )(*xs)


def sibling_exchange(gs, *, name):
    n = len(gs)

    def body(*refs):
        g_refs, out_refs = refs[:n], refs[n:2 * n]
        send_sems, recv_sems = refs[2 * n:]
        x, y, c = _me()
        cps = [pltpu.make_async_remote_copy(
            src_ref=g_refs[a].at[:, 1 - c], dst_ref=out_refs[a], send_sem=send_sems.at[a],
            recv_sem=recv_sems.at[a], device_id=(x, y, 1 - c), device_id_type=MESH) for a in range(n)]
        for cp in cps:
            cp.start()
        for cp in cps:
            cp.wait()

    return _call(
        body, name=name, in_specs=_any_specs(n), out_specs=_any_specs(n),
        out_shape=[jax.ShapeDtypeStruct((4,) + g.shape[2:], g.dtype) for g in gs],
        scratch_shapes=[pltpu.SemaphoreType.DMA((n,)), pltpu.SemaphoreType.DMA((n,))],
    )(*gs)


class ChipExchange:
    def __init__(self, ps):
        n = len(ps)
        self.n, self.inputs = n, list(ps)
        self.out_shape = [jax.ShapeDtypeStruct(p.shape, p.dtype) for p in ps]
        self.scratch = [pltpu.SemaphoreType.DMA((3, n)), pltpu.SemaphoreType.DMA((3, n))]

    def _copies(self, p_refs, out_refs, sems, outgoing):
        send_sems, recv_sems = sems
        x, y, c = _me()
        my_chip = 2 * x + y
        cps = []
        for k in range(3):
            px, py = x ^ ((k + 1) >> 1), y ^ ((k + 1) & 1)
            src, dst = (2 * px + py, my_chip) if outgoing else (my_chip, 2 * px + py)
            for a in range(self.n):
                cps.append(pltpu.make_async_remote_copy(
                    src_ref=p_refs[a].at[src], dst_ref=out_refs[a].at[dst], send_sem=send_sems.at[k, a],
                    recv_sem=recv_sems.at[k, a], device_id=(px, py, c), device_id_type=MESH))
        return cps

    def start(self, in_refs, out_refs, sems):
        for cp in self._copies(in_refs, out_refs, sems, True):
            cp.start()

    def finish(self, in_refs, out_refs, sems):
        for cp in self._copies(in_refs, out_refs, sems, False):
            cp.wait_recv()
        for cp in self._copies(in_refs, out_refs, sems, True):
            cp.wait_send()


class AllGatherStage1:
    def __init__(self, xs):
        n = len(xs)
        self.n, self.inputs = n, list(xs)
        self.out_shape = [jax.ShapeDtypeStruct((N_DEV,) + v.shape, v.dtype) for v in xs]
        self.scratch = [pltpu.SemaphoreType.DMA((4, n)), pltpu.SemaphoreType.DMA((4, n)),
                        pltpu.SemaphoreType.DMA((n,))]

    def _copies(self, x_refs, out_refs, sems, kind):
        send_sems, recv_sems, local_sems = sems
        x, y, c = _me()
        slot = lambda a, d: out_refs[a].at[4 * d[0] + 2 * d[1] + d[2]]
        if kind == "local":
            return [pltpu.make_async_copy(x_refs[a], slot(a, (x, y, c)), local_sems.at[a]) for a in range(self.n)]
        cps = []
        for k, peer in enumerate([(x, y, 1 - c), (1 - x, y, c), (x, 1 - y, c), (1 - x, 1 - y, c)]):
            for a in range(self.n):
                cps.append(pltpu.make_async_remote_copy(
                    src_ref=x_refs[a], dst_ref=slot(a, (x, y, c) if kind == "out" else peer),
                    send_sem=send_sems.at[k, a], recv_sem=recv_sems.at[k, a], device_id=peer, device_id_type=MESH))
        return cps

    def start(self, in_refs, out_refs, sems):
        for cp in self._copies(in_refs, out_refs, sems, "local") + self._copies(in_refs, out_refs, sems, "out"):
            cp.start()

    def finish(self, in_refs, out_refs, sems):
        for cp in self._copies(in_refs, out_refs, sems, "in"):
            cp.wait_recv()
        for cp in self._copies(in_refs, out_refs, sems, "out"):
            cp.wait_send()
        for cp in self._copies(in_refs, out_refs, sems, "local"):
            cp.wait()


def all_gather_stage2(outs, *, name):
    n = len(outs)

    def body(*refs):
        out_refs = refs[n:2 * n]
        send_sems, recv_sems = refs[2 * n:]
        x, y, c = _me()
        sends, recvs = [], []
        for k, (px, py) in enumerate([(1 - x, y), (x, 1 - y), (1 - x, 1 - y)]):
            for a in range(n):
                mk = lambda pc: pltpu.make_async_remote_copy(
                    src_ref=out_refs[a].at[4 * px + 2 * py + c], dst_ref=out_refs[a].at[4 * px + 2 * py + pc],
                    send_sem=send_sems.at[k, a], recv_sem=recv_sems.at[k, a], device_id=(x, y, 1 - c),
                    device_id_type=MESH)
                sends.append(mk(c))
                recvs.append(mk(1 - c))
        for cp in sends:
            cp.start()
        for cp in recvs:
            cp.wait_recv()
        for cp in sends:
            cp.wait_send()

    return _call(
        body, name=name, in_specs=_any_specs(n), out_specs=_any_specs(n),
        out_shape=[jax.ShapeDtypeStruct(o.shape, o.dtype) for o in outs],
        input_output_aliases={a: a for a in range(n)},
        scratch_shapes=[pltpu.SemaphoreType.DMA((3, n)), pltpu.SemaphoreType.DMA((3, n))],
    )(*outs)


def hosted_call(body, rider, *, name, grid, in_specs, out_specs, out_shape, scratch_shapes, args, vmem=None):
    n_in, n_out, n_scr = len(in_specs), len(out_specs), len(scratch_shapes)
    r_in, r_out = (len(rider.inputs), len(rider.out_shape)) if rider is not None else (0, 0)

    def wrapped(*refs):
        ins, refs = refs[:n_in], refs[n_in:]
        rins, refs = refs[:r_in], refs[r_in:]
        outs, refs = refs[:n_out], refs[n_out:]
        routs, refs = refs[:r_out], refs[r_out:]
        scr, rscr = refs[:n_scr], refs[n_scr:]
        ids = [pl.program_id(d) for d in range(len(grid))]
        first = functools.reduce(jnp.logical_and, [i == 0 for i in ids], True)
        last = functools.reduce(jnp.logical_and, [i == g - 1 for i, g in zip(ids, grid)], True)
        if rider is not None and grid:
            pl.when(first)(lambda: rider.start(rins, routs, rscr))
        elif rider is not None:
            rider.start(rins, routs, rscr)
        if body is not None:
            body(*ins, *outs, *scr)
        if rider is not None and grid:
            pl.when(last)(lambda: rider.finish(rins, routs, rscr))
        elif rider is not None:
            rider.finish(rins, routs, rscr)

    kw = dict(grid=grid) if grid else {}
    if grid or vmem is not None:
        kw["compiler_params"] = _params(("arbitrary",) * len(grid) if grid else None, vmem)
    res = _call(
        wrapped, name=name, in_specs=list(in_specs) + _any_specs(r_in), out_specs=list(out_specs) + _any_specs(r_out),
        out_shape=list(out_shape) + (rider.out_shape if rider is not None else []),
        scratch_shapes=list(scratch_shapes) + (rider.scratch if rider is not None else []), **kw,
    )(*args, *(rider.inputs if rider is not None else []))
    return list(res[:n_out]), list(res[n_out:])


def _pick_rows(r, target=256):
    best = None
    for d in range(16, min(r, target) + 1, 16):
        if r % d == 0:
            best = d
    return r if best is None else best


def pair_sum(g, got, *, name):
    _, _, R, C = g.shape
    tr = _pick_rows(R)

    def body(g_ref, got_ref, o_ref):
        mine = jnp.where(lax.axis_index("c") == 0, g_ref[:, 0], g_ref[:, 1])
        o_ref[...] = (mine.astype(F32) + got_ref[...].astype(F32)).astype(o_ref.dtype)

    return _call(
        body, name=name, grid=(R // tr,),
        in_specs=[pl.BlockSpec((4, 2, tr, C), lambda i: (0, 0, i, 0)), pl.BlockSpec((4, tr, C), lambda i: (0, i, 0))],
        out_specs=pl.BlockSpec((4, tr, C), lambda i: (0, i, 0)),
        out_shape=jax.ShapeDtypeStruct((4, R, C), g.dtype),
        compiler_params=_params(("parallel",)),
    )(g, got)


def chip_sum_adamw(p, got, w, m, v, *, name):
    _, R, C = p.shape
    tr = _pick_rows(R)

    def body(p_ref, got_ref, w_ref, m_ref, v_ref, g_ref, d_ref, mo_ref, vo_ref):
        my_chip = 2 * lax.axis_index("x") + lax.axis_index("y")
        g = jnp.zeros((tr, C), F32)
        for j in range(4):
            g = g + jnp.where(my_chip == j, p_ref[j], got_ref[j]).astype(F32)
        g_ref[...] = g
        d_ref[...], mo_ref[...], vo_ref[...] = _adamw_math(w_ref[...], g, m_ref[...], v_ref[...])

    part = pl.BlockSpec((4, tr, C), lambda i: (0, i, 0))
    spec = pl.BlockSpec((tr, C), lambda i: (i, 0))
    return _call(
        body, name=name, grid=(R // tr,), in_specs=[part, part, spec, spec, spec], out_specs=[spec] * 4,
        out_shape=[jax.ShapeDtypeStruct((R, C), F32)] * 4,
        compiler_params=_params(("parallel",)),
    )(p, got, w, m, v)


def rows_sum(g8, *, name):
    _, R, C = g8.shape

    def body(g_ref, o_ref):
        acc = g_ref[0]
        for j in range(1, N_DEV):
            acc = acc + g_ref[j]
        o_ref[...] = acc

    return _call(body, name=name, out_shape=jax.ShapeDtypeStruct((R, C), F32))(g8)


def _adamw_math(w, g, m, v):
    m = ADAM_B1 * m + (1.0 - ADAM_B1) * g
    v = ADAM_B2 * v + (1.0 - ADAM_B2) * (g * g)
    m_hat = m / (1.0 - ADAM_B1 ** ADAM_STEP)
    v_hat = v / (1.0 - ADAM_B2 ** ADAM_STEP)
    delta = -ADAM_LR * (m_hat / (jnp.sqrt(v_hat) + ADAM_EPS) + ADAM_WD * w)
    return delta, m, v


def adamw_small(wgmv, *, name):
    n = len(wgmv)

    def body(*refs):
        ins, outs = refs[:4 * n], refs[4 * n:]
        for a in range(n):
            w_ref, g_ref, m_ref, v_ref = ins[4 * a:4 * a + 4]
            d, mn, vn = _adamw_math(w_ref[...], g_ref[...], m_ref[...], v_ref[...])
            outs[3 * a][...] = d
            outs[3 * a + 1][...] = mn
            outs[3 * a + 2][...] = vn

    flat = [t for tup in wgmv for t in tup]
    res = _call(
        body, name=name,
        out_shape=[jax.ShapeDtypeStruct(tup[0].shape, F32) for tup in wgmv for _ in range(3)],
    )(*flat)
    return [tuple(res[3 * a:3 * a + 3]) for a in range(n)]


BIG = ("w_in", "w_out", "w_mq", "w_mkv", "w_mo", "w_gu", "w_down")
COL_SHARDED = ("w_in", "w_mkv", "w_gu")
SMALL = ("g_mix", "b_f", "conv_w", "conv_b", "ln_g", "ln_b", "g_x", "g_mem", "g_ffn", "g_final")


def _full_from_gathered(n, blk):
    _, rr, cc = blk.shape
    if n in COL_SHARDED:
        return jnp.concatenate([blk[k] for k in range(N_DEV)], axis=1)
    return blk.reshape(N_DEV * rr, cc)


def _shards_from_full(n, g):
    rr, cc = g.shape
    if n in COL_SHARDED:
        w = cc // N_DEV
        return jnp.stack([g[:, k * w:(k + 1) * w] for k in range(N_DEV)]).reshape(4, 2, rr, w)
    return g.reshape(4, 2, rr // N_DEV, cc)


def _small_layout():
    sizes = dict(g_mix=1024, b_f=8, conv_w=CONV_K * CONV_CH, conv_b=512, ln_g=512, ln_b=512, g_x=1024,
                 g_mem=1024, g_ffn=1024, g_final=1024, loss=1)
    lay, r0 = {}, 0
    for n, sz in sizes.items():
        r = -(-sz // LANES)
        lay[n] = (r0, r, sz)
        r0 += r
    return lay, -(-r0 // 8) * 8


def kernel(x, mem, g_mix, w_in, b_f, conv_w, conv_b, ln_g, ln_b, w_out, g_x, g_mem, w_mq, w_mkv, w_mo, g_ffn, w_gu, w_down, g_final, loss_target, m_g_mix, m_w_in, m_b_f, m_conv_w, m_conv_b, m_ln_g, m_ln_b, m_w_out, m_g_x, m_g_mem, m_w_mq, m_w_mkv, m_w_mo, m_g_ffn, m_w_gu, m_w_down, m_g_final, v_g_mix, v_w_in, v_b_f, v_conv_w, v_conv_b, v_ln_g, v_ln_b, v_w_out, v_g_x, v_g_mem, v_w_mq, v_w_mkv, v_w_mo, v_g_ffn, v_w_gu, v_w_down, v_g_final):
    names = ["g_mix", "w_in", "b_f", "conv_w", "conv_b", "ln_g", "ln_b", "w_out", "g_x", "g_mem", "w_mq",
             "w_mkv", "w_mo", "g_ffn", "w_gu", "w_down", "g_final"]
    W = dict(zip(names, [g_mix, w_in, b_f, conv_w, conv_b, ln_g, ln_b, w_out, g_x, g_mem, w_mq, w_mkv, w_mo,
                         g_ffn, w_gu, w_down, g_final]))
    Mo = dict(zip(names, [m_g_mix, m_w_in, m_b_f, m_conv_w, m_conv_b, m_ln_g, m_ln_b, m_w_out, m_g_x, m_g_mem,
                          m_w_mq, m_w_mkv, m_w_mo, m_g_ffn, m_w_gu, m_w_down, m_g_final]))
    Vo = dict(zip(names, [v_g_mix, v_w_in, v_b_f, v_conv_w, v_conv_b, v_ln_g, v_ln_b, v_w_out, v_g_x, v_g_mem,
                          v_w_mq, v_w_mkv, v_w_mo, v_g_ffn, v_w_gu, v_w_down, v_g_final]))
    dev = 4 * lax.axis_index("x") + 2 * lax.axis_index("y") + lax.axis_index("c")

    two = lambda a: a.reshape(-1, a.shape[-1])
    cw_shard = jnp.pad(two(conv_w), ((0, HALO - CONV_K), (0, 0)))
    w_in8, cw8 = all_gather([two(w_in).astype(BF16), cw_shard], name="ag_first")
    cw_full = cw8.transpose(1, 0, 2).reshape(HALO, -1)[:CONV_K]

    sp = dict(g_mix=g_mix, b_f=b_f, conv_w=cw_full, conv_b=conv_b, ln_g=ln_g, ln_b=ln_b, g_x=g_x, g_mem=g_mem,
              g_ffn=g_ffn, g_final=g_final)
    loss_blk, grad_x, gs, reduced = local_step(x, mem, loss_target, sp, _full_from_gathered("w_in", w_in8),
                                               [two(W[n]).astype(BF16) for n in LATE])

    lay, rs = _small_layout()
    small = {**{n: gs[n] for n in SMALL}, "loss": loss_blk[:, :1]}
    parts = []
    for n, (r0, r, sz) in lay.items():
        flat = small[n].reshape(-1).astype(F32)
        parts.append(jnp.pad(flat, (0, r * LANES - sz)).reshape(r, LANES))
    spack = jnp.concatenate(parts, axis=0)
    spack = jnp.pad(spack, ((0, rs - spack.shape[0]), (0, 0)))
    ssum = rows_sum(all_gather([spack], name="ag_small")[0], name="small_sum")
    gsmall = {n: ssum[r0:r0 + r].reshape(-1)[:sz] for n, (r0, r, sz) in lay.items()}
    loss = gsmall["loss"].reshape(())

    grads, delta, new_m, new_v = {}, {}, {}, {}
    for n in BIG:
        p, o = reduced[n]
        shp = W[n].shape
        g, d, mn, vn = chip_sum_adamw(p, o, two(W[n]), two(Mo[n]), two(Vo[n]), name="adamw_" + n)
        grads[n], delta[n], new_m[n], new_v[n] = g.reshape(shp), d.reshape(shp), mn.reshape(shp), vn.reshape(shp)
    for n in SMALL:
        if n == "conv_w":
            full = gsmall[n].reshape(CONV_K, CONV_CH)
            ncol = conv_w.shape[-1]
            grads[n] = lax.dynamic_slice(full, (0, dev * ncol), (CONV_K, ncol)).reshape(conv_w.shape)
        else:
            grads[n] = gsmall[n].reshape(W[n].shape)
    upd = adamw_small([(two(W[n]), two(grads[n]), two(Mo[n]), two(Vo[n])) for n in SMALL], name="adamw_small")
    for n, (d, mn, vn) in zip(SMALL, upd):
        shp = W[n].shape
        delta[n], new_m[n], new_v[n] = d.reshape(shp), mn.reshape(shp), vn.reshape(shp)
    return (loss, grad_x, *[grads[n] for n in names], *[delta[n] for n in names],
            *[new_m[n] for n in names], *[new_v[n] for n in names])
```

```python
import functools
import math

import jax
import jax.numpy as jnp
from jax import lax
from jax.experimental import pallas as pl
from jax.experimental.pallas import tpu as pltpu

F32 = jnp.float32
BF16 = jnp.bfloat16
EPS = 1e-6
N_DEV = 8
CONV_CH = 512
CONV_K = 31
FOX_HEADS = 8
FOX_HEAD_DIM = 64
FOX_W = 512
MEM_HEADS = 4
MEM_HEAD_DIM = 256
HALO = 32
LANES = 128
ADAM_LR, ADAM_B1, ADAM_B2, ADAM_EPS, ADAM_WD, ADAM_STEP = 0.001, 0.9, 0.999, 1e-08, 0.01, 10
NEG = -1e30
VMEM_CAP = 60 * 1024 * 1024
MESH = pl.DeviceIdType.MESH


def _call(body, **kw):
    call = pl.pallas_call(body, **kw)
    return lambda *args: call(*[pltpu.with_memory_space_constraint(a, pltpu.HBM) for a in args])


def _params(sem=None, vmem=None):
    kw = {}
    if sem is not None:
        kw["dimension_semantics"] = sem
    if vmem is not None:
        kw["vmem_limit_bytes"] = int(min(VMEM_CAP, vmem))
    return pltpu.CompilerParams(**kw)


def _nbytes(shape, dtype):
    return math.prod(shape) * jnp.dtype(dtype).itemsize


def _pick(n, target):
    best = None
    for d in range(LANES, min(n, target) + 1, LANES):
        if n % d == 0:
            best = d
    return n if best is None else best


def matmul(a, b, *, tb=False, out_dtype, res=None, tm=512, tn=512, tk=None, name, rider=None):
    M, K = a.shape
    N = b.shape[0] if tb else b.shape[1]
    assert (b.shape[1] if tb else b.shape[0]) == K
    tm, tn = _pick(M, tm), _pick(N, tn)
    tk = K if tk is None else _pick(K, tk)
    assert M % tm == 0 and N % tn == 0 and K % tk == 0, (name, M, N, K, tm, tn, tk)
    nk = K // tk
    dn = (((1,), (1 if tb else 0,)), ((), ()))

    def body(*refs):
        if res is not None:
            a_ref, b_ref, r_ref, o_ref = refs[:4]
        else:
            a_ref, b_ref, o_ref = refs[:3]
        p = lax.dot_general(a_ref[...].astype(BF16), b_ref[...].astype(BF16), dn,
                            preferred_element_type=F32)

        def finish(acc):
            if res is not None:
                acc = acc + r_ref[...].astype(F32)
            o_ref[...] = acc.astype(out_dtype)

        if nk == 1:
            finish(p)
        else:
            acc_ref = refs[-1]
            k = pl.program_id(2)

            @pl.when(k == 0)
            def _():
                acc_ref[...] = p

            @pl.when(k > 0)
            def _():
                acc_ref[...] += p

            @pl.when(k == nk - 1)
            def _():
                finish(acc_ref[...])

    a_spec = pl.BlockSpec((tm, tk), lambda i, j, k: (i, k))
    b_spec = pl.BlockSpec((tn, tk), lambda i, j, k: (j, k)) if tb else pl.BlockSpec((tk, tn), lambda i, j, k: (k, j))
    o_spec = pl.BlockSpec((tm, tn), lambda i, j, k: (i, j))
    in_specs, args = [a_spec, b_spec], [a, b]
    est = 2 * (_nbytes((tm, tk), a.dtype) + _nbytes((tk, tn), b.dtype) + _nbytes((tm, tn), out_dtype))
    est += (a.dtype != BF16) * _nbytes((tm, tk), BF16) + (b.dtype != BF16) * _nbytes((tk, tn), BF16)
    est += 2 * _nbytes((tm, tn), F32)
    if res is not None:
        in_specs.append(o_spec)
        args.append(res)
        est += 2 * _nbytes((tm, tn), res.dtype)
    (out,), rode = hosted_call(
        body, rider, name=name, grid=(M // tm, N // tn, nk),
        in_specs=in_specs, out_specs=[o_spec],
        out_shape=[jax.ShapeDtypeStruct((M, N), out_dtype)],
        scratch_shapes=[] if nk == 1 else [pltpu.VMEM((tm, tn), F32)],
        args=args, vmem=est + (8 << 20),
    )
    return out if rider is None else (out, rode)


def _rms_scale(x):
    return lax.rsqrt(jnp.mean(x * x, axis=-1, keepdims=True) + EPS)


def rmsnorm_fwd(x, g, *, name, tm=512):
    T, D = x.shape
    tm = min(tm, T)

    def body(x_ref, g_ref, o_ref, ot_ref):
        xv = x_ref[...]
        h = xv * _rms_scale(xv) * g_ref[...]
        o_ref[...] = h.astype(BF16)
        ot_ref[...] = h.T.astype(BF16)

    return _call(
        body, name=name, grid=(T // tm,),
        in_specs=[pl.BlockSpec((tm, D), lambda i: (i, 0)), pl.BlockSpec((1, D), lambda i: (0, 0))],
        out_specs=[pl.BlockSpec((tm, D), lambda i: (i, 0)), pl.BlockSpec((D, tm), lambda i: (0, i))],
        out_shape=[jax.ShapeDtypeStruct((T, D), BF16), jax.ShapeDtypeStruct((D, T), BF16)],
        compiler_params=_params(("parallel",)),
    )(x, g)


def _rms_bwd_math(xv, gv, dh):
    r = _rms_scale(xv)
    xh = xv * r
    dg = jnp.sum(dh * xh, axis=0, keepdims=True)
    dxh = dh * gv
    dx = r * (dxh - xh * jnp.mean(dxh * xh, axis=-1, keepdims=True))
    return dx, dg


def rmsnorm_bwd(x, g, dh, dres, *, name, tm=256):
    T, D = x.shape
    tm = min(tm, T)

    def body(*refs):
        if dres is not None:
            x_ref, g_ref, dh_ref, dr_ref, dx_ref, dg_ref = refs
        else:
            x_ref, g_ref, dh_ref, dx_ref, dg_ref = refs
        dx, dg = _rms_bwd_math(x_ref[...], g_ref[...], dh_ref[...].astype(F32))
        if dres is not None:
            dx = dx + dr_ref[...]
        dx_ref[...] = dx

        @pl.when(pl.program_id(0) == 0)
        def _():
            dg_ref[...] = jnp.zeros_like(dg_ref)

        dg_ref[...] += dg

    row = pl.BlockSpec((tm, D), lambda i: (i, 0))
    vec = pl.BlockSpec((1, D), lambda i: (0, 0))
    ins, args = [row, vec, row], [x, g, dh]
    if dres is not None:
        ins.append(row)
        args.append(dres)
    return _call(
        body, name=name, grid=(T // tm,), in_specs=ins, out_specs=[row, vec],
        out_shape=[jax.ShapeDtypeStruct((T, D), F32), jax.ShapeDtypeStruct((1, D), F32)],
        compiler_params=_params(("arbitrary",)),
    )(*args)


def final_loss_bwd(x, g, target, *, name, tm=256):
    T, D = x.shape
    tm = min(tm, T)

    def body(x_ref, g_ref, t_ref, dx_ref, dg_ref, l_ref):
        xv, gv = x_ref[...], g_ref[...]
        e = xv * _rms_scale(xv) * gv - t_ref[...]
        part = 0.5 * jnp.sum(jnp.mean(e * e, axis=-1, keepdims=True), axis=0, keepdims=True)
        dx, dg = _rms_bwd_math(xv, gv, e * (1.0 / D))
        dx_ref[...] = dx

        @pl.when(pl.program_id(0) == 0)
        def _():
            dg_ref[...] = jnp.zeros_like(dg_ref)
            l_ref[...] = jnp.zeros_like(l_ref)

        dg_ref[...] += dg
        l_ref[...] += jnp.broadcast_to(part, l_ref.shape)

    row = pl.BlockSpec((tm, D), lambda i: (i, 0))
    vec = pl.BlockSpec((1, D), lambda i: (0, 0))
    return _call(
        body, name=name, grid=(T // tm,), in_specs=[row, vec, row],
        out_specs=[row, vec, pl.BlockSpec((1, LANES), lambda i: (0, 0))],
        out_shape=[jax.ShapeDtypeStruct((T, D), F32), jax.ShapeDtypeStruct((1, D), F32),
                   jax.ShapeDtypeStruct((1, LANES), F32)],
        compiler_params=_params(("arbitrary",)),
    )(x, g, target)


def _sigmoid(v):
    return 1.0 / (1.0 + jnp.exp(-v))


def _glu(blk):
    u = blk[:, :CONV_CH].astype(F32)
    gt = blk[:, CONV_CH:].astype(F32)
    return u * _sigmoid(gt)


def _fill_causal_ext(ext, cur_ref, halo_ref, s, ts):
    ext[pl.ds(HALO, ts), :] = _glu(cur_ref[0])
    hal = _glu(halo_ref[0])
    ext[pl.ds(0, HALO), :] = jnp.where(s > 0, hal, 0.0)


SUBLANES = 8


def _make_shifted(ext, sh):
    n = ext.shape[0]
    full = ext[...]
    for r in range(1, SUBLANES):
        sh[r - 1] = pltpu.roll(full, n - r, 0)


def _tap(ext, sh, off, ts):
    r = off % SUBLANES
    return ext[pl.ds(off, ts), :] if r == 0 else sh[r - 1, pl.ds(off - r, ts), :]


def _causal_conv(ext, sh, w_ref, ts):
    acc = jnp.zeros((ts, CONV_CH), F32)
    for j in range(CONV_K):
        acc = acc + _tap(ext, sh, HALO - (CONV_K - 1) + j, ts) * w_ref[pl.ds(j, 1), :]
    return acc


def _ln_stats(y):
    mu = jnp.mean(y, axis=-1, keepdims=True)
    yc = y - mu
    rstd = lax.rsqrt(jnp.mean(yc * yc, axis=-1, keepdims=True) + EPS)
    return yc * rstd, rstd


def _conv_specs(ts, S):
    nh = ts // HALO
    cur = pl.BlockSpec((1, ts, 2 * CONV_CH), lambda b, s: (b, s, 0))
    halo = pl.BlockSpec((1, HALO, 2 * CONV_CH), lambda b, s: (b, jnp.maximum(s * nh - 1, 0), 0))
    w = pl.BlockSpec((HALO, CONV_CH), lambda b, s: (0, 0))
    vec = pl.BlockSpec((1, CONV_CH), lambda b, s: (0, 0))
    return cur, halo, w, vec


def conv_branch_fwd(ug, conv_w, conv_b, ln_g, ln_b, *, name, ts=256, rider=None):
    B, S, _ = ug.shape
    ts = min(ts, S)
    ns = S // ts
    cur, halo, w, vec = _conv_specs(ts, S)

    def body(cur_ref, halo_ref, w_ref, cb_ref, lg_ref, lb_ref, o_ref, ot_ref, ext, sh):
        _fill_causal_ext(ext, cur_ref, halo_ref, pl.program_id(1), ts)
        _make_shifted(ext, sh)
        y = _causal_conv(ext, sh, w_ref, ts) + cb_ref[...]
        yh, _ = _ln_stats(y)
        ln = yh * lg_ref[...] + lb_ref[...]
        out = ln * _sigmoid(ln)
        o_ref[0] = out.astype(BF16)
        ot_ref[...] = out.T.astype(BF16)

    return hosted_call(
        body, rider, name=name, grid=(B, ns), in_specs=[cur, halo, w, vec, vec, vec],
        out_specs=[pl.BlockSpec((1, ts, CONV_CH), lambda b, s: (b, s, 0)),
                   pl.BlockSpec((CONV_CH, ts), lambda b, s: (0, b * ns + s))],
        out_shape=[jax.ShapeDtypeStruct((B, S, CONV_CH), BF16), jax.ShapeDtypeStruct((CONV_CH, B * S), BF16)],
        scratch_shapes=[pltpu.VMEM((ts + HALO, CONV_CH), F32),
                        pltpu.VMEM((SUBLANES - 1, ts + HALO, CONV_CH), F32)],
        args=(ug, ug, conv_w, conv_b, ln_g, ln_b),
    )


def conv_branch_bwd_a(ug, dcat, conv_w, conv_b, ln_g, ln_b, *, name, ts=256):
    B, S, _ = ug.shape
    ts = min(ts, S)
    cur, halo, w, vec = _conv_specs(ts, S)

    def body(cur_ref, halo_ref, d_ref, w_ref, cb_ref, lg_ref, lb_ref, dy_ref, dw_ref, dv_ref, ext, sh):
        _fill_causal_ext(ext, cur_ref, halo_ref, pl.program_id(1), ts)
        _make_shifted(ext, sh)
        y = _causal_conv(ext, sh, w_ref, ts) + cb_ref[...]
        yh, rstd = _ln_stats(y)
        lg = lg_ref[...]
        ln = yh * lg + lb_ref[...]
        sg = _sigmoid(ln)
        dln = d_ref[0].astype(F32) * (sg * (1.0 + ln * (1.0 - sg)))
        dyh = dln * lg
        dy = rstd * (dyh - jnp.mean(dyh, axis=-1, keepdims=True)
                     - yh * jnp.mean(dyh * yh, axis=-1, keepdims=True))
        dy_ref[0] = dy

        @pl.when((pl.program_id(0) == 0) & (pl.program_id(1) == 0))
        def _():
            dw_ref[...] = jnp.zeros_like(dw_ref)
            dv_ref[...] = jnp.zeros_like(dv_ref)

        dv_ref[pl.ds(0, 1), :] += jnp.sum(dy, axis=0, keepdims=True)
        dv_ref[pl.ds(1, 1), :] += jnp.sum(dln * yh, axis=0, keepdims=True)
        dv_ref[pl.ds(2, 1), :] += jnp.sum(dln, axis=0, keepdims=True)
        for j in range(CONV_K):
            tap = _tap(ext, sh, HALO - (CONV_K - 1) + j, ts)
            dw_ref[pl.ds(j, 1), :] += jnp.sum(dy * tap, axis=0, keepdims=True)

    return _call(
        body, name=name, grid=(B, S // ts),
        in_specs=[cur, halo, pl.BlockSpec((1, ts, CONV_CH), lambda b, s: (b, s, 0)), w, vec, vec, vec],
        out_specs=[pl.BlockSpec((1, ts, CONV_CH), lambda b, s: (b, s, 0)),
                   pl.BlockSpec((HALO, CONV_CH), lambda b, s: (0, 0)),
                   pl.BlockSpec((8, CONV_CH), lambda b, s: (0, 0))],
        out_shape=[jax.ShapeDtypeStruct((B, S, CONV_CH), F32),
                   jax.ShapeDtypeStruct((HALO, CONV_CH), F32),
                   jax.ShapeDtypeStruct((8, CONV_CH), F32)],
        scratch_shapes=[pltpu.VMEM((ts + HALO, CONV_CH), F32),
                        pltpu.VMEM((SUBLANES - 1, ts + HALO, CONV_CH), F32)],
        compiler_params=_params(("arbitrary", "arbitrary")),
    )(ug, ug, dcat, conv_w, conv_b, ln_g, ln_b)


def conv_branch_bwd_b(ug, dy, conv_w, *, name, ts=256):
    B, S, _ = ug.shape
    ts = min(ts, S)
    nh, n_halo = ts // HALO, S // HALO

    def body(cur_ref, dy_ref, nxt_ref, w_ref, o_ref, ext, sh):
        last = pl.program_id(1) == pl.num_programs(1) - 1
        ext[pl.ds(0, ts), :] = dy_ref[0]
        ext[pl.ds(ts, HALO), :] = jnp.where(last, 0.0, nxt_ref[0])
        _make_shifted(ext, sh)
        da = jnp.zeros((ts, CONV_CH), F32)
        for j in range(CONV_K):
            da = da + _tap(ext, sh, CONV_K - 1 - j, ts) * w_ref[pl.ds(j, 1), :]
        blk = cur_ref[0]
        u = blk[:, :CONV_CH].astype(F32)
        sg = _sigmoid(blk[:, CONV_CH:].astype(F32))
        o_ref[0, :, :CONV_CH] = (da * sg).astype(BF16)
        o_ref[0, :, CONV_CH:] = (da * u * sg * (1.0 - sg)).astype(BF16)

    return _call(
        body, name=name, grid=(B, S // ts),
        in_specs=[pl.BlockSpec((1, ts, 2 * CONV_CH), lambda b, s: (b, s, 0)),
                  pl.BlockSpec((1, ts, CONV_CH), lambda b, s: (b, s, 0)),
                  pl.BlockSpec((1, HALO, CONV_CH), lambda b, s: (b, jnp.minimum((s + 1) * nh, n_halo - 1), 0)),
                  pl.BlockSpec((HALO, CONV_CH), lambda b, s: (0, 0))],
        out_specs=pl.BlockSpec((1, ts, 2 * CONV_CH), lambda b, s: (b, s, 0)),
        out_shape=jax.ShapeDtypeStruct((B, S, 2 * CONV_CH), BF16),
        scratch_shapes=[pltpu.VMEM((ts + HALO, CONV_CH), F32),
                        pltpu.VMEM((SUBLANES - 1, ts + HALO, CONV_CH), F32)],
        compiler_params=_params(("parallel", "parallel")),
    )(ug, dy, dy, conv_w)


def _tri(n, lower):
    r = lax.broadcasted_iota(jnp.int32, (n, n), 0)
    c = lax.broadcasted_iota(jnp.int32, (n, n), 1)
    return ((r >= c) if lower else (r <= c)).astype(F32)


def _eye(n):
    r = lax.broadcasted_iota(jnp.int32, (n, n), 0)
    c = lax.broadcasted_iota(jnp.int32, (n, n), 1)
    return (r == c).astype(F32)


def _dot_hi(a, b, dn):
    return lax.dot_general(a, b, dn, precision=lax.Precision.HIGHEST, preferred_element_type=F32)


NN = (((1,), (0,)), ((), ()))
NT = (((1,), (1,)), ((), ()))
TN = (((0,), (0,)), ((), ()))


def _log_sigmoid(v):
    e = jnp.exp(-jnp.abs(v))
    log1p_e = jnp.where(e < 1e-3, e * (1.0 - 0.5 * e), jnp.log(1.0 + e))
    return jnp.minimum(v, 0.0) - log1p_e


def fgate_fwd(h, w_f, b_f, *, name, ts=256):
    B, S, D = h.shape
    ts = min(ts, S)

    def body(h_ref, w_ref, b_ref, f_ref, cc_ref, cr_ref, carry):
        @pl.when(pl.program_id(1) == 0)
        def _():
            carry[...] = jnp.zeros_like(carry)

        f = jnp.dot(h_ref[0], w_ref[...], preferred_element_type=F32)
        f_ref[0] = f
        logf = _log_sigmoid(f + b_ref[...])
        c = _dot_hi(_tri(ts, True), logf, NN) + carry[pl.ds(0, 1), :]
        cc_ref[0] = c
        carry[pl.ds(0, 1), :] = c[ts - 1:ts, :]
        cr_ref[0] = _dot_hi(_eye(LANES), c, NT)

    return _call(
        body, name=name, grid=(B, S // ts),
        in_specs=[pl.BlockSpec((1, ts, D), lambda b, s: (b, s, 0)),
                  pl.BlockSpec((D, LANES), lambda b, s: (0, 0)),
                  pl.BlockSpec((1, LANES), lambda b, s: (0, 0))],
        out_specs=[pl.BlockSpec((1, ts, LANES), lambda b, s: (b, s, 0)),
                   pl.BlockSpec((1, ts, LANES), lambda b, s: (b, s, 0)),
                   pl.BlockSpec((1, LANES, ts), lambda b, s: (b, 0, s))],
        out_shape=[jax.ShapeDtypeStruct((B, S, LANES), F32), jax.ShapeDtypeStruct((B, S, LANES), F32),
                   jax.ShapeDtypeStruct((B, LANES, S), F32)],
        scratch_shapes=[pltpu.VMEM((8, LANES), F32)],
        compiler_params=_params(("parallel", "arbitrary")),
    )(h, w_f, b_f)


def fgate_bwd(dc, f, b_f, *, name, ts=256):
    B, S, _ = f.shape
    P = dc.shape[1]
    ts = min(ts, S)
    ns = S // ts

    def body(dc_ref, f_ref, b_ref, df_ref, db_ref, carry):
        @pl.when(pl.program_id(1) == 0)
        def _():
            carry[...] = jnp.zeros_like(carry)

        @pl.when((pl.program_id(0) == 0) & (pl.program_id(1) == 0))
        def _():
            db_ref[...] = jnp.zeros_like(db_ref)

        dc_t = dc_ref[0, 0]
        for j in range(1, P):
            dc_t = dc_t + dc_ref[0, j]
        dlogf = _dot_hi(_tri(ts, False), dc_t, NN) + carry[pl.ds(0, 1), :]
        carry[pl.ds(0, 1), :] = dlogf[0:1, :]
        df = dlogf * _sigmoid(-(f_ref[0] + b_ref[...]))
        df_ref[0] = df.astype(BF16)
        db_ref[...] += jnp.sum(df, axis=0, keepdims=True)

    return _call(
        body, name=name, grid=(B, ns),
        in_specs=[pl.BlockSpec((1, P, ts, LANES), lambda b, s: (b, 0, ns - 1 - s, 0)),
                  pl.BlockSpec((1, ts, LANES), lambda b, s: (b, ns - 1 - s, 0)),
                  pl.BlockSpec((1, LANES), lambda b, s: (0, 0))],
        out_specs=[pl.BlockSpec((1, ts, LANES), lambda b, s: (b, ns - 1 - s, 0)),
                   pl.BlockSpec((1, LANES), lambda b, s: (0, 0))],
        out_shape=[jax.ShapeDtypeStruct((B, S, LANES), BF16), jax.ShapeDtypeStruct((1, LANES), F32)],
        scratch_shapes=[pltpu.VMEM((8, LANES), F32)],
        compiler_params=_params(("arbitrary", "arbitrary")),
    )(dc, f, b_f)


def _lane_pick(tile, idx):
    lane = lax.broadcasted_iota(jnp.int32, tile.shape, 1)
    return jnp.sum(jnp.where(lane == idx, tile, 0.0), axis=-1, keepdims=True)


FOX_T = 512


def _fox_heads(q, cc_ref, p):
    lane = lax.broadcasted_iota(jnp.int32, q.shape, 1)
    qs = q * (1.0 / math.sqrt(FOX_HEAD_DIM))
    qhs = [jnp.where((lane < FOX_HEAD_DIM) == (hh == 0), qs, jnp.zeros_like(qs)) for hh in range(2)]
    crefs = [_lane_pick(cc_ref[0, pl.ds(0, 1), :], 2 * p + hh) for hh in range(2)]
    return qhs, crefs


def _fold_lanes(x, op):
    out = x[:, :LANES]
    for j in range(1, x.shape[1] // LANES):
        out = op(out, x[:, j * LANES:(j + 1) * LANES])
    return out


def _causal(t, transposed):
    r = lax.broadcasted_iota(jnp.int32, (t, t), 0)
    c = lax.broadcasted_iota(jnp.int32, (t, t), 1)
    return (r <= c) if transposed else (c <= r)


QKV0 = 8


def fox_fwd(z, c_col, c_row, *, name, rider=None):
    B, S, _ = z.shape
    assert S % FOX_T == 0
    tq, nq = FOX_T, S // FOX_T
    npair = FOX_HEADS // 2

    def body(q_ref, k_ref, v_ref, cc_ref, cr_ref, o_ref, l_ref, ot_ref, s_scr, m_scr, acc_scr):
        p, qi = pl.program_id(1), pl.program_id(2)
        qhs, crefs = _fox_heads(q_ref[0], cc_ref, p)
        lane = lax.broadcasted_iota(jnp.int32, (tq, LANES), 1)
        first = lane < FOX_HEAD_DIM
        for hh in range(2):
            m_scr[hh] = jnp.full((tq, LANES), NEG, F32)
            acc_scr[hh] = jnp.zeros((tq, LANES), F32)

        def logits(kb, diagonal):
            k0 = pl.multiple_of(kb * tq, tq)
            k = k_ref[0, pl.ds(k0, tq), :]
            for hh in range(2):
                s = lax.dot_general(qhs[hh], k, NT, preferred_element_type=F32)
                s = s + (crefs[hh] - cr_ref[0, pl.ds(2 * p + hh, 1), pl.ds(k0, tq)])
                if diagonal:
                    s = jnp.where(_causal(tq, False), s, NEG)
                s_scr[hh, kb] = s
                m_scr[hh] = jnp.maximum(m_scr[hh], _fold_lanes(s, jnp.maximum))

        def sweep1(kb, carry):
            logits(kb, False)
            return carry

        lax.fori_loop(0, qi, sweep1, 0)
        logits(qi, True)
        ms = [jnp.max(m_scr[hh], axis=-1, keepdims=True) for hh in range(2)]
        mbs = [jnp.broadcast_to(ms[hh], (tq, tq)) for hh in range(2)]

        for hh in range(2):
            m_scr[hh] = jnp.zeros((tq, LANES), F32)

        def weigh(kb, carry):
            k0 = pl.multiple_of(kb * tq, tq)
            v = v_ref[0, pl.ds(k0, tq), :]
            for hh in range(2):
                pr = jnp.exp(s_scr[hh, kb] - mbs[hh])
                m_scr[hh] += _fold_lanes(pr, jnp.add)
                acc_scr[hh] += jnp.dot(pr.astype(BF16), v, preferred_element_type=F32)
            return carry

        lax.fori_loop(0, qi + 1, weigh, 0)
        accs = [acc_scr[hh] for hh in range(2)]
        ls = [jnp.sum(m_scr[hh], axis=-1, keepdims=True) for hh in range(2)]
        out = jnp.where(first, accs[0] / ls[0], accs[1] / ls[1])
        o_ref[0] = out.astype(BF16)
        ot_ref[...] = out.T.astype(BF16)
        l_ref[0, 0] = jnp.where(first, ms[0] + jnp.log(ls[0]), ms[1] + jnp.log(ls[1]))

    return hosted_call(
        body, rider, name=name, grid=(B, npair, nq),
        in_specs=[pl.BlockSpec((1, tq, LANES), lambda b, p, i: (b, i, QKV0 + p)),
                  pl.BlockSpec((1, S, LANES), lambda b, p, i: (b, 0, QKV0 + npair + p)),
                  pl.BlockSpec((1, S, LANES), lambda b, p, i: (b, 0, QKV0 + 2 * npair + p)),
                  pl.BlockSpec((1, tq, LANES), lambda b, p, i: (b, i, 0)),
                  pl.BlockSpec((1, 8, S), lambda b, p, i: (b, 0, 0))],
        out_specs=[pl.BlockSpec((1, tq, LANES), lambda b, p, i: (b, i, p)),
                   pl.BlockSpec((1, 1, tq, LANES), lambda b, p, i: (b, p, i, 0)),
                   pl.BlockSpec((LANES, tq), lambda b, p, i: (p, b * nq + i))],
        out_shape=[jax.ShapeDtypeStruct((B, S, FOX_W), BF16),
                   jax.ShapeDtypeStruct((B, npair, S, LANES), F32),
                   jax.ShapeDtypeStruct((FOX_W, B * S), BF16)],
        scratch_shapes=[pltpu.VMEM((2, nq, tq, tq), F32), pltpu.VMEM((2, tq, LANES), F32),
                        pltpu.VMEM((2, tq, LANES), F32)],
        args=(z, z, z, c_col, c_row),
    )


def fox_bwd_dq(z, dcat, lse, c_col, c_row, *, name, rider=None):
    B, S, _ = z.shape
    tq, nq = FOX_T, S // FOX_T
    npair = FOX_HEADS // 2

    def body(q_ref, k_ref, v_ref, do_ref, l_ref, cc_ref, cr_ref, dq_ref, st_ref, p_scr, dp_scr, dl_scr):
        p, qi = pl.program_id(1), pl.program_id(2)
        qhs, crefs = _fox_heads(q_ref[0], cc_ref, p)
        lane = lax.broadcasted_iota(jnp.int32, (tq, LANES), 1)
        do_b = do_ref[0].astype(BF16)
        dohs = [jnp.where((lane < FOX_HEAD_DIM) == (hh == 0), do_b, jnp.zeros_like(do_b)) for hh in range(2)]
        lses = [_lane_pick(l_ref[0, 0], hh * FOX_HEAD_DIM) for hh in range(2)]
        lbs = [jnp.broadcast_to(lses[hh], (tq, tq)) for hh in range(2)]
        for hh in range(2):
            dl_scr[hh] = jnp.zeros((tq, LANES), F32)

        def probs(kb, diagonal):
            k0 = pl.multiple_of(kb * tq, tq)
            k = k_ref[0, pl.ds(k0, tq), :]
            v = v_ref[0, pl.ds(k0, tq), :]
            for hh in range(2):
                s = lax.dot_general(qhs[hh], k, NT, preferred_element_type=F32)
                s = s + (crefs[hh] - cr_ref[0, pl.ds(2 * p + hh, 1), pl.ds(k0, tq)])
                pr = jnp.exp(s - lbs[hh])
                if diagonal:
                    pr = jnp.where(_causal(tq, False), pr, 0.0)
                dp = lax.dot_general(dohs[hh], v, NT, preferred_element_type=F32)
                pdp = pr * dp
                dl_scr[hh] += _fold_lanes(pdp, jnp.add)
                p_scr[hh, kb] = pr
                dp_scr[hh, kb] = dp

        def first_pass(kb, carry):
            probs(kb, False)
            return carry

        lax.fori_loop(0, qi, first_pass, 0)
        probs(qi, True)

        dls = [jnp.sum(dl_scr[hh], axis=-1, keepdims=True) for hh in range(2)]
        dlbs = [jnp.broadcast_to(dls[hh], (tq, tq)) for hh in range(2)]

        def second_pass(kb, dq):
            k0 = pl.multiple_of(kb * tq, tq)
            k = k_ref[0, pl.ds(k0, tq), :]
            for hh in range(2):
                ds = p_scr[hh, kb] * (dp_scr[hh, kb] - dlbs[hh])
                kh = jnp.where((lane < FOX_HEAD_DIM) == (hh == 0), k, jnp.zeros_like(k))
                dq = dq + jnp.dot(ds.astype(BF16), kh, preferred_element_type=F32)
            return dq

        dq = lax.fori_loop(0, qi + 1, second_pass, jnp.zeros((tq, LANES), F32))
        dq_ref[0] = (dq * (1.0 / math.sqrt(FOX_HEAD_DIM))).astype(BF16)
        cols = jnp.zeros((tq, LANES), F32)
        for j, col in enumerate([crefs[0] - lses[0], crefs[1] - lses[1], dls[0], dls[1]]):
            cols = jnp.where(lane == j, col, cols)
        st_ref[0, 0] = _dot_hi(_eye(LANES), cols, NT)[:8]

    return hosted_call(
        body, rider, name=name, grid=(B, npair, nq),
        in_specs=[pl.BlockSpec((1, tq, LANES), lambda b, p, i: (b, i, QKV0 + p)),
                  pl.BlockSpec((1, S, LANES), lambda b, p, i: (b, 0, QKV0 + npair + p)),
                  pl.BlockSpec((1, S, LANES), lambda b, p, i: (b, 0, QKV0 + 2 * npair + p)),
                  pl.BlockSpec((1, tq, LANES), lambda b, p, i: (b, i, npair + p)),
                  pl.BlockSpec((1, 1, tq, LANES), lambda b, p, i: (b, p, i, 0)),
                  pl.BlockSpec((1, tq, LANES), lambda b, p, i: (b, i, 0)),
                  pl.BlockSpec((1, 8, S), lambda b, p, i: (b, 0, 0))],
        out_specs=[pl.BlockSpec((1, tq, LANES), lambda b, p, i: (b, i, p)),
                   pl.BlockSpec((1, 1, 8, tq), lambda b, p, i: (b, p, 0, i))],
        out_shape=[jax.ShapeDtypeStruct((B, S, FOX_W), BF16), jax.ShapeDtypeStruct((B, npair, 8, S), F32)],
        scratch_shapes=[pltpu.VMEM((2, nq, tq, tq), F32), pltpu.VMEM((2, nq, tq, tq), F32),
                        pltpu.VMEM((2, tq, LANES), F32)],
        args=(z, z, z, dcat, lse, c_col, c_row), vmem=56 << 20,
    )


def fox_bwd_dkdv(z, dcat, stats, c_col, *, name, rider=None):
    B, S, _ = z.shape
    tk, nq = FOX_T, S // FOX_T
    npair = FOX_HEADS // 2
    inv = 1.0 / math.sqrt(FOX_HEAD_DIM)

    def body(q_ref, k_ref, v_ref, do_ref, st_ref, cc_ref, dk_ref, dv_ref, dc_ref, dk_scr, dv_scr, dc_scr):
        p, kt = pl.program_id(1), pl.program_id(2)
        lane = lax.broadcasted_iota(jnp.int32, (tk, LANES), 1)
        masks = [(lane < FOX_HEAD_DIM) == (hh == 0) for hh in range(2)]
        k = k_ref[0]
        v = v_ref[0]
        khs = [jnp.where(masks[hh], k, jnp.zeros_like(k)) for hh in range(2)]
        vhs = [jnp.where(masks[hh], v, jnp.zeros_like(v)) for hh in range(2)]
        ccbs = [jnp.broadcast_to(_lane_pick(cc_ref[0], 2 * p + hh), (tk, tk)) for hh in range(2)]
        dk_scr[...] = jnp.zeros_like(dk_scr)
        dv_scr[...] = jnp.zeros_like(dv_scr)
        dc_scr[...] = jnp.zeros_like(dc_scr)

        def tile(qb, diagonal):
            q0 = pl.multiple_of(qb * tk, tk)
            qs = q_ref[0, pl.ds(q0, tk), :] * inv
            do_b = do_ref[0, pl.ds(q0, tk), :].astype(BF16)
            for hh in range(2):
                st = lax.dot_general(khs[hh], qs, NT, preferred_element_type=F32)
                pr = jnp.exp(st - ccbs[hh] + st_ref[0, 0, pl.ds(hh, 1), pl.ds(q0, tk)])
                if diagonal:
                    pr = jnp.where(_causal(tk, True), pr, 0.0)
                dp = lax.dot_general(vhs[hh], do_b, NT, preferred_element_type=F32)
                ds = pr * (dp - st_ref[0, 0, pl.ds(2 + hh, 1), pl.ds(q0, tk)])
                dv_scr[...] += jnp.dot(pr.astype(BF16), jnp.where(masks[hh], do_b, jnp.zeros_like(do_b)),
                                       preferred_element_type=F32)
                dk_scr[...] += jnp.dot(ds.astype(BF16), jnp.where(masks[hh], qs, jnp.zeros_like(qs)),
                                       preferred_element_type=F32)
                dc_scr[hh] -= _fold_lanes(ds, jnp.add)

        def later(qb, carry):
            tile(qb, False)
            return carry

        tile(kt, True)
        lax.fori_loop(kt + 1, nq, later, 0)
        dk_ref[0] = dk_scr[...].astype(BF16)
        dv_ref[0] = dv_scr[...].astype(BF16)
        dcs = [jnp.sum(dc_scr[hh], axis=-1, keepdims=True) for hh in range(2)]
        dc_ref[0, 0] = jnp.where(lane == 2 * p, dcs[0], jnp.where(lane == 2 * p + 1, dcs[1], 0.0))

    full = lambda col: pl.BlockSpec((1, S, LANES), col)
    tile_spec = lambda col: pl.BlockSpec((1, tk, LANES), col)
    return hosted_call(
        body, rider, name=name, grid=(B, npair, nq),
        in_specs=[full(lambda b, p, t: (b, 0, QKV0 + p)),
                  tile_spec(lambda b, p, t: (b, t, QKV0 + npair + p)),
                  tile_spec(lambda b, p, t: (b, t, QKV0 + 2 * npair + p)),
                  full(lambda b, p, t: (b, 0, npair + p)),
                  pl.BlockSpec((1, 1, 8, S), lambda b, p, t: (b, p, 0, 0)),
                  tile_spec(lambda b, p, t: (b, t, 0))],
        out_specs=[tile_spec(lambda b, p, t: (b, t, p)), tile_spec(lambda b, p, t: (b, t, p)),
                   pl.BlockSpec((1, 1, tk, LANES), lambda b, p, t: (b, p, t, 0))],
        out_shape=[jax.ShapeDtypeStruct((B, S, FOX_W), BF16)] * 2
        + [jax.ShapeDtypeStruct((B, npair, S, LANES), F32)],
        scratch_shapes=[pltpu.VMEM((tk, LANES), F32), pltpu.VMEM((tk, LANES), F32),
                        pltpu.VMEM((2, tk, LANES), F32)],
        args=(z, z, z, dcat, stats, c_col),
    )


def xattn_fwd(qm, kv, *, name, tq=256):
    B, S, D = qm.shape
    M = kv.shape[1]
    tq = min(tq, S)
    inv = 1.0 / math.sqrt(MEM_HEAD_DIM)

    nq = S // tq

    def body(q_ref, kv_ref, o_ref, ot_ref):
        for h in range(MEM_HEADS):
            c0 = h * MEM_HEAD_DIM
            qh = q_ref[0, :, c0:c0 + MEM_HEAD_DIM]
            kh = kv_ref[0, :, c0:c0 + MEM_HEAD_DIM]
            vh = kv_ref[0, :, D + c0:D + c0 + MEM_HEAD_DIM]
            s = lax.dot_general(qh, kh, NT, preferred_element_type=F32) * inv
            e = jnp.exp(s - jnp.max(s, axis=-1, keepdims=True))
            o = jnp.dot(e.astype(BF16), vh, preferred_element_type=F32) / jnp.sum(e, axis=-1, keepdims=True)
            o_ref[0, :, c0:c0 + MEM_HEAD_DIM] = o.astype(BF16)
            ot_ref[c0:c0 + MEM_HEAD_DIM, :] = o.T.astype(BF16)

    return _call(
        body, name=name, grid=(B, nq),
        in_specs=[pl.BlockSpec((1, tq, D), lambda b, i: (b, i, 0)),
                  pl.BlockSpec((1, M, 2 * D), lambda b, i: (b, 0, 0))],
        out_specs=[pl.BlockSpec((1, tq, D), lambda b, i: (b, i, 0)),
                   pl.BlockSpec((D, tq), lambda b, i: (0, b * nq + i))],
        out_shape=[jax.ShapeDtypeStruct((B, S, D), BF16), jax.ShapeDtypeStruct((D, B * S), BF16)],
        compiler_params=_params(("parallel", "parallel")),
    )(qm, kv)


def xattn_bwd(qm, kv, do, *, name, tq=256):
    B, S, D = qm.shape
    M = kv.shape[1]
    tq = min(tq, S)
    inv = 1.0 / math.sqrt(MEM_HEAD_DIM)

    def body(q_ref, kv_ref, do_ref, dq_ref, dkv_ref):
        @pl.when(pl.program_id(1) == 0)
        def _():
            dkv_ref[...] = jnp.zeros_like(dkv_ref)

        for h in range(MEM_HEADS):
            c0 = h * MEM_HEAD_DIM
            qh = q_ref[0, :, c0:c0 + MEM_HEAD_DIM]
            kh = kv_ref[0, :, c0:c0 + MEM_HEAD_DIM]
            vh = kv_ref[0, :, D + c0:D + c0 + MEM_HEAD_DIM]
            doh = do_ref[0, :, c0:c0 + MEM_HEAD_DIM]
            s = lax.dot_general(qh, kh, NT, preferred_element_type=F32) * inv
            e = jnp.exp(s - jnp.max(s, axis=-1, keepdims=True))
            pr = e / jnp.sum(e, axis=-1, keepdims=True)
            dp = lax.dot_general(doh, vh, NT, preferred_element_type=F32)
            ds = pr * (dp - jnp.sum(pr * dp, axis=-1, keepdims=True))
            ds_b = ds.astype(BF16)
            dq_ref[0, :, c0:c0 + MEM_HEAD_DIM] = (jnp.dot(ds_b, kh, preferred_element_type=F32) * inv).astype(BF16)
            dkv_ref[0, :, c0:c0 + MEM_HEAD_DIM] += lax.dot_general(ds_b, qh, TN, preferred_element_type=F32) * inv
            dkv_ref[0, :, D + c0:D + c0 + MEM_HEAD_DIM] += lax.dot_general(
                pr.astype(BF16), doh, TN, preferred_element_type=F32)

    row = pl.BlockSpec((1, tq, D), lambda b, i: (b, i, 0))
    kvs = pl.BlockSpec((1, M, 2 * D), lambda b, i: (b, 0, 0))
    return _call(
        body, name=name, grid=(B, S // tq), in_specs=[row, kvs, row], out_specs=[row, kvs],
        out_shape=[jax.ShapeDtypeStruct((B, S, D), BF16), jax.ShapeDtypeStruct((B, M, 2 * D), F32)],
        compiler_params=_params(("parallel", "arbitrary")),
    )(qm, kv, do)


def swiglu_fwd(gu, *, name, tm=256):
    T, F2 = gu.shape
    Fh = F2 // 2
    tm = min(tm, T)

    def body(gu_ref, o_ref, ot_ref):
        g = gu_ref[:, :Fh].astype(F32)
        u = gu_ref[:, Fh:].astype(F32)
        act = g * _sigmoid(g) * u
        o_ref[...] = act.astype(BF16)
        ot_ref[...] = act.T.astype(BF16)

    return _call(
        body, name=name, grid=(T // tm,),
        in_specs=[pl.BlockSpec((tm, F2), lambda i: (i, 0))],
        out_specs=[pl.BlockSpec((tm, Fh), lambda i: (i, 0)), pl.BlockSpec((Fh, tm), lambda i: (0, i))],
        out_shape=[jax.ShapeDtypeStruct((T, Fh), BF16), jax.ShapeDtypeStruct((Fh, T), BF16)],
        compiler_params=_params(("parallel",)),
    )(gu)


def swiglu_bwd(gu, dact, *, name, tm=256):
    T, F2 = gu.shape
    Fh = F2 // 2
    tm = min(tm, T)

    def body(gu_ref, d_ref, o_ref):
        g = gu_ref[:, :Fh].astype(F32)
        u = gu_ref[:, Fh:].astype(F32)
        d = d_ref[...].astype(F32)
        sg = _sigmoid(g)
        o_ref[:, :Fh] = (d * u * (sg * (1.0 + g * (1.0 - sg)))).astype(BF16)
        o_ref[:, Fh:] = (d * g * sg).astype(BF16)

    return _call(
        body, name=name, grid=(T // tm,),
        in_specs=[pl.BlockSpec((tm, F2), lambda i: (i, 0)), pl.BlockSpec((tm, Fh), lambda i: (i, 0))],
        out_specs=pl.BlockSpec((tm, F2), lambda i: (i, 0)),
        out_shape=jax.ShapeDtypeStruct((T, F2), BF16),
        compiler_params=_params(("parallel",)),
    )(gu, dact)


LATE_MID = ("w_out", "w_mq", "w_mkv", "w_mo")
LATE_FFN = ("w_gu", "w_down")
LATE = LATE_MID + LATE_FFN
RS_GROUPS = (("w_gu", "w_down"), ("w_out", "w_mq", "w_mkv", "w_mo"), ("w_in",))


def reduce_to_chips(names, gw, *, tag):
    g42 = [_shards_from_full(n, gw[n]) for n in names]
    got = sibling_exchange(g42, name="rs_sibling_" + tag)
    return [pair_sum(g, o, name="rs_pair_sum_" + n) for n, g, o in zip(names, g42, got)]


def local_step(x, mem, target, sp, w_in_full, late_shards):
    B, S, D = x.shape
    T = B * S
    M = mem.shape[1]
    row = lambda v: v.reshape(1, -1).astype(F32)
    g_mix, g_x, g_mem, g_ffn, g_final = (row(sp[k]) for k in ("g_mix", "g_x", "g_mem", "g_ffn", "g_final"))
    conv_b, ln_g, ln_b = row(sp["conv_b"]), row(sp["ln_g"]), row(sp["ln_b"])
    conv_w = jnp.pad(sp["conv_w"].astype(F32), ((0, HALO - CONV_K), (0, 0)))
    b_f = jnp.pad(row(sp["b_f"]), ((0, 0), (0, LANES - FOX_HEADS)))
    n_main = 2 * CONV_CH + 3 * FOX_W
    w_main = w_in_full[:, :n_main]
    w_f = jnp.pad(w_in_full[:, n_main:], ((0, 0), (0, LANES - FOX_HEADS)))

    x2d = x.reshape(T, D)
    h, h_t = rmsnorm_fwd(x2d, g_mix, name="rms_mix")
    z = matmul(h, w_main, out_dtype=BF16, tn=n_main, name="mm_in")
    z3 = z.reshape(B, S, n_main)
    n_mid = len(LATE_MID)
    (conv_out, conv_t), partly_mid = conv_branch_fwd(z3, conv_w, conv_b, ln_g, ln_b, name="conv_fwd",
                                                     rider=AllGatherStage1(late_shards[:n_mid]))
    f_raw, c_col, c_row = fgate_fwd(h.reshape(B, S, D), w_f, b_f, name="fgate_fwd")
    (att, lse, att_t), partly_ffn = fox_fwd(z3, c_col, c_row, name="fox_fwd",
                                            rider=AllGatherStage1(late_shards[n_mid:]))
    gathered = all_gather_stage2(partly_mid + partly_ffn, name="ag_late_stage2")
    wf = {n: _full_from_gathered(n, blk) for n, blk in zip(LATE, gathered)}
    cat =jnp.concatenate([conv_out, att], axis=-1).reshape(T, D)
    x1 = matmul(cat, wf["w_out"], out_dtype=F32, res=x2d, tn=D, name="mm_out")
    hx, hx_t = rmsnorm_fwd(x1, g_x, name="rms_x")
    qm = matmul(hx, wf["w_mq"], out_dtype=BF16, tn=D, name="mm_mq")
    mem2d = mem.reshape(B * M, D)
    mem_n, mem_n_t = rmsnorm_fwd(mem2d, g_mem, name="rms_mem")
    kv = matmul(mem_n, wf["w_mkv"], out_dtype=BF16, tn=2 * D, name="mm_mkv").reshape(B, M, 2 * D)
    o, o_t = xattn_fwd(qm.reshape(B, S, D), kv, name="xattn_fwd")
    o = o.reshape(T, D)
    x2 = matmul(o, wf["w_mo"], out_dtype=F32, res=x1, tn=D, name="mm_mo")
    hf, hf_t = rmsnorm_fwd(x2, g_ffn, name="rms_ffn")
    gu = matmul(hf, wf["w_gu"], out_dtype=BF16, tn=2816, name="mm_gu")
    act, act_t = swiglu_fwd(gu, name="swiglu_fwd")
    x3 = matmul(act, wf["w_down"], out_dtype=F32, res=x2, tn=D, name="mm_down")
    dx3, dg_final, loss = final_loss_bwd(x3, g_final, target.reshape(T, D), name="loss_bwd")
    gw = {}
    gw["w_down"] = matmul(act_t, dx3, out_dtype=BF16, tm=1408, tn=256, name="dw_down")
    dact = matmul(dx3, wf["w_down"], tb=True, out_dtype=BF16, tn=2816, name="dx_down")
    dgu = swiglu_bwd(gu, dact, name="swiglu_bwd")
    gw["w_gu"] = matmul(hf_t, dgu, out_dtype=BF16, tn=1408, name="dw_gu")
    dhf = matmul(dgu, wf["w_gu"], tb=True, out_dtype=BF16, tm=256, tn=D, name="dx_gu")
    dx2, dg_ffn = rmsnorm_bwd(x2, g_ffn, dhf, dx3, name="rms_ffn_bwd")
    gw["w_mo"] = matmul(o_t, dx2, out_dtype=BF16, name="dw_mo")
    do = matmul(dx2, wf["w_mo"], tb=True, out_dtype=BF16, tn=D, name="dx_mo")
    dqm, dkv = xattn_bwd(qm.reshape(B, S, D), kv, do.reshape(B, S, D), name="xattn_bwd")
    dqm = dqm.reshape(T, D)
    dkv = dkv.reshape(B * M, 2 * D)
    gw["w_mq"] = matmul(hx_t, dqm, out_dtype=BF16, tn=D, name="dw_mq")
    dhx = matmul(dqm, wf["w_mq"], tb=True, out_dtype=BF16, tn=D, name="dx_mq")
    gw["w_mkv"] = matmul(mem_n_t, dkv, out_dtype=BF16, tn=D, name="dw_mkv")
    dmem_n = matmul(dkv, wf["w_mkv"], tb=True, out_dtype=BF16, tn=D, name="dx_mkv")
    _, dg_mem = rmsnorm_bwd(mem2d, g_mem, dmem_n, None, name="rms_mem_bwd")
    dx1, dg_x = rmsnorm_bwd(x1, g_x, dhx, dx2, name="rms_x_bwd")
    gw["w_out"] = jnp.concatenate([matmul(conv_t, dx1, out_dtype=BF16, name="dw_out_conv"),
                                   matmul(att_t, dx1, out_dtype=BF16, name="dw_out_att")], axis=0)
    dcat = matmul(dx1, wf["w_out"], tb=True, out_dtype=BF16, tn=D, name="dx_out").reshape(B, S, D)
    dy, dconv_w, dvec = conv_branch_bwd_a(z3, dcat, conv_w, conv_b, ln_g, ln_b, name="conv_bwd_a")
    dug = conv_branch_bwd_b(z3, dy, conv_w, name="conv_bwd_b")
    parts, gots = {}, {}
    for n, p in zip(RS_GROUPS[0], reduce_to_chips(RS_GROUPS[0], gw, tag="ffn")):
        parts[n] = p
    for n, p in zip(RS_GROUPS[1], reduce_to_chips(RS_GROUPS[1], gw, tag="mid")):
        parts[n] = p
    (dq, stats), got = fox_bwd_dq(z3, dcat, lse, c_col, c_row, name="fox_bwd_dq",
                                  rider=ChipExchange([parts[n] for n in RS_GROUPS[0]]))
    gots.update(zip(RS_GROUPS[0], got))
    (dk, dv, dc), got = fox_bwd_dkdv(z3, dcat, stats, c_col, name="fox_bwd_dkdv",
                                     rider=ChipExchange([parts[n] for n in RS_GROUPS[1]]))
    gots.update(zip(RS_GROUPS[1], got))
    df, db_f = fgate_bwd(dc, f_raw, b_f, name="fgate_bwd")
    dz = jnp.concatenate([dug, dq, dk, dv], axis=-1).reshape(T, n_main)
    df2 = df.reshape(T, LANES)
    dw_main = matmul(h_t, dz, out_dtype=BF16, tn=1280, name="dw_in")
    dw_f = matmul(h_t, df2, out_dtype=BF16, name="dw_f")
    gw["w_in"] = jnp.concatenate([dw_main, dw_f[:, :FOX_HEADS]], axis=-1)
    dh_f = matmul(df2, w_f, tb=True, out_dtype=F32, tn=D, name="dx_f")
    parts["w_in"] = reduce_to_chips(RS_GROUPS[2], gw, tag="in")[0]
    dh, (gots["w_in"],) = matmul(dz, w_main, tb=True, out_dtype=F32, res=dh_f, tn=D, name="dx_in",
                                 rider=ChipExchange([parts["w_in"]]))
    dx, dg_mix = rmsnorm_bwd(x2d, g_mix, dh, dx1, name="rms_mix_bwd")
    gs = dict(g_mix=dg_mix, b_f=db_f[:, :FOX_HEADS], conv_w=dconv_w[:CONV_K], conv_b=dvec[0:1],
              ln_g=dvec[1:2], ln_b=dvec[2:3], g_x=dg_x, g_mem=dg_mem, g_ffn=dg_ffn, g_final=dg_final)
    return loss, dx.reshape(B, S, D), gs, {n: (parts[n], gots[n]) for n in BIG}


def _me():
    return lax.axis_index("x"), lax.axis_index("y"), lax.axis_index("c")


def _any_specs(n):
    return [pl.BlockSpec(memory_space=pl.ANY)] * n


def all_gather(xs, *, name):
    n = len(xs)

    def body(*refs):
        x_refs, out_refs = refs[:n], refs[n:2 * n]
        send_sems, recv_sems, local_sems = refs[2 * n:]
        x, y, c = _me()
        me, sibling = (x, y, c), (x, y, 1 - c)
        chips = [(1 - x, y), (x, 1 - y), (1 - x, 1 - y)]

        def slot(a, px, py, pc):
            return out_refs[a].at[4 * px + 2 * py + pc]

        def copy(a, k, block, to, own=False):
            return pltpu.make_async_remote_copy(
                src_ref=x_refs[a] if own else slot(a, *block), dst_ref=slot(a, *block),
                send_sem=send_sems.at[k, a], recv_sem=recv_sems.at[k, a], device_id=to, device_id_type=MESH)

        mine = [pltpu.make_async_copy(x_refs[a], slot(a, *me), local_sems.at[a]) for a in range(n)]
        first = [copy(a, 0, me, sibling, own=True) for a in range(n)]
        first += [copy(a, 1 + j, me, (*chip, c), own=True) for j, chip in enumerate(chips) for a in range(n)]
        for cp in mine + first:
            cp.start()
        passed = []
        for j, chip in enumerate(chips):
            for a in range(n):
                copy(a, 1 + j, (*chip, c), me).wait_recv()
                passed.append(copy(a, 4 + j, (*chip, c), sibling))
                passed[-1].start()
        for a in range(n):
            copy(a, 0, sibling, me).wait_recv()
            for j, chip in enumerate(chips):
                copy(a, 4 + j, (*chip, 1 - c), me).wait_recv()
        for cp in first + passed:
            cp.wait_send()
        for cp in mine:
            cp.wait()

    return _call(
        body, name=name, in_specs=_any_specs(n), out_specs=_any_specs(n),
        out_shape=[jax.ShapeDtypeStruct((N_DEV,) + v.shape, v.dtype) for v in xs],
        scratch_shapes=[pltpu.SemaphoreType.DMA((7, n)), pltpu.SemaphoreType.DMA((7, n)),
                        pltpu.SemaphoreType.DMA((n,))],
    )(*xs)


def sibling_exchange(gs, *, name):
    n = len(gs)

    def body(*refs):
        g_refs, out_refs = refs[:n], refs[n:2 * n]
        send_sems, recv_sems = refs[2 * n:]
        x, y, c = _me()
        cps = [pltpu.make_async_remote_copy(
            src_ref=g_refs[a].at[:, 1 - c], dst_ref=out_refs[a], send_sem=send_sems.at[a],
            recv_sem=recv_sems.at[a], device_id=(x, y, 1 - c), device_id_type=MESH) for a in range(n)]
        for cp in cps:
            cp.start()
        for cp in cps:
            cp.wait()

    return _call(
        body, name=name, in_specs=_any_specs(n), out_specs=_any_specs(n),
        out_shape=[jax.ShapeDtypeStruct((4,) + g.shape[2:], g.dtype) for g in gs],
        scratch_shapes=[pltpu.SemaphoreType.DMA((n,)), pltpu.SemaphoreType.DMA((n,))],
    )(*gs)


class ChipExchange:
    def __init__(self, ps):
        n = len(ps)
        self.n, self.inputs = n, list(ps)
        self.out_shape = [jax.ShapeDtypeStruct(p.shape, p.dtype) for p in ps]
        self.scratch = [pltpu.SemaphoreType.DMA((3, n)), pltpu.SemaphoreType.DMA((3, n))]

    def _copies(self, p_refs, out_refs, sems, outgoing):
        send_sems, recv_sems = sems
        x, y, c = _me()
        my_chip = 2 * x + y
        cps = []
        for k in range(3):
            px, py = x ^ ((k + 1) >> 1), y ^ ((k + 1) & 1)
            src, dst = (2 * px + py, my_chip) if outgoing else (my_chip, 2 * px + py)
            for a in range(self.n):
                cps.append(pltpu.make_async_remote_copy(
                    src_ref=p_refs[a].at[src], dst_ref=out_refs[a].at[dst], send_sem=send_sems.at[k, a],
                    recv_sem=recv_sems.at[k, a], device_id=(px, py, c), device_id_type=MESH))
        return cps

    def start(self, in_refs, out_refs, sems):
        for cp in self._copies(in_refs, out_refs, sems, True):
            cp.start()

    def finish(self, in_refs, out_refs, sems):
        for cp in self._copies(in_refs, out_refs, sems, False):
            cp.wait_recv()
        for cp in self._copies(in_refs, out_refs, sems, True):
            cp.wait_send()


class AllGatherStage1:
    def __init__(self, xs):
        n = len(xs)
        self.n, self.inputs = n, list(xs)
        self.out_shape = [jax.ShapeDtypeStruct((N_DEV,) + v.shape, v.dtype) for v in xs]
        self.scratch = [pltpu.SemaphoreType.DMA((4, n)), pltpu.SemaphoreType.DMA((4, n)),
                        pltpu.SemaphoreType.DMA((n,))]

    def _copies(self, x_refs, out_refs, sems, kind):
        send_sems, recv_sems, local_sems = sems
        x, y, c = _me()
        slot = lambda a, d: out_refs[a].at[4 * d[0] + 2 * d[1] + d[2]]
        if kind == "local":
            return [pltpu.make_async_copy(x_refs[a], slot(a, (x, y, c)), local_sems.at[a]) for a in range(self.n)]
        cps = []
        for k, peer in enumerate([(x, y, 1 - c), (1 - x, y, c), (x, 1 - y, c), (1 - x, 1 - y, c)]):
            for a in range(self.n):
                cps.append(pltpu.make_async_remote_copy(
                    src_ref=x_refs[a], dst_ref=slot(a, (x, y, c) if kind == "out" else peer),
                    send_sem=send_sems.at[k, a], recv_sem=recv_sems.at[k, a], device_id=peer, device_id_type=MESH))
        return cps

    def start(self, in_refs, out_refs, sems):
        for cp in self._copies(in_refs, out_refs, sems, "local") + self._copies(in_refs, out_refs, sems, "out"):
            cp.start()

    def finish(self, in_refs, out_refs, sems):
        for cp in self._copies(in_refs, out_refs, sems, "in"):
            cp.wait_recv()
        for cp in self._copies(in_refs, out_refs, sems, "out"):
            cp.wait_send()
        for cp in self._copies(in_refs, out_refs, sems, "local"):
            cp.wait()


def all_gather_stage2(outs, *, name):
    n = len(outs)

    def body(*refs):
        out_refs = refs[n:2 * n]
        send_sems, recv_sems = refs[2 * n:]
        x, y, c = _me()
        sends, recvs = [], []
        for k, (px, py) in enumerate([(1 - x, y), (x, 1 - y), (1 - x, 1 - y)]):
            for a in range(n):
                mk = lambda pc: pltpu.make_async_remote_copy(
                    src_ref=out_refs[a].at[4 * px + 2 * py + c], dst_ref=out_refs[a].at[4 * px + 2 * py + pc],
                    send_sem=send_sems.at[k, a], recv_sem=recv_sems.at[k, a], device_id=(x, y, 1 - c),
                    device_id_type=MESH)
                sends.append(mk(c))
                recvs.append(mk(1 - c))
        for cp in sends:
            cp.start()
        for cp in recvs:
            cp.wait_recv()
        for cp in sends:
            cp.wait_send()

    return _call(
        body, name=name, in_specs=_any_specs(n), out_specs=_any_specs(n),
        out_shape=[jax.ShapeDtypeStruct(o.shape, o.dtype) for o in outs],
        input_output_aliases={a: a for a in range(n)},
        scratch_shapes=[pltpu.SemaphoreType.DMA((3, n)), pltpu.SemaphoreType.DMA((3, n))],
    )(*outs)


def hosted_call(body, rider, *, name, grid, in_specs, out_specs, out_shape, scratch_shapes, args, vmem=None):
    n_in, n_out, n_scr = len(in_specs), len(out_specs), len(scratch_shapes)
    r_in, r_out = (len(rider.inputs), len(rider.out_shape)) if rider is not None else (0, 0)

    def wrapped(*refs):
        ins, refs = refs[:n_in], refs[n_in:]
        rins, refs = refs[:r_in], refs[r_in:]
        outs, refs = refs[:n_out], refs[n_out:]
        routs, refs = refs[:r_out], refs[r_out:]
        scr, rscr = refs[:n_scr], refs[n_scr:]
        ids = [pl.program_id(d) for d in range(len(grid))]
        first = functools.reduce(jnp.logical_and, [i == 0 for i in ids], True)
        last = functools.reduce(jnp.logical_and, [i == g - 1 for i, g in zip(ids, grid)], True)
        if rider is not None and grid:
            pl.when(first)(lambda: rider.start(rins, routs, rscr))
        elif rider is not None:
            rider.start(rins, routs, rscr)
        if body is not None:
            body(*ins, *outs, *scr)
        if rider is not None and grid:
            pl.when(last)(lambda: rider.finish(rins, routs, rscr))
        elif rider is not None:
            rider.finish(rins, routs, rscr)

    kw = dict(grid=grid) if grid else {}
    if grid or vmem is not None:
        kw["compiler_params"] = _params(("arbitrary",) * len(grid) if grid else None, vmem)
    res = _call(
        wrapped, name=name, in_specs=list(in_specs) + _any_specs(r_in), out_specs=list(out_specs) + _any_specs(r_out),
        out_shape=list(out_shape) + (rider.out_shape if rider is not None else []),
        scratch_shapes=list(scratch_shapes) + (rider.scratch if rider is not None else []), **kw,
    )(*args, *(rider.inputs if rider is not None else []))
    return list(res[:n_out]), list(res[n_out:])


def _pick_rows(r, target=256):
    best = None
    for d in range(16, min(r, target) + 1, 16):
        if r % d == 0:
            best = d
    return r if best is None else best


def pair_sum(g, got, *, name):
    _, _, R, C = g.shape
    tr = _pick_rows(R)

    def body(g_ref, got_ref, o_ref):
        mine = jnp.where(lax.axis_index("c") == 0, g_ref[:, 0], g_ref[:, 1])
        o_ref[...] = (mine.astype(F32) + got_ref[...].astype(F32)).astype(o_ref.dtype)

    return _call(
        body, name=name, grid=(R // tr,),
        in_specs=[pl.BlockSpec((4, 2, tr, C), lambda i: (0, 0, i, 0)), pl.BlockSpec((4, tr, C), lambda i: (0, i, 0))],
        out_specs=pl.BlockSpec((4, tr, C), lambda i: (0, i, 0)),
        out_shape=jax.ShapeDtypeStruct((4, R, C), g.dtype),
        compiler_params=_params(("parallel",)),
    )(g, got)


def chip_sum_adamw(p, got, w, m, v, *, name):
    _, R, C = p.shape
    tr = _pick_rows(R)

    def body(p_ref, got_ref, w_ref, m_ref, v_ref, g_ref, d_ref, mo_ref, vo_ref):
        my_chip = 2 * lax.axis_index("x") + lax.axis_index("y")
        g = jnp.zeros((tr, C), F32)
        for j in range(4):
            g = g + jnp.where(my_chip == j, p_ref[j], got_ref[j]).astype(F32)
        g_ref[...] = g
        d_ref[...], mo_ref[...], vo_ref[...] = _adamw_math(w_ref[...], g, m_ref[...], v_ref[...])

    part = pl.BlockSpec((4, tr, C), lambda i: (0, i, 0))
    spec = pl.BlockSpec((tr, C), lambda i: (i, 0))
    return _call(
        body, name=name, grid=(R // tr,), in_specs=[part, part, spec, spec, spec], out_specs=[spec] * 4,
        out_shape=[jax.ShapeDtypeStruct((R, C), F32)] * 4,
        compiler_params=_params(("parallel",)),
    )(p, got, w, m, v)


def rows_sum(g8, *, name):
    _, R, C = g8.shape

    def body(g_ref, o_ref):
        acc = g_ref[0]
        for j in range(1, N_DEV):
            acc = acc + g_ref[j]
        o_ref[...] = acc

    return _call(body, name=name, out_shape=jax.ShapeDtypeStruct((R, C), F32))(g8)


def _adamw_math(w, g, m, v):
    m = ADAM_B1 * m + (1.0 - ADAM_B1) * g
    v = ADAM_B2 * v + (1.0 - ADAM_B2) * (g * g)
    m_hat = m / (1.0 - ADAM_B1 ** ADAM_STEP)
    v_hat = v / (1.0 - ADAM_B2 ** ADAM_STEP)
    delta = -ADAM_LR * (m_hat / (jnp.sqrt(v_hat) + ADAM_EPS) + ADAM_WD * w)
    return delta, m, v


def adamw_small(wgmv, *, name):
    n = len(wgmv)

    def body(*refs):
        ins, outs = refs[:4 * n], refs[4 * n:]
        for a in range(n):
            w_ref, g_ref, m_ref, v_ref = ins[4 * a:4 * a + 4]
            d, mn, vn = _adamw_math(w_ref[...], g_ref[...], m_ref[...], v_ref[...])
            outs[3 * a][...] = d
            outs[3 * a + 1][...] = mn
            outs[3 * a + 2][...] = vn

    flat = [t for tup in wgmv for t in tup]
    res = _call(
        body, name=name,
        out_shape=[jax.ShapeDtypeStruct(tup[0].shape, F32) for tup in wgmv for _ in range(3)],
    )(*flat)
    return [tuple(res[3 * a:3 * a + 3]) for a in range(n)]


BIG = ("w_in", "w_out", "w_mq", "w_mkv", "w_mo", "w_gu", "w_down")
COL_SHARDED = ("w_in", "w_mkv", "w_gu")
SMALL = ("g_mix", "b_f", "conv_w", "conv_b", "ln_g", "ln_b", "g_x", "g_mem", "g_ffn", "g_final")


def _full_from_gathered(n, blk):
    _, rr, cc = blk.shape
    if n in COL_SHARDED:
        return jnp.concatenate([blk[k] for k in range(N_DEV)], axis=1)
    return blk.reshape(N_DEV * rr, cc)


def _shards_from_full(n, g):
    rr, cc = g.shape
    if n in COL_SHARDED:
        w = cc // N_DEV
        return jnp.stack([g[:, k * w:(k + 1) * w] for k in range(N_DEV)]).reshape(4, 2, rr, w)
    return g.reshape(4, 2, rr // N_DEV, cc)


def _small_layout():
    sizes = dict(g_mix=1024, b_f=8, conv_w=CONV_K * CONV_CH, conv_b=512, ln_g=512, ln_b=512, g_x=1024,
                 g_mem=1024, g_ffn=1024, g_final=1024, loss=1)
    lay, r0 = {}, 0
    for n, sz in sizes.items():
        r = -(-sz // LANES)
        lay[n] = (r0, r, sz)
        r0 += r
    return lay, -(-r0 // 8) * 8


def kernel(x, mem, g_mix, w_in, b_f, conv_w, conv_b, ln_g, ln_b, w_out, g_x, g_mem, w_mq, w_mkv, w_mo, g_ffn, w_gu, w_down, g_final, loss_target, m_g_mix, m_w_in, m_b_f, m_conv_w, m_conv_b, m_ln_g, m_ln_b, m_w_out, m_g_x, m_g_mem, m_w_mq, m_w_mkv, m_w_mo, m_g_ffn, m_w_gu, m_w_down, m_g_final, v_g_mix, v_w_in, v_b_f, v_conv_w, v_conv_b, v_ln_g, v_ln_b, v_w_out, v_g_x, v_g_mem, v_w_mq, v_w_mkv, v_w_mo, v_g_ffn, v_w_gu, v_w_down, v_g_final):
    names = ["g_mix", "w_in", "b_f", "conv_w", "conv_b", "ln_g", "ln_b", "w_out", "g_x", "g_mem", "w_mq",
             "w_mkv", "w_mo", "g_ffn", "w_gu", "w_down", "g_final"]
    W = dict(zip(names, [g_mix, w_in, b_f, conv_w, conv_b, ln_g, ln_b, w_out, g_x, g_mem, w_mq, w_mkv, w_mo,
                         g_ffn, w_gu, w_down, g_final]))
    Mo = dict(zip(names, [m_g_mix, m_w_in, m_b_f, m_conv_w, m_conv_b, m_ln_g, m_ln_b, m_w_out, m_g_x, m_g_mem,
                          m_w_mq, m_w_mkv, m_w_mo, m_g_ffn, m_w_gu, m_w_down, m_g_final]))
    Vo = dict(zip(names, [v_g_mix, v_w_in, v_b_f, v_conv_w, v_conv_b, v_ln_g, v_ln_b, v_w_out, v_g_x, v_g_mem,
                          v_w_mq, v_w_mkv, v_w_mo, v_g_ffn, v_w_gu, v_w_down, v_g_final]))
    dev = 4 * lax.axis_index("x") + 2 * lax.axis_index("y") + lax.axis_index("c")

    two = lambda a: a.reshape(-1, a.shape[-1])
    cw_shard = jnp.pad(two(conv_w), ((0, HALO - CONV_K), (0, 0)))
    w_in8, cw8 = all_gather([two(w_in).astype(BF16), cw_shard], name="ag_first")
    cw_full = cw8.transpose(1, 0, 2).reshape(HALO, -1)[:CONV_K]

    sp = dict(g_mix=g_mix, b_f=b_f, conv_w=cw_full, conv_b=conv_b, ln_g=ln_g, ln_b=ln_b, g_x=g_x, g_mem=g_mem,
              g_ffn=g_ffn, g_final=g_final)
    loss_blk, grad_x, gs, reduced = local_step(x, mem, loss_target, sp, _full_from_gathered("w_in", w_in8),
                                               [two(W[n]).astype(BF16) for n in LATE])

    lay, rs = _small_layout()
    small = {**{n: gs[n] for n in SMALL}, "loss": loss_blk[:, :1]}
    parts = []
    for n, (r0, r, sz) in lay.items():
        flat = small[n].reshape(-1).astype(F32)
        parts.append(jnp.pad(flat, (0, r * LANES - sz)).reshape(r, LANES))
    spack = jnp.concatenate(parts, axis=0)
    spack = jnp.pad(spack, ((0, rs - spack.shape[0]), (0, 0)))
    ssum = rows_sum(all_gather([spack], name="ag_small")[0], name="small_sum")
    gsmall = {n: ssum[r0:r0 + r].reshape(-1)[:sz] for n, (r0, r, sz) in lay.items()}
    loss = gsmall["loss"].reshape(())

    grads, delta, new_m, new_v = {}, {}, {}, {}
    for n in BIG:
        p, o = reduced[n]
        shp = W[n].shape
        g, d, mn, vn = chip_sum_adamw(p, o, two(W[n]), two(Mo[n]), two(Vo[n]), name="adamw_" + n)
        grads[n], delta[n], new_m[n], new_v[n] = g.reshape(shp), d.reshape(shp), mn.reshape(shp), vn.reshape(shp)
    for n in SMALL:
        if n == "conv_w":
            full = gsmall[n].reshape(CONV_K, CONV_CH)
            ncol = conv_w.shape[-1]
            grads[n] = lax.dynamic_slice(full, (0, dev * ncol), (CONV_K, ncol)).reshape(conv_w.shape)
        else:
            grads[n] = gsmall[n].reshape(W[n].shape)
    upd = adamw_small([(two(W[n]), two(grads[n]), two(Mo[n]), two(Vo[n])) for n in SMALL], name="adamw_small")
    for n, (d, mn, vn) in zip(SMALL, upd):
        shp = W[n].shape
        delta[n], new_m[n], new_v[n] = d.reshape(shp), mn.reshape(shp), vn.reshape(shp)
    return (loss, grad_x, *[grads[n] for n in names], *[delta[n] for n in names],
            *[new_m[n] for n in names], *[new_v[n] for n in names])
```

```python
import functools
import math

import jax
import jax.numpy as jnp
from jax import lax
from jax.experimental import pallas as pl
from jax.experimental.pallas import tpu as pltpu

F32 = jnp.float32
BF16 = jnp.bfloat16
EPS = 1e-6
N_DEV = 8
CONV_CH = 512
CONV_K = 31
FOX_HEADS = 8
FOX_HEAD_DIM = 64
FOX_W = 512
MEM_HEADS = 4
MEM_HEAD_DIM = 256
HALO = 32
LANES = 128
ADAM_LR, ADAM_B1, ADAM_B2, ADAM_EPS, ADAM_WD, ADAM_STEP = 0.001, 0.9, 0.999, 1e-08, 0.01, 10
NEG = -1e30
VMEM_CAP = 60 * 1024 * 1024
MESH = pl.DeviceIdType.MESH


def _call(body, **kw):
    call = pl.pallas_call(body, **kw)
    return lambda *args: call(*[pltpu.with_memory_space_constraint(a, pltpu.HBM) for a in args])


def _params(sem=None, vmem=None):
    kw = {}
    if sem is not None:
        kw["dimension_semantics"] = sem
    if vmem is not None:
        kw["vmem_limit_bytes"] = int(min(VMEM_CAP, vmem))
    return pltpu.CompilerParams(**kw)


def _nbytes(shape, dtype):
    return math.prod(shape) * jnp.dtype(dtype).itemsize


def _pick(n, target):
    best = None
    for d in range(LANES, min(n, target) + 1, LANES):
        if n % d == 0:
            best = d
    return n if best is None else best


def matmul(a, b, *, tb=False, out_dtype, res=None, tm=512, tn=512, tk=None, name, rider=None):
    M, K = a.shape
    N = b.shape[0] if tb else b.shape[1]
    assert (b.shape[1] if tb else b.shape[0]) == K
    tm, tn = _pick(M, tm), _pick(N, tn)
    tk = K if tk is None else _pick(K, tk)
    assert M % tm == 0 and N % tn == 0 and K % tk == 0, (name, M, N, K, tm, tn, tk)
    nk = K // tk
    dn = (((1,), (1 if tb else 0,)), ((), ()))

    def body(*refs):
        if res is not None:
            a_ref, b_ref, r_ref, o_ref = refs[:4]
        else:
            a_ref, b_ref, o_ref = refs[:3]
        p = lax.dot_general(a_ref[...].astype(BF16), b_ref[...].astype(BF16), dn,
                            preferred_element_type=F32)

        def finish(acc):
            if res is not None:
                acc = acc + r_ref[...].astype(F32)
            o_ref[...] = acc.astype(out_dtype)

        if nk == 1:
            finish(p)
        else:
            acc_ref = refs[-1]
            k = pl.program_id(2)

            @pl.when(k == 0)
            def _():
                acc_ref[...] = p

            @pl.when(k > 0)
            def _():
                acc_ref[...] += p

            @pl.when(k == nk - 1)
            def _():
                finish(acc_ref[...])

    a_spec = pl.BlockSpec((tm, tk), lambda i, j, k: (i, k))
    b_spec = pl.BlockSpec((tn, tk), lambda i, j, k: (j, k)) if tb else pl.BlockSpec((tk, tn), lambda i, j, k: (k, j))
    o_spec = pl.BlockSpec((tm, tn), lambda i, j, k: (i, j))
    in_specs, args = [a_spec, b_spec], [a, b]
    est = 2 * (_nbytes((tm, tk), a.dtype) + _nbytes((tk, tn), b.dtype) + _nbytes((tm, tn), out_dtype))
    est += (a.dtype != BF16) * _nbytes((tm, tk), BF16) + (b.dtype != BF16) * _nbytes((tk, tn), BF16)
    est += 2 * _nbytes((tm, tn), F32)
    if res is not None:
        in_specs.append(o_spec)
        args.append(res)
        est += 2 * _nbytes((tm, tn), res.dtype)
    (out,), rode = hosted_call(
        body, rider, name=name, grid=(M // tm, N // tn, nk),
        in_specs=in_specs, out_specs=[o_spec],
        out_shape=[jax.ShapeDtypeStruct((M, N), out_dtype)],
        scratch_shapes=[] if nk == 1 else [pltpu.VMEM((tm, tn), F32)],
        args=args, vmem=est + (8 << 20),
    )
    return out if rider is None else (out, rode)


def _rms_scale(x):
    return lax.rsqrt(jnp.mean(x * x, axis=-1, keepdims=True) + EPS)


def rmsnorm_fwd(x, g, *, name, tm=512):
    T, D = x.shape
    tm = min(tm, T)

    def body(x_ref, g_ref, o_ref, ot_ref):
        xv = x_ref[...]
        h = xv * _rms_scale(xv) * g_ref[...]
        o_ref[...] = h.astype(BF16)
        ot_ref[...] = h.T.astype(BF16)

    return _call(
        body, name=name, grid=(T // tm,),
        in_specs=[pl.BlockSpec((tm, D), lambda i: (i, 0)), pl.BlockSpec((1, D), lambda i: (0, 0))],
        out_specs=[pl.BlockSpec((tm, D), lambda i: (i, 0)), pl.BlockSpec((D, tm), lambda i: (0, i))],
        out_shape=[jax.ShapeDtypeStruct((T, D), BF16), jax.ShapeDtypeStruct((D, T), BF16)],
        compiler_params=_params(("parallel",)),
    )(x, g)


def _rms_bwd_math(xv, gv, dh):
    r = _rms_scale(xv)
    xh = xv * r
    dg = jnp.sum(dh * xh, axis=0, keepdims=True)
    dxh = dh * gv
    dx = r * (dxh - xh * jnp.mean(dxh * xh, axis=-1, keepdims=True))
    return dx, dg


def rmsnorm_bwd(x, g, dh, dres, *, name, tm=256):
    T, D = x.shape
    tm = min(tm, T)

    def body(*refs):
        if dres is not None:
            x_ref, g_ref, dh_ref, dr_ref, dx_ref, dg_ref = refs
        else:
            x_ref, g_ref, dh_ref, dx_ref, dg_ref = refs
        dx, dg = _rms_bwd_math(x_ref[...], g_ref[...], dh_ref[...].astype(F32))
        if dres is not None:
            dx = dx + dr_ref[...]
        dx_ref[...] = dx

        @pl.when(pl.program_id(0) == 0)
        def _():
            dg_ref[...] = jnp.zeros_like(dg_ref)

        dg_ref[...] += dg

    row = pl.BlockSpec((tm, D), lambda i: (i, 0))
    vec = pl.BlockSpec((1, D), lambda i: (0, 0))
    ins, args = [row, vec, row], [x, g, dh]
    if dres is not None:
        ins.append(row)
        args.append(dres)
    return _call(
        body, name=name, grid=(T // tm,), in_specs=ins, out_specs=[row, vec],
        out_shape=[jax.ShapeDtypeStruct((T, D), F32), jax.ShapeDtypeStruct((1, D), F32)],
        compiler_params=_params(("arbitrary",)),
    )(*args)


def final_loss_bwd(x, g, target, *, name, tm=256):
    T, D = x.shape
    tm = min(tm, T)

    def body(x_ref, g_ref, t_ref, dx_ref, dg_ref, l_ref):
        xv, gv = x_ref[...], g_ref[...]
        e = xv * _rms_scale(xv) * gv - t_ref[...]
        part = 0.5 * jnp.sum(jnp.mean(e * e, axis=-1, keepdims=True), axis=0, keepdims=True)
        dx, dg = _rms_bwd_math(xv, gv, e * (1.0 / D))
        dx_ref[...] = dx

        @pl.when(pl.program_id(0) == 0)
        def _():
            dg_ref[...] = jnp.zeros_like(dg_ref)
            l_ref[...] = jnp.zeros_like(l_ref)

        dg_ref[...] += dg
        l_ref[...] += jnp.broadcast_to(part, l_ref.shape)

    row = pl.BlockSpec((tm, D), lambda i: (i, 0))
    vec = pl.BlockSpec((1, D), lambda i: (0, 0))
    return _call(
        body, name=name, grid=(T // tm,), in_specs=[row, vec, row],
        out_specs=[row, vec, pl.BlockSpec((1, LANES), lambda i: (0, 0))],
        out_shape=[jax.ShapeDtypeStruct((T, D), F32), jax.ShapeDtypeStruct((1, D), F32),
                   jax.ShapeDtypeStruct((1, LANES), F32)],
        compiler_params=_params(("arbitrary",)),
    )(x, g, target)


def _sigmoid(v):
    return 1.0 / (1.0 + jnp.exp(-v))


def _glu(blk):
    u = blk[:, :CONV_CH].astype(F32)
    gt = blk[:, CONV_CH:].astype(F32)
    return u * _sigmoid(gt)


def _fill_causal_ext(ext, cur_ref, halo_ref, s, ts):
    ext[pl.ds(HALO, ts), :] = _glu(cur_ref[0])
    hal = _glu(halo_ref[0])
    ext[pl.ds(0, HALO), :] = jnp.where(s > 0, hal, 0.0)


SUBLANES = 8


def _make_shifted(ext, sh):
    n = ext.shape[0]
    full = ext[...]
    for r in range(1, SUBLANES):
        sh[r - 1] = pltpu.roll(full, n - r, 0)


def _tap(ext, sh, off, ts):
    r = off % SUBLANES
    return ext[pl.ds(off, ts), :] if r == 0 else sh[r - 1, pl.ds(off - r, ts), :]


def _causal_conv(ext, sh, w_ref, ts):
    acc = jnp.zeros((ts, CONV_CH), F32)
    for j in range(CONV_K):
        acc = acc + _tap(ext, sh, HALO - (CONV_K - 1) + j, ts) * w_ref[pl.ds(j, 1), :]
    return acc


def _ln_stats(y):
    mu = jnp.mean(y, axis=-1, keepdims=True)
    yc = y - mu
    rstd = lax.rsqrt(jnp.mean(yc * yc, axis=-1, keepdims=True) + EPS)
    return yc * rstd, rstd


def _conv_specs(ts, S):
    nh = ts // HALO
    cur = pl.BlockSpec((1, ts, 2 * CONV_CH), lambda b, s: (b, s, 0))
    halo = pl.BlockSpec((1, HALO, 2 * CONV_CH), lambda b, s: (b, jnp.maximum(s * nh - 1, 0), 0))
    w = pl.BlockSpec((HALO, CONV_CH), lambda b, s: (0, 0))
    vec = pl.BlockSpec((1, CONV_CH), lambda b, s: (0, 0))
    return cur, halo, w, vec


def conv_branch_fwd(ug, conv_w, conv_b, ln_g, ln_b, *, name, ts=256, rider=None):
    B, S, _ = ug.shape
    ts = min(ts, S)
    ns = S // ts
    cur, halo, w, vec = _conv_specs(ts, S)

    def body(cur_ref, halo_ref, w_ref, cb_ref, lg_ref, lb_ref, o_ref, ot_ref, ext, sh):
        _fill_causal_ext(ext, cur_ref, halo_ref, pl.program_id(1), ts)
        _make_shifted(ext, sh)
        y = _causal_conv(ext, sh, w_ref, ts) + cb_ref[...]
        yh, _ = _ln_stats(y)
        ln = yh * lg_ref[...] + lb_ref[...]
        out = ln * _sigmoid(ln)
        o_ref[0] = out.astype(BF16)
        ot_ref[...] = out.T.astype(BF16)

    return hosted_call(
        body, rider, name=name, grid=(B, ns), in_specs=[cur, halo, w, vec, vec, vec],
        out_specs=[pl.BlockSpec((1, ts, CONV_CH), lambda b, s: (b, s, 0)),
                   pl.BlockSpec((CONV_CH, ts), lambda b, s: (0, b * ns + s))],
        out_shape=[jax.ShapeDtypeStruct((B, S, CONV_CH), BF16), jax.ShapeDtypeStruct((CONV_CH, B * S), BF16)],
        scratch_shapes=[pltpu.VMEM((ts + HALO, CONV_CH), F32),
                        pltpu.VMEM((SUBLANES - 1, ts + HALO, CONV_CH), F32)],
        args=(ug, ug, conv_w, conv_b, ln_g, ln_b),
    )


def conv_branch_bwd_a(ug, dcat, conv_w, conv_b, ln_g, ln_b, *, name, ts=256):
    B, S, _ = ug.shape
    ts = min(ts, S)
    cur, halo, w, vec = _conv_specs(ts, S)

    def body(cur_ref, halo_ref, d_ref, w_ref, cb_ref, lg_ref, lb_ref, dy_ref, dw_ref, dv_ref, ext, sh):
        _fill_causal_ext(ext, cur_ref, halo_ref, pl.program_id(1), ts)
        _make_shifted(ext, sh)
        y = _causal_conv(ext, sh, w_ref, ts) + cb_ref[...]
        yh, rstd = _ln_stats(y)
        lg = lg_ref[...]
        ln = yh * lg + lb_ref[...]
        sg = _sigmoid(ln)
        dln = d_ref[0].astype(F32) * (sg * (1.0 + ln * (1.0 - sg)))
        dyh = dln * lg
        dy = rstd * (dyh - jnp.mean(dyh, axis=-1, keepdims=True)
                     - yh * jnp.mean(dyh * yh, axis=-1, keepdims=True))
        dy_ref[0] = dy

        @pl.when((pl.program_id(0) == 0) & (pl.program_id(1) == 0))
        def _():
            dw_ref[...] = jnp.zeros_like(dw_ref)
            dv_ref[...] = jnp.zeros_like(dv_ref)

        dv_ref[pl.ds(0, 1), :] += jnp.sum(dy, axis=0, keepdims=True)
        dv_ref[pl.ds(1, 1), :] += jnp.sum(dln * yh, axis=0, keepdims=True)
        dv_ref[pl.ds(2, 1), :] += jnp.sum(dln, axis=0, keepdims=True)
        for j in range(CONV_K):
            tap = _tap(ext, sh, HALO - (CONV_K - 1) + j, ts)
            dw_ref[pl.ds(j, 1), :] += jnp.sum(dy * tap, axis=0, keepdims=True)

    return _call(
        body, name=name, grid=(B, S // ts),
        in_specs=[cur, halo, pl.BlockSpec((1, ts, CONV_CH), lambda b, s: (b, s, 0)), w, vec, vec, vec],
        out_specs=[pl.BlockSpec((1, ts, CONV_CH), lambda b, s: (b, s, 0)),
                   pl.BlockSpec((HALO, CONV_CH), lambda b, s: (0, 0)),
                   pl.BlockSpec((8, CONV_CH), lambda b, s: (0, 0))],
        out_shape=[jax.ShapeDtypeStruct((B, S, CONV_CH), F32),
                   jax.ShapeDtypeStruct((HALO, CONV_CH), F32),
                   jax.ShapeDtypeStruct((8, CONV_CH), F32)],
        scratch_shapes=[pltpu.VMEM((ts + HALO, CONV_CH), F32),
                        pltpu.VMEM((SUBLANES - 1, ts + HALO, CONV_CH), F32)],
        compiler_params=_params(("arbitrary", "arbitrary")),
    )(ug, ug, dcat, conv_w, conv_b, ln_g, ln_b)


def conv_branch_bwd_b(ug, dy, conv_w, *, name, ts=256):
    B, S, _ = ug.shape
    ts = min(ts, S)
    nh, n_halo = ts // HALO, S // HALO

    def body(cur_ref, dy_ref, nxt_ref, w_ref, o_ref, ext, sh):
        last = pl.program_id(1) == pl.num_programs(1) - 1
        ext[pl.ds(0, ts), :] = dy_ref[0]
        ext[pl.ds(ts, HALO), :] = jnp.where(last, 0.0, nxt_ref[0])
        _make_shifted(ext, sh)
        da = jnp.zeros((ts, CONV_CH), F32)
        for j in range(CONV_K):
            da = da + _tap(ext, sh, CONV_K - 1 - j, ts) * w_ref[pl.ds(j, 1), :]
        blk = cur_ref[0]
        u = blk[:, :CONV_CH].astype(F32)
        sg = _sigmoid(blk[:, CONV_CH:].astype(F32))
        o_ref[0, :, :CONV_CH] = (da * sg).astype(BF16)
        o_ref[0, :, CONV_CH:] = (da * u * sg * (1.0 - sg)).astype(BF16)

    return _call(
        body, name=name, grid=(B, S // ts),
        in_specs=[pl.BlockSpec((1, ts, 2 * CONV_CH), lambda b, s: (b, s, 0)),
                  pl.BlockSpec((1, ts, CONV_CH), lambda b, s: (b, s, 0)),
                  pl.BlockSpec((1, HALO, CONV_CH), lambda b, s: (b, jnp.minimum((s + 1) * nh, n_halo - 1), 0)),
                  pl.BlockSpec((HALO, CONV_CH), lambda b, s: (0, 0))],
        out_specs=pl.BlockSpec((1, ts, 2 * CONV_CH), lambda b, s: (b, s, 0)),
        out_shape=jax.ShapeDtypeStruct((B, S, 2 * CONV_CH), BF16),
        scratch_shapes=[pltpu.VMEM((ts + HALO, CONV_CH), F32),
                        pltpu.VMEM((SUBLANES - 1, ts + HALO, CONV_CH), F32)],
        compiler_params=_params(("parallel", "parallel")),
    )(ug, dy, dy, conv_w)


def _tri(n, lower):
    r = lax.broadcasted_iota(jnp.int32, (n, n), 0)
    c = lax.broadcasted_iota(jnp.int32, (n, n), 1)
    return ((r >= c) if lower else (r <= c)).astype(F32)


def _eye(n):
    r = lax.broadcasted_iota(jnp.int32, (n, n), 0)
    c = lax.broadcasted_iota(jnp.int32, (n, n), 1)
    return (r == c).astype(F32)


def _dot_hi(a, b, dn):
    return lax.dot_general(a, b, dn, precision=lax.Precision.HIGHEST, preferred_element_type=F32)


NN = (((1,), (0,)), ((), ()))
NT = (((1,), (1,)), ((), ()))
TN = (((0,), (0,)), ((), ()))


def _log_sigmoid(v):
    e = jnp.exp(-jnp.abs(v))
    log1p_e = jnp.where(e < 1e-3, e * (1.0 - 0.5 * e), jnp.log(1.0 + e))
    return jnp.minimum(v, 0.0) - log1p_e


def fgate_fwd(h, w_f, b_f, *, name, ts=256, rider=None):
    B, S, D = h.shape
    ts = min(ts, S)

    def body(h_ref, w_ref, b_ref, f_ref, cc_ref, cr_ref, carry):
        @pl.when(pl.program_id(1) == 0)
        def _():
            carry[...] = jnp.zeros_like(carry)

        f = jnp.dot(h_ref[0], w_ref[...], preferred_element_type=F32)
        f_ref[0] = f
        logf = _log_sigmoid(f + b_ref[...])
        c = _dot_hi(_tri(ts, True), logf, NN) + carry[pl.ds(0, 1), :]
        cc_ref[0] = c
        carry[pl.ds(0, 1), :] = c[ts - 1:ts, :]
        cr_ref[0] = _dot_hi(_eye(LANES), c, NT)

    return hosted_call(
        body, rider, name=name, grid=(B, S // ts),
        in_specs=[pl.BlockSpec((1, ts, D), lambda b, s: (b, s, 0)),
                  pl.BlockSpec((D, LANES), lambda b, s: (0, 0)),
                  pl.BlockSpec((1, LANES), lambda b, s: (0, 0))],
        out_specs=[pl.BlockSpec((1, ts, LANES), lambda b, s: (b, s, 0)),
                   pl.BlockSpec((1, ts, LANES), lambda b, s: (b, s, 0)),
                   pl.BlockSpec((1, LANES, ts), lambda b, s: (b, 0, s))],
        out_shape=[jax.ShapeDtypeStruct((B, S, LANES), F32), jax.ShapeDtypeStruct((B, S, LANES), F32),
                   jax.ShapeDtypeStruct((B, LANES, S), F32)],
        scratch_shapes=[pltpu.VMEM((8, LANES), F32)],
        args=(h, w_f, b_f),
    )


def fgate_bwd(dc, f, b_f, *, name, ts=256):
    B, S, _ = f.shape
    P = dc.shape[1]
    ts = min(ts, S)
    ns = S // ts

    def body(dc_ref, f_ref, b_ref, df_ref, db_ref, carry):
        @pl.when(pl.program_id(1) == 0)
        def _():
            carry[...] = jnp.zeros_like(carry)

        @pl.when((pl.program_id(0) == 0) & (pl.program_id(1) == 0))
        def _():
            db_ref[...] = jnp.zeros_like(db_ref)

        dc_t = dc_ref[0, 0]
        for j in range(1, P):
            dc_t = dc_t + dc_ref[0, j]
        dlogf = _dot_hi(_tri(ts, False), dc_t, NN) + carry[pl.ds(0, 1), :]
        carry[pl.ds(0, 1), :] = dlogf[0:1, :]
        df = dlogf * _sigmoid(-(f_ref[0] + b_ref[...]))
        df_ref[0] = df.astype(BF16)
        db_ref[...] += jnp.sum(df, axis=0, keepdims=True)

    return _call(
        body, name=name, grid=(B, ns),
        in_specs=[pl.BlockSpec((1, P, ts, LANES), lambda b, s: (b, 0, ns - 1 - s, 0)),
                  pl.BlockSpec((1, ts, LANES), lambda b, s: (b, ns - 1 - s, 0)),
                  pl.BlockSpec((1, LANES), lambda b, s: (0, 0))],
        out_specs=[pl.BlockSpec((1, ts, LANES), lambda b, s: (b, ns - 1 - s, 0)),
                   pl.BlockSpec((1, LANES), lambda b, s: (0, 0))],
        out_shape=[jax.ShapeDtypeStruct((B, S, LANES), BF16), jax.ShapeDtypeStruct((1, LANES), F32)],
        scratch_shapes=[pltpu.VMEM((8, LANES), F32)],
        compiler_params=_params(("arbitrary", "arbitrary")),
    )(dc, f, b_f)


def _lane_pick(tile, idx):
    lane = lax.broadcasted_iota(jnp.int32, tile.shape, 1)
    return jnp.sum(jnp.where(lane == idx, tile, 0.0), axis=-1, keepdims=True)


FOX_T = 512


def _fox_heads(q, cc_ref, p):
    lane = lax.broadcasted_iota(jnp.int32, q.shape, 1)
    qs = q * (1.0 / math.sqrt(FOX_HEAD_DIM))
    qhs = [jnp.where((lane < FOX_HEAD_DIM) == (hh == 0), qs, jnp.zeros_like(qs)) for hh in range(2)]
    crefs = [_lane_pick(cc_ref[0, pl.ds(0, 1), :], 2 * p + hh) for hh in range(2)]
    return qhs, crefs


def _fold_lanes(x, op):
    out = x[:, :LANES]
    for j in range(1, x.shape[1] // LANES):
        out = op(out, x[:, j * LANES:(j + 1) * LANES])
    return out


def _causal(t, transposed):
    r = lax.broadcasted_iota(jnp.int32, (t, t), 0)
    c = lax.broadcasted_iota(jnp.int32, (t, t), 1)
    return (r <= c) if transposed else (c <= r)


QKV0 = 8


def fox_fwd(z, c_col, c_row, *, name, rider=None):
    B, S, _ = z.shape
    assert S % FOX_T == 0
    tq, nq = FOX_T, S // FOX_T
    npair = FOX_HEADS // 2

    def body(q_ref, k_ref, v_ref, cc_ref, cr_ref, o_ref, l_ref, ot_ref, s_scr, m_scr, acc_scr):
        p, qi = pl.program_id(1), pl.program_id(2)
        qhs, crefs = _fox_heads(q_ref[0], cc_ref, p)
        lane = lax.broadcasted_iota(jnp.int32, (tq, LANES), 1)
        first = lane < FOX_HEAD_DIM
        for hh in range(2):
            m_scr[hh] = jnp.full((tq, LANES), NEG, F32)
            acc_scr[hh] = jnp.zeros((tq, LANES), F32)

        def logits(kb, diagonal):
            k0 = pl.multiple_of(kb * tq, tq)
            k = k_ref[0, pl.ds(k0, tq), :]
            for hh in range(2):
                s = lax.dot_general(qhs[hh], k, NT, preferred_element_type=F32)
                s = s + (crefs[hh] - cr_ref[0, pl.ds(2 * p + hh, 1), pl.ds(k0, tq)])
                if diagonal:
                    s = jnp.where(_causal(tq, False), s, NEG)
                s_scr[hh, kb] = s
                m_scr[hh] = jnp.maximum(m_scr[hh], _fold_lanes(s, jnp.maximum))

        def sweep1(kb, carry):
            logits(kb, False)
            return carry

        lax.fori_loop(0, qi, sweep1, 0)
        logits(qi, True)
        ms = [jnp.max(m_scr[hh], axis=-1, keepdims=True) for hh in range(2)]
        mbs = [jnp.broadcast_to(ms[hh], (tq, tq)) for hh in range(2)]

        for hh in range(2):
            m_scr[hh] = jnp.zeros((tq, LANES), F32)

        def weigh(kb, carry):
            k0 = pl.multiple_of(kb * tq, tq)
            v = v_ref[0, pl.ds(k0, tq), :]
            for hh in range(2):
                pr = jnp.exp(s_scr[hh, kb] - mbs[hh])
                m_scr[hh] += _fold_lanes(pr, jnp.add)
                acc_scr[hh] += jnp.dot(pr.astype(BF16), v, preferred_element_type=F32)
            return carry

        lax.fori_loop(0, qi + 1, weigh, 0)
        accs = [acc_scr[hh] for hh in range(2)]
        ls = [jnp.sum(m_scr[hh], axis=-1, keepdims=True) for hh in range(2)]
        out = jnp.where(first, accs[0] / ls[0], accs[1] / ls[1])
        o_ref[0] = out.astype(BF16)
        ot_ref[...] = out.T.astype(BF16)
        l_ref[0, 0] = jnp.where(first, ms[0] + jnp.log(ls[0]), ms[1] + jnp.log(ls[1]))

    return hosted_call(
        body, rider, name=name, grid=(B, npair, nq),
        in_specs=[pl.BlockSpec((1, tq, LANES), lambda b, p, i: (b, i, QKV0 + p)),
                  pl.BlockSpec((1, S, LANES), lambda b, p, i: (b, 0, QKV0 + npair + p)),
                  pl.BlockSpec((1, S, LANES), lambda b, p, i: (b, 0, QKV0 + 2 * npair + p)),
                  pl.BlockSpec((1, tq, LANES), lambda b, p, i: (b, i, 0)),
                  pl.BlockSpec((1, 8, S), lambda b, p, i: (b, 0, 0))],
        out_specs=[pl.BlockSpec((1, tq, LANES), lambda b, p, i: (b, i, p)),
                   pl.BlockSpec((1, 1, tq, LANES), lambda b, p, i: (b, p, i, 0)),
                   pl.BlockSpec((LANES, tq), lambda b, p, i: (p, b * nq + i))],
        out_shape=[jax.ShapeDtypeStruct((B, S, FOX_W), BF16),
                   jax.ShapeDtypeStruct((B, npair, S, LANES), F32),
                   jax.ShapeDtypeStruct((FOX_W, B * S), BF16)],
        scratch_shapes=[pltpu.VMEM((2, nq, tq, tq), F32), pltpu.VMEM((2, tq, LANES), F32),
                        pltpu.VMEM((2, tq, LANES), F32)],
        args=(z, z, z, c_col, c_row),
    )


def fox_bwd_dq(z, dcat, lse, c_col, c_row, *, name, rider=None):
    B, S, _ = z.shape
    tq, nq = FOX_T, S // FOX_T
    npair = FOX_HEADS // 2

    def body(q_ref, k_ref, v_ref, do_ref, l_ref, cc_ref, cr_ref, dq_ref, st_ref, p_scr, dp_scr, dl_scr):
        p, qi = pl.program_id(1), pl.program_id(2)
        qhs, crefs = _fox_heads(q_ref[0], cc_ref, p)
        lane = lax.broadcasted_iota(jnp.int32, (tq, LANES), 1)
        do_b = do_ref[0].astype(BF16)
        dohs = [jnp.where((lane < FOX_HEAD_DIM) == (hh == 0), do_b, jnp.zeros_like(do_b)) for hh in range(2)]
        lses = [_lane_pick(l_ref[0, 0], hh * FOX_HEAD_DIM) for hh in range(2)]
        lbs = [jnp.broadcast_to(lses[hh], (tq, tq)) for hh in range(2)]
        for hh in range(2):
            dl_scr[hh] = jnp.zeros((tq, LANES), F32)

        def probs(kb, diagonal):
            k0 = pl.multiple_of(kb * tq, tq)
            k = k_ref[0, pl.ds(k0, tq), :]
            v = v_ref[0, pl.ds(k0, tq), :]
            for hh in range(2):
                s = lax.dot_general(qhs[hh], k, NT, preferred_element_type=F32)
                s = s + (crefs[hh] - cr_ref[0, pl.ds(2 * p + hh, 1), pl.ds(k0, tq)])
                pr = jnp.exp(s - lbs[hh])
                if diagonal:
                    pr = jnp.where(_causal(tq, False), pr, 0.0)
                dp = lax.dot_general(dohs[hh], v, NT, preferred_element_type=F32)
                pdp = pr * dp
                dl_scr[hh] += _fold_lanes(pdp, jnp.add)
                p_scr[hh, kb] = pr
                dp_scr[hh, kb] = dp

        def first_pass(kb, carry):
            probs(kb, False)
            return carry

        lax.fori_loop(0, qi, first_pass, 0)
        probs(qi, True)

        dls = [jnp.sum(dl_scr[hh], axis=-1, keepdims=True) for hh in range(2)]
        dlbs = [jnp.broadcast_to(dls[hh], (tq, tq)) for hh in range(2)]

        def second_pass(kb, dq):
            k0 = pl.multiple_of(kb * tq, tq)
            k = k_ref[0, pl.ds(k0, tq), :]
            for hh in range(2):
                ds = p_scr[hh, kb] * (dp_scr[hh, kb] - dlbs[hh])
                kh = jnp.where((lane < FOX_HEAD_DIM) == (hh == 0), k, jnp.zeros_like(k))
                dq = dq + jnp.dot(ds.astype(BF16), kh, preferred_element_type=F32)
            return dq

        dq = lax.fori_loop(0, qi + 1, second_pass, jnp.zeros((tq, LANES), F32))
        dq_ref[0] = (dq * (1.0 / math.sqrt(FOX_HEAD_DIM))).astype(BF16)
        cols = jnp.zeros((tq, LANES), F32)
        for j, col in enumerate([crefs[0] - lses[0], crefs[1] - lses[1], dls[0], dls[1]]):
            cols = jnp.where(lane == j, col, cols)
        st_ref[0, 0] = _dot_hi(_eye(LANES), cols, NT)[:8]

    return hosted_call(
        body, rider, name=name, grid=(B, npair, nq),
        in_specs=[pl.BlockSpec((1, tq, LANES), lambda b, p, i: (b, i, QKV0 + p)),
                  pl.BlockSpec((1, S, LANES), lambda b, p, i: (b, 0, QKV0 + npair + p)),
                  pl.BlockSpec((1, S, LANES), lambda b, p, i: (b, 0, QKV0 + 2 * npair + p)),
                  pl.BlockSpec((1, tq, LANES), lambda b, p, i: (b, i, npair + p)),
                  pl.BlockSpec((1, 1, tq, LANES), lambda b, p, i: (b, p, i, 0)),
                  pl.BlockSpec((1, tq, LANES), lambda b, p, i: (b, i, 0)),
                  pl.BlockSpec((1, 8, S), lambda b, p, i: (b, 0, 0))],
        out_specs=[pl.BlockSpec((1, tq, LANES), lambda b, p, i: (b, i, p)),
                   pl.BlockSpec((1, 1, 8, tq), lambda b, p, i: (b, p, 0, i))],
        out_shape=[jax.ShapeDtypeStruct((B, S, FOX_W), BF16), jax.ShapeDtypeStruct((B, npair, 8, S), F32)],
        scratch_shapes=[pltpu.VMEM((2, nq, tq, tq), F32), pltpu.VMEM((2, nq, tq, tq), F32),
                        pltpu.VMEM((2, tq, LANES), F32)],
        args=(z, z, z, dcat, lse, c_col, c_row), vmem=56 << 20,
    )


def fox_bwd_dkdv(z, dcat, stats, c_col, *, name, rider=None):
    B, S, _ = z.shape
    tk, nq = FOX_T, S // FOX_T
    npair = FOX_HEADS // 2
    inv = 1.0 / math.sqrt(FOX_HEAD_DIM)

    def body(q_ref, k_ref, v_ref, do_ref, st_ref, cc_ref, dk_ref, dv_ref, dc_ref, dk_scr, dv_scr, dc_scr):
        p, kt = pl.program_id(1), pl.program_id(2)
        lane = lax.broadcasted_iota(jnp.int32, (tk, LANES), 1)
        masks = [(lane < FOX_HEAD_DIM) == (hh == 0) for hh in range(2)]
        k = k_ref[0]
        v = v_ref[0]
        khs = [jnp.where(masks[hh], k, jnp.zeros_like(k)) for hh in range(2)]
        vhs = [jnp.where(masks[hh], v, jnp.zeros_like(v)) for hh in range(2)]
        ccbs = [jnp.broadcast_to(_lane_pick(cc_ref[0], 2 * p + hh), (tk, tk)) for hh in range(2)]
        dk_scr[...] = jnp.zeros_like(dk_scr)
        dv_scr[...] = jnp.zeros_like(dv_scr)
        dc_scr[...] = jnp.zeros_like(dc_scr)

        def tile(qb, diagonal):
            q0 = pl.multiple_of(qb * tk, tk)
            qs = q_ref[0, pl.ds(q0, tk), :] * inv
            do_b = do_ref[0, pl.ds(q0, tk), :].astype(BF16)
            for hh in range(2):
                st = lax.dot_general(khs[hh], qs, NT, preferred_element_type=F32)
                pr = jnp.exp(st - ccbs[hh] + st_ref[0, 0, pl.ds(hh, 1), pl.ds(q0, tk)])
                if diagonal:
                    pr = jnp.where(_causal(tk, True), pr, 0.0)
                dp = lax.dot_general(vhs[hh], do_b, NT, preferred_element_type=F32)
                ds = pr * (dp - st_ref[0, 0, pl.ds(2 + hh, 1), pl.ds(q0, tk)])
                dv_scr[...] += jnp.dot(pr.astype(BF16), jnp.where(masks[hh], do_b, jnp.zeros_like(do_b)),
                                       preferred_element_type=F32)
                dk_scr[...] += jnp.dot(ds.astype(BF16), jnp.where(masks[hh], qs, jnp.zeros_like(qs)),
                                       preferred_element_type=F32)
                dc_scr[hh] -= _fold_lanes(ds, jnp.add)

        def later(qb, carry):
            tile(qb, False)
            return carry

        tile(kt, True)
        lax.fori_loop(kt + 1, nq, later, 0)
        dk_ref[0] = dk_scr[...].astype(BF16)
        dv_ref[0] = dv_scr[...].astype(BF16)
        dcs = [jnp.sum(dc_scr[hh], axis=-1, keepdims=True) for hh in range(2)]
        dc_ref[0, 0] = jnp.where(lane == 2 * p, dcs[0], jnp.where(lane == 2 * p + 1, dcs[1], 0.0))

    full = lambda col: pl.BlockSpec((1, S, LANES), col)
    tile_spec = lambda col: pl.BlockSpec((1, tk, LANES), col)
    return hosted_call(
        body, rider, name=name, grid=(B, npair, nq),
        in_specs=[full(lambda b, p, t: (b, 0, QKV0 + p)),
                  tile_spec(lambda b, p, t: (b, t, QKV0 + npair + p)),
                  tile_spec(lambda b, p, t: (b, t, QKV0 + 2 * npair + p)),
                  full(lambda b, p, t: (b, 0, npair + p)),
                  pl.BlockSpec((1, 1, 8, S), lambda b, p, t: (b, p, 0, 0)),
                  tile_spec(lambda b, p, t: (b, t, 0))],
        out_specs=[tile_spec(lambda b, p, t: (b, t, p)), tile_spec(lambda b, p, t: (b, t, p)),
                   pl.BlockSpec((1, 1, tk, LANES), lambda b, p, t: (b, p, t, 0))],
        out_shape=[jax.ShapeDtypeStruct((B, S, FOX_W), BF16)] * 2
        + [jax.ShapeDtypeStruct((B, npair, S, LANES), F32)],
        scratch_shapes=[pltpu.VMEM((tk, LANES), F32), pltpu.VMEM((tk, LANES), F32),
                        pltpu.VMEM((2, tk, LANES), F32)],
        args=(z, z, z, dcat, stats, c_col),
    )


def xattn_fwd(qm, kv, *, name, tq=256):
    B, S, D = qm.shape
    M = kv.shape[1]
    tq = min(tq, S)
    inv = 1.0 / math.sqrt(MEM_HEAD_DIM)

    nq = S // tq

    def body(q_ref, kv_ref, o_ref, ot_ref):
        for h in range(MEM_HEADS):
            c0 = h * MEM_HEAD_DIM
            qh = q_ref[0, :, c0:c0 + MEM_HEAD_DIM]
            kh = kv_ref[0, :, c0:c0 + MEM_HEAD_DIM]
            vh = kv_ref[0, :, D + c0:D + c0 + MEM_HEAD_DIM]
            s = lax.dot_general(qh, kh, NT, preferred_element_type=F32) * inv
            e = jnp.exp(s - jnp.max(s, axis=-1, keepdims=True))
            o = jnp.dot(e.astype(BF16), vh, preferred_element_type=F32) / jnp.sum(e, axis=-1, keepdims=True)
            o_ref[0, :, c0:c0 + MEM_HEAD_DIM] = o.astype(BF16)
            ot_ref[c0:c0 + MEM_HEAD_DIM, :] = o.T.astype(BF16)

    return _call(
        body, name=name, grid=(B, nq),
        in_specs=[pl.BlockSpec((1, tq, D), lambda b, i: (b, i, 0)),
                  pl.BlockSpec((1, M, 2 * D), lambda b, i: (b, 0, 0))],
        out_specs=[pl.BlockSpec((1, tq, D), lambda b, i: (b, i, 0)),
                   pl.BlockSpec((D, tq), lambda b, i: (0, b * nq + i))],
        out_shape=[jax.ShapeDtypeStruct((B, S, D), BF16), jax.ShapeDtypeStruct((D, B * S), BF16)],
        compiler_params=_params(("parallel", "parallel")),
    )(qm, kv)


def xattn_bwd(qm, kv, do, *, name, tq=256):
    B, S, D = qm.shape
    M = kv.shape[1]
    tq = min(tq, S)
    inv = 1.0 / math.sqrt(MEM_HEAD_DIM)

    def body(q_ref, kv_ref, do_ref, dq_ref, dkv_ref):
        @pl.when(pl.program_id(1) == 0)
        def _():
            dkv_ref[...] = jnp.zeros_like(dkv_ref)

        for h in range(MEM_HEADS):
            c0 = h * MEM_HEAD_DIM
            qh = q_ref[0, :, c0:c0 + MEM_HEAD_DIM]
            kh = kv_ref[0, :, c0:c0 + MEM_HEAD_DIM]
            vh = kv_ref[0, :, D + c0:D + c0 + MEM_HEAD_DIM]
            doh = do_ref[0, :, c0:c0 + MEM_HEAD_DIM]
            s = lax.dot_general(qh, kh, NT, preferred_element_type=F32) * inv
            e = jnp.exp(s - jnp.max(s, axis=-1, keepdims=True))
            pr = e / jnp.sum(e, axis=-1, keepdims=True)
            dp = lax.dot_general(doh, vh, NT, preferred_element_type=F32)
            ds = pr * (dp - jnp.sum(pr * dp, axis=-1, keepdims=True))
            ds_b = ds.astype(BF16)
            dq_ref[0, :, c0:c0 + MEM_HEAD_DIM] = (jnp.dot(ds_b, kh, preferred_element_type=F32) * inv).astype(BF16)
            dkv_ref[0, :, c0:c0 + MEM_HEAD_DIM] += lax.dot_general(ds_b, qh, TN, preferred_element_type=F32) * inv
            dkv_ref[0, :, D + c0:D + c0 + MEM_HEAD_DIM] += lax.dot_general(
                pr.astype(BF16), doh, TN, preferred_element_type=F32)

    row = pl.BlockSpec((1, tq, D), lambda b, i: (b, i, 0))
    kvs = pl.BlockSpec((1, M, 2 * D), lambda b, i: (b, 0, 0))
    return _call(
        body, name=name, grid=(B, S // tq), in_specs=[row, kvs, row], out_specs=[row, kvs],
        out_shape=[jax.ShapeDtypeStruct((B, S, D), BF16), jax.ShapeDtypeStruct((B, M, 2 * D), F32)],
        compiler_params=_params(("parallel", "arbitrary")),
    )(qm, kv, do)


def swiglu_fwd(gu, *, name, tm=256):
    T, F2 = gu.shape
    Fh = F2 // 2
    tm = min(tm, T)

    def body(gu_ref, o_ref, ot_ref):
        g = gu_ref[:, :Fh].astype(F32)
        u = gu_ref[:, Fh:].astype(F32)
        act = g * _sigmoid(g) * u
        o_ref[...] = act.astype(BF16)
        ot_ref[...] = act.T.astype(BF16)

    return _call(
        body, name=name, grid=(T // tm,),
        in_specs=[pl.BlockSpec((tm, F2), lambda i: (i, 0))],
        out_specs=[pl.BlockSpec((tm, Fh), lambda i: (i, 0)), pl.BlockSpec((Fh, tm), lambda i: (0, i))],
        out_shape=[jax.ShapeDtypeStruct((T, Fh), BF16), jax.ShapeDtypeStruct((Fh, T), BF16)],
        compiler_params=_params(("parallel",)),
    )(gu)


def swiglu_bwd(gu, dact, *, name, tm=256):
    T, F2 = gu.shape
    Fh = F2 // 2
    tm = min(tm, T)

    def body(gu_ref, d_ref, o_ref):
        g = gu_ref[:, :Fh].astype(F32)
        u = gu_ref[:, Fh:].astype(F32)
        d = d_ref[...].astype(F32)
        sg = _sigmoid(g)
        o_ref[:, :Fh] = (d * u * (sg * (1.0 + g * (1.0 - sg)))).astype(BF16)
        o_ref[:, Fh:] = (d * g * sg).astype(BF16)

    return _call(
        body, name=name, grid=(T // tm,),
        in_specs=[pl.BlockSpec((tm, F2), lambda i: (i, 0)), pl.BlockSpec((tm, Fh), lambda i: (i, 0))],
        out_specs=pl.BlockSpec((tm, F2), lambda i: (i, 0)),
        out_shape=jax.ShapeDtypeStruct((T, F2), BF16),
        compiler_params=_params(("parallel",)),
    )(gu, dact)


LATE_MID = ("w_out", "w_mq", "w_mo")
LATE_KV = ("w_mkv",)
LATE_FFN = ("w_gu", "w_down")
LATE = LATE_MID + LATE_KV + LATE_FFN
RS_GROUPS = (("w_gu", "w_down"), ("w_out", "w_mq", "w_mkv", "w_mo"), ("w_in",))


def pair_sums(names, g42, got):
    return {n: pair_sum(g, o, name="rs_pair_sum_" + n) for n, g, o in zip(names, g42, got)}


def local_step(x, mem, target, sp, w_in_full, late_shards):
    B, S, D = x.shape
    T = B * S
    M = mem.shape[1]
    row = lambda v: v.reshape(1, -1).astype(F32)
    g_mix, g_x, g_mem, g_ffn, g_final = (row(sp[k]) for k in ("g_mix", "g_x", "g_mem", "g_ffn", "g_final"))
    conv_b, ln_g, ln_b = row(sp["conv_b"]), row(sp["ln_g"]), row(sp["ln_b"])
    conv_w = jnp.pad(sp["conv_w"].astype(F32), ((0, HALO - CONV_K), (0, 0)))
    b_f = jnp.pad(row(sp["b_f"]), ((0, 0), (0, LANES - FOX_HEADS)))
    n_main = 2 * CONV_CH + 3 * FOX_W
    w_main = w_in_full[:, :n_main]
    w_f = jnp.pad(w_in_full[:, n_main:], ((0, 0), (0, LANES - FOX_HEADS)))

    x2d = x.reshape(T, D)
    h, h_t = rmsnorm_fwd(x2d, g_mix, name="rms_mix")
    z = matmul(h, w_main, out_dtype=BF16, tn=n_main, name="mm_in")
    z3 = z.reshape(B, S, n_main)
    n_mid, n_kv = len(LATE_MID), len(LATE_MID) + len(LATE_KV)
    (conv_out, conv_t), partly_mid = conv_branch_fwd(z3, conv_w, conv_b, ln_g, ln_b, name="conv_fwd",
                                                     rider=AllGatherStage1(late_shards[:n_mid]))
    (f_raw, c_col, c_row), partly_kv = fgate_fwd(h.reshape(B, S, D), w_f, b_f, name="fgate_fwd",
                                                 rider=AllGatherStage1(late_shards[n_mid:n_kv]))
    (att, lse, att_t), partly_ffn = fox_fwd(z3, c_col, c_row, name="fox_fwd",
                                            rider=AllGatherStage1(late_shards[n_kv:]))
    gathered = all_gather_stage2(partly_mid + partly_kv + partly_ffn, name="ag_late_stage2")
    wf = {n: _full_from_gathered(n, blk) for n, blk in zip(LATE, gathered)}
    cat =jnp.concatenate([conv_out, att], axis=-1).reshape(T, D)
    x1 = matmul(cat, wf["w_out"], out_dtype=F32, res=x2d, tn=D, name="mm_out")
    hx, hx_t = rmsnorm_fwd(x1, g_x, name="rms_x")
    qm = matmul(hx, wf["w_mq"], out_dtype=BF16, tn=D, name="mm_mq")
    mem2d = mem.reshape(B * M, D)
    mem_n, mem_n_t = rmsnorm_fwd(mem2d, g_mem, name="rms_mem")
    kv = matmul(mem_n, wf["w_mkv"], out_dtype=BF16, tn=2 * D, name="mm_mkv").reshape(B, M, 2 * D)
    o, o_t = xattn_fwd(qm.reshape(B, S, D), kv, name="xattn_fwd")
    o = o.reshape(T, D)
    x2 = matmul(o, wf["w_mo"], out_dtype=F32, res=x1, tn=D, name="mm_mo")
    hf, hf_t = rmsnorm_fwd(x2, g_ffn, name="rms_ffn")
    gu = matmul(hf, wf["w_gu"], out_dtype=BF16, tn=2816, name="mm_gu")
    act, act_t = swiglu_fwd(gu, name="swiglu_fwd")
    x3 = matmul(act, wf["w_down"], out_dtype=F32, res=x2, tn=D, name="mm_down")
    dx3, dg_final, loss = final_loss_bwd(x3, g_final, target.reshape(T, D), name="loss_bwd")
    gw = {}
    gw["w_down"] = matmul(act_t, dx3, out_dtype=BF16, tm=1408, tn=256, name="dw_down")
    dact = matmul(dx3, wf["w_down"], tb=True, out_dtype=BF16, tn=2816, name="dx_down")
    dgu = swiglu_bwd(gu, dact, name="swiglu_bwd")
    gw["w_gu"] = matmul(hf_t, dgu, out_dtype=BF16, tn=1408, name="dw_gu")
    g42 = [_shards_from_full(n, gw[n]) for n in RS_GROUPS[0]]
    dhf, got = matmul(dgu, wf["w_gu"], tb=True, out_dtype=BF16, tm=256, tn=D, name="dx_gu",
                      rider=SiblingExchange(g42))
    parts = pair_sums(RS_GROUPS[0], g42, got)
    dx2, dg_ffn = rmsnorm_bwd(x2, g_ffn, dhf, dx3, name="rms_ffn_bwd")
    gw["w_mo"] = matmul(o_t, dx2, out_dtype=BF16, name="dw_mo")
    do = matmul(dx2, wf["w_mo"], tb=True, out_dtype=BF16, tn=D, name="dx_mo")
    dqm, dkv = xattn_bwd(qm.reshape(B, S, D), kv, do.reshape(B, S, D), name="xattn_bwd")
    dqm = dqm.reshape(T, D)
    dkv = dkv.reshape(B * M, 2 * D)
    gw["w_mq"] = matmul(hx_t, dqm, out_dtype=BF16, tn=D, name="dw_mq")
    dhx = matmul(dqm, wf["w_mq"], tb=True, out_dtype=BF16, tn=D, name="dx_mq")
    gw["w_mkv"] = matmul(mem_n_t, dkv, out_dtype=BF16, tn=D, name="dw_mkv")
    dmem_n = matmul(dkv, wf["w_mkv"], tb=True, out_dtype=BF16, tn=D, name="dx_mkv")
    _, dg_mem = rmsnorm_bwd(mem2d, g_mem, dmem_n, None, name="rms_mem_bwd")
    dx1, dg_x = rmsnorm_bwd(x1, g_x, dhx, dx2, name="rms_x_bwd")
    gw["w_out"] = jnp.concatenate([matmul(conv_t, dx1, out_dtype=BF16, name="dw_out_conv"),
                                   matmul(att_t, dx1, out_dtype=BF16, name="dw_out_att")], axis=0)
    g42 = [_shards_from_full(n, gw[n]) for n in RS_GROUPS[1]]
    dcat, got = matmul(dx1, wf["w_out"], tb=True, out_dtype=BF16, tn=D, name="dx_out", rider=SiblingExchange(g42))
    dcat = dcat.reshape(B, S, D)
    parts.update(pair_sums(RS_GROUPS[1], g42, got))
    dy, dconv_w, dvec = conv_branch_bwd_a(z3, dcat, conv_w, conv_b, ln_g, ln_b, name="conv_bwd_a")
    dug = conv_branch_bwd_b(z3, dy, conv_w, name="conv_bwd_b")
    gots = {}
    (dq, stats), got = fox_bwd_dq(z3, dcat, lse, c_col, c_row, name="fox_bwd_dq",
                                  rider=ChipExchange([parts[n] for n in RS_GROUPS[0]]))
    gots.update(zip(RS_GROUPS[0], got))
    (dk, dv, dc), got = fox_bwd_dkdv(z3, dcat, stats, c_col, name="fox_bwd_dkdv",
                                     rider=ChipExchange([parts[n] for n in RS_GROUPS[1]]))
    gots.update(zip(RS_GROUPS[1], got))
    df, db_f = fgate_bwd(dc, f_raw, b_f, name="fgate_bwd")
    dz = jnp.concatenate([dug, dq, dk, dv], axis=-1).reshape(T, n_main)
    df2 = df.reshape(T, LANES)
    dw_main = matmul(h_t, dz, out_dtype=BF16, tn=1280, name="dw_in")
    dw_f = matmul(h_t, df2, out_dtype=BF16, name="dw_f")
    gw["w_in"] = jnp.concatenate([dw_main, dw_f[:, :FOX_HEADS]], axis=-1)
    dh_f = matmul(df2, w_f, tb=True, out_dtype=F32, tn=D, name="dx_f")
    g42 = [_shards_from_full("w_in", gw["w_in"])]
    parts.update(pair_sums(RS_GROUPS[2], g42, run_rider(SiblingExchange(g42), name="rs_sibling_in")))
    dh, (gots["w_in"],) = matmul(dz, w_main, tb=True, out_dtype=F32, res=dh_f, tn=D, name="dx_in",
                                 rider=ChipExchange([parts["w_in"]]))
    dx, dg_mix = rmsnorm_bwd(x2d, g_mix, dh, dx1, name="rms_mix_bwd")
    gs = dict(g_mix=dg_mix, b_f=db_f[:, :FOX_HEADS], conv_w=dconv_w[:CONV_K], conv_b=dvec[0:1],
              ln_g=dvec[1:2], ln_b=dvec[2:3], g_x=dg_x, g_mem=dg_mem, g_ffn=dg_ffn, g_final=dg_final)
    return loss, dx.reshape(B, S, D), gs, {n: (parts[n], gots[n]) for n in BIG}


def _me():
    return lax.axis_index("x"), lax.axis_index("y"), lax.axis_index("c")


def _any_specs(n):
    return [pl.BlockSpec(memory_space=pl.ANY)] * n


def all_gather(xs, *, name):
    n = len(xs)

    def body(*refs):
        x_refs, out_refs = refs[:n], refs[n:2 * n]
        send_sems, recv_sems, local_sems = refs[2 * n:]
        x, y, c = _me()
        me, sibling = (x, y, c), (x, y, 1 - c)
        chips = [(1 - x, y), (x, 1 - y), (1 - x, 1 - y)]

        def slot(a, px, py, pc):
            return out_refs[a].at[4 * px + 2 * py + pc]

        def copy(a, k, block, to, own=False):
            return pltpu.make_async_remote_copy(
                src_ref=x_refs[a] if own else slot(a, *block), dst_ref=slot(a, *block),
                send_sem=send_sems.at[k, a], recv_sem=recv_sems.at[k, a], device_id=to, device_id_type=MESH)

        mine = [pltpu.make_async_copy(x_refs[a], slot(a, *me), local_sems.at[a]) for a in range(n)]
        first = [copy(a, 0, me, sibling, own=True) for a in range(n)]
        first += [copy(a, 1 + j, me, (*chip, c), own=True) for j, chip in enumerate(chips) for a in range(n)]
        for cp in mine + first:
            cp.start()
        passed = []
        for j, chip in enumerate(chips):
            for a in range(n):
                copy(a, 1 + j, (*chip, c), me).wait_recv()
                passed.append(copy(a, 4 + j, (*chip, c), sibling))
                passed[-1].start()
        for a in range(n):
            copy(a, 0, sibling, me).wait_recv()
            for j, chip in enumerate(chips):
                copy(a, 4 + j, (*chip, 1 - c), me).wait_recv()
        for cp in first + passed:
            cp.wait_send()
        for cp in mine:
            cp.wait()

    return _call(
        body, name=name, in_specs=_any_specs(n), out_specs=_any_specs(n),
        out_shape=[jax.ShapeDtypeStruct((N_DEV,) + v.shape, v.dtype) for v in xs],
        scratch_shapes=[pltpu.SemaphoreType.DMA((7, n)), pltpu.SemaphoreType.DMA((7, n)),
                        pltpu.SemaphoreType.DMA((n,))],
    )(*xs)


class SiblingExchange:
    def __init__(self, gs):
        n = len(gs)
        self.n, self.inputs = n, list(gs)
        self.out_shape = [jax.ShapeDtypeStruct((4,) + g.shape[2:], g.dtype) for g in gs]
        self.scratch = [pltpu.SemaphoreType.DMA((n,)), pltpu.SemaphoreType.DMA((n,))]

    def _copies(self, g_refs, out_refs, sems):
        send_sems, recv_sems = sems
        x, y, c = _me()
        return [pltpu.make_async_remote_copy(
            src_ref=g_refs[a].at[:, 1 - c], dst_ref=out_refs[a], send_sem=send_sems.at[a],
            recv_sem=recv_sems.at[a], device_id=(x, y, 1 - c), device_id_type=MESH) for a in range(self.n)]

    def start(self, in_refs, out_refs, sems):
        for cp in self._copies(in_refs, out_refs, sems):
            cp.start()

    def finish(self, in_refs, out_refs, sems):
        for cp in self._copies(in_refs, out_refs, sems):
            cp.wait()


def run_rider(rider, *, name):
    return hosted_call(None, rider, name=name, grid=(), in_specs=[], out_specs=[], out_shape=[],
                       scratch_shapes=[], args=[])[1]


class ChipExchange:
    def __init__(self, ps):
        n = len(ps)
        self.n, self.inputs = n, list(ps)
        self.out_shape = [jax.ShapeDtypeStruct(p.shape, p.dtype) for p in ps]
        self.scratch = [pltpu.SemaphoreType.DMA((3, n)), pltpu.SemaphoreType.DMA((3, n))]

    def _copies(self, p_refs, out_refs, sems, outgoing):
        send_sems, recv_sems = sems
        x, y, c = _me()
        my_chip = 2 * x + y
        cps = []
        for k in range(3):
            px, py = x ^ ((k + 1) >> 1), y ^ ((k + 1) & 1)
            src, dst = (2 * px + py, my_chip) if outgoing else (my_chip, 2 * px + py)
            for a in range(self.n):
                cps.append(pltpu.make_async_remote_copy(
                    src_ref=p_refs[a].at[src], dst_ref=out_refs[a].at[dst], send_sem=send_sems.at[k, a],
                    recv_sem=recv_sems.at[k, a], device_id=(px, py, c), device_id_type=MESH))
        return cps

    def start(self, in_refs, out_refs, sems):
        for cp in self._copies(in_refs, out_refs, sems, True):
            cp.start()

    def finish(self, in_refs, out_refs, sems):
        for cp in self._copies(in_refs, out_refs, sems, False):
            cp.wait_recv()
        for cp in self._copies(in_refs, out_refs, sems, True):
            cp.wait_send()


class AllGatherStage1:
    def __init__(self, xs):
        n = len(xs)
        self.n, self.inputs = n, list(xs)
        self.out_shape = [jax.ShapeDtypeStruct((N_DEV,) + v.shape, v.dtype) for v in xs]
        self.scratch = [pltpu.SemaphoreType.DMA((4, n)), pltpu.SemaphoreType.DMA((4, n)),
                        pltpu.SemaphoreType.DMA((n,))]

    def _copies(self, x_refs, out_refs, sems, kind):
        send_sems, recv_sems, local_sems = sems
        x, y, c = _me()
        slot = lambda a, d: out_refs[a].at[4 * d[0] + 2 * d[1] + d[2]]
        if kind == "local":
            return [pltpu.make_async_copy(x_refs[a], slot(a, (x, y, c)), local_sems.at[a]) for a in range(self.n)]
        cps = []
        for k, peer in enumerate([(x, y, 1 - c), (1 - x, y, c), (x, 1 - y, c), (1 - x, 1 - y, c)]):
            for a in range(self.n):
                cps.append(pltpu.make_async_remote_copy(
                    src_ref=x_refs[a], dst_ref=slot(a, (x, y, c) if kind == "out" else peer),
                    send_sem=send_sems.at[k, a], recv_sem=recv_sems.at[k, a], device_id=peer, device_id_type=MESH))
        return cps

    def start(self, in_refs, out_refs, sems):
        for cp in self._copies(in_refs, out_refs, sems, "local") + self._copies(in_refs, out_refs, sems, "out"):
            cp.start()

    def finish(self, in_refs, out_refs, sems):
        for cp in self._copies(in_refs, out_refs, sems, "in"):
            cp.wait_recv()
        for cp in self._copies(in_refs, out_refs, sems, "out"):
            cp.wait_send()
        for cp in self._copies(in_refs, out_refs, sems, "local"):
            cp.wait()


def all_gather_stage2(outs, *, name):
    n = len(outs)

    def body(*refs):
        out_refs = refs[n:2 * n]
        send_sems, recv_sems = refs[2 * n:]
        x, y, c = _me()
        sends, recvs = [], []
        for k, (px, py) in enumerate([(1 - x, y), (x, 1 - y), (1 - x, 1 - y)]):
            for a in range(n):
                mk = lambda pc: pltpu.make_async_remote_copy(
                    src_ref=out_refs[a].at[4 * px + 2 * py + c], dst_ref=out_refs[a].at[4 * px + 2 * py + pc],
                    send_sem=send_sems.at[k, a], recv_sem=recv_sems.at[k, a], device_id=(x, y, 1 - c),
                    device_id_type=MESH)
                sends.append(mk(c))
                recvs.append(mk(1 - c))
        for cp in sends:
            cp.start()
        for cp in recvs:
            cp.wait_recv()
        for cp in sends:
            cp.wait_send()

    return _call(
        body, name=name, in_specs=_any_specs(n), out_specs=_any_specs(n),
        out_shape=[jax.ShapeDtypeStruct(o.shape, o.dtype) for o in outs],
        input_output_aliases={a: a for a in range(n)},
        scratch_shapes=[pltpu.SemaphoreType.DMA((3, n)), pltpu.SemaphoreType.DMA((3, n))],
    )(*outs)


def hosted_call(body, rider, *, name, grid, in_specs, out_specs, out_shape, scratch_shapes, args, vmem=None):
    n_in, n_out, n_scr = len(in_specs), len(out_specs), len(scratch_shapes)
    r_in, r_out = (len(rider.inputs), len(rider.out_shape)) if rider is not None else (0, 0)

    def wrapped(*refs):
        ins, refs = refs[:n_in], refs[n_in:]
        rins, refs = refs[:r_in], refs[r_in:]
        outs, refs = refs[:n_out], refs[n_out:]
        routs, refs = refs[:r_out], refs[r_out:]
        scr, rscr = refs[:n_scr], refs[n_scr:]
        ids = [pl.program_id(d) for d in range(len(grid))]
        first = functools.reduce(jnp.logical_and, [i == 0 for i in ids], True)
        last = functools.reduce(jnp.logical_and, [i == g - 1 for i, g in zip(ids, grid)], True)
        if rider is not None and grid:
            pl.when(first)(lambda: rider.start(rins, routs, rscr))
        elif rider is not None:
            rider.start(rins, routs, rscr)
        if body is not None:
            body(*ins, *outs, *scr)
        if rider is not None and grid:
            pl.when(last)(lambda: rider.finish(rins, routs, rscr))
        elif rider is not None:
            rider.finish(rins, routs, rscr)

    kw = dict(grid=grid) if grid else {}
    if grid or vmem is not None:
        kw["compiler_params"] = _params(("arbitrary",) * len(grid) if grid else None, vmem)
    res = _call(
        wrapped, name=name, in_specs=list(in_specs) + _any_specs(r_in), out_specs=list(out_specs) + _any_specs(r_out),
        out_shape=list(out_shape) + (rider.out_shape if rider is not None else []),
        scratch_shapes=list(scratch_shapes) + (rider.scratch if rider is not None else []), **kw,
    )(*args, *(rider.inputs if rider is not None else []))
    return list(res[:n_out]), list(res[n_out:])


def _pick_rows(r, target=256):
    best = None
    for d in range(16, min(r, target) + 1, 16):
        if r % d == 0:
            best = d
    return r if best is None else best


def pair_sum(g, got, *, name):
    _, _, R, C = g.shape
    tr = _pick_rows(R)

    def body(g_ref, got_ref, o_ref):
        mine = jnp.where(lax.axis_index("c") == 0, g_ref[:, 0], g_ref[:, 1])
        o_ref[...] = (mine.astype(F32) + got_ref[...].astype(F32)).astype(o_ref.dtype)

    return _call(
        body, name=name, grid=(R // tr,),
        in_specs=[pl.BlockSpec((4, 2, tr, C), lambda i: (0, 0, i, 0)), pl.BlockSpec((4, tr, C), lambda i: (0, i, 0))],
        out_specs=pl.BlockSpec((4, tr, C), lambda i: (0, i, 0)),
        out_shape=jax.ShapeDtypeStruct((4, R, C), g.dtype),
        compiler_params=_params(("parallel",)),
    )(g, got)


def chip_sum_adamw(p, got, w, m, v, *, name):
    _, R, C = p.shape
    tr = _pick_rows(R)

    def body(p_ref, got_ref, w_ref, m_ref, v_ref, g_ref, d_ref, mo_ref, vo_ref):
        my_chip = 2 * lax.axis_index("x") + lax.axis_index("y")
        g = jnp.zeros((tr, C), F32)
        for j in range(4):
            g = g + jnp.where(my_chip == j, p_ref[j], got_ref[j]).astype(F32)
        g_ref[...] = g
        d_ref[...], mo_ref[...], vo_ref[...] = _adamw_math(w_ref[...], g, m_ref[...], v_ref[...])

    part = pl.BlockSpec((4, tr, C), lambda i: (0, i, 0))
    spec = pl.BlockSpec((tr, C), lambda i: (i, 0))
    return _call(
        body, name=name, grid=(R // tr,), in_specs=[part, part, spec, spec, spec], out_specs=[spec] * 4,
        out_shape=[jax.ShapeDtypeStruct((R, C), F32)] * 4,
        compiler_params=_params(("parallel",)),
    )(p, got, w, m, v)


def rows_sum(g8, *, name):
    _, R, C = g8.shape

    def body(g_ref, o_ref):
        acc = g_ref[0]
        for j in range(1, N_DEV):
            acc = acc + g_ref[j]
        o_ref[...] = acc

    return _call(body, name=name, out_shape=jax.ShapeDtypeStruct((R, C), F32))(g8)


def _adamw_math(w, g, m, v):
    m = ADAM_B1 * m + (1.0 - ADAM_B1) * g
    v = ADAM_B2 * v + (1.0 - ADAM_B2) * (g * g)
    m_hat = m / (1.0 - ADAM_B1 ** ADAM_STEP)
    v_hat = v / (1.0 - ADAM_B2 ** ADAM_STEP)
    delta = -ADAM_LR * (m_hat / (jnp.sqrt(v_hat) + ADAM_EPS) + ADAM_WD * w)
    return delta, m, v


def adamw_small(wgmv, *, name):
    n = len(wgmv)

    def body(*refs):
        ins, outs = refs[:4 * n], refs[4 * n:]
        for a in range(n):
            w_ref, g_ref, m_ref, v_ref = ins[4 * a:4 * a + 4]
            d, mn, vn = _adamw_math(w_ref[...], g_ref[...], m_ref[...], v_ref[...])
            outs[3 * a][...] = d
            outs[3 * a + 1][...] = mn
            outs[3 * a + 2][...] = vn

    flat = [t for tup in wgmv for t in tup]
    res = _call(
        body, name=name,
        out_shape=[jax.ShapeDtypeStruct(tup[0].shape, F32) for tup in wgmv for _ in range(3)],
    )(*flat)
    return [tuple(res[3 * a:3 * a + 3]) for a in range(n)]


BIG = ("w_in", "w_out", "w_mq", "w_mkv", "w_mo", "w_gu", "w_down")
COL_SHARDED = ("w_in", "w_mkv", "w_gu")
SMALL = ("g_mix", "b_f", "conv_w", "conv_b", "ln_g", "ln_b", "g_x", "g_mem", "g_ffn", "g_final")


def _full_from_gathered(n, blk):
    _, rr, cc = blk.shape
    if n in COL_SHARDED:
        return jnp.concatenate([blk[k] for k in range(N_DEV)], axis=1)
    return blk.reshape(N_DEV * rr, cc)


def _shards_from_full(n, g):
    rr, cc = g.shape
    if n in COL_SHARDED:
        w = cc // N_DEV
        return jnp.stack([g[:, k * w:(k + 1) * w] for k in range(N_DEV)]).reshape(4, 2, rr, w)
    return g.reshape(4, 2, rr // N_DEV, cc)


def _small_layout():
    sizes = dict(g_mix=1024, b_f=8, conv_w=CONV_K * CONV_CH, conv_b=512, ln_g=512, ln_b=512, g_x=1024,
                 g_mem=1024, g_ffn=1024, g_final=1024, loss=1)
    lay, r0 = {}, 0
    for n, sz in sizes.items():
        r = -(-sz // LANES)
        lay[n] = (r0, r, sz)
        r0 += r
    return lay, -(-r0 // 8) * 8


def kernel(x, mem, g_mix, w_in, b_f, conv_w, conv_b, ln_g, ln_b, w_out, g_x, g_mem, w_mq, w_mkv, w_mo, g_ffn, w_gu, w_down, g_final, loss_target, m_g_mix, m_w_in, m_b_f, m_conv_w, m_conv_b, m_ln_g, m_ln_b, m_w_out, m_g_x, m_g_mem, m_w_mq, m_w_mkv, m_w_mo, m_g_ffn, m_w_gu, m_w_down, m_g_final, v_g_mix, v_w_in, v_b_f, v_conv_w, v_conv_b, v_ln_g, v_ln_b, v_w_out, v_g_x, v_g_mem, v_w_mq, v_w_mkv, v_w_mo, v_g_ffn, v_w_gu, v_w_down, v_g_final):
    names = ["g_mix", "w_in", "b_f", "conv_w", "conv_b", "ln_g", "ln_b", "w_out", "g_x", "g_mem", "w_mq",
             "w_mkv", "w_mo", "g_ffn", "w_gu", "w_down", "g_final"]
    W = dict(zip(names, [g_mix, w_in, b_f, conv_w, conv_b, ln_g, ln_b, w_out, g_x, g_mem, w_mq, w_mkv, w_mo,
                         g_ffn, w_gu, w_down, g_final]))
    Mo = dict(zip(names, [m_g_mix, m_w_in, m_b_f, m_conv_w, m_conv_b, m_ln_g, m_ln_b, m_w_out, m_g_x, m_g_mem,
                          m_w_mq, m_w_mkv, m_w_mo, m_g_ffn, m_w_gu, m_w_down, m_g_final]))
    Vo = dict(zip(names, [v_g_mix, v_w_in, v_b_f, v_conv_w, v_conv_b, v_ln_g, v_ln_b, v_w_out, v_g_x, v_g_mem,
                          v_w_mq, v_w_mkv, v_w_mo, v_g_ffn, v_w_gu, v_w_down, v_g_final]))
    dev = 4 * lax.axis_index("x") + 2 * lax.axis_index("y") + lax.axis_index("c")

    two = lambda a: a.reshape(-1, a.shape[-1])
    cw_shard = jnp.pad(two(conv_w), ((0, HALO - CONV_K), (0, 0)))
    w_in8, cw8 = all_gather([two(w_in).astype(BF16), cw_shard], name="ag_first")
    cw_full = cw8.transpose(1, 0, 2).reshape(HALO, -1)[:CONV_K]

    sp = dict(g_mix=g_mix, b_f=b_f, conv_w=cw_full, conv_b=conv_b, ln_g=ln_g, ln_b=ln_b, g_x=g_x, g_mem=g_mem,
              g_ffn=g_ffn, g_final=g_final)
    loss_blk, grad_x, gs, reduced = local_step(x, mem, loss_target, sp, _full_from_gathered("w_in", w_in8),
                                               [two(W[n]).astype(BF16) for n in LATE])

    lay, rs = _small_layout()
    small = {**{n: gs[n] for n in SMALL}, "loss": loss_blk[:, :1]}
    parts = []
    for n, (r0, r, sz) in lay.items():
        flat = small[n].reshape(-1).astype(F32)
        parts.append(jnp.pad(flat, (0, r * LANES - sz)).reshape(r, LANES))
    spack = jnp.concatenate(parts, axis=0)
    spack = jnp.pad(spack, ((0, rs - spack.shape[0]), (0, 0)))
    ssum = rows_sum(all_gather([spack], name="ag_small")[0], name="small_sum")
    gsmall = {n: ssum[r0:r0 + r].reshape(-1)[:sz] for n, (r0, r, sz) in lay.items()}
    loss = gsmall["loss"].reshape(())

    grads, delta, new_m, new_v = {}, {}, {}, {}
    for n in BIG:
        p, o = reduced[n]
        shp = W[n].shape
        g, d, mn, vn = chip_sum_adamw(p, o, two(W[n]), two(Mo[n]), two(Vo[n]), name="adamw_" + n)
        grads[n], delta[n], new_m[n], new_v[n] = g.reshape(shp), d.reshape(shp), mn.reshape(shp), vn.reshape(shp)
    for n in SMALL:
        if n == "conv_w":
            full = gsmall[n].reshape(CONV_K, CONV_CH)
            ncol = conv_w.shape[-1]
            grads[n] = lax.dynamic_slice(full, (0, dev * ncol), (CONV_K, ncol)).reshape(conv_w.shape)
        else:
            grads[n] = gsmall[n].reshape(W[n].shape)
    upd = adamw_small([(two(W[n]), two(grads[n]), two(Mo[n]), two(Vo[n])) for n in SMALL], name="adamw_small")
    for n, (d, mn, vn) in zip(SMALL, upd):
        shp = W[n].shape
        delta[n], new_m[n], new_v[n] = d.reshape(shp), mn.reshape(shp), vn.reshape(shp)
    return (loss, grad_x, *[grads[n] for n in names], *[delta[n] for n in names],
            *[new_m[n] for n in names], *[new_v[n] for n in names])
```

```python
import functools
import math

import jax
import jax.numpy as jnp
from jax import lax
from jax.experimental import pallas as pl
from jax.experimental.pallas import tpu as pltpu

F32 = jnp.float32
BF16 = jnp.bfloat16
EPS = 1e-6
N_DEV = 8
CONV_CH = 512
CONV_K = 31
FOX_HEADS = 8
FOX_HEAD_DIM = 64
FOX_W = 512
MEM_HEADS = 4
MEM_HEAD_DIM = 256
HALO = 32
LANES = 128
ADAM_LR, ADAM_B1, ADAM_B2, ADAM_EPS, ADAM_WD, ADAM_STEP = 0.001, 0.9, 0.999, 1e-08, 0.01, 10
NEG = -1e30
VMEM_CAP = 60 * 1024 * 1024
MESH = pl.DeviceIdType.MESH


def _call(body, **kw):
    kw["out_shape"] = jax.tree.map(lambda s: pltpu.HBM(s.shape, s.dtype), kw["out_shape"])
    call = pl.pallas_call(body, **kw)
    return lambda *args: call(*[pltpu.with_memory_space_constraint(a, pltpu.HBM) for a in args])


def _params(sem=None, vmem=None):
    kw = {}
    if sem is not None:
        kw["dimension_semantics"] = sem
    if vmem is not None:
        kw["vmem_limit_bytes"] = int(min(VMEM_CAP, vmem))
    return pltpu.CompilerParams(**kw)


def _nbytes(shape, dtype):
    return math.prod(shape) * jnp.dtype(dtype).itemsize


def _pick(n, target):
    best = None
    for d in range(LANES, min(n, target) + 1, LANES):
        if n % d == 0:
            best = d
    return n if best is None else best


def matmul(a, b, *, tb=False, out_dtype, res=None, tm=512, tn=512, tk=None, name, rider=None):
    M, K = a.shape
    N = b.shape[0] if tb else b.shape[1]
    assert (b.shape[1] if tb else b.shape[0]) == K
    tm, tn = _pick(M, tm), _pick(N, tn)
    tk = K if tk is None else _pick(K, tk)
    assert M % tm == 0 and N % tn == 0 and K % tk == 0, (name, M, N, K, tm, tn, tk)
    nk = K // tk
    dn = (((1,), (1 if tb else 0,)), ((), ()))

    def body(*refs):
        if res is not None:
            a_ref, b_ref, r_ref, o_ref = refs[:4]
        else:
            a_ref, b_ref, o_ref = refs[:3]
        p = lax.dot_general(a_ref[...].astype(BF16), b_ref[...].astype(BF16), dn,
                            preferred_element_type=F32)

        def finish(acc):
            if res is not None:
                acc = acc + r_ref[...].astype(F32)
            o_ref[...] = acc.astype(out_dtype)

        if nk == 1:
            finish(p)
        else:
            acc_ref = refs[-1]
            k = pl.program_id(2)

            @pl.when(k == 0)
            def _():
                acc_ref[...] = p

            @pl.when(k > 0)
            def _():
                acc_ref[...] += p

            @pl.when(k == nk - 1)
            def _():
                finish(acc_ref[...])

    a_spec = pl.BlockSpec((tm, tk), lambda i, j, k: (i, k))
    b_spec = pl.BlockSpec((tn, tk), lambda i, j, k: (j, k)) if tb else pl.BlockSpec((tk, tn), lambda i, j, k: (k, j))
    o_spec = pl.BlockSpec((tm, tn), lambda i, j, k: (i, j))
    in_specs, args = [a_spec, b_spec], [a, b]
    est = 2 * (_nbytes((tm, tk), a.dtype) + _nbytes((tk, tn), b.dtype) + _nbytes((tm, tn), out_dtype))
    est += (a.dtype != BF16) * _nbytes((tm, tk), BF16) + (b.dtype != BF16) * _nbytes((tk, tn), BF16)
    est += 2 * _nbytes((tm, tn), F32)
    if res is not None:
        in_specs.append(o_spec)
        args.append(res)
        est += 2 * _nbytes((tm, tn), res.dtype)
    (out,), rode = hosted_call(
        body, rider, name=name, grid=(M // tm, N // tn, nk),
        in_specs=in_specs, out_specs=[o_spec],
        out_shape=[jax.ShapeDtypeStruct((M, N), out_dtype)],
        scratch_shapes=[] if nk == 1 else [pltpu.VMEM((tm, tn), F32)],
        args=args, vmem=est + (8 << 20),
    )
    return out if rider is None else (out, rode)


def _rms_scale(x):
    return lax.rsqrt(jnp.mean(x * x, axis=-1, keepdims=True) + EPS)


def rmsnorm_fwd(x, g, *, name, tm=512):
    T, D = x.shape
    tm = min(tm, T)

    def body(x_ref, g_ref, o_ref, ot_ref):
        xv = x_ref[...]
        h = xv * _rms_scale(xv) * g_ref[...]
        o_ref[...] = h.astype(BF16)
        ot_ref[...] = h.T.astype(BF16)

    return _call(
        body, name=name, grid=(T // tm,),
        in_specs=[pl.BlockSpec((tm, D), lambda i: (i, 0)), pl.BlockSpec((1, D), lambda i: (0, 0))],
        out_specs=[pl.BlockSpec((tm, D), lambda i: (i, 0)), pl.BlockSpec((D, tm), lambda i: (0, i))],
        out_shape=[jax.ShapeDtypeStruct((T, D), BF16), jax.ShapeDtypeStruct((D, T), BF16)],
        compiler_params=_params(("parallel",)),
    )(x, g)


def _rms_bwd_math(xv, gv, dh):
    r = _rms_scale(xv)
    xh = xv * r
    dg = jnp.sum(dh * xh, axis=0, keepdims=True)
    dxh = dh * gv
    dx = r * (dxh - xh * jnp.mean(dxh * xh, axis=-1, keepdims=True))
    return dx, dg


def rmsnorm_bwd(x, g, dh, dres, *, name, tm=256):
    T, D = x.shape
    tm = min(tm, T)

    def body(*refs):
        if dres is not None:
            x_ref, g_ref, dh_ref, dr_ref, dx_ref, dg_ref = refs
        else:
            x_ref, g_ref, dh_ref, dx_ref, dg_ref = refs
        dx, dg = _rms_bwd_math(x_ref[...], g_ref[...], dh_ref[...].astype(F32))
        if dres is not None:
            dx = dx + dr_ref[...]
        dx_ref[...] = dx

        @pl.when(pl.program_id(0) == 0)
        def _():
            dg_ref[...] = jnp.zeros_like(dg_ref)

        dg_ref[...] += dg

    row = pl.BlockSpec((tm, D), lambda i: (i, 0))
    vec = pl.BlockSpec((1, D), lambda i: (0, 0))
    ins, args = [row, vec, row], [x, g, dh]
    if dres is not None:
        ins.append(row)
        args.append(dres)
    return _call(
        body, name=name, grid=(T // tm,), in_specs=ins, out_specs=[row, vec],
        out_shape=[jax.ShapeDtypeStruct((T, D), F32), jax.ShapeDtypeStruct((1, D), F32)],
        compiler_params=_params(("arbitrary",)),
    )(*args)


def final_loss_bwd(x, g, target, *, name, tm=256):
    T, D = x.shape
    tm = min(tm, T)

    def body(x_ref, g_ref, t_ref, dx_ref, dg_ref, l_ref):
        xv, gv = x_ref[...], g_ref[...]
        e = xv * _rms_scale(xv) * gv - t_ref[...]
        part = 0.5 * jnp.sum(jnp.mean(e * e, axis=-1, keepdims=True), axis=0, keepdims=True)
        dx, dg = _rms_bwd_math(xv, gv, e * (1.0 / D))
        dx_ref[...] = dx

        @pl.when(pl.program_id(0) == 0)
        def _():
            dg_ref[...] = jnp.zeros_like(dg_ref)
            l_ref[...] = jnp.zeros_like(l_ref)

        dg_ref[...] += dg
        l_ref[...] += jnp.broadcast_to(part, l_ref.shape)

    row = pl.BlockSpec((tm, D), lambda i: (i, 0))
    vec = pl.BlockSpec((1, D), lambda i: (0, 0))
    return _call(
        body, name=name, grid=(T // tm,), in_specs=[row, vec, row],
        out_specs=[row, vec, pl.BlockSpec((1, LANES), lambda i: (0, 0))],
        out_shape=[jax.ShapeDtypeStruct((T, D), F32), jax.ShapeDtypeStruct((1, D), F32),
                   jax.ShapeDtypeStruct((1, LANES), F32)],
        compiler_params=_params(("arbitrary",)),
    )(x, g, target)


def _sigmoid(v):
    return 1.0 / (1.0 + jnp.exp(-v))


def _glu(blk):
    u = blk[:, :CONV_CH].astype(F32)
    gt = blk[:, CONV_CH:].astype(F32)
    return u * _sigmoid(gt)


def _fill_causal_ext(ext, cur_ref, halo_ref, s, ts):
    ext[pl.ds(HALO, ts), :] = _glu(cur_ref[0])
    hal = _glu(halo_ref[0])
    ext[pl.ds(0, HALO), :] = jnp.where(s > 0, hal, 0.0)


SUBLANES = 8


def _make_shifted(ext, sh):
    n = ext.shape[0]
    full = ext[...]
    for r in range(1, SUBLANES):
        sh[r - 1] = pltpu.roll(full, n - r, 0)


def _tap(ext, sh, off, ts):
    r = off % SUBLANES
    return ext[pl.ds(off, ts), :] if r == 0 else sh[r - 1, pl.ds(off - r, ts), :]


def _causal_conv(ext, sh, w_ref, ts):
    acc = jnp.zeros((ts, CONV_CH), F32)
    for j in range(CONV_K):
        acc = acc + _tap(ext, sh, HALO - (CONV_K - 1) + j, ts) * w_ref[pl.ds(j, 1), :]
    return acc


def _ln_stats(y):
    mu = jnp.mean(y, axis=-1, keepdims=True)
    yc = y - mu
    rstd = lax.rsqrt(jnp.mean(yc * yc, axis=-1, keepdims=True) + EPS)
    return yc * rstd, rstd


def _conv_specs(ts, S):
    nh = ts // HALO
    cur = pl.BlockSpec((1, ts, 2 * CONV_CH), lambda b, s: (b, s, 0))
    halo = pl.BlockSpec((1, HALO, 2 * CONV_CH), lambda b, s: (b, jnp.maximum(s * nh - 1, 0), 0))
    w = pl.BlockSpec((HALO, CONV_CH), lambda b, s: (0, 0))
    vec = pl.BlockSpec((1, CONV_CH), lambda b, s: (0, 0))
    return cur, halo, w, vec


def conv_branch_fwd(ug, conv_w, conv_b, ln_g, ln_b, *, name, ts=256, rider=None):
    B, S, _ = ug.shape
    ts = min(ts, S)
    ns = S // ts
    cur, halo, w, vec = _conv_specs(ts, S)

    def body(cur_ref, halo_ref, w_ref, cb_ref, lg_ref, lb_ref, o_ref, ot_ref, ext, sh):
        _fill_causal_ext(ext, cur_ref, halo_ref, pl.program_id(1), ts)
        _make_shifted(ext, sh)
        y = _causal_conv(ext, sh, w_ref, ts) + cb_ref[...]
        yh, _ = _ln_stats(y)
        ln = yh * lg_ref[...] + lb_ref[...]
        out = ln * _sigmoid(ln)
        o_ref[0] = out.astype(BF16)
        ot_ref[...] = out.T.astype(BF16)

    return hosted_call(
        body, rider, name=name, grid=(B, ns), in_specs=[cur, halo, w, vec, vec, vec],
        out_specs=[pl.BlockSpec((1, ts, CONV_CH), lambda b, s: (b, s, 0)),
                   pl.BlockSpec((CONV_CH, ts), lambda b, s: (0, b * ns + s))],
        out_shape=[jax.ShapeDtypeStruct((B, S, CONV_CH), BF16), jax.ShapeDtypeStruct((CONV_CH, B * S), BF16)],
        scratch_shapes=[pltpu.VMEM((ts + HALO, CONV_CH), F32),
                        pltpu.VMEM((SUBLANES - 1, ts + HALO, CONV_CH), F32)],
        args=(ug, ug, conv_w, conv_b, ln_g, ln_b),
    )


def conv_branch_bwd_a(ug, dcat, conv_w, conv_b, ln_g, ln_b, *, name, ts=256):
    B, S, _ = ug.shape
    ts = min(ts, S)
    cur, halo, w, vec = _conv_specs(ts, S)

    def body(cur_ref, halo_ref, d_ref, w_ref, cb_ref, lg_ref, lb_ref, dy_ref, dw_ref, dv_ref, ext, sh):
        _fill_causal_ext(ext, cur_ref, halo_ref, pl.program_id(1), ts)
        _make_shifted(ext, sh)
        y = _causal_conv(ext, sh, w_ref, ts) + cb_ref[...]
        yh, rstd = _ln_stats(y)
        lg = lg_ref[...]
        ln = yh * lg + lb_ref[...]
        sg = _sigmoid(ln)
        dln = d_ref[0].astype(F32) * (sg * (1.0 + ln * (1.0 - sg)))
        dyh = dln * lg
        dy = rstd * (dyh - jnp.mean(dyh, axis=-1, keepdims=True)
                     - yh * jnp.mean(dyh * yh, axis=-1, keepdims=True))
        dy_ref[0] = dy

        @pl.when((pl.program_id(0) == 0) & (pl.program_id(1) == 0))
        def _():
            dw_ref[...] = jnp.zeros_like(dw_ref)
            dv_ref[...] = jnp.zeros_like(dv_ref)

        dv_ref[pl.ds(0, 1), :] += jnp.sum(dy, axis=0, keepdims=True)
        dv_ref[pl.ds(1, 1), :] += jnp.sum(dln * yh, axis=0, keepdims=True)
        dv_ref[pl.ds(2, 1), :] += jnp.sum(dln, axis=0, keepdims=True)
        for j in range(CONV_K):
            tap = _tap(ext, sh, HALO - (CONV_K - 1) + j, ts)
            dw_ref[pl.ds(j, 1), :] += jnp.sum(dy * tap, axis=0, keepdims=True)

    return _call(
        body, name=name, grid=(B, S // ts),
        in_specs=[cur, halo, pl.BlockSpec((1, ts, CONV_CH), lambda b, s: (b, s, 0)), w, vec, vec, vec],
        out_specs=[pl.BlockSpec((1, ts, CONV_CH), lambda b, s: (b, s, 0)),
                   pl.BlockSpec((HALO, CONV_CH), lambda b, s: (0, 0)),
                   pl.BlockSpec((8, CONV_CH), lambda b, s: (0, 0))],
        out_shape=[jax.ShapeDtypeStruct((B, S, CONV_CH), F32),
                   jax.ShapeDtypeStruct((HALO, CONV_CH), F32),
                   jax.ShapeDtypeStruct((8, CONV_CH), F32)],
        scratch_shapes=[pltpu.VMEM((ts + HALO, CONV_CH), F32),
                        pltpu.VMEM((SUBLANES - 1, ts + HALO, CONV_CH), F32)],
        compiler_params=_params(("arbitrary", "arbitrary")),
    )(ug, ug, dcat, conv_w, conv_b, ln_g, ln_b)


def conv_branch_bwd_b(ug, dy, conv_w, *, name, ts=256):
    B, S, _ = ug.shape
    ts = min(ts, S)
    nh, n_halo = ts // HALO, S // HALO

    def body(cur_ref, dy_ref, nxt_ref, w_ref, o_ref, ext, sh):
        last = pl.program_id(1) == pl.num_programs(1) - 1
        ext[pl.ds(0, ts), :] = dy_ref[0]
        ext[pl.ds(ts, HALO), :] = jnp.where(last, 0.0, nxt_ref[0])
        _make_shifted(ext, sh)
        da = jnp.zeros((ts, CONV_CH), F32)
        for j in range(CONV_K):
            da = da + _tap(ext, sh, CONV_K - 1 - j, ts) * w_ref[pl.ds(j, 1), :]
        blk = cur_ref[0]
        u = blk[:, :CONV_CH].astype(F32)
        sg = _sigmoid(blk[:, CONV_CH:].astype(F32))
        o_ref[0, :, :CONV_CH] = (da * sg).astype(BF16)
        o_ref[0, :, CONV_CH:] = (da * u * sg * (1.0 - sg)).astype(BF16)

    return _call(
        body, name=name, grid=(B, S // ts),
        in_specs=[pl.BlockSpec((1, ts, 2 * CONV_CH), lambda b, s: (b, s, 0)),
                  pl.BlockSpec((1, ts, CONV_CH), lambda b, s: (b, s, 0)),
                  pl.BlockSpec((1, HALO, CONV_CH), lambda b, s: (b, jnp.minimum((s + 1) * nh, n_halo - 1), 0)),
                  pl.BlockSpec((HALO, CONV_CH), lambda b, s: (0, 0))],
        out_specs=pl.BlockSpec((1, ts, 2 * CONV_CH), lambda b, s: (b, s, 0)),
        out_shape=jax.ShapeDtypeStruct((B, S, 2 * CONV_CH), BF16),
        scratch_shapes=[pltpu.VMEM((ts + HALO, CONV_CH), F32),
                        pltpu.VMEM((SUBLANES - 1, ts + HALO, CONV_CH), F32)],
        compiler_params=_params(("parallel", "parallel")),
    )(ug, dy, dy, conv_w)


def _tri(n, lower):
    r = lax.broadcasted_iota(jnp.int32, (n, n), 0)
    c = lax.broadcasted_iota(jnp.int32, (n, n), 1)
    return ((r >= c) if lower else (r <= c)).astype(F32)


def _eye(n):
    r = lax.broadcasted_iota(jnp.int32, (n, n), 0)
    c = lax.broadcasted_iota(jnp.int32, (n, n), 1)
    return (r == c).astype(F32)


def _dot_hi(a, b, dn):
    return lax.dot_general(a, b, dn, precision=lax.Precision.HIGHEST, preferred_element_type=F32)


NN = (((1,), (0,)), ((), ()))
NT = (((1,), (1,)), ((), ()))
TN = (((0,), (0,)), ((), ()))


def _log_sigmoid(v):
    e = jnp.exp(-jnp.abs(v))
    log1p_e = jnp.where(e < 1e-3, e * (1.0 - 0.5 * e), jnp.log(1.0 + e))
    return jnp.minimum(v, 0.0) - log1p_e


def fgate_fwd(h, w_f, b_f, *, name, ts=256, rider=None):
    B, S, D = h.shape
    ts = min(ts, S)

    def body(h_ref, w_ref, b_ref, f_ref, cc_ref, cr_ref, carry):
        @pl.when(pl.program_id(1) == 0)
        def _():
            carry[...] = jnp.zeros_like(carry)

        f = jnp.dot(h_ref[0], w_ref[...], preferred_element_type=F32)
        f_ref[0] = f
        logf = _log_sigmoid(f + b_ref[...])
        c = _dot_hi(_tri(ts, True), logf, NN) + carry[pl.ds(0, 1), :]
        cc_ref[0] = c
        carry[pl.ds(0, 1), :] = c[ts - 1:ts, :]
        cr_ref[0] = _dot_hi(_eye(LANES), c, NT)

    return hosted_call(
        body, rider, name=name, grid=(B, S // ts),
        in_specs=[pl.BlockSpec((1, ts, D), lambda b, s: (b, s, 0)),
                  pl.BlockSpec((D, LANES), lambda b, s: (0, 0)),
                  pl.BlockSpec((1, LANES), lambda b, s: (0, 0))],
        out_specs=[pl.BlockSpec((1, ts, LANES), lambda b, s: (b, s, 0)),
                   pl.BlockSpec((1, ts, LANES), lambda b, s: (b, s, 0)),
                   pl.BlockSpec((1, LANES, ts), lambda b, s: (b, 0, s))],
        out_shape=[jax.ShapeDtypeStruct((B, S, LANES), F32), jax.ShapeDtypeStruct((B, S, LANES), F32),
                   jax.ShapeDtypeStruct((B, LANES, S), F32)],
        scratch_shapes=[pltpu.VMEM((8, LANES), F32)],
        args=(h, w_f, b_f),
    )


def fgate_bwd(dc, f, b_f, *, name, ts=256):
    B, S, _ = f.shape
    P = dc.shape[1]
    ts = min(ts, S)
    ns = S // ts

    def body(dc_ref, f_ref, b_ref, df_ref, db_ref, carry):
        @pl.when(pl.program_id(1) == 0)
        def _():
            carry[...] = jnp.zeros_like(carry)

        @pl.when((pl.program_id(0) == 0) & (pl.program_id(1) == 0))
        def _():
            db_ref[...] = jnp.zeros_like(db_ref)

        dc_t = dc_ref[0, 0]
        for j in range(1, P):
            dc_t = dc_t + dc_ref[0, j]
        dlogf = _dot_hi(_tri(ts, False), dc_t, NN) + carry[pl.ds(0, 1), :]
        carry[pl.ds(0, 1), :] = dlogf[0:1, :]
        df = dlogf * _sigmoid(-(f_ref[0] + b_ref[...]))
        df_ref[0] = df.astype(BF16)
        db_ref[...] += jnp.sum(df, axis=0, keepdims=True)

    return _call(
        body, name=name, grid=(B, ns),
        in_specs=[pl.BlockSpec((1, P, ts, LANES), lambda b, s: (b, 0, ns - 1 - s, 0)),
                  pl.BlockSpec((1, ts, LANES), lambda b, s: (b, ns - 1 - s, 0)),
                  pl.BlockSpec((1, LANES), lambda b, s: (0, 0))],
        out_specs=[pl.BlockSpec((1, ts, LANES), lambda b, s: (b, ns - 1 - s, 0)),
                   pl.BlockSpec((1, LANES), lambda b, s: (0, 0))],
        out_shape=[jax.ShapeDtypeStruct((B, S, LANES), BF16), jax.ShapeDtypeStruct((1, LANES), F32)],
        scratch_shapes=[pltpu.VMEM((8, LANES), F32)],
        compiler_params=_params(("arbitrary", "arbitrary")),
    )(dc, f, b_f)


def _lane_pick(tile, idx):
    lane = lax.broadcasted_iota(jnp.int32, tile.shape, 1)
    return jnp.sum(jnp.where(lane == idx, tile, 0.0), axis=-1, keepdims=True)


FOX_T = 512


def _fox_heads(q, cc_ref, p):
    lane = lax.broadcasted_iota(jnp.int32, q.shape, 1)
    qs = q * (1.0 / math.sqrt(FOX_HEAD_DIM))
    qhs = [jnp.where((lane < FOX_HEAD_DIM) == (hh == 0), qs, jnp.zeros_like(qs)) for hh in range(2)]
    crefs = [_lane_pick(cc_ref[0, pl.ds(0, 1), :], 2 * p + hh) for hh in range(2)]
    return qhs, crefs


def _fold_lanes(x, op):
    out = x[:, :LANES]
    for j in range(1, x.shape[1] // LANES):
        out = op(out, x[:, j * LANES:(j + 1) * LANES])
    return out


def _causal(t, transposed):
    r = lax.broadcasted_iota(jnp.int32, (t, t), 0)
    c = lax.broadcasted_iota(jnp.int32, (t, t), 1)
    return (r <= c) if transposed else (c <= r)


QKV0 = 8


def fox_fwd(z, c_col, c_row, *, name, rider=None):
    B, S, _ = z.shape
    assert S % FOX_T == 0
    tq, nq = FOX_T, S // FOX_T
    npair = FOX_HEADS // 2

    def body(q_ref, k_ref, v_ref, cc_ref, cr_ref, o_ref, l_ref, ot_ref, s_scr, m_scr, acc_scr):
        p, qi = pl.program_id(1), pl.program_id(2)
        qhs, crefs = _fox_heads(q_ref[0], cc_ref, p)
        lane = lax.broadcasted_iota(jnp.int32, (tq, LANES), 1)
        first = lane < FOX_HEAD_DIM
        for hh in range(2):
            m_scr[hh] = jnp.full((tq, LANES), NEG, F32)
            acc_scr[hh] = jnp.zeros((tq, LANES), F32)

        def logits(kb, diagonal):
            k0 = pl.multiple_of(kb * tq, tq)
            k = k_ref[0, pl.ds(k0, tq), :]
            for hh in range(2):
                s = lax.dot_general(qhs[hh], k, NT, preferred_element_type=F32)
                s = s + (crefs[hh] - cr_ref[0, pl.ds(2 * p + hh, 1), pl.ds(k0, tq)])
                if diagonal:
                    s = jnp.where(_causal(tq, False), s, NEG)
                s_scr[hh, kb] = s
                m_scr[hh] = jnp.maximum(m_scr[hh], _fold_lanes(s, jnp.maximum))

        def sweep1(kb, carry):
            logits(kb, False)
            return carry

        lax.fori_loop(0, qi, sweep1, 0)
        logits(qi, True)
        ms = [jnp.max(m_scr[hh], axis=-1, keepdims=True) for hh in range(2)]
        mbs = [jnp.broadcast_to(ms[hh], (tq, tq)) for hh in range(2)]

        for hh in range(2):
            m_scr[hh] = jnp.zeros((tq, LANES), F32)

        def weigh(kb, carry):
            k0 = pl.multiple_of(kb * tq, tq)
            v = v_ref[0, pl.ds(k0, tq), :]
            for hh in range(2):
                pr = jnp.exp(s_scr[hh, kb] - mbs[hh])
                m_scr[hh] += _fold_lanes(pr, jnp.add)
                acc_scr[hh] += jnp.dot(pr.astype(BF16), v, preferred_element_type=F32)
            return carry

        lax.fori_loop(0, qi + 1, weigh, 0)
        accs = [acc_scr[hh] for hh in range(2)]
        ls = [jnp.sum(m_scr[hh], axis=-1, keepdims=True) for hh in range(2)]
        out = jnp.where(first, accs[0] / ls[0], accs[1] / ls[1])
        o_ref[0] = out.astype(BF16)
        ot_ref[...] = out.T.astype(BF16)
        l_ref[0, 0] = jnp.where(first, ms[0] + jnp.log(ls[0]), ms[1] + jnp.log(ls[1]))

    return hosted_call(
        body, rider, name=name, grid=(B, npair, nq),
        in_specs=[pl.BlockSpec((1, tq, LANES), lambda b, p, i: (b, i, QKV0 + p)),
                  pl.BlockSpec((1, S, LANES), lambda b, p, i: (b, 0, QKV0 + npair + p)),
                  pl.BlockSpec((1, S, LANES), lambda b, p, i: (b, 0, QKV0 + 2 * npair + p)),
                  pl.BlockSpec((1, tq, LANES), lambda b, p, i: (b, i, 0)),
                  pl.BlockSpec((1, 8, S), lambda b, p, i: (b, 0, 0))],
        out_specs=[pl.BlockSpec((1, tq, LANES), lambda b, p, i: (b, i, p)),
                   pl.BlockSpec((1, 1, tq, LANES), lambda b, p, i: (b, p, i, 0)),
                   pl.BlockSpec((LANES, tq), lambda b, p, i: (p, b * nq + i))],
        out_shape=[jax.ShapeDtypeStruct((B, S, FOX_W), BF16),
                   jax.ShapeDtypeStruct((B, npair, S, LANES), F32),
                   jax.ShapeDtypeStruct((FOX_W, B * S), BF16)],
        scratch_shapes=[pltpu.VMEM((2, nq, tq, tq), F32), pltpu.VMEM((2, tq, LANES), F32),
                        pltpu.VMEM((2, tq, LANES), F32)],
        args=(z, z, z, c_col, c_row),
    )


def fox_bwd_dq(z, dcat, lse, c_col, c_row, *, name, rider=None):
    B, S, _ = z.shape
    tq, nq = FOX_T, S // FOX_T
    npair = FOX_HEADS // 2

    def body(q_ref, k_ref, v_ref, do_ref, l_ref, cc_ref, cr_ref, dq_ref, st_ref, p_scr, dp_scr, dl_scr):
        p, qi = pl.program_id(1), pl.program_id(2)
        qhs, crefs = _fox_heads(q_ref[0], cc_ref, p)
        lane = lax.broadcasted_iota(jnp.int32, (tq, LANES), 1)
        do_b = do_ref[0].astype(BF16)
        dohs = [jnp.where((lane < FOX_HEAD_DIM) == (hh == 0), do_b, jnp.zeros_like(do_b)) for hh in range(2)]
        lses = [_lane_pick(l_ref[0, 0], hh * FOX_HEAD_DIM) for hh in range(2)]
        lbs = [jnp.broadcast_to(lses[hh], (tq, tq)) for hh in range(2)]
        for hh in range(2):
            dl_scr[hh] = jnp.zeros((tq, LANES), F32)

        def probs(kb, diagonal):
            k0 = pl.multiple_of(kb * tq, tq)
            k = k_ref[0, pl.ds(k0, tq), :]
            v = v_ref[0, pl.ds(k0, tq), :]
            for hh in range(2):
                s = lax.dot_general(qhs[hh], k, NT, preferred_element_type=F32)
                s = s + (crefs[hh] - cr_ref[0, pl.ds(2 * p + hh, 1), pl.ds(k0, tq)])
                pr = jnp.exp(s - lbs[hh])
                if diagonal:
                    pr = jnp.where(_causal(tq, False), pr, 0.0)
                dp = lax.dot_general(dohs[hh], v, NT, preferred_element_type=F32)
                pdp = pr * dp
                dl_scr[hh] += _fold_lanes(pdp, jnp.add)
                p_scr[hh, kb] = pr
                dp_scr[hh, kb] = dp

        def first_pass(kb, carry):
            probs(kb, False)
            return carry

        lax.fori_loop(0, qi, first_pass, 0)
        probs(qi, True)

        dls = [jnp.sum(dl_scr[hh], axis=-1, keepdims=True) for hh in range(2)]
        dlbs = [jnp.broadcast_to(dls[hh], (tq, tq)) for hh in range(2)]

        def second_pass(kb, dq):
            k0 = pl.multiple_of(kb * tq, tq)
            k = k_ref[0, pl.ds(k0, tq), :]
            for hh in range(2):
                ds = p_scr[hh, kb] * (dp_scr[hh, kb] - dlbs[hh])
                kh = jnp.where((lane < FOX_HEAD_DIM) == (hh == 0), k, jnp.zeros_like(k))
                dq = dq + jnp.dot(ds.astype(BF16), kh, preferred_element_type=F32)
            return dq

        dq = lax.fori_loop(0, qi + 1, second_pass, jnp.zeros((tq, LANES), F32))
        dq_ref[0] = (dq * (1.0 / math.sqrt(FOX_HEAD_DIM))).astype(BF16)
        cols = jnp.zeros((tq, LANES), F32)
        for j, col in enumerate([crefs[0] - lses[0], crefs[1] - lses[1], dls[0], dls[1]]):
            cols = jnp.where(lane == j, col, cols)
        st_ref[0, 0] = _dot_hi(_eye(LANES), cols, NT)[:8]

    return hosted_call(
        body, rider, name=name, grid=(B, npair, nq),
        in_specs=[pl.BlockSpec((1, tq, LANES), lambda b, p, i: (b, i, QKV0 + p)),
                  pl.BlockSpec((1, S, LANES), lambda b, p, i: (b, 0, QKV0 + npair + p)),
                  pl.BlockSpec((1, S, LANES), lambda b, p, i: (b, 0, QKV0 + 2 * npair + p)),
                  pl.BlockSpec((1, tq, LANES), lambda b, p, i: (b, i, npair + p)),
                  pl.BlockSpec((1, 1, tq, LANES), lambda b, p, i: (b, p, i, 0)),
                  pl.BlockSpec((1, tq, LANES), lambda b, p, i: (b, i, 0)),
                  pl.BlockSpec((1, 8, S), lambda b, p, i: (b, 0, 0))],
        out_specs=[pl.BlockSpec((1, tq, LANES), lambda b, p, i: (b, i, p)),
                   pl.BlockSpec((1, 1, 8, tq), lambda b, p, i: (b, p, 0, i))],
        out_shape=[jax.ShapeDtypeStruct((B, S, FOX_W), BF16), jax.ShapeDtypeStruct((B, npair, 8, S), F32)],
        scratch_shapes=[pltpu.VMEM((2, nq, tq, tq), F32), pltpu.VMEM((2, nq, tq, tq), F32),
                        pltpu.VMEM((2, tq, LANES), F32)],
        args=(z, z, z, dcat, lse, c_col, c_row), vmem=56 << 20,
    )


def fox_bwd_dkdv(z, dcat, stats, c_col, *, name, rider=None):
    B, S, _ = z.shape
    tk, nq = FOX_T, S // FOX_T
    npair = FOX_HEADS // 2
    inv = 1.0 / math.sqrt(FOX_HEAD_DIM)

    def body(q_ref, k_ref, v_ref, do_ref, st_ref, cc_ref, dk_ref, dv_ref, dc_ref, dk_scr, dv_scr, dc_scr):
        p, kt = pl.program_id(1), pl.program_id(2)
        lane = lax.broadcasted_iota(jnp.int32, (tk, LANES), 1)
        masks = [(lane < FOX_HEAD_DIM) == (hh == 0) for hh in range(2)]
        k = k_ref[0]
        v = v_ref[0]
        khs = [jnp.where(masks[hh], k, jnp.zeros_like(k)) for hh in range(2)]
        vhs = [jnp.where(masks[hh], v, jnp.zeros_like(v)) for hh in range(2)]
        ccbs = [jnp.broadcast_to(_lane_pick(cc_ref[0], 2 * p + hh), (tk, tk)) for hh in range(2)]
        dk_scr[...] = jnp.zeros_like(dk_scr)
        dv_scr[...] = jnp.zeros_like(dv_scr)
        dc_scr[...] = jnp.zeros_like(dc_scr)

        def tile(qb, diagonal):
            q0 = pl.multiple_of(qb * tk, tk)
            qs = q_ref[0, pl.ds(q0, tk), :] * inv
            do_b = do_ref[0, pl.ds(q0, tk), :].astype(BF16)
            for hh in range(2):
                st = lax.dot_general(khs[hh], qs, NT, preferred_element_type=F32)
                pr = jnp.exp(st - ccbs[hh] + st_ref[0, 0, pl.ds(hh, 1), pl.ds(q0, tk)])
                if diagonal:
                    pr = jnp.where(_causal(tk, True), pr, 0.0)
                dp = lax.dot_general(vhs[hh], do_b, NT, preferred_element_type=F32)
                ds = pr * (dp - st_ref[0, 0, pl.ds(2 + hh, 1), pl.ds(q0, tk)])
                dv_scr[...] += jnp.dot(pr.astype(BF16), jnp.where(masks[hh], do_b, jnp.zeros_like(do_b)),
                                       preferred_element_type=F32)
                dk_scr[...] += jnp.dot(ds.astype(BF16), jnp.where(masks[hh], qs, jnp.zeros_like(qs)),
                                       preferred_element_type=F32)
                dc_scr[hh] -= _fold_lanes(ds, jnp.add)

        def later(qb, carry):
            tile(qb, False)
            return carry

        tile(kt, True)
        lax.fori_loop(kt + 1, nq, later, 0)
        dk_ref[0] = dk_scr[...].astype(BF16)
        dv_ref[0] = dv_scr[...].astype(BF16)
        dcs = [jnp.sum(dc_scr[hh], axis=-1, keepdims=True) for hh in range(2)]
        dc_ref[0, 0] = jnp.where(lane == 2 * p, dcs[0], jnp.where(lane == 2 * p + 1, dcs[1], 0.0))

    full = lambda col: pl.BlockSpec((1, S, LANES), col)
    tile_spec = lambda col: pl.BlockSpec((1, tk, LANES), col)
    return hosted_call(
        body, rider, name=name, grid=(B, npair, nq),
        in_specs=[full(lambda b, p, t: (b, 0, QKV0 + p)),
                  tile_spec(lambda b, p, t: (b, t, QKV0 + npair + p)),
                  tile_spec(lambda b, p, t: (b, t, QKV0 + 2 * npair + p)),
                  full(lambda b, p, t: (b, 0, npair + p)),
                  pl.BlockSpec((1, 1, 8, S), lambda b, p, t: (b, p, 0, 0)),
                  tile_spec(lambda b, p, t: (b, t, 0))],
        out_specs=[tile_spec(lambda b, p, t: (b, t, p)), tile_spec(lambda b, p, t: (b, t, p)),
                   pl.BlockSpec((1, 1, tk, LANES), lambda b, p, t: (b, p, t, 0))],
        out_shape=[jax.ShapeDtypeStruct((B, S, FOX_W), BF16)] * 2
        + [jax.ShapeDtypeStruct((B, npair, S, LANES), F32)],
        scratch_shapes=[pltpu.VMEM((tk, LANES), F32), pltpu.VMEM((tk, LANES), F32),
                        pltpu.VMEM((2, tk, LANES), F32)],
        args=(z, z, z, dcat, stats, c_col),
    )


def xattn_fwd(qm, kv, *, name, tq=256):
    B, S, D = qm.shape
    M = kv.shape[1]
    tq = min(tq, S)
    inv = 1.0 / math.sqrt(MEM_HEAD_DIM)

    nq = S // tq

    def body(q_ref, kv_ref, o_ref, ot_ref):
        for h in range(MEM_HEADS):
            c0 = h * MEM_HEAD_DIM
            qh = q_ref[0, :, c0:c0 + MEM_HEAD_DIM]
            kh = kv_ref[0, :, c0:c0 + MEM_HEAD_DIM]
            vh = kv_ref[0, :, D + c0:D + c0 + MEM_HEAD_DIM]
            s = lax.dot_general(qh, kh, NT, preferred_element_type=F32) * inv
            e = jnp.exp(s - jnp.max(s, axis=-1, keepdims=True))
            o = jnp.dot(e.astype(BF16), vh, preferred_element_type=F32) / jnp.sum(e, axis=-1, keepdims=True)
            o_ref[0, :, c0:c0 + MEM_HEAD_DIM] = o.astype(BF16)
            ot_ref[c0:c0 + MEM_HEAD_DIM, :] = o.T.astype(BF16)

    return _call(
        body, name=name, grid=(B, nq),
        in_specs=[pl.BlockSpec((1, tq, D), lambda b, i: (b, i, 0)),
                  pl.BlockSpec((1, M, 2 * D), lambda b, i: (b, 0, 0))],
        out_specs=[pl.BlockSpec((1, tq, D), lambda b, i: (b, i, 0)),
                   pl.BlockSpec((D, tq), lambda b, i: (0, b * nq + i))],
        out_shape=[jax.ShapeDtypeStruct((B, S, D), BF16), jax.ShapeDtypeStruct((D, B * S), BF16)],
        compiler_params=_params(("parallel", "parallel")),
    )(qm, kv)


def xattn_bwd(qm, kv, do, *, name, tq=256):
    B, S, D = qm.shape
    M = kv.shape[1]
    tq = min(tq, S)
    inv = 1.0 / math.sqrt(MEM_HEAD_DIM)

    def body(q_ref, kv_ref, do_ref, dq_ref, dkv_ref):
        @pl.when(pl.program_id(1) == 0)
        def _():
            dkv_ref[...] = jnp.zeros_like(dkv_ref)

        for h in range(MEM_HEADS):
            c0 = h * MEM_HEAD_DIM
            qh = q_ref[0, :, c0:c0 + MEM_HEAD_DIM]
            kh = kv_ref[0, :, c0:c0 + MEM_HEAD_DIM]
            vh = kv_ref[0, :, D + c0:D + c0 + MEM_HEAD_DIM]
            doh = do_ref[0, :, c0:c0 + MEM_HEAD_DIM]
            s = lax.dot_general(qh, kh, NT, preferred_element_type=F32) * inv
            e = jnp.exp(s - jnp.max(s, axis=-1, keepdims=True))
            pr = e / jnp.sum(e, axis=-1, keepdims=True)
            dp = lax.dot_general(doh, vh, NT, preferred_element_type=F32)
            ds = pr * (dp - jnp.sum(pr * dp, axis=-1, keepdims=True))
            ds_b = ds.astype(BF16)
            dq_ref[0, :, c0:c0 + MEM_HEAD_DIM] = (jnp.dot(ds_b, kh, preferred_element_type=F32) * inv).astype(BF16)
            dkv_ref[0, :, c0:c0 + MEM_HEAD_DIM] += lax.dot_general(ds_b, qh, TN, preferred_element_type=F32) * inv
            dkv_ref[0, :, D + c0:D + c0 + MEM_HEAD_DIM] += lax.dot_general(
                pr.astype(BF16), doh, TN, preferred_element_type=F32)

    row = pl.BlockSpec((1, tq, D), lambda b, i: (b, i, 0))
    kvs = pl.BlockSpec((1, M, 2 * D), lambda b, i: (b, 0, 0))
    return _call(
        body, name=name, grid=(B, S // tq), in_specs=[row, kvs, row], out_specs=[row, kvs],
        out_shape=[jax.ShapeDtypeStruct((B, S, D), BF16), jax.ShapeDtypeStruct((B, M, 2 * D), F32)],
        compiler_params=_params(("parallel", "arbitrary")),
    )(qm, kv, do)


def swiglu_fwd(gu, *, name, tm=256):
    T, F2 = gu.shape
    Fh = F2 // 2
    tm = min(tm, T)

    def body(gu_ref, o_ref, ot_ref):
        g = gu_ref[:, :Fh].astype(F32)
        u = gu_ref[:, Fh:].astype(F32)
        act = g * _sigmoid(g) * u
        o_ref[...] = act.astype(BF16)
        ot_ref[...] = act.T.astype(BF16)

    return _call(
        body, name=name, grid=(T // tm,),
        in_specs=[pl.BlockSpec((tm, F2), lambda i: (i, 0))],
        out_specs=[pl.BlockSpec((tm, Fh), lambda i: (i, 0)), pl.BlockSpec((Fh, tm), lambda i: (0, i))],
        out_shape=[jax.ShapeDtypeStruct((T, Fh), BF16), jax.ShapeDtypeStruct((Fh, T), BF16)],
        compiler_params=_params(("parallel",)),
    )(gu)


def swiglu_bwd(gu, dact, *, name, tm=256):
    T, F2 = gu.shape
    Fh = F2 // 2
    tm = min(tm, T)

    def body(gu_ref, d_ref, o_ref):
        g = gu_ref[:, :Fh].astype(F32)
        u = gu_ref[:, Fh:].astype(F32)
        d = d_ref[...].astype(F32)
        sg = _sigmoid(g)
        o_ref[:, :Fh] = (d * u * (sg * (1.0 + g * (1.0 - sg)))).astype(BF16)
        o_ref[:, Fh:] = (d * g * sg).astype(BF16)

    return _call(
        body, name=name, grid=(T // tm,),
        in_specs=[pl.BlockSpec((tm, F2), lambda i: (i, 0)), pl.BlockSpec((tm, Fh), lambda i: (i, 0))],
        out_specs=pl.BlockSpec((tm, F2), lambda i: (i, 0)),
        out_shape=jax.ShapeDtypeStruct((T, F2), BF16),
        compiler_params=_params(("parallel",)),
    )(gu, dact)


LATE_MID = ("w_out", "w_mq", "w_mo")
LATE_KV = ("w_mkv",)
LATE_FFN = ("w_gu", "w_down")
LATE = LATE_MID + LATE_KV + LATE_FFN
RS_GROUPS = (("w_gu", "w_down"), ("w_out", "w_mq", "w_mkv", "w_mo"), ("w_in",))


def pair_sums(names, g42, got):
    return {n: pair_sum(g, o, name="rs_pair_sum_" + n) for n, g, o in zip(names, g42, got)}


def local_step(x, mem, target, sp, w_in_full, late_shards):
    B, S, D = x.shape
    T = B * S
    M = mem.shape[1]
    row = lambda v: v.reshape(1, -1).astype(F32)
    g_mix, g_x, g_mem, g_ffn, g_final = (row(sp[k]) for k in ("g_mix", "g_x", "g_mem", "g_ffn", "g_final"))
    conv_b, ln_g, ln_b = row(sp["conv_b"]), row(sp["ln_g"]), row(sp["ln_b"])
    conv_w = jnp.pad(sp["conv_w"].astype(F32), ((0, HALO - CONV_K), (0, 0)))
    b_f = jnp.pad(row(sp["b_f"]), ((0, 0), (0, LANES - FOX_HEADS)))
    n_main = 2 * CONV_CH + 3 * FOX_W
    w_main = w_in_full[:, :n_main]
    w_f = jnp.pad(w_in_full[:, n_main:], ((0, 0), (0, LANES - FOX_HEADS)))

    x2d = x.reshape(T, D)
    h, h_t = rmsnorm_fwd(x2d, g_mix, name="rms_mix")
    z = matmul(h, w_main, out_dtype=BF16, tn=n_main, name="mm_in")
    z3 = z.reshape(B, S, n_main)
    n_mid, n_kv = len(LATE_MID), len(LATE_MID) + len(LATE_KV)
    (conv_out, conv_t), partly_mid = conv_branch_fwd(z3, conv_w, conv_b, ln_g, ln_b, name="conv_fwd",
                                                     rider=AllGatherStage1(late_shards[:n_mid]))
    (f_raw, c_col, c_row), partly_kv = fgate_fwd(h.reshape(B, S, D), w_f, b_f, name="fgate_fwd",
                                                 rider=AllGatherStage1(late_shards[n_mid:n_kv]))
    (att, lse, att_t), partly_ffn = fox_fwd(z3, c_col, c_row, name="fox_fwd",
                                            rider=AllGatherStage1(late_shards[n_kv:]))
    gathered = all_gather_stage2(partly_mid + partly_kv + partly_ffn, name="ag_late_stage2")
    wf = {n: _full_from_gathered(n, blk) for n, blk in zip(LATE, gathered)}
    cat =jnp.concatenate([conv_out, att], axis=-1).reshape(T, D)
    x1 = matmul(cat, wf["w_out"], out_dtype=F32, res=x2d, tn=D, name="mm_out")
    hx, hx_t = rmsnorm_fwd(x1, g_x, name="rms_x")
    qm = matmul(hx, wf["w_mq"], out_dtype=BF16, tn=D, name="mm_mq")
    mem2d = mem.reshape(B * M, D)
    mem_n, mem_n_t = rmsnorm_fwd(mem2d, g_mem, name="rms_mem")
    kv = matmul(mem_n, wf["w_mkv"], out_dtype=BF16, tn=2 * D, name="mm_mkv").reshape(B, M, 2 * D)
    o, o_t = xattn_fwd(qm.reshape(B, S, D), kv, name="xattn_fwd")
    o = o.reshape(T, D)
    x2 = matmul(o, wf["w_mo"], out_dtype=F32, res=x1, tn=D, name="mm_mo")
    hf, hf_t = rmsnorm_fwd(x2, g_ffn, name="rms_ffn")
    gu = matmul(hf, wf["w_gu"], out_dtype=BF16, tn=2816, name="mm_gu")
    act, act_t = swiglu_fwd(gu, name="swiglu_fwd")
    x3 = matmul(act, wf["w_down"], out_dtype=F32, res=x2, tn=D, name="mm_down")
    dx3, dg_final, loss = final_loss_bwd(x3, g_final, target.reshape(T, D), name="loss_bwd")
    gw = {}
    gw["w_down"] = matmul(act_t, dx3, out_dtype=BF16, tm=1408, tn=256, name="dw_down")
    dact = matmul(dx3, wf["w_down"], tb=True, out_dtype=BF16, tn=2816, name="dx_down")
    dgu = swiglu_bwd(gu, dact, name="swiglu_bwd")
    gw["w_gu"] = matmul(hf_t, dgu, out_dtype=BF16, tn=1408, name="dw_gu")
    g42 = [_shards_from_full(n, gw[n]) for n in RS_GROUPS[0]]
    dhf, got = matmul(dgu, wf["w_gu"], tb=True, out_dtype=BF16, tm=256, tn=D, name="dx_gu",
                      rider=SiblingExchange(g42))
    parts = pair_sums(RS_GROUPS[0], g42, got)
    dx2, dg_ffn = rmsnorm_bwd(x2, g_ffn, dhf, dx3, name="rms_ffn_bwd")
    gw["w_mo"] = matmul(o_t, dx2, out_dtype=BF16, name="dw_mo")
    do = matmul(dx2, wf["w_mo"], tb=True, out_dtype=BF16, tn=D, name="dx_mo")
    dqm, dkv = xattn_bwd(qm.reshape(B, S, D), kv, do.reshape(B, S, D), name="xattn_bwd")
    dqm = dqm.reshape(T, D)
    dkv = dkv.reshape(B * M, 2 * D)
    gw["w_mq"] = matmul(hx_t, dqm, out_dtype=BF16, tn=D, name="dw_mq")
    dhx = matmul(dqm, wf["w_mq"], tb=True, out_dtype=BF16, tn=D, name="dx_mq")
    gw["w_mkv"] = matmul(mem_n_t, dkv, out_dtype=BF16, tn=D, name="dw_mkv")
    dmem_n = matmul(dkv, wf["w_mkv"], tb=True, out_dtype=BF16, tn=D, name="dx_mkv")
    _, dg_mem = rmsnorm_bwd(mem2d, g_mem, dmem_n, None, name="rms_mem_bwd")
    dx1, dg_x = rmsnorm_bwd(x1, g_x, dhx, dx2, name="rms_x_bwd")
    gw["w_out"] = jnp.concatenate([matmul(conv_t, dx1, out_dtype=BF16, name="dw_out_conv"),
                                   matmul(att_t, dx1, out_dtype=BF16, name="dw_out_att")], axis=0)
    g42 = [_shards_from_full(n, gw[n]) for n in RS_GROUPS[1]]
    dcat, got = matmul(dx1, wf["w_out"], tb=True, out_dtype=BF16, tn=D, name="dx_out", rider=SiblingExchange(g42))
    dcat = dcat.reshape(B, S, D)
    parts.update(pair_sums(RS_GROUPS[1], g42, got))
    dy, dconv_w, dvec = conv_branch_bwd_a(z3, dcat, conv_w, conv_b, ln_g, ln_b, name="conv_bwd_a")
    dug = conv_branch_bwd_b(z3, dy, conv_w, name="conv_bwd_b")
    gots = {}
    (dq, stats), got = fox_bwd_dq(z3, dcat, lse, c_col, c_row, name="fox_bwd_dq",
                                  rider=ChipExchange([parts[n] for n in RS_GROUPS[0]]))
    gots.update(zip(RS_GROUPS[0], got))
    (dk, dv, dc), got = fox_bwd_dkdv(z3, dcat, stats, c_col, name="fox_bwd_dkdv",
                                     rider=ChipExchange([parts[n] for n in RS_GROUPS[1]]))
    gots.update(zip(RS_GROUPS[1], got))
    df, db_f = fgate_bwd(dc, f_raw, b_f, name="fgate_bwd")
    dz = jnp.concatenate([dug, dq, dk, dv], axis=-1).reshape(T, n_main)
    df2 = df.reshape(T, LANES)
    dw_main = matmul(h_t, dz, out_dtype=BF16, tn=1280, name="dw_in")
    dw_f = matmul(h_t, df2, out_dtype=BF16, name="dw_f")
    gw["w_in"] = jnp.concatenate([dw_main, dw_f[:, :FOX_HEADS]], axis=-1)
    dh_f = matmul(df2, w_f, tb=True, out_dtype=F32, tn=D, name="dx_f")
    g42 = [_shards_from_full("w_in", gw["w_in"])]
    parts.update(pair_sums(RS_GROUPS[2], g42, run_rider(SiblingExchange(g42), name="rs_sibling_in")))
    dh, (gots["w_in"],) = matmul(dz, w_main, tb=True, out_dtype=F32, res=dh_f, tn=D, name="dx_in",
                                 rider=ChipExchange([parts["w_in"]]))
    dx, dg_mix = rmsnorm_bwd(x2d, g_mix, dh, dx1, name="rms_mix_bwd")
    gs = dict(g_mix=dg_mix, b_f=db_f[:, :FOX_HEADS], conv_w=dconv_w[:CONV_K], conv_b=dvec[0:1],
              ln_g=dvec[1:2], ln_b=dvec[2:3], g_x=dg_x, g_mem=dg_mem, g_ffn=dg_ffn, g_final=dg_final)
    return loss, dx.reshape(B, S, D), gs, {n: (parts[n], gots[n]) for n in BIG}


def _me():
    return lax.axis_index("x"), lax.axis_index("y"), lax.axis_index("c")


def _any_specs(n):
    return [pl.BlockSpec(memory_space=pl.ANY)] * n


def all_gather(xs, *, name):
    n = len(xs)

    def body(*refs):
        x_refs, out_refs = refs[:n], refs[n:2 * n]
        send_sems, recv_sems, local_sems = refs[2 * n:]
        x, y, c = _me()
        me, sibling = (x, y, c), (x, y, 1 - c)
        chips = [(1 - x, y), (x, 1 - y), (1 - x, 1 - y)]

        def slot(a, px, py, pc):
            return out_refs[a].at[4 * px + 2 * py + pc]

        def copy(a, k, block, to, own=False):
            return pltpu.make_async_remote_copy(
                src_ref=x_refs[a] if own else slot(a, *block), dst_ref=slot(a, *block),
                send_sem=send_sems.at[k, a], recv_sem=recv_sems.at[k, a], device_id=to, device_id_type=MESH)

        mine = [pltpu.make_async_copy(x_refs[a], slot(a, *me), local_sems.at[a]) for a in range(n)]
        first = [copy(a, 0, me, sibling, own=True) for a in range(n)]
        first += [copy(a, 1 + j, me, (*chip, c), own=True) for j, chip in enumerate(chips) for a in range(n)]
        for cp in mine + first:
            cp.start()
        passed = []
        for j, chip in enumerate(chips):
            for a in range(n):
                copy(a, 1 + j, (*chip, c), me).wait_recv()
                passed.append(copy(a, 4 + j, (*chip, c), sibling))
                passed[-1].start()
        for a in range(n):
            copy(a, 0, sibling, me).wait_recv()
            for j, chip in enumerate(chips):
                copy(a, 4 + j, (*chip, 1 - c), me).wait_recv()
        for cp in first + passed:
            cp.wait_send()
        for cp in mine:
            cp.wait()

    return _call(
        body, name=name, in_specs=_any_specs(n), out_specs=_any_specs(n),
        out_shape=[jax.ShapeDtypeStruct((N_DEV,) + v.shape, v.dtype) for v in xs],
        scratch_shapes=[pltpu.SemaphoreType.DMA((7, n)), pltpu.SemaphoreType.DMA((7, n)),
                        pltpu.SemaphoreType.DMA((n,))],
    )(*xs)


class SiblingExchange:
    def __init__(self, gs):
        n = len(gs)
        self.n, self.inputs = n, list(gs)
        self.out_shape = [jax.ShapeDtypeStruct((4,) + g.shape[2:], g.dtype) for g in gs]
        self.scratch = [pltpu.SemaphoreType.DMA((n,)), pltpu.SemaphoreType.DMA((n,))]

    def _copies(self, g_refs, out_refs, sems):
        send_sems, recv_sems = sems
        x, y, c = _me()
        return [pltpu.make_async_remote_copy(
            src_ref=g_refs[a].at[:, 1 - c], dst_ref=out_refs[a], send_sem=send_sems.at[a],
            recv_sem=recv_sems.at[a], device_id=(x, y, 1 - c), device_id_type=MESH) for a in range(self.n)]

    def start(self, in_refs, out_refs, sems):
        for cp in self._copies(in_refs, out_refs, sems):
            cp.start()

    def finish(self, in_refs, out_refs, sems):
        for cp in self._copies(in_refs, out_refs, sems):
            cp.wait()


def run_rider(rider, *, name):
    return hosted_call(None, rider, name=name, grid=(), in_specs=[], out_specs=[], out_shape=[],
                       scratch_shapes=[], args=[])[1]


class ChipExchange:
    def __init__(self, ps):
        n = len(ps)
        self.n, self.inputs = n, list(ps)
        self.out_shape = [jax.ShapeDtypeStruct(p.shape, p.dtype) for p in ps]
        self.scratch = [pltpu.SemaphoreType.DMA((3, n)), pltpu.SemaphoreType.DMA((3, n))]

    def _copies(self, p_refs, out_refs, sems, outgoing):
        send_sems, recv_sems = sems
        x, y, c = _me()
        my_chip = 2 * x + y
        cps = []
        for k in range(3):
            px, py = x ^ ((k + 1) >> 1), y ^ ((k + 1) & 1)
            src, dst = (2 * px + py, my_chip) if outgoing else (my_chip, 2 * px + py)
            for a in range(self.n):
                cps.append(pltpu.make_async_remote_copy(
                    src_ref=p_refs[a].at[src], dst_ref=out_refs[a].at[dst], send_sem=send_sems.at[k, a],
                    recv_sem=recv_sems.at[k, a], device_id=(px, py, c), device_id_type=MESH))
        return cps

    def start(self, in_refs, out_refs, sems):
        for cp in self._copies(in_refs, out_refs, sems, True):
            cp.start()

    def finish(self, in_refs, out_refs, sems):
        for cp in self._copies(in_refs, out_refs, sems, False):
            cp.wait_recv()
        for cp in self._copies(in_refs, out_refs, sems, True):
            cp.wait_send()


class AllGatherStage1:
    def __init__(self, xs):
        n = len(xs)
        self.n, self.inputs = n, list(xs)
        self.out_shape = [jax.ShapeDtypeStruct((N_DEV,) + v.shape, v.dtype) for v in xs]
        self.scratch = [pltpu.SemaphoreType.DMA((4, n)), pltpu.SemaphoreType.DMA((4, n)),
                        pltpu.SemaphoreType.DMA((n,))]

    def _copies(self, x_refs, out_refs, sems, kind):
        send_sems, recv_sems, local_sems = sems
        x, y, c = _me()
        slot = lambda a, d: out_refs[a].at[4 * d[0] + 2 * d[1] + d[2]]
        if kind == "local":
            return [pltpu.make_async_copy(x_refs[a], slot(a, (x, y, c)), local_sems.at[a]) for a in range(self.n)]
        cps = []
        for k, peer in enumerate([(x, y, 1 - c), (1 - x, y, c), (x, 1 - y, c), (1 - x, 1 - y, c)]):
            for a in range(self.n):
                cps.append(pltpu.make_async_remote_copy(
                    src_ref=x_refs[a], dst_ref=slot(a, (x, y, c) if kind == "out" else peer),
                    send_sem=send_sems.at[k, a], recv_sem=recv_sems.at[k, a], device_id=peer, device_id_type=MESH))
        return cps

    def start(self, in_refs, out_refs, sems):
        for cp in self._copies(in_refs, out_refs, sems, "local") + self._copies(in_refs, out_refs, sems, "out"):
            cp.start()

    def finish(self, in_refs, out_refs, sems):
        for cp in self._copies(in_refs, out_refs, sems, "in"):
            cp.wait_recv()
        for cp in self._copies(in_refs, out_refs, sems, "out"):
            cp.wait_send()
        for cp in self._copies(in_refs, out_refs, sems, "local"):
            cp.wait()


def all_gather_stage2(outs, *, name):
    n = len(outs)

    def body(*refs):
        out_refs = refs[n:2 * n]
        send_sems, recv_sems = refs[2 * n:]
        x, y, c = _me()
        sends, recvs = [], []
        for k, (px, py) in enumerate([(1 - x, y), (x, 1 - y), (1 - x, 1 - y)]):
            for a in range(n):
                mk = lambda pc: pltpu.make_async_remote_copy(
                    src_ref=out_refs[a].at[4 * px + 2 * py + c], dst_ref=out_refs[a].at[4 * px + 2 * py + pc],
                    send_sem=send_sems.at[k, a], recv_sem=recv_sems.at[k, a], device_id=(x, y, 1 - c),
                    device_id_type=MESH)
                sends.append(mk(c))
                recvs.append(mk(1 - c))
        for cp in sends:
            cp.start()
        for cp in recvs:
            cp.wait_recv()
        for cp in sends:
            cp.wait_send()

    return _call(
        body, name=name, in_specs=_any_specs(n), out_specs=_any_specs(n),
        out_shape=[jax.ShapeDtypeStruct(o.shape, o.dtype) for o in outs],
        input_output_aliases={a: a for a in range(n)},
        scratch_shapes=[pltpu.SemaphoreType.DMA((3, n)), pltpu.SemaphoreType.DMA((3, n))],
    )(*outs)


def hosted_call(body, rider, *, name, grid, in_specs, out_specs, out_shape, scratch_shapes, args, vmem=None):
    n_in, n_out, n_scr = len(in_specs), len(out_specs), len(scratch_shapes)
    r_in, r_out = (len(rider.inputs), len(rider.out_shape)) if rider is not None else (0, 0)

    def wrapped(*refs):
        ins, refs = refs[:n_in], refs[n_in:]
        rins, refs = refs[:r_in], refs[r_in:]
        outs, refs = refs[:n_out], refs[n_out:]
        routs, refs = refs[:r_out], refs[r_out:]
        scr, rscr = refs[:n_scr], refs[n_scr:]
        ids = [pl.program_id(d) for d in range(len(grid))]
        first = functools.reduce(jnp.logical_and, [i == 0 for i in ids], True)
        last = functools.reduce(jnp.logical_and, [i == g - 1 for i, g in zip(ids, grid)], True)
        if rider is not None and grid:
            pl.when(first)(lambda: rider.start(rins, routs, rscr))
        elif rider is not None:
            rider.start(rins, routs, rscr)
        if body is not None:
            body(*ins, *outs, *scr)
        if rider is not None and grid:
            pl.when(last)(lambda: rider.finish(rins, routs, rscr))
        elif rider is not None:
            rider.finish(rins, routs, rscr)

    kw = dict(grid=grid) if grid else {}
    if grid or vmem is not None:
        kw["compiler_params"] = _params(("arbitrary",) * len(grid) if grid else None, vmem)
    res = _call(
        wrapped, name=name, in_specs=list(in_specs) + _any_specs(r_in), out_specs=list(out_specs) + _any_specs(r_out),
        out_shape=list(out_shape) + (rider.out_shape if rider is not None else []),
        scratch_shapes=list(scratch_shapes) + (rider.scratch if rider is not None else []), **kw,
    )(*args, *(rider.inputs if rider is not None else []))
    return list(res[:n_out]), list(res[n_out:])


def _pick_rows(r, target=256):
    best = None
    for d in range(16, min(r, target) + 1, 16):
        if r % d == 0:
            best = d
    return r if best is None else best


def pair_sum(g, got, *, name):
    _, _, R, C = g.shape
    tr = _pick_rows(R)

    def body(g_ref, got_ref, o_ref):
        mine = jnp.where(lax.axis_index("c") == 0, g_ref[:, 0], g_ref[:, 1])
        o_ref[...] = (mine.astype(F32) + got_ref[...].astype(F32)).astype(o_ref.dtype)

    return _call(
        body, name=name, grid=(R // tr,),
        in_specs=[pl.BlockSpec((4, 2, tr, C), lambda i: (0, 0, i, 0)), pl.BlockSpec((4, tr, C), lambda i: (0, i, 0))],
        out_specs=pl.BlockSpec((4, tr, C), lambda i: (0, i, 0)),
        out_shape=jax.ShapeDtypeStruct((4, R, C), g.dtype),
        compiler_params=_params(("parallel",)),
    )(g, got)


def chip_sum_adamw(p, got, w, m, v, *, name):
    _, R, C = p.shape
    assert w.shape == (1, R, C), (name, w.shape, p.shape)
    tr = _pick_rows(R)

    def body(p_ref, got_ref, w_ref, m_ref, v_ref, g_ref, d_ref, mo_ref, vo_ref):
        my_chip = 2 * lax.axis_index("x") + lax.axis_index("y")
        g = jnp.zeros((tr, C), F32)
        for j in range(4):
            g = g + jnp.where(my_chip == j, p_ref[j], got_ref[j]).astype(F32)
        g_ref[0] = g
        d_ref[0], mo_ref[0], vo_ref[0] = _adamw_math(w_ref[0], g, m_ref[0], v_ref[0])

    part = pl.BlockSpec((4, tr, C), lambda i: (0, i, 0))
    spec = pl.BlockSpec((1, tr, C), lambda i: (0, i, 0))
    return _call(
        body, name=name, grid=(R // tr,), in_specs=[part, part, spec, spec, spec], out_specs=[spec] * 4,
        out_shape=[jax.ShapeDtypeStruct((1, R, C), F32)] * 4,
        compiler_params=_params(("parallel",)),
    )(p, got, w, m, v)


def rows_sum(g8, *, name):
    _, R, C = g8.shape

    def body(g_ref, o_ref):
        acc = g_ref[0]
        for j in range(1, N_DEV):
            acc = acc + g_ref[j]
        o_ref[...] = acc

    return _call(body, name=name, out_shape=jax.ShapeDtypeStruct((R, C), F32))(g8)


def _adamw_math(w, g, m, v):
    m = ADAM_B1 * m + (1.0 - ADAM_B1) * g
    v = ADAM_B2 * v + (1.0 - ADAM_B2) * (g * g)
    m_hat = m / (1.0 - ADAM_B1 ** ADAM_STEP)
    v_hat = v / (1.0 - ADAM_B2 ** ADAM_STEP)
    delta = -ADAM_LR * (m_hat / (jnp.sqrt(v_hat) + ADAM_EPS) + ADAM_WD * w)
    return delta, m, v


def adamw_small(wgmv, *, name):
    n = len(wgmv)

    def body(*refs):
        ins, outs = refs[:4 * n], refs[4 * n:]
        for a in range(n):
            w_ref, g_ref, m_ref, v_ref = ins[4 * a:4 * a + 4]
            d, mn, vn = _adamw_math(w_ref[...], g_ref[...], m_ref[...], v_ref[...])
            outs[3 * a][...] = d
            outs[3 * a + 1][...] = mn
            outs[3 * a + 2][...] = vn

    flat = [t for tup in wgmv for t in tup]
    res = _call(
        body, name=name,
        out_shape=[jax.ShapeDtypeStruct(tup[0].shape, F32) for tup in wgmv for _ in range(3)],
    )(*flat)
    return [tuple(res[3 * a:3 * a + 3]) for a in range(n)]


BIG = ("w_in", "w_out", "w_mq", "w_mkv", "w_mo", "w_gu", "w_down")
COL_SHARDED = ("w_in", "w_mkv", "w_gu")
SMALL = ("g_mix", "b_f", "conv_w", "conv_b", "ln_g", "ln_b", "g_x", "g_mem", "g_ffn", "g_final")


def _full_from_gathered(n, blk):
    _, rr, cc = blk.shape
    if n in COL_SHARDED:
        return jnp.concatenate([blk[k] for k in range(N_DEV)], axis=1)
    return blk.reshape(N_DEV * rr, cc)


def _shards_from_full(n, g):
    rr, cc = g.shape
    if n in COL_SHARDED:
        w = cc // N_DEV
        return jnp.stack([g[:, k * w:(k + 1) * w] for k in range(N_DEV)]).reshape(4, 2, rr, w)
    return g.reshape(4, 2, rr // N_DEV, cc)


def _small_layout():
    sizes = dict(g_mix=1024, b_f=8, conv_w=CONV_K * CONV_CH, conv_b=512, ln_g=512, ln_b=512, g_x=1024,
                 g_mem=1024, g_ffn=1024, g_final=1024, loss=1)
    lay, r0 = {}, 0
    for n, sz in sizes.items():
        r = -(-sz // LANES)
        lay[n] = (r0, r, sz)
        r0 += r
    return lay, -(-r0 // 8) * 8


def kernel(x, mem, g_mix, w_in, b_f, conv_w, conv_b, ln_g, ln_b, w_out, g_x, g_mem, w_mq, w_mkv, w_mo, g_ffn, w_gu, w_down, g_final, loss_target, m_g_mix, m_w_in, m_b_f, m_conv_w, m_conv_b, m_ln_g, m_ln_b, m_w_out, m_g_x, m_g_mem, m_w_mq, m_w_mkv, m_w_mo, m_g_ffn, m_w_gu, m_w_down, m_g_final, v_g_mix, v_w_in, v_b_f, v_conv_w, v_conv_b, v_ln_g, v_ln_b, v_w_out, v_g_x, v_g_mem, v_w_mq, v_w_mkv, v_w_mo, v_g_ffn, v_w_gu, v_w_down, v_g_final):
    names = ["g_mix", "w_in", "b_f", "conv_w", "conv_b", "ln_g", "ln_b", "w_out", "g_x", "g_mem", "w_mq",
             "w_mkv", "w_mo", "g_ffn", "w_gu", "w_down", "g_final"]
    W = dict(zip(names, [g_mix, w_in, b_f, conv_w, conv_b, ln_g, ln_b, w_out, g_x, g_mem, w_mq, w_mkv, w_mo,
                         g_ffn, w_gu, w_down, g_final]))
    Mo = dict(zip(names, [m_g_mix, m_w_in, m_b_f, m_conv_w, m_conv_b, m_ln_g, m_ln_b, m_w_out, m_g_x, m_g_mem,
                          m_w_mq, m_w_mkv, m_w_mo, m_g_ffn, m_w_gu, m_w_down, m_g_final]))
    Vo = dict(zip(names, [v_g_mix, v_w_in, v_b_f, v_conv_w, v_conv_b, v_ln_g, v_ln_b, v_w_out, v_g_x, v_g_mem,
                          v_w_mq, v_w_mkv, v_w_mo, v_g_ffn, v_w_gu, v_w_down, v_g_final]))
    dev = 4 * lax.axis_index("x") + 2 * lax.axis_index("y") + lax.axis_index("c")

    two = lambda a: a.reshape(-1, a.shape[-1])
    cw_shard = jnp.pad(two(conv_w), ((0, HALO - CONV_K), (0, 0)))
    w_in8, cw8 = all_gather([two(w_in).astype(BF16), cw_shard], name="ag_first")
    cw_full = cw8.transpose(1, 0, 2).reshape(HALO, -1)[:CONV_K]

    sp = dict(g_mix=g_mix, b_f=b_f, conv_w=cw_full, conv_b=conv_b, ln_g=ln_g, ln_b=ln_b, g_x=g_x, g_mem=g_mem,
              g_ffn=g_ffn, g_final=g_final)
    loss_blk, grad_x, gs, reduced = local_step(x, mem, loss_target, sp, _full_from_gathered("w_in", w_in8),
                                               [two(W[n]).astype(BF16) for n in LATE])

    lay, rs = _small_layout()
    small = {**{n: gs[n] for n in SMALL}, "loss": loss_blk[:, :1]}
    parts = []
    for n, (r0, r, sz) in lay.items():
        flat = small[n].reshape(-1).astype(F32)
        parts.append(jnp.pad(flat, (0, r * LANES - sz)).reshape(r, LANES))
    spack = jnp.concatenate(parts, axis=0)
    spack = jnp.pad(spack, ((0, rs - spack.shape[0]), (0, 0)))
    ssum = rows_sum(all_gather([spack], name="ag_small")[0], name="small_sum")
    gsmall = {n: ssum[r0:r0 + r].reshape(-1)[:sz] for n, (r0, r, sz) in lay.items()}
    loss = gsmall["loss"].reshape(())

    grads, delta, new_m, new_v = {}, {}, {}, {}
    for n in BIG:
        p, o = reduced[n]
        grads[n], delta[n], new_m[n], new_v[n] = chip_sum_adamw(p, o, W[n], Mo[n], Vo[n], name="adamw_" + n)
    for n in SMALL:
        if n == "conv_w":
            full = gsmall[n].reshape(CONV_K, CONV_CH)
            ncol = conv_w.shape[-1]
            grads[n] = lax.dynamic_slice(full, (0, dev * ncol), (CONV_K, ncol)).reshape(conv_w.shape)
        else:
            grads[n] = gsmall[n].reshape(W[n].shape)
    upd = adamw_small([(two(W[n]), two(grads[n]), two(Mo[n]), two(Vo[n])) for n in SMALL], name="adamw_small")
    for n, (d, mn, vn) in zip(SMALL, upd):
        shp = W[n].shape
        delta[n], new_m[n], new_v[n] = d.reshape(shp), mn.reshape(shp), vn.reshape(shp)
    return (loss, grad_x, *[grads[n] for n in names], *[delta[n] for n in names],
            *[new_m[n] for n in names], *[new_v[n] for n in names])
```

```python
import functools
import math

import jax
import jax.numpy as jnp
from jax import lax
from jax.experimental import pallas as pl
from jax.experimental.pallas import tpu as pltpu

F32 = jnp.float32
BF16 = jnp.bfloat16
EPS = 1e-6
N_DEV = 8
CONV_CH = 512
CONV_K = 31
FOX_HEADS = 8
FOX_HEAD_DIM = 64
FOX_W = 512
MEM_HEADS = 4
MEM_HEAD_DIM = 256
HALO = 32
LANES = 128
ADAM_LR, ADAM_B1, ADAM_B2, ADAM_EPS, ADAM_WD, ADAM_STEP = 0.001, 0.9, 0.999, 1e-08, 0.01, 10
NEG = -1e30
VMEM_CAP = 60 * 1024 * 1024
MESH = pl.DeviceIdType.MESH


def _call(body, **kw):
    kw["out_shape"] = jax.tree.map(lambda s: pltpu.HBM(s.shape, s.dtype), kw["out_shape"])
    call = pl.pallas_call(body, **kw)
    return lambda *args: call(*[pltpu.with_memory_space_constraint(a, pltpu.HBM) for a in args])


def _params(sem=None, vmem=None):
    kw = {}
    if sem is not None:
        kw["dimension_semantics"] = sem
    if vmem is not None:
        kw["vmem_limit_bytes"] = int(min(VMEM_CAP, vmem))
    return pltpu.CompilerParams(**kw)


def _nbytes(shape, dtype):
    return math.prod(shape) * jnp.dtype(dtype).itemsize


def _pick(n, target):
    best = None
    for d in range(LANES, min(n, target) + 1, LANES):
        if n % d == 0:
            best = d
    return n if best is None else best


def matmul(a, b, *, tb=False, out_dtype, res=None, tm=512, tn=512, name, rider=None):
    a_list = list(a) if isinstance(a, (list, tuple)) else [a]
    b_list = list(b) if isinstance(b, (list, tuple)) else [b]
    n = len(a_list)
    assert len(b_list) == n
    M = a_list[0].shape[0]
    N = b_list[0].shape[0] if tb else b_list[0].shape[1]
    tm, tn = _pick(M, tm), _pick(N, tn)
    assert M % tm == 0 and N % tn == 0, (name, M, N, tm, tn)
    dn = (((1,), (1 if tb else 0,)), ((), ()))

    def body(*refs):
        acc = None
        for a_ref, b_ref in zip(refs[:n], refs[n:2 * n]):
            p = lax.dot_general(a_ref[...].astype(BF16), b_ref[...].astype(BF16), dn, preferred_element_type=F32)
            acc = p if acc is None else acc + p
        if res is not None:
            acc = acc + refs[2 * n][...].astype(F32)
        refs[-1][...] = acc.astype(out_dtype)

    o_spec = pl.BlockSpec((tm, tn), lambda i, j: (i, j))
    in_specs, est = [], 2 * _nbytes((tm, tn), out_dtype) + 2 * _nbytes((tm, tn), F32)
    for av in a_list:
        assert av.shape[0] == M
        in_specs.append(pl.BlockSpec((tm, av.shape[1]), lambda i, j: (i, 0)))
        est += (2 * jnp.dtype(av.dtype).itemsize + (av.dtype != BF16) * 2) * tm * av.shape[1]
    for av, bv in zip(a_list, b_list):
        K = av.shape[1]
        assert bv.shape == ((N, K) if tb else (K, N)), (name, av.shape, bv.shape)
        in_specs.append(pl.BlockSpec((tn, K), lambda i, j: (j, 0)) if tb else pl.BlockSpec((K, tn), lambda i, j: (0, j)))
        est += (2 * jnp.dtype(bv.dtype).itemsize + (bv.dtype != BF16) * 2) * tn * K
    args = a_list + b_list
    if res is not None:
        in_specs.append(o_spec)
        args.append(res)
        est += 2 * _nbytes((tm, tn), res.dtype)
    (out,), rode = hosted_call(
        body, rider, name=name, grid=(M // tm, N // tn), in_specs=in_specs, out_specs=[o_spec],
        out_shape=[jax.ShapeDtypeStruct((M, N), out_dtype)], scratch_shapes=[],
        args=args, vmem=est + (8 << 20),
    )
    return out if rider is None else (out, rode)


def _rms_scale(x):
    return lax.rsqrt(jnp.mean(x * x, axis=-1, keepdims=True) + EPS)


def rmsnorm_fwd(x, g, *, name, tm=512, rider=None):
    T, D = x.shape
    tm = min(tm, T)

    def body(x_ref, g_ref, o_ref, ot_ref):
        xv = x_ref[...]
        h = xv * _rms_scale(xv) * g_ref[...]
        o_ref[...] = h.astype(BF16)
        ot_ref[...] = h.T.astype(BF16)

    (h, h_t), rode = hosted_call(
        body, rider, name=name, grid=(T // tm,),
        in_specs=[pl.BlockSpec((tm, D), lambda i: (i, 0)), pl.BlockSpec((1, D), lambda i: (0, 0))],
        out_specs=[pl.BlockSpec((tm, D), lambda i: (i, 0)), pl.BlockSpec((D, tm), lambda i: (0, i))],
        out_shape=[jax.ShapeDtypeStruct((T, D), BF16), jax.ShapeDtypeStruct((D, T), BF16)],
        scratch_shapes=[], args=(x, g),
    )
    return (h, h_t) if rider is None else (h, h_t, rode)


def _rms_bwd_math(xv, gv, dh):
    r = _rms_scale(xv)
    xh = xv * r
    dg = jnp.sum(dh * xh, axis=0, keepdims=True)
    dxh = dh * gv
    dx = r * (dxh - xh * jnp.mean(dxh * xh, axis=-1, keepdims=True))
    return dx, dg


def rmsnorm_bwd(x, g, dh, dres, *, name, tm=256):
    T, D = x.shape
    tm = min(tm, T)

    def body(*refs):
        if dres is not None:
            x_ref, g_ref, dh_ref, dr_ref, dx_ref, dg_ref = refs
        else:
            x_ref, g_ref, dh_ref, dx_ref, dg_ref = refs
        dx, dg = _rms_bwd_math(x_ref[...], g_ref[...], dh_ref[...].astype(F32))
        if dres is not None:
            dx = dx + dr_ref[...]
        dx_ref[...] = dx

        @pl.when(pl.program_id(0) == 0)
        def _():
            dg_ref[...] = jnp.zeros_like(dg_ref)

        dg_ref[...] += dg

    row = pl.BlockSpec((tm, D), lambda i: (i, 0))
    vec = pl.BlockSpec((1, D), lambda i: (0, 0))
    ins, args = [row, vec, row], [x, g, dh]
    if dres is not None:
        ins.append(row)
        args.append(dres)
    return _call(
        body, name=name, grid=(T // tm,), in_specs=ins, out_specs=[row, vec],
        out_shape=[jax.ShapeDtypeStruct((T, D), F32), jax.ShapeDtypeStruct((1, D), F32)],
        compiler_params=_params(("arbitrary",)),
    )(*args)


def final_loss_bwd(x, g, target, *, name, tm=256):
    T, D = x.shape
    tm = min(tm, T)

    def body(x_ref, g_ref, t_ref, dx_ref, dg_ref, l_ref):
        xv, gv = x_ref[...], g_ref[...]
        e = xv * _rms_scale(xv) * gv - t_ref[...]
        part = 0.5 * jnp.sum(jnp.mean(e * e, axis=-1, keepdims=True), axis=0, keepdims=True)
        dx, dg = _rms_bwd_math(xv, gv, e * (1.0 / D))
        dx_ref[...] = dx

        @pl.when(pl.program_id(0) == 0)
        def _():
            dg_ref[...] = jnp.zeros_like(dg_ref)
            l_ref[...] = jnp.zeros_like(l_ref)

        dg_ref[...] += dg
        l_ref[...] += jnp.broadcast_to(part, l_ref.shape)

    row = pl.BlockSpec((tm, D), lambda i: (i, 0))
    vec = pl.BlockSpec((1, D), lambda i: (0, 0))
    return _call(
        body, name=name, grid=(T // tm,), in_specs=[row, vec, row],
        out_specs=[row, vec, pl.BlockSpec((1, LANES), lambda i: (0, 0))],
        out_shape=[jax.ShapeDtypeStruct((T, D), F32), jax.ShapeDtypeStruct((1, D), F32),
                   jax.ShapeDtypeStruct((1, LANES), F32)],
        compiler_params=_params(("arbitrary",)),
    )(x, g, target)


def _sigmoid(v):
    return 1.0 / (1.0 + jnp.exp(-v))


def _glu(blk):
    u = blk[:, :CONV_CH].astype(F32)
    gt = blk[:, CONV_CH:].astype(F32)
    return u * _sigmoid(gt)


def _fill_causal_ext(ext, cur_ref, halo_ref, s, ts):
    ext[pl.ds(HALO, ts), :] = _glu(cur_ref[0])
    hal = _glu(halo_ref[0])
    ext[pl.ds(0, HALO), :] = jnp.where(s > 0, hal, 0.0)


SUBLANES = 8


def _make_shifted(ext, sh):
    n = ext.shape[0]
    full = ext[...]
    for r in range(1, SUBLANES):
        sh[r - 1] = pltpu.roll(full, n - r, 0)


def _tap(ext, sh, off, ts):
    r = off % SUBLANES
    return ext[pl.ds(off, ts), :] if r == 0 else sh[r - 1, pl.ds(off - r, ts), :]


def _causal_conv(ext, sh, w_ref, ts):
    acc = jnp.zeros((ts, CONV_CH), F32)
    for j in range(CONV_K):
        acc = acc + _tap(ext, sh, HALO - (CONV_K - 1) + j, ts) * w_ref[pl.ds(j, 1), :]
    return acc


def _ln_stats(y):
    mu = jnp.mean(y, axis=-1, keepdims=True)
    yc = y - mu
    rstd = lax.rsqrt(jnp.mean(yc * yc, axis=-1, keepdims=True) + EPS)
    return yc * rstd, rstd


def _conv_specs(ts, S):
    nh = ts // HALO
    cur = pl.BlockSpec((1, ts, 2 * CONV_CH), lambda b, s: (b, s, 0))
    halo = pl.BlockSpec((1, HALO, 2 * CONV_CH), lambda b, s: (b, jnp.maximum(s * nh - 1, 0), 0))
    w = pl.BlockSpec((HALO, CONV_CH), lambda b, s: (0, 0))
    vec = pl.BlockSpec((1, CONV_CH), lambda b, s: (0, 0))
    return cur, halo, w, vec


def conv_branch_fwd(ug, conv_w, conv_b, ln_g, ln_b, *, name, ts=256, rider=None):
    B, S, _ = ug.shape
    ts = min(ts, S)
    ns = S // ts
    cur, halo, w, vec = _conv_specs(ts, S)

    def body(cur_ref, halo_ref, w_ref, cb_ref, lg_ref, lb_ref, o_ref, ot_ref, ext, sh):
        _fill_causal_ext(ext, cur_ref, halo_ref, pl.program_id(1), ts)
        _make_shifted(ext, sh)
        y = _causal_conv(ext, sh, w_ref, ts) + cb_ref[...]
        yh, _ = _ln_stats(y)
        ln = yh * lg_ref[...] + lb_ref[...]
        out = ln * _sigmoid(ln)
        o_ref[0] = out.astype(BF16)
        ot_ref[...] = out.T.astype(BF16)

    return hosted_call(
        body, rider, name=name, grid=(B, ns), in_specs=[cur, halo, w, vec, vec, vec],
        out_specs=[pl.BlockSpec((1, ts, CONV_CH), lambda b, s: (b, s, 0)),
                   pl.BlockSpec((CONV_CH, ts), lambda b, s: (0, b * ns + s))],
        out_shape=[jax.ShapeDtypeStruct((B, S, CONV_CH), BF16), jax.ShapeDtypeStruct((CONV_CH, B * S), BF16)],
        scratch_shapes=[pltpu.VMEM((ts + HALO, CONV_CH), F32),
                        pltpu.VMEM((SUBLANES - 1, ts + HALO, CONV_CH), F32)],
        args=(ug, ug, conv_w, conv_b, ln_g, ln_b),
    )


def conv_branch_bwd_a(ug, dcat, conv_w, conv_b, ln_g, ln_b, *, name, ts=256):
    B, S, _ = ug.shape
    ts = min(ts, S)
    cur, halo, w, vec = _conv_specs(ts, S)

    def body(cur_ref, halo_ref, d_ref, w_ref, cb_ref, lg_ref, lb_ref, dy_ref, dw_ref, dv_ref, ext, sh):
        _fill_causal_ext(ext, cur_ref, halo_ref, pl.program_id(1), ts)
        _make_shifted(ext, sh)
        y = _causal_conv(ext, sh, w_ref, ts) + cb_ref[...]
        yh, rstd = _ln_stats(y)
        lg = lg_ref[...]
        ln = yh * lg + lb_ref[...]
        sg = _sigmoid(ln)
        dln = d_ref[0].astype(F32) * (sg * (1.0 + ln * (1.0 - sg)))
        dyh = dln * lg
        dy = rstd * (dyh - jnp.mean(dyh, axis=-1, keepdims=True)
                     - yh * jnp.mean(dyh * yh, axis=-1, keepdims=True))
        dy_ref[0] = dy

        @pl.when((pl.program_id(0) == 0) & (pl.program_id(1) == 0))
        def _():
            dw_ref[...] = jnp.zeros_like(dw_ref)
            dv_ref[...] = jnp.zeros_like(dv_ref)

        dv_ref[pl.ds(0, 1), :] += jnp.sum(dy, axis=0, keepdims=True)
        dv_ref[pl.ds(1, 1), :] += jnp.sum(dln * yh, axis=0, keepdims=True)
        dv_ref[pl.ds(2, 1), :] += jnp.sum(dln, axis=0, keepdims=True)
        for j in range(CONV_K):
            tap = _tap(ext, sh, HALO - (CONV_K - 1) + j, ts)
            dw_ref[pl.ds(j, 1), :] += jnp.sum(dy * tap, axis=0, keepdims=True)

    return _call(
        body, name=name, grid=(B, S // ts),
        in_specs=[cur, halo, pl.BlockSpec((1, ts, CONV_CH), lambda b, s: (b, s, 0)), w, vec, vec, vec],
        out_specs=[pl.BlockSpec((1, ts, CONV_CH), lambda b, s: (b, s, 0)),
                   pl.BlockSpec((HALO, CONV_CH), lambda b, s: (0, 0)),
                   pl.BlockSpec((8, CONV_CH), lambda b, s: (0, 0))],
        out_shape=[jax.ShapeDtypeStruct((B, S, CONV_CH), F32),
                   jax.ShapeDtypeStruct((HALO, CONV_CH), F32),
                   jax.ShapeDtypeStruct((8, CONV_CH), F32)],
        scratch_shapes=[pltpu.VMEM((ts + HALO, CONV_CH), F32),
                        pltpu.VMEM((SUBLANES - 1, ts + HALO, CONV_CH), F32)],
        compiler_params=_params(("arbitrary", "arbitrary")),
    )(ug, ug, dcat, conv_w, conv_b, ln_g, ln_b)


def conv_branch_bwd_b(ug, dy, conv_w, *, name, ts=256):
    B, S, _ = ug.shape
    ts = min(ts, S)
    nh, n_halo = ts // HALO, S // HALO

    def body(cur_ref, dy_ref, nxt_ref, w_ref, o_ref, ext, sh):
        last = pl.program_id(1) == pl.num_programs(1) - 1
        ext[pl.ds(0, ts), :] = dy_ref[0]
        ext[pl.ds(ts, HALO), :] = jnp.where(last, 0.0, nxt_ref[0])
        _make_shifted(ext, sh)
        da = jnp.zeros((ts, CONV_CH), F32)
        for j in range(CONV_K):
            da = da + _tap(ext, sh, CONV_K - 1 - j, ts) * w_ref[pl.ds(j, 1), :]
        blk = cur_ref[0]
        u = blk[:, :CONV_CH].astype(F32)
        sg = _sigmoid(blk[:, CONV_CH:].astype(F32))
        o_ref[0, :, :CONV_CH] = (da * sg).astype(BF16)
        o_ref[0, :, CONV_CH:] = (da * u * sg * (1.0 - sg)).astype(BF16)

    return _call(
        body, name=name, grid=(B, S // ts),
        in_specs=[pl.BlockSpec((1, ts, 2 * CONV_CH), lambda b, s: (b, s, 0)),
                  pl.BlockSpec((1, ts, CONV_CH), lambda b, s: (b, s, 0)),
                  pl.BlockSpec((1, HALO, CONV_CH), lambda b, s: (b, jnp.minimum((s + 1) * nh, n_halo - 1), 0)),
                  pl.BlockSpec((HALO, CONV_CH), lambda b, s: (0, 0))],
        out_specs=pl.BlockSpec((1, ts, 2 * CONV_CH), lambda b, s: (b, s, 0)),
        out_shape=jax.ShapeDtypeStruct((B, S, 2 * CONV_CH), BF16),
        scratch_shapes=[pltpu.VMEM((ts + HALO, CONV_CH), F32),
                        pltpu.VMEM((SUBLANES - 1, ts + HALO, CONV_CH), F32)],
        compiler_params=_params(("parallel", "parallel")),
    )(ug, dy, dy, conv_w)


def _tri(n, lower):
    r = lax.broadcasted_iota(jnp.int32, (n, n), 0)
    c = lax.broadcasted_iota(jnp.int32, (n, n), 1)
    return ((r >= c) if lower else (r <= c)).astype(F32)


def _eye(n):
    r = lax.broadcasted_iota(jnp.int32, (n, n), 0)
    c = lax.broadcasted_iota(jnp.int32, (n, n), 1)
    return (r == c).astype(F32)


def _dot_hi(a, b, dn):
    return lax.dot_general(a, b, dn, precision=lax.Precision.HIGHEST, preferred_element_type=F32)


NN = (((1,), (0,)), ((), ()))
NT = (((1,), (1,)), ((), ()))
TN = (((0,), (0,)), ((), ()))


def _log_sigmoid(v):
    e = jnp.exp(-jnp.abs(v))
    log1p_e = jnp.where(e < 1e-3, e * (1.0 - 0.5 * e), jnp.log(1.0 + e))
    return jnp.minimum(v, 0.0) - log1p_e


def fgate_fwd(h, w_f, b_f, *, name, ts=256, rider=None):
    B, S, D = h.shape
    ts = min(ts, S)

    def body(h_ref, w_ref, b_ref, f_ref, cc_ref, cr_ref, carry):
        @pl.when(pl.program_id(1) == 0)
        def _():
            carry[...] = jnp.zeros_like(carry)

        f = jnp.dot(h_ref[0], w_ref[...], preferred_element_type=F32)
        f_ref[0] = f
        logf = _log_sigmoid(f + b_ref[...])
        c = _dot_hi(_tri(ts, True), logf, NN) + carry[pl.ds(0, 1), :]
        cc_ref[0] = c
        carry[pl.ds(0, 1), :] = c[ts - 1:ts, :]
        cr_ref[0] = _dot_hi(_eye(LANES), c, NT)

    return hosted_call(
        body, rider, name=name, grid=(B, S // ts),
        in_specs=[pl.BlockSpec((1, ts, D), lambda b, s: (b, s, 0)),
                  pl.BlockSpec((D, LANES), lambda b, s: (0, 0)),
                  pl.BlockSpec((1, LANES), lambda b, s: (0, 0))],
        out_specs=[pl.BlockSpec((1, ts, LANES), lambda b, s: (b, s, 0)),
                   pl.BlockSpec((1, ts, LANES), lambda b, s: (b, s, 0)),
                   pl.BlockSpec((1, LANES, ts), lambda b, s: (b, 0, s))],
        out_shape=[jax.ShapeDtypeStruct((B, S, LANES), F32), jax.ShapeDtypeStruct((B, S, LANES), F32),
                   jax.ShapeDtypeStruct((B, LANES, S), F32)],
        scratch_shapes=[pltpu.VMEM((8, LANES), F32)],
        args=(h, w_f, b_f),
    )


def fgate_bwd(dc, f, b_f, *, name, ts=256):
    B, S, _ = f.shape
    P = dc.shape[1]
    ts = min(ts, S)
    ns = S // ts

    def body(dc_ref, f_ref, b_ref, df_ref, db_ref, carry):
        @pl.when(pl.program_id(1) == 0)
        def _():
            carry[...] = jnp.zeros_like(carry)

        @pl.when((pl.program_id(0) == 0) & (pl.program_id(1) == 0))
        def _():
            db_ref[...] = jnp.zeros_like(db_ref)

        dc_t = dc_ref[0, 0]
        for j in range(1, P):
            dc_t = dc_t + dc_ref[0, j]
        dlogf = _dot_hi(_tri(ts, False), dc_t, NN) + carry[pl.ds(0, 1), :]
        carry[pl.ds(0, 1), :] = dlogf[0:1, :]
        df = dlogf * _sigmoid(-(f_ref[0] + b_ref[...]))
        df_ref[0] = df.astype(BF16)
        db_ref[...] += jnp.sum(df, axis=0, keepdims=True)

    return _call(
        body, name=name, grid=(B, ns),
        in_specs=[pl.BlockSpec((1, P, ts, LANES), lambda b, s: (b, 0, ns - 1 - s, 0)),
                  pl.BlockSpec((1, ts, LANES), lambda b, s: (b, ns - 1 - s, 0)),
                  pl.BlockSpec((1, LANES), lambda b, s: (0, 0))],
        out_specs=[pl.BlockSpec((1, ts, LANES), lambda b, s: (b, ns - 1 - s, 0)),
                   pl.BlockSpec((1, LANES), lambda b, s: (0, 0))],
        out_shape=[jax.ShapeDtypeStruct((B, S, LANES), BF16), jax.ShapeDtypeStruct((1, LANES), F32)],
        scratch_shapes=[pltpu.VMEM((8, LANES), F32)],
        compiler_params=_params(("arbitrary", "arbitrary")),
    )(dc, f, b_f)


def _lane_pick(tile, idx):
    lane = lax.broadcasted_iota(jnp.int32, tile.shape, 1)
    return jnp.sum(jnp.where(lane == idx, tile, 0.0), axis=-1, keepdims=True)


FOX_T = 512


def _fox_heads(q, cc_ref, p):
    lane = lax.broadcasted_iota(jnp.int32, q.shape, 1)
    qs = q * (1.0 / math.sqrt(FOX_HEAD_DIM))
    qhs = [jnp.where((lane < FOX_HEAD_DIM) == (hh == 0), qs, jnp.zeros_like(qs)) for hh in range(2)]
    crefs = [_lane_pick(cc_ref[0, pl.ds(0, 1), :], 2 * p + hh) for hh in range(2)]
    return qhs, crefs


def _fold_lanes(x, op):
    out = x[:, :LANES]
    for j in range(1, x.shape[1] // LANES):
        out = op(out, x[:, j * LANES:(j + 1) * LANES])
    return out


def _causal(t, transposed):
    r = lax.broadcasted_iota(jnp.int32, (t, t), 0)
    c = lax.broadcasted_iota(jnp.int32, (t, t), 1)
    return (r <= c) if transposed else (c <= r)


QKV0 = 8


def fox_fwd(z, c_col, c_row, *, name, rider=None):
    B, S, _ = z.shape
    assert S % FOX_T == 0
    tq, nq = FOX_T, S // FOX_T
    npair = FOX_HEADS // 2

    def body(q_ref, k_ref, v_ref, cc_ref, cr_ref, o_ref, l_ref, ot_ref, s_scr, m_scr, acc_scr):
        p, qi = pl.program_id(1), pl.program_id(2)
        qhs, crefs = _fox_heads(q_ref[0], cc_ref, p)
        lane = lax.broadcasted_iota(jnp.int32, (tq, LANES), 1)
        first = lane < FOX_HEAD_DIM
        for hh in range(2):
            m_scr[hh] = jnp.full((tq, LANES), NEG, F32)
            acc_scr[hh] = jnp.zeros((tq, LANES), F32)

        def logits(kb, diagonal):
            k0 = pl.multiple_of(kb * tq, tq)
            k = k_ref[0, pl.ds(k0, tq), :]
            for hh in range(2):
                s = lax.dot_general(qhs[hh], k, NT, preferred_element_type=F32)
                s = s + (crefs[hh] - cr_ref[0, pl.ds(2 * p + hh, 1), pl.ds(k0, tq)])
                if diagonal:
                    s = jnp.where(_causal(tq, False), s, NEG)
                s_scr[hh, kb] = s
                m_scr[hh] = jnp.maximum(m_scr[hh], _fold_lanes(s, jnp.maximum))

        def sweep1(kb, carry):
            logits(kb, False)
            return carry

        lax.fori_loop(0, qi, sweep1, 0)
        logits(qi, True)
        ms = [jnp.max(m_scr[hh], axis=-1, keepdims=True) for hh in range(2)]
        mbs = [jnp.broadcast_to(ms[hh], (tq, tq)) for hh in range(2)]

        for hh in range(2):
            m_scr[hh] = jnp.zeros((tq, LANES), F32)

        def weigh(kb, carry):
            k0 = pl.multiple_of(kb * tq, tq)
            v = v_ref[0, pl.ds(k0, tq), :]
            for hh in range(2):
                pr = jnp.exp(s_scr[hh, kb] - mbs[hh])
                m_scr[hh] += _fold_lanes(pr, jnp.add)
                acc_scr[hh] += jnp.dot(pr.astype(BF16), v, preferred_element_type=F32)
            return carry

        lax.fori_loop(0, qi + 1, weigh, 0)
        accs = [acc_scr[hh] for hh in range(2)]
        ls = [jnp.sum(m_scr[hh], axis=-1, keepdims=True) for hh in range(2)]
        out = jnp.where(first, accs[0] / ls[0], accs[1] / ls[1])
        o_ref[0] = out.astype(BF16)
        ot_ref[...] = out.T.astype(BF16)
        l_ref[0, 0] = jnp.where(first, ms[0] + jnp.log(ls[0]), ms[1] + jnp.log(ls[1]))

    return hosted_call(
        body, rider, name=name, grid=(B, npair, nq),
        in_specs=[pl.BlockSpec((1, tq, LANES), lambda b, p, i: (b, i, QKV0 + p)),
                  pl.BlockSpec((1, S, LANES), lambda b, p, i: (b, 0, QKV0 + npair + p)),
                  pl.BlockSpec((1, S, LANES), lambda b, p, i: (b, 0, QKV0 + 2 * npair + p)),
                  pl.BlockSpec((1, tq, LANES), lambda b, p, i: (b, i, 0)),
                  pl.BlockSpec((1, 8, S), lambda b, p, i: (b, 0, 0))],
        out_specs=[pl.BlockSpec((1, tq, LANES), lambda b, p, i: (b, i, p)),
                   pl.BlockSpec((1, 1, tq, LANES), lambda b, p, i: (b, p, i, 0)),
                   pl.BlockSpec((LANES, tq), lambda b, p, i: (p, b * nq + i))],
        out_shape=[jax.ShapeDtypeStruct((B, S, FOX_W), BF16),
                   jax.ShapeDtypeStruct((B, npair, S, LANES), F32),
                   jax.ShapeDtypeStruct((FOX_W, B * S), BF16)],
        scratch_shapes=[pltpu.VMEM((2, nq, tq, tq), F32), pltpu.VMEM((2, tq, LANES), F32),
                        pltpu.VMEM((2, tq, LANES), F32)],
        args=(z, z, z, c_col, c_row),
    )


def fox_bwd_dq(z, dcat, lse, c_col, c_row, *, name, rider=None):
    B, S, _ = z.shape
    tq, nq = FOX_T, S // FOX_T
    npair = FOX_HEADS // 2

    def body(q_ref, k_ref, v_ref, do_ref, l_ref, cc_ref, cr_ref, dq_ref, st_ref, p_scr, dp_scr, dl_scr):
        p, qi = pl.program_id(1), pl.program_id(2)
        qhs, crefs = _fox_heads(q_ref[0], cc_ref, p)
        lane = lax.broadcasted_iota(jnp.int32, (tq, LANES), 1)
        do_b = do_ref[0].astype(BF16)
        dohs = [jnp.where((lane < FOX_HEAD_DIM) == (hh == 0), do_b, jnp.zeros_like(do_b)) for hh in range(2)]
        lses = [_lane_pick(l_ref[0, 0], hh * FOX_HEAD_DIM) for hh in range(2)]
        lbs = [jnp.broadcast_to(lses[hh], (tq, tq)) for hh in range(2)]
        for hh in range(2):
            dl_scr[hh] = jnp.zeros((tq, LANES), F32)

        def probs(kb, diagonal):
            k0 = pl.multiple_of(kb * tq, tq)
            k = k_ref[0, pl.ds(k0, tq), :]
            v = v_ref[0, pl.ds(k0, tq), :]
            for hh in range(2):
                s = lax.dot_general(qhs[hh], k, NT, preferred_element_type=F32)
                s = s + (crefs[hh] - cr_ref[0, pl.ds(2 * p + hh, 1), pl.ds(k0, tq)])
                pr = jnp.exp(s - lbs[hh])
                if diagonal:
                    pr = jnp.where(_causal(tq, False), pr, 0.0)
                dp = lax.dot_general(dohs[hh], v, NT, preferred_element_type=F32)
                pdp = pr * dp
                dl_scr[hh] += _fold_lanes(pdp, jnp.add)
                p_scr[hh, kb] = pr
                dp_scr[hh, kb] = dp

        def first_pass(kb, carry):
            probs(kb, False)
            return carry

        lax.fori_loop(0, qi, first_pass, 0)
        probs(qi, True)

        dls = [jnp.sum(dl_scr[hh], axis=-1, keepdims=True) for hh in range(2)]
        dlbs = [jnp.broadcast_to(dls[hh], (tq, tq)) for hh in range(2)]

        def second_pass(kb, dq):
            k0 = pl.multiple_of(kb * tq, tq)
            k = k_ref[0, pl.ds(k0, tq), :]
            for hh in range(2):
                ds = p_scr[hh, kb] * (dp_scr[hh, kb] - dlbs[hh])
                kh = jnp.where((lane < FOX_HEAD_DIM) == (hh == 0), k, jnp.zeros_like(k))
                dq = dq + jnp.dot(ds.astype(BF16), kh, preferred_element_type=F32)
            return dq

        dq = lax.fori_loop(0, qi + 1, second_pass, jnp.zeros((tq, LANES), F32))
        dq_ref[0] = (dq * (1.0 / math.sqrt(FOX_HEAD_DIM))).astype(BF16)
        cols = jnp.zeros((tq, LANES), F32)
        for j, col in enumerate([crefs[0] - lses[0], crefs[1] - lses[1], dls[0], dls[1]]):
            cols = jnp.where(lane == j, col, cols)
        st_ref[0, 0] = _dot_hi(_eye(LANES), cols, NT)[:8]

    return hosted_call(
        body, rider, name=name, grid=(B, npair, nq),
        in_specs=[pl.BlockSpec((1, tq, LANES), lambda b, p, i: (b, i, QKV0 + p)),
                  pl.BlockSpec((1, S, LANES), lambda b, p, i: (b, 0, QKV0 + npair + p)),
                  pl.BlockSpec((1, S, LANES), lambda b, p, i: (b, 0, QKV0 + 2 * npair + p)),
                  pl.BlockSpec((1, tq, LANES), lambda b, p, i: (b, i, npair + p)),
                  pl.BlockSpec((1, 1, tq, LANES), lambda b, p, i: (b, p, i, 0)),
                  pl.BlockSpec((1, tq, LANES), lambda b, p, i: (b, i, 0)),
                  pl.BlockSpec((1, 8, S), lambda b, p, i: (b, 0, 0))],
        out_specs=[pl.BlockSpec((1, tq, LANES), lambda b, p, i: (b, i, p)),
                   pl.BlockSpec((1, 1, 8, tq), lambda b, p, i: (b, p, 0, i))],
        out_shape=[jax.ShapeDtypeStruct((B, S, FOX_W), BF16), jax.ShapeDtypeStruct((B, npair, 8, S), F32)],
        scratch_shapes=[pltpu.VMEM((2, nq, tq, tq), F32), pltpu.VMEM((2, nq, tq, tq), F32),
                        pltpu.VMEM((2, tq, LANES), F32)],
        args=(z, z, z, dcat, lse, c_col, c_row), vmem=56 << 20,
    )


def fox_bwd_dkdv(z, dcat, stats, c_col, *, name, rider=None):
    B, S, _ = z.shape
    tk, nq = FOX_T, S // FOX_T
    npair = FOX_HEADS // 2
    inv = 1.0 / math.sqrt(FOX_HEAD_DIM)

    def body(q_ref, k_ref, v_ref, do_ref, st_ref, cc_ref, dk_ref, dv_ref, dc_ref, dk_scr, dv_scr, dc_scr):
        p, kt = pl.program_id(1), pl.program_id(2)
        lane = lax.broadcasted_iota(jnp.int32, (tk, LANES), 1)
        masks = [(lane < FOX_HEAD_DIM) == (hh == 0) for hh in range(2)]
        k = k_ref[0]
        v = v_ref[0]
        khs = [jnp.where(masks[hh], k, jnp.zeros_like(k)) for hh in range(2)]
        vhs = [jnp.where(masks[hh], v, jnp.zeros_like(v)) for hh in range(2)]
        ccbs = [jnp.broadcast_to(_lane_pick(cc_ref[0], 2 * p + hh), (tk, tk)) for hh in range(2)]
        dk_scr[...] = jnp.zeros_like(dk_scr)
        dv_scr[...] = jnp.zeros_like(dv_scr)
        dc_scr[...] = jnp.zeros_like(dc_scr)

        def tile(qb, diagonal):
            q0 = pl.multiple_of(qb * tk, tk)
            qs = q_ref[0, pl.ds(q0, tk), :] * inv
            do_b = do_ref[0, pl.ds(q0, tk), :].astype(BF16)
            for hh in range(2):
                st = lax.dot_general(khs[hh], qs, NT, preferred_element_type=F32)
                pr = jnp.exp(st - ccbs[hh] + st_ref[0, 0, pl.ds(hh, 1), pl.ds(q0, tk)])
                if diagonal:
                    pr = jnp.where(_causal(tk, True), pr, 0.0)
                dp = lax.dot_general(vhs[hh], do_b, NT, preferred_element_type=F32)
                ds = pr * (dp - st_ref[0, 0, pl.ds(2 + hh, 1), pl.ds(q0, tk)])
                dv_scr[...] += jnp.dot(pr.astype(BF16), jnp.where(masks[hh], do_b, jnp.zeros_like(do_b)),
                                       preferred_element_type=F32)
                dk_scr[...] += jnp.dot(ds.astype(BF16), jnp.where(masks[hh], qs, jnp.zeros_like(qs)),
                                       preferred_element_type=F32)
                dc_scr[hh] -= _fold_lanes(ds, jnp.add)

        def later(qb, carry):
            tile(qb, False)
            return carry

        tile(kt, True)
        lax.fori_loop(kt + 1, nq, later, 0)
        dk_ref[0] = dk_scr[...].astype(BF16)
        dv_ref[0] = dv_scr[...].astype(BF16)
        dcs = [jnp.sum(dc_scr[hh], axis=-1, keepdims=True) for hh in range(2)]
        dc_ref[0, 0] = jnp.where(lane == 2 * p, dcs[0], jnp.where(lane == 2 * p + 1, dcs[1], 0.0))

    full = lambda col: pl.BlockSpec((1, S, LANES), col)
    tile_spec = lambda col: pl.BlockSpec((1, tk, LANES), col)
    return hosted_call(
        body, rider, name=name, grid=(B, npair, nq),
        in_specs=[full(lambda b, p, t: (b, 0, QKV0 + p)),
                  tile_spec(lambda b, p, t: (b, t, QKV0 + npair + p)),
                  tile_spec(lambda b, p, t: (b, t, QKV0 + 2 * npair + p)),
                  full(lambda b, p, t: (b, 0, npair + p)),
                  pl.BlockSpec((1, 1, 8, S), lambda b, p, t: (b, p, 0, 0)),
                  tile_spec(lambda b, p, t: (b, t, 0))],
        out_specs=[tile_spec(lambda b, p, t: (b, t, p)), tile_spec(lambda b, p, t: (b, t, p)),
                   pl.BlockSpec((1, 1, tk, LANES), lambda b, p, t: (b, p, t, 0))],
        out_shape=[jax.ShapeDtypeStruct((B, S, FOX_W), BF16)] * 2
        + [jax.ShapeDtypeStruct((B, npair, S, LANES), F32)],
        scratch_shapes=[pltpu.VMEM((tk, LANES), F32), pltpu.VMEM((tk, LANES), F32),
                        pltpu.VMEM((2, tk, LANES), F32)],
        args=(z, z, z, dcat, stats, c_col),
    )


def xattn_fwd(qm, kv, *, name, tq=512):
    B, S, D = qm.shape
    M = kv.shape[1]
    tq = min(tq, S)
    inv = 1.0 / math.sqrt(MEM_HEAD_DIM)

    nq = S // tq

    def body(q_ref, kv_ref, o_ref, ot_ref):
        for h in range(MEM_HEADS):
            c0 = h * MEM_HEAD_DIM
            qh = q_ref[0, :, c0:c0 + MEM_HEAD_DIM]
            kh = kv_ref[0, :, c0:c0 + MEM_HEAD_DIM]
            vh = kv_ref[0, :, D + c0:D + c0 + MEM_HEAD_DIM]
            s = lax.dot_general(qh, kh, NT, preferred_element_type=F32) * inv
            e = jnp.exp(s - jnp.max(s, axis=-1, keepdims=True))
            o = jnp.dot(e.astype(BF16), vh, preferred_element_type=F32) / jnp.sum(e, axis=-1, keepdims=True)
            o_ref[0, :, c0:c0 + MEM_HEAD_DIM] = o.astype(BF16)
            ot_ref[c0:c0 + MEM_HEAD_DIM, :] = o.T.astype(BF16)

    return _call(
        body, name=name, grid=(B, nq),
        in_specs=[pl.BlockSpec((1, tq, D), lambda b, i: (b, i, 0)),
                  pl.BlockSpec((1, M, 2 * D), lambda b, i: (b, 0, 0))],
        out_specs=[pl.BlockSpec((1, tq, D), lambda b, i: (b, i, 0)),
                   pl.BlockSpec((D, tq), lambda b, i: (0, b * nq + i))],
        out_shape=[jax.ShapeDtypeStruct((B, S, D), BF16), jax.ShapeDtypeStruct((D, B * S), BF16)],
        compiler_params=_params(("parallel", "parallel")),
    )(qm, kv)


def xattn_bwd(qm, kv, do, *, name, tq=512):
    B, S, D = qm.shape
    M = kv.shape[1]
    tq = min(tq, S)
    inv = 1.0 / math.sqrt(MEM_HEAD_DIM)

    def body(q_ref, kv_ref, do_ref, dq_ref, dkv_ref):
        @pl.when(pl.program_id(1) == 0)
        def _():
            dkv_ref[...] = jnp.zeros_like(dkv_ref)

        for h in range(MEM_HEADS):
            c0 = h * MEM_HEAD_DIM
            qh = q_ref[0, :, c0:c0 + MEM_HEAD_DIM]
            kh = kv_ref[0, :, c0:c0 + MEM_HEAD_DIM]
            vh = kv_ref[0, :, D + c0:D + c0 + MEM_HEAD_DIM]
            doh = do_ref[0, :, c0:c0 + MEM_HEAD_DIM]
            s = lax.dot_general(qh, kh, NT, preferred_element_type=F32) * inv
            e = jnp.exp(s - jnp.max(s, axis=-1, keepdims=True))
            pr = e / jnp.sum(e, axis=-1, keepdims=True)
            dp = lax.dot_general(doh, vh, NT, preferred_element_type=F32)
            ds = pr * (dp - jnp.sum(pr * dp, axis=-1, keepdims=True))
            ds_b = ds.astype(BF16)
            dq_ref[0, :, c0:c0 + MEM_HEAD_DIM] = (jnp.dot(ds_b, kh, preferred_element_type=F32) * inv).astype(BF16)
            dkv_ref[0, :, c0:c0 + MEM_HEAD_DIM] += lax.dot_general(ds_b, qh, TN, preferred_element_type=F32) * inv
            dkv_ref[0, :, D + c0:D + c0 + MEM_HEAD_DIM] += lax.dot_general(
                pr.astype(BF16), doh, TN, preferred_element_type=F32)

    row = pl.BlockSpec((1, tq, D), lambda b, i: (b, i, 0))
    kvs = pl.BlockSpec((1, M, 2 * D), lambda b, i: (b, 0, 0))
    return _call(
        body, name=name, grid=(B, S // tq), in_specs=[row, kvs, row], out_specs=[row, kvs],
        out_shape=[jax.ShapeDtypeStruct((B, S, D), BF16), jax.ShapeDtypeStruct((B, M, 2 * D), F32)],
        compiler_params=_params(("parallel", "arbitrary")),
    )(qm, kv, do)


def swiglu_fwd(gu, *, name, tm=256):
    T, F2 = gu.shape
    Fh = F2 // 2
    tm = min(tm, T)

    def body(gu_ref, o_ref, ot_ref):
        g = gu_ref[:, :Fh].astype(F32)
        u = gu_ref[:, Fh:].astype(F32)
        act = g * _sigmoid(g) * u
        o_ref[...] = act.astype(BF16)
        ot_ref[...] = act.T.astype(BF16)

    return _call(
        body, name=name, grid=(T // tm,),
        in_specs=[pl.BlockSpec((tm, F2), lambda i: (i, 0))],
        out_specs=[pl.BlockSpec((tm, Fh), lambda i: (i, 0)), pl.BlockSpec((Fh, tm), lambda i: (0, i))],
        out_shape=[jax.ShapeDtypeStruct((T, Fh), BF16), jax.ShapeDtypeStruct((Fh, T), BF16)],
        compiler_params=_params(("parallel",)),
    )(gu)


def swiglu_bwd(gu, dact, *, name, tm=256):
    T, F2 = gu.shape
    Fh = F2 // 2
    tm = min(tm, T)

    def body(gu_ref, d_ref, o_ref):
        g = gu_ref[:, :Fh].astype(F32)
        u = gu_ref[:, Fh:].astype(F32)
        d = d_ref[...].astype(F32)
        sg = _sigmoid(g)
        o_ref[:, :Fh] = (d * u * (sg * (1.0 + g * (1.0 - sg)))).astype(BF16)
        o_ref[:, Fh:] = (d * g * sg).astype(BF16)

    return _call(
        body, name=name, grid=(T // tm,),
        in_specs=[pl.BlockSpec((tm, F2), lambda i: (i, 0)), pl.BlockSpec((tm, Fh), lambda i: (i, 0))],
        out_specs=pl.BlockSpec((tm, F2), lambda i: (i, 0)),
        out_shape=jax.ShapeDtypeStruct((T, F2), BF16),
        compiler_params=_params(("parallel",)),
    )(gu, dact)


LATE_MID = ("w_out", "w_mq", "w_mo")
LATE_KV = ("w_mkv",)
LATE_FFN = ("w_gu", "w_down")
LATE = LATE_MID + LATE_KV + LATE_FFN
RS_GROUPS = (("w_gu", "w_down"), ("w_out", "w_mq", "w_mkv", "w_mo"), ("w_in",))


def pair_sums(names, g42, got):
    return {n: pair_sum(g, o, name="rs_pair_sum_" + n) for n, g, o in zip(names, g42, got)}


def local_step(x, mem, target, sp, first_shards, late_shards):
    B, S, D = x.shape
    T = B * S
    M = mem.shape[1]
    row = lambda v: v.reshape(1, -1).astype(F32)
    g_mix, g_x, g_mem, g_ffn, g_final = (row(sp[k]) for k in ("g_mix", "g_x", "g_mem", "g_ffn", "g_final"))
    conv_b, ln_g, ln_b = row(sp["conv_b"]), row(sp["ln_g"]), row(sp["ln_b"])
    b_f = jnp.pad(row(sp["b_f"]), ((0, 0), (0, LANES - FOX_HEADS)))
    n_ug, n_main = 2 * CONV_CH, 2 * CONV_CH + 3 * FOX_W

    x2d = x.reshape(T, D)
    h, h_t, partly = rmsnorm_fwd(x2d, g_mix, name="rms_mix", rider=AllGatherStage1(first_shards))
    w_in8, cw8 = all_gather_stage2(partly, name="ag_first_stage2")
    w_in_full = _full_from_gathered("w_in", w_in8)
    conv_w = cw8.transpose(1, 0, 2).reshape(HALO, -1)
    w_main, w_ug, w_qkv = w_in_full[:, :n_main], w_in_full[:, :n_ug], w_in_full[:, n_ug:n_main]
    w_f = jnp.pad(w_in_full[:, n_main:], ((0, 0), (0, LANES - FOX_HEADS)))
    z = matmul(h, w_main, out_dtype=BF16, tn=n_main, name="mm_in")
    z3 = z.reshape(B, S, n_main)
    n_mid, n_kv = len(LATE_MID), len(LATE_MID) + len(LATE_KV)
    (conv_out, conv_t), partly_mid = conv_branch_fwd(z3, conv_w, conv_b, ln_g, ln_b, name="conv_fwd",
                                                     rider=AllGatherStage1(late_shards[:n_mid]))
    (f_raw, c_col, c_row), partly_kv = fgate_fwd(h.reshape(B, S, D), w_f, b_f, name="fgate_fwd",
                                                 rider=AllGatherStage1(late_shards[n_mid:n_kv]))
    (att, lse, att_t), partly_ffn = fox_fwd(z3, c_col, c_row, name="fox_fwd",
                                            rider=AllGatherStage1(late_shards[n_kv:]))
    gathered = all_gather_stage2(partly_mid + partly_kv + partly_ffn, name="ag_late_stage2")
    wf = {n: _full_from_gathered(n, blk) for n, blk in zip(LATE, gathered)}
    x1 = matmul([conv_out.reshape(T, CONV_CH), att.reshape(T, FOX_W)],
                [wf["w_out"][:CONV_CH], wf["w_out"][CONV_CH:]], out_dtype=F32, res=x2d, tn=D, name="mm_out")
    hx, hx_t = rmsnorm_fwd(x1, g_x, name="rms_x")
    qm = matmul(hx, wf["w_mq"], out_dtype=BF16, tn=D, name="mm_mq")
    mem2d = mem.reshape(B * M, D)
    mem_n, mem_n_t = rmsnorm_fwd(mem2d, g_mem, name="rms_mem")
    kv = matmul(mem_n, wf["w_mkv"], out_dtype=BF16, tn=2 * D, name="mm_mkv").reshape(B, M, 2 * D)
    o, o_t = xattn_fwd(qm.reshape(B, S, D), kv, name="xattn_fwd")
    o = o.reshape(T, D)
    x2 = matmul(o, wf["w_mo"], out_dtype=F32, res=x1, tn=D, name="mm_mo")
    hf, hf_t = rmsnorm_fwd(x2, g_ffn, name="rms_ffn")
    gu = matmul(hf, wf["w_gu"], out_dtype=BF16, tn=2816, name="mm_gu")
    act, act_t = swiglu_fwd(gu, name="swiglu_fwd")
    x3 = matmul(act, wf["w_down"], out_dtype=F32, res=x2, tn=D, name="mm_down")
    dx3, dg_final, loss = final_loss_bwd(x3, g_final, target.reshape(T, D), name="loss_bwd")
    gw = {}
    gw["w_down"] = matmul(act_t, dx3, out_dtype=BF16, tm=1408, tn=256, name="dw_down")
    dact = matmul(dx3, wf["w_down"], tb=True, out_dtype=BF16, tn=2816, name="dx_down")
    dgu = swiglu_bwd(gu, dact, name="swiglu_bwd")
    gw["w_gu"] = matmul(hf_t, dgu, out_dtype=BF16, tn=1408, name="dw_gu")
    g42 = [_shards_from_full(n, gw[n]) for n in RS_GROUPS[0]]
    dhf, got = matmul(dgu, wf["w_gu"], tb=True, out_dtype=BF16, tm=256, tn=D, name="dx_gu",
                      rider=SiblingExchange(g42))
    parts = pair_sums(RS_GROUPS[0], g42, got)
    dx2, dg_ffn = rmsnorm_bwd(x2, g_ffn, dhf, dx3, name="rms_ffn_bwd")
    gw["w_mo"] = matmul(o_t, dx2, out_dtype=BF16, name="dw_mo")
    do = matmul(dx2, wf["w_mo"], tb=True, out_dtype=BF16, tn=D, name="dx_mo")
    dqm, dkv = xattn_bwd(qm.reshape(B, S, D), kv, do.reshape(B, S, D), name="xattn_bwd")
    dqm = dqm.reshape(T, D)
    dkv = dkv.reshape(B * M, 2 * D)
    gw["w_mq"] = matmul(hx_t, dqm, out_dtype=BF16, tn=D, name="dw_mq")
    dhx = matmul(dqm, wf["w_mq"], tb=True, out_dtype=BF16, tn=D, name="dx_mq")
    gw["w_mkv"] = matmul(mem_n_t, dkv, out_dtype=BF16, tn=D, name="dw_mkv")
    dmem_n = matmul(dkv, wf["w_mkv"], tb=True, out_dtype=BF16, tn=D, name="dx_mkv")
    _, dg_mem = rmsnorm_bwd(mem2d, g_mem, dmem_n, None, name="rms_mem_bwd")
    dx1, dg_x = rmsnorm_bwd(x1, g_x, dhx, dx2, name="rms_x_bwd")
    gw["w_out"] = jnp.concatenate([matmul(conv_t, dx1, out_dtype=BF16, name="dw_out_conv"),
                                   matmul(att_t, dx1, out_dtype=BF16, name="dw_out_att")], axis=0)
    g42 = [_shards_from_full(n, gw[n]) for n in RS_GROUPS[1]]
    dcat, got = matmul(dx1, wf["w_out"], tb=True, out_dtype=BF16, tn=D, name="dx_out", rider=SiblingExchange(g42))
    dcat = dcat.reshape(B, S, D)
    parts.update(pair_sums(RS_GROUPS[1], g42, got))
    dy, dconv_w, dvec = conv_branch_bwd_a(z3, dcat, conv_w, conv_b, ln_g, ln_b, name="conv_bwd_a")
    dug = conv_branch_bwd_b(z3, dy, conv_w, name="conv_bwd_b")
    gots = {}
    (dq, stats), got = fox_bwd_dq(z3, dcat, lse, c_col, c_row, name="fox_bwd_dq",
                                  rider=ChipExchange([parts[n] for n in RS_GROUPS[0]]))
    gots.update(zip(RS_GROUPS[0], got))
    (dk, dv, dc), got = fox_bwd_dkdv(z3, dcat, stats, c_col, name="fox_bwd_dkdv",
                                     rider=ChipExchange([parts[n] for n in RS_GROUPS[1]]))
    gots.update(zip(RS_GROUPS[1], got))
    df, db_f = fgate_bwd(dc, f_raw, b_f, name="fgate_bwd")
    dug2 = dug.reshape(T, n_ug)
    dqkv = jnp.concatenate([dq, dk, dv], axis=-1).reshape(T, 3 * FOX_W)
    df2 = df.reshape(T, LANES)
    dw_in = [matmul(h_t, dug2, out_dtype=BF16, tn=n_ug, name="dw_in_ug"),
             matmul(h_t, dqkv, out_dtype=BF16, tn=3 * FOX_W, name="dw_in_qkv"),
             matmul(h_t, df2, out_dtype=BF16, name="dw_f")[:, :FOX_HEADS]]
    g42 = [_shards_from_full("w_in", dw_in)]
    parts.update(pair_sums(RS_GROUPS[2], g42, run_rider(SiblingExchange(g42), name="rs_sibling_in")))
    dh, (gots["w_in"],) = matmul([dug2, dqkv, df2], [w_ug, w_qkv, w_f], tb=True, out_dtype=F32, tn=D,
                                 name="dx_in", rider=ChipExchange([parts["w_in"]]))
    dx, dg_mix = rmsnorm_bwd(x2d, g_mix, dh, dx1, name="rms_mix_bwd")
    gs = dict(g_mix=dg_mix, b_f=db_f[:, :FOX_HEADS], conv_w=dconv_w[:CONV_K], conv_b=dvec[0:1],
              ln_g=dvec[1:2], ln_b=dvec[2:3], g_x=dg_x, g_mem=dg_mem, g_ffn=dg_ffn, g_final=dg_final)
    return loss, dx.reshape(B, S, D), gs, {n: (parts[n], gots[n]) for n in BIG}


def _me():
    return lax.axis_index("x"), lax.axis_index("y"), lax.axis_index("c")


def _any_specs(n):
    return [pl.BlockSpec(memory_space=pl.ANY)] * n


def all_gather(xs, *, name):
    n = len(xs)

    def body(*refs):
        x_refs, out_refs = refs[:n], refs[n:2 * n]
        send_sems, recv_sems, local_sems = refs[2 * n:]
        x, y, c = _me()
        me, sibling = (x, y, c), (x, y, 1 - c)
        chips = [(1 - x, y), (x, 1 - y), (1 - x, 1 - y)]

        def slot(a, px, py, pc):
            return out_refs[a].at[4 * px + 2 * py + pc]

        def copy(a, k, block, to, own=False):
            return pltpu.make_async_remote_copy(
                src_ref=x_refs[a] if own else slot(a, *block), dst_ref=slot(a, *block),
                send_sem=send_sems.at[k, a], recv_sem=recv_sems.at[k, a], device_id=to, device_id_type=MESH)

        mine = [pltpu.make_async_copy(x_refs[a], slot(a, *me), local_sems.at[a]) for a in range(n)]
        first = [copy(a, 0, me, sibling, own=True) for a in range(n)]
        first += [copy(a, 1 + j, me, (*chip, c), own=True) for j, chip in enumerate(chips) for a in range(n)]
        for cp in mine + first:
            cp.start()
        passed = []
        for j, chip in enumerate(chips):
            for a in range(n):
                copy(a, 1 + j, (*chip, c), me).wait_recv()
                passed.append(copy(a, 4 + j, (*chip, c), sibling))
                passed[-1].start()
        for a in range(n):
            copy(a, 0, sibling, me).wait_recv()
            for j, chip in enumerate(chips):
                copy(a, 4 + j, (*chip, 1 - c), me).wait_recv()
        for cp in first + passed:
            cp.wait_send()
        for cp in mine:
            cp.wait()

    return _call(
        body, name=name, in_specs=_any_specs(n), out_specs=_any_specs(n),
        out_shape=[jax.ShapeDtypeStruct((N_DEV,) + v.shape, v.dtype) for v in xs],
        scratch_shapes=[pltpu.SemaphoreType.DMA((7, n)), pltpu.SemaphoreType.DMA((7, n)),
                        pltpu.SemaphoreType.DMA((n,))],
    )(*xs)


class SiblingExchange:
    def __init__(self, gs):
        n = len(gs)
        self.n, self.inputs = n, list(gs)
        self.out_shape = [jax.ShapeDtypeStruct((4,) + g.shape[2:], g.dtype) for g in gs]
        self.scratch = [pltpu.SemaphoreType.DMA((n,)), pltpu.SemaphoreType.DMA((n,))]

    def _copies(self, g_refs, out_refs, sems):
        send_sems, recv_sems = sems
        x, y, c = _me()
        return [pltpu.make_async_remote_copy(
            src_ref=g_refs[a].at[:, 1 - c], dst_ref=out_refs[a], send_sem=send_sems.at[a],
            recv_sem=recv_sems.at[a], device_id=(x, y, 1 - c), device_id_type=MESH) for a in range(self.n)]

    def start(self, in_refs, out_refs, sems):
        for cp in self._copies(in_refs, out_refs, sems):
            cp.start()

    def finish(self, in_refs, out_refs, sems):
        for cp in self._copies(in_refs, out_refs, sems):
            cp.wait()


def run_rider(rider, *, name):
    return hosted_call(None, rider, name=name, grid=(), in_specs=[], out_specs=[], out_shape=[],
                       scratch_shapes=[], args=[])[1]


class ChipExchange:
    def __init__(self, ps):
        n = len(ps)
        self.n, self.inputs = n, list(ps)
        self.out_shape = [jax.ShapeDtypeStruct(p.shape, p.dtype) for p in ps]
        self.scratch = [pltpu.SemaphoreType.DMA((3, n)), pltpu.SemaphoreType.DMA((3, n))]

    def _copies(self, p_refs, out_refs, sems, outgoing):
        send_sems, recv_sems = sems
        x, y, c = _me()
        my_chip = 2 * x + y
        cps = []
        for k in range(3):
            px, py = x ^ ((k + 1) >> 1), y ^ ((k + 1) & 1)
            src, dst = (2 * px + py, my_chip) if outgoing else (my_chip, 2 * px + py)
            for a in range(self.n):
                cps.append(pltpu.make_async_remote_copy(
                    src_ref=p_refs[a].at[src], dst_ref=out_refs[a].at[dst], send_sem=send_sems.at[k, a],
                    recv_sem=recv_sems.at[k, a], device_id=(px, py, c), device_id_type=MESH))
        return cps

    def start(self, in_refs, out_refs, sems):
        for cp in self._copies(in_refs, out_refs, sems, True):
            cp.start()

    def finish(self, in_refs, out_refs, sems):
        for cp in self._copies(in_refs, out_refs, sems, False):
            cp.wait_recv()
        for cp in self._copies(in_refs, out_refs, sems, True):
            cp.wait_send()


class AllGatherStage1:
    def __init__(self, xs):
        n = len(xs)
        self.n, self.inputs = n, list(xs)
        self.out_shape = [jax.ShapeDtypeStruct((N_DEV,) + v.shape, v.dtype) for v in xs]
        self.scratch = [pltpu.SemaphoreType.DMA((4, n)), pltpu.SemaphoreType.DMA((4, n)),
                        pltpu.SemaphoreType.DMA((n,))]

    def _copies(self, x_refs, out_refs, sems, kind):
        send_sems, recv_sems, local_sems = sems
        x, y, c = _me()
        slot = lambda a, d: out_refs[a].at[4 * d[0] + 2 * d[1] + d[2]]
        if kind == "local":
            return [pltpu.make_async_copy(x_refs[a], slot(a, (x, y, c)), local_sems.at[a]) for a in range(self.n)]
        cps = []
        for k, peer in enumerate([(x, y, 1 - c), (1 - x, y, c), (x, 1 - y, c), (1 - x, 1 - y, c)]):
            for a in range(self.n):
                cps.append(pltpu.make_async_remote_copy(
                    src_ref=x_refs[a], dst_ref=slot(a, (x, y, c) if kind == "out" else peer),
                    send_sem=send_sems.at[k, a], recv_sem=recv_sems.at[k, a], device_id=peer, device_id_type=MESH))
        return cps

    def start(self, in_refs, out_refs, sems):
        for cp in self._copies(in_refs, out_refs, sems, "local") + self._copies(in_refs, out_refs, sems, "out"):
            cp.start()

    def finish(self, in_refs, out_refs, sems):
        for cp in self._copies(in_refs, out_refs, sems, "in"):
            cp.wait_recv()
        for cp in self._copies(in_refs, out_refs, sems, "out"):
            cp.wait_send()
        for cp in self._copies(in_refs, out_refs, sems, "local"):
            cp.wait()


def all_gather_stage2(outs, *, name):
    n = len(outs)

    def body(*refs):
        out_refs = refs[n:2 * n]
        send_sems, recv_sems = refs[2 * n:]
        x, y, c = _me()
        sends, recvs = [], []
        for k, (px, py) in enumerate([(1 - x, y), (x, 1 - y), (1 - x, 1 - y)]):
            for a in range(n):
                mk = lambda pc: pltpu.make_async_remote_copy(
                    src_ref=out_refs[a].at[4 * px + 2 * py + c], dst_ref=out_refs[a].at[4 * px + 2 * py + pc],
                    send_sem=send_sems.at[k, a], recv_sem=recv_sems.at[k, a], device_id=(x, y, 1 - c),
                    device_id_type=MESH)
                sends.append(mk(c))
                recvs.append(mk(1 - c))
        for cp in sends:
            cp.start()
        for cp in recvs:
            cp.wait_recv()
        for cp in sends:
            cp.wait_send()

    return _call(
        body, name=name, in_specs=_any_specs(n), out_specs=_any_specs(n),
        out_shape=[jax.ShapeDtypeStruct(o.shape, o.dtype) for o in outs],
        input_output_aliases={a: a for a in range(n)},
        scratch_shapes=[pltpu.SemaphoreType.DMA((3, n)), pltpu.SemaphoreType.DMA((3, n))],
    )(*outs)


def hosted_call(body, rider, *, name, grid, in_specs, out_specs, out_shape, scratch_shapes, args, vmem=None):
    n_in, n_out, n_scr = len(in_specs), len(out_specs), len(scratch_shapes)
    r_in, r_out = (len(rider.inputs), len(rider.out_shape)) if rider is not None else (0, 0)

    def wrapped(*refs):
        ins, refs = refs[:n_in], refs[n_in:]
        rins, refs = refs[:r_in], refs[r_in:]
        outs, refs = refs[:n_out], refs[n_out:]
        routs, refs = refs[:r_out], refs[r_out:]
        scr, rscr = refs[:n_scr], refs[n_scr:]
        ids = [pl.program_id(d) for d in range(len(grid))]
        first = functools.reduce(jnp.logical_and, [i == 0 for i in ids], True)
        last = functools.reduce(jnp.logical_and, [i == g - 1 for i, g in zip(ids, grid)], True)
        if rider is not None and grid:
            pl.when(first)(lambda: rider.start(rins, routs, rscr))
        elif rider is not None:
            rider.start(rins, routs, rscr)
        if body is not None:
            body(*ins, *outs, *scr)
        if rider is not None and grid:
            pl.when(last)(lambda: rider.finish(rins, routs, rscr))
        elif rider is not None:
            rider.finish(rins, routs, rscr)

    kw = dict(grid=grid) if grid else {}
    if grid or vmem is not None:
        kw["compiler_params"] = _params(("arbitrary",) * len(grid) if grid else None, vmem)
    res = _call(
        wrapped, name=name, in_specs=list(in_specs) + _any_specs(r_in), out_specs=list(out_specs) + _any_specs(r_out),
        out_shape=list(out_shape) + (rider.out_shape if rider is not None else []),
        scratch_shapes=list(scratch_shapes) + (rider.scratch if rider is not None else []), **kw,
    )(*args, *(rider.inputs if rider is not None else []))
    return list(res[:n_out]), list(res[n_out:])


def _pick_rows(r, target=256):
    best = None
    for d in range(16, min(r, target) + 1, 16):
        if r % d == 0:
            best = d
    return r if best is None else best


def pair_sum(g, got, *, name):
    _, _, R, C = g.shape
    tr = _pick_rows(R)

    def body(g_ref, got_ref, o_ref):
        mine = jnp.where(lax.axis_index("c") == 0, g_ref[:, 0], g_ref[:, 1])
        o_ref[...] = (mine.astype(F32) + got_ref[...].astype(F32)).astype(o_ref.dtype)

    return _call(
        body, name=name, grid=(R // tr,),
        in_specs=[pl.BlockSpec((4, 2, tr, C), lambda i: (0, 0, i, 0)), pl.BlockSpec((4, tr, C), lambda i: (0, i, 0))],
        out_specs=pl.BlockSpec((4, tr, C), lambda i: (0, i, 0)),
        out_shape=jax.ShapeDtypeStruct((4, R, C), g.dtype),
        compiler_params=_params(("parallel",)),
    )(g, got)


def chip_sum_adamw(p, got, w, m, v, *, name):
    _, R, C = p.shape
    assert w.shape == (1, R, C), (name, w.shape, p.shape)
    tr = _pick_rows(R)

    def body(p_ref, got_ref, w_ref, m_ref, v_ref, g_ref, d_ref, mo_ref, vo_ref):
        my_chip = 2 * lax.axis_index("x") + lax.axis_index("y")
        g = jnp.zeros((tr, C), F32)
        for j in range(4):
            g = g + jnp.where(my_chip == j, p_ref[j], got_ref[j]).astype(F32)
        g_ref[0] = g
        d_ref[0], mo_ref[0], vo_ref[0] = _adamw_math(w_ref[0], g, m_ref[0], v_ref[0])

    part = pl.BlockSpec((4, tr, C), lambda i: (0, i, 0))
    spec = pl.BlockSpec((1, tr, C), lambda i: (0, i, 0))
    return _call(
        body, name=name, grid=(R // tr,), in_specs=[part, part, spec, spec, spec], out_specs=[spec] * 4,
        out_shape=[jax.ShapeDtypeStruct((1, R, C), F32)] * 4,
        compiler_params=_params(("parallel",)),
    )(p, got, w, m, v)


def rows_sum(g8, *, name):
    _, R, C = g8.shape

    def body(g_ref, o_ref):
        acc = g_ref[0]
        for j in range(1, N_DEV):
            acc = acc + g_ref[j]
        o_ref[...] = acc

    return _call(body, name=name, out_shape=jax.ShapeDtypeStruct((R, C), F32))(g8)


def _adamw_math(w, g, m, v):
    m = ADAM_B1 * m + (1.0 - ADAM_B1) * g
    v = ADAM_B2 * v + (1.0 - ADAM_B2) * (g * g)
    m_hat = m / (1.0 - ADAM_B1 ** ADAM_STEP)
    v_hat = v / (1.0 - ADAM_B2 ** ADAM_STEP)
    delta = -ADAM_LR * (m_hat / (jnp.sqrt(v_hat) + ADAM_EPS) + ADAM_WD * w)
    return delta, m, v


def adamw_small(wgmv, *, name):
    n = len(wgmv)

    def body(*refs):
        ins, outs = refs[:4 * n], refs[4 * n:]
        for a in range(n):
            w_ref, g_ref, m_ref, v_ref = ins[4 * a:4 * a + 4]
            d, mn, vn = _adamw_math(w_ref[...], g_ref[...], m_ref[...], v_ref[...])
            outs[3 * a][...] = d
            outs[3 * a + 1][...] = mn
            outs[3 * a + 2][...] = vn

    flat = [t for tup in wgmv for t in tup]
    res = _call(
        body, name=name,
        out_shape=[jax.ShapeDtypeStruct(tup[0].shape, F32) for tup in wgmv for _ in range(3)],
    )(*flat)
    return [tuple(res[3 * a:3 * a + 3]) for a in range(n)]


BIG = ("w_in", "w_out", "w_mq", "w_mkv", "w_mo", "w_gu", "w_down")
COL_SHARDED = ("w_in", "w_mkv", "w_gu")
SMALL = ("g_mix", "b_f", "conv_w", "conv_b", "ln_g", "ln_b", "g_x", "g_mem", "g_ffn", "g_final")


def _full_from_gathered(n, blk):
    _, rr, cc = blk.shape
    if n in COL_SHARDED:
        return jnp.concatenate([blk[k] for k in range(N_DEV)], axis=1)
    return blk.reshape(N_DEV * rr, cc)


def _shards_from_full(n, g):
    pieces = g if isinstance(g, list) else [g]
    rr, cc = pieces[0].shape[0], sum(p.shape[1] for p in pieces)
    if n in COL_SHARDED:
        w = cc // N_DEV
        return jnp.stack([_columns(pieces, k * w, w) for k in range(N_DEV)]).reshape(4, 2, rr, w)
    return pieces[0].reshape(4, 2, rr // N_DEV, cc)


def _columns(pieces, start, width):
    out, c0 = [], 0
    for p in pieces:
        lo, hi = max(start, c0), min(start + width, c0 + p.shape[1])
        if lo < hi:
            out.append(p[:, lo - c0:hi - c0])
        c0 += p.shape[1]
    return out[0] if len(out) == 1 else jnp.concatenate(out, axis=1)


def _small_layout():
    sizes = dict(g_mix=1024, b_f=8, conv_w=CONV_K * CONV_CH, conv_b=512, ln_g=512, ln_b=512, g_x=1024,
                 g_mem=1024, g_ffn=1024, g_final=1024, loss=1)
    lay, r0 = {}, 0
    for n, sz in sizes.items():
        r = -(-sz // LANES)
        lay[n] = (r0, r, sz)
        r0 += r
    return lay, -(-r0 // 8) * 8


def kernel(x, mem, g_mix, w_in, b_f, conv_w, conv_b, ln_g, ln_b, w_out, g_x, g_mem, w_mq, w_mkv, w_mo, g_ffn, w_gu, w_down, g_final, loss_target, m_g_mix, m_w_in, m_b_f, m_conv_w, m_conv_b, m_ln_g, m_ln_b, m_w_out, m_g_x, m_g_mem, m_w_mq, m_w_mkv, m_w_mo, m_g_ffn, m_w_gu, m_w_down, m_g_final, v_g_mix, v_w_in, v_b_f, v_conv_w, v_conv_b, v_ln_g, v_ln_b, v_w_out, v_g_x, v_g_mem, v_w_mq, v_w_mkv, v_w_mo, v_g_ffn, v_w_gu, v_w_down, v_g_final):
    names = ["g_mix", "w_in", "b_f", "conv_w", "conv_b", "ln_g", "ln_b", "w_out", "g_x", "g_mem", "w_mq",
             "w_mkv", "w_mo", "g_ffn", "w_gu", "w_down", "g_final"]
    W = dict(zip(names, [g_mix, w_in, b_f, conv_w, conv_b, ln_g, ln_b, w_out, g_x, g_mem, w_mq, w_mkv, w_mo,
                         g_ffn, w_gu, w_down, g_final]))
    Mo = dict(zip(names, [m_g_mix, m_w_in, m_b_f, m_conv_w, m_conv_b, m_ln_g, m_ln_b, m_w_out, m_g_x, m_g_mem,
                          m_w_mq, m_w_mkv, m_w_mo, m_g_ffn, m_w_gu, m_w_down, m_g_final]))
    Vo = dict(zip(names, [v_g_mix, v_w_in, v_b_f, v_conv_w, v_conv_b, v_ln_g, v_ln_b, v_w_out, v_g_x, v_g_mem,
                          v_w_mq, v_w_mkv, v_w_mo, v_g_ffn, v_w_gu, v_w_down, v_g_final]))
    dev = 4 * lax.axis_index("x") + 2 * lax.axis_index("y") + lax.axis_index("c")

    two = lambda a: a.reshape(-1, a.shape[-1])
    cw_shard = jnp.pad(two(conv_w), ((0, HALO - CONV_K), (0, 0)))
    sp = dict(g_mix=g_mix, b_f=b_f, conv_b=conv_b, ln_g=ln_g, ln_b=ln_b, g_x=g_x, g_mem=g_mem,
              g_ffn=g_ffn, g_final=g_final)
    loss_blk, grad_x, gs, reduced = local_step(x, mem, loss_target, sp, [two(w_in).astype(BF16), cw_shard],
                                               [two(W[n]).astype(BF16) for n in LATE])

    lay, rs = _small_layout()
    small = {**{n: gs[n] for n in SMALL}, "loss": loss_blk[:, :1]}
    parts = []
    for n, (r0, r, sz) in lay.items():
        flat = small[n].reshape(-1).astype(F32)
        parts.append(jnp.pad(flat, (0, r * LANES - sz)).reshape(r, LANES))
    spack = jnp.concatenate(parts, axis=0)
    spack = jnp.pad(spack, ((0, rs - spack.shape[0]), (0, 0)))
    ssum = rows_sum(all_gather([spack], name="ag_small")[0], name="small_sum")
    gsmall = {n: ssum[r0:r0 + r].reshape(-1)[:sz] for n, (r0, r, sz) in lay.items()}
    loss = gsmall["loss"].reshape(())

    grads, delta, new_m, new_v = {}, {}, {}, {}
    for n in BIG:
        p, o = reduced[n]
        grads[n], delta[n], new_m[n], new_v[n] = chip_sum_adamw(p, o, W[n], Mo[n], Vo[n], name="adamw_" + n)
    for n in SMALL:
        if n == "conv_w":
            full = gsmall[n].reshape(CONV_K, CONV_CH)
            ncol = conv_w.shape[-1]
            grads[n] = lax.dynamic_slice(full, (0, dev * ncol), (CONV_K, ncol)).reshape(conv_w.shape)
        else:
            grads[n] = gsmall[n].reshape(W[n].shape)
    upd = adamw_small([(two(W[n]), two(grads[n]), two(Mo[n]), two(Vo[n])) for n in SMALL], name="adamw_small")
    for n, (d, mn, vn) in zip(SMALL, upd):
        shp = W[n].shape
        delta[n], new_m[n], new_v[n] = d.reshape(shp), mn.reshape(shp), vn.reshape(shp)
    return (loss, grad_x, *[grads[n] for n in names], *[delta[n] for n in names],
            *[new_m[n] for n in names], *[new_v[n] for n in names])
```

```python
import functools
import math

import jax
import jax.numpy as jnp
from jax import lax
from jax.experimental import pallas as pl
from jax.experimental.pallas import tpu as pltpu

F32 = jnp.float32
BF16 = jnp.bfloat16
EPS = 1e-6
N_DEV = 8
CONV_CH = 512
CONV_K = 31
FOX_HEADS = 8
FOX_HEAD_DIM = 64
FOX_W = 512
MEM_HEADS = 4
MEM_HEAD_DIM = 256
HALO = 32
LANES = 128
ADAM_LR, ADAM_B1, ADAM_B2, ADAM_EPS, ADAM_WD, ADAM_STEP = 0.001, 0.9, 0.999, 1e-08, 0.01, 10
NEG = -1e30
VMEM_CAP = 60 * 1024 * 1024
MESH = pl.DeviceIdType.MESH


def _call(body, **kw):
    kw["out_shape"] = jax.tree.map(lambda s: pltpu.HBM(s.shape, s.dtype), kw["out_shape"])
    call = pl.pallas_call(body, **kw)
    return lambda *args: call(*[pltpu.with_memory_space_constraint(a, pltpu.HBM) for a in args])


def _params(sem=None, vmem=None):
    kw = {}
    if sem is not None:
        kw["dimension_semantics"] = sem
    if vmem is not None:
        kw["vmem_limit_bytes"] = int(min(VMEM_CAP, vmem))
    return pltpu.CompilerParams(**kw)


def _nbytes(shape, dtype):
    return math.prod(shape) * jnp.dtype(dtype).itemsize


def _pick(n, target):
    best = None
    for d in range(LANES, min(n, target) + 1, LANES):
        if n % d == 0:
            best = d
    return n if best is None else best


def matmul(a, b, *, tb=False, out_dtype, res=None, tm=512, tn=512, name, rider=None):
    a_list = list(a) if isinstance(a, (list, tuple)) else [a]
    b_list = list(b) if isinstance(b, (list, tuple)) else [b]
    n = len(a_list)
    assert len(b_list) == n
    M = a_list[0].shape[0]
    N = b_list[0].shape[0] if tb else b_list[0].shape[1]
    tm, tn = _pick(M, tm), _pick(N, tn)
    assert M % tm == 0 and N % tn == 0, (name, M, N, tm, tn)
    dn = (((1,), (1 if tb else 0,)), ((), ()))

    def body(*refs):
        acc = None
        for a_ref, b_ref in zip(refs[:n], refs[n:2 * n]):
            p = lax.dot_general(a_ref[...].astype(BF16), b_ref[...].astype(BF16), dn, preferred_element_type=F32)
            acc = p if acc is None else acc + p
        if res is not None:
            acc = acc + refs[2 * n][...].astype(F32)
        refs[-1][...] = acc.astype(out_dtype)

    o_spec = pl.BlockSpec((tm, tn), lambda i, j: (i, j))
    in_specs, est = [], 2 * _nbytes((tm, tn), out_dtype) + 2 * _nbytes((tm, tn), F32)
    for av in a_list:
        assert av.shape[0] == M
        in_specs.append(pl.BlockSpec((tm, av.shape[1]), lambda i, j: (i, 0)))
        est += (2 * jnp.dtype(av.dtype).itemsize + (av.dtype != BF16) * 2) * tm * av.shape[1]
    for av, bv in zip(a_list, b_list):
        K = av.shape[1]
        assert bv.shape == ((N, K) if tb else (K, N)), (name, av.shape, bv.shape)
        in_specs.append(pl.BlockSpec((tn, K), lambda i, j: (j, 0)) if tb else pl.BlockSpec((K, tn), lambda i, j: (0, j)))
        est += (2 * jnp.dtype(bv.dtype).itemsize + (bv.dtype != BF16) * 2) * tn * K
    args = a_list + b_list
    if res is not None:
        in_specs.append(o_spec)
        args.append(res)
        est += 2 * _nbytes((tm, tn), res.dtype)
    (out,), rode = hosted_call(
        body, rider, name=name, grid=(M // tm, N // tn), in_specs=in_specs, out_specs=[o_spec],
        out_shape=[jax.ShapeDtypeStruct((M, N), out_dtype)], scratch_shapes=[],
        args=args, vmem=est + (8 << 20),
    )
    return out if rider is None else (out, rode)


def _rms_scale(x):
    return lax.rsqrt(jnp.mean(x * x, axis=-1, keepdims=True) + EPS)


def rmsnorm_fwd(x, g, *, name, tm=512, rider=None):
    T, D = x.shape
    tm = min(tm, T)

    def body(x_ref, g_ref, o_ref, ot_ref):
        xv = x_ref[...]
        h = xv * _rms_scale(xv) * g_ref[...]
        o_ref[...] = h.astype(BF16)
        ot_ref[...] = h.T.astype(BF16)

    (h, h_t), rode = hosted_call(
        body, rider, name=name, grid=(T // tm,),
        in_specs=[pl.BlockSpec((tm, D), lambda i: (i, 0)), pl.BlockSpec((1, D), lambda i: (0, 0))],
        out_specs=[pl.BlockSpec((tm, D), lambda i: (i, 0)), pl.BlockSpec((D, tm), lambda i: (0, i))],
        out_shape=[jax.ShapeDtypeStruct((T, D), BF16), jax.ShapeDtypeStruct((D, T), BF16)],
        scratch_shapes=[], args=(x, g),
    )
    return (h, h_t) if rider is None else (h, h_t, rode)


def _rms_bwd_math(xv, gv, dh):
    r = _rms_scale(xv)
    xh = xv * r
    dg = jnp.sum(dh * xh, axis=0, keepdims=True)
    dxh = dh * gv
    dx = r * (dxh - xh * jnp.mean(dxh * xh, axis=-1, keepdims=True))
    return dx, dg


def rmsnorm_bwd(x, g, dh, dres, *, name, tm=256):
    T, D = x.shape
    tm = min(tm, T)

    def body(*refs):
        if dres is not None:
            x_ref, g_ref, dh_ref, dr_ref, dx_ref, dg_ref = refs
        else:
            x_ref, g_ref, dh_ref, dx_ref, dg_ref = refs
        dx, dg = _rms_bwd_math(x_ref[...], g_ref[...], dh_ref[...].astype(F32))
        if dres is not None:
            dx = dx + dr_ref[...]
        dx_ref[...] = dx

        @pl.when(pl.program_id(0) == 0)
        def _():
            dg_ref[...] = jnp.zeros_like(dg_ref)

        dg_ref[...] += dg

    row = pl.BlockSpec((tm, D), lambda i: (i, 0))
    vec = pl.BlockSpec((1, D), lambda i: (0, 0))
    ins, args = [row, vec, row], [x, g, dh]
    if dres is not None:
        ins.append(row)
        args.append(dres)
    return _call(
        body, name=name, grid=(T // tm,), in_specs=ins, out_specs=[row, vec],
        out_shape=[jax.ShapeDtypeStruct((T, D), F32), jax.ShapeDtypeStruct((1, D), F32)],
        compiler_params=_params(("arbitrary",)),
    )(*args)


def final_loss_bwd(x, g, target, *, name, tm=256):
    T, D = x.shape
    tm = min(tm, T)

    def body(x_ref, g_ref, t_ref, dx_ref, dg_ref, l_ref):
        xv, gv = x_ref[...], g_ref[...]
        e = xv * _rms_scale(xv) * gv - t_ref[...]
        part = 0.5 * jnp.sum(jnp.mean(e * e, axis=-1, keepdims=True), axis=0, keepdims=True)
        dx, dg = _rms_bwd_math(xv, gv, e * (1.0 / D))
        dx_ref[...] = dx

        @pl.when(pl.program_id(0) == 0)
        def _():
            dg_ref[...] = jnp.zeros_like(dg_ref)
            l_ref[...] = jnp.zeros_like(l_ref)

        dg_ref[...] += dg
        l_ref[...] += jnp.broadcast_to(part, l_ref.shape)

    row = pl.BlockSpec((tm, D), lambda i: (i, 0))
    vec = pl.BlockSpec((1, D), lambda i: (0, 0))
    return _call(
        body, name=name, grid=(T // tm,), in_specs=[row, vec, row],
        out_specs=[row, vec, pl.BlockSpec((1, LANES), lambda i: (0, 0))],
        out_shape=[jax.ShapeDtypeStruct((T, D), F32), jax.ShapeDtypeStruct((1, D), F32),
                   jax.ShapeDtypeStruct((1, LANES), F32)],
        compiler_params=_params(("arbitrary",)),
    )(x, g, target)


def _sigmoid(v):
    return 1.0 / (1.0 + jnp.exp(-v))


def _glu(blk):
    u = blk[:, :CONV_CH].astype(F32)
    gt = blk[:, CONV_CH:].astype(F32)
    return u * _sigmoid(gt)


def _fill_causal_ext(ext, cur_ref, halo_ref, s, ts):
    ext[pl.ds(HALO, ts), :] = _glu(cur_ref[0])
    hal = _glu(halo_ref[0])
    ext[pl.ds(0, HALO), :] = jnp.where(s > 0, hal, 0.0)


SUBLANES = 8


def _make_shifted(ext, sh):
    n = ext.shape[0]
    full = ext[...]
    for r in range(1, SUBLANES):
        sh[r - 1] = pltpu.roll(full, n - r, 0)


def _tap(ext, sh, off, ts):
    r = off % SUBLANES
    return ext[pl.ds(off, ts), :] if r == 0 else sh[r - 1, pl.ds(off - r, ts), :]


def _causal_conv(ext, sh, w_ref, ts):
    acc = jnp.zeros((ts, CONV_CH), F32)
    for j in range(CONV_K):
        acc = acc + _tap(ext, sh, HALO - (CONV_K - 1) + j, ts) * w_ref[pl.ds(j, 1), :]
    return acc


def _ln_stats(y):
    mu = jnp.mean(y, axis=-1, keepdims=True)
    yc = y - mu
    rstd = lax.rsqrt(jnp.mean(yc * yc, axis=-1, keepdims=True) + EPS)
    return yc * rstd, rstd


def _conv_specs(ts, S):
    nh = ts // HALO
    cur = pl.BlockSpec((1, ts, 2 * CONV_CH), lambda b, s: (b, s, 0))
    halo = pl.BlockSpec((1, HALO, 2 * CONV_CH), lambda b, s: (b, jnp.maximum(s * nh - 1, 0), 0))
    w = pl.BlockSpec((HALO, CONV_CH), lambda b, s: (0, 0))
    vec = pl.BlockSpec((1, CONV_CH), lambda b, s: (0, 0))
    return cur, halo, w, vec


def conv_branch_fwd(ug, conv_w, conv_b, ln_g, ln_b, *, name, ts=256, rider=None):
    B, S, _ = ug.shape
    ts = min(ts, S)
    ns = S // ts
    cur, halo, w, vec = _conv_specs(ts, S)

    def body(cur_ref, halo_ref, w_ref, cb_ref, lg_ref, lb_ref, o_ref, ot_ref, ext, sh):
        _fill_causal_ext(ext, cur_ref, halo_ref, pl.program_id(1), ts)
        _make_shifted(ext, sh)
        y = _causal_conv(ext, sh, w_ref, ts) + cb_ref[...]
        yh, _ = _ln_stats(y)
        ln = yh * lg_ref[...] + lb_ref[...]
        out = ln * _sigmoid(ln)
        o_ref[0] = out.astype(BF16)
        ot_ref[...] = out.T.astype(BF16)

    return hosted_call(
        body, rider, name=name, grid=(B, ns), in_specs=[cur, halo, w, vec, vec, vec],
        out_specs=[pl.BlockSpec((1, ts, CONV_CH), lambda b, s: (b, s, 0)),
                   pl.BlockSpec((CONV_CH, ts), lambda b, s: (0, b * ns + s))],
        out_shape=[jax.ShapeDtypeStruct((B, S, CONV_CH), BF16), jax.ShapeDtypeStruct((CONV_CH, B * S), BF16)],
        scratch_shapes=[pltpu.VMEM((ts + HALO, CONV_CH), F32),
                        pltpu.VMEM((SUBLANES - 1, ts + HALO, CONV_CH), F32)],
        args=(ug, ug, conv_w, conv_b, ln_g, ln_b),
    )


def conv_branch_bwd_a(ug, dcat, conv_w, conv_b, ln_g, ln_b, *, name, ts=256):
    B, S, _ = ug.shape
    ts = min(ts, S)
    cur, halo, w, vec = _conv_specs(ts, S)

    def body(cur_ref, halo_ref, d_ref, w_ref, cb_ref, lg_ref, lb_ref, dy_ref, dw_ref, dv_ref, ext, sh):
        _fill_causal_ext(ext, cur_ref, halo_ref, pl.program_id(1), ts)
        _make_shifted(ext, sh)
        y = _causal_conv(ext, sh, w_ref, ts) + cb_ref[...]
        yh, rstd = _ln_stats(y)
        lg = lg_ref[...]
        ln = yh * lg + lb_ref[...]
        sg = _sigmoid(ln)
        dln = d_ref[0].astype(F32) * (sg * (1.0 + ln * (1.0 - sg)))
        dyh = dln * lg
        dy = rstd * (dyh - jnp.mean(dyh, axis=-1, keepdims=True)
                     - yh * jnp.mean(dyh * yh, axis=-1, keepdims=True))
        dy_ref[0] = dy

        @pl.when((pl.program_id(0) == 0) & (pl.program_id(1) == 0))
        def _():
            dw_ref[...] = jnp.zeros_like(dw_ref)
            dv_ref[...] = jnp.zeros_like(dv_ref)

        dv_ref[pl.ds(0, 1), :] += jnp.sum(dy, axis=0, keepdims=True)
        dv_ref[pl.ds(1, 1), :] += jnp.sum(dln * yh, axis=0, keepdims=True)
        dv_ref[pl.ds(2, 1), :] += jnp.sum(dln, axis=0, keepdims=True)
        for j in range(CONV_K):
            tap = _tap(ext, sh, HALO - (CONV_K - 1) + j, ts)
            dw_ref[pl.ds(j, 1), :] += jnp.sum(dy * tap, axis=0, keepdims=True)

    return _call(
        body, name=name, grid=(B, S // ts),
        in_specs=[cur, halo, pl.BlockSpec((1, ts, CONV_CH), lambda b, s: (b, s, 0)), w, vec, vec, vec],
        out_specs=[pl.BlockSpec((1, ts, CONV_CH), lambda b, s: (b, s, 0)),
                   pl.BlockSpec((HALO, CONV_CH), lambda b, s: (0, 0)),
                   pl.BlockSpec((8, CONV_CH), lambda b, s: (0, 0))],
        out_shape=[jax.ShapeDtypeStruct((B, S, CONV_CH), F32),
                   jax.ShapeDtypeStruct((HALO, CONV_CH), F32),
                   jax.ShapeDtypeStruct((8, CONV_CH), F32)],
        scratch_shapes=[pltpu.VMEM((ts + HALO, CONV_CH), F32),
                        pltpu.VMEM((SUBLANES - 1, ts + HALO, CONV_CH), F32)],
        compiler_params=_params(("arbitrary", "arbitrary")),
    )(ug, ug, dcat, conv_w, conv_b, ln_g, ln_b)


def conv_branch_bwd_b(ug, dy, conv_w, *, name, ts=256):
    B, S, _ = ug.shape
    ts = min(ts, S)
    nh, n_halo = ts // HALO, S // HALO

    def body(cur_ref, dy_ref, nxt_ref, w_ref, o_ref, ext, sh):
        last = pl.program_id(1) == pl.num_programs(1) - 1
        ext[pl.ds(0, ts), :] = dy_ref[0]
        ext[pl.ds(ts, HALO), :] = jnp.where(last, 0.0, nxt_ref[0])
        _make_shifted(ext, sh)
        da = jnp.zeros((ts, CONV_CH), F32)
        for j in range(CONV_K):
            da = da + _tap(ext, sh, CONV_K - 1 - j, ts) * w_ref[pl.ds(j, 1), :]
        blk = cur_ref[0]
        u = blk[:, :CONV_CH].astype(F32)
        sg = _sigmoid(blk[:, CONV_CH:].astype(F32))
        o_ref[0, :, :CONV_CH] = (da * sg).astype(BF16)
        o_ref[0, :, CONV_CH:] = (da * u * sg * (1.0 - sg)).astype(BF16)

    return _call(
        body, name=name, grid=(B, S // ts),
        in_specs=[pl.BlockSpec((1, ts, 2 * CONV_CH), lambda b, s: (b, s, 0)),
                  pl.BlockSpec((1, ts, CONV_CH), lambda b, s: (b, s, 0)),
                  pl.BlockSpec((1, HALO, CONV_CH), lambda b, s: (b, jnp.minimum((s + 1) * nh, n_halo - 1), 0)),
                  pl.BlockSpec((HALO, CONV_CH), lambda b, s: (0, 0))],
        out_specs=pl.BlockSpec((1, ts, 2 * CONV_CH), lambda b, s: (b, s, 0)),
        out_shape=jax.ShapeDtypeStruct((B, S, 2 * CONV_CH), BF16),
        scratch_shapes=[pltpu.VMEM((ts + HALO, CONV_CH), F32),
                        pltpu.VMEM((SUBLANES - 1, ts + HALO, CONV_CH), F32)],
        compiler_params=_params(("parallel", "parallel")),
    )(ug, dy, dy, conv_w)


def _tri(n, lower):
    r = lax.broadcasted_iota(jnp.int32, (n, n), 0)
    c = lax.broadcasted_iota(jnp.int32, (n, n), 1)
    return ((r >= c) if lower else (r <= c)).astype(F32)


def _eye(n):
    r = lax.broadcasted_iota(jnp.int32, (n, n), 0)
    c = lax.broadcasted_iota(jnp.int32, (n, n), 1)
    return (r == c).astype(F32)


def _dot_hi(a, b, dn):
    return lax.dot_general(a, b, dn, precision=lax.Precision.HIGHEST, preferred_element_type=F32)


NN = (((1,), (0,)), ((), ()))
NT = (((1,), (1,)), ((), ()))
TN = (((0,), (0,)), ((), ()))


def _log_sigmoid(v):
    e = jnp.exp(-jnp.abs(v))
    log1p_e = jnp.where(e < 1e-3, e * (1.0 - 0.5 * e), jnp.log(1.0 + e))
    return jnp.minimum(v, 0.0) - log1p_e


def fgate_fwd(h, w_f, b_f, *, name, ts=256, rider=None):
    B, S, D = h.shape
    ts = min(ts, S)

    def body(h_ref, w_ref, b_ref, f_ref, cc_ref, cr_ref, carry):
        @pl.when(pl.program_id(1) == 0)
        def _():
            carry[...] = jnp.zeros_like(carry)

        f = jnp.dot(h_ref[0], w_ref[...], preferred_element_type=F32)
        f_ref[0] = f
        logf = _log_sigmoid(f + b_ref[...])
        c = _dot_hi(_tri(ts, True), logf, NN) + carry[pl.ds(0, 1), :]
        cc_ref[0] = c
        carry[pl.ds(0, 1), :] = c[ts - 1:ts, :]
        cr_ref[0] = _dot_hi(_eye(LANES), c, NT)

    return hosted_call(
        body, rider, name=name, grid=(B, S // ts),
        in_specs=[pl.BlockSpec((1, ts, D), lambda b, s: (b, s, 0)),
                  pl.BlockSpec((D, LANES), lambda b, s: (0, 0)),
                  pl.BlockSpec((1, LANES), lambda b, s: (0, 0))],
        out_specs=[pl.BlockSpec((1, ts, LANES), lambda b, s: (b, s, 0)),
                   pl.BlockSpec((1, ts, LANES), lambda b, s: (b, s, 0)),
                   pl.BlockSpec((1, LANES, ts), lambda b, s: (b, 0, s))],
        out_shape=[jax.ShapeDtypeStruct((B, S, LANES), F32), jax.ShapeDtypeStruct((B, S, LANES), F32),
                   jax.ShapeDtypeStruct((B, LANES, S), F32)],
        scratch_shapes=[pltpu.VMEM((8, LANES), F32)],
        args=(h, w_f, b_f),
    )


def fgate_bwd(dc, f, b_f, *, name, ts=256):
    B, S, _ = f.shape
    P = dc.shape[1]
    ts = min(ts, S)
    ns = S // ts

    def body(dc_ref, f_ref, b_ref, df_ref, db_ref, carry):
        @pl.when(pl.program_id(1) == 0)
        def _():
            carry[...] = jnp.zeros_like(carry)

        @pl.when((pl.program_id(0) == 0) & (pl.program_id(1) == 0))
        def _():
            db_ref[...] = jnp.zeros_like(db_ref)

        dc_t = dc_ref[0, 0]
        for j in range(1, P):
            dc_t = dc_t + dc_ref[0, j]
        dlogf = _dot_hi(_tri(ts, False), dc_t, NN) + carry[pl.ds(0, 1), :]
        carry[pl.ds(0, 1), :] = dlogf[0:1, :]
        df = dlogf * _sigmoid(-(f_ref[0] + b_ref[...]))
        df_ref[0] = df.astype(BF16)
        db_ref[...] += jnp.sum(df, axis=0, keepdims=True)

    return _call(
        body, name=name, grid=(B, ns),
        in_specs=[pl.BlockSpec((1, P, ts, LANES), lambda b, s: (b, 0, ns - 1 - s, 0)),
                  pl.BlockSpec((1, ts, LANES), lambda b, s: (b, ns - 1 - s, 0)),
                  pl.BlockSpec((1, LANES), lambda b, s: (0, 0))],
        out_specs=[pl.BlockSpec((1, ts, LANES), lambda b, s: (b, ns - 1 - s, 0)),
                   pl.BlockSpec((1, LANES), lambda b, s: (0, 0))],
        out_shape=[jax.ShapeDtypeStruct((B, S, LANES), BF16), jax.ShapeDtypeStruct((1, LANES), F32)],
        scratch_shapes=[pltpu.VMEM((8, LANES), F32)],
        compiler_params=_params(("arbitrary", "arbitrary")),
    )(dc, f, b_f)


def _lane_pick(tile, idx):
    lane = lax.broadcasted_iota(jnp.int32, tile.shape, 1)
    return jnp.sum(jnp.where(lane == idx, tile, 0.0), axis=-1, keepdims=True)


FOX_T = 512


def _fox_heads(q, cc_ref, p):
    lane = lax.broadcasted_iota(jnp.int32, q.shape, 1)
    qs = q * (1.0 / math.sqrt(FOX_HEAD_DIM))
    qhs = [jnp.where((lane < FOX_HEAD_DIM) == (hh == 0), qs, jnp.zeros_like(qs)) for hh in range(2)]
    crefs = [_lane_pick(cc_ref[0, pl.ds(0, 1), :], 2 * p + hh) for hh in range(2)]
    return qhs, crefs


def _fold_lanes(x, op):
    out = x[:, :LANES]
    for j in range(1, x.shape[1] // LANES):
        out = op(out, x[:, j * LANES:(j + 1) * LANES])
    return out


def _causal(t, transposed):
    r = lax.broadcasted_iota(jnp.int32, (t, t), 0)
    c = lax.broadcasted_iota(jnp.int32, (t, t), 1)
    return (r <= c) if transposed else (c <= r)


QKV0 = 8


def fox_fwd(z, c_col, c_row, *, name, rider=None):
    B, S, _ = z.shape
    assert S % FOX_T == 0
    tq, nq = FOX_T, S // FOX_T
    npair = FOX_HEADS // 2

    def body(q_ref, k_ref, v_ref, cc_ref, cr_ref, o_ref, l_ref, ot_ref, s_scr, m_scr, acc_scr):
        p, qi = pl.program_id(1), pl.program_id(2)
        qhs, crefs = _fox_heads(q_ref[0], cc_ref, p)
        lane = lax.broadcasted_iota(jnp.int32, (tq, LANES), 1)
        first = lane < FOX_HEAD_DIM
        for hh in range(2):
            m_scr[hh] = jnp.full((tq, LANES), NEG, F32)
            acc_scr[hh] = jnp.zeros((tq, LANES), F32)

        def logits(kb, diagonal):
            k0 = pl.multiple_of(kb * tq, tq)
            k = k_ref[0, pl.ds(k0, tq), :]
            for hh in range(2):
                s = lax.dot_general(qhs[hh], k, NT, preferred_element_type=F32)
                s = s + (crefs[hh] - cr_ref[0, pl.ds(2 * p + hh, 1), pl.ds(k0, tq)])
                if diagonal:
                    s = jnp.where(_causal(tq, False), s, NEG)
                s_scr[hh, kb] = s
                m_scr[hh] = jnp.maximum(m_scr[hh], _fold_lanes(s, jnp.maximum))

        def sweep1(kb, carry):
            logits(kb, False)
            return carry

        lax.fori_loop(0, qi, sweep1, 0)
        logits(qi, True)
        ms = [jnp.max(m_scr[hh], axis=-1, keepdims=True) for hh in range(2)]
        mbs = [jnp.broadcast_to(ms[hh], (tq, tq)) for hh in range(2)]

        for hh in range(2):
            m_scr[hh] = jnp.zeros((tq, LANES), F32)

        def weigh(kb, carry):
            k0 = pl.multiple_of(kb * tq, tq)
            v = v_ref[0, pl.ds(k0, tq), :]
            for hh in range(2):
                pr = jnp.exp(s_scr[hh, kb] - mbs[hh])
                m_scr[hh] += _fold_lanes(pr, jnp.add)
                acc_scr[hh] += jnp.dot(pr.astype(BF16), v, preferred_element_type=F32)
            return carry

        lax.fori_loop(0, qi + 1, weigh, 0)
        accs = [acc_scr[hh] for hh in range(2)]
        ls = [jnp.sum(m_scr[hh], axis=-1, keepdims=True) for hh in range(2)]
        out = jnp.where(first, accs[0] / ls[0], accs[1] / ls[1])
        o_ref[0] = out.astype(BF16)
        ot_ref[...] = out.T.astype(BF16)
        l_ref[0, 0] = jnp.where(first, ms[0] + jnp.log(ls[0]), ms[1] + jnp.log(ls[1]))

    return hosted_call(
        body, rider, name=name, grid=(B, npair, nq),
        in_specs=[pl.BlockSpec((1, tq, LANES), lambda b, p, i: (b, i, QKV0 + p)),
                  pl.BlockSpec((1, S, LANES), lambda b, p, i: (b, 0, QKV0 + npair + p)),
                  pl.BlockSpec((1, S, LANES), lambda b, p, i: (b, 0, QKV0 + 2 * npair + p)),
                  pl.BlockSpec((1, tq, LANES), lambda b, p, i: (b, i, 0)),
                  pl.BlockSpec((1, 8, S), lambda b, p, i: (b, 0, 0))],
        out_specs=[pl.BlockSpec((1, tq, LANES), lambda b, p, i: (b, i, p)),
                   pl.BlockSpec((1, 1, tq, LANES), lambda b, p, i: (b, p, i, 0)),
                   pl.BlockSpec((LANES, tq), lambda b, p, i: (p, b * nq + i))],
        out_shape=[jax.ShapeDtypeStruct((B, S, FOX_W), BF16),
                   jax.ShapeDtypeStruct((B, npair, S, LANES), F32),
                   jax.ShapeDtypeStruct((FOX_W, B * S), BF16)],
        scratch_shapes=[pltpu.VMEM((2, nq, tq, tq), F32), pltpu.VMEM((2, tq, LANES), F32),
                        pltpu.VMEM((2, tq, LANES), F32)],
        args=(z, z, z, c_col, c_row),
    )


def fox_bwd_dq(z, dcat, lse, c_col, c_row, *, name, rider=None):
    B, S, _ = z.shape
    tq, nq = FOX_T, S // FOX_T
    npair = FOX_HEADS // 2

    def body(q_ref, k_ref, v_ref, do_ref, l_ref, cc_ref, cr_ref, dq_ref, st_ref, p_scr, dp_scr, dl_scr):
        p, qi = pl.program_id(1), pl.program_id(2)
        qhs, crefs = _fox_heads(q_ref[0], cc_ref, p)
        lane = lax.broadcasted_iota(jnp.int32, (tq, LANES), 1)
        do_b = do_ref[0].astype(BF16)
        dohs = [jnp.where((lane < FOX_HEAD_DIM) == (hh == 0), do_b, jnp.zeros_like(do_b)) for hh in range(2)]
        lses = [_lane_pick(l_ref[0, 0], hh * FOX_HEAD_DIM) for hh in range(2)]
        lbs = [jnp.broadcast_to(lses[hh], (tq, tq)) for hh in range(2)]
        for hh in range(2):
            dl_scr[hh] = jnp.zeros((tq, LANES), F32)

        def probs(kb, diagonal):
            k0 = pl.multiple_of(kb * tq, tq)
            k = k_ref[0, pl.ds(k0, tq), :]
            v = v_ref[0, pl.ds(k0, tq), :]
            for hh in range(2):
                s = lax.dot_general(qhs[hh], k, NT, preferred_element_type=F32)
                s = s + (crefs[hh] - cr_ref[0, pl.ds(2 * p + hh, 1), pl.ds(k0, tq)])
                pr = jnp.exp(s - lbs[hh])
                if diagonal:
                    pr = jnp.where(_causal(tq, False), pr, 0.0)
                dp = lax.dot_general(dohs[hh], v, NT, preferred_element_type=F32)
                pdp = pr * dp
                dl_scr[hh] += _fold_lanes(pdp, jnp.add)
                p_scr[hh, kb] = pr
                dp_scr[hh, kb] = dp

        def first_pass(kb, carry):
            probs(kb, False)
            return carry

        lax.fori_loop(0, qi, first_pass, 0)
        probs(qi, True)

        dls = [jnp.sum(dl_scr[hh], axis=-1, keepdims=True) for hh in range(2)]
        dlbs = [jnp.broadcast_to(dls[hh], (tq, tq)) for hh in range(2)]

        def second_pass(kb, dq):
            k0 = pl.multiple_of(kb * tq, tq)
            k = k_ref[0, pl.ds(k0, tq), :]
            for hh in range(2):
                ds = p_scr[hh, kb] * (dp_scr[hh, kb] - dlbs[hh])
                kh = jnp.where((lane < FOX_HEAD_DIM) == (hh == 0), k, jnp.zeros_like(k))
                dq = dq + jnp.dot(ds.astype(BF16), kh, preferred_element_type=F32)
            return dq

        dq = lax.fori_loop(0, qi + 1, second_pass, jnp.zeros((tq, LANES), F32))
        dq_ref[0] = (dq * (1.0 / math.sqrt(FOX_HEAD_DIM))).astype(BF16)
        cols = jnp.zeros((tq, LANES), F32)
        for j, col in enumerate([crefs[0] - lses[0], crefs[1] - lses[1], dls[0], dls[1]]):
            cols = jnp.where(lane == j, col, cols)
        st_ref[0, 0] = _dot_hi(_eye(LANES), cols, NT)[:8]

    return hosted_call(
        body, rider, name=name, grid=(B, npair, nq),
        in_specs=[pl.BlockSpec((1, tq, LANES), lambda b, p, i: (b, i, QKV0 + p)),
                  pl.BlockSpec((1, S, LANES), lambda b, p, i: (b, 0, QKV0 + npair + p)),
                  pl.BlockSpec((1, S, LANES), lambda b, p, i: (b, 0, QKV0 + 2 * npair + p)),
                  pl.BlockSpec((1, tq, LANES), lambda b, p, i: (b, i, npair + p)),
                  pl.BlockSpec((1, 1, tq, LANES), lambda b, p, i: (b, p, i, 0)),
                  pl.BlockSpec((1, tq, LANES), lambda b, p, i: (b, i, 0)),
                  pl.BlockSpec((1, 8, S), lambda b, p, i: (b, 0, 0))],
        out_specs=[pl.BlockSpec((1, tq, LANES), lambda b, p, i: (b, i, p)),
                   pl.BlockSpec((1, 1, 8, tq), lambda b, p, i: (b, p, 0, i))],
        out_shape=[jax.ShapeDtypeStruct((B, S, FOX_W), BF16), jax.ShapeDtypeStruct((B, npair, 8, S), F32)],
        scratch_shapes=[pltpu.VMEM((2, nq, tq, tq), F32), pltpu.VMEM((2, nq, tq, tq), F32),
                        pltpu.VMEM((2, tq, LANES), F32)],
        args=(z, z, z, dcat, lse, c_col, c_row), vmem=56 << 20,
    )


def fox_bwd_dkdv(z, dcat, stats, c_col, *, name, rider=None):
    B, S, _ = z.shape
    tk, nq = FOX_T, S // FOX_T
    npair = FOX_HEADS // 2
    inv = 1.0 / math.sqrt(FOX_HEAD_DIM)

    def body(q_ref, k_ref, v_ref, do_ref, st_ref, cc_ref, dk_ref, dv_ref, dc_ref, dk_scr, dv_scr, dc_scr):
        p, kt = pl.program_id(1), pl.program_id(2)
        lane = lax.broadcasted_iota(jnp.int32, (tk, LANES), 1)
        masks = [(lane < FOX_HEAD_DIM) == (hh == 0) for hh in range(2)]
        k = k_ref[0]
        v = v_ref[0]
        khs = [jnp.where(masks[hh], k, jnp.zeros_like(k)) for hh in range(2)]
        vhs = [jnp.where(masks[hh], v, jnp.zeros_like(v)) for hh in range(2)]
        ccbs = [jnp.broadcast_to(_lane_pick(cc_ref[0], 2 * p + hh), (tk, tk)) for hh in range(2)]
        dk_scr[...] = jnp.zeros_like(dk_scr)
        dv_scr[...] = jnp.zeros_like(dv_scr)
        dc_scr[...] = jnp.zeros_like(dc_scr)

        def tile(qb, diagonal):
            q0 = pl.multiple_of(qb * tk, tk)
            qs = q_ref[0, pl.ds(q0, tk), :] * inv
            do_b = do_ref[0, pl.ds(q0, tk), :].astype(BF16)
            for hh in range(2):
                st = lax.dot_general(khs[hh], qs, NT, preferred_element_type=F32)
                pr = jnp.exp(st - ccbs[hh] + st_ref[0, 0, pl.ds(hh, 1), pl.ds(q0, tk)])
                if diagonal:
                    pr = jnp.where(_causal(tk, True), pr, 0.0)
                dp = lax.dot_general(vhs[hh], do_b, NT, preferred_element_type=F32)
                ds = pr * (dp - st_ref[0, 0, pl.ds(2 + hh, 1), pl.ds(q0, tk)])
                dv_scr[...] += jnp.dot(pr.astype(BF16), jnp.where(masks[hh], do_b, jnp.zeros_like(do_b)),
                                       preferred_element_type=F32)
                dk_scr[...] += jnp.dot(ds.astype(BF16), jnp.where(masks[hh], qs, jnp.zeros_like(qs)),
                                       preferred_element_type=F32)
                dc_scr[hh] -= _fold_lanes(ds, jnp.add)

        def later(qb, carry):
            tile(qb, False)
            return carry

        tile(kt, True)
        lax.fori_loop(kt + 1, nq, later, 0)
        dk_ref[0] = dk_scr[...].astype(BF16)
        dv_ref[0] = dv_scr[...].astype(BF16)
        dcs = [jnp.sum(dc_scr[hh], axis=-1, keepdims=True) for hh in range(2)]
        dc_ref[0, 0] = jnp.where(lane == 2 * p, dcs[0], jnp.where(lane == 2 * p + 1, dcs[1], 0.0))

    full = lambda col: pl.BlockSpec((1, S, LANES), col)
    tile_spec = lambda col: pl.BlockSpec((1, tk, LANES), col)
    return hosted_call(
        body, rider, name=name, grid=(B, npair, nq),
        in_specs=[full(lambda b, p, t: (b, 0, QKV0 + p)),
                  tile_spec(lambda b, p, t: (b, t, QKV0 + npair + p)),
                  tile_spec(lambda b, p, t: (b, t, QKV0 + 2 * npair + p)),
                  full(lambda b, p, t: (b, 0, npair + p)),
                  pl.BlockSpec((1, 1, 8, S), lambda b, p, t: (b, p, 0, 0)),
                  tile_spec(lambda b, p, t: (b, t, 0))],
        out_specs=[tile_spec(lambda b, p, t: (b, t, p)), tile_spec(lambda b, p, t: (b, t, p)),
                   pl.BlockSpec((1, 1, tk, LANES), lambda b, p, t: (b, p, t, 0))],
        out_shape=[jax.ShapeDtypeStruct((B, S, FOX_W), BF16)] * 2
        + [jax.ShapeDtypeStruct((B, npair, S, LANES), F32)],
        scratch_shapes=[pltpu.VMEM((tk, LANES), F32), pltpu.VMEM((tk, LANES), F32),
                        pltpu.VMEM((2, tk, LANES), F32)],
        args=(z, z, z, dcat, stats, c_col),
    )


def xattn_fwd(qm, kv, *, name, tq=512):
    B, S, D = qm.shape
    M = kv.shape[1]
    tq = min(tq, S)
    inv = 1.0 / math.sqrt(MEM_HEAD_DIM)

    nq = S // tq

    def body(q_ref, kv_ref, o_ref, ot_ref):
        for h in range(MEM_HEADS):
            c0 = h * MEM_HEAD_DIM
            qh = q_ref[0, :, c0:c0 + MEM_HEAD_DIM]
            kh = kv_ref[0, :, c0:c0 + MEM_HEAD_DIM]
            vh = kv_ref[0, :, D + c0:D + c0 + MEM_HEAD_DIM]
            s = lax.dot_general(qh, kh, NT, preferred_element_type=F32) * inv
            e = jnp.exp(s - jnp.max(s, axis=-1, keepdims=True))
            o = jnp.dot(e.astype(BF16), vh, preferred_element_type=F32) / jnp.sum(e, axis=-1, keepdims=True)
            o_ref[0, :, c0:c0 + MEM_HEAD_DIM] = o.astype(BF16)
            ot_ref[c0:c0 + MEM_HEAD_DIM, :] = o.T.astype(BF16)

    return _call(
        body, name=name, grid=(B, nq),
        in_specs=[pl.BlockSpec((1, tq, D), lambda b, i: (b, i, 0)),
                  pl.BlockSpec((1, M, 2 * D), lambda b, i: (b, 0, 0))],
        out_specs=[pl.BlockSpec((1, tq, D), lambda b, i: (b, i, 0)),
                   pl.BlockSpec((D, tq), lambda b, i: (0, b * nq + i))],
        out_shape=[jax.ShapeDtypeStruct((B, S, D), BF16), jax.ShapeDtypeStruct((D, B * S), BF16)],
        compiler_params=_params(("parallel", "parallel")),
    )(qm, kv)


def xattn_bwd(qm, kv, do, *, name, tq=512):
    B, S, D = qm.shape
    M = kv.shape[1]
    tq = min(tq, S)
    inv = 1.0 / math.sqrt(MEM_HEAD_DIM)

    def body(q_ref, kv_ref, do_ref, dq_ref, dkv_ref):
        @pl.when(pl.program_id(1) == 0)
        def _():
            dkv_ref[...] = jnp.zeros_like(dkv_ref)

        for h in range(MEM_HEADS):
            c0 = h * MEM_HEAD_DIM
            qh = q_ref[0, :, c0:c0 + MEM_HEAD_DIM]
            kh = kv_ref[0, :, c0:c0 + MEM_HEAD_DIM]
            vh = kv_ref[0, :, D + c0:D + c0 + MEM_HEAD_DIM]
            doh = do_ref[0, :, c0:c0 + MEM_HEAD_DIM]
            s = lax.dot_general(qh, kh, NT, preferred_element_type=F32) * inv
            e = jnp.exp(s - jnp.max(s, axis=-1, keepdims=True))
            pr = e / jnp.sum(e, axis=-1, keepdims=True)
            dp = lax.dot_general(doh, vh, NT, preferred_element_type=F32)
            ds = pr * (dp - jnp.sum(pr * dp, axis=-1, keepdims=True))
            ds_b = ds.astype(BF16)
            dq_ref[0, :, c0:c0 + MEM_HEAD_DIM] = (jnp.dot(ds_b, kh, preferred_element_type=F32) * inv).astype(BF16)
            dkv_ref[0, :, c0:c0 + MEM_HEAD_DIM] += lax.dot_general(ds_b, qh, TN, preferred_element_type=F32) * inv
            dkv_ref[0, :, D + c0:D + c0 + MEM_HEAD_DIM] += lax.dot_general(
                pr.astype(BF16), doh, TN, preferred_element_type=F32)

    row = pl.BlockSpec((1, tq, D), lambda b, i: (b, i, 0))
    kvs = pl.BlockSpec((1, M, 2 * D), lambda b, i: (b, 0, 0))
    return _call(
        body, name=name, grid=(B, S // tq), in_specs=[row, kvs, row], out_specs=[row, kvs],
        out_shape=[jax.ShapeDtypeStruct((B, S, D), BF16), jax.ShapeDtypeStruct((B, M, 2 * D), F32)],
        compiler_params=_params(("parallel", "arbitrary")),
    )(qm, kv, do)


SWIGLU_TN = 1408


def mm_swiglu_fwd(hf, w_gu, *, name, tm=512):
    T, D = hf.shape
    Fh = w_gu.shape[1] // 2
    tm, tn = min(tm, T), SWIGLU_TN
    nj = Fh // tn
    assert Fh % tn == 0 and T % tm == 0

    def body(a_ref, bg_ref, bu_ref, g_ref, u_ref, o_ref, ot_ref):
        a = a_ref[...]
        g = jnp.dot(a, bg_ref[...], preferred_element_type=F32)
        u = jnp.dot(a, bu_ref[...], preferred_element_type=F32)
        act = g * _sigmoid(g) * u
        g_ref[...] = g.astype(BF16)
        u_ref[...] = u.astype(BF16)
        o_ref[...] = act.astype(BF16)
        ot_ref[...] = act.T.astype(BF16)

    tile = pl.BlockSpec((tm, tn), lambda i, j: (i, j))
    return _call(
        body, name=name, grid=(T // tm, nj),
        in_specs=[pl.BlockSpec((tm, D), lambda i, j: (i, 0)), pl.BlockSpec((D, tn), lambda i, j: (0, j)),
                  pl.BlockSpec((D, tn), lambda i, j: (0, nj + j))],
        out_specs=[tile, tile, tile, pl.BlockSpec((tn, tm), lambda i, j: (j, i))],
        out_shape=[jax.ShapeDtypeStruct((T, Fh), BF16)] * 3 + [jax.ShapeDtypeStruct((Fh, T), BF16)],
        compiler_params=_params(("parallel", "parallel"), 48 << 20),
    )(hf, w_gu, w_gu)


def mm_swiglu_bwd(dx, w_down, g, u, *, name, tm=512):
    T, D = dx.shape
    Fh = w_down.shape[0]
    tm, tn = min(tm, T), SWIGLU_TN
    assert Fh % tn == 0 and T % tm == 0

    def body(a_ref, b_ref, g_ref, u_ref, dg_ref, du_ref):
        d = lax.dot_general(a_ref[...].astype(BF16), b_ref[...], NT, preferred_element_type=F32)
        gv = g_ref[...].astype(F32)
        uv = u_ref[...].astype(F32)
        sg = _sigmoid(gv)
        dg_ref[...] = (d * uv * (sg * (1.0 + gv * (1.0 - sg)))).astype(BF16)
        du_ref[...] = (d * gv * sg).astype(BF16)

    tile = pl.BlockSpec((tm, tn), lambda i, j: (i, j))
    return _call(
        body, name=name, grid=(T // tm, Fh // tn),
        in_specs=[pl.BlockSpec((tm, D), lambda i, j: (i, 0)), pl.BlockSpec((tn, D), lambda i, j: (j, 0)), tile, tile],
        out_specs=[tile, tile],
        out_shape=[jax.ShapeDtypeStruct((T, Fh), BF16)] * 2,
        compiler_params=_params(("parallel", "parallel"), 48 << 20),
    )(dx, w_down, g, u)


LATE_MID = ("w_out", "w_mq", "w_mo")
LATE_KV = ("w_mkv",)
LATE_FFN = ("w_gu", "w_down")
LATE = LATE_MID + LATE_KV + LATE_FFN
RS_GROUPS = (("w_gu", "w_down"), ("w_out", "w_mq", "w_mkv", "w_mo"), ("w_in",))


def pair_sums(names, g42, got):
    return {n: pair_sum(g, o, name="rs_pair_sum_" + n) for n, g, o in zip(names, g42, got)}


def local_step(x, mem, target, sp, first_shards, late_shards):
    B, S, D = x.shape
    T = B * S
    M = mem.shape[1]
    row = lambda v: v.reshape(1, -1).astype(F32)
    g_mix, g_x, g_mem, g_ffn, g_final = (row(sp[k]) for k in ("g_mix", "g_x", "g_mem", "g_ffn", "g_final"))
    conv_b, ln_g, ln_b = row(sp["conv_b"]), row(sp["ln_g"]), row(sp["ln_b"])
    b_f = jnp.pad(row(sp["b_f"]), ((0, 0), (0, LANES - FOX_HEADS)))
    n_ug, n_main = 2 * CONV_CH, 2 * CONV_CH + 3 * FOX_W

    x2d = x.reshape(T, D)
    h, h_t, partly = rmsnorm_fwd(x2d, g_mix, name="rms_mix", rider=AllGatherStage1(first_shards))
    w_in8, cw8 = all_gather_stage2(partly, name="ag_first_stage2")
    w_in_full = _full_from_gathered("w_in", w_in8)
    conv_w = cw8.transpose(1, 0, 2).reshape(HALO, -1)
    w_main, w_ug, w_qkv = w_in_full[:, :n_main], w_in_full[:, :n_ug], w_in_full[:, n_ug:n_main]
    w_f = jnp.pad(w_in_full[:, n_main:], ((0, 0), (0, LANES - FOX_HEADS)))
    z = matmul(h, w_main, out_dtype=BF16, tn=n_main, name="mm_in")
    z3 = z.reshape(B, S, n_main)
    n_mid, n_kv = len(LATE_MID), len(LATE_MID) + len(LATE_KV)
    (conv_out, conv_t), partly_mid = conv_branch_fwd(z3, conv_w, conv_b, ln_g, ln_b, name="conv_fwd",
                                                     rider=AllGatherStage1(late_shards[:n_mid]))
    (f_raw, c_col, c_row), partly_kv = fgate_fwd(h.reshape(B, S, D), w_f, b_f, name="fgate_fwd",
                                                 rider=AllGatherStage1(late_shards[n_mid:n_kv]))
    (att, lse, att_t), partly_ffn = fox_fwd(z3, c_col, c_row, name="fox_fwd",
                                            rider=AllGatherStage1(late_shards[n_kv:]))
    gathered = all_gather_stage2(partly_mid + partly_kv + partly_ffn, name="ag_late_stage2")
    wf = {n: _full_from_gathered(n, blk) for n, blk in zip(LATE, gathered)}
    x1 = matmul([conv_out.reshape(T, CONV_CH), att.reshape(T, FOX_W)],
                [wf["w_out"][:CONV_CH], wf["w_out"][CONV_CH:]], out_dtype=F32, res=x2d, tn=D, name="mm_out")
    hx, hx_t = rmsnorm_fwd(x1, g_x, name="rms_x")
    qm = matmul(hx, wf["w_mq"], out_dtype=BF16, tn=D, name="mm_mq")
    mem2d = mem.reshape(B * M, D)
    mem_n, mem_n_t = rmsnorm_fwd(mem2d, g_mem, name="rms_mem")
    kv = matmul(mem_n, wf["w_mkv"], out_dtype=BF16, tn=2 * D, name="mm_mkv").reshape(B, M, 2 * D)
    o, o_t = xattn_fwd(qm.reshape(B, S, D), kv, name="xattn_fwd")
    o = o.reshape(T, D)
    x2 = matmul(o, wf["w_mo"], out_dtype=F32, res=x1, tn=D, name="mm_mo")
    hf, hf_t = rmsnorm_fwd(x2, g_ffn, name="rms_ffn")
    gate, up, act, act_t = mm_swiglu_fwd(hf, wf["w_gu"], name="mm_gu")
    d_ff = gate.shape[1]
    x3 = matmul(act, wf["w_down"], out_dtype=F32, res=x2, tn=D, name="mm_down")
    dx3, dg_final, loss = final_loss_bwd(x3, g_final, target.reshape(T, D), name="loss_bwd")
    gw = {}
    gw["w_down"] = matmul(act_t, dx3, out_dtype=BF16, tm=1408, tn=256, name="dw_down")
    dgate, dup = mm_swiglu_bwd(dx3, wf["w_down"], gate, up, name="dx_down")
    gw["w_gu"] = [matmul(hf_t, dgate, out_dtype=BF16, tn=SWIGLU_TN, name="dw_gate"),
                  matmul(hf_t, dup, out_dtype=BF16, tn=SWIGLU_TN, name="dw_up")]
    g42 = [_shards_from_full(n, gw[n]) for n in RS_GROUPS[0]]
    dhf, got = matmul([dgate, dup], [wf["w_gu"][:, :d_ff], wf["w_gu"][:, d_ff:]], tb=True, out_dtype=BF16,
                      tm=256, tn=D, name="dx_gu", rider=SiblingExchange(g42))
    parts = pair_sums(RS_GROUPS[0], g42, got)
    dx2, dg_ffn = rmsnorm_bwd(x2, g_ffn, dhf, dx3, name="rms_ffn_bwd")
    gw["w_mo"] = matmul(o_t, dx2, out_dtype=BF16, name="dw_mo")
    do = matmul(dx2, wf["w_mo"], tb=True, out_dtype=BF16, tn=D, name="dx_mo")
    dqm, dkv = xattn_bwd(qm.reshape(B, S, D), kv, do.reshape(B, S, D), name="xattn_bwd")
    dqm = dqm.reshape(T, D)
    dkv = dkv.reshape(B * M, 2 * D)
    gw["w_mq"] = matmul(hx_t, dqm, out_dtype=BF16, tn=D, name="dw_mq")
    dhx = matmul(dqm, wf["w_mq"], tb=True, out_dtype=BF16, tn=D, name="dx_mq")
    gw["w_mkv"] = matmul(mem_n_t, dkv, out_dtype=BF16, tn=D, name="dw_mkv")
    dmem_n = matmul(dkv, wf["w_mkv"], tb=True, out_dtype=BF16, tn=D, name="dx_mkv")
    _, dg_mem = rmsnorm_bwd(mem2d, g_mem, dmem_n, None, name="rms_mem_bwd")
    dx1, dg_x = rmsnorm_bwd(x1, g_x, dhx, dx2, name="rms_x_bwd")
    gw["w_out"] = jnp.concatenate([matmul(conv_t, dx1, out_dtype=BF16, name="dw_out_conv"),
                                   matmul(att_t, dx1, out_dtype=BF16, name="dw_out_att")], axis=0)
    g42 = [_shards_from_full(n, gw[n]) for n in RS_GROUPS[1]]
    dcat, got = matmul(dx1, wf["w_out"], tb=True, out_dtype=BF16, tn=D, name="dx_out", rider=SiblingExchange(g42))
    dcat = dcat.reshape(B, S, D)
    parts.update(pair_sums(RS_GROUPS[1], g42, got))
    dy, dconv_w, dvec = conv_branch_bwd_a(z3, dcat, conv_w, conv_b, ln_g, ln_b, name="conv_bwd_a")
    dug = conv_branch_bwd_b(z3, dy, conv_w, name="conv_bwd_b")
    gots = {}
    (dq, stats), got = fox_bwd_dq(z3, dcat, lse, c_col, c_row, name="fox_bwd_dq",
                                  rider=ChipExchange([parts[n] for n in RS_GROUPS[0]]))
    gots.update(zip(RS_GROUPS[0], got))
    (dk, dv, dc), got = fox_bwd_dkdv(z3, dcat, stats, c_col, name="fox_bwd_dkdv",
                                     rider=ChipExchange([parts[n] for n in RS_GROUPS[1]]))
    gots.update(zip(RS_GROUPS[1], got))
    df, db_f = fgate_bwd(dc, f_raw, b_f, name="fgate_bwd")
    dug2 = dug.reshape(T, n_ug)
    dqkv = jnp.concatenate([dq, dk, dv], axis=-1).reshape(T, 3 * FOX_W)
    df2 = df.reshape(T, LANES)
    dw_in = [matmul(h_t, dug2, out_dtype=BF16, tn=n_ug, name="dw_in_ug"),
             matmul(h_t, dqkv, out_dtype=BF16, tn=3 * FOX_W, name="dw_in_qkv"),
             matmul(h_t, df2, out_dtype=BF16, name="dw_f")[:, :FOX_HEADS]]
    g42 = [_shards_from_full("w_in", dw_in)]
    parts.update(pair_sums(RS_GROUPS[2], g42, run_rider(SiblingExchange(g42), name="rs_sibling_in")))
    dh, (gots["w_in"],) = matmul([dug2, dqkv, df2], [w_ug, w_qkv, w_f], tb=True, out_dtype=F32, tn=D,
                                 name="dx_in", rider=ChipExchange([parts["w_in"]]))
    dx, dg_mix = rmsnorm_bwd(x2d, g_mix, dh, dx1, name="rms_mix_bwd")
    gs = dict(g_mix=dg_mix, b_f=db_f[:, :FOX_HEADS], conv_w=dconv_w[:CONV_K], conv_b=dvec[0:1],
              ln_g=dvec[1:2], ln_b=dvec[2:3], g_x=dg_x, g_mem=dg_mem, g_ffn=dg_ffn, g_final=dg_final)
    return loss, dx.reshape(B, S, D), gs, {n: (parts[n], gots[n]) for n in BIG}


def _me():
    return lax.axis_index("x"), lax.axis_index("y"), lax.axis_index("c")


def _any_specs(n):
    return [pl.BlockSpec(memory_space=pl.ANY)] * n


def all_gather(xs, *, name):
    n = len(xs)

    def body(*refs):
        x_refs, out_refs = refs[:n], refs[n:2 * n]
        send_sems, recv_sems, local_sems = refs[2 * n:]
        x, y, c = _me()
        me, sibling = (x, y, c), (x, y, 1 - c)
        chips = [(1 - x, y), (x, 1 - y), (1 - x, 1 - y)]

        def slot(a, px, py, pc):
            return out_refs[a].at[4 * px + 2 * py + pc]

        def copy(a, k, block, to, own=False):
            return pltpu.make_async_remote_copy(
                src_ref=x_refs[a] if own else slot(a, *block), dst_ref=slot(a, *block),
                send_sem=send_sems.at[k, a], recv_sem=recv_sems.at[k, a], device_id=to, device_id_type=MESH)

        mine = [pltpu.make_async_copy(x_refs[a], slot(a, *me), local_sems.at[a]) for a in range(n)]
        first = [copy(a, 0, me, sibling, own=True) for a in range(n)]
        first += [copy(a, 1 + j, me, (*chip, c), own=True) for j, chip in enumerate(chips) for a in range(n)]
        for cp in mine + first:
            cp.start()
        passed = []
        for j, chip in enumerate(chips):
            for a in range(n):
                copy(a, 1 + j, (*chip, c), me).wait_recv()
                passed.append(copy(a, 4 + j, (*chip, c), sibling))
                passed[-1].start()
        for a in range(n):
            copy(a, 0, sibling, me).wait_recv()
            for j, chip in enumerate(chips):
                copy(a, 4 + j, (*chip, 1 - c), me).wait_recv()
        for cp in first + passed:
            cp.wait_send()
        for cp in mine:
            cp.wait()

    return _call(
        body, name=name, in_specs=_any_specs(n), out_specs=_any_specs(n),
        out_shape=[jax.ShapeDtypeStruct((N_DEV,) + v.shape, v.dtype) for v in xs],
        scratch_shapes=[pltpu.SemaphoreType.DMA((7, n)), pltpu.SemaphoreType.DMA((7, n)),
                        pltpu.SemaphoreType.DMA((n,))],
    )(*xs)


class SiblingExchange:
    def __init__(self, gs):
        n = len(gs)
        self.n, self.inputs = n, list(gs)
        self.out_shape = [jax.ShapeDtypeStruct((4,) + g.shape[2:], g.dtype) for g in gs]
        self.scratch = [pltpu.SemaphoreType.DMA((n,)), pltpu.SemaphoreType.DMA((n,))]

    def _copies(self, g_refs, out_refs, sems):
        send_sems, recv_sems = sems
        x, y, c = _me()
        return [pltpu.make_async_remote_copy(
            src_ref=g_refs[a].at[:, 1 - c], dst_ref=out_refs[a], send_sem=send_sems.at[a],
            recv_sem=recv_sems.at[a], device_id=(x, y, 1 - c), device_id_type=MESH) for a in range(self.n)]

    def start(self, in_refs, out_refs, sems):
        for cp in self._copies(in_refs, out_refs, sems):
            cp.start()

    def finish(self, in_refs, out_refs, sems):
        for cp in self._copies(in_refs, out_refs, sems):
            cp.wait()


def run_rider(rider, *, name):
    return hosted_call(None, rider, name=name, grid=(), in_specs=[], out_specs=[], out_shape=[],
                       scratch_shapes=[], args=[])[1]


class ChipExchange:
    def __init__(self, ps):
        n = len(ps)
        self.n, self.inputs = n, list(ps)
        self.out_shape = [jax.ShapeDtypeStruct(p.shape, p.dtype) for p in ps]
        self.scratch = [pltpu.SemaphoreType.DMA((3, n)), pltpu.SemaphoreType.DMA((3, n))]

    def _copies(self, p_refs, out_refs, sems, outgoing):
        send_sems, recv_sems = sems
        x, y, c = _me()
        my_chip = 2 * x + y
        cps = []
        for k in range(3):
            px, py = x ^ ((k + 1) >> 1), y ^ ((k + 1) & 1)
            src, dst = (2 * px + py, my_chip) if outgoing else (my_chip, 2 * px + py)
            for a in range(self.n):
                cps.append(pltpu.make_async_remote_copy(
                    src_ref=p_refs[a].at[src], dst_ref=out_refs[a].at[dst], send_sem=send_sems.at[k, a],
                    recv_sem=recv_sems.at[k, a], device_id=(px, py, c), device_id_type=MESH))
        return cps

    def start(self, in_refs, out_refs, sems):
        for cp in self._copies(in_refs, out_refs, sems, True):
            cp.start()

    def finish(self, in_refs, out_refs, sems):
        for cp in self._copies(in_refs, out_refs, sems, False):
            cp.wait_recv()
        for cp in self._copies(in_refs, out_refs, sems, True):
            cp.wait_send()


class AllGatherStage1:
    def __init__(self, xs):
        n = len(xs)
        self.n, self.inputs = n, list(xs)
        self.out_shape = [jax.ShapeDtypeStruct((N_DEV,) + v.shape, v.dtype) for v in xs]
        self.scratch = [pltpu.SemaphoreType.DMA((4, n)), pltpu.SemaphoreType.DMA((4, n)),
                        pltpu.SemaphoreType.DMA((n,))]

    def _copies(self, x_refs, out_refs, sems, kind):
        send_sems, recv_sems, local_sems = sems
        x, y, c = _me()
        slot = lambda a, d: out_refs[a].at[4 * d[0] + 2 * d[1] + d[2]]
        if kind == "local":
            return [pltpu.make_async_copy(x_refs[a], slot(a, (x, y, c)), local_sems.at[a]) for a in range(self.n)]
        cps = []
        for k, peer in enumerate([(x, y, 1 - c), (1 - x, y, c), (x, 1 - y, c), (1 - x, 1 - y, c)]):
            for a in range(self.n):
                cps.append(pltpu.make_async_remote_copy(
                    src_ref=x_refs[a], dst_ref=slot(a, (x, y, c) if kind == "out" else peer),
                    send_sem=send_sems.at[k, a], recv_sem=recv_sems.at[k, a], device_id=peer, device_id_type=MESH))
        return cps

    def start(self, in_refs, out_refs, sems):
        for cp in self._copies(in_refs, out_refs, sems, "local") + self._copies(in_refs, out_refs, sems, "out"):
            cp.start()

    def finish(self, in_refs, out_refs, sems):
        for cp in self._copies(in_refs, out_refs, sems, "in"):
            cp.wait_recv()
        for cp in self._copies(in_refs, out_refs, sems, "out"):
            cp.wait_send()
        for cp in self._copies(in_refs, out_refs, sems, "local"):
            cp.wait()


def all_gather_stage2(outs, *, name):
    n = len(outs)

    def body(*refs):
        out_refs = refs[n:2 * n]
        send_sems, recv_sems = refs[2 * n:]
        x, y, c = _me()
        sends, recvs = [], []
        for k, (px, py) in enumerate([(1 - x, y), (x, 1 - y), (1 - x, 1 - y)]):
            for a in range(n):
                mk = lambda pc: pltpu.make_async_remote_copy(
                    src_ref=out_refs[a].at[4 * px + 2 * py + c], dst_ref=out_refs[a].at[4 * px + 2 * py + pc],
                    send_sem=send_sems.at[k, a], recv_sem=recv_sems.at[k, a], device_id=(x, y, 1 - c),
                    device_id_type=MESH)
                sends.append(mk(c))
                recvs.append(mk(1 - c))
        for cp in sends:
            cp.start()
        for cp in recvs:
            cp.wait_recv()
        for cp in sends:
            cp.wait_send()

    return _call(
        body, name=name, in_specs=_any_specs(n), out_specs=_any_specs(n),
        out_shape=[jax.ShapeDtypeStruct(o.shape, o.dtype) for o in outs],
        input_output_aliases={a: a for a in range(n)},
        scratch_shapes=[pltpu.SemaphoreType.DMA((3, n)), pltpu.SemaphoreType.DMA((3, n))],
    )(*outs)


def hosted_call(body, rider, *, name, grid, in_specs, out_specs, out_shape, scratch_shapes, args, vmem=None):
    n_in, n_out, n_scr = len(in_specs), len(out_specs), len(scratch_shapes)
    r_in, r_out = (len(rider.inputs), len(rider.out_shape)) if rider is not None else (0, 0)

    def wrapped(*refs):
        ins, refs = refs[:n_in], refs[n_in:]
        rins, refs = refs[:r_in], refs[r_in:]
        outs, refs = refs[:n_out], refs[n_out:]
        routs, refs = refs[:r_out], refs[r_out:]
        scr, rscr = refs[:n_scr], refs[n_scr:]
        ids = [pl.program_id(d) for d in range(len(grid))]
        first = functools.reduce(jnp.logical_and, [i == 0 for i in ids], True)
        last = functools.reduce(jnp.logical_and, [i == g - 1 for i, g in zip(ids, grid)], True)
        if rider is not None and grid:
            pl.when(first)(lambda: rider.start(rins, routs, rscr))
        elif rider is not None:
            rider.start(rins, routs, rscr)
        if body is not None:
            body(*ins, *outs, *scr)
        if rider is not None and grid:
            pl.when(last)(lambda: rider.finish(rins, routs, rscr))
        elif rider is not None:
            rider.finish(rins, routs, rscr)

    kw = dict(grid=grid) if grid else {}
    if grid or vmem is not None:
        kw["compiler_params"] = _params(("arbitrary",) * len(grid) if grid else None, vmem)
    res = _call(
        wrapped, name=name, in_specs=list(in_specs) + _any_specs(r_in), out_specs=list(out_specs) + _any_specs(r_out),
        out_shape=list(out_shape) + (rider.out_shape if rider is not None else []),
        scratch_shapes=list(scratch_shapes) + (rider.scratch if rider is not None else []), **kw,
    )(*args, *(rider.inputs if rider is not None else []))
    return list(res[:n_out]), list(res[n_out:])


def _pick_rows(r, target=256):
    best = None
    for d in range(16, min(r, target) + 1, 16):
        if r % d == 0:
            best = d
    return r if best is None else best


def pair_sum(g, got, *, name):
    _, _, R, C = g.shape
    tr = _pick_rows(R)

    def body(g_ref, got_ref, o_ref):
        mine = jnp.where(lax.axis_index("c") == 0, g_ref[:, 0], g_ref[:, 1])
        o_ref[...] = (mine.astype(F32) + got_ref[...].astype(F32)).astype(o_ref.dtype)

    return _call(
        body, name=name, grid=(R // tr,),
        in_specs=[pl.BlockSpec((4, 2, tr, C), lambda i: (0, 0, i, 0)), pl.BlockSpec((4, tr, C), lambda i: (0, i, 0))],
        out_specs=pl.BlockSpec((4, tr, C), lambda i: (0, i, 0)),
        out_shape=jax.ShapeDtypeStruct((4, R, C), g.dtype),
        compiler_params=_params(("parallel",)),
    )(g, got)


def chip_sum_adamw(p, got, w, m, v, *, name):
    _, R, C = p.shape
    assert w.shape == (1, R, C), (name, w.shape, p.shape)
    tr = _pick_rows(R)

    def body(p_ref, got_ref, w_ref, m_ref, v_ref, g_ref, d_ref, mo_ref, vo_ref):
        my_chip = 2 * lax.axis_index("x") + lax.axis_index("y")
        g = jnp.zeros((tr, C), F32)
        for j in range(4):
            g = g + jnp.where(my_chip == j, p_ref[j], got_ref[j]).astype(F32)
        g_ref[0] = g
        d_ref[0], mo_ref[0], vo_ref[0] = _adamw_math(w_ref[0], g, m_ref[0], v_ref[0])

    part = pl.BlockSpec((4, tr, C), lambda i: (0, i, 0))
    spec = pl.BlockSpec((1, tr, C), lambda i: (0, i, 0))
    return _call(
        body, name=name, grid=(R // tr,), in_specs=[part, part, spec, spec, spec], out_specs=[spec] * 4,
        out_shape=[jax.ShapeDtypeStruct((1, R, C), F32)] * 4,
        compiler_params=_params(("parallel",)),
    )(p, got, w, m, v)


def rows_sum(g8, *, name):
    _, R, C = g8.shape

    def body(g_ref, o_ref):
        acc = g_ref[0]
        for j in range(1, N_DEV):
            acc = acc + g_ref[j]
        o_ref[...] = acc

    return _call(body, name=name, out_shape=jax.ShapeDtypeStruct((R, C), F32))(g8)


def _adamw_math(w, g, m, v):
    m = ADAM_B1 * m + (1.0 - ADAM_B1) * g
    v = ADAM_B2 * v + (1.0 - ADAM_B2) * (g * g)
    m_hat = m / (1.0 - ADAM_B1 ** ADAM_STEP)
    v_hat = v / (1.0 - ADAM_B2 ** ADAM_STEP)
    delta = -ADAM_LR * (m_hat / (jnp.sqrt(v_hat) + ADAM_EPS) + ADAM_WD * w)
    return delta, m, v


def adamw_small(wgmv, *, name):
    n = len(wgmv)

    def body(*refs):
        ins, outs = refs[:4 * n], refs[4 * n:]
        for a in range(n):
            w_ref, g_ref, m_ref, v_ref = ins[4 * a:4 * a + 4]
            d, mn, vn = _adamw_math(w_ref[...], g_ref[...], m_ref[...], v_ref[...])
            outs[3 * a][...] = d
            outs[3 * a + 1][...] = mn
            outs[3 * a + 2][...] = vn

    flat = [t for tup in wgmv for t in tup]
    res = _call(
        body, name=name,
        out_shape=[jax.ShapeDtypeStruct(tup[0].shape, F32) for tup in wgmv for _ in range(3)],
    )(*flat)
    return [tuple(res[3 * a:3 * a + 3]) for a in range(n)]


BIG = ("w_in", "w_out", "w_mq", "w_mkv", "w_mo", "w_gu", "w_down")
COL_SHARDED = ("w_in", "w_mkv", "w_gu")
SMALL = ("g_mix", "b_f", "conv_w", "conv_b", "ln_g", "ln_b", "g_x", "g_mem", "g_ffn", "g_final")


def _full_from_gathered(n, blk):
    _, rr, cc = blk.shape
    if n in COL_SHARDED:
        return jnp.concatenate([blk[k] for k in range(N_DEV)], axis=1)
    return blk.reshape(N_DEV * rr, cc)


def _shards_from_full(n, g):
    pieces = g if isinstance(g, list) else [g]
    rr, cc = pieces[0].shape[0], sum(p.shape[1] for p in pieces)
    if n in COL_SHARDED:
        w = cc // N_DEV
        return jnp.stack([_columns(pieces, k * w, w) for k in range(N_DEV)]).reshape(4, 2, rr, w)
    return pieces[0].reshape(4, 2, rr // N_DEV, cc)


def _columns(pieces, start, width):
    out, c0 = [], 0
    for p in pieces:
        lo, hi = max(start, c0), min(start + width, c0 + p.shape[1])
        if lo < hi:
            out.append(p[:, lo - c0:hi - c0])
        c0 += p.shape[1]
    return out[0] if len(out) == 1 else jnp.concatenate(out, axis=1)


def _small_layout():
    sizes = dict(g_mix=1024, b_f=8, conv_w=CONV_K * CONV_CH, conv_b=512, ln_g=512, ln_b=512, g_x=1024,
                 g_mem=1024, g_ffn=1024, g_final=1024, loss=1)
    lay, r0 = {}, 0
    for n, sz in sizes.items():
        r = -(-sz // LANES)
        lay[n] = (r0, r, sz)
        r0 += r
    return lay, -(-r0 // 8) * 8


def kernel(x, mem, g_mix, w_in, b_f, conv_w, conv_b, ln_g, ln_b, w_out, g_x, g_mem, w_mq, w_mkv, w_mo, g_ffn, w_gu, w_down, g_final, loss_target, m_g_mix, m_w_in, m_b_f, m_conv_w, m_conv_b, m_ln_g, m_ln_b, m_w_out, m_g_x, m_g_mem, m_w_mq, m_w_mkv, m_w_mo, m_g_ffn, m_w_gu, m_w_down, m_g_final, v_g_mix, v_w_in, v_b_f, v_conv_w, v_conv_b, v_ln_g, v_ln_b, v_w_out, v_g_x, v_g_mem, v_w_mq, v_w_mkv, v_w_mo, v_g_ffn, v_w_gu, v_w_down, v_g_final):
    names = ["g_mix", "w_in", "b_f", "conv_w", "conv_b", "ln_g", "ln_b", "w_out", "g_x", "g_mem", "w_mq",
             "w_mkv", "w_mo", "g_ffn", "w_gu", "w_down", "g_final"]
    W = dict(zip(names, [g_mix, w_in, b_f, conv_w, conv_b, ln_g, ln_b, w_out, g_x, g_mem, w_mq, w_mkv, w_mo,
                         g_ffn, w_gu, w_down, g_final]))
    Mo = dict(zip(names, [m_g_mix, m_w_in, m_b_f, m_conv_w, m_conv_b, m_ln_g, m_ln_b, m_w_out, m_g_x, m_g_mem,
                          m_w_mq, m_w_mkv, m_w_mo, m_g_ffn, m_w_gu, m_w_down, m_g_final]))
    Vo = dict(zip(names, [v_g_mix, v_w_in, v_b_f, v_conv_w, v_conv_b, v_ln_g, v_ln_b, v_w_out, v_g_x, v_g_mem,
                          v_w_mq, v_w_mkv, v_w_mo, v_g_ffn, v_w_gu, v_w_down, v_g_final]))
    dev = 4 * lax.axis_index("x") + 2 * lax.axis_index("y") + lax.axis_index("c")

    two = lambda a: a.reshape(-1, a.shape[-1])
    cw_shard = jnp.pad(two(conv_w), ((0, HALO - CONV_K), (0, 0)))
    sp = dict(g_mix=g_mix, b_f=b_f, conv_b=conv_b, ln_g=ln_g, ln_b=ln_b, g_x=g_x, g_mem=g_mem,
              g_ffn=g_ffn, g_final=g_final)
    loss_blk, grad_x, gs, reduced = local_step(x, mem, loss_target, sp, [two(w_in).astype(BF16), cw_shard],
                                               [two(W[n]).astype(BF16) for n in LATE])

    lay, rs = _small_layout()
    small = {**{n: gs[n] for n in SMALL}, "loss": loss_blk[:, :1]}
    parts = []
    for n, (r0, r, sz) in lay.items():
        flat = small[n].reshape(-1).astype(F32)
        parts.append(jnp.pad(flat, (0, r * LANES - sz)).reshape(r, LANES))
    spack = jnp.concatenate(parts, axis=0)
    spack = jnp.pad(spack, ((0, rs - spack.shape[0]), (0, 0)))
    ssum = rows_sum(all_gather([spack], name="ag_small")[0], name="small_sum")
    gsmall = {n: ssum[r0:r0 + r].reshape(-1)[:sz] for n, (r0, r, sz) in lay.items()}
    loss = gsmall["loss"].reshape(())

    grads, delta, new_m, new_v = {}, {}, {}, {}
    for n in BIG:
        p, o = reduced[n]
        grads[n], delta[n], new_m[n], new_v[n] = chip_sum_adamw(p, o, W[n], Mo[n], Vo[n], name="adamw_" + n)
    for n in SMALL:
        if n == "conv_w":
            full = gsmall[n].reshape(CONV_K, CONV_CH)
            ncol = conv_w.shape[-1]
            grads[n] = lax.dynamic_slice(full, (0, dev * ncol), (CONV_K, ncol)).reshape(conv_w.shape)
        else:
            grads[n] = gsmall[n].reshape(W[n].shape)
    upd = adamw_small([(two(W[n]), two(grads[n]), two(Mo[n]), two(Vo[n])) for n in SMALL], name="adamw_small")
    for n, (d, mn, vn) in zip(SMALL, upd):
        shp = W[n].shape
        delta[n], new_m[n], new_v[n] = d.reshape(shp), mn.reshape(shp), vn.reshape(shp)
    return (loss, grad_x, *[grads[n] for n in names], *[delta[n] for n in names],
            *[new_m[n] for n in names], *[new_v[n] for n in names])
```

```python
import functools
import math

import jax
import jax.numpy as jnp
from jax import lax
from jax.experimental import pallas as pl
from jax.experimental.pallas import tpu as pltpu

F32 = jnp.float32
BF16 = jnp.bfloat16
EPS = 1e-6
N_DEV = 8
CONV_CH = 512
CONV_K = 31
FOX_HEADS = 8
FOX_HEAD_DIM = 64
FOX_W = 512
MEM_HEADS = 4
MEM_HEAD_DIM = 256
HALO = 32
LANES = 128
ADAM_LR, ADAM_B1, ADAM_B2, ADAM_EPS, ADAM_WD, ADAM_STEP = 0.001, 0.9, 0.999, 1e-08, 0.01, 10
NEG = -1e30
VMEM_CAP = 60 * 1024 * 1024
MESH = pl.DeviceIdType.MESH


def _call(body, **kw):
    kw["out_shape"] = jax.tree.map(lambda s: pltpu.HBM(s.shape, s.dtype), kw["out_shape"])
    call = pl.pallas_call(body, **kw)
    return lambda *args: call(*[pltpu.with_memory_space_constraint(a, pltpu.HBM) for a in args])


def _params(sem=None, vmem=None):
    kw = {}
    if sem is not None:
        kw["dimension_semantics"] = sem
    if vmem is not None:
        kw["vmem_limit_bytes"] = int(min(VMEM_CAP, vmem))
    return pltpu.CompilerParams(**kw)


def _nbytes(shape, dtype):
    return math.prod(shape) * jnp.dtype(dtype).itemsize


def _pick(n, target):
    best = None
    for d in range(LANES, min(n, target) + 1, LANES):
        if n % d == 0:
            best = d
    return n if best is None else best


def matmul(a, b, *, tb=False, out_dtype, res=None, tm=512, tn=512, name, rider=None, b_blk=None):
    a_list = list(a) if isinstance(a, (list, tuple)) else [a]
    b_list = list(b) if isinstance(b, (list, tuple)) else [b]
    n = len(a_list)
    assert len(b_list) == n
    M = a_list[0].shape[0]
    N = b_list[0].shape[0] if tb else b_list[0].shape[1]
    tm, tn = _pick(M, tm), _pick(N, tn)
    assert M % tm == 0 and N % tn == 0, (name, M, N, tm, tn)
    dn = (((1,), (1 if tb else 0,)), ((), ()))

    def body(*refs):
        acc = None
        for a_ref, b_ref in zip(refs[:n], refs[n:2 * n]):
            p = lax.dot_general(a_ref[...].astype(BF16), b_ref[...].astype(BF16), dn, preferred_element_type=F32)
            acc = p if acc is None else acc + p
        if res is not None:
            acc = acc + refs[2 * n][...].astype(F32)
        refs[-1][...] = acc.astype(out_dtype)

    o_spec = pl.BlockSpec((tm, tn), lambda i, j: (i, j))
    in_specs, est = [], 2 * _nbytes((tm, tn), out_dtype) + 2 * _nbytes((tm, tn), F32)
    for av in a_list:
        assert av.shape[0] == M
        in_specs.append(pl.BlockSpec((tm, av.shape[1]), lambda i, j: (i, 0)))
        est += (2 * jnp.dtype(av.dtype).itemsize + (av.dtype != BF16) * 2) * tm * av.shape[1]
    for idx, (av, bv) in enumerate(zip(a_list, b_list)):
        K = av.shape[1]
        kb = 0 if b_blk is None else b_blk[idx]
        assert bv.shape[0 if tb else 1] == N and bv.shape[1 if tb else 0] >= (kb + 1) * K, (name, av.shape, bv.shape)
        assert b_blk is not None or bv.shape[1 if tb else 0] == K, (name, av.shape, bv.shape)
        in_specs.append(pl.BlockSpec((tn, K), lambda i, j, kb=kb: (j, kb)) if tb
                        else pl.BlockSpec((K, tn), lambda i, j, kb=kb: (kb, j)))
        est += (2 * jnp.dtype(bv.dtype).itemsize + (bv.dtype != BF16) * 2) * tn * K
    args = a_list + b_list
    if res is not None:
        in_specs.append(o_spec)
        args.append(res)
        est += 2 * _nbytes((tm, tn), res.dtype)
    (out,), rode = hosted_call(
        body, rider, name=name, grid=(M // tm, N // tn), in_specs=in_specs, out_specs=[o_spec],
        out_shape=[jax.ShapeDtypeStruct((M, N), out_dtype)], scratch_shapes=[],
        args=args, vmem=est + (8 << 20),
    )
    return out if rider is None else (out, rode)


def _rms_scale(x):
    return lax.rsqrt(jnp.mean(x * x, axis=-1, keepdims=True) + EPS)


def rmsnorm_fwd(x, g, *, name, tm=512, rider=None):
    T, D = x.shape
    tm = min(tm, T)

    def body(x_ref, g_ref, o_ref, ot_ref):
        xv = x_ref[...]
        h = xv * _rms_scale(xv) * g_ref[...]
        o_ref[...] = h.astype(BF16)
        ot_ref[...] = h.T.astype(BF16)

    (h, h_t), rode = hosted_call(
        body, rider, name=name, grid=(T // tm,),
        in_specs=[pl.BlockSpec((tm, D), lambda i: (i, 0)), pl.BlockSpec((1, D), lambda i: (0, 0))],
        out_specs=[pl.BlockSpec((tm, D), lambda i: (i, 0)), pl.BlockSpec((D, tm), lambda i: (0, i))],
        out_shape=[jax.ShapeDtypeStruct((T, D), BF16), jax.ShapeDtypeStruct((D, T), BF16)],
        scratch_shapes=[], args=(x, g),
    )
    return (h, h_t) if rider is None else (h, h_t, rode)


def _rms_bwd_math(xv, gv, dh):
    r = _rms_scale(xv)
    xh = xv * r
    dg = jnp.sum(dh * xh, axis=0, keepdims=True)
    dxh = dh * gv
    dx = r * (dxh - xh * jnp.mean(dxh * xh, axis=-1, keepdims=True))
    return dx, dg


def rmsnorm_bwd(x, g, dh, dres, *, name, tm=256):
    T, D = x.shape
    tm = min(tm, T)

    def body(*refs):
        if dres is not None:
            x_ref, g_ref, dh_ref, dr_ref, dx_ref, dg_ref = refs
        else:
            x_ref, g_ref, dh_ref, dx_ref, dg_ref = refs
        dx, dg = _rms_bwd_math(x_ref[...], g_ref[...], dh_ref[...].astype(F32))
        if dres is not None:
            dx = dx + dr_ref[...]
        dx_ref[...] = dx

        @pl.when(pl.program_id(0) == 0)
        def _():
            dg_ref[...] = jnp.zeros_like(dg_ref)

        dg_ref[...] += dg

    row = pl.BlockSpec((tm, D), lambda i: (i, 0))
    vec = pl.BlockSpec((1, D), lambda i: (0, 0))
    ins, args = [row, vec, row], [x, g, dh]
    if dres is not None:
        ins.append(row)
        args.append(dres)
    return _call(
        body, name=name, grid=(T // tm,), in_specs=ins, out_specs=[row, vec],
        out_shape=[jax.ShapeDtypeStruct((T, D), F32), jax.ShapeDtypeStruct((1, D), F32)],
        compiler_params=_params(("arbitrary",)),
    )(*args)


def final_loss_bwd(x, g, target, *, name, tm=256):
    T, D = x.shape
    tm = min(tm, T)

    def body(x_ref, g_ref, t_ref, dx_ref, dg_ref, l_ref):
        xv, gv = x_ref[...], g_ref[...]
        e = xv * _rms_scale(xv) * gv - t_ref[...]
        part = 0.5 * jnp.sum(jnp.mean(e * e, axis=-1, keepdims=True), axis=0, keepdims=True)
        dx, dg = _rms_bwd_math(xv, gv, e * (1.0 / D))
        dx_ref[...] = dx

        @pl.when(pl.program_id(0) == 0)
        def _():
            dg_ref[...] = jnp.zeros_like(dg_ref)
            l_ref[...] = jnp.zeros_like(l_ref)

        dg_ref[...] += dg
        l_ref[...] += jnp.broadcast_to(part, l_ref.shape)

    row = pl.BlockSpec((tm, D), lambda i: (i, 0))
    vec = pl.BlockSpec((1, D), lambda i: (0, 0))
    return _call(
        body, name=name, grid=(T // tm,), in_specs=[row, vec, row],
        out_specs=[row, vec, pl.BlockSpec((1, LANES), lambda i: (0, 0))],
        out_shape=[jax.ShapeDtypeStruct((T, D), F32), jax.ShapeDtypeStruct((1, D), F32),
                   jax.ShapeDtypeStruct((1, LANES), F32)],
        compiler_params=_params(("arbitrary",)),
    )(x, g, target)


def _sigmoid(v):
    return 1.0 / (1.0 + jnp.exp(-v))


def _glu(blk):
    u = blk[:, :CONV_CH].astype(F32)
    gt = blk[:, CONV_CH:].astype(F32)
    return u * _sigmoid(gt)


def _fill_causal_ext(ext, cur_ref, halo_ref, s, ts):
    ext[pl.ds(HALO, ts), :] = _glu(cur_ref[0])
    hal = _glu(halo_ref[0])
    ext[pl.ds(0, HALO), :] = jnp.where(s > 0, hal, 0.0)


SUBLANES = 8


def _make_shifted(ext, sh):
    n = ext.shape[0]
    full = ext[...]
    for r in range(1, SUBLANES):
        sh[r - 1] = pltpu.roll(full, n - r, 0)


def _tap(ext, sh, off, ts):
    r = off % SUBLANES
    return ext[pl.ds(off, ts), :] if r == 0 else sh[r - 1, pl.ds(off - r, ts), :]


def _causal_conv(ext, sh, w_ref, ts):
    acc = jnp.zeros((ts, CONV_CH), F32)
    for j in range(CONV_K):
        acc = acc + _tap(ext, sh, HALO - (CONV_K - 1) + j, ts) * w_ref[pl.ds(j, 1), :]
    return acc


def _ln_stats(y):
    mu = jnp.mean(y, axis=-1, keepdims=True)
    yc = y - mu
    rstd = lax.rsqrt(jnp.mean(yc * yc, axis=-1, keepdims=True) + EPS)
    return yc * rstd, rstd


def _conv_specs(ts, S):
    nh = ts // HALO
    cur = pl.BlockSpec((1, ts, 2 * CONV_CH), lambda b, s: (b, s, 0))
    halo = pl.BlockSpec((1, HALO, 2 * CONV_CH), lambda b, s: (b, jnp.maximum(s * nh - 1, 0), 0))
    w = pl.BlockSpec((HALO, CONV_CH), lambda b, s: (0, 0))
    vec = pl.BlockSpec((1, CONV_CH), lambda b, s: (0, 0))
    return cur, halo, w, vec


def conv_branch_fwd(ug, conv_w, conv_b, ln_g, ln_b, *, name, ts=256, rider=None):
    B, S, _ = ug.shape
    ts = min(ts, S)
    ns = S // ts
    cur, halo, w, vec = _conv_specs(ts, S)

    def body(cur_ref, halo_ref, w_ref, cb_ref, lg_ref, lb_ref, o_ref, ot_ref, ext, sh):
        _fill_causal_ext(ext, cur_ref, halo_ref, pl.program_id(1), ts)
        _make_shifted(ext, sh)
        y = _causal_conv(ext, sh, w_ref, ts) + cb_ref[...]
        yh, _ = _ln_stats(y)
        ln = yh * lg_ref[...] + lb_ref[...]
        out = ln * _sigmoid(ln)
        o_ref[0] = out.astype(BF16)
        ot_ref[...] = out.T.astype(BF16)

    return hosted_call(
        body, rider, name=name, grid=(B, ns), in_specs=[cur, halo, w, vec, vec, vec],
        out_specs=[pl.BlockSpec((1, ts, CONV_CH), lambda b, s: (b, s, 0)),
                   pl.BlockSpec((CONV_CH, ts), lambda b, s: (0, b * ns + s))],
        out_shape=[jax.ShapeDtypeStruct((B, S, CONV_CH), BF16), jax.ShapeDtypeStruct((CONV_CH, B * S), BF16)],
        scratch_shapes=[pltpu.VMEM((ts + HALO, CONV_CH), F32),
                        pltpu.VMEM((SUBLANES - 1, ts + HALO, CONV_CH), F32)],
        args=(ug, ug, conv_w, conv_b, ln_g, ln_b),
    )


def conv_branch_bwd_a(ug, dcat, conv_w, conv_b, ln_g, ln_b, *, name, ts=256):
    B, S, _ = ug.shape
    ts = min(ts, S)
    cur, halo, w, vec = _conv_specs(ts, S)

    def body(cur_ref, halo_ref, d_ref, w_ref, cb_ref, lg_ref, lb_ref, dy_ref, dw_ref, dv_ref, ext, sh):
        _fill_causal_ext(ext, cur_ref, halo_ref, pl.program_id(1), ts)
        _make_shifted(ext, sh)
        y = _causal_conv(ext, sh, w_ref, ts) + cb_ref[...]
        yh, rstd = _ln_stats(y)
        lg = lg_ref[...]
        ln = yh * lg + lb_ref[...]
        sg = _sigmoid(ln)
        dln = d_ref[0].astype(F32) * (sg * (1.0 + ln * (1.0 - sg)))
        dyh = dln * lg
        dy = rstd * (dyh - jnp.mean(dyh, axis=-1, keepdims=True)
                     - yh * jnp.mean(dyh * yh, axis=-1, keepdims=True))
        dy_ref[0] = dy

        @pl.when((pl.program_id(0) == 0) & (pl.program_id(1) == 0))
        def _():
            dw_ref[...] = jnp.zeros_like(dw_ref)
            dv_ref[...] = jnp.zeros_like(dv_ref)

        dv_ref[pl.ds(0, 1), :] += jnp.sum(dy, axis=0, keepdims=True)
        dv_ref[pl.ds(1, 1), :] += jnp.sum(dln * yh, axis=0, keepdims=True)
        dv_ref[pl.ds(2, 1), :] += jnp.sum(dln, axis=0, keepdims=True)
        for j in range(CONV_K):
            tap = _tap(ext, sh, HALO - (CONV_K - 1) + j, ts)
            dw_ref[pl.ds(j, 1), :] += jnp.sum(dy * tap, axis=0, keepdims=True)

    return _call(
        body, name=name, grid=(B, S // ts),
        in_specs=[cur, halo, pl.BlockSpec((1, ts, CONV_CH), lambda b, s: (b, s, 0)), w, vec, vec, vec],
        out_specs=[pl.BlockSpec((1, ts, CONV_CH), lambda b, s: (b, s, 0)),
                   pl.BlockSpec((HALO, CONV_CH), lambda b, s: (0, 0)),
                   pl.BlockSpec((8, CONV_CH), lambda b, s: (0, 0))],
        out_shape=[jax.ShapeDtypeStruct((B, S, CONV_CH), F32),
                   jax.ShapeDtypeStruct((HALO, CONV_CH), F32),
                   jax.ShapeDtypeStruct((8, CONV_CH), F32)],
        scratch_shapes=[pltpu.VMEM((ts + HALO, CONV_CH), F32),
                        pltpu.VMEM((SUBLANES - 1, ts + HALO, CONV_CH), F32)],
        compiler_params=_params(("arbitrary", "arbitrary")),
    )(ug, ug, dcat, conv_w, conv_b, ln_g, ln_b)


def conv_branch_bwd_b(ug, dy, conv_w, *, name, ts=256):
    B, S, _ = ug.shape
    ts = min(ts, S)
    nh, n_halo = ts // HALO, S // HALO

    def body(cur_ref, dy_ref, nxt_ref, w_ref, o_ref, ext, sh):
        last = pl.program_id(1) == pl.num_programs(1) - 1
        ext[pl.ds(0, ts), :] = dy_ref[0]
        ext[pl.ds(ts, HALO), :] = jnp.where(last, 0.0, nxt_ref[0])
        _make_shifted(ext, sh)
        da = jnp.zeros((ts, CONV_CH), F32)
        for j in range(CONV_K):
            da = da + _tap(ext, sh, CONV_K - 1 - j, ts) * w_ref[pl.ds(j, 1), :]
        blk = cur_ref[0]
        u = blk[:, :CONV_CH].astype(F32)
        sg = _sigmoid(blk[:, CONV_CH:].astype(F32))
        o_ref[0, :, :CONV_CH] = (da * sg).astype(BF16)
        o_ref[0, :, CONV_CH:] = (da * u * sg * (1.0 - sg)).astype(BF16)

    return _call(
        body, name=name, grid=(B, S // ts),
        in_specs=[pl.BlockSpec((1, ts, 2 * CONV_CH), lambda b, s: (b, s, 0)),
                  pl.BlockSpec((1, ts, CONV_CH), lambda b, s: (b, s, 0)),
                  pl.BlockSpec((1, HALO, CONV_CH), lambda b, s: (b, jnp.minimum((s + 1) * nh, n_halo - 1), 0)),
                  pl.BlockSpec((HALO, CONV_CH), lambda b, s: (0, 0))],
        out_specs=pl.BlockSpec((1, ts, 2 * CONV_CH), lambda b, s: (b, s, 0)),
        out_shape=jax.ShapeDtypeStruct((B, S, 2 * CONV_CH), BF16),
        scratch_shapes=[pltpu.VMEM((ts + HALO, CONV_CH), F32),
                        pltpu.VMEM((SUBLANES - 1, ts + HALO, CONV_CH), F32)],
        compiler_params=_params(("parallel", "parallel")),
    )(ug, dy, dy, conv_w)


def _tri(n, lower):
    r = lax.broadcasted_iota(jnp.int32, (n, n), 0)
    c = lax.broadcasted_iota(jnp.int32, (n, n), 1)
    return ((r >= c) if lower else (r <= c)).astype(F32)


def _eye(n):
    r = lax.broadcasted_iota(jnp.int32, (n, n), 0)
    c = lax.broadcasted_iota(jnp.int32, (n, n), 1)
    return (r == c).astype(F32)


def _dot_hi(a, b, dn):
    return lax.dot_general(a, b, dn, precision=lax.Precision.HIGHEST, preferred_element_type=F32)


NN = (((1,), (0,)), ((), ()))
NT = (((1,), (1,)), ((), ()))
TN = (((0,), (0,)), ((), ()))


def _log_sigmoid(v):
    e = jnp.exp(-jnp.abs(v))
    log1p_e = jnp.where(e < 1e-3, e * (1.0 - 0.5 * e), jnp.log(1.0 + e))
    return jnp.minimum(v, 0.0) - log1p_e


def fgate_fwd(h, w_f, b_f, *, name, ts=256, rider=None):
    B, S, D = h.shape
    ts = min(ts, S)

    def body(h_ref, w_ref, b_ref, f_ref, cc_ref, cr_ref, carry):
        @pl.when(pl.program_id(1) == 0)
        def _():
            carry[...] = jnp.zeros_like(carry)

        f = jnp.dot(h_ref[0], w_ref[...], preferred_element_type=F32)
        f_ref[0] = f
        logf = _log_sigmoid(f + b_ref[...])
        c = _dot_hi(_tri(ts, True), logf, NN) + carry[pl.ds(0, 1), :]
        cc_ref[0] = c
        carry[pl.ds(0, 1), :] = c[ts - 1:ts, :]
        cr_ref[0] = _dot_hi(_eye(LANES), c, NT)

    return hosted_call(
        body, rider, name=name, grid=(B, S // ts),
        in_specs=[pl.BlockSpec((1, ts, D), lambda b, s: (b, s, 0)),
                  pl.BlockSpec((D, LANES), lambda b, s: (0, 0)),
                  pl.BlockSpec((1, LANES), lambda b, s: (0, 0))],
        out_specs=[pl.BlockSpec((1, ts, LANES), lambda b, s: (b, s, 0)),
                   pl.BlockSpec((1, ts, LANES), lambda b, s: (b, s, 0)),
                   pl.BlockSpec((1, LANES, ts), lambda b, s: (b, 0, s))],
        out_shape=[jax.ShapeDtypeStruct((B, S, LANES), F32), jax.ShapeDtypeStruct((B, S, LANES), F32),
                   jax.ShapeDtypeStruct((B, LANES, S), F32)],
        scratch_shapes=[pltpu.VMEM((8, LANES), F32)],
        args=(h, w_f, b_f),
    )


def fgate_bwd(dc, f, b_f, *, name, ts=256):
    B, S, _ = f.shape
    P = dc.shape[1]
    ts = min(ts, S)
    ns = S // ts

    def body(dc_ref, f_ref, b_ref, df_ref, db_ref, carry):
        @pl.when(pl.program_id(1) == 0)
        def _():
            carry[...] = jnp.zeros_like(carry)

        @pl.when((pl.program_id(0) == 0) & (pl.program_id(1) == 0))
        def _():
            db_ref[...] = jnp.zeros_like(db_ref)

        dc_t = dc_ref[0, 0]
        for j in range(1, P):
            dc_t = dc_t + dc_ref[0, j]
        dlogf = _dot_hi(_tri(ts, False), dc_t, NN) + carry[pl.ds(0, 1), :]
        carry[pl.ds(0, 1), :] = dlogf[0:1, :]
        df = dlogf * _sigmoid(-(f_ref[0] + b_ref[...]))
        df_ref[0] = df.astype(BF16)
        db_ref[...] += jnp.sum(df, axis=0, keepdims=True)

    return _call(
        body, name=name, grid=(B, ns),
        in_specs=[pl.BlockSpec((1, P, ts, LANES), lambda b, s: (b, 0, ns - 1 - s, 0)),
                  pl.BlockSpec((1, ts, LANES), lambda b, s: (b, ns - 1 - s, 0)),
                  pl.BlockSpec((1, LANES), lambda b, s: (0, 0))],
        out_specs=[pl.BlockSpec((1, ts, LANES), lambda b, s: (b, ns - 1 - s, 0)),
                   pl.BlockSpec((1, LANES), lambda b, s: (0, 0))],
        out_shape=[jax.ShapeDtypeStruct((B, S, LANES), BF16), jax.ShapeDtypeStruct((1, LANES), F32)],
        scratch_shapes=[pltpu.VMEM((8, LANES), F32)],
        compiler_params=_params(("arbitrary", "arbitrary")),
    )(dc, f, b_f)


def _lane_pick(tile, idx):
    lane = lax.broadcasted_iota(jnp.int32, tile.shape, 1)
    return jnp.sum(jnp.where(lane == idx, tile, 0.0), axis=-1, keepdims=True)


FOX_T = 512


def _fox_heads(q, cc_ref, p):
    lane = lax.broadcasted_iota(jnp.int32, q.shape, 1)
    qs = q * (1.0 / math.sqrt(FOX_HEAD_DIM))
    qhs = [jnp.where((lane < FOX_HEAD_DIM) == (hh == 0), qs, jnp.zeros_like(qs)) for hh in range(2)]
    crefs = [_lane_pick(cc_ref[0, pl.ds(0, 1), :], 2 * p + hh) for hh in range(2)]
    return qhs, crefs


def _fold_lanes(x, op):
    out = x[:, :LANES]
    for j in range(1, x.shape[1] // LANES):
        out = op(out, x[:, j * LANES:(j + 1) * LANES])
    return out


def _causal(t, transposed):
    r = lax.broadcasted_iota(jnp.int32, (t, t), 0)
    c = lax.broadcasted_iota(jnp.int32, (t, t), 1)
    return (r <= c) if transposed else (c <= r)


QKV0 = 8


def fox_fwd(z, c_col, c_row, *, name, rider=None):
    B, S, _ = z.shape
    assert S % FOX_T == 0
    tq, nq = FOX_T, S // FOX_T
    npair = FOX_HEADS // 2

    def body(q_ref, k_ref, v_ref, cc_ref, cr_ref, o_ref, l_ref, ot_ref, s_scr, m_scr, acc_scr):
        p, qi = pl.program_id(1), pl.program_id(2)
        qhs, crefs = _fox_heads(q_ref[0], cc_ref, p)
        lane = lax.broadcasted_iota(jnp.int32, (tq, LANES), 1)
        first = lane < FOX_HEAD_DIM
        for hh in range(2):
            m_scr[hh] = jnp.full((tq, LANES), NEG, F32)
            acc_scr[hh] = jnp.zeros((tq, LANES), F32)

        def logits(kb, diagonal):
            k0 = pl.multiple_of(kb * tq, tq)
            k = k_ref[0, pl.ds(k0, tq), :]
            for hh in range(2):
                s = lax.dot_general(qhs[hh], k, NT, preferred_element_type=F32)
                s = s + (crefs[hh] - cr_ref[0, pl.ds(2 * p + hh, 1), pl.ds(k0, tq)])
                if diagonal:
                    s = jnp.where(_causal(tq, False), s, NEG)
                s_scr[hh, kb] = s
                m_scr[hh] = jnp.maximum(m_scr[hh], _fold_lanes(s, jnp.maximum))

        def sweep1(kb, carry):
            logits(kb, False)
            return carry

        lax.fori_loop(0, qi, sweep1, 0)
        logits(qi, True)
        ms = [jnp.max(m_scr[hh], axis=-1, keepdims=True) for hh in range(2)]
        mbs = [jnp.broadcast_to(ms[hh], (tq, tq)) for hh in range(2)]

        for hh in range(2):
            m_scr[hh] = jnp.zeros((tq, LANES), F32)

        def weigh(kb, carry):
            k0 = pl.multiple_of(kb * tq, tq)
            v = v_ref[0, pl.ds(k0, tq), :]
            for hh in range(2):
                pr = jnp.exp(s_scr[hh, kb] - mbs[hh])
                m_scr[hh] += _fold_lanes(pr, jnp.add)
                acc_scr[hh] += jnp.dot(pr.astype(BF16), v, preferred_element_type=F32)
            return carry

        lax.fori_loop(0, qi + 1, weigh, 0)
        accs = [acc_scr[hh] for hh in range(2)]
        ls = [jnp.sum(m_scr[hh], axis=-1, keepdims=True) for hh in range(2)]
        out = jnp.where(first, accs[0] / ls[0], accs[1] / ls[1])
        o_ref[0] = out.astype(BF16)
        ot_ref[...] = out.T.astype(BF16)
        l_ref[0, 0] = jnp.where(first, ms[0] + jnp.log(ls[0]), ms[1] + jnp.log(ls[1]))

    return hosted_call(
        body, rider, name=name, grid=(B, npair, nq),
        in_specs=[pl.BlockSpec((1, tq, LANES), lambda b, p, i: (b, i, QKV0 + p)),
                  pl.BlockSpec((1, S, LANES), lambda b, p, i: (b, 0, QKV0 + npair + p)),
                  pl.BlockSpec((1, S, LANES), lambda b, p, i: (b, 0, QKV0 + 2 * npair + p)),
                  pl.BlockSpec((1, tq, LANES), lambda b, p, i: (b, i, 0)),
                  pl.BlockSpec((1, 8, S), lambda b, p, i: (b, 0, 0))],
        out_specs=[pl.BlockSpec((1, tq, LANES), lambda b, p, i: (b, i, p)),
                   pl.BlockSpec((1, 1, tq, LANES), lambda b, p, i: (b, p, i, 0)),
                   pl.BlockSpec((LANES, tq), lambda b, p, i: (p, b * nq + i))],
        out_shape=[jax.ShapeDtypeStruct((B, S, FOX_W), BF16),
                   jax.ShapeDtypeStruct((B, npair, S, LANES), F32),
                   jax.ShapeDtypeStruct((FOX_W, B * S), BF16)],
        scratch_shapes=[pltpu.VMEM((2, nq, tq, tq), F32), pltpu.VMEM((2, tq, LANES), F32),
                        pltpu.VMEM((2, tq, LANES), F32)],
        args=(z, z, z, c_col, c_row),
    )


def fox_bwd_dq(z, dcat, lse, c_col, c_row, *, name, rider=None):
    B, S, _ = z.shape
    tq, nq = FOX_T, S // FOX_T
    npair = FOX_HEADS // 2

    def body(q_ref, k_ref, v_ref, do_ref, l_ref, cc_ref, cr_ref, dq_ref, st_ref, p_scr, dp_scr, dl_scr):
        p, qi = pl.program_id(1), pl.program_id(2)
        qhs, crefs = _fox_heads(q_ref[0], cc_ref, p)
        lane = lax.broadcasted_iota(jnp.int32, (tq, LANES), 1)
        do_b = do_ref[0].astype(BF16)
        dohs = [jnp.where((lane < FOX_HEAD_DIM) == (hh == 0), do_b, jnp.zeros_like(do_b)) for hh in range(2)]
        lses = [_lane_pick(l_ref[0, 0], hh * FOX_HEAD_DIM) for hh in range(2)]
        lbs = [jnp.broadcast_to(lses[hh], (tq, tq)) for hh in range(2)]
        for hh in range(2):
            dl_scr[hh] = jnp.zeros((tq, LANES), F32)

        def probs(kb, diagonal):
            k0 = pl.multiple_of(kb * tq, tq)
            k = k_ref[0, pl.ds(k0, tq), :]
            v = v_ref[0, pl.ds(k0, tq), :]
            for hh in range(2):
                s = lax.dot_general(qhs[hh], k, NT, preferred_element_type=F32)
                s = s + (crefs[hh] - cr_ref[0, pl.ds(2 * p + hh, 1), pl.ds(k0, tq)])
                pr = jnp.exp(s - lbs[hh])
                if diagonal:
                    pr = jnp.where(_causal(tq, False), pr, 0.0)
                dp = lax.dot_general(dohs[hh], v, NT, preferred_element_type=F32)
                pdp = pr * dp
                dl_scr[hh] += _fold_lanes(pdp, jnp.add)
                p_scr[hh, kb] = pr
                dp_scr[hh, kb] = dp

        def first_pass(kb, carry):
            probs(kb, False)
            return carry

        lax.fori_loop(0, qi, first_pass, 0)
        probs(qi, True)

        dls = [jnp.sum(dl_scr[hh], axis=-1, keepdims=True) for hh in range(2)]
        dlbs = [jnp.broadcast_to(dls[hh], (tq, tq)) for hh in range(2)]

        def second_pass(kb, dq):
            k0 = pl.multiple_of(kb * tq, tq)
            k = k_ref[0, pl.ds(k0, tq), :]
            for hh in range(2):
                ds = p_scr[hh, kb] * (dp_scr[hh, kb] - dlbs[hh])
                kh = jnp.where((lane < FOX_HEAD_DIM) == (hh == 0), k, jnp.zeros_like(k))
                dq = dq + jnp.dot(ds.astype(BF16), kh, preferred_element_type=F32)
            return dq

        dq = lax.fori_loop(0, qi + 1, second_pass, jnp.zeros((tq, LANES), F32))
        dq_ref[0] = (dq * (1.0 / math.sqrt(FOX_HEAD_DIM))).astype(BF16)
        cols = jnp.zeros((tq, LANES), F32)
        for j, col in enumerate([crefs[0] - lses[0], crefs[1] - lses[1], dls[0], dls[1]]):
            cols = jnp.where(lane == j, col, cols)
        st_ref[0, 0] = _dot_hi(_eye(LANES), cols, NT)[:8]

    return hosted_call(
        body, rider, name=name, grid=(B, npair, nq),
        in_specs=[pl.BlockSpec((1, tq, LANES), lambda b, p, i: (b, i, QKV0 + p)),
                  pl.BlockSpec((1, S, LANES), lambda b, p, i: (b, 0, QKV0 + npair + p)),
                  pl.BlockSpec((1, S, LANES), lambda b, p, i: (b, 0, QKV0 + 2 * npair + p)),
                  pl.BlockSpec((1, tq, LANES), lambda b, p, i: (b, i, npair + p)),
                  pl.BlockSpec((1, 1, tq, LANES), lambda b, p, i: (b, p, i, 0)),
                  pl.BlockSpec((1, tq, LANES), lambda b, p, i: (b, i, 0)),
                  pl.BlockSpec((1, 8, S), lambda b, p, i: (b, 0, 0))],
        out_specs=[pl.BlockSpec((1, tq, LANES), lambda b, p, i: (b, i, p)),
                   pl.BlockSpec((1, 1, 8, tq), lambda b, p, i: (b, p, 0, i))],
        out_shape=[jax.ShapeDtypeStruct((B, S, FOX_W), BF16), jax.ShapeDtypeStruct((B, npair, 8, S), F32)],
        scratch_shapes=[pltpu.VMEM((2, nq, tq, tq), F32), pltpu.VMEM((2, nq, tq, tq), F32),
                        pltpu.VMEM((2, tq, LANES), F32)],
        args=(z, z, z, dcat, lse, c_col, c_row), vmem=56 << 20,
    )


def fox_bwd_dkdv(z, dcat, stats, c_col, *, name, rider=None):
    B, S, _ = z.shape
    tk, nq = FOX_T, S // FOX_T
    npair = FOX_HEADS // 2
    inv = 1.0 / math.sqrt(FOX_HEAD_DIM)

    def body(q_ref, k_ref, v_ref, do_ref, st_ref, cc_ref, dk_ref, dv_ref, dc_ref, dk_scr, dv_scr, dc_scr):
        p, kt = pl.program_id(1), pl.program_id(2)
        lane = lax.broadcasted_iota(jnp.int32, (tk, LANES), 1)
        masks = [(lane < FOX_HEAD_DIM) == (hh == 0) for hh in range(2)]
        k = k_ref[0]
        v = v_ref[0]
        khs = [jnp.where(masks[hh], k, jnp.zeros_like(k)) for hh in range(2)]
        vhs = [jnp.where(masks[hh], v, jnp.zeros_like(v)) for hh in range(2)]
        ccbs = [jnp.broadcast_to(_lane_pick(cc_ref[0], 2 * p + hh), (tk, tk)) for hh in range(2)]
        dk_scr[...] = jnp.zeros_like(dk_scr)
        dv_scr[...] = jnp.zeros_like(dv_scr)
        dc_scr[...] = jnp.zeros_like(dc_scr)

        def tile(qb, diagonal):
            q0 = pl.multiple_of(qb * tk, tk)
            qs = q_ref[0, pl.ds(q0, tk), :] * inv
            do_b = do_ref[0, pl.ds(q0, tk), :].astype(BF16)
            for hh in range(2):
                st = lax.dot_general(khs[hh], qs, NT, preferred_element_type=F32)
                pr = jnp.exp(st - ccbs[hh] + st_ref[0, 0, pl.ds(hh, 1), pl.ds(q0, tk)])
                if diagonal:
                    pr = jnp.where(_causal(tk, True), pr, 0.0)
                dp = lax.dot_general(vhs[hh], do_b, NT, preferred_element_type=F32)
                ds = pr * (dp - st_ref[0, 0, pl.ds(2 + hh, 1), pl.ds(q0, tk)])
                dv_scr[...] += jnp.dot(pr.astype(BF16), jnp.where(masks[hh], do_b, jnp.zeros_like(do_b)),
                                       preferred_element_type=F32)
                dk_scr[...] += jnp.dot(ds.astype(BF16), jnp.where(masks[hh], qs, jnp.zeros_like(qs)),
                                       preferred_element_type=F32)
                dc_scr[hh] -= _fold_lanes(ds, jnp.add)

        def later(qb, carry):
            tile(qb, False)
            return carry

        tile(kt, True)
        lax.fori_loop(kt + 1, nq, later, 0)
        dk_ref[0] = dk_scr[...].astype(BF16)
        dv_ref[0] = dv_scr[...].astype(BF16)
        dcs = [jnp.sum(dc_scr[hh], axis=-1, keepdims=True) for hh in range(2)]
        dc_ref[0, 0] = jnp.where(lane == 2 * p, dcs[0], jnp.where(lane == 2 * p + 1, dcs[1], 0.0))

    full = lambda col: pl.BlockSpec((1, S, LANES), col)
    tile_spec = lambda col: pl.BlockSpec((1, tk, LANES), col)
    return hosted_call(
        body, rider, name=name, grid=(B, npair, nq),
        in_specs=[full(lambda b, p, t: (b, 0, QKV0 + p)),
                  tile_spec(lambda b, p, t: (b, t, QKV0 + npair + p)),
                  tile_spec(lambda b, p, t: (b, t, QKV0 + 2 * npair + p)),
                  full(lambda b, p, t: (b, 0, npair + p)),
                  pl.BlockSpec((1, 1, 8, S), lambda b, p, t: (b, p, 0, 0)),
                  tile_spec(lambda b, p, t: (b, t, 0))],
        out_specs=[tile_spec(lambda b, p, t: (b, t, p)), tile_spec(lambda b, p, t: (b, t, p)),
                   pl.BlockSpec((1, 1, tk, LANES), lambda b, p, t: (b, p, t, 0))],
        out_shape=[jax.ShapeDtypeStruct((B, S, FOX_W), BF16)] * 2
        + [jax.ShapeDtypeStruct((B, npair, S, LANES), F32)],
        scratch_shapes=[pltpu.VMEM((tk, LANES), F32), pltpu.VMEM((tk, LANES), F32),
                        pltpu.VMEM((2, tk, LANES), F32)],
        args=(z, z, z, dcat, stats, c_col),
    )


def xattn_fwd(qm, kv, *, name, tq=512):
    B, S, D = qm.shape
    M = kv.shape[1]
    tq = min(tq, S)
    inv = 1.0 / math.sqrt(MEM_HEAD_DIM)

    nq = S // tq

    def body(q_ref, kv_ref, o_ref, ot_ref):
        for h in range(MEM_HEADS):
            c0 = h * MEM_HEAD_DIM
            qh = q_ref[0, :, c0:c0 + MEM_HEAD_DIM]
            kh = kv_ref[0, :, c0:c0 + MEM_HEAD_DIM]
            vh = kv_ref[0, :, D + c0:D + c0 + MEM_HEAD_DIM]
            s = lax.dot_general(qh, kh, NT, preferred_element_type=F32) * inv
            e = jnp.exp(s - jnp.max(s, axis=-1, keepdims=True))
            o = jnp.dot(e.astype(BF16), vh, preferred_element_type=F32) / jnp.sum(e, axis=-1, keepdims=True)
            o_ref[0, :, c0:c0 + MEM_HEAD_DIM] = o.astype(BF16)
            ot_ref[c0:c0 + MEM_HEAD_DIM, :] = o.T.astype(BF16)

    return _call(
        body, name=name, grid=(B, nq),
        in_specs=[pl.BlockSpec((1, tq, D), lambda b, i: (b, i, 0)),
                  pl.BlockSpec((1, M, 2 * D), lambda b, i: (b, 0, 0))],
        out_specs=[pl.BlockSpec((1, tq, D), lambda b, i: (b, i, 0)),
                   pl.BlockSpec((D, tq), lambda b, i: (0, b * nq + i))],
        out_shape=[jax.ShapeDtypeStruct((B, S, D), BF16), jax.ShapeDtypeStruct((D, B * S), BF16)],
        compiler_params=_params(("parallel", "parallel")),
    )(qm, kv)


def xattn_bwd(qm, kv, do, *, name, tq=512):
    B, S, D = qm.shape
    M = kv.shape[1]
    tq = min(tq, S)
    inv = 1.0 / math.sqrt(MEM_HEAD_DIM)

    def body(q_ref, kv_ref, do_ref, dq_ref, dkv_ref):
        @pl.when(pl.program_id(1) == 0)
        def _():
            dkv_ref[...] = jnp.zeros_like(dkv_ref)

        for h in range(MEM_HEADS):
            c0 = h * MEM_HEAD_DIM
            qh = q_ref[0, :, c0:c0 + MEM_HEAD_DIM]
            kh = kv_ref[0, :, c0:c0 + MEM_HEAD_DIM]
            vh = kv_ref[0, :, D + c0:D + c0 + MEM_HEAD_DIM]
            doh = do_ref[0, :, c0:c0 + MEM_HEAD_DIM]
            s = lax.dot_general(qh, kh, NT, preferred_element_type=F32) * inv
            e = jnp.exp(s - jnp.max(s, axis=-1, keepdims=True))
            pr = e / jnp.sum(e, axis=-1, keepdims=True)
            dp = lax.dot_general(doh, vh, NT, preferred_element_type=F32)
            ds = pr * (dp - jnp.sum(pr * dp, axis=-1, keepdims=True))
            ds_b = ds.astype(BF16)
            dq_ref[0, :, c0:c0 + MEM_HEAD_DIM] = (jnp.dot(ds_b, kh, preferred_element_type=F32) * inv).astype(BF16)
            dkv_ref[0, :, c0:c0 + MEM_HEAD_DIM] += lax.dot_general(ds_b, qh, TN, preferred_element_type=F32) * inv
            dkv_ref[0, :, D + c0:D + c0 + MEM_HEAD_DIM] += lax.dot_general(
                pr.astype(BF16), doh, TN, preferred_element_type=F32)

    row = pl.BlockSpec((1, tq, D), lambda b, i: (b, i, 0))
    kvs = pl.BlockSpec((1, M, 2 * D), lambda b, i: (b, 0, 0))
    return _call(
        body, name=name, grid=(B, S // tq), in_specs=[row, kvs, row], out_specs=[row, kvs],
        out_shape=[jax.ShapeDtypeStruct((B, S, D), BF16), jax.ShapeDtypeStruct((B, M, 2 * D), F32)],
        compiler_params=_params(("parallel", "arbitrary")),
    )(qm, kv, do)


SWIGLU_TN = 1408


def _chunks(n, w=256):
    return [(c0, min(w, n - c0)) for c0 in range(0, n, w)]


def mm_swiglu_fwd(hf, w_gu, *, name, tm=512):
    T, D = hf.shape
    Fh = w_gu.shape[1] // 2
    tm, tn = min(tm, T), SWIGLU_TN
    nj = Fh // tn
    assert Fh % tn == 0 and T % tm == 0

    def body(a_ref, bg_ref, bu_ref, g_ref, u_ref, o_ref, ot_ref):
        a = a_ref[...]
        for c0, cw in _chunks(tn):
            cols = pl.ds(c0, cw)
            g = jnp.dot(a, bg_ref[:, cols], preferred_element_type=F32)
            u = jnp.dot(a, bu_ref[:, cols], preferred_element_type=F32)
            act = g * _sigmoid(g) * u
            g_ref[:, cols] = g.astype(BF16)
            u_ref[:, cols] = u.astype(BF16)
            o_ref[:, cols] = act.astype(BF16)
            ot_ref[cols, :] = act.T.astype(BF16)

    tile = pl.BlockSpec((tm, tn), lambda i, j: (i, j))
    return _call(
        body, name=name, grid=(T // tm, nj),
        in_specs=[pl.BlockSpec((tm, D), lambda i, j: (i, 0)), pl.BlockSpec((D, tn), lambda i, j: (0, j)),
                  pl.BlockSpec((D, tn), lambda i, j: (0, nj + j))],
        out_specs=[tile, tile, tile, pl.BlockSpec((tn, tm), lambda i, j: (j, i))],
        out_shape=[jax.ShapeDtypeStruct((T, Fh), BF16)] * 3 + [jax.ShapeDtypeStruct((Fh, T), BF16)],
        compiler_params=_params(("parallel", "parallel"), 48 << 20),
    )(hf, w_gu, w_gu)


def mm_swiglu_bwd(dx, w_down, g, u, *, name, tm=512):
    T, D = dx.shape
    Fh = w_down.shape[0]
    tm, tn = min(tm, T), SWIGLU_TN
    assert Fh % tn == 0 and T % tm == 0

    def body(a_ref, b_ref, g_ref, u_ref, dg_ref, du_ref):
        a = a_ref[...].astype(BF16)
        for c0, cw in _chunks(tn):
            cols = pl.ds(c0, cw)
            d = lax.dot_general(a, b_ref[cols, :], NT, preferred_element_type=F32)
            gv = g_ref[:, cols].astype(F32)
            uv = u_ref[:, cols].astype(F32)
            sg = _sigmoid(gv)
            dg_ref[:, cols] = (d * uv * (sg * (1.0 + gv * (1.0 - sg)))).astype(BF16)
            du_ref[:, cols] = (d * gv * sg).astype(BF16)

    tile = pl.BlockSpec((tm, tn), lambda i, j: (i, j))
    return _call(
        body, name=name, grid=(T // tm, Fh // tn),
        in_specs=[pl.BlockSpec((tm, D), lambda i, j: (i, 0)), pl.BlockSpec((tn, D), lambda i, j: (j, 0)), tile, tile],
        out_specs=[tile, tile],
        out_shape=[jax.ShapeDtypeStruct((T, Fh), BF16)] * 2,
        compiler_params=_params(("parallel", "parallel"), 48 << 20),
    )(dx, w_down, g, u)


LATE_MID = ("w_out", "w_mq", "w_mo")
LATE_KV = ("w_mkv",)
LATE_FFN = ("w_gu", "w_down")
LATE = LATE_MID + LATE_KV + LATE_FFN
RS_GROUPS = (("w_gu", "w_down"), ("w_out", "w_mq", "w_mkv", "w_mo"), ("w_in",))


def pair_sums(names, g42, got):
    return {n: pair_sum(g, o, name="rs_pair_sum_" + n) for n, g, o in zip(names, g42, got)}


def local_step(x, mem, target, sp, first_shards, late_shards):
    B, S, D = x.shape
    T = B * S
    M = mem.shape[1]
    row = lambda v: v.reshape(1, -1).astype(F32)
    g_mix, g_x, g_mem, g_ffn, g_final = (row(sp[k]) for k in ("g_mix", "g_x", "g_mem", "g_ffn", "g_final"))
    conv_b, ln_g, ln_b = row(sp["conv_b"]), row(sp["ln_g"]), row(sp["ln_b"])
    b_f = jnp.pad(row(sp["b_f"]), ((0, 0), (0, LANES - FOX_HEADS)))
    n_ug, n_main = 2 * CONV_CH, 2 * CONV_CH + 3 * FOX_W

    x2d = x.reshape(T, D)
    h, h_t, partly = rmsnorm_fwd(x2d, g_mix, name="rms_mix", rider=AllGatherStage1(first_shards))
    w_in8, cw8 = all_gather_stage2(partly, name="ag_first_stage2")
    w_in_full = _full_from_gathered("w_in", w_in8)
    conv_w = cw8.transpose(1, 0, 2).reshape(HALO, -1)
    w_main, w_ug, w_qkv = w_in_full[:, :n_main], w_in_full[:, :n_ug], w_in_full[:, n_ug:n_main]
    w_f = jnp.pad(w_in_full[:, n_main:], ((0, 0), (0, LANES - FOX_HEADS)))
    z = matmul(h, w_main, out_dtype=BF16, tn=n_main, name="mm_in")
    z3 = z.reshape(B, S, n_main)
    n_mid, n_kv = len(LATE_MID), len(LATE_MID) + len(LATE_KV)
    (conv_out, conv_t), partly_mid = conv_branch_fwd(z3, conv_w, conv_b, ln_g, ln_b, name="conv_fwd",
                                                     rider=AllGatherStage1(late_shards[:n_mid]))
    (f_raw, c_col, c_row), partly_kv = fgate_fwd(h.reshape(B, S, D), w_f, b_f, name="fgate_fwd",
                                                 rider=AllGatherStage1(late_shards[n_mid:n_kv]))
    (att, lse, att_t), partly_ffn = fox_fwd(z3, c_col, c_row, name="fox_fwd",
                                            rider=AllGatherStage1(late_shards[n_kv:]))
    gathered = all_gather_stage2(partly_mid + partly_kv + partly_ffn, name="ag_late_stage2")
    wf = {n: _full_from_gathered(n, blk) for n, blk in zip(LATE, gathered)}
    x1 = matmul([conv_out.reshape(T, CONV_CH), att.reshape(T, FOX_W)],
                [wf["w_out"], wf["w_out"]], b_blk=[0, 1], out_dtype=F32, res=x2d, tn=D, name="mm_out")
    hx, hx_t = rmsnorm_fwd(x1, g_x, name="rms_x")
    qm = matmul(hx, wf["w_mq"], out_dtype=BF16, tn=D, name="mm_mq")
    mem2d = mem.reshape(B * M, D)
    mem_n, mem_n_t = rmsnorm_fwd(mem2d, g_mem, name="rms_mem")
    kv = matmul(mem_n, wf["w_mkv"], out_dtype=BF16, tn=2 * D, name="mm_mkv").reshape(B, M, 2 * D)
    o, o_t = xattn_fwd(qm.reshape(B, S, D), kv, name="xattn_fwd")
    o = o.reshape(T, D)
    x2 = matmul(o, wf["w_mo"], out_dtype=F32, res=x1, tn=D, name="mm_mo")
    hf, hf_t = rmsnorm_fwd(x2, g_ffn, name="rms_ffn")
    gate, up, act, act_t = mm_swiglu_fwd(hf, wf["w_gu"], name="mm_gu")
    x3 = matmul(act, wf["w_down"], out_dtype=F32, res=x2, tn=D, name="mm_down")
    dx3, dg_final, loss = final_loss_bwd(x3, g_final, target.reshape(T, D), name="loss_bwd")
    gw = {}
    gw["w_down"] = matmul(act_t, dx3, out_dtype=BF16, tm=1408, tn=256, name="dw_down")
    dgate, dup = mm_swiglu_bwd(dx3, wf["w_down"], gate, up, name="dx_down")
    gw["w_gu"] = [matmul(hf_t, dgate, out_dtype=BF16, tn=SWIGLU_TN, name="dw_gate"),
                  matmul(hf_t, dup, out_dtype=BF16, tn=SWIGLU_TN, name="dw_up")]
    g42 = [_shards_from_full(n, gw[n]) for n in RS_GROUPS[0]]
    dhf, got = matmul([dgate, dup], [wf["w_gu"], wf["w_gu"]], b_blk=[0, 1], tb=True, out_dtype=BF16,
                      tm=256, tn=D, name="dx_gu", rider=SiblingExchange(g42))
    parts = pair_sums(RS_GROUPS[0], g42, got)
    dx2, dg_ffn = rmsnorm_bwd(x2, g_ffn, dhf, dx3, name="rms_ffn_bwd")
    gw["w_mo"] = matmul(o_t, dx2, out_dtype=BF16, name="dw_mo")
    do = matmul(dx2, wf["w_mo"], tb=True, out_dtype=BF16, tn=D, name="dx_mo")
    dqm, dkv = xattn_bwd(qm.reshape(B, S, D), kv, do.reshape(B, S, D), name="xattn_bwd")
    dqm = dqm.reshape(T, D)
    dkv = dkv.reshape(B * M, 2 * D)
    gw["w_mq"] = matmul(hx_t, dqm, out_dtype=BF16, tn=D, name="dw_mq")
    dhx = matmul(dqm, wf["w_mq"], tb=True, out_dtype=BF16, tn=D, name="dx_mq")
    gw["w_mkv"] = matmul(mem_n_t, dkv, out_dtype=BF16, tn=D, name="dw_mkv")
    dmem_n = matmul(dkv, wf["w_mkv"], tb=True, out_dtype=BF16, tn=D, name="dx_mkv")
    _, dg_mem = rmsnorm_bwd(mem2d, g_mem, dmem_n, None, name="rms_mem_bwd")
    dx1, dg_x = rmsnorm_bwd(x1, g_x, dhx, dx2, name="rms_x_bwd")
    gw["w_out"] = jnp.concatenate([matmul(conv_t, dx1, out_dtype=BF16, name="dw_out_conv"),
                                   matmul(att_t, dx1, out_dtype=BF16, name="dw_out_att")], axis=0)
    g42 = [_shards_from_full(n, gw[n]) for n in RS_GROUPS[1]]
    dcat, got = matmul(dx1, wf["w_out"], tb=True, out_dtype=BF16, tn=D, name="dx_out", rider=SiblingExchange(g42))
    dcat = dcat.reshape(B, S, D)
    parts.update(pair_sums(RS_GROUPS[1], g42, got))
    dy, dconv_w, dvec = conv_branch_bwd_a(z3, dcat, conv_w, conv_b, ln_g, ln_b, name="conv_bwd_a")
    dug = conv_branch_bwd_b(z3, dy, conv_w, name="conv_bwd_b")
    gots = {}
    (dq, stats), got = fox_bwd_dq(z3, dcat, lse, c_col, c_row, name="fox_bwd_dq",
                                  rider=ChipExchange([parts[n] for n in RS_GROUPS[0]]))
    gots.update(zip(RS_GROUPS[0], got))
    (dk, dv, dc), got = fox_bwd_dkdv(z3, dcat, stats, c_col, name="fox_bwd_dkdv",
                                     rider=ChipExchange([parts[n] for n in RS_GROUPS[1]]))
    gots.update(zip(RS_GROUPS[1], got))
    df, db_f = fgate_bwd(dc, f_raw, b_f, name="fgate_bwd")
    dug2 = dug.reshape(T, n_ug)
    dqkv = jnp.concatenate([dq, dk, dv], axis=-1).reshape(T, 3 * FOX_W)
    df2 = df.reshape(T, LANES)
    dw_in = [matmul(h_t, dug2, out_dtype=BF16, tn=n_ug, name="dw_in_ug"),
             matmul(h_t, dqkv, out_dtype=BF16, tn=3 * FOX_W, name="dw_in_qkv"),
             matmul(h_t, df2, out_dtype=BF16, name="dw_f")[:, :FOX_HEADS]]
    g42 = [_shards_from_full("w_in", dw_in)]
    parts.update(pair_sums(RS_GROUPS[2], g42, run_rider(SiblingExchange(g42), name="rs_sibling_in")))
    dh, (gots["w_in"],) = matmul([dug2, dqkv, df2], [w_ug, w_qkv, w_f], tb=True, out_dtype=F32, tn=D,
                                 name="dx_in", rider=ChipExchange([parts["w_in"]]))
    dx, dg_mix = rmsnorm_bwd(x2d, g_mix, dh, dx1, name="rms_mix_bwd")
    gs = dict(g_mix=dg_mix, b_f=db_f[:, :FOX_HEADS], conv_w=dconv_w[:CONV_K], conv_b=dvec[0:1],
              ln_g=dvec[1:2], ln_b=dvec[2:3], g_x=dg_x, g_mem=dg_mem, g_ffn=dg_ffn, g_final=dg_final)
    return loss, dx.reshape(B, S, D), gs, {n: (parts[n], gots[n]) for n in BIG}


def _me():
    return lax.axis_index("x"), lax.axis_index("y"), lax.axis_index("c")


def _any_specs(n):
    return [pl.BlockSpec(memory_space=pl.ANY)] * n


def all_gather(xs, *, name):
    n = len(xs)

    def body(*refs):
        x_refs, out_refs = refs[:n], refs[n:2 * n]
        send_sems, recv_sems, local_sems = refs[2 * n:]
        x, y, c = _me()
        me, sibling = (x, y, c), (x, y, 1 - c)
        chips = [(1 - x, y), (x, 1 - y), (1 - x, 1 - y)]

        def slot(a, px, py, pc):
            return out_refs[a].at[4 * px + 2 * py + pc]

        def copy(a, k, block, to, own=False):
            return pltpu.make_async_remote_copy(
                src_ref=x_refs[a] if own else slot(a, *block), dst_ref=slot(a, *block),
                send_sem=send_sems.at[k, a], recv_sem=recv_sems.at[k, a], device_id=to, device_id_type=MESH)

        mine = [pltpu.make_async_copy(x_refs[a], slot(a, *me), local_sems.at[a]) for a in range(n)]
        first = [copy(a, 0, me, sibling, own=True) for a in range(n)]
        first += [copy(a, 1 + j, me, (*chip, c), own=True) for j, chip in enumerate(chips) for a in range(n)]
        for cp in mine + first:
            cp.start()
        passed = []
        for j, chip in enumerate(chips):
            for a in range(n):
                copy(a, 1 + j, (*chip, c), me).wait_recv()
                passed.append(copy(a, 4 + j, (*chip, c), sibling))
                passed[-1].start()
        for a in range(n):
            copy(a, 0, sibling, me).wait_recv()
            for j, chip in enumerate(chips):
                copy(a, 4 + j, (*chip, 1 - c), me).wait_recv()
        for cp in first + passed:
            cp.wait_send()
        for cp in mine:
            cp.wait()

    return _call(
        body, name=name, in_specs=_any_specs(n), out_specs=_any_specs(n),
        out_shape=[jax.ShapeDtypeStruct((N_DEV,) + v.shape, v.dtype) for v in xs],
        scratch_shapes=[pltpu.SemaphoreType.DMA((7, n)), pltpu.SemaphoreType.DMA((7, n)),
                        pltpu.SemaphoreType.DMA((n,))],
    )(*xs)


class SiblingExchange:
    def __init__(self, gs):
        n = len(gs)
        self.n, self.inputs = n, list(gs)
        self.out_shape = [jax.ShapeDtypeStruct((4,) + g.shape[2:], g.dtype) for g in gs]
        self.scratch = [pltpu.SemaphoreType.DMA((n,)), pltpu.SemaphoreType.DMA((n,))]

    def _copies(self, g_refs, out_refs, sems):
        send_sems, recv_sems = sems
        x, y, c = _me()
        return [pltpu.make_async_remote_copy(
            src_ref=g_refs[a].at[:, 1 - c], dst_ref=out_refs[a], send_sem=send_sems.at[a],
            recv_sem=recv_sems.at[a], device_id=(x, y, 1 - c), device_id_type=MESH) for a in range(self.n)]

    def start(self, in_refs, out_refs, sems):
        for cp in self._copies(in_refs, out_refs, sems):
            cp.start()

    def finish(self, in_refs, out_refs, sems):
        for cp in self._copies(in_refs, out_refs, sems):
            cp.wait()


def run_rider(rider, *, name):
    return hosted_call(None, rider, name=name, grid=(), in_specs=[], out_specs=[], out_shape=[],
                       scratch_shapes=[], args=[])[1]


class ChipExchange:
    def __init__(self, ps):
        n = len(ps)
        self.n, self.inputs = n, list(ps)
        self.out_shape = [jax.ShapeDtypeStruct(p.shape, p.dtype) for p in ps]
        self.scratch = [pltpu.SemaphoreType.DMA((3, n)), pltpu.SemaphoreType.DMA((3, n))]

    def _copies(self, p_refs, out_refs, sems, outgoing):
        send_sems, recv_sems = sems
        x, y, c = _me()
        my_chip = 2 * x + y
        cps = []
        for k in range(3):
            px, py = x ^ ((k + 1) >> 1), y ^ ((k + 1) & 1)
            src, dst = (2 * px + py, my_chip) if outgoing else (my_chip, 2 * px + py)
            for a in range(self.n):
                cps.append(pltpu.make_async_remote_copy(
                    src_ref=p_refs[a].at[src], dst_ref=out_refs[a].at[dst], send_sem=send_sems.at[k, a],
                    recv_sem=recv_sems.at[k, a], device_id=(px, py, c), device_id_type=MESH))
        return cps

    def start(self, in_refs, out_refs, sems):
        for cp in self._copies(in_refs, out_refs, sems, True):
            cp.start()

    def finish(self, in_refs, out_refs, sems):
        for cp in self._copies(in_refs, out_refs, sems, False):
            cp.wait_recv()
        for cp in self._copies(in_refs, out_refs, sems, True):
            cp.wait_send()


class AllGatherStage1:
    def __init__(self, xs):
        n = len(xs)
        self.n, self.inputs = n, list(xs)
        self.out_shape = [jax.ShapeDtypeStruct((N_DEV,) + v.shape, v.dtype) for v in xs]
        self.scratch = [pltpu.SemaphoreType.DMA((4, n)), pltpu.SemaphoreType.DMA((4, n)),
                        pltpu.SemaphoreType.DMA((n,))]

    def _copies(self, x_refs, out_refs, sems, kind):
        send_sems, recv_sems, local_sems = sems
        x, y, c = _me()
        slot = lambda a, d: out_refs[a].at[4 * d[0] + 2 * d[1] + d[2]]
        if kind == "local":
            return [pltpu.make_async_copy(x_refs[a], slot(a, (x, y, c)), local_sems.at[a]) for a in range(self.n)]
        cps = []
        for k, peer in enumerate([(x, y, 1 - c), (1 - x, y, c), (x, 1 - y, c), (1 - x, 1 - y, c)]):
            for a in range(self.n):
                cps.append(pltpu.make_async_remote_copy(
                    src_ref=x_refs[a], dst_ref=slot(a, (x, y, c) if kind == "out" else peer),
                    send_sem=send_sems.at[k, a], recv_sem=recv_sems.at[k, a], device_id=peer, device_id_type=MESH))
        return cps

    def start(self, in_refs, out_refs, sems):
        for cp in self._copies(in_refs, out_refs, sems, "local") + self._copies(in_refs, out_refs, sems, "out"):
            cp.start()

    def finish(self, in_refs, out_refs, sems):
        for cp in self._copies(in_refs, out_refs, sems, "in"):
            cp.wait_recv()
        for cp in self._copies(in_refs, out_refs, sems, "out"):
            cp.wait_send()
        for cp in self._copies(in_refs, out_refs, sems, "local"):
            cp.wait()


def all_gather_stage2(outs, *, name):
    n = len(outs)

    def body(*refs):
        out_refs = refs[n:2 * n]
        send_sems, recv_sems = refs[2 * n:]
        x, y, c = _me()
        sends, recvs = [], []
        for k, (px, py) in enumerate([(1 - x, y), (x, 1 - y), (1 - x, 1 - y)]):
            for a in range(n):
                mk = lambda pc: pltpu.make_async_remote_copy(
                    src_ref=out_refs[a].at[4 * px + 2 * py + c], dst_ref=out_refs[a].at[4 * px + 2 * py + pc],
                    send_sem=send_sems.at[k, a], recv_sem=recv_sems.at[k, a], device_id=(x, y, 1 - c),
                    device_id_type=MESH)
                sends.append(mk(c))
                recvs.append(mk(1 - c))
        for cp in sends:
            cp.start()
        for cp in recvs:
            cp.wait_recv()
        for cp in sends:
            cp.wait_send()

    return _call(
        body, name=name, in_specs=_any_specs(n), out_specs=_any_specs(n),
        out_shape=[jax.ShapeDtypeStruct(o.shape, o.dtype) for o in outs],
        input_output_aliases={a: a for a in range(n)},
        scratch_shapes=[pltpu.SemaphoreType.DMA((3, n)), pltpu.SemaphoreType.DMA((3, n))],
    )(*outs)


def hosted_call(body, rider, *, name, grid, in_specs, out_specs, out_shape, scratch_shapes, args, vmem=None):
    n_in, n_out, n_scr = len(in_specs), len(out_specs), len(scratch_shapes)
    r_in, r_out = (len(rider.inputs), len(rider.out_shape)) if rider is not None else (0, 0)

    def wrapped(*refs):
        ins, refs = refs[:n_in], refs[n_in:]
        rins, refs = refs[:r_in], refs[r_in:]
        outs, refs = refs[:n_out], refs[n_out:]
        routs, refs = refs[:r_out], refs[r_out:]
        scr, rscr = refs[:n_scr], refs[n_scr:]
        ids = [pl.program_id(d) for d in range(len(grid))]
        first = functools.reduce(jnp.logical_and, [i == 0 for i in ids], True)
        last = functools.reduce(jnp.logical_and, [i == g - 1 for i, g in zip(ids, grid)], True)
        if rider is not None and grid:
            pl.when(first)(lambda: rider.start(rins, routs, rscr))
        elif rider is not None:
            rider.start(rins, routs, rscr)
        if body is not None:
            body(*ins, *outs, *scr)
        if rider is not None and grid:
            pl.when(last)(lambda: rider.finish(rins, routs, rscr))
        elif rider is not None:
            rider.finish(rins, routs, rscr)

    kw = dict(grid=grid) if grid else {}
    if grid or vmem is not None:
        kw["compiler_params"] = _params(("arbitrary",) * len(grid) if grid else None, vmem)
    res = _call(
        wrapped, name=name, in_specs=list(in_specs) + _any_specs(r_in), out_specs=list(out_specs) + _any_specs(r_out),
        out_shape=list(out_shape) + (rider.out_shape if rider is not None else []),
        scratch_shapes=list(scratch_shapes) + (rider.scratch if rider is not None else []), **kw,
    )(*args, *(rider.inputs if rider is not None else []))
    return list(res[:n_out]), list(res[n_out:])


def _pick_rows(r, target=256):
    best = None
    for d in range(16, min(r, target) + 1, 16):
        if r % d == 0:
            best = d
    return r if best is None else best


def pair_sum(g, got, *, name):
    _, _, R, C = g.shape
    tr = _pick_rows(R)

    def body(g_ref, got_ref, o_ref):
        mine = jnp.where(lax.axis_index("c") == 0, g_ref[:, 0], g_ref[:, 1])
        o_ref[...] = (mine.astype(F32) + got_ref[...].astype(F32)).astype(o_ref.dtype)

    return _call(
        body, name=name, grid=(R // tr,),
        in_specs=[pl.BlockSpec((4, 2, tr, C), lambda i: (0, 0, i, 0)), pl.BlockSpec((4, tr, C), lambda i: (0, i, 0))],
        out_specs=pl.BlockSpec((4, tr, C), lambda i: (0, i, 0)),
        out_shape=jax.ShapeDtypeStruct((4, R, C), g.dtype),
        compiler_params=_params(("parallel",)),
    )(g, got)


def chip_sum_adamw(p, got, w, m, v, *, name):
    _, R, C = p.shape
    assert w.shape == (1, R, C), (name, w.shape, p.shape)
    tr = _pick_rows(R)

    def body(p_ref, got_ref, w_ref, m_ref, v_ref, g_ref, d_ref, mo_ref, vo_ref):
        my_chip = 2 * lax.axis_index("x") + lax.axis_index("y")
        g = jnp.zeros((tr, C), F32)
        for j in range(4):
            g = g + jnp.where(my_chip == j, p_ref[j], got_ref[j]).astype(F32)
        g_ref[0] = g
        d_ref[0], mo_ref[0], vo_ref[0] = _adamw_math(w_ref[0], g, m_ref[0], v_ref[0])

    part = pl.BlockSpec((4, tr, C), lambda i: (0, i, 0))
    spec = pl.BlockSpec((1, tr, C), lambda i: (0, i, 0))
    return _call(
        body, name=name, grid=(R // tr,), in_specs=[part, part, spec, spec, spec], out_specs=[spec] * 4,
        out_shape=[jax.ShapeDtypeStruct((1, R, C), F32)] * 4,
        compiler_params=_params(("parallel",)),
    )(p, got, w, m, v)


def rows_sum(g8, *, name):
    _, R, C = g8.shape

    def body(g_ref, o_ref):
        acc = g_ref[0]
        for j in range(1, N_DEV):
            acc = acc + g_ref[j]
        o_ref[...] = acc

    return _call(body, name=name, out_shape=jax.ShapeDtypeStruct((R, C), F32))(g8)


def _adamw_math(w, g, m, v):
    m = ADAM_B1 * m + (1.0 - ADAM_B1) * g
    v = ADAM_B2 * v + (1.0 - ADAM_B2) * (g * g)
    m_hat = m / (1.0 - ADAM_B1 ** ADAM_STEP)
    v_hat = v / (1.0 - ADAM_B2 ** ADAM_STEP)
    delta = -ADAM_LR * (m_hat / (jnp.sqrt(v_hat) + ADAM_EPS) + ADAM_WD * w)
    return delta, m, v


def adamw_small(wgmv, *, name):
    n = len(wgmv)

    def body(*refs):
        ins, outs = refs[:4 * n], refs[4 * n:]
        for a in range(n):
            w_ref, g_ref, m_ref, v_ref = ins[4 * a:4 * a + 4]
            d, mn, vn = _adamw_math(w_ref[...], g_ref[...], m_ref[...], v_ref[...])
            outs[3 * a][...] = d
            outs[3 * a + 1][...] = mn
            outs[3 * a + 2][...] = vn

    flat = [t for tup in wgmv for t in tup]
    res = _call(
        body, name=name,
        out_shape=[jax.ShapeDtypeStruct(tup[0].shape, F32) for tup in wgmv for _ in range(3)],
    )(*flat)
    return [tuple(res[3 * a:3 * a + 3]) for a in range(n)]


BIG = ("w_in", "w_out", "w_mq", "w_mkv", "w_mo", "w_gu", "w_down")
COL_SHARDED = ("w_in", "w_mkv", "w_gu")
SMALL = ("g_mix", "b_f", "conv_w", "conv_b", "ln_g", "ln_b", "g_x", "g_mem", "g_ffn", "g_final")


def _full_from_gathered(n, blk):
    _, rr, cc = blk.shape
    if n in COL_SHARDED:
        return jnp.concatenate([blk[k] for k in range(N_DEV)], axis=1)
    return blk.reshape(N_DEV * rr, cc)


def _shards_from_full(n, g):
    pieces = g if isinstance(g, list) else [g]
    rr, cc = pieces[0].shape[0], sum(p.shape[1] for p in pieces)
    if n in COL_SHARDED:
        w = cc // N_DEV
        return jnp.stack([_columns(pieces, k * w, w) for k in range(N_DEV)]).reshape(4, 2, rr, w)
    return pieces[0].reshape(4, 2, rr // N_DEV, cc)


def _columns(pieces, start, width):
    out, c0 = [], 0
    for p in pieces:
        lo, hi = max(start, c0), min(start + width, c0 + p.shape[1])
        if lo < hi:
            out.append(p[:, lo - c0:hi - c0])
        c0 += p.shape[1]
    return out[0] if len(out) == 1 else jnp.concatenate(out, axis=1)


def _small_layout():
    sizes = dict(g_mix=1024, b_f=8, conv_w=CONV_K * CONV_CH, conv_b=512, ln_g=512, ln_b=512, g_x=1024,
                 g_mem=1024, g_ffn=1024, g_final=1024, loss=1)
    lay, r0 = {}, 0
    for n, sz in sizes.items():
        r = -(-sz // LANES)
        lay[n] = (r0, r, sz)
        r0 += r
    return lay, -(-r0 // 8) * 8


def kernel(x, mem, g_mix, w_in, b_f, conv_w, conv_b, ln_g, ln_b, w_out, g_x, g_mem, w_mq, w_mkv, w_mo, g_ffn, w_gu, w_down, g_final, loss_target, m_g_mix, m_w_in, m_b_f, m_conv_w, m_conv_b, m_ln_g, m_ln_b, m_w_out, m_g_x, m_g_mem, m_w_mq, m_w_mkv, m_w_mo, m_g_ffn, m_w_gu, m_w_down, m_g_final, v_g_mix, v_w_in, v_b_f, v_conv_w, v_conv_b, v_ln_g, v_ln_b, v_w_out, v_g_x, v_g_mem, v_w_mq, v_w_mkv, v_w_mo, v_g_ffn, v_w_gu, v_w_down, v_g_final):
    names = ["g_mix", "w_in", "b_f", "conv_w", "conv_b", "ln_g", "ln_b", "w_out", "g_x", "g_mem", "w_mq",
             "w_mkv", "w_mo", "g_ffn", "w_gu", "w_down", "g_final"]
    W = dict(zip(names, [g_mix, w_in, b_f, conv_w, conv_b, ln_g, ln_b, w_out, g_x, g_mem, w_mq, w_mkv, w_mo,
                         g_ffn, w_gu, w_down, g_final]))
    Mo = dict(zip(names, [m_g_mix, m_w_in, m_b_f, m_conv_w, m_conv_b, m_ln_g, m_ln_b, m_w_out, m_g_x, m_g_mem,
                          m_w_mq, m_w_mkv, m_w_mo, m_g_ffn, m_w_gu, m_w_down, m_g_final]))
    Vo = dict(zip(names, [v_g_mix, v_w_in, v_b_f, v_conv_w, v_conv_b, v_ln_g, v_ln_b, v_w_out, v_g_x, v_g_mem,
                          v_w_mq, v_w_mkv, v_w_mo, v_g_ffn, v_w_gu, v_w_down, v_g_final]))
    dev = 4 * lax.axis_index("x") + 2 * lax.axis_index("y") + lax.axis_index("c")

    two = lambda a: a.reshape(-1, a.shape[-1])
    cw_shard = jnp.pad(two(conv_w), ((0, HALO - CONV_K), (0, 0)))
    sp = dict(g_mix=g_mix, b_f=b_f, conv_b=conv_b, ln_g=ln_g, ln_b=ln_b, g_x=g_x, g_mem=g_mem,
              g_ffn=g_ffn, g_final=g_final)
    loss_blk, grad_x, gs, reduced = local_step(x, mem, loss_target, sp, [two(w_in).astype(BF16), cw_shard],
                                               [two(W[n]).astype(BF16) for n in LATE])

    lay, rs = _small_layout()
    small = {**{n: gs[n] for n in SMALL}, "loss": loss_blk[:, :1]}
    parts = []
    for n, (r0, r, sz) in lay.items():
        flat = small[n].reshape(-1).astype(F32)
        parts.append(jnp.pad(flat, (0, r * LANES - sz)).reshape(r, LANES))
    spack = jnp.concatenate(parts, axis=0)
    spack = jnp.pad(spack, ((0, rs - spack.shape[0]), (0, 0)))
    ssum = rows_sum(all_gather([spack], name="ag_small")[0], name="small_sum")
    gsmall = {n: ssum[r0:r0 + r].reshape(-1)[:sz] for n, (r0, r, sz) in lay.items()}
    loss = gsmall["loss"].reshape(())

    grads, delta, new_m, new_v = {}, {}, {}, {}
    for n in BIG:
        p, o = reduced[n]
        grads[n], delta[n], new_m[n], new_v[n] = chip_sum_adamw(p, o, W[n], Mo[n], Vo[n], name="adamw_" + n)
    for n in SMALL:
        if n == "conv_w":
            full = gsmall[n].reshape(CONV_K, CONV_CH)
            ncol = conv_w.shape[-1]
            grads[n] = lax.dynamic_slice(full, (0, dev * ncol), (CONV_K, ncol)).reshape(conv_w.shape)
        else:
            grads[n] = gsmall[n].reshape(W[n].shape)
    upd = adamw_small([(two(W[n]), two(grads[n]), two(Mo[n]), two(Vo[n])) for n in SMALL], name="adamw_small")
    for n, (d, mn, vn) in zip(SMALL, upd):
        shp = W[n].shape
        delta[n], new_m[n], new_v[n] = d.reshape(shp), mn.reshape(shp), vn.reshape(shp)
    return (loss, grad_x, *[grads[n] for n in names], *[delta[n] for n in names],
            *[new_m[n] for n in names], *[new_v[n] for n in names])
```

```python
import functools
import math

import jax
import jax.numpy as jnp
from jax import lax
from jax.experimental import pallas as pl
from jax.experimental.pallas import tpu as pltpu

F32 = jnp.float32
BF16 = jnp.bfloat16
EPS = 1e-6
N_DEV = 8
CONV_CH = 512
CONV_K = 31
FOX_HEADS = 8
FOX_HEAD_DIM = 64
FOX_W = 512
MEM_HEADS = 4
MEM_HEAD_DIM = 256
HALO = 32
LANES = 128
ADAM_LR, ADAM_B1, ADAM_B2, ADAM_EPS, ADAM_WD, ADAM_STEP = 0.001, 0.9, 0.999, 1e-08, 0.01, 10
NEG = -1e30
VMEM_CAP = 60 * 1024 * 1024
MESH = pl.DeviceIdType.MESH


def _call(body, **kw):
    kw["out_shape"] = jax.tree.map(lambda s: pltpu.HBM(s.shape, s.dtype), kw["out_shape"])
    call = pl.pallas_call(body, **kw)
    return lambda *args: call(*[pltpu.with_memory_space_constraint(a, pltpu.HBM) for a in args])


def _params(sem=None, vmem=None, collective_id=None):
    kw = {} if collective_id is None else {"collective_id": collective_id}
    if sem is not None:
        kw["dimension_semantics"] = sem
    if vmem is not None:
        kw["vmem_limit_bytes"] = int(min(VMEM_CAP, vmem))
    return pltpu.CompilerParams(**kw)


def _nbytes(shape, dtype):
    return math.prod(shape) * jnp.dtype(dtype).itemsize


def _pick(n, target):
    best = None
    for d in range(LANES, min(n, target) + 1, LANES):
        if n % d == 0:
            best = d
    return n if best is None else best


def matmul(a, b, *, tb=False, out_dtype, res=None, tm=512, tn=512, name, rider=None, b_blk=None):
    a_list = list(a) if isinstance(a, (list, tuple)) else [a]
    b_list = list(b) if isinstance(b, (list, tuple)) else [b]
    n = len(a_list)
    assert len(b_list) == n
    M = a_list[0].shape[0]
    N = b_list[0].shape[0] if tb else b_list[0].shape[1]
    tm, tn = _pick(M, tm), _pick(N, tn)
    assert M % tm == 0 and N % tn == 0, (name, M, N, tm, tn)
    dn = (((1,), (1 if tb else 0,)), ((), ()))

    def body(*refs):
        acc = None
        for a_ref, b_ref in zip(refs[:n], refs[n:2 * n]):
            p = lax.dot_general(a_ref[...].astype(BF16), b_ref[...].astype(BF16), dn, preferred_element_type=F32)
            acc = p if acc is None else acc + p
        if res is not None:
            acc = acc + refs[2 * n][...].astype(F32)
        refs[-1][...] = acc.astype(out_dtype)

    o_spec = pl.BlockSpec((tm, tn), lambda i, j: (i, j))
    in_specs, est = [], 2 * _nbytes((tm, tn), out_dtype) + 2 * _nbytes((tm, tn), F32)
    for av in a_list:
        assert av.shape[0] == M
        in_specs.append(pl.BlockSpec((tm, av.shape[1]), lambda i, j: (i, 0)))
        est += (2 * jnp.dtype(av.dtype).itemsize + (av.dtype != BF16) * 2) * tm * av.shape[1]
    for idx, (av, bv) in enumerate(zip(a_list, b_list)):
        K = av.shape[1]
        kb = 0 if b_blk is None else b_blk[idx]
        assert bv.shape[0 if tb else 1] == N and bv.shape[1 if tb else 0] >= (kb + 1) * K, (name, av.shape, bv.shape)
        assert b_blk is not None or bv.shape[1 if tb else 0] == K, (name, av.shape, bv.shape)
        in_specs.append(pl.BlockSpec((tn, K), lambda i, j, kb=kb: (j, kb)) if tb
                        else pl.BlockSpec((K, tn), lambda i, j, kb=kb: (kb, j)))
        est += (2 * jnp.dtype(bv.dtype).itemsize + (bv.dtype != BF16) * 2) * tn * K
    args = a_list + b_list
    if res is not None:
        in_specs.append(o_spec)
        args.append(res)
        est += 2 * _nbytes((tm, tn), res.dtype)
    (out,), rode = hosted_call(
        body, rider, name=name, grid=(M // tm, N // tn), in_specs=in_specs, out_specs=[o_spec],
        out_shape=[jax.ShapeDtypeStruct((M, N), out_dtype)], scratch_shapes=[],
        args=args, vmem=est + (8 << 20),
    )
    return out if rider is None else (out, rode)


def _rms_scale(x):
    return lax.rsqrt(jnp.mean(x * x, axis=-1, keepdims=True) + EPS)


def rmsnorm_fwd(x, g, *, name, tm=512, rider=None):
    T, D = x.shape
    tm = min(tm, T)

    def body(x_ref, g_ref, o_ref, ot_ref):
        xv = x_ref[...]
        h = xv * _rms_scale(xv) * g_ref[...]
        o_ref[...] = h.astype(BF16)
        ot_ref[...] = h.T.astype(BF16)

    (h, h_t), rode = hosted_call(
        body, rider, name=name, grid=(T // tm,),
        in_specs=[pl.BlockSpec((tm, D), lambda i: (i, 0)), pl.BlockSpec((1, D), lambda i: (0, 0))],
        out_specs=[pl.BlockSpec((tm, D), lambda i: (i, 0)), pl.BlockSpec((D, tm), lambda i: (0, i))],
        out_shape=[jax.ShapeDtypeStruct((T, D), BF16), jax.ShapeDtypeStruct((D, T), BF16)],
        scratch_shapes=[], args=(x, g),
    )
    return (h, h_t) if rider is None else (h, h_t, rode)


def _rms_bwd_math(xv, gv, dh):
    r = _rms_scale(xv)
    xh = xv * r
    dg = jnp.sum(dh * xh, axis=0, keepdims=True)
    dxh = dh * gv
    dx = r * (dxh - xh * jnp.mean(dxh * xh, axis=-1, keepdims=True))
    return dx, dg


def rmsnorm_bwd(x, g, dh, dres, *, name, tm=256):
    T, D = x.shape
    tm = min(tm, T)

    def body(*refs):
        if dres is not None:
            x_ref, g_ref, dh_ref, dr_ref, dx_ref, dg_ref = refs
        else:
            x_ref, g_ref, dh_ref, dx_ref, dg_ref = refs
        dx, dg = _rms_bwd_math(x_ref[...], g_ref[...], dh_ref[...].astype(F32))
        if dres is not None:
            dx = dx + dr_ref[...]
        dx_ref[...] = dx

        @pl.when(pl.program_id(0) == 0)
        def _():
            dg_ref[...] = jnp.zeros_like(dg_ref)

        dg_ref[...] += dg

    row = pl.BlockSpec((tm, D), lambda i: (i, 0))
    vec = pl.BlockSpec((1, D), lambda i: (0, 0))
    ins, args = [row, vec, row], [x, g, dh]
    if dres is not None:
        ins.append(row)
        args.append(dres)
    return _call(
        body, name=name, grid=(T // tm,), in_specs=ins, out_specs=[row, vec],
        out_shape=[jax.ShapeDtypeStruct((T, D), F32), jax.ShapeDtypeStruct((1, D), F32)],
        compiler_params=_params(("arbitrary",)),
    )(*args)


def final_loss_bwd(x, g, target, *, name, tm=256):
    T, D = x.shape
    tm = min(tm, T)

    def body(x_ref, g_ref, t_ref, dx_ref, dg_ref, l_ref):
        xv, gv = x_ref[...], g_ref[...]
        e = xv * _rms_scale(xv) * gv - t_ref[...]
        part = 0.5 * jnp.sum(jnp.mean(e * e, axis=-1, keepdims=True), axis=0, keepdims=True)
        dx, dg = _rms_bwd_math(xv, gv, e * (1.0 / D))
        dx_ref[...] = dx

        @pl.when(pl.program_id(0) == 0)
        def _():
            dg_ref[...] = jnp.zeros_like(dg_ref)
            l_ref[...] = jnp.zeros_like(l_ref)

        dg_ref[...] += dg
        l_ref[...] += jnp.broadcast_to(part, l_ref.shape)

    row = pl.BlockSpec((tm, D), lambda i: (i, 0))
    vec = pl.BlockSpec((1, D), lambda i: (0, 0))
    return _call(
        body, name=name, grid=(T // tm,), in_specs=[row, vec, row],
        out_specs=[row, vec, pl.BlockSpec((1, LANES), lambda i: (0, 0))],
        out_shape=[jax.ShapeDtypeStruct((T, D), F32), jax.ShapeDtypeStruct((1, D), F32),
                   jax.ShapeDtypeStruct((1, LANES), F32)],
        compiler_params=_params(("arbitrary",)),
    )(x, g, target)


def _sigmoid(v):
    return 1.0 / (1.0 + jnp.exp(-v))


def _glu(blk):
    u = blk[:, :CONV_CH].astype(F32)
    gt = blk[:, CONV_CH:].astype(F32)
    return u * _sigmoid(gt)


def _fill_causal_ext(ext, cur_ref, halo_ref, s, ts):
    ext[pl.ds(HALO, ts), :] = _glu(cur_ref[0])
    hal = _glu(halo_ref[0])
    ext[pl.ds(0, HALO), :] = jnp.where(s > 0, hal, 0.0)


SUBLANES = 8


def _make_shifted(ext, sh):
    n = ext.shape[0]
    full = ext[...]
    for r in range(1, SUBLANES):
        sh[r - 1] = pltpu.roll(full, n - r, 0)


def _tap(ext, sh, off, ts):
    r = off % SUBLANES
    return ext[pl.ds(off, ts), :] if r == 0 else sh[r - 1, pl.ds(off - r, ts), :]


def _causal_conv(ext, sh, w_ref, ts):
    acc = jnp.zeros((ts, CONV_CH), F32)
    for j in range(CONV_K):
        acc = acc + _tap(ext, sh, HALO - (CONV_K - 1) + j, ts) * w_ref[pl.ds(j, 1), :]
    return acc


def _ln_stats(y):
    mu = jnp.mean(y, axis=-1, keepdims=True)
    yc = y - mu
    rstd = lax.rsqrt(jnp.mean(yc * yc, axis=-1, keepdims=True) + EPS)
    return yc * rstd, rstd


def _conv_specs(ts, S):
    nh = ts // HALO
    cur = pl.BlockSpec((1, ts, 2 * CONV_CH), lambda b, s: (b, s, 0))
    halo = pl.BlockSpec((1, HALO, 2 * CONV_CH), lambda b, s: (b, jnp.maximum(s * nh - 1, 0), 0))
    w = pl.BlockSpec((HALO, CONV_CH), lambda b, s: (0, 0))
    vec = pl.BlockSpec((1, CONV_CH), lambda b, s: (0, 0))
    return cur, halo, w, vec


def conv_branch_fwd(ug, conv_w, conv_b, ln_g, ln_b, *, name, ts=256, rider=None):
    B, S, _ = ug.shape
    ts = min(ts, S)
    ns = S // ts
    cur, halo, w, vec = _conv_specs(ts, S)

    def body(cur_ref, halo_ref, w_ref, cb_ref, lg_ref, lb_ref, o_ref, ot_ref, ext, sh):
        _fill_causal_ext(ext, cur_ref, halo_ref, pl.program_id(1), ts)
        _make_shifted(ext, sh)
        y = _causal_conv(ext, sh, w_ref, ts) + cb_ref[...]
        yh, _ = _ln_stats(y)
        ln = yh * lg_ref[...] + lb_ref[...]
        out = ln * _sigmoid(ln)
        o_ref[0] = out.astype(BF16)
        ot_ref[...] = out.T.astype(BF16)

    return hosted_call(
        body, rider, name=name, grid=(B, ns), in_specs=[cur, halo, w, vec, vec, vec],
        out_specs=[pl.BlockSpec((1, ts, CONV_CH), lambda b, s: (b, s, 0)),
                   pl.BlockSpec((CONV_CH, ts), lambda b, s: (0, b * ns + s))],
        out_shape=[jax.ShapeDtypeStruct((B, S, CONV_CH), BF16), jax.ShapeDtypeStruct((CONV_CH, B * S), BF16)],
        scratch_shapes=[pltpu.VMEM((ts + HALO, CONV_CH), F32),
                        pltpu.VMEM((SUBLANES - 1, ts + HALO, CONV_CH), F32)],
        args=(ug, ug, conv_w, conv_b, ln_g, ln_b),
    )


def conv_branch_bwd_a(ug, dcat, conv_w, conv_b, ln_g, ln_b, *, name, ts=256):
    B, S, _ = ug.shape
    ts = min(ts, S)
    cur, halo, w, vec = _conv_specs(ts, S)

    def body(cur_ref, halo_ref, d_ref, w_ref, cb_ref, lg_ref, lb_ref, dy_ref, dw_ref, dv_ref, ext, sh):
        _fill_causal_ext(ext, cur_ref, halo_ref, pl.program_id(1), ts)
        _make_shifted(ext, sh)
        y = _causal_conv(ext, sh, w_ref, ts) + cb_ref[...]
        yh, rstd = _ln_stats(y)
        lg = lg_ref[...]
        ln = yh * lg + lb_ref[...]
        sg = _sigmoid(ln)
        dln = d_ref[0].astype(F32) * (sg * (1.0 + ln * (1.0 - sg)))
        dyh = dln * lg
        dy = rstd * (dyh - jnp.mean(dyh, axis=-1, keepdims=True)
                     - yh * jnp.mean(dyh * yh, axis=-1, keepdims=True))
        dy_ref[0] = dy

        @pl.when((pl.program_id(0) == 0) & (pl.program_id(1) == 0))
        def _():
            dw_ref[...] = jnp.zeros_like(dw_ref)
            dv_ref[...] = jnp.zeros_like(dv_ref)

        dv_ref[pl.ds(0, 1), :] += jnp.sum(dy, axis=0, keepdims=True)
        dv_ref[pl.ds(1, 1), :] += jnp.sum(dln * yh, axis=0, keepdims=True)
        dv_ref[pl.ds(2, 1), :] += jnp.sum(dln, axis=0, keepdims=True)
        for j in range(CONV_K):
            tap = _tap(ext, sh, HALO - (CONV_K - 1) + j, ts)
            dw_ref[pl.ds(j, 1), :] += jnp.sum(dy * tap, axis=0, keepdims=True)

    return _call(
        body, name=name, grid=(B, S // ts),
        in_specs=[cur, halo, pl.BlockSpec((1, ts, CONV_CH), lambda b, s: (b, s, 0)), w, vec, vec, vec],
        out_specs=[pl.BlockSpec((1, ts, CONV_CH), lambda b, s: (b, s, 0)),
                   pl.BlockSpec((HALO, CONV_CH), lambda b, s: (0, 0)),
                   pl.BlockSpec((8, CONV_CH), lambda b, s: (0, 0))],
        out_shape=[jax.ShapeDtypeStruct((B, S, CONV_CH), F32),
                   jax.ShapeDtypeStruct((HALO, CONV_CH), F32),
                   jax.ShapeDtypeStruct((8, CONV_CH), F32)],
        scratch_shapes=[pltpu.VMEM((ts + HALO, CONV_CH), F32),
                        pltpu.VMEM((SUBLANES - 1, ts + HALO, CONV_CH), F32)],
        compiler_params=_params(("arbitrary", "arbitrary")),
    )(ug, ug, dcat, conv_w, conv_b, ln_g, ln_b)


def conv_branch_bwd_b(ug, dy, conv_w, *, name, ts=256):
    B, S, _ = ug.shape
    ts = min(ts, S)
    nh, n_halo = ts // HALO, S // HALO

    def body(cur_ref, dy_ref, nxt_ref, w_ref, o_ref, ext, sh):
        last = pl.program_id(1) == pl.num_programs(1) - 1
        ext[pl.ds(0, ts), :] = dy_ref[0]
        ext[pl.ds(ts, HALO), :] = jnp.where(last, 0.0, nxt_ref[0])
        _make_shifted(ext, sh)
        da = jnp.zeros((ts, CONV_CH), F32)
        for j in range(CONV_K):
            da = da + _tap(ext, sh, CONV_K - 1 - j, ts) * w_ref[pl.ds(j, 1), :]
        blk = cur_ref[0]
        u = blk[:, :CONV_CH].astype(F32)
        sg = _sigmoid(blk[:, CONV_CH:].astype(F32))
        o_ref[0, :, :CONV_CH] = (da * sg).astype(BF16)
        o_ref[0, :, CONV_CH:] = (da * u * sg * (1.0 - sg)).astype(BF16)

    return _call(
        body, name=name, grid=(B, S // ts),
        in_specs=[pl.BlockSpec((1, ts, 2 * CONV_CH), lambda b, s: (b, s, 0)),
                  pl.BlockSpec((1, ts, CONV_CH), lambda b, s: (b, s, 0)),
                  pl.BlockSpec((1, HALO, CONV_CH), lambda b, s: (b, jnp.minimum((s + 1) * nh, n_halo - 1), 0)),
                  pl.BlockSpec((HALO, CONV_CH), lambda b, s: (0, 0))],
        out_specs=pl.BlockSpec((1, ts, 2 * CONV_CH), lambda b, s: (b, s, 0)),
        out_shape=jax.ShapeDtypeStruct((B, S, 2 * CONV_CH), BF16),
        scratch_shapes=[pltpu.VMEM((ts + HALO, CONV_CH), F32),
                        pltpu.VMEM((SUBLANES - 1, ts + HALO, CONV_CH), F32)],
        compiler_params=_params(("parallel", "parallel")),
    )(ug, dy, dy, conv_w)


def _tri(n, lower):
    r = lax.broadcasted_iota(jnp.int32, (n, n), 0)
    c = lax.broadcasted_iota(jnp.int32, (n, n), 1)
    return ((r >= c) if lower else (r <= c)).astype(F32)


def _eye(n):
    r = lax.broadcasted_iota(jnp.int32, (n, n), 0)
    c = lax.broadcasted_iota(jnp.int32, (n, n), 1)
    return (r == c).astype(F32)


def _dot_hi(a, b, dn):
    return lax.dot_general(a, b, dn, precision=lax.Precision.HIGHEST, preferred_element_type=F32)


NN = (((1,), (0,)), ((), ()))
NT = (((1,), (1,)), ((), ()))
TN = (((0,), (0,)), ((), ()))


def _log_sigmoid(v):
    e = jnp.exp(-jnp.abs(v))
    log1p_e = jnp.where(e < 1e-3, e * (1.0 - 0.5 * e), jnp.log(1.0 + e))
    return jnp.minimum(v, 0.0) - log1p_e


def fgate_fwd(h, w_f, b_f, *, name, ts=256, rider=None):
    B, S, D = h.shape
    ts = min(ts, S)

    def body(h_ref, w_ref, b_ref, f_ref, cc_ref, cr_ref, carry):
        @pl.when(pl.program_id(1) == 0)
        def _():
            carry[...] = jnp.zeros_like(carry)

        f = jnp.dot(h_ref[0], w_ref[...], preferred_element_type=F32)
        f_ref[0] = f
        logf = _log_sigmoid(f + b_ref[...])
        c = _dot_hi(_tri(ts, True), logf, NN) + carry[pl.ds(0, 1), :]
        cc_ref[0] = c
        carry[pl.ds(0, 1), :] = c[ts - 1:ts, :]
        cr_ref[0] = _dot_hi(_eye(LANES), c, NT)

    return hosted_call(
        body, rider, name=name, grid=(B, S // ts),
        in_specs=[pl.BlockSpec((1, ts, D), lambda b, s: (b, s, 0)),
                  pl.BlockSpec((D, LANES), lambda b, s: (0, 0)),
                  pl.BlockSpec((1, LANES), lambda b, s: (0, 0))],
        out_specs=[pl.BlockSpec((1, ts, LANES), lambda b, s: (b, s, 0)),
                   pl.BlockSpec((1, ts, LANES), lambda b, s: (b, s, 0)),
                   pl.BlockSpec((1, LANES, ts), lambda b, s: (b, 0, s))],
        out_shape=[jax.ShapeDtypeStruct((B, S, LANES), F32), jax.ShapeDtypeStruct((B, S, LANES), F32),
                   jax.ShapeDtypeStruct((B, LANES, S), F32)],
        scratch_shapes=[pltpu.VMEM((8, LANES), F32)],
        args=(h, w_f, b_f),
    )


def fgate_bwd(dc, f, b_f, *, name, ts=256):
    B, S, _ = f.shape
    P = dc.shape[1]
    ts = min(ts, S)
    ns = S // ts

    def body(dc_ref, f_ref, b_ref, df_ref, db_ref, carry):
        @pl.when(pl.program_id(1) == 0)
        def _():
            carry[...] = jnp.zeros_like(carry)

        @pl.when((pl.program_id(0) == 0) & (pl.program_id(1) == 0))
        def _():
            db_ref[...] = jnp.zeros_like(db_ref)

        dc_t = dc_ref[0, 0]
        for j in range(1, P):
            dc_t = dc_t + dc_ref[0, j]
        dlogf = _dot_hi(_tri(ts, False), dc_t, NN) + carry[pl.ds(0, 1), :]
        carry[pl.ds(0, 1), :] = dlogf[0:1, :]
        df = dlogf * _sigmoid(-(f_ref[0] + b_ref[...]))
        df_ref[0] = df.astype(BF16)
        db_ref[...] += jnp.sum(df, axis=0, keepdims=True)

    return _call(
        body, name=name, grid=(B, ns),
        in_specs=[pl.BlockSpec((1, P, ts, LANES), lambda b, s: (b, 0, ns - 1 - s, 0)),
                  pl.BlockSpec((1, ts, LANES), lambda b, s: (b, ns - 1 - s, 0)),
                  pl.BlockSpec((1, LANES), lambda b, s: (0, 0))],
        out_specs=[pl.BlockSpec((1, ts, LANES), lambda b, s: (b, ns - 1 - s, 0)),
                   pl.BlockSpec((1, LANES), lambda b, s: (0, 0))],
        out_shape=[jax.ShapeDtypeStruct((B, S, LANES), BF16), jax.ShapeDtypeStruct((1, LANES), F32)],
        scratch_shapes=[pltpu.VMEM((8, LANES), F32)],
        compiler_params=_params(("arbitrary", "arbitrary")),
    )(dc, f, b_f)


def _lane_pick(tile, idx):
    lane = lax.broadcasted_iota(jnp.int32, tile.shape, 1)
    return jnp.sum(jnp.where(lane == idx, tile, 0.0), axis=-1, keepdims=True)


FOX_T = 512


def _fox_heads(q, cc_ref, p):
    lane = lax.broadcasted_iota(jnp.int32, q.shape, 1)
    qs = q * (1.0 / math.sqrt(FOX_HEAD_DIM))
    qhs = [jnp.where((lane < FOX_HEAD_DIM) == (hh == 0), qs, jnp.zeros_like(qs)) for hh in range(2)]
    crefs = [_lane_pick(cc_ref[0, pl.ds(0, 1), :], 2 * p + hh) for hh in range(2)]
    return qhs, crefs


def _fold_lanes(x, op):
    out = x[:, :LANES]
    for j in range(1, x.shape[1] // LANES):
        out = op(out, x[:, j * LANES:(j + 1) * LANES])
    return out


def _causal(t, transposed):
    r = lax.broadcasted_iota(jnp.int32, (t, t), 0)
    c = lax.broadcasted_iota(jnp.int32, (t, t), 1)
    return (r <= c) if transposed else (c <= r)


QKV0 = 8


def fox_fwd(z, c_col, c_row, *, name, rider=None):
    B, S, _ = z.shape
    assert S % FOX_T == 0
    tq, nq = FOX_T, S // FOX_T
    npair = FOX_HEADS // 2

    def body(q_ref, k_ref, v_ref, cc_ref, cr_ref, o_ref, l_ref, ot_ref, s_scr, m_scr, acc_scr):
        p, qi = pl.program_id(1), pl.program_id(2)
        qhs, crefs = _fox_heads(q_ref[0], cc_ref, p)
        lane = lax.broadcasted_iota(jnp.int32, (tq, LANES), 1)
        first = lane < FOX_HEAD_DIM
        for hh in range(2):
            m_scr[hh] = jnp.full((tq, LANES), NEG, F32)
            acc_scr[hh] = jnp.zeros((tq, LANES), F32)

        def logits(kb, diagonal):
            k0 = pl.multiple_of(kb * tq, tq)
            k = k_ref[0, pl.ds(k0, tq), :]
            for hh in range(2):
                s = lax.dot_general(qhs[hh], k, NT, preferred_element_type=F32)
                s = s + (crefs[hh] - cr_ref[0, pl.ds(2 * p + hh, 1), pl.ds(k0, tq)])
                if diagonal:
                    s = jnp.where(_causal(tq, False), s, NEG)
                s_scr[hh, kb] = s
                m_scr[hh] = jnp.maximum(m_scr[hh], _fold_lanes(s, jnp.maximum))

        def sweep1(kb, carry):
            logits(kb, False)
            return carry

        lax.fori_loop(0, qi, sweep1, 0)
        logits(qi, True)
        ms = [jnp.max(m_scr[hh], axis=-1, keepdims=True) for hh in range(2)]
        mbs = [jnp.broadcast_to(ms[hh], (tq, tq)) for hh in range(2)]

        for hh in range(2):
            m_scr[hh] = jnp.zeros((tq, LANES), F32)

        def weigh(kb, carry):
            k0 = pl.multiple_of(kb * tq, tq)
            v = v_ref[0, pl.ds(k0, tq), :]
            for hh in range(2):
                pr = jnp.exp(s_scr[hh, kb] - mbs[hh])
                m_scr[hh] += _fold_lanes(pr, jnp.add)
                acc_scr[hh] += jnp.dot(pr.astype(BF16), v, preferred_element_type=F32)
            return carry

        lax.fori_loop(0, qi + 1, weigh, 0)
        accs = [acc_scr[hh] for hh in range(2)]
        ls = [jnp.sum(m_scr[hh], axis=-1, keepdims=True) for hh in range(2)]
        out = jnp.where(first, accs[0] / ls[0], accs[1] / ls[1])
        o_ref[0] = out.astype(BF16)
        ot_ref[...] = out.T.astype(BF16)
        l_ref[0, 0] = jnp.where(first, ms[0] + jnp.log(ls[0]), ms[1] + jnp.log(ls[1]))

    return hosted_call(
        body, rider, name=name, grid=(B, npair, nq),
        in_specs=[pl.BlockSpec((1, tq, LANES), lambda b, p, i: (b, i, QKV0 + p)),
                  pl.BlockSpec((1, S, LANES), lambda b, p, i: (b, 0, QKV0 + npair + p)),
                  pl.BlockSpec((1, S, LANES), lambda b, p, i: (b, 0, QKV0 + 2 * npair + p)),
                  pl.BlockSpec((1, tq, LANES), lambda b, p, i: (b, i, 0)),
                  pl.BlockSpec((1, 8, S), lambda b, p, i: (b, 0, 0))],
        out_specs=[pl.BlockSpec((1, tq, LANES), lambda b, p, i: (b, i, p)),
                   pl.BlockSpec((1, 1, tq, LANES), lambda b, p, i: (b, p, i, 0)),
                   pl.BlockSpec((LANES, tq), lambda b, p, i: (p, b * nq + i))],
        out_shape=[jax.ShapeDtypeStruct((B, S, FOX_W), BF16),
                   jax.ShapeDtypeStruct((B, npair, S, LANES), F32),
                   jax.ShapeDtypeStruct((FOX_W, B * S), BF16)],
        scratch_shapes=[pltpu.VMEM((2, nq, tq, tq), F32), pltpu.VMEM((2, tq, LANES), F32),
                        pltpu.VMEM((2, tq, LANES), F32)],
        args=(z, z, z, c_col, c_row),
    )


def fox_bwd_dq(z, dcat, lse, c_col, c_row, *, name, rider=None):
    B, S, _ = z.shape
    tq, nq = FOX_T, S // FOX_T
    npair = FOX_HEADS // 2

    def body(q_ref, k_ref, v_ref, do_ref, l_ref, cc_ref, cr_ref, dq_ref, st_ref, p_scr, dp_scr, dl_scr):
        p, qi = pl.program_id(1), pl.program_id(2)
        qhs, crefs = _fox_heads(q_ref[0], cc_ref, p)
        lane = lax.broadcasted_iota(jnp.int32, (tq, LANES), 1)
        do_b = do_ref[0].astype(BF16)
        dohs = [jnp.where((lane < FOX_HEAD_DIM) == (hh == 0), do_b, jnp.zeros_like(do_b)) for hh in range(2)]
        lses = [_lane_pick(l_ref[0, 0], hh * FOX_HEAD_DIM) for hh in range(2)]
        lbs = [jnp.broadcast_to(lses[hh], (tq, tq)) for hh in range(2)]
        for hh in range(2):
            dl_scr[hh] = jnp.zeros((tq, LANES), F32)

        def probs(kb, diagonal):
            k0 = pl.multiple_of(kb * tq, tq)
            k = k_ref[0, pl.ds(k0, tq), :]
            v = v_ref[0, pl.ds(k0, tq), :]
            for hh in range(2):
                s = lax.dot_general(qhs[hh], k, NT, preferred_element_type=F32)
                s = s + (crefs[hh] - cr_ref[0, pl.ds(2 * p + hh, 1), pl.ds(k0, tq)])
                pr = jnp.exp(s - lbs[hh])
                if diagonal:
                    pr = jnp.where(_causal(tq, False), pr, 0.0)
                dp = lax.dot_general(dohs[hh], v, NT, preferred_element_type=F32)
                pdp = pr * dp
                dl_scr[hh] += _fold_lanes(pdp, jnp.add)
                p_scr[hh, kb] = pr
                dp_scr[hh, kb] = dp

        def first_pass(kb, carry):
            probs(kb, False)
            return carry

        lax.fori_loop(0, qi, first_pass, 0)
        probs(qi, True)

        dls = [jnp.sum(dl_scr[hh], axis=-1, keepdims=True) for hh in range(2)]
        dlbs = [jnp.broadcast_to(dls[hh], (tq, tq)) for hh in range(2)]

        def second_pass(kb, dq):
            k0 = pl.multiple_of(kb * tq, tq)
            k = k_ref[0, pl.ds(k0, tq), :]
            for hh in range(2):
                ds = p_scr[hh, kb] * (dp_scr[hh, kb] - dlbs[hh])
                kh = jnp.where((lane < FOX_HEAD_DIM) == (hh == 0), k, jnp.zeros_like(k))
                dq = dq + jnp.dot(ds.astype(BF16), kh, preferred_element_type=F32)
            return dq

        dq = lax.fori_loop(0, qi + 1, second_pass, jnp.zeros((tq, LANES), F32))
        dq_ref[0] = (dq * (1.0 / math.sqrt(FOX_HEAD_DIM))).astype(BF16)
        cols = jnp.zeros((tq, LANES), F32)
        for j, col in enumerate([crefs[0] - lses[0], crefs[1] - lses[1], dls[0], dls[1]]):
            cols = jnp.where(lane == j, col, cols)
        st_ref[0, 0] = _dot_hi(_eye(LANES), cols, NT)[:8]

    return hosted_call(
        body, rider, name=name, grid=(B, npair, nq),
        in_specs=[pl.BlockSpec((1, tq, LANES), lambda b, p, i: (b, i, QKV0 + p)),
                  pl.BlockSpec((1, S, LANES), lambda b, p, i: (b, 0, QKV0 + npair + p)),
                  pl.BlockSpec((1, S, LANES), lambda b, p, i: (b, 0, QKV0 + 2 * npair + p)),
                  pl.BlockSpec((1, tq, LANES), lambda b, p, i: (b, i, npair + p)),
                  pl.BlockSpec((1, 1, tq, LANES), lambda b, p, i: (b, p, i, 0)),
                  pl.BlockSpec((1, tq, LANES), lambda b, p, i: (b, i, 0)),
                  pl.BlockSpec((1, 8, S), lambda b, p, i: (b, 0, 0))],
        out_specs=[pl.BlockSpec((1, tq, LANES), lambda b, p, i: (b, i, p)),
                   pl.BlockSpec((1, 1, 8, tq), lambda b, p, i: (b, p, 0, i))],
        out_shape=[jax.ShapeDtypeStruct((B, S, FOX_W), BF16), jax.ShapeDtypeStruct((B, npair, 8, S), F32)],
        scratch_shapes=[pltpu.VMEM((2, nq, tq, tq), F32), pltpu.VMEM((2, nq, tq, tq), F32),
                        pltpu.VMEM((2, tq, LANES), F32)],
        args=(z, z, z, dcat, lse, c_col, c_row), vmem=56 << 20,
    )


def fox_bwd_dkdv(z, dcat, stats, c_col, *, name, rider=None):
    B, S, _ = z.shape
    tk, nq = FOX_T, S // FOX_T
    npair = FOX_HEADS // 2
    inv = 1.0 / math.sqrt(FOX_HEAD_DIM)

    def body(q_ref, k_ref, v_ref, do_ref, st_ref, cc_ref, dk_ref, dv_ref, dc_ref, dk_scr, dv_scr, dc_scr):
        p, kt = pl.program_id(1), pl.program_id(2)
        lane = lax.broadcasted_iota(jnp.int32, (tk, LANES), 1)
        masks = [(lane < FOX_HEAD_DIM) == (hh == 0) for hh in range(2)]
        k = k_ref[0]
        v = v_ref[0]
        khs = [jnp.where(masks[hh], k, jnp.zeros_like(k)) for hh in range(2)]
        vhs = [jnp.where(masks[hh], v, jnp.zeros_like(v)) for hh in range(2)]
        ccbs = [jnp.broadcast_to(_lane_pick(cc_ref[0], 2 * p + hh), (tk, tk)) for hh in range(2)]
        dk_scr[...] = jnp.zeros_like(dk_scr)
        dv_scr[...] = jnp.zeros_like(dv_scr)
        dc_scr[...] = jnp.zeros_like(dc_scr)

        def tile(qb, diagonal):
            q0 = pl.multiple_of(qb * tk, tk)
            qs = q_ref[0, pl.ds(q0, tk), :] * inv
            do_b = do_ref[0, pl.ds(q0, tk), :].astype(BF16)
            for hh in range(2):
                st = lax.dot_general(khs[hh], qs, NT, preferred_element_type=F32)
                pr = jnp.exp(st - ccbs[hh] + st_ref[0, 0, pl.ds(hh, 1), pl.ds(q0, tk)])
                if diagonal:
                    pr = jnp.where(_causal(tk, True), pr, 0.0)
                dp = lax.dot_general(vhs[hh], do_b, NT, preferred_element_type=F32)
                ds = pr * (dp - st_ref[0, 0, pl.ds(2 + hh, 1), pl.ds(q0, tk)])
                dv_scr[...] += jnp.dot(pr.astype(BF16), jnp.where(masks[hh], do_b, jnp.zeros_like(do_b)),
                                       preferred_element_type=F32)
                dk_scr[...] += jnp.dot(ds.astype(BF16), jnp.where(masks[hh], qs, jnp.zeros_like(qs)),
                                       preferred_element_type=F32)
                dc_scr[hh] -= _fold_lanes(ds, jnp.add)

        def later(qb, carry):
            tile(qb, False)
            return carry

        tile(kt, True)
        lax.fori_loop(kt + 1, nq, later, 0)
        dk_ref[0] = dk_scr[...].astype(BF16)
        dv_ref[0] = dv_scr[...].astype(BF16)
        dcs = [jnp.sum(dc_scr[hh], axis=-1, keepdims=True) for hh in range(2)]
        dc_ref[0, 0] = jnp.where(lane == 2 * p, dcs[0], jnp.where(lane == 2 * p + 1, dcs[1], 0.0))

    full = lambda col: pl.BlockSpec((1, S, LANES), col)
    tile_spec = lambda col: pl.BlockSpec((1, tk, LANES), col)
    return hosted_call(
        body, rider, name=name, grid=(B, npair, nq),
        in_specs=[full(lambda b, p, t: (b, 0, QKV0 + p)),
                  tile_spec(lambda b, p, t: (b, t, QKV0 + npair + p)),
                  tile_spec(lambda b, p, t: (b, t, QKV0 + 2 * npair + p)),
                  full(lambda b, p, t: (b, 0, npair + p)),
                  pl.BlockSpec((1, 1, 8, S), lambda b, p, t: (b, p, 0, 0)),
                  tile_spec(lambda b, p, t: (b, t, 0))],
        out_specs=[tile_spec(lambda b, p, t: (b, t, p)), tile_spec(lambda b, p, t: (b, t, p)),
                   pl.BlockSpec((1, 1, tk, LANES), lambda b, p, t: (b, p, t, 0))],
        out_shape=[jax.ShapeDtypeStruct((B, S, FOX_W), BF16)] * 2
        + [jax.ShapeDtypeStruct((B, npair, S, LANES), F32)],
        scratch_shapes=[pltpu.VMEM((tk, LANES), F32), pltpu.VMEM((tk, LANES), F32),
                        pltpu.VMEM((2, tk, LANES), F32)],
        args=(z, z, z, dcat, stats, c_col),
    )


def xattn_fwd(qm, kv, *, name, tq=512):
    B, S, D = qm.shape
    M = kv.shape[1]
    tq = min(tq, S)
    inv = 1.0 / math.sqrt(MEM_HEAD_DIM)

    nq = S // tq

    def body(q_ref, kv_ref, o_ref, ot_ref):
        for h in range(MEM_HEADS):
            c0 = h * MEM_HEAD_DIM
            qh = q_ref[0, :, c0:c0 + MEM_HEAD_DIM]
            kh = kv_ref[0, :, c0:c0 + MEM_HEAD_DIM]
            vh = kv_ref[0, :, D + c0:D + c0 + MEM_HEAD_DIM]
            s = lax.dot_general(qh, kh, NT, preferred_element_type=F32) * inv
            e = jnp.exp(s - jnp.max(s, axis=-1, keepdims=True))
            o = jnp.dot(e.astype(BF16), vh, preferred_element_type=F32) / jnp.sum(e, axis=-1, keepdims=True)
            o_ref[0, :, c0:c0 + MEM_HEAD_DIM] = o.astype(BF16)
            ot_ref[c0:c0 + MEM_HEAD_DIM, :] = o.T.astype(BF16)

    return _call(
        body, name=name, grid=(B, nq),
        in_specs=[pl.BlockSpec((1, tq, D), lambda b, i: (b, i, 0)),
                  pl.BlockSpec((1, M, 2 * D), lambda b, i: (b, 0, 0))],
        out_specs=[pl.BlockSpec((1, tq, D), lambda b, i: (b, i, 0)),
                   pl.BlockSpec((D, tq), lambda b, i: (0, b * nq + i))],
        out_shape=[jax.ShapeDtypeStruct((B, S, D), BF16), jax.ShapeDtypeStruct((D, B * S), BF16)],
        compiler_params=_params(("parallel", "parallel")),
    )(qm, kv)


def xattn_bwd(qm, kv, do, *, name, tq=512):
    B, S, D = qm.shape
    M = kv.shape[1]
    tq = min(tq, S)
    inv = 1.0 / math.sqrt(MEM_HEAD_DIM)

    def body(q_ref, kv_ref, do_ref, dq_ref, dkv_ref):
        @pl.when(pl.program_id(1) == 0)
        def _():
            dkv_ref[...] = jnp.zeros_like(dkv_ref)

        for h in range(MEM_HEADS):
            c0 = h * MEM_HEAD_DIM
            qh = q_ref[0, :, c0:c0 + MEM_HEAD_DIM]
            kh = kv_ref[0, :, c0:c0 + MEM_HEAD_DIM]
            vh = kv_ref[0, :, D + c0:D + c0 + MEM_HEAD_DIM]
            doh = do_ref[0, :, c0:c0 + MEM_HEAD_DIM]
            s = lax.dot_general(qh, kh, NT, preferred_element_type=F32) * inv
            e = jnp.exp(s - jnp.max(s, axis=-1, keepdims=True))
            pr = e / jnp.sum(e, axis=-1, keepdims=True)
            dp = lax.dot_general(doh, vh, NT, preferred_element_type=F32)
            ds = pr * (dp - jnp.sum(pr * dp, axis=-1, keepdims=True))
            ds_b = ds.astype(BF16)
            dq_ref[0, :, c0:c0 + MEM_HEAD_DIM] = (jnp.dot(ds_b, kh, preferred_element_type=F32) * inv).astype(BF16)
            dkv_ref[0, :, c0:c0 + MEM_HEAD_DIM] += lax.dot_general(ds_b, qh, TN, preferred_element_type=F32) * inv
            dkv_ref[0, :, D + c0:D + c0 + MEM_HEAD_DIM] += lax.dot_general(
                pr.astype(BF16), doh, TN, preferred_element_type=F32)

    row = pl.BlockSpec((1, tq, D), lambda b, i: (b, i, 0))
    kvs = pl.BlockSpec((1, M, 2 * D), lambda b, i: (b, 0, 0))
    return _call(
        body, name=name, grid=(B, S // tq), in_specs=[row, kvs, row], out_specs=[row, kvs],
        out_shape=[jax.ShapeDtypeStruct((B, S, D), BF16), jax.ShapeDtypeStruct((B, M, 2 * D), F32)],
        compiler_params=_params(("parallel", "arbitrary")),
    )(qm, kv, do)


SWIGLU_TN = 1408


def _chunks(n, w=256):
    return [(c0, min(w, n - c0)) for c0 in range(0, n, w)]


def mm_swiglu_fwd(hf, w_gu, *, name, tm=512):
    T, D = hf.shape
    Fh = w_gu.shape[1] // 2
    tm, tn = min(tm, T), SWIGLU_TN
    nj = Fh // tn
    assert Fh % tn == 0 and T % tm == 0

    def body(a_ref, bg_ref, bu_ref, g_ref, u_ref, o_ref, ot_ref):
        a = a_ref[...]
        for c0, cw in _chunks(tn):
            cols = pl.ds(c0, cw)
            g = jnp.dot(a, bg_ref[:, cols], preferred_element_type=F32)
            u = jnp.dot(a, bu_ref[:, cols], preferred_element_type=F32)
            act = g * _sigmoid(g) * u
            g_ref[:, cols] = g.astype(BF16)
            u_ref[:, cols] = u.astype(BF16)
            o_ref[:, cols] = act.astype(BF16)
            ot_ref[cols, :] = act.T.astype(BF16)

    tile = pl.BlockSpec((tm, tn), lambda i, j: (i, j))
    return _call(
        body, name=name, grid=(T // tm, nj),
        in_specs=[pl.BlockSpec((tm, D), lambda i, j: (i, 0)), pl.BlockSpec((D, tn), lambda i, j: (0, j)),
                  pl.BlockSpec((D, tn), lambda i, j: (0, nj + j))],
        out_specs=[tile, tile, tile, pl.BlockSpec((tn, tm), lambda i, j: (j, i))],
        out_shape=[jax.ShapeDtypeStruct((T, Fh), BF16)] * 3 + [jax.ShapeDtypeStruct((Fh, T), BF16)],
        compiler_params=_params(("parallel", "parallel"), 48 << 20),
    )(hf, w_gu, w_gu)


def mm_swiglu_bwd(dx, w_down, g, u, *, name, tm=512):
    T, D = dx.shape
    Fh = w_down.shape[0]
    tm, tn = min(tm, T), SWIGLU_TN
    assert Fh % tn == 0 and T % tm == 0

    def body(a_ref, b_ref, g_ref, u_ref, dg_ref, du_ref):
        a = a_ref[...].astype(BF16)
        for c0, cw in _chunks(tn):
            cols = pl.ds(c0, cw)
            d = lax.dot_general(a, b_ref[cols, :], NT, preferred_element_type=F32)
            gv = g_ref[:, cols].astype(F32)
            uv = u_ref[:, cols].astype(F32)
            sg = _sigmoid(gv)
            dg_ref[:, cols] = (d * uv * (sg * (1.0 + gv * (1.0 - sg)))).astype(BF16)
            du_ref[:, cols] = (d * gv * sg).astype(BF16)

    tile = pl.BlockSpec((tm, tn), lambda i, j: (i, j))
    return _call(
        body, name=name, grid=(T // tm, Fh // tn),
        in_specs=[pl.BlockSpec((tm, D), lambda i, j: (i, 0)), pl.BlockSpec((tn, D), lambda i, j: (j, 0)), tile, tile],
        out_specs=[tile, tile],
        out_shape=[jax.ShapeDtypeStruct((T, Fh), BF16)] * 2,
        compiler_params=_params(("parallel", "parallel"), 48 << 20),
    )(dx, w_down, g, u)


LATE_MID = ("w_out", "w_mq", "w_mo")
LATE_KV = ("w_mkv",)
LATE_FFN = ("w_gu", "w_down")
LATE = LATE_MID + LATE_KV + LATE_FFN
RS_GROUPS = (("w_gu", "w_down"), ("w_out", "w_mq", "w_mkv", "w_mo"), ("w_in",))


def pair_sums(names, g42, got):
    return {n: pair_sum(g, o, name="rs_pair_sum_" + n) for n, g, o in zip(names, g42, got)}


def local_step(x, mem, target, sp, first_shards, late_shards):
    B, S, D = x.shape
    T = B * S
    M = mem.shape[1]
    row = lambda v: v.reshape(1, -1).astype(F32)
    g_mix, g_x, g_mem, g_ffn, g_final = (row(sp[k]) for k in ("g_mix", "g_x", "g_mem", "g_ffn", "g_final"))
    conv_b, ln_g, ln_b = row(sp["conv_b"]), row(sp["ln_g"]), row(sp["ln_b"])
    b_f = jnp.pad(row(sp["b_f"]), ((0, 0), (0, LANES - FOX_HEADS)))
    n_ug, n_main = 2 * CONV_CH, 2 * CONV_CH + 3 * FOX_W

    x2d = x.reshape(T, D)
    h, h_t, partly = rmsnorm_fwd(x2d, g_mix, name="rms_mix", rider=AllGatherStage1(first_shards))
    w_in8, cw8 = all_gather_stage2(partly, name="ag_first_stage2")
    w_in_full = _full_from_gathered("w_in", w_in8)
    conv_w = cw8.transpose(1, 0, 2).reshape(HALO, -1)
    w_main, w_ug, w_qkv = w_in_full[:, :n_main], w_in_full[:, :n_ug], w_in_full[:, n_ug:n_main]
    w_f = jnp.pad(w_in_full[:, n_main:], ((0, 0), (0, LANES - FOX_HEADS)))
    z = matmul(h, w_main, out_dtype=BF16, tn=n_main, name="mm_in")
    z3 = z.reshape(B, S, n_main)
    n_mid, n_kv = len(LATE_MID), len(LATE_MID) + len(LATE_KV)
    (conv_out, conv_t), partly_mid = conv_branch_fwd(z3, conv_w, conv_b, ln_g, ln_b, name="conv_fwd",
                                                     rider=AllGatherStage1(late_shards[:n_mid]))
    (f_raw, c_col, c_row), partly_kv = fgate_fwd(h.reshape(B, S, D), w_f, b_f, name="fgate_fwd",
                                                 rider=AllGatherStage1(late_shards[n_mid:n_kv]))
    (att, lse, att_t), partly_ffn = fox_fwd(z3, c_col, c_row, name="fox_fwd",
                                            rider=AllGatherStage1(late_shards[n_kv:]))
    gathered = all_gather_stage2(partly_mid + partly_kv + partly_ffn, name="ag_late_stage2")
    wf = {n: _full_from_gathered(n, blk) for n, blk in zip(LATE, gathered)}
    x1 = matmul([conv_out.reshape(T, CONV_CH), att.reshape(T, FOX_W)],
                [wf["w_out"], wf["w_out"]], b_blk=[0, 1], out_dtype=F32, res=x2d, tn=D, name="mm_out")
    hx, hx_t = rmsnorm_fwd(x1, g_x, name="rms_x")
    qm = matmul(hx, wf["w_mq"], out_dtype=BF16, tn=D, name="mm_mq")
    mem2d = mem.reshape(B * M, D)
    mem_n, mem_n_t = rmsnorm_fwd(mem2d, g_mem, name="rms_mem")
    kv = matmul(mem_n, wf["w_mkv"], out_dtype=BF16, tn=2 * D, name="mm_mkv").reshape(B, M, 2 * D)
    o, o_t = xattn_fwd(qm.reshape(B, S, D), kv, name="xattn_fwd")
    o = o.reshape(T, D)
    x2 = matmul(o, wf["w_mo"], out_dtype=F32, res=x1, tn=D, name="mm_mo")
    hf, hf_t = rmsnorm_fwd(x2, g_ffn, name="rms_ffn")
    gate, up, act, act_t = mm_swiglu_fwd(hf, wf["w_gu"], name="mm_gu")
    x3 = matmul(act, wf["w_down"], out_dtype=F32, res=x2, tn=D, name="mm_down")
    dx3, dg_final, loss = final_loss_bwd(x3, g_final, target.reshape(T, D), name="loss_bwd")
    gw = {}
    gw["w_down"] = matmul(act_t, dx3, out_dtype=BF16, tm=1408, tn=256, name="dw_down")
    dgate, dup = mm_swiglu_bwd(dx3, wf["w_down"], gate, up, name="dx_down")
    gw["w_gu"] = [matmul(hf_t, dgate, out_dtype=BF16, tn=SWIGLU_TN, name="dw_gate"),
                  matmul(hf_t, dup, out_dtype=BF16, tn=SWIGLU_TN, name="dw_up")]
    g42 = [_shards_from_full(n, gw[n]) for n in RS_GROUPS[0]]
    dhf, got = matmul([dgate, dup], [wf["w_gu"], wf["w_gu"]], b_blk=[0, 1], tb=True, out_dtype=BF16,
                      tm=256, tn=D, name="dx_gu", rider=SiblingExchange(g42))
    parts = pair_sums(RS_GROUPS[0], g42, got)
    dx2, dg_ffn = rmsnorm_bwd(x2, g_ffn, dhf, dx3, name="rms_ffn_bwd")
    gw["w_mo"] = matmul(o_t, dx2, out_dtype=BF16, name="dw_mo")
    do = matmul(dx2, wf["w_mo"], tb=True, out_dtype=BF16, tn=D, name="dx_mo")
    dqm, dkv = xattn_bwd(qm.reshape(B, S, D), kv, do.reshape(B, S, D), name="xattn_bwd")
    dqm = dqm.reshape(T, D)
    dkv = dkv.reshape(B * M, 2 * D)
    gw["w_mq"] = matmul(hx_t, dqm, out_dtype=BF16, tn=D, name="dw_mq")
    dhx = matmul(dqm, wf["w_mq"], tb=True, out_dtype=BF16, tn=D, name="dx_mq")
    gw["w_mkv"] = matmul(mem_n_t, dkv, out_dtype=BF16, tn=D, name="dw_mkv")
    dmem_n = matmul(dkv, wf["w_mkv"], tb=True, out_dtype=BF16, tn=D, name="dx_mkv")
    _, dg_mem = rmsnorm_bwd(mem2d, g_mem, dmem_n, None, name="rms_mem_bwd")
    dx1, dg_x = rmsnorm_bwd(x1, g_x, dhx, dx2, name="rms_x_bwd")
    gw["w_out"] = jnp.concatenate([matmul(conv_t, dx1, out_dtype=BF16, name="dw_out_conv"),
                                   matmul(att_t, dx1, out_dtype=BF16, name="dw_out_att")], axis=0)
    g42 = [_shards_from_full(n, gw[n]) for n in RS_GROUPS[1]]
    dcat, got = matmul(dx1, wf["w_out"], tb=True, out_dtype=BF16, tn=D, name="dx_out", rider=SiblingExchange(g42))
    dcat = dcat.reshape(B, S, D)
    parts.update(pair_sums(RS_GROUPS[1], g42, got))
    dy, dconv_w, dvec = conv_branch_bwd_a(z3, dcat, conv_w, conv_b, ln_g, ln_b, name="conv_bwd_a")
    dug = conv_branch_bwd_b(z3, dy, conv_w, name="conv_bwd_b")
    gots = {}
    (dq, stats), got = fox_bwd_dq(z3, dcat, lse, c_col, c_row, name="fox_bwd_dq",
                                  rider=ChipExchange([parts[n] for n in RS_GROUPS[0]]))
    gots.update(zip(RS_GROUPS[0], got))
    (dk, dv, dc), got = fox_bwd_dkdv(z3, dcat, stats, c_col, name="fox_bwd_dkdv",
                                     rider=ChipExchange([parts[n] for n in RS_GROUPS[1]]))
    gots.update(zip(RS_GROUPS[1], got))
    df, db_f = fgate_bwd(dc, f_raw, b_f, name="fgate_bwd")
    dug2 = dug.reshape(T, n_ug)
    dqkv = jnp.concatenate([dq, dk, dv], axis=-1).reshape(T, 3 * FOX_W)
    df2 = df.reshape(T, LANES)
    dw_in = [matmul(h_t, dug2, out_dtype=BF16, tn=n_ug, name="dw_in_ug"),
             matmul(h_t, dqkv, out_dtype=BF16, tn=3 * FOX_W, name="dw_in_qkv"),
             matmul(h_t, df2, out_dtype=BF16, name="dw_f")[:, :FOX_HEADS]]
    g42 = [_shards_from_full("w_in", dw_in)]
    parts.update(pair_sums(RS_GROUPS[2], g42, run_rider(SiblingExchange(g42), name="rs_sibling_in")))
    dh, (gots["w_in"],) = matmul([dug2, dqkv, df2], [w_ug, w_qkv, w_f], tb=True, out_dtype=F32, tn=D,
                                 name="dx_in", rider=ChipExchange([parts["w_in"]]))
    dx, dg_mix = rmsnorm_bwd(x2d, g_mix, dh, dx1, name="rms_mix_bwd")
    gs = dict(g_mix=dg_mix, b_f=db_f[:, :FOX_HEADS], conv_w=dconv_w[:CONV_K], conv_b=dvec[0:1],
              ln_g=dvec[1:2], ln_b=dvec[2:3], g_x=dg_x, g_mem=dg_mem, g_ffn=dg_ffn, g_final=dg_final)
    return loss, dx.reshape(B, S, D), gs, {n: (parts[n], gots[n]) for n in BIG}


def _me():
    return lax.axis_index("x"), lax.axis_index("y"), lax.axis_index("c")


def _any_specs(n):
    return [pl.BlockSpec(memory_space=pl.ANY)] * n


def all_gather(xs, *, name):
    n = len(xs)

    def body(*refs):
        x_refs, out_refs = refs[:n], refs[n:2 * n]
        send_sems, recv_sems, local_sems = refs[2 * n:]
        x, y, c = _me()
        me, sibling = (x, y, c), (x, y, 1 - c)
        chips = [(1 - x, y), (x, 1 - y), (1 - x, 1 - y)]

        def slot(a, px, py, pc):
            return out_refs[a].at[4 * px + 2 * py + pc]

        def copy(a, k, block, to, own=False):
            return pltpu.make_async_remote_copy(
                src_ref=x_refs[a] if own else slot(a, *block), dst_ref=slot(a, *block),
                send_sem=send_sems.at[k, a], recv_sem=recv_sems.at[k, a], device_id=to, device_id_type=MESH)

        mine = [pltpu.make_async_copy(x_refs[a], slot(a, *me), local_sems.at[a]) for a in range(n)]
        first = [copy(a, 0, me, sibling, own=True) for a in range(n)]
        first += [copy(a, 1 + j, me, (*chip, c), own=True) for j, chip in enumerate(chips) for a in range(n)]
        for cp in mine + first:
            cp.start()
        passed = []
        for j, chip in enumerate(chips):
            for a in range(n):
                copy(a, 1 + j, (*chip, c), me).wait_recv()
                passed.append(copy(a, 4 + j, (*chip, c), sibling))
                passed[-1].start()
        for a in range(n):
            copy(a, 0, sibling, me).wait_recv()
            for j, chip in enumerate(chips):
                copy(a, 4 + j, (*chip, 1 - c), me).wait_recv()
        for cp in first + passed:
            cp.wait_send()
        for cp in mine:
            cp.wait()

    return _call(
        body, name=name, in_specs=_any_specs(n), out_specs=_any_specs(n),
        out_shape=[jax.ShapeDtypeStruct((N_DEV,) + v.shape, v.dtype) for v in xs],
        scratch_shapes=[pltpu.SemaphoreType.DMA((7, n)), pltpu.SemaphoreType.DMA((7, n)),
                        pltpu.SemaphoreType.DMA((n,))],
    )(*xs)


SIBLING_BARRIER = 1


class SiblingExchange:
    collective_id = SIBLING_BARRIER

    def __init__(self, gs):
        n = len(gs)
        self.n, self.inputs = n, list(gs)
        self.out_shape = [jax.ShapeDtypeStruct((4,) + g.shape[2:], g.dtype) for g in gs]
        self.scratch = [pltpu.SemaphoreType.DMA((n,)), pltpu.SemaphoreType.DMA((n,))]

    @staticmethod
    def barrier_peers():
        x, y, c = _me()
        return [(x, y, 1 - c)]

    def _copies(self, g_refs, out_refs, sems):
        send_sems, recv_sems = sems
        x, y, c = _me()
        return [pltpu.make_async_remote_copy(
            src_ref=g_refs[a].at[:, 1 - c], dst_ref=out_refs[a], send_sem=send_sems.at[a],
            recv_sem=recv_sems.at[a], device_id=(x, y, 1 - c), device_id_type=MESH) for a in range(self.n)]

    def start(self, in_refs, out_refs, sems):
        for cp in self._copies(in_refs, out_refs, sems):
            cp.start()

    def finish(self, in_refs, out_refs, sems):
        for cp in self._copies(in_refs, out_refs, sems):
            cp.wait()


def run_rider(rider, *, name):
    return hosted_call(None, rider, name=name, grid=(), in_specs=[], out_specs=[], out_shape=[],
                       scratch_shapes=[], args=[])[1]


class ChipExchange:
    def __init__(self, ps):
        n = len(ps)
        self.n, self.inputs = n, list(ps)
        self.out_shape = [jax.ShapeDtypeStruct(p.shape, p.dtype) for p in ps]
        self.scratch = [pltpu.SemaphoreType.DMA((3, n)), pltpu.SemaphoreType.DMA((3, n))]

    def _copies(self, p_refs, out_refs, sems, outgoing):
        send_sems, recv_sems = sems
        x, y, c = _me()
        my_chip = 2 * x + y
        cps = []
        for k in range(3):
            px, py = x ^ ((k + 1) >> 1), y ^ ((k + 1) & 1)
            src, dst = (2 * px + py, my_chip) if outgoing else (my_chip, 2 * px + py)
            for a in range(self.n):
                cps.append(pltpu.make_async_remote_copy(
                    src_ref=p_refs[a].at[src], dst_ref=out_refs[a].at[dst], send_sem=send_sems.at[k, a],
                    recv_sem=recv_sems.at[k, a], device_id=(px, py, c), device_id_type=MESH))
        return cps

    def start(self, in_refs, out_refs, sems):
        for cp in self._copies(in_refs, out_refs, sems, True):
            cp.start()

    def finish(self, in_refs, out_refs, sems):
        for cp in self._copies(in_refs, out_refs, sems, False):
            cp.wait_recv()
        for cp in self._copies(in_refs, out_refs, sems, True):
            cp.wait_send()


class AllGatherStage1:
    def __init__(self, xs):
        n = len(xs)
        self.n, self.inputs = n, list(xs)
        self.out_shape = [jax.ShapeDtypeStruct((N_DEV,) + v.shape, v.dtype) for v in xs]
        self.scratch = [pltpu.SemaphoreType.DMA((4, n)), pltpu.SemaphoreType.DMA((4, n)),
                        pltpu.SemaphoreType.DMA((n,))]

    def _copies(self, x_refs, out_refs, sems, kind):
        send_sems, recv_sems, local_sems = sems
        x, y, c = _me()
        slot = lambda a, d: out_refs[a].at[4 * d[0] + 2 * d[1] + d[2]]
        if kind == "local":
            return [pltpu.make_async_copy(x_refs[a], slot(a, (x, y, c)), local_sems.at[a]) for a in range(self.n)]
        cps = []
        for k, peer in enumerate([(x, y, 1 - c), (1 - x, y, c), (x, 1 - y, c), (1 - x, 1 - y, c)]):
            for a in range(self.n):
                cps.append(pltpu.make_async_remote_copy(
                    src_ref=x_refs[a], dst_ref=slot(a, (x, y, c) if kind == "out" else peer),
                    send_sem=send_sems.at[k, a], recv_sem=recv_sems.at[k, a], device_id=peer, device_id_type=MESH))
        return cps

    def start(self, in_refs, out_refs, sems):
        for cp in self._copies(in_refs, out_refs, sems, "local") + self._copies(in_refs, out_refs, sems, "out"):
            cp.start()

    def finish(self, in_refs, out_refs, sems):
        for cp in self._copies(in_refs, out_refs, sems, "in"):
            cp.wait_recv()
        for cp in self._copies(in_refs, out_refs, sems, "out"):
            cp.wait_send()
        for cp in self._copies(in_refs, out_refs, sems, "local"):
            cp.wait()


def all_gather_stage2(outs, *, name):
    n = len(outs)

    def body(*refs):
        out_refs = refs[n:2 * n]
        send_sems, recv_sems = refs[2 * n:]
        x, y, c = _me()
        _peer_barrier([(x, y, 1 - c)])
        sends, recvs = [], []
        for k, (px, py) in enumerate([(1 - x, y), (x, 1 - y), (1 - x, 1 - y)]):
            for a in range(n):
                mk = lambda pc: pltpu.make_async_remote_copy(
                    src_ref=out_refs[a].at[4 * px + 2 * py + c], dst_ref=out_refs[a].at[4 * px + 2 * py + pc],
                    send_sem=send_sems.at[k, a], recv_sem=recv_sems.at[k, a], device_id=(x, y, 1 - c),
                    device_id_type=MESH)
                sends.append(mk(c))
                recvs.append(mk(1 - c))
        for cp in sends:
            cp.start()
        for cp in recvs:
            cp.wait_recv()
        for cp in sends:
            cp.wait_send()

    return _call(
        body, name=name, in_specs=_any_specs(n), out_specs=_any_specs(n),
        out_shape=[jax.ShapeDtypeStruct(o.shape, o.dtype) for o in outs],
        input_output_aliases={a: a for a in range(n)},
        scratch_shapes=[pltpu.SemaphoreType.DMA((3, n)), pltpu.SemaphoreType.DMA((3, n))],
        compiler_params=_params(collective_id=SIBLING_BARRIER),
    )(*outs)


def _peer_barrier(peers):
    barrier = pltpu.get_barrier_semaphore()
    for peer in peers:
        pl.semaphore_signal(barrier, inc=1, device_id=peer, device_id_type=MESH)
    pl.semaphore_wait(barrier, len(peers))


def hosted_call(body, rider, *, name, grid, in_specs, out_specs, out_shape, scratch_shapes, args, vmem=None):
    n_in, n_out, n_scr = len(in_specs), len(out_specs), len(scratch_shapes)
    r_in, r_out = (len(rider.inputs), len(rider.out_shape)) if rider is not None else (0, 0)
    own_barrier = getattr(rider, "collective_id", None) is not None

    def wrapped(*refs):
        ins, refs = refs[:n_in], refs[n_in:]
        rins, refs = refs[:r_in], refs[r_in:]
        outs, refs = refs[:n_out], refs[n_out:]
        routs, refs = refs[:r_out], refs[r_out:]
        scr, rscr = refs[:n_scr], refs[n_scr:]
        ids = [pl.program_id(d) for d in range(len(grid))]
        first = functools.reduce(jnp.logical_and, [i == 0 for i in ids], True)
        last = functools.reduce(jnp.logical_and, [i == g - 1 for i, g in zip(ids, grid)], True)

        def begin():
            if own_barrier:
                _peer_barrier(rider.barrier_peers())
            rider.start(rins, routs, rscr)

        if rider is not None and grid:
            pl.when(first)(begin)
        elif rider is not None:
            begin()
        if body is not None:
            body(*ins, *outs, *scr)
        if rider is not None and grid:
            pl.when(last)(lambda: rider.finish(rins, routs, rscr))
        elif rider is not None:
            rider.finish(rins, routs, rscr)

    kw = dict(grid=grid) if grid else {}
    if grid or vmem is not None or own_barrier:
        kw["compiler_params"] = _params(("arbitrary",) * len(grid) if grid else None, vmem,
                                        rider.collective_id if own_barrier else None)
    res = _call(
        wrapped, name=name, in_specs=list(in_specs) + _any_specs(r_in), out_specs=list(out_specs) + _any_specs(r_out),
        out_shape=list(out_shape) + (rider.out_shape if rider is not None else []),
        scratch_shapes=list(scratch_shapes) + (rider.scratch if rider is not None else []), **kw,
    )(*args, *(rider.inputs if rider is not None else []))
    return list(res[:n_out]), list(res[n_out:])


def _pick_rows(r, target=256):
    best = None
    for d in range(16, min(r, target) + 1, 16):
        if r % d == 0:
            best = d
    return r if best is None else best


def pair_sum(g, got, *, name):
    _, _, R, C = g.shape
    tr = _pick_rows(R)

    def body(g_ref, got_ref, o_ref):
        mine = jnp.where(lax.axis_index("c") == 0, g_ref[:, 0], g_ref[:, 1])
        o_ref[...] = (mine.astype(F32) + got_ref[...].astype(F32)).astype(o_ref.dtype)

    return _call(
        body, name=name, grid=(R // tr,),
        in_specs=[pl.BlockSpec((4, 2, tr, C), lambda i: (0, 0, i, 0)), pl.BlockSpec((4, tr, C), lambda i: (0, i, 0))],
        out_specs=pl.BlockSpec((4, tr, C), lambda i: (0, i, 0)),
        out_shape=jax.ShapeDtypeStruct((4, R, C), g.dtype),
        compiler_params=_params(("parallel",)),
    )(g, got)


def chip_sum_adamw(p, got, w, m, v, *, name):
    _, R, C = p.shape
    assert w.shape == (1, R, C), (name, w.shape, p.shape)
    tr = _pick_rows(R)

    def body(p_ref, got_ref, w_ref, m_ref, v_ref, g_ref, d_ref, mo_ref, vo_ref):
        my_chip = 2 * lax.axis_index("x") + lax.axis_index("y")
        g = jnp.zeros((tr, C), F32)
        for j in range(4):
            g = g + jnp.where(my_chip == j, p_ref[j], got_ref[j]).astype(F32)
        g_ref[0] = g
        d_ref[0], mo_ref[0], vo_ref[0] = _adamw_math(w_ref[0], g, m_ref[0], v_ref[0])

    part = pl.BlockSpec((4, tr, C), lambda i: (0, i, 0))
    spec = pl.BlockSpec((1, tr, C), lambda i: (0, i, 0))
    return _call(
        body, name=name, grid=(R // tr,), in_specs=[part, part, spec, spec, spec], out_specs=[spec] * 4,
        out_shape=[jax.ShapeDtypeStruct((1, R, C), F32)] * 4,
        compiler_params=_params(("parallel",)),
    )(p, got, w, m, v)


def rows_sum(g8, *, name):
    _, R, C = g8.shape

    def body(g_ref, o_ref):
        acc = g_ref[0]
        for j in range(1, N_DEV):
            acc = acc + g_ref[j]
        o_ref[...] = acc

    return _call(body, name=name, out_shape=jax.ShapeDtypeStruct((R, C), F32))(g8)


def _adamw_math(w, g, m, v):
    m = ADAM_B1 * m + (1.0 - ADAM_B1) * g
    v = ADAM_B2 * v + (1.0 - ADAM_B2) * (g * g)
    m_hat = m / (1.0 - ADAM_B1 ** ADAM_STEP)
    v_hat = v / (1.0 - ADAM_B2 ** ADAM_STEP)
    delta = -ADAM_LR * (m_hat / (jnp.sqrt(v_hat) + ADAM_EPS) + ADAM_WD * w)
    return delta, m, v


def adamw_small(wgmv, *, name):
    n = len(wgmv)

    def body(*refs):
        ins, outs = refs[:4 * n], refs[4 * n:]
        for a in range(n):
            w_ref, g_ref, m_ref, v_ref = ins[4 * a:4 * a + 4]
            d, mn, vn = _adamw_math(w_ref[...], g_ref[...], m_ref[...], v_ref[...])
            outs[3 * a][...] = d
            outs[3 * a + 1][...] = mn
            outs[3 * a + 2][...] = vn

    flat = [t for tup in wgmv for t in tup]
    res = _call(
        body, name=name,
        out_shape=[jax.ShapeDtypeStruct(tup[0].shape, F32) for tup in wgmv for _ in range(3)],
    )(*flat)
    return [tuple(res[3 * a:3 * a + 3]) for a in range(n)]


BIG = ("w_in", "w_out", "w_mq", "w_mkv", "w_mo", "w_gu", "w_down")
COL_SHARDED = ("w_in", "w_mkv", "w_gu")
SMALL = ("g_mix", "b_f", "conv_w", "conv_b", "ln_g", "ln_b", "g_x", "g_mem", "g_ffn", "g_final")


def _full_from_gathered(n, blk):
    _, rr, cc = blk.shape
    if n in COL_SHARDED:
        return jnp.concatenate([blk[k] for k in range(N_DEV)], axis=1)
    return blk.reshape(N_DEV * rr, cc)


def _shards_from_full(n, g):
    pieces = g if isinstance(g, list) else [g]
    rr, cc = pieces[0].shape[0], sum(p.shape[1] for p in pieces)
    if n in COL_SHARDED:
        w = cc // N_DEV
        return jnp.stack([_columns(pieces, k * w, w) for k in range(N_DEV)]).reshape(4, 2, rr, w)
    return pieces[0].reshape(4, 2, rr // N_DEV, cc)


def _columns(pieces, start, width):
    out, c0 = [], 0
    for p in pieces:
        lo, hi = max(start, c0), min(start + width, c0 + p.shape[1])
        if lo < hi:
            out.append(p[:, lo - c0:hi - c0])
        c0 += p.shape[1]
    return out[0] if len(out) == 1 else jnp.concatenate(out, axis=1)


def _small_layout():
    sizes = dict(g_mix=1024, b_f=8, conv_w=CONV_K * CONV_CH, conv_b=512, ln_g=512, ln_b=512, g_x=1024,
                 g_mem=1024, g_ffn=1024, g_final=1024, loss=1)
    lay, r0 = {}, 0
    for n, sz in sizes.items():
        r = -(-sz // LANES)
        lay[n] = (r0, r, sz)
        r0 += r
    return lay, -(-r0 // 8) * 8


def kernel(x, mem, g_mix, w_in, b_f, conv_w, conv_b, ln_g, ln_b, w_out, g_x, g_mem, w_mq, w_mkv, w_mo, g_ffn, w_gu, w_down, g_final, loss_target, m_g_mix, m_w_in, m_b_f, m_conv_w, m_conv_b, m_ln_g, m_ln_b, m_w_out, m_g_x, m_g_mem, m_w_mq, m_w_mkv, m_w_mo, m_g_ffn, m_w_gu, m_w_down, m_g_final, v_g_mix, v_w_in, v_b_f, v_conv_w, v_conv_b, v_ln_g, v_ln_b, v_w_out, v_g_x, v_g_mem, v_w_mq, v_w_mkv, v_w_mo, v_g_ffn, v_w_gu, v_w_down, v_g_final):
    names = ["g_mix", "w_in", "b_f", "conv_w", "conv_b", "ln_g", "ln_b", "w_out", "g_x", "g_mem", "w_mq",
             "w_mkv", "w_mo", "g_ffn", "w_gu", "w_down", "g_final"]
    W = dict(zip(names, [g_mix, w_in, b_f, conv_w, conv_b, ln_g, ln_b, w_out, g_x, g_mem, w_mq, w_mkv, w_mo,
                         g_ffn, w_gu, w_down, g_final]))
    Mo = dict(zip(names, [m_g_mix, m_w_in, m_b_f, m_conv_w, m_conv_b, m_ln_g, m_ln_b, m_w_out, m_g_x, m_g_mem,
                          m_w_mq, m_w_mkv, m_w_mo, m_g_ffn, m_w_gu, m_w_down, m_g_final]))
    Vo = dict(zip(names, [v_g_mix, v_w_in, v_b_f, v_conv_w, v_conv_b, v_ln_g, v_ln_b, v_w_out, v_g_x, v_g_mem,
                          v_w_mq, v_w_mkv, v_w_mo, v_g_ffn, v_w_gu, v_w_down, v_g_final]))
    dev = 4 * lax.axis_index("x") + 2 * lax.axis_index("y") + lax.axis_index("c")

    two = lambda a: a.reshape(-1, a.shape[-1])
    cw_shard = jnp.pad(two(conv_w), ((0, HALO - CONV_K), (0, 0)))
    sp = dict(g_mix=g_mix, b_f=b_f, conv_b=conv_b, ln_g=ln_g, ln_b=ln_b, g_x=g_x, g_mem=g_mem,
              g_ffn=g_ffn, g_final=g_final)
    loss_blk, grad_x, gs, reduced = local_step(x, mem, loss_target, sp, [two(w_in).astype(BF16), cw_shard],
                                               [two(W[n]).astype(BF16) for n in LATE])

    lay, rs = _small_layout()
    small = {**{n: gs[n] for n in SMALL}, "loss": loss_blk[:, :1]}
    parts = []
    for n, (r0, r, sz) in lay.items():
        flat = small[n].reshape(-1).astype(F32)
        parts.append(jnp.pad(flat, (0, r * LANES - sz)).reshape(r, LANES))
    spack = jnp.concatenate(parts, axis=0)
    spack = jnp.pad(spack, ((0, rs - spack.shape[0]), (0, 0)))
    ssum = rows_sum(all_gather([spack], name="ag_small")[0], name="small_sum")
    gsmall = {n: ssum[r0:r0 + r].reshape(-1)[:sz] for n, (r0, r, sz) in lay.items()}
    loss = gsmall["loss"].reshape(())

    grads, delta, new_m, new_v = {}, {}, {}, {}
    for n in BIG:
        p, o = reduced[n]
        grads[n], delta[n], new_m[n], new_v[n] = chip_sum_adamw(p, o, W[n], Mo[n], Vo[n], name="adamw_" + n)
    for n in SMALL:
        if n == "conv_w":
            full = gsmall[n].reshape(CONV_K, CONV_CH)
            ncol = conv_w.shape[-1]
            grads[n] = lax.dynamic_slice(full, (0, dev * ncol), (CONV_K, ncol)).reshape(conv_w.shape)
        else:
            grads[n] = gsmall[n].reshape(W[n].shape)
    upd = adamw_small([(two(W[n]), two(grads[n]), two(Mo[n]), two(Vo[n])) for n in SMALL], name="adamw_small")
    for n, (d, mn, vn) in zip(SMALL, upd):
        shp = W[n].shape
        delta[n], new_m[n], new_v[n] = d.reshape(shp), mn.reshape(shp), vn.reshape(shp)
    return (loss, grad_x, *[grads[n] for n in names], *[delta[n] for n in names],
            *[new_m[n] for n in names], *[new_v[n] for n in names])
```

```python
import functools
import math

import jax
import jax.numpy as jnp
from jax import lax
from jax.experimental import pallas as pl
from jax.experimental.pallas import tpu as pltpu

F32 = jnp.float32
BF16 = jnp.bfloat16
EPS = 1e-6
N_DEV = 8
CONV_CH = 512
CONV_K = 31
FOX_HEADS = 8
FOX_HEAD_DIM = 64
FOX_W = 512
MEM_HEADS = 4
MEM_HEAD_DIM = 256
HALO = 32
LANES = 128
ADAM_LR, ADAM_B1, ADAM_B2, ADAM_EPS, ADAM_WD, ADAM_STEP = 0.001, 0.9, 0.999, 1e-08, 0.01, 10
NEG = -1e30
VMEM_CAP = 60 * 1024 * 1024
MESH = pl.DeviceIdType.MESH


def _call(body, **kw):
    kw["out_shape"] = jax.tree.map(lambda s: pltpu.HBM(s.shape, s.dtype), kw["out_shape"])
    call = pl.pallas_call(body, **kw)
    return lambda *args: call(*[pltpu.with_memory_space_constraint(a, pltpu.HBM) for a in args])


def _params(sem=None, vmem=None, collective_id=None):
    kw = {} if collective_id is None else {"collective_id": collective_id}
    if sem is not None:
        kw["dimension_semantics"] = sem
    if vmem is not None:
        kw["vmem_limit_bytes"] = int(min(VMEM_CAP, vmem))
    return pltpu.CompilerParams(**kw)


def _nbytes(shape, dtype):
    return math.prod(shape) * jnp.dtype(dtype).itemsize


def _pick(n, target):
    best = None
    for d in range(LANES, min(n, target) + 1, LANES):
        if n % d == 0:
            best = d
    return n if best is None else best


def matmul(a, b, *, tb=False, out_dtype, res=None, tm=512, tn=512, name, rider=None, b_blk=None):
    a_list = list(a) if isinstance(a, (list, tuple)) else [a]
    b_list = list(b) if isinstance(b, (list, tuple)) else [b]
    n = len(a_list)
    assert len(b_list) == n
    M = a_list[0].shape[0]
    N = b_list[0].shape[0] if tb else b_list[0].shape[1]
    tm, tn = _pick(M, tm), _pick(N, tn)
    assert M % tm == 0 and N % tn == 0, (name, M, N, tm, tn)
    dn = (((1,), (1 if tb else 0,)), ((), ()))

    def body(*refs):
        acc = None
        for a_ref, b_ref in zip(refs[:n], refs[n:2 * n]):
            p = lax.dot_general(a_ref[...].astype(BF16), b_ref[...].astype(BF16), dn, preferred_element_type=F32)
            acc = p if acc is None else acc + p
        if res is not None:
            acc = acc + refs[2 * n][...].astype(F32)
        refs[-1][...] = acc.astype(out_dtype)

    o_spec = pl.BlockSpec((tm, tn), lambda i, j: (i, j))
    in_specs, est = [], 2 * _nbytes((tm, tn), out_dtype) + 2 * _nbytes((tm, tn), F32)
    for av in a_list:
        assert av.shape[0] == M
        in_specs.append(pl.BlockSpec((tm, av.shape[1]), lambda i, j: (i, 0)))
        est += (2 * jnp.dtype(av.dtype).itemsize + (av.dtype != BF16) * 2) * tm * av.shape[1]
    for idx, (av, bv) in enumerate(zip(a_list, b_list)):
        K = av.shape[1]
        kb = 0 if b_blk is None else b_blk[idx]
        assert bv.shape[0 if tb else 1] == N and bv.shape[1 if tb else 0] >= (kb + 1) * K, (name, av.shape, bv.shape)
        assert b_blk is not None or bv.shape[1 if tb else 0] == K, (name, av.shape, bv.shape)
        in_specs.append(pl.BlockSpec((tn, K), lambda i, j, kb=kb: (j, kb)) if tb
                        else pl.BlockSpec((K, tn), lambda i, j, kb=kb: (kb, j)))
        est += (2 * jnp.dtype(bv.dtype).itemsize + (bv.dtype != BF16) * 2) * tn * K
    args = a_list + b_list
    if res is not None:
        in_specs.append(o_spec)
        args.append(res)
        est += 2 * _nbytes((tm, tn), res.dtype)
    (out,), rode = hosted_call(
        body, rider, name=name, grid=(M // tm, N // tn), in_specs=in_specs, out_specs=[o_spec],
        out_shape=[jax.ShapeDtypeStruct((M, N), out_dtype)], scratch_shapes=[],
        args=args, vmem=est + (8 << 20),
    )
    return out if rider is None else (out, rode)


def _rms_scale(x):
    return lax.rsqrt(jnp.mean(x * x, axis=-1, keepdims=True) + EPS)


def rmsnorm_fwd(x, g, *, name, tm=512, rider=None):
    T, D = x.shape
    tm = min(tm, T)

    def body(x_ref, g_ref, o_ref, ot_ref):
        xv = x_ref[...]
        h = xv * _rms_scale(xv) * g_ref[...]
        o_ref[...] = h.astype(BF16)
        ot_ref[...] = h.T.astype(BF16)

    (h, h_t), rode = hosted_call(
        body, rider, name=name, grid=(T // tm,),
        in_specs=[pl.BlockSpec((tm, D), lambda i: (i, 0)), pl.BlockSpec((1, D), lambda i: (0, 0))],
        out_specs=[pl.BlockSpec((tm, D), lambda i: (i, 0)), pl.BlockSpec((D, tm), lambda i: (0, i))],
        out_shape=[jax.ShapeDtypeStruct((T, D), BF16), jax.ShapeDtypeStruct((D, T), BF16)],
        scratch_shapes=[], args=(x, g),
    )
    return (h, h_t) if rider is None else (h, h_t, rode)


def _rms_bwd_math(xv, gv, dh):
    r = _rms_scale(xv)
    xh = xv * r
    dg = jnp.sum(dh * xh, axis=0, keepdims=True)
    dxh = dh * gv
    dx = r * (dxh - xh * jnp.mean(dxh * xh, axis=-1, keepdims=True))
    return dx, dg


def rmsnorm_bwd(x, g, dh, dres, *, name, tm=256):
    T, D = x.shape
    tm = min(tm, T)

    def body(*refs):
        if dres is not None:
            x_ref, g_ref, dh_ref, dr_ref, dx_ref, dg_ref = refs
        else:
            x_ref, g_ref, dh_ref, dx_ref, dg_ref = refs
        dx, dg = _rms_bwd_math(x_ref[...], g_ref[...], dh_ref[...].astype(F32))
        if dres is not None:
            dx = dx + dr_ref[...]
        dx_ref[...] = dx

        @pl.when(pl.program_id(0) == 0)
        def _():
            dg_ref[...] = jnp.zeros_like(dg_ref)

        dg_ref[...] += dg

    row = pl.BlockSpec((tm, D), lambda i: (i, 0))
    vec = pl.BlockSpec((1, D), lambda i: (0, 0))
    ins, args = [row, vec, row], [x, g, dh]
    if dres is not None:
        ins.append(row)
        args.append(dres)
    return _call(
        body, name=name, grid=(T // tm,), in_specs=ins, out_specs=[row, vec],
        out_shape=[jax.ShapeDtypeStruct((T, D), F32), jax.ShapeDtypeStruct((1, D), F32)],
        compiler_params=_params(("arbitrary",)),
    )(*args)


def final_loss_bwd(x, g, target, *, name, tm=256):
    T, D = x.shape
    tm = min(tm, T)

    def body(x_ref, g_ref, t_ref, dx_ref, dg_ref, l_ref):
        xv, gv = x_ref[...], g_ref[...]
        e = xv * _rms_scale(xv) * gv - t_ref[...]
        part = 0.5 * jnp.sum(jnp.mean(e * e, axis=-1, keepdims=True), axis=0, keepdims=True)
        dx, dg = _rms_bwd_math(xv, gv, e * (1.0 / D))
        dx_ref[...] = dx

        @pl.when(pl.program_id(0) == 0)
        def _():
            dg_ref[...] = jnp.zeros_like(dg_ref)
            l_ref[...] = jnp.zeros_like(l_ref)

        dg_ref[...] += dg
        l_ref[...] += jnp.broadcast_to(part, l_ref.shape)

    row = pl.BlockSpec((tm, D), lambda i: (i, 0))
    vec = pl.BlockSpec((1, D), lambda i: (0, 0))
    return _call(
        body, name=name, grid=(T // tm,), in_specs=[row, vec, row],
        out_specs=[row, vec, pl.BlockSpec((1, LANES), lambda i: (0, 0))],
        out_shape=[jax.ShapeDtypeStruct((T, D), F32), jax.ShapeDtypeStruct((1, D), F32),
                   jax.ShapeDtypeStruct((1, LANES), F32)],
        compiler_params=_params(("arbitrary",)),
    )(x, g, target)


def _sigmoid(v):
    return 1.0 / (1.0 + jnp.exp(-v))


def _glu(blk):
    u = blk[:, :CONV_CH].astype(F32)
    gt = blk[:, CONV_CH:].astype(F32)
    return u * _sigmoid(gt)


def _fill_causal_ext(ext, cur_ref, halo_ref, s, ts):
    ext[pl.ds(HALO, ts), :] = _glu(cur_ref[0])
    hal = _glu(halo_ref[0])
    ext[pl.ds(0, HALO), :] = jnp.where(s > 0, hal, 0.0)


SUBLANES = 8


def _make_shifted(ext, sh):
    n = ext.shape[0]
    full = ext[...]
    for r in range(1, SUBLANES):
        sh[r - 1] = pltpu.roll(full, n - r, 0)


def _tap(ext, sh, off, ts):
    r = off % SUBLANES
    return ext[pl.ds(off, ts), :] if r == 0 else sh[r - 1, pl.ds(off - r, ts), :]


def _causal_conv(ext, sh, w_ref, ts):
    acc = jnp.zeros((ts, CONV_CH), F32)
    for j in range(CONV_K):
        acc = acc + _tap(ext, sh, HALO - (CONV_K - 1) + j, ts) * w_ref[pl.ds(j, 1), :]
    return acc


def _ln_stats(y):
    mu = jnp.mean(y, axis=-1, keepdims=True)
    yc = y - mu
    rstd = lax.rsqrt(jnp.mean(yc * yc, axis=-1, keepdims=True) + EPS)
    return yc * rstd, rstd


def _conv_specs(ts, S):
    nh = ts // HALO
    cur = pl.BlockSpec((1, ts, 2 * CONV_CH), lambda b, s: (b, s, 0))
    halo = pl.BlockSpec((1, HALO, 2 * CONV_CH), lambda b, s: (b, jnp.maximum(s * nh - 1, 0), 0))
    w = pl.BlockSpec((HALO, CONV_CH), lambda b, s: (0, 0))
    vec = pl.BlockSpec((1, CONV_CH), lambda b, s: (0, 0))
    return cur, halo, w, vec


def conv_branch_fwd(ug, conv_w, conv_b, ln_g, ln_b, *, name, ts=256, rider=None):
    B, S, _ = ug.shape
    ts = min(ts, S)
    ns = S // ts
    cur, halo, w, vec = _conv_specs(ts, S)

    def body(cur_ref, halo_ref, w_ref, cb_ref, lg_ref, lb_ref, o_ref, ot_ref, ext, sh):
        _fill_causal_ext(ext, cur_ref, halo_ref, pl.program_id(1), ts)
        _make_shifted(ext, sh)
        y = _causal_conv(ext, sh, w_ref, ts) + cb_ref[...]
        yh, _ = _ln_stats(y)
        ln = yh * lg_ref[...] + lb_ref[...]
        out = ln * _sigmoid(ln)
        o_ref[0] = out.astype(BF16)
        ot_ref[...] = out.T.astype(BF16)

    return hosted_call(
        body, rider, name=name, grid=(B, ns), in_specs=[cur, halo, w, vec, vec, vec],
        out_specs=[pl.BlockSpec((1, ts, CONV_CH), lambda b, s: (b, s, 0)),
                   pl.BlockSpec((CONV_CH, ts), lambda b, s: (0, b * ns + s))],
        out_shape=[jax.ShapeDtypeStruct((B, S, CONV_CH), BF16), jax.ShapeDtypeStruct((CONV_CH, B * S), BF16)],
        scratch_shapes=[pltpu.VMEM((ts + HALO, CONV_CH), F32),
                        pltpu.VMEM((SUBLANES - 1, ts + HALO, CONV_CH), F32)],
        args=(ug, ug, conv_w, conv_b, ln_g, ln_b),
    )


def conv_branch_bwd_a(ug, dcat, conv_w, conv_b, ln_g, ln_b, *, name, ts=256):
    B, S, _ = ug.shape
    ts = min(ts, S)
    cur, halo, w, vec = _conv_specs(ts, S)

    def body(cur_ref, halo_ref, d_ref, w_ref, cb_ref, lg_ref, lb_ref, dy_ref, dw_ref, dv_ref, ext, sh):
        _fill_causal_ext(ext, cur_ref, halo_ref, pl.program_id(1), ts)
        _make_shifted(ext, sh)
        y = _causal_conv(ext, sh, w_ref, ts) + cb_ref[...]
        yh, rstd = _ln_stats(y)
        lg = lg_ref[...]
        ln = yh * lg + lb_ref[...]
        sg = _sigmoid(ln)
        dln = d_ref[0].astype(F32) * (sg * (1.0 + ln * (1.0 - sg)))
        dyh = dln * lg
        dy = rstd * (dyh - jnp.mean(dyh, axis=-1, keepdims=True)
                     - yh * jnp.mean(dyh * yh, axis=-1, keepdims=True))
        dy_ref[0] = dy

        @pl.when((pl.program_id(0) == 0) & (pl.program_id(1) == 0))
        def _():
            dw_ref[...] = jnp.zeros_like(dw_ref)
            dv_ref[...] = jnp.zeros_like(dv_ref)

        dv_ref[pl.ds(0, 1), :] += jnp.sum(dy, axis=0, keepdims=True)
        dv_ref[pl.ds(1, 1), :] += jnp.sum(dln * yh, axis=0, keepdims=True)
        dv_ref[pl.ds(2, 1), :] += jnp.sum(dln, axis=0, keepdims=True)
        for j in range(CONV_K):
            tap = _tap(ext, sh, HALO - (CONV_K - 1) + j, ts)
            dw_ref[pl.ds(j, 1), :] += jnp.sum(dy * tap, axis=0, keepdims=True)

    return _call(
        body, name=name, grid=(B, S // ts),
        in_specs=[cur, halo, pl.BlockSpec((1, ts, CONV_CH), lambda b, s: (b, s, 0)), w, vec, vec, vec],
        out_specs=[pl.BlockSpec((1, ts, CONV_CH), lambda b, s: (b, s, 0)),
                   pl.BlockSpec((HALO, CONV_CH), lambda b, s: (0, 0)),
                   pl.BlockSpec((8, CONV_CH), lambda b, s: (0, 0))],
        out_shape=[jax.ShapeDtypeStruct((B, S, CONV_CH), F32),
                   jax.ShapeDtypeStruct((HALO, CONV_CH), F32),
                   jax.ShapeDtypeStruct((8, CONV_CH), F32)],
        scratch_shapes=[pltpu.VMEM((ts + HALO, CONV_CH), F32),
                        pltpu.VMEM((SUBLANES - 1, ts + HALO, CONV_CH), F32)],
        compiler_params=_params(("arbitrary", "arbitrary")),
    )(ug, ug, dcat, conv_w, conv_b, ln_g, ln_b)


def conv_branch_bwd_b(ug, dy, conv_w, *, name, ts=256):
    B, S, _ = ug.shape
    ts = min(ts, S)
    nh, n_halo = ts // HALO, S // HALO

    def body(cur_ref, dy_ref, nxt_ref, w_ref, o_ref, ext, sh):
        last = pl.program_id(1) == pl.num_programs(1) - 1
        ext[pl.ds(0, ts), :] = dy_ref[0]
        ext[pl.ds(ts, HALO), :] = jnp.where(last, 0.0, nxt_ref[0])
        _make_shifted(ext, sh)
        da = jnp.zeros((ts, CONV_CH), F32)
        for j in range(CONV_K):
            da = da + _tap(ext, sh, CONV_K - 1 - j, ts) * w_ref[pl.ds(j, 1), :]
        blk = cur_ref[0]
        u = blk[:, :CONV_CH].astype(F32)
        sg = _sigmoid(blk[:, CONV_CH:].astype(F32))
        o_ref[0, :, :CONV_CH] = (da * sg).astype(BF16)
        o_ref[0, :, CONV_CH:] = (da * u * sg * (1.0 - sg)).astype(BF16)

    return _call(
        body, name=name, grid=(B, S // ts),
        in_specs=[pl.BlockSpec((1, ts, 2 * CONV_CH), lambda b, s: (b, s, 0)),
                  pl.BlockSpec((1, ts, CONV_CH), lambda b, s: (b, s, 0)),
                  pl.BlockSpec((1, HALO, CONV_CH), lambda b, s: (b, jnp.minimum((s + 1) * nh, n_halo - 1), 0)),
                  pl.BlockSpec((HALO, CONV_CH), lambda b, s: (0, 0))],
        out_specs=pl.BlockSpec((1, ts, 2 * CONV_CH), lambda b, s: (b, s, 0)),
        out_shape=jax.ShapeDtypeStruct((B, S, 2 * CONV_CH), BF16),
        scratch_shapes=[pltpu.VMEM((ts + HALO, CONV_CH), F32),
                        pltpu.VMEM((SUBLANES - 1, ts + HALO, CONV_CH), F32)],
        compiler_params=_params(("parallel", "parallel")),
    )(ug, dy, dy, conv_w)


def _tri(n, lower):
    r = lax.broadcasted_iota(jnp.int32, (n, n), 0)
    c = lax.broadcasted_iota(jnp.int32, (n, n), 1)
    return ((r >= c) if lower else (r <= c)).astype(F32)


def _eye(n):
    r = lax.broadcasted_iota(jnp.int32, (n, n), 0)
    c = lax.broadcasted_iota(jnp.int32, (n, n), 1)
    return (r == c).astype(F32)


def _dot_hi(a, b, dn):
    return lax.dot_general(a, b, dn, precision=lax.Precision.HIGHEST, preferred_element_type=F32)


NN = (((1,), (0,)), ((), ()))
NT = (((1,), (1,)), ((), ()))
TN = (((0,), (0,)), ((), ()))


def _log_sigmoid(v):
    e = jnp.exp(-jnp.abs(v))
    log1p_e = jnp.where(e < 1e-3, e * (1.0 - 0.5 * e), jnp.log(1.0 + e))
    return jnp.minimum(v, 0.0) - log1p_e


def fgate_fwd(h, w_f, b_f, *, name, ts=256, rider=None):
    B, S, D = h.shape
    ts = min(ts, S)

    def body(h_ref, w_ref, b_ref, f_ref, cc_ref, cr_ref, carry):
        @pl.when(pl.program_id(1) == 0)
        def _():
            carry[...] = jnp.zeros_like(carry)

        f = jnp.dot(h_ref[0], w_ref[...], preferred_element_type=F32)
        f_ref[0] = f
        logf = _log_sigmoid(f + b_ref[...])
        c = _dot_hi(_tri(ts, True), logf, NN) + carry[pl.ds(0, 1), :]
        cc_ref[0] = c
        carry[pl.ds(0, 1), :] = c[ts - 1:ts, :]
        cr_ref[0] = _dot_hi(_eye(LANES), c, NT)

    return hosted_call(
        body, rider, name=name, grid=(B, S // ts),
        in_specs=[pl.BlockSpec((1, ts, D), lambda b, s: (b, s, 0)),
                  pl.BlockSpec((D, LANES), lambda b, s: (0, 0)),
                  pl.BlockSpec((1, LANES), lambda b, s: (0, 0))],
        out_specs=[pl.BlockSpec((1, ts, LANES), lambda b, s: (b, s, 0)),
                   pl.BlockSpec((1, ts, LANES), lambda b, s: (b, s, 0)),
                   pl.BlockSpec((1, LANES, ts), lambda b, s: (b, 0, s))],
        out_shape=[jax.ShapeDtypeStruct((B, S, LANES), F32), jax.ShapeDtypeStruct((B, S, LANES), F32),
                   jax.ShapeDtypeStruct((B, LANES, S), F32)],
        scratch_shapes=[pltpu.VMEM((8, LANES), F32)],
        args=(h, w_f, b_f),
    )


def fgate_bwd(dc, f, b_f, *, name, ts=256):
    B, S, _ = f.shape
    P = dc.shape[1]
    ts = min(ts, S)
    ns = S // ts

    def body(dc_ref, f_ref, b_ref, df_ref, db_ref, carry):
        @pl.when(pl.program_id(1) == 0)
        def _():
            carry[...] = jnp.zeros_like(carry)

        @pl.when((pl.program_id(0) == 0) & (pl.program_id(1) == 0))
        def _():
            db_ref[...] = jnp.zeros_like(db_ref)

        dc_t = dc_ref[0, 0]
        for j in range(1, P):
            dc_t = dc_t + dc_ref[0, j]
        dlogf = _dot_hi(_tri(ts, False), dc_t, NN) + carry[pl.ds(0, 1), :]
        carry[pl.ds(0, 1), :] = dlogf[0:1, :]
        df = dlogf * _sigmoid(-(f_ref[0] + b_ref[...]))
        df_ref[0] = df.astype(BF16)
        db_ref[...] += jnp.sum(df, axis=0, keepdims=True)

    return _call(
        body, name=name, grid=(B, ns),
        in_specs=[pl.BlockSpec((1, P, ts, LANES), lambda b, s: (b, 0, ns - 1 - s, 0)),
                  pl.BlockSpec((1, ts, LANES), lambda b, s: (b, ns - 1 - s, 0)),
                  pl.BlockSpec((1, LANES), lambda b, s: (0, 0))],
        out_specs=[pl.BlockSpec((1, ts, LANES), lambda b, s: (b, ns - 1 - s, 0)),
                   pl.BlockSpec((1, LANES), lambda b, s: (0, 0))],
        out_shape=[jax.ShapeDtypeStruct((B, S, LANES), BF16), jax.ShapeDtypeStruct((1, LANES), F32)],
        scratch_shapes=[pltpu.VMEM((8, LANES), F32)],
        compiler_params=_params(("arbitrary", "arbitrary")),
    )(dc, f, b_f)


def _lane_pick(tile, idx):
    lane = lax.broadcasted_iota(jnp.int32, tile.shape, 1)
    return jnp.sum(jnp.where(lane == idx, tile, 0.0), axis=-1, keepdims=True)


FOX_T = 512


def _fox_heads(q, cc_ref, p):
    lane = lax.broadcasted_iota(jnp.int32, q.shape, 1)
    qs = q * (1.0 / math.sqrt(FOX_HEAD_DIM))
    qhs = [jnp.where((lane < FOX_HEAD_DIM) == (hh == 0), qs, jnp.zeros_like(qs)) for hh in range(2)]
    crefs = [_lane_pick(cc_ref[0, pl.ds(0, 1), :], 2 * p + hh) for hh in range(2)]
    return qhs, crefs


def _fold_lanes(x, op):
    out = x[:, :LANES]
    for j in range(1, x.shape[1] // LANES):
        out = op(out, x[:, j * LANES:(j + 1) * LANES])
    return out


def _causal(t, transposed):
    r = lax.broadcasted_iota(jnp.int32, (t, t), 0)
    c = lax.broadcasted_iota(jnp.int32, (t, t), 1)
    return (r <= c) if transposed else (c <= r)


QKV0 = 8


def fox_fwd(z, c_col, c_row, *, name, rider=None):
    B, S, _ = z.shape
    assert S % FOX_T == 0
    tq, nq = FOX_T, S // FOX_T
    npair = FOX_HEADS // 2

    def body(q_ref, k_ref, v_ref, cc_ref, cr_ref, o_ref, l_ref, ot_ref, s_scr, m_scr, acc_scr):
        p, qi = pl.program_id(1), pl.program_id(2)
        qhs, crefs = _fox_heads(q_ref[0], cc_ref, p)
        lane = lax.broadcasted_iota(jnp.int32, (tq, LANES), 1)
        first = lane < FOX_HEAD_DIM
        for hh in range(2):
            m_scr[hh] = jnp.full((tq, LANES), NEG, F32)
            acc_scr[hh] = jnp.zeros((tq, LANES), F32)

        def logits(kb, diagonal):
            k0 = pl.multiple_of(kb * tq, tq)
            k = k_ref[0, pl.ds(k0, tq), :]
            for hh in range(2):
                s = lax.dot_general(qhs[hh], k, NT, preferred_element_type=F32)
                s = s + (crefs[hh] - cr_ref[0, pl.ds(2 * p + hh, 1), pl.ds(k0, tq)])
                if diagonal:
                    s = jnp.where(_causal(tq, False), s, NEG)
                s_scr[hh, kb] = s
                m_scr[hh] = jnp.maximum(m_scr[hh], _fold_lanes(s, jnp.maximum))

        def sweep1(kb, carry):
            logits(kb, False)
            return carry

        lax.fori_loop(0, qi, sweep1, 0)
        logits(qi, True)
        ms = [jnp.max(m_scr[hh], axis=-1, keepdims=True) for hh in range(2)]
        mbs = [jnp.broadcast_to(ms[hh], (tq, tq)) for hh in range(2)]

        for hh in range(2):
            m_scr[hh] = jnp.zeros((tq, LANES), F32)

        def weigh(kb, carry):
            k0 = pl.multiple_of(kb * tq, tq)
            v = v_ref[0, pl.ds(k0, tq), :]
            for hh in range(2):
                pr = jnp.exp(s_scr[hh, kb] - mbs[hh])
                m_scr[hh] += _fold_lanes(pr, jnp.add)
                acc_scr[hh] += jnp.dot(pr.astype(BF16), v, preferred_element_type=F32)
            return carry

        lax.fori_loop(0, qi + 1, weigh, 0)
        accs = [acc_scr[hh] for hh in range(2)]
        ls = [jnp.sum(m_scr[hh], axis=-1, keepdims=True) for hh in range(2)]
        out = jnp.where(first, accs[0] / ls[0], accs[1] / ls[1])
        o_ref[0] = out.astype(BF16)
        ot_ref[...] = out.T.astype(BF16)
        l_ref[0, 0] = jnp.where(first, ms[0] + jnp.log(ls[0]), ms[1] + jnp.log(ls[1]))

    return hosted_call(
        body, rider, name=name, grid=(B, npair, nq),
        in_specs=[pl.BlockSpec((1, tq, LANES), lambda b, p, i: (b, i, QKV0 + p)),
                  pl.BlockSpec((1, S, LANES), lambda b, p, i: (b, 0, QKV0 + npair + p)),
                  pl.BlockSpec((1, S, LANES), lambda b, p, i: (b, 0, QKV0 + 2 * npair + p)),
                  pl.BlockSpec((1, tq, LANES), lambda b, p, i: (b, i, 0)),
                  pl.BlockSpec((1, 8, S), lambda b, p, i: (b, 0, 0))],
        out_specs=[pl.BlockSpec((1, tq, LANES), lambda b, p, i: (b, i, p)),
                   pl.BlockSpec((1, 1, tq, LANES), lambda b, p, i: (b, p, i, 0)),
                   pl.BlockSpec((LANES, tq), lambda b, p, i: (p, b * nq + i))],
        out_shape=[jax.ShapeDtypeStruct((B, S, FOX_W), BF16),
                   jax.ShapeDtypeStruct((B, npair, S, LANES), F32),
                   jax.ShapeDtypeStruct((FOX_W, B * S), BF16)],
        scratch_shapes=[pltpu.VMEM((2, nq, tq, tq), F32), pltpu.VMEM((2, tq, LANES), F32),
                        pltpu.VMEM((2, tq, LANES), F32)],
        args=(z, z, z, c_col, c_row),
    )


def fox_bwd_dq(z, dcat, lse, c_col, c_row, *, name, rider=None):
    B, S, _ = z.shape
    tq, nq = FOX_T, S // FOX_T
    npair = FOX_HEADS // 2

    def body(q_ref, k_ref, v_ref, do_ref, l_ref, cc_ref, cr_ref, dq_ref, st_ref, p_scr, dp_scr, dl_scr):
        p, qi = pl.program_id(1), pl.program_id(2)
        qhs, crefs = _fox_heads(q_ref[0], cc_ref, p)
        lane = lax.broadcasted_iota(jnp.int32, (tq, LANES), 1)
        do_b = do_ref[0].astype(BF16)
        dohs = [jnp.where((lane < FOX_HEAD_DIM) == (hh == 0), do_b, jnp.zeros_like(do_b)) for hh in range(2)]
        lses = [_lane_pick(l_ref[0, 0], hh * FOX_HEAD_DIM) for hh in range(2)]
        lbs = [jnp.broadcast_to(lses[hh], (tq, tq)) for hh in range(2)]
        for hh in range(2):
            dl_scr[hh] = jnp.zeros((tq, LANES), F32)

        def probs(kb, diagonal):
            k0 = pl.multiple_of(kb * tq, tq)
            k = k_ref[0, pl.ds(k0, tq), :]
            v = v_ref[0, pl.ds(k0, tq), :]
            for hh in range(2):
                s = lax.dot_general(qhs[hh], k, NT, preferred_element_type=F32)
                s = s + (crefs[hh] - cr_ref[0, pl.ds(2 * p + hh, 1), pl.ds(k0, tq)])
                pr = jnp.exp(s - lbs[hh])
                if diagonal:
                    pr = jnp.where(_causal(tq, False), pr, 0.0)
                dp = lax.dot_general(dohs[hh], v, NT, preferred_element_type=F32)
                pdp = pr * dp
                dl_scr[hh] += _fold_lanes(pdp, jnp.add)
                p_scr[hh, kb] = pr
                dp_scr[hh, kb] = dp

        def first_pass(kb, carry):
            probs(kb, False)
            return carry

        lax.fori_loop(0, qi, first_pass, 0)
        probs(qi, True)

        dls = [jnp.sum(dl_scr[hh], axis=-1, keepdims=True) for hh in range(2)]
        dlbs = [jnp.broadcast_to(dls[hh], (tq, tq)) for hh in range(2)]

        def second_pass(kb, dq):
            k0 = pl.multiple_of(kb * tq, tq)
            k = k_ref[0, pl.ds(k0, tq), :]
            for hh in range(2):
                ds = p_scr[hh, kb] * (dp_scr[hh, kb] - dlbs[hh])
                kh = jnp.where((lane < FOX_HEAD_DIM) == (hh == 0), k, jnp.zeros_like(k))
                dq = dq + jnp.dot(ds.astype(BF16), kh, preferred_element_type=F32)
            return dq

        dq = lax.fori_loop(0, qi + 1, second_pass, jnp.zeros((tq, LANES), F32))
        dq_ref[0] = (dq * (1.0 / math.sqrt(FOX_HEAD_DIM))).astype(BF16)
        cols = jnp.zeros((tq, LANES), F32)
        for j, col in enumerate([crefs[0] - lses[0], crefs[1] - lses[1], dls[0], dls[1]]):
            cols = jnp.where(lane == j, col, cols)
        st_ref[0, 0] = _dot_hi(_eye(LANES), cols, NT)[:8]

    return hosted_call(
        body, rider, name=name, grid=(B, npair, nq),
        in_specs=[pl.BlockSpec((1, tq, LANES), lambda b, p, i: (b, i, QKV0 + p)),
                  pl.BlockSpec((1, S, LANES), lambda b, p, i: (b, 0, QKV0 + npair + p)),
                  pl.BlockSpec((1, S, LANES), lambda b, p, i: (b, 0, QKV0 + 2 * npair + p)),
                  pl.BlockSpec((1, tq, LANES), lambda b, p, i: (b, i, npair + p)),
                  pl.BlockSpec((1, 1, tq, LANES), lambda b, p, i: (b, p, i, 0)),
                  pl.BlockSpec((1, tq, LANES), lambda b, p, i: (b, i, 0)),
                  pl.BlockSpec((1, 8, S), lambda b, p, i: (b, 0, 0))],
        out_specs=[pl.BlockSpec((1, tq, LANES), lambda b, p, i: (b, i, p)),
                   pl.BlockSpec((1, 1, 8, tq), lambda b, p, i: (b, p, 0, i))],
        out_shape=[jax.ShapeDtypeStruct((B, S, FOX_W), BF16), jax.ShapeDtypeStruct((B, npair, 8, S), F32)],
        scratch_shapes=[pltpu.VMEM((2, nq, tq, tq), F32), pltpu.VMEM((2, nq, tq, tq), F32),
                        pltpu.VMEM((2, tq, LANES), F32)],
        args=(z, z, z, dcat, lse, c_col, c_row), vmem=56 << 20,
    )


def fox_bwd_dkdv(z, dcat, stats, c_col, *, name, rider=None):
    B, S, _ = z.shape
    tk, nq = FOX_T, S // FOX_T
    npair = FOX_HEADS // 2
    inv = 1.0 / math.sqrt(FOX_HEAD_DIM)

    def body(q_ref, k_ref, v_ref, do_ref, st_ref, cc_ref, dk_ref, dv_ref, dc_ref, dk_scr, dv_scr, dc_scr):
        p, kt = pl.program_id(1), pl.program_id(2)
        lane = lax.broadcasted_iota(jnp.int32, (tk, LANES), 1)
        masks = [(lane < FOX_HEAD_DIM) == (hh == 0) for hh in range(2)]
        k = k_ref[0]
        v = v_ref[0]
        khs = [jnp.where(masks[hh], k, jnp.zeros_like(k)) for hh in range(2)]
        vhs = [jnp.where(masks[hh], v, jnp.zeros_like(v)) for hh in range(2)]
        ccbs = [jnp.broadcast_to(_lane_pick(cc_ref[0], 2 * p + hh), (tk, tk)) for hh in range(2)]
        dk_scr[...] = jnp.zeros_like(dk_scr)
        dv_scr[...] = jnp.zeros_like(dv_scr)
        dc_scr[...] = jnp.zeros_like(dc_scr)

        def tile(qb, diagonal):
            q0 = pl.multiple_of(qb * tk, tk)
            qs = q_ref[0, pl.ds(q0, tk), :] * inv
            do_b = do_ref[0, pl.ds(q0, tk), :].astype(BF16)
            for hh in range(2):
                st = lax.dot_general(khs[hh], qs, NT, preferred_element_type=F32)
                pr = jnp.exp(st - ccbs[hh] + st_ref[0, 0, pl.ds(hh, 1), pl.ds(q0, tk)])
                if diagonal:
                    pr = jnp.where(_causal(tk, True), pr, 0.0)
                dp = lax.dot_general(vhs[hh], do_b, NT, preferred_element_type=F32)
                ds = pr * (dp - st_ref[0, 0, pl.ds(2 + hh, 1), pl.ds(q0, tk)])
                dv_scr[...] += jnp.dot(pr.astype(BF16), jnp.where(masks[hh], do_b, jnp.zeros_like(do_b)),
                                       preferred_element_type=F32)
                dk_scr[...] += jnp.dot(ds.astype(BF16), jnp.where(masks[hh], qs, jnp.zeros_like(qs)),
                                       preferred_element_type=F32)
                dc_scr[hh] -= _fold_lanes(ds, jnp.add)

        def later(qb, carry):
            tile(qb, False)
            return carry

        tile(kt, True)
        lax.fori_loop(kt + 1, nq, later, 0)
        dk_ref[0] = dk_scr[...].astype(BF16)
        dv_ref[0] = dv_scr[...].astype(BF16)
        dcs = [jnp.sum(dc_scr[hh], axis=-1, keepdims=True) for hh in range(2)]
        dc_ref[0, 0] = jnp.where(lane == 2 * p, dcs[0], jnp.where(lane == 2 * p + 1, dcs[1], 0.0))

    full = lambda col: pl.BlockSpec((1, S, LANES), col)
    tile_spec = lambda col: pl.BlockSpec((1, tk, LANES), col)
    return hosted_call(
        body, rider, name=name, grid=(B, npair, nq),
        in_specs=[full(lambda b, p, t: (b, 0, QKV0 + p)),
                  tile_spec(lambda b, p, t: (b, t, QKV0 + npair + p)),
                  tile_spec(lambda b, p, t: (b, t, QKV0 + 2 * npair + p)),
                  full(lambda b, p, t: (b, 0, npair + p)),
                  pl.BlockSpec((1, 1, 8, S), lambda b, p, t: (b, p, 0, 0)),
                  tile_spec(lambda b, p, t: (b, t, 0))],
        out_specs=[tile_spec(lambda b, p, t: (b, t, p)), tile_spec(lambda b, p, t: (b, t, p)),
                   pl.BlockSpec((1, 1, tk, LANES), lambda b, p, t: (b, p, t, 0))],
        out_shape=[jax.ShapeDtypeStruct((B, S, FOX_W), BF16)] * 2
        + [jax.ShapeDtypeStruct((B, npair, S, LANES), F32)],
        scratch_shapes=[pltpu.VMEM((tk, LANES), F32), pltpu.VMEM((tk, LANES), F32),
                        pltpu.VMEM((2, tk, LANES), F32)],
        args=(z, z, z, dcat, stats, c_col),
    )


def xattn_fwd(qm, kv, *, name, tq=512):
    B, S, D = qm.shape
    M = kv.shape[1]
    tq = min(tq, S)
    inv = 1.0 / math.sqrt(MEM_HEAD_DIM)

    nq = S // tq

    def body(q_ref, kv_ref, o_ref, ot_ref):
        for h in range(MEM_HEADS):
            c0 = h * MEM_HEAD_DIM
            qh = q_ref[0, :, c0:c0 + MEM_HEAD_DIM]
            kh = kv_ref[0, :, c0:c0 + MEM_HEAD_DIM]
            vh = kv_ref[0, :, D + c0:D + c0 + MEM_HEAD_DIM]
            s = lax.dot_general(qh, kh, NT, preferred_element_type=F32) * inv
            e = jnp.exp(s - jnp.max(s, axis=-1, keepdims=True))
            o = jnp.dot(e.astype(BF16), vh, preferred_element_type=F32) / jnp.sum(e, axis=-1, keepdims=True)
            o_ref[0, :, c0:c0 + MEM_HEAD_DIM] = o.astype(BF16)
            ot_ref[c0:c0 + MEM_HEAD_DIM, :] = o.T.astype(BF16)

    return _call(
        body, name=name, grid=(B, nq),
        in_specs=[pl.BlockSpec((1, tq, D), lambda b, i: (b, i, 0)),
                  pl.BlockSpec((1, M, 2 * D), lambda b, i: (b, 0, 0))],
        out_specs=[pl.BlockSpec((1, tq, D), lambda b, i: (b, i, 0)),
                   pl.BlockSpec((D, tq), lambda b, i: (0, b * nq + i))],
        out_shape=[jax.ShapeDtypeStruct((B, S, D), BF16), jax.ShapeDtypeStruct((D, B * S), BF16)],
        compiler_params=_params(("parallel", "parallel")),
    )(qm, kv)


def xattn_bwd(qm, kv, do, *, name, tq=512):
    B, S, D = qm.shape
    M = kv.shape[1]
    tq = min(tq, S)
    inv = 1.0 / math.sqrt(MEM_HEAD_DIM)

    def body(q_ref, kv_ref, do_ref, dq_ref, dkv_ref):
        @pl.when(pl.program_id(1) == 0)
        def _():
            dkv_ref[...] = jnp.zeros_like(dkv_ref)

        for h in range(MEM_HEADS):
            c0 = h * MEM_HEAD_DIM
            qh = q_ref[0, :, c0:c0 + MEM_HEAD_DIM]
            kh = kv_ref[0, :, c0:c0 + MEM_HEAD_DIM]
            vh = kv_ref[0, :, D + c0:D + c0 + MEM_HEAD_DIM]
            doh = do_ref[0, :, c0:c0 + MEM_HEAD_DIM]
            s = lax.dot_general(qh, kh, NT, preferred_element_type=F32) * inv
            e = jnp.exp(s - jnp.max(s, axis=-1, keepdims=True))
            pr = e / jnp.sum(e, axis=-1, keepdims=True)
            dp = lax.dot_general(doh, vh, NT, preferred_element_type=F32)
            ds = pr * (dp - jnp.sum(pr * dp, axis=-1, keepdims=True))
            ds_b = ds.astype(BF16)
            dq_ref[0, :, c0:c0 + MEM_HEAD_DIM] = (jnp.dot(ds_b, kh, preferred_element_type=F32) * inv).astype(BF16)
            dkv_ref[0, :, c0:c0 + MEM_HEAD_DIM] += lax.dot_general(ds_b, qh, TN, preferred_element_type=F32) * inv
            dkv_ref[0, :, D + c0:D + c0 + MEM_HEAD_DIM] += lax.dot_general(
                pr.astype(BF16), doh, TN, preferred_element_type=F32)

    row = pl.BlockSpec((1, tq, D), lambda b, i: (b, i, 0))
    kvs = pl.BlockSpec((1, M, 2 * D), lambda b, i: (b, 0, 0))
    return _call(
        body, name=name, grid=(B, S // tq), in_specs=[row, kvs, row], out_specs=[row, kvs],
        out_shape=[jax.ShapeDtypeStruct((B, S, D), BF16), jax.ShapeDtypeStruct((B, M, 2 * D), F32)],
        compiler_params=_params(("parallel", "arbitrary")),
    )(qm, kv, do)


SWIGLU_TN = 1408


def _chunks(n, w=256):
    return [(c0, min(w, n - c0)) for c0 in range(0, n, w)]


def mm_swiglu_fwd(hf, w_gu, *, name, tm=512):
    T, D = hf.shape
    Fh = w_gu.shape[1] // 2
    tm, tn = min(tm, T), SWIGLU_TN
    nj = Fh // tn
    assert Fh % tn == 0 and T % tm == 0

    def body(a_ref, bg_ref, bu_ref, g_ref, u_ref, o_ref, ot_ref):
        a = a_ref[...]
        for c0, cw in _chunks(tn):
            cols = pl.ds(c0, cw)
            g = jnp.dot(a, bg_ref[:, cols], preferred_element_type=F32)
            u = jnp.dot(a, bu_ref[:, cols], preferred_element_type=F32)
            act = g * _sigmoid(g) * u
            g_ref[:, cols] = g.astype(BF16)
            u_ref[:, cols] = u.astype(BF16)
            o_ref[:, cols] = act.astype(BF16)
            ot_ref[cols, :] = act.T.astype(BF16)

    tile = pl.BlockSpec((tm, tn), lambda i, j: (i, j))
    return _call(
        body, name=name, grid=(T // tm, nj),
        in_specs=[pl.BlockSpec((tm, D), lambda i, j: (i, 0)), pl.BlockSpec((D, tn), lambda i, j: (0, j)),
                  pl.BlockSpec((D, tn), lambda i, j: (0, nj + j))],
        out_specs=[tile, tile, tile, pl.BlockSpec((tn, tm), lambda i, j: (j, i))],
        out_shape=[jax.ShapeDtypeStruct((T, Fh), BF16)] * 3 + [jax.ShapeDtypeStruct((Fh, T), BF16)],
        compiler_params=_params(("parallel", "parallel"), 48 << 20),
    )(hf, w_gu, w_gu)


def mm_swiglu_bwd(dx, w_down, g, u, *, name, tm=512):
    T, D = dx.shape
    Fh = w_down.shape[0]
    tm, tn = min(tm, T), SWIGLU_TN
    assert Fh % tn == 0 and T % tm == 0

    def body(a_ref, b_ref, g_ref, u_ref, dg_ref, du_ref):
        a = a_ref[...].astype(BF16)
        for c0, cw in _chunks(tn):
            cols = pl.ds(c0, cw)
            d = lax.dot_general(a, b_ref[cols, :], NT, preferred_element_type=F32)
            gv = g_ref[:, cols].astype(F32)
            uv = u_ref[:, cols].astype(F32)
            sg = _sigmoid(gv)
            dg_ref[:, cols] = (d * uv * (sg * (1.0 + gv * (1.0 - sg)))).astype(BF16)
            du_ref[:, cols] = (d * gv * sg).astype(BF16)

    tile = pl.BlockSpec((tm, tn), lambda i, j: (i, j))
    return _call(
        body, name=name, grid=(T // tm, Fh // tn),
        in_specs=[pl.BlockSpec((tm, D), lambda i, j: (i, 0)), pl.BlockSpec((tn, D), lambda i, j: (j, 0)), tile, tile],
        out_specs=[tile, tile],
        out_shape=[jax.ShapeDtypeStruct((T, Fh), BF16)] * 2,
        compiler_params=_params(("parallel", "parallel"), 48 << 20),
    )(dx, w_down, g, u)


LATE_MID = ("w_out", "w_mq", "w_mo")
LATE_KV = ("w_mkv",)
LATE_FFN = ("w_gu", "w_down")
LATE = LATE_MID + LATE_KV + LATE_FFN
RS_GROUPS = (("w_gu", "w_down"), ("w_out", "w_mq", "w_mkv", "w_mo"), ("w_in",))


def pair_sums(names, g42, got):
    return {n: pair_sum(g, o, name="rs_pair_sum_" + n) for n, g, o in zip(names, g42, got)}


def local_step(x, mem, target, sp, first_shards, late_shards):
    B, S, D = x.shape
    T = B * S
    M = mem.shape[1]
    row = lambda v: v.reshape(1, -1).astype(F32)
    g_mix, g_x, g_mem, g_ffn, g_final = (row(sp[k]) for k in ("g_mix", "g_x", "g_mem", "g_ffn", "g_final"))
    conv_b, ln_g, ln_b = row(sp["conv_b"]), row(sp["ln_g"]), row(sp["ln_b"])
    b_f = jnp.pad(row(sp["b_f"]), ((0, 0), (0, LANES - FOX_HEADS)))
    n_ug, n_main = 2 * CONV_CH, 2 * CONV_CH + 3 * FOX_W

    x2d = x.reshape(T, D)
    h, h_t, partly = rmsnorm_fwd(x2d, g_mix, name="rms_mix", rider=AllGatherStage1(first_shards))
    w_in8, cw8 = all_gather_stage2(partly, name="ag_first_stage2")
    w_in_full = _full_from_gathered("w_in", w_in8)
    conv_w = cw8.transpose(1, 0, 2).reshape(HALO, -1)
    w_main, w_ug, w_qkv = w_in_full[:, :n_main], w_in_full[:, :n_ug], w_in_full[:, n_ug:n_main]
    w_f = jnp.pad(w_in_full[:, n_main:], ((0, 0), (0, LANES - FOX_HEADS)))
    z = matmul(h, w_main, out_dtype=BF16, tn=n_main, name="mm_in")
    z3 = z.reshape(B, S, n_main)
    n_mid, n_kv = len(LATE_MID), len(LATE_MID) + len(LATE_KV)
    (conv_out, conv_t), partly_mid = conv_branch_fwd(z3, conv_w, conv_b, ln_g, ln_b, name="conv_fwd",
                                                     rider=AllGatherStage1(late_shards[:n_mid]))
    (f_raw, c_col, c_row), partly_kv = fgate_fwd(h.reshape(B, S, D), w_f, b_f, name="fgate_fwd",
                                                 rider=AllGatherStage1(late_shards[n_mid:n_kv]))
    (att, lse, att_t), partly_ffn = fox_fwd(z3, c_col, c_row, name="fox_fwd",
                                            rider=AllGatherStage1(late_shards[n_kv:]))
    gathered = all_gather_stage2(partly_mid + partly_kv + partly_ffn, name="ag_late_stage2")
    wf = {n: _full_from_gathered(n, blk) for n, blk in zip(LATE, gathered)}
    x1 = matmul([conv_out.reshape(T, CONV_CH), att.reshape(T, FOX_W)],
                [wf["w_out"], wf["w_out"]], b_blk=[0, 1], out_dtype=F32, res=x2d, tn=D, name="mm_out")
    hx, hx_t = rmsnorm_fwd(x1, g_x, name="rms_x")
    qm = matmul(hx, wf["w_mq"], out_dtype=BF16, tn=D, name="mm_mq")
    mem2d = mem.reshape(B * M, D)
    mem_n, mem_n_t = rmsnorm_fwd(mem2d, g_mem, name="rms_mem")
    kv = matmul(mem_n, wf["w_mkv"], out_dtype=BF16, tn=2 * D, name="mm_mkv").reshape(B, M, 2 * D)
    o, o_t = xattn_fwd(qm.reshape(B, S, D), kv, name="xattn_fwd")
    o = o.reshape(T, D)
    x2 = matmul(o, wf["w_mo"], out_dtype=F32, res=x1, tn=D, name="mm_mo")
    hf, hf_t = rmsnorm_fwd(x2, g_ffn, name="rms_ffn")
    gate, up, act, act_t = mm_swiglu_fwd(hf, wf["w_gu"], name="mm_gu")
    x3 = matmul(act, wf["w_down"], out_dtype=F32, res=x2, tn=D, name="mm_down")
    dx3, dg_final, loss = final_loss_bwd(x3, g_final, target.reshape(T, D), name="loss_bwd")
    gw = {}
    gw["w_down"] = matmul(act_t, dx3, out_dtype=BF16, tm=1408, tn=256, name="dw_down")
    dgate, dup = mm_swiglu_bwd(dx3, wf["w_down"], gate, up, name="dx_down")
    gw["w_gu"] = [matmul(hf_t, dgate, out_dtype=BF16, tn=SWIGLU_TN, name="dw_gate"),
                  matmul(hf_t, dup, out_dtype=BF16, tn=SWIGLU_TN, name="dw_up")]
    g42 = [_shards_from_full(n, gw[n]) for n in RS_GROUPS[0]]
    dhf, got = matmul([dgate, dup], [wf["w_gu"], wf["w_gu"]], b_blk=[0, 1], tb=True, out_dtype=BF16,
                      tm=256, tn=D, name="dx_gu", rider=SiblingExchange(g42))
    parts = pair_sums(RS_GROUPS[0], g42, got)
    dx2, dg_ffn = rmsnorm_bwd(x2, g_ffn, dhf, dx3, name="rms_ffn_bwd")
    gw["w_mo"] = matmul(o_t, dx2, out_dtype=BF16, name="dw_mo")
    do = matmul(dx2, wf["w_mo"], tb=True, out_dtype=BF16, tn=D, name="dx_mo")
    dqm, dkv = xattn_bwd(qm.reshape(B, S, D), kv, do.reshape(B, S, D), name="xattn_bwd")
    dqm = dqm.reshape(T, D)
    dkv = dkv.reshape(B * M, 2 * D)
    gw["w_mq"] = matmul(hx_t, dqm, out_dtype=BF16, tn=D, name="dw_mq")
    dhx = matmul(dqm, wf["w_mq"], tb=True, out_dtype=BF16, tn=D, name="dx_mq")
    gw["w_mkv"] = matmul(mem_n_t, dkv, out_dtype=BF16, tn=D, name="dw_mkv")
    dmem_n = matmul(dkv, wf["w_mkv"], tb=True, out_dtype=BF16, tn=D, name="dx_mkv")
    _, dg_mem = rmsnorm_bwd(mem2d, g_mem, dmem_n, None, name="rms_mem_bwd")
    dx1, dg_x = rmsnorm_bwd(x1, g_x, dhx, dx2, name="rms_x_bwd")
    gw["w_out"] = jnp.concatenate([matmul(conv_t, dx1, out_dtype=BF16, name="dw_out_conv"),
                                   matmul(att_t, dx1, out_dtype=BF16, name="dw_out_att")], axis=0)
    g42 = [_shards_from_full(n, gw[n]) for n in RS_GROUPS[1]]
    dcat, got = matmul(dx1, wf["w_out"], tb=True, out_dtype=BF16, tn=D, name="dx_out", rider=SiblingExchange(g42))
    dcat = dcat.reshape(B, S, D)
    parts.update(pair_sums(RS_GROUPS[1], g42, got))
    dy, dconv_w, dvec = conv_branch_bwd_a(z3, dcat, conv_w, conv_b, ln_g, ln_b, name="conv_bwd_a")
    dug = conv_branch_bwd_b(z3, dy, conv_w, name="conv_bwd_b")
    gots = {}
    (dq, stats), got = fox_bwd_dq(z3, dcat, lse, c_col, c_row, name="fox_bwd_dq",
                                  rider=ChipExchange([parts[n] for n in RS_GROUPS[0]]))
    gots.update(zip(RS_GROUPS[0], got))
    (dk, dv, dc), got = fox_bwd_dkdv(z3, dcat, stats, c_col, name="fox_bwd_dkdv",
                                     rider=ChipExchange([parts[n] for n in RS_GROUPS[1]]))
    gots.update(zip(RS_GROUPS[1], got))
    df, db_f = fgate_bwd(dc, f_raw, b_f, name="fgate_bwd")
    dug2 = dug.reshape(T, n_ug)
    dqkv = jnp.concatenate([dq, dk, dv], axis=-1).reshape(T, 3 * FOX_W)
    df2 = df.reshape(T, LANES)
    dw_in = [matmul(h_t, dug2, out_dtype=BF16, tn=n_ug, name="dw_in_ug"),
             matmul(h_t, dqkv, out_dtype=BF16, tn=3 * FOX_W, name="dw_in_qkv"),
             matmul(h_t, df2, out_dtype=BF16, name="dw_f")[:, :FOX_HEADS]]
    g42 = [_shards_from_full("w_in", dw_in)]
    parts.update(pair_sums(RS_GROUPS[2], g42, run_rider(SiblingExchange(g42), name="rs_sibling_in")))
    dh, (gots["w_in"],) = matmul([dug2, dqkv, df2], [w_ug, w_qkv, w_f], tb=True, out_dtype=F32, tn=D,
                                 name="dx_in", rider=ChipExchange([parts["w_in"]]))
    dx, dg_mix = rmsnorm_bwd(x2d, g_mix, dh, dx1, name="rms_mix_bwd")
    gs = dict(g_mix=dg_mix, b_f=db_f[:, :FOX_HEADS], conv_w=dconv_w[:CONV_K], conv_b=dvec[0:1],
              ln_g=dvec[1:2], ln_b=dvec[2:3], g_x=dg_x, g_mem=dg_mem, g_ffn=dg_ffn, g_final=dg_final)
    return loss, dx.reshape(B, S, D), gs, {n: (parts[n], gots[n]) for n in BIG}


def _me():
    return lax.axis_index("x"), lax.axis_index("y"), lax.axis_index("c")


def _any_specs(n):
    return [pl.BlockSpec(memory_space=pl.ANY)] * n


def all_gather(xs, *, name):
    n = len(xs)

    def body(*refs):
        x_refs, out_refs = refs[:n], refs[n:2 * n]
        send_sems, recv_sems, local_sems = refs[2 * n:]
        x, y, c = _me()
        me, sibling = (x, y, c), (x, y, 1 - c)
        chips = [(1 - x, y), (x, 1 - y), (1 - x, 1 - y)]

        def slot(a, px, py, pc):
            return out_refs[a].at[4 * px + 2 * py + pc]

        def copy(a, k, block, to, own=False):
            return pltpu.make_async_remote_copy(
                src_ref=x_refs[a] if own else slot(a, *block), dst_ref=slot(a, *block),
                send_sem=send_sems.at[k, a], recv_sem=recv_sems.at[k, a], device_id=to, device_id_type=MESH)

        mine = [pltpu.make_async_copy(x_refs[a], slot(a, *me), local_sems.at[a]) for a in range(n)]
        first = [copy(a, 0, me, sibling, own=True) for a in range(n)]
        first += [copy(a, 1 + j, me, (*chip, c), own=True) for j, chip in enumerate(chips) for a in range(n)]
        for cp in mine + first:
            cp.start()
        passed = []
        for j, chip in enumerate(chips):
            for a in range(n):
                copy(a, 1 + j, (*chip, c), me).wait_recv()
                passed.append(copy(a, 4 + j, (*chip, c), sibling))
                passed[-1].start()
        for a in range(n):
            copy(a, 0, sibling, me).wait_recv()
            for j, chip in enumerate(chips):
                copy(a, 4 + j, (*chip, 1 - c), me).wait_recv()
        for cp in first + passed:
            cp.wait_send()
        for cp in mine:
            cp.wait()

    return _call(
        body, name=name, in_specs=_any_specs(n), out_specs=_any_specs(n),
        out_shape=[jax.ShapeDtypeStruct((N_DEV,) + v.shape, v.dtype) for v in xs],
        scratch_shapes=[pltpu.SemaphoreType.DMA((7, n)), pltpu.SemaphoreType.DMA((7, n)),
                        pltpu.SemaphoreType.DMA((n,))],
    )(*xs)


SIBLING_BARRIER = 1
CHIPS_BARRIER = 2
GATHER_BARRIER = 3


class SiblingExchange:
    collective_id = SIBLING_BARRIER

    def __init__(self, gs):
        n = len(gs)
        self.n, self.inputs = n, list(gs)
        self.out_shape = [jax.ShapeDtypeStruct((4,) + g.shape[2:], g.dtype) for g in gs]
        self.scratch = [pltpu.SemaphoreType.DMA((n,)), pltpu.SemaphoreType.DMA((n,))]

    @staticmethod
    def barrier_peers():
        x, y, c = _me()
        return [(x, y, 1 - c)]

    def _copies(self, g_refs, out_refs, sems):
        send_sems, recv_sems = sems
        x, y, c = _me()
        return [pltpu.make_async_remote_copy(
            src_ref=g_refs[a].at[:, 1 - c], dst_ref=out_refs[a], send_sem=send_sems.at[a],
            recv_sem=recv_sems.at[a], device_id=(x, y, 1 - c), device_id_type=MESH) for a in range(self.n)]

    def start(self, in_refs, out_refs, sems):
        for cp in self._copies(in_refs, out_refs, sems):
            cp.start()

    def finish(self, in_refs, out_refs, sems):
        for cp in self._copies(in_refs, out_refs, sems):
            cp.wait()


def run_rider(rider, *, name):
    return hosted_call(None, rider, name=name, grid=(), in_specs=[], out_specs=[], out_shape=[],
                       scratch_shapes=[], args=[])[1]


class ChipExchange:
    collective_id = CHIPS_BARRIER

    @staticmethod
    def barrier_peers():
        x, y, c = _me()
        return [(1 - x, y, c), (x, 1 - y, c), (1 - x, 1 - y, c)]

    def __init__(self, ps):
        n = len(ps)
        self.n, self.inputs = n, list(ps)
        self.out_shape = [jax.ShapeDtypeStruct(p.shape, p.dtype) for p in ps]
        self.scratch = [pltpu.SemaphoreType.DMA((3, n)), pltpu.SemaphoreType.DMA((3, n))]

    def _copies(self, p_refs, out_refs, sems, outgoing):
        send_sems, recv_sems = sems
        x, y, c = _me()
        my_chip = 2 * x + y
        cps = []
        for k in range(3):
            px, py = x ^ ((k + 1) >> 1), y ^ ((k + 1) & 1)
            src, dst = (2 * px + py, my_chip) if outgoing else (my_chip, 2 * px + py)
            for a in range(self.n):
                cps.append(pltpu.make_async_remote_copy(
                    src_ref=p_refs[a].at[src], dst_ref=out_refs[a].at[dst], send_sem=send_sems.at[k, a],
                    recv_sem=recv_sems.at[k, a], device_id=(px, py, c), device_id_type=MESH))
        return cps

    def start(self, in_refs, out_refs, sems):
        for cp in self._copies(in_refs, out_refs, sems, True):
            cp.start()

    def finish(self, in_refs, out_refs, sems):
        for cp in self._copies(in_refs, out_refs, sems, False):
            cp.wait_recv()
        for cp in self._copies(in_refs, out_refs, sems, True):
            cp.wait_send()


class AllGatherStage1:
    collective_id = GATHER_BARRIER

    @staticmethod
    def barrier_peers():
        x, y, c = _me()
        return [(x, y, 1 - c), (1 - x, y, c), (x, 1 - y, c), (1 - x, 1 - y, c)]

    def __init__(self, xs):
        n = len(xs)
        self.n, self.inputs = n, list(xs)
        self.out_shape = [jax.ShapeDtypeStruct((N_DEV,) + v.shape, v.dtype) for v in xs]
        self.scratch = [pltpu.SemaphoreType.DMA((4, n)), pltpu.SemaphoreType.DMA((4, n)),
                        pltpu.SemaphoreType.DMA((n,))]

    def _copies(self, x_refs, out_refs, sems, kind):
        send_sems, recv_sems, local_sems = sems
        x, y, c = _me()
        slot = lambda a, d: out_refs[a].at[4 * d[0] + 2 * d[1] + d[2]]
        if kind == "local":
            return [pltpu.make_async_copy(x_refs[a], slot(a, (x, y, c)), local_sems.at[a]) for a in range(self.n)]
        cps = []
        for k, peer in enumerate([(x, y, 1 - c), (1 - x, y, c), (x, 1 - y, c), (1 - x, 1 - y, c)]):
            for a in range(self.n):
                cps.append(pltpu.make_async_remote_copy(
                    src_ref=x_refs[a], dst_ref=slot(a, (x, y, c) if kind == "out" else peer),
                    send_sem=send_sems.at[k, a], recv_sem=recv_sems.at[k, a], device_id=peer, device_id_type=MESH))
        return cps

    def start(self, in_refs, out_refs, sems):
        for cp in self._copies(in_refs, out_refs, sems, "local") + self._copies(in_refs, out_refs, sems, "out"):
            cp.start()

    def finish(self, in_refs, out_refs, sems):
        for cp in self._copies(in_refs, out_refs, sems, "in"):
            cp.wait_recv()
        for cp in self._copies(in_refs, out_refs, sems, "out"):
            cp.wait_send()
        for cp in self._copies(in_refs, out_refs, sems, "local"):
            cp.wait()


def all_gather_stage2(outs, *, name):
    n = len(outs)

    def body(*refs):
        out_refs = refs[n:2 * n]
        send_sems, recv_sems = refs[2 * n:]
        x, y, c = _me()
        _peer_barrier([(x, y, 1 - c)])
        sends, recvs = [], []
        for k, (px, py) in enumerate([(1 - x, y), (x, 1 - y), (1 - x, 1 - y)]):
            for a in range(n):
                mk = lambda pc: pltpu.make_async_remote_copy(
                    src_ref=out_refs[a].at[4 * px + 2 * py + c], dst_ref=out_refs[a].at[4 * px + 2 * py + pc],
                    send_sem=send_sems.at[k, a], recv_sem=recv_sems.at[k, a], device_id=(x, y, 1 - c),
                    device_id_type=MESH)
                sends.append(mk(c))
                recvs.append(mk(1 - c))
        for cp in sends:
            cp.start()
        for cp in recvs:
            cp.wait_recv()
        for cp in sends:
            cp.wait_send()

    return _call(
        body, name=name, in_specs=_any_specs(n), out_specs=_any_specs(n),
        out_shape=[jax.ShapeDtypeStruct(o.shape, o.dtype) for o in outs],
        input_output_aliases={a: a for a in range(n)},
        scratch_shapes=[pltpu.SemaphoreType.DMA((3, n)), pltpu.SemaphoreType.DMA((3, n))],
        compiler_params=_params(collective_id=SIBLING_BARRIER),
    )(*outs)


def _peer_barrier(peers):
    barrier = pltpu.get_barrier_semaphore()
    for peer in peers:
        pl.semaphore_signal(barrier, inc=1, device_id=peer, device_id_type=MESH)
    pl.semaphore_wait(barrier, len(peers))


def hosted_call(body, rider, *, name, grid, in_specs, out_specs, out_shape, scratch_shapes, args, vmem=None):
    n_in, n_out, n_scr = len(in_specs), len(out_specs), len(scratch_shapes)
    r_in, r_out = (len(rider.inputs), len(rider.out_shape)) if rider is not None else (0, 0)
    own_barrier = getattr(rider, "collective_id", None) is not None

    def wrapped(*refs):
        ins, refs = refs[:n_in], refs[n_in:]
        rins, refs = refs[:r_in], refs[r_in:]
        outs, refs = refs[:n_out], refs[n_out:]
        routs, refs = refs[:r_out], refs[r_out:]
        scr, rscr = refs[:n_scr], refs[n_scr:]
        ids = [pl.program_id(d) for d in range(len(grid))]
        first = functools.reduce(jnp.logical_and, [i == 0 for i in ids], True)
        last = functools.reduce(jnp.logical_and, [i == g - 1 for i, g in zip(ids, grid)], True)

        def begin():
            if own_barrier:
                _peer_barrier(rider.barrier_peers())
            rider.start(rins, routs, rscr)

        if rider is not None and grid:
            pl.when(first)(begin)
        elif rider is not None:
            begin()
        if body is not None:
            body(*ins, *outs, *scr)
        if rider is not None and grid:
            pl.when(last)(lambda: rider.finish(rins, routs, rscr))
        elif rider is not None:
            rider.finish(rins, routs, rscr)

    kw = dict(grid=grid) if grid else {}
    if grid or vmem is not None or own_barrier:
        kw["compiler_params"] = _params(("arbitrary",) * len(grid) if grid else None, vmem,
                                        rider.collective_id if own_barrier else None)
    res = _call(
        wrapped, name=name, in_specs=list(in_specs) + _any_specs(r_in), out_specs=list(out_specs) + _any_specs(r_out),
        out_shape=list(out_shape) + (rider.out_shape if rider is not None else []),
        scratch_shapes=list(scratch_shapes) + (rider.scratch if rider is not None else []), **kw,
    )(*args, *(rider.inputs if rider is not None else []))
    return list(res[:n_out]), list(res[n_out:])


def _pick_rows(r, target=256):
    best = None
    for d in range(16, min(r, target) + 1, 16):
        if r % d == 0:
            best = d
    return r if best is None else best


def pair_sum(g, got, *, name):
    _, _, R, C = g.shape
    tr = _pick_rows(R)

    def body(g_ref, got_ref, o_ref):
        mine = jnp.where(lax.axis_index("c") == 0, g_ref[:, 0], g_ref[:, 1])
        o_ref[...] = (mine.astype(F32) + got_ref[...].astype(F32)).astype(o_ref.dtype)

    return _call(
        body, name=name, grid=(R // tr,),
        in_specs=[pl.BlockSpec((4, 2, tr, C), lambda i: (0, 0, i, 0)), pl.BlockSpec((4, tr, C), lambda i: (0, i, 0))],
        out_specs=pl.BlockSpec((4, tr, C), lambda i: (0, i, 0)),
        out_shape=jax.ShapeDtypeStruct((4, R, C), g.dtype),
        compiler_params=_params(("parallel",)),
    )(g, got)


def chip_sum_adamw(p, got, w, m, v, *, name):
    _, R, C = p.shape
    assert w.shape == (1, R, C), (name, w.shape, p.shape)
    tr = _pick_rows(R)

    def body(p_ref, got_ref, w_ref, m_ref, v_ref, g_ref, d_ref, mo_ref, vo_ref):
        my_chip = 2 * lax.axis_index("x") + lax.axis_index("y")
        g = jnp.zeros((tr, C), F32)
        for j in range(4):
            g = g + jnp.where(my_chip == j, p_ref[j], got_ref[j]).astype(F32)
        g_ref[0] = g
        d_ref[0], mo_ref[0], vo_ref[0] = _adamw_math(w_ref[0], g, m_ref[0], v_ref[0])

    part = pl.BlockSpec((4, tr, C), lambda i: (0, i, 0))
    spec = pl.BlockSpec((1, tr, C), lambda i: (0, i, 0))
    return _call(
        body, name=name, grid=(R // tr,), in_specs=[part, part, spec, spec, spec], out_specs=[spec] * 4,
        out_shape=[jax.ShapeDtypeStruct((1, R, C), F32)] * 4,
        compiler_params=_params(("parallel",)),
    )(p, got, w, m, v)


def rows_sum(g8, *, name):
    _, R, C = g8.shape

    def body(g_ref, o_ref):
        acc = g_ref[0]
        for j in range(1, N_DEV):
            acc = acc + g_ref[j]
        o_ref[...] = acc

    return _call(body, name=name, out_shape=jax.ShapeDtypeStruct((R, C), F32))(g8)


def _adamw_math(w, g, m, v):
    m = ADAM_B1 * m + (1.0 - ADAM_B1) * g
    v = ADAM_B2 * v + (1.0 - ADAM_B2) * (g * g)
    m_hat = m / (1.0 - ADAM_B1 ** ADAM_STEP)
    v_hat = v / (1.0 - ADAM_B2 ** ADAM_STEP)
    delta = -ADAM_LR * (m_hat / (jnp.sqrt(v_hat) + ADAM_EPS) + ADAM_WD * w)
    return delta, m, v


def adamw_small(wgmv, *, name):
    n = len(wgmv)

    def body(*refs):
        ins, outs = refs[:4 * n], refs[4 * n:]
        for a in range(n):
            w_ref, g_ref, m_ref, v_ref = ins[4 * a:4 * a + 4]
            d, mn, vn = _adamw_math(w_ref[...], g_ref[...], m_ref[...], v_ref[...])
            outs[3 * a][...] = d
            outs[3 * a + 1][...] = mn
            outs[3 * a + 2][...] = vn

    flat = [t for tup in wgmv for t in tup]
    res = _call(
        body, name=name,
        out_shape=[jax.ShapeDtypeStruct(tup[0].shape, F32) for tup in wgmv for _ in range(3)],
    )(*flat)
    return [tuple(res[3 * a:3 * a + 3]) for a in range(n)]


BIG = ("w_in", "w_out", "w_mq", "w_mkv", "w_mo", "w_gu", "w_down")
COL_SHARDED = ("w_in", "w_mkv", "w_gu")
SMALL = ("g_mix", "b_f", "conv_w", "conv_b", "ln_g", "ln_b", "g_x", "g_mem", "g_ffn", "g_final")


def _full_from_gathered(n, blk):
    _, rr, cc = blk.shape
    if n in COL_SHARDED:
        return jnp.concatenate([blk[k] for k in range(N_DEV)], axis=1)
    return blk.reshape(N_DEV * rr, cc)


def _shards_from_full(n, g):
    pieces = g if isinstance(g, list) else [g]
    rr, cc = pieces[0].shape[0], sum(p.shape[1] for p in pieces)
    if n in COL_SHARDED:
        w = cc // N_DEV
        return jnp.stack([_columns(pieces, k * w, w) for k in range(N_DEV)]).reshape(4, 2, rr, w)
    return pieces[0].reshape(4, 2, rr // N_DEV, cc)


def _columns(pieces, start, width):
    out, c0 = [], 0
    for p in pieces:
        lo, hi = max(start, c0), min(start + width, c0 + p.shape[1])
        if lo < hi:
            out.append(p[:, lo - c0:hi - c0])
        c0 += p.shape[1]
    return out[0] if len(out) == 1 else jnp.concatenate(out, axis=1)


def _small_layout():
    sizes = dict(g_mix=1024, b_f=8, conv_w=CONV_K * CONV_CH, conv_b=512, ln_g=512, ln_b=512, g_x=1024,
                 g_mem=1024, g_ffn=1024, g_final=1024, loss=1)
    lay, r0 = {}, 0
    for n, sz in sizes.items():
        r = -(-sz // LANES)
        lay[n] = (r0, r, sz)
        r0 += r
    return lay, -(-r0 // 8) * 8


def kernel(x, mem, g_mix, w_in, b_f, conv_w, conv_b, ln_g, ln_b, w_out, g_x, g_mem, w_mq, w_mkv, w_mo, g_ffn, w_gu, w_down, g_final, loss_target, m_g_mix, m_w_in, m_b_f, m_conv_w, m_conv_b, m_ln_g, m_ln_b, m_w_out, m_g_x, m_g_mem, m_w_mq, m_w_mkv, m_w_mo, m_g_ffn, m_w_gu, m_w_down, m_g_final, v_g_mix, v_w_in, v_b_f, v_conv_w, v_conv_b, v_ln_g, v_ln_b, v_w_out, v_g_x, v_g_mem, v_w_mq, v_w_mkv, v_w_mo, v_g_ffn, v_w_gu, v_w_down, v_g_final):
    names = ["g_mix", "w_in", "b_f", "conv_w", "conv_b", "ln_g", "ln_b", "w_out", "g_x", "g_mem", "w_mq",
             "w_mkv", "w_mo", "g_ffn", "w_gu", "w_down", "g_final"]
    W = dict(zip(names, [g_mix, w_in, b_f, conv_w, conv_b, ln_g, ln_b, w_out, g_x, g_mem, w_mq, w_mkv, w_mo,
                         g_ffn, w_gu, w_down, g_final]))
    Mo = dict(zip(names, [m_g_mix, m_w_in, m_b_f, m_conv_w, m_conv_b, m_ln_g, m_ln_b, m_w_out, m_g_x, m_g_mem,
                          m_w_mq, m_w_mkv, m_w_mo, m_g_ffn, m_w_gu, m_w_down, m_g_final]))
    Vo = dict(zip(names, [v_g_mix, v_w_in, v_b_f, v_conv_w, v_conv_b, v_ln_g, v_ln_b, v_w_out, v_g_x, v_g_mem,
                          v_w_mq, v_w_mkv, v_w_mo, v_g_ffn, v_w_gu, v_w_down, v_g_final]))
    dev = 4 * lax.axis_index("x") + 2 * lax.axis_index("y") + lax.axis_index("c")

    two = lambda a: a.reshape(-1, a.shape[-1])
    cw_shard = jnp.pad(two(conv_w), ((0, HALO - CONV_K), (0, 0)))
    sp = dict(g_mix=g_mix, b_f=b_f, conv_b=conv_b, ln_g=ln_g, ln_b=ln_b, g_x=g_x, g_mem=g_mem,
              g_ffn=g_ffn, g_final=g_final)
    loss_blk, grad_x, gs, reduced = local_step(x, mem, loss_target, sp, [two(w_in).astype(BF16), cw_shard],
                                               [two(W[n]).astype(BF16) for n in LATE])

    lay, rs = _small_layout()
    small = {**{n: gs[n] for n in SMALL}, "loss": loss_blk[:, :1]}
    parts = []
    for n, (r0, r, sz) in lay.items():
        flat = small[n].reshape(-1).astype(F32)
        parts.append(jnp.pad(flat, (0, r * LANES - sz)).reshape(r, LANES))
    spack = jnp.concatenate(parts, axis=0)
    spack = jnp.pad(spack, ((0, rs - spack.shape[0]), (0, 0)))
    ssum = rows_sum(all_gather([spack], name="ag_small")[0], name="small_sum")
    gsmall = {n: ssum[r0:r0 + r].reshape(-1)[:sz] for n, (r0, r, sz) in lay.items()}
    loss = gsmall["loss"].reshape(())

    grads, delta, new_m, new_v = {}, {}, {}, {}
    for n in BIG:
        p, o = reduced[n]
        grads[n], delta[n], new_m[n], new_v[n] = chip_sum_adamw(p, o, W[n], Mo[n], Vo[n], name="adamw_" + n)
    for n in SMALL:
        if n == "conv_w":
            full = gsmall[n].reshape(CONV_K, CONV_CH)
            ncol = conv_w.shape[-1]
            grads[n] = lax.dynamic_slice(full, (0, dev * ncol), (CONV_K, ncol)).reshape(conv_w.shape)
        else:
            grads[n] = gsmall[n].reshape(W[n].shape)
    upd = adamw_small([(two(W[n]), two(grads[n]), two(Mo[n]), two(Vo[n])) for n in SMALL], name="adamw_small")
    for n, (d, mn, vn) in zip(SMALL, upd):
        shp = W[n].shape
        delta[n], new_m[n], new_v[n] = d.reshape(shp), mn.reshape(shp), vn.reshape(shp)
    return (loss, grad_x, *[grads[n] for n in names], *[delta[n] for n in names],
            *[new_m[n] for n in names], *[new_v[n] for n in names])
```

```python
import functools
import math

import jax
import jax.numpy as jnp
from jax import lax
from jax.experimental import pallas as pl
from jax.experimental.pallas import tpu as pltpu

F32 = jnp.float32
BF16 = jnp.bfloat16
EPS = 1e-6
N_DEV = 8
CONV_CH = 512
CONV_K = 31
FOX_HEADS = 8
FOX_HEAD_DIM = 64
FOX_W = 512
MEM_HEADS = 4
MEM_HEAD_DIM = 256
HALO = 32
LANES = 128
ADAM_LR, ADAM_B1, ADAM_B2, ADAM_EPS, ADAM_WD, ADAM_STEP = 0.001, 0.9, 0.999, 1e-08, 0.01, 10
NEG = -1e30
VMEM_CAP = 60 * 1024 * 1024
MESH = pl.DeviceIdType.MESH


def _call(body, **kw):
    kw["out_shape"] = jax.tree.map(lambda s: pltpu.HBM(s.shape, s.dtype), kw["out_shape"])
    call = pl.pallas_call(body, **kw)
    return lambda *args: call(*[pltpu.with_memory_space_constraint(a, pltpu.HBM) for a in args])


def _params(sem=None, vmem=None, collective_id=None):
    kw = {} if collective_id is None else {"collective_id": collective_id}
    if sem is not None:
        kw["dimension_semantics"] = sem
    if vmem is not None:
        kw["vmem_limit_bytes"] = int(min(VMEM_CAP, vmem))
    return pltpu.CompilerParams(**kw)


def _nbytes(shape, dtype):
    return math.prod(shape) * jnp.dtype(dtype).itemsize


def _pick(n, target):
    best = None
    for d in range(LANES, min(n, target) + 1, LANES):
        if n % d == 0:
            best = d
    return n if best is None else best


def matmul(a, b, *, tb=False, out_dtype, res=None, tm=512, tn=512, name, rider=None, b_blk=None):
    a_list = list(a) if isinstance(a, (list, tuple)) else [a]
    b_list = list(b) if isinstance(b, (list, tuple)) else [b]
    n = len(a_list)
    assert len(b_list) == n
    M = a_list[0].shape[0]
    N = b_list[0].shape[0] if tb else b_list[0].shape[1]
    tm, tn = _pick(M, tm), _pick(N, tn)
    assert M % tm == 0 and N % tn == 0, (name, M, N, tm, tn)
    dn = (((1,), (1 if tb else 0,)), ((), ()))

    def body(*refs):
        acc = None
        for a_ref, b_ref in zip(refs[:n], refs[n:2 * n]):
            p = lax.dot_general(a_ref[...].astype(BF16), b_ref[...].astype(BF16), dn, preferred_element_type=F32)
            acc = p if acc is None else acc + p
        if res is not None:
            acc = acc + refs[2 * n][...].astype(F32)
        refs[-1][...] = acc.astype(out_dtype)

    o_spec = pl.BlockSpec((tm, tn), lambda i, j: (i, j))
    in_specs, est = [], 2 * _nbytes((tm, tn), out_dtype) + 2 * _nbytes((tm, tn), F32)
    for av in a_list:
        assert av.shape[0] == M
        in_specs.append(pl.BlockSpec((tm, av.shape[1]), lambda i, j: (i, 0)))
        est += (2 * jnp.dtype(av.dtype).itemsize + (av.dtype != BF16) * 2) * tm * av.shape[1]
    for idx, (av, bv) in enumerate(zip(a_list, b_list)):
        K = av.shape[1]
        kb = 0 if b_blk is None else b_blk[idx]
        assert bv.shape[0 if tb else 1] == N and bv.shape[1 if tb else 0] >= (kb + 1) * K, (name, av.shape, bv.shape)
        assert b_blk is not None or bv.shape[1 if tb else 0] == K, (name, av.shape, bv.shape)
        in_specs.append(pl.BlockSpec((tn, K), lambda i, j, kb=kb: (j, kb)) if tb
                        else pl.BlockSpec((K, tn), lambda i, j, kb=kb: (kb, j)))
        est += (2 * jnp.dtype(bv.dtype).itemsize + (bv.dtype != BF16) * 2) * tn * K
    args = a_list + b_list
    if res is not None:
        in_specs.append(o_spec)
        args.append(res)
        est += 2 * _nbytes((tm, tn), res.dtype)
    (out,), rode = hosted_call(
        body, rider, name=name, grid=(M // tm, N // tn), in_specs=in_specs, out_specs=[o_spec],
        out_shape=[jax.ShapeDtypeStruct((M, N), out_dtype)], scratch_shapes=[],
        args=args, vmem=est + (8 << 20),
    )
    return out if rider is None else (out, rode)


def _rms_scale(x):
    return lax.rsqrt(jnp.mean(x * x, axis=-1, keepdims=True) + EPS)


def rmsnorm_fwd(x, g, *, name, tm=512, rider=None):
    T, D = x.shape
    tm = min(tm, T)

    def body(x_ref, g_ref, o_ref, ot_ref):
        xv = x_ref[...]
        h = xv * _rms_scale(xv) * g_ref[...]
        o_ref[...] = h.astype(BF16)
        ot_ref[...] = h.T.astype(BF16)

    (h, h_t), rode = hosted_call(
        body, rider, name=name, grid=(T // tm,),
        in_specs=[pl.BlockSpec((tm, D), lambda i: (i, 0)), pl.BlockSpec((1, D), lambda i: (0, 0))],
        out_specs=[pl.BlockSpec((tm, D), lambda i: (i, 0)), pl.BlockSpec((D, tm), lambda i: (0, i))],
        out_shape=[jax.ShapeDtypeStruct((T, D), BF16), jax.ShapeDtypeStruct((D, T), BF16)],
        scratch_shapes=[], args=(x, g),
    )
    return (h, h_t) if rider is None else (h, h_t, rode)


def _rms_bwd_math(xv, gv, dh):
    r = _rms_scale(xv)
    xh = xv * r
    dg = jnp.sum(dh * xh, axis=0, keepdims=True)
    dxh = dh * gv
    dx = r * (dxh - xh * jnp.mean(dxh * xh, axis=-1, keepdims=True))
    return dx, dg


def rmsnorm_bwd(x, g, dh, dres, *, name, tm=256, out_dtype=BF16):
    T, D = x.shape
    tm = min(tm, T)

    def body(*refs):
        if dres is not None:
            x_ref, g_ref, dh_ref, dr_ref, dx_ref, dg_ref = refs
        else:
            x_ref, g_ref, dh_ref, dx_ref, dg_ref = refs
        dx, dg = _rms_bwd_math(x_ref[...], g_ref[...], dh_ref[...].astype(F32))
        if dres is not None:
            dx = dx + dr_ref[...].astype(F32)
        dx_ref[...] = dx.astype(out_dtype)

        @pl.when(pl.program_id(0) == 0)
        def _():
            dg_ref[...] = jnp.zeros_like(dg_ref)

        dg_ref[...] += dg

    row = pl.BlockSpec((tm, D), lambda i: (i, 0))
    vec = pl.BlockSpec((1, D), lambda i: (0, 0))
    ins, args = [row, vec, row], [x, g, dh]
    if dres is not None:
        ins.append(row)
        args.append(dres)
    return _call(
        body, name=name, grid=(T // tm,), in_specs=ins, out_specs=[row, vec],
        out_shape=[jax.ShapeDtypeStruct((T, D), out_dtype), jax.ShapeDtypeStruct((1, D), F32)],
        compiler_params=_params(("arbitrary",)),
    )(*args)


def final_loss_bwd(x, g, target, *, name, tm=256):
    T, D = x.shape
    tm = min(tm, T)

    def body(x_ref, g_ref, t_ref, dx_ref, dg_ref, l_ref):
        xv, gv = x_ref[...], g_ref[...]
        e = xv * _rms_scale(xv) * gv - t_ref[...]
        part = 0.5 * jnp.sum(jnp.mean(e * e, axis=-1, keepdims=True), axis=0, keepdims=True)
        dx, dg = _rms_bwd_math(xv, gv, e * (1.0 / D))
        dx_ref[...] = dx.astype(BF16)

        @pl.when(pl.program_id(0) == 0)
        def _():
            dg_ref[...] = jnp.zeros_like(dg_ref)
            l_ref[...] = jnp.zeros_like(l_ref)

        dg_ref[...] += dg
        l_ref[...] += jnp.broadcast_to(part, l_ref.shape)

    row = pl.BlockSpec((tm, D), lambda i: (i, 0))
    vec = pl.BlockSpec((1, D), lambda i: (0, 0))
    return _call(
        body, name=name, grid=(T // tm,), in_specs=[row, vec, row],
        out_specs=[row, vec, pl.BlockSpec((1, LANES), lambda i: (0, 0))],
        out_shape=[jax.ShapeDtypeStruct((T, D), BF16), jax.ShapeDtypeStruct((1, D), F32),
                   jax.ShapeDtypeStruct((1, LANES), F32)],
        compiler_params=_params(("arbitrary",)),
    )(x, g, target)


def _sigmoid(v):
    return 1.0 / (1.0 + jnp.exp(-v))


def _glu(blk):
    u = blk[:, :CONV_CH].astype(F32)
    gt = blk[:, CONV_CH:].astype(F32)
    return u * _sigmoid(gt)


def _fill_causal_ext(ext, cur_ref, halo_ref, s, ts):
    ext[pl.ds(HALO, ts), :] = _glu(cur_ref[0])
    hal = _glu(halo_ref[0])
    ext[pl.ds(0, HALO), :] = jnp.where(s > 0, hal, 0.0)


SUBLANES = 8


def _make_shifted(ext, sh):
    n = ext.shape[0]
    full = ext[...]
    for r in range(1, SUBLANES):
        sh[r - 1] = pltpu.roll(full, n - r, 0)


def _tap(ext, sh, off, ts):
    r = off % SUBLANES
    return ext[pl.ds(off, ts), :] if r == 0 else sh[r - 1, pl.ds(off - r, ts), :]


def _causal_conv(ext, sh, w_ref, ts):
    acc = jnp.zeros((ts, CONV_CH), F32)
    for j in range(CONV_K):
        acc = acc + _tap(ext, sh, HALO - (CONV_K - 1) + j, ts) * w_ref[pl.ds(j, 1), :]
    return acc


def _ln_stats(y):
    mu = jnp.mean(y, axis=-1, keepdims=True)
    yc = y - mu
    rstd = lax.rsqrt(jnp.mean(yc * yc, axis=-1, keepdims=True) + EPS)
    return yc * rstd, rstd


def _conv_specs(ts, S):
    nh = ts // HALO
    cur = pl.BlockSpec((1, ts, 2 * CONV_CH), lambda b, s: (b, s, 0))
    halo = pl.BlockSpec((1, HALO, 2 * CONV_CH), lambda b, s: (b, jnp.maximum(s * nh - 1, 0), 0))
    w = pl.BlockSpec((HALO, CONV_CH), lambda b, s: (0, 0))
    vec = pl.BlockSpec((1, CONV_CH), lambda b, s: (0, 0))
    return cur, halo, w, vec


def conv_branch_fwd(ug, conv_w, conv_b, ln_g, ln_b, *, name, ts=256, rider=None):
    B, S, _ = ug.shape
    ts = min(ts, S)
    ns = S // ts
    cur, halo, w, vec = _conv_specs(ts, S)

    def body(cur_ref, halo_ref, w_ref, cb_ref, lg_ref, lb_ref, o_ref, ot_ref, ext, sh):
        _fill_causal_ext(ext, cur_ref, halo_ref, pl.program_id(1), ts)
        _make_shifted(ext, sh)
        y = _causal_conv(ext, sh, w_ref, ts) + cb_ref[...]
        yh, _ = _ln_stats(y)
        ln = yh * lg_ref[...] + lb_ref[...]
        out = ln * _sigmoid(ln)
        o_ref[0] = out.astype(BF16)
        ot_ref[...] = out.T.astype(BF16)

    return hosted_call(
        body, rider, name=name, grid=(B, ns), in_specs=[cur, halo, w, vec, vec, vec],
        out_specs=[pl.BlockSpec((1, ts, CONV_CH), lambda b, s: (b, s, 0)),
                   pl.BlockSpec((CONV_CH, ts), lambda b, s: (0, b * ns + s))],
        out_shape=[jax.ShapeDtypeStruct((B, S, CONV_CH), BF16), jax.ShapeDtypeStruct((CONV_CH, B * S), BF16)],
        scratch_shapes=[pltpu.VMEM((ts + HALO, CONV_CH), F32),
                        pltpu.VMEM((SUBLANES - 1, ts + HALO, CONV_CH), F32)],
        args=(ug, ug, conv_w, conv_b, ln_g, ln_b),
    )


def conv_branch_bwd_a(ug, dcat, conv_w, conv_b, ln_g, ln_b, *, name, ts=256):
    B, S, _ = ug.shape
    ts = min(ts, S)
    cur, halo, w, vec = _conv_specs(ts, S)

    def body(cur_ref, halo_ref, d_ref, w_ref, cb_ref, lg_ref, lb_ref, dy_ref, dw_ref, dv_ref, ext, sh):
        _fill_causal_ext(ext, cur_ref, halo_ref, pl.program_id(1), ts)
        _make_shifted(ext, sh)
        y = _causal_conv(ext, sh, w_ref, ts) + cb_ref[...]
        yh, rstd = _ln_stats(y)
        lg = lg_ref[...]
        ln = yh * lg + lb_ref[...]
        sg = _sigmoid(ln)
        dln = d_ref[0].astype(F32) * (sg * (1.0 + ln * (1.0 - sg)))
        dyh = dln * lg
        dy = rstd * (dyh - jnp.mean(dyh, axis=-1, keepdims=True)
                     - yh * jnp.mean(dyh * yh, axis=-1, keepdims=True))
        dy_ref[0] = dy

        @pl.when((pl.program_id(0) == 0) & (pl.program_id(1) == 0))
        def _():
            dw_ref[...] = jnp.zeros_like(dw_ref)
            dv_ref[...] = jnp.zeros_like(dv_ref)

        dv_ref[pl.ds(0, 1), :] += jnp.sum(dy, axis=0, keepdims=True)
        dv_ref[pl.ds(1, 1), :] += jnp.sum(dln * yh, axis=0, keepdims=True)
        dv_ref[pl.ds(2, 1), :] += jnp.sum(dln, axis=0, keepdims=True)
        for j in range(CONV_K):
            tap = _tap(ext, sh, HALO - (CONV_K - 1) + j, ts)
            dw_ref[pl.ds(j, 1), :] += jnp.sum(dy * tap, axis=0, keepdims=True)

    return _call(
        body, name=name, grid=(B, S // ts),
        in_specs=[cur, halo, pl.BlockSpec((1, ts, CONV_CH), lambda b, s: (b, s, 0)), w, vec, vec, vec],
        out_specs=[pl.BlockSpec((1, ts, CONV_CH), lambda b, s: (b, s, 0)),
                   pl.BlockSpec((HALO, CONV_CH), lambda b, s: (0, 0)),
                   pl.BlockSpec((8, CONV_CH), lambda b, s: (0, 0))],
        out_shape=[jax.ShapeDtypeStruct((B, S, CONV_CH), F32),
                   jax.ShapeDtypeStruct((HALO, CONV_CH), F32),
                   jax.ShapeDtypeStruct((8, CONV_CH), F32)],
        scratch_shapes=[pltpu.VMEM((ts + HALO, CONV_CH), F32),
                        pltpu.VMEM((SUBLANES - 1, ts + HALO, CONV_CH), F32)],
        compiler_params=_params(("arbitrary", "arbitrary")),
    )(ug, ug, dcat, conv_w, conv_b, ln_g, ln_b)


def conv_branch_bwd_b(ug, dy, conv_w, *, name, ts=256):
    B, S, _ = ug.shape
    ts = min(ts, S)
    nh, n_halo = ts // HALO, S // HALO

    def body(cur_ref, dy_ref, nxt_ref, w_ref, o_ref, ext, sh):
        last = pl.program_id(1) == pl.num_programs(1) - 1
        ext[pl.ds(0, ts), :] = dy_ref[0]
        ext[pl.ds(ts, HALO), :] = jnp.where(last, 0.0, nxt_ref[0])
        _make_shifted(ext, sh)
        da = jnp.zeros((ts, CONV_CH), F32)
        for j in range(CONV_K):
            da = da + _tap(ext, sh, CONV_K - 1 - j, ts) * w_ref[pl.ds(j, 1), :]
        blk = cur_ref[0]
        u = blk[:, :CONV_CH].astype(F32)
        sg = _sigmoid(blk[:, CONV_CH:].astype(F32))
        o_ref[0, :, :CONV_CH] = (da * sg).astype(BF16)
        o_ref[0, :, CONV_CH:] = (da * u * sg * (1.0 - sg)).astype(BF16)

    return _call(
        body, name=name, grid=(B, S // ts),
        in_specs=[pl.BlockSpec((1, ts, 2 * CONV_CH), lambda b, s: (b, s, 0)),
                  pl.BlockSpec((1, ts, CONV_CH), lambda b, s: (b, s, 0)),
                  pl.BlockSpec((1, HALO, CONV_CH), lambda b, s: (b, jnp.minimum((s + 1) * nh, n_halo - 1), 0)),
                  pl.BlockSpec((HALO, CONV_CH), lambda b, s: (0, 0))],
        out_specs=pl.BlockSpec((1, ts, 2 * CONV_CH), lambda b, s: (b, s, 0)),
        out_shape=jax.ShapeDtypeStruct((B, S, 2 * CONV_CH), BF16),
        scratch_shapes=[pltpu.VMEM((ts + HALO, CONV_CH), F32),
                        pltpu.VMEM((SUBLANES - 1, ts + HALO, CONV_CH), F32)],
        compiler_params=_params(("parallel", "parallel")),
    )(ug, dy, dy, conv_w)


def _tri(n, lower):
    r = lax.broadcasted_iota(jnp.int32, (n, n), 0)
    c = lax.broadcasted_iota(jnp.int32, (n, n), 1)
    return ((r >= c) if lower else (r <= c)).astype(F32)


def _eye(n):
    r = lax.broadcasted_iota(jnp.int32, (n, n), 0)
    c = lax.broadcasted_iota(jnp.int32, (n, n), 1)
    return (r == c).astype(F32)


def _dot_hi(a, b, dn):
    return lax.dot_general(a, b, dn, precision=lax.Precision.HIGHEST, preferred_element_type=F32)


NN = (((1,), (0,)), ((), ()))
NT = (((1,), (1,)), ((), ()))
TN = (((0,), (0,)), ((), ()))


def _log_sigmoid(v):
    e = jnp.exp(-jnp.abs(v))
    log1p_e = jnp.where(e < 1e-3, e * (1.0 - 0.5 * e), jnp.log(1.0 + e))
    return jnp.minimum(v, 0.0) - log1p_e


def fgate_fwd(h, w_f, b_f, *, name, ts=256, rider=None):
    B, S, D = h.shape
    ts = min(ts, S)

    def body(h_ref, w_ref, b_ref, f_ref, cc_ref, cr_ref, carry):
        @pl.when(pl.program_id(1) == 0)
        def _():
            carry[...] = jnp.zeros_like(carry)

        f = jnp.dot(h_ref[0], w_ref[...], preferred_element_type=F32)
        f_ref[0] = f
        logf = _log_sigmoid(f + b_ref[...])
        c = _dot_hi(_tri(ts, True), logf, NN) + carry[pl.ds(0, 1), :]
        cc_ref[0] = c
        carry[pl.ds(0, 1), :] = c[ts - 1:ts, :]
        cr_ref[0] = _dot_hi(_eye(LANES), c, NT)

    return hosted_call(
        body, rider, name=name, grid=(B, S // ts),
        in_specs=[pl.BlockSpec((1, ts, D), lambda b, s: (b, s, 0)),
                  pl.BlockSpec((D, LANES), lambda b, s: (0, 0)),
                  pl.BlockSpec((1, LANES), lambda b, s: (0, 0))],
        out_specs=[pl.BlockSpec((1, ts, LANES), lambda b, s: (b, s, 0)),
                   pl.BlockSpec((1, ts, LANES), lambda b, s: (b, s, 0)),
                   pl.BlockSpec((1, LANES, ts), lambda b, s: (b, 0, s))],
        out_shape=[jax.ShapeDtypeStruct((B, S, LANES), F32), jax.ShapeDtypeStruct((B, S, LANES), F32),
                   jax.ShapeDtypeStruct((B, LANES, S), F32)],
        scratch_shapes=[pltpu.VMEM((8, LANES), F32)],
        args=(h, w_f, b_f),
    )


def fgate_bwd(dc, f, b_f, *, name, ts=256):
    B, S, _ = f.shape
    P = dc.shape[1]
    ts = min(ts, S)
    ns = S // ts

    def body(dc_ref, f_ref, b_ref, df_ref, db_ref, carry):
        @pl.when(pl.program_id(1) == 0)
        def _():
            carry[...] = jnp.zeros_like(carry)

        @pl.when((pl.program_id(0) == 0) & (pl.program_id(1) == 0))
        def _():
            db_ref[...] = jnp.zeros_like(db_ref)

        dc_t = dc_ref[0, 0]
        for j in range(1, P):
            dc_t = dc_t + dc_ref[0, j]
        dlogf = _dot_hi(_tri(ts, False), dc_t, NN) + carry[pl.ds(0, 1), :]
        carry[pl.ds(0, 1), :] = dlogf[0:1, :]
        df = dlogf * _sigmoid(-(f_ref[0] + b_ref[...]))
        df_ref[0] = df.astype(BF16)
        db_ref[...] += jnp.sum(df, axis=0, keepdims=True)

    return _call(
        body, name=name, grid=(B, ns),
        in_specs=[pl.BlockSpec((1, P, ts, LANES), lambda b, s: (b, 0, ns - 1 - s, 0)),
                  pl.BlockSpec((1, ts, LANES), lambda b, s: (b, ns - 1 - s, 0)),
                  pl.BlockSpec((1, LANES), lambda b, s: (0, 0))],
        out_specs=[pl.BlockSpec((1, ts, LANES), lambda b, s: (b, ns - 1 - s, 0)),
                   pl.BlockSpec((1, LANES), lambda b, s: (0, 0))],
        out_shape=[jax.ShapeDtypeStruct((B, S, LANES), BF16), jax.ShapeDtypeStruct((1, LANES), F32)],
        scratch_shapes=[pltpu.VMEM((8, LANES), F32)],
        compiler_params=_params(("arbitrary", "arbitrary")),
    )(dc, f, b_f)


def _lane_pick(tile, idx):
    lane = lax.broadcasted_iota(jnp.int32, tile.shape, 1)
    return jnp.sum(jnp.where(lane == idx, tile, 0.0), axis=-1, keepdims=True)


FOX_T = 512


def _fox_heads(q, cc_ref, p):
    lane = lax.broadcasted_iota(jnp.int32, q.shape, 1)
    qs = q * (1.0 / math.sqrt(FOX_HEAD_DIM))
    qhs = [jnp.where((lane < FOX_HEAD_DIM) == (hh == 0), qs, jnp.zeros_like(qs)) for hh in range(2)]
    crefs = [_lane_pick(cc_ref[0, pl.ds(0, 1), :], 2 * p + hh) for hh in range(2)]
    return qhs, crefs


def _fold_lanes(x, op):
    out = x[:, :LANES]
    for j in range(1, x.shape[1] // LANES):
        out = op(out, x[:, j * LANES:(j + 1) * LANES])
    return out


def _causal(t, transposed):
    r = lax.broadcasted_iota(jnp.int32, (t, t), 0)
    c = lax.broadcasted_iota(jnp.int32, (t, t), 1)
    return (r <= c) if transposed else (c <= r)


QKV0 = 8


def fox_fwd(z, c_col, c_row, *, name, rider=None):
    B, S, _ = z.shape
    assert S % FOX_T == 0
    tq, nq = FOX_T, S // FOX_T
    npair = FOX_HEADS // 2

    def body(q_ref, k_ref, v_ref, cc_ref, cr_ref, o_ref, l_ref, ot_ref, s_scr, m_scr, acc_scr):
        p, qi = pl.program_id(1), pl.program_id(2)
        qhs, crefs = _fox_heads(q_ref[0], cc_ref, p)
        lane = lax.broadcasted_iota(jnp.int32, (tq, LANES), 1)
        first = lane < FOX_HEAD_DIM
        for hh in range(2):
            m_scr[hh] = jnp.full((tq, LANES), NEG, F32)
            acc_scr[hh] = jnp.zeros((tq, LANES), F32)

        def logits(kb, diagonal):
            k0 = pl.multiple_of(kb * tq, tq)
            k = k_ref[0, pl.ds(k0, tq), :]
            for hh in range(2):
                s = lax.dot_general(qhs[hh], k, NT, preferred_element_type=F32)
                s = s + (crefs[hh] - cr_ref[0, pl.ds(2 * p + hh, 1), pl.ds(k0, tq)])
                if diagonal:
                    s = jnp.where(_causal(tq, False), s, NEG)
                s_scr[hh, kb] = s
                m_scr[hh] = jnp.maximum(m_scr[hh], _fold_lanes(s, jnp.maximum))

        def sweep1(kb, carry):
            logits(kb, False)
            return carry

        lax.fori_loop(0, qi, sweep1, 0)
        logits(qi, True)
        ms = [jnp.max(m_scr[hh], axis=-1, keepdims=True) for hh in range(2)]
        mbs = [jnp.broadcast_to(ms[hh], (tq, tq)) for hh in range(2)]

        for hh in range(2):
            m_scr[hh] = jnp.zeros((tq, LANES), F32)

        def weigh(kb, carry):
            k0 = pl.multiple_of(kb * tq, tq)
            v = v_ref[0, pl.ds(k0, tq), :]
            for hh in range(2):
                pr = jnp.exp(s_scr[hh, kb] - mbs[hh])
                m_scr[hh] += _fold_lanes(pr, jnp.add)
                acc_scr[hh] += jnp.dot(pr.astype(BF16), v, preferred_element_type=F32)
            return carry

        lax.fori_loop(0, qi + 1, weigh, 0)
        accs = [acc_scr[hh] for hh in range(2)]
        ls = [jnp.sum(m_scr[hh], axis=-1, keepdims=True) for hh in range(2)]
        out = jnp.where(first, accs[0] / ls[0], accs[1] / ls[1])
        o_ref[0] = out.astype(BF16)
        ot_ref[...] = out.T.astype(BF16)
        l_ref[0, 0] = jnp.where(first, ms[0] + jnp.log(ls[0]), ms[1] + jnp.log(ls[1]))

    return hosted_call(
        body, rider, name=name, grid=(B, npair, nq),
        in_specs=[pl.BlockSpec((1, tq, LANES), lambda b, p, i: (b, i, QKV0 + p)),
                  pl.BlockSpec((1, S, LANES), lambda b, p, i: (b, 0, QKV0 + npair + p)),
                  pl.BlockSpec((1, S, LANES), lambda b, p, i: (b, 0, QKV0 + 2 * npair + p)),
                  pl.BlockSpec((1, tq, LANES), lambda b, p, i: (b, i, 0)),
                  pl.BlockSpec((1, 8, S), lambda b, p, i: (b, 0, 0))],
        out_specs=[pl.BlockSpec((1, tq, LANES), lambda b, p, i: (b, i, p)),
                   pl.BlockSpec((1, 1, tq, LANES), lambda b, p, i: (b, p, i, 0)),
                   pl.BlockSpec((LANES, tq), lambda b, p, i: (p, b * nq + i))],
        out_shape=[jax.ShapeDtypeStruct((B, S, FOX_W), BF16),
                   jax.ShapeDtypeStruct((B, npair, S, LANES), F32),
                   jax.ShapeDtypeStruct((FOX_W, B * S), BF16)],
        scratch_shapes=[pltpu.VMEM((2, nq, tq, tq), F32), pltpu.VMEM((2, tq, LANES), F32),
                        pltpu.VMEM((2, tq, LANES), F32)],
        args=(z, z, z, c_col, c_row),
    )


def fox_bwd_dq(z, dcat, lse, c_col, c_row, *, name, rider=None):
    B, S, _ = z.shape
    tq, nq = FOX_T, S // FOX_T
    npair = FOX_HEADS // 2

    def body(q_ref, k_ref, v_ref, do_ref, l_ref, cc_ref, cr_ref, dq_ref, st_ref, p_scr, dp_scr, dl_scr):
        p, qi = pl.program_id(1), pl.program_id(2)
        qhs, crefs = _fox_heads(q_ref[0], cc_ref, p)
        lane = lax.broadcasted_iota(jnp.int32, (tq, LANES), 1)
        do_b = do_ref[0].astype(BF16)
        dohs = [jnp.where((lane < FOX_HEAD_DIM) == (hh == 0), do_b, jnp.zeros_like(do_b)) for hh in range(2)]
        lses = [_lane_pick(l_ref[0, 0], hh * FOX_HEAD_DIM) for hh in range(2)]
        lbs = [jnp.broadcast_to(lses[hh], (tq, tq)) for hh in range(2)]
        for hh in range(2):
            dl_scr[hh] = jnp.zeros((tq, LANES), F32)

        def probs(kb, diagonal):
            k0 = pl.multiple_of(kb * tq, tq)
            k = k_ref[0, pl.ds(k0, tq), :]
            v = v_ref[0, pl.ds(k0, tq), :]
            for hh in range(2):
                s = lax.dot_general(qhs[hh], k, NT, preferred_element_type=F32)
                s = s + (crefs[hh] - cr_ref[0, pl.ds(2 * p + hh, 1), pl.ds(k0, tq)])
                pr = jnp.exp(s - lbs[hh])
                if diagonal:
                    pr = jnp.where(_causal(tq, False), pr, 0.0)
                dp = lax.dot_general(dohs[hh], v, NT, preferred_element_type=F32)
                pdp = pr * dp
                dl_scr[hh] += _fold_lanes(pdp, jnp.add)
                p_scr[hh, kb] = pr
                dp_scr[hh, kb] = dp

        def first_pass(kb, carry):
            probs(kb, False)
            return carry

        lax.fori_loop(0, qi, first_pass, 0)
        probs(qi, True)

        dls = [jnp.sum(dl_scr[hh], axis=-1, keepdims=True) for hh in range(2)]
        dlbs = [jnp.broadcast_to(dls[hh], (tq, tq)) for hh in range(2)]

        def second_pass(kb, dq):
            k0 = pl.multiple_of(kb * tq, tq)
            k = k_ref[0, pl.ds(k0, tq), :]
            for hh in range(2):
                ds = p_scr[hh, kb] * (dp_scr[hh, kb] - dlbs[hh])
                kh = jnp.where((lane < FOX_HEAD_DIM) == (hh == 0), k, jnp.zeros_like(k))
                dq = dq + jnp.dot(ds.astype(BF16), kh, preferred_element_type=F32)
            return dq

        dq = lax.fori_loop(0, qi + 1, second_pass, jnp.zeros((tq, LANES), F32))
        dq_ref[0] = (dq * (1.0 / math.sqrt(FOX_HEAD_DIM))).astype(BF16)
        cols = jnp.zeros((tq, LANES), F32)
        for j, col in enumerate([crefs[0] - lses[0], crefs[1] - lses[1], dls[0], dls[1]]):
            cols = jnp.where(lane == j, col, cols)
        st_ref[0, 0] = _dot_hi(_eye(LANES), cols, NT)[:8]

    return hosted_call(
        body, rider, name=name, grid=(B, npair, nq),
        in_specs=[pl.BlockSpec((1, tq, LANES), lambda b, p, i: (b, i, QKV0 + p)),
                  pl.BlockSpec((1, S, LANES), lambda b, p, i: (b, 0, QKV0 + npair + p)),
                  pl.BlockSpec((1, S, LANES), lambda b, p, i: (b, 0, QKV0 + 2 * npair + p)),
                  pl.BlockSpec((1, tq, LANES), lambda b, p, i: (b, i, npair + p)),
                  pl.BlockSpec((1, 1, tq, LANES), lambda b, p, i: (b, p, i, 0)),
                  pl.BlockSpec((1, tq, LANES), lambda b, p, i: (b, i, 0)),
                  pl.BlockSpec((1, 8, S), lambda b, p, i: (b, 0, 0))],
        out_specs=[pl.BlockSpec((1, tq, LANES), lambda b, p, i: (b, i, p)),
                   pl.BlockSpec((1, 1, 8, tq), lambda b, p, i: (b, p, 0, i))],
        out_shape=[jax.ShapeDtypeStruct((B, S, FOX_W), BF16), jax.ShapeDtypeStruct((B, npair, 8, S), F32)],
        scratch_shapes=[pltpu.VMEM((2, nq, tq, tq), F32), pltpu.VMEM((2, nq, tq, tq), F32),
                        pltpu.VMEM((2, tq, LANES), F32)],
        args=(z, z, z, dcat, lse, c_col, c_row), vmem=56 << 20,
    )


def fox_bwd_dkdv(z, dcat, stats, c_col, *, name, rider=None):
    B, S, _ = z.shape
    tk, nq = FOX_T, S // FOX_T
    npair = FOX_HEADS // 2
    inv = 1.0 / math.sqrt(FOX_HEAD_DIM)

    def body(q_ref, k_ref, v_ref, do_ref, st_ref, cc_ref, dk_ref, dv_ref, dc_ref, dk_scr, dv_scr, dc_scr):
        p, kt = pl.program_id(1), pl.program_id(2)
        lane = lax.broadcasted_iota(jnp.int32, (tk, LANES), 1)
        masks = [(lane < FOX_HEAD_DIM) == (hh == 0) for hh in range(2)]
        k = k_ref[0]
        v = v_ref[0]
        khs = [jnp.where(masks[hh], k, jnp.zeros_like(k)) for hh in range(2)]
        vhs = [jnp.where(masks[hh], v, jnp.zeros_like(v)) for hh in range(2)]
        ccbs = [jnp.broadcast_to(_lane_pick(cc_ref[0], 2 * p + hh), (tk, tk)) for hh in range(2)]
        dk_scr[...] = jnp.zeros_like(dk_scr)
        dv_scr[...] = jnp.zeros_like(dv_scr)
        dc_scr[...] = jnp.zeros_like(dc_scr)

        def tile(qb, diagonal):
            q0 = pl.multiple_of(qb * tk, tk)
            qs = q_ref[0, pl.ds(q0, tk), :] * inv
            do_b = do_ref[0, pl.ds(q0, tk), :].astype(BF16)
            for hh in range(2):
                st = lax.dot_general(khs[hh], qs, NT, preferred_element_type=F32)
                pr = jnp.exp(st - ccbs[hh] + st_ref[0, 0, pl.ds(hh, 1), pl.ds(q0, tk)])
                if diagonal:
                    pr = jnp.where(_causal(tk, True), pr, 0.0)
                dp = lax.dot_general(vhs[hh], do_b, NT, preferred_element_type=F32)
                ds = pr * (dp - st_ref[0, 0, pl.ds(2 + hh, 1), pl.ds(q0, tk)])
                dv_scr[...] += jnp.dot(pr.astype(BF16), jnp.where(masks[hh], do_b, jnp.zeros_like(do_b)),
                                       preferred_element_type=F32)
                dk_scr[...] += jnp.dot(ds.astype(BF16), jnp.where(masks[hh], qs, jnp.zeros_like(qs)),
                                       preferred_element_type=F32)
                dc_scr[hh] -= _fold_lanes(ds, jnp.add)

        def later(qb, carry):
            tile(qb, False)
            return carry

        tile(kt, True)
        lax.fori_loop(kt + 1, nq, later, 0)
        dk_ref[0] = dk_scr[...].astype(BF16)
        dv_ref[0] = dv_scr[...].astype(BF16)
        dcs = [jnp.sum(dc_scr[hh], axis=-1, keepdims=True) for hh in range(2)]
        dc_ref[0, 0] = jnp.where(lane == 2 * p, dcs[0], jnp.where(lane == 2 * p + 1, dcs[1], 0.0))

    full = lambda col: pl.BlockSpec((1, S, LANES), col)
    tile_spec = lambda col: pl.BlockSpec((1, tk, LANES), col)
    return hosted_call(
        body, rider, name=name, grid=(B, npair, nq),
        in_specs=[full(lambda b, p, t: (b, 0, QKV0 + p)),
                  tile_spec(lambda b, p, t: (b, t, QKV0 + npair + p)),
                  tile_spec(lambda b, p, t: (b, t, QKV0 + 2 * npair + p)),
                  full(lambda b, p, t: (b, 0, npair + p)),
                  pl.BlockSpec((1, 1, 8, S), lambda b, p, t: (b, p, 0, 0)),
                  tile_spec(lambda b, p, t: (b, t, 0))],
        out_specs=[tile_spec(lambda b, p, t: (b, t, p)), tile_spec(lambda b, p, t: (b, t, p)),
                   pl.BlockSpec((1, 1, tk, LANES), lambda b, p, t: (b, p, t, 0))],
        out_shape=[jax.ShapeDtypeStruct((B, S, FOX_W), BF16)] * 2
        + [jax.ShapeDtypeStruct((B, npair, S, LANES), F32)],
        scratch_shapes=[pltpu.VMEM((tk, LANES), F32), pltpu.VMEM((tk, LANES), F32),
                        pltpu.VMEM((2, tk, LANES), F32)],
        args=(z, z, z, dcat, stats, c_col),
    )


def xattn_fwd(qm, kv, *, name, tq=512):
    B, S, D = qm.shape
    M = kv.shape[1]
    tq = min(tq, S)
    inv = 1.0 / math.sqrt(MEM_HEAD_DIM)

    nq = S // tq

    def body(q_ref, kv_ref, o_ref, ot_ref):
        for h in range(MEM_HEADS):
            c0 = h * MEM_HEAD_DIM
            qh = q_ref[0, :, c0:c0 + MEM_HEAD_DIM]
            kh = kv_ref[0, :, c0:c0 + MEM_HEAD_DIM]
            vh = kv_ref[0, :, D + c0:D + c0 + MEM_HEAD_DIM]
            s = lax.dot_general(qh, kh, NT, preferred_element_type=F32) * inv
            e = jnp.exp(s - jnp.max(s, axis=-1, keepdims=True))
            o = jnp.dot(e.astype(BF16), vh, preferred_element_type=F32) / jnp.sum(e, axis=-1, keepdims=True)
            o_ref[0, :, c0:c0 + MEM_HEAD_DIM] = o.astype(BF16)
            ot_ref[c0:c0 + MEM_HEAD_DIM, :] = o.T.astype(BF16)

    return _call(
        body, name=name, grid=(B, nq),
        in_specs=[pl.BlockSpec((1, tq, D), lambda b, i: (b, i, 0)),
                  pl.BlockSpec((1, M, 2 * D), lambda b, i: (b, 0, 0))],
        out_specs=[pl.BlockSpec((1, tq, D), lambda b, i: (b, i, 0)),
                   pl.BlockSpec((D, tq), lambda b, i: (0, b * nq + i))],
        out_shape=[jax.ShapeDtypeStruct((B, S, D), BF16), jax.ShapeDtypeStruct((D, B * S), BF16)],
        compiler_params=_params(("parallel", "parallel")),
    )(qm, kv)


def xattn_bwd(qm, kv, do, *, name, tq=512):
    B, S, D = qm.shape
    M = kv.shape[1]
    tq = min(tq, S)
    inv = 1.0 / math.sqrt(MEM_HEAD_DIM)

    def body(q_ref, kv_ref, do_ref, dq_ref, dkv_ref):
        @pl.when(pl.program_id(1) == 0)
        def _():
            dkv_ref[...] = jnp.zeros_like(dkv_ref)

        for h in range(MEM_HEADS):
            c0 = h * MEM_HEAD_DIM
            qh = q_ref[0, :, c0:c0 + MEM_HEAD_DIM]
            kh = kv_ref[0, :, c0:c0 + MEM_HEAD_DIM]
            vh = kv_ref[0, :, D + c0:D + c0 + MEM_HEAD_DIM]
            doh = do_ref[0, :, c0:c0 + MEM_HEAD_DIM]
            s = lax.dot_general(qh, kh, NT, preferred_element_type=F32) * inv
            e = jnp.exp(s - jnp.max(s, axis=-1, keepdims=True))
            pr = e / jnp.sum(e, axis=-1, keepdims=True)
            dp = lax.dot_general(doh, vh, NT, preferred_element_type=F32)
            ds = pr * (dp - jnp.sum(pr * dp, axis=-1, keepdims=True))
            ds_b = ds.astype(BF16)
            dq_ref[0, :, c0:c0 + MEM_HEAD_DIM] = (jnp.dot(ds_b, kh, preferred_element_type=F32) * inv).astype(BF16)
            dkv_ref[0, :, c0:c0 + MEM_HEAD_DIM] += lax.dot_general(ds_b, qh, TN, preferred_element_type=F32) * inv
            dkv_ref[0, :, D + c0:D + c0 + MEM_HEAD_DIM] += lax.dot_general(
                pr.astype(BF16), doh, TN, preferred_element_type=F32)

    row = pl.BlockSpec((1, tq, D), lambda b, i: (b, i, 0))
    kvs = pl.BlockSpec((1, M, 2 * D), lambda b, i: (b, 0, 0))
    return _call(
        body, name=name, grid=(B, S // tq), in_specs=[row, kvs, row], out_specs=[row, kvs],
        out_shape=[jax.ShapeDtypeStruct((B, S, D), BF16), jax.ShapeDtypeStruct((B, M, 2 * D), F32)],
        compiler_params=_params(("parallel", "arbitrary")),
    )(qm, kv, do)


SWIGLU_TN = 1408


def _chunks(n, w=256):
    return [(c0, min(w, n - c0)) for c0 in range(0, n, w)]


def mm_swiglu_fwd(hf, w_gu, *, name, tm=512):
    T, D = hf.shape
    Fh = w_gu.shape[1] // 2
    tm, tn = min(tm, T), SWIGLU_TN
    nj = Fh // tn
    assert Fh % tn == 0 and T % tm == 0

    def body(a_ref, bg_ref, bu_ref, g_ref, u_ref, o_ref, ot_ref):
        a = a_ref[...]
        for c0, cw in _chunks(tn):
            cols = pl.ds(c0, cw)
            g = jnp.dot(a, bg_ref[:, cols], preferred_element_type=F32)
            u = jnp.dot(a, bu_ref[:, cols], preferred_element_type=F32)
            act = g * _sigmoid(g) * u
            g_ref[:, cols] = g.astype(BF16)
            u_ref[:, cols] = u.astype(BF16)
            o_ref[:, cols] = act.astype(BF16)
            ot_ref[cols, :] = act.T.astype(BF16)

    tile = pl.BlockSpec((tm, tn), lambda i, j: (i, j))
    return _call(
        body, name=name, grid=(T // tm, nj),
        in_specs=[pl.BlockSpec((tm, D), lambda i, j: (i, 0)), pl.BlockSpec((D, tn), lambda i, j: (0, j)),
                  pl.BlockSpec((D, tn), lambda i, j: (0, nj + j))],
        out_specs=[tile, tile, tile, pl.BlockSpec((tn, tm), lambda i, j: (j, i))],
        out_shape=[jax.ShapeDtypeStruct((T, Fh), BF16)] * 3 + [jax.ShapeDtypeStruct((Fh, T), BF16)],
        compiler_params=_params(("parallel", "parallel"), 48 << 20),
    )(hf, w_gu, w_gu)


def mm_swiglu_bwd(dx, w_down, g, u, *, name, tm=512):
    T, D = dx.shape
    Fh = w_down.shape[0]
    tm, tn = min(tm, T), SWIGLU_TN
    assert Fh % tn == 0 and T % tm == 0

    def body(a_ref, b_ref, g_ref, u_ref, dg_ref, du_ref):
        a = a_ref[...].astype(BF16)
        for c0, cw in _chunks(tn):
            cols = pl.ds(c0, cw)
            d = lax.dot_general(a, b_ref[cols, :], NT, preferred_element_type=F32)
            gv = g_ref[:, cols].astype(F32)
            uv = u_ref[:, cols].astype(F32)
            sg = _sigmoid(gv)
            dg_ref[:, cols] = (d * uv * (sg * (1.0 + gv * (1.0 - sg)))).astype(BF16)
            du_ref[:, cols] = (d * gv * sg).astype(BF16)

    tile = pl.BlockSpec((tm, tn), lambda i, j: (i, j))
    return _call(
        body, name=name, grid=(T // tm, Fh // tn),
        in_specs=[pl.BlockSpec((tm, D), lambda i, j: (i, 0)), pl.BlockSpec((tn, D), lambda i, j: (j, 0)), tile, tile],
        out_specs=[tile, tile],
        out_shape=[jax.ShapeDtypeStruct((T, Fh), BF16)] * 2,
        compiler_params=_params(("parallel", "parallel"), 48 << 20),
    )(dx, w_down, g, u)


LATE_MID = ("w_out", "w_mq", "w_mo")
LATE_KV = ("w_mkv",)
LATE_FFN = ("w_gu", "w_down")
LATE = LATE_MID + LATE_KV + LATE_FFN
RS_GROUPS = (("w_gu", "w_down"), ("w_out", "w_mq", "w_mkv", "w_mo"), ("w_in",))


def pair_sums(names, g42, got):
    return {n: pair_sum(g, o, name="rs_pair_sum_" + n) for n, g, o in zip(names, g42, got)}


def local_step(x, mem, target, sp, first_shards, late_shards):
    B, S, D = x.shape
    T = B * S
    M = mem.shape[1]
    row = lambda v: v.reshape(1, -1).astype(F32)
    g_mix, g_x, g_mem, g_ffn, g_final = (row(sp[k]) for k in ("g_mix", "g_x", "g_mem", "g_ffn", "g_final"))
    conv_b, ln_g, ln_b = row(sp["conv_b"]), row(sp["ln_g"]), row(sp["ln_b"])
    b_f = jnp.pad(row(sp["b_f"]), ((0, 0), (0, LANES - FOX_HEADS)))
    n_ug, n_main = 2 * CONV_CH, 2 * CONV_CH + 3 * FOX_W

    x2d = x.reshape(T, D)
    h, h_t, partly = rmsnorm_fwd(x2d, g_mix, name="rms_mix", rider=AllGatherStage1(first_shards))
    w_in8, cw8 = all_gather_stage2(partly, name="ag_first_stage2")
    w_in_full = _full_from_gathered("w_in", w_in8)
    conv_w = cw8.transpose(1, 0, 2).reshape(HALO, -1)
    w_main, w_ug, w_qkv = w_in_full[:, :n_main], w_in_full[:, :n_ug], w_in_full[:, n_ug:n_main]
    w_f = jnp.pad(w_in_full[:, n_main:], ((0, 0), (0, LANES - FOX_HEADS)))
    z = matmul(h, w_main, out_dtype=BF16, tn=n_main, name="mm_in")
    z3 = z.reshape(B, S, n_main)
    n_mid, n_kv = len(LATE_MID), len(LATE_MID) + len(LATE_KV)
    (conv_out, conv_t), partly_mid = conv_branch_fwd(z3, conv_w, conv_b, ln_g, ln_b, name="conv_fwd",
                                                     rider=AllGatherStage1(late_shards[:n_mid]))
    (f_raw, c_col, c_row), partly_kv = fgate_fwd(h.reshape(B, S, D), w_f, b_f, name="fgate_fwd",
                                                 rider=AllGatherStage1(late_shards[n_mid:n_kv]))
    (att, lse, att_t), partly_ffn = fox_fwd(z3, c_col, c_row, name="fox_fwd",
                                            rider=AllGatherStage1(late_shards[n_kv:]))
    gathered = all_gather_stage2(partly_mid + partly_kv + partly_ffn, name="ag_late_stage2")
    wf = {n: _full_from_gathered(n, blk) for n, blk in zip(LATE, gathered)}
    x1 = matmul([conv_out.reshape(T, CONV_CH), att.reshape(T, FOX_W)],
                [wf["w_out"], wf["w_out"]], b_blk=[0, 1], out_dtype=F32, res=x2d, tn=D, name="mm_out")
    hx, hx_t = rmsnorm_fwd(x1, g_x, name="rms_x")
    qm = matmul(hx, wf["w_mq"], out_dtype=BF16, tn=D, name="mm_mq")
    mem2d = mem.reshape(B * M, D)
    mem_n, mem_n_t = rmsnorm_fwd(mem2d, g_mem, name="rms_mem")
    kv = matmul(mem_n, wf["w_mkv"], out_dtype=BF16, tn=2 * D, name="mm_mkv").reshape(B, M, 2 * D)
    o, o_t = xattn_fwd(qm.reshape(B, S, D), kv, name="xattn_fwd")
    o = o.reshape(T, D)
    x2 = matmul(o, wf["w_mo"], out_dtype=F32, res=x1, tn=D, name="mm_mo")
    hf, hf_t = rmsnorm_fwd(x2, g_ffn, name="rms_ffn")
    gate, up, act, act_t = mm_swiglu_fwd(hf, wf["w_gu"], name="mm_gu")
    x3 = matmul(act, wf["w_down"], out_dtype=F32, res=x2, tn=D, name="mm_down")
    dx3, dg_final, loss = final_loss_bwd(x3, g_final, target.reshape(T, D), name="loss_bwd")
    gw = {}
    gw["w_down"] = matmul(act_t, dx3, out_dtype=BF16, tm=1408, tn=512, name="dw_down")
    dgate, dup = mm_swiglu_bwd(dx3, wf["w_down"], gate, up, name="dx_down")
    gw["w_gu"] = [matmul(hf_t, dgate, out_dtype=BF16, tn=SWIGLU_TN, name="dw_gate"),
                  matmul(hf_t, dup, out_dtype=BF16, tn=SWIGLU_TN, name="dw_up")]
    g42 = [_shards_from_full(n, gw[n]) for n in RS_GROUPS[0]]
    dhf, got = matmul([dgate, dup], [wf["w_gu"], wf["w_gu"]], b_blk=[0, 1], tb=True, out_dtype=BF16,
                      tm=256, tn=D, name="dx_gu", rider=SiblingExchange(g42))
    parts = pair_sums(RS_GROUPS[0], g42, got)
    dx2, dg_ffn = rmsnorm_bwd(x2, g_ffn, dhf, dx3, name="rms_ffn_bwd")
    gw["w_mo"] = matmul(o_t, dx2, out_dtype=BF16, tn=D, name="dw_mo")
    do = matmul(dx2, wf["w_mo"], tb=True, out_dtype=BF16, tn=D, name="dx_mo")
    dqm, dkv = xattn_bwd(qm.reshape(B, S, D), kv, do.reshape(B, S, D), name="xattn_bwd")
    dqm = dqm.reshape(T, D)
    dkv = dkv.reshape(B * M, 2 * D)
    gw["w_mq"] = matmul(hx_t, dqm, out_dtype=BF16, tn=D, name="dw_mq")
    dhx = matmul(dqm, wf["w_mq"], tb=True, out_dtype=BF16, tn=D, name="dx_mq")
    gw["w_mkv"] = matmul(mem_n_t, dkv, out_dtype=BF16, tn=D, name="dw_mkv")
    dmem_n = matmul(dkv, wf["w_mkv"], tb=True, out_dtype=BF16, tn=D, name="dx_mkv")
    _, dg_mem = rmsnorm_bwd(mem2d, g_mem, dmem_n, None, name="rms_mem_bwd")
    dx1, dg_x = rmsnorm_bwd(x1, g_x, dhx, dx2, name="rms_x_bwd")
    gw["w_out"] = jnp.concatenate([matmul(conv_t, dx1, out_dtype=BF16, tn=D, name="dw_out_conv"),
                                   matmul(att_t, dx1, out_dtype=BF16, tn=D, name="dw_out_att")], axis=0)
    g42 = [_shards_from_full(n, gw[n]) for n in RS_GROUPS[1]]
    dcat, got = matmul(dx1, wf["w_out"], tb=True, out_dtype=BF16, tn=D, name="dx_out", rider=SiblingExchange(g42))
    dcat = dcat.reshape(B, S, D)
    parts.update(pair_sums(RS_GROUPS[1], g42, got))
    dy, dconv_w, dvec = conv_branch_bwd_a(z3, dcat, conv_w, conv_b, ln_g, ln_b, name="conv_bwd_a")
    dug = conv_branch_bwd_b(z3, dy, conv_w, name="conv_bwd_b")
    gots = {}
    (dq, stats), got = fox_bwd_dq(z3, dcat, lse, c_col, c_row, name="fox_bwd_dq",
                                  rider=ChipExchange([parts[n] for n in RS_GROUPS[0]]))
    gots.update(zip(RS_GROUPS[0], got))
    (dk, dv, dc), got = fox_bwd_dkdv(z3, dcat, stats, c_col, name="fox_bwd_dkdv",
                                     rider=ChipExchange([parts[n] for n in RS_GROUPS[1]]))
    gots.update(zip(RS_GROUPS[1], got))
    df, db_f = fgate_bwd(dc, f_raw, b_f, name="fgate_bwd")
    dug2 = dug.reshape(T, n_ug)
    dqkv = jnp.concatenate([dq, dk, dv], axis=-1).reshape(T, 3 * FOX_W)
    df2 = df.reshape(T, LANES)
    dw_in = [matmul(h_t, dug2, out_dtype=BF16, tn=n_ug, name="dw_in_ug"),
             matmul(h_t, dqkv, out_dtype=BF16, tn=3 * FOX_W, name="dw_in_qkv"),
             matmul(h_t, df2, out_dtype=BF16, name="dw_f")[:, :FOX_HEADS]]
    g42 = [_shards_from_full("w_in", dw_in)]
    parts.update(pair_sums(RS_GROUPS[2], g42, run_rider(SiblingExchange(g42), name="rs_sibling_in")))
    dh, (gots["w_in"],) = matmul([dug2, dqkv, df2], [w_ug, w_qkv, w_f], tb=True, out_dtype=F32, tn=D,
                                 name="dx_in", rider=ChipExchange([parts["w_in"]]))
    dx, dg_mix = rmsnorm_bwd(x2d, g_mix, dh, dx1, name="rms_mix_bwd", out_dtype=F32)
    gs = dict(g_mix=dg_mix, b_f=db_f[:, :FOX_HEADS], conv_w=dconv_w[:CONV_K], conv_b=dvec[0:1],
              ln_g=dvec[1:2], ln_b=dvec[2:3], g_x=dg_x, g_mem=dg_mem, g_ffn=dg_ffn, g_final=dg_final)
    return loss, dx.reshape(B, S, D), gs, {n: (parts[n], gots[n]) for n in BIG}


def _me():
    return lax.axis_index("x"), lax.axis_index("y"), lax.axis_index("c")


def _any_specs(n):
    return [pl.BlockSpec(memory_space=pl.ANY)] * n


def all_gather(xs, *, name):
    n = len(xs)

    def body(*refs):
        x_refs, out_refs = refs[:n], refs[n:2 * n]
        send_sems, recv_sems, local_sems = refs[2 * n:]
        x, y, c = _me()
        me, sibling = (x, y, c), (x, y, 1 - c)
        chips = [(1 - x, y), (x, 1 - y), (1 - x, 1 - y)]

        def slot(a, px, py, pc):
            return out_refs[a].at[4 * px + 2 * py + pc]

        def copy(a, k, block, to, own=False):
            return pltpu.make_async_remote_copy(
                src_ref=x_refs[a] if own else slot(a, *block), dst_ref=slot(a, *block),
                send_sem=send_sems.at[k, a], recv_sem=recv_sems.at[k, a], device_id=to, device_id_type=MESH)

        mine = [pltpu.make_async_copy(x_refs[a], slot(a, *me), local_sems.at[a]) for a in range(n)]
        first = [copy(a, 0, me, sibling, own=True) for a in range(n)]
        first += [copy(a, 1 + j, me, (*chip, c), own=True) for j, chip in enumerate(chips) for a in range(n)]
        for cp in mine + first:
            cp.start()
        passed = []
        for j, chip in enumerate(chips):
            for a in range(n):
                copy(a, 1 + j, (*chip, c), me).wait_recv()
                passed.append(copy(a, 4 + j, (*chip, c), sibling))
                passed[-1].start()
        for a in range(n):
            copy(a, 0, sibling, me).wait_recv()
            for j, chip in enumerate(chips):
                copy(a, 4 + j, (*chip, 1 - c), me).wait_recv()
        for cp in first + passed:
            cp.wait_send()
        for cp in mine:
            cp.wait()

    return _call(
        body, name=name, in_specs=_any_specs(n), out_specs=_any_specs(n),
        out_shape=[jax.ShapeDtypeStruct((N_DEV,) + v.shape, v.dtype) for v in xs],
        scratch_shapes=[pltpu.SemaphoreType.DMA((7, n)), pltpu.SemaphoreType.DMA((7, n)),
                        pltpu.SemaphoreType.DMA((n,))],
    )(*xs)


SIBLING_BARRIER = 1
CHIPS_BARRIER = 2
GATHER_BARRIER = 3


class SiblingExchange:
    collective_id = SIBLING_BARRIER

    def __init__(self, gs):
        n = len(gs)
        self.n, self.inputs = n, list(gs)
        self.out_shape = [jax.ShapeDtypeStruct((4,) + g.shape[2:], g.dtype) for g in gs]
        self.scratch = [pltpu.SemaphoreType.DMA((n,)), pltpu.SemaphoreType.DMA((n,))]

    @staticmethod
    def barrier_peers():
        x, y, c = _me()
        return [(x, y, 1 - c)]

    def _copies(self, g_refs, out_refs, sems):
        send_sems, recv_sems = sems
        x, y, c = _me()
        return [pltpu.make_async_remote_copy(
            src_ref=g_refs[a].at[:, 1 - c], dst_ref=out_refs[a], send_sem=send_sems.at[a],
            recv_sem=recv_sems.at[a], device_id=(x, y, 1 - c), device_id_type=MESH) for a in range(self.n)]

    def start(self, in_refs, out_refs, sems):
        for cp in self._copies(in_refs, out_refs, sems):
            cp.start()

    def finish(self, in_refs, out_refs, sems):
        for cp in self._copies(in_refs, out_refs, sems):
            cp.wait()


def run_rider(rider, *, name):
    return hosted_call(None, rider, name=name, grid=(), in_specs=[], out_specs=[], out_shape=[],
                       scratch_shapes=[], args=[])[1]


class ChipExchange:
    collective_id = CHIPS_BARRIER

    @staticmethod
    def barrier_peers():
        x, y, c = _me()
        return [(1 - x, y, c), (x, 1 - y, c), (1 - x, 1 - y, c)]

    def __init__(self, ps):
        n = len(ps)
        self.n, self.inputs = n, list(ps)
        self.out_shape = [jax.ShapeDtypeStruct(p.shape, p.dtype) for p in ps]
        self.scratch = [pltpu.SemaphoreType.DMA((3, n)), pltpu.SemaphoreType.DMA((3, n))]

    def _copies(self, p_refs, out_refs, sems, outgoing):
        send_sems, recv_sems = sems
        x, y, c = _me()
        my_chip = 2 * x + y
        cps = []
        for k in range(3):
            px, py = x ^ ((k + 1) >> 1), y ^ ((k + 1) & 1)
            src, dst = (2 * px + py, my_chip) if outgoing else (my_chip, 2 * px + py)
            for a in range(self.n):
                cps.append(pltpu.make_async_remote_copy(
                    src_ref=p_refs[a].at[src], dst_ref=out_refs[a].at[dst], send_sem=send_sems.at[k, a],
                    recv_sem=recv_sems.at[k, a], device_id=(px, py, c), device_id_type=MESH))
        return cps

    def start(self, in_refs, out_refs, sems):
        for cp in self._copies(in_refs, out_refs, sems, True):
            cp.start()

    def finish(self, in_refs, out_refs, sems):
        for cp in self._copies(in_refs, out_refs, sems, False):
            cp.wait_recv()
        for cp in self._copies(in_refs, out_refs, sems, True):
            cp.wait_send()


class AllGatherStage1:
    collective_id = GATHER_BARRIER

    @staticmethod
    def barrier_peers():
        x, y, c = _me()
        return [(x, y, 1 - c), (1 - x, y, c), (x, 1 - y, c), (1 - x, 1 - y, c)]

    def __init__(self, xs):
        n = len(xs)
        self.n, self.inputs = n, list(xs)
        self.out_shape = [jax.ShapeDtypeStruct((N_DEV,) + v.shape, v.dtype) for v in xs]
        self.scratch = [pltpu.SemaphoreType.DMA((4, n)), pltpu.SemaphoreType.DMA((4, n)),
                        pltpu.SemaphoreType.DMA((n,))]

    def _copies(self, x_refs, out_refs, sems, kind):
        send_sems, recv_sems, local_sems = sems
        x, y, c = _me()
        slot = lambda a, d: out_refs[a].at[4 * d[0] + 2 * d[1] + d[2]]
        if kind == "local":
            return [pltpu.make_async_copy(x_refs[a], slot(a, (x, y, c)), local_sems.at[a]) for a in range(self.n)]
        cps = []
        for k, peer in enumerate([(x, y, 1 - c), (1 - x, y, c), (x, 1 - y, c), (1 - x, 1 - y, c)]):
            for a in range(self.n):
                cps.append(pltpu.make_async_remote_copy(
                    src_ref=x_refs[a], dst_ref=slot(a, (x, y, c) if kind == "out" else peer),
                    send_sem=send_sems.at[k, a], recv_sem=recv_sems.at[k, a], device_id=peer, device_id_type=MESH))
        return cps

    def start(self, in_refs, out_refs, sems):
        for cp in self._copies(in_refs, out_refs, sems, "local") + self._copies(in_refs, out_refs, sems, "out"):
            cp.start()

    def finish(self, in_refs, out_refs, sems):
        for cp in self._copies(in_refs, out_refs, sems, "in"):
            cp.wait_recv()
        for cp in self._copies(in_refs, out_refs, sems, "out"):
            cp.wait_send()
        for cp in self._copies(in_refs, out_refs, sems, "local"):
            cp.wait()


def all_gather_stage2(outs, *, name):
    n = len(outs)

    def body(*refs):
        out_refs = refs[n:2 * n]
        send_sems, recv_sems = refs[2 * n:]
        x, y, c = _me()
        _peer_barrier([(x, y, 1 - c)])
        sends, recvs = [], []
        for k, (px, py) in enumerate([(1 - x, y), (x, 1 - y), (1 - x, 1 - y)]):
            for a in range(n):
                mk = lambda pc: pltpu.make_async_remote_copy(
                    src_ref=out_refs[a].at[4 * px + 2 * py + c], dst_ref=out_refs[a].at[4 * px + 2 * py + pc],
                    send_sem=send_sems.at[k, a], recv_sem=recv_sems.at[k, a], device_id=(x, y, 1 - c),
                    device_id_type=MESH)
                sends.append(mk(c))
                recvs.append(mk(1 - c))
        for cp in sends:
            cp.start()
        for cp in recvs:
            cp.wait_recv()
        for cp in sends:
            cp.wait_send()

    return _call(
        body, name=name, in_specs=_any_specs(n), out_specs=_any_specs(n),
        out_shape=[jax.ShapeDtypeStruct(o.shape, o.dtype) for o in outs],
        input_output_aliases={a: a for a in range(n)},
        scratch_shapes=[pltpu.SemaphoreType.DMA((3, n)), pltpu.SemaphoreType.DMA((3, n))],
        compiler_params=_params(collective_id=SIBLING_BARRIER),
    )(*outs)


def _peer_barrier(peers):
    barrier = pltpu.get_barrier_semaphore()
    for peer in peers:
        pl.semaphore_signal(barrier, inc=1, device_id=peer, device_id_type=MESH)
    pl.semaphore_wait(barrier, len(peers))


def hosted_call(body, rider, *, name, grid, in_specs, out_specs, out_shape, scratch_shapes, args, vmem=None):
    n_in, n_out, n_scr = len(in_specs), len(out_specs), len(scratch_shapes)
    r_in, r_out = (len(rider.inputs), len(rider.out_shape)) if rider is not None else (0, 0)
    own_barrier = getattr(rider, "collective_id", None) is not None

    def wrapped(*refs):
        ins, refs = refs[:n_in], refs[n_in:]
        rins, refs = refs[:r_in], refs[r_in:]
        outs, refs = refs[:n_out], refs[n_out:]
        routs, refs = refs[:r_out], refs[r_out:]
        scr, rscr = refs[:n_scr], refs[n_scr:]
        ids = [pl.program_id(d) for d in range(len(grid))]
        first = functools.reduce(jnp.logical_and, [i == 0 for i in ids], True)
        last = functools.reduce(jnp.logical_and, [i == g - 1 for i, g in zip(ids, grid)], True)

        def begin():
            if own_barrier:
                _peer_barrier(rider.barrier_peers())
            rider.start(rins, routs, rscr)

        if rider is not None and grid:
            pl.when(first)(begin)
        elif rider is not None:
            begin()
        if body is not None:
            body(*ins, *outs, *scr)
        if rider is not None and grid:
            pl.when(last)(lambda: rider.finish(rins, routs, rscr))
        elif rider is not None:
            rider.finish(rins, routs, rscr)

    kw = dict(grid=grid) if grid else {}
    if grid or vmem is not None or own_barrier:
        kw["compiler_params"] = _params(("arbitrary",) * len(grid) if grid else None, vmem,
                                        rider.collective_id if own_barrier else None)
    res = _call(
        wrapped, name=name, in_specs=list(in_specs) + _any_specs(r_in), out_specs=list(out_specs) + _any_specs(r_out),
        out_shape=list(out_shape) + (rider.out_shape if rider is not None else []),
        scratch_shapes=list(scratch_shapes) + (rider.scratch if rider is not None else []), **kw,
    )(*args, *(rider.inputs if rider is not None else []))
    return list(res[:n_out]), list(res[n_out:])


def _pick_rows(r, target=256):
    best = None
    for d in range(16, min(r, target) + 1, 16):
        if r % d == 0:
            best = d
    return r if best is None else best


def pair_sum(g, got, *, name):
    _, _, R, C = g.shape
    tr = _pick_rows(R)

    def body(g_ref, got_ref, o_ref):
        mine = jnp.where(lax.axis_index("c") == 0, g_ref[:, 0], g_ref[:, 1])
        o_ref[...] = (mine.astype(F32) + got_ref[...].astype(F32)).astype(o_ref.dtype)

    return _call(
        body, name=name, grid=(R // tr,),
        in_specs=[pl.BlockSpec((4, 2, tr, C), lambda i: (0, 0, i, 0)), pl.BlockSpec((4, tr, C), lambda i: (0, i, 0))],
        out_specs=pl.BlockSpec((4, tr, C), lambda i: (0, i, 0)),
        out_shape=jax.ShapeDtypeStruct((4, R, C), g.dtype),
        compiler_params=_params(("parallel",)),
    )(g, got)


def chip_sum_adamw(p, got, w, m, v, *, name):
    _, R, C = p.shape
    assert w.shape == (1, R, C), (name, w.shape, p.shape)
    tr = _pick_rows(R)

    def body(p_ref, got_ref, w_ref, m_ref, v_ref, g_ref, d_ref, mo_ref, vo_ref):
        my_chip = 2 * lax.axis_index("x") + lax.axis_index("y")
        g = jnp.zeros((tr, C), F32)
        for j in range(4):
            g = g + jnp.where(my_chip == j, p_ref[j], got_ref[j]).astype(F32)
        g_ref[0] = g
        d_ref[0], mo_ref[0], vo_ref[0] = _adamw_math(w_ref[0], g, m_ref[0], v_ref[0])

    part = pl.BlockSpec((4, tr, C), lambda i: (0, i, 0))
    spec = pl.BlockSpec((1, tr, C), lambda i: (0, i, 0))
    return _call(
        body, name=name, grid=(R // tr,), in_specs=[part, part, spec, spec, spec], out_specs=[spec] * 4,
        out_shape=[jax.ShapeDtypeStruct((1, R, C), F32)] * 4,
        compiler_params=_params(("parallel",)),
    )(p, got, w, m, v)


def rows_sum(g8, *, name):
    _, R, C = g8.shape

    def body(g_ref, o_ref):
        acc = g_ref[0]
        for j in range(1, N_DEV):
            acc = acc + g_ref[j]
        o_ref[...] = acc

    return _call(body, name=name, out_shape=jax.ShapeDtypeStruct((R, C), F32))(g8)


def _adamw_math(w, g, m, v):
    m = ADAM_B1 * m + (1.0 - ADAM_B1) * g
    v = ADAM_B2 * v + (1.0 - ADAM_B2) * (g * g)
    m_hat = m / (1.0 - ADAM_B1 ** ADAM_STEP)
    v_hat = v / (1.0 - ADAM_B2 ** ADAM_STEP)
    delta = -ADAM_LR * (m_hat / (jnp.sqrt(v_hat) + ADAM_EPS) + ADAM_WD * w)
    return delta, m, v


def adamw_small(wgmv, *, name):
    n = len(wgmv)

    def body(*refs):
        ins, outs = refs[:4 * n], refs[4 * n:]
        for a in range(n):
            w_ref, g_ref, m_ref, v_ref = ins[4 * a:4 * a + 4]
            d, mn, vn = _adamw_math(w_ref[...], g_ref[...], m_ref[...], v_ref[...])
            outs[3 * a][...] = d
            outs[3 * a + 1][...] = mn
            outs[3 * a + 2][...] = vn

    flat = [t for tup in wgmv for t in tup]
    res = _call(
        body, name=name,
        out_shape=[jax.ShapeDtypeStruct(tup[0].shape, F32) for tup in wgmv for _ in range(3)],
    )(*flat)
    return [tuple(res[3 * a:3 * a + 3]) for a in range(n)]


BIG = ("w_in", "w_out", "w_mq", "w_mkv", "w_mo", "w_gu", "w_down")
COL_SHARDED = ("w_in", "w_mkv", "w_gu")
SMALL = ("g_mix", "b_f", "conv_w", "conv_b", "ln_g", "ln_b", "g_x", "g_mem", "g_ffn", "g_final")


def _full_from_gathered(n, blk):
    _, rr, cc = blk.shape
    if n in COL_SHARDED:
        return jnp.concatenate([blk[k] for k in range(N_DEV)], axis=1)
    return blk.reshape(N_DEV * rr, cc)


def _shards_from_full(n, g):
    pieces = g if isinstance(g, list) else [g]
    rr, cc = pieces[0].shape[0], sum(p.shape[1] for p in pieces)
    if n in COL_SHARDED:
        w = cc // N_DEV
        return jnp.stack([_columns(pieces, k * w, w) for k in range(N_DEV)]).reshape(4, 2, rr, w)
    return pieces[0].reshape(4, 2, rr // N_DEV, cc)


def _columns(pieces, start, width):
    out, c0 = [], 0
    for p in pieces:
        lo, hi = max(start, c0), min(start + width, c0 + p.shape[1])
        if lo < hi:
            out.append(p[:, lo - c0:hi - c0])
        c0 += p.shape[1]
    return out[0] if len(out) == 1 else jnp.concatenate(out, axis=1)


def _small_layout():
    sizes = dict(g_mix=1024, b_f=8, conv_w=CONV_K * CONV_CH, conv_b=512, ln_g=512, ln_b=512, g_x=1024,
                 g_mem=1024, g_ffn=1024, g_final=1024, loss=1)
    lay, r0 = {}, 0
    for n, sz in sizes.items():
        r = -(-sz // LANES)
        lay[n] = (r0, r, sz)
        r0 += r
    return lay, -(-r0 // 8) * 8


def kernel(x, mem, g_mix, w_in, b_f, conv_w, conv_b, ln_g, ln_b, w_out, g_x, g_mem, w_mq, w_mkv, w_mo, g_ffn, w_gu, w_down, g_final, loss_target, m_g_mix, m_w_in, m_b_f, m_conv_w, m_conv_b, m_ln_g, m_ln_b, m_w_out, m_g_x, m_g_mem, m_w_mq, m_w_mkv, m_w_mo, m_g_ffn, m_w_gu, m_w_down, m_g_final, v_g_mix, v_w_in, v_b_f, v_conv_w, v_conv_b, v_ln_g, v_ln_b, v_w_out, v_g_x, v_g_mem, v_w_mq, v_w_mkv, v_w_mo, v_g_ffn, v_w_gu, v_w_down, v_g_final):
    names = ["g_mix", "w_in", "b_f", "conv_w", "conv_b", "ln_g", "ln_b", "w_out", "g_x", "g_mem", "w_mq",
             "w_mkv", "w_mo", "g_ffn", "w_gu", "w_down", "g_final"]
    W = dict(zip(names, [g_mix, w_in, b_f, conv_w, conv_b, ln_g, ln_b, w_out, g_x, g_mem, w_mq, w_mkv, w_mo,
                         g_ffn, w_gu, w_down, g_final]))
    Mo = dict(zip(names, [m_g_mix, m_w_in, m_b_f, m_conv_w, m_conv_b, m_ln_g, m_ln_b, m_w_out, m_g_x, m_g_mem,
                          m_w_mq, m_w_mkv, m_w_mo, m_g_ffn, m_w_gu, m_w_down, m_g_final]))
    Vo = dict(zip(names, [v_g_mix, v_w_in, v_b_f, v_conv_w, v_conv_b, v_ln_g, v_ln_b, v_w_out, v_g_x, v_g_mem,
                          v_w_mq, v_w_mkv, v_w_mo, v_g_ffn, v_w_gu, v_w_down, v_g_final]))
    dev = 4 * lax.axis_index("x") + 2 * lax.axis_index("y") + lax.axis_index("c")

    two = lambda a: a.reshape(-1, a.shape[-1])
    cw_shard = jnp.pad(two(conv_w), ((0, HALO - CONV_K), (0, 0)))
    sp = dict(g_mix=g_mix, b_f=b_f, conv_b=conv_b, ln_g=ln_g, ln_b=ln_b, g_x=g_x, g_mem=g_mem,
              g_ffn=g_ffn, g_final=g_final)
    loss_blk, grad_x, gs, reduced = local_step(x, mem, loss_target, sp, [two(w_in).astype(BF16), cw_shard],
                                               [two(W[n]).astype(BF16) for n in LATE])

    lay, rs = _small_layout()
    small = {**{n: gs[n] for n in SMALL}, "loss": loss_blk[:, :1]}
    parts = []
    for n, (r0, r, sz) in lay.items():
        flat = small[n].reshape(-1).astype(F32)
        parts.append(jnp.pad(flat, (0, r * LANES - sz)).reshape(r, LANES))
    spack = jnp.concatenate(parts, axis=0)
    spack = jnp.pad(spack, ((0, rs - spack.shape[0]), (0, 0)))
    ssum = rows_sum(all_gather([spack], name="ag_small")[0], name="small_sum")
    gsmall = {n: ssum[r0:r0 + r].reshape(-1)[:sz] for n, (r0, r, sz) in lay.items()}
    loss = gsmall["loss"].reshape(())

    grads, delta, new_m, new_v = {}, {}, {}, {}
    for n in BIG:
        p, o = reduced[n]
        grads[n], delta[n], new_m[n], new_v[n] = chip_sum_adamw(p, o, W[n], Mo[n], Vo[n], name="adamw_" + n)
    for n in SMALL:
        if n == "conv_w":
            full = gsmall[n].reshape(CONV_K, CONV_CH)
            ncol = conv_w.shape[-1]
            grads[n] = lax.dynamic_slice(full, (0, dev * ncol), (CONV_K, ncol)).reshape(conv_w.shape)
        else:
            grads[n] = gsmall[n].reshape(W[n].shape)
    upd = adamw_small([(two(W[n]), two(grads[n]), two(Mo[n]), two(Vo[n])) for n in SMALL], name="adamw_small")
    for n, (d, mn, vn) in zip(SMALL, upd):
        shp = W[n].shape
        delta[n], new_m[n], new_v[n] = d.reshape(shp), mn.reshape(shp), vn.reshape(shp)
    return (loss, grad_x, *[grads[n] for n in names], *[delta[n] for n in names],
            *[new_m[n] for n in names], *[new_v[n] for n in names])
```

```python
import functools
import math

import jax
import jax.numpy as jnp
from jax import lax
from jax.experimental import pallas as pl
from jax.experimental.pallas import tpu as pltpu

F32 = jnp.float32
BF16 = jnp.bfloat16
EPS = 1e-6
N_DEV = 8
CONV_CH = 512
CONV_K = 31
FOX_HEADS = 8
FOX_HEAD_DIM = 64
FOX_W = 512
MEM_HEADS = 4
MEM_HEAD_DIM = 256
HALO = 32
LANES = 128
ADAM_LR, ADAM_B1, ADAM_B2, ADAM_EPS, ADAM_WD, ADAM_STEP = 0.001, 0.9, 0.999, 1e-08, 0.01, 10
NEG = -1e30
VMEM_CAP = 60 * 1024 * 1024
MESH = pl.DeviceIdType.MESH


def _call(body, **kw):
    kw["out_shape"] = jax.tree.map(lambda s: pltpu.HBM(s.shape, s.dtype), kw["out_shape"])
    call = pl.pallas_call(body, **kw)
    return lambda *args: call(*[pltpu.with_memory_space_constraint(a, pltpu.HBM) for a in args])


def _params(sem=None, vmem=None, collective_id=None):
    kw = {} if collective_id is None else {"collective_id": collective_id}
    if sem is not None:
        kw["dimension_semantics"] = sem
    if vmem is not None:
        kw["vmem_limit_bytes"] = int(min(VMEM_CAP, vmem))
    return pltpu.CompilerParams(**kw)


def _nbytes(shape, dtype):
    return math.prod(shape) * jnp.dtype(dtype).itemsize


def _pick(n, target):
    best = None
    for d in range(LANES, min(n, target) + 1, LANES):
        if n % d == 0:
            best = d
    return n if best is None else best


def matmul(a, b, *, tb=False, out_dtype, res=None, tm=512, tn=512, name, rider=None, b_blk=None):
    a_list = list(a) if isinstance(a, (list, tuple)) else [a]
    b_list = list(b) if isinstance(b, (list, tuple)) else [b]
    n = len(a_list)
    assert len(b_list) == n
    M = a_list[0].shape[0]
    N = b_list[0].shape[0] if tb else b_list[0].shape[1]
    tm, tn = _pick(M, tm), _pick(N, tn)
    assert M % tm == 0 and N % tn == 0, (name, M, N, tm, tn)
    dn = (((1,), (1 if tb else 0,)), ((), ()))

    def body(*refs):
        acc = None
        for a_ref, b_ref in zip(refs[:n], refs[n:2 * n]):
            p = lax.dot_general(a_ref[...].astype(BF16), b_ref[...].astype(BF16), dn, preferred_element_type=F32)
            acc = p if acc is None else acc + p
        if res is not None:
            acc = acc + refs[2 * n][...].astype(F32)
        refs[-1][...] = acc.astype(out_dtype)

    o_spec = pl.BlockSpec((tm, tn), lambda i, j: (i, j))
    in_specs, est = [], 2 * _nbytes((tm, tn), out_dtype) + 2 * _nbytes((tm, tn), F32)
    for av in a_list:
        assert av.shape[0] == M
        in_specs.append(pl.BlockSpec((tm, av.shape[1]), lambda i, j: (i, 0)))
        est += (2 * jnp.dtype(av.dtype).itemsize + (av.dtype != BF16) * 2) * tm * av.shape[1]
    for idx, (av, bv) in enumerate(zip(a_list, b_list)):
        K = av.shape[1]
        kb = 0 if b_blk is None else b_blk[idx]
        assert bv.shape[0 if tb else 1] == N and bv.shape[1 if tb else 0] >= (kb + 1) * K, (name, av.shape, bv.shape)
        assert b_blk is not None or bv.shape[1 if tb else 0] == K, (name, av.shape, bv.shape)
        in_specs.append(pl.BlockSpec((tn, K), lambda i, j, kb=kb: (j, kb)) if tb
                        else pl.BlockSpec((K, tn), lambda i, j, kb=kb: (kb, j)))
        est += (2 * jnp.dtype(bv.dtype).itemsize + (bv.dtype != BF16) * 2) * tn * K
    args = a_list + b_list
    if res is not None:
        in_specs.append(o_spec)
        args.append(res)
        est += 2 * _nbytes((tm, tn), res.dtype)
    (out,), rode = hosted_call(
        body, rider, name=name, grid=(M // tm, N // tn), in_specs=in_specs, out_specs=[o_spec],
        out_shape=[jax.ShapeDtypeStruct((M, N), out_dtype)], scratch_shapes=[],
        args=args, vmem=est + (8 << 20),
    )
    return out if rider is None else (out, rode)


def _rms_scale(x):
    return lax.rsqrt(jnp.mean(x * x, axis=-1, keepdims=True) + EPS)


def rmsnorm_fwd(x, g, *, name, tm=512, rider=None):
    T, D = x.shape
    tm = min(tm, T)

    def body(x_ref, g_ref, o_ref, ot_ref):
        xv = x_ref[...]
        h = xv * _rms_scale(xv) * g_ref[...]
        o_ref[...] = h.astype(BF16)
        ot_ref[...] = h.T.astype(BF16)

    (h, h_t), rode = hosted_call(
        body, rider, name=name, grid=(T // tm,),
        in_specs=[pl.BlockSpec((tm, D), lambda i: (i, 0)), pl.BlockSpec((1, D), lambda i: (0, 0))],
        out_specs=[pl.BlockSpec((tm, D), lambda i: (i, 0)), pl.BlockSpec((D, tm), lambda i: (0, i))],
        out_shape=[jax.ShapeDtypeStruct((T, D), BF16), jax.ShapeDtypeStruct((D, T), BF16)],
        scratch_shapes=[], args=(x, g),
    )
    return (h, h_t) if rider is None else (h, h_t, rode)


def _rms_bwd_math(xv, gv, dh):
    r = _rms_scale(xv)
    xh = xv * r
    dg = jnp.sum(dh * xh, axis=0, keepdims=True)
    dxh = dh * gv
    dx = r * (dxh - xh * jnp.mean(dxh * xh, axis=-1, keepdims=True))
    return dx, dg


def rmsnorm_bwd(x, g, dh, dres, *, name, tm=256, out_dtype=BF16):
    T, D = x.shape
    tm = min(tm, T)

    def body(*refs):
        if dres is not None:
            x_ref, g_ref, dh_ref, dr_ref, dx_ref, dg_ref = refs
        else:
            x_ref, g_ref, dh_ref, dx_ref, dg_ref = refs
        dx, dg = _rms_bwd_math(x_ref[...], g_ref[...], dh_ref[...].astype(F32))
        if dres is not None:
            dx = dx + dr_ref[...].astype(F32)
        dx_ref[...] = dx.astype(out_dtype)

        @pl.when(pl.program_id(0) == 0)
        def _():
            dg_ref[...] = jnp.zeros_like(dg_ref)

        dg_ref[...] += dg

    row = pl.BlockSpec((tm, D), lambda i: (i, 0))
    vec = pl.BlockSpec((1, D), lambda i: (0, 0))
    ins, args = [row, vec, row], [x, g, dh]
    if dres is not None:
        ins.append(row)
        args.append(dres)
    return _call(
        body, name=name, grid=(T // tm,), in_specs=ins, out_specs=[row, vec],
        out_shape=[jax.ShapeDtypeStruct((T, D), out_dtype), jax.ShapeDtypeStruct((1, D), F32)],
        compiler_params=_params(("arbitrary",)),
    )(*args)


def final_loss_bwd(x, g, target, *, name, tm=256):
    T, D = x.shape
    tm = min(tm, T)

    def body(x_ref, g_ref, t_ref, dx_ref, dg_ref, l_ref):
        xv, gv = x_ref[...], g_ref[...]
        e = xv * _rms_scale(xv) * gv - t_ref[...]
        part = 0.5 * jnp.sum(jnp.mean(e * e, axis=-1, keepdims=True), axis=0, keepdims=True)
        dx, dg = _rms_bwd_math(xv, gv, e * (1.0 / D))
        dx_ref[...] = dx.astype(BF16)

        @pl.when(pl.program_id(0) == 0)
        def _():
            dg_ref[...] = jnp.zeros_like(dg_ref)
            l_ref[...] = jnp.zeros_like(l_ref)

        dg_ref[...] += dg
        l_ref[...] += jnp.broadcast_to(part, l_ref.shape)

    row = pl.BlockSpec((tm, D), lambda i: (i, 0))
    vec = pl.BlockSpec((1, D), lambda i: (0, 0))
    return _call(
        body, name=name, grid=(T // tm,), in_specs=[row, vec, row],
        out_specs=[row, vec, pl.BlockSpec((1, LANES), lambda i: (0, 0))],
        out_shape=[jax.ShapeDtypeStruct((T, D), BF16), jax.ShapeDtypeStruct((1, D), F32),
                   jax.ShapeDtypeStruct((1, LANES), F32)],
        compiler_params=_params(("arbitrary",)),
    )(x, g, target)


def _sigmoid(v):
    return 1.0 / (1.0 + jnp.exp(-v))


def _glu(blk):
    u = blk[:, :CONV_CH].astype(F32)
    gt = blk[:, CONV_CH:].astype(F32)
    return u * _sigmoid(gt)


def _fill_causal_ext(ext, cur_ref, halo_ref, s, ts):
    ext[pl.ds(HALO, ts), :] = _glu(cur_ref[0])
    hal = _glu(halo_ref[0])
    ext[pl.ds(0, HALO), :] = jnp.where(s > 0, hal, 0.0)


SUBLANES = 8


def _make_shifted(ext, sh):
    n = ext.shape[0]
    full = ext[...]
    for r in range(1, SUBLANES):
        sh[r - 1] = pltpu.roll(full, n - r, 0)


def _tap(ext, sh, off, ts):
    r = off % SUBLANES
    return ext[pl.ds(off, ts), :] if r == 0 else sh[r - 1, pl.ds(off - r, ts), :]


def _causal_conv(ext, sh, w_ref, ts):
    acc = jnp.zeros((ts, CONV_CH), F32)
    for j in range(CONV_K):
        acc = acc + _tap(ext, sh, HALO - (CONV_K - 1) + j, ts) * w_ref[pl.ds(j, 1), :]
    return acc


def _ln_stats(y):
    mu = jnp.mean(y, axis=-1, keepdims=True)
    yc = y - mu
    rstd = lax.rsqrt(jnp.mean(yc * yc, axis=-1, keepdims=True) + EPS)
    return yc * rstd, rstd


def _conv_specs(ts, S):
    nh = ts // HALO
    cur = pl.BlockSpec((1, ts, 2 * CONV_CH), lambda b, s: (b, s, 0))
    halo = pl.BlockSpec((1, HALO, 2 * CONV_CH), lambda b, s: (b, jnp.maximum(s * nh - 1, 0), 0))
    w = pl.BlockSpec((HALO, CONV_CH), lambda b, s: (0, 0))
    vec = pl.BlockSpec((1, CONV_CH), lambda b, s: (0, 0))
    return cur, halo, w, vec


def conv_branch_fwd(ug, conv_w, conv_b, ln_g, ln_b, *, name, ts=256, rider=None):
    B, S, _ = ug.shape
    ts = min(ts, S)
    ns = S // ts
    cur, halo, w, vec = _conv_specs(ts, S)

    def body(cur_ref, halo_ref, w_ref, cb_ref, lg_ref, lb_ref, o_ref, ot_ref, ext, sh):
        _fill_causal_ext(ext, cur_ref, halo_ref, pl.program_id(1), ts)
        _make_shifted(ext, sh)
        y = _causal_conv(ext, sh, w_ref, ts) + cb_ref[...]
        yh, _ = _ln_stats(y)
        ln = yh * lg_ref[...] + lb_ref[...]
        out = ln * _sigmoid(ln)
        o_ref[0] = out.astype(BF16)
        ot_ref[...] = out.T.astype(BF16)

    return hosted_call(
        body, rider, name=name, grid=(B, ns), in_specs=[cur, halo, w, vec, vec, vec],
        out_specs=[pl.BlockSpec((1, ts, CONV_CH), lambda b, s: (b, s, 0)),
                   pl.BlockSpec((CONV_CH, ts), lambda b, s: (0, b * ns + s))],
        out_shape=[jax.ShapeDtypeStruct((B, S, CONV_CH), BF16), jax.ShapeDtypeStruct((CONV_CH, B * S), BF16)],
        scratch_shapes=[pltpu.VMEM((ts + HALO, CONV_CH), F32),
                        pltpu.VMEM((SUBLANES - 1, ts + HALO, CONV_CH), F32)],
        args=(ug, ug, conv_w, conv_b, ln_g, ln_b),
    )


def conv_branch_bwd_a(ug, dcat, conv_w, conv_b, ln_g, ln_b, *, name, ts=256):
    B, S, _ = ug.shape
    ts = min(ts, S)
    cur, halo, w, vec = _conv_specs(ts, S)

    def body(cur_ref, halo_ref, d_ref, w_ref, cb_ref, lg_ref, lb_ref, dy_ref, dw_ref, dv_ref, ext, sh):
        _fill_causal_ext(ext, cur_ref, halo_ref, pl.program_id(1), ts)
        _make_shifted(ext, sh)
        y = _causal_conv(ext, sh, w_ref, ts) + cb_ref[...]
        yh, rstd = _ln_stats(y)
        lg = lg_ref[...]
        ln = yh * lg + lb_ref[...]
        sg = _sigmoid(ln)
        dln = d_ref[0].astype(F32) * (sg * (1.0 + ln * (1.0 - sg)))
        dyh = dln * lg
        dy = rstd * (dyh - jnp.mean(dyh, axis=-1, keepdims=True)
                     - yh * jnp.mean(dyh * yh, axis=-1, keepdims=True))
        dy_ref[0] = dy

        @pl.when((pl.program_id(0) == 0) & (pl.program_id(1) == 0))
        def _():
            dw_ref[...] = jnp.zeros_like(dw_ref)
            dv_ref[...] = jnp.zeros_like(dv_ref)

        dv_ref[pl.ds(0, 1), :] += jnp.sum(dy, axis=0, keepdims=True)
        dv_ref[pl.ds(1, 1), :] += jnp.sum(dln * yh, axis=0, keepdims=True)
        dv_ref[pl.ds(2, 1), :] += jnp.sum(dln, axis=0, keepdims=True)
        for j in range(CONV_K):
            tap = _tap(ext, sh, HALO - (CONV_K - 1) + j, ts)
            dw_ref[pl.ds(j, 1), :] += jnp.sum(dy * tap, axis=0, keepdims=True)

    return _call(
        body, name=name, grid=(B, S // ts),
        in_specs=[cur, halo, pl.BlockSpec((1, ts, CONV_CH), lambda b, s: (b, s, 0)), w, vec, vec, vec],
        out_specs=[pl.BlockSpec((1, ts, CONV_CH), lambda b, s: (b, s, 0)),
                   pl.BlockSpec((HALO, CONV_CH), lambda b, s: (0, 0)),
                   pl.BlockSpec((8, CONV_CH), lambda b, s: (0, 0))],
        out_shape=[jax.ShapeDtypeStruct((B, S, CONV_CH), F32),
                   jax.ShapeDtypeStruct((HALO, CONV_CH), F32),
                   jax.ShapeDtypeStruct((8, CONV_CH), F32)],
        scratch_shapes=[pltpu.VMEM((ts + HALO, CONV_CH), F32),
                        pltpu.VMEM((SUBLANES - 1, ts + HALO, CONV_CH), F32)],
        compiler_params=_params(("arbitrary", "arbitrary")),
    )(ug, ug, dcat, conv_w, conv_b, ln_g, ln_b)


def conv_branch_bwd_b(ug, dy, conv_w, *, name, ts=256):
    B, S, _ = ug.shape
    ts = min(ts, S)
    nh, n_halo = ts // HALO, S // HALO

    def body(cur_ref, dy_ref, nxt_ref, w_ref, o_ref, ext, sh):
        last = pl.program_id(1) == pl.num_programs(1) - 1
        ext[pl.ds(0, ts), :] = dy_ref[0]
        ext[pl.ds(ts, HALO), :] = jnp.where(last, 0.0, nxt_ref[0])
        _make_shifted(ext, sh)
        da = jnp.zeros((ts, CONV_CH), F32)
        for j in range(CONV_K):
            da = da + _tap(ext, sh, CONV_K - 1 - j, ts) * w_ref[pl.ds(j, 1), :]
        blk = cur_ref[0]
        u = blk[:, :CONV_CH].astype(F32)
        sg = _sigmoid(blk[:, CONV_CH:].astype(F32))
        o_ref[0, :, :CONV_CH] = (da * sg).astype(BF16)
        o_ref[0, :, CONV_CH:] = (da * u * sg * (1.0 - sg)).astype(BF16)

    return _call(
        body, name=name, grid=(B, S // ts),
        in_specs=[pl.BlockSpec((1, ts, 2 * CONV_CH), lambda b, s: (b, s, 0)),
                  pl.BlockSpec((1, ts, CONV_CH), lambda b, s: (b, s, 0)),
                  pl.BlockSpec((1, HALO, CONV_CH), lambda b, s: (b, jnp.minimum((s + 1) * nh, n_halo - 1), 0)),
                  pl.BlockSpec((HALO, CONV_CH), lambda b, s: (0, 0))],
        out_specs=pl.BlockSpec((1, ts, 2 * CONV_CH), lambda b, s: (b, s, 0)),
        out_shape=jax.ShapeDtypeStruct((B, S, 2 * CONV_CH), BF16),
        scratch_shapes=[pltpu.VMEM((ts + HALO, CONV_CH), F32),
                        pltpu.VMEM((SUBLANES - 1, ts + HALO, CONV_CH), F32)],
        compiler_params=_params(("parallel", "parallel")),
    )(ug, dy, dy, conv_w)


def _tri(n, lower):
    r = lax.broadcasted_iota(jnp.int32, (n, n), 0)
    c = lax.broadcasted_iota(jnp.int32, (n, n), 1)
    return ((r >= c) if lower else (r <= c)).astype(F32)


def _eye(n):
    r = lax.broadcasted_iota(jnp.int32, (n, n), 0)
    c = lax.broadcasted_iota(jnp.int32, (n, n), 1)
    return (r == c).astype(F32)


def _dot_hi(a, b, dn):
    return lax.dot_general(a, b, dn, precision=lax.Precision.HIGHEST, preferred_element_type=F32)


NN = (((1,), (0,)), ((), ()))
NT = (((1,), (1,)), ((), ()))
TN = (((0,), (0,)), ((), ()))


def _log_sigmoid(v):
    e = jnp.exp(-jnp.abs(v))
    log1p_e = jnp.where(e < 1e-3, e * (1.0 - 0.5 * e), jnp.log(1.0 + e))
    return jnp.minimum(v, 0.0) - log1p_e


def fgate_fwd(h, w_f, b_f, *, name, ts=256, rider=None):
    B, S, D = h.shape
    ts = min(ts, S)

    def body(h_ref, w_ref, b_ref, f_ref, cc_ref, cr_ref, carry):
        @pl.when(pl.program_id(1) == 0)
        def _():
            carry[...] = jnp.zeros_like(carry)

        f = jnp.dot(h_ref[0], w_ref[...], preferred_element_type=F32)
        f_ref[0] = f
        logf = _log_sigmoid(f + b_ref[...])
        c = _dot_hi(_tri(ts, True), logf, NN) + carry[pl.ds(0, 1), :]
        cc_ref[0] = c
        carry[pl.ds(0, 1), :] = c[ts - 1:ts, :]
        cr_ref[0] = _dot_hi(_eye(LANES), c, NT)

    return hosted_call(
        body, rider, name=name, grid=(B, S // ts),
        in_specs=[pl.BlockSpec((1, ts, D), lambda b, s: (b, s, 0)),
                  pl.BlockSpec((D, LANES), lambda b, s: (0, 0)),
                  pl.BlockSpec((1, LANES), lambda b, s: (0, 0))],
        out_specs=[pl.BlockSpec((1, ts, LANES), lambda b, s: (b, s, 0)),
                   pl.BlockSpec((1, ts, LANES), lambda b, s: (b, s, 0)),
                   pl.BlockSpec((1, LANES, ts), lambda b, s: (b, 0, s))],
        out_shape=[jax.ShapeDtypeStruct((B, S, LANES), F32), jax.ShapeDtypeStruct((B, S, LANES), F32),
                   jax.ShapeDtypeStruct((B, LANES, S), F32)],
        scratch_shapes=[pltpu.VMEM((8, LANES), F32)],
        args=(h, w_f, b_f),
    )


def fgate_bwd(dc, f, b_f, *, name, ts=256):
    B, S, _ = f.shape
    P = dc.shape[1]
    ts = min(ts, S)
    ns = S // ts

    def body(dc_ref, f_ref, b_ref, df_ref, db_ref, carry):
        @pl.when(pl.program_id(1) == 0)
        def _():
            carry[...] = jnp.zeros_like(carry)

        @pl.when((pl.program_id(0) == 0) & (pl.program_id(1) == 0))
        def _():
            db_ref[...] = jnp.zeros_like(db_ref)

        dc_t = dc_ref[0, 0]
        for j in range(1, P):
            dc_t = dc_t + dc_ref[0, j]
        dlogf = _dot_hi(_tri(ts, False), dc_t, NN) + carry[pl.ds(0, 1), :]
        carry[pl.ds(0, 1), :] = dlogf[0:1, :]
        df = dlogf * _sigmoid(-(f_ref[0] + b_ref[...]))
        df_ref[0] = df.astype(BF16)
        db_ref[...] += jnp.sum(df, axis=0, keepdims=True)

    return _call(
        body, name=name, grid=(B, ns),
        in_specs=[pl.BlockSpec((1, P, ts, LANES), lambda b, s: (b, 0, ns - 1 - s, 0)),
                  pl.BlockSpec((1, ts, LANES), lambda b, s: (b, ns - 1 - s, 0)),
                  pl.BlockSpec((1, LANES), lambda b, s: (0, 0))],
        out_specs=[pl.BlockSpec((1, ts, LANES), lambda b, s: (b, ns - 1 - s, 0)),
                   pl.BlockSpec((1, LANES), lambda b, s: (0, 0))],
        out_shape=[jax.ShapeDtypeStruct((B, S, LANES), BF16), jax.ShapeDtypeStruct((1, LANES), F32)],
        scratch_shapes=[pltpu.VMEM((8, LANES), F32)],
        compiler_params=_params(("arbitrary", "arbitrary")),
    )(dc, f, b_f)


def _lane_pick(tile, idx):
    lane = lax.broadcasted_iota(jnp.int32, tile.shape, 1)
    return jnp.sum(jnp.where(lane == idx, tile, 0.0), axis=-1, keepdims=True)


FOX_T = 512


def _fox_heads(q, cc_ref, p):
    lane = lax.broadcasted_iota(jnp.int32, q.shape, 1)
    qs = q * (1.0 / math.sqrt(FOX_HEAD_DIM))
    qhs = [jnp.where((lane < FOX_HEAD_DIM) == (hh == 0), qs, jnp.zeros_like(qs)) for hh in range(2)]
    crefs = [_lane_pick(cc_ref[0, pl.ds(0, 1), :], 2 * p + hh) for hh in range(2)]
    return qhs, crefs


def _fold_lanes(x, op):
    out = x[:, :LANES]
    for j in range(1, x.shape[1] // LANES):
        out = op(out, x[:, j * LANES:(j + 1) * LANES])
    return out


def _causal(t, transposed):
    r = lax.broadcasted_iota(jnp.int32, (t, t), 0)
    c = lax.broadcasted_iota(jnp.int32, (t, t), 1)
    return (r <= c) if transposed else (c <= r)


QKV0 = 8


def fox_fwd(z, c_col, c_row, *, name, rider=None):
    B, S, _ = z.shape
    assert S % FOX_T == 0
    tq, nq = FOX_T, S // FOX_T
    npair = FOX_HEADS // 2

    def body(q_ref, k_ref, v_ref, cc_ref, cr_ref, o_ref, l_ref, ot_ref, s_scr, m_scr, acc_scr):
        p, qi = pl.program_id(1), pl.program_id(2)
        qhs, crefs = _fox_heads(q_ref[0], cc_ref, p)
        lane = lax.broadcasted_iota(jnp.int32, (tq, LANES), 1)
        first = lane < FOX_HEAD_DIM
        for hh in range(2):
            m_scr[hh] = jnp.full((tq, LANES), NEG, F32)
            acc_scr[hh] = jnp.zeros((tq, LANES), F32)

        def logits(kb, diagonal):
            k0 = pl.multiple_of(kb * tq, tq)
            k = k_ref[0, pl.ds(k0, tq), :]
            for hh in range(2):
                s = lax.dot_general(qhs[hh], k, NT, preferred_element_type=F32)
                s = s + (crefs[hh] - cr_ref[0, pl.ds(2 * p + hh, 1), pl.ds(k0, tq)])
                if diagonal:
                    s = jnp.where(_causal(tq, False), s, NEG)
                s_scr[hh, kb] = s
                m_scr[hh] = jnp.maximum(m_scr[hh], _fold_lanes(s, jnp.maximum))

        def sweep1(kb, carry):
            logits(kb, False)
            return carry

        lax.fori_loop(0, qi, sweep1, 0)
        logits(qi, True)
        ms = [jnp.max(m_scr[hh], axis=-1, keepdims=True) for hh in range(2)]
        mbs = [jnp.broadcast_to(ms[hh], (tq, tq)) for hh in range(2)]

        for hh in range(2):
            m_scr[hh] = jnp.zeros((tq, LANES), F32)

        def weigh(kb, carry):
            k0 = pl.multiple_of(kb * tq, tq)
            v = v_ref[0, pl.ds(k0, tq), :]
            for hh in range(2):
                pr = jnp.exp(s_scr[hh, kb] - mbs[hh])
                m_scr[hh] += _fold_lanes(pr, jnp.add)
                acc_scr[hh] += jnp.dot(pr.astype(BF16), v, preferred_element_type=F32)
            return carry

        lax.fori_loop(0, qi + 1, weigh, 0)
        accs = [acc_scr[hh] for hh in range(2)]
        ls = [jnp.sum(m_scr[hh], axis=-1, keepdims=True) for hh in range(2)]
        out = jnp.where(first, accs[0] / ls[0], accs[1] / ls[1])
        o_ref[0] = out.astype(BF16)
        ot_ref[...] = out.T.astype(BF16)
        l_ref[0, 0] = jnp.where(first, ms[0] + jnp.log(ls[0]), ms[1] + jnp.log(ls[1]))

    return hosted_call(
        body, rider, name=name, grid=(B, npair, nq),
        in_specs=[pl.BlockSpec((1, tq, LANES), lambda b, p, i: (b, i, QKV0 + p)),
                  pl.BlockSpec((1, S, LANES), lambda b, p, i: (b, 0, QKV0 + npair + p)),
                  pl.BlockSpec((1, S, LANES), lambda b, p, i: (b, 0, QKV0 + 2 * npair + p)),
                  pl.BlockSpec((1, tq, LANES), lambda b, p, i: (b, i, 0)),
                  pl.BlockSpec((1, 8, S), lambda b, p, i: (b, 0, 0))],
        out_specs=[pl.BlockSpec((1, tq, LANES), lambda b, p, i: (b, i, p)),
                   pl.BlockSpec((1, 1, tq, LANES), lambda b, p, i: (b, p, i, 0)),
                   pl.BlockSpec((LANES, tq), lambda b, p, i: (p, b * nq + i))],
        out_shape=[jax.ShapeDtypeStruct((B, S, FOX_W), BF16),
                   jax.ShapeDtypeStruct((B, npair, S, LANES), F32),
                   jax.ShapeDtypeStruct((FOX_W, B * S), BF16)],
        scratch_shapes=[pltpu.VMEM((2, nq, tq, tq), F32), pltpu.VMEM((2, tq, LANES), F32),
                        pltpu.VMEM((2, tq, LANES), F32)],
        args=(z, z, z, c_col, c_row),
    )


def fox_bwd_dq(z, dcat, lse, c_col, c_row, *, name, rider=None):
    B, S, _ = z.shape
    tq, nq = FOX_T, S // FOX_T
    npair = FOX_HEADS // 2

    def body(q_ref, k_ref, v_ref, do_ref, l_ref, cc_ref, cr_ref, dq_ref, st_ref, p_scr, dp_scr, dl_scr):
        p, qi = pl.program_id(1), pl.program_id(2)
        qhs, crefs = _fox_heads(q_ref[0], cc_ref, p)
        lane = lax.broadcasted_iota(jnp.int32, (tq, LANES), 1)
        do_b = do_ref[0].astype(BF16)
        dohs = [jnp.where((lane < FOX_HEAD_DIM) == (hh == 0), do_b, jnp.zeros_like(do_b)) for hh in range(2)]
        lses = [_lane_pick(l_ref[0, 0], hh * FOX_HEAD_DIM) for hh in range(2)]
        lbs = [jnp.broadcast_to(lses[hh], (tq, tq)) for hh in range(2)]
        for hh in range(2):
            dl_scr[hh] = jnp.zeros((tq, LANES), F32)

        def probs(kb, diagonal):
            k0 = pl.multiple_of(kb * tq, tq)
            k = k_ref[0, pl.ds(k0, tq), :]
            v = v_ref[0, pl.ds(k0, tq), :]
            for hh in range(2):
                s = lax.dot_general(qhs[hh], k, NT, preferred_element_type=F32)
                s = s + (crefs[hh] - cr_ref[0, pl.ds(2 * p + hh, 1), pl.ds(k0, tq)])
                pr = jnp.exp(s - lbs[hh])
                if diagonal:
                    pr = jnp.where(_causal(tq, False), pr, 0.0)
                dp = lax.dot_general(dohs[hh], v, NT, preferred_element_type=F32)
                pdp = pr * dp
                dl_scr[hh] += _fold_lanes(pdp, jnp.add)
                p_scr[hh, kb] = pr
                dp_scr[hh, kb] = dp

        def first_pass(kb, carry):
            probs(kb, False)
            return carry

        lax.fori_loop(0, qi, first_pass, 0)
        probs(qi, True)

        dls = [jnp.sum(dl_scr[hh], axis=-1, keepdims=True) for hh in range(2)]
        dlbs = [jnp.broadcast_to(dls[hh], (tq, tq)) for hh in range(2)]

        def second_pass(kb, dq):
            k0 = pl.multiple_of(kb * tq, tq)
            k = k_ref[0, pl.ds(k0, tq), :]
            for hh in range(2):
                ds = p_scr[hh, kb] * (dp_scr[hh, kb] - dlbs[hh])
                kh = jnp.where((lane < FOX_HEAD_DIM) == (hh == 0), k, jnp.zeros_like(k))
                dq = dq + jnp.dot(ds.astype(BF16), kh, preferred_element_type=F32)
            return dq

        dq = lax.fori_loop(0, qi + 1, second_pass, jnp.zeros((tq, LANES), F32))
        dq_ref[0] = (dq * (1.0 / math.sqrt(FOX_HEAD_DIM))).astype(BF16)
        cols = jnp.zeros((tq, LANES), F32)
        for j, col in enumerate([crefs[0] - lses[0], crefs[1] - lses[1], dls[0], dls[1]]):
            cols = jnp.where(lane == j, col, cols)
        st_ref[0, 0] = _dot_hi(_eye(LANES), cols, NT)[:8]

    return hosted_call(
        body, rider, name=name, grid=(B, npair, nq),
        in_specs=[pl.BlockSpec((1, tq, LANES), lambda b, p, i: (b, i, QKV0 + p)),
                  pl.BlockSpec((1, S, LANES), lambda b, p, i: (b, 0, QKV0 + npair + p)),
                  pl.BlockSpec((1, S, LANES), lambda b, p, i: (b, 0, QKV0 + 2 * npair + p)),
                  pl.BlockSpec((1, tq, LANES), lambda b, p, i: (b, i, npair + p)),
                  pl.BlockSpec((1, 1, tq, LANES), lambda b, p, i: (b, p, i, 0)),
                  pl.BlockSpec((1, tq, LANES), lambda b, p, i: (b, i, 0)),
                  pl.BlockSpec((1, 8, S), lambda b, p, i: (b, 0, 0))],
        out_specs=[pl.BlockSpec((1, tq, LANES), lambda b, p, i: (b, i, p)),
                   pl.BlockSpec((1, 1, 8, tq), lambda b, p, i: (b, p, 0, i))],
        out_shape=[jax.ShapeDtypeStruct((B, S, FOX_W), BF16), jax.ShapeDtypeStruct((B, npair, 8, S), F32)],
        scratch_shapes=[pltpu.VMEM((2, nq, tq, tq), F32), pltpu.VMEM((2, nq, tq, tq), F32),
                        pltpu.VMEM((2, tq, LANES), F32)],
        args=(z, z, z, dcat, lse, c_col, c_row), vmem=56 << 20,
    )


def fox_bwd_dkdv(z, dcat, stats, c_col, *, name, rider=None):
    B, S, _ = z.shape
    tk, nq = FOX_T, S // FOX_T
    npair = FOX_HEADS // 2
    inv = 1.0 / math.sqrt(FOX_HEAD_DIM)

    def body(q_ref, k_ref, v_ref, do_ref, st_ref, cc_ref, dk_ref, dv_ref, dc_ref, dk_scr, dv_scr, dc_scr):
        p, kt = pl.program_id(1), pl.program_id(2)
        lane = lax.broadcasted_iota(jnp.int32, (tk, LANES), 1)
        masks = [(lane < FOX_HEAD_DIM) == (hh == 0) for hh in range(2)]
        k = k_ref[0]
        v = v_ref[0]
        khs = [jnp.where(masks[hh], k, jnp.zeros_like(k)) for hh in range(2)]
        vhs = [jnp.where(masks[hh], v, jnp.zeros_like(v)) for hh in range(2)]
        ccbs = [jnp.broadcast_to(_lane_pick(cc_ref[0], 2 * p + hh), (tk, tk)) for hh in range(2)]
        dk_scr[...] = jnp.zeros_like(dk_scr)
        dv_scr[...] = jnp.zeros_like(dv_scr)
        dc_scr[...] = jnp.zeros_like(dc_scr)

        def tile(qb, diagonal):
            q0 = pl.multiple_of(qb * tk, tk)
            qs = q_ref[0, pl.ds(q0, tk), :] * inv
            do_b = do_ref[0, pl.ds(q0, tk), :].astype(BF16)
            for hh in range(2):
                st = lax.dot_general(khs[hh], qs, NT, preferred_element_type=F32)
                pr = jnp.exp(st - ccbs[hh] + st_ref[0, 0, pl.ds(hh, 1), pl.ds(q0, tk)])
                if diagonal:
                    pr = jnp.where(_causal(tk, True), pr, 0.0)
                dp = lax.dot_general(vhs[hh], do_b, NT, preferred_element_type=F32)
                ds = pr * (dp - st_ref[0, 0, pl.ds(2 + hh, 1), pl.ds(q0, tk)])
                dv_scr[...] += jnp.dot(pr.astype(BF16), jnp.where(masks[hh], do_b, jnp.zeros_like(do_b)),
                                       preferred_element_type=F32)
                dk_scr[...] += jnp.dot(ds.astype(BF16), jnp.where(masks[hh], qs, jnp.zeros_like(qs)),
                                       preferred_element_type=F32)
                dc_scr[hh] -= _fold_lanes(ds, jnp.add)

        def later(qb, carry):
            tile(qb, False)
            return carry

        tile(kt, True)
        lax.fori_loop(kt + 1, nq, later, 0)
        dk_ref[0] = dk_scr[...].astype(BF16)
        dv_ref[0] = dv_scr[...].astype(BF16)
        dcs = [jnp.sum(dc_scr[hh], axis=-1, keepdims=True) for hh in range(2)]
        dc_ref[0, 0] = jnp.where(lane == 2 * p, dcs[0], jnp.where(lane == 2 * p + 1, dcs[1], 0.0))

    full = lambda col: pl.BlockSpec((1, S, LANES), col)
    tile_spec = lambda col: pl.BlockSpec((1, tk, LANES), col)
    return hosted_call(
        body, rider, name=name, grid=(B, npair, nq),
        in_specs=[full(lambda b, p, t: (b, 0, QKV0 + p)),
                  tile_spec(lambda b, p, t: (b, t, QKV0 + npair + p)),
                  tile_spec(lambda b, p, t: (b, t, QKV0 + 2 * npair + p)),
                  full(lambda b, p, t: (b, 0, npair + p)),
                  pl.BlockSpec((1, 1, 8, S), lambda b, p, t: (b, p, 0, 0)),
                  tile_spec(lambda b, p, t: (b, t, 0))],
        out_specs=[tile_spec(lambda b, p, t: (b, t, p)), tile_spec(lambda b, p, t: (b, t, p)),
                   pl.BlockSpec((1, 1, tk, LANES), lambda b, p, t: (b, p, t, 0))],
        out_shape=[jax.ShapeDtypeStruct((B, S, FOX_W), BF16)] * 2
        + [jax.ShapeDtypeStruct((B, npair, S, LANES), F32)],
        scratch_shapes=[pltpu.VMEM((tk, LANES), F32), pltpu.VMEM((tk, LANES), F32),
                        pltpu.VMEM((2, tk, LANES), F32)],
        args=(z, z, z, dcat, stats, c_col),
    )


def xattn_fwd(qm, kv, *, name, tq=512):
    B, S, D = qm.shape
    M = kv.shape[1]
    tq = min(tq, S)
    inv = 1.0 / math.sqrt(MEM_HEAD_DIM)

    nq = S // tq

    def body(q_ref, kv_ref, o_ref, ot_ref):
        for h in range(MEM_HEADS):
            c0 = h * MEM_HEAD_DIM
            qh = q_ref[0, :, c0:c0 + MEM_HEAD_DIM]
            kh = kv_ref[0, :, c0:c0 + MEM_HEAD_DIM]
            vh = kv_ref[0, :, D + c0:D + c0 + MEM_HEAD_DIM]
            s = lax.dot_general(qh, kh, NT, preferred_element_type=F32) * inv
            e = jnp.exp(s - jnp.max(s, axis=-1, keepdims=True))
            o = jnp.dot(e.astype(BF16), vh, preferred_element_type=F32) / jnp.sum(e, axis=-1, keepdims=True)
            o_ref[0, :, c0:c0 + MEM_HEAD_DIM] = o.astype(BF16)
            ot_ref[c0:c0 + MEM_HEAD_DIM, :] = o.T.astype(BF16)

    return _call(
        body, name=name, grid=(B, nq),
        in_specs=[pl.BlockSpec((1, tq, D), lambda b, i: (b, i, 0)),
                  pl.BlockSpec((1, M, 2 * D), lambda b, i: (b, 0, 0))],
        out_specs=[pl.BlockSpec((1, tq, D), lambda b, i: (b, i, 0)),
                   pl.BlockSpec((D, tq), lambda b, i: (0, b * nq + i))],
        out_shape=[jax.ShapeDtypeStruct((B, S, D), BF16), jax.ShapeDtypeStruct((D, B * S), BF16)],
        compiler_params=_params(("parallel", "parallel")),
    )(qm, kv)


def xattn_bwd(qm, kv, do, *, name, tq=512):
    B, S, D = qm.shape
    M = kv.shape[1]
    tq = min(tq, S)
    inv = 1.0 / math.sqrt(MEM_HEAD_DIM)

    def body(q_ref, kv_ref, do_ref, dq_ref, dkv_ref):
        @pl.when(pl.program_id(1) == 0)
        def _():
            dkv_ref[...] = jnp.zeros_like(dkv_ref)

        for h in range(MEM_HEADS):
            c0 = h * MEM_HEAD_DIM
            qh = q_ref[0, :, c0:c0 + MEM_HEAD_DIM]
            kh = kv_ref[0, :, c0:c0 + MEM_HEAD_DIM]
            vh = kv_ref[0, :, D + c0:D + c0 + MEM_HEAD_DIM]
            doh = do_ref[0, :, c0:c0 + MEM_HEAD_DIM]
            s = lax.dot_general(qh, kh, NT, preferred_element_type=F32) * inv
            e = jnp.exp(s - jnp.max(s, axis=-1, keepdims=True))
            pr = e / jnp.sum(e, axis=-1, keepdims=True)
            dp = lax.dot_general(doh, vh, NT, preferred_element_type=F32)
            ds = pr * (dp - jnp.sum(pr * dp, axis=-1, keepdims=True))
            ds_b = ds.astype(BF16)
            dq_ref[0, :, c0:c0 + MEM_HEAD_DIM] = (jnp.dot(ds_b, kh, preferred_element_type=F32) * inv).astype(BF16)
            dkv_ref[0, :, c0:c0 + MEM_HEAD_DIM] += lax.dot_general(ds_b, qh, TN, preferred_element_type=F32) * inv
            dkv_ref[0, :, D + c0:D + c0 + MEM_HEAD_DIM] += lax.dot_general(
                pr.astype(BF16), doh, TN, preferred_element_type=F32)

    row = pl.BlockSpec((1, tq, D), lambda b, i: (b, i, 0))
    kvs = pl.BlockSpec((1, M, 2 * D), lambda b, i: (b, 0, 0))
    return _call(
        body, name=name, grid=(B, S // tq), in_specs=[row, kvs, row], out_specs=[row, kvs],
        out_shape=[jax.ShapeDtypeStruct((B, S, D), BF16), jax.ShapeDtypeStruct((B, M, 2 * D), F32)],
        compiler_params=_params(("parallel", "arbitrary")),
    )(qm, kv, do)


SWIGLU_TN = 1408


def _chunks(n, w=256):
    return [(c0, min(w, n - c0)) for c0 in range(0, n, w)]


def mm_swiglu_fwd(hf, w_gu, *, name, tm=512):
    T, D = hf.shape
    Fh = w_gu.shape[1] // 2
    tm, tn = min(tm, T), SWIGLU_TN
    nj = Fh // tn
    assert Fh % tn == 0 and T % tm == 0

    def body(a_ref, bg_ref, bu_ref, g_ref, u_ref, o_ref, ot_ref):
        a = a_ref[...]
        for c0, cw in _chunks(tn):
            cols = pl.ds(c0, cw)
            g = jnp.dot(a, bg_ref[:, cols], preferred_element_type=F32)
            u = jnp.dot(a, bu_ref[:, cols], preferred_element_type=F32)
            act = g * _sigmoid(g) * u
            g_ref[:, cols] = g.astype(BF16)
            u_ref[:, cols] = u.astype(BF16)
            o_ref[:, cols] = act.astype(BF16)
            ot_ref[cols, :] = act.T.astype(BF16)

    tile = pl.BlockSpec((tm, tn), lambda i, j: (i, j))
    return _call(
        body, name=name, grid=(T // tm, nj),
        in_specs=[pl.BlockSpec((tm, D), lambda i, j: (i, 0)), pl.BlockSpec((D, tn), lambda i, j: (0, j)),
                  pl.BlockSpec((D, tn), lambda i, j: (0, nj + j))],
        out_specs=[tile, tile, tile, pl.BlockSpec((tn, tm), lambda i, j: (j, i))],
        out_shape=[jax.ShapeDtypeStruct((T, Fh), BF16)] * 3 + [jax.ShapeDtypeStruct((Fh, T), BF16)],
        compiler_params=_params(("parallel", "parallel"), 48 << 20),
    )(hf, w_gu, w_gu)


def mm_swiglu_bwd(dx, w_down, g, u, *, name, tm=512):
    T, D = dx.shape
    Fh = w_down.shape[0]
    tm, tn = min(tm, T), SWIGLU_TN
    assert Fh % tn == 0 and T % tm == 0

    def body(a_ref, b_ref, g_ref, u_ref, dg_ref, du_ref):
        a = a_ref[...].astype(BF16)
        for c0, cw in _chunks(tn):
            cols = pl.ds(c0, cw)
            d = lax.dot_general(a, b_ref[cols, :], NT, preferred_element_type=F32)
            gv = g_ref[:, cols].astype(F32)
            uv = u_ref[:, cols].astype(F32)
            sg = _sigmoid(gv)
            dg_ref[:, cols] = (d * uv * (sg * (1.0 + gv * (1.0 - sg)))).astype(BF16)
            du_ref[:, cols] = (d * gv * sg).astype(BF16)

    tile = pl.BlockSpec((tm, tn), lambda i, j: (i, j))
    return _call(
        body, name=name, grid=(T // tm, Fh // tn),
        in_specs=[pl.BlockSpec((tm, D), lambda i, j: (i, 0)), pl.BlockSpec((tn, D), lambda i, j: (j, 0)), tile, tile],
        out_specs=[tile, tile],
        out_shape=[jax.ShapeDtypeStruct((T, Fh), BF16)] * 2,
        compiler_params=_params(("parallel", "parallel"), 48 << 20),
    )(dx, w_down, g, u)


LATE_MID = ("w_out", "w_mq", "w_mo")
LATE_KV = ("w_mkv",)
LATE_FFN = ("w_gu", "w_down")
LATE = LATE_MID + LATE_KV + LATE_FFN
RS_GROUPS = (("w_gu", "w_down"), ("w_out", "w_mq", "w_mkv", "w_mo"), ("w_in",))


def pair_sums(names, g42, got):
    return {n: pair_sum(g, o, name="rs_pair_sum_" + n) for n, g, o in zip(names, g42, got)}


def local_step(x, mem, target, sp, first_shards, late_shards):
    B, S, D = x.shape
    T = B * S
    M = mem.shape[1]
    row = lambda v: v.reshape(1, -1).astype(F32)
    g_mix, g_x, g_mem, g_ffn, g_final = (row(sp[k]) for k in ("g_mix", "g_x", "g_mem", "g_ffn", "g_final"))
    conv_b, ln_g, ln_b = row(sp["conv_b"]), row(sp["ln_g"]), row(sp["ln_b"])
    b_f = jnp.pad(row(sp["b_f"]), ((0, 0), (0, LANES - FOX_HEADS)))
    n_ug, n_main = 2 * CONV_CH, 2 * CONV_CH + 3 * FOX_W

    x2d = x.reshape(T, D)
    h, h_t, partly = rmsnorm_fwd(x2d, g_mix, name="rms_mix", rider=AllGatherStage1(first_shards))
    w_in8, cw8 = run_rider(AllGatherStage2(partly), name="ag_first_stage2")
    w_in_full = _full_from_gathered("w_in", w_in8)
    conv_w = cw8.transpose(1, 0, 2).reshape(HALO, -1)
    w_main, w_ug, w_qkv = w_in_full[:, :n_main], w_in_full[:, :n_ug], w_in_full[:, n_ug:n_main]
    w_f = jnp.pad(w_in_full[:, n_main:], ((0, 0), (0, LANES - FOX_HEADS)))
    z = matmul(h, w_main, out_dtype=BF16, tn=n_main, name="mm_in")
    z3 = z.reshape(B, S, n_main)
    n_mid, n_kv = len(LATE_MID), len(LATE_MID) + len(LATE_KV)
    (conv_out, conv_t), partly_mid = conv_branch_fwd(z3, conv_w, conv_b, ln_g, ln_b, name="conv_fwd",
                                                     rider=AllGatherStage1(late_shards[:n_mid]))
    (f_raw, c_col, c_row), rode = fgate_fwd(
        h.reshape(B, S, D), w_f, b_f, name="fgate_fwd",
        rider=Riders(AllGatherStage1(late_shards[n_mid:n_kv]), AllGatherStage2(partly_mid)))
    partly_kv, full_mid = rode[:n_kv - n_mid], rode[n_kv - n_mid:]
    (att, lse, att_t), rode = fox_fwd(
        z3, c_col, c_row, name="fox_fwd",
        rider=Riders(AllGatherStage1(late_shards[n_kv:]), AllGatherStage2(partly_kv)))
    partly_ffn, full_kv = rode[:len(LATE_FFN)], rode[len(LATE_FFN):]
    wf = {n: _full_from_gathered(n, blk) for n, blk in zip(LATE_MID + LATE_KV, full_mid + full_kv)}
    x1, full_ffn = matmul([conv_out.reshape(T, CONV_CH), att.reshape(T, FOX_W)], [wf["w_out"], wf["w_out"]],
                          b_blk=[0, 1], out_dtype=F32, res=x2d, tn=D, name="mm_out",
                          rider=AllGatherStage2(partly_ffn))
    wf.update({n: _full_from_gathered(n, blk) for n, blk in zip(LATE_FFN, full_ffn)})
    hx, hx_t = rmsnorm_fwd(x1, g_x, name="rms_x")
    qm = matmul(hx, wf["w_mq"], out_dtype=BF16, tn=D, name="mm_mq")
    mem2d = mem.reshape(B * M, D)
    mem_n, mem_n_t = rmsnorm_fwd(mem2d, g_mem, name="rms_mem")
    kv = matmul(mem_n, wf["w_mkv"], out_dtype=BF16, tn=2 * D, name="mm_mkv").reshape(B, M, 2 * D)
    o, o_t = xattn_fwd(qm.reshape(B, S, D), kv, name="xattn_fwd")
    o = o.reshape(T, D)
    x2 = matmul(o, wf["w_mo"], out_dtype=F32, res=x1, tn=D, name="mm_mo")
    hf, hf_t = rmsnorm_fwd(x2, g_ffn, name="rms_ffn")
    gate, up, act, act_t = mm_swiglu_fwd(hf, wf["w_gu"], name="mm_gu")
    x3 = matmul(act, wf["w_down"], out_dtype=F32, res=x2, tn=D, name="mm_down")
    dx3, dg_final, loss = final_loss_bwd(x3, g_final, target.reshape(T, D), name="loss_bwd")
    gw = {}
    gw["w_down"] = matmul(act_t, dx3, out_dtype=BF16, tm=1408, tn=512, name="dw_down")
    dgate, dup = mm_swiglu_bwd(dx3, wf["w_down"], gate, up, name="dx_down")
    gw["w_gu"] = [matmul(hf_t, dgate, out_dtype=BF16, tn=SWIGLU_TN, name="dw_gate"),
                  matmul(hf_t, dup, out_dtype=BF16, tn=SWIGLU_TN, name="dw_up")]
    g42 = [_shards_from_full(n, gw[n]) for n in RS_GROUPS[0]]
    dhf, got = matmul([dgate, dup], [wf["w_gu"], wf["w_gu"]], b_blk=[0, 1], tb=True, out_dtype=BF16,
                      tm=256, tn=D, name="dx_gu", rider=SiblingExchange(g42))
    parts = pair_sums(RS_GROUPS[0], g42, got)
    dx2, dg_ffn = rmsnorm_bwd(x2, g_ffn, dhf, dx3, name="rms_ffn_bwd")
    gw["w_mo"] = matmul(o_t, dx2, out_dtype=BF16, tn=D, name="dw_mo")
    do = matmul(dx2, wf["w_mo"], tb=True, out_dtype=BF16, tn=D, name="dx_mo")
    dqm, dkv = xattn_bwd(qm.reshape(B, S, D), kv, do.reshape(B, S, D), name="xattn_bwd")
    dqm = dqm.reshape(T, D)
    dkv = dkv.reshape(B * M, 2 * D)
    gw["w_mq"] = matmul(hx_t, dqm, out_dtype=BF16, tn=D, name="dw_mq")
    dhx = matmul(dqm, wf["w_mq"], tb=True, out_dtype=BF16, tn=D, name="dx_mq")
    gw["w_mkv"] = matmul(mem_n_t, dkv, out_dtype=BF16, tn=D, name="dw_mkv")
    dmem_n = matmul(dkv, wf["w_mkv"], tb=True, out_dtype=BF16, tn=D, name="dx_mkv")
    _, dg_mem = rmsnorm_bwd(mem2d, g_mem, dmem_n, None, name="rms_mem_bwd")
    dx1, dg_x = rmsnorm_bwd(x1, g_x, dhx, dx2, name="rms_x_bwd")
    gw["w_out"] = jnp.concatenate([matmul(conv_t, dx1, out_dtype=BF16, tn=D, name="dw_out_conv"),
                                   matmul(att_t, dx1, out_dtype=BF16, tn=D, name="dw_out_att")], axis=0)
    g42 = [_shards_from_full(n, gw[n]) for n in RS_GROUPS[1]]
    dcat, got = matmul(dx1, wf["w_out"], tb=True, out_dtype=BF16, tn=D, name="dx_out", rider=SiblingExchange(g42))
    dcat = dcat.reshape(B, S, D)
    parts.update(pair_sums(RS_GROUPS[1], g42, got))
    dy, dconv_w, dvec = conv_branch_bwd_a(z3, dcat, conv_w, conv_b, ln_g, ln_b, name="conv_bwd_a")
    dug = conv_branch_bwd_b(z3, dy, conv_w, name="conv_bwd_b")
    gots = {}
    (dq, stats), got = fox_bwd_dq(z3, dcat, lse, c_col, c_row, name="fox_bwd_dq",
                                  rider=ChipExchange([parts[n] for n in RS_GROUPS[0]]))
    gots.update(zip(RS_GROUPS[0], got))
    (dk, dv, dc), got = fox_bwd_dkdv(z3, dcat, stats, c_col, name="fox_bwd_dkdv",
                                     rider=ChipExchange([parts[n] for n in RS_GROUPS[1]]))
    gots.update(zip(RS_GROUPS[1], got))
    df, db_f = fgate_bwd(dc, f_raw, b_f, name="fgate_bwd")
    dug2 = dug.reshape(T, n_ug)
    dqkv = jnp.concatenate([dq, dk, dv], axis=-1).reshape(T, 3 * FOX_W)
    df2 = df.reshape(T, LANES)
    dw_in = [matmul(h_t, dug2, out_dtype=BF16, tn=n_ug, name="dw_in_ug"),
             matmul(h_t, dqkv, out_dtype=BF16, tn=3 * FOX_W, name="dw_in_qkv"),
             matmul(h_t, df2, out_dtype=BF16, name="dw_f")[:, :FOX_HEADS]]
    g42 = [_shards_from_full("w_in", dw_in)]
    parts.update(pair_sums(RS_GROUPS[2], g42, run_rider(SiblingExchange(g42), name="rs_sibling_in")))
    dh, (gots["w_in"],) = matmul([dug2, dqkv, df2], [w_ug, w_qkv, w_f], tb=True, out_dtype=BF16, tn=D,
                                 name="dx_in", rider=ChipExchange([parts["w_in"]]))
    dx, dg_mix = rmsnorm_bwd(x2d, g_mix, dh, dx1, name="rms_mix_bwd", out_dtype=F32)
    gs = dict(g_mix=dg_mix, b_f=db_f[:, :FOX_HEADS], conv_w=dconv_w[:CONV_K], conv_b=dvec[0:1],
              ln_g=dvec[1:2], ln_b=dvec[2:3], g_x=dg_x, g_mem=dg_mem, g_ffn=dg_ffn, g_final=dg_final)
    return loss, dx.reshape(B, S, D), gs, {n: (parts[n], gots[n]) for n in BIG}


def _me():
    return lax.axis_index("x"), lax.axis_index("y"), lax.axis_index("c")


def _any_specs(n):
    return [pl.BlockSpec(memory_space=pl.ANY)] * n


def all_gather(xs, *, name):
    n = len(xs)

    def body(*refs):
        x_refs, out_refs = refs[:n], refs[n:2 * n]
        send_sems, recv_sems, local_sems = refs[2 * n:]
        x, y, c = _me()
        me, sibling = (x, y, c), (x, y, 1 - c)
        chips = [(1 - x, y), (x, 1 - y), (1 - x, 1 - y)]

        def slot(a, px, py, pc):
            return out_refs[a].at[4 * px + 2 * py + pc]

        def copy(a, k, block, to, own=False):
            return pltpu.make_async_remote_copy(
                src_ref=x_refs[a] if own else slot(a, *block), dst_ref=slot(a, *block),
                send_sem=send_sems.at[k, a], recv_sem=recv_sems.at[k, a], device_id=to, device_id_type=MESH)

        mine = [pltpu.make_async_copy(x_refs[a], slot(a, *me), local_sems.at[a]) for a in range(n)]
        first = [copy(a, 0, me, sibling, own=True) for a in range(n)]
        first += [copy(a, 1 + j, me, (*chip, c), own=True) for j, chip in enumerate(chips) for a in range(n)]
        for cp in mine + first:
            cp.start()
        passed = []
        for j, chip in enumerate(chips):
            for a in range(n):
                copy(a, 1 + j, (*chip, c), me).wait_recv()
                passed.append(copy(a, 4 + j, (*chip, c), sibling))
                passed[-1].start()
        for a in range(n):
            copy(a, 0, sibling, me).wait_recv()
            for j, chip in enumerate(chips):
                copy(a, 4 + j, (*chip, 1 - c), me).wait_recv()
        for cp in first + passed:
            cp.wait_send()
        for cp in mine:
            cp.wait()

    return _call(
        body, name=name, in_specs=_any_specs(n), out_specs=_any_specs(n),
        out_shape=[jax.ShapeDtypeStruct((N_DEV,) + v.shape, v.dtype) for v in xs],
        scratch_shapes=[pltpu.SemaphoreType.DMA((7, n)), pltpu.SemaphoreType.DMA((7, n)),
                        pltpu.SemaphoreType.DMA((n,))],
    )(*xs)


SIBLING_BARRIER = 1
CHIPS_BARRIER = 2
GATHER_BARRIER = 3


class SiblingExchange:
    collective_id = SIBLING_BARRIER

    def __init__(self, gs):
        n = len(gs)
        self.n, self.inputs = n, list(gs)
        self.out_shape = [jax.ShapeDtypeStruct((4,) + g.shape[2:], g.dtype) for g in gs]
        self.scratch = [pltpu.SemaphoreType.DMA((n,)), pltpu.SemaphoreType.DMA((n,))]

    @staticmethod
    def barrier_peers():
        x, y, c = _me()
        return [(x, y, 1 - c)]

    def _copies(self, g_refs, out_refs, sems):
        send_sems, recv_sems = sems
        x, y, c = _me()
        return [pltpu.make_async_remote_copy(
            src_ref=g_refs[a].at[:, 1 - c], dst_ref=out_refs[a], send_sem=send_sems.at[a],
            recv_sem=recv_sems.at[a], device_id=(x, y, 1 - c), device_id_type=MESH) for a in range(self.n)]

    def start(self, in_refs, out_refs, sems):
        for cp in self._copies(in_refs, out_refs, sems):
            cp.start()

    def finish(self, in_refs, out_refs, sems):
        for cp in self._copies(in_refs, out_refs, sems):
            cp.wait()


def run_rider(rider, *, name):
    return hosted_call(None, rider, name=name, grid=(), in_specs=[], out_specs=[], out_shape=[],
                       scratch_shapes=[], args=[])[1]


class ChipExchange:
    collective_id = CHIPS_BARRIER

    @staticmethod
    def barrier_peers():
        x, y, c = _me()
        return [(1 - x, y, c), (x, 1 - y, c), (1 - x, 1 - y, c)]

    def __init__(self, ps):
        n = len(ps)
        self.n, self.inputs = n, list(ps)
        self.out_shape = [jax.ShapeDtypeStruct(p.shape, p.dtype) for p in ps]
        self.scratch = [pltpu.SemaphoreType.DMA((3, n)), pltpu.SemaphoreType.DMA((3, n))]

    def _copies(self, p_refs, out_refs, sems, outgoing):
        send_sems, recv_sems = sems
        x, y, c = _me()
        my_chip = 2 * x + y
        cps = []
        for k in range(3):
            px, py = x ^ ((k + 1) >> 1), y ^ ((k + 1) & 1)
            src, dst = (2 * px + py, my_chip) if outgoing else (my_chip, 2 * px + py)
            for a in range(self.n):
                cps.append(pltpu.make_async_remote_copy(
                    src_ref=p_refs[a].at[src], dst_ref=out_refs[a].at[dst], send_sem=send_sems.at[k, a],
                    recv_sem=recv_sems.at[k, a], device_id=(px, py, c), device_id_type=MESH))
        return cps

    def start(self, in_refs, out_refs, sems):
        for cp in self._copies(in_refs, out_refs, sems, True):
            cp.start()

    def finish(self, in_refs, out_refs, sems):
        for cp in self._copies(in_refs, out_refs, sems, False):
            cp.wait_recv()
        for cp in self._copies(in_refs, out_refs, sems, True):
            cp.wait_send()


class AllGatherStage1:
    collective_id = GATHER_BARRIER

    @staticmethod
    def barrier_peers():
        x, y, c = _me()
        return [(x, y, 1 - c), (1 - x, y, c), (x, 1 - y, c), (1 - x, 1 - y, c)]

    def __init__(self, xs):
        n = len(xs)
        self.n, self.inputs = n, list(xs)
        self.out_shape = [jax.ShapeDtypeStruct((N_DEV,) + v.shape, v.dtype) for v in xs]
        self.scratch = [pltpu.SemaphoreType.DMA((4, n)), pltpu.SemaphoreType.DMA((4, n)),
                        pltpu.SemaphoreType.DMA((n,))]

    def _copies(self, x_refs, out_refs, sems, kind):
        send_sems, recv_sems, local_sems = sems
        x, y, c = _me()
        slot = lambda a, d: out_refs[a].at[4 * d[0] + 2 * d[1] + d[2]]
        if kind == "local":
            return [pltpu.make_async_copy(x_refs[a], slot(a, (x, y, c)), local_sems.at[a]) for a in range(self.n)]
        cps = []
        for k, peer in enumerate([(x, y, 1 - c), (1 - x, y, c), (x, 1 - y, c), (1 - x, 1 - y, c)]):
            for a in range(self.n):
                cps.append(pltpu.make_async_remote_copy(
                    src_ref=x_refs[a], dst_ref=slot(a, (x, y, c) if kind == "out" else peer),
                    send_sem=send_sems.at[k, a], recv_sem=recv_sems.at[k, a], device_id=peer, device_id_type=MESH))
        return cps

    def start(self, in_refs, out_refs, sems):
        for cp in self._copies(in_refs, out_refs, sems, "local") + self._copies(in_refs, out_refs, sems, "out"):
            cp.start()

    def finish(self, in_refs, out_refs, sems):
        for cp in self._copies(in_refs, out_refs, sems, "in"):
            cp.wait_recv()
        for cp in self._copies(in_refs, out_refs, sems, "out"):
            cp.wait_send()
        for cp in self._copies(in_refs, out_refs, sems, "local"):
            cp.wait()


class AllGatherStage2:
    collective_id = SIBLING_BARRIER

    @staticmethod
    def barrier_peers():
        x, y, c = _me()
        return [(x, y, 1 - c)]

    def __init__(self, outs):
        n = len(outs)
        self.n, self.inputs = n, list(outs)
        self.out_shape = [jax.ShapeDtypeStruct(o.shape, o.dtype) for o in outs]
        self.scratch = [pltpu.SemaphoreType.DMA((3, n)), pltpu.SemaphoreType.DMA((3, n))]
        self.aliases = {a: a for a in range(n)}

    def _copies(self, out_refs, sems, outgoing):
        send_sems, recv_sems = sems
        x, y, c = _me()
        cps = []
        for k, (px, py) in enumerate([(1 - x, y), (x, 1 - y), (1 - x, 1 - y)]):
            for a in range(self.n):
                cps.append(pltpu.make_async_remote_copy(
                    src_ref=out_refs[a].at[4 * px + 2 * py + c],
                    dst_ref=out_refs[a].at[4 * px + 2 * py + (c if outgoing else 1 - c)],
                    send_sem=send_sems.at[k, a], recv_sem=recv_sems.at[k, a], device_id=(x, y, 1 - c),
                    device_id_type=MESH))
        return cps

    def start(self, in_refs, out_refs, sems):
        for cp in self._copies(out_refs, sems, True):
            cp.start()

    def finish(self, in_refs, out_refs, sems):
        for cp in self._copies(out_refs, sems, False):
            cp.wait_recv()
        for cp in self._copies(out_refs, sems, True):
            cp.wait_send()


class Riders:
    def __init__(self, *riders):
        self.riders = riders
        self.collective_id = riders[0].collective_id
        self.barrier_peers = riders[0].barrier_peers
        self.inputs = [v for r in riders for v in r.inputs]
        self.out_shape = [s for r in riders for s in r.out_shape]
        self.scratch = [s for r in riders for s in r.scratch]
        self.aliases, i0, o0 = {}, 0, 0
        for r in riders:
            self.aliases.update({i0 + i: o0 + o for i, o in getattr(r, "aliases", {}).items()})
            i0, o0 = i0 + len(r.inputs), o0 + len(r.out_shape)

    def _split(self, in_refs, out_refs, sems):
        i0 = o0 = s0 = 0
        for r in self.riders:
            ni, no, ns = len(r.inputs), len(r.out_shape), len(r.scratch)
            yield r, in_refs[i0:i0 + ni], out_refs[o0:o0 + no], sems[s0:s0 + ns]
            i0, o0, s0 = i0 + ni, o0 + no, s0 + ns

    def start(self, in_refs, out_refs, sems):
        for r, i, o, s in self._split(in_refs, out_refs, sems):
            r.start(i, o, s)

    def finish(self, in_refs, out_refs, sems):
        for r, i, o, s in self._split(in_refs, out_refs, sems):
            r.finish(i, o, s)


def _peer_barrier(peers):
    barrier = pltpu.get_barrier_semaphore()
    for peer in peers:
        pl.semaphore_signal(barrier, inc=1, device_id=peer, device_id_type=MESH)
    pl.semaphore_wait(barrier, len(peers))


def hosted_call(body, rider, *, name, grid, in_specs, out_specs, out_shape, scratch_shapes, args, vmem=None):
    n_in, n_out, n_scr = len(in_specs), len(out_specs), len(scratch_shapes)
    r_in, r_out = (len(rider.inputs), len(rider.out_shape)) if rider is not None else (0, 0)
    own_barrier = getattr(rider, "collective_id", None) is not None

    def wrapped(*refs):
        ins, refs = refs[:n_in], refs[n_in:]
        rins, refs = refs[:r_in], refs[r_in:]
        outs, refs = refs[:n_out], refs[n_out:]
        routs, refs = refs[:r_out], refs[r_out:]
        scr, rscr = refs[:n_scr], refs[n_scr:]
        ids = [pl.program_id(d) for d in range(len(grid))]
        first = functools.reduce(jnp.logical_and, [i == 0 for i in ids], True)
        last = functools.reduce(jnp.logical_and, [i == g - 1 for i, g in zip(ids, grid)], True)

        def begin():
            if own_barrier:
                _peer_barrier(rider.barrier_peers())
            rider.start(rins, routs, rscr)

        if rider is not None and grid:
            pl.when(first)(begin)
        elif rider is not None:
            begin()
        if body is not None:
            body(*ins, *outs, *scr)
        if rider is not None and grid:
            pl.when(last)(lambda: rider.finish(rins, routs, rscr))
        elif rider is not None:
            rider.finish(rins, routs, rscr)

    kw = dict(grid=grid) if grid else {}
    aliases = getattr(rider, "aliases", {})
    if aliases:
        kw["input_output_aliases"] = {n_in + i: n_out + o for i, o in aliases.items()}
    if grid or vmem is not None or own_barrier:
        kw["compiler_params"] = _params(("arbitrary",) * len(grid) if grid else None, vmem,
                                        rider.collective_id if own_barrier else None)
    res = _call(
        wrapped, name=name, in_specs=list(in_specs) + _any_specs(r_in), out_specs=list(out_specs) + _any_specs(r_out),
        out_shape=list(out_shape) + (rider.out_shape if rider is not None else []),
        scratch_shapes=list(scratch_shapes) + (rider.scratch if rider is not None else []), **kw,
    )(*args, *(rider.inputs if rider is not None else []))
    return list(res[:n_out]), list(res[n_out:])


def _pick_rows(r, target=256):
    best = None
    for d in range(16, min(r, target) + 1, 16):
        if r % d == 0:
            best = d
    return r if best is None else best


def pair_sum(g, got, *, name):
    _, _, R, C = g.shape
    tr = _pick_rows(R)

    def body(g_ref, got_ref, o_ref):
        mine = jnp.where(lax.axis_index("c") == 0, g_ref[:, 0], g_ref[:, 1])
        o_ref[...] = (mine.astype(F32) + got_ref[...].astype(F32)).astype(o_ref.dtype)

    return _call(
        body, name=name, grid=(R // tr,),
        in_specs=[pl.BlockSpec((4, 2, tr, C), lambda i: (0, 0, i, 0)), pl.BlockSpec((4, tr, C), lambda i: (0, i, 0))],
        out_specs=pl.BlockSpec((4, tr, C), lambda i: (0, i, 0)),
        out_shape=jax.ShapeDtypeStruct((4, R, C), g.dtype),
        compiler_params=_params(("parallel",)),
    )(g, got)


def chip_sum_adamw(p, got, w, m, v, *, name):
    _, R, C = p.shape
    assert w.shape == (1, R, C), (name, w.shape, p.shape)
    tr = _pick_rows(R)

    def body(p_ref, got_ref, w_ref, m_ref, v_ref, g_ref, d_ref, mo_ref, vo_ref):
        my_chip = 2 * lax.axis_index("x") + lax.axis_index("y")
        g = jnp.zeros((tr, C), F32)
        for j in range(4):
            g = g + jnp.where(my_chip == j, p_ref[j], got_ref[j]).astype(F32)
        g_ref[0] = g
        d_ref[0], mo_ref[0], vo_ref[0] = _adamw_math(w_ref[0], g, m_ref[0], v_ref[0])

    part = pl.BlockSpec((4, tr, C), lambda i: (0, i, 0))
    spec = pl.BlockSpec((1, tr, C), lambda i: (0, i, 0))
    return _call(
        body, name=name, grid=(R // tr,), in_specs=[part, part, spec, spec, spec], out_specs=[spec] * 4,
        out_shape=[jax.ShapeDtypeStruct((1, R, C), F32)] * 4,
        compiler_params=_params(("parallel",)),
    )(p, got, w, m, v)


def rows_sum(g8, *, name):
    _, R, C = g8.shape

    def body(g_ref, o_ref):
        acc = g_ref[0]
        for j in range(1, N_DEV):
            acc = acc + g_ref[j]
        o_ref[...] = acc

    return _call(body, name=name, out_shape=jax.ShapeDtypeStruct((R, C), F32))(g8)


def _adamw_math(w, g, m, v):
    m = ADAM_B1 * m + (1.0 - ADAM_B1) * g
    v = ADAM_B2 * v + (1.0 - ADAM_B2) * (g * g)
    m_hat = m / (1.0 - ADAM_B1 ** ADAM_STEP)
    v_hat = v / (1.0 - ADAM_B2 ** ADAM_STEP)
    delta = -ADAM_LR * (m_hat / (jnp.sqrt(v_hat) + ADAM_EPS) + ADAM_WD * w)
    return delta, m, v


def to_bf16(xs, *, name):
    def body(*refs):
        for x_ref, o_ref in zip(refs[:len(xs)], refs[len(xs):]):
            o_ref[...] = x_ref[...].astype(BF16)

    total = sum(_nbytes(v.shape, F32) + _nbytes(v.shape, BF16) for v in xs)
    return _call(body, name=name, out_shape=[jax.ShapeDtypeStruct(v.shape, BF16) for v in xs],
                 compiler_params=_params(vmem=2 * total + (4 << 20)))(*xs)


def adamw_small(wgmv, *, name):
    n = len(wgmv)

    def body(*refs):
        ins, outs = refs[:4 * n], refs[4 * n:]
        for a in range(n):
            w_ref, g_ref, m_ref, v_ref = ins[4 * a:4 * a + 4]
            d, mn, vn = _adamw_math(w_ref[...], g_ref[...], m_ref[...], v_ref[...])
            outs[3 * a][...] = d
            outs[3 * a + 1][...] = mn
            outs[3 * a + 2][...] = vn

    flat = [t for tup in wgmv for t in tup]
    res = _call(
        body, name=name,
        out_shape=[jax.ShapeDtypeStruct(tup[0].shape, F32) for tup in wgmv for _ in range(3)],
    )(*flat)
    return [tuple(res[3 * a:3 * a + 3]) for a in range(n)]


BIG = ("w_in", "w_out", "w_mq", "w_mkv", "w_mo", "w_gu", "w_down")
COL_SHARDED = ("w_in", "w_mkv", "w_gu")
SMALL = ("g_mix", "b_f", "conv_w", "conv_b", "ln_g", "ln_b", "g_x", "g_mem", "g_ffn", "g_final")


def _full_from_gathered(n, blk):
    _, rr, cc = blk.shape
    if n in COL_SHARDED:
        return jnp.concatenate([blk[k] for k in range(N_DEV)], axis=1)
    return blk.reshape(N_DEV * rr, cc)


def _shards_from_full(n, g):
    pieces = g if isinstance(g, list) else [g]
    rr, cc = pieces[0].shape[0], sum(p.shape[1] for p in pieces)
    if n in COL_SHARDED:
        w = cc // N_DEV
        return jnp.stack([_columns(pieces, k * w, w) for k in range(N_DEV)]).reshape(4, 2, rr, w)
    return pieces[0].reshape(4, 2, rr // N_DEV, cc)


def _columns(pieces, start, width):
    out, c0 = [], 0
    for p in pieces:
        lo, hi = max(start, c0), min(start + width, c0 + p.shape[1])
        if lo < hi:
            out.append(p[:, lo - c0:hi - c0])
        c0 += p.shape[1]
    return out[0] if len(out) == 1 else jnp.concatenate(out, axis=1)


def _small_layout():
    sizes = dict(g_mix=1024, b_f=8, conv_w=CONV_K * CONV_CH, conv_b=512, ln_g=512, ln_b=512, g_x=1024,
                 g_mem=1024, g_ffn=1024, g_final=1024, loss=1)
    lay, r0 = {}, 0
    for n, sz in sizes.items():
        r = -(-sz // LANES)
        lay[n] = (r0, r, sz)
        r0 += r
    return lay, -(-r0 // 8) * 8


def kernel(x, mem, g_mix, w_in, b_f, conv_w, conv_b, ln_g, ln_b, w_out, g_x, g_mem, w_mq, w_mkv, w_mo, g_ffn, w_gu, w_down, g_final, loss_target, m_g_mix, m_w_in, m_b_f, m_conv_w, m_conv_b, m_ln_g, m_ln_b, m_w_out, m_g_x, m_g_mem, m_w_mq, m_w_mkv, m_w_mo, m_g_ffn, m_w_gu, m_w_down, m_g_final, v_g_mix, v_w_in, v_b_f, v_conv_w, v_conv_b, v_ln_g, v_ln_b, v_w_out, v_g_x, v_g_mem, v_w_mq, v_w_mkv, v_w_mo, v_g_ffn, v_w_gu, v_w_down, v_g_final):
    names = ["g_mix", "w_in", "b_f", "conv_w", "conv_b", "ln_g", "ln_b", "w_out", "g_x", "g_mem", "w_mq",
             "w_mkv", "w_mo", "g_ffn", "w_gu", "w_down", "g_final"]
    W = dict(zip(names, [g_mix, w_in, b_f, conv_w, conv_b, ln_g, ln_b, w_out, g_x, g_mem, w_mq, w_mkv, w_mo,
                         g_ffn, w_gu, w_down, g_final]))
    Mo = dict(zip(names, [m_g_mix, m_w_in, m_b_f, m_conv_w, m_conv_b, m_ln_g, m_ln_b, m_w_out, m_g_x, m_g_mem,
                          m_w_mq, m_w_mkv, m_w_mo, m_g_ffn, m_w_gu, m_w_down, m_g_final]))
    Vo = dict(zip(names, [v_g_mix, v_w_in, v_b_f, v_conv_w, v_conv_b, v_ln_g, v_ln_b, v_w_out, v_g_x, v_g_mem,
                          v_w_mq, v_w_mkv, v_w_mo, v_g_ffn, v_w_gu, v_w_down, v_g_final]))
    dev = 4 * lax.axis_index("x") + 2 * lax.axis_index("y") + lax.axis_index("c")

    two = lambda a: a.reshape(-1, a.shape[-1])
    cw_shard = jnp.pad(two(conv_w), ((0, HALO - CONV_K), (0, 0)))
    sp = dict(g_mix=g_mix, b_f=b_f, conv_b=conv_b, ln_g=ln_g, ln_b=ln_b, g_x=g_x, g_mem=g_mem,
              g_ffn=g_ffn, g_final=g_final)
    shards = to_bf16([two(W[n]) for n in ("w_in",) + LATE], name="cast_shards")
    loss_blk, grad_x, gs, reduced = local_step(x, mem, loss_target, sp, [shards[0], cw_shard], shards[1:])

    lay, rs = _small_layout()
    small = {**{n: gs[n] for n in SMALL}, "loss": loss_blk[:, :1]}
    parts = []
    for n, (r0, r, sz) in lay.items():
        flat = small[n].reshape(-1).astype(F32)
        parts.append(jnp.pad(flat, (0, r * LANES - sz)).reshape(r, LANES))
    spack = jnp.concatenate(parts, axis=0)
    spack = jnp.pad(spack, ((0, rs - spack.shape[0]), (0, 0)))
    ssum = rows_sum(all_gather([spack], name="ag_small")[0], name="small_sum")
    gsmall = {n: ssum[r0:r0 + r].reshape(-1)[:sz] for n, (r0, r, sz) in lay.items()}
    loss = gsmall["loss"].reshape(())

    grads, delta, new_m, new_v = {}, {}, {}, {}
    for n in BIG:
        p, o = reduced[n]
        grads[n], delta[n], new_m[n], new_v[n] = chip_sum_adamw(p, o, W[n], Mo[n], Vo[n], name="adamw_" + n)
    for n in SMALL:
        if n == "conv_w":
            full = gsmall[n].reshape(CONV_K, CONV_CH)
            ncol = conv_w.shape[-1]
            grads[n] = lax.dynamic_slice(full, (0, dev * ncol), (CONV_K, ncol)).reshape(conv_w.shape)
        else:
            grads[n] = gsmall[n].reshape(W[n].shape)
    upd = adamw_small([(two(W[n]), two(grads[n]), two(Mo[n]), two(Vo[n])) for n in SMALL], name="adamw_small")
    for n, (d, mn, vn) in zip(SMALL, upd):
        shp = W[n].shape
        delta[n], new_m[n], new_v[n] = d.reshape(shp), mn.reshape(shp), vn.reshape(shp)
    return (loss, grad_x, *[grads[n] for n in names], *[delta[n] for n in names],
            *[new_m[n] for n in names], *[new_v[n] for n in names])
```

```python
import functools
import math

import jax
import jax.numpy as jnp
from jax import lax
from jax.experimental import pallas as pl
from jax.experimental.pallas import tpu as pltpu

F32 = jnp.float32
BF16 = jnp.bfloat16
EPS = 1e-6
N_DEV = 8
CONV_CH = 512
CONV_K = 31
FOX_HEADS = 8
FOX_HEAD_DIM = 64
FOX_W = 512
MEM_HEADS = 4
MEM_HEAD_DIM = 256
HALO = 32
LANES = 128
ADAM_LR, ADAM_B1, ADAM_B2, ADAM_EPS, ADAM_WD, ADAM_STEP = 0.001, 0.9, 0.999, 1e-08, 0.01, 10
NEG = -1e30
VMEM_CAP = 60 * 1024 * 1024
MESH = pl.DeviceIdType.MESH


def _call(body, **kw):
    kw["out_shape"] = jax.tree.map(lambda s: pltpu.HBM(s.shape, s.dtype), kw["out_shape"])
    call = pl.pallas_call(body, **kw)
    return lambda *args: call(*[pltpu.with_memory_space_constraint(a, pltpu.HBM) for a in args])


def _params(sem=None, vmem=None, collective_id=None):
    kw = {} if collective_id is None else {"collective_id": collective_id}
    if sem is not None:
        kw["dimension_semantics"] = sem
    if vmem is not None:
        kw["vmem_limit_bytes"] = int(min(VMEM_CAP, vmem))
    return pltpu.CompilerParams(**kw)


def _nbytes(shape, dtype):
    return math.prod(shape) * jnp.dtype(dtype).itemsize


def _pick(n, target):
    best = None
    for d in range(LANES, min(n, target) + 1, LANES):
        if n % d == 0:
            best = d
    return n if best is None else best


def matmul(a, b, *, tb=False, out_dtype, res=None, tm=512, tn=512, name, rider=None, b_blk=None):
    a_list = list(a) if isinstance(a, (list, tuple)) else [a]
    b_list = list(b) if isinstance(b, (list, tuple)) else [b]
    n = len(a_list)
    assert len(b_list) == n
    M = a_list[0].shape[0]
    N = b_list[0].shape[0] if tb else b_list[0].shape[1]
    tm, tn = _pick(M, tm), _pick(N, tn)
    assert M % tm == 0 and N % tn == 0, (name, M, N, tm, tn)
    dn = (((1,), (1 if tb else 0,)), ((), ()))

    def body(*refs):
        acc = None
        for a_ref, b_ref in zip(refs[:n], refs[n:2 * n]):
            p = lax.dot_general(a_ref[...].astype(BF16), b_ref[...].astype(BF16), dn, preferred_element_type=F32)
            acc = p if acc is None else acc + p
        if res is not None:
            acc = acc + refs[2 * n][...].astype(F32)
        refs[-1][...] = acc.astype(out_dtype)

    o_spec = pl.BlockSpec((tm, tn), lambda i, j: (i, j))
    in_specs, est = [], 2 * _nbytes((tm, tn), out_dtype) + 2 * _nbytes((tm, tn), F32)
    for av in a_list:
        assert av.shape[0] == M
        in_specs.append(pl.BlockSpec((tm, av.shape[1]), lambda i, j: (i, 0)))
        est += (2 * jnp.dtype(av.dtype).itemsize + (av.dtype != BF16) * 2) * tm * av.shape[1]
    for idx, (av, bv) in enumerate(zip(a_list, b_list)):
        K = av.shape[1]
        kb = 0 if b_blk is None else b_blk[idx]
        assert bv.shape[0 if tb else 1] == N and bv.shape[1 if tb else 0] >= (kb + 1) * K, (name, av.shape, bv.shape)
        assert b_blk is not None or bv.shape[1 if tb else 0] == K, (name, av.shape, bv.shape)
        in_specs.append(pl.BlockSpec((tn, K), lambda i, j, kb=kb: (j, kb)) if tb
                        else pl.BlockSpec((K, tn), lambda i, j, kb=kb: (kb, j)))
        est += (2 * jnp.dtype(bv.dtype).itemsize + (bv.dtype != BF16) * 2) * tn * K
    args = a_list + b_list
    if res is not None:
        in_specs.append(o_spec)
        args.append(res)
        est += 2 * _nbytes((tm, tn), res.dtype)
    (out,), rode = hosted_call(
        body, rider, name=name, grid=(M // tm, N // tn), in_specs=in_specs, out_specs=[o_spec],
        out_shape=[jax.ShapeDtypeStruct((M, N), out_dtype)], scratch_shapes=[],
        args=args, vmem=est + (8 << 20),
    )
    return out if rider is None else (out, rode)


def _rms_scale(x):
    return lax.rsqrt(jnp.mean(x * x, axis=-1, keepdims=True) + EPS)


def rmsnorm_fwd(x, g, *, name, tm=512, rider=None):
    T, D = x.shape
    tm = min(tm, T)

    def body(x_ref, g_ref, o_ref, ot_ref):
        xv = x_ref[...]
        h = xv * _rms_scale(xv) * g_ref[...]
        o_ref[...] = h.astype(BF16)
        ot_ref[...] = h.T.astype(BF16)

    (h, h_t), rode = hosted_call(
        body, rider, name=name, grid=(T // tm,),
        in_specs=[pl.BlockSpec((tm, D), lambda i: (i, 0)), pl.BlockSpec((1, D), lambda i: (0, 0))],
        out_specs=[pl.BlockSpec((tm, D), lambda i: (i, 0)), pl.BlockSpec((D, tm), lambda i: (0, i))],
        out_shape=[jax.ShapeDtypeStruct((T, D), BF16), jax.ShapeDtypeStruct((D, T), BF16)],
        scratch_shapes=[], args=(x, g),
    )
    return (h, h_t) if rider is None else (h, h_t, rode)


def _rms_bwd_math(xv, gv, dh):
    r = _rms_scale(xv)
    xh = xv * r
    dg = jnp.sum(dh * xh, axis=0, keepdims=True)
    dxh = dh * gv
    dx = r * (dxh - xh * jnp.mean(dxh * xh, axis=-1, keepdims=True))
    return dx, dg


def rmsnorm_bwd(x, g, dh, dres, *, name, tm=256, out_dtype=BF16):
    T, D = x.shape
    tm = min(tm, T)

    def body(*refs):
        if dres is not None:
            x_ref, g_ref, dh_ref, dr_ref, dx_ref, dg_ref = refs
        else:
            x_ref, g_ref, dh_ref, dx_ref, dg_ref = refs
        dx, dg = _rms_bwd_math(x_ref[...], g_ref[...], dh_ref[...].astype(F32))
        if dres is not None:
            dx = dx + dr_ref[...].astype(F32)
        dx_ref[...] = dx.astype(out_dtype)

        @pl.when(pl.program_id(0) == 0)
        def _():
            dg_ref[...] = jnp.zeros_like(dg_ref)

        dg_ref[...] += dg

    row = pl.BlockSpec((tm, D), lambda i: (i, 0))
    vec = pl.BlockSpec((1, D), lambda i: (0, 0))
    ins, args = [row, vec, row], [x, g, dh]
    if dres is not None:
        ins.append(row)
        args.append(dres)
    return _call(
        body, name=name, grid=(T // tm,), in_specs=ins, out_specs=[row, vec],
        out_shape=[jax.ShapeDtypeStruct((T, D), out_dtype), jax.ShapeDtypeStruct((1, D), F32)],
        compiler_params=_params(("arbitrary",)),
    )(*args)


def final_loss_bwd(x, g, target, *, name, tm=256):
    T, D = x.shape
    tm = min(tm, T)

    def body(x_ref, g_ref, t_ref, dx_ref, dg_ref, l_ref):
        xv, gv = x_ref[...], g_ref[...]
        e = xv * _rms_scale(xv) * gv - t_ref[...]
        part = 0.5 * jnp.sum(jnp.mean(e * e, axis=-1, keepdims=True), axis=0, keepdims=True)
        dx, dg = _rms_bwd_math(xv, gv, e * (1.0 / D))
        dx_ref[...] = dx.astype(BF16)

        @pl.when(pl.program_id(0) == 0)
        def _():
            dg_ref[...] = jnp.zeros_like(dg_ref)
            l_ref[...] = jnp.zeros_like(l_ref)

        dg_ref[...] += dg
        l_ref[...] += jnp.broadcast_to(part, l_ref.shape)

    row = pl.BlockSpec((tm, D), lambda i: (i, 0))
    vec = pl.BlockSpec((1, D), lambda i: (0, 0))
    return _call(
        body, name=name, grid=(T // tm,), in_specs=[row, vec, row],
        out_specs=[row, vec, pl.BlockSpec((1, LANES), lambda i: (0, 0))],
        out_shape=[jax.ShapeDtypeStruct((T, D), BF16), jax.ShapeDtypeStruct((1, D), F32),
                   jax.ShapeDtypeStruct((1, LANES), F32)],
        compiler_params=_params(("arbitrary",)),
    )(x, g, target)


def _sigmoid(v):
    return 0.5 * jnp.tanh(0.5 * v) + 0.5


def _glu(blk):
    u = blk[:, :CONV_CH].astype(F32)
    gt = blk[:, CONV_CH:].astype(F32)
    return u * _sigmoid(gt)


def _fill_causal_ext(ext, cur_ref, halo_ref, s, ts):
    ext[pl.ds(HALO, ts), :] = _glu(cur_ref[0])
    hal = _glu(halo_ref[0])
    ext[pl.ds(0, HALO), :] = jnp.where(s > 0, hal, 0.0)


SUBLANES = 8


def _make_shifted(ext, sh):
    n = ext.shape[0]
    full = ext[...]
    for r in range(1, SUBLANES):
        sh[r - 1] = pltpu.roll(full, n - r, 0)


def _tap(ext, sh, off, ts):
    r = off % SUBLANES
    return ext[pl.ds(off, ts), :] if r == 0 else sh[r - 1, pl.ds(off - r, ts), :]


def _causal_conv(ext, sh, w_ref, ts):
    acc = jnp.zeros((ts, CONV_CH), F32)
    for j in range(CONV_K):
        acc = acc + _tap(ext, sh, HALO - (CONV_K - 1) + j, ts) * w_ref[pl.ds(j, 1), :]
    return acc


def _ln_stats(y):
    mu = jnp.mean(y, axis=-1, keepdims=True)
    yc = y - mu
    rstd = lax.rsqrt(jnp.mean(yc * yc, axis=-1, keepdims=True) + EPS)
    return yc * rstd, rstd


def _conv_specs(ts, S):
    nh = ts // HALO
    cur = pl.BlockSpec((1, ts, 2 * CONV_CH), lambda b, s: (b, s, 0))
    halo = pl.BlockSpec((1, HALO, 2 * CONV_CH), lambda b, s: (b, jnp.maximum(s * nh - 1, 0), 0))
    w = pl.BlockSpec((HALO, CONV_CH), lambda b, s: (0, 0))
    vec = pl.BlockSpec((1, CONV_CH), lambda b, s: (0, 0))
    return cur, halo, w, vec


def conv_branch_fwd(ug, conv_w, conv_b, ln_g, ln_b, *, name, ts=256, rider=None):
    B, S, _ = ug.shape
    ts = min(ts, S)
    ns = S // ts
    cur, halo, w, vec = _conv_specs(ts, S)

    def body(cur_ref, halo_ref, w_ref, cb_ref, lg_ref, lb_ref, o_ref, ot_ref, ext, sh):
        _fill_causal_ext(ext, cur_ref, halo_ref, pl.program_id(1), ts)
        _make_shifted(ext, sh)
        y = _causal_conv(ext, sh, w_ref, ts) + cb_ref[...]
        yh, _ = _ln_stats(y)
        ln = yh * lg_ref[...] + lb_ref[...]
        out = ln * _sigmoid(ln)
        o_ref[0] = out.astype(BF16)
        ot_ref[...] = out.T.astype(BF16)

    return hosted_call(
        body, rider, name=name, grid=(B, ns), in_specs=[cur, halo, w, vec, vec, vec],
        out_specs=[pl.BlockSpec((1, ts, CONV_CH), lambda b, s: (b, s, 0)),
                   pl.BlockSpec((CONV_CH, ts), lambda b, s: (0, b * ns + s))],
        out_shape=[jax.ShapeDtypeStruct((B, S, CONV_CH), BF16), jax.ShapeDtypeStruct((CONV_CH, B * S), BF16)],
        scratch_shapes=[pltpu.VMEM((ts + HALO, CONV_CH), F32),
                        pltpu.VMEM((SUBLANES - 1, ts + HALO, CONV_CH), F32)],
        args=(ug, ug, conv_w, conv_b, ln_g, ln_b),
    )


def conv_branch_bwd_a(ug, dcat, conv_w, conv_b, ln_g, ln_b, *, name, ts=256):
    B, S, _ = ug.shape
    ts = min(ts, S)
    cur, halo, w, vec = _conv_specs(ts, S)

    def body(cur_ref, halo_ref, d_ref, w_ref, cb_ref, lg_ref, lb_ref, dy_ref, dw_ref, dv_ref, ext, sh):
        _fill_causal_ext(ext, cur_ref, halo_ref, pl.program_id(1), ts)
        _make_shifted(ext, sh)
        y = _causal_conv(ext, sh, w_ref, ts) + cb_ref[...]
        yh, rstd = _ln_stats(y)
        lg = lg_ref[...]
        ln = yh * lg + lb_ref[...]
        sg = _sigmoid(ln)
        dln = d_ref[0].astype(F32) * (sg * (1.0 + ln * (1.0 - sg)))
        dyh = dln * lg
        dy = rstd * (dyh - jnp.mean(dyh, axis=-1, keepdims=True)
                     - yh * jnp.mean(dyh * yh, axis=-1, keepdims=True))
        dy_ref[0] = dy

        @pl.when((pl.program_id(0) == 0) & (pl.program_id(1) == 0))
        def _():
            dw_ref[...] = jnp.zeros_like(dw_ref)
            dv_ref[...] = jnp.zeros_like(dv_ref)

        dv_ref[pl.ds(0, 1), :] += jnp.sum(dy, axis=0, keepdims=True)
        dv_ref[pl.ds(1, 1), :] += jnp.sum(dln * yh, axis=0, keepdims=True)
        dv_ref[pl.ds(2, 1), :] += jnp.sum(dln, axis=0, keepdims=True)
        for j in range(CONV_K):
            tap = _tap(ext, sh, HALO - (CONV_K - 1) + j, ts)
            dw_ref[pl.ds(j, 1), :] += jnp.sum(dy * tap, axis=0, keepdims=True)

    return _call(
        body, name=name, grid=(B, S // ts),
        in_specs=[cur, halo, pl.BlockSpec((1, ts, CONV_CH), lambda b, s: (b, s, 0)), w, vec, vec, vec],
        out_specs=[pl.BlockSpec((1, ts, CONV_CH), lambda b, s: (b, s, 0)),
                   pl.BlockSpec((HALO, CONV_CH), lambda b, s: (0, 0)),
                   pl.BlockSpec((8, CONV_CH), lambda b, s: (0, 0))],
        out_shape=[jax.ShapeDtypeStruct((B, S, CONV_CH), F32),
                   jax.ShapeDtypeStruct((HALO, CONV_CH), F32),
                   jax.ShapeDtypeStruct((8, CONV_CH), F32)],
        scratch_shapes=[pltpu.VMEM((ts + HALO, CONV_CH), F32),
                        pltpu.VMEM((SUBLANES - 1, ts + HALO, CONV_CH), F32)],
        compiler_params=_params(("arbitrary", "arbitrary")),
    )(ug, ug, dcat, conv_w, conv_b, ln_g, ln_b)


def conv_branch_bwd_b(ug, dy, conv_w, *, name, ts=256):
    B, S, _ = ug.shape
    ts = min(ts, S)
    nh, n_halo = ts // HALO, S // HALO

    def body(cur_ref, dy_ref, nxt_ref, w_ref, o_ref, ext, sh):
        last = pl.program_id(1) == pl.num_programs(1) - 1
        ext[pl.ds(0, ts), :] = dy_ref[0]
        ext[pl.ds(ts, HALO), :] = jnp.where(last, 0.0, nxt_ref[0])
        _make_shifted(ext, sh)
        da = jnp.zeros((ts, CONV_CH), F32)
        for j in range(CONV_K):
            da = da + _tap(ext, sh, CONV_K - 1 - j, ts) * w_ref[pl.ds(j, 1), :]
        blk = cur_ref[0]
        u = blk[:, :CONV_CH].astype(F32)
        sg = _sigmoid(blk[:, CONV_CH:].astype(F32))
        o_ref[0, :, :CONV_CH] = (da * sg).astype(BF16)
        o_ref[0, :, CONV_CH:] = (da * u * sg * (1.0 - sg)).astype(BF16)

    return _call(
        body, name=name, grid=(B, S // ts),
        in_specs=[pl.BlockSpec((1, ts, 2 * CONV_CH), lambda b, s: (b, s, 0)),
                  pl.BlockSpec((1, ts, CONV_CH), lambda b, s: (b, s, 0)),
                  pl.BlockSpec((1, HALO, CONV_CH), lambda b, s: (b, jnp.minimum((s + 1) * nh, n_halo - 1), 0)),
                  pl.BlockSpec((HALO, CONV_CH), lambda b, s: (0, 0))],
        out_specs=pl.BlockSpec((1, ts, 2 * CONV_CH), lambda b, s: (b, s, 0)),
        out_shape=jax.ShapeDtypeStruct((B, S, 2 * CONV_CH), BF16),
        scratch_shapes=[pltpu.VMEM((ts + HALO, CONV_CH), F32),
                        pltpu.VMEM((SUBLANES - 1, ts + HALO, CONV_CH), F32)],
        compiler_params=_params(("parallel", "parallel")),
    )(ug, dy, dy, conv_w)


def _tri(n, lower):
    r = lax.broadcasted_iota(jnp.int32, (n, n), 0)
    c = lax.broadcasted_iota(jnp.int32, (n, n), 1)
    return ((r >= c) if lower else (r <= c)).astype(F32)


def _eye(n):
    r = lax.broadcasted_iota(jnp.int32, (n, n), 0)
    c = lax.broadcasted_iota(jnp.int32, (n, n), 1)
    return (r == c).astype(F32)


def _dot_hi(a, b, dn):
    return lax.dot_general(a, b, dn, precision=lax.Precision.HIGHEST, preferred_element_type=F32)


NN = (((1,), (0,)), ((), ()))
NT = (((1,), (1,)), ((), ()))
TN = (((0,), (0,)), ((), ()))


def _log_sigmoid(v):
    e = jnp.exp(-jnp.abs(v))
    log1p_e = jnp.where(e < 1e-3, e * (1.0 - 0.5 * e), jnp.log(1.0 + e))
    return jnp.minimum(v, 0.0) - log1p_e


def fgate_fwd(h, w_f, b_f, *, name, ts=256, rider=None):
    B, S, D = h.shape
    ts = min(ts, S)

    def body(h_ref, w_ref, b_ref, f_ref, cc_ref, cr_ref, carry):
        @pl.when(pl.program_id(1) == 0)
        def _():
            carry[...] = jnp.zeros_like(carry)

        f = jnp.dot(h_ref[0], w_ref[...], preferred_element_type=F32)
        f_ref[0] = f
        logf = _log_sigmoid(f + b_ref[...])
        c = _dot_hi(_tri(ts, True), logf, NN) + carry[pl.ds(0, 1), :]
        cc_ref[0] = c
        carry[pl.ds(0, 1), :] = c[ts - 1:ts, :]
        cr_ref[0] = _dot_hi(_eye(LANES), c, NT)

    return hosted_call(
        body, rider, name=name, grid=(B, S // ts),
        in_specs=[pl.BlockSpec((1, ts, D), lambda b, s: (b, s, 0)),
                  pl.BlockSpec((D, LANES), lambda b, s: (0, 0)),
                  pl.BlockSpec((1, LANES), lambda b, s: (0, 0))],
        out_specs=[pl.BlockSpec((1, ts, LANES), lambda b, s: (b, s, 0)),
                   pl.BlockSpec((1, ts, LANES), lambda b, s: (b, s, 0)),
                   pl.BlockSpec((1, LANES, ts), lambda b, s: (b, 0, s))],
        out_shape=[jax.ShapeDtypeStruct((B, S, LANES), F32), jax.ShapeDtypeStruct((B, S, LANES), F32),
                   jax.ShapeDtypeStruct((B, LANES, S), F32)],
        scratch_shapes=[pltpu.VMEM((8, LANES), F32)],
        args=(h, w_f, b_f),
    )


def fgate_bwd(dc, f, b_f, *, name, ts=256):
    B, S, _ = f.shape
    P = dc.shape[1]
    ts = min(ts, S)
    ns = S // ts

    def body(dc_ref, f_ref, b_ref, df_ref, db_ref, carry):
        @pl.when(pl.program_id(1) == 0)
        def _():
            carry[...] = jnp.zeros_like(carry)

        @pl.when((pl.program_id(0) == 0) & (pl.program_id(1) == 0))
        def _():
            db_ref[...] = jnp.zeros_like(db_ref)

        dc_t = dc_ref[0, 0]
        for j in range(1, P):
            dc_t = dc_t + dc_ref[0, j]
        dlogf = _dot_hi(_tri(ts, False), dc_t, NN) + carry[pl.ds(0, 1), :]
        carry[pl.ds(0, 1), :] = dlogf[0:1, :]
        df = dlogf * _sigmoid(-(f_ref[0] + b_ref[...]))
        df_ref[0] = df.astype(BF16)
        db_ref[...] += jnp.sum(df, axis=0, keepdims=True)

    return _call(
        body, name=name, grid=(B, ns),
        in_specs=[pl.BlockSpec((1, P, ts, LANES), lambda b, s: (b, 0, ns - 1 - s, 0)),
                  pl.BlockSpec((1, ts, LANES), lambda b, s: (b, ns - 1 - s, 0)),
                  pl.BlockSpec((1, LANES), lambda b, s: (0, 0))],
        out_specs=[pl.BlockSpec((1, ts, LANES), lambda b, s: (b, ns - 1 - s, 0)),
                   pl.BlockSpec((1, LANES), lambda b, s: (0, 0))],
        out_shape=[jax.ShapeDtypeStruct((B, S, LANES), BF16), jax.ShapeDtypeStruct((1, LANES), F32)],
        scratch_shapes=[pltpu.VMEM((8, LANES), F32)],
        compiler_params=_params(("arbitrary", "arbitrary")),
    )(dc, f, b_f)


def _lane_pick(tile, idx):
    lane = lax.broadcasted_iota(jnp.int32, tile.shape, 1)
    return jnp.sum(jnp.where(lane == idx, tile, 0.0), axis=-1, keepdims=True)


FOX_T = 512


def _fox_heads(q, cc_ref, p):
    lane = lax.broadcasted_iota(jnp.int32, q.shape, 1)
    qs = q * (1.0 / math.sqrt(FOX_HEAD_DIM))
    qhs = [jnp.where((lane < FOX_HEAD_DIM) == (hh == 0), qs, jnp.zeros_like(qs)) for hh in range(2)]
    crefs = [_lane_pick(cc_ref[0, pl.ds(0, 1), :], 2 * p + hh) for hh in range(2)]
    return qhs, crefs


def _fold_lanes(x, op):
    out = x[:, :LANES]
    for j in range(1, x.shape[1] // LANES):
        out = op(out, x[:, j * LANES:(j + 1) * LANES])
    return out


def _causal(t, transposed):
    r = lax.broadcasted_iota(jnp.int32, (t, t), 0)
    c = lax.broadcasted_iota(jnp.int32, (t, t), 1)
    return (r <= c) if transposed else (c <= r)


QKV0 = 8


def fox_fwd(z, c_col, c_row, *, name, rider=None):
    B, S, _ = z.shape
    assert S % FOX_T == 0
    tq, nq = FOX_T, S // FOX_T
    npair = FOX_HEADS // 2

    def body(q_ref, k_ref, v_ref, cc_ref, cr_ref, o_ref, l_ref, ot_ref, s_scr, m_scr, acc_scr):
        p, qi = pl.program_id(1), pl.program_id(2)
        qhs, crefs = _fox_heads(q_ref[0], cc_ref, p)
        lane = lax.broadcasted_iota(jnp.int32, (tq, LANES), 1)
        first = lane < FOX_HEAD_DIM
        for hh in range(2):
            m_scr[hh] = jnp.full((tq, LANES), NEG, F32)
            acc_scr[hh] = jnp.zeros((tq, LANES), F32)

        def logits(kb, diagonal):
            k0 = pl.multiple_of(kb * tq, tq)
            k = k_ref[0, pl.ds(k0, tq), :]
            for hh in range(2):
                s = lax.dot_general(qhs[hh], k, NT, preferred_element_type=F32)
                s = s + (crefs[hh] - cr_ref[0, pl.ds(2 * p + hh, 1), pl.ds(k0, tq)])
                if diagonal:
                    s = jnp.where(_causal(tq, False), s, NEG)
                s_scr[hh, kb] = s
                m_scr[hh] = jnp.maximum(m_scr[hh], _fold_lanes(s, jnp.maximum))

        def sweep1(kb, carry):
            logits(kb, False)
            return carry

        lax.fori_loop(0, qi, sweep1, 0)
        logits(qi, True)
        ms = [jnp.max(m_scr[hh], axis=-1, keepdims=True) for hh in range(2)]
        mbs = [jnp.broadcast_to(ms[hh], (tq, tq)) for hh in range(2)]

        for hh in range(2):
            m_scr[hh] = jnp.zeros((tq, LANES), F32)

        def weigh(kb, carry):
            k0 = pl.multiple_of(kb * tq, tq)
            v = v_ref[0, pl.ds(k0, tq), :]
            for hh in range(2):
                pr = jnp.exp(s_scr[hh, kb] - mbs[hh])
                m_scr[hh] += _fold_lanes(pr, jnp.add)
                acc_scr[hh] += jnp.dot(pr.astype(BF16), v, preferred_element_type=F32)
            return carry

        lax.fori_loop(0, qi + 1, weigh, 0)
        accs = [acc_scr[hh] for hh in range(2)]
        ls = [jnp.sum(m_scr[hh], axis=-1, keepdims=True) for hh in range(2)]
        out = jnp.where(first, accs[0] / ls[0], accs[1] / ls[1])
        o_ref[0] = out.astype(BF16)
        ot_ref[...] = out.T.astype(BF16)
        l_ref[0, 0] = jnp.where(first, ms[0] + jnp.log(ls[0]), ms[1] + jnp.log(ls[1]))

    return hosted_call(
        body, rider, name=name, grid=(B, npair, nq),
        in_specs=[pl.BlockSpec((1, tq, LANES), lambda b, p, i: (b, i, QKV0 + p)),
                  pl.BlockSpec((1, S, LANES), lambda b, p, i: (b, 0, QKV0 + npair + p)),
                  pl.BlockSpec((1, S, LANES), lambda b, p, i: (b, 0, QKV0 + 2 * npair + p)),
                  pl.BlockSpec((1, tq, LANES), lambda b, p, i: (b, i, 0)),
                  pl.BlockSpec((1, 8, S), lambda b, p, i: (b, 0, 0))],
        out_specs=[pl.BlockSpec((1, tq, LANES), lambda b, p, i: (b, i, p)),
                   pl.BlockSpec((1, 1, tq, LANES), lambda b, p, i: (b, p, i, 0)),
                   pl.BlockSpec((LANES, tq), lambda b, p, i: (p, b * nq + i))],
        out_shape=[jax.ShapeDtypeStruct((B, S, FOX_W), BF16),
                   jax.ShapeDtypeStruct((B, npair, S, LANES), F32),
                   jax.ShapeDtypeStruct((FOX_W, B * S), BF16)],
        scratch_shapes=[pltpu.VMEM((2, nq, tq, tq), F32), pltpu.VMEM((2, tq, LANES), F32),
                        pltpu.VMEM((2, tq, LANES), F32)],
        args=(z, z, z, c_col, c_row),
    )


def fox_bwd_dq(z, dcat, lse, c_col, c_row, *, name, rider=None):
    B, S, _ = z.shape
    tq, nq = FOX_T, S // FOX_T
    npair = FOX_HEADS // 2

    def body(q_ref, k_ref, v_ref, do_ref, l_ref, cc_ref, cr_ref, dq_ref, st_ref, p_scr, dp_scr, dl_scr):
        p, qi = pl.program_id(1), pl.program_id(2)
        qhs, crefs = _fox_heads(q_ref[0], cc_ref, p)
        lane = lax.broadcasted_iota(jnp.int32, (tq, LANES), 1)
        do_b = do_ref[0].astype(BF16)
        dohs = [jnp.where((lane < FOX_HEAD_DIM) == (hh == 0), do_b, jnp.zeros_like(do_b)) for hh in range(2)]
        lses = [_lane_pick(l_ref[0, 0], hh * FOX_HEAD_DIM) for hh in range(2)]
        lbs = [jnp.broadcast_to(lses[hh], (tq, tq)) for hh in range(2)]
        for hh in range(2):
            dl_scr[hh] = jnp.zeros((tq, LANES), F32)

        def probs(kb, diagonal):
            k0 = pl.multiple_of(kb * tq, tq)
            k = k_ref[0, pl.ds(k0, tq), :]
            v = v_ref[0, pl.ds(k0, tq), :]
            for hh in range(2):
                s = lax.dot_general(qhs[hh], k, NT, preferred_element_type=F32)
                s = s + (crefs[hh] - cr_ref[0, pl.ds(2 * p + hh, 1), pl.ds(k0, tq)])
                pr = jnp.exp(s - lbs[hh])
                if diagonal:
                    pr = jnp.where(_causal(tq, False), pr, 0.0)
                dp = lax.dot_general(dohs[hh], v, NT, preferred_element_type=F32)
                pdp = pr * dp
                dl_scr[hh] += _fold_lanes(pdp, jnp.add)
                p_scr[hh, kb] = pr
                dp_scr[hh, kb] = dp

        def first_pass(kb, carry):
            probs(kb, False)
            return carry

        lax.fori_loop(0, qi, first_pass, 0)
        probs(qi, True)

        dls = [jnp.sum(dl_scr[hh], axis=-1, keepdims=True) for hh in range(2)]
        dlbs = [jnp.broadcast_to(dls[hh], (tq, tq)) for hh in range(2)]

        def second_pass(kb, dq):
            k0 = pl.multiple_of(kb * tq, tq)
            k = k_ref[0, pl.ds(k0, tq), :]
            for hh in range(2):
                ds = p_scr[hh, kb] * (dp_scr[hh, kb] - dlbs[hh])
                kh = jnp.where((lane < FOX_HEAD_DIM) == (hh == 0), k, jnp.zeros_like(k))
                dq = dq + jnp.dot(ds.astype(BF16), kh, preferred_element_type=F32)
            return dq

        dq = lax.fori_loop(0, qi + 1, second_pass, jnp.zeros((tq, LANES), F32))
        dq_ref[0] = (dq * (1.0 / math.sqrt(FOX_HEAD_DIM))).astype(BF16)
        cols = jnp.zeros((tq, LANES), F32)
        for j, col in enumerate([crefs[0] - lses[0], crefs[1] - lses[1], dls[0], dls[1]]):
            cols = jnp.where(lane == j, col, cols)
        st_ref[0, 0] = _dot_hi(_eye(LANES), cols, NT)[:8]

    return hosted_call(
        body, rider, name=name, grid=(B, npair, nq),
        in_specs=[pl.BlockSpec((1, tq, LANES), lambda b, p, i: (b, i, QKV0 + p)),
                  pl.BlockSpec((1, S, LANES), lambda b, p, i: (b, 0, QKV0 + npair + p)),
                  pl.BlockSpec((1, S, LANES), lambda b, p, i: (b, 0, QKV0 + 2 * npair + p)),
                  pl.BlockSpec((1, tq, LANES), lambda b, p, i: (b, i, npair + p)),
                  pl.BlockSpec((1, 1, tq, LANES), lambda b, p, i: (b, p, i, 0)),
                  pl.BlockSpec((1, tq, LANES), lambda b, p, i: (b, i, 0)),
                  pl.BlockSpec((1, 8, S), lambda b, p, i: (b, 0, 0))],
        out_specs=[pl.BlockSpec((1, tq, LANES), lambda b, p, i: (b, i, p)),
                   pl.BlockSpec((1, 1, 8, tq), lambda b, p, i: (b, p, 0, i))],
        out_shape=[jax.ShapeDtypeStruct((B, S, FOX_W), BF16), jax.ShapeDtypeStruct((B, npair, 8, S), F32)],
        scratch_shapes=[pltpu.VMEM((2, nq, tq, tq), F32), pltpu.VMEM((2, nq, tq, tq), F32),
                        pltpu.VMEM((2, tq, LANES), F32)],
        args=(z, z, z, dcat, lse, c_col, c_row), vmem=56 << 20,
    )


def fox_bwd_dkdv(z, dcat, stats, c_col, *, name, rider=None):
    B, S, _ = z.shape
    tk, nq = FOX_T, S // FOX_T
    npair = FOX_HEADS // 2
    inv = 1.0 / math.sqrt(FOX_HEAD_DIM)

    def body(q_ref, k_ref, v_ref, do_ref, st_ref, cc_ref, dk_ref, dv_ref, dc_ref, dk_scr, dv_scr, dc_scr):
        p, kt = pl.program_id(1), pl.program_id(2)
        lane = lax.broadcasted_iota(jnp.int32, (tk, LANES), 1)
        masks = [(lane < FOX_HEAD_DIM) == (hh == 0) for hh in range(2)]
        k = k_ref[0]
        v = v_ref[0]
        khs = [jnp.where(masks[hh], k, jnp.zeros_like(k)) for hh in range(2)]
        vhs = [jnp.where(masks[hh], v, jnp.zeros_like(v)) for hh in range(2)]
        ccbs = [jnp.broadcast_to(_lane_pick(cc_ref[0], 2 * p + hh), (tk, tk)) for hh in range(2)]
        dk_scr[...] = jnp.zeros_like(dk_scr)
        dv_scr[...] = jnp.zeros_like(dv_scr)
        dc_scr[...] = jnp.zeros_like(dc_scr)

        def tile(qb, diagonal):
            q0 = pl.multiple_of(qb * tk, tk)
            qs = q_ref[0, pl.ds(q0, tk), :] * inv
            do_b = do_ref[0, pl.ds(q0, tk), :].astype(BF16)
            for hh in range(2):
                st = lax.dot_general(khs[hh], qs, NT, preferred_element_type=F32)
                pr = jnp.exp(st - ccbs[hh] + st_ref[0, 0, pl.ds(hh, 1), pl.ds(q0, tk)])
                if diagonal:
                    pr = jnp.where(_causal(tk, True), pr, 0.0)
                dp = lax.dot_general(vhs[hh], do_b, NT, preferred_element_type=F32)
                ds = pr * (dp - st_ref[0, 0, pl.ds(2 + hh, 1), pl.ds(q0, tk)])
                dv_scr[...] += jnp.dot(pr.astype(BF16), jnp.where(masks[hh], do_b, jnp.zeros_like(do_b)),
                                       preferred_element_type=F32)
                dk_scr[...] += jnp.dot(ds.astype(BF16), jnp.where(masks[hh], qs, jnp.zeros_like(qs)),
                                       preferred_element_type=F32)
                dc_scr[hh] -= _fold_lanes(ds, jnp.add)

        def later(qb, carry):
            tile(qb, False)
            return carry

        tile(kt, True)
        lax.fori_loop(kt + 1, nq, later, 0)
        dk_ref[0] = dk_scr[...].astype(BF16)
        dv_ref[0] = dv_scr[...].astype(BF16)
        dcs = [jnp.sum(dc_scr[hh], axis=-1, keepdims=True) for hh in range(2)]
        dc_ref[0, 0] = jnp.where(lane == 2 * p, dcs[0], jnp.where(lane == 2 * p + 1, dcs[1], 0.0))

    full = lambda col: pl.BlockSpec((1, S, LANES), col)
    tile_spec = lambda col: pl.BlockSpec((1, tk, LANES), col)
    return hosted_call(
        body, rider, name=name, grid=(B, npair, nq),
        in_specs=[full(lambda b, p, t: (b, 0, QKV0 + p)),
                  tile_spec(lambda b, p, t: (b, t, QKV0 + npair + p)),
                  tile_spec(lambda b, p, t: (b, t, QKV0 + 2 * npair + p)),
                  full(lambda b, p, t: (b, 0, npair + p)),
                  pl.BlockSpec((1, 1, 8, S), lambda b, p, t: (b, p, 0, 0)),
                  tile_spec(lambda b, p, t: (b, t, 0))],
        out_specs=[tile_spec(lambda b, p, t: (b, t, p)), tile_spec(lambda b, p, t: (b, t, p)),
                   pl.BlockSpec((1, 1, tk, LANES), lambda b, p, t: (b, p, t, 0))],
        out_shape=[jax.ShapeDtypeStruct((B, S, FOX_W), BF16)] * 2
        + [jax.ShapeDtypeStruct((B, npair, S, LANES), F32)],
        scratch_shapes=[pltpu.VMEM((tk, LANES), F32), pltpu.VMEM((tk, LANES), F32),
                        pltpu.VMEM((2, tk, LANES), F32)],
        args=(z, z, z, dcat, stats, c_col),
    )


def xattn_fwd(qm, kv, *, name, tq=512):
    B, S, D = qm.shape
    M = kv.shape[1]
    tq = min(tq, S)
    inv = 1.0 / math.sqrt(MEM_HEAD_DIM)

    nq = S // tq

    def body(q_ref, kv_ref, o_ref, ot_ref):
        for h in range(MEM_HEADS):
            c0 = h * MEM_HEAD_DIM
            qh = q_ref[0, :, c0:c0 + MEM_HEAD_DIM]
            kh = kv_ref[0, :, c0:c0 + MEM_HEAD_DIM]
            vh = kv_ref[0, :, D + c0:D + c0 + MEM_HEAD_DIM]
            s = lax.dot_general(qh, kh, NT, preferred_element_type=F32) * inv
            e = jnp.exp(s - jnp.max(s, axis=-1, keepdims=True))
            o = jnp.dot(e.astype(BF16), vh, preferred_element_type=F32) / jnp.sum(e, axis=-1, keepdims=True)
            o_ref[0, :, c0:c0 + MEM_HEAD_DIM] = o.astype(BF16)
            ot_ref[c0:c0 + MEM_HEAD_DIM, :] = o.T.astype(BF16)

    return _call(
        body, name=name, grid=(B, nq),
        in_specs=[pl.BlockSpec((1, tq, D), lambda b, i: (b, i, 0)),
                  pl.BlockSpec((1, M, 2 * D), lambda b, i: (b, 0, 0))],
        out_specs=[pl.BlockSpec((1, tq, D), lambda b, i: (b, i, 0)),
                   pl.BlockSpec((D, tq), lambda b, i: (0, b * nq + i))],
        out_shape=[jax.ShapeDtypeStruct((B, S, D), BF16), jax.ShapeDtypeStruct((D, B * S), BF16)],
        compiler_params=_params(("parallel", "parallel")),
    )(qm, kv)


def xattn_bwd(qm, kv, do, *, name, tq=512):
    B, S, D = qm.shape
    M = kv.shape[1]
    tq = min(tq, S)
    inv = 1.0 / math.sqrt(MEM_HEAD_DIM)

    def body(q_ref, kv_ref, do_ref, dq_ref, dkv_ref):
        @pl.when(pl.program_id(1) == 0)
        def _():
            dkv_ref[...] = jnp.zeros_like(dkv_ref)

        for h in range(MEM_HEADS):
            c0 = h * MEM_HEAD_DIM
            qh = q_ref[0, :, c0:c0 + MEM_HEAD_DIM]
            kh = kv_ref[0, :, c0:c0 + MEM_HEAD_DIM]
            vh = kv_ref[0, :, D + c0:D + c0 + MEM_HEAD_DIM]
            doh = do_ref[0, :, c0:c0 + MEM_HEAD_DIM]
            s = lax.dot_general(qh, kh, NT, preferred_element_type=F32) * inv
            e = jnp.exp(s - jnp.max(s, axis=-1, keepdims=True))
            pr = e / jnp.sum(e, axis=-1, keepdims=True)
            dp = lax.dot_general(doh, vh, NT, preferred_element_type=F32)
            ds = pr * (dp - jnp.sum(pr * dp, axis=-1, keepdims=True))
            ds_b = ds.astype(BF16)
            dq_ref[0, :, c0:c0 + MEM_HEAD_DIM] = (jnp.dot(ds_b, kh, preferred_element_type=F32) * inv).astype(BF16)
            dkv_ref[0, :, c0:c0 + MEM_HEAD_DIM] += lax.dot_general(ds_b, qh, TN, preferred_element_type=F32) * inv
            dkv_ref[0, :, D + c0:D + c0 + MEM_HEAD_DIM] += lax.dot_general(
                pr.astype(BF16), doh, TN, preferred_element_type=F32)

    row = pl.BlockSpec((1, tq, D), lambda b, i: (b, i, 0))
    kvs = pl.BlockSpec((1, M, 2 * D), lambda b, i: (b, 0, 0))
    return _call(
        body, name=name, grid=(B, S // tq), in_specs=[row, kvs, row], out_specs=[row, kvs],
        out_shape=[jax.ShapeDtypeStruct((B, S, D), BF16), jax.ShapeDtypeStruct((B, M, 2 * D), F32)],
        compiler_params=_params(("parallel", "arbitrary")),
    )(qm, kv, do)


SWIGLU_TN = 1408


def _chunks(n, w=256):
    return [(c0, min(w, n - c0)) for c0 in range(0, n, w)]


def mm_swiglu_fwd(hf, w_gu, *, name, tm=512):
    T, D = hf.shape
    Fh = w_gu.shape[1] // 2
    tm, tn = min(tm, T), SWIGLU_TN
    nj = Fh // tn
    assert Fh % tn == 0 and T % tm == 0

    def body(a_ref, bg_ref, bu_ref, g_ref, u_ref, o_ref, ot_ref):
        a = a_ref[...]
        for c0, cw in _chunks(tn):
            cols = pl.ds(c0, cw)
            g = jnp.dot(a, bg_ref[:, cols], preferred_element_type=F32)
            u = jnp.dot(a, bu_ref[:, cols], preferred_element_type=F32)
            act = g * _sigmoid(g) * u
            g_ref[:, cols] = g.astype(BF16)
            u_ref[:, cols] = u.astype(BF16)
            o_ref[:, cols] = act.astype(BF16)
            ot_ref[cols, :] = act.T.astype(BF16)

    tile = pl.BlockSpec((tm, tn), lambda i, j: (i, j))
    return _call(
        body, name=name, grid=(T // tm, nj),
        in_specs=[pl.BlockSpec((tm, D), lambda i, j: (i, 0)), pl.BlockSpec((D, tn), lambda i, j: (0, j)),
                  pl.BlockSpec((D, tn), lambda i, j: (0, nj + j))],
        out_specs=[tile, tile, tile, pl.BlockSpec((tn, tm), lambda i, j: (j, i))],
        out_shape=[jax.ShapeDtypeStruct((T, Fh), BF16)] * 3 + [jax.ShapeDtypeStruct((Fh, T), BF16)],
        compiler_params=_params(("parallel", "parallel"), 48 << 20),
    )(hf, w_gu, w_gu)


def mm_swiglu_bwd(dx, w_down, g, u, *, name, tm=512):
    T, D = dx.shape
    Fh = w_down.shape[0]
    tm, tn = min(tm, T), SWIGLU_TN
    assert Fh % tn == 0 and T % tm == 0

    def body(a_ref, b_ref, g_ref, u_ref, dg_ref, du_ref):
        a = a_ref[...].astype(BF16)
        for c0, cw in _chunks(tn):
            cols = pl.ds(c0, cw)
            d = lax.dot_general(a, b_ref[cols, :], NT, preferred_element_type=F32)
            gv = g_ref[:, cols].astype(F32)
            uv = u_ref[:, cols].astype(F32)
            sg = _sigmoid(gv)
            dg_ref[:, cols] = (d * uv * (sg * (1.0 + gv * (1.0 - sg)))).astype(BF16)
            du_ref[:, cols] = (d * gv * sg).astype(BF16)

    tile = pl.BlockSpec((tm, tn), lambda i, j: (i, j))
    return _call(
        body, name=name, grid=(T // tm, Fh // tn),
        in_specs=[pl.BlockSpec((tm, D), lambda i, j: (i, 0)), pl.BlockSpec((tn, D), lambda i, j: (j, 0)), tile, tile],
        out_specs=[tile, tile],
        out_shape=[jax.ShapeDtypeStruct((T, Fh), BF16)] * 2,
        compiler_params=_params(("parallel", "parallel"), 48 << 20),
    )(dx, w_down, g, u)


LATE_MID = ("w_out", "w_mq", "w_mo")
LATE_KV = ("w_mkv",)
LATE_FFN = ("w_gu", "w_down")
LATE = LATE_MID + LATE_KV + LATE_FFN
RS_GROUPS = (("w_gu", "w_down"), ("w_out", "w_mq", "w_mkv", "w_mo"), ("w_in",))


def pair_sums(names, g42, got):
    return {n: pair_sum(g, o, name="rs_pair_sum_" + n) for n, g, o in zip(names, g42, got)}


def local_step(x, mem, target, sp, first_shards, late_shards):
    B, S, D = x.shape
    T = B * S
    M = mem.shape[1]
    row = lambda v: v.reshape(1, -1).astype(F32)
    g_mix, g_x, g_mem, g_ffn, g_final = (row(sp[k]) for k in ("g_mix", "g_x", "g_mem", "g_ffn", "g_final"))
    conv_b, ln_g, ln_b = row(sp["conv_b"]), row(sp["ln_g"]), row(sp["ln_b"])
    b_f = jnp.pad(row(sp["b_f"]), ((0, 0), (0, LANES - FOX_HEADS)))
    n_ug, n_main = 2 * CONV_CH, 2 * CONV_CH + 3 * FOX_W

    x2d = x.reshape(T, D)
    h, h_t, partly = rmsnorm_fwd(x2d, g_mix, name="rms_mix", rider=AllGatherStage1(first_shards))
    w_in8, cw8 = run_rider(AllGatherStage2(partly), name="ag_first_stage2")
    w_in_full = _full_from_gathered("w_in", w_in8)
    conv_w = cw8.transpose(1, 0, 2).reshape(HALO, -1)
    w_main, w_ug, w_qkv = w_in_full[:, :n_main], w_in_full[:, :n_ug], w_in_full[:, n_ug:n_main]
    w_f = jnp.pad(w_in_full[:, n_main:], ((0, 0), (0, LANES - FOX_HEADS)))
    z = matmul(h, w_main, out_dtype=BF16, tn=n_main, name="mm_in")
    z3 = z.reshape(B, S, n_main)
    n_mid, n_kv = len(LATE_MID), len(LATE_MID) + len(LATE_KV)
    (conv_out, conv_t), partly_mid = conv_branch_fwd(z3, conv_w, conv_b, ln_g, ln_b, name="conv_fwd",
                                                     rider=AllGatherStage1(late_shards[:n_mid]))
    (f_raw, c_col, c_row), rode = fgate_fwd(
        h.reshape(B, S, D), w_f, b_f, name="fgate_fwd",
        rider=Riders(AllGatherStage1(late_shards[n_mid:n_kv]), AllGatherStage2(partly_mid)))
    partly_kv, full_mid = rode[:n_kv - n_mid], rode[n_kv - n_mid:]
    (att, lse, att_t), rode = fox_fwd(
        z3, c_col, c_row, name="fox_fwd",
        rider=Riders(AllGatherStage1(late_shards[n_kv:]), AllGatherStage2(partly_kv)))
    partly_ffn, full_kv = rode[:len(LATE_FFN)], rode[len(LATE_FFN):]
    wf = {n: _full_from_gathered(n, blk) for n, blk in zip(LATE_MID + LATE_KV, full_mid + full_kv)}
    x1, full_ffn = matmul([conv_out.reshape(T, CONV_CH), att.reshape(T, FOX_W)], [wf["w_out"], wf["w_out"]],
                          b_blk=[0, 1], out_dtype=F32, res=x2d, tn=D, name="mm_out",
                          rider=AllGatherStage2(partly_ffn))
    wf.update({n: _full_from_gathered(n, blk) for n, blk in zip(LATE_FFN, full_ffn)})
    hx, hx_t = rmsnorm_fwd(x1, g_x, name="rms_x")
    qm = matmul(hx, wf["w_mq"], out_dtype=BF16, tn=D, name="mm_mq")
    mem2d = mem.reshape(B * M, D)
    mem_n, mem_n_t = rmsnorm_fwd(mem2d, g_mem, name="rms_mem")
    kv = matmul(mem_n, wf["w_mkv"], out_dtype=BF16, tn=2 * D, name="mm_mkv").reshape(B, M, 2 * D)
    o, o_t = xattn_fwd(qm.reshape(B, S, D), kv, name="xattn_fwd")
    o = o.reshape(T, D)
    x2 = matmul(o, wf["w_mo"], out_dtype=F32, res=x1, tn=D, name="mm_mo")
    hf, hf_t = rmsnorm_fwd(x2, g_ffn, name="rms_ffn")
    gate, up, act, act_t = mm_swiglu_fwd(hf, wf["w_gu"], name="mm_gu")
    x3 = matmul(act, wf["w_down"], out_dtype=F32, res=x2, tn=D, name="mm_down")
    dx3, dg_final, loss = final_loss_bwd(x3, g_final, target.reshape(T, D), name="loss_bwd")
    gw = {}
    gw["w_down"] = matmul(act_t, dx3, out_dtype=BF16, tm=1408, tn=512, name="dw_down")
    dgate, dup = mm_swiglu_bwd(dx3, wf["w_down"], gate, up, name="dx_down")
    gw["w_gu"] = [matmul(hf_t, dgate, out_dtype=BF16, tn=SWIGLU_TN, name="dw_gate"),
                  matmul(hf_t, dup, out_dtype=BF16, tn=SWIGLU_TN, name="dw_up")]
    g42 = [_shards_from_full(n, gw[n]) for n in RS_GROUPS[0]]
    dhf, got = matmul([dgate, dup], [wf["w_gu"], wf["w_gu"]], b_blk=[0, 1], tb=True, out_dtype=BF16,
                      tm=256, tn=D, name="dx_gu", rider=SiblingExchange(g42))
    parts = pair_sums(RS_GROUPS[0], g42, got)
    dx2, dg_ffn = rmsnorm_bwd(x2, g_ffn, dhf, dx3, name="rms_ffn_bwd")
    gw["w_mo"] = matmul(o_t, dx2, out_dtype=BF16, tn=D, name="dw_mo")
    do = matmul(dx2, wf["w_mo"], tb=True, out_dtype=BF16, tn=D, name="dx_mo")
    dqm, dkv = xattn_bwd(qm.reshape(B, S, D), kv, do.reshape(B, S, D), name="xattn_bwd")
    dqm = dqm.reshape(T, D)
    dkv = dkv.reshape(B * M, 2 * D)
    gw["w_mq"] = matmul(hx_t, dqm, out_dtype=BF16, tn=D, name="dw_mq")
    dhx = matmul(dqm, wf["w_mq"], tb=True, out_dtype=BF16, tn=D, name="dx_mq")
    gw["w_mkv"] = matmul(mem_n_t, dkv, out_dtype=BF16, tn=D, name="dw_mkv")
    dmem_n = matmul(dkv, wf["w_mkv"], tb=True, out_dtype=BF16, tn=D, name="dx_mkv")
    _, dg_mem = rmsnorm_bwd(mem2d, g_mem, dmem_n, None, name="rms_mem_bwd")
    dx1, dg_x = rmsnorm_bwd(x1, g_x, dhx, dx2, name="rms_x_bwd")
    gw["w_out"] = jnp.concatenate([matmul(conv_t, dx1, out_dtype=BF16, tn=D, name="dw_out_conv"),
                                   matmul(att_t, dx1, out_dtype=BF16, tn=D, name="dw_out_att")], axis=0)
    g42 = [_shards_from_full(n, gw[n]) for n in RS_GROUPS[1]]
    dcat, got = matmul(dx1, wf["w_out"], tb=True, out_dtype=BF16, tn=D, name="dx_out", rider=SiblingExchange(g42))
    dcat = dcat.reshape(B, S, D)
    parts.update(pair_sums(RS_GROUPS[1], g42, got))
    dy, dconv_w, dvec = conv_branch_bwd_a(z3, dcat, conv_w, conv_b, ln_g, ln_b, name="conv_bwd_a")
    dug = conv_branch_bwd_b(z3, dy, conv_w, name="conv_bwd_b")
    gots = {}
    (dq, stats), got = fox_bwd_dq(z3, dcat, lse, c_col, c_row, name="fox_bwd_dq",
                                  rider=ChipExchange([parts[n] for n in RS_GROUPS[0]]))
    gots.update(zip(RS_GROUPS[0], got))
    (dk, dv, dc), got = fox_bwd_dkdv(z3, dcat, stats, c_col, name="fox_bwd_dkdv",
                                     rider=ChipExchange([parts[n] for n in RS_GROUPS[1]]))
    gots.update(zip(RS_GROUPS[1], got))
    df, db_f = fgate_bwd(dc, f_raw, b_f, name="fgate_bwd")
    dug2 = dug.reshape(T, n_ug)
    dqkv = jnp.concatenate([dq, dk, dv], axis=-1).reshape(T, 3 * FOX_W)
    df2 = df.reshape(T, LANES)
    dw_in = [matmul(h_t, dug2, out_dtype=BF16, tn=n_ug, name="dw_in_ug"),
             matmul(h_t, dqkv, out_dtype=BF16, tn=3 * FOX_W, name="dw_in_qkv"),
             matmul(h_t, df2, out_dtype=BF16, name="dw_f")[:, :FOX_HEADS]]
    g42 = [_shards_from_full("w_in", dw_in)]
    parts.update(pair_sums(RS_GROUPS[2], g42, run_rider(SiblingExchange(g42), name="rs_sibling_in")))
    dh, (gots["w_in"],) = matmul([dug2, dqkv, df2], [w_ug, w_qkv, w_f], tb=True, out_dtype=BF16, tn=D,
                                 name="dx_in", rider=ChipExchange([parts["w_in"]]))
    dx, dg_mix = rmsnorm_bwd(x2d, g_mix, dh, dx1, name="rms_mix_bwd", out_dtype=F32)
    gs = dict(g_mix=dg_mix, b_f=db_f[:, :FOX_HEADS], conv_w=dconv_w[:CONV_K], conv_b=dvec[0:1],
              ln_g=dvec[1:2], ln_b=dvec[2:3], g_x=dg_x, g_mem=dg_mem, g_ffn=dg_ffn, g_final=dg_final)
    return loss, dx.reshape(B, S, D), gs, {n: (parts[n], gots[n]) for n in BIG}


def _me():
    return lax.axis_index("x"), lax.axis_index("y"), lax.axis_index("c")


def _any_specs(n):
    return [pl.BlockSpec(memory_space=pl.ANY)] * n


def all_gather(xs, *, name):
    n = len(xs)

    def body(*refs):
        x_refs, out_refs = refs[:n], refs[n:2 * n]
        send_sems, recv_sems, local_sems = refs[2 * n:]
        x, y, c = _me()
        me, sibling = (x, y, c), (x, y, 1 - c)
        chips = [(1 - x, y), (x, 1 - y), (1 - x, 1 - y)]

        def slot(a, px, py, pc):
            return out_refs[a].at[4 * px + 2 * py + pc]

        def copy(a, k, block, to, own=False):
            return pltpu.make_async_remote_copy(
                src_ref=x_refs[a] if own else slot(a, *block), dst_ref=slot(a, *block),
                send_sem=send_sems.at[k, a], recv_sem=recv_sems.at[k, a], device_id=to, device_id_type=MESH)

        mine = [pltpu.make_async_copy(x_refs[a], slot(a, *me), local_sems.at[a]) for a in range(n)]
        first = [copy(a, 0, me, sibling, own=True) for a in range(n)]
        first += [copy(a, 1 + j, me, (*chip, c), own=True) for j, chip in enumerate(chips) for a in range(n)]
        for cp in mine + first:
            cp.start()
        passed = []
        for j, chip in enumerate(chips):
            for a in range(n):
                copy(a, 1 + j, (*chip, c), me).wait_recv()
                passed.append(copy(a, 4 + j, (*chip, c), sibling))
                passed[-1].start()
        for a in range(n):
            copy(a, 0, sibling, me).wait_recv()
            for j, chip in enumerate(chips):
                copy(a, 4 + j, (*chip, 1 - c), me).wait_recv()
        for cp in first + passed:
            cp.wait_send()
        for cp in mine:
            cp.wait()

    return _call(
        body, name=name, in_specs=_any_specs(n), out_specs=_any_specs(n),
        out_shape=[jax.ShapeDtypeStruct((N_DEV,) + v.shape, v.dtype) for v in xs],
        scratch_shapes=[pltpu.SemaphoreType.DMA((7, n)), pltpu.SemaphoreType.DMA((7, n)),
                        pltpu.SemaphoreType.DMA((n,))],
    )(*xs)


SIBLING_BARRIER = 1
CHIPS_BARRIER = 2
GATHER_BARRIER = 3


class SiblingExchange:
    collective_id = SIBLING_BARRIER

    def __init__(self, gs):
        n = len(gs)
        self.n, self.inputs = n, list(gs)
        self.out_shape = [jax.ShapeDtypeStruct((4,) + g.shape[2:], g.dtype) for g in gs]
        self.scratch = [pltpu.SemaphoreType.DMA((n,)), pltpu.SemaphoreType.DMA((n,))]

    @staticmethod
    def barrier_peers():
        x, y, c = _me()
        return [(x, y, 1 - c)]

    def _copies(self, g_refs, out_refs, sems):
        send_sems, recv_sems = sems
        x, y, c = _me()
        return [pltpu.make_async_remote_copy(
            src_ref=g_refs[a].at[:, 1 - c], dst_ref=out_refs[a], send_sem=send_sems.at[a],
            recv_sem=recv_sems.at[a], device_id=(x, y, 1 - c), device_id_type=MESH) for a in range(self.n)]

    def start(self, in_refs, out_refs, sems):
        for cp in self._copies(in_refs, out_refs, sems):
            cp.start()

    def finish(self, in_refs, out_refs, sems):
        for cp in self._copies(in_refs, out_refs, sems):
            cp.wait()


def run_rider(rider, *, name):
    return hosted_call(None, rider, name=name, grid=(), in_specs=[], out_specs=[], out_shape=[],
                       scratch_shapes=[], args=[])[1]


class ChipExchange:
    collective_id = CHIPS_BARRIER

    @staticmethod
    def barrier_peers():
        x, y, c = _me()
        return [(1 - x, y, c), (x, 1 - y, c), (1 - x, 1 - y, c)]

    def __init__(self, ps):
        n = len(ps)
        self.n, self.inputs = n, list(ps)
        self.out_shape = [jax.ShapeDtypeStruct(p.shape, p.dtype) for p in ps]
        self.scratch = [pltpu.SemaphoreType.DMA((3, n)), pltpu.SemaphoreType.DMA((3, n))]

    def _copies(self, p_refs, out_refs, sems, outgoing):
        send_sems, recv_sems = sems
        x, y, c = _me()
        my_chip = 2 * x + y
        cps = []
        for k in range(3):
            px, py = x ^ ((k + 1) >> 1), y ^ ((k + 1) & 1)
            src, dst = (2 * px + py, my_chip) if outgoing else (my_chip, 2 * px + py)
            for a in range(self.n):
                cps.append(pltpu.make_async_remote_copy(
                    src_ref=p_refs[a].at[src], dst_ref=out_refs[a].at[dst], send_sem=send_sems.at[k, a],
                    recv_sem=recv_sems.at[k, a], device_id=(px, py, c), device_id_type=MESH))
        return cps

    def start(self, in_refs, out_refs, sems):
        for cp in self._copies(in_refs, out_refs, sems, True):
            cp.start()

    def finish(self, in_refs, out_refs, sems):
        for cp in self._copies(in_refs, out_refs, sems, False):
            cp.wait_recv()
        for cp in self._copies(in_refs, out_refs, sems, True):
            cp.wait_send()


class AllGatherStage1:
    collective_id = GATHER_BARRIER

    @staticmethod
    def barrier_peers():
        x, y, c = _me()
        return [(x, y, 1 - c), (1 - x, y, c), (x, 1 - y, c), (1 - x, 1 - y, c)]

    def __init__(self, xs):
        n = len(xs)
        self.n, self.inputs = n, list(xs)
        self.out_shape = [jax.ShapeDtypeStruct((N_DEV,) + v.shape, v.dtype) for v in xs]
        self.scratch = [pltpu.SemaphoreType.DMA((4, n)), pltpu.SemaphoreType.DMA((4, n)),
                        pltpu.SemaphoreType.DMA((n,))]

    def _copies(self, x_refs, out_refs, sems, kind):
        send_sems, recv_sems, local_sems = sems
        x, y, c = _me()
        slot = lambda a, d: out_refs[a].at[4 * d[0] + 2 * d[1] + d[2]]
        if kind == "local":
            return [pltpu.make_async_copy(x_refs[a], slot(a, (x, y, c)), local_sems.at[a]) for a in range(self.n)]
        cps = []
        for k, peer in enumerate([(x, y, 1 - c), (1 - x, y, c), (x, 1 - y, c), (1 - x, 1 - y, c)]):
            for a in range(self.n):
                cps.append(pltpu.make_async_remote_copy(
                    src_ref=x_refs[a], dst_ref=slot(a, (x, y, c) if kind == "out" else peer),
                    send_sem=send_sems.at[k, a], recv_sem=recv_sems.at[k, a], device_id=peer, device_id_type=MESH))
        return cps

    def start(self, in_refs, out_refs, sems):
        for cp in self._copies(in_refs, out_refs, sems, "local") + self._copies(in_refs, out_refs, sems, "out"):
            cp.start()

    def finish(self, in_refs, out_refs, sems):
        for cp in self._copies(in_refs, out_refs, sems, "in"):
            cp.wait_recv()
        for cp in self._copies(in_refs, out_refs, sems, "out"):
            cp.wait_send()
        for cp in self._copies(in_refs, out_refs, sems, "local"):
            cp.wait()


class AllGatherStage2:
    collective_id = SIBLING_BARRIER

    @staticmethod
    def barrier_peers():
        x, y, c = _me()
        return [(x, y, 1 - c)]

    def __init__(self, outs):
        n = len(outs)
        self.n, self.inputs = n, list(outs)
        self.out_shape = [jax.ShapeDtypeStruct(o.shape, o.dtype) for o in outs]
        self.scratch = [pltpu.SemaphoreType.DMA((3, n)), pltpu.SemaphoreType.DMA((3, n))]
        self.aliases = {a: a for a in range(n)}

    def _copies(self, out_refs, sems, outgoing):
        send_sems, recv_sems = sems
        x, y, c = _me()
        cps = []
        for k, (px, py) in enumerate([(1 - x, y), (x, 1 - y), (1 - x, 1 - y)]):
            for a in range(self.n):
                cps.append(pltpu.make_async_remote_copy(
                    src_ref=out_refs[a].at[4 * px + 2 * py + c],
                    dst_ref=out_refs[a].at[4 * px + 2 * py + (c if outgoing else 1 - c)],
                    send_sem=send_sems.at[k, a], recv_sem=recv_sems.at[k, a], device_id=(x, y, 1 - c),
                    device_id_type=MESH))
        return cps

    def start(self, in_refs, out_refs, sems):
        for cp in self._copies(out_refs, sems, True):
            cp.start()

    def finish(self, in_refs, out_refs, sems):
        for cp in self._copies(out_refs, sems, False):
            cp.wait_recv()
        for cp in self._copies(out_refs, sems, True):
            cp.wait_send()


class Riders:
    def __init__(self, *riders):
        self.riders = riders
        self.collective_id = riders[0].collective_id
        self.barrier_peers = riders[0].barrier_peers
        self.inputs = [v for r in riders for v in r.inputs]
        self.out_shape = [s for r in riders for s in r.out_shape]
        self.scratch = [s for r in riders for s in r.scratch]
        self.aliases, i0, o0 = {}, 0, 0
        for r in riders:
            self.aliases.update({i0 + i: o0 + o for i, o in getattr(r, "aliases", {}).items()})
            i0, o0 = i0 + len(r.inputs), o0 + len(r.out_shape)

    def _split(self, in_refs, out_refs, sems):
        i0 = o0 = s0 = 0
        for r in self.riders:
            ni, no, ns = len(r.inputs), len(r.out_shape), len(r.scratch)
            yield r, in_refs[i0:i0 + ni], out_refs[o0:o0 + no], sems[s0:s0 + ns]
            i0, o0, s0 = i0 + ni, o0 + no, s0 + ns

    def start(self, in_refs, out_refs, sems):
        for r, i, o, s in self._split(in_refs, out_refs, sems):
            r.start(i, o, s)

    def finish(self, in_refs, out_refs, sems):
        for r, i, o, s in self._split(in_refs, out_refs, sems):
            r.finish(i, o, s)


def _peer_barrier(peers):
    barrier = pltpu.get_barrier_semaphore()
    for peer in peers:
        pl.semaphore_signal(barrier, inc=1, device_id=peer, device_id_type=MESH)
    pl.semaphore_wait(barrier, len(peers))


def hosted_call(body, rider, *, name, grid, in_specs, out_specs, out_shape, scratch_shapes, args, vmem=None):
    n_in, n_out, n_scr = len(in_specs), len(out_specs), len(scratch_shapes)
    r_in, r_out = (len(rider.inputs), len(rider.out_shape)) if rider is not None else (0, 0)
    own_barrier = getattr(rider, "collective_id", None) is not None

    def wrapped(*refs):
        ins, refs = refs[:n_in], refs[n_in:]
        rins, refs = refs[:r_in], refs[r_in:]
        outs, refs = refs[:n_out], refs[n_out:]
        routs, refs = refs[:r_out], refs[r_out:]
        scr, rscr = refs[:n_scr], refs[n_scr:]
        ids = [pl.program_id(d) for d in range(len(grid))]
        first = functools.reduce(jnp.logical_and, [i == 0 for i in ids], True)
        last = functools.reduce(jnp.logical_and, [i == g - 1 for i, g in zip(ids, grid)], True)

        def begin():
            if own_barrier:
                _peer_barrier(rider.barrier_peers())
            rider.start(rins, routs, rscr)

        if rider is not None and grid:
            pl.when(first)(begin)
        elif rider is not None:
            begin()
        if body is not None:
            body(*ins, *outs, *scr)
        if rider is not None and grid:
            pl.when(last)(lambda: rider.finish(rins, routs, rscr))
        elif rider is not None:
            rider.finish(rins, routs, rscr)

    kw = dict(grid=grid) if grid else {}
    aliases = getattr(rider, "aliases", {})
    if aliases:
        kw["input_output_aliases"] = {n_in + i: n_out + o for i, o in aliases.items()}
    if grid or vmem is not None or own_barrier:
        kw["compiler_params"] = _params(("arbitrary",) * len(grid) if grid else None, vmem,
                                        rider.collective_id if own_barrier else None)
    res = _call(
        wrapped, name=name, in_specs=list(in_specs) + _any_specs(r_in), out_specs=list(out_specs) + _any_specs(r_out),
        out_shape=list(out_shape) + (rider.out_shape if rider is not None else []),
        scratch_shapes=list(scratch_shapes) + (rider.scratch if rider is not None else []), **kw,
    )(*args, *(rider.inputs if rider is not None else []))
    return list(res[:n_out]), list(res[n_out:])


def _pick_rows(r, target=256):
    best = None
    for d in range(16, min(r, target) + 1, 16):
        if r % d == 0:
            best = d
    return r if best is None else best


def pair_sum(g, got, *, name):
    _, _, R, C = g.shape
    tr = _pick_rows(R)

    def body(g_ref, got_ref, o_ref):
        mine = jnp.where(lax.axis_index("c") == 0, g_ref[:, 0], g_ref[:, 1])
        o_ref[...] = (mine.astype(F32) + got_ref[...].astype(F32)).astype(o_ref.dtype)

    return _call(
        body, name=name, grid=(R // tr,),
        in_specs=[pl.BlockSpec((4, 2, tr, C), lambda i: (0, 0, i, 0)), pl.BlockSpec((4, tr, C), lambda i: (0, i, 0))],
        out_specs=pl.BlockSpec((4, tr, C), lambda i: (0, i, 0)),
        out_shape=jax.ShapeDtypeStruct((4, R, C), g.dtype),
        compiler_params=_params(("parallel",)),
    )(g, got)


def chip_sum_adamw(p, got, w, m, v, *, name):
    _, R, C = p.shape
    assert w.shape == (1, R, C), (name, w.shape, p.shape)
    tr = _pick_rows(R)

    def body(p_ref, got_ref, w_ref, m_ref, v_ref, g_ref, d_ref, mo_ref, vo_ref):
        my_chip = 2 * lax.axis_index("x") + lax.axis_index("y")
        g = jnp.zeros((tr, C), F32)
        for j in range(4):
            g = g + jnp.where(my_chip == j, p_ref[j], got_ref[j]).astype(F32)
        g_ref[0] = g
        d_ref[0], mo_ref[0], vo_ref[0] = _adamw_math(w_ref[0], g, m_ref[0], v_ref[0])

    part = pl.BlockSpec((4, tr, C), lambda i: (0, i, 0))
    spec = pl.BlockSpec((1, tr, C), lambda i: (0, i, 0))
    return _call(
        body, name=name, grid=(R // tr,), in_specs=[part, part, spec, spec, spec], out_specs=[spec] * 4,
        out_shape=[jax.ShapeDtypeStruct((1, R, C), F32)] * 4,
        compiler_params=_params(("parallel",)),
    )(p, got, w, m, v)


def rows_sum(g8, *, name):
    _, R, C = g8.shape

    def body(g_ref, o_ref):
        acc = g_ref[0]
        for j in range(1, N_DEV):
            acc = acc + g_ref[j]
        o_ref[...] = acc

    return _call(body, name=name, out_shape=jax.ShapeDtypeStruct((R, C), F32))(g8)


def _adamw_math(w, g, m, v):
    m = ADAM_B1 * m + (1.0 - ADAM_B1) * g
    v = ADAM_B2 * v + (1.0 - ADAM_B2) * (g * g)
    m_hat = m / (1.0 - ADAM_B1 ** ADAM_STEP)
    v_hat = v / (1.0 - ADAM_B2 ** ADAM_STEP)
    delta = -ADAM_LR * (m_hat / (jnp.sqrt(v_hat) + ADAM_EPS) + ADAM_WD * w)
    return delta, m, v


def to_bf16(xs, *, name):
    def body(*refs):
        for x_ref, o_ref in zip(refs[:len(xs)], refs[len(xs):]):
            o_ref[...] = x_ref[...].astype(BF16)

    total = sum(_nbytes(v.shape, F32) + _nbytes(v.shape, BF16) for v in xs)
    return _call(body, name=name, out_shape=[jax.ShapeDtypeStruct(v.shape, BF16) for v in xs],
                 compiler_params=_params(vmem=2 * total + (4 << 20)))(*xs)


def adamw_small(wgmv, *, name):
    n = len(wgmv)

    def body(*refs):
        ins, outs = refs[:4 * n], refs[4 * n:]
        for a in range(n):
            w_ref, g_ref, m_ref, v_ref = ins[4 * a:4 * a + 4]
            d, mn, vn = _adamw_math(w_ref[...], g_ref[...], m_ref[...], v_ref[...])
            outs[3 * a][...] = d
            outs[3 * a + 1][...] = mn
            outs[3 * a + 2][...] = vn

    flat = [t for tup in wgmv for t in tup]
    res = _call(
        body, name=name,
        out_shape=[jax.ShapeDtypeStruct(tup[0].shape, F32) for tup in wgmv for _ in range(3)],
    )(*flat)
    return [tuple(res[3 * a:3 * a + 3]) for a in range(n)]


BIG = ("w_in", "w_out", "w_mq", "w_mkv", "w_mo", "w_gu", "w_down")
COL_SHARDED = ("w_in", "w_mkv", "w_gu")
SMALL = ("g_mix", "b_f", "conv_w", "conv_b", "ln_g", "ln_b", "g_x", "g_mem", "g_ffn", "g_final")


def _full_from_gathered(n, blk):
    _, rr, cc = blk.shape
    if n in COL_SHARDED:
        return jnp.concatenate([blk[k] for k in range(N_DEV)], axis=1)
    return blk.reshape(N_DEV * rr, cc)


def _shards_from_full(n, g):
    pieces = g if isinstance(g, list) else [g]
    rr, cc = pieces[0].shape[0], sum(p.shape[1] for p in pieces)
    if n in COL_SHARDED:
        w = cc // N_DEV
        return jnp.stack([_columns(pieces, k * w, w) for k in range(N_DEV)]).reshape(4, 2, rr, w)
    return pieces[0].reshape(4, 2, rr // N_DEV, cc)


def _columns(pieces, start, width):
    out, c0 = [], 0
    for p in pieces:
        lo, hi = max(start, c0), min(start + width, c0 + p.shape[1])
        if lo < hi:
            out.append(p[:, lo - c0:hi - c0])
        c0 += p.shape[1]
    return out[0] if len(out) == 1 else jnp.concatenate(out, axis=1)


def _small_layout():
    sizes = dict(g_mix=1024, b_f=8, conv_w=CONV_K * CONV_CH, conv_b=512, ln_g=512, ln_b=512, g_x=1024,
                 g_mem=1024, g_ffn=1024, g_final=1024, loss=1)
    lay, r0 = {}, 0
    for n, sz in sizes.items():
        r = -(-sz // LANES)
        lay[n] = (r0, r, sz)
        r0 += r
    return lay, -(-r0 // 8) * 8


def kernel(x, mem, g_mix, w_in, b_f, conv_w, conv_b, ln_g, ln_b, w_out, g_x, g_mem, w_mq, w_mkv, w_mo, g_ffn, w_gu, w_down, g_final, loss_target, m_g_mix, m_w_in, m_b_f, m_conv_w, m_conv_b, m_ln_g, m_ln_b, m_w_out, m_g_x, m_g_mem, m_w_mq, m_w_mkv, m_w_mo, m_g_ffn, m_w_gu, m_w_down, m_g_final, v_g_mix, v_w_in, v_b_f, v_conv_w, v_conv_b, v_ln_g, v_ln_b, v_w_out, v_g_x, v_g_mem, v_w_mq, v_w_mkv, v_w_mo, v_g_ffn, v_w_gu, v_w_down, v_g_final):
    names = ["g_mix", "w_in", "b_f", "conv_w", "conv_b", "ln_g", "ln_b", "w_out", "g_x", "g_mem", "w_mq",
             "w_mkv", "w_mo", "g_ffn", "w_gu", "w_down", "g_final"]
    W = dict(zip(names, [g_mix, w_in, b_f, conv_w, conv_b, ln_g, ln_b, w_out, g_x, g_mem, w_mq, w_mkv, w_mo,
                         g_ffn, w_gu, w_down, g_final]))
    Mo = dict(zip(names, [m_g_mix, m_w_in, m_b_f, m_conv_w, m_conv_b, m_ln_g, m_ln_b, m_w_out, m_g_x, m_g_mem,
                          m_w_mq, m_w_mkv, m_w_mo, m_g_ffn, m_w_gu, m_w_down, m_g_final]))
    Vo = dict(zip(names, [v_g_mix, v_w_in, v_b_f, v_conv_w, v_conv_b, v_ln_g, v_ln_b, v_w_out, v_g_x, v_g_mem,
                          v_w_mq, v_w_mkv, v_w_mo, v_g_ffn, v_w_gu, v_w_down, v_g_final]))
    dev = 4 * lax.axis_index("x") + 2 * lax.axis_index("y") + lax.axis_index("c")

    two = lambda a: a.reshape(-1, a.shape[-1])
    cw_shard = jnp.pad(two(conv_w), ((0, HALO - CONV_K), (0, 0)))
    sp = dict(g_mix=g_mix, b_f=b_f, conv_b=conv_b, ln_g=ln_g, ln_b=ln_b, g_x=g_x, g_mem=g_mem,
              g_ffn=g_ffn, g_final=g_final)
    shards = to_bf16([two(W[n]) for n in ("w_in",) + LATE], name="cast_shards")
    loss_blk, grad_x, gs, reduced = local_step(x, mem, loss_target, sp, [shards[0], cw_shard], shards[1:])

    lay, rs = _small_layout()
    small = {**{n: gs[n] for n in SMALL}, "loss": loss_blk[:, :1]}
    parts = []
    for n, (r0, r, sz) in lay.items():
        flat = small[n].reshape(-1).astype(F32)
        parts.append(jnp.pad(flat, (0, r * LANES - sz)).reshape(r, LANES))
    spack = jnp.concatenate(parts, axis=0)
    spack = jnp.pad(spack, ((0, rs - spack.shape[0]), (0, 0)))
    ssum = rows_sum(all_gather([spack], name="ag_small")[0], name="small_sum")
    gsmall = {n: ssum[r0:r0 + r].reshape(-1)[:sz] for n, (r0, r, sz) in lay.items()}
    loss = gsmall["loss"].reshape(())

    grads, delta, new_m, new_v = {}, {}, {}, {}
    for n in BIG:
        p, o = reduced[n]
        grads[n], delta[n], new_m[n], new_v[n] = chip_sum_adamw(p, o, W[n], Mo[n], Vo[n], name="adamw_" + n)
    for n in SMALL:
        if n == "conv_w":
            full = gsmall[n].reshape(CONV_K, CONV_CH)
            ncol = conv_w.shape[-1]
            grads[n] = lax.dynamic_slice(full, (0, dev * ncol), (CONV_K, ncol)).reshape(conv_w.shape)
        else:
            grads[n] = gsmall[n].reshape(W[n].shape)
    upd = adamw_small([(two(W[n]), two(grads[n]), two(Mo[n]), two(Vo[n])) for n in SMALL], name="adamw_small")
    for n, (d, mn, vn) in zip(SMALL, upd):
        shp = W[n].shape
        delta[n], new_m[n], new_v[n] = d.reshape(shp), mn.reshape(shp), vn.reshape(shp)
    return (loss, grad_x, *[grads[n] for n in names], *[delta[n] for n in names],
            *[new_m[n] for n in names], *[new_v[n] for n in names])
```

```python
import functools
import math

import jax
import jax.numpy as jnp
from jax import lax
from jax.experimental import pallas as pl
from jax.experimental.pallas import tpu as pltpu

F32 = jnp.float32
BF16 = jnp.bfloat16
EPS = 1e-6
N_DEV = 8
CONV_CH = 512
CONV_K = 31
FOX_HEADS = 8
FOX_HEAD_DIM = 64
FOX_W = 512
MEM_HEADS = 4
MEM_HEAD_DIM = 256
HALO = 32
LANES = 128
ADAM_LR, ADAM_B1, ADAM_B2, ADAM_EPS, ADAM_WD, ADAM_STEP = 0.001, 0.9, 0.999, 1e-08, 0.01, 10
NEG = -1e30
VMEM_CAP = 60 * 1024 * 1024
MESH = pl.DeviceIdType.MESH


def _call(body, **kw):
    kw["out_shape"] = jax.tree.map(lambda s: pltpu.HBM(s.shape, s.dtype), kw["out_shape"])
    call = pl.pallas_call(body, **kw)
    return lambda *args: call(*[pltpu.with_memory_space_constraint(a, pltpu.HBM) for a in args])


def _params(sem=None, vmem=None, collective_id=None):
    kw = {} if collective_id is None else {"collective_id": collective_id}
    if sem is not None:
        kw["dimension_semantics"] = sem
    if vmem is not None:
        kw["vmem_limit_bytes"] = int(min(VMEM_CAP, vmem))
    return pltpu.CompilerParams(**kw)


def _nbytes(shape, dtype):
    return math.prod(shape) * jnp.dtype(dtype).itemsize


def _pick(n, target):
    best = None
    for d in range(LANES, min(n, target) + 1, LANES):
        if n % d == 0:
            best = d
    return n if best is None else best


def matmul(a, b, *, tb=False, out_dtype, res=None, tm=512, tn=512, name, rider=None, b_blk=None):
    a_list = list(a) if isinstance(a, (list, tuple)) else [a]
    b_list = list(b) if isinstance(b, (list, tuple)) else [b]
    n = len(a_list)
    assert len(b_list) == n
    M = a_list[0].shape[0]
    N = b_list[0].shape[0] if tb else b_list[0].shape[1]
    tm, tn = _pick(M, tm), _pick(N, tn)
    assert M % tm == 0 and N % tn == 0, (name, M, N, tm, tn)
    dn = (((1,), (1 if tb else 0,)), ((), ()))

    def body(*refs):
        acc = None
        for a_ref, b_ref in zip(refs[:n], refs[n:2 * n]):
            p = lax.dot_general(a_ref[...].astype(BF16), b_ref[...].astype(BF16), dn, preferred_element_type=F32)
            acc = p if acc is None else acc + p
        if res is not None:
            acc = acc + refs[2 * n][...].astype(F32)
        refs[-1][...] = acc.astype(out_dtype)

    o_spec = pl.BlockSpec((tm, tn), lambda i, j: (i, j))
    in_specs, est = [], 2 * _nbytes((tm, tn), out_dtype) + 2 * _nbytes((tm, tn), F32)
    for av in a_list:
        assert av.shape[0] == M
        in_specs.append(pl.BlockSpec((tm, av.shape[1]), lambda i, j: (i, 0)))
        est += (2 * jnp.dtype(av.dtype).itemsize + (av.dtype != BF16) * 2) * tm * av.shape[1]
    for idx, (av, bv) in enumerate(zip(a_list, b_list)):
        K = av.shape[1]
        kb = 0 if b_blk is None else b_blk[idx]
        assert bv.shape[0 if tb else 1] == N and bv.shape[1 if tb else 0] >= (kb + 1) * K, (name, av.shape, bv.shape)
        assert b_blk is not None or bv.shape[1 if tb else 0] == K, (name, av.shape, bv.shape)
        in_specs.append(pl.BlockSpec((tn, K), lambda i, j, kb=kb: (j, kb)) if tb
                        else pl.BlockSpec((K, tn), lambda i, j, kb=kb: (kb, j)))
        est += (2 * jnp.dtype(bv.dtype).itemsize + (bv.dtype != BF16) * 2) * tn * K
    args = a_list + b_list
    if res is not None:
        in_specs.append(o_spec)
        args.append(res)
        est += 2 * _nbytes((tm, tn), res.dtype)
    (out,), rode = hosted_call(
        body, rider, name=name, grid=(M // tm, N // tn), in_specs=in_specs, out_specs=[o_spec],
        out_shape=[jax.ShapeDtypeStruct((M, N), out_dtype)], scratch_shapes=[],
        args=args, vmem=est + (8 << 20),
    )
    return out if rider is None else (out, rode)


def _rms_scale(x):
    return lax.rsqrt(jnp.mean(x * x, axis=-1, keepdims=True) + EPS)


def rmsnorm_fwd(x, g, *, name, tm=512, rider=None):
    T, D = x.shape
    tm = min(tm, T)

    def body(x_ref, g_ref, o_ref, ot_ref):
        xv = x_ref[...]
        h = xv * _rms_scale(xv) * g_ref[...]
        o_ref[...] = h.astype(BF16)
        ot_ref[...] = h.T.astype(BF16)

    (h, h_t), rode = hosted_call(
        body, rider, name=name, grid=(T // tm,),
        in_specs=[pl.BlockSpec((tm, D), lambda i: (i, 0)), pl.BlockSpec((1, D), lambda i: (0, 0))],
        out_specs=[pl.BlockSpec((tm, D), lambda i: (i, 0)), pl.BlockSpec((D, tm), lambda i: (0, i))],
        out_shape=[jax.ShapeDtypeStruct((T, D), BF16), jax.ShapeDtypeStruct((D, T), BF16)],
        scratch_shapes=[], args=(x, g),
    )
    return (h, h_t) if rider is None else (h, h_t, rode)


def _rms_bwd_math(xv, gv, dh):
    r = _rms_scale(xv)
    xh = xv * r
    dg = jnp.sum(dh * xh, axis=0, keepdims=True)
    dxh = dh * gv
    dx = r * (dxh - xh * jnp.mean(dxh * xh, axis=-1, keepdims=True))
    return dx, dg


def rmsnorm_bwd(x, g, dh, dres, *, name, tm=256, out_dtype=BF16):
    T, D = x.shape
    tm = min(tm, T)

    def body(*refs):
        if dres is not None:
            x_ref, g_ref, dh_ref, dr_ref, dx_ref, dg_ref = refs
        else:
            x_ref, g_ref, dh_ref, dx_ref, dg_ref = refs
        dx, dg = _rms_bwd_math(x_ref[...], g_ref[...], dh_ref[...].astype(F32))
        if dres is not None:
            dx = dx + dr_ref[...].astype(F32)
        dx_ref[...] = dx.astype(out_dtype)

        @pl.when(pl.program_id(0) == 0)
        def _():
            dg_ref[...] = jnp.zeros_like(dg_ref)

        dg_ref[...] += dg

    row = pl.BlockSpec((tm, D), lambda i: (i, 0))
    vec = pl.BlockSpec((1, D), lambda i: (0, 0))
    ins, args = [row, vec, row], [x, g, dh]
    if dres is not None:
        ins.append(row)
        args.append(dres)
    return _call(
        body, name=name, grid=(T // tm,), in_specs=ins, out_specs=[row, vec],
        out_shape=[jax.ShapeDtypeStruct((T, D), out_dtype), jax.ShapeDtypeStruct((1, D), F32)],
        compiler_params=_params(("arbitrary",)),
    )(*args)


def final_loss_bwd(x, g, target, *, name, tm=256):
    T, D = x.shape
    tm = min(tm, T)

    def body(x_ref, g_ref, t_ref, dx_ref, dg_ref, l_ref):
        xv, gv = x_ref[...], g_ref[...]
        e = xv * _rms_scale(xv) * gv - t_ref[...]
        part = 0.5 * jnp.sum(jnp.mean(e * e, axis=-1, keepdims=True), axis=0, keepdims=True)
        dx, dg = _rms_bwd_math(xv, gv, e * (1.0 / D))
        dx_ref[...] = dx.astype(BF16)

        @pl.when(pl.program_id(0) == 0)
        def _():
            dg_ref[...] = jnp.zeros_like(dg_ref)
            l_ref[...] = jnp.zeros_like(l_ref)

        dg_ref[...] += dg
        l_ref[...] += jnp.broadcast_to(part, l_ref.shape)

    row = pl.BlockSpec((tm, D), lambda i: (i, 0))
    vec = pl.BlockSpec((1, D), lambda i: (0, 0))
    return _call(
        body, name=name, grid=(T // tm,), in_specs=[row, vec, row],
        out_specs=[row, vec, pl.BlockSpec((1, LANES), lambda i: (0, 0))],
        out_shape=[jax.ShapeDtypeStruct((T, D), BF16), jax.ShapeDtypeStruct((1, D), F32),
                   jax.ShapeDtypeStruct((1, LANES), F32)],
        compiler_params=_params(("arbitrary",)),
    )(x, g, target)


def _sigmoid(v):
    return 0.5 * jnp.tanh(0.5 * v) + 0.5


def _glu(blk):
    u = blk[:, :CONV_CH].astype(F32)
    gt = blk[:, CONV_CH:].astype(F32)
    return u * _sigmoid(gt)


def _fill_causal_ext(ext, cur_ref, halo_ref, s, ts):
    ext[pl.ds(HALO, ts), :] = _glu(cur_ref[0])
    hal = _glu(halo_ref[0])
    ext[pl.ds(0, HALO), :] = jnp.where(s > 0, hal, 0.0)


SUBLANES = 8


def _make_shifted(ext, sh):
    n = ext.shape[0]
    full = ext[...]
    for r in range(1, SUBLANES):
        sh[r - 1] = pltpu.roll(full, n - r, 0)


def _tap(ext, sh, off, ts):
    r = off % SUBLANES
    return ext[pl.ds(off, ts), :] if r == 0 else sh[r - 1, pl.ds(off - r, ts), :]


def _causal_conv(ext, sh, w_ref, ts):
    acc = jnp.zeros((ts, CONV_CH), F32)
    for j in range(CONV_K):
        acc = acc + _tap(ext, sh, HALO - (CONV_K - 1) + j, ts) * w_ref[pl.ds(j, 1), :]
    return acc


def _ln_stats(y):
    mu = jnp.mean(y, axis=-1, keepdims=True)
    yc = y - mu
    rstd = lax.rsqrt(jnp.mean(yc * yc, axis=-1, keepdims=True) + EPS)
    return yc * rstd, rstd


def _conv_specs(ts, S):
    nh = ts // HALO
    cur = pl.BlockSpec((1, ts, 2 * CONV_CH), lambda b, s: (b, s, 0))
    halo = pl.BlockSpec((1, HALO, 2 * CONV_CH), lambda b, s: (b, jnp.maximum(s * nh - 1, 0), 0))
    w = pl.BlockSpec((HALO, CONV_CH), lambda b, s: (0, 0))
    vec = pl.BlockSpec((1, CONV_CH), lambda b, s: (0, 0))
    return cur, halo, w, vec


def conv_branch_fwd(ug, conv_w, conv_b, ln_g, ln_b, *, name, ts=256, rider=None):
    B, S, _ = ug.shape
    ts = min(ts, S)
    ns = S // ts
    cur, halo, w, vec = _conv_specs(ts, S)

    def body(cur_ref, halo_ref, w_ref, cb_ref, lg_ref, lb_ref, o_ref, ot_ref, y_ref, ext, sh):
        _fill_causal_ext(ext, cur_ref, halo_ref, pl.program_id(1), ts)
        _make_shifted(ext, sh)
        y = _causal_conv(ext, sh, w_ref, ts) + cb_ref[...]
        y_ref[0] = y
        yh, _ = _ln_stats(y)
        ln = yh * lg_ref[...] + lb_ref[...]
        out = ln * _sigmoid(ln)
        o_ref[0] = out.astype(BF16)
        ot_ref[...] = out.T.astype(BF16)

    return hosted_call(
        body, rider, name=name, grid=(B, ns), in_specs=[cur, halo, w, vec, vec, vec],
        out_specs=[pl.BlockSpec((1, ts, CONV_CH), lambda b, s: (b, s, 0)),
                   pl.BlockSpec((CONV_CH, ts), lambda b, s: (0, b * ns + s)),
                   pl.BlockSpec((1, ts, CONV_CH), lambda b, s: (b, s, 0))],
        out_shape=[jax.ShapeDtypeStruct((B, S, CONV_CH), BF16), jax.ShapeDtypeStruct((CONV_CH, B * S), BF16),
                   jax.ShapeDtypeStruct((B, S, CONV_CH), F32)],
        scratch_shapes=[pltpu.VMEM((ts + HALO, CONV_CH), F32),
                        pltpu.VMEM((SUBLANES - 1, ts + HALO, CONV_CH), F32)],
        args=(ug, ug, conv_w, conv_b, ln_g, ln_b),
    )


def conv_branch_bwd_a(ug, y, dcat, ln_g, ln_b, *, name, ts=256):
    B, S, _ = ug.shape
    ts = min(ts, S)
    cur, halo, _, vec = _conv_specs(ts, S)
    tile = pl.BlockSpec((1, ts, CONV_CH), lambda b, s: (b, s, 0))

    def body(cur_ref, halo_ref, y_ref, d_ref, lg_ref, lb_ref, dy_ref, dw_ref, dv_ref, ext, sh):
        _fill_causal_ext(ext, cur_ref, halo_ref, pl.program_id(1), ts)
        _make_shifted(ext, sh)
        yh, rstd = _ln_stats(y_ref[0])
        lg = lg_ref[...]
        ln = yh * lg + lb_ref[...]
        sg = _sigmoid(ln)
        dln = d_ref[0].astype(F32) * (sg * (1.0 + ln * (1.0 - sg)))
        dyh = dln * lg
        dy = rstd * (dyh - jnp.mean(dyh, axis=-1, keepdims=True)
                     - yh * jnp.mean(dyh * yh, axis=-1, keepdims=True))
        dy_ref[0] = dy

        @pl.when((pl.program_id(0) == 0) & (pl.program_id(1) == 0))
        def _():
            dw_ref[...] = jnp.zeros_like(dw_ref)
            dv_ref[...] = jnp.zeros_like(dv_ref)

        dv_ref[pl.ds(0, 1), :] += jnp.sum(dy, axis=0, keepdims=True)
        dv_ref[pl.ds(1, 1), :] += jnp.sum(dln * yh, axis=0, keepdims=True)
        dv_ref[pl.ds(2, 1), :] += jnp.sum(dln, axis=0, keepdims=True)
        for j in range(CONV_K):
            tap = _tap(ext, sh, HALO - (CONV_K - 1) + j, ts)
            dw_ref[pl.ds(j, 1), :] += jnp.sum(dy * tap, axis=0, keepdims=True)

    return _call(
        body, name=name, grid=(B, S // ts),
        in_specs=[cur, halo, tile, tile, vec, vec],
        out_specs=[tile,
                   pl.BlockSpec((HALO, CONV_CH), lambda b, s: (0, 0)),
                   pl.BlockSpec((8, CONV_CH), lambda b, s: (0, 0))],
        out_shape=[jax.ShapeDtypeStruct((B, S, CONV_CH), F32),
                   jax.ShapeDtypeStruct((HALO, CONV_CH), F32),
                   jax.ShapeDtypeStruct((8, CONV_CH), F32)],
        scratch_shapes=[pltpu.VMEM((ts + HALO, CONV_CH), F32),
                        pltpu.VMEM((SUBLANES - 1, ts + HALO, CONV_CH), F32)],
        compiler_params=_params(("arbitrary", "arbitrary")),
    )(ug, ug, y, dcat, ln_g, ln_b)


def conv_branch_bwd_b(ug, dy, conv_w, *, name, ts=256):
    B, S, _ = ug.shape
    ts = min(ts, S)
    nh, n_halo = ts // HALO, S // HALO

    def body(cur_ref, dy_ref, nxt_ref, w_ref, o_ref, ext, sh):
        last = pl.program_id(1) == pl.num_programs(1) - 1
        ext[pl.ds(0, ts), :] = dy_ref[0]
        ext[pl.ds(ts, HALO), :] = jnp.where(last, 0.0, nxt_ref[0])
        _make_shifted(ext, sh)
        da = jnp.zeros((ts, CONV_CH), F32)
        for j in range(CONV_K):
            da = da + _tap(ext, sh, CONV_K - 1 - j, ts) * w_ref[pl.ds(j, 1), :]
        blk = cur_ref[0]
        u = blk[:, :CONV_CH].astype(F32)
        sg = _sigmoid(blk[:, CONV_CH:].astype(F32))
        o_ref[0, :, :CONV_CH] = (da * sg).astype(BF16)
        o_ref[0, :, CONV_CH:] = (da * u * sg * (1.0 - sg)).astype(BF16)

    return _call(
        body, name=name, grid=(B, S // ts),
        in_specs=[pl.BlockSpec((1, ts, 2 * CONV_CH), lambda b, s: (b, s, 0)),
                  pl.BlockSpec((1, ts, CONV_CH), lambda b, s: (b, s, 0)),
                  pl.BlockSpec((1, HALO, CONV_CH), lambda b, s: (b, jnp.minimum((s + 1) * nh, n_halo - 1), 0)),
                  pl.BlockSpec((HALO, CONV_CH), lambda b, s: (0, 0))],
        out_specs=pl.BlockSpec((1, ts, 2 * CONV_CH), lambda b, s: (b, s, 0)),
        out_shape=jax.ShapeDtypeStruct((B, S, 2 * CONV_CH), BF16),
        scratch_shapes=[pltpu.VMEM((ts + HALO, CONV_CH), F32),
                        pltpu.VMEM((SUBLANES - 1, ts + HALO, CONV_CH), F32)],
        compiler_params=_params(("parallel", "parallel")),
    )(ug, dy, dy, conv_w)


def _tri(n, lower):
    r = lax.broadcasted_iota(jnp.int32, (n, n), 0)
    c = lax.broadcasted_iota(jnp.int32, (n, n), 1)
    return ((r >= c) if lower else (r <= c)).astype(F32)


def _eye(n):
    r = lax.broadcasted_iota(jnp.int32, (n, n), 0)
    c = lax.broadcasted_iota(jnp.int32, (n, n), 1)
    return (r == c).astype(F32)


def _dot_hi(a, b, dn):
    return lax.dot_general(a, b, dn, precision=lax.Precision.HIGHEST, preferred_element_type=F32)


NN = (((1,), (0,)), ((), ()))
NT = (((1,), (1,)), ((), ()))
TN = (((0,), (0,)), ((), ()))


def _log_sigmoid(v):
    e = jnp.exp(-jnp.abs(v))
    log1p_e = jnp.where(e < 1e-3, e * (1.0 - 0.5 * e), jnp.log(1.0 + e))
    return jnp.minimum(v, 0.0) - log1p_e


def fgate_fwd(h, w_f, b_f, *, name, ts=256, rider=None):
    B, S, D = h.shape
    ts = min(ts, S)

    def body(h_ref, w_ref, b_ref, f_ref, cc_ref, cr_ref, carry):
        @pl.when(pl.program_id(1) == 0)
        def _():
            carry[...] = jnp.zeros_like(carry)

        f = jnp.dot(h_ref[0], w_ref[...], preferred_element_type=F32)
        f_ref[0] = f
        logf = _log_sigmoid(f + b_ref[...])
        c = _dot_hi(_tri(ts, True), logf, NN) + carry[pl.ds(0, 1), :]
        cc_ref[0] = c
        carry[pl.ds(0, 1), :] = c[ts - 1:ts, :]
        cr_ref[0] = _dot_hi(_eye(LANES), c, NT)

    return hosted_call(
        body, rider, name=name, grid=(B, S // ts),
        in_specs=[pl.BlockSpec((1, ts, D), lambda b, s: (b, s, 0)),
                  pl.BlockSpec((D, LANES), lambda b, s: (0, 0)),
                  pl.BlockSpec((1, LANES), lambda b, s: (0, 0))],
        out_specs=[pl.BlockSpec((1, ts, LANES), lambda b, s: (b, s, 0)),
                   pl.BlockSpec((1, ts, LANES), lambda b, s: (b, s, 0)),
                   pl.BlockSpec((1, LANES, ts), lambda b, s: (b, 0, s))],
        out_shape=[jax.ShapeDtypeStruct((B, S, LANES), F32), jax.ShapeDtypeStruct((B, S, LANES), F32),
                   jax.ShapeDtypeStruct((B, LANES, S), F32)],
        scratch_shapes=[pltpu.VMEM((8, LANES), F32)],
        args=(h, w_f, b_f),
    )


def fgate_bwd(dc, f, b_f, *, name, ts=256):
    B, S, _ = f.shape
    P = dc.shape[1]
    ts = min(ts, S)
    ns = S // ts

    def body(dc_ref, f_ref, b_ref, df_ref, db_ref, carry):
        @pl.when(pl.program_id(1) == 0)
        def _():
            carry[...] = jnp.zeros_like(carry)

        @pl.when((pl.program_id(0) == 0) & (pl.program_id(1) == 0))
        def _():
            db_ref[...] = jnp.zeros_like(db_ref)

        dc_t = dc_ref[0, 0]
        for j in range(1, P):
            dc_t = dc_t + dc_ref[0, j]
        dlogf = _dot_hi(_tri(ts, False), dc_t, NN) + carry[pl.ds(0, 1), :]
        carry[pl.ds(0, 1), :] = dlogf[0:1, :]
        df = dlogf * _sigmoid(-(f_ref[0] + b_ref[...]))
        df_ref[0] = df.astype(BF16)
        db_ref[...] += jnp.sum(df, axis=0, keepdims=True)

    return _call(
        body, name=name, grid=(B, ns),
        in_specs=[pl.BlockSpec((1, P, ts, LANES), lambda b, s: (b, 0, ns - 1 - s, 0)),
                  pl.BlockSpec((1, ts, LANES), lambda b, s: (b, ns - 1 - s, 0)),
                  pl.BlockSpec((1, LANES), lambda b, s: (0, 0))],
        out_specs=[pl.BlockSpec((1, ts, LANES), lambda b, s: (b, ns - 1 - s, 0)),
                   pl.BlockSpec((1, LANES), lambda b, s: (0, 0))],
        out_shape=[jax.ShapeDtypeStruct((B, S, LANES), BF16), jax.ShapeDtypeStruct((1, LANES), F32)],
        scratch_shapes=[pltpu.VMEM((8, LANES), F32)],
        compiler_params=_params(("arbitrary", "arbitrary")),
    )(dc, f, b_f)


def _lane_pick(tile, idx):
    lane = lax.broadcasted_iota(jnp.int32, tile.shape, 1)
    return jnp.sum(jnp.where(lane == idx, tile, 0.0), axis=-1, keepdims=True)


FOX_T = 512


def _fox_heads(q, cc_ref, p):
    lane = lax.broadcasted_iota(jnp.int32, q.shape, 1)
    qs = q * (1.0 / math.sqrt(FOX_HEAD_DIM))
    qhs = [jnp.where((lane < FOX_HEAD_DIM) == (hh == 0), qs, jnp.zeros_like(qs)) for hh in range(2)]
    crefs = [_lane_pick(cc_ref[0, pl.ds(0, 1), :], 2 * p + hh) for hh in range(2)]
    return qhs, crefs


def _fold_lanes(x, op):
    out = x[:, :LANES]
    for j in range(1, x.shape[1] // LANES):
        out = op(out, x[:, j * LANES:(j + 1) * LANES])
    return out


def _causal(t, transposed):
    r = lax.broadcasted_iota(jnp.int32, (t, t), 0)
    c = lax.broadcasted_iota(jnp.int32, (t, t), 1)
    return (r <= c) if transposed else (c <= r)


QKV0 = 8


def fox_fwd(z, c_col, c_row, *, name, rider=None):
    B, S, _ = z.shape
    assert S % FOX_T == 0
    tq, nq = FOX_T, S // FOX_T
    npair = FOX_HEADS // 2

    def body(q_ref, k_ref, v_ref, cc_ref, cr_ref, o_ref, l_ref, ot_ref, s_scr, m_scr, acc_scr):
        p, qi = pl.program_id(1), pl.program_id(2)
        qhs, crefs = _fox_heads(q_ref[0], cc_ref, p)
        lane = lax.broadcasted_iota(jnp.int32, (tq, LANES), 1)
        first = lane < FOX_HEAD_DIM
        for hh in range(2):
            m_scr[hh] = jnp.full((tq, LANES), NEG, F32)
            acc_scr[hh] = jnp.zeros((tq, LANES), F32)

        def logits(kb, diagonal):
            k0 = pl.multiple_of(kb * tq, tq)
            k = k_ref[0, pl.ds(k0, tq), :]
            for hh in range(2):
                s = lax.dot_general(qhs[hh], k, NT, preferred_element_type=F32)
                s = s + (crefs[hh] - cr_ref[0, pl.ds(2 * p + hh, 1), pl.ds(k0, tq)])
                if diagonal:
                    s = jnp.where(_causal(tq, False), s, NEG)
                s_scr[hh, kb] = s
                m_scr[hh] = jnp.maximum(m_scr[hh], _fold_lanes(s, jnp.maximum))

        def sweep1(kb, carry):
            logits(kb, False)
            return carry

        lax.fori_loop(0, qi, sweep1, 0)
        logits(qi, True)
        ms = [jnp.max(m_scr[hh], axis=-1, keepdims=True) for hh in range(2)]
        mbs = [jnp.broadcast_to(ms[hh], (tq, tq)) for hh in range(2)]

        for hh in range(2):
            m_scr[hh] = jnp.zeros((tq, LANES), F32)

        def weigh(kb, carry):
            k0 = pl.multiple_of(kb * tq, tq)
            v = v_ref[0, pl.ds(k0, tq), :]
            for hh in range(2):
                pr = jnp.exp(s_scr[hh, kb] - mbs[hh])
                m_scr[hh] += _fold_lanes(pr, jnp.add)
                acc_scr[hh] += jnp.dot(pr.astype(BF16), v, preferred_element_type=F32)
            return carry

        lax.fori_loop(0, qi + 1, weigh, 0)
        accs = [acc_scr[hh] for hh in range(2)]
        ls = [jnp.sum(m_scr[hh], axis=-1, keepdims=True) for hh in range(2)]
        out = jnp.where(first, accs[0] / ls[0], accs[1] / ls[1])
        o_ref[0] = out.astype(BF16)
        ot_ref[...] = out.T.astype(BF16)
        l_ref[0, 0] = jnp.where(first, ms[0] + jnp.log(ls[0]), ms[1] + jnp.log(ls[1]))

    return hosted_call(
        body, rider, name=name, grid=(B, npair, nq),
        in_specs=[pl.BlockSpec((1, tq, LANES), lambda b, p, i: (b, i, QKV0 + p)),
                  pl.BlockSpec((1, S, LANES), lambda b, p, i: (b, 0, QKV0 + npair + p)),
                  pl.BlockSpec((1, S, LANES), lambda b, p, i: (b, 0, QKV0 + 2 * npair + p)),
                  pl.BlockSpec((1, tq, LANES), lambda b, p, i: (b, i, 0)),
                  pl.BlockSpec((1, 8, S), lambda b, p, i: (b, 0, 0))],
        out_specs=[pl.BlockSpec((1, tq, LANES), lambda b, p, i: (b, i, p)),
                   pl.BlockSpec((1, 1, tq, LANES), lambda b, p, i: (b, p, i, 0)),
                   pl.BlockSpec((LANES, tq), lambda b, p, i: (p, b * nq + i))],
        out_shape=[jax.ShapeDtypeStruct((B, S, FOX_W), BF16),
                   jax.ShapeDtypeStruct((B, npair, S, LANES), F32),
                   jax.ShapeDtypeStruct((FOX_W, B * S), BF16)],
        scratch_shapes=[pltpu.VMEM((2, nq, tq, tq), F32), pltpu.VMEM((2, tq, LANES), F32),
                        pltpu.VMEM((2, tq, LANES), F32)],
        args=(z, z, z, c_col, c_row),
    )


def fox_bwd_dq(z, dcat, lse, c_col, c_row, *, name, rider=None):
    B, S, _ = z.shape
    tq, nq = FOX_T, S // FOX_T
    npair = FOX_HEADS // 2

    def body(q_ref, k_ref, v_ref, do_ref, l_ref, cc_ref, cr_ref, dq_ref, st_ref, p_scr, dp_scr, dl_scr):
        p, qi = pl.program_id(1), pl.program_id(2)
        qhs, crefs = _fox_heads(q_ref[0], cc_ref, p)
        lane = lax.broadcasted_iota(jnp.int32, (tq, LANES), 1)
        do_b = do_ref[0].astype(BF16)
        dohs = [jnp.where((lane < FOX_HEAD_DIM) == (hh == 0), do_b, jnp.zeros_like(do_b)) for hh in range(2)]
        lses = [_lane_pick(l_ref[0, 0], hh * FOX_HEAD_DIM) for hh in range(2)]
        lbs = [jnp.broadcast_to(lses[hh], (tq, tq)) for hh in range(2)]
        for hh in range(2):
            dl_scr[hh] = jnp.zeros((tq, LANES), F32)

        def probs(kb, diagonal):
            k0 = pl.multiple_of(kb * tq, tq)
            k = k_ref[0, pl.ds(k0, tq), :]
            v = v_ref[0, pl.ds(k0, tq), :]
            for hh in range(2):
                s = lax.dot_general(qhs[hh], k, NT, preferred_element_type=F32)
                s = s + (crefs[hh] - cr_ref[0, pl.ds(2 * p + hh, 1), pl.ds(k0, tq)])
                pr = jnp.exp(s - lbs[hh])
                if diagonal:
                    pr = jnp.where(_causal(tq, False), pr, 0.0)
                dp = lax.dot_general(dohs[hh], v, NT, preferred_element_type=F32)
                pdp = pr * dp
                dl_scr[hh] += _fold_lanes(pdp, jnp.add)
                p_scr[hh, kb] = pr
                dp_scr[hh, kb] = dp

        def first_pass(kb, carry):
            probs(kb, False)
            return carry

        lax.fori_loop(0, qi, first_pass, 0)
        probs(qi, True)

        dls = [jnp.sum(dl_scr[hh], axis=-1, keepdims=True) for hh in range(2)]
        dlbs = [jnp.broadcast_to(dls[hh], (tq, tq)) for hh in range(2)]

        def second_pass(kb, dq):
            k0 = pl.multiple_of(kb * tq, tq)
            k = k_ref[0, pl.ds(k0, tq), :]
            for hh in range(2):
                ds = p_scr[hh, kb] * (dp_scr[hh, kb] - dlbs[hh])
                kh = jnp.where((lane < FOX_HEAD_DIM) == (hh == 0), k, jnp.zeros_like(k))
                dq = dq + jnp.dot(ds.astype(BF16), kh, preferred_element_type=F32)
            return dq

        dq = lax.fori_loop(0, qi + 1, second_pass, jnp.zeros((tq, LANES), F32))
        dq_ref[0] = (dq * (1.0 / math.sqrt(FOX_HEAD_DIM))).astype(BF16)
        cols = jnp.zeros((tq, LANES), F32)
        for j, col in enumerate([crefs[0] - lses[0], crefs[1] - lses[1], dls[0], dls[1]]):
            cols = jnp.where(lane == j, col, cols)
        st_ref[0, 0] = _dot_hi(_eye(LANES), cols, NT)[:8]

    return hosted_call(
        body, rider, name=name, grid=(B, npair, nq),
        in_specs=[pl.BlockSpec((1, tq, LANES), lambda b, p, i: (b, i, QKV0 + p)),
                  pl.BlockSpec((1, S, LANES), lambda b, p, i: (b, 0, QKV0 + npair + p)),
                  pl.BlockSpec((1, S, LANES), lambda b, p, i: (b, 0, QKV0 + 2 * npair + p)),
                  pl.BlockSpec((1, tq, LANES), lambda b, p, i: (b, i, npair + p)),
                  pl.BlockSpec((1, 1, tq, LANES), lambda b, p, i: (b, p, i, 0)),
                  pl.BlockSpec((1, tq, LANES), lambda b, p, i: (b, i, 0)),
                  pl.BlockSpec((1, 8, S), lambda b, p, i: (b, 0, 0))],
        out_specs=[pl.BlockSpec((1, tq, LANES), lambda b, p, i: (b, i, p)),
                   pl.BlockSpec((1, 1, 8, tq), lambda b, p, i: (b, p, 0, i))],
        out_shape=[jax.ShapeDtypeStruct((B, S, FOX_W), BF16), jax.ShapeDtypeStruct((B, npair, 8, S), F32)],
        scratch_shapes=[pltpu.VMEM((2, nq, tq, tq), F32), pltpu.VMEM((2, nq, tq, tq), F32),
                        pltpu.VMEM((2, tq, LANES), F32)],
        args=(z, z, z, dcat, lse, c_col, c_row), vmem=56 << 20,
    )


def fox_bwd_dkdv(z, dcat, stats, c_col, *, name, rider=None):
    B, S, _ = z.shape
    tk, nq = FOX_T, S // FOX_T
    npair = FOX_HEADS // 2
    inv = 1.0 / math.sqrt(FOX_HEAD_DIM)

    def body(q_ref, k_ref, v_ref, do_ref, st_ref, cc_ref, dk_ref, dv_ref, dc_ref, dk_scr, dv_scr, dc_scr):
        p, kt = pl.program_id(1), pl.program_id(2)
        lane = lax.broadcasted_iota(jnp.int32, (tk, LANES), 1)
        masks = [(lane < FOX_HEAD_DIM) == (hh == 0) for hh in range(2)]
        k = k_ref[0]
        v = v_ref[0]
        khs = [jnp.where(masks[hh], k, jnp.zeros_like(k)) for hh in range(2)]
        vhs = [jnp.where(masks[hh], v, jnp.zeros_like(v)) for hh in range(2)]
        ccbs = [jnp.broadcast_to(_lane_pick(cc_ref[0], 2 * p + hh), (tk, tk)) for hh in range(2)]
        dk_scr[...] = jnp.zeros_like(dk_scr)
        dv_scr[...] = jnp.zeros_like(dv_scr)
        dc_scr[...] = jnp.zeros_like(dc_scr)

        def tile(qb, diagonal):
            q0 = pl.multiple_of(qb * tk, tk)
            qs = q_ref[0, pl.ds(q0, tk), :] * inv
            do_b = do_ref[0, pl.ds(q0, tk), :].astype(BF16)
            for hh in range(2):
                st = lax.dot_general(khs[hh], qs, NT, preferred_element_type=F32)
                pr = jnp.exp(st - ccbs[hh] + st_ref[0, 0, pl.ds(hh, 1), pl.ds(q0, tk)])
                if diagonal:
                    pr = jnp.where(_causal(tk, True), pr, 0.0)
                dp = lax.dot_general(vhs[hh], do_b, NT, preferred_element_type=F32)
                ds = pr * (dp - st_ref[0, 0, pl.ds(2 + hh, 1), pl.ds(q0, tk)])
                dv_scr[...] += jnp.dot(pr.astype(BF16), jnp.where(masks[hh], do_b, jnp.zeros_like(do_b)),
                                       preferred_element_type=F32)
                dk_scr[...] += jnp.dot(ds.astype(BF16), jnp.where(masks[hh], qs, jnp.zeros_like(qs)),
                                       preferred_element_type=F32)
                dc_scr[hh] -= _fold_lanes(ds, jnp.add)

        def later(qb, carry):
            tile(qb, False)
            return carry

        tile(kt, True)
        lax.fori_loop(kt + 1, nq, later, 0)
        dk_ref[0] = dk_scr[...].astype(BF16)
        dv_ref[0] = dv_scr[...].astype(BF16)
        dcs = [jnp.sum(dc_scr[hh], axis=-1, keepdims=True) for hh in range(2)]
        dc_ref[0, 0] = jnp.where(lane == 2 * p, dcs[0], jnp.where(lane == 2 * p + 1, dcs[1], 0.0))

    full = lambda col: pl.BlockSpec((1, S, LANES), col)
    tile_spec = lambda col: pl.BlockSpec((1, tk, LANES), col)
    return hosted_call(
        body, rider, name=name, grid=(B, npair, nq),
        in_specs=[full(lambda b, p, t: (b, 0, QKV0 + p)),
                  tile_spec(lambda b, p, t: (b, t, QKV0 + npair + p)),
                  tile_spec(lambda b, p, t: (b, t, QKV0 + 2 * npair + p)),
                  full(lambda b, p, t: (b, 0, npair + p)),
                  pl.BlockSpec((1, 1, 8, S), lambda b, p, t: (b, p, 0, 0)),
                  tile_spec(lambda b, p, t: (b, t, 0))],
        out_specs=[tile_spec(lambda b, p, t: (b, t, p)), tile_spec(lambda b, p, t: (b, t, p)),
                   pl.BlockSpec((1, 1, tk, LANES), lambda b, p, t: (b, p, t, 0))],
        out_shape=[jax.ShapeDtypeStruct((B, S, FOX_W), BF16)] * 2
        + [jax.ShapeDtypeStruct((B, npair, S, LANES), F32)],
        scratch_shapes=[pltpu.VMEM((tk, LANES), F32), pltpu.VMEM((tk, LANES), F32),
                        pltpu.VMEM((2, tk, LANES), F32)],
        args=(z, z, z, dcat, stats, c_col),
    )


def xattn_fwd(qm, kv, *, name, tq=512):
    B, S, D = qm.shape
    M = kv.shape[1]
    tq = min(tq, S)
    inv = 1.0 / math.sqrt(MEM_HEAD_DIM)

    nq = S // tq

    def body(q_ref, kv_ref, o_ref, ot_ref):
        for h in range(MEM_HEADS):
            c0 = h * MEM_HEAD_DIM
            qh = q_ref[0, :, c0:c0 + MEM_HEAD_DIM]
            kh = kv_ref[0, :, c0:c0 + MEM_HEAD_DIM]
            vh = kv_ref[0, :, D + c0:D + c0 + MEM_HEAD_DIM]
            s = lax.dot_general(qh, kh, NT, preferred_element_type=F32) * inv
            e = jnp.exp(s - jnp.max(s, axis=-1, keepdims=True))
            o = jnp.dot(e.astype(BF16), vh, preferred_element_type=F32) / jnp.sum(e, axis=-1, keepdims=True)
            o_ref[0, :, c0:c0 + MEM_HEAD_DIM] = o.astype(BF16)
            ot_ref[c0:c0 + MEM_HEAD_DIM, :] = o.T.astype(BF16)

    return _call(
        body, name=name, grid=(B, nq),
        in_specs=[pl.BlockSpec((1, tq, D), lambda b, i: (b, i, 0)),
                  pl.BlockSpec((1, M, 2 * D), lambda b, i: (b, 0, 0))],
        out_specs=[pl.BlockSpec((1, tq, D), lambda b, i: (b, i, 0)),
                   pl.BlockSpec((D, tq), lambda b, i: (0, b * nq + i))],
        out_shape=[jax.ShapeDtypeStruct((B, S, D), BF16), jax.ShapeDtypeStruct((D, B * S), BF16)],
        compiler_params=_params(("parallel", "parallel")),
    )(qm, kv)


def xattn_bwd(qm, kv, do, *, name, tq=512):
    B, S, D = qm.shape
    M = kv.shape[1]
    tq = min(tq, S)
    inv = 1.0 / math.sqrt(MEM_HEAD_DIM)

    def body(q_ref, kv_ref, do_ref, dq_ref, dkv_ref):
        @pl.when(pl.program_id(1) == 0)
        def _():
            dkv_ref[...] = jnp.zeros_like(dkv_ref)

        for h in range(MEM_HEADS):
            c0 = h * MEM_HEAD_DIM
            qh = q_ref[0, :, c0:c0 + MEM_HEAD_DIM]
            kh = kv_ref[0, :, c0:c0 + MEM_HEAD_DIM]
            vh = kv_ref[0, :, D + c0:D + c0 + MEM_HEAD_DIM]
            doh = do_ref[0, :, c0:c0 + MEM_HEAD_DIM]
            s = lax.dot_general(qh, kh, NT, preferred_element_type=F32) * inv
            e = jnp.exp(s - jnp.max(s, axis=-1, keepdims=True))
            pr = e / jnp.sum(e, axis=-1, keepdims=True)
            dp = lax.dot_general(doh, vh, NT, preferred_element_type=F32)
            ds = pr * (dp - jnp.sum(pr * dp, axis=-1, keepdims=True))
            ds_b = ds.astype(BF16)
            dq_ref[0, :, c0:c0 + MEM_HEAD_DIM] = (jnp.dot(ds_b, kh, preferred_element_type=F32) * inv).astype(BF16)
            dkv_ref[0, :, c0:c0 + MEM_HEAD_DIM] += lax.dot_general(ds_b, qh, TN, preferred_element_type=F32) * inv
            dkv_ref[0, :, D + c0:D + c0 + MEM_HEAD_DIM] += lax.dot_general(
                pr.astype(BF16), doh, TN, preferred_element_type=F32)

    row = pl.BlockSpec((1, tq, D), lambda b, i: (b, i, 0))
    kvs = pl.BlockSpec((1, M, 2 * D), lambda b, i: (b, 0, 0))
    return _call(
        body, name=name, grid=(B, S // tq), in_specs=[row, kvs, row], out_specs=[row, kvs],
        out_shape=[jax.ShapeDtypeStruct((B, S, D), BF16), jax.ShapeDtypeStruct((B, M, 2 * D), F32)],
        compiler_params=_params(("parallel", "arbitrary")),
    )(qm, kv, do)


SWIGLU_TN = 1408


def _chunks(n, w=256):
    return [(c0, min(w, n - c0)) for c0 in range(0, n, w)]


def mm_swiglu_fwd(hf, w_gu, *, name, tm=512):
    T, D = hf.shape
    Fh = w_gu.shape[1] // 2
    tm, tn = min(tm, T), SWIGLU_TN
    nj = Fh // tn
    assert Fh % tn == 0 and T % tm == 0

    def body(a_ref, bg_ref, bu_ref, g_ref, u_ref, o_ref, ot_ref):
        a = a_ref[...]
        for c0, cw in _chunks(tn):
            cols = pl.ds(c0, cw)
            g = jnp.dot(a, bg_ref[:, cols], preferred_element_type=F32)
            u = jnp.dot(a, bu_ref[:, cols], preferred_element_type=F32)
            act = g * _sigmoid(g) * u
            g_ref[:, cols] = g.astype(BF16)
            u_ref[:, cols] = u.astype(BF16)
            o_ref[:, cols] = act.astype(BF16)
            ot_ref[cols, :] = act.T.astype(BF16)

    tile = pl.BlockSpec((tm, tn), lambda i, j: (i, j))
    return _call(
        body, name=name, grid=(T // tm, nj),
        in_specs=[pl.BlockSpec((tm, D), lambda i, j: (i, 0)), pl.BlockSpec((D, tn), lambda i, j: (0, j)),
                  pl.BlockSpec((D, tn), lambda i, j: (0, nj + j))],
        out_specs=[tile, tile, tile, pl.BlockSpec((tn, tm), lambda i, j: (j, i))],
        out_shape=[jax.ShapeDtypeStruct((T, Fh), BF16)] * 3 + [jax.ShapeDtypeStruct((Fh, T), BF16)],
        compiler_params=_params(("parallel", "parallel"), 48 << 20),
    )(hf, w_gu, w_gu)


def mm_swiglu_bwd(dx, w_down, g, u, *, name, tm=512):
    T, D = dx.shape
    Fh = w_down.shape[0]
    tm, tn = min(tm, T), SWIGLU_TN
    assert Fh % tn == 0 and T % tm == 0

    def body(a_ref, b_ref, g_ref, u_ref, dg_ref, du_ref):
        a = a_ref[...].astype(BF16)
        for c0, cw in _chunks(tn):
            cols = pl.ds(c0, cw)
            d = lax.dot_general(a, b_ref[cols, :], NT, preferred_element_type=F32)
            gv = g_ref[:, cols].astype(F32)
            uv = u_ref[:, cols].astype(F32)
            sg = _sigmoid(gv)
            dg_ref[:, cols] = (d * uv * (sg * (1.0 + gv * (1.0 - sg)))).astype(BF16)
            du_ref[:, cols] = (d * gv * sg).astype(BF16)

    tile = pl.BlockSpec((tm, tn), lambda i, j: (i, j))
    return _call(
        body, name=name, grid=(T // tm, Fh // tn),
        in_specs=[pl.BlockSpec((tm, D), lambda i, j: (i, 0)), pl.BlockSpec((tn, D), lambda i, j: (j, 0)), tile, tile],
        out_specs=[tile, tile],
        out_shape=[jax.ShapeDtypeStruct((T, Fh), BF16)] * 2,
        compiler_params=_params(("parallel", "parallel"), 48 << 20),
    )(dx, w_down, g, u)


LATE_MID = ("w_out", "w_mq", "w_mo")
LATE_KV = ("w_mkv",)
LATE_FFN = ("w_gu", "w_down")
LATE = LATE_MID + LATE_KV + LATE_FFN
RS_GROUPS = (("w_gu", "w_down"), ("w_out", "w_mq", "w_mkv", "w_mo"), ("w_in",))


def pair_sums(names, g42, got):
    return {n: pair_sum(g, o, name="rs_pair_sum_" + n) for n, g, o in zip(names, g42, got)}


def local_step(x, mem, target, sp, first_shards, late_shards):
    B, S, D = x.shape
    T = B * S
    M = mem.shape[1]
    row = lambda v: v.reshape(1, -1).astype(F32)
    g_mix, g_x, g_mem, g_ffn, g_final = (row(sp[k]) for k in ("g_mix", "g_x", "g_mem", "g_ffn", "g_final"))
    conv_b, ln_g, ln_b = row(sp["conv_b"]), row(sp["ln_g"]), row(sp["ln_b"])
    b_f = jnp.pad(row(sp["b_f"]), ((0, 0), (0, LANES - FOX_HEADS)))
    n_ug, n_main = 2 * CONV_CH, 2 * CONV_CH + 3 * FOX_W

    x2d = x.reshape(T, D)
    h, h_t, partly = rmsnorm_fwd(x2d, g_mix, name="rms_mix", rider=AllGatherStage1(first_shards))
    w_in8, cw8 = run_rider(AllGatherStage2(partly), name="ag_first_stage2")
    w_in_full = _full_from_gathered("w_in", w_in8)
    conv_w = cw8.transpose(1, 0, 2).reshape(HALO, -1)
    w_main, w_ug, w_qkv = w_in_full[:, :n_main], w_in_full[:, :n_ug], w_in_full[:, n_ug:n_main]
    w_f = jnp.pad(w_in_full[:, n_main:], ((0, 0), (0, LANES - FOX_HEADS)))
    z = matmul(h, w_main, out_dtype=BF16, tn=n_main, name="mm_in")
    z3 = z.reshape(B, S, n_main)
    n_mid, n_kv = len(LATE_MID), len(LATE_MID) + len(LATE_KV)
    (conv_out, conv_t, conv_y), partly_mid = conv_branch_fwd(z3, conv_w, conv_b, ln_g, ln_b, name="conv_fwd",
                                                     rider=AllGatherStage1(late_shards[:n_mid]))
    (f_raw, c_col, c_row), rode = fgate_fwd(
        h.reshape(B, S, D), w_f, b_f, name="fgate_fwd",
        rider=Riders(AllGatherStage1(late_shards[n_mid:n_kv]), AllGatherStage2(partly_mid)))
    partly_kv, full_mid = rode[:n_kv - n_mid], rode[n_kv - n_mid:]
    (att, lse, att_t), rode = fox_fwd(
        z3, c_col, c_row, name="fox_fwd",
        rider=Riders(AllGatherStage1(late_shards[n_kv:]), AllGatherStage2(partly_kv)))
    partly_ffn, full_kv = rode[:len(LATE_FFN)], rode[len(LATE_FFN):]
    wf = {n: _full_from_gathered(n, blk) for n, blk in zip(LATE_MID + LATE_KV, full_mid + full_kv)}
    x1, full_ffn = matmul([conv_out.reshape(T, CONV_CH), att.reshape(T, FOX_W)], [wf["w_out"], wf["w_out"]],
                          b_blk=[0, 1], out_dtype=F32, res=x2d, tn=D, name="mm_out",
                          rider=AllGatherStage2(partly_ffn))
    wf.update({n: _full_from_gathered(n, blk) for n, blk in zip(LATE_FFN, full_ffn)})
    hx, hx_t = rmsnorm_fwd(x1, g_x, name="rms_x")
    qm = matmul(hx, wf["w_mq"], out_dtype=BF16, tn=D, name="mm_mq")
    mem2d = mem.reshape(B * M, D)
    mem_n, mem_n_t = rmsnorm_fwd(mem2d, g_mem, name="rms_mem")
    kv = matmul(mem_n, wf["w_mkv"], out_dtype=BF16, tn=2 * D, name="mm_mkv").reshape(B, M, 2 * D)
    o, o_t = xattn_fwd(qm.reshape(B, S, D), kv, name="xattn_fwd")
    o = o.reshape(T, D)
    x2 = matmul(o, wf["w_mo"], out_dtype=F32, res=x1, tn=D, name="mm_mo")
    hf, hf_t = rmsnorm_fwd(x2, g_ffn, name="rms_ffn")
    gate, up, act, act_t = mm_swiglu_fwd(hf, wf["w_gu"], name="mm_gu")
    x3 = matmul(act, wf["w_down"], out_dtype=F32, res=x2, tn=D, name="mm_down")
    dx3, dg_final, loss = final_loss_bwd(x3, g_final, target.reshape(T, D), name="loss_bwd")
    gw = {}
    gw["w_down"] = matmul(act_t, dx3, out_dtype=BF16, tm=1408, tn=512, name="dw_down")
    dgate, dup = mm_swiglu_bwd(dx3, wf["w_down"], gate, up, name="dx_down")
    gw["w_gu"] = [matmul(hf_t, dgate, out_dtype=BF16, tn=SWIGLU_TN, name="dw_gate"),
                  matmul(hf_t, dup, out_dtype=BF16, tn=SWIGLU_TN, name="dw_up")]
    g42 = [_shards_from_full(n, gw[n]) for n in RS_GROUPS[0]]
    dhf, got = matmul([dgate, dup], [wf["w_gu"], wf["w_gu"]], b_blk=[0, 1], tb=True, out_dtype=BF16,
                      tm=256, tn=D, name="dx_gu", rider=SiblingExchange(g42))
    parts = pair_sums(RS_GROUPS[0], g42, got)
    dx2, dg_ffn = rmsnorm_bwd(x2, g_ffn, dhf, dx3, name="rms_ffn_bwd")
    gw["w_mo"] = matmul(o_t, dx2, out_dtype=BF16, tn=D, name="dw_mo")
    do = matmul(dx2, wf["w_mo"], tb=True, out_dtype=BF16, tn=D, name="dx_mo")
    dqm, dkv = xattn_bwd(qm.reshape(B, S, D), kv, do.reshape(B, S, D), name="xattn_bwd")
    dqm = dqm.reshape(T, D)
    dkv = dkv.reshape(B * M, 2 * D)
    gw["w_mq"] = matmul(hx_t, dqm, out_dtype=BF16, tn=D, name="dw_mq")
    dhx = matmul(dqm, wf["w_mq"], tb=True, out_dtype=BF16, tn=D, name="dx_mq")
    gw["w_mkv"] = matmul(mem_n_t, dkv, out_dtype=BF16, tn=D, name="dw_mkv")
    dmem_n = matmul(dkv, wf["w_mkv"], tb=True, out_dtype=BF16, tn=D, name="dx_mkv")
    _, dg_mem = rmsnorm_bwd(mem2d, g_mem, dmem_n, None, name="rms_mem_bwd")
    dx1, dg_x = rmsnorm_bwd(x1, g_x, dhx, dx2, name="rms_x_bwd")
    gw["w_out"] = jnp.concatenate([matmul(conv_t, dx1, out_dtype=BF16, tn=D, name="dw_out_conv"),
                                   matmul(att_t, dx1, out_dtype=BF16, tn=D, name="dw_out_att")], axis=0)
    g42 = [_shards_from_full(n, gw[n]) for n in RS_GROUPS[1]]
    dcat, got = matmul(dx1, wf["w_out"], tb=True, out_dtype=BF16, tn=D, name="dx_out", rider=SiblingExchange(g42))
    dcat = dcat.reshape(B, S, D)
    parts.update(pair_sums(RS_GROUPS[1], g42, got))
    dy, dconv_w, dvec = conv_branch_bwd_a(z3, conv_y, dcat, ln_g, ln_b, name="conv_bwd_a")
    dug = conv_branch_bwd_b(z3, dy, conv_w, name="conv_bwd_b")
    gots = {}
    (dq, stats), got = fox_bwd_dq(z3, dcat, lse, c_col, c_row, name="fox_bwd_dq",
                                  rider=ChipExchange([parts[n] for n in RS_GROUPS[0]]))
    gots.update(zip(RS_GROUPS[0], got))
    (dk, dv, dc), got = fox_bwd_dkdv(z3, dcat, stats, c_col, name="fox_bwd_dkdv",
                                     rider=ChipExchange([parts[n] for n in RS_GROUPS[1]]))
    gots.update(zip(RS_GROUPS[1], got))
    df, db_f = fgate_bwd(dc, f_raw, b_f, name="fgate_bwd")
    dug2 = dug.reshape(T, n_ug)
    dqkv = jnp.concatenate([dq, dk, dv], axis=-1).reshape(T, 3 * FOX_W)
    df2 = df.reshape(T, LANES)
    dw_in = [matmul(h_t, dug2, out_dtype=BF16, tn=n_ug, name="dw_in_ug"),
             matmul(h_t, dqkv, out_dtype=BF16, tn=3 * FOX_W, name="dw_in_qkv"),
             matmul(h_t, df2, out_dtype=BF16, name="dw_f")[:, :FOX_HEADS]]
    g42 = [_shards_from_full("w_in", dw_in)]
    parts.update(pair_sums(RS_GROUPS[2], g42, run_rider(SiblingExchange(g42), name="rs_sibling_in")))
    dh, (gots["w_in"],) = matmul([dug2, dqkv, df2], [w_ug, w_qkv, w_f], tb=True, out_dtype=BF16, tn=D,
                                 name="dx_in", rider=ChipExchange([parts["w_in"]]))
    dx, dg_mix = rmsnorm_bwd(x2d, g_mix, dh, dx1, name="rms_mix_bwd", out_dtype=F32)
    gs = dict(g_mix=dg_mix, b_f=db_f[:, :FOX_HEADS], conv_w=dconv_w[:CONV_K], conv_b=dvec[0:1],
              ln_g=dvec[1:2], ln_b=dvec[2:3], g_x=dg_x, g_mem=dg_mem, g_ffn=dg_ffn, g_final=dg_final)
    return loss, dx.reshape(B, S, D), gs, {n: (parts[n], gots[n]) for n in BIG}


def _me():
    return lax.axis_index("x"), lax.axis_index("y"), lax.axis_index("c")


def _any_specs(n):
    return [pl.BlockSpec(memory_space=pl.ANY)] * n


def all_gather(xs, *, name):
    n = len(xs)

    def body(*refs):
        x_refs, out_refs = refs[:n], refs[n:2 * n]
        send_sems, recv_sems, local_sems = refs[2 * n:]
        x, y, c = _me()
        me, sibling = (x, y, c), (x, y, 1 - c)
        chips = [(1 - x, y), (x, 1 - y), (1 - x, 1 - y)]

        def slot(a, px, py, pc):
            return out_refs[a].at[4 * px + 2 * py + pc]

        def copy(a, k, block, to, own=False):
            return pltpu.make_async_remote_copy(
                src_ref=x_refs[a] if own else slot(a, *block), dst_ref=slot(a, *block),
                send_sem=send_sems.at[k, a], recv_sem=recv_sems.at[k, a], device_id=to, device_id_type=MESH)

        mine = [pltpu.make_async_copy(x_refs[a], slot(a, *me), local_sems.at[a]) for a in range(n)]
        first = [copy(a, 0, me, sibling, own=True) for a in range(n)]
        first += [copy(a, 1 + j, me, (*chip, c), own=True) for j, chip in enumerate(chips) for a in range(n)]
        for cp in mine + first:
            cp.start()
        passed = []
        for j, chip in enumerate(chips):
            for a in range(n):
                copy(a, 1 + j, (*chip, c), me).wait_recv()
                passed.append(copy(a, 4 + j, (*chip, c), sibling))
                passed[-1].start()
        for a in range(n):
            copy(a, 0, sibling, me).wait_recv()
            for j, chip in enumerate(chips):
                copy(a, 4 + j, (*chip, 1 - c), me).wait_recv()
        for cp in first + passed:
            cp.wait_send()
        for cp in mine:
            cp.wait()

    return _call(
        body, name=name, in_specs=_any_specs(n), out_specs=_any_specs(n),
        out_shape=[jax.ShapeDtypeStruct((N_DEV,) + v.shape, v.dtype) for v in xs],
        scratch_shapes=[pltpu.SemaphoreType.DMA((7, n)), pltpu.SemaphoreType.DMA((7, n)),
                        pltpu.SemaphoreType.DMA((n,))],
    )(*xs)


SIBLING_BARRIER = 1
CHIPS_BARRIER = 2
GATHER_BARRIER = 3


class SiblingExchange:
    collective_id = SIBLING_BARRIER

    def __init__(self, gs):
        n = len(gs)
        self.n, self.inputs = n, list(gs)
        self.out_shape = [jax.ShapeDtypeStruct((4,) + g.shape[2:], g.dtype) for g in gs]
        self.scratch = [pltpu.SemaphoreType.DMA((n,)), pltpu.SemaphoreType.DMA((n,))]

    @staticmethod
    def barrier_peers():
        x, y, c = _me()
        return [(x, y, 1 - c)]

    def _copies(self, g_refs, out_refs, sems):
        send_sems, recv_sems = sems
        x, y, c = _me()
        return [pltpu.make_async_remote_copy(
            src_ref=g_refs[a].at[:, 1 - c], dst_ref=out_refs[a], send_sem=send_sems.at[a],
            recv_sem=recv_sems.at[a], device_id=(x, y, 1 - c), device_id_type=MESH) for a in range(self.n)]

    def start(self, in_refs, out_refs, sems):
        for cp in self._copies(in_refs, out_refs, sems):
            cp.start()

    def finish(self, in_refs, out_refs, sems):
        for cp in self._copies(in_refs, out_refs, sems):
            cp.wait()


def run_rider(rider, *, name):
    return hosted_call(None, rider, name=name, grid=(), in_specs=[], out_specs=[], out_shape=[],
                       scratch_shapes=[], args=[])[1]


class ChipExchange:
    collective_id = CHIPS_BARRIER

    @staticmethod
    def barrier_peers():
        x, y, c = _me()
        return [(1 - x, y, c), (x, 1 - y, c), (1 - x, 1 - y, c)]

    def __init__(self, ps):
        n = len(ps)
        self.n, self.inputs = n, list(ps)
        self.out_shape = [jax.ShapeDtypeStruct(p.shape, p.dtype) for p in ps]
        self.scratch = [pltpu.SemaphoreType.DMA((3, n)), pltpu.SemaphoreType.DMA((3, n))]

    def _copies(self, p_refs, out_refs, sems, outgoing):
        send_sems, recv_sems = sems
        x, y, c = _me()
        my_chip = 2 * x + y
        cps = []
        for k in range(3):
            px, py = x ^ ((k + 1) >> 1), y ^ ((k + 1) & 1)
            src, dst = (2 * px + py, my_chip) if outgoing else (my_chip, 2 * px + py)
            for a in range(self.n):
                cps.append(pltpu.make_async_remote_copy(
                    src_ref=p_refs[a].at[src], dst_ref=out_refs[a].at[dst], send_sem=send_sems.at[k, a],
                    recv_sem=recv_sems.at[k, a], device_id=(px, py, c), device_id_type=MESH))
        return cps

    def start(self, in_refs, out_refs, sems):
        for cp in self._copies(in_refs, out_refs, sems, True):
            cp.start()

    def finish(self, in_refs, out_refs, sems):
        for cp in self._copies(in_refs, out_refs, sems, False):
            cp.wait_recv()
        for cp in self._copies(in_refs, out_refs, sems, True):
            cp.wait_send()


class AllGatherStage1:
    collective_id = GATHER_BARRIER

    @staticmethod
    def barrier_peers():
        x, y, c = _me()
        return [(x, y, 1 - c), (1 - x, y, c), (x, 1 - y, c), (1 - x, 1 - y, c)]

    def __init__(self, xs):
        n = len(xs)
        self.n, self.inputs = n, list(xs)
        self.out_shape = [jax.ShapeDtypeStruct((N_DEV,) + v.shape, v.dtype) for v in xs]
        self.scratch = [pltpu.SemaphoreType.DMA((4, n)), pltpu.SemaphoreType.DMA((4, n)),
                        pltpu.SemaphoreType.DMA((n,))]

    def _copies(self, x_refs, out_refs, sems, kind):
        send_sems, recv_sems, local_sems = sems
        x, y, c = _me()
        slot = lambda a, d: out_refs[a].at[4 * d[0] + 2 * d[1] + d[2]]
        if kind == "local":
            return [pltpu.make_async_copy(x_refs[a], slot(a, (x, y, c)), local_sems.at[a]) for a in range(self.n)]
        cps = []
        for k, peer in enumerate([(x, y, 1 - c), (1 - x, y, c), (x, 1 - y, c), (1 - x, 1 - y, c)]):
            for a in range(self.n):
                cps.append(pltpu.make_async_remote_copy(
                    src_ref=x_refs[a], dst_ref=slot(a, (x, y, c) if kind == "out" else peer),
                    send_sem=send_sems.at[k, a], recv_sem=recv_sems.at[k, a], device_id=peer, device_id_type=MESH))
        return cps

    def start(self, in_refs, out_refs, sems):
        for cp in self._copies(in_refs, out_refs, sems, "local") + self._copies(in_refs, out_refs, sems, "out"):
            cp.start()

    def finish(self, in_refs, out_refs, sems):
        for cp in self._copies(in_refs, out_refs, sems, "in"):
            cp.wait_recv()
        for cp in self._copies(in_refs, out_refs, sems, "out"):
            cp.wait_send()
        for cp in self._copies(in_refs, out_refs, sems, "local"):
            cp.wait()


class AllGatherStage2:
    collective_id = SIBLING_BARRIER

    @staticmethod
    def barrier_peers():
        x, y, c = _me()
        return [(x, y, 1 - c)]

    def __init__(self, outs):
        n = len(outs)
        self.n, self.inputs = n, list(outs)
        self.out_shape = [jax.ShapeDtypeStruct(o.shape, o.dtype) for o in outs]
        self.scratch = [pltpu.SemaphoreType.DMA((3, n)), pltpu.SemaphoreType.DMA((3, n))]
        self.aliases = {a: a for a in range(n)}

    def _copies(self, out_refs, sems, outgoing):
        send_sems, recv_sems = sems
        x, y, c = _me()
        cps = []
        for k, (px, py) in enumerate([(1 - x, y), (x, 1 - y), (1 - x, 1 - y)]):
            for a in range(self.n):
                cps.append(pltpu.make_async_remote_copy(
                    src_ref=out_refs[a].at[4 * px + 2 * py + c],
                    dst_ref=out_refs[a].at[4 * px + 2 * py + (c if outgoing else 1 - c)],
                    send_sem=send_sems.at[k, a], recv_sem=recv_sems.at[k, a], device_id=(x, y, 1 - c),
                    device_id_type=MESH))
        return cps

    def start(self, in_refs, out_refs, sems):
        for cp in self._copies(out_refs, sems, True):
            cp.start()

    def finish(self, in_refs, out_refs, sems):
        for cp in self._copies(out_refs, sems, False):
            cp.wait_recv()
        for cp in self._copies(out_refs, sems, True):
            cp.wait_send()


class Riders:
    def __init__(self, *riders):
        self.riders = riders
        self.collective_id = riders[0].collective_id
        self.barrier_peers = riders[0].barrier_peers
        self.inputs = [v for r in riders for v in r.inputs]
        self.out_shape = [s for r in riders for s in r.out_shape]
        self.scratch = [s for r in riders for s in r.scratch]
        self.aliases, i0, o0 = {}, 0, 0
        for r in riders:
            self.aliases.update({i0 + i: o0 + o for i, o in getattr(r, "aliases", {}).items()})
            i0, o0 = i0 + len(r.inputs), o0 + len(r.out_shape)

    def _split(self, in_refs, out_refs, sems):
        i0 = o0 = s0 = 0
        for r in self.riders:
            ni, no, ns = len(r.inputs), len(r.out_shape), len(r.scratch)
            yield r, in_refs[i0:i0 + ni], out_refs[o0:o0 + no], sems[s0:s0 + ns]
            i0, o0, s0 = i0 + ni, o0 + no, s0 + ns

    def start(self, in_refs, out_refs, sems):
        for r, i, o, s in self._split(in_refs, out_refs, sems):
            r.start(i, o, s)

    def finish(self, in_refs, out_refs, sems):
        for r, i, o, s in self._split(in_refs, out_refs, sems):
            r.finish(i, o, s)


def _peer_barrier(peers):
    barrier = pltpu.get_barrier_semaphore()
    for peer in peers:
        pl.semaphore_signal(barrier, inc=1, device_id=peer, device_id_type=MESH)
    pl.semaphore_wait(barrier, len(peers))


def hosted_call(body, rider, *, name, grid, in_specs, out_specs, out_shape, scratch_shapes, args, vmem=None):
    n_in, n_out, n_scr = len(in_specs), len(out_specs), len(scratch_shapes)
    r_in, r_out = (len(rider.inputs), len(rider.out_shape)) if rider is not None else (0, 0)
    own_barrier = getattr(rider, "collective_id", None) is not None

    def wrapped(*refs):
        ins, refs = refs[:n_in], refs[n_in:]
        rins, refs = refs[:r_in], refs[r_in:]
        outs, refs = refs[:n_out], refs[n_out:]
        routs, refs = refs[:r_out], refs[r_out:]
        scr, rscr = refs[:n_scr], refs[n_scr:]
        ids = [pl.program_id(d) for d in range(len(grid))]
        first = functools.reduce(jnp.logical_and, [i == 0 for i in ids], True)
        last = functools.reduce(jnp.logical_and, [i == g - 1 for i, g in zip(ids, grid)], True)

        def begin():
            if own_barrier:
                _peer_barrier(rider.barrier_peers())
            rider.start(rins, routs, rscr)

        if rider is not None and grid:
            pl.when(first)(begin)
        elif rider is not None:
            begin()
        if body is not None:
            body(*ins, *outs, *scr)
        if rider is not None and grid:
            pl.when(last)(lambda: rider.finish(rins, routs, rscr))
        elif rider is not None:
            rider.finish(rins, routs, rscr)

    kw = dict(grid=grid) if grid else {}
    aliases = getattr(rider, "aliases", {})
    if aliases:
        kw["input_output_aliases"] = {n_in + i: n_out + o for i, o in aliases.items()}
    if grid or vmem is not None or own_barrier:
        kw["compiler_params"] = _params(("arbitrary",) * len(grid) if grid else None, vmem,
                                        rider.collective_id if own_barrier else None)
    res = _call(
        wrapped, name=name, in_specs=list(in_specs) + _any_specs(r_in), out_specs=list(out_specs) + _any_specs(r_out),
        out_shape=list(out_shape) + (rider.out_shape if rider is not None else []),
        scratch_shapes=list(scratch_shapes) + (rider.scratch if rider is not None else []), **kw,
    )(*args, *(rider.inputs if rider is not None else []))
    return list(res[:n_out]), list(res[n_out:])


def _pick_rows(r, target=256):
    best = None
    for d in range(16, min(r, target) + 1, 16):
        if r % d == 0:
            best = d
    return r if best is None else best


def pair_sum(g, got, *, name):
    _, _, R, C = g.shape
    tr = _pick_rows(R)

    def body(g_ref, got_ref, o_ref):
        mine = jnp.where(lax.axis_index("c") == 0, g_ref[:, 0], g_ref[:, 1])
        o_ref[...] = (mine.astype(F32) + got_ref[...].astype(F32)).astype(o_ref.dtype)

    return _call(
        body, name=name, grid=(R // tr,),
        in_specs=[pl.BlockSpec((4, 2, tr, C), lambda i: (0, 0, i, 0)), pl.BlockSpec((4, tr, C), lambda i: (0, i, 0))],
        out_specs=pl.BlockSpec((4, tr, C), lambda i: (0, i, 0)),
        out_shape=jax.ShapeDtypeStruct((4, R, C), g.dtype),
        compiler_params=_params(("parallel",)),
    )(g, got)


def chip_sum_adamw(p, got, w, m, v, *, name):
    _, R, C = p.shape
    assert w.shape == (1, R, C), (name, w.shape, p.shape)
    tr = _pick_rows(R)

    def body(p_ref, got_ref, w_ref, m_ref, v_ref, g_ref, d_ref, mo_ref, vo_ref):
        my_chip = 2 * lax.axis_index("x") + lax.axis_index("y")
        g = jnp.zeros((tr, C), F32)
        for j in range(4):
            g = g + jnp.where(my_chip == j, p_ref[j], got_ref[j]).astype(F32)
        g_ref[0] = g
        d_ref[0], mo_ref[0], vo_ref[0] = _adamw_math(w_ref[0], g, m_ref[0], v_ref[0])

    part = pl.BlockSpec((4, tr, C), lambda i: (0, i, 0))
    spec = pl.BlockSpec((1, tr, C), lambda i: (0, i, 0))
    return _call(
        body, name=name, grid=(R // tr,), in_specs=[part, part, spec, spec, spec], out_specs=[spec] * 4,
        out_shape=[jax.ShapeDtypeStruct((1, R, C), F32)] * 4,
        compiler_params=_params(("parallel",)),
    )(p, got, w, m, v)


def rows_sum(g8, *, name):
    _, R, C = g8.shape

    def body(g_ref, o_ref):
        acc = g_ref[0]
        for j in range(1, N_DEV):
            acc = acc + g_ref[j]
        o_ref[...] = acc

    return _call(body, name=name, out_shape=jax.ShapeDtypeStruct((R, C), F32))(g8)


def _adamw_math(w, g, m, v):
    m = ADAM_B1 * m + (1.0 - ADAM_B1) * g
    v = ADAM_B2 * v + (1.0 - ADAM_B2) * (g * g)
    m_hat = m / (1.0 - ADAM_B1 ** ADAM_STEP)
    v_hat = v / (1.0 - ADAM_B2 ** ADAM_STEP)
    delta = -ADAM_LR * (m_hat / (jnp.sqrt(v_hat) + ADAM_EPS) + ADAM_WD * w)
    return delta, m, v


def to_bf16(xs, *, name):
    def body(*refs):
        for x_ref, o_ref in zip(refs[:len(xs)], refs[len(xs):]):
            o_ref[...] = x_ref[...].astype(BF16)

    total = sum(_nbytes(v.shape, F32) + _nbytes(v.shape, BF16) for v in xs)
    return _call(body, name=name, out_shape=[jax.ShapeDtypeStruct(v.shape, BF16) for v in xs],
                 compiler_params=_params(vmem=2 * total + (4 << 20)))(*xs)


def adamw_small(wgmv, *, name):
    n = len(wgmv)

    def body(*refs):
        ins, outs = refs[:4 * n], refs[4 * n:]
        for a in range(n):
            w_ref, g_ref, m_ref, v_ref = ins[4 * a:4 * a + 4]
            d, mn, vn = _adamw_math(w_ref[...], g_ref[...], m_ref[...], v_ref[...])
            outs[3 * a][...] = d
            outs[3 * a + 1][...] = mn
            outs[3 * a + 2][...] = vn

    flat = [t for tup in wgmv for t in tup]
    res = _call(
        body, name=name,
        out_shape=[jax.ShapeDtypeStruct(tup[0].shape, F32) for tup in wgmv for _ in range(3)],
    )(*flat)
    return [tuple(res[3 * a:3 * a + 3]) for a in range(n)]


BIG = ("w_in", "w_out", "w_mq", "w_mkv", "w_mo", "w_gu", "w_down")
COL_SHARDED = ("w_in", "w_mkv", "w_gu")
SMALL = ("g_mix", "b_f", "conv_w", "conv_b", "ln_g", "ln_b", "g_x", "g_mem", "g_ffn", "g_final")


def _full_from_gathered(n, blk):
    _, rr, cc = blk.shape
    if n in COL_SHARDED:
        return jnp.concatenate([blk[k] for k in range(N_DEV)], axis=1)
    return blk.reshape(N_DEV * rr, cc)


def _shards_from_full(n, g):
    pieces = g if isinstance(g, list) else [g]
    rr, cc = pieces[0].shape[0], sum(p.shape[1] for p in pieces)
    if n in COL_SHARDED:
        w = cc // N_DEV
        return jnp.stack([_columns(pieces, k * w, w) for k in range(N_DEV)]).reshape(4, 2, rr, w)
    return pieces[0].reshape(4, 2, rr // N_DEV, cc)


def _columns(pieces, start, width):
    out, c0 = [], 0
    for p in pieces:
        lo, hi = max(start, c0), min(start + width, c0 + p.shape[1])
        if lo < hi:
            out.append(p[:, lo - c0:hi - c0])
        c0 += p.shape[1]
    return out[0] if len(out) == 1 else jnp.concatenate(out, axis=1)


def _small_layout():
    sizes = dict(g_mix=1024, b_f=8, conv_w=CONV_K * CONV_CH, conv_b=512, ln_g=512, ln_b=512, g_x=1024,
                 g_mem=1024, g_ffn=1024, g_final=1024, loss=1)
    lay, r0 = {}, 0
    for n, sz in sizes.items():
        r = -(-sz // LANES)
        lay[n] = (r0, r, sz)
        r0 += r
    return lay, -(-r0 // 8) * 8


def kernel(x, mem, g_mix, w_in, b_f, conv_w, conv_b, ln_g, ln_b, w_out, g_x, g_mem, w_mq, w_mkv, w_mo, g_ffn, w_gu, w_down, g_final, loss_target, m_g_mix, m_w_in, m_b_f, m_conv_w, m_conv_b, m_ln_g, m_ln_b, m_w_out, m_g_x, m_g_mem, m_w_mq, m_w_mkv, m_w_mo, m_g_ffn, m_w_gu, m_w_down, m_g_final, v_g_mix, v_w_in, v_b_f, v_conv_w, v_conv_b, v_ln_g, v_ln_b, v_w_out, v_g_x, v_g_mem, v_w_mq, v_w_mkv, v_w_mo, v_g_ffn, v_w_gu, v_w_down, v_g_final):
    names = ["g_mix", "w_in", "b_f", "conv_w", "conv_b", "ln_g", "ln_b", "w_out", "g_x", "g_mem", "w_mq",
             "w_mkv", "w_mo", "g_ffn", "w_gu", "w_down", "g_final"]
    W = dict(zip(names, [g_mix, w_in, b_f, conv_w, conv_b, ln_g, ln_b, w_out, g_x, g_mem, w_mq, w_mkv, w_mo,
                         g_ffn, w_gu, w_down, g_final]))
    Mo = dict(zip(names, [m_g_mix, m_w_in, m_b_f, m_conv_w, m_conv_b, m_ln_g, m_ln_b, m_w_out, m_g_x, m_g_mem,
                          m_w_mq, m_w_mkv, m_w_mo, m_g_ffn, m_w_gu, m_w_down, m_g_final]))
    Vo = dict(zip(names, [v_g_mix, v_w_in, v_b_f, v_conv_w, v_conv_b, v_ln_g, v_ln_b, v_w_out, v_g_x, v_g_mem,
                          v_w_mq, v_w_mkv, v_w_mo, v_g_ffn, v_w_gu, v_w_down, v_g_final]))
    dev = 4 * lax.axis_index("x") + 2 * lax.axis_index("y") + lax.axis_index("c")

    two = lambda a: a.reshape(-1, a.shape[-1])
    cw_shard = jnp.pad(two(conv_w), ((0, HALO - CONV_K), (0, 0)))
    sp = dict(g_mix=g_mix, b_f=b_f, conv_b=conv_b, ln_g=ln_g, ln_b=ln_b, g_x=g_x, g_mem=g_mem,
              g_ffn=g_ffn, g_final=g_final)
    shards = to_bf16([two(W[n]) for n in ("w_in",) + LATE], name="cast_shards")
    loss_blk, grad_x, gs, reduced = local_step(x, mem, loss_target, sp, [shards[0], cw_shard], shards[1:])

    lay, rs = _small_layout()
    small = {**{n: gs[n] for n in SMALL}, "loss": loss_blk[:, :1]}
    parts = []
    for n, (r0, r, sz) in lay.items():
        flat = small[n].reshape(-1).astype(F32)
        parts.append(jnp.pad(flat, (0, r * LANES - sz)).reshape(r, LANES))
    spack = jnp.concatenate(parts, axis=0)
    spack = jnp.pad(spack, ((0, rs - spack.shape[0]), (0, 0)))
    ssum = rows_sum(all_gather([spack], name="ag_small")[0], name="small_sum")
    gsmall = {n: ssum[r0:r0 + r].reshape(-1)[:sz] for n, (r0, r, sz) in lay.items()}
    loss = gsmall["loss"].reshape(())

    grads, delta, new_m, new_v = {}, {}, {}, {}
    for n in BIG:
        p, o = reduced[n]
        grads[n], delta[n], new_m[n], new_v[n] = chip_sum_adamw(p, o, W[n], Mo[n], Vo[n], name="adamw_" + n)
    for n in SMALL:
        if n == "conv_w":
            full = gsmall[n].reshape(CONV_K, CONV_CH)
            ncol = conv_w.shape[-1]
            grads[n] = lax.dynamic_slice(full, (0, dev * ncol), (CONV_K, ncol)).reshape(conv_w.shape)
        else:
            grads[n] = gsmall[n].reshape(W[n].shape)
    upd = adamw_small([(two(W[n]), two(grads[n]), two(Mo[n]), two(Vo[n])) for n in SMALL], name="adamw_small")
    for n, (d, mn, vn) in zip(SMALL, upd):
        shp = W[n].shape
        delta[n], new_m[n], new_v[n] = d.reshape(shp), mn.reshape(shp), vn.reshape(shp)
    return (loss, grad_x, *[grads[n] for n in names], *[delta[n] for n in names],
            *[new_m[n] for n in names], *[new_v[n] for n in names])
```

```python
import functools
import math

import jax
import jax.numpy as jnp
from jax import lax
from jax.experimental import pallas as pl
from jax.experimental.pallas import tpu as pltpu

F32 = jnp.float32
BF16 = jnp.bfloat16
EPS = 1e-6
N_DEV = 8
CONV_CH = 512
CONV_K = 31
FOX_HEADS = 8
FOX_HEAD_DIM = 64
FOX_W = 512
MEM_HEADS = 4
MEM_HEAD_DIM = 256
HALO = 32
LANES = 128
ADAM_LR, ADAM_B1, ADAM_B2, ADAM_EPS, ADAM_WD, ADAM_STEP = 0.001, 0.9, 0.999, 1e-08, 0.01, 10
NEG = -1e30
VMEM_CAP = 60 * 1024 * 1024
MESH = pl.DeviceIdType.MESH


def _call(body, **kw):
    kw["out_shape"] = jax.tree.map(lambda s: pltpu.HBM(s.shape, s.dtype), kw["out_shape"])
    call = pl.pallas_call(body, **kw)
    return lambda *args: call(*[pltpu.with_memory_space_constraint(a, pltpu.HBM) for a in args])


def _params(sem=None, vmem=None, collective_id=None):
    kw = {} if collective_id is None else {"collective_id": collective_id}
    if sem is not None:
        kw["dimension_semantics"] = sem
    if vmem is not None:
        kw["vmem_limit_bytes"] = int(min(VMEM_CAP, vmem))
    return pltpu.CompilerParams(**kw)


def _nbytes(shape, dtype):
    return math.prod(shape) * jnp.dtype(dtype).itemsize


def _pick(n, target):
    best = None
    for d in range(LANES, min(n, target) + 1, LANES):
        if n % d == 0:
            best = d
    return n if best is None else best


class RowEpilogue:
    def __init__(self, fn, ins, outs):
        self.fn, self.ins, self.outs = fn, list(ins), list(outs)


def matmul(a, b, *, tb=False, out_dtype=None, res=None, tm=512, tn=512, name, rider=None, b_blk=None, post=None):
    a_list = list(a) if isinstance(a, (list, tuple)) else [a]
    b_list = list(b) if isinstance(b, (list, tuple)) else [b]
    n = len(a_list)
    assert len(b_list) == n
    M = a_list[0].shape[0]
    N = b_list[0].shape[0] if tb else b_list[0].shape[1]
    tm, tn = _pick(M, tm), _pick(N, tn)
    assert M % tm == 0 and N % tn == 0, (name, M, N, tm, tn)
    dn = (((1,), (1 if tb else 0,)), ((), ()))

    n_res = int(res is not None)
    n_pin = len(post.ins) if post is not None else 0

    def body(*refs):
        acc = None
        for a_ref, b_ref in zip(refs[:n], refs[n:2 * n]):
            p = lax.dot_general(a_ref[...].astype(BF16), b_ref[...].astype(BF16), dn, preferred_element_type=F32)
            acc = p if acc is None else acc + p
        if res is not None:
            acc = acc + refs[2 * n][...].astype(F32)
        if post is None:
            refs[-1][...] = acc.astype(out_dtype)
            return
        first_in = 2 * n + n_res
        vals = post.fn(acc, *[r[...] for r in refs[first_in:first_in + n_pin]])
        for (dtype, kind), o_ref, val in zip(post.outs, refs[first_in + n_pin:], vals):
            if kind in ("row", "rowT"):
                o_ref[...] = val.astype(dtype)
            else:
                @pl.when(pl.program_id(0) == 0)
                def _(o_ref=o_ref):
                    o_ref[...] = jnp.zeros_like(o_ref)

                o_ref[...] += jnp.broadcast_to(val, o_ref.shape).astype(dtype)

    o_spec = pl.BlockSpec((tm, tn), lambda i, j: (i, j))
    in_specs, est = [], 2 * _nbytes((tm, tn), out_dtype or F32) + 2 * _nbytes((tm, tn), F32)
    for av in a_list:
        assert av.shape[0] == M
        in_specs.append(pl.BlockSpec((tm, av.shape[1]), lambda i, j: (i, 0)))
        est += (2 * jnp.dtype(av.dtype).itemsize + (av.dtype != BF16) * 2) * tm * av.shape[1]
    for idx, (av, bv) in enumerate(zip(a_list, b_list)):
        K = av.shape[1]
        kb = 0 if b_blk is None else b_blk[idx]
        assert bv.shape[0 if tb else 1] == N and bv.shape[1 if tb else 0] >= (kb + 1) * K, (name, av.shape, bv.shape)
        assert b_blk is not None or bv.shape[1 if tb else 0] == K, (name, av.shape, bv.shape)
        in_specs.append(pl.BlockSpec((tn, K), lambda i, j, kb=kb: (j, kb)) if tb
                        else pl.BlockSpec((K, tn), lambda i, j, kb=kb: (kb, j)))
        est += (2 * jnp.dtype(bv.dtype).itemsize + (bv.dtype != BF16) * 2) * tn * K
    args = a_list + b_list
    if res is not None:
        in_specs.append(o_spec)
        args.append(res)
        est += 2 * _nbytes((tm, tn), res.dtype)
    if post is None:
        out_specs, out_shape = [o_spec], [jax.ShapeDtypeStruct((M, N), out_dtype)]
    else:
        assert tn == N, (name, tn, N)
        row = pl.BlockSpec((tm, N), lambda i, j: (i, 0))
        for arr, kind in post.ins:
            in_specs.append(row if kind == "row" else pl.BlockSpec((1, N), lambda i, j: (0, 0)))
            args.append(arr)
            est += 2 * _nbytes((tm, N), arr.dtype) * (kind == "row")
        specs = {"row": (row, (M, N)), "rowT": (pl.BlockSpec((N, tm), lambda i, j: (0, i)), (N, M)),
                 "vec": (pl.BlockSpec((1, N), lambda i, j: (0, 0)), (1, N)),
                 "lanes": (pl.BlockSpec((1, LANES), lambda i, j: (0, 0)), (1, LANES))}
        out_specs = [specs[kind][0] for _, kind in post.outs]
        out_shape = [jax.ShapeDtypeStruct(specs[kind][1], dtype) for dtype, kind in post.outs]
        est += sum(2 * _nbytes((tm, N), dtype) + _nbytes((tm, N), F32) for dtype, kind in post.outs if kind[:3] == "row")
    outs, rode = hosted_call(
        body, rider, name=name, grid=(M // tm, N // tn), in_specs=in_specs, out_specs=out_specs,
        out_shape=out_shape, scratch_shapes=[], args=args, vmem=est + (8 << 20),
    )
    result = outs[0] if post is None else outs
    return result if rider is None else (result, rode)


def _rms_scale(x):
    return lax.rsqrt(jnp.mean(x * x, axis=-1, keepdims=True) + EPS)


def rmsnorm_fwd(x, g, *, name, tm=512, rider=None):
    T, D = x.shape
    tm = min(tm, T)

    def body(x_ref, g_ref, o_ref, ot_ref):
        xv = x_ref[...]
        h = xv * _rms_scale(xv) * g_ref[...]
        o_ref[...] = h.astype(BF16)
        ot_ref[...] = h.T.astype(BF16)

    (h, h_t), rode = hosted_call(
        body, rider, name=name, grid=(T // tm,),
        in_specs=[pl.BlockSpec((tm, D), lambda i: (i, 0)), pl.BlockSpec((1, D), lambda i: (0, 0))],
        out_specs=[pl.BlockSpec((tm, D), lambda i: (i, 0)), pl.BlockSpec((D, tm), lambda i: (0, i))],
        out_shape=[jax.ShapeDtypeStruct((T, D), BF16), jax.ShapeDtypeStruct((D, T), BF16)],
        scratch_shapes=[], args=(x, g),
    )
    return (h, h_t) if rider is None else (h, h_t, rode)


def _rms_bwd_math(xv, gv, dh):
    r = _rms_scale(xv)
    xh = xv * r
    dg = jnp.sum(dh * xh, axis=0, keepdims=True)
    dxh = dh * gv
    dx = r * (dxh - xh * jnp.mean(dxh * xh, axis=-1, keepdims=True))
    return dx, dg


def rms_fwd_epilogue(g):
    def fn(acc, gv):
        h = acc * _rms_scale(acc) * gv
        return acc, h, h.T
    return RowEpilogue(fn, [(g, "vec")], [(F32, "row"), (BF16, "row"), (BF16, "rowT")])


def rms_bwd_epilogue(x, g, dres, out_dtype=BF16):
    def fn(acc, xv, gv, *dr):
        dx, dg = _rms_bwd_math(xv, gv, acc)
        return (dx + dr[0].astype(F32) if dr else dx), dg
    ins = [(x, "row"), (g, "vec")] + ([(dres, "row")] if dres is not None else [])
    return RowEpilogue(fn, ins, [(out_dtype, "row"), (F32, "vec")])


def loss_epilogue(g, target):
    def fn(acc, gv, tv):
        e = acc * _rms_scale(acc) * gv - tv
        part = 0.5 * jnp.sum(jnp.mean(e * e, axis=-1, keepdims=True), axis=0, keepdims=True)
        dx, dg = _rms_bwd_math(acc, gv, e * (1.0 / acc.shape[-1]))
        return dx, dg, part
    return RowEpilogue(fn, [(g, "vec"), (target, "row")], [(BF16, "row"), (F32, "vec"), (F32, "lanes")])


def _sigmoid(v):
    return 0.5 * jnp.tanh(0.5 * v) + 0.5


def _glu(blk):
    u = blk[:, :CONV_CH].astype(F32)
    gt = blk[:, CONV_CH:].astype(F32)
    return u * _sigmoid(gt)


def _fill_causal_ext(ext, cur_ref, halo_ref, s, ts):
    ext[pl.ds(HALO, ts), :] = _glu(cur_ref[0])
    hal = _glu(halo_ref[0])
    ext[pl.ds(0, HALO), :] = jnp.where(s > 0, hal, 0.0)


SUBLANES = 8


def _make_shifted(ext, sh):
    n = ext.shape[0]
    full = ext[...]
    for r in range(1, SUBLANES):
        sh[r - 1] = pltpu.roll(full, n - r, 0)


def _tap(ext, sh, off, ts):
    r = off % SUBLANES
    return ext[pl.ds(off, ts), :] if r == 0 else sh[r - 1, pl.ds(off - r, ts), :]


def _causal_conv(ext, sh, w_ref, ts):
    acc = jnp.zeros((ts, CONV_CH), F32)
    for j in range(CONV_K):
        acc = acc + _tap(ext, sh, HALO - (CONV_K - 1) + j, ts) * w_ref[pl.ds(j, 1), :]
    return acc


def _ln_stats(y):
    mu = jnp.mean(y, axis=-1, keepdims=True)
    yc = y - mu
    rstd = lax.rsqrt(jnp.mean(yc * yc, axis=-1, keepdims=True) + EPS)
    return yc * rstd, rstd


def _conv_specs(ts, S):
    nh = ts // HALO
    cur = pl.BlockSpec((1, ts, 2 * CONV_CH), lambda b, s: (b, s, 0))
    halo = pl.BlockSpec((1, HALO, 2 * CONV_CH), lambda b, s: (b, jnp.maximum(s * nh - 1, 0), 0))
    w = pl.BlockSpec((HALO, CONV_CH), lambda b, s: (0, 0))
    vec = pl.BlockSpec((1, CONV_CH), lambda b, s: (0, 0))
    return cur, halo, w, vec


def conv_branch_fwd(ug, conv_w, conv_b, ln_g, ln_b, *, name, ts=256, rider=None):
    B, S, _ = ug.shape
    ts = min(ts, S)
    ns = S // ts
    cur, halo, w, vec = _conv_specs(ts, S)

    def body(cur_ref, halo_ref, w_ref, cb_ref, lg_ref, lb_ref, o_ref, ot_ref, y_ref, ext, sh):
        _fill_causal_ext(ext, cur_ref, halo_ref, pl.program_id(1), ts)
        _make_shifted(ext, sh)
        y = _causal_conv(ext, sh, w_ref, ts) + cb_ref[...]
        y_ref[0] = y
        yh, _ = _ln_stats(y)
        ln = yh * lg_ref[...] + lb_ref[...]
        out = ln * _sigmoid(ln)
        o_ref[0] = out.astype(BF16)
        ot_ref[...] = out.T.astype(BF16)

    return hosted_call(
        body, rider, name=name, grid=(B, ns), in_specs=[cur, halo, w, vec, vec, vec],
        out_specs=[pl.BlockSpec((1, ts, CONV_CH), lambda b, s: (b, s, 0)),
                   pl.BlockSpec((CONV_CH, ts), lambda b, s: (0, b * ns + s)),
                   pl.BlockSpec((1, ts, CONV_CH), lambda b, s: (b, s, 0))],
        out_shape=[jax.ShapeDtypeStruct((B, S, CONV_CH), BF16), jax.ShapeDtypeStruct((CONV_CH, B * S), BF16),
                   jax.ShapeDtypeStruct((B, S, CONV_CH), F32)],
        scratch_shapes=[pltpu.VMEM((ts + HALO, CONV_CH), F32),
                        pltpu.VMEM((SUBLANES - 1, ts + HALO, CONV_CH), F32)],
        args=(ug, ug, conv_w, conv_b, ln_g, ln_b),
    )


def conv_branch_bwd_a(ug, y, dcat, ln_g, ln_b, *, name, ts=256):
    B, S, _ = ug.shape
    ts = min(ts, S)
    cur, halo, _, vec = _conv_specs(ts, S)
    tile = pl.BlockSpec((1, ts, CONV_CH), lambda b, s: (b, s, 0))

    def body(cur_ref, halo_ref, y_ref, d_ref, lg_ref, lb_ref, dy_ref, dw_ref, dv_ref, ext, sh):
        _fill_causal_ext(ext, cur_ref, halo_ref, pl.program_id(1), ts)
        _make_shifted(ext, sh)
        yh, rstd = _ln_stats(y_ref[0])
        lg = lg_ref[...]
        ln = yh * lg + lb_ref[...]
        sg = _sigmoid(ln)
        dln = d_ref[0].astype(F32) * (sg * (1.0 + ln * (1.0 - sg)))
        dyh = dln * lg
        dy = rstd * (dyh - jnp.mean(dyh, axis=-1, keepdims=True)
                     - yh * jnp.mean(dyh * yh, axis=-1, keepdims=True))
        dy_ref[0] = dy

        @pl.when((pl.program_id(0) == 0) & (pl.program_id(1) == 0))
        def _():
            dw_ref[...] = jnp.zeros_like(dw_ref)
            dv_ref[...] = jnp.zeros_like(dv_ref)

        dv_ref[pl.ds(0, 1), :] += jnp.sum(dy, axis=0, keepdims=True)
        dv_ref[pl.ds(1, 1), :] += jnp.sum(dln * yh, axis=0, keepdims=True)
        dv_ref[pl.ds(2, 1), :] += jnp.sum(dln, axis=0, keepdims=True)
        for j in range(CONV_K):
            tap = _tap(ext, sh, HALO - (CONV_K - 1) + j, ts)
            dw_ref[pl.ds(j, 1), :] += jnp.sum(dy * tap, axis=0, keepdims=True)

    return _call(
        body, name=name, grid=(B, S // ts),
        in_specs=[cur, halo, tile, tile, vec, vec],
        out_specs=[tile,
                   pl.BlockSpec((HALO, CONV_CH), lambda b, s: (0, 0)),
                   pl.BlockSpec((8, CONV_CH), lambda b, s: (0, 0))],
        out_shape=[jax.ShapeDtypeStruct((B, S, CONV_CH), F32),
                   jax.ShapeDtypeStruct((HALO, CONV_CH), F32),
                   jax.ShapeDtypeStruct((8, CONV_CH), F32)],
        scratch_shapes=[pltpu.VMEM((ts + HALO, CONV_CH), F32),
                        pltpu.VMEM((SUBLANES - 1, ts + HALO, CONV_CH), F32)],
        compiler_params=_params(("arbitrary", "arbitrary")),
    )(ug, ug, y, dcat, ln_g, ln_b)


def conv_branch_bwd_b(ug, dy, conv_w, *, name, ts=256):
    B, S, _ = ug.shape
    ts = min(ts, S)
    nh, n_halo = ts // HALO, S // HALO

    def body(cur_ref, dy_ref, nxt_ref, w_ref, o_ref, ext, sh):
        last = pl.program_id(1) == pl.num_programs(1) - 1
        ext[pl.ds(0, ts), :] = dy_ref[0]
        ext[pl.ds(ts, HALO), :] = jnp.where(last, 0.0, nxt_ref[0])
        _make_shifted(ext, sh)
        da = jnp.zeros((ts, CONV_CH), F32)
        for j in range(CONV_K):
            da = da + _tap(ext, sh, CONV_K - 1 - j, ts) * w_ref[pl.ds(j, 1), :]
        blk = cur_ref[0]
        u = blk[:, :CONV_CH].astype(F32)
        sg = _sigmoid(blk[:, CONV_CH:].astype(F32))
        o_ref[0, :, :CONV_CH] = (da * sg).astype(BF16)
        o_ref[0, :, CONV_CH:] = (da * u * sg * (1.0 - sg)).astype(BF16)

    return _call(
        body, name=name, grid=(B, S // ts),
        in_specs=[pl.BlockSpec((1, ts, 2 * CONV_CH), lambda b, s: (b, s, 0)),
                  pl.BlockSpec((1, ts, CONV_CH), lambda b, s: (b, s, 0)),
                  pl.BlockSpec((1, HALO, CONV_CH), lambda b, s: (b, jnp.minimum((s + 1) * nh, n_halo - 1), 0)),
                  pl.BlockSpec((HALO, CONV_CH), lambda b, s: (0, 0))],
        out_specs=pl.BlockSpec((1, ts, 2 * CONV_CH), lambda b, s: (b, s, 0)),
        out_shape=jax.ShapeDtypeStruct((B, S, 2 * CONV_CH), BF16),
        scratch_shapes=[pltpu.VMEM((ts + HALO, CONV_CH), F32),
                        pltpu.VMEM((SUBLANES - 1, ts + HALO, CONV_CH), F32)],
        compiler_params=_params(("parallel", "parallel")),
    )(ug, dy, dy, conv_w)


def _tri(n, lower):
    r = lax.broadcasted_iota(jnp.int32, (n, n), 0)
    c = lax.broadcasted_iota(jnp.int32, (n, n), 1)
    return ((r >= c) if lower else (r <= c)).astype(F32)


def _eye(n):
    r = lax.broadcasted_iota(jnp.int32, (n, n), 0)
    c = lax.broadcasted_iota(jnp.int32, (n, n), 1)
    return (r == c).astype(F32)


def _dot_hi(a, b, dn):
    return lax.dot_general(a, b, dn, precision=lax.Precision.HIGHEST, preferred_element_type=F32)


NN = (((1,), (0,)), ((), ()))
NT = (((1,), (1,)), ((), ()))
TN = (((0,), (0,)), ((), ()))


def _log_sigmoid(v):
    e = jnp.exp(-jnp.abs(v))
    log1p_e = jnp.where(e < 1e-3, e * (1.0 - 0.5 * e), jnp.log(1.0 + e))
    return jnp.minimum(v, 0.0) - log1p_e


def fgate_fwd(h, w_f, b_f, *, name, ts=256, rider=None):
    B, S, D = h.shape
    ts = min(ts, S)

    def body(h_ref, w_ref, b_ref, f_ref, cc_ref, cr_ref, carry):
        @pl.when(pl.program_id(1) == 0)
        def _():
            carry[...] = jnp.zeros_like(carry)

        f = jnp.dot(h_ref[0], w_ref[...], preferred_element_type=F32)
        f_ref[0] = f
        logf = _log_sigmoid(f + b_ref[...])
        c = _dot_hi(_tri(ts, True), logf, NN) + carry[pl.ds(0, 1), :]
        cc_ref[0] = c
        carry[pl.ds(0, 1), :] = c[ts - 1:ts, :]
        cr_ref[0] = _dot_hi(_eye(LANES), c, NT)

    return hosted_call(
        body, rider, name=name, grid=(B, S // ts),
        in_specs=[pl.BlockSpec((1, ts, D), lambda b, s: (b, s, 0)),
                  pl.BlockSpec((D, LANES), lambda b, s: (0, 0)),
                  pl.BlockSpec((1, LANES), lambda b, s: (0, 0))],
        out_specs=[pl.BlockSpec((1, ts, LANES), lambda b, s: (b, s, 0)),
                   pl.BlockSpec((1, ts, LANES), lambda b, s: (b, s, 0)),
                   pl.BlockSpec((1, LANES, ts), lambda b, s: (b, 0, s))],
        out_shape=[jax.ShapeDtypeStruct((B, S, LANES), F32), jax.ShapeDtypeStruct((B, S, LANES), F32),
                   jax.ShapeDtypeStruct((B, LANES, S), F32)],
        scratch_shapes=[pltpu.VMEM((8, LANES), F32)],
        args=(h, w_f, b_f),
    )


def fgate_bwd(dc, f, b_f, *, name, ts=256):
    B, S, _ = f.shape
    P = dc.shape[1]
    ts = min(ts, S)
    ns = S // ts

    def body(dc_ref, f_ref, b_ref, df_ref, db_ref, carry):
        @pl.when(pl.program_id(1) == 0)
        def _():
            carry[...] = jnp.zeros_like(carry)

        @pl.when((pl.program_id(0) == 0) & (pl.program_id(1) == 0))
        def _():
            db_ref[...] = jnp.zeros_like(db_ref)

        dc_t = dc_ref[0, 0]
        for j in range(1, P):
            dc_t = dc_t + dc_ref[0, j]
        dlogf = _dot_hi(_tri(ts, False), dc_t, NN) + carry[pl.ds(0, 1), :]
        carry[pl.ds(0, 1), :] = dlogf[0:1, :]
        df = dlogf * _sigmoid(-(f_ref[0] + b_ref[...]))
        df_ref[0] = df.astype(BF16)
        db_ref[...] += jnp.sum(df, axis=0, keepdims=True)

    return _call(
        body, name=name, grid=(B, ns),
        in_specs=[pl.BlockSpec((1, P, ts, LANES), lambda b, s: (b, 0, ns - 1 - s, 0)),
                  pl.BlockSpec((1, ts, LANES), lambda b, s: (b, ns - 1 - s, 0)),
                  pl.BlockSpec((1, LANES), lambda b, s: (0, 0))],
        out_specs=[pl.BlockSpec((1, ts, LANES), lambda b, s: (b, ns - 1 - s, 0)),
                   pl.BlockSpec((1, LANES), lambda b, s: (0, 0))],
        out_shape=[jax.ShapeDtypeStruct((B, S, LANES), BF16), jax.ShapeDtypeStruct((1, LANES), F32)],
        scratch_shapes=[pltpu.VMEM((8, LANES), F32)],
        compiler_params=_params(("arbitrary", "arbitrary")),
    )(dc, f, b_f)


def _lane_pick(tile, idx):
    lane = lax.broadcasted_iota(jnp.int32, tile.shape, 1)
    return jnp.sum(jnp.where(lane == idx, tile, 0.0), axis=-1, keepdims=True)


FOX_T = 512


def _fox_heads(q, cc_ref, p):
    lane = lax.broadcasted_iota(jnp.int32, q.shape, 1)
    qs = q * (1.0 / math.sqrt(FOX_HEAD_DIM))
    qhs = [jnp.where((lane < FOX_HEAD_DIM) == (hh == 0), qs, jnp.zeros_like(qs)) for hh in range(2)]
    crefs = [_lane_pick(cc_ref[0, pl.ds(0, 1), :], 2 * p + hh) for hh in range(2)]
    return qhs, crefs


def _fold_lanes(x, op):
    out = x[:, :LANES]
    for j in range(1, x.shape[1] // LANES):
        out = op(out, x[:, j * LANES:(j + 1) * LANES])
    return out


def _causal(t, transposed):
    r = lax.broadcasted_iota(jnp.int32, (t, t), 0)
    c = lax.broadcasted_iota(jnp.int32, (t, t), 1)
    return (r <= c) if transposed else (c <= r)


QKV0 = 8


def fox_fwd(z, c_col, c_row, *, name, rider=None):
    B, S, _ = z.shape
    assert S % FOX_T == 0
    tq, nq = FOX_T, S // FOX_T
    npair = FOX_HEADS // 2

    def body(q_ref, k_ref, v_ref, cc_ref, cr_ref, o_ref, l_ref, ot_ref, s_scr, m_scr, acc_scr):
        p, qi = pl.program_id(1), pl.program_id(2)
        qhs, crefs = _fox_heads(q_ref[0], cc_ref, p)
        lane = lax.broadcasted_iota(jnp.int32, (tq, LANES), 1)
        first = lane < FOX_HEAD_DIM
        for hh in range(2):
            m_scr[hh] = jnp.full((tq, LANES), NEG, F32)
            acc_scr[hh] = jnp.zeros((tq, LANES), F32)

        def logits(kb, diagonal):
            k0 = pl.multiple_of(kb * tq, tq)
            k = k_ref[0, pl.ds(k0, tq), :]
            for hh in range(2):
                s = lax.dot_general(qhs[hh], k, NT, preferred_element_type=F32)
                s = s + (crefs[hh] - cr_ref[0, pl.ds(2 * p + hh, 1), pl.ds(k0, tq)])
                if diagonal:
                    s = jnp.where(_causal(tq, False), s, NEG)
                s_scr[hh, kb] = s
                m_scr[hh] = jnp.maximum(m_scr[hh], _fold_lanes(s, jnp.maximum))

        def sweep1(kb, carry):
            logits(kb, False)
            return carry

        lax.fori_loop(0, qi, sweep1, 0)
        logits(qi, True)
        ms = [jnp.max(m_scr[hh], axis=-1, keepdims=True) for hh in range(2)]
        mbs = [jnp.broadcast_to(ms[hh], (tq, tq)) for hh in range(2)]

        for hh in range(2):
            m_scr[hh] = jnp.zeros((tq, LANES), F32)

        def weigh(kb, carry):
            k0 = pl.multiple_of(kb * tq, tq)
            v = v_ref[0, pl.ds(k0, tq), :]
            for hh in range(2):
                pr = jnp.exp(s_scr[hh, kb] - mbs[hh])
                m_scr[hh] += _fold_lanes(pr, jnp.add)
                acc_scr[hh] += jnp.dot(pr.astype(BF16), v, preferred_element_type=F32)
            return carry

        lax.fori_loop(0, qi + 1, weigh, 0)
        accs = [acc_scr[hh] for hh in range(2)]
        ls = [jnp.sum(m_scr[hh], axis=-1, keepdims=True) for hh in range(2)]
        out = jnp.where(first, accs[0] / ls[0], accs[1] / ls[1])
        o_ref[0] = out.astype(BF16)
        ot_ref[...] = out.T.astype(BF16)
        l_ref[0, 0] = jnp.where(first, ms[0] + jnp.log(ls[0]), ms[1] + jnp.log(ls[1]))

    return hosted_call(
        body, rider, name=name, grid=(B, npair, nq),
        in_specs=[pl.BlockSpec((1, tq, LANES), lambda b, p, i: (b, i, QKV0 + p)),
                  pl.BlockSpec((1, S, LANES), lambda b, p, i: (b, 0, QKV0 + npair + p)),
                  pl.BlockSpec((1, S, LANES), lambda b, p, i: (b, 0, QKV0 + 2 * npair + p)),
                  pl.BlockSpec((1, tq, LANES), lambda b, p, i: (b, i, 0)),
                  pl.BlockSpec((1, 8, S), lambda b, p, i: (b, 0, 0))],
        out_specs=[pl.BlockSpec((1, tq, LANES), lambda b, p, i: (b, i, p)),
                   pl.BlockSpec((1, 1, tq, LANES), lambda b, p, i: (b, p, i, 0)),
                   pl.BlockSpec((LANES, tq), lambda b, p, i: (p, b * nq + i))],
        out_shape=[jax.ShapeDtypeStruct((B, S, FOX_W), BF16),
                   jax.ShapeDtypeStruct((B, npair, S, LANES), F32),
                   jax.ShapeDtypeStruct((FOX_W, B * S), BF16)],
        scratch_shapes=[pltpu.VMEM((2, nq, tq, tq), F32), pltpu.VMEM((2, tq, LANES), F32),
                        pltpu.VMEM((2, tq, LANES), F32)],
        args=(z, z, z, c_col, c_row),
    )


def fox_bwd_dq(z, dcat, lse, c_col, c_row, *, name, rider=None):
    B, S, _ = z.shape
    tq, nq = FOX_T, S // FOX_T
    npair = FOX_HEADS // 2

    def body(q_ref, k_ref, v_ref, do_ref, l_ref, cc_ref, cr_ref, dq_ref, st_ref, p_scr, dp_scr, dl_scr):
        p, qi = pl.program_id(1), pl.program_id(2)
        qhs, crefs = _fox_heads(q_ref[0], cc_ref, p)
        lane = lax.broadcasted_iota(jnp.int32, (tq, LANES), 1)
        do_b = do_ref[0].astype(BF16)
        dohs = [jnp.where((lane < FOX_HEAD_DIM) == (hh == 0), do_b, jnp.zeros_like(do_b)) for hh in range(2)]
        lses = [_lane_pick(l_ref[0, 0], hh * FOX_HEAD_DIM) for hh in range(2)]
        lbs = [jnp.broadcast_to(lses[hh], (tq, tq)) for hh in range(2)]
        for hh in range(2):
            dl_scr[hh] = jnp.zeros((tq, LANES), F32)

        def probs(kb, diagonal):
            k0 = pl.multiple_of(kb * tq, tq)
            k = k_ref[0, pl.ds(k0, tq), :]
            v = v_ref[0, pl.ds(k0, tq), :]
            for hh in range(2):
                s = lax.dot_general(qhs[hh], k, NT, preferred_element_type=F32)
                s = s + (crefs[hh] - cr_ref[0, pl.ds(2 * p + hh, 1), pl.ds(k0, tq)])
                pr = jnp.exp(s - lbs[hh])
                if diagonal:
                    pr = jnp.where(_causal(tq, False), pr, 0.0)
                dp = lax.dot_general(dohs[hh], v, NT, preferred_element_type=F32)
                pdp = pr * dp
                dl_scr[hh] += _fold_lanes(pdp, jnp.add)
                p_scr[hh, kb] = pr
                dp_scr[hh, kb] = dp

        def first_pass(kb, carry):
            probs(kb, False)
            return carry

        lax.fori_loop(0, qi, first_pass, 0)
        probs(qi, True)

        dls = [jnp.sum(dl_scr[hh], axis=-1, keepdims=True) for hh in range(2)]
        dlbs = [jnp.broadcast_to(dls[hh], (tq, tq)) for hh in range(2)]

        def second_pass(kb, dq):
            k0 = pl.multiple_of(kb * tq, tq)
            k = k_ref[0, pl.ds(k0, tq), :]
            for hh in range(2):
                ds = p_scr[hh, kb] * (dp_scr[hh, kb] - dlbs[hh])
                kh = jnp.where((lane < FOX_HEAD_DIM) == (hh == 0), k, jnp.zeros_like(k))
                dq = dq + jnp.dot(ds.astype(BF16), kh, preferred_element_type=F32)
            return dq

        dq = lax.fori_loop(0, qi + 1, second_pass, jnp.zeros((tq, LANES), F32))
        dq_ref[0] = (dq * (1.0 / math.sqrt(FOX_HEAD_DIM))).astype(BF16)
        cols = jnp.zeros((tq, LANES), F32)
        for j, col in enumerate([crefs[0] - lses[0], crefs[1] - lses[1], dls[0], dls[1]]):
            cols = jnp.where(lane == j, col, cols)
        st_ref[0, 0] = _dot_hi(_eye(LANES), cols, NT)[:8]

    return hosted_call(
        body, rider, name=name, grid=(B, npair, nq),
        in_specs=[pl.BlockSpec((1, tq, LANES), lambda b, p, i: (b, i, QKV0 + p)),
                  pl.BlockSpec((1, S, LANES), lambda b, p, i: (b, 0, QKV0 + npair + p)),
                  pl.BlockSpec((1, S, LANES), lambda b, p, i: (b, 0, QKV0 + 2 * npair + p)),
                  pl.BlockSpec((1, tq, LANES), lambda b, p, i: (b, i, npair + p)),
                  pl.BlockSpec((1, 1, tq, LANES), lambda b, p, i: (b, p, i, 0)),
                  pl.BlockSpec((1, tq, LANES), lambda b, p, i: (b, i, 0)),
                  pl.BlockSpec((1, 8, S), lambda b, p, i: (b, 0, 0))],
        out_specs=[pl.BlockSpec((1, tq, LANES), lambda b, p, i: (b, i, p)),
                   pl.BlockSpec((1, 1, 8, tq), lambda b, p, i: (b, p, 0, i))],
        out_shape=[jax.ShapeDtypeStruct((B, S, FOX_W), BF16), jax.ShapeDtypeStruct((B, npair, 8, S), F32)],
        scratch_shapes=[pltpu.VMEM((2, nq, tq, tq), F32), pltpu.VMEM((2, nq, tq, tq), F32),
                        pltpu.VMEM((2, tq, LANES), F32)],
        args=(z, z, z, dcat, lse, c_col, c_row), vmem=56 << 20,
    )


def fox_bwd_dkdv(z, dcat, stats, c_col, *, name, rider=None):
    B, S, _ = z.shape
    tk, nq = FOX_T, S // FOX_T
    npair = FOX_HEADS // 2
    inv = 1.0 / math.sqrt(FOX_HEAD_DIM)

    def body(q_ref, k_ref, v_ref, do_ref, st_ref, cc_ref, dk_ref, dv_ref, dc_ref, dk_scr, dv_scr, dc_scr):
        p, kt = pl.program_id(1), pl.program_id(2)
        lane = lax.broadcasted_iota(jnp.int32, (tk, LANES), 1)
        masks = [(lane < FOX_HEAD_DIM) == (hh == 0) for hh in range(2)]
        k = k_ref[0]
        v = v_ref[0]
        khs = [jnp.where(masks[hh], k, jnp.zeros_like(k)) for hh in range(2)]
        vhs = [jnp.where(masks[hh], v, jnp.zeros_like(v)) for hh in range(2)]
        ccbs = [jnp.broadcast_to(_lane_pick(cc_ref[0], 2 * p + hh), (tk, tk)) for hh in range(2)]
        dk_scr[...] = jnp.zeros_like(dk_scr)
        dv_scr[...] = jnp.zeros_like(dv_scr)
        dc_scr[...] = jnp.zeros_like(dc_scr)

        def tile(qb, diagonal):
            q0 = pl.multiple_of(qb * tk, tk)
            qs = q_ref[0, pl.ds(q0, tk), :] * inv
            do_b = do_ref[0, pl.ds(q0, tk), :].astype(BF16)
            for hh in range(2):
                st = lax.dot_general(khs[hh], qs, NT, preferred_element_type=F32)
                pr = jnp.exp(st - ccbs[hh] + st_ref[0, 0, pl.ds(hh, 1), pl.ds(q0, tk)])
                if diagonal:
                    pr = jnp.where(_causal(tk, True), pr, 0.0)
                dp = lax.dot_general(vhs[hh], do_b, NT, preferred_element_type=F32)
                ds = pr * (dp - st_ref[0, 0, pl.ds(2 + hh, 1), pl.ds(q0, tk)])
                dv_scr[...] += jnp.dot(pr.astype(BF16), jnp.where(masks[hh], do_b, jnp.zeros_like(do_b)),
                                       preferred_element_type=F32)
                dk_scr[...] += jnp.dot(ds.astype(BF16), jnp.where(masks[hh], qs, jnp.zeros_like(qs)),
                                       preferred_element_type=F32)
                dc_scr[hh] -= _fold_lanes(ds, jnp.add)

        def later(qb, carry):
            tile(qb, False)
            return carry

        tile(kt, True)
        lax.fori_loop(kt + 1, nq, later, 0)
        dk_ref[0] = dk_scr[...].astype(BF16)
        dv_ref[0] = dv_scr[...].astype(BF16)
        dcs = [jnp.sum(dc_scr[hh], axis=-1, keepdims=True) for hh in range(2)]
        dc_ref[0, 0] = jnp.where(lane == 2 * p, dcs[0], jnp.where(lane == 2 * p + 1, dcs[1], 0.0))

    full = lambda col: pl.BlockSpec((1, S, LANES), col)
    tile_spec = lambda col: pl.BlockSpec((1, tk, LANES), col)
    return hosted_call(
        body, rider, name=name, grid=(B, npair, nq),
        in_specs=[full(lambda b, p, t: (b, 0, QKV0 + p)),
                  tile_spec(lambda b, p, t: (b, t, QKV0 + npair + p)),
                  tile_spec(lambda b, p, t: (b, t, QKV0 + 2 * npair + p)),
                  full(lambda b, p, t: (b, 0, npair + p)),
                  pl.BlockSpec((1, 1, 8, S), lambda b, p, t: (b, p, 0, 0)),
                  tile_spec(lambda b, p, t: (b, t, 0))],
        out_specs=[tile_spec(lambda b, p, t: (b, t, p)), tile_spec(lambda b, p, t: (b, t, p)),
                   pl.BlockSpec((1, 1, tk, LANES), lambda b, p, t: (b, p, t, 0))],
        out_shape=[jax.ShapeDtypeStruct((B, S, FOX_W), BF16)] * 2
        + [jax.ShapeDtypeStruct((B, npair, S, LANES), F32)],
        scratch_shapes=[pltpu.VMEM((tk, LANES), F32), pltpu.VMEM((tk, LANES), F32),
                        pltpu.VMEM((2, tk, LANES), F32)],
        args=(z, z, z, dcat, stats, c_col),
    )


def xattn_fwd(qm, kv, *, name, tq=512):
    B, S, D = qm.shape
    M = kv.shape[1]
    tq = min(tq, S)
    inv = 1.0 / math.sqrt(MEM_HEAD_DIM)

    nq = S // tq

    def body(q_ref, kv_ref, o_ref, ot_ref):
        for h in range(MEM_HEADS):
            c0 = h * MEM_HEAD_DIM
            qh = q_ref[0, :, c0:c0 + MEM_HEAD_DIM]
            kh = kv_ref[0, :, c0:c0 + MEM_HEAD_DIM]
            vh = kv_ref[0, :, D + c0:D + c0 + MEM_HEAD_DIM]
            s = lax.dot_general(qh, kh, NT, preferred_element_type=F32) * inv
            e = jnp.exp(s - jnp.max(s, axis=-1, keepdims=True))
            o = jnp.dot(e.astype(BF16), vh, preferred_element_type=F32) / jnp.sum(e, axis=-1, keepdims=True)
            o_ref[0, :, c0:c0 + MEM_HEAD_DIM] = o.astype(BF16)
            ot_ref[c0:c0 + MEM_HEAD_DIM, :] = o.T.astype(BF16)

    return _call(
        body, name=name, grid=(B, nq),
        in_specs=[pl.BlockSpec((1, tq, D), lambda b, i: (b, i, 0)),
                  pl.BlockSpec((1, M, 2 * D), lambda b, i: (b, 0, 0))],
        out_specs=[pl.BlockSpec((1, tq, D), lambda b, i: (b, i, 0)),
                   pl.BlockSpec((D, tq), lambda b, i: (0, b * nq + i))],
        out_shape=[jax.ShapeDtypeStruct((B, S, D), BF16), jax.ShapeDtypeStruct((D, B * S), BF16)],
        compiler_params=_params(("parallel", "parallel")),
    )(qm, kv)


def xattn_bwd(qm, kv, do, *, name, tq=512):
    B, S, D = qm.shape
    M = kv.shape[1]
    tq = min(tq, S)
    inv = 1.0 / math.sqrt(MEM_HEAD_DIM)

    def body(q_ref, kv_ref, do_ref, dq_ref, dkv_ref):
        @pl.when(pl.program_id(1) == 0)
        def _():
            dkv_ref[...] = jnp.zeros_like(dkv_ref)

        for h in range(MEM_HEADS):
            c0 = h * MEM_HEAD_DIM
            qh = q_ref[0, :, c0:c0 + MEM_HEAD_DIM]
            kh = kv_ref[0, :, c0:c0 + MEM_HEAD_DIM]
            vh = kv_ref[0, :, D + c0:D + c0 + MEM_HEAD_DIM]
            doh = do_ref[0, :, c0:c0 + MEM_HEAD_DIM]
            s = lax.dot_general(qh, kh, NT, preferred_element_type=F32) * inv
            e = jnp.exp(s - jnp.max(s, axis=-1, keepdims=True))
            pr = e / jnp.sum(e, axis=-1, keepdims=True)
            dp = lax.dot_general(doh, vh, NT, preferred_element_type=F32)
            ds = pr * (dp - jnp.sum(pr * dp, axis=-1, keepdims=True))
            ds_b = ds.astype(BF16)
            dq_ref[0, :, c0:c0 + MEM_HEAD_DIM] = (jnp.dot(ds_b, kh, preferred_element_type=F32) * inv).astype(BF16)
            dkv_ref[0, :, c0:c0 + MEM_HEAD_DIM] += lax.dot_general(ds_b, qh, TN, preferred_element_type=F32) * inv
            dkv_ref[0, :, D + c0:D + c0 + MEM_HEAD_DIM] += lax.dot_general(
                pr.astype(BF16), doh, TN, preferred_element_type=F32)

    row = pl.BlockSpec((1, tq, D), lambda b, i: (b, i, 0))
    kvs = pl.BlockSpec((1, M, 2 * D), lambda b, i: (b, 0, 0))
    return _call(
        body, name=name, grid=(B, S // tq), in_specs=[row, kvs, row], out_specs=[row, kvs],
        out_shape=[jax.ShapeDtypeStruct((B, S, D), BF16), jax.ShapeDtypeStruct((B, M, 2 * D), F32)],
        compiler_params=_params(("parallel", "arbitrary")),
    )(qm, kv, do)


SWIGLU_TN = 1408


def _chunks(n, w=256):
    return [(c0, min(w, n - c0)) for c0 in range(0, n, w)]


def mm_swiglu_fwd(hf, w_gu, *, name, tm=512):
    T, D = hf.shape
    Fh = w_gu.shape[1] // 2
    tm, tn = min(tm, T), SWIGLU_TN
    nj = Fh // tn
    assert Fh % tn == 0 and T % tm == 0

    def body(a_ref, bg_ref, bu_ref, g_ref, u_ref, o_ref, ot_ref):
        a = a_ref[...]
        for c0, cw in _chunks(tn):
            cols = pl.ds(c0, cw)
            g = jnp.dot(a, bg_ref[:, cols], preferred_element_type=F32)
            u = jnp.dot(a, bu_ref[:, cols], preferred_element_type=F32)
            act = g * _sigmoid(g) * u
            g_ref[:, cols] = g.astype(BF16)
            u_ref[:, cols] = u.astype(BF16)
            o_ref[:, cols] = act.astype(BF16)
            ot_ref[cols, :] = act.T.astype(BF16)

    tile = pl.BlockSpec((tm, tn), lambda i, j: (i, j))
    return _call(
        body, name=name, grid=(T // tm, nj),
        in_specs=[pl.BlockSpec((tm, D), lambda i, j: (i, 0)), pl.BlockSpec((D, tn), lambda i, j: (0, j)),
                  pl.BlockSpec((D, tn), lambda i, j: (0, nj + j))],
        out_specs=[tile, tile, tile, pl.BlockSpec((tn, tm), lambda i, j: (j, i))],
        out_shape=[jax.ShapeDtypeStruct((T, Fh), BF16)] * 3 + [jax.ShapeDtypeStruct((Fh, T), BF16)],
        compiler_params=_params(("parallel", "parallel"), 48 << 20),
    )(hf, w_gu, w_gu)


def mm_swiglu_bwd(dx, w_down, g, u, *, name, tm=512):
    T, D = dx.shape
    Fh = w_down.shape[0]
    tm, tn = min(tm, T), SWIGLU_TN
    assert Fh % tn == 0 and T % tm == 0

    def body(a_ref, b_ref, g_ref, u_ref, dg_ref, du_ref):
        a = a_ref[...].astype(BF16)
        for c0, cw in _chunks(tn):
            cols = pl.ds(c0, cw)
            d = lax.dot_general(a, b_ref[cols, :], NT, preferred_element_type=F32)
            gv = g_ref[:, cols].astype(F32)
            uv = u_ref[:, cols].astype(F32)
            sg = _sigmoid(gv)
            dg_ref[:, cols] = (d * uv * (sg * (1.0 + gv * (1.0 - sg)))).astype(BF16)
            du_ref[:, cols] = (d * gv * sg).astype(BF16)

    tile = pl.BlockSpec((tm, tn), lambda i, j: (i, j))
    return _call(
        body, name=name, grid=(T // tm, Fh // tn),
        in_specs=[pl.BlockSpec((tm, D), lambda i, j: (i, 0)), pl.BlockSpec((tn, D), lambda i, j: (j, 0)), tile, tile],
        out_specs=[tile, tile],
        out_shape=[jax.ShapeDtypeStruct((T, Fh), BF16)] * 2,
        compiler_params=_params(("parallel", "parallel"), 48 << 20),
    )(dx, w_down, g, u)


LATE_MID = ("w_out", "w_mq", "w_mo")
LATE_KV = ("w_mkv",)
LATE_FFN = ("w_gu", "w_down")
LATE = LATE_MID + LATE_KV + LATE_FFN
RS_GROUPS = (("w_gu", "w_down"), ("w_out", "w_mq", "w_mkv", "w_mo"), ("w_in",))


def pair_sums(names, g42, got):
    return {n: pair_sum(g, o, name="rs_pair_sum_" + n) for n, g, o in zip(names, g42, got)}


def local_step(x, mem, target, sp, first_shards, late_shards):
    B, S, D = x.shape
    T = B * S
    M = mem.shape[1]
    row = lambda v: v.reshape(1, -1).astype(F32)
    g_mix, g_x, g_mem, g_ffn, g_final = (row(sp[k]) for k in ("g_mix", "g_x", "g_mem", "g_ffn", "g_final"))
    conv_b, ln_g, ln_b = row(sp["conv_b"]), row(sp["ln_g"]), row(sp["ln_b"])
    b_f = jnp.pad(row(sp["b_f"]), ((0, 0), (0, LANES - FOX_HEADS)))
    n_ug, n_main = 2 * CONV_CH, 2 * CONV_CH + 3 * FOX_W

    x2d = x.reshape(T, D)
    h, h_t, partly = rmsnorm_fwd(x2d, g_mix, name="rms_mix", rider=AllGatherStage1(first_shards))
    w_in8, cw8 = run_rider(AllGatherStage2(partly), name="ag_first_stage2")
    w_in_full = _full_from_gathered("w_in", w_in8)
    conv_w = cw8.transpose(1, 0, 2).reshape(HALO, -1)
    w_main, w_ug, w_qkv = w_in_full[:, :n_main], w_in_full[:, :n_ug], w_in_full[:, n_ug:n_main]
    w_f = jnp.pad(w_in_full[:, n_main:], ((0, 0), (0, LANES - FOX_HEADS)))
    z = matmul(h, w_main, out_dtype=BF16, tn=n_main, name="mm_in")
    z3 = z.reshape(B, S, n_main)
    n_mid, n_kv = len(LATE_MID), len(LATE_MID) + len(LATE_KV)
    (conv_out, conv_t, conv_y), partly_mid = conv_branch_fwd(z3, conv_w, conv_b, ln_g, ln_b, name="conv_fwd",
                                                     rider=AllGatherStage1(late_shards[:n_mid]))
    (f_raw, c_col, c_row), rode = fgate_fwd(
        h.reshape(B, S, D), w_f, b_f, name="fgate_fwd",
        rider=Riders(AllGatherStage1(late_shards[n_mid:n_kv]), AllGatherStage2(partly_mid)))
    partly_kv, full_mid = rode[:n_kv - n_mid], rode[n_kv - n_mid:]
    (att, lse, att_t), rode = fox_fwd(
        z3, c_col, c_row, name="fox_fwd",
        rider=Riders(AllGatherStage1(late_shards[n_kv:]), AllGatherStage2(partly_kv)))
    partly_ffn, full_kv = rode[:len(LATE_FFN)], rode[len(LATE_FFN):]
    wf = {n: _full_from_gathered(n, blk) for n, blk in zip(LATE_MID + LATE_KV, full_mid + full_kv)}
    (x1, hx, hx_t), full_ffn = matmul(
        [conv_out.reshape(T, CONV_CH), att.reshape(T, FOX_W)], [wf["w_out"], wf["w_out"]], b_blk=[0, 1],
        res=x2d, tn=D, name="mm_out", post=rms_fwd_epilogue(g_x), rider=AllGatherStage2(partly_ffn))
    wf.update({n: _full_from_gathered(n, blk) for n, blk in zip(LATE_FFN, full_ffn)})
    qm = matmul(hx, wf["w_mq"], out_dtype=BF16, tn=D, name="mm_mq")
    mem2d = mem.reshape(B * M, D)
    mem_n, mem_n_t = rmsnorm_fwd(mem2d, g_mem, name="rms_mem")
    kv = matmul(mem_n, wf["w_mkv"], out_dtype=BF16, tn=2 * D, name="mm_mkv").reshape(B, M, 2 * D)
    o, o_t = xattn_fwd(qm.reshape(B, S, D), kv, name="xattn_fwd")
    o = o.reshape(T, D)
    x2, hf, hf_t = matmul(o, wf["w_mo"], res=x1, tn=D, name="mm_mo", post=rms_fwd_epilogue(g_ffn))
    gate, up, act, act_t = mm_swiglu_fwd(hf, wf["w_gu"], name="mm_gu")
    dx3, dg_final, loss = matmul(act, wf["w_down"], res=x2, tn=D, name="mm_down",
                                 post=loss_epilogue(g_final, target.reshape(T, D)))
    gw = {}
    gw["w_down"] = matmul(act_t, dx3, out_dtype=BF16, tm=1408, tn=512, name="dw_down")
    dgate, dup = mm_swiglu_bwd(dx3, wf["w_down"], gate, up, name="dx_down")
    gw["w_gu"] = [matmul(hf_t, dgate, out_dtype=BF16, tn=SWIGLU_TN, name="dw_gate"),
                  matmul(hf_t, dup, out_dtype=BF16, tn=SWIGLU_TN, name="dw_up")]
    g42 = [_shards_from_full(n, gw[n]) for n in RS_GROUPS[0]]
    (dx2, dg_ffn), got = matmul([dgate, dup], [wf["w_gu"], wf["w_gu"]], b_blk=[0, 1], tb=True, tm=256, tn=D,
                                name="dx_gu", post=rms_bwd_epilogue(x2, g_ffn, dx3), rider=SiblingExchange(g42))
    parts = pair_sums(RS_GROUPS[0], g42, got)
    gw["w_mo"] = matmul(o_t, dx2, out_dtype=BF16, tn=D, name="dw_mo")
    do = matmul(dx2, wf["w_mo"], tb=True, out_dtype=BF16, tn=D, name="dx_mo")
    dqm, dkv = xattn_bwd(qm.reshape(B, S, D), kv, do.reshape(B, S, D), name="xattn_bwd")
    dqm = dqm.reshape(T, D)
    dkv = dkv.reshape(B * M, 2 * D)
    gw["w_mq"] = matmul(hx_t, dqm, out_dtype=BF16, tn=D, name="dw_mq")
    dx1, dg_x = matmul(dqm, wf["w_mq"], tb=True, tn=D, name="dx_mq", post=rms_bwd_epilogue(x1, g_x, dx2))
    gw["w_mkv"] = matmul(mem_n_t, dkv, out_dtype=BF16, tn=D, name="dw_mkv")
    _, dg_mem = matmul(dkv, wf["w_mkv"], tb=True, tn=D, name="dx_mkv", post=rms_bwd_epilogue(mem2d, g_mem, None))
    gw["w_out"] = jnp.concatenate([matmul(conv_t, dx1, out_dtype=BF16, tn=D, name="dw_out_conv"),
                                   matmul(att_t, dx1, out_dtype=BF16, tn=D, name="dw_out_att")], axis=0)
    g42 = [_shards_from_full(n, gw[n]) for n in RS_GROUPS[1]]
    dcat, got = matmul(dx1, wf["w_out"], tb=True, out_dtype=BF16, tn=D, name="dx_out", rider=SiblingExchange(g42))
    dcat = dcat.reshape(B, S, D)
    parts.update(pair_sums(RS_GROUPS[1], g42, got))
    dy, dconv_w, dvec = conv_branch_bwd_a(z3, conv_y, dcat, ln_g, ln_b, name="conv_bwd_a")
    dug = conv_branch_bwd_b(z3, dy, conv_w, name="conv_bwd_b")
    gots = {}
    (dq, stats), got = fox_bwd_dq(z3, dcat, lse, c_col, c_row, name="fox_bwd_dq",
                                  rider=ChipExchange([parts[n] for n in RS_GROUPS[0]]))
    gots.update(zip(RS_GROUPS[0], got))
    (dk, dv, dc), got = fox_bwd_dkdv(z3, dcat, stats, c_col, name="fox_bwd_dkdv",
                                     rider=ChipExchange([parts[n] for n in RS_GROUPS[1]]))
    gots.update(zip(RS_GROUPS[1], got))
    df, db_f = fgate_bwd(dc, f_raw, b_f, name="fgate_bwd")
    dug2 = dug.reshape(T, n_ug)
    dqkv = jnp.concatenate([dq, dk, dv], axis=-1).reshape(T, 3 * FOX_W)
    df2 = df.reshape(T, LANES)
    dw_in = [matmul(h_t, dug2, out_dtype=BF16, tn=n_ug, name="dw_in_ug"),
             matmul(h_t, dqkv, out_dtype=BF16, tn=3 * FOX_W, name="dw_in_qkv"),
             matmul(h_t, df2, out_dtype=BF16, name="dw_f")[:, :FOX_HEADS]]
    g42 = [_shards_from_full("w_in", dw_in)]
    parts.update(pair_sums(RS_GROUPS[2], g42, run_rider(SiblingExchange(g42), name="rs_sibling_in")))
    (dx, dg_mix), (gots["w_in"],) = matmul(
        [dug2, dqkv, df2], [w_ug, w_qkv, w_f], tb=True, tn=D, name="dx_in",
        post=rms_bwd_epilogue(x2d, g_mix, dx1, out_dtype=F32), rider=ChipExchange([parts["w_in"]]))
    gs = dict(g_mix=dg_mix, b_f=db_f[:, :FOX_HEADS], conv_w=dconv_w[:CONV_K], conv_b=dvec[0:1],
              ln_g=dvec[1:2], ln_b=dvec[2:3], g_x=dg_x, g_mem=dg_mem, g_ffn=dg_ffn, g_final=dg_final)
    return loss, dx.reshape(B, S, D), gs, {n: (parts[n], gots[n]) for n in BIG}


def _me():
    return lax.axis_index("x"), lax.axis_index("y"), lax.axis_index("c")


def _any_specs(n):
    return [pl.BlockSpec(memory_space=pl.ANY)] * n


def all_gather(xs, *, name):
    n = len(xs)

    def body(*refs):
        x_refs, out_refs = refs[:n], refs[n:2 * n]
        send_sems, recv_sems, local_sems = refs[2 * n:]
        x, y, c = _me()
        me, sibling = (x, y, c), (x, y, 1 - c)
        chips = [(1 - x, y), (x, 1 - y), (1 - x, 1 - y)]

        def slot(a, px, py, pc):
            return out_refs[a].at[4 * px + 2 * py + pc]

        def copy(a, k, block, to, own=False):
            return pltpu.make_async_remote_copy(
                src_ref=x_refs[a] if own else slot(a, *block), dst_ref=slot(a, *block),
                send_sem=send_sems.at[k, a], recv_sem=recv_sems.at[k, a], device_id=to, device_id_type=MESH)

        mine = [pltpu.make_async_copy(x_refs[a], slot(a, *me), local_sems.at[a]) for a in range(n)]
        first = [copy(a, 0, me, sibling, own=True) for a in range(n)]
        first += [copy(a, 1 + j, me, (*chip, c), own=True) for j, chip in enumerate(chips) for a in range(n)]
        for cp in mine + first:
            cp.start()
        passed = []
        for j, chip in enumerate(chips):
            for a in range(n):
                copy(a, 1 + j, (*chip, c), me).wait_recv()
                passed.append(copy(a, 4 + j, (*chip, c), sibling))
                passed[-1].start()
        for a in range(n):
            copy(a, 0, sibling, me).wait_recv()
            for j, chip in enumerate(chips):
                copy(a, 4 + j, (*chip, 1 - c), me).wait_recv()
        for cp in first + passed:
            cp.wait_send()
        for cp in mine:
            cp.wait()

    return _call(
        body, name=name, in_specs=_any_specs(n), out_specs=_any_specs(n),
        out_shape=[jax.ShapeDtypeStruct((N_DEV,) + v.shape, v.dtype) for v in xs],
        scratch_shapes=[pltpu.SemaphoreType.DMA((7, n)), pltpu.SemaphoreType.DMA((7, n)),
                        pltpu.SemaphoreType.DMA((n,))],
    )(*xs)


SIBLING_BARRIER = 1
CHIPS_BARRIER = 2
GATHER_BARRIER = 3


class SiblingExchange:
    collective_id = SIBLING_BARRIER

    def __init__(self, gs):
        n = len(gs)
        self.n, self.inputs = n, list(gs)
        self.out_shape = [jax.ShapeDtypeStruct((4,) + g.shape[2:], g.dtype) for g in gs]
        self.scratch = [pltpu.SemaphoreType.DMA((n,)), pltpu.SemaphoreType.DMA((n,))]

    @staticmethod
    def barrier_peers():
        x, y, c = _me()
        return [(x, y, 1 - c)]

    def _copies(self, g_refs, out_refs, sems):
        send_sems, recv_sems = sems
        x, y, c = _me()
        return [pltpu.make_async_remote_copy(
            src_ref=g_refs[a].at[:, 1 - c], dst_ref=out_refs[a], send_sem=send_sems.at[a],
            recv_sem=recv_sems.at[a], device_id=(x, y, 1 - c), device_id_type=MESH) for a in range(self.n)]

    def start(self, in_refs, out_refs, sems):
        for cp in self._copies(in_refs, out_refs, sems):
            cp.start()

    def finish(self, in_refs, out_refs, sems):
        for cp in self._copies(in_refs, out_refs, sems):
            cp.wait()


def run_rider(rider, *, name):
    return hosted_call(None, rider, name=name, grid=(), in_specs=[], out_specs=[], out_shape=[],
                       scratch_shapes=[], args=[])[1]


class ChipExchange:
    collective_id = CHIPS_BARRIER

    @staticmethod
    def barrier_peers():
        x, y, c = _me()
        return [(1 - x, y, c), (x, 1 - y, c), (1 - x, 1 - y, c)]

    def __init__(self, ps):
        n = len(ps)
        self.n, self.inputs = n, list(ps)
        self.out_shape = [jax.ShapeDtypeStruct(p.shape, p.dtype) for p in ps]
        self.scratch = [pltpu.SemaphoreType.DMA((3, n)), pltpu.SemaphoreType.DMA((3, n))]

    def _copies(self, p_refs, out_refs, sems, outgoing):
        send_sems, recv_sems = sems
        x, y, c = _me()
        my_chip = 2 * x + y
        cps = []
        for k in range(3):
            px, py = x ^ ((k + 1) >> 1), y ^ ((k + 1) & 1)
            src, dst = (2 * px + py, my_chip) if outgoing else (my_chip, 2 * px + py)
            for a in range(self.n):
                cps.append(pltpu.make_async_remote_copy(
                    src_ref=p_refs[a].at[src], dst_ref=out_refs[a].at[dst], send_sem=send_sems.at[k, a],
                    recv_sem=recv_sems.at[k, a], device_id=(px, py, c), device_id_type=MESH))
        return cps

    def start(self, in_refs, out_refs, sems):
        for cp in self._copies(in_refs, out_refs, sems, True):
            cp.start()

    def finish(self, in_refs, out_refs, sems):
        for cp in self._copies(in_refs, out_refs, sems, False):
            cp.wait_recv()
        for cp in self._copies(in_refs, out_refs, sems, True):
            cp.wait_send()


class AllGatherStage1:
    collective_id = GATHER_BARRIER

    @staticmethod
    def barrier_peers():
        x, y, c = _me()
        return [(x, y, 1 - c), (1 - x, y, c), (x, 1 - y, c), (1 - x, 1 - y, c)]

    def __init__(self, xs):
        n = len(xs)
        self.n, self.inputs = n, list(xs)
        self.out_shape = [jax.ShapeDtypeStruct((N_DEV,) + v.shape, v.dtype) for v in xs]
        self.scratch = [pltpu.SemaphoreType.DMA((4, n)), pltpu.SemaphoreType.DMA((4, n)),
                        pltpu.SemaphoreType.DMA((n,))]

    def _copies(self, x_refs, out_refs, sems, kind):
        send_sems, recv_sems, local_sems = sems
        x, y, c = _me()
        slot = lambda a, d: out_refs[a].at[4 * d[0] + 2 * d[1] + d[2]]
        if kind == "local":
            return [pltpu.make_async_copy(x_refs[a], slot(a, (x, y, c)), local_sems.at[a]) for a in range(self.n)]
        cps = []
        for k, peer in enumerate([(x, y, 1 - c), (1 - x, y, c), (x, 1 - y, c), (1 - x, 1 - y, c)]):
            for a in range(self.n):
                cps.append(pltpu.make_async_remote_copy(
                    src_ref=x_refs[a], dst_ref=slot(a, (x, y, c) if kind == "out" else peer),
                    send_sem=send_sems.at[k, a], recv_sem=recv_sems.at[k, a], device_id=peer, device_id_type=MESH))
        return cps

    def start(self, in_refs, out_refs, sems):
        for cp in self._copies(in_refs, out_refs, sems, "local") + self._copies(in_refs, out_refs, sems, "out"):
            cp.start()

    def finish(self, in_refs, out_refs, sems):
        for cp in self._copies(in_refs, out_refs, sems, "in"):
            cp.wait_recv()
        for cp in self._copies(in_refs, out_refs, sems, "out"):
            cp.wait_send()
        for cp in self._copies(in_refs, out_refs, sems, "local"):
            cp.wait()


class AllGatherStage2:
    collective_id = SIBLING_BARRIER

    @staticmethod
    def barrier_peers():
        x, y, c = _me()
        return [(x, y, 1 - c)]

    def __init__(self, outs):
        n = len(outs)
        self.n, self.inputs = n, list(outs)
        self.out_shape = [jax.ShapeDtypeStruct(o.shape, o.dtype) for o in outs]
        self.scratch = [pltpu.SemaphoreType.DMA((3, n)), pltpu.SemaphoreType.DMA((3, n))]
        self.aliases = {a: a for a in range(n)}

    def _copies(self, out_refs, sems, outgoing):
        send_sems, recv_sems = sems
        x, y, c = _me()
        cps = []
        for k, (px, py) in enumerate([(1 - x, y), (x, 1 - y), (1 - x, 1 - y)]):
            for a in range(self.n):
                cps.append(pltpu.make_async_remote_copy(
                    src_ref=out_refs[a].at[4 * px + 2 * py + c],
                    dst_ref=out_refs[a].at[4 * px + 2 * py + (c if outgoing else 1 - c)],
                    send_sem=send_sems.at[k, a], recv_sem=recv_sems.at[k, a], device_id=(x, y, 1 - c),
                    device_id_type=MESH))
        return cps

    def start(self, in_refs, out_refs, sems):
        for cp in self._copies(out_refs, sems, True):
            cp.start()

    def finish(self, in_refs, out_refs, sems):
        for cp in self._copies(out_refs, sems, False):
            cp.wait_recv()
        for cp in self._copies(out_refs, sems, True):
            cp.wait_send()


class Riders:
    def __init__(self, *riders):
        self.riders = riders
        self.collective_id = riders[0].collective_id
        self.barrier_peers = riders[0].barrier_peers
        self.inputs = [v for r in riders for v in r.inputs]
        self.out_shape = [s for r in riders for s in r.out_shape]
        self.scratch = [s for r in riders for s in r.scratch]
        self.aliases, i0, o0 = {}, 0, 0
        for r in riders:
            self.aliases.update({i0 + i: o0 + o for i, o in getattr(r, "aliases", {}).items()})
            i0, o0 = i0 + len(r.inputs), o0 + len(r.out_shape)

    def _split(self, in_refs, out_refs, sems):
        i0 = o0 = s0 = 0
        for r in self.riders:
            ni, no, ns = len(r.inputs), len(r.out_shape), len(r.scratch)
            yield r, in_refs[i0:i0 + ni], out_refs[o0:o0 + no], sems[s0:s0 + ns]
            i0, o0, s0 = i0 + ni, o0 + no, s0 + ns

    def start(self, in_refs, out_refs, sems):
        for r, i, o, s in self._split(in_refs, out_refs, sems):
            r.start(i, o, s)

    def finish(self, in_refs, out_refs, sems):
        for r, i, o, s in self._split(in_refs, out_refs, sems):
            r.finish(i, o, s)


def _peer_barrier(peers):
    barrier = pltpu.get_barrier_semaphore()
    for peer in peers:
        pl.semaphore_signal(barrier, inc=1, device_id=peer, device_id_type=MESH)
    pl.semaphore_wait(barrier, len(peers))


def hosted_call(body, rider, *, name, grid, in_specs, out_specs, out_shape, scratch_shapes, args, vmem=None):
    n_in, n_out, n_scr = len(in_specs), len(out_specs), len(scratch_shapes)
    r_in, r_out = (len(rider.inputs), len(rider.out_shape)) if rider is not None else (0, 0)
    own_barrier = getattr(rider, "collective_id", None) is not None

    def wrapped(*refs):
        ins, refs = refs[:n_in], refs[n_in:]
        rins, refs = refs[:r_in], refs[r_in:]
        outs, refs = refs[:n_out], refs[n_out:]
        routs, refs = refs[:r_out], refs[r_out:]
        scr, rscr = refs[:n_scr], refs[n_scr:]
        ids = [pl.program_id(d) for d in range(len(grid))]
        first = functools.reduce(jnp.logical_and, [i == 0 for i in ids], True)
        last = functools.reduce(jnp.logical_and, [i == g - 1 for i, g in zip(ids, grid)], True)

        def begin():
            if own_barrier:
                _peer_barrier(rider.barrier_peers())
            rider.start(rins, routs, rscr)

        if rider is not None and grid:
            pl.when(first)(begin)
        elif rider is not None:
            begin()
        if body is not None:
            body(*ins, *outs, *scr)
        if rider is not None and grid:
            pl.when(last)(lambda: rider.finish(rins, routs, rscr))
        elif rider is not None:
            rider.finish(rins, routs, rscr)

    kw = dict(grid=grid) if grid else {}
    aliases = getattr(rider, "aliases", {})
    if aliases:
        kw["input_output_aliases"] = {n_in + i: n_out + o for i, o in aliases.items()}
    if grid or vmem is not None or own_barrier:
        kw["compiler_params"] = _params(("arbitrary",) * len(grid) if grid else None, vmem,
                                        rider.collective_id if own_barrier else None)
    res = _call(
        wrapped, name=name, in_specs=list(in_specs) + _any_specs(r_in), out_specs=list(out_specs) + _any_specs(r_out),
        out_shape=list(out_shape) + (rider.out_shape if rider is not None else []),
        scratch_shapes=list(scratch_shapes) + (rider.scratch if rider is not None else []), **kw,
    )(*args, *(rider.inputs if rider is not None else []))
    return list(res[:n_out]), list(res[n_out:])


def _pick_rows(r, target=256):
    best = None
    for d in range(16, min(r, target) + 1, 16):
        if r % d == 0:
            best = d
    return r if best is None else best


def pair_sum(g, got, *, name):
    _, _, R, C = g.shape
    tr = _pick_rows(R)

    def body(g_ref, got_ref, o_ref):
        mine = jnp.where(lax.axis_index("c") == 0, g_ref[:, 0], g_ref[:, 1])
        o_ref[...] = (mine.astype(F32) + got_ref[...].astype(F32)).astype(o_ref.dtype)

    return _call(
        body, name=name, grid=(R // tr,),
        in_specs=[pl.BlockSpec((4, 2, tr, C), lambda i: (0, 0, i, 0)), pl.BlockSpec((4, tr, C), lambda i: (0, i, 0))],
        out_specs=pl.BlockSpec((4, tr, C), lambda i: (0, i, 0)),
        out_shape=jax.ShapeDtypeStruct((4, R, C), g.dtype),
        compiler_params=_params(("parallel",)),
    )(g, got)


def chip_sum_adamw(p, got, w, m, v, *, name):
    _, R, C = p.shape
    assert w.shape == (1, R, C), (name, w.shape, p.shape)
    tr = _pick_rows(R)

    def body(p_ref, got_ref, w_ref, m_ref, v_ref, g_ref, d_ref, mo_ref, vo_ref):
        my_chip = 2 * lax.axis_index("x") + lax.axis_index("y")
        g = jnp.zeros((tr, C), F32)
        for j in range(4):
            g = g + jnp.where(my_chip == j, p_ref[j], got_ref[j]).astype(F32)
        g_ref[0] = g
        d_ref[0], mo_ref[0], vo_ref[0] = _adamw_math(w_ref[0], g, m_ref[0], v_ref[0])

    part = pl.BlockSpec((4, tr, C), lambda i: (0, i, 0))
    spec = pl.BlockSpec((1, tr, C), lambda i: (0, i, 0))
    return _call(
        body, name=name, grid=(R // tr,), in_specs=[part, part, spec, spec, spec], out_specs=[spec] * 4,
        out_shape=[jax.ShapeDtypeStruct((1, R, C), F32)] * 4,
        compiler_params=_params(("parallel",)),
    )(p, got, w, m, v)


def rows_sum(g8, *, name):
    _, R, C = g8.shape

    def body(g_ref, o_ref):
        acc = g_ref[0]
        for j in range(1, N_DEV):
            acc = acc + g_ref[j]
        o_ref[...] = acc

    return _call(body, name=name, out_shape=jax.ShapeDtypeStruct((R, C), F32))(g8)


def _adamw_math(w, g, m, v):
    m = ADAM_B1 * m + (1.0 - ADAM_B1) * g
    v = ADAM_B2 * v + (1.0 - ADAM_B2) * (g * g)
    m_hat = m / (1.0 - ADAM_B1 ** ADAM_STEP)
    v_hat = v / (1.0 - ADAM_B2 ** ADAM_STEP)
    delta = -ADAM_LR * (m_hat / (jnp.sqrt(v_hat) + ADAM_EPS) + ADAM_WD * w)
    return delta, m, v


def to_bf16(xs, *, name):
    def body(*refs):
        for x_ref, o_ref in zip(refs[:len(xs)], refs[len(xs):]):
            o_ref[...] = x_ref[...].astype(BF16)

    total = sum(_nbytes(v.shape, F32) + _nbytes(v.shape, BF16) for v in xs)
    return _call(body, name=name, out_shape=[jax.ShapeDtypeStruct(v.shape, BF16) for v in xs],
                 compiler_params=_params(vmem=2 * total + (4 << 20)))(*xs)


def adamw_small(wgmv, *, name):
    n = len(wgmv)

    def body(*refs):
        ins, outs = refs[:4 * n], refs[4 * n:]
        for a in range(n):
            w_ref, g_ref, m_ref, v_ref = ins[4 * a:4 * a + 4]
            d, mn, vn = _adamw_math(w_ref[...], g_ref[...], m_ref[...], v_ref[...])
            outs[3 * a][...] = d
            outs[3 * a + 1][...] = mn
            outs[3 * a + 2][...] = vn

    flat = [t for tup in wgmv for t in tup]
    res = _call(
        body, name=name,
        out_shape=[jax.ShapeDtypeStruct(tup[0].shape, F32) for tup in wgmv for _ in range(3)],
    )(*flat)
    return [tuple(res[3 * a:3 * a + 3]) for a in range(n)]


BIG = ("w_in", "w_out", "w_mq", "w_mkv", "w_mo", "w_gu", "w_down")
COL_SHARDED = ("w_in", "w_mkv", "w_gu")
SMALL = ("g_mix", "b_f", "conv_w", "conv_b", "ln_g", "ln_b", "g_x", "g_mem", "g_ffn", "g_final")


def _full_from_gathered(n, blk):
    _, rr, cc = blk.shape
    if n in COL_SHARDED:
        return jnp.concatenate([blk[k] for k in range(N_DEV)], axis=1)
    return blk.reshape(N_DEV * rr, cc)


def _shards_from_full(n, g):
    pieces = g if isinstance(g, list) else [g]
    rr, cc = pieces[0].shape[0], sum(p.shape[1] for p in pieces)
    if n in COL_SHARDED:
        w = cc // N_DEV
        return jnp.stack([_columns(pieces, k * w, w) for k in range(N_DEV)]).reshape(4, 2, rr, w)
    return pieces[0].reshape(4, 2, rr // N_DEV, cc)


def _columns(pieces, start, width):
    out, c0 = [], 0
    for p in pieces:
        lo, hi = max(start, c0), min(start + width, c0 + p.shape[1])
        if lo < hi:
            out.append(p[:, lo - c0:hi - c0])
        c0 += p.shape[1]
    return out[0] if len(out) == 1 else jnp.concatenate(out, axis=1)


def _small_layout():
    sizes = dict(g_mix=1024, b_f=8, conv_w=CONV_K * CONV_CH, conv_b=512, ln_g=512, ln_b=512, g_x=1024,
                 g_mem=1024, g_ffn=1024, g_final=1024, loss=1)
    lay, r0 = {}, 0
    for n, sz in sizes.items():
        r = -(-sz // LANES)
        lay[n] = (r0, r, sz)
        r0 += r
    return lay, -(-r0 // 8) * 8


def kernel(x, mem, g_mix, w_in, b_f, conv_w, conv_b, ln_g, ln_b, w_out, g_x, g_mem, w_mq, w_mkv, w_mo, g_ffn, w_gu, w_down, g_final, loss_target, m_g_mix, m_w_in, m_b_f, m_conv_w, m_conv_b, m_ln_g, m_ln_b, m_w_out, m_g_x, m_g_mem, m_w_mq, m_w_mkv, m_w_mo, m_g_ffn, m_w_gu, m_w_down, m_g_final, v_g_mix, v_w_in, v_b_f, v_conv_w, v_conv_b, v_ln_g, v_ln_b, v_w_out, v_g_x, v_g_mem, v_w_mq, v_w_mkv, v_w_mo, v_g_ffn, v_w_gu, v_w_down, v_g_final):
    names = ["g_mix", "w_in", "b_f", "conv_w", "conv_b", "ln_g", "ln_b", "w_out", "g_x", "g_mem", "w_mq",
             "w_mkv", "w_mo", "g_ffn", "w_gu", "w_down", "g_final"]
    W = dict(zip(names, [g_mix, w_in, b_f, conv_w, conv_b, ln_g, ln_b, w_out, g_x, g_mem, w_mq, w_mkv, w_mo,
                         g_ffn, w_gu, w_down, g_final]))
    Mo = dict(zip(names, [m_g_mix, m_w_in, m_b_f, m_conv_w, m_conv_b, m_ln_g, m_ln_b, m_w_out, m_g_x, m_g_mem,
                          m_w_mq, m_w_mkv, m_w_mo, m_g_ffn, m_w_gu, m_w_down, m_g_final]))
    Vo = dict(zip(names, [v_g_mix, v_w_in, v_b_f, v_conv_w, v_conv_b, v_ln_g, v_ln_b, v_w_out, v_g_x, v_g_mem,
                          v_w_mq, v_w_mkv, v_w_mo, v_g_ffn, v_w_gu, v_w_down, v_g_final]))
    dev = 4 * lax.axis_index("x") + 2 * lax.axis_index("y") + lax.axis_index("c")

    two = lambda a: a.reshape(-1, a.shape[-1])
    cw_shard = jnp.pad(two(conv_w), ((0, HALO - CONV_K), (0, 0)))
    sp = dict(g_mix=g_mix, b_f=b_f, conv_b=conv_b, ln_g=ln_g, ln_b=ln_b, g_x=g_x, g_mem=g_mem,
              g_ffn=g_ffn, g_final=g_final)
    shards = to_bf16([two(W[n]) for n in ("w_in",) + LATE], name="cast_shards")
    loss_blk, grad_x, gs, reduced = local_step(x, mem, loss_target, sp, [shards[0], cw_shard], shards[1:])

    lay, rs = _small_layout()
    small = {**{n: gs[n] for n in SMALL}, "loss": loss_blk[:, :1]}
    parts = []
    for n, (r0, r, sz) in lay.items():
        flat = small[n].reshape(-1).astype(F32)
        parts.append(jnp.pad(flat, (0, r * LANES - sz)).reshape(r, LANES))
    spack = jnp.concatenate(parts, axis=0)
    spack = jnp.pad(spack, ((0, rs - spack.shape[0]), (0, 0)))
    ssum = rows_sum(all_gather([spack], name="ag_small")[0], name="small_sum")
    gsmall = {n: ssum[r0:r0 + r].reshape(-1)[:sz] for n, (r0, r, sz) in lay.items()}
    loss = gsmall["loss"].reshape(())

    grads, delta, new_m, new_v = {}, {}, {}, {}
    for n in BIG:
        p, o = reduced[n]
        grads[n], delta[n], new_m[n], new_v[n] = chip_sum_adamw(p, o, W[n], Mo[n], Vo[n], name="adamw_" + n)
    for n in SMALL:
        if n == "conv_w":
            full = gsmall[n].reshape(CONV_K, CONV_CH)
            ncol = conv_w.shape[-1]
            grads[n] = lax.dynamic_slice(full, (0, dev * ncol), (CONV_K, ncol)).reshape(conv_w.shape)
        else:
            grads[n] = gsmall[n].reshape(W[n].shape)
    upd = adamw_small([(two(W[n]), two(grads[n]), two(Mo[n]), two(Vo[n])) for n in SMALL], name="adamw_small")
    for n, (d, mn, vn) in zip(SMALL, upd):
        shp = W[n].shape
        delta[n], new_m[n], new_v[n] = d.reshape(shp), mn.reshape(shp), vn.reshape(shp)
    return (loss, grad_x, *[grads[n] for n in names], *[delta[n] for n in names],
            *[new_m[n] for n in names], *[new_v[n] for n in names])
```

```python
import functools
import math

import jax
import jax.numpy as jnp
from jax import lax
from jax.experimental import pallas as pl
from jax.experimental.pallas import tpu as pltpu

F32 = jnp.float32
BF16 = jnp.bfloat16
EPS = 1e-6
N_DEV = 8
CONV_CH = 512
CONV_K = 31
FOX_HEADS = 8
FOX_HEAD_DIM = 64
FOX_W = 512
MEM_HEADS = 4
MEM_HEAD_DIM = 256
HALO = 32
LANES = 128
ADAM_LR, ADAM_B1, ADAM_B2, ADAM_EPS, ADAM_WD, ADAM_STEP = 0.001, 0.9, 0.999, 1e-08, 0.01, 10
NEG = -1e30
VMEM_CAP = 60 * 1024 * 1024
MESH = pl.DeviceIdType.MESH


def _call(body, **kw):
    kw["out_shape"] = jax.tree.map(lambda s: pltpu.HBM(s.shape, s.dtype), kw["out_shape"])
    call = pl.pallas_call(body, **kw)
    return lambda *args: call(*[pltpu.with_memory_space_constraint(a, pltpu.HBM) for a in args])


def _params(sem=None, vmem=None, collective_id=None):
    kw = {} if collective_id is None else {"collective_id": collective_id}
    if sem is not None:
        kw["dimension_semantics"] = sem
    if vmem is not None:
        kw["vmem_limit_bytes"] = int(min(VMEM_CAP, vmem))
    return pltpu.CompilerParams(**kw)


def _nbytes(shape, dtype):
    return math.prod(shape) * jnp.dtype(dtype).itemsize


def _pick(n, target):
    best = None
    for d in range(LANES, min(n, target) + 1, LANES):
        if n % d == 0:
            best = d
    return n if best is None else best


class RowEpilogue:
    def __init__(self, fn, ins, outs):
        self.fn, self.ins, self.outs = fn, list(ins), list(outs)


def matmul(a, b, *, tb=False, out_dtype=None, res=None, tm=512, tn=512, name, rider=None, b_blk=None, post=None):
    a_list = list(a) if isinstance(a, (list, tuple)) else [a]
    b_list = list(b) if isinstance(b, (list, tuple)) else [b]
    n = len(a_list)
    assert len(b_list) == n
    M = a_list[0].shape[0]
    N = b_list[0].shape[0] if tb else b_list[0].shape[1]
    tm, tn = _pick(M, tm), _pick(N, tn)
    assert M % tm == 0 and N % tn == 0, (name, M, N, tm, tn)
    dn = (((1,), (1 if tb else 0,)), ((), ()))

    n_res = int(res is not None)
    n_pin = len(post.ins) if post is not None else 0

    def body(*refs):
        acc = None
        for a_ref, b_ref in zip(refs[:n], refs[n:2 * n]):
            p = lax.dot_general(a_ref[...].astype(BF16), b_ref[...].astype(BF16), dn, preferred_element_type=F32)
            acc = p if acc is None else acc + p
        if res is not None:
            acc = acc + refs[2 * n][...].astype(F32)
        if post is None:
            refs[-1][...] = acc.astype(out_dtype)
            return
        first_in = 2 * n + n_res
        vals = post.fn(acc, *[r[...] for r in refs[first_in:first_in + n_pin]])
        for (dtype, kind), o_ref, val in zip(post.outs, refs[first_in + n_pin:], vals):
            if kind in ("row", "rowT"):
                o_ref[...] = val.astype(dtype)
            else:
                @pl.when(pl.program_id(0) == 0)
                def _(o_ref=o_ref):
                    o_ref[...] = jnp.zeros_like(o_ref)

                o_ref[...] += jnp.broadcast_to(val, o_ref.shape).astype(dtype)

    o_spec = pl.BlockSpec((tm, tn), lambda i, j: (i, j))
    in_specs, est = [], 2 * _nbytes((tm, tn), out_dtype or F32) + 2 * _nbytes((tm, tn), F32)
    for av in a_list:
        assert av.shape[0] == M
        in_specs.append(pl.BlockSpec((tm, av.shape[1]), lambda i, j: (i, 0)))
        est += (2 * jnp.dtype(av.dtype).itemsize + (av.dtype != BF16) * 2) * tm * av.shape[1]
    for idx, (av, bv) in enumerate(zip(a_list, b_list)):
        K = av.shape[1]
        kb = 0 if b_blk is None else b_blk[idx]
        assert bv.shape[0 if tb else 1] == N and bv.shape[1 if tb else 0] >= (kb + 1) * K, (name, av.shape, bv.shape)
        assert b_blk is not None or bv.shape[1 if tb else 0] == K, (name, av.shape, bv.shape)
        in_specs.append(pl.BlockSpec((tn, K), lambda i, j, kb=kb: (j, kb)) if tb
                        else pl.BlockSpec((K, tn), lambda i, j, kb=kb: (kb, j)))
        est += (2 * jnp.dtype(bv.dtype).itemsize + (bv.dtype != BF16) * 2) * tn * K
    args = a_list + b_list
    if res is not None:
        in_specs.append(o_spec)
        args.append(res)
        est += 2 * _nbytes((tm, tn), res.dtype)
    if post is None:
        out_specs, out_shape = [o_spec], [jax.ShapeDtypeStruct((M, N), out_dtype)]
    else:
        assert tn == N, (name, tn, N)
        row = pl.BlockSpec((tm, N), lambda i, j: (i, 0))
        for arr, kind in post.ins:
            in_specs.append(row if kind == "row" else pl.BlockSpec((1, N), lambda i, j: (0, 0)))
            args.append(arr)
            est += 2 * _nbytes((tm, N), arr.dtype) * (kind == "row")
        specs = {"row": (row, (M, N)), "rowT": (pl.BlockSpec((N, tm), lambda i, j: (0, i)), (N, M)),
                 "vec": (pl.BlockSpec((1, N), lambda i, j: (0, 0)), (1, N)),
                 "lanes": (pl.BlockSpec((1, LANES), lambda i, j: (0, 0)), (1, LANES))}
        out_specs = [specs[kind][0] for _, kind in post.outs]
        out_shape = [jax.ShapeDtypeStruct(specs[kind][1], dtype) for dtype, kind in post.outs]
        est += sum(2 * _nbytes((tm, N), dtype) + _nbytes((tm, N), F32) for dtype, kind in post.outs if kind[:3] == "row")
    outs, rode = hosted_call(
        body, rider, name=name, grid=(M // tm, N // tn), in_specs=in_specs, out_specs=out_specs,
        out_shape=out_shape, scratch_shapes=[], args=args, vmem=est + (8 << 20),
    )
    result = outs[0] if post is None else outs
    return result if rider is None else (result, rode)


def _rms_scale(x):
    return lax.rsqrt(jnp.mean(x * x, axis=-1, keepdims=True) + EPS)


def rmsnorm_fwd(x, g, *, name, tm=512, rider=None):
    T, D = x.shape
    tm = min(tm, T)

    def body(x_ref, g_ref, o_ref, ot_ref):
        xv = x_ref[...]
        h = xv * _rms_scale(xv) * g_ref[...]
        o_ref[...] = h.astype(BF16)
        ot_ref[...] = h.T.astype(BF16)

    (h, h_t), rode = hosted_call(
        body, rider, name=name, grid=(T // tm,),
        in_specs=[pl.BlockSpec((tm, D), lambda i: (i, 0)), pl.BlockSpec((1, D), lambda i: (0, 0))],
        out_specs=[pl.BlockSpec((tm, D), lambda i: (i, 0)), pl.BlockSpec((D, tm), lambda i: (0, i))],
        out_shape=[jax.ShapeDtypeStruct((T, D), BF16), jax.ShapeDtypeStruct((D, T), BF16)],
        scratch_shapes=[], args=(x, g),
    )
    return (h, h_t) if rider is None else (h, h_t, rode)


def _rms_bwd_math(xv, gv, dh):
    r = _rms_scale(xv)
    xh = xv * r
    dg = jnp.sum(dh * xh, axis=0, keepdims=True)
    dxh = dh * gv
    dx = r * (dxh - xh * jnp.mean(dxh * xh, axis=-1, keepdims=True))
    return dx, dg


def rms_fwd_epilogue(g):
    def fn(acc, gv):
        h = acc * _rms_scale(acc) * gv
        return acc, h, h.T
    return RowEpilogue(fn, [(g, "vec")], [(F32, "row"), (BF16, "row"), (BF16, "rowT")])


def rms_bwd_epilogue(x, g, dres, out_dtype=BF16):
    def fn(acc, xv, gv, *dr):
        dx, dg = _rms_bwd_math(xv, gv, acc)
        return (dx + dr[0].astype(F32) if dr else dx), dg
    ins = [(x, "row"), (g, "vec")] + ([(dres, "row")] if dres is not None else [])
    return RowEpilogue(fn, ins, [(out_dtype, "row"), (F32, "vec")])


def loss_epilogue(g, target):
    def fn(acc, gv, tv):
        e = acc * _rms_scale(acc) * gv - tv
        part = 0.5 * jnp.sum(jnp.mean(e * e, axis=-1, keepdims=True), axis=0, keepdims=True)
        dx, dg = _rms_bwd_math(acc, gv, e * (1.0 / acc.shape[-1]))
        return dx, dg, part
    return RowEpilogue(fn, [(g, "vec"), (target, "row")], [(BF16, "row"), (F32, "vec"), (F32, "lanes")])


def _sigmoid(v):
    return 0.5 * jnp.tanh(0.5 * v) + 0.5


def _glu(blk):
    u = blk[:, :CONV_CH].astype(F32)
    gt = blk[:, CONV_CH:].astype(F32)
    return u * _sigmoid(gt)


def _fill_causal_ext(ext, cur_ref, halo_ref, s, ts):
    ext[pl.ds(HALO, ts), :] = _glu(cur_ref[0])
    hal = _glu(halo_ref[0])
    ext[pl.ds(0, HALO), :] = jnp.where(s > 0, hal, 0.0)


SUBLANES = 8


def _make_shifted(ext, sh):
    n = ext.shape[0]
    full = ext[...]
    for r in range(1, SUBLANES):
        sh[r - 1] = pltpu.roll(full, n - r, 0)


def _tap(ext, sh, off, ts):
    r = off % SUBLANES
    return ext[pl.ds(off, ts), :] if r == 0 else sh[r - 1, pl.ds(off - r, ts), :]


def _causal_conv(ext, sh, w_ref, ts):
    acc = jnp.zeros((ts, CONV_CH), F32)
    for j in range(CONV_K):
        acc = acc + _tap(ext, sh, HALO - (CONV_K - 1) + j, ts) * w_ref[pl.ds(j, 1), :]
    return acc


def _ln_stats(y):
    mu = jnp.mean(y, axis=-1, keepdims=True)
    yc = y - mu
    rstd = lax.rsqrt(jnp.mean(yc * yc, axis=-1, keepdims=True) + EPS)
    return yc * rstd, rstd


def _conv_specs(ts, S):
    nh = ts // HALO
    cur = pl.BlockSpec((1, ts, 2 * CONV_CH), lambda b, s: (b, s, 0))
    halo = pl.BlockSpec((1, HALO, 2 * CONV_CH), lambda b, s: (b, jnp.maximum(s * nh - 1, 0), 0))
    w = pl.BlockSpec((HALO, CONV_CH), lambda b, s: (0, 0))
    vec = pl.BlockSpec((1, CONV_CH), lambda b, s: (0, 0))
    return cur, halo, w, vec


def conv_branch_fwd(ug, conv_w, conv_b, ln_g, ln_b, *, name, ts=256, rider=None):
    B, S, _ = ug.shape
    ts = min(ts, S)
    ns = S // ts
    cur, halo, w, vec = _conv_specs(ts, S)

    def body(cur_ref, halo_ref, w_ref, cb_ref, lg_ref, lb_ref, o_ref, ot_ref, y_ref, ext, sh):
        _fill_causal_ext(ext, cur_ref, halo_ref, pl.program_id(1), ts)
        _make_shifted(ext, sh)
        y = _causal_conv(ext, sh, w_ref, ts) + cb_ref[...]
        y_ref[0] = y
        yh, _ = _ln_stats(y)
        ln = yh * lg_ref[...] + lb_ref[...]
        out = ln * _sigmoid(ln)
        o_ref[0] = out.astype(BF16)
        ot_ref[...] = out.T.astype(BF16)

    return hosted_call(
        body, rider, name=name, grid=(B, ns), in_specs=[cur, halo, w, vec, vec, vec],
        out_specs=[pl.BlockSpec((1, ts, CONV_CH), lambda b, s: (b, s, 0)),
                   pl.BlockSpec((CONV_CH, ts), lambda b, s: (0, b * ns + s)),
                   pl.BlockSpec((1, ts, CONV_CH), lambda b, s: (b, s, 0))],
        out_shape=[jax.ShapeDtypeStruct((B, S, CONV_CH), BF16), jax.ShapeDtypeStruct((CONV_CH, B * S), BF16),
                   jax.ShapeDtypeStruct((B, S, CONV_CH), F32)],
        scratch_shapes=[pltpu.VMEM((ts + HALO, CONV_CH), F32),
                        pltpu.VMEM((SUBLANES - 1, ts + HALO, CONV_CH), F32)],
        args=(ug, ug, conv_w, conv_b, ln_g, ln_b),
    )


def conv_branch_bwd_a(ug, y, dcat, ln_g, ln_b, *, name, ts=256):
    B, S, _ = ug.shape
    ts = min(ts, S)
    cur, halo, _, vec = _conv_specs(ts, S)
    tile = pl.BlockSpec((1, ts, CONV_CH), lambda b, s: (b, s, 0))

    def body(cur_ref, halo_ref, y_ref, d_ref, lg_ref, lb_ref, dy_ref, dw_ref, dv_ref, ext, sh):
        _fill_causal_ext(ext, cur_ref, halo_ref, pl.program_id(1), ts)
        _make_shifted(ext, sh)
        yh, rstd = _ln_stats(y_ref[0])
        lg = lg_ref[...]
        ln = yh * lg + lb_ref[...]
        sg = _sigmoid(ln)
        dln = d_ref[0].astype(F32) * (sg * (1.0 + ln * (1.0 - sg)))
        dyh = dln * lg
        dy = rstd * (dyh - jnp.mean(dyh, axis=-1, keepdims=True)
                     - yh * jnp.mean(dyh * yh, axis=-1, keepdims=True))
        dy_ref[0] = dy

        @pl.when((pl.program_id(0) == 0) & (pl.program_id(1) == 0))
        def _():
            dw_ref[...] = jnp.zeros_like(dw_ref)
            dv_ref[...] = jnp.zeros_like(dv_ref)

        dv_ref[pl.ds(0, 1), :] += jnp.sum(dy, axis=0, keepdims=True)
        dv_ref[pl.ds(1, 1), :] += jnp.sum(dln * yh, axis=0, keepdims=True)
        dv_ref[pl.ds(2, 1), :] += jnp.sum(dln, axis=0, keepdims=True)
        for j in range(CONV_K):
            tap = _tap(ext, sh, HALO - (CONV_K - 1) + j, ts)
            dw_ref[pl.ds(j, 1), :] += jnp.sum(dy * tap, axis=0, keepdims=True)

    return _call(
        body, name=name, grid=(B, S // ts),
        in_specs=[cur, halo, tile, tile, vec, vec],
        out_specs=[tile,
                   pl.BlockSpec((HALO, CONV_CH), lambda b, s: (0, 0)),
                   pl.BlockSpec((8, CONV_CH), lambda b, s: (0, 0))],
        out_shape=[jax.ShapeDtypeStruct((B, S, CONV_CH), F32),
                   jax.ShapeDtypeStruct((HALO, CONV_CH), F32),
                   jax.ShapeDtypeStruct((8, CONV_CH), F32)],
        scratch_shapes=[pltpu.VMEM((ts + HALO, CONV_CH), F32),
                        pltpu.VMEM((SUBLANES - 1, ts + HALO, CONV_CH), F32)],
        compiler_params=_params(("arbitrary", "arbitrary")),
    )(ug, ug, y, dcat, ln_g, ln_b)


def conv_branch_bwd_b(ug, dy, conv_w, *, name, ts=256):
    B, S, _ = ug.shape
    ts = min(ts, S)
    nh, n_halo = ts // HALO, S // HALO

    def body(cur_ref, dy_ref, nxt_ref, w_ref, o_ref, ext, sh):
        last = pl.program_id(1) == pl.num_programs(1) - 1
        ext[pl.ds(0, ts), :] = dy_ref[0]
        ext[pl.ds(ts, HALO), :] = jnp.where(last, 0.0, nxt_ref[0])
        _make_shifted(ext, sh)
        da = jnp.zeros((ts, CONV_CH), F32)
        for j in range(CONV_K):
            da = da + _tap(ext, sh, CONV_K - 1 - j, ts) * w_ref[pl.ds(j, 1), :]
        blk = cur_ref[0]
        u = blk[:, :CONV_CH].astype(F32)
        sg = _sigmoid(blk[:, CONV_CH:].astype(F32))
        o_ref[0, :, :CONV_CH] = (da * sg).astype(BF16)
        o_ref[0, :, CONV_CH:] = (da * u * sg * (1.0 - sg)).astype(BF16)

    return _call(
        body, name=name, grid=(B, S // ts),
        in_specs=[pl.BlockSpec((1, ts, 2 * CONV_CH), lambda b, s: (b, s, 0)),
                  pl.BlockSpec((1, ts, CONV_CH), lambda b, s: (b, s, 0)),
                  pl.BlockSpec((1, HALO, CONV_CH), lambda b, s: (b, jnp.minimum((s + 1) * nh, n_halo - 1), 0)),
                  pl.BlockSpec((HALO, CONV_CH), lambda b, s: (0, 0))],
        out_specs=pl.BlockSpec((1, ts, 2 * CONV_CH), lambda b, s: (b, s, 0)),
        out_shape=jax.ShapeDtypeStruct((B, S, 2 * CONV_CH), BF16),
        scratch_shapes=[pltpu.VMEM((ts + HALO, CONV_CH), F32),
                        pltpu.VMEM((SUBLANES - 1, ts + HALO, CONV_CH), F32)],
        compiler_params=_params(("parallel", "parallel")),
    )(ug, dy, dy, conv_w)


def _tri(n, lower):
    r = lax.broadcasted_iota(jnp.int32, (n, n), 0)
    c = lax.broadcasted_iota(jnp.int32, (n, n), 1)
    return ((r >= c) if lower else (r <= c)).astype(F32)


def _eye(n):
    r = lax.broadcasted_iota(jnp.int32, (n, n), 0)
    c = lax.broadcasted_iota(jnp.int32, (n, n), 1)
    return (r == c).astype(F32)


def _dot_hi(a, b, dn):
    return lax.dot_general(a, b, dn, precision=lax.Precision.HIGHEST, preferred_element_type=F32)


NN = (((1,), (0,)), ((), ()))
NT = (((1,), (1,)), ((), ()))
TN = (((0,), (0,)), ((), ()))


def _log_sigmoid(v):
    e = jnp.exp(-jnp.abs(v))
    log1p_e = jnp.where(e < 1e-3, e * (1.0 - 0.5 * e), jnp.log(1.0 + e))
    return jnp.minimum(v, 0.0) - log1p_e


def fgate_fwd(h, w_f, b_f, *, name, ts=256, rider=None):
    B, S, D = h.shape
    ts = min(ts, S)

    def body(h_ref, w_ref, b_ref, f_ref, cc_ref, cr_ref, carry):
        @pl.when(pl.program_id(1) == 0)
        def _():
            carry[...] = jnp.zeros_like(carry)

        f = jnp.dot(h_ref[0], w_ref[...], preferred_element_type=F32)
        f_ref[0] = f
        logf = _log_sigmoid(f + b_ref[...])
        c = _dot_hi(_tri(ts, True), logf, NN) + carry[pl.ds(0, 1), :]
        cc_ref[0] = c
        carry[pl.ds(0, 1), :] = c[ts - 1:ts, :]
        cr_ref[0] = _dot_hi(_eye(LANES), c, NT)

    return hosted_call(
        body, rider, name=name, grid=(B, S // ts),
        in_specs=[pl.BlockSpec((1, ts, D), lambda b, s: (b, s, 0)),
                  pl.BlockSpec((D, LANES), lambda b, s: (0, 0)),
                  pl.BlockSpec((1, LANES), lambda b, s: (0, 0))],
        out_specs=[pl.BlockSpec((1, ts, LANES), lambda b, s: (b, s, 0)),
                   pl.BlockSpec((1, ts, LANES), lambda b, s: (b, s, 0)),
                   pl.BlockSpec((1, LANES, ts), lambda b, s: (b, 0, s))],
        out_shape=[jax.ShapeDtypeStruct((B, S, LANES), F32), jax.ShapeDtypeStruct((B, S, LANES), F32),
                   jax.ShapeDtypeStruct((B, LANES, S), F32)],
        scratch_shapes=[pltpu.VMEM((8, LANES), F32)],
        args=(h, w_f, b_f),
    )


def fgate_bwd(dc, f, b_f, *, name, ts=256):
    B, S, _ = f.shape
    P = dc.shape[1]
    ts = min(ts, S)
    ns = S // ts

    def body(dc_ref, f_ref, b_ref, df_ref, db_ref, carry):
        @pl.when(pl.program_id(1) == 0)
        def _():
            carry[...] = jnp.zeros_like(carry)

        @pl.when((pl.program_id(0) == 0) & (pl.program_id(1) == 0))
        def _():
            db_ref[...] = jnp.zeros_like(db_ref)

        dc_t = dc_ref[0, 0]
        for j in range(1, P):
            dc_t = dc_t + dc_ref[0, j]
        dlogf = _dot_hi(_tri(ts, False), dc_t, NN) + carry[pl.ds(0, 1), :]
        carry[pl.ds(0, 1), :] = dlogf[0:1, :]
        df = dlogf * _sigmoid(-(f_ref[0] + b_ref[...]))
        df_ref[0] = df.astype(BF16)
        db_ref[...] += jnp.sum(df, axis=0, keepdims=True)

    return _call(
        body, name=name, grid=(B, ns),
        in_specs=[pl.BlockSpec((1, P, ts, LANES), lambda b, s: (b, 0, ns - 1 - s, 0)),
                  pl.BlockSpec((1, ts, LANES), lambda b, s: (b, ns - 1 - s, 0)),
                  pl.BlockSpec((1, LANES), lambda b, s: (0, 0))],
        out_specs=[pl.BlockSpec((1, ts, LANES), lambda b, s: (b, ns - 1 - s, 0)),
                   pl.BlockSpec((1, LANES), lambda b, s: (0, 0))],
        out_shape=[jax.ShapeDtypeStruct((B, S, LANES), BF16), jax.ShapeDtypeStruct((1, LANES), F32)],
        scratch_shapes=[pltpu.VMEM((8, LANES), F32)],
        compiler_params=_params(("arbitrary", "arbitrary")),
    )(dc, f, b_f)


def _lane_pick(tile, idx):
    lane = lax.broadcasted_iota(jnp.int32, tile.shape, 1)
    return jnp.sum(jnp.where(lane == idx, tile, 0.0), axis=-1, keepdims=True)


FOX_T = 512


def _fox_heads(q, cc_ref, p):
    lane = lax.broadcasted_iota(jnp.int32, q.shape, 1)
    qs = q * (1.0 / math.sqrt(FOX_HEAD_DIM))
    qhs = [jnp.where((lane < FOX_HEAD_DIM) == (hh == 0), qs, jnp.zeros_like(qs)) for hh in range(2)]
    crefs = [_lane_pick(cc_ref[0, pl.ds(0, 1), :], 2 * p + hh) for hh in range(2)]
    return qhs, crefs


def _fold_lanes(x, op):
    out = x[:, :LANES]
    for j in range(1, x.shape[1] // LANES):
        out = op(out, x[:, j * LANES:(j + 1) * LANES])
    return out


def _causal(t, transposed):
    r = lax.broadcasted_iota(jnp.int32, (t, t), 0)
    c = lax.broadcasted_iota(jnp.int32, (t, t), 1)
    return (r <= c) if transposed else (c <= r)


QKV0 = 8


def fox_fwd(z, c_col, c_row, *, name, rider=None):
    B, S, _ = z.shape
    assert S % FOX_T == 0
    tq, nq = FOX_T, S // FOX_T
    npair = FOX_HEADS // 2

    def body(q_ref, k_ref, v_ref, cc_ref, cr_ref, o_ref, l_ref, ot_ref, s_scr, m_scr, acc_scr):
        p, qi = pl.program_id(1), pl.program_id(2)
        qhs, crefs = _fox_heads(q_ref[0], cc_ref, p)
        lane = lax.broadcasted_iota(jnp.int32, (tq, LANES), 1)
        first = lane < FOX_HEAD_DIM
        for hh in range(2):
            m_scr[hh] = jnp.full((tq, LANES), NEG, F32)
            acc_scr[hh] = jnp.zeros((tq, LANES), F32)

        def logits(kb, diagonal):
            k0 = pl.multiple_of(kb * tq, tq)
            k = k_ref[0, pl.ds(k0, tq), :]
            for hh in range(2):
                s = lax.dot_general(qhs[hh], k, NT, preferred_element_type=F32)
                s = s + (crefs[hh] - cr_ref[0, pl.ds(2 * p + hh, 1), pl.ds(k0, tq)])
                if diagonal:
                    s = jnp.where(_causal(tq, False), s, NEG)
                s_scr[hh, kb] = s
                m_scr[hh] = jnp.maximum(m_scr[hh], _fold_lanes(s, jnp.maximum))

        def sweep1(kb, carry):
            logits(kb, False)
            return carry

        lax.fori_loop(0, qi, sweep1, 0)
        logits(qi, True)
        ms = [jnp.max(m_scr[hh], axis=-1, keepdims=True) for hh in range(2)]
        mbs = [jnp.broadcast_to(ms[hh], (tq, tq)) for hh in range(2)]

        for hh in range(2):
            m_scr[hh] = jnp.zeros((tq, LANES), F32)

        def weigh(kb, carry):
            k0 = pl.multiple_of(kb * tq, tq)
            v = v_ref[0, pl.ds(k0, tq), :]
            for hh in range(2):
                pr = jnp.exp(s_scr[hh, kb] - mbs[hh])
                m_scr[hh] += _fold_lanes(pr, jnp.add)
                acc_scr[hh] += jnp.dot(pr.astype(BF16), v, preferred_element_type=F32)
            return carry

        lax.fori_loop(0, qi + 1, weigh, 0)
        accs = [acc_scr[hh] for hh in range(2)]
        ls = [jnp.sum(m_scr[hh], axis=-1, keepdims=True) for hh in range(2)]
        out = jnp.where(first, accs[0] / ls[0], accs[1] / ls[1])
        o_ref[0] = out.astype(BF16)
        ot_ref[...] = out.T.astype(BF16)
        l_ref[0, 0] = jnp.where(first, ms[0] + jnp.log(ls[0]), ms[1] + jnp.log(ls[1]))

    return hosted_call(
        body, rider, name=name, grid=(B, npair, nq),
        in_specs=[pl.BlockSpec((1, tq, LANES), lambda b, p, i: (b, i, QKV0 + p)),
                  pl.BlockSpec((1, S, LANES), lambda b, p, i: (b, 0, QKV0 + npair + p)),
                  pl.BlockSpec((1, S, LANES), lambda b, p, i: (b, 0, QKV0 + 2 * npair + p)),
                  pl.BlockSpec((1, tq, LANES), lambda b, p, i: (b, i, 0)),
                  pl.BlockSpec((1, 8, S), lambda b, p, i: (b, 0, 0))],
        out_specs=[pl.BlockSpec((1, tq, LANES), lambda b, p, i: (b, i, p)),
                   pl.BlockSpec((1, 1, tq, LANES), lambda b, p, i: (b, p, i, 0)),
                   pl.BlockSpec((LANES, tq), lambda b, p, i: (p, b * nq + i))],
        out_shape=[jax.ShapeDtypeStruct((B, S, FOX_W), BF16),
                   jax.ShapeDtypeStruct((B, npair, S, LANES), F32),
                   jax.ShapeDtypeStruct((FOX_W, B * S), BF16)],
        scratch_shapes=[pltpu.VMEM((2, nq, tq, tq), F32), pltpu.VMEM((2, tq, LANES), F32),
                        pltpu.VMEM((2, tq, LANES), F32)],
        args=(z, z, z, c_col, c_row),
    )


def fox_bwd_dq(z, dcat, lse, c_col, c_row, *, name, rider=None):
    B, S, _ = z.shape
    tq, nq = FOX_T, S // FOX_T
    npair = FOX_HEADS // 2

    def body(q_ref, k_ref, v_ref, do_ref, l_ref, cc_ref, cr_ref, dq_ref, st_ref, p_scr, dp_scr, dl_scr):
        p, qi = pl.program_id(1), pl.program_id(2)
        qhs, crefs = _fox_heads(q_ref[0], cc_ref, p)
        lane = lax.broadcasted_iota(jnp.int32, (tq, LANES), 1)
        do_b = do_ref[0].astype(BF16)
        dohs = [jnp.where((lane < FOX_HEAD_DIM) == (hh == 0), do_b, jnp.zeros_like(do_b)) for hh in range(2)]
        lses = [_lane_pick(l_ref[0, 0], hh * FOX_HEAD_DIM) for hh in range(2)]
        lbs = [jnp.broadcast_to(lses[hh], (tq, tq)) for hh in range(2)]
        for hh in range(2):
            dl_scr[hh] = jnp.zeros((tq, LANES), F32)

        def probs(kb, diagonal):
            k0 = pl.multiple_of(kb * tq, tq)
            k = k_ref[0, pl.ds(k0, tq), :]
            v = v_ref[0, pl.ds(k0, tq), :]
            for hh in range(2):
                s = lax.dot_general(qhs[hh], k, NT, preferred_element_type=F32)
                s = s + (crefs[hh] - cr_ref[0, pl.ds(2 * p + hh, 1), pl.ds(k0, tq)])
                pr = jnp.exp(s - lbs[hh])
                if diagonal:
                    pr = jnp.where(_causal(tq, False), pr, 0.0)
                dp = lax.dot_general(dohs[hh], v, NT, preferred_element_type=F32)
                pdp = pr * dp
                dl_scr[hh] += _fold_lanes(pdp, jnp.add)
                p_scr[hh, kb] = pr
                dp_scr[hh, kb] = dp

        def first_pass(kb, carry):
            probs(kb, False)
            return carry

        lax.fori_loop(0, qi, first_pass, 0)
        probs(qi, True)

        dls = [jnp.sum(dl_scr[hh], axis=-1, keepdims=True) for hh in range(2)]
        dlbs = [jnp.broadcast_to(dls[hh], (tq, tq)) for hh in range(2)]

        def second_pass(kb, dq):
            k0 = pl.multiple_of(kb * tq, tq)
            k = k_ref[0, pl.ds(k0, tq), :]
            for hh in range(2):
                ds = p_scr[hh, kb] * (dp_scr[hh, kb] - dlbs[hh])
                kh = jnp.where((lane < FOX_HEAD_DIM) == (hh == 0), k, jnp.zeros_like(k))
                dq = dq + jnp.dot(ds.astype(BF16), kh, preferred_element_type=F32)
            return dq

        dq = lax.fori_loop(0, qi + 1, second_pass, jnp.zeros((tq, LANES), F32))
        dq_ref[0] = (dq * (1.0 / math.sqrt(FOX_HEAD_DIM))).astype(BF16)
        cols = jnp.zeros((tq, LANES), F32)
        for j, col in enumerate([crefs[0] - lses[0], crefs[1] - lses[1], dls[0], dls[1]]):
            cols = jnp.where(lane == j, col, cols)
        st_ref[0, 0] = _dot_hi(_eye(LANES), cols, NT)[:8]

    return hosted_call(
        body, rider, name=name, grid=(B, npair, nq),
        in_specs=[pl.BlockSpec((1, tq, LANES), lambda b, p, i: (b, i, QKV0 + p)),
                  pl.BlockSpec((1, S, LANES), lambda b, p, i: (b, 0, QKV0 + npair + p)),
                  pl.BlockSpec((1, S, LANES), lambda b, p, i: (b, 0, QKV0 + 2 * npair + p)),
                  pl.BlockSpec((1, tq, LANES), lambda b, p, i: (b, i, npair + p)),
                  pl.BlockSpec((1, 1, tq, LANES), lambda b, p, i: (b, p, i, 0)),
                  pl.BlockSpec((1, tq, LANES), lambda b, p, i: (b, i, 0)),
                  pl.BlockSpec((1, 8, S), lambda b, p, i: (b, 0, 0))],
        out_specs=[pl.BlockSpec((1, tq, LANES), lambda b, p, i: (b, i, p)),
                   pl.BlockSpec((1, 1, 8, tq), lambda b, p, i: (b, p, 0, i))],
        out_shape=[jax.ShapeDtypeStruct((B, S, FOX_W), BF16), jax.ShapeDtypeStruct((B, npair, 8, S), F32)],
        scratch_shapes=[pltpu.VMEM((2, nq, tq, tq), F32), pltpu.VMEM((2, nq, tq, tq), F32),
                        pltpu.VMEM((2, tq, LANES), F32)],
        args=(z, z, z, dcat, lse, c_col, c_row), vmem=56 << 20,
    )


def fox_bwd_dkdv(z, dcat, stats, c_col, *, name, rider=None):
    B, S, _ = z.shape
    tk, nq = FOX_T, S // FOX_T
    npair = FOX_HEADS // 2
    inv = 1.0 / math.sqrt(FOX_HEAD_DIM)

    def body(q_ref, k_ref, v_ref, do_ref, st_ref, cc_ref, dk_ref, dv_ref, dc_ref, dk_scr, dv_scr, dc_scr):
        p, kt = pl.program_id(1), pl.program_id(2)
        lane = lax.broadcasted_iota(jnp.int32, (tk, LANES), 1)
        masks = [(lane < FOX_HEAD_DIM) == (hh == 0) for hh in range(2)]
        k = k_ref[0]
        v = v_ref[0]
        khs = [jnp.where(masks[hh], k, jnp.zeros_like(k)) for hh in range(2)]
        vhs = [jnp.where(masks[hh], v, jnp.zeros_like(v)) for hh in range(2)]
        ccbs = [jnp.broadcast_to(_lane_pick(cc_ref[0], 2 * p + hh), (tk, tk)) for hh in range(2)]
        dk_scr[...] = jnp.zeros_like(dk_scr)
        dv_scr[...] = jnp.zeros_like(dv_scr)
        dc_scr[...] = jnp.zeros_like(dc_scr)

        def tile(qb, diagonal):
            q0 = pl.multiple_of(qb * tk, tk)
            qs = q_ref[0, pl.ds(q0, tk), :] * inv
            do_b = do_ref[0, pl.ds(q0, tk), :].astype(BF16)
            for hh in range(2):
                st = lax.dot_general(khs[hh], qs, NT, preferred_element_type=F32)
                pr = jnp.exp(st - ccbs[hh] + st_ref[0, 0, pl.ds(hh, 1), pl.ds(q0, tk)])
                if diagonal:
                    pr = jnp.where(_causal(tk, True), pr, 0.0)
                dp = lax.dot_general(vhs[hh], do_b, NT, preferred_element_type=F32)
                ds = pr * (dp - st_ref[0, 0, pl.ds(2 + hh, 1), pl.ds(q0, tk)])
                dv_scr[...] += jnp.dot(pr.astype(BF16), jnp.where(masks[hh], do_b, jnp.zeros_like(do_b)),
                                       preferred_element_type=F32)
                dk_scr[...] += jnp.dot(ds.astype(BF16), jnp.where(masks[hh], qs, jnp.zeros_like(qs)),
                                       preferred_element_type=F32)
                dc_scr[hh] -= _fold_lanes(ds, jnp.add)

        def later(qb, carry):
            tile(qb, False)
            return carry

        tile(kt, True)
        lax.fori_loop(kt + 1, nq, later, 0)
        dk_ref[0] = dk_scr[...].astype(BF16)
        dv_ref[0] = dv_scr[...].astype(BF16)
        dcs = [jnp.sum(dc_scr[hh], axis=-1, keepdims=True) for hh in range(2)]
        dc_ref[0, 0] = jnp.where(lane == 2 * p, dcs[0], jnp.where(lane == 2 * p + 1, dcs[1], 0.0))

    full = lambda col: pl.BlockSpec((1, S, LANES), col)
    tile_spec = lambda col: pl.BlockSpec((1, tk, LANES), col)
    return hosted_call(
        body, rider, name=name, grid=(B, npair, nq),
        in_specs=[full(lambda b, p, t: (b, 0, QKV0 + p)),
                  tile_spec(lambda b, p, t: (b, t, QKV0 + npair + p)),
                  tile_spec(lambda b, p, t: (b, t, QKV0 + 2 * npair + p)),
                  full(lambda b, p, t: (b, 0, npair + p)),
                  pl.BlockSpec((1, 1, 8, S), lambda b, p, t: (b, p, 0, 0)),
                  tile_spec(lambda b, p, t: (b, t, 0))],
        out_specs=[tile_spec(lambda b, p, t: (b, t, p)), tile_spec(lambda b, p, t: (b, t, p)),
                   pl.BlockSpec((1, 1, tk, LANES), lambda b, p, t: (b, p, t, 0))],
        out_shape=[jax.ShapeDtypeStruct((B, S, FOX_W), BF16)] * 2
        + [jax.ShapeDtypeStruct((B, npair, S, LANES), F32)],
        scratch_shapes=[pltpu.VMEM((tk, LANES), F32), pltpu.VMEM((tk, LANES), F32),
                        pltpu.VMEM((2, tk, LANES), F32)],
        args=(z, z, z, dcat, stats, c_col),
    )


def xattn_fwd(qm, kv, *, name, tq=512):
    B, S, D = qm.shape
    M = kv.shape[1]
    tq = min(tq, S)
    inv = 1.0 / math.sqrt(MEM_HEAD_DIM)

    nq = S // tq

    def body(q_ref, kv_ref, o_ref, ot_ref):
        for h in range(MEM_HEADS):
            c0 = h * MEM_HEAD_DIM
            qh = q_ref[0, :, c0:c0 + MEM_HEAD_DIM]
            kh = kv_ref[0, :, c0:c0 + MEM_HEAD_DIM]
            vh = kv_ref[0, :, D + c0:D + c0 + MEM_HEAD_DIM]
            s = lax.dot_general(qh, kh, NT, preferred_element_type=F32) * inv
            e = jnp.exp(s - jnp.max(s, axis=-1, keepdims=True))
            o = jnp.dot(e.astype(BF16), vh, preferred_element_type=F32) / jnp.sum(e, axis=-1, keepdims=True)
            o_ref[0, :, c0:c0 + MEM_HEAD_DIM] = o.astype(BF16)
            ot_ref[c0:c0 + MEM_HEAD_DIM, :] = o.T.astype(BF16)

    return _call(
        body, name=name, grid=(B, nq),
        in_specs=[pl.BlockSpec((1, tq, D), lambda b, i: (b, i, 0)),
                  pl.BlockSpec((1, M, 2 * D), lambda b, i: (b, 0, 0))],
        out_specs=[pl.BlockSpec((1, tq, D), lambda b, i: (b, i, 0)),
                   pl.BlockSpec((D, tq), lambda b, i: (0, b * nq + i))],
        out_shape=[jax.ShapeDtypeStruct((B, S, D), BF16), jax.ShapeDtypeStruct((D, B * S), BF16)],
        compiler_params=_params(("parallel", "parallel")),
    )(qm, kv)


def xattn_bwd(qm, kv, do, *, name, tq=512):
    B, S, D = qm.shape
    M = kv.shape[1]
    tq = min(tq, S)
    inv = 1.0 / math.sqrt(MEM_HEAD_DIM)

    def body(q_ref, kv_ref, do_ref, dq_ref, dkv_ref):
        @pl.when(pl.program_id(1) == 0)
        def _():
            dkv_ref[...] = jnp.zeros_like(dkv_ref)

        for h in range(MEM_HEADS):
            c0 = h * MEM_HEAD_DIM
            qh = q_ref[0, :, c0:c0 + MEM_HEAD_DIM]
            kh = kv_ref[0, :, c0:c0 + MEM_HEAD_DIM]
            vh = kv_ref[0, :, D + c0:D + c0 + MEM_HEAD_DIM]
            doh = do_ref[0, :, c0:c0 + MEM_HEAD_DIM]
            s = lax.dot_general(qh, kh, NT, preferred_element_type=F32) * inv
            e = jnp.exp(s - jnp.max(s, axis=-1, keepdims=True))
            pr = e / jnp.sum(e, axis=-1, keepdims=True)
            dp = lax.dot_general(doh, vh, NT, preferred_element_type=F32)
            ds = pr * (dp - jnp.sum(pr * dp, axis=-1, keepdims=True))
            ds_b = ds.astype(BF16)
            dq_ref[0, :, c0:c0 + MEM_HEAD_DIM] = (jnp.dot(ds_b, kh, preferred_element_type=F32) * inv).astype(BF16)
            dkv_ref[0, :, c0:c0 + MEM_HEAD_DIM] += lax.dot_general(ds_b, qh, TN, preferred_element_type=F32) * inv
            dkv_ref[0, :, D + c0:D + c0 + MEM_HEAD_DIM] += lax.dot_general(
                pr.astype(BF16), doh, TN, preferred_element_type=F32)

    row = pl.BlockSpec((1, tq, D), lambda b, i: (b, i, 0))
    kvs = pl.BlockSpec((1, M, 2 * D), lambda b, i: (b, 0, 0))
    return _call(
        body, name=name, grid=(B, S // tq), in_specs=[row, kvs, row], out_specs=[row, kvs],
        out_shape=[jax.ShapeDtypeStruct((B, S, D), BF16), jax.ShapeDtypeStruct((B, M, 2 * D), F32)],
        compiler_params=_params(("parallel", "arbitrary")),
    )(qm, kv, do)


SWIGLU_TN = 2816


def _chunks(n, w=256):
    return [(c0, min(w, n - c0)) for c0 in range(0, n, w)]


def mm_swiglu_fwd(hf, w_gu, *, name, tm=256):
    T, D = hf.shape
    Fh = w_gu.shape[1] // 2
    tm, tn = min(tm, T), SWIGLU_TN
    nj = Fh // tn
    assert Fh % tn == 0 and T % tm == 0

    def body(a_ref, bg_ref, bu_ref, g_ref, u_ref, o_ref, ot_ref):
        a = a_ref[...]
        for c0, cw in _chunks(tn):
            cols = pl.ds(c0, cw)
            g = jnp.dot(a, bg_ref[:, cols], preferred_element_type=F32)
            u = jnp.dot(a, bu_ref[:, cols], preferred_element_type=F32)
            act = g * _sigmoid(g) * u
            g_ref[:, cols] = g.astype(BF16)
            u_ref[:, cols] = u.astype(BF16)
            o_ref[:, cols] = act.astype(BF16)
            ot_ref[cols, :] = act.T.astype(BF16)

    tile = pl.BlockSpec((tm, tn), lambda i, j: (i, j))
    return _call(
        body, name=name, grid=(T // tm, nj),
        in_specs=[pl.BlockSpec((tm, D), lambda i, j: (i, 0)), pl.BlockSpec((D, tn), lambda i, j: (0, j)),
                  pl.BlockSpec((D, tn), lambda i, j: (0, nj + j))],
        out_specs=[tile, tile, tile, pl.BlockSpec((tn, tm), lambda i, j: (j, i))],
        out_shape=[jax.ShapeDtypeStruct((T, Fh), BF16)] * 3 + [jax.ShapeDtypeStruct((Fh, T), BF16)],
        compiler_params=_params(("parallel", "parallel"), 48 << 20),
    )(hf, w_gu, w_gu)


def mm_swiglu_bwd(dx, w_down, g, u, *, name, tm=256):
    T, D = dx.shape
    Fh = w_down.shape[0]
    tm, tn = min(tm, T), SWIGLU_TN
    assert Fh % tn == 0 and T % tm == 0

    def body(a_ref, b_ref, g_ref, u_ref, dg_ref, du_ref):
        a = a_ref[...].astype(BF16)
        for c0, cw in _chunks(tn):
            cols = pl.ds(c0, cw)
            d = lax.dot_general(a, b_ref[cols, :], NT, preferred_element_type=F32)
            gv = g_ref[:, cols].astype(F32)
            uv = u_ref[:, cols].astype(F32)
            sg = _sigmoid(gv)
            dg_ref[:, cols] = (d * uv * (sg * (1.0 + gv * (1.0 - sg)))).astype(BF16)
            du_ref[:, cols] = (d * gv * sg).astype(BF16)

    tile = pl.BlockSpec((tm, tn), lambda i, j: (i, j))
    return _call(
        body, name=name, grid=(T // tm, Fh // tn),
        in_specs=[pl.BlockSpec((tm, D), lambda i, j: (i, 0)), pl.BlockSpec((tn, D), lambda i, j: (j, 0)), tile, tile],
        out_specs=[tile, tile],
        out_shape=[jax.ShapeDtypeStruct((T, Fh), BF16)] * 2,
        compiler_params=_params(("parallel", "parallel"), 48 << 20),
    )(dx, w_down, g, u)


LATE_MID = ("w_out", "w_mq", "w_mo")
LATE_KV = ("w_mkv",)
LATE_FFN = ("w_gu", "w_down")
LATE = LATE_MID + LATE_KV + LATE_FFN
RS_GROUPS = (("w_gu", "w_down"), ("w_out", "w_mq", "w_mkv", "w_mo"), ("w_in",))


def pair_sums(names, g42, got):
    return {n: pair_sum(g, o, name="rs_pair_sum_" + n) for n, g, o in zip(names, g42, got)}


def local_step(x, mem, target, sp, first_shards, late_shards):
    B, S, D = x.shape
    T = B * S
    M = mem.shape[1]
    row = lambda v: v.reshape(1, -1).astype(F32)
    g_mix, g_x, g_mem, g_ffn, g_final = (row(sp[k]) for k in ("g_mix", "g_x", "g_mem", "g_ffn", "g_final"))
    conv_b, ln_g, ln_b = row(sp["conv_b"]), row(sp["ln_g"]), row(sp["ln_b"])
    b_f = jnp.pad(row(sp["b_f"]), ((0, 0), (0, LANES - FOX_HEADS)))
    n_ug, n_main = 2 * CONV_CH, 2 * CONV_CH + 3 * FOX_W

    x2d = x.reshape(T, D)
    h, h_t, partly = rmsnorm_fwd(x2d, g_mix, name="rms_mix", rider=AllGatherStage1(first_shards))
    w_in8, cw8 = run_rider(AllGatherStage2(partly), name="ag_first_stage2")
    w_in_full = _full_from_gathered("w_in", w_in8)
    conv_w = cw8.transpose(1, 0, 2).reshape(HALO, -1)
    w_main, w_ug, w_qkv = w_in_full[:, :n_main], w_in_full[:, :n_ug], w_in_full[:, n_ug:n_main]
    w_f = jnp.pad(w_in_full[:, n_main:], ((0, 0), (0, LANES - FOX_HEADS)))
    z = matmul(h, w_main, out_dtype=BF16, tn=n_main, name="mm_in")
    z3 = z.reshape(B, S, n_main)
    n_mid, n_kv = len(LATE_MID), len(LATE_MID) + len(LATE_KV)
    (conv_out, conv_t, conv_y), partly_mid = conv_branch_fwd(z3, conv_w, conv_b, ln_g, ln_b, name="conv_fwd",
                                                     rider=AllGatherStage1(late_shards[:n_mid]))
    (f_raw, c_col, c_row), rode = fgate_fwd(
        h.reshape(B, S, D), w_f, b_f, name="fgate_fwd",
        rider=Riders(AllGatherStage1(late_shards[n_mid:n_kv]), AllGatherStage2(partly_mid)))
    partly_kv, full_mid = rode[:n_kv - n_mid], rode[n_kv - n_mid:]
    (att, lse, att_t), rode = fox_fwd(
        z3, c_col, c_row, name="fox_fwd",
        rider=Riders(AllGatherStage1(late_shards[n_kv:]), AllGatherStage2(partly_kv)))
    partly_ffn, full_kv = rode[:len(LATE_FFN)], rode[len(LATE_FFN):]
    wf = {n: _full_from_gathered(n, blk) for n, blk in zip(LATE_MID + LATE_KV, full_mid + full_kv)}
    (x1, hx, hx_t), full_ffn = matmul(
        [conv_out.reshape(T, CONV_CH), att.reshape(T, FOX_W)], [wf["w_out"], wf["w_out"]], b_blk=[0, 1],
        res=x2d, tn=D, name="mm_out", post=rms_fwd_epilogue(g_x), rider=AllGatherStage2(partly_ffn))
    wf.update({n: _full_from_gathered(n, blk) for n, blk in zip(LATE_FFN, full_ffn)})
    qm = matmul(hx, wf["w_mq"], out_dtype=BF16, tn=D, name="mm_mq")
    mem2d = mem.reshape(B * M, D)
    mem_n, mem_n_t = rmsnorm_fwd(mem2d, g_mem, name="rms_mem")
    kv = matmul(mem_n, wf["w_mkv"], out_dtype=BF16, tn=2 * D, name="mm_mkv").reshape(B, M, 2 * D)
    o, o_t = xattn_fwd(qm.reshape(B, S, D), kv, name="xattn_fwd")
    o = o.reshape(T, D)
    x2, hf, hf_t = matmul(o, wf["w_mo"], res=x1, tn=D, name="mm_mo", post=rms_fwd_epilogue(g_ffn))
    gate, up, act, act_t = mm_swiglu_fwd(hf, wf["w_gu"], name="mm_gu")
    dx3, dg_final, loss = matmul(act, wf["w_down"], res=x2, tn=D, name="mm_down",
                                 post=loss_epilogue(g_final, target.reshape(T, D)))
    gw = {}
    gw["w_down"] = matmul(act_t, dx3, out_dtype=BF16, tm=1408, tn=512, name="dw_down")
    dgate, dup = mm_swiglu_bwd(dx3, wf["w_down"], gate, up, name="dx_down")
    gw["w_gu"] = [matmul(hf_t, dgate, out_dtype=BF16, tn=1408, name="dw_gate"),
                  matmul(hf_t, dup, out_dtype=BF16, tn=1408, name="dw_up")]
    g42 = [_shards_from_full(n, gw[n]) for n in RS_GROUPS[0]]
    (dx2, dg_ffn), got = matmul([dgate, dup], [wf["w_gu"], wf["w_gu"]], b_blk=[0, 1], tb=True, tm=256, tn=D,
                                name="dx_gu", post=rms_bwd_epilogue(x2, g_ffn, dx3), rider=SiblingExchange(g42))
    parts = pair_sums(RS_GROUPS[0], g42, got)
    gw["w_mo"] = matmul(o_t, dx2, out_dtype=BF16, tn=D, name="dw_mo")
    do = matmul(dx2, wf["w_mo"], tb=True, out_dtype=BF16, tn=D, name="dx_mo")
    dqm, dkv = xattn_bwd(qm.reshape(B, S, D), kv, do.reshape(B, S, D), name="xattn_bwd")
    dqm = dqm.reshape(T, D)
    dkv = dkv.reshape(B * M, 2 * D)
    gw["w_mq"] = matmul(hx_t, dqm, out_dtype=BF16, tn=D, name="dw_mq")
    dx1, dg_x = matmul(dqm, wf["w_mq"], tb=True, tn=D, name="dx_mq", post=rms_bwd_epilogue(x1, g_x, dx2))
    gw["w_mkv"] = matmul(mem_n_t, dkv, out_dtype=BF16, tn=D, name="dw_mkv")
    _, dg_mem = matmul(dkv, wf["w_mkv"], tb=True, tn=D, name="dx_mkv", post=rms_bwd_epilogue(mem2d, g_mem, None))
    gw["w_out"] = jnp.concatenate([matmul(conv_t, dx1, out_dtype=BF16, tn=D, name="dw_out_conv"),
                                   matmul(att_t, dx1, out_dtype=BF16, tn=D, name="dw_out_att")], axis=0)
    g42 = [_shards_from_full(n, gw[n]) for n in RS_GROUPS[1]]
    dcat, got = matmul(dx1, wf["w_out"], tb=True, out_dtype=BF16, tn=D, name="dx_out", rider=SiblingExchange(g42))
    dcat = dcat.reshape(B, S, D)
    parts.update(pair_sums(RS_GROUPS[1], g42, got))
    dy, dconv_w, dvec = conv_branch_bwd_a(z3, conv_y, dcat, ln_g, ln_b, name="conv_bwd_a")
    dug = conv_branch_bwd_b(z3, dy, conv_w, name="conv_bwd_b")
    gots = {}
    (dq, stats), got = fox_bwd_dq(z3, dcat, lse, c_col, c_row, name="fox_bwd_dq",
                                  rider=ChipExchange([parts[n] for n in RS_GROUPS[0]]))
    gots.update(zip(RS_GROUPS[0], got))
    (dk, dv, dc), got = fox_bwd_dkdv(z3, dcat, stats, c_col, name="fox_bwd_dkdv",
                                     rider=ChipExchange([parts[n] for n in RS_GROUPS[1]]))
    gots.update(zip(RS_GROUPS[1], got))
    df, db_f = fgate_bwd(dc, f_raw, b_f, name="fgate_bwd")
    dug2 = dug.reshape(T, n_ug)
    dqkv = jnp.concatenate([dq, dk, dv], axis=-1).reshape(T, 3 * FOX_W)
    df2 = df.reshape(T, LANES)
    dw_in = [matmul(h_t, dug2, out_dtype=BF16, tn=n_ug, name="dw_in_ug"),
             matmul(h_t, dqkv, out_dtype=BF16, tn=3 * FOX_W, name="dw_in_qkv"),
             matmul(h_t, df2, out_dtype=BF16, name="dw_f")[:, :FOX_HEADS]]
    g42 = [_shards_from_full("w_in", dw_in)]
    parts.update(pair_sums(RS_GROUPS[2], g42, run_rider(SiblingExchange(g42), name="rs_sibling_in")))
    (dx, dg_mix), (gots["w_in"],) = matmul(
        [dug2, dqkv, df2], [w_ug, w_qkv, w_f], tb=True, tn=D, name="dx_in",
        post=rms_bwd_epilogue(x2d, g_mix, dx1, out_dtype=F32), rider=ChipExchange([parts["w_in"]]))
    gs = dict(g_mix=dg_mix, b_f=db_f[:, :FOX_HEADS], conv_w=dconv_w[:CONV_K], conv_b=dvec[0:1],
              ln_g=dvec[1:2], ln_b=dvec[2:3], g_x=dg_x, g_mem=dg_mem, g_ffn=dg_ffn, g_final=dg_final)
    return loss, dx.reshape(B, S, D), gs, {n: (parts[n], gots[n]) for n in BIG}


def _me():
    return lax.axis_index("x"), lax.axis_index("y"), lax.axis_index("c")


def _any_specs(n):
    return [pl.BlockSpec(memory_space=pl.ANY)] * n


def all_gather(xs, *, name):
    n = len(xs)

    def body(*refs):
        x_refs, out_refs = refs[:n], refs[n:2 * n]
        send_sems, recv_sems, local_sems = refs[2 * n:]
        x, y, c = _me()
        me, sibling = (x, y, c), (x, y, 1 - c)
        chips = [(1 - x, y), (x, 1 - y), (1 - x, 1 - y)]

        def slot(a, px, py, pc):
            return out_refs[a].at[4 * px + 2 * py + pc]

        def copy(a, k, block, to, own=False):
            return pltpu.make_async_remote_copy(
                src_ref=x_refs[a] if own else slot(a, *block), dst_ref=slot(a, *block),
                send_sem=send_sems.at[k, a], recv_sem=recv_sems.at[k, a], device_id=to, device_id_type=MESH)

        mine = [pltpu.make_async_copy(x_refs[a], slot(a, *me), local_sems.at[a]) for a in range(n)]
        first = [copy(a, 0, me, sibling, own=True) for a in range(n)]
        first += [copy(a, 1 + j, me, (*chip, c), own=True) for j, chip in enumerate(chips) for a in range(n)]
        for cp in mine + first:
            cp.start()
        passed = []
        for j, chip in enumerate(chips):
            for a in range(n):
                copy(a, 1 + j, (*chip, c), me).wait_recv()
                passed.append(copy(a, 4 + j, (*chip, c), sibling))
                passed[-1].start()
        for a in range(n):
            copy(a, 0, sibling, me).wait_recv()
            for j, chip in enumerate(chips):
                copy(a, 4 + j, (*chip, 1 - c), me).wait_recv()
        for cp in first + passed:
            cp.wait_send()
        for cp in mine:
            cp.wait()

    return _call(
        body, name=name, in_specs=_any_specs(n), out_specs=_any_specs(n),
        out_shape=[jax.ShapeDtypeStruct((N_DEV,) + v.shape, v.dtype) for v in xs],
        scratch_shapes=[pltpu.SemaphoreType.DMA((7, n)), pltpu.SemaphoreType.DMA((7, n)),
                        pltpu.SemaphoreType.DMA((n,))],
    )(*xs)


SIBLING_BARRIER = 1
CHIPS_BARRIER = 2
GATHER_BARRIER = 3


class SiblingExchange:
    collective_id = SIBLING_BARRIER

    def __init__(self, gs):
        n = len(gs)
        self.n, self.inputs = n, list(gs)
        self.out_shape = [jax.ShapeDtypeStruct((4,) + g.shape[2:], g.dtype) for g in gs]
        self.scratch = [pltpu.SemaphoreType.DMA((n,)), pltpu.SemaphoreType.DMA((n,))]

    @staticmethod
    def barrier_peers():
        x, y, c = _me()
        return [(x, y, 1 - c)]

    def _copies(self, g_refs, out_refs, sems):
        send_sems, recv_sems = sems
        x, y, c = _me()
        return [pltpu.make_async_remote_copy(
            src_ref=g_refs[a].at[:, 1 - c], dst_ref=out_refs[a], send_sem=send_sems.at[a],
            recv_sem=recv_sems.at[a], device_id=(x, y, 1 - c), device_id_type=MESH) for a in range(self.n)]

    def start(self, in_refs, out_refs, sems):
        for cp in self._copies(in_refs, out_refs, sems):
            cp.start()

    def finish(self, in_refs, out_refs, sems):
        for cp in self._copies(in_refs, out_refs, sems):
            cp.wait()


def run_rider(rider, *, name):
    return hosted_call(None, rider, name=name, grid=(), in_specs=[], out_specs=[], out_shape=[],
                       scratch_shapes=[], args=[])[1]


class ChipExchange:
    collective_id = CHIPS_BARRIER

    @staticmethod
    def barrier_peers():
        x, y, c = _me()
        return [(1 - x, y, c), (x, 1 - y, c), (1 - x, 1 - y, c)]

    def __init__(self, ps):
        n = len(ps)
        self.n, self.inputs = n, list(ps)
        self.out_shape = [jax.ShapeDtypeStruct(p.shape, p.dtype) for p in ps]
        self.scratch = [pltpu.SemaphoreType.DMA((3, n)), pltpu.SemaphoreType.DMA((3, n))]

    def _copies(self, p_refs, out_refs, sems, outgoing):
        send_sems, recv_sems = sems
        x, y, c = _me()
        my_chip = 2 * x + y
        cps = []
        for k in range(3):
            px, py = x ^ ((k + 1) >> 1), y ^ ((k + 1) & 1)
            src, dst = (2 * px + py, my_chip) if outgoing else (my_chip, 2 * px + py)
            for a in range(self.n):
                cps.append(pltpu.make_async_remote_copy(
                    src_ref=p_refs[a].at[src], dst_ref=out_refs[a].at[dst], send_sem=send_sems.at[k, a],
                    recv_sem=recv_sems.at[k, a], device_id=(px, py, c), device_id_type=MESH))
        return cps

    def start(self, in_refs, out_refs, sems):
        for cp in self._copies(in_refs, out_refs, sems, True):
            cp.start()

    def finish(self, in_refs, out_refs, sems):
        for cp in self._copies(in_refs, out_refs, sems, False):
            cp.wait_recv()
        for cp in self._copies(in_refs, out_refs, sems, True):
            cp.wait_send()


class AllGatherStage1:
    collective_id = GATHER_BARRIER

    @staticmethod
    def barrier_peers():
        x, y, c = _me()
        return [(x, y, 1 - c), (1 - x, y, c), (x, 1 - y, c), (1 - x, 1 - y, c)]

    def __init__(self, xs):
        n = len(xs)
        self.n, self.inputs = n, list(xs)
        self.out_shape = [jax.ShapeDtypeStruct((N_DEV,) + v.shape, v.dtype) for v in xs]
        self.scratch = [pltpu.SemaphoreType.DMA((4, n)), pltpu.SemaphoreType.DMA((4, n)),
                        pltpu.SemaphoreType.DMA((n,))]

    def _copies(self, x_refs, out_refs, sems, kind):
        send_sems, recv_sems, local_sems = sems
        x, y, c = _me()
        slot = lambda a, d: out_refs[a].at[4 * d[0] + 2 * d[1] + d[2]]
        if kind == "local":
            return [pltpu.make_async_copy(x_refs[a], slot(a, (x, y, c)), local_sems.at[a]) for a in range(self.n)]
        cps = []
        for k, peer in enumerate([(x, y, 1 - c), (1 - x, y, c), (x, 1 - y, c), (1 - x, 1 - y, c)]):
            for a in range(self.n):
                cps.append(pltpu.make_async_remote_copy(
                    src_ref=x_refs[a], dst_ref=slot(a, (x, y, c) if kind == "out" else peer),
                    send_sem=send_sems.at[k, a], recv_sem=recv_sems.at[k, a], device_id=peer, device_id_type=MESH))
        return cps

    def start(self, in_refs, out_refs, sems):
        for cp in self._copies(in_refs, out_refs, sems, "local") + self._copies(in_refs, out_refs, sems, "out"):
            cp.start()

    def finish(self, in_refs, out_refs, sems):
        for cp in self._copies(in_refs, out_refs, sems, "in"):
            cp.wait_recv()
        for cp in self._copies(in_refs, out_refs, sems, "out"):
            cp.wait_send()
        for cp in self._copies(in_refs, out_refs, sems, "local"):
            cp.wait()


class AllGatherStage2:
    collective_id = SIBLING_BARRIER

    @staticmethod
    def barrier_peers():
        x, y, c = _me()
        return [(x, y, 1 - c)]

    def __init__(self, outs):
        n = len(outs)
        self.n, self.inputs = n, list(outs)
        self.out_shape = [jax.ShapeDtypeStruct(o.shape, o.dtype) for o in outs]
        self.scratch = [pltpu.SemaphoreType.DMA((3, n)), pltpu.SemaphoreType.DMA((3, n))]
        self.aliases = {a: a for a in range(n)}

    def _copies(self, out_refs, sems, outgoing):
        send_sems, recv_sems = sems
        x, y, c = _me()
        cps = []
        for k, (px, py) in enumerate([(1 - x, y), (x, 1 - y), (1 - x, 1 - y)]):
            for a in range(self.n):
                cps.append(pltpu.make_async_remote_copy(
                    src_ref=out_refs[a].at[4 * px + 2 * py + c],
                    dst_ref=out_refs[a].at[4 * px + 2 * py + (c if outgoing else 1 - c)],
                    send_sem=send_sems.at[k, a], recv_sem=recv_sems.at[k, a], device_id=(x, y, 1 - c),
                    device_id_type=MESH))
        return cps

    def start(self, in_refs, out_refs, sems):
        for cp in self._copies(out_refs, sems, True):
            cp.start()

    def finish(self, in_refs, out_refs, sems):
        for cp in self._copies(out_refs, sems, False):
            cp.wait_recv()
        for cp in self._copies(out_refs, sems, True):
            cp.wait_send()


class Riders:
    def __init__(self, *riders):
        self.riders = riders
        self.collective_id = riders[0].collective_id
        self.barrier_peers = riders[0].barrier_peers
        self.inputs = [v for r in riders for v in r.inputs]
        self.out_shape = [s for r in riders for s in r.out_shape]
        self.scratch = [s for r in riders for s in r.scratch]
        self.aliases, i0, o0 = {}, 0, 0
        for r in riders:
            self.aliases.update({i0 + i: o0 + o for i, o in getattr(r, "aliases", {}).items()})
            i0, o0 = i0 + len(r.inputs), o0 + len(r.out_shape)

    def _split(self, in_refs, out_refs, sems):
        i0 = o0 = s0 = 0
        for r in self.riders:
            ni, no, ns = len(r.inputs), len(r.out_shape), len(r.scratch)
            yield r, in_refs[i0:i0 + ni], out_refs[o0:o0 + no], sems[s0:s0 + ns]
            i0, o0, s0 = i0 + ni, o0 + no, s0 + ns

    def start(self, in_refs, out_refs, sems):
        for r, i, o, s in self._split(in_refs, out_refs, sems):
            r.start(i, o, s)

    def finish(self, in_refs, out_refs, sems):
        for r, i, o, s in self._split(in_refs, out_refs, sems):
            r.finish(i, o, s)


def _peer_barrier(peers):
    barrier = pltpu.get_barrier_semaphore()
    for peer in peers:
        pl.semaphore_signal(barrier, inc=1, device_id=peer, device_id_type=MESH)
    pl.semaphore_wait(barrier, len(peers))


def hosted_call(body, rider, *, name, grid, in_specs, out_specs, out_shape, scratch_shapes, args, vmem=None):
    n_in, n_out, n_scr = len(in_specs), len(out_specs), len(scratch_shapes)
    r_in, r_out = (len(rider.inputs), len(rider.out_shape)) if rider is not None else (0, 0)
    own_barrier = getattr(rider, "collective_id", None) is not None

    def wrapped(*refs):
        ins, refs = refs[:n_in], refs[n_in:]
        rins, refs = refs[:r_in], refs[r_in:]
        outs, refs = refs[:n_out], refs[n_out:]
        routs, refs = refs[:r_out], refs[r_out:]
        scr, rscr = refs[:n_scr], refs[n_scr:]
        ids = [pl.program_id(d) for d in range(len(grid))]
        first = functools.reduce(jnp.logical_and, [i == 0 for i in ids], True)
        last = functools.reduce(jnp.logical_and, [i == g - 1 for i, g in zip(ids, grid)], True)

        def begin():
            if own_barrier:
                _peer_barrier(rider.barrier_peers())
            rider.start(rins, routs, rscr)

        if rider is not None and grid:
            pl.when(first)(begin)
        elif rider is not None:
            begin()
        if body is not None:
            body(*ins, *outs, *scr)
        if rider is not None and grid:
            pl.when(last)(lambda: rider.finish(rins, routs, rscr))
        elif rider is not None:
            rider.finish(rins, routs, rscr)

    kw = dict(grid=grid) if grid else {}
    aliases = getattr(rider, "aliases", {})
    if aliases:
        kw["input_output_aliases"] = {n_in + i: n_out + o for i, o in aliases.items()}
    if grid or vmem is not None or own_barrier:
        kw["compiler_params"] = _params(("arbitrary",) * len(grid) if grid else None, vmem,
                                        rider.collective_id if own_barrier else None)
    res = _call(
        wrapped, name=name, in_specs=list(in_specs) + _any_specs(r_in), out_specs=list(out_specs) + _any_specs(r_out),
        out_shape=list(out_shape) + (rider.out_shape if rider is not None else []),
        scratch_shapes=list(scratch_shapes) + (rider.scratch if rider is not None else []), **kw,
    )(*args, *(rider.inputs if rider is not None else []))
    return list(res[:n_out]), list(res[n_out:])


def _pick_rows(r, target=256):
    best = None
    for d in range(16, min(r, target) + 1, 16):
        if r % d == 0:
            best = d
    return r if best is None else best


def pair_sum(g, got, *, name):
    _, _, R, C = g.shape
    tr = _pick_rows(R)

    def body(g_ref, got_ref, o_ref):
        mine = jnp.where(lax.axis_index("c") == 0, g_ref[:, 0], g_ref[:, 1])
        o_ref[...] = (mine.astype(F32) + got_ref[...].astype(F32)).astype(o_ref.dtype)

    return _call(
        body, name=name, grid=(R // tr,),
        in_specs=[pl.BlockSpec((4, 2, tr, C), lambda i: (0, 0, i, 0)), pl.BlockSpec((4, tr, C), lambda i: (0, i, 0))],
        out_specs=pl.BlockSpec((4, tr, C), lambda i: (0, i, 0)),
        out_shape=jax.ShapeDtypeStruct((4, R, C), g.dtype),
        compiler_params=_params(("parallel",)),
    )(g, got)


def chip_sum_adamw(p, got, w, m, v, *, name):
    _, R, C = p.shape
    assert w.shape == (1, R, C), (name, w.shape, p.shape)
    tr = _pick_rows(R)

    def body(p_ref, got_ref, w_ref, m_ref, v_ref, g_ref, d_ref, mo_ref, vo_ref):
        my_chip = 2 * lax.axis_index("x") + lax.axis_index("y")
        g = jnp.zeros((tr, C), F32)
        for j in range(4):
            g = g + jnp.where(my_chip == j, p_ref[j], got_ref[j]).astype(F32)
        g_ref[0] = g
        d_ref[0], mo_ref[0], vo_ref[0] = _adamw_math(w_ref[0], g, m_ref[0], v_ref[0])

    part = pl.BlockSpec((4, tr, C), lambda i: (0, i, 0))
    spec = pl.BlockSpec((1, tr, C), lambda i: (0, i, 0))
    return _call(
        body, name=name, grid=(R // tr,), in_specs=[part, part, spec, spec, spec], out_specs=[spec] * 4,
        out_shape=[jax.ShapeDtypeStruct((1, R, C), F32)] * 4,
        compiler_params=_params(("parallel",)),
    )(p, got, w, m, v)


def rows_sum(g8, *, name):
    _, R, C = g8.shape

    def body(g_ref, o_ref):
        acc = g_ref[0]
        for j in range(1, N_DEV):
            acc = acc + g_ref[j]
        o_ref[...] = acc

    return _call(body, name=name, out_shape=jax.ShapeDtypeStruct((R, C), F32))(g8)


def _adamw_math(w, g, m, v):
    m = ADAM_B1 * m + (1.0 - ADAM_B1) * g
    v = ADAM_B2 * v + (1.0 - ADAM_B2) * (g * g)
    m_hat = m / (1.0 - ADAM_B1 ** ADAM_STEP)
    v_hat = v / (1.0 - ADAM_B2 ** ADAM_STEP)
    delta = -ADAM_LR * (m_hat / (jnp.sqrt(v_hat) + ADAM_EPS) + ADAM_WD * w)
    return delta, m, v


def to_bf16(xs, *, name):
    def body(*refs):
        for x_ref, o_ref in zip(refs[:len(xs)], refs[len(xs):]):
            o_ref[...] = x_ref[...].astype(BF16)

    total = sum(_nbytes(v.shape, F32) + _nbytes(v.shape, BF16) for v in xs)
    return _call(body, name=name, out_shape=[jax.ShapeDtypeStruct(v.shape, BF16) for v in xs],
                 compiler_params=_params(vmem=2 * total + (4 << 20)))(*xs)


def adamw_small(wgmv, *, name):
    n = len(wgmv)

    def body(*refs):
        ins, outs = refs[:4 * n], refs[4 * n:]
        for a in range(n):
            w_ref, g_ref, m_ref, v_ref = ins[4 * a:4 * a + 4]
            d, mn, vn = _adamw_math(w_ref[...], g_ref[...], m_ref[...], v_ref[...])
            outs[3 * a][...] = d
            outs[3 * a + 1][...] = mn
            outs[3 * a + 2][...] = vn

    flat = [t for tup in wgmv for t in tup]
    res = _call(
        body, name=name,
        out_shape=[jax.ShapeDtypeStruct(tup[0].shape, F32) for tup in wgmv for _ in range(3)],
    )(*flat)
    return [tuple(res[3 * a:3 * a + 3]) for a in range(n)]


BIG = ("w_in", "w_out", "w_mq", "w_mkv", "w_mo", "w_gu", "w_down")
COL_SHARDED = ("w_in", "w_mkv", "w_gu")
SMALL = ("g_mix", "b_f", "conv_w", "conv_b", "ln_g", "ln_b", "g_x", "g_mem", "g_ffn", "g_final")


def _full_from_gathered(n, blk):
    _, rr, cc = blk.shape
    if n in COL_SHARDED:
        return jnp.concatenate([blk[k] for k in range(N_DEV)], axis=1)
    return blk.reshape(N_DEV * rr, cc)


def _shards_from_full(n, g):
    pieces = g if isinstance(g, list) else [g]
    rr, cc = pieces[0].shape[0], sum(p.shape[1] for p in pieces)
    if n in COL_SHARDED:
        w = cc // N_DEV
        return jnp.stack([_columns(pieces, k * w, w) for k in range(N_DEV)]).reshape(4, 2, rr, w)
    return pieces[0].reshape(4, 2, rr // N_DEV, cc)


def _columns(pieces, start, width):
    out, c0 = [], 0
    for p in pieces:
        lo, hi = max(start, c0), min(start + width, c0 + p.shape[1])
        if lo < hi:
            out.append(p[:, lo - c0:hi - c0])
        c0 += p.shape[1]
    return out[0] if len(out) == 1 else jnp.concatenate(out, axis=1)


def _small_layout():
    sizes = dict(g_mix=1024, b_f=8, conv_w=CONV_K * CONV_CH, conv_b=512, ln_g=512, ln_b=512, g_x=1024,
                 g_mem=1024, g_ffn=1024, g_final=1024, loss=1)
    lay, r0 = {}, 0
    for n, sz in sizes.items():
        r = -(-sz // LANES)
        lay[n] = (r0, r, sz)
        r0 += r
    return lay, -(-r0 // 8) * 8


def kernel(x, mem, g_mix, w_in, b_f, conv_w, conv_b, ln_g, ln_b, w_out, g_x, g_mem, w_mq, w_mkv, w_mo, g_ffn, w_gu, w_down, g_final, loss_target, m_g_mix, m_w_in, m_b_f, m_conv_w, m_conv_b, m_ln_g, m_ln_b, m_w_out, m_g_x, m_g_mem, m_w_mq, m_w_mkv, m_w_mo, m_g_ffn, m_w_gu, m_w_down, m_g_final, v_g_mix, v_w_in, v_b_f, v_conv_w, v_conv_b, v_ln_g, v_ln_b, v_w_out, v_g_x, v_g_mem, v_w_mq, v_w_mkv, v_w_mo, v_g_ffn, v_w_gu, v_w_down, v_g_final):
    names = ["g_mix", "w_in", "b_f", "conv_w", "conv_b", "ln_g", "ln_b", "w_out", "g_x", "g_mem", "w_mq",
             "w_mkv", "w_mo", "g_ffn", "w_gu", "w_down", "g_final"]
    W = dict(zip(names, [g_mix, w_in, b_f, conv_w, conv_b, ln_g, ln_b, w_out, g_x, g_mem, w_mq, w_mkv, w_mo,
                         g_ffn, w_gu, w_down, g_final]))
    Mo = dict(zip(names, [m_g_mix, m_w_in, m_b_f, m_conv_w, m_conv_b, m_ln_g, m_ln_b, m_w_out, m_g_x, m_g_mem,
                          m_w_mq, m_w_mkv, m_w_mo, m_g_ffn, m_w_gu, m_w_down, m_g_final]))
    Vo = dict(zip(names, [v_g_mix, v_w_in, v_b_f, v_conv_w, v_conv_b, v_ln_g, v_ln_b, v_w_out, v_g_x, v_g_mem,
                          v_w_mq, v_w_mkv, v_w_mo, v_g_ffn, v_w_gu, v_w_down, v_g_final]))
    dev = 4 * lax.axis_index("x") + 2 * lax.axis_index("y") + lax.axis_index("c")

    two = lambda a: a.reshape(-1, a.shape[-1])
    cw_shard = jnp.pad(two(conv_w), ((0, HALO - CONV_K), (0, 0)))
    sp = dict(g_mix=g_mix, b_f=b_f, conv_b=conv_b, ln_g=ln_g, ln_b=ln_b, g_x=g_x, g_mem=g_mem,
              g_ffn=g_ffn, g_final=g_final)
    shards = to_bf16([two(W[n]) for n in ("w_in",) + LATE], name="cast_shards")
    loss_blk, grad_x, gs, reduced = local_step(x, mem, loss_target, sp, [shards[0], cw_shard], shards[1:])

    lay, rs = _small_layout()
    small = {**{n: gs[n] for n in SMALL}, "loss": loss_blk[:, :1]}
    parts = []
    for n, (r0, r, sz) in lay.items():
        flat = small[n].reshape(-1).astype(F32)
        parts.append(jnp.pad(flat, (0, r * LANES - sz)).reshape(r, LANES))
    spack = jnp.concatenate(parts, axis=0)
    spack = jnp.pad(spack, ((0, rs - spack.shape[0]), (0, 0)))
    ssum = rows_sum(all_gather([spack], name="ag_small")[0], name="small_sum")
    gsmall = {n: ssum[r0:r0 + r].reshape(-1)[:sz] for n, (r0, r, sz) in lay.items()}
    loss = gsmall["loss"].reshape(())

    grads, delta, new_m, new_v = {}, {}, {}, {}
    for n in BIG:
        p, o = reduced[n]
        grads[n], delta[n], new_m[n], new_v[n] = chip_sum_adamw(p, o, W[n], Mo[n], Vo[n], name="adamw_" + n)
    for n in SMALL:
        if n == "conv_w":
            full = gsmall[n].reshape(CONV_K, CONV_CH)
            ncol = conv_w.shape[-1]
            grads[n] = lax.dynamic_slice(full, (0, dev * ncol), (CONV_K, ncol)).reshape(conv_w.shape)
        else:
            grads[n] = gsmall[n].reshape(W[n].shape)
    upd = adamw_small([(two(W[n]), two(grads[n]), two(Mo[n]), two(Vo[n])) for n in SMALL], name="adamw_small")
    for n, (d, mn, vn) in zip(SMALL, upd):
        shp = W[n].shape
        delta[n], new_m[n], new_v[n] = d.reshape(shp), mn.reshape(shp), vn.reshape(shp)
    return (loss, grad_x, *[grads[n] for n in names], *[delta[n] for n in names],
            *[new_m[n] for n in names], *[new_v[n] for n in names])
```

```python
import functools
import math

import jax
import jax.numpy as jnp
from jax import lax
from jax.experimental import pallas as pl
from jax.experimental.pallas import tpu as pltpu

F32 = jnp.float32
BF16 = jnp.bfloat16
EPS = 1e-6
N_DEV = 8
CONV_CH = 512
CONV_K = 31
FOX_HEADS = 8
FOX_HEAD_DIM = 64
FOX_W = 512
MEM_HEADS = 4
MEM_HEAD_DIM = 256
HALO = 32
LANES = 128
ADAM_LR, ADAM_B1, ADAM_B2, ADAM_EPS, ADAM_WD, ADAM_STEP = 0.001, 0.9, 0.999, 1e-08, 0.01, 10
NEG = -1e30
VMEM_CAP = 60 * 1024 * 1024
MESH = pl.DeviceIdType.MESH


def _call(body, **kw):
    kw["out_shape"] = jax.tree.map(lambda s: pltpu.HBM(s.shape, s.dtype), kw["out_shape"])
    call = pl.pallas_call(body, **kw)
    return lambda *args: call(*[pltpu.with_memory_space_constraint(a, pltpu.HBM) for a in args])


def _params(sem=None, vmem=None, collective_id=None):
    kw = {} if collective_id is None else {"collective_id": collective_id}
    if sem is not None:
        kw["dimension_semantics"] = sem
    if vmem is not None:
        kw["vmem_limit_bytes"] = int(min(VMEM_CAP, vmem))
    return pltpu.CompilerParams(**kw)


def _nbytes(shape, dtype):
    return math.prod(shape) * jnp.dtype(dtype).itemsize


def _pick(n, target):
    best = None
    for d in range(LANES, min(n, target) + 1, LANES):
        if n % d == 0:
            best = d
    return n if best is None else best


class RowEpilogue:
    def __init__(self, fn, ins, outs):
        self.fn, self.ins, self.outs = fn, list(ins), list(outs)


def matmul(a, b, *, tb=False, out_dtype=None, res=None, tm=512, tn=512, name, rider=None, b_blk=None, post=None):
    a_list = list(a) if isinstance(a, (list, tuple)) else [a]
    b_list = list(b) if isinstance(b, (list, tuple)) else [b]
    n = len(a_list)
    assert len(b_list) == n
    M = a_list[0].shape[0]
    N = b_list[0].shape[0] if tb else b_list[0].shape[1]
    tm, tn = _pick(M, tm), _pick(N, tn)
    assert M % tm == 0 and N % tn == 0, (name, M, N, tm, tn)
    dn = (((1,), (1 if tb else 0,)), ((), ()))

    n_res = int(res is not None)
    n_pin = len(post.ins) if post is not None else 0

    def body(*refs):
        acc = None
        for a_ref, b_ref in zip(refs[:n], refs[n:2 * n]):
            p = lax.dot_general(a_ref[...].astype(BF16), b_ref[...].astype(BF16), dn, preferred_element_type=F32)
            acc = p if acc is None else acc + p
        if res is not None:
            acc = acc + refs[2 * n][...].astype(F32)
        if post is None:
            refs[-1][...] = acc.astype(out_dtype)
            return
        first_in = 2 * n + n_res
        vals = post.fn(acc, *[r[...] for r in refs[first_in:first_in + n_pin]])
        for (dtype, kind), o_ref, val in zip(post.outs, refs[first_in + n_pin:], vals):
            if kind in ("row", "rowT"):
                o_ref[...] = val.astype(dtype)
            else:
                @pl.when(pl.program_id(0) == 0)
                def _(o_ref=o_ref):
                    o_ref[...] = jnp.zeros_like(o_ref)

                o_ref[...] += jnp.broadcast_to(val, o_ref.shape).astype(dtype)

    o_spec = pl.BlockSpec((tm, tn), lambda i, j: (i, j))
    in_specs, est = [], 2 * _nbytes((tm, tn), out_dtype or F32) + 2 * _nbytes((tm, tn), F32)
    for av in a_list:
        assert av.shape[0] == M
        in_specs.append(pl.BlockSpec((tm, av.shape[1]), lambda i, j: (i, 0)))
        est += (2 * jnp.dtype(av.dtype).itemsize + (av.dtype != BF16) * 2) * tm * av.shape[1]
    for idx, (av, bv) in enumerate(zip(a_list, b_list)):
        K = av.shape[1]
        kb = 0 if b_blk is None else b_blk[idx]
        assert bv.shape[0 if tb else 1] == N and bv.shape[1 if tb else 0] >= (kb + 1) * K, (name, av.shape, bv.shape)
        assert b_blk is not None or bv.shape[1 if tb else 0] == K, (name, av.shape, bv.shape)
        in_specs.append(pl.BlockSpec((tn, K), lambda i, j, kb=kb: (j, kb)) if tb
                        else pl.BlockSpec((K, tn), lambda i, j, kb=kb: (kb, j)))
        est += (2 * jnp.dtype(bv.dtype).itemsize + (bv.dtype != BF16) * 2) * tn * K
    args = a_list + b_list
    if res is not None:
        in_specs.append(o_spec)
        args.append(res)
        est += 2 * _nbytes((tm, tn), res.dtype)
    if post is None:
        out_specs, out_shape = [o_spec], [jax.ShapeDtypeStruct((M, N), out_dtype)]
    else:
        assert tn == N, (name, tn, N)
        row = pl.BlockSpec((tm, N), lambda i, j: (i, 0))
        for arr, kind in post.ins:
            in_specs.append(row if kind == "row" else pl.BlockSpec((1, N), lambda i, j: (0, 0)))
            args.append(arr)
            est += 2 * _nbytes((tm, N), arr.dtype) * (kind == "row")
        specs = {"row": (row, (M, N)), "rowT": (pl.BlockSpec((N, tm), lambda i, j: (0, i)), (N, M)),
                 "vec": (pl.BlockSpec((1, N), lambda i, j: (0, 0)), (1, N)),
                 "lanes": (pl.BlockSpec((1, LANES), lambda i, j: (0, 0)), (1, LANES))}
        out_specs = [specs[kind][0] for _, kind in post.outs]
        out_shape = [jax.ShapeDtypeStruct(specs[kind][1], dtype) for dtype, kind in post.outs]
        est += sum(2 * _nbytes((tm, N), dtype) + _nbytes((tm, N), F32) for dtype, kind in post.outs if kind[:3] == "row")
    outs, rode = hosted_call(
        body, rider, name=name, grid=(M // tm, N // tn), in_specs=in_specs, out_specs=out_specs,
        out_shape=out_shape, scratch_shapes=[], args=args, vmem=est + (8 << 20),
    )
    result = outs[0] if post is None else outs
    return result if rider is None else (result, rode)


def _rms_scale(x):
    return lax.rsqrt(jnp.mean(x * x, axis=-1, keepdims=True) + EPS)


def rmsnorm_fwd(x, g, *, name, tm=512, rider=None):
    T, D = x.shape
    tm = min(tm, T)

    def body(x_ref, g_ref, o_ref, ot_ref):
        xv = x_ref[...]
        h = xv * _rms_scale(xv) * g_ref[...]
        o_ref[...] = h.astype(BF16)
        ot_ref[...] = h.T.astype(BF16)

    (h, h_t), rode = hosted_call(
        body, rider, name=name, grid=(T // tm,),
        in_specs=[pl.BlockSpec((tm, D), lambda i: (i, 0)), pl.BlockSpec((1, D), lambda i: (0, 0))],
        out_specs=[pl.BlockSpec((tm, D), lambda i: (i, 0)), pl.BlockSpec((D, tm), lambda i: (0, i))],
        out_shape=[jax.ShapeDtypeStruct((T, D), BF16), jax.ShapeDtypeStruct((D, T), BF16)],
        scratch_shapes=[], args=(x, g),
    )
    return (h, h_t) if rider is None else (h, h_t, rode)


def _rms_bwd_math(xv, gv, dh):
    r = _rms_scale(xv)
    xh = xv * r
    dg = jnp.sum(dh * xh, axis=0, keepdims=True)
    dxh = dh * gv
    dx = r * (dxh - xh * jnp.mean(dxh * xh, axis=-1, keepdims=True))
    return dx, dg


def rms_fwd_epilogue(g):
    def fn(acc, gv):
        h = acc * _rms_scale(acc) * gv
        return acc, h, h.T
    return RowEpilogue(fn, [(g, "vec")], [(F32, "row"), (BF16, "row"), (BF16, "rowT")])


def rms_bwd_epilogue(x, g, dres, out_dtype=BF16):
    def fn(acc, xv, gv, *dr):
        dx, dg = _rms_bwd_math(xv, gv, acc)
        return (dx + dr[0].astype(F32) if dr else dx), dg
    ins = [(x, "row"), (g, "vec")] + ([(dres, "row")] if dres is not None else [])
    return RowEpilogue(fn, ins, [(out_dtype, "row"), (F32, "vec")])


def loss_epilogue(g, target):
    def fn(acc, gv, tv):
        e = acc * _rms_scale(acc) * gv - tv
        part = 0.5 * jnp.sum(jnp.mean(e * e, axis=-1, keepdims=True), axis=0, keepdims=True)
        dx, dg = _rms_bwd_math(acc, gv, e * (1.0 / acc.shape[-1]))
        return dx, dg, part
    return RowEpilogue(fn, [(g, "vec"), (target, "row")], [(BF16, "row"), (F32, "vec"), (F32, "lanes")])


def _sigmoid(v):
    return 0.5 * jnp.tanh(0.5 * v) + 0.5


def _glu(blk):
    u = blk[:, :CONV_CH].astype(F32)
    gt = blk[:, CONV_CH:].astype(F32)
    return u * _sigmoid(gt)


def _fill_causal_ext(ext, cur_ref, halo_ref, s, ts):
    ext[pl.ds(HALO, ts), :] = _glu(cur_ref[0])
    hal = _glu(halo_ref[0])
    ext[pl.ds(0, HALO), :] = jnp.where(s > 0, hal, 0.0)


SUBLANES = 8


def _make_shifted(ext, sh):
    n = ext.shape[0]
    full = ext[...]
    for r in range(1, SUBLANES):
        sh[r - 1] = pltpu.roll(full, n - r, 0)


def _tap(ext, sh, off, ts):
    r = off % SUBLANES
    return ext[pl.ds(off, ts), :] if r == 0 else sh[r - 1, pl.ds(off - r, ts), :]


def _causal_conv(ext, sh, w_ref, ts):
    acc = jnp.zeros((ts, CONV_CH), F32)
    for j in range(CONV_K):
        acc = acc + _tap(ext, sh, HALO - (CONV_K - 1) + j, ts) * w_ref[pl.ds(j, 1), :]
    return acc


def _ln_stats(y):
    mu = jnp.mean(y, axis=-1, keepdims=True)
    yc = y - mu
    rstd = lax.rsqrt(jnp.mean(yc * yc, axis=-1, keepdims=True) + EPS)
    return yc * rstd, rstd


def _conv_specs(ts, S):
    nh = ts // HALO
    cur = pl.BlockSpec((1, ts, 2 * CONV_CH), lambda b, s: (b, s, 0))
    halo = pl.BlockSpec((1, HALO, 2 * CONV_CH), lambda b, s: (b, jnp.maximum(s * nh - 1, 0), 0))
    w = pl.BlockSpec((HALO, CONV_CH), lambda b, s: (0, 0))
    vec = pl.BlockSpec((1, CONV_CH), lambda b, s: (0, 0))
    return cur, halo, w, vec


def conv_branch_fwd(ug, conv_w, conv_b, ln_g, ln_b, *, name, ts=256, rider=None):
    B, S, _ = ug.shape
    ts = min(ts, S)
    ns = S // ts
    cur, halo, w, vec = _conv_specs(ts, S)

    def body(cur_ref, halo_ref, w_ref, cb_ref, lg_ref, lb_ref, o_ref, ot_ref, y_ref, ext, sh):
        _fill_causal_ext(ext, cur_ref, halo_ref, pl.program_id(1), ts)
        _make_shifted(ext, sh)
        y = _causal_conv(ext, sh, w_ref, ts) + cb_ref[...]
        y_ref[0] = y
        yh, _ = _ln_stats(y)
        ln = yh * lg_ref[...] + lb_ref[...]
        out = ln * _sigmoid(ln)
        o_ref[0] = out.astype(BF16)
        ot_ref[...] = out.T.astype(BF16)

    return hosted_call(
        body, rider, name=name, grid=(B, ns), in_specs=[cur, halo, w, vec, vec, vec],
        out_specs=[pl.BlockSpec((1, ts, CONV_CH), lambda b, s: (b, s, 0)),
                   pl.BlockSpec((CONV_CH, ts), lambda b, s: (0, b * ns + s)),
                   pl.BlockSpec((1, ts, CONV_CH), lambda b, s: (b, s, 0))],
        out_shape=[jax.ShapeDtypeStruct((B, S, CONV_CH), BF16), jax.ShapeDtypeStruct((CONV_CH, B * S), BF16),
                   jax.ShapeDtypeStruct((B, S, CONV_CH), F32)],
        scratch_shapes=[pltpu.VMEM((ts + HALO, CONV_CH), F32),
                        pltpu.VMEM((SUBLANES - 1, ts + HALO, CONV_CH), F32)],
        args=(ug, ug, conv_w, conv_b, ln_g, ln_b),
    )


def conv_branch_bwd_a(ug, y, dcat, ln_g, ln_b, *, name, ts=256):
    B, S, _ = ug.shape
    ts = min(ts, S)
    cur, halo, _, vec = _conv_specs(ts, S)
    tile = pl.BlockSpec((1, ts, CONV_CH), lambda b, s: (b, s, 0))

    def body(cur_ref, halo_ref, y_ref, d_ref, lg_ref, lb_ref, dy_ref, dw_ref, dv_ref, ext, sh):
        _fill_causal_ext(ext, cur_ref, halo_ref, pl.program_id(1), ts)
        _make_shifted(ext, sh)
        yh, rstd = _ln_stats(y_ref[0])
        lg = lg_ref[...]
        ln = yh * lg + lb_ref[...]
        sg = _sigmoid(ln)
        dln = d_ref[0].astype(F32) * (sg * (1.0 + ln * (1.0 - sg)))
        dyh = dln * lg
        dy = rstd * (dyh - jnp.mean(dyh, axis=-1, keepdims=True)
                     - yh * jnp.mean(dyh * yh, axis=-1, keepdims=True))
        dy_ref[0] = dy

        @pl.when((pl.program_id(0) == 0) & (pl.program_id(1) == 0))
        def _():
            dw_ref[...] = jnp.zeros_like(dw_ref)
            dv_ref[...] = jnp.zeros_like(dv_ref)

        dv_ref[pl.ds(0, 1), :] += jnp.sum(dy, axis=0, keepdims=True)
        dv_ref[pl.ds(1, 1), :] += jnp.sum(dln * yh, axis=0, keepdims=True)
        dv_ref[pl.ds(2, 1), :] += jnp.sum(dln, axis=0, keepdims=True)
        for j in range(CONV_K):
            tap = _tap(ext, sh, HALO - (CONV_K - 1) + j, ts)
            dw_ref[pl.ds(j, 1), :] += jnp.sum(dy * tap, axis=0, keepdims=True)

    return _call(
        body, name=name, grid=(B, S // ts),
        in_specs=[cur, halo, tile, tile, vec, vec],
        out_specs=[tile,
                   pl.BlockSpec((HALO, CONV_CH), lambda b, s: (0, 0)),
                   pl.BlockSpec((8, CONV_CH), lambda b, s: (0, 0))],
        out_shape=[jax.ShapeDtypeStruct((B, S, CONV_CH), F32),
                   jax.ShapeDtypeStruct((HALO, CONV_CH), F32),
                   jax.ShapeDtypeStruct((8, CONV_CH), F32)],
        scratch_shapes=[pltpu.VMEM((ts + HALO, CONV_CH), F32),
                        pltpu.VMEM((SUBLANES - 1, ts + HALO, CONV_CH), F32)],
        compiler_params=_params(("arbitrary", "arbitrary")),
    )(ug, ug, y, dcat, ln_g, ln_b)


def conv_branch_bwd_b(ug, dy, conv_w, *, name, ts=256):
    B, S, _ = ug.shape
    ts = min(ts, S)
    nh, n_halo = ts // HALO, S // HALO

    def body(cur_ref, dy_ref, nxt_ref, w_ref, o_ref, ext, sh):
        last = pl.program_id(1) == pl.num_programs(1) - 1
        ext[pl.ds(0, ts), :] = dy_ref[0]
        ext[pl.ds(ts, HALO), :] = jnp.where(last, 0.0, nxt_ref[0])
        _make_shifted(ext, sh)
        da = jnp.zeros((ts, CONV_CH), F32)
        for j in range(CONV_K):
            da = da + _tap(ext, sh, CONV_K - 1 - j, ts) * w_ref[pl.ds(j, 1), :]
        blk = cur_ref[0]
        u = blk[:, :CONV_CH].astype(F32)
        sg = _sigmoid(blk[:, CONV_CH:].astype(F32))
        o_ref[0, :, :CONV_CH] = (da * sg).astype(BF16)
        o_ref[0, :, CONV_CH:] = (da * u * sg * (1.0 - sg)).astype(BF16)

    return _call(
        body, name=name, grid=(B, S // ts),
        in_specs=[pl.BlockSpec((1, ts, 2 * CONV_CH), lambda b, s: (b, s, 0)),
                  pl.BlockSpec((1, ts, CONV_CH), lambda b, s: (b, s, 0)),
                  pl.BlockSpec((1, HALO, CONV_CH), lambda b, s: (b, jnp.minimum((s + 1) * nh, n_halo - 1), 0)),
                  pl.BlockSpec((HALO, CONV_CH), lambda b, s: (0, 0))],
        out_specs=pl.BlockSpec((1, ts, 2 * CONV_CH), lambda b, s: (b, s, 0)),
        out_shape=jax.ShapeDtypeStruct((B, S, 2 * CONV_CH), BF16),
        scratch_shapes=[pltpu.VMEM((ts + HALO, CONV_CH), F32),
                        pltpu.VMEM((SUBLANES - 1, ts + HALO, CONV_CH), F32)],
        compiler_params=_params(("parallel", "parallel")),
    )(ug, dy, dy, conv_w)


def _tri(n, lower):
    r = lax.broadcasted_iota(jnp.int32, (n, n), 0)
    c = lax.broadcasted_iota(jnp.int32, (n, n), 1)
    return ((r >= c) if lower else (r <= c)).astype(F32)


def _dot_hi(a, b, dn):
    return lax.dot_general(a, b, dn, precision=lax.Precision.HIGHEST, preferred_element_type=F32)


NN = (((1,), (0,)), ((), ()))
NT = (((1,), (1,)), ((), ()))
TN = (((0,), (0,)), ((), ()))


def _log_sigmoid(v):
    e = jnp.exp(-jnp.abs(v))
    log1p_e = jnp.where(e < 1e-3, e * (1.0 - 0.5 * e), jnp.log(1.0 + e))
    return jnp.minimum(v, 0.0) - log1p_e


def fgate_fwd(h, w_f, b_f, *, name, ts=256, rider=None):
    B, S, D = h.shape
    ts = min(ts, S)

    def body(h_ref, w_ref, b_ref, f_ref, cc_ref, cr_ref, carry):
        @pl.when(pl.program_id(1) == 0)
        def _():
            carry[...] = jnp.zeros_like(carry)

        f = jnp.dot(h_ref[0], w_ref[...], preferred_element_type=F32)
        f_ref[0] = f
        logf = _log_sigmoid(f + b_ref[...])
        c = _dot_hi(_tri(ts, True), logf, NN) + carry[pl.ds(0, 1), :]
        cc_ref[0] = c
        carry[pl.ds(0, 1), :] = c[ts - 1:ts, :]
        cr_ref[0] = c.T

    return hosted_call(
        body, rider, name=name, grid=(B, S // ts),
        in_specs=[pl.BlockSpec((1, ts, D), lambda b, s: (b, s, 0)),
                  pl.BlockSpec((D, LANES), lambda b, s: (0, 0)),
                  pl.BlockSpec((1, LANES), lambda b, s: (0, 0))],
        out_specs=[pl.BlockSpec((1, ts, LANES), lambda b, s: (b, s, 0)),
                   pl.BlockSpec((1, ts, LANES), lambda b, s: (b, s, 0)),
                   pl.BlockSpec((1, LANES, ts), lambda b, s: (b, 0, s))],
        out_shape=[jax.ShapeDtypeStruct((B, S, LANES), F32), jax.ShapeDtypeStruct((B, S, LANES), F32),
                   jax.ShapeDtypeStruct((B, LANES, S), F32)],
        scratch_shapes=[pltpu.VMEM((8, LANES), F32)],
        args=(h, w_f, b_f),
    )


def fgate_bwd(dc, f, b_f, *, name, ts=256):
    B, S, _ = f.shape
    P = dc.shape[1]
    ts = min(ts, S)
    ns = S // ts

    def body(dc_ref, f_ref, b_ref, df_ref, db_ref, carry):
        @pl.when(pl.program_id(1) == 0)
        def _():
            carry[...] = jnp.zeros_like(carry)

        @pl.when((pl.program_id(0) == 0) & (pl.program_id(1) == 0))
        def _():
            db_ref[...] = jnp.zeros_like(db_ref)

        dc_t = dc_ref[0, 0]
        for j in range(1, P):
            dc_t = dc_t + dc_ref[0, j]
        dlogf = _dot_hi(_tri(ts, False), dc_t, NN) + carry[pl.ds(0, 1), :]
        carry[pl.ds(0, 1), :] = dlogf[0:1, :]
        df = dlogf * _sigmoid(-(f_ref[0] + b_ref[...]))
        df_ref[0] = df.astype(BF16)
        db_ref[...] += jnp.sum(df, axis=0, keepdims=True)

    return _call(
        body, name=name, grid=(B, ns),
        in_specs=[pl.BlockSpec((1, P, ts, LANES), lambda b, s: (b, 0, ns - 1 - s, 0)),
                  pl.BlockSpec((1, ts, LANES), lambda b, s: (b, ns - 1 - s, 0)),
                  pl.BlockSpec((1, LANES), lambda b, s: (0, 0))],
        out_specs=[pl.BlockSpec((1, ts, LANES), lambda b, s: (b, ns - 1 - s, 0)),
                   pl.BlockSpec((1, LANES), lambda b, s: (0, 0))],
        out_shape=[jax.ShapeDtypeStruct((B, S, LANES), BF16), jax.ShapeDtypeStruct((1, LANES), F32)],
        scratch_shapes=[pltpu.VMEM((8, LANES), F32)],
        compiler_params=_params(("arbitrary", "arbitrary")),
    )(dc, f, b_f)


def _lane_pick(tile, idx):
    lane = lax.broadcasted_iota(jnp.int32, tile.shape, 1)
    return jnp.sum(jnp.where(lane == idx, tile, 0.0), axis=-1, keepdims=True)


FOX_T = 512


def _fox_heads(q, cc_ref, p):
    lane = lax.broadcasted_iota(jnp.int32, q.shape, 1)
    qs = q * (1.0 / math.sqrt(FOX_HEAD_DIM))
    qhs = [jnp.where((lane < FOX_HEAD_DIM) == (hh == 0), qs, jnp.zeros_like(qs)) for hh in range(2)]
    crefs = [_lane_pick(cc_ref[0, pl.ds(0, 1), :], 2 * p + hh) for hh in range(2)]
    return qhs, crefs


def _fold_lanes(x, op):
    out = x[:, :LANES]
    for j in range(1, x.shape[1] // LANES):
        out = op(out, x[:, j * LANES:(j + 1) * LANES])
    return out


def _causal(t, transposed):
    r = lax.broadcasted_iota(jnp.int32, (t, t), 0)
    c = lax.broadcasted_iota(jnp.int32, (t, t), 1)
    return (r <= c) if transposed else (c <= r)


QKV0 = 8


def fox_fwd(z, c_col, c_row, *, name, rider=None):
    B, S, _ = z.shape
    assert S % FOX_T == 0
    tq, nq = FOX_T, S // FOX_T
    npair = FOX_HEADS // 2

    def body(q_ref, k_ref, v_ref, cc_ref, cr_ref, o_ref, l_ref, ot_ref, s_scr, m_scr, acc_scr):
        p, qi = pl.program_id(1), pl.program_id(2)
        qhs, crefs = _fox_heads(q_ref[0], cc_ref, p)
        lane = lax.broadcasted_iota(jnp.int32, (tq, LANES), 1)
        first = lane < FOX_HEAD_DIM
        for hh in range(2):
            m_scr[hh] = jnp.full((tq, LANES), NEG, F32)
            acc_scr[hh] = jnp.zeros((tq, LANES), F32)

        def logits(kb, diagonal):
            k0 = pl.multiple_of(kb * tq, tq)
            k = k_ref[0, pl.ds(k0, tq), :]
            for hh in range(2):
                s = lax.dot_general(qhs[hh], k, NT, preferred_element_type=F32)
                s = s + (crefs[hh] - cr_ref[0, pl.ds(2 * p + hh, 1), pl.ds(k0, tq)])
                if diagonal:
                    s = jnp.where(_causal(tq, False), s, NEG)
                s_scr[hh, kb] = s
                m_scr[hh] = jnp.maximum(m_scr[hh], _fold_lanes(s, jnp.maximum))

        def sweep1(kb, carry):
            logits(kb, False)
            return carry

        lax.fori_loop(0, qi, sweep1, 0)
        logits(qi, True)
        ms = [jnp.max(m_scr[hh], axis=-1, keepdims=True) for hh in range(2)]
        mbs = [jnp.broadcast_to(ms[hh], (tq, tq)) for hh in range(2)]

        for hh in range(2):
            m_scr[hh] = jnp.zeros((tq, LANES), F32)

        def weigh(kb, carry):
            k0 = pl.multiple_of(kb * tq, tq)
            v = v_ref[0, pl.ds(k0, tq), :]
            for hh in range(2):
                pr = jnp.exp(s_scr[hh, kb] - mbs[hh])
                m_scr[hh] += _fold_lanes(pr, jnp.add)
                acc_scr[hh] += jnp.dot(pr.astype(BF16), v, preferred_element_type=F32)
            return carry

        lax.fori_loop(0, qi + 1, weigh, 0)
        accs = [acc_scr[hh] for hh in range(2)]
        ls = [jnp.sum(m_scr[hh], axis=-1, keepdims=True) for hh in range(2)]
        out = jnp.where(first, accs[0] / ls[0], accs[1] / ls[1])
        o_ref[0] = out.astype(BF16)
        ot_ref[...] = out.T.astype(BF16)
        l_ref[0, 0] = jnp.where(first, ms[0] + jnp.log(ls[0]), ms[1] + jnp.log(ls[1]))

    return hosted_call(
        body, rider, name=name, grid=(B, npair, nq),
        in_specs=[pl.BlockSpec((1, tq, LANES), lambda b, p, i: (b, i, QKV0 + p)),
                  pl.BlockSpec((1, S, LANES), lambda b, p, i: (b, 0, QKV0 + npair + p)),
                  pl.BlockSpec((1, S, LANES), lambda b, p, i: (b, 0, QKV0 + 2 * npair + p)),
                  pl.BlockSpec((1, tq, LANES), lambda b, p, i: (b, i, 0)),
                  pl.BlockSpec((1, 8, S), lambda b, p, i: (b, 0, 0))],
        out_specs=[pl.BlockSpec((1, tq, LANES), lambda b, p, i: (b, i, p)),
                   pl.BlockSpec((1, 1, tq, LANES), lambda b, p, i: (b, p, i, 0)),
                   pl.BlockSpec((LANES, tq), lambda b, p, i: (p, b * nq + i))],
        out_shape=[jax.ShapeDtypeStruct((B, S, FOX_W), BF16),
                   jax.ShapeDtypeStruct((B, npair, S, LANES), F32),
                   jax.ShapeDtypeStruct((FOX_W, B * S), BF16)],
        scratch_shapes=[pltpu.VMEM((2, nq, tq, tq), F32), pltpu.VMEM((2, tq, LANES), F32),
                        pltpu.VMEM((2, tq, LANES), F32)],
        args=(z, z, z, c_col, c_row),
    )


def fox_bwd_dq(z, dcat, lse, c_col, c_row, *, name, rider=None):
    B, S, _ = z.shape
    tq, nq = FOX_T, S // FOX_T
    npair = FOX_HEADS // 2

    def body(q_ref, k_ref, v_ref, do_ref, l_ref, cc_ref, cr_ref, dq_ref, st_ref, p_scr, dp_scr, dl_scr):
        p, qi = pl.program_id(1), pl.program_id(2)
        qhs, crefs = _fox_heads(q_ref[0], cc_ref, p)
        lane = lax.broadcasted_iota(jnp.int32, (tq, LANES), 1)
        do_b = do_ref[0].astype(BF16)
        dohs = [jnp.where((lane < FOX_HEAD_DIM) == (hh == 0), do_b, jnp.zeros_like(do_b)) for hh in range(2)]
        lses = [_lane_pick(l_ref[0, 0], hh * FOX_HEAD_DIM) for hh in range(2)]
        lbs = [jnp.broadcast_to(lses[hh], (tq, tq)) for hh in range(2)]
        for hh in range(2):
            dl_scr[hh] = jnp.zeros((tq, LANES), F32)

        def probs(kb, diagonal):
            k0 = pl.multiple_of(kb * tq, tq)
            k = k_ref[0, pl.ds(k0, tq), :]
            v = v_ref[0, pl.ds(k0, tq), :]
            for hh in range(2):
                s = lax.dot_general(qhs[hh], k, NT, preferred_element_type=F32)
                s = s + (crefs[hh] - cr_ref[0, pl.ds(2 * p + hh, 1), pl.ds(k0, tq)])
                pr = jnp.exp(s - lbs[hh])
                if diagonal:
                    pr = jnp.where(_causal(tq, False), pr, 0.0)
                dp = lax.dot_general(dohs[hh], v, NT, preferred_element_type=F32)
                pdp = pr * dp
                dl_scr[hh] += _fold_lanes(pdp, jnp.add)
                p_scr[hh, kb] = pr
                dp_scr[hh, kb] = dp

        def first_pass(kb, carry):
            probs(kb, False)
            return carry

        lax.fori_loop(0, qi, first_pass, 0)
        probs(qi, True)

        dls = [jnp.sum(dl_scr[hh], axis=-1, keepdims=True) for hh in range(2)]
        dlbs = [jnp.broadcast_to(dls[hh], (tq, tq)) for hh in range(2)]

        def second_pass(kb, dq):
            k0 = pl.multiple_of(kb * tq, tq)
            k = k_ref[0, pl.ds(k0, tq), :]
            for hh in range(2):
                ds = p_scr[hh, kb] * (dp_scr[hh, kb] - dlbs[hh])
                kh = jnp.where((lane < FOX_HEAD_DIM) == (hh == 0), k, jnp.zeros_like(k))
                dq = dq + jnp.dot(ds.astype(BF16), kh, preferred_element_type=F32)
            return dq

        dq = lax.fori_loop(0, qi + 1, second_pass, jnp.zeros((tq, LANES), F32))
        dq_ref[0] = (dq * (1.0 / math.sqrt(FOX_HEAD_DIM))).astype(BF16)
        cols = jnp.zeros((tq, LANES), F32)
        for j, col in enumerate([crefs[0] - lses[0], crefs[1] - lses[1], dls[0], dls[1]]):
            cols = jnp.where(lane == j, col, cols)
        st_ref[0, 0] = cols.T[:8]

    return hosted_call(
        body, rider, name=name, grid=(B, npair, nq),
        in_specs=[pl.BlockSpec((1, tq, LANES), lambda b, p, i: (b, i, QKV0 + p)),
                  pl.BlockSpec((1, S, LANES), lambda b, p, i: (b, 0, QKV0 + npair + p)),
                  pl.BlockSpec((1, S, LANES), lambda b, p, i: (b, 0, QKV0 + 2 * npair + p)),
                  pl.BlockSpec((1, tq, LANES), lambda b, p, i: (b, i, npair + p)),
                  pl.BlockSpec((1, 1, tq, LANES), lambda b, p, i: (b, p, i, 0)),
                  pl.BlockSpec((1, tq, LANES), lambda b, p, i: (b, i, 0)),
                  pl.BlockSpec((1, 8, S), lambda b, p, i: (b, 0, 0))],
        out_specs=[pl.BlockSpec((1, tq, LANES), lambda b, p, i: (b, i, p)),
                   pl.BlockSpec((1, 1, 8, tq), lambda b, p, i: (b, p, 0, i))],
        out_shape=[jax.ShapeDtypeStruct((B, S, FOX_W), BF16), jax.ShapeDtypeStruct((B, npair, 8, S), F32)],
        scratch_shapes=[pltpu.VMEM((2, nq, tq, tq), F32), pltpu.VMEM((2, nq, tq, tq), F32),
                        pltpu.VMEM((2, tq, LANES), F32)],
        args=(z, z, z, dcat, lse, c_col, c_row), vmem=56 << 20,
    )


def fox_bwd_dkdv(z, dcat, stats, c_col, *, name, rider=None):
    B, S, _ = z.shape
    tk, nq = FOX_T, S // FOX_T
    npair = FOX_HEADS // 2
    inv = 1.0 / math.sqrt(FOX_HEAD_DIM)

    def body(q_ref, k_ref, v_ref, do_ref, st_ref, cc_ref, dk_ref, dv_ref, dc_ref, dk_scr, dv_scr, dc_scr):
        p, kt = pl.program_id(1), pl.program_id(2)
        lane = lax.broadcasted_iota(jnp.int32, (tk, LANES), 1)
        masks = [(lane < FOX_HEAD_DIM) == (hh == 0) for hh in range(2)]
        k = k_ref[0]
        v = v_ref[0]
        khs = [jnp.where(masks[hh], k, jnp.zeros_like(k)) for hh in range(2)]
        vhs = [jnp.where(masks[hh], v, jnp.zeros_like(v)) for hh in range(2)]
        ccbs = [jnp.broadcast_to(_lane_pick(cc_ref[0], 2 * p + hh), (tk, tk)) for hh in range(2)]
        dk_scr[...] = jnp.zeros_like(dk_scr)
        dv_scr[...] = jnp.zeros_like(dv_scr)
        dc_scr[...] = jnp.zeros_like(dc_scr)

        def tile(qb, diagonal):
            q0 = pl.multiple_of(qb * tk, tk)
            qs = q_ref[0, pl.ds(q0, tk), :] * inv
            do_b = do_ref[0, pl.ds(q0, tk), :].astype(BF16)
            for hh in range(2):
                st = lax.dot_general(khs[hh], qs, NT, preferred_element_type=F32)
                pr = jnp.exp(st - ccbs[hh] + st_ref[0, 0, pl.ds(hh, 1), pl.ds(q0, tk)])
                if diagonal:
                    pr = jnp.where(_causal(tk, True), pr, 0.0)
                dp = lax.dot_general(vhs[hh], do_b, NT, preferred_element_type=F32)
                ds = pr * (dp - st_ref[0, 0, pl.ds(2 + hh, 1), pl.ds(q0, tk)])
                dv_scr[...] += jnp.dot(pr.astype(BF16), jnp.where(masks[hh], do_b, jnp.zeros_like(do_b)),
                                       preferred_element_type=F32)
                dk_scr[...] += jnp.dot(ds.astype(BF16), jnp.where(masks[hh], qs, jnp.zeros_like(qs)),
                                       preferred_element_type=F32)
                dc_scr[hh] -= _fold_lanes(ds, jnp.add)

        def later(qb, carry):
            tile(qb, False)
            return carry

        tile(kt, True)
        lax.fori_loop(kt + 1, nq, later, 0)
        dk_ref[0] = dk_scr[...].astype(BF16)
        dv_ref[0] = dv_scr[...].astype(BF16)
        dcs = [jnp.sum(dc_scr[hh], axis=-1, keepdims=True) for hh in range(2)]
        dc_ref[0, 0] = jnp.where(lane == 2 * p, dcs[0], jnp.where(lane == 2 * p + 1, dcs[1], 0.0))

    full = lambda col: pl.BlockSpec((1, S, LANES), col)
    tile_spec = lambda col: pl.BlockSpec((1, tk, LANES), col)
    return hosted_call(
        body, rider, name=name, grid=(B, npair, nq),
        in_specs=[full(lambda b, p, t: (b, 0, QKV0 + p)),
                  tile_spec(lambda b, p, t: (b, t, QKV0 + npair + p)),
                  tile_spec(lambda b, p, t: (b, t, QKV0 + 2 * npair + p)),
                  full(lambda b, p, t: (b, 0, npair + p)),
                  pl.BlockSpec((1, 1, 8, S), lambda b, p, t: (b, p, 0, 0)),
                  tile_spec(lambda b, p, t: (b, t, 0))],
        out_specs=[tile_spec(lambda b, p, t: (b, t, p)), tile_spec(lambda b, p, t: (b, t, p)),
                   pl.BlockSpec((1, 1, tk, LANES), lambda b, p, t: (b, p, t, 0))],
        out_shape=[jax.ShapeDtypeStruct((B, S, FOX_W), BF16)] * 2
        + [jax.ShapeDtypeStruct((B, npair, S, LANES), F32)],
        scratch_shapes=[pltpu.VMEM((tk, LANES), F32), pltpu.VMEM((tk, LANES), F32),
                        pltpu.VMEM((2, tk, LANES), F32)],
        args=(z, z, z, dcat, stats, c_col),
    )


def xattn_fwd(qm, kv, *, name, tq=512):
    B, S, D = qm.shape
    M = kv.shape[1]
    tq = min(tq, S)
    inv = 1.0 / math.sqrt(MEM_HEAD_DIM)

    nq = S // tq

    def body(q_ref, kv_ref, o_ref, ot_ref):
        for h in range(MEM_HEADS):
            c0 = h * MEM_HEAD_DIM
            qh = q_ref[0, :, c0:c0 + MEM_HEAD_DIM]
            kh = kv_ref[0, :, c0:c0 + MEM_HEAD_DIM]
            vh = kv_ref[0, :, D + c0:D + c0 + MEM_HEAD_DIM]
            s = lax.dot_general(qh, kh, NT, preferred_element_type=F32) * inv
            e = jnp.exp(s - jnp.max(s, axis=-1, keepdims=True))
            o = jnp.dot(e.astype(BF16), vh, preferred_element_type=F32) / jnp.sum(e, axis=-1, keepdims=True)
            o_ref[0, :, c0:c0 + MEM_HEAD_DIM] = o.astype(BF16)
            ot_ref[c0:c0 + MEM_HEAD_DIM, :] = o.T.astype(BF16)

    return _call(
        body, name=name, grid=(B, nq),
        in_specs=[pl.BlockSpec((1, tq, D), lambda b, i: (b, i, 0)),
                  pl.BlockSpec((1, M, 2 * D), lambda b, i: (b, 0, 0))],
        out_specs=[pl.BlockSpec((1, tq, D), lambda b, i: (b, i, 0)),
                   pl.BlockSpec((D, tq), lambda b, i: (0, b * nq + i))],
        out_shape=[jax.ShapeDtypeStruct((B, S, D), BF16), jax.ShapeDtypeStruct((D, B * S), BF16)],
        compiler_params=_params(("parallel", "parallel")),
    )(qm, kv)


def xattn_bwd(qm, kv, do, *, name, tq=512):
    B, S, D = qm.shape
    M = kv.shape[1]
    tq = min(tq, S)
    inv = 1.0 / math.sqrt(MEM_HEAD_DIM)

    def body(q_ref, kv_ref, do_ref, dq_ref, dkv_ref):
        @pl.when(pl.program_id(1) == 0)
        def _():
            dkv_ref[...] = jnp.zeros_like(dkv_ref)

        for h in range(MEM_HEADS):
            c0 = h * MEM_HEAD_DIM
            qh = q_ref[0, :, c0:c0 + MEM_HEAD_DIM]
            kh = kv_ref[0, :, c0:c0 + MEM_HEAD_DIM]
            vh = kv_ref[0, :, D + c0:D + c0 + MEM_HEAD_DIM]
            doh = do_ref[0, :, c0:c0 + MEM_HEAD_DIM]
            s = lax.dot_general(qh, kh, NT, preferred_element_type=F32) * inv
            e = jnp.exp(s - jnp.max(s, axis=-1, keepdims=True))
            pr = e / jnp.sum(e, axis=-1, keepdims=True)
            dp = lax.dot_general(doh, vh, NT, preferred_element_type=F32)
            ds = pr * (dp - jnp.sum(pr * dp, axis=-1, keepdims=True))
            ds_b = ds.astype(BF16)
            dq_ref[0, :, c0:c0 + MEM_HEAD_DIM] = (jnp.dot(ds_b, kh, preferred_element_type=F32) * inv).astype(BF16)
            dkv_ref[0, :, c0:c0 + MEM_HEAD_DIM] += lax.dot_general(ds_b, qh, TN, preferred_element_type=F32) * inv
            dkv_ref[0, :, D + c0:D + c0 + MEM_HEAD_DIM] += lax.dot_general(
                pr.astype(BF16), doh, TN, preferred_element_type=F32)

    row = pl.BlockSpec((1, tq, D), lambda b, i: (b, i, 0))
    kvs = pl.BlockSpec((1, M, 2 * D), lambda b, i: (b, 0, 0))
    return _call(
        body, name=name, grid=(B, S // tq), in_specs=[row, kvs, row], out_specs=[row, kvs],
        out_shape=[jax.ShapeDtypeStruct((B, S, D), BF16), jax.ShapeDtypeStruct((B, M, 2 * D), F32)],
        compiler_params=_params(("parallel", "arbitrary")),
    )(qm, kv, do)


SWIGLU_TN = 2816


def _chunks(n, w=256):
    return [(c0, min(w, n - c0)) for c0 in range(0, n, w)]


def mm_swiglu_fwd(hf, w_gu, *, name, tm=256):
    T, D = hf.shape
    Fh = w_gu.shape[1] // 2
    tm, tn = min(tm, T), SWIGLU_TN
    nj = Fh // tn
    assert Fh % tn == 0 and T % tm == 0

    def body(a_ref, bg_ref, bu_ref, g_ref, u_ref, o_ref, ot_ref):
        a = a_ref[...]
        for c0, cw in _chunks(tn):
            cols = pl.ds(c0, cw)
            g = jnp.dot(a, bg_ref[:, cols], preferred_element_type=F32)
            u = jnp.dot(a, bu_ref[:, cols], preferred_element_type=F32)
            act = g * _sigmoid(g) * u
            g_ref[:, cols] = g.astype(BF16)
            u_ref[:, cols] = u.astype(BF16)
            o_ref[:, cols] = act.astype(BF16)
            ot_ref[cols, :] = act.T.astype(BF16)

    tile = pl.BlockSpec((tm, tn), lambda i, j: (i, j))
    return _call(
        body, name=name, grid=(T // tm, nj),
        in_specs=[pl.BlockSpec((tm, D), lambda i, j: (i, 0)), pl.BlockSpec((D, tn), lambda i, j: (0, j)),
                  pl.BlockSpec((D, tn), lambda i, j: (0, nj + j))],
        out_specs=[tile, tile, tile, pl.BlockSpec((tn, tm), lambda i, j: (j, i))],
        out_shape=[jax.ShapeDtypeStruct((T, Fh), BF16)] * 3 + [jax.ShapeDtypeStruct((Fh, T), BF16)],
        compiler_params=_params(("parallel", "parallel"), 48 << 20),
    )(hf, w_gu, w_gu)


def mm_swiglu_bwd(dx, w_down, g, u, *, name, tm=256):
    T, D = dx.shape
    Fh = w_down.shape[0]
    tm, tn = min(tm, T), SWIGLU_TN
    assert Fh % tn == 0 and T % tm == 0

    def body(a_ref, b_ref, g_ref, u_ref, dg_ref, du_ref):
        a = a_ref[...].astype(BF16)
        for c0, cw in _chunks(tn):
            cols = pl.ds(c0, cw)
            d = lax.dot_general(a, b_ref[cols, :], NT, preferred_element_type=F32)
            gv = g_ref[:, cols].astype(F32)
            uv = u_ref[:, cols].astype(F32)
            sg = _sigmoid(gv)
            dg_ref[:, cols] = (d * uv * (sg * (1.0 + gv * (1.0 - sg)))).astype(BF16)
            du_ref[:, cols] = (d * gv * sg).astype(BF16)

    tile = pl.BlockSpec((tm, tn), lambda i, j: (i, j))
    return _call(
        body, name=name, grid=(T // tm, Fh // tn),
        in_specs=[pl.BlockSpec((tm, D), lambda i, j: (i, 0)), pl.BlockSpec((tn, D), lambda i, j: (j, 0)), tile, tile],
        out_specs=[tile, tile],
        out_shape=[jax.ShapeDtypeStruct((T, Fh), BF16)] * 2,
        compiler_params=_params(("parallel", "parallel"), 48 << 20),
    )(dx, w_down, g, u)


LATE_MID = ("w_out", "w_mq", "w_mo")
LATE_KV = ("w_mkv",)
LATE_FFN = ("w_gu", "w_down")
LATE = LATE_MID + LATE_KV + LATE_FFN
RS_GROUPS = (("w_gu", "w_down"), ("w_out", "w_mq", "w_mkv", "w_mo"), ("w_in",))


def pair_sums(names, g42, got):
    return {n: pair_sum(g, o, name="rs_pair_sum_" + n) for n, g, o in zip(names, g42, got)}


def local_step(x, mem, target, sp, first_shards, late_shards):
    B, S, D = x.shape
    T = B * S
    M = mem.shape[1]
    row = lambda v: v.reshape(1, -1).astype(F32)
    g_mix, g_x, g_mem, g_ffn, g_final = (row(sp[k]) for k in ("g_mix", "g_x", "g_mem", "g_ffn", "g_final"))
    conv_b, ln_g, ln_b = row(sp["conv_b"]), row(sp["ln_g"]), row(sp["ln_b"])
    b_f = jnp.pad(row(sp["b_f"]), ((0, 0), (0, LANES - FOX_HEADS)))
    n_ug, n_main = 2 * CONV_CH, 2 * CONV_CH + 3 * FOX_W

    x2d = x.reshape(T, D)
    h, h_t, partly = rmsnorm_fwd(x2d, g_mix, name="rms_mix", rider=AllGatherStage1(first_shards))
    w_in8, cw8 = run_rider(AllGatherStage2(partly), name="ag_first_stage2")
    w_in_full = _full_from_gathered("w_in", w_in8)
    conv_w = cw8.transpose(1, 0, 2).reshape(HALO, -1)
    w_main, w_ug, w_qkv = w_in_full[:, :n_main], w_in_full[:, :n_ug], w_in_full[:, n_ug:n_main]
    w_f = jnp.pad(w_in_full[:, n_main:], ((0, 0), (0, LANES - FOX_HEADS)))
    z = matmul(h, w_main, out_dtype=BF16, tn=n_main, name="mm_in")
    z3 = z.reshape(B, S, n_main)
    n_mid, n_kv = len(LATE_MID), len(LATE_MID) + len(LATE_KV)
    (conv_out, conv_t, conv_y), partly_mid = conv_branch_fwd(z3, conv_w, conv_b, ln_g, ln_b, name="conv_fwd",
                                                     rider=AllGatherStage1(late_shards[:n_mid]))
    (f_raw, c_col, c_row), rode = fgate_fwd(
        h.reshape(B, S, D), w_f, b_f, name="fgate_fwd",
        rider=Riders(AllGatherStage1(late_shards[n_mid:n_kv]), AllGatherStage2(partly_mid)))
    partly_kv, full_mid = rode[:n_kv - n_mid], rode[n_kv - n_mid:]
    (att, lse, att_t), rode = fox_fwd(
        z3, c_col, c_row, name="fox_fwd",
        rider=Riders(AllGatherStage1(late_shards[n_kv:]), AllGatherStage2(partly_kv)))
    partly_ffn, full_kv = rode[:len(LATE_FFN)], rode[len(LATE_FFN):]
    wf = {n: _full_from_gathered(n, blk) for n, blk in zip(LATE_MID + LATE_KV, full_mid + full_kv)}
    (x1, hx, hx_t), full_ffn = matmul(
        [conv_out.reshape(T, CONV_CH), att.reshape(T, FOX_W)], [wf["w_out"], wf["w_out"]], b_blk=[0, 1],
        res=x2d, tn=D, name="mm_out", post=rms_fwd_epilogue(g_x), rider=AllGatherStage2(partly_ffn))
    wf.update({n: _full_from_gathered(n, blk) for n, blk in zip(LATE_FFN, full_ffn)})
    qm = matmul(hx, wf["w_mq"], out_dtype=BF16, tn=D, name="mm_mq")
    mem2d = mem.reshape(B * M, D)
    mem_n, mem_n_t = rmsnorm_fwd(mem2d, g_mem, name="rms_mem")
    kv = matmul(mem_n, wf["w_mkv"], out_dtype=BF16, tn=2 * D, name="mm_mkv").reshape(B, M, 2 * D)
    o, o_t = xattn_fwd(qm.reshape(B, S, D), kv, name="xattn_fwd")
    o = o.reshape(T, D)
    x2, hf, hf_t = matmul(o, wf["w_mo"], res=x1, tn=D, name="mm_mo", post=rms_fwd_epilogue(g_ffn))
    gate, up, act, act_t = mm_swiglu_fwd(hf, wf["w_gu"], name="mm_gu")
    dx3, dg_final, loss = matmul(act, wf["w_down"], res=x2, tn=D, name="mm_down",
                                 post=loss_epilogue(g_final, target.reshape(T, D)))
    gw = {}
    gw["w_down"] = matmul(act_t, dx3, out_dtype=BF16, tm=1408, tn=512, name="dw_down")
    dgate, dup = mm_swiglu_bwd(dx3, wf["w_down"], gate, up, name="dx_down")
    gw["w_gu"] = [matmul(hf_t, dgate, out_dtype=BF16, tn=1408, name="dw_gate"),
                  matmul(hf_t, dup, out_dtype=BF16, tn=1408, name="dw_up")]
    g42 = [_shards_from_full(n, gw[n]) for n in RS_GROUPS[0]]
    (dx2, dg_ffn), got = matmul([dgate, dup], [wf["w_gu"], wf["w_gu"]], b_blk=[0, 1], tb=True, tm=256, tn=D,
                                name="dx_gu", post=rms_bwd_epilogue(x2, g_ffn, dx3), rider=SiblingExchange(g42))
    parts = pair_sums(RS_GROUPS[0], g42, got)
    gw["w_mo"] = matmul(o_t, dx2, out_dtype=BF16, tn=D, name="dw_mo")
    do = matmul(dx2, wf["w_mo"], tb=True, out_dtype=BF16, tn=D, name="dx_mo")
    dqm, dkv = xattn_bwd(qm.reshape(B, S, D), kv, do.reshape(B, S, D), name="xattn_bwd")
    dqm = dqm.reshape(T, D)
    dkv = dkv.reshape(B * M, 2 * D)
    gw["w_mq"] = matmul(hx_t, dqm, out_dtype=BF16, tn=D, name="dw_mq")
    dx1, dg_x = matmul(dqm, wf["w_mq"], tb=True, tn=D, name="dx_mq", post=rms_bwd_epilogue(x1, g_x, dx2))
    gw["w_mkv"] = matmul(mem_n_t, dkv, out_dtype=BF16, tn=D, name="dw_mkv")
    _, dg_mem = matmul(dkv, wf["w_mkv"], tb=True, tn=D, name="dx_mkv", post=rms_bwd_epilogue(mem2d, g_mem, None))
    gw["w_out"] = jnp.concatenate([matmul(conv_t, dx1, out_dtype=BF16, tn=D, name="dw_out_conv"),
                                   matmul(att_t, dx1, out_dtype=BF16, tn=D, name="dw_out_att")], axis=0)
    g42 = [_shards_from_full(n, gw[n]) for n in RS_GROUPS[1]]
    dcat, got = matmul(dx1, wf["w_out"], tb=True, out_dtype=BF16, tn=D, name="dx_out", rider=SiblingExchange(g42))
    dcat = dcat.reshape(B, S, D)
    parts.update(pair_sums(RS_GROUPS[1], g42, got))
    dy, dconv_w, dvec = conv_branch_bwd_a(z3, conv_y, dcat, ln_g, ln_b, name="conv_bwd_a")
    dug = conv_branch_bwd_b(z3, dy, conv_w, name="conv_bwd_b")
    gots = {}
    (dq, stats), got = fox_bwd_dq(z3, dcat, lse, c_col, c_row, name="fox_bwd_dq",
                                  rider=ChipExchange([parts[n] for n in RS_GROUPS[0]]))
    gots.update(zip(RS_GROUPS[0], got))
    (dk, dv, dc), got = fox_bwd_dkdv(z3, dcat, stats, c_col, name="fox_bwd_dkdv",
                                     rider=ChipExchange([parts[n] for n in RS_GROUPS[1]]))
    gots.update(zip(RS_GROUPS[1], got))
    df, db_f = fgate_bwd(dc, f_raw, b_f, name="fgate_bwd")
    dug2 = dug.reshape(T, n_ug)
    dqkv = jnp.concatenate([dq, dk, dv], axis=-1).reshape(T, 3 * FOX_W)
    df2 = df.reshape(T, LANES)
    dw_in = [matmul(h_t, dug2, out_dtype=BF16, tn=n_ug, name="dw_in_ug"),
             matmul(h_t, dqkv, out_dtype=BF16, tn=3 * FOX_W, name="dw_in_qkv"),
             matmul(h_t, df2, out_dtype=BF16, name="dw_f")[:, :FOX_HEADS]]
    g42 = [_shards_from_full("w_in", dw_in)]
    parts.update(pair_sums(RS_GROUPS[2], g42, run_rider(SiblingExchange(g42), name="rs_sibling_in")))
    (dx, dg_mix), (gots["w_in"],) = matmul(
        [dug2, dqkv, df2], [w_ug, w_qkv, w_f], tb=True, tn=D, name="dx_in",
        post=rms_bwd_epilogue(x2d, g_mix, dx1, out_dtype=F32), rider=ChipExchange([parts["w_in"]]))
    gs = dict(g_mix=dg_mix, b_f=db_f[:, :FOX_HEADS], conv_w=dconv_w[:CONV_K], conv_b=dvec[0:1],
              ln_g=dvec[1:2], ln_b=dvec[2:3], g_x=dg_x, g_mem=dg_mem, g_ffn=dg_ffn, g_final=dg_final)
    return loss, dx.reshape(B, S, D), gs, {n: (parts[n], gots[n]) for n in BIG}


def _me():
    return lax.axis_index("x"), lax.axis_index("y"), lax.axis_index("c")


def _any_specs(n):
    return [pl.BlockSpec(memory_space=pl.ANY)] * n


def all_gather(xs, *, name):
    n = len(xs)

    def body(*refs):
        x_refs, out_refs = refs[:n], refs[n:2 * n]
        send_sems, recv_sems, local_sems = refs[2 * n:]
        x, y, c = _me()
        me, sibling = (x, y, c), (x, y, 1 - c)
        chips = [(1 - x, y), (x, 1 - y), (1 - x, 1 - y)]

        def slot(a, px, py, pc):
            return out_refs[a].at[4 * px + 2 * py + pc]

        def copy(a, k, block, to, own=False):
            return pltpu.make_async_remote_copy(
                src_ref=x_refs[a] if own else slot(a, *block), dst_ref=slot(a, *block),
                send_sem=send_sems.at[k, a], recv_sem=recv_sems.at[k, a], device_id=to, device_id_type=MESH)

        mine = [pltpu.make_async_copy(x_refs[a], slot(a, *me), local_sems.at[a]) for a in range(n)]
        first = [copy(a, 0, me, sibling, own=True) for a in range(n)]
        first += [copy(a, 1 + j, me, (*chip, c), own=True) for j, chip in enumerate(chips) for a in range(n)]
        for cp in mine + first:
            cp.start()
        passed = []
        for j, chip in enumerate(chips):
            for a in range(n):
                copy(a, 1 + j, (*chip, c), me).wait_recv()
                passed.append(copy(a, 4 + j, (*chip, c), sibling))
                passed[-1].start()
        for a in range(n):
            copy(a, 0, sibling, me).wait_recv()
            for j, chip in enumerate(chips):
                copy(a, 4 + j, (*chip, 1 - c), me).wait_recv()
        for cp in first + passed:
            cp.wait_send()
        for cp in mine:
            cp.wait()

    return _call(
        body, name=name, in_specs=_any_specs(n), out_specs=_any_specs(n),
        out_shape=[jax.ShapeDtypeStruct((N_DEV,) + v.shape, v.dtype) for v in xs],
        scratch_shapes=[pltpu.SemaphoreType.DMA((7, n)), pltpu.SemaphoreType.DMA((7, n)),
                        pltpu.SemaphoreType.DMA((n,))],
    )(*xs)


SIBLING_BARRIER = 1
CHIPS_BARRIER = 2
GATHER_BARRIER = 3


class SiblingExchange:
    collective_id = SIBLING_BARRIER

    def __init__(self, gs):
        n = len(gs)
        self.n, self.inputs = n, list(gs)
        self.out_shape = [jax.ShapeDtypeStruct((4,) + g.shape[2:], g.dtype) for g in gs]
        self.scratch = [pltpu.SemaphoreType.DMA((n,)), pltpu.SemaphoreType.DMA((n,))]

    @staticmethod
    def barrier_peers():
        x, y, c = _me()
        return [(x, y, 1 - c)]

    def _copies(self, g_refs, out_refs, sems):
        send_sems, recv_sems = sems
        x, y, c = _me()
        return [pltpu.make_async_remote_copy(
            src_ref=g_refs[a].at[:, 1 - c], dst_ref=out_refs[a], send_sem=send_sems.at[a],
            recv_sem=recv_sems.at[a], device_id=(x, y, 1 - c), device_id_type=MESH) for a in range(self.n)]

    def start(self, in_refs, out_refs, sems):
        for cp in self._copies(in_refs, out_refs, sems):
            cp.start()

    def finish(self, in_refs, out_refs, sems):
        for cp in self._copies(in_refs, out_refs, sems):
            cp.wait()


def run_rider(rider, *, name):
    return hosted_call(None, rider, name=name, grid=(), in_specs=[], out_specs=[], out_shape=[],
                       scratch_shapes=[], args=[])[1]


class ChipExchange:
    collective_id = CHIPS_BARRIER

    @staticmethod
    def barrier_peers():
        x, y, c = _me()
        return [(1 - x, y, c), (x, 1 - y, c), (1 - x, 1 - y, c)]

    def __init__(self, ps):
        n = len(ps)
        self.n, self.inputs = n, list(ps)
        self.out_shape = [jax.ShapeDtypeStruct(p.shape, p.dtype) for p in ps]
        self.scratch = [pltpu.SemaphoreType.DMA((3, n)), pltpu.SemaphoreType.DMA((3, n))]

    def _copies(self, p_refs, out_refs, sems, outgoing):
        send_sems, recv_sems = sems
        x, y, c = _me()
        my_chip = 2 * x + y
        cps = []
        for k in range(3):
            px, py = x ^ ((k + 1) >> 1), y ^ ((k + 1) & 1)
            src, dst = (2 * px + py, my_chip) if outgoing else (my_chip, 2 * px + py)
            for a in range(self.n):
                cps.append(pltpu.make_async_remote_copy(
                    src_ref=p_refs[a].at[src], dst_ref=out_refs[a].at[dst], send_sem=send_sems.at[k, a],
                    recv_sem=recv_sems.at[k, a], device_id=(px, py, c), device_id_type=MESH))
        return cps

    def start(self, in_refs, out_refs, sems):
        for cp in self._copies(in_refs, out_refs, sems, True):
            cp.start()

    def finish(self, in_refs, out_refs, sems):
        for cp in self._copies(in_refs, out_refs, sems, False):
            cp.wait_recv()
        for cp in self._copies(in_refs, out_refs, sems, True):
            cp.wait_send()


class AllGatherStage1:
    collective_id = GATHER_BARRIER

    @staticmethod
    def barrier_peers():
        x, y, c = _me()
        return [(x, y, 1 - c), (1 - x, y, c), (x, 1 - y, c), (1 - x, 1 - y, c)]

    def __init__(self, xs):
        n = len(xs)
        self.n, self.inputs = n, list(xs)
        self.out_shape = [jax.ShapeDtypeStruct((N_DEV,) + v.shape, v.dtype) for v in xs]
        self.scratch = [pltpu.SemaphoreType.DMA((4, n)), pltpu.SemaphoreType.DMA((4, n)),
                        pltpu.SemaphoreType.DMA((n,))]

    def _copies(self, x_refs, out_refs, sems, kind):
        send_sems, recv_sems, local_sems = sems
        x, y, c = _me()
        slot = lambda a, d: out_refs[a].at[4 * d[0] + 2 * d[1] + d[2]]
        if kind == "local":
            return [pltpu.make_async_copy(x_refs[a], slot(a, (x, y, c)), local_sems.at[a]) for a in range(self.n)]
        cps = []
        for k, peer in enumerate([(x, y, 1 - c), (1 - x, y, c), (x, 1 - y, c), (1 - x, 1 - y, c)]):
            for a in range(self.n):
                cps.append(pltpu.make_async_remote_copy(
                    src_ref=x_refs[a], dst_ref=slot(a, (x, y, c) if kind == "out" else peer),
                    send_sem=send_sems.at[k, a], recv_sem=recv_sems.at[k, a], device_id=peer, device_id_type=MESH))
        return cps

    def start(self, in_refs, out_refs, sems):
        for cp in self._copies(in_refs, out_refs, sems, "local") + self._copies(in_refs, out_refs, sems, "out"):
            cp.start()

    def finish(self, in_refs, out_refs, sems):
        for cp in self._copies(in_refs, out_refs, sems, "in"):
            cp.wait_recv()
        for cp in self._copies(in_refs, out_refs, sems, "out"):
            cp.wait_send()
        for cp in self._copies(in_refs, out_refs, sems, "local"):
            cp.wait()


class AllGatherStage2:
    collective_id = SIBLING_BARRIER

    @staticmethod
    def barrier_peers():
        x, y, c = _me()
        return [(x, y, 1 - c)]

    def __init__(self, outs):
        n = len(outs)
        self.n, self.inputs = n, list(outs)
        self.out_shape = [jax.ShapeDtypeStruct(o.shape, o.dtype) for o in outs]
        self.scratch = [pltpu.SemaphoreType.DMA((3, n)), pltpu.SemaphoreType.DMA((3, n))]
        self.aliases = {a: a for a in range(n)}

    def _copies(self, out_refs, sems, outgoing):
        send_sems, recv_sems = sems
        x, y, c = _me()
        cps = []
        for k, (px, py) in enumerate([(1 - x, y), (x, 1 - y), (1 - x, 1 - y)]):
            for a in range(self.n):
                cps.append(pltpu.make_async_remote_copy(
                    src_ref=out_refs[a].at[4 * px + 2 * py + c],
                    dst_ref=out_refs[a].at[4 * px + 2 * py + (c if outgoing else 1 - c)],
                    send_sem=send_sems.at[k, a], recv_sem=recv_sems.at[k, a], device_id=(x, y, 1 - c),
                    device_id_type=MESH))
        return cps

    def start(self, in_refs, out_refs, sems):
        for cp in self._copies(out_refs, sems, True):
            cp.start()

    def finish(self, in_refs, out_refs, sems):
        for cp in self._copies(out_refs, sems, False):
            cp.wait_recv()
        for cp in self._copies(out_refs, sems, True):
            cp.wait_send()


class Riders:
    def __init__(self, *riders):
        self.riders = riders
        self.collective_id = riders[0].collective_id
        self.barrier_peers = riders[0].barrier_peers
        self.inputs = [v for r in riders for v in r.inputs]
        self.out_shape = [s for r in riders for s in r.out_shape]
        self.scratch = [s for r in riders for s in r.scratch]
        self.aliases, i0, o0 = {}, 0, 0
        for r in riders:
            self.aliases.update({i0 + i: o0 + o for i, o in getattr(r, "aliases", {}).items()})
            i0, o0 = i0 + len(r.inputs), o0 + len(r.out_shape)

    def _split(self, in_refs, out_refs, sems):
        i0 = o0 = s0 = 0
        for r in self.riders:
            ni, no, ns = len(r.inputs), len(r.out_shape), len(r.scratch)
            yield r, in_refs[i0:i0 + ni], out_refs[o0:o0 + no], sems[s0:s0 + ns]
            i0, o0, s0 = i0 + ni, o0 + no, s0 + ns

    def start(self, in_refs, out_refs, sems):
        for r, i, o, s in self._split(in_refs, out_refs, sems):
            r.start(i, o, s)

    def finish(self, in_refs, out_refs, sems):
        for r, i, o, s in self._split(in_refs, out_refs, sems):
            r.finish(i, o, s)


def _peer_barrier(peers):
    barrier = pltpu.get_barrier_semaphore()
    for peer in peers:
        pl.semaphore_signal(barrier, inc=1, device_id=peer, device_id_type=MESH)
    pl.semaphore_wait(barrier, len(peers))


def hosted_call(body, rider, *, name, grid, in_specs, out_specs, out_shape, scratch_shapes, args, vmem=None):
    n_in, n_out, n_scr = len(in_specs), len(out_specs), len(scratch_shapes)
    r_in, r_out = (len(rider.inputs), len(rider.out_shape)) if rider is not None else (0, 0)
    own_barrier = getattr(rider, "collective_id", None) is not None

    def wrapped(*refs):
        ins, refs = refs[:n_in], refs[n_in:]
        rins, refs = refs[:r_in], refs[r_in:]
        outs, refs = refs[:n_out], refs[n_out:]
        routs, refs = refs[:r_out], refs[r_out:]
        scr, rscr = refs[:n_scr], refs[n_scr:]
        ids = [pl.program_id(d) for d in range(len(grid))]
        first = functools.reduce(jnp.logical_and, [i == 0 for i in ids], True)
        last = functools.reduce(jnp.logical_and, [i == g - 1 for i, g in zip(ids, grid)], True)

        def begin():
            if own_barrier:
                _peer_barrier(rider.barrier_peers())
            rider.start(rins, routs, rscr)

        if rider is not None and grid:
            pl.when(first)(begin)
        elif rider is not None:
            begin()
        if body is not None:
            body(*ins, *outs, *scr)
        if rider is not None and grid:
            pl.when(last)(lambda: rider.finish(rins, routs, rscr))
        elif rider is not None:
            rider.finish(rins, routs, rscr)

    kw = dict(grid=grid) if grid else {}
    aliases = getattr(rider, "aliases", {})
    if aliases:
        kw["input_output_aliases"] = {n_in + i: n_out + o for i, o in aliases.items()}
    if grid or vmem is not None or own_barrier:
        kw["compiler_params"] = _params(("arbitrary",) * len(grid) if grid else None, vmem,
                                        rider.collective_id if own_barrier else None)
    res = _call(
        wrapped, name=name, in_specs=list(in_specs) + _any_specs(r_in), out_specs=list(out_specs) + _any_specs(r_out),
        out_shape=list(out_shape) + (rider.out_shape if rider is not None else []),
        scratch_shapes=list(scratch_shapes) + (rider.scratch if rider is not None else []), **kw,
    )(*args, *(rider.inputs if rider is not None else []))
    return list(res[:n_out]), list(res[n_out:])


def _pick_rows(r, target=256):
    best = None
    for d in range(16, min(r, target) + 1, 16):
        if r % d == 0:
            best = d
    return r if best is None else best


def pair_sum(g, got, *, name):
    _, _, R, C = g.shape
    tr = _pick_rows(R)

    def body(g_ref, got_ref, o_ref):
        mine = jnp.where(lax.axis_index("c") == 0, g_ref[:, 0], g_ref[:, 1])
        o_ref[...] = (mine.astype(F32) + got_ref[...].astype(F32)).astype(o_ref.dtype)

    return _call(
        body, name=name, grid=(R // tr,),
        in_specs=[pl.BlockSpec((4, 2, tr, C), lambda i: (0, 0, i, 0)), pl.BlockSpec((4, tr, C), lambda i: (0, i, 0))],
        out_specs=pl.BlockSpec((4, tr, C), lambda i: (0, i, 0)),
        out_shape=jax.ShapeDtypeStruct((4, R, C), g.dtype),
        compiler_params=_params(("parallel",)),
    )(g, got)


def chip_sum_adamw(p, got, w, m, v, *, name):
    _, R, C = p.shape
    assert w.shape == (1, R, C), (name, w.shape, p.shape)
    tr = _pick_rows(R)

    def body(p_ref, got_ref, w_ref, m_ref, v_ref, g_ref, d_ref, mo_ref, vo_ref):
        my_chip = 2 * lax.axis_index("x") + lax.axis_index("y")
        g = jnp.zeros((tr, C), F32)
        for j in range(4):
            g = g + jnp.where(my_chip == j, p_ref[j], got_ref[j]).astype(F32)
        g_ref[0] = g
        d_ref[0], mo_ref[0], vo_ref[0] = _adamw_math(w_ref[0], g, m_ref[0], v_ref[0])

    part = pl.BlockSpec((4, tr, C), lambda i: (0, i, 0))
    spec = pl.BlockSpec((1, tr, C), lambda i: (0, i, 0))
    return _call(
        body, name=name, grid=(R // tr,), in_specs=[part, part, spec, spec, spec], out_specs=[spec] * 4,
        out_shape=[jax.ShapeDtypeStruct((1, R, C), F32)] * 4,
        compiler_params=_params(("parallel",)),
    )(p, got, w, m, v)


def rows_sum(g8, *, name):
    _, R, C = g8.shape

    def body(g_ref, o_ref):
        acc = g_ref[0]
        for j in range(1, N_DEV):
            acc = acc + g_ref[j]
        o_ref[...] = acc

    return _call(body, name=name, out_shape=jax.ShapeDtypeStruct((R, C), F32))(g8)


def _adamw_math(w, g, m, v):
    m = ADAM_B1 * m + (1.0 - ADAM_B1) * g
    v = ADAM_B2 * v + (1.0 - ADAM_B2) * (g * g)
    m_hat = m / (1.0 - ADAM_B1 ** ADAM_STEP)
    v_hat = v / (1.0 - ADAM_B2 ** ADAM_STEP)
    delta = -ADAM_LR * (m_hat / (jnp.sqrt(v_hat) + ADAM_EPS) + ADAM_WD * w)
    return delta, m, v


def to_bf16(xs, *, name):
    def body(*refs):
        for x_ref, o_ref in zip(refs[:len(xs)], refs[len(xs):]):
            o_ref[...] = x_ref[...].astype(BF16)

    total = sum(_nbytes(v.shape, F32) + _nbytes(v.shape, BF16) for v in xs)
    return _call(body, name=name, out_shape=[jax.ShapeDtypeStruct(v.shape, BF16) for v in xs],
                 compiler_params=_params(vmem=2 * total + (4 << 20)))(*xs)


def adamw_small(wgmv, *, name):
    n = len(wgmv)

    def body(*refs):
        ins, outs = refs[:4 * n], refs[4 * n:]
        for a in range(n):
            w_ref, g_ref, m_ref, v_ref = ins[4 * a:4 * a + 4]
            d, mn, vn = _adamw_math(w_ref[...], g_ref[...], m_ref[...], v_ref[...])
            outs[3 * a][...] = d
            outs[3 * a + 1][...] = mn
            outs[3 * a + 2][...] = vn

    flat = [t for tup in wgmv for t in tup]
    res = _call(
        body, name=name,
        out_shape=[jax.ShapeDtypeStruct(tup[0].shape, F32) for tup in wgmv for _ in range(3)],
    )(*flat)
    return [tuple(res[3 * a:3 * a + 3]) for a in range(n)]


BIG = ("w_in", "w_out", "w_mq", "w_mkv", "w_mo", "w_gu", "w_down")
COL_SHARDED = ("w_in", "w_mkv", "w_gu")
SMALL = ("g_mix", "b_f", "conv_w", "conv_b", "ln_g", "ln_b", "g_x", "g_mem", "g_ffn", "g_final")


def _full_from_gathered(n, blk):
    _, rr, cc = blk.shape
    if n in COL_SHARDED:
        return jnp.concatenate([blk[k] for k in range(N_DEV)], axis=1)
    return blk.reshape(N_DEV * rr, cc)


def _shards_from_full(n, g):
    pieces = g if isinstance(g, list) else [g]
    rr, cc = pieces[0].shape[0], sum(p.shape[1] for p in pieces)
    if n in COL_SHARDED:
        w = cc // N_DEV
        return jnp.stack([_columns(pieces, k * w, w) for k in range(N_DEV)]).reshape(4, 2, rr, w)
    return pieces[0].reshape(4, 2, rr // N_DEV, cc)


def _columns(pieces, start, width):
    out, c0 = [], 0
    for p in pieces:
        lo, hi = max(start, c0), min(start + width, c0 + p.shape[1])
        if lo < hi:
            out.append(p[:, lo - c0:hi - c0])
        c0 += p.shape[1]
    return out[0] if len(out) == 1 else jnp.concatenate(out, axis=1)


def _small_layout():
    sizes = dict(g_mix=1024, b_f=8, conv_w=CONV_K * CONV_CH, conv_b=512, ln_g=512, ln_b=512, g_x=1024,
                 g_mem=1024, g_ffn=1024, g_final=1024, loss=1)
    lay, r0 = {}, 0
    for n, sz in sizes.items():
        r = -(-sz // LANES)
        lay[n] = (r0, r, sz)
        r0 += r
    return lay, -(-r0 // 8) * 8


def kernel(x, mem, g_mix, w_in, b_f, conv_w, conv_b, ln_g, ln_b, w_out, g_x, g_mem, w_mq, w_mkv, w_mo, g_ffn, w_gu, w_down, g_final, loss_target, m_g_mix, m_w_in, m_b_f, m_conv_w, m_conv_b, m_ln_g, m_ln_b, m_w_out, m_g_x, m_g_mem, m_w_mq, m_w_mkv, m_w_mo, m_g_ffn, m_w_gu, m_w_down, m_g_final, v_g_mix, v_w_in, v_b_f, v_conv_w, v_conv_b, v_ln_g, v_ln_b, v_w_out, v_g_x, v_g_mem, v_w_mq, v_w_mkv, v_w_mo, v_g_ffn, v_w_gu, v_w_down, v_g_final):
    names = ["g_mix", "w_in", "b_f", "conv_w", "conv_b", "ln_g", "ln_b", "w_out", "g_x", "g_mem", "w_mq",
             "w_mkv", "w_mo", "g_ffn", "w_gu", "w_down", "g_final"]
    W = dict(zip(names, [g_mix, w_in, b_f, conv_w, conv_b, ln_g, ln_b, w_out, g_x, g_mem, w_mq, w_mkv, w_mo,
                         g_ffn, w_gu, w_down, g_final]))
    Mo = dict(zip(names, [m_g_mix, m_w_in, m_b_f, m_conv_w, m_conv_b, m_ln_g, m_ln_b, m_w_out, m_g_x, m_g_mem,
                          m_w_mq, m_w_mkv, m_w_mo, m_g_ffn, m_w_gu, m_w_down, m_g_final]))
    Vo = dict(zip(names, [v_g_mix, v_w_in, v_b_f, v_conv_w, v_conv_b, v_ln_g, v_ln_b, v_w_out, v_g_x, v_g_mem,
                          v_w_mq, v_w_mkv, v_w_mo, v_g_ffn, v_w_gu, v_w_down, v_g_final]))
    dev = 4 * lax.axis_index("x") + 2 * lax.axis_index("y") + lax.axis_index("c")

    two = lambda a: a.reshape(-1, a.shape[-1])
    cw_shard = jnp.pad(two(conv_w), ((0, HALO - CONV_K), (0, 0)))
    sp = dict(g_mix=g_mix, b_f=b_f, conv_b=conv_b, ln_g=ln_g, ln_b=ln_b, g_x=g_x, g_mem=g_mem,
              g_ffn=g_ffn, g_final=g_final)
    shards = to_bf16([two(W[n]) for n in ("w_in",) + LATE], name="cast_shards")
    loss_blk, grad_x, gs, reduced = local_step(x, mem, loss_target, sp, [shards[0], cw_shard], shards[1:])

    lay, rs = _small_layout()
    small = {**{n: gs[n] for n in SMALL}, "loss": loss_blk[:, :1]}
    parts = []
    for n, (r0, r, sz) in lay.items():
        flat = small[n].reshape(-1).astype(F32)
        parts.append(jnp.pad(flat, (0, r * LANES - sz)).reshape(r, LANES))
    spack = jnp.concatenate(parts, axis=0)
    spack = jnp.pad(spack, ((0, rs - spack.shape[0]), (0, 0)))
    ssum = rows_sum(all_gather([spack], name="ag_small")[0], name="small_sum")
    gsmall = {n: ssum[r0:r0 + r].reshape(-1)[:sz] for n, (r0, r, sz) in lay.items()}
    loss = gsmall["loss"].reshape(())

    grads, delta, new_m, new_v = {}, {}, {}, {}
    for n in BIG:
        p, o = reduced[n]
        grads[n], delta[n], new_m[n], new_v[n] = chip_sum_adamw(p, o, W[n], Mo[n], Vo[n], name="adamw_" + n)
    for n in SMALL:
        if n == "conv_w":
            full = gsmall[n].reshape(CONV_K, CONV_CH)
            ncol = conv_w.shape[-1]
            grads[n] = lax.dynamic_slice(full, (0, dev * ncol), (CONV_K, ncol)).reshape(conv_w.shape)
        else:
            grads[n] = gsmall[n].reshape(W[n].shape)
    upd = adamw_small([(two(W[n]), two(grads[n]), two(Mo[n]), two(Vo[n])) for n in SMALL], name="adamw_small")
    for n, (d, mn, vn) in zip(SMALL, upd):
        shp = W[n].shape
        delta[n], new_m[n], new_v[n] = d.reshape(shp), mn.reshape(shp), vn.reshape(shp)
    return (loss, grad_x, *[grads[n] for n in names], *[delta[n] for n in names],
            *[new_m[n] for n in names], *[new_v[n] for n in names])
```

```python
import functools
import math

import jax
import jax.numpy as jnp
from jax import lax
from jax.experimental import pallas as pl
from jax.experimental.pallas import tpu as pltpu

F32 = jnp.float32
BF16 = jnp.bfloat16
EPS = 1e-6
N_DEV = 8
CONV_CH = 512
CONV_K = 31
FOX_HEADS = 8
FOX_HEAD_DIM = 64
FOX_W = 512
MEM_HEADS = 4
MEM_HEAD_DIM = 256
HALO = 32
LANES = 128
ADAM_LR, ADAM_B1, ADAM_B2, ADAM_EPS, ADAM_WD, ADAM_STEP = 0.001, 0.9, 0.999, 1e-08, 0.01, 10
NEG = -1e30
VMEM_CAP = 60 * 1024 * 1024
VMEM_FOX_BWD = 56 << 20
VMEM_SWIGLU = 48 << 20
MESH = pl.DeviceIdType.MESH


def _call(body, **kw):
    kw["out_shape"] = jax.tree.map(lambda s: pltpu.HBM(s.shape, s.dtype), kw["out_shape"])
    call = pl.pallas_call(body, **kw)
    return lambda *args: call(*[pltpu.with_memory_space_constraint(a, pltpu.HBM) for a in args])


def _params(sem=None, vmem=None, collective_id=None):
    kw = {} if collective_id is None else {"collective_id": collective_id}
    if sem is not None:
        kw["dimension_semantics"] = sem
    if vmem is not None:
        kw["vmem_limit_bytes"] = int(min(VMEM_CAP, vmem))
    return pltpu.CompilerParams(**kw)


def _nbytes(shape, dtype):
    return math.prod(shape) * jnp.dtype(dtype).itemsize


def _pick(n, target):
    best = None
    for d in range(LANES, min(n, target) + 1, LANES):
        if n % d == 0:
            best = d
    return n if best is None else best


class RowEpilogue:
    def __init__(self, fn, ins, outs):
        self.fn, self.ins, self.outs = fn, list(ins), list(outs)


def matmul(a, b, *, tb=False, out_dtype=None, res=None, tm=512, tn=512, name, rider=None, b_blk=None, post=None):
    a_list = list(a) if isinstance(a, (list, tuple)) else [a]
    b_list = list(b) if isinstance(b, (list, tuple)) else [b]
    n = len(a_list)
    assert len(b_list) == n
    M = a_list[0].shape[0]
    N = b_list[0].shape[0] if tb else b_list[0].shape[1]
    tm, tn = _pick(M, tm), _pick(N, tn)
    assert M % tm == 0 and N % tn == 0, (name, M, N, tm, tn)
    dn = (((1,), (1 if tb else 0,)), ((), ()))

    n_res = int(res is not None)
    n_pin = len(post.ins) if post is not None else 0

    def body(*refs):
        acc = None
        for a_ref, b_ref in zip(refs[:n], refs[n:2 * n]):
            p = lax.dot_general(a_ref[...].astype(BF16), b_ref[...].astype(BF16), dn, preferred_element_type=F32)
            acc = p if acc is None else acc + p
        if res is not None:
            acc = acc + refs[2 * n][...].astype(F32)
        if post is None:
            refs[-1][...] = acc.astype(out_dtype)
            return
        first_in = 2 * n + n_res
        vals = post.fn(acc, *[r[...] for r in refs[first_in:first_in + n_pin]])
        for (dtype, kind), o_ref, val in zip(post.outs, refs[first_in + n_pin:], vals):
            if kind in ("row", "rowT"):
                o_ref[...] = val.astype(dtype)
            else:
                @pl.when(pl.program_id(0) == 0)
                def _(o_ref=o_ref):
                    o_ref[...] = jnp.zeros_like(o_ref)

                o_ref[...] += jnp.broadcast_to(val, o_ref.shape).astype(dtype)

    o_spec = pl.BlockSpec((tm, tn), lambda i, j: (i, j))
    in_specs, est = [], 2 * _nbytes((tm, tn), out_dtype or F32) + 2 * _nbytes((tm, tn), F32)
    for av in a_list:
        assert av.shape[0] == M
        in_specs.append(pl.BlockSpec((tm, av.shape[1]), lambda i, j: (i, 0)))
        est += (2 * jnp.dtype(av.dtype).itemsize + (av.dtype != BF16) * 2) * tm * av.shape[1]
    for idx, (av, bv) in enumerate(zip(a_list, b_list)):
        K = av.shape[1]
        kb = 0 if b_blk is None else b_blk[idx]
        assert bv.shape[0 if tb else 1] == N and bv.shape[1 if tb else 0] >= (kb + 1) * K, (name, av.shape, bv.shape)
        assert b_blk is not None or bv.shape[1 if tb else 0] == K, (name, av.shape, bv.shape)
        in_specs.append(pl.BlockSpec((tn, K), lambda i, j, kb=kb: (j, kb)) if tb
                        else pl.BlockSpec((K, tn), lambda i, j, kb=kb: (kb, j)))
        est += (2 * jnp.dtype(bv.dtype).itemsize + (bv.dtype != BF16) * 2) * tn * K
    args = a_list + b_list
    if res is not None:
        in_specs.append(o_spec)
        args.append(res)
        est += 2 * _nbytes((tm, tn), res.dtype)
    if post is None:
        out_specs, out_shape = [o_spec], [jax.ShapeDtypeStruct((M, N), out_dtype)]
    else:
        assert tn == N, (name, tn, N)
        row = pl.BlockSpec((tm, N), lambda i, j: (i, 0))
        for arr, kind in post.ins:
            in_specs.append(row if kind == "row" else pl.BlockSpec((1, N), lambda i, j: (0, 0)))
            args.append(arr)
            est += 2 * _nbytes((tm, N), arr.dtype) * (kind == "row")
        specs = {"row": (row, (M, N)), "rowT": (pl.BlockSpec((N, tm), lambda i, j: (0, i)), (N, M)),
                 "vec": (pl.BlockSpec((1, N), lambda i, j: (0, 0)), (1, N)),
                 "lanes": (pl.BlockSpec((1, LANES), lambda i, j: (0, 0)), (1, LANES))}
        out_specs = [specs[kind][0] for _, kind in post.outs]
        out_shape = [jax.ShapeDtypeStruct(specs[kind][1], dtype) for dtype, kind in post.outs]
        est += sum(2 * _nbytes((tm, N), dtype) + _nbytes((tm, N), F32) for dtype, kind in post.outs if kind[:3] == "row")
    outs, rode = hosted_call(
        body, rider, name=name, grid=(M // tm, N // tn), in_specs=in_specs, out_specs=out_specs,
        out_shape=out_shape, scratch_shapes=[], args=args, vmem=est + (8 << 20),
    )
    result = outs[0] if post is None else outs
    return result if rider is None else (result, rode)


def _rms_scale(x):
    return lax.rsqrt(jnp.mean(x * x, axis=-1, keepdims=True) + EPS)


def rmsnorm_fwd(x, g, *, name, tm=512, rider=None):
    T, D = x.shape
    tm = min(tm, T)

    def body(x_ref, g_ref, o_ref, ot_ref):
        xv = x_ref[...]
        h = xv * _rms_scale(xv) * g_ref[...]
        o_ref[...] = h.astype(BF16)
        ot_ref[...] = h.T.astype(BF16)

    (h, h_t), rode = hosted_call(
        body, rider, name=name, grid=(T // tm,),
        in_specs=[pl.BlockSpec((tm, D), lambda i: (i, 0)), pl.BlockSpec((1, D), lambda i: (0, 0))],
        out_specs=[pl.BlockSpec((tm, D), lambda i: (i, 0)), pl.BlockSpec((D, tm), lambda i: (0, i))],
        out_shape=[jax.ShapeDtypeStruct((T, D), BF16), jax.ShapeDtypeStruct((D, T), BF16)],
        scratch_shapes=[], args=(x, g),
    )
    return (h, h_t) if rider is None else (h, h_t, rode)


def _rms_bwd_math(xv, gv, dh):
    r = _rms_scale(xv)
    xh = xv * r
    dg = jnp.sum(dh * xh, axis=0, keepdims=True)
    dxh = dh * gv
    dx = r * (dxh - xh * jnp.mean(dxh * xh, axis=-1, keepdims=True))
    return dx, dg


def rms_fwd_epilogue(g):
    def fn(acc, gv):
        h = acc * _rms_scale(acc) * gv
        return acc, h, h.T
    return RowEpilogue(fn, [(g, "vec")], [(F32, "row"), (BF16, "row"), (BF16, "rowT")])


def rms_bwd_epilogue(x, g, dres, out_dtype=BF16):
    def fn(acc, xv, gv, *dr):
        dx, dg = _rms_bwd_math(xv, gv, acc)
        return (dx + dr[0].astype(F32) if dr else dx), dg
    ins = [(x, "row"), (g, "vec")] + ([(dres, "row")] if dres is not None else [])
    return RowEpilogue(fn, ins, [(out_dtype, "row"), (F32, "vec")])


def loss_epilogue(g, target):
    def fn(acc, gv, tv):
        e = acc * _rms_scale(acc) * gv - tv
        part = 0.5 * jnp.sum(jnp.mean(e * e, axis=-1, keepdims=True), axis=0, keepdims=True)
        dx, dg = _rms_bwd_math(acc, gv, e * (1.0 / acc.shape[-1]))
        return dx, dg, part
    return RowEpilogue(fn, [(g, "vec"), (target, "row")], [(BF16, "row"), (F32, "vec"), (F32, "lanes")])


def _sigmoid(v):
    return 0.5 * jnp.tanh(0.5 * v) + 0.5


def _glu(blk):
    u = blk[:, :CONV_CH].astype(F32)
    gt = blk[:, CONV_CH:].astype(F32)
    return u * _sigmoid(gt)


def _fill_causal_ext(ext, cur_ref, halo_ref, s, ts):
    ext[pl.ds(HALO, ts), :] = _glu(cur_ref[0])
    hal = _glu(halo_ref[0])
    ext[pl.ds(0, HALO), :] = jnp.where(s > 0, hal, 0.0)


SUBLANES = 8


def _make_shifted(ext, sh):
    n = ext.shape[0]
    full = ext[...]
    for r in range(1, SUBLANES):
        sh[r - 1] = pltpu.roll(full, n - r, 0)


def _tap(ext, sh, off, ts):
    r = off % SUBLANES
    return ext[pl.ds(off, ts), :] if r == 0 else sh[r - 1, pl.ds(off - r, ts), :]


def _causal_conv(ext, sh, w_ref, ts):
    acc = jnp.zeros((ts, CONV_CH), F32)
    for j in range(CONV_K):
        acc = acc + _tap(ext, sh, HALO - (CONV_K - 1) + j, ts) * w_ref[pl.ds(j, 1), :]
    return acc


def _ln_stats(y):
    mu = jnp.mean(y, axis=-1, keepdims=True)
    yc = y - mu
    rstd = lax.rsqrt(jnp.mean(yc * yc, axis=-1, keepdims=True) + EPS)
    return yc * rstd, rstd


def _conv_specs(ts, S):
    nh = ts // HALO
    cur = pl.BlockSpec((1, ts, 2 * CONV_CH), lambda b, s: (b, s, 0))
    halo = pl.BlockSpec((1, HALO, 2 * CONV_CH), lambda b, s: (b, jnp.maximum(s * nh - 1, 0), 0))
    w = pl.BlockSpec((HALO, CONV_CH), lambda b, s: (0, 0))
    vec = pl.BlockSpec((1, CONV_CH), lambda b, s: (0, 0))
    return cur, halo, w, vec


def conv_branch_fwd(ug, conv_w, conv_b, ln_g, ln_b, *, name, ts=256, rider=None):
    B, S, _ = ug.shape
    ts = min(ts, S)
    ns = S // ts
    cur, halo, w, vec = _conv_specs(ts, S)

    def body(cur_ref, halo_ref, w_ref, cb_ref, lg_ref, lb_ref, o_ref, ot_ref, y_ref, ext, sh):
        _fill_causal_ext(ext, cur_ref, halo_ref, pl.program_id(1), ts)
        _make_shifted(ext, sh)
        y = _causal_conv(ext, sh, w_ref, ts) + cb_ref[...]
        y_ref[0] = y
        yh, _ = _ln_stats(y)
        ln = yh * lg_ref[...] + lb_ref[...]
        out = ln * _sigmoid(ln)
        o_ref[0] = out.astype(BF16)
        ot_ref[...] = out.T.astype(BF16)

    return hosted_call(
        body, rider, name=name, grid=(B, ns), in_specs=[cur, halo, w, vec, vec, vec],
        out_specs=[pl.BlockSpec((1, ts, CONV_CH), lambda b, s: (b, s, 0)),
                   pl.BlockSpec((CONV_CH, ts), lambda b, s: (0, b * ns + s)),
                   pl.BlockSpec((1, ts, CONV_CH), lambda b, s: (b, s, 0))],
        out_shape=[jax.ShapeDtypeStruct((B, S, CONV_CH), BF16), jax.ShapeDtypeStruct((CONV_CH, B * S), BF16),
                   jax.ShapeDtypeStruct((B, S, CONV_CH), F32)],
        scratch_shapes=[pltpu.VMEM((ts + HALO, CONV_CH), F32),
                        pltpu.VMEM((SUBLANES - 1, ts + HALO, CONV_CH), F32)],
        args=(ug, ug, conv_w, conv_b, ln_g, ln_b),
    )


def conv_branch_bwd_a(ug, y, dcat, ln_g, ln_b, *, name, ts=256):
    B, S, _ = ug.shape
    ts = min(ts, S)
    cur, halo, _, vec = _conv_specs(ts, S)
    tile = pl.BlockSpec((1, ts, CONV_CH), lambda b, s: (b, s, 0))

    def body(cur_ref, halo_ref, y_ref, d_ref, lg_ref, lb_ref, dy_ref, dw_ref, dv_ref, ext, sh):
        _fill_causal_ext(ext, cur_ref, halo_ref, pl.program_id(1), ts)
        _make_shifted(ext, sh)
        yh, rstd = _ln_stats(y_ref[0])
        lg = lg_ref[...]
        ln = yh * lg + lb_ref[...]
        sg = _sigmoid(ln)
        dln = d_ref[0].astype(F32) * (sg * (1.0 + ln * (1.0 - sg)))
        dyh = dln * lg
        dy = rstd * (dyh - jnp.mean(dyh, axis=-1, keepdims=True)
                     - yh * jnp.mean(dyh * yh, axis=-1, keepdims=True))
        dy_ref[0] = dy

        @pl.when((pl.program_id(0) == 0) & (pl.program_id(1) == 0))
        def _():
            dw_ref[...] = jnp.zeros_like(dw_ref)
            dv_ref[...] = jnp.zeros_like(dv_ref)

        dv_ref[pl.ds(0, 1), :] += jnp.sum(dy, axis=0, keepdims=True)
        dv_ref[pl.ds(1, 1), :] += jnp.sum(dln * yh, axis=0, keepdims=True)
        dv_ref[pl.ds(2, 1), :] += jnp.sum(dln, axis=0, keepdims=True)
        for j in range(CONV_K):
            tap = _tap(ext, sh, HALO - (CONV_K - 1) + j, ts)
            dw_ref[pl.ds(j, 1), :] += jnp.sum(dy * tap, axis=0, keepdims=True)

    return _call(
        body, name=name, grid=(B, S // ts),
        in_specs=[cur, halo, tile, tile, vec, vec],
        out_specs=[tile,
                   pl.BlockSpec((HALO, CONV_CH), lambda b, s: (0, 0)),
                   pl.BlockSpec((8, CONV_CH), lambda b, s: (0, 0))],
        out_shape=[jax.ShapeDtypeStruct((B, S, CONV_CH), F32),
                   jax.ShapeDtypeStruct((HALO, CONV_CH), F32),
                   jax.ShapeDtypeStruct((8, CONV_CH), F32)],
        scratch_shapes=[pltpu.VMEM((ts + HALO, CONV_CH), F32),
                        pltpu.VMEM((SUBLANES - 1, ts + HALO, CONV_CH), F32)],
        compiler_params=_params(("arbitrary", "arbitrary")),
    )(ug, ug, y, dcat, ln_g, ln_b)


def conv_branch_bwd_b(ug, dy, conv_w, *, name, ts=256):
    B, S, _ = ug.shape
    ts = min(ts, S)
    nh, n_halo = ts // HALO, S // HALO

    def body(cur_ref, dy_ref, nxt_ref, w_ref, o_ref, ext, sh):
        last = pl.program_id(1) == pl.num_programs(1) - 1
        ext[pl.ds(0, ts), :] = dy_ref[0]
        ext[pl.ds(ts, HALO), :] = jnp.where(last, 0.0, nxt_ref[0])
        _make_shifted(ext, sh)
        da = jnp.zeros((ts, CONV_CH), F32)
        for j in range(CONV_K):
            da = da + _tap(ext, sh, CONV_K - 1 - j, ts) * w_ref[pl.ds(j, 1), :]
        blk = cur_ref[0]
        u = blk[:, :CONV_CH].astype(F32)
        sg = _sigmoid(blk[:, CONV_CH:].astype(F32))
        o_ref[0, :, :CONV_CH] = (da * sg).astype(BF16)
        o_ref[0, :, CONV_CH:] = (da * u * sg * (1.0 - sg)).astype(BF16)

    return _call(
        body, name=name, grid=(B, S // ts),
        in_specs=[pl.BlockSpec((1, ts, 2 * CONV_CH), lambda b, s: (b, s, 0)),
                  pl.BlockSpec((1, ts, CONV_CH), lambda b, s: (b, s, 0)),
                  pl.BlockSpec((1, HALO, CONV_CH), lambda b, s: (b, jnp.minimum((s + 1) * nh, n_halo - 1), 0)),
                  pl.BlockSpec((HALO, CONV_CH), lambda b, s: (0, 0))],
        out_specs=pl.BlockSpec((1, ts, 2 * CONV_CH), lambda b, s: (b, s, 0)),
        out_shape=jax.ShapeDtypeStruct((B, S, 2 * CONV_CH), BF16),
        scratch_shapes=[pltpu.VMEM((ts + HALO, CONV_CH), F32),
                        pltpu.VMEM((SUBLANES - 1, ts + HALO, CONV_CH), F32)],
        compiler_params=_params(("parallel", "parallel")),
    )(ug, dy, dy, conv_w)


def _tri(n, lower):
    r = lax.broadcasted_iota(jnp.int32, (n, n), 0)
    c = lax.broadcasted_iota(jnp.int32, (n, n), 1)
    return ((r >= c) if lower else (r <= c)).astype(F32)


def _dot_hi(a, b, dn):
    return lax.dot_general(a, b, dn, precision=lax.Precision.HIGHEST, preferred_element_type=F32)


NN = (((1,), (0,)), ((), ()))
NT = (((1,), (1,)), ((), ()))
TN = (((0,), (0,)), ((), ()))


def _log_sigmoid(v):
    e = jnp.exp(-jnp.abs(v))
    log1p_e = jnp.where(e < 1e-3, e * (1.0 - 0.5 * e), jnp.log(1.0 + e))
    return jnp.minimum(v, 0.0) - log1p_e


def fgate_fwd(h, w_f, b_f, *, name, ts=256, rider=None):
    B, S, D = h.shape
    ts = min(ts, S)

    def body(h_ref, w_ref, b_ref, f_ref, cc_ref, cr_ref, carry):
        @pl.when(pl.program_id(1) == 0)
        def _():
            carry[...] = jnp.zeros_like(carry)

        f = jnp.dot(h_ref[0], w_ref[...], preferred_element_type=F32)
        f_ref[0] = f
        logf = _log_sigmoid(f + b_ref[...])
        c = _dot_hi(_tri(ts, True), logf, NN) + carry[pl.ds(0, 1), :]
        cc_ref[0] = c
        carry[pl.ds(0, 1), :] = c[ts - 1:ts, :]
        cr_ref[0] = c.T

    return hosted_call(
        body, rider, name=name, grid=(B, S // ts),
        in_specs=[pl.BlockSpec((1, ts, D), lambda b, s: (b, s, 0)),
                  pl.BlockSpec((D, LANES), lambda b, s: (0, 0)),
                  pl.BlockSpec((1, LANES), lambda b, s: (0, 0))],
        out_specs=[pl.BlockSpec((1, ts, LANES), lambda b, s: (b, s, 0)),
                   pl.BlockSpec((1, ts, LANES), lambda b, s: (b, s, 0)),
                   pl.BlockSpec((1, LANES, ts), lambda b, s: (b, 0, s))],
        out_shape=[jax.ShapeDtypeStruct((B, S, LANES), F32), jax.ShapeDtypeStruct((B, S, LANES), F32),
                   jax.ShapeDtypeStruct((B, LANES, S), F32)],
        scratch_shapes=[pltpu.VMEM((8, LANES), F32)],
        args=(h, w_f, b_f),
    )


def fgate_bwd(dc, f, b_f, *, name, ts=256):
    B, S, _ = f.shape
    P = dc.shape[1]
    ts = min(ts, S)
    ns = S // ts

    def body(dc_ref, f_ref, b_ref, df_ref, db_ref, carry):
        @pl.when(pl.program_id(1) == 0)
        def _():
            carry[...] = jnp.zeros_like(carry)

        @pl.when((pl.program_id(0) == 0) & (pl.program_id(1) == 0))
        def _():
            db_ref[...] = jnp.zeros_like(db_ref)

        dc_t = dc_ref[0, 0]
        for j in range(1, P):
            dc_t = dc_t + dc_ref[0, j]
        dlogf = _dot_hi(_tri(ts, False), dc_t, NN) + carry[pl.ds(0, 1), :]
        carry[pl.ds(0, 1), :] = dlogf[0:1, :]
        df = dlogf * _sigmoid(-(f_ref[0] + b_ref[...]))
        df_ref[0] = df.astype(BF16)
        db_ref[...] += jnp.sum(df, axis=0, keepdims=True)

    return _call(
        body, name=name, grid=(B, ns),
        in_specs=[pl.BlockSpec((1, P, ts, LANES), lambda b, s: (b, 0, ns - 1 - s, 0)),
                  pl.BlockSpec((1, ts, LANES), lambda b, s: (b, ns - 1 - s, 0)),
                  pl.BlockSpec((1, LANES), lambda b, s: (0, 0))],
        out_specs=[pl.BlockSpec((1, ts, LANES), lambda b, s: (b, ns - 1 - s, 0)),
                   pl.BlockSpec((1, LANES), lambda b, s: (0, 0))],
        out_shape=[jax.ShapeDtypeStruct((B, S, LANES), BF16), jax.ShapeDtypeStruct((1, LANES), F32)],
        scratch_shapes=[pltpu.VMEM((8, LANES), F32)],
        compiler_params=_params(("arbitrary", "arbitrary")),
    )(dc, f, b_f)


def _lane_pick(tile, idx):
    lane = lax.broadcasted_iota(jnp.int32, tile.shape, 1)
    return jnp.sum(jnp.where(lane == idx, tile, 0.0), axis=-1, keepdims=True)


FOX_T = 512


def _fox_heads(q, cc_ref, p):
    lane = lax.broadcasted_iota(jnp.int32, q.shape, 1)
    qs = q * (1.0 / math.sqrt(FOX_HEAD_DIM))
    qhs = [jnp.where((lane < FOX_HEAD_DIM) == (hh == 0), qs, jnp.zeros_like(qs)) for hh in range(2)]
    crefs = [_lane_pick(cc_ref[0, pl.ds(0, 1), :], 2 * p + hh) for hh in range(2)]
    return qhs, crefs


def _fold_lanes(x, op):
    out = x[:, :LANES]
    for j in range(1, x.shape[1] // LANES):
        out = op(out, x[:, j * LANES:(j + 1) * LANES])
    return out


def _causal(t, transposed):
    r = lax.broadcasted_iota(jnp.int32, (t, t), 0)
    c = lax.broadcasted_iota(jnp.int32, (t, t), 1)
    return (r <= c) if transposed else (c <= r)


QKV0 = 8


def fox_fwd(z, c_col, c_row, *, name, rider=None):
    B, S, _ = z.shape
    assert S % FOX_T == 0
    tq, nq = FOX_T, S // FOX_T
    npair = FOX_HEADS // 2

    def body(q_ref, k_ref, v_ref, cc_ref, cr_ref, o_ref, l_ref, ot_ref, s_scr, m_scr, acc_scr):
        p, qi = pl.program_id(1), pl.program_id(2)
        qhs, crefs = _fox_heads(q_ref[0], cc_ref, p)
        lane = lax.broadcasted_iota(jnp.int32, (tq, LANES), 1)
        first = lane < FOX_HEAD_DIM
        for hh in range(2):
            m_scr[hh] = jnp.full((tq, LANES), NEG, F32)
            acc_scr[hh] = jnp.zeros((tq, LANES), F32)

        def logits(kb, diagonal):
            k0 = pl.multiple_of(kb * tq, tq)
            k = k_ref[0, pl.ds(k0, tq), :]
            for hh in range(2):
                s = lax.dot_general(qhs[hh], k, NT, preferred_element_type=F32)
                s = s + (crefs[hh] - cr_ref[0, pl.ds(2 * p + hh, 1), pl.ds(k0, tq)])
                if diagonal:
                    s = jnp.where(_causal(tq, False), s, NEG)
                s_scr[hh, kb] = s
                m_scr[hh] = jnp.maximum(m_scr[hh], _fold_lanes(s, jnp.maximum))

        def sweep1(kb, carry):
            logits(kb, False)
            return carry

        lax.fori_loop(0, qi, sweep1, 0)
        logits(qi, True)
        ms = [jnp.max(m_scr[hh], axis=-1, keepdims=True) for hh in range(2)]
        mbs = [jnp.broadcast_to(ms[hh], (tq, tq)) for hh in range(2)]

        for hh in range(2):
            m_scr[hh] = jnp.zeros((tq, LANES), F32)

        def weigh(kb, carry):
            k0 = pl.multiple_of(kb * tq, tq)
            v = v_ref[0, pl.ds(k0, tq), :]
            for hh in range(2):
                pr = jnp.exp(s_scr[hh, kb] - mbs[hh])
                m_scr[hh] += _fold_lanes(pr, jnp.add)
                acc_scr[hh] += jnp.dot(pr.astype(BF16), v, preferred_element_type=F32)
            return carry

        lax.fori_loop(0, qi + 1, weigh, 0)
        accs = [acc_scr[hh] for hh in range(2)]
        ls = [jnp.sum(m_scr[hh], axis=-1, keepdims=True) for hh in range(2)]
        out = jnp.where(first, accs[0] / ls[0], accs[1] / ls[1])
        o_ref[0] = out.astype(BF16)
        ot_ref[...] = out.T.astype(BF16)
        l_ref[0, 0] = jnp.where(first, ms[0] + jnp.log(ls[0]), ms[1] + jnp.log(ls[1]))

    return hosted_call(
        body, rider, name=name, grid=(B, npair, nq),
        in_specs=[pl.BlockSpec((1, tq, LANES), lambda b, p, i: (b, i, QKV0 + p)),
                  pl.BlockSpec((1, S, LANES), lambda b, p, i: (b, 0, QKV0 + npair + p)),
                  pl.BlockSpec((1, S, LANES), lambda b, p, i: (b, 0, QKV0 + 2 * npair + p)),
                  pl.BlockSpec((1, tq, LANES), lambda b, p, i: (b, i, 0)),
                  pl.BlockSpec((1, 8, S), lambda b, p, i: (b, 0, 0))],
        out_specs=[pl.BlockSpec((1, tq, LANES), lambda b, p, i: (b, i, p)),
                   pl.BlockSpec((1, 1, tq, LANES), lambda b, p, i: (b, p, i, 0)),
                   pl.BlockSpec((LANES, tq), lambda b, p, i: (p, b * nq + i))],
        out_shape=[jax.ShapeDtypeStruct((B, S, FOX_W), BF16),
                   jax.ShapeDtypeStruct((B, npair, S, LANES), F32),
                   jax.ShapeDtypeStruct((FOX_W, B * S), BF16)],
        scratch_shapes=[pltpu.VMEM((2, nq, tq, tq), F32), pltpu.VMEM((2, tq, LANES), F32),
                        pltpu.VMEM((2, tq, LANES), F32)],
        args=(z, z, z, c_col, c_row),
    )


def fox_bwd_dq(z, dcat, lse, c_col, c_row, *, name, rider=None):
    B, S, _ = z.shape
    tq, nq = FOX_T, S // FOX_T
    npair = FOX_HEADS // 2

    def body(q_ref, k_ref, v_ref, do_ref, l_ref, cc_ref, cr_ref, dq_ref, st_ref, p_scr, dp_scr, dl_scr):
        p, qi = pl.program_id(1), pl.program_id(2)
        qhs, crefs = _fox_heads(q_ref[0], cc_ref, p)
        lane = lax.broadcasted_iota(jnp.int32, (tq, LANES), 1)
        do_b = do_ref[0].astype(BF16)
        dohs = [jnp.where((lane < FOX_HEAD_DIM) == (hh == 0), do_b, jnp.zeros_like(do_b)) for hh in range(2)]
        lses = [_lane_pick(l_ref[0, 0], hh * FOX_HEAD_DIM) for hh in range(2)]
        lbs = [jnp.broadcast_to(lses[hh], (tq, tq)) for hh in range(2)]
        for hh in range(2):
            dl_scr[hh] = jnp.zeros((tq, LANES), F32)

        def probs(kb, diagonal):
            k0 = pl.multiple_of(kb * tq, tq)
            k = k_ref[0, pl.ds(k0, tq), :]
            v = v_ref[0, pl.ds(k0, tq), :]
            for hh in range(2):
                s = lax.dot_general(qhs[hh], k, NT, preferred_element_type=F32)
                s = s + (crefs[hh] - cr_ref[0, pl.ds(2 * p + hh, 1), pl.ds(k0, tq)])
                pr = jnp.exp(s - lbs[hh])
                if diagonal:
                    pr = jnp.where(_causal(tq, False), pr, 0.0)
                dp = lax.dot_general(dohs[hh], v, NT, preferred_element_type=F32)
                pdp = pr * dp
                dl_scr[hh] += _fold_lanes(pdp, jnp.add)
                p_scr[hh, kb] = pr
                dp_scr[hh, kb] = dp

        def first_pass(kb, carry):
            probs(kb, False)
            return carry

        lax.fori_loop(0, qi, first_pass, 0)
        probs(qi, True)

        dls = [jnp.sum(dl_scr[hh], axis=-1, keepdims=True) for hh in range(2)]
        dlbs = [jnp.broadcast_to(dls[hh], (tq, tq)) for hh in range(2)]

        def second_pass(kb, dq):
            k0 = pl.multiple_of(kb * tq, tq)
            k = k_ref[0, pl.ds(k0, tq), :]
            for hh in range(2):
                ds = p_scr[hh, kb] * (dp_scr[hh, kb] - dlbs[hh])
                kh = jnp.where((lane < FOX_HEAD_DIM) == (hh == 0), k, jnp.zeros_like(k))
                dq = dq + jnp.dot(ds.astype(BF16), kh, preferred_element_type=F32)
            return dq

        dq = lax.fori_loop(0, qi + 1, second_pass, jnp.zeros((tq, LANES), F32))
        dq_ref[0] = (dq * (1.0 / math.sqrt(FOX_HEAD_DIM))).astype(BF16)
        cols = jnp.zeros((tq, LANES), F32)
        for j, col in enumerate([crefs[0] - lses[0], crefs[1] - lses[1], dls[0], dls[1]]):
            cols = jnp.where(lane == j, col, cols)
        st_ref[0, 0] = cols.T[:8]

    return hosted_call(
        body, rider, name=name, grid=(B, npair, nq),
        in_specs=[pl.BlockSpec((1, tq, LANES), lambda b, p, i: (b, i, QKV0 + p)),
                  pl.BlockSpec((1, S, LANES), lambda b, p, i: (b, 0, QKV0 + npair + p)),
                  pl.BlockSpec((1, S, LANES), lambda b, p, i: (b, 0, QKV0 + 2 * npair + p)),
                  pl.BlockSpec((1, tq, LANES), lambda b, p, i: (b, i, npair + p)),
                  pl.BlockSpec((1, 1, tq, LANES), lambda b, p, i: (b, p, i, 0)),
                  pl.BlockSpec((1, tq, LANES), lambda b, p, i: (b, i, 0)),
                  pl.BlockSpec((1, 8, S), lambda b, p, i: (b, 0, 0))],
        out_specs=[pl.BlockSpec((1, tq, LANES), lambda b, p, i: (b, i, p)),
                   pl.BlockSpec((1, 1, 8, tq), lambda b, p, i: (b, p, 0, i))],
        out_shape=[jax.ShapeDtypeStruct((B, S, FOX_W), BF16), jax.ShapeDtypeStruct((B, npair, 8, S), F32)],
        scratch_shapes=[pltpu.VMEM((2, nq, tq, tq), F32), pltpu.VMEM((2, nq, tq, tq), F32),
                        pltpu.VMEM((2, tq, LANES), F32)],
        args=(z, z, z, dcat, lse, c_col, c_row), vmem=VMEM_FOX_BWD,
    )


def fox_bwd_dkdv(z, dcat, stats, c_col, *, name, rider=None):
    B, S, _ = z.shape
    tk, nq = FOX_T, S // FOX_T
    npair = FOX_HEADS // 2
    inv = 1.0 / math.sqrt(FOX_HEAD_DIM)

    def body(q_ref, k_ref, v_ref, do_ref, st_ref, cc_ref, dk_ref, dv_ref, dc_ref, dk_scr, dv_scr, dc_scr):
        p, kt = pl.program_id(1), pl.program_id(2)
        lane = lax.broadcasted_iota(jnp.int32, (tk, LANES), 1)
        masks = [(lane < FOX_HEAD_DIM) == (hh == 0) for hh in range(2)]
        k = k_ref[0]
        v = v_ref[0]
        khs = [jnp.where(masks[hh], k, jnp.zeros_like(k)) for hh in range(2)]
        vhs = [jnp.where(masks[hh], v, jnp.zeros_like(v)) for hh in range(2)]
        ccbs = [jnp.broadcast_to(_lane_pick(cc_ref[0], 2 * p + hh), (tk, tk)) for hh in range(2)]
        dk_scr[...] = jnp.zeros_like(dk_scr)
        dv_scr[...] = jnp.zeros_like(dv_scr)
        dc_scr[...] = jnp.zeros_like(dc_scr)

        def tile(qb, diagonal):
            q0 = pl.multiple_of(qb * tk, tk)
            qs = q_ref[0, pl.ds(q0, tk), :] * inv
            do_b = do_ref[0, pl.ds(q0, tk), :].astype(BF16)
            for hh in range(2):
                st = lax.dot_general(khs[hh], qs, NT, preferred_element_type=F32)
                pr = jnp.exp(st - ccbs[hh] + st_ref[0, 0, pl.ds(hh, 1), pl.ds(q0, tk)])
                if diagonal:
                    pr = jnp.where(_causal(tk, True), pr, 0.0)
                dp = lax.dot_general(vhs[hh], do_b, NT, preferred_element_type=F32)
                ds = pr * (dp - st_ref[0, 0, pl.ds(2 + hh, 1), pl.ds(q0, tk)])
                dv_scr[...] += jnp.dot(pr.astype(BF16), jnp.where(masks[hh], do_b, jnp.zeros_like(do_b)),
                                       preferred_element_type=F32)
                dk_scr[...] += jnp.dot(ds.astype(BF16), jnp.where(masks[hh], qs, jnp.zeros_like(qs)),
                                       preferred_element_type=F32)
                dc_scr[hh] -= _fold_lanes(ds, jnp.add)

        def later(qb, carry):
            tile(qb, False)
            return carry

        tile(kt, True)
        lax.fori_loop(kt + 1, nq, later, 0)
        dk_ref[0] = dk_scr[...].astype(BF16)
        dv_ref[0] = dv_scr[...].astype(BF16)
        dcs = [jnp.sum(dc_scr[hh], axis=-1, keepdims=True) for hh in range(2)]
        dc_ref[0, 0] = jnp.where(lane == 2 * p, dcs[0], jnp.where(lane == 2 * p + 1, dcs[1], 0.0))

    full = lambda col: pl.BlockSpec((1, S, LANES), col)
    tile_spec = lambda col: pl.BlockSpec((1, tk, LANES), col)
    return hosted_call(
        body, rider, name=name, grid=(B, npair, nq),
        in_specs=[full(lambda b, p, t: (b, 0, QKV0 + p)),
                  tile_spec(lambda b, p, t: (b, t, QKV0 + npair + p)),
                  tile_spec(lambda b, p, t: (b, t, QKV0 + 2 * npair + p)),
                  full(lambda b, p, t: (b, 0, npair + p)),
                  pl.BlockSpec((1, 1, 8, S), lambda b, p, t: (b, p, 0, 0)),
                  tile_spec(lambda b, p, t: (b, t, 0))],
        out_specs=[tile_spec(lambda b, p, t: (b, t, p)), tile_spec(lambda b, p, t: (b, t, p)),
                   pl.BlockSpec((1, 1, tk, LANES), lambda b, p, t: (b, p, t, 0))],
        out_shape=[jax.ShapeDtypeStruct((B, S, FOX_W), BF16)] * 2
        + [jax.ShapeDtypeStruct((B, npair, S, LANES), F32)],
        scratch_shapes=[pltpu.VMEM((tk, LANES), F32), pltpu.VMEM((tk, LANES), F32),
                        pltpu.VMEM((2, tk, LANES), F32)],
        args=(z, z, z, dcat, stats, c_col),
    )


def xattn_fwd(qm, kv, *, name, tq=512):
    B, S, D = qm.shape
    M = kv.shape[1]
    tq = min(tq, S)
    inv = 1.0 / math.sqrt(MEM_HEAD_DIM)

    nq = S // tq

    def body(q_ref, kv_ref, o_ref, ot_ref):
        for h in range(MEM_HEADS):
            c0 = h * MEM_HEAD_DIM
            qh = q_ref[0, :, c0:c0 + MEM_HEAD_DIM]
            kh = kv_ref[0, :, c0:c0 + MEM_HEAD_DIM]
            vh = kv_ref[0, :, D + c0:D + c0 + MEM_HEAD_DIM]
            s = lax.dot_general(qh, kh, NT, preferred_element_type=F32) * inv
            e = jnp.exp(s - jnp.max(s, axis=-1, keepdims=True))
            o = jnp.dot(e.astype(BF16), vh, preferred_element_type=F32) / jnp.sum(e, axis=-1, keepdims=True)
            o_ref[0, :, c0:c0 + MEM_HEAD_DIM] = o.astype(BF16)
            ot_ref[c0:c0 + MEM_HEAD_DIM, :] = o.T.astype(BF16)

    return _call(
        body, name=name, grid=(B, nq),
        in_specs=[pl.BlockSpec((1, tq, D), lambda b, i: (b, i, 0)),
                  pl.BlockSpec((1, M, 2 * D), lambda b, i: (b, 0, 0))],
        out_specs=[pl.BlockSpec((1, tq, D), lambda b, i: (b, i, 0)),
                   pl.BlockSpec((D, tq), lambda b, i: (0, b * nq + i))],
        out_shape=[jax.ShapeDtypeStruct((B, S, D), BF16), jax.ShapeDtypeStruct((D, B * S), BF16)],
        compiler_params=_params(("parallel", "parallel")),
    )(qm, kv)


def xattn_bwd(qm, kv, do, *, name, tq=512):
    B, S, D = qm.shape
    M = kv.shape[1]
    tq = min(tq, S)
    inv = 1.0 / math.sqrt(MEM_HEAD_DIM)

    def body(q_ref, kv_ref, do_ref, dq_ref, dkv_ref):
        @pl.when(pl.program_id(1) == 0)
        def _():
            dkv_ref[...] = jnp.zeros_like(dkv_ref)

        for h in range(MEM_HEADS):
            c0 = h * MEM_HEAD_DIM
            qh = q_ref[0, :, c0:c0 + MEM_HEAD_DIM]
            kh = kv_ref[0, :, c0:c0 + MEM_HEAD_DIM]
            vh = kv_ref[0, :, D + c0:D + c0 + MEM_HEAD_DIM]
            doh = do_ref[0, :, c0:c0 + MEM_HEAD_DIM]
            s = lax.dot_general(qh, kh, NT, preferred_element_type=F32) * inv
            e = jnp.exp(s - jnp.max(s, axis=-1, keepdims=True))
            pr = e / jnp.sum(e, axis=-1, keepdims=True)
            dp = lax.dot_general(doh, vh, NT, preferred_element_type=F32)
            ds = pr * (dp - jnp.sum(pr * dp, axis=-1, keepdims=True))
            ds_b = ds.astype(BF16)
            dq_ref[0, :, c0:c0 + MEM_HEAD_DIM] = (jnp.dot(ds_b, kh, preferred_element_type=F32) * inv).astype(BF16)
            dkv_ref[0, :, c0:c0 + MEM_HEAD_DIM] += lax.dot_general(ds_b, qh, TN, preferred_element_type=F32) * inv
            dkv_ref[0, :, D + c0:D + c0 + MEM_HEAD_DIM] += lax.dot_general(
                pr.astype(BF16), doh, TN, preferred_element_type=F32)

    row = pl.BlockSpec((1, tq, D), lambda b, i: (b, i, 0))
    kvs = pl.BlockSpec((1, M, 2 * D), lambda b, i: (b, 0, 0))
    return _call(
        body, name=name, grid=(B, S // tq), in_specs=[row, kvs, row], out_specs=[row, kvs],
        out_shape=[jax.ShapeDtypeStruct((B, S, D), BF16), jax.ShapeDtypeStruct((B, M, 2 * D), F32)],
        compiler_params=_params(("parallel", "arbitrary")),
    )(qm, kv, do)


SWIGLU_TN = 2816


def _chunks(n, w=256):
    return [(c0, min(w, n - c0)) for c0 in range(0, n, w)]


def mm_swiglu_fwd(hf, w_gu, *, name, tm=256):
    T, D = hf.shape
    Fh = w_gu.shape[1] // 2
    tm, tn = min(tm, T), SWIGLU_TN
    nj = Fh // tn
    assert Fh % tn == 0 and T % tm == 0

    def body(a_ref, bg_ref, bu_ref, g_ref, u_ref, o_ref, ot_ref):
        a = a_ref[...]
        for c0, cw in _chunks(tn):
            cols = pl.ds(c0, cw)
            g = jnp.dot(a, bg_ref[:, cols], preferred_element_type=F32)
            u = jnp.dot(a, bu_ref[:, cols], preferred_element_type=F32)
            act = g * _sigmoid(g) * u
            g_ref[:, cols] = g.astype(BF16)
            u_ref[:, cols] = u.astype(BF16)
            o_ref[:, cols] = act.astype(BF16)
            ot_ref[cols, :] = act.T.astype(BF16)

    tile = pl.BlockSpec((tm, tn), lambda i, j: (i, j))
    return _call(
        body, name=name, grid=(T // tm, nj),
        in_specs=[pl.BlockSpec((tm, D), lambda i, j: (i, 0)), pl.BlockSpec((D, tn), lambda i, j: (0, j)),
                  pl.BlockSpec((D, tn), lambda i, j: (0, nj + j))],
        out_specs=[tile, tile, tile, pl.BlockSpec((tn, tm), lambda i, j: (j, i))],
        out_shape=[jax.ShapeDtypeStruct((T, Fh), BF16)] * 3 + [jax.ShapeDtypeStruct((Fh, T), BF16)],
        compiler_params=_params(("parallel", "parallel"), VMEM_SWIGLU),
    )(hf, w_gu, w_gu)


def mm_swiglu_bwd(dx, w_down, g, u, *, name, tm=256):
    T, D = dx.shape
    Fh = w_down.shape[0]
    tm, tn = min(tm, T), SWIGLU_TN
    assert Fh % tn == 0 and T % tm == 0

    def body(a_ref, b_ref, g_ref, u_ref, dg_ref, du_ref):
        a = a_ref[...].astype(BF16)
        for c0, cw in _chunks(tn):
            cols = pl.ds(c0, cw)
            d = lax.dot_general(a, b_ref[cols, :], NT, preferred_element_type=F32)
            gv = g_ref[:, cols].astype(F32)
            uv = u_ref[:, cols].astype(F32)
            sg = _sigmoid(gv)
            dg_ref[:, cols] = (d * uv * (sg * (1.0 + gv * (1.0 - sg)))).astype(BF16)
            du_ref[:, cols] = (d * gv * sg).astype(BF16)

    tile = pl.BlockSpec((tm, tn), lambda i, j: (i, j))
    return _call(
        body, name=name, grid=(T // tm, Fh // tn),
        in_specs=[pl.BlockSpec((tm, D), lambda i, j: (i, 0)), pl.BlockSpec((tn, D), lambda i, j: (j, 0)), tile, tile],
        out_specs=[tile, tile],
        out_shape=[jax.ShapeDtypeStruct((T, Fh), BF16)] * 2,
        compiler_params=_params(("parallel", "parallel"), VMEM_SWIGLU),
    )(dx, w_down, g, u)


LATE_MID = ("w_out", "w_mq", "w_mo")
LATE_KV = ("w_mkv",)
LATE_FFN = ("w_gu", "w_down")
LATE = LATE_MID + LATE_KV + LATE_FFN
RS_GROUPS = (("w_gu", "w_down"), ("w_out", "w_mq", "w_mkv", "w_mo"), ("w_in",))


def pair_sums(names, g42, got):
    return {n: pair_sum(g, o, name="rs_pair_sum_" + n) for n, g, o in zip(names, g42, got)}


def local_step(x, mem, target, sp, first_shards, late_shards):
    B, S, D = x.shape
    T = B * S
    M = mem.shape[1]
    row = lambda v: v.reshape(1, -1).astype(F32)
    g_mix, g_x, g_mem, g_ffn, g_final = (row(sp[k]) for k in ("g_mix", "g_x", "g_mem", "g_ffn", "g_final"))
    conv_b, ln_g, ln_b = row(sp["conv_b"]), row(sp["ln_g"]), row(sp["ln_b"])
    b_f = jnp.pad(row(sp["b_f"]), ((0, 0), (0, LANES - FOX_HEADS)))
    n_ug, n_main = 2 * CONV_CH, 2 * CONV_CH + 3 * FOX_W

    x2d = x.reshape(T, D)
    h, h_t, partly = rmsnorm_fwd(x2d, g_mix, name="rms_mix", rider=AllGatherStage1(first_shards))
    w_in8, cw8 = run_rider(AllGatherStage2(partly), name="ag_first_stage2")
    w_in_full = _full_from_gathered("w_in", w_in8)
    conv_w = cw8.transpose(1, 0, 2).reshape(HALO, -1)
    w_main, w_ug, w_qkv = w_in_full[:, :n_main], w_in_full[:, :n_ug], w_in_full[:, n_ug:n_main]
    w_f = jnp.pad(w_in_full[:, n_main:], ((0, 0), (0, LANES - FOX_HEADS)))
    z = matmul(h, w_main, out_dtype=BF16, tn=n_main, name="mm_in")
    z3 = z.reshape(B, S, n_main)
    n_mid, n_kv = len(LATE_MID), len(LATE_MID) + len(LATE_KV)
    (conv_out, conv_t, conv_y), partly_mid = conv_branch_fwd(z3, conv_w, conv_b, ln_g, ln_b, name="conv_fwd",
                                                     rider=AllGatherStage1(late_shards[:n_mid]))
    (f_raw, c_col, c_row), rode = fgate_fwd(
        h.reshape(B, S, D), w_f, b_f, name="fgate_fwd",
        rider=Riders(AllGatherStage1(late_shards[n_mid:n_kv]), AllGatherStage2(partly_mid)))
    partly_kv, full_mid = rode[:n_kv - n_mid], rode[n_kv - n_mid:]
    (att, lse, att_t), rode = fox_fwd(
        z3, c_col, c_row, name="fox_fwd",
        rider=Riders(AllGatherStage1(late_shards[n_kv:]), AllGatherStage2(partly_kv)))
    partly_ffn, full_kv = rode[:len(LATE_FFN)], rode[len(LATE_FFN):]
    wf = {n: _full_from_gathered(n, blk) for n, blk in zip(LATE_MID + LATE_KV, full_mid + full_kv)}
    (x1, hx, hx_t), full_ffn = matmul(
        [conv_out.reshape(T, CONV_CH), att.reshape(T, FOX_W)], [wf["w_out"], wf["w_out"]], b_blk=[0, 1],
        res=x2d, tn=D, name="mm_out", post=rms_fwd_epilogue(g_x), rider=AllGatherStage2(partly_ffn))
    wf.update({n: _full_from_gathered(n, blk) for n, blk in zip(LATE_FFN, full_ffn)})
    qm = matmul(hx, wf["w_mq"], out_dtype=BF16, tn=D, name="mm_mq")
    mem2d = mem.reshape(B * M, D)
    mem_n, mem_n_t = rmsnorm_fwd(mem2d, g_mem, name="rms_mem")
    kv = matmul(mem_n, wf["w_mkv"], out_dtype=BF16, tn=2 * D, name="mm_mkv").reshape(B, M, 2 * D)
    o, o_t = xattn_fwd(qm.reshape(B, S, D), kv, name="xattn_fwd")
    o = o.reshape(T, D)
    x2, hf, hf_t = matmul(o, wf["w_mo"], res=x1, tn=D, name="mm_mo", post=rms_fwd_epilogue(g_ffn))
    gate, up, act, act_t = mm_swiglu_fwd(hf, wf["w_gu"], name="mm_gu")
    dx3, dg_final, loss = matmul(act, wf["w_down"], res=x2, tn=D, name="mm_down",
                                 post=loss_epilogue(g_final, target.reshape(T, D)))
    gw = {}
    gw["w_down"] = matmul(act_t, dx3, out_dtype=BF16, tm=1408, tn=512, name="dw_down")
    dgate, dup = mm_swiglu_bwd(dx3, wf["w_down"], gate, up, name="dx_down")
    gw["w_gu"] = [matmul(hf_t, dgate, out_dtype=BF16, tn=1408, name="dw_gate"),
                  matmul(hf_t, dup, out_dtype=BF16, tn=1408, name="dw_up")]
    g42 = [_shards_from_full(n, gw[n]) for n in RS_GROUPS[0]]
    (dx2, dg_ffn), got = matmul([dgate, dup], [wf["w_gu"], wf["w_gu"]], b_blk=[0, 1], tb=True, tm=256, tn=D,
                                name="dx_gu", post=rms_bwd_epilogue(x2, g_ffn, dx3), rider=SiblingExchange(g42))
    parts = pair_sums(RS_GROUPS[0], g42, got)
    gw["w_mo"] = matmul(o_t, dx2, out_dtype=BF16, tn=D, name="dw_mo")
    do = matmul(dx2, wf["w_mo"], tb=True, out_dtype=BF16, tn=D, name="dx_mo")
    dqm, dkv = xattn_bwd(qm.reshape(B, S, D), kv, do.reshape(B, S, D), name="xattn_bwd")
    dqm = dqm.reshape(T, D)
    dkv = dkv.reshape(B * M, 2 * D)
    gw["w_mq"] = matmul(hx_t, dqm, out_dtype=BF16, tn=D, name="dw_mq")
    dx1, dg_x = matmul(dqm, wf["w_mq"], tb=True, tn=D, name="dx_mq", post=rms_bwd_epilogue(x1, g_x, dx2))
    gw["w_mkv"] = matmul(mem_n_t, dkv, out_dtype=BF16, tn=D, name="dw_mkv")
    _, dg_mem = matmul(dkv, wf["w_mkv"], tb=True, tn=D, name="dx_mkv", post=rms_bwd_epilogue(mem2d, g_mem, None))
    gw["w_out"] = jnp.concatenate([matmul(conv_t, dx1, out_dtype=BF16, tn=D, name="dw_out_conv"),
                                   matmul(att_t, dx1, out_dtype=BF16, tn=D, name="dw_out_att")], axis=0)
    g42 = [_shards_from_full(n, gw[n]) for n in RS_GROUPS[1]]
    dcat, got = matmul(dx1, wf["w_out"], tb=True, out_dtype=BF16, tn=D, name="dx_out", rider=SiblingExchange(g42))
    dcat = dcat.reshape(B, S, D)
    parts.update(pair_sums(RS_GROUPS[1], g42, got))
    dy, dconv_w, dvec = conv_branch_bwd_a(z3, conv_y, dcat, ln_g, ln_b, name="conv_bwd_a")
    dug = conv_branch_bwd_b(z3, dy, conv_w, name="conv_bwd_b")
    gots = {}
    (dq, stats), got = fox_bwd_dq(z3, dcat, lse, c_col, c_row, name="fox_bwd_dq",
                                  rider=ChipExchange([parts[n] for n in RS_GROUPS[0]]))
    gots.update(zip(RS_GROUPS[0], got))
    (dk, dv, dc), got = fox_bwd_dkdv(z3, dcat, stats, c_col, name="fox_bwd_dkdv",
                                     rider=ChipExchange([parts[n] for n in RS_GROUPS[1]]))
    gots.update(zip(RS_GROUPS[1], got))
    df, db_f = fgate_bwd(dc, f_raw, b_f, name="fgate_bwd")
    dug2 = dug.reshape(T, n_ug)
    dqkv = jnp.concatenate([dq, dk, dv], axis=-1).reshape(T, 3 * FOX_W)
    df2 = df.reshape(T, LANES)
    dw_in = [matmul(h_t, dug2, out_dtype=BF16, tn=n_ug, name="dw_in_ug"),
             matmul(h_t, dqkv, out_dtype=BF16, tn=3 * FOX_W, name="dw_in_qkv"),
             matmul(h_t, df2, out_dtype=BF16, name="dw_f")[:, :FOX_HEADS]]
    g42 = [_shards_from_full("w_in", dw_in)]
    parts.update(pair_sums(RS_GROUPS[2], g42, run_rider(SiblingExchange(g42), name="rs_sibling_in")))
    (dx, dg_mix), (gots["w_in"],) = matmul(
        [dug2, dqkv, df2], [w_ug, w_qkv, w_f], tb=True, tn=D, name="dx_in",
        post=rms_bwd_epilogue(x2d, g_mix, dx1, out_dtype=F32), rider=ChipExchange([parts["w_in"]]))
    gs = dict(g_mix=dg_mix, b_f=db_f[:, :FOX_HEADS], conv_w=dconv_w[:CONV_K], conv_b=dvec[0:1],
              ln_g=dvec[1:2], ln_b=dvec[2:3], g_x=dg_x, g_mem=dg_mem, g_ffn=dg_ffn, g_final=dg_final)
    return loss, dx.reshape(B, S, D), gs, {n: (parts[n], gots[n]) for n in BIG}


def _me():
    return lax.axis_index("x"), lax.axis_index("y"), lax.axis_index("c")


def _any_specs(n):
    return [pl.BlockSpec(memory_space=pl.ANY)] * n


def all_gather(xs, *, name):
    n = len(xs)

    def body(*refs):
        x_refs, out_refs = refs[:n], refs[n:2 * n]
        send_sems, recv_sems, local_sems = refs[2 * n:]
        x, y, c = _me()
        me, sibling = (x, y, c), (x, y, 1 - c)
        chips = [(1 - x, y), (x, 1 - y), (1 - x, 1 - y)]

        def slot(a, px, py, pc):
            return out_refs[a].at[4 * px + 2 * py + pc]

        def copy(a, k, block, to, own=False):
            return pltpu.make_async_remote_copy(
                src_ref=x_refs[a] if own else slot(a, *block), dst_ref=slot(a, *block),
                send_sem=send_sems.at[k, a], recv_sem=recv_sems.at[k, a], device_id=to, device_id_type=MESH)

        mine = [pltpu.make_async_copy(x_refs[a], slot(a, *me), local_sems.at[a]) for a in range(n)]
        first = [copy(a, 0, me, sibling, own=True) for a in range(n)]
        first += [copy(a, 1 + j, me, (*chip, c), own=True) for j, chip in enumerate(chips) for a in range(n)]
        for cp in mine + first:
            cp.start()
        passed = []
        for j, chip in enumerate(chips):
            for a in range(n):
                copy(a, 1 + j, (*chip, c), me).wait_recv()
                passed.append(copy(a, 4 + j, (*chip, c), sibling))
                passed[-1].start()
        for a in range(n):
            copy(a, 0, sibling, me).wait_recv()
            for j, chip in enumerate(chips):
                copy(a, 4 + j, (*chip, 1 - c), me).wait_recv()
        for cp in first + passed:
            cp.wait_send()
        for cp in mine:
            cp.wait()

    return _call(
        body, name=name, in_specs=_any_specs(n), out_specs=_any_specs(n),
        out_shape=[jax.ShapeDtypeStruct((N_DEV,) + v.shape, v.dtype) for v in xs],
        scratch_shapes=[pltpu.SemaphoreType.DMA((7, n)), pltpu.SemaphoreType.DMA((7, n)),
                        pltpu.SemaphoreType.DMA((n,))],
    )(*xs)


SIBLING_BARRIER = 1
CHIPS_BARRIER = 2
GATHER_BARRIER = 3


class SiblingExchange:
    collective_id = SIBLING_BARRIER

    def __init__(self, gs):
        n = len(gs)
        self.n, self.inputs = n, list(gs)
        self.out_shape = [jax.ShapeDtypeStruct((4,) + g.shape[2:], g.dtype) for g in gs]
        self.scratch = [pltpu.SemaphoreType.DMA((n,)), pltpu.SemaphoreType.DMA((n,))]

    @staticmethod
    def barrier_peers():
        x, y, c = _me()
        return [(x, y, 1 - c)]

    def _copies(self, g_refs, out_refs, sems):
        send_sems, recv_sems = sems
        x, y, c = _me()
        return [pltpu.make_async_remote_copy(
            src_ref=g_refs[a].at[:, 1 - c], dst_ref=out_refs[a], send_sem=send_sems.at[a],
            recv_sem=recv_sems.at[a], device_id=(x, y, 1 - c), device_id_type=MESH) for a in range(self.n)]

    def start(self, in_refs, out_refs, sems):
        for cp in self._copies(in_refs, out_refs, sems):
            cp.start()

    def finish(self, in_refs, out_refs, sems):
        for cp in self._copies(in_refs, out_refs, sems):
            cp.wait()


def run_rider(rider, *, name):
    return hosted_call(None, rider, name=name, grid=(), in_specs=[], out_specs=[], out_shape=[],
                       scratch_shapes=[], args=[])[1]


class ChipExchange:
    collective_id = CHIPS_BARRIER

    @staticmethod
    def barrier_peers():
        x, y, c = _me()
        return [(1 - x, y, c), (x, 1 - y, c), (1 - x, 1 - y, c)]

    def __init__(self, ps):
        n = len(ps)
        self.n, self.inputs = n, list(ps)
        self.out_shape = [jax.ShapeDtypeStruct(p.shape, p.dtype) for p in ps]
        self.scratch = [pltpu.SemaphoreType.DMA((3, n)), pltpu.SemaphoreType.DMA((3, n))]

    def _copies(self, p_refs, out_refs, sems, outgoing):
        send_sems, recv_sems = sems
        x, y, c = _me()
        my_chip = 2 * x + y
        cps = []
        for k in range(3):
            px, py = x ^ ((k + 1) >> 1), y ^ ((k + 1) & 1)
            src, dst = (2 * px + py, my_chip) if outgoing else (my_chip, 2 * px + py)
            for a in range(self.n):
                cps.append(pltpu.make_async_remote_copy(
                    src_ref=p_refs[a].at[src], dst_ref=out_refs[a].at[dst], send_sem=send_sems.at[k, a],
                    recv_sem=recv_sems.at[k, a], device_id=(px, py, c), device_id_type=MESH))
        return cps

    def start(self, in_refs, out_refs, sems):
        for cp in self._copies(in_refs, out_refs, sems, True):
            cp.start()

    def finish(self, in_refs, out_refs, sems):
        for cp in self._copies(in_refs, out_refs, sems, False):
            cp.wait_recv()
        for cp in self._copies(in_refs, out_refs, sems, True):
            cp.wait_send()


class AllGatherStage1:
    collective_id = GATHER_BARRIER

    @staticmethod
    def barrier_peers():
        x, y, c = _me()
        return [(x, y, 1 - c), (1 - x, y, c), (x, 1 - y, c), (1 - x, 1 - y, c)]

    def __init__(self, xs):
        n = len(xs)
        self.n, self.inputs = n, list(xs)
        self.out_shape = [jax.ShapeDtypeStruct((N_DEV,) + v.shape, v.dtype) for v in xs]
        self.scratch = [pltpu.SemaphoreType.DMA((4, n)), pltpu.SemaphoreType.DMA((4, n)),
                        pltpu.SemaphoreType.DMA((n,))]

    def _copies(self, x_refs, out_refs, sems, kind):
        send_sems, recv_sems, local_sems = sems
        x, y, c = _me()
        slot = lambda a, d: out_refs[a].at[4 * d[0] + 2 * d[1] + d[2]]
        if kind == "local":
            return [pltpu.make_async_copy(x_refs[a], slot(a, (x, y, c)), local_sems.at[a]) for a in range(self.n)]
        cps = []
        for k, peer in enumerate([(x, y, 1 - c), (1 - x, y, c), (x, 1 - y, c), (1 - x, 1 - y, c)]):
            for a in range(self.n):
                cps.append(pltpu.make_async_remote_copy(
                    src_ref=x_refs[a], dst_ref=slot(a, (x, y, c) if kind == "out" else peer),
                    send_sem=send_sems.at[k, a], recv_sem=recv_sems.at[k, a], device_id=peer, device_id_type=MESH))
        return cps

    def start(self, in_refs, out_refs, sems):
        for cp in self._copies(in_refs, out_refs, sems, "local") + self._copies(in_refs, out_refs, sems, "out"):
            cp.start()

    def finish(self, in_refs, out_refs, sems):
        for cp in self._copies(in_refs, out_refs, sems, "in"):
            cp.wait_recv()
        for cp in self._copies(in_refs, out_refs, sems, "out"):
            cp.wait_send()
        for cp in self._copies(in_refs, out_refs, sems, "local"):
            cp.wait()


class AllGatherStage2:
    collective_id = SIBLING_BARRIER

    @staticmethod
    def barrier_peers():
        x, y, c = _me()
        return [(x, y, 1 - c)]

    def __init__(self, outs):
        n = len(outs)
        self.n, self.inputs = n, list(outs)
        self.out_shape = [jax.ShapeDtypeStruct(o.shape, o.dtype) for o in outs]
        self.scratch = [pltpu.SemaphoreType.DMA((3, n)), pltpu.SemaphoreType.DMA((3, n))]
        self.aliases = {a: a for a in range(n)}

    def _copies(self, out_refs, sems, outgoing):
        send_sems, recv_sems = sems
        x, y, c = _me()
        cps = []
        for k, (px, py) in enumerate([(1 - x, y), (x, 1 - y), (1 - x, 1 - y)]):
            for a in range(self.n):
                cps.append(pltpu.make_async_remote_copy(
                    src_ref=out_refs[a].at[4 * px + 2 * py + c],
                    dst_ref=out_refs[a].at[4 * px + 2 * py + (c if outgoing else 1 - c)],
                    send_sem=send_sems.at[k, a], recv_sem=recv_sems.at[k, a], device_id=(x, y, 1 - c),
                    device_id_type=MESH))
        return cps

    def start(self, in_refs, out_refs, sems):
        for cp in self._copies(out_refs, sems, True):
            cp.start()

    def finish(self, in_refs, out_refs, sems):
        for cp in self._copies(out_refs, sems, False):
            cp.wait_recv()
        for cp in self._copies(out_refs, sems, True):
            cp.wait_send()


class Riders:
    def __init__(self, *riders):
        self.riders = riders
        self.collective_id = riders[0].collective_id
        self.barrier_peers = riders[0].barrier_peers
        self.inputs = [v for r in riders for v in r.inputs]
        self.out_shape = [s for r in riders for s in r.out_shape]
        self.scratch = [s for r in riders for s in r.scratch]
        self.aliases, i0, o0 = {}, 0, 0
        for r in riders:
            self.aliases.update({i0 + i: o0 + o for i, o in getattr(r, "aliases", {}).items()})
            i0, o0 = i0 + len(r.inputs), o0 + len(r.out_shape)

    def _split(self, in_refs, out_refs, sems):
        i0 = o0 = s0 = 0
        for r in self.riders:
            ni, no, ns = len(r.inputs), len(r.out_shape), len(r.scratch)
            yield r, in_refs[i0:i0 + ni], out_refs[o0:o0 + no], sems[s0:s0 + ns]
            i0, o0, s0 = i0 + ni, o0 + no, s0 + ns

    def start(self, in_refs, out_refs, sems):
        for r, i, o, s in self._split(in_refs, out_refs, sems):
            r.start(i, o, s)

    def finish(self, in_refs, out_refs, sems):
        for r, i, o, s in self._split(in_refs, out_refs, sems):
            r.finish(i, o, s)


def _peer_barrier(peers):
    barrier = pltpu.get_barrier_semaphore()
    for peer in peers:
        pl.semaphore_signal(barrier, inc=1, device_id=peer, device_id_type=MESH)
    pl.semaphore_wait(barrier, len(peers))


def hosted_call(body, rider, *, name, grid, in_specs, out_specs, out_shape, scratch_shapes, args, vmem=None):
    n_in, n_out, n_scr = len(in_specs), len(out_specs), len(scratch_shapes)
    r_in, r_out = (len(rider.inputs), len(rider.out_shape)) if rider is not None else (0, 0)
    own_barrier = getattr(rider, "collective_id", None) is not None

    def wrapped(*refs):
        ins, refs = refs[:n_in], refs[n_in:]
        rins, refs = refs[:r_in], refs[r_in:]
        outs, refs = refs[:n_out], refs[n_out:]
        routs, refs = refs[:r_out], refs[r_out:]
        scr, rscr = refs[:n_scr], refs[n_scr:]
        ids = [pl.program_id(d) for d in range(len(grid))]
        first = functools.reduce(jnp.logical_and, [i == 0 for i in ids], True)
        last = functools.reduce(jnp.logical_and, [i == g - 1 for i, g in zip(ids, grid)], True)

        def begin():
            if own_barrier:
                _peer_barrier(rider.barrier_peers())
            rider.start(rins, routs, rscr)

        if rider is not None and grid:
            pl.when(first)(begin)
        elif rider is not None:
            begin()
        if body is not None:
            body(*ins, *outs, *scr)
        if rider is not None and grid:
            pl.when(last)(lambda: rider.finish(rins, routs, rscr))
        elif rider is not None:
            rider.finish(rins, routs, rscr)

    kw = dict(grid=grid) if grid else {}
    aliases = getattr(rider, "aliases", {})
    if aliases:
        kw["input_output_aliases"] = {n_in + i: n_out + o for i, o in aliases.items()}
    if grid or vmem is not None or own_barrier:
        kw["compiler_params"] = _params(("arbitrary",) * len(grid) if grid else None, vmem,
                                        rider.collective_id if own_barrier else None)
    res = _call(
        wrapped, name=name, in_specs=list(in_specs) + _any_specs(r_in), out_specs=list(out_specs) + _any_specs(r_out),
        out_shape=list(out_shape) + (rider.out_shape if rider is not None else []),
        scratch_shapes=list(scratch_shapes) + (rider.scratch if rider is not None else []), **kw,
    )(*args, *(rider.inputs if rider is not None else []))
    return list(res[:n_out]), list(res[n_out:])


def _pick_rows(r, target=256):
    best = None
    for d in range(16, min(r, target) + 1, 16):
        if r % d == 0:
            best = d
    return r if best is None else best


def pair_sum(g, got, *, name):
    _, _, R, C = g.shape
    tr = _pick_rows(R)

    def body(g_ref, got_ref, o_ref):
        mine = jnp.where(lax.axis_index("c") == 0, g_ref[:, 0], g_ref[:, 1])
        o_ref[...] = (mine.astype(F32) + got_ref[...].astype(F32)).astype(o_ref.dtype)

    return _call(
        body, name=name, grid=(R // tr,),
        in_specs=[pl.BlockSpec((4, 2, tr, C), lambda i: (0, 0, i, 0)), pl.BlockSpec((4, tr, C), lambda i: (0, i, 0))],
        out_specs=pl.BlockSpec((4, tr, C), lambda i: (0, i, 0)),
        out_shape=jax.ShapeDtypeStruct((4, R, C), g.dtype),
        compiler_params=_params(("parallel",)),
    )(g, got)


def chip_sum_adamw(p, got, w, m, v, *, name):
    _, R, C = p.shape
    assert w.shape == (1, R, C), (name, w.shape, p.shape)
    tr = _pick_rows(R)

    def body(p_ref, got_ref, w_ref, m_ref, v_ref, g_ref, d_ref, mo_ref, vo_ref):
        my_chip = 2 * lax.axis_index("x") + lax.axis_index("y")
        g = jnp.zeros((tr, C), F32)
        for j in range(4):
            g = g + jnp.where(my_chip == j, p_ref[j], got_ref[j]).astype(F32)
        g_ref[0] = g
        d_ref[0], mo_ref[0], vo_ref[0] = _adamw_math(w_ref[0], g, m_ref[0], v_ref[0])

    part = pl.BlockSpec((4, tr, C), lambda i: (0, i, 0))
    spec = pl.BlockSpec((1, tr, C), lambda i: (0, i, 0))
    return _call(
        body, name=name, grid=(R // tr,), in_specs=[part, part, spec, spec, spec], out_specs=[spec] * 4,
        out_shape=[jax.ShapeDtypeStruct((1, R, C), F32)] * 4,
        compiler_params=_params(("parallel",)),
    )(p, got, w, m, v)


def rows_sum(g8, *, name):
    _, R, C = g8.shape

    def body(g_ref, o_ref):
        acc = g_ref[0]
        for j in range(1, N_DEV):
            acc = acc + g_ref[j]
        o_ref[...] = acc

    return _call(body, name=name, out_shape=jax.ShapeDtypeStruct((R, C), F32))(g8)


def _adamw_math(w, g, m, v):
    m = ADAM_B1 * m + (1.0 - ADAM_B1) * g
    v = ADAM_B2 * v + (1.0 - ADAM_B2) * (g * g)
    m_hat = m / (1.0 - ADAM_B1 ** ADAM_STEP)
    v_hat = v / (1.0 - ADAM_B2 ** ADAM_STEP)
    delta = -ADAM_LR * (m_hat / (jnp.sqrt(v_hat) + ADAM_EPS) + ADAM_WD * w)
    return delta, m, v


def to_bf16(xs, *, name):
    def body(*refs):
        for x_ref, o_ref in zip(refs[:len(xs)], refs[len(xs):]):
            o_ref[...] = x_ref[...].astype(BF16)

    total = sum(_nbytes(v.shape, F32) + _nbytes(v.shape, BF16) for v in xs)
    return _call(body, name=name, out_shape=[jax.ShapeDtypeStruct(v.shape, BF16) for v in xs],
                 compiler_params=_params(vmem=2 * total + (4 << 20)))(*xs)


def adamw_small(wgmv, *, name):
    n = len(wgmv)

    def body(*refs):
        ins, outs = refs[:4 * n], refs[4 * n:]
        for a in range(n):
            w_ref, g_ref, m_ref, v_ref = ins[4 * a:4 * a + 4]
            d, mn, vn = _adamw_math(w_ref[...], g_ref[...], m_ref[...], v_ref[...])
            outs[3 * a][...] = d
            outs[3 * a + 1][...] = mn
            outs[3 * a + 2][...] = vn

    flat = [t for tup in wgmv for t in tup]
    res = _call(
        body, name=name,
        out_shape=[jax.ShapeDtypeStruct(tup[0].shape, F32) for tup in wgmv for _ in range(3)],
    )(*flat)
    return [tuple(res[3 * a:3 * a + 3]) for a in range(n)]


BIG = ("w_in", "w_out", "w_mq", "w_mkv", "w_mo", "w_gu", "w_down")
COL_SHARDED = ("w_in", "w_mkv", "w_gu")
SMALL = ("g_mix", "b_f", "conv_w", "conv_b", "ln_g", "ln_b", "g_x", "g_mem", "g_ffn", "g_final")


def _full_from_gathered(n, blk):
    _, rr, cc = blk.shape
    if n in COL_SHARDED:
        return join_columns(blk, name="join_" + n)
    return blk.reshape(N_DEV * rr, cc)


def join_columns(blk, *, name, tr=256):
    n, R, w = blk.shape
    tr = min(tr, R)

    def body(b_ref, o_ref):
        for k in range(n):
            o_ref[:, pl.ds(k * w, w)] = b_ref[k]

    return _call(
        body, name=name, grid=(R // tr,),
        in_specs=[pl.BlockSpec((n, tr, w), lambda r: (0, r, 0))],
        out_specs=pl.BlockSpec((tr, n * w), lambda r: (r, 0)),
        out_shape=jax.ShapeDtypeStruct((R, n * w), blk.dtype),
        compiler_params=_params(("parallel",)),
    )(blk)


def _shards_from_full(n, g):
    pieces = g if isinstance(g, list) else [g]
    rr, cc = pieces[0].shape[0], sum(p.shape[1] for p in pieces)
    if n in COL_SHARDED:
        return split_columns(pieces, name="shards_" + n).reshape(4, 2, rr, cc // N_DEV)
    return pieces[0].reshape(4, 2, rr // N_DEV, cc)


def split_columns(pieces, *, name, tr=256):
    R = pieces[0].shape[0]
    w = sum(p.shape[1] for p in pieces) // N_DEV
    tr = min(tr, R)
    moves, c0 = [], 0
    for i, p in enumerate(pieces):
        for k in range(N_DEV):
            lo, hi = max(k * w, c0), min((k + 1) * w, c0 + p.shape[1])
            if lo < hi:
                moves.append((k, i, lo - c0, hi - c0, lo - k * w))
        c0 += p.shape[1]

    def body(*refs):
        o_ref = refs[-1]
        for k, i, lo, hi, off in moves:
            o_ref[k, :, pl.ds(off, hi - lo)] = refs[i][:, pl.ds(lo, hi - lo)]

    return _call(
        body, name=name, grid=(R // tr,),
        in_specs=[pl.BlockSpec((tr, p.shape[1]), lambda r: (r, 0)) for p in pieces],
        out_specs=pl.BlockSpec((N_DEV, tr, w), lambda r: (0, r, 0)),
        out_shape=jax.ShapeDtypeStruct((N_DEV, R, w), pieces[0].dtype),
        compiler_params=_params(("parallel",)),
    )(*pieces)


def _columns(pieces, start, width):
    out, c0 = [], 0
    for p in pieces:
        lo, hi = max(start, c0), min(start + width, c0 + p.shape[1])
        if lo < hi:
            out.append(p[:, lo - c0:hi - c0])
        c0 += p.shape[1]
    return out[0] if len(out) == 1 else jnp.concatenate(out, axis=1)


def _small_layout():
    sizes = dict(g_mix=1024, b_f=8, conv_w=CONV_K * CONV_CH, conv_b=512, ln_g=512, ln_b=512, g_x=1024,
                 g_mem=1024, g_ffn=1024, g_final=1024, loss=1)
    lay, r0 = {}, 0
    for n, sz in sizes.items():
        r = -(-sz // LANES)
        lay[n] = (r0, r, sz)
        r0 += r
    return lay, -(-r0 // 8) * 8


def kernel(x, mem, g_mix, w_in, b_f, conv_w, conv_b, ln_g, ln_b, w_out, g_x, g_mem, w_mq, w_mkv, w_mo, g_ffn, w_gu, w_down, g_final, loss_target, m_g_mix, m_w_in, m_b_f, m_conv_w, m_conv_b, m_ln_g, m_ln_b, m_w_out, m_g_x, m_g_mem, m_w_mq, m_w_mkv, m_w_mo, m_g_ffn, m_w_gu, m_w_down, m_g_final, v_g_mix, v_w_in, v_b_f, v_conv_w, v_conv_b, v_ln_g, v_ln_b, v_w_out, v_g_x, v_g_mem, v_w_mq, v_w_mkv, v_w_mo, v_g_ffn, v_w_gu, v_w_down, v_g_final):
    names = ["g_mix", "w_in", "b_f", "conv_w", "conv_b", "ln_g", "ln_b", "w_out", "g_x", "g_mem", "w_mq",
             "w_mkv", "w_mo", "g_ffn", "w_gu", "w_down", "g_final"]
    W = dict(zip(names, [g_mix, w_in, b_f, conv_w, conv_b, ln_g, ln_b, w_out, g_x, g_mem, w_mq, w_mkv, w_mo,
                         g_ffn, w_gu, w_down, g_final]))
    Mo = dict(zip(names, [m_g_mix, m_w_in, m_b_f, m_conv_w, m_conv_b, m_ln_g, m_ln_b, m_w_out, m_g_x, m_g_mem,
                          m_w_mq, m_w_mkv, m_w_mo, m_g_ffn, m_w_gu, m_w_down, m_g_final]))
    Vo = dict(zip(names, [v_g_mix, v_w_in, v_b_f, v_conv_w, v_conv_b, v_ln_g, v_ln_b, v_w_out, v_g_x, v_g_mem,
                          v_w_mq, v_w_mkv, v_w_mo, v_g_ffn, v_w_gu, v_w_down, v_g_final]))
    dev = 4 * lax.axis_index("x") + 2 * lax.axis_index("y") + lax.axis_index("c")

    two = lambda a: a.reshape(-1, a.shape[-1])
    cw_shard = jnp.pad(two(conv_w), ((0, HALO - CONV_K), (0, 0)))
    sp = dict(g_mix=g_mix, b_f=b_f, conv_b=conv_b, ln_g=ln_g, ln_b=ln_b, g_x=g_x, g_mem=g_mem,
              g_ffn=g_ffn, g_final=g_final)
    shards = to_bf16([two(W[n]) for n in ("w_in",) + LATE], name="cast_shards")
    loss_blk, grad_x, gs, reduced = local_step(x, mem, loss_target, sp, [shards[0], cw_shard], shards[1:])

    lay, rs = _small_layout()
    small = {**{n: gs[n] for n in SMALL}, "loss": loss_blk[:, :1]}
    parts = []
    for n, (r0, r, sz) in lay.items():
        flat = small[n].reshape(-1).astype(F32)
        parts.append(jnp.pad(flat, (0, r * LANES - sz)).reshape(r, LANES))
    spack = jnp.concatenate(parts, axis=0)
    spack = jnp.pad(spack, ((0, rs - spack.shape[0]), (0, 0)))
    ssum = rows_sum(all_gather([spack], name="ag_small")[0], name="small_sum")
    gsmall = {n: ssum[r0:r0 + r].reshape(-1)[:sz] for n, (r0, r, sz) in lay.items()}
    loss = gsmall["loss"].reshape(())

    grads, delta, new_m, new_v = {}, {}, {}, {}
    for n in BIG:
        p, o = reduced[n]
        grads[n], delta[n], new_m[n], new_v[n] = chip_sum_adamw(p, o, W[n], Mo[n], Vo[n], name="adamw_" + n)
    for n in SMALL:
        if n == "conv_w":
            full = gsmall[n].reshape(CONV_K, CONV_CH)
            ncol = conv_w.shape[-1]
            grads[n] = lax.dynamic_slice(full, (0, dev * ncol), (CONV_K, ncol)).reshape(conv_w.shape)
        else:
            grads[n] = gsmall[n].reshape(W[n].shape)
    upd = adamw_small([(two(W[n]), two(grads[n]), two(Mo[n]), two(Vo[n])) for n in SMALL], name="adamw_small")
    for n, (d, mn, vn) in zip(SMALL, upd):
        shp = W[n].shape
        delta[n], new_m[n], new_v[n] = d.reshape(shp), mn.reshape(shp), vn.reshape(shp)
    return (loss, grad_x, *[grads[n] for n in names], *[delta[n] for n in names],
            *[new_m[n] for n in names], *[new_v[n] for n in names])
```

```python
import functools
import math

import jax
import jax.numpy as jnp
from jax import lax
from jax.experimental import pallas as pl
from jax.experimental.pallas import tpu as pltpu

F32 = jnp.float32
BF16 = jnp.bfloat16
EPS = 1e-6
N_DEV = 8
CONV_CH = 512
CONV_K = 31
FOX_HEADS = 8
FOX_HEAD_DIM = 64
FOX_W = 512
MEM_HEADS = 4
MEM_HEAD_DIM = 256
HALO = 32
LANES = 128
ADAM_LR, ADAM_B1, ADAM_B2, ADAM_EPS, ADAM_WD, ADAM_STEP = 0.001, 0.9, 0.999, 1e-08, 0.01, 10
NEG = -1e30
VMEM_CAP = 60 * 1024 * 1024
VMEM_FOX_BWD = 56 << 20
VMEM_SWIGLU = 48 << 20
MESH = pl.DeviceIdType.MESH


def _call(body, **kw):
    kw["out_shape"] = jax.tree.map(lambda s: pltpu.HBM(s.shape, s.dtype), kw["out_shape"])
    call = pl.pallas_call(body, **kw)
    return lambda *args: call(*[pltpu.with_memory_space_constraint(a, pltpu.HBM) for a in args])


def _params(sem=None, vmem=None, collective_id=None):
    kw = {} if collective_id is None else {"collective_id": collective_id}
    if sem is not None:
        kw["dimension_semantics"] = sem
    if vmem is not None:
        kw["vmem_limit_bytes"] = int(min(VMEM_CAP, vmem))
    return pltpu.CompilerParams(**kw)


def _nbytes(shape, dtype):
    return math.prod(shape) * jnp.dtype(dtype).itemsize


def _pick(n, target):
    best = None
    for d in range(LANES, min(n, target) + 1, LANES):
        if n % d == 0:
            best = d
    return n if best is None else best


class RowEpilogue:
    def __init__(self, fn, ins, outs):
        self.fn, self.ins, self.outs = fn, list(ins), list(outs)


def matmul(a, b, *, tb=False, out_dtype=None, res=None, tm=512, tn=512, name, rider=None, b_blk=None, post=None):
    a_list = list(a) if isinstance(a, (list, tuple)) else [a]
    b_list = list(b) if isinstance(b, (list, tuple)) else [b]
    n = len(a_list)
    assert len(b_list) == n
    M = a_list[0].shape[0]
    N = b_list[0].shape[0] if tb else b_list[0].shape[1]
    tm, tn = _pick(M, tm), _pick(N, tn)
    assert M % tm == 0 and N % tn == 0, (name, M, N, tm, tn)
    dn = (((1,), (1 if tb else 0,)), ((), ()))

    n_res = int(res is not None)
    n_pin = len(post.ins) if post is not None else 0

    def body(*refs):
        acc = None
        for a_ref, b_ref in zip(refs[:n], refs[n:2 * n]):
            p = lax.dot_general(a_ref[...].astype(BF16), b_ref[...].astype(BF16), dn, preferred_element_type=F32)
            acc = p if acc is None else acc + p
        if res is not None:
            acc = acc + refs[2 * n][...].astype(F32)
        if post is None:
            refs[-1][...] = acc.astype(out_dtype)
            return
        first_in = 2 * n + n_res
        vals = post.fn(acc, *[r[...] for r in refs[first_in:first_in + n_pin]])
        for (dtype, kind), o_ref, val in zip(post.outs, refs[first_in + n_pin:], vals):
            if kind in ("row", "rowT"):
                o_ref[...] = val.astype(dtype)
            else:
                @pl.when(pl.program_id(0) == 0)
                def _(o_ref=o_ref):
                    o_ref[...] = jnp.zeros_like(o_ref)

                o_ref[...] += jnp.broadcast_to(val, o_ref.shape).astype(dtype)

    o_spec = pl.BlockSpec((tm, tn), lambda i, j: (i, j))
    in_specs, est = [], 2 * _nbytes((tm, tn), out_dtype or F32) + 2 * _nbytes((tm, tn), F32)
    for av in a_list:
        assert av.shape[0] == M
        in_specs.append(pl.BlockSpec((tm, av.shape[1]), lambda i, j: (i, 0)))
        est += (2 * jnp.dtype(av.dtype).itemsize + (av.dtype != BF16) * 2) * tm * av.shape[1]
    for idx, (av, bv) in enumerate(zip(a_list, b_list)):
        K = av.shape[1]
        kb = 0 if b_blk is None else b_blk[idx]
        assert bv.shape[0 if tb else 1] == N and bv.shape[1 if tb else 0] >= (kb + 1) * K, (name, av.shape, bv.shape)
        assert b_blk is not None or bv.shape[1 if tb else 0] == K, (name, av.shape, bv.shape)
        in_specs.append(pl.BlockSpec((tn, K), lambda i, j, kb=kb: (j, kb)) if tb
                        else pl.BlockSpec((K, tn), lambda i, j, kb=kb: (kb, j)))
        est += (2 * jnp.dtype(bv.dtype).itemsize + (bv.dtype != BF16) * 2) * tn * K
    args = a_list + b_list
    if res is not None:
        in_specs.append(o_spec)
        args.append(res)
        est += 2 * _nbytes((tm, tn), res.dtype)
    if post is None:
        out_specs, out_shape = [o_spec], [jax.ShapeDtypeStruct((M, N), out_dtype)]
    else:
        assert tn == N, (name, tn, N)
        row = pl.BlockSpec((tm, N), lambda i, j: (i, 0))
        for arr, kind in post.ins:
            in_specs.append(row if kind == "row" else pl.BlockSpec((1, N), lambda i, j: (0, 0)))
            args.append(arr)
            est += 2 * _nbytes((tm, N), arr.dtype) * (kind == "row")
        specs = {"row": (row, (M, N)), "rowT": (pl.BlockSpec((N, tm), lambda i, j: (0, i)), (N, M)),
                 "vec": (pl.BlockSpec((1, N), lambda i, j: (0, 0)), (1, N)),
                 "lanes": (pl.BlockSpec((1, LANES), lambda i, j: (0, 0)), (1, LANES))}
        out_specs = [specs[kind][0] for _, kind in post.outs]
        out_shape = [jax.ShapeDtypeStruct(specs[kind][1], dtype) for dtype, kind in post.outs]
        est += sum(2 * _nbytes((tm, N), dtype) + _nbytes((tm, N), F32) for dtype, kind in post.outs if kind[:3] == "row")
    outs, rode = hosted_call(
        body, rider, name=name, grid=(M // tm, N // tn), in_specs=in_specs, out_specs=out_specs,
        out_shape=out_shape, scratch_shapes=[], args=args, vmem=est + (8 << 20),
    )
    result = outs[0] if post is None else outs
    return result if rider is None else (result, rode)


def _rms_scale(x):
    return lax.rsqrt(jnp.mean(x * x, axis=-1, keepdims=True) + EPS)


def rmsnorm_fwd(x, g, *, name, tm=512, rider=None):
    T, D = x.shape
    tm = min(tm, T)

    def body(x_ref, g_ref, o_ref, ot_ref):
        xv = x_ref[...]
        h = xv * _rms_scale(xv) * g_ref[...]
        o_ref[...] = h.astype(BF16)
        ot_ref[...] = h.T.astype(BF16)

    (h, h_t), rode = hosted_call(
        body, rider, name=name, grid=(T // tm,),
        in_specs=[pl.BlockSpec((tm, D), lambda i: (i, 0)), pl.BlockSpec((1, D), lambda i: (0, 0))],
        out_specs=[pl.BlockSpec((tm, D), lambda i: (i, 0)), pl.BlockSpec((D, tm), lambda i: (0, i))],
        out_shape=[jax.ShapeDtypeStruct((T, D), BF16), jax.ShapeDtypeStruct((D, T), BF16)],
        scratch_shapes=[], args=(x, g),
    )
    return (h, h_t) if rider is None else (h, h_t, rode)


def _rms_bwd_math(xv, gv, dh):
    r = _rms_scale(xv)
    xh = xv * r
    dg = jnp.sum(dh * xh, axis=0, keepdims=True)
    dxh = dh * gv
    dx = r * (dxh - xh * jnp.mean(dxh * xh, axis=-1, keepdims=True))
    return dx, dg


def rms_fwd_epilogue(g):
    def fn(acc, gv):
        h = acc * _rms_scale(acc) * gv
        return acc, h, h.T
    return RowEpilogue(fn, [(g, "vec")], [(F32, "row"), (BF16, "row"), (BF16, "rowT")])


def rms_bwd_epilogue(x, g, dres, out_dtype=BF16):
    def fn(acc, xv, gv, *dr):
        dx, dg = _rms_bwd_math(xv, gv, acc)
        return (dx + dr[0].astype(F32) if dr else dx), dg
    ins = [(x, "row"), (g, "vec")] + ([(dres, "row")] if dres is not None else [])
    return RowEpilogue(fn, ins, [(out_dtype, "row"), (F32, "vec")])


def loss_epilogue(g, target):
    def fn(acc, gv, tv):
        e = acc * _rms_scale(acc) * gv - tv
        part = 0.5 * jnp.sum(jnp.mean(e * e, axis=-1, keepdims=True), axis=0, keepdims=True)
        dx, dg = _rms_bwd_math(acc, gv, e * (1.0 / acc.shape[-1]))
        return dx, dg, part
    return RowEpilogue(fn, [(g, "vec"), (target, "row")], [(BF16, "row"), (F32, "vec"), (F32, "lanes")])


def _sigmoid(v):
    return 0.5 * jnp.tanh(0.5 * v) + 0.5


def _glu(blk):
    u = blk[:, :CONV_CH].astype(F32)
    gt = blk[:, CONV_CH:].astype(F32)
    return u * _sigmoid(gt)


def _fill_causal_ext(ext, cur_ref, halo_ref, s, ts):
    ext[pl.ds(HALO, ts), :] = _glu(cur_ref[0])
    hal = _glu(halo_ref[0])
    ext[pl.ds(0, HALO), :] = jnp.where(s > 0, hal, 0.0)


SUBLANES = 8


def _make_shifted(ext, sh):
    n = ext.shape[0]
    full = ext[...]
    for r in range(1, SUBLANES):
        sh[r - 1] = pltpu.roll(full, n - r, 0)


def _tap(ext, sh, off, ts):
    r = off % SUBLANES
    return ext[pl.ds(off, ts), :] if r == 0 else sh[r - 1, pl.ds(off - r, ts), :]


def _causal_conv(ext, sh, w_ref, ts):
    acc = jnp.zeros((ts, CONV_CH), F32)
    for j in range(CONV_K):
        acc = acc + _tap(ext, sh, HALO - (CONV_K - 1) + j, ts) * w_ref[pl.ds(j, 1), :]
    return acc


def _ln_stats(y):
    mu = jnp.mean(y, axis=-1, keepdims=True)
    yc = y - mu
    rstd = lax.rsqrt(jnp.mean(yc * yc, axis=-1, keepdims=True) + EPS)
    return yc * rstd, rstd


def _conv_specs(ts, S):
    nh = ts // HALO
    cur = pl.BlockSpec((1, ts, 2 * CONV_CH), lambda b, s: (b, s, 0))
    halo = pl.BlockSpec((1, HALO, 2 * CONV_CH), lambda b, s: (b, jnp.maximum(s * nh - 1, 0), 0))
    w = pl.BlockSpec((HALO, CONV_CH), lambda b, s: (0, 0))
    vec = pl.BlockSpec((1, CONV_CH), lambda b, s: (0, 0))
    return cur, halo, w, vec


def conv_branch_fwd(ug, conv_w, conv_b, ln_g, ln_b, *, name, ts=256, rider=None):
    B, S, _ = ug.shape
    ts = min(ts, S)
    ns = S // ts
    cur, halo, w, vec = _conv_specs(ts, S)

    def body(cur_ref, halo_ref, w_ref, cb_ref, lg_ref, lb_ref, o_ref, ot_ref, y_ref, ext, sh):
        _fill_causal_ext(ext, cur_ref, halo_ref, pl.program_id(1), ts)
        _make_shifted(ext, sh)
        y = _causal_conv(ext, sh, w_ref, ts) + cb_ref[...]
        y_ref[0] = y
        yh, _ = _ln_stats(y)
        ln = yh * lg_ref[...] + lb_ref[...]
        out = ln * _sigmoid(ln)
        o_ref[0] = out.astype(BF16)
        ot_ref[...] = out.T.astype(BF16)

    return hosted_call(
        body, rider, name=name, grid=(B, ns), in_specs=[cur, halo, w, vec, vec, vec],
        out_specs=[pl.BlockSpec((1, ts, CONV_CH), lambda b, s: (b, s, 0)),
                   pl.BlockSpec((CONV_CH, ts), lambda b, s: (0, b * ns + s)),
                   pl.BlockSpec((1, ts, CONV_CH), lambda b, s: (b, s, 0))],
        out_shape=[jax.ShapeDtypeStruct((B, S, CONV_CH), BF16), jax.ShapeDtypeStruct((CONV_CH, B * S), BF16),
                   jax.ShapeDtypeStruct((B, S, CONV_CH), F32)],
        scratch_shapes=[pltpu.VMEM((ts + HALO, CONV_CH), F32),
                        pltpu.VMEM((SUBLANES - 1, ts + HALO, CONV_CH), F32)],
        args=(ug, ug, conv_w, conv_b, ln_g, ln_b),
    )


def conv_branch_bwd_a(ug, y, dcat, ln_g, ln_b, *, name, ts=256):
    B, S, _ = ug.shape
    ts = min(ts, S)
    cur, halo, _, vec = _conv_specs(ts, S)
    tile = pl.BlockSpec((1, ts, CONV_CH), lambda b, s: (b, s, 0))

    def body(cur_ref, halo_ref, y_ref, d_ref, lg_ref, lb_ref, dy_ref, dw_ref, dv_ref, ext, sh):
        _fill_causal_ext(ext, cur_ref, halo_ref, pl.program_id(1), ts)
        _make_shifted(ext, sh)
        yh, rstd = _ln_stats(y_ref[0])
        lg = lg_ref[...]
        ln = yh * lg + lb_ref[...]
        sg = _sigmoid(ln)
        dln = d_ref[0].astype(F32) * (sg * (1.0 + ln * (1.0 - sg)))
        dyh = dln * lg
        dy = rstd * (dyh - jnp.mean(dyh, axis=-1, keepdims=True)
                     - yh * jnp.mean(dyh * yh, axis=-1, keepdims=True))
        dy_ref[0] = dy

        @pl.when((pl.program_id(0) == 0) & (pl.program_id(1) == 0))
        def _():
            dw_ref[...] = jnp.zeros_like(dw_ref)
            dv_ref[...] = jnp.zeros_like(dv_ref)

        dv_ref[pl.ds(0, 1), :] += jnp.sum(dy, axis=0, keepdims=True)
        dv_ref[pl.ds(1, 1), :] += jnp.sum(dln * yh, axis=0, keepdims=True)
        dv_ref[pl.ds(2, 1), :] += jnp.sum(dln, axis=0, keepdims=True)
        for j in range(CONV_K):
            tap = _tap(ext, sh, HALO - (CONV_K - 1) + j, ts)
            dw_ref[pl.ds(j, 1), :] += jnp.sum(dy * tap, axis=0, keepdims=True)

    return _call(
        body, name=name, grid=(B, S // ts),
        in_specs=[cur, halo, tile, tile, vec, vec],
        out_specs=[tile,
                   pl.BlockSpec((HALO, CONV_CH), lambda b, s: (0, 0)),
                   pl.BlockSpec((8, CONV_CH), lambda b, s: (0, 0))],
        out_shape=[jax.ShapeDtypeStruct((B, S, CONV_CH), F32),
                   jax.ShapeDtypeStruct((HALO, CONV_CH), F32),
                   jax.ShapeDtypeStruct((8, CONV_CH), F32)],
        scratch_shapes=[pltpu.VMEM((ts + HALO, CONV_CH), F32),
                        pltpu.VMEM((SUBLANES - 1, ts + HALO, CONV_CH), F32)],
        compiler_params=_params(("arbitrary", "arbitrary")),
    )(ug, ug, y, dcat, ln_g, ln_b)


def conv_branch_bwd_b(ug, dy, conv_w, *, name, ts=256):
    B, S, _ = ug.shape
    ts = min(ts, S)
    nh, n_halo = ts // HALO, S // HALO

    def body(cur_ref, dy_ref, nxt_ref, w_ref, o_ref, ext, sh):
        last = pl.program_id(1) == pl.num_programs(1) - 1
        ext[pl.ds(0, ts), :] = dy_ref[0]
        ext[pl.ds(ts, HALO), :] = jnp.where(last, 0.0, nxt_ref[0])
        _make_shifted(ext, sh)
        da = jnp.zeros((ts, CONV_CH), F32)
        for j in range(CONV_K):
            da = da + _tap(ext, sh, CONV_K - 1 - j, ts) * w_ref[pl.ds(j, 1), :]
        blk = cur_ref[0]
        u = blk[:, :CONV_CH].astype(F32)
        sg = _sigmoid(blk[:, CONV_CH:].astype(F32))
        o_ref[0, :, :CONV_CH] = (da * sg).astype(BF16)
        o_ref[0, :, CONV_CH:] = (da * u * sg * (1.0 - sg)).astype(BF16)

    return _call(
        body, name=name, grid=(B, S // ts),
        in_specs=[pl.BlockSpec((1, ts, 2 * CONV_CH), lambda b, s: (b, s, 0)),
                  pl.BlockSpec((1, ts, CONV_CH), lambda b, s: (b, s, 0)),
                  pl.BlockSpec((1, HALO, CONV_CH), lambda b, s: (b, jnp.minimum((s + 1) * nh, n_halo - 1), 0)),
                  pl.BlockSpec((HALO, CONV_CH), lambda b, s: (0, 0))],
        out_specs=pl.BlockSpec((1, ts, 2 * CONV_CH), lambda b, s: (b, s, 0)),
        out_shape=jax.ShapeDtypeStruct((B, S, 2 * CONV_CH), BF16),
        scratch_shapes=[pltpu.VMEM((ts + HALO, CONV_CH), F32),
                        pltpu.VMEM((SUBLANES - 1, ts + HALO, CONV_CH), F32)],
        compiler_params=_params(("parallel", "parallel")),
    )(ug, dy, dy, conv_w)


def _tri(n, lower):
    r = lax.broadcasted_iota(jnp.int32, (n, n), 0)
    c = lax.broadcasted_iota(jnp.int32, (n, n), 1)
    return ((r >= c) if lower else (r <= c)).astype(F32)


def _dot_hi(a, b, dn):
    return lax.dot_general(a, b, dn, precision=lax.Precision.HIGHEST, preferred_element_type=F32)


NN = (((1,), (0,)), ((), ()))
NT = (((1,), (1,)), ((), ()))
TN = (((0,), (0,)), ((), ()))


def _log_sigmoid(v):
    e = jnp.exp(-jnp.abs(v))
    log1p_e = jnp.where(e < 1e-3, e * (1.0 - 0.5 * e), jnp.log(1.0 + e))
    return jnp.minimum(v, 0.0) - log1p_e


def fgate_fwd(h, w_f, b_f, *, name, ts=256, rider=None):
    B, S, D = h.shape
    ts = min(ts, S)

    def body(h_ref, w_ref, b_ref, f_ref, cc_ref, cr_ref, carry):
        @pl.when(pl.program_id(1) == 0)
        def _():
            carry[...] = jnp.zeros_like(carry)

        f = jnp.dot(h_ref[0], w_ref[...], preferred_element_type=F32)
        f_ref[0] = f
        logf = _log_sigmoid(f + b_ref[...])
        c = _dot_hi(_tri(ts, True), logf, NN) + carry[pl.ds(0, 1), :]
        cc_ref[0] = c
        carry[pl.ds(0, 1), :] = c[ts - 1:ts, :]
        cr_ref[0] = c.T

    return hosted_call(
        body, rider, name=name, grid=(B, S // ts),
        in_specs=[pl.BlockSpec((1, ts, D), lambda b, s: (b, s, 0)),
                  pl.BlockSpec((D, LANES), lambda b, s: (0, 0)),
                  pl.BlockSpec((1, LANES), lambda b, s: (0, 0))],
        out_specs=[pl.BlockSpec((1, ts, LANES), lambda b, s: (b, s, 0)),
                   pl.BlockSpec((1, ts, LANES), lambda b, s: (b, s, 0)),
                   pl.BlockSpec((1, LANES, ts), lambda b, s: (b, 0, s))],
        out_shape=[jax.ShapeDtypeStruct((B, S, LANES), F32), jax.ShapeDtypeStruct((B, S, LANES), F32),
                   jax.ShapeDtypeStruct((B, LANES, S), F32)],
        scratch_shapes=[pltpu.VMEM((8, LANES), F32)],
        args=(h, w_f, b_f),
    )


def fgate_bwd(dc, f, b_f, *, name, ts=256):
    B, S, _ = f.shape
    P = dc.shape[1]
    ts = min(ts, S)
    ns = S // ts

    def body(dc_ref, f_ref, b_ref, df_ref, db_ref, carry):
        @pl.when(pl.program_id(1) == 0)
        def _():
            carry[...] = jnp.zeros_like(carry)

        @pl.when((pl.program_id(0) == 0) & (pl.program_id(1) == 0))
        def _():
            db_ref[...] = jnp.zeros_like(db_ref)

        dc_t = dc_ref[0, 0]
        for j in range(1, P):
            dc_t = dc_t + dc_ref[0, j]
        dlogf = _dot_hi(_tri(ts, False), dc_t, NN) + carry[pl.ds(0, 1), :]
        carry[pl.ds(0, 1), :] = dlogf[0:1, :]
        df = dlogf * _sigmoid(-(f_ref[0] + b_ref[...]))
        df_ref[0] = df.astype(BF16)
        db_ref[...] += jnp.sum(df, axis=0, keepdims=True)

    return _call(
        body, name=name, grid=(B, ns),
        in_specs=[pl.BlockSpec((1, P, ts, LANES), lambda b, s: (b, 0, ns - 1 - s, 0)),
                  pl.BlockSpec((1, ts, LANES), lambda b, s: (b, ns - 1 - s, 0)),
                  pl.BlockSpec((1, LANES), lambda b, s: (0, 0))],
        out_specs=[pl.BlockSpec((1, ts, LANES), lambda b, s: (b, ns - 1 - s, 0)),
                   pl.BlockSpec((1, LANES), lambda b, s: (0, 0))],
        out_shape=[jax.ShapeDtypeStruct((B, S, LANES), BF16), jax.ShapeDtypeStruct((1, LANES), F32)],
        scratch_shapes=[pltpu.VMEM((8, LANES), F32)],
        compiler_params=_params(("arbitrary", "arbitrary")),
    )(dc, f, b_f)


def _lane_pick(tile, idx):
    lane = lax.broadcasted_iota(jnp.int32, tile.shape, 1)
    return jnp.sum(jnp.where(lane == idx, tile, 0.0), axis=-1, keepdims=True)


FOX_T = 512


def _fox_heads(q, cc_ref, p):
    lane = lax.broadcasted_iota(jnp.int32, q.shape, 1)
    qs = q * (1.0 / math.sqrt(FOX_HEAD_DIM))
    qhs = [jnp.where((lane < FOX_HEAD_DIM) == (hh == 0), qs, jnp.zeros_like(qs)) for hh in range(2)]
    crefs = [_lane_pick(cc_ref[0, pl.ds(0, 1), :], 2 * p + hh) for hh in range(2)]
    return qhs, crefs


def _lane_bcast(col, width):
    tile = jnp.broadcast_to(col, (col.shape[0], LANES))
    return jnp.concatenate([tile] * (width // LANES), axis=1)


def _fold_lanes(x, op):
    out = x[:, :LANES]
    for j in range(1, x.shape[1] // LANES):
        out = op(out, x[:, j * LANES:(j + 1) * LANES])
    return out


def _causal(t, transposed):
    r = lax.broadcasted_iota(jnp.int32, (t, t), 0)
    c = lax.broadcasted_iota(jnp.int32, (t, t), 1)
    return (r <= c) if transposed else (c <= r)


QKV0 = 8


def fox_fwd(z, c_col, c_row, *, name, rider=None):
    B, S, _ = z.shape
    assert S % FOX_T == 0
    tq, nq = FOX_T, S // FOX_T
    npair = FOX_HEADS // 2

    def body(q_ref, k_ref, v_ref, cc_ref, cr_ref, o_ref, l_ref, ot_ref, s_scr, m_scr, acc_scr):
        p, qi = pl.program_id(1), pl.program_id(2)
        qhs, crefs = _fox_heads(q_ref[0], cc_ref, p)
        lane = lax.broadcasted_iota(jnp.int32, (tq, LANES), 1)
        first = lane < FOX_HEAD_DIM
        for hh in range(2):
            m_scr[hh] = jnp.full((tq, LANES), NEG, F32)
            acc_scr[hh] = jnp.zeros((tq, LANES), F32)

        def logits(kb, diagonal):
            k0 = pl.multiple_of(kb * tq, tq)
            k = k_ref[0, pl.ds(k0, tq), :]
            for hh in range(2):
                s = lax.dot_general(qhs[hh], k, NT, preferred_element_type=F32)
                s = s + (crefs[hh] - cr_ref[0, pl.ds(2 * p + hh, 1), pl.ds(k0, tq)])
                if diagonal:
                    s = jnp.where(_causal(tq, False), s, NEG)
                s_scr[hh, kb] = s
                m_scr[hh] = jnp.maximum(m_scr[hh], _fold_lanes(s, jnp.maximum))

        def sweep1(kb, carry):
            logits(kb, False)
            return carry

        lax.fori_loop(0, qi, sweep1, 0)
        logits(qi, True)
        ms = [jnp.max(m_scr[hh], axis=-1, keepdims=True) for hh in range(2)]
        mbs = [_lane_bcast(ms[hh], tq) for hh in range(2)]

        for hh in range(2):
            m_scr[hh] = jnp.zeros((tq, LANES), F32)

        def weigh(kb, carry):
            k0 = pl.multiple_of(kb * tq, tq)
            v = v_ref[0, pl.ds(k0, tq), :]
            for hh in range(2):
                pr = jnp.exp(s_scr[hh, kb] - mbs[hh])
                m_scr[hh] += _fold_lanes(pr, jnp.add)
                acc_scr[hh] += jnp.dot(pr.astype(BF16), v, preferred_element_type=F32)
            return carry

        lax.fori_loop(0, qi + 1, weigh, 0)
        accs = [acc_scr[hh] for hh in range(2)]
        ls = [jnp.sum(m_scr[hh], axis=-1, keepdims=True) for hh in range(2)]
        out = jnp.where(first, accs[0] / ls[0], accs[1] / ls[1])
        o_ref[0] = out.astype(BF16)
        ot_ref[...] = out.T.astype(BF16)
        l_ref[0, 0] = jnp.where(first, ms[0] + jnp.log(ls[0]), ms[1] + jnp.log(ls[1]))

    return hosted_call(
        body, rider, name=name, grid=(B, npair, nq),
        in_specs=[pl.BlockSpec((1, tq, LANES), lambda b, p, i: (b, i, QKV0 + p)),
                  pl.BlockSpec((1, S, LANES), lambda b, p, i: (b, 0, QKV0 + npair + p)),
                  pl.BlockSpec((1, S, LANES), lambda b, p, i: (b, 0, QKV0 + 2 * npair + p)),
                  pl.BlockSpec((1, tq, LANES), lambda b, p, i: (b, i, 0)),
                  pl.BlockSpec((1, 8, S), lambda b, p, i: (b, 0, 0))],
        out_specs=[pl.BlockSpec((1, tq, LANES), lambda b, p, i: (b, i, p)),
                   pl.BlockSpec((1, 1, tq, LANES), lambda b, p, i: (b, p, i, 0)),
                   pl.BlockSpec((LANES, tq), lambda b, p, i: (p, b * nq + i))],
        out_shape=[jax.ShapeDtypeStruct((B, S, FOX_W), BF16),
                   jax.ShapeDtypeStruct((B, npair, S, LANES), F32),
                   jax.ShapeDtypeStruct((FOX_W, B * S), BF16)],
        scratch_shapes=[pltpu.VMEM((2, nq, tq, tq), F32), pltpu.VMEM((2, tq, LANES), F32),
                        pltpu.VMEM((2, tq, LANES), F32)],
        args=(z, z, z, c_col, c_row),
    )


def fox_bwd_dq(z, dcat, lse, c_col, c_row, *, name, rider=None):
    B, S, _ = z.shape
    tq, nq = FOX_T, S // FOX_T
    npair = FOX_HEADS // 2

    def body(q_ref, k_ref, v_ref, do_ref, l_ref, cc_ref, cr_ref, dq_ref, st_ref, p_scr, dp_scr, dl_scr):
        p, qi = pl.program_id(1), pl.program_id(2)
        qhs, crefs = _fox_heads(q_ref[0], cc_ref, p)
        lane = lax.broadcasted_iota(jnp.int32, (tq, LANES), 1)
        do_b = do_ref[0].astype(BF16)
        dohs = [jnp.where((lane < FOX_HEAD_DIM) == (hh == 0), do_b, jnp.zeros_like(do_b)) for hh in range(2)]
        lses = [l_ref[0, 0, :, pl.ds(hh * FOX_HEAD_DIM, 1)] for hh in range(2)]
        lbs = [_lane_bcast(lses[hh], tq) for hh in range(2)]
        for hh in range(2):
            dl_scr[hh] = jnp.zeros((tq, LANES), F32)

        def probs(kb, diagonal):
            k0 = pl.multiple_of(kb * tq, tq)
            k = k_ref[0, pl.ds(k0, tq), :]
            v = v_ref[0, pl.ds(k0, tq), :]
            for hh in range(2):
                s = lax.dot_general(qhs[hh], k, NT, preferred_element_type=F32)
                s = s + (crefs[hh] - cr_ref[0, pl.ds(2 * p + hh, 1), pl.ds(k0, tq)])
                pr = jnp.exp(s - lbs[hh])
                if diagonal:
                    pr = jnp.where(_causal(tq, False), pr, 0.0)
                dp = lax.dot_general(dohs[hh], v, NT, preferred_element_type=F32)
                pdp = pr * dp
                dl_scr[hh] += _fold_lanes(pdp, jnp.add)
                p_scr[hh, kb] = pr
                dp_scr[hh, kb] = dp

        def first_pass(kb, carry):
            probs(kb, False)
            return carry

        lax.fori_loop(0, qi, first_pass, 0)
        probs(qi, True)

        dls = [jnp.sum(dl_scr[hh], axis=-1, keepdims=True) for hh in range(2)]
        dlbs = [_lane_bcast(dls[hh], tq) for hh in range(2)]

        def second_pass(kb, dq):
            k0 = pl.multiple_of(kb * tq, tq)
            k = k_ref[0, pl.ds(k0, tq), :]
            for hh in range(2):
                ds = p_scr[hh, kb] * (dp_scr[hh, kb] - dlbs[hh])
                kh = jnp.where((lane < FOX_HEAD_DIM) == (hh == 0), k, jnp.zeros_like(k))
                dq = dq + jnp.dot(ds.astype(BF16), kh, preferred_element_type=F32)
            return dq

        dq = lax.fori_loop(0, qi + 1, second_pass, jnp.zeros((tq, LANES), F32))
        dq_ref[0] = (dq * (1.0 / math.sqrt(FOX_HEAD_DIM))).astype(BF16)
        cols = jnp.zeros((tq, LANES), F32)
        for j, col in enumerate([crefs[0] - lses[0], crefs[1] - lses[1], dls[0], dls[1]]):
            cols = jnp.where(lane == j, col, cols)
        st_ref[0, 0] = cols.T[:8]

    return hosted_call(
        body, rider, name=name, grid=(B, npair, nq),
        in_specs=[pl.BlockSpec((1, tq, LANES), lambda b, p, i: (b, i, QKV0 + p)),
                  pl.BlockSpec((1, S, LANES), lambda b, p, i: (b, 0, QKV0 + npair + p)),
                  pl.BlockSpec((1, S, LANES), lambda b, p, i: (b, 0, QKV0 + 2 * npair + p)),
                  pl.BlockSpec((1, tq, LANES), lambda b, p, i: (b, i, npair + p)),
                  pl.BlockSpec((1, 1, tq, LANES), lambda b, p, i: (b, p, i, 0)),
                  pl.BlockSpec((1, tq, LANES), lambda b, p, i: (b, i, 0)),
                  pl.BlockSpec((1, 8, S), lambda b, p, i: (b, 0, 0))],
        out_specs=[pl.BlockSpec((1, tq, LANES), lambda b, p, i: (b, i, p)),
                   pl.BlockSpec((1, 1, 8, tq), lambda b, p, i: (b, p, 0, i))],
        out_shape=[jax.ShapeDtypeStruct((B, S, FOX_W), BF16), jax.ShapeDtypeStruct((B, npair, 8, S), F32)],
        scratch_shapes=[pltpu.VMEM((2, nq, tq, tq), F32), pltpu.VMEM((2, nq, tq, tq), F32),
                        pltpu.VMEM((2, tq, LANES), F32)],
        args=(z, z, z, dcat, lse, c_col, c_row), vmem=VMEM_FOX_BWD,
    )


def fox_bwd_dkdv(z, dcat, stats, c_col, *, name, rider=None):
    B, S, _ = z.shape
    tk, nq = FOX_T, S // FOX_T
    npair = FOX_HEADS // 2
    inv = 1.0 / math.sqrt(FOX_HEAD_DIM)

    def body(q_ref, k_ref, v_ref, do_ref, st_ref, cc_ref, dk_ref, dv_ref, dc_ref, dk_scr, dv_scr, dc_scr):
        p, kt = pl.program_id(1), pl.program_id(2)
        lane = lax.broadcasted_iota(jnp.int32, (tk, LANES), 1)
        masks = [(lane < FOX_HEAD_DIM) == (hh == 0) for hh in range(2)]
        k = k_ref[0]
        v = v_ref[0]
        khs = [jnp.where(masks[hh], k, jnp.zeros_like(k)) for hh in range(2)]
        vhs = [jnp.where(masks[hh], v, jnp.zeros_like(v)) for hh in range(2)]
        ccbs = [_lane_bcast(_lane_pick(cc_ref[0], 2 * p + hh), tk) for hh in range(2)]
        dk_scr[...] = jnp.zeros_like(dk_scr)
        dv_scr[...] = jnp.zeros_like(dv_scr)
        dc_scr[...] = jnp.zeros_like(dc_scr)

        def tile(qb, diagonal):
            q0 = pl.multiple_of(qb * tk, tk)
            qs = q_ref[0, pl.ds(q0, tk), :] * inv
            do_b = do_ref[0, pl.ds(q0, tk), :].astype(BF16)
            for hh in range(2):
                st = lax.dot_general(khs[hh], qs, NT, preferred_element_type=F32)
                pr = jnp.exp(st - ccbs[hh] + st_ref[0, 0, pl.ds(hh, 1), pl.ds(q0, tk)])
                if diagonal:
                    pr = jnp.where(_causal(tk, True), pr, 0.0)
                dp = lax.dot_general(vhs[hh], do_b, NT, preferred_element_type=F32)
                ds = pr * (dp - st_ref[0, 0, pl.ds(2 + hh, 1), pl.ds(q0, tk)])
                dv_scr[...] += jnp.dot(pr.astype(BF16), jnp.where(masks[hh], do_b, jnp.zeros_like(do_b)),
                                       preferred_element_type=F32)
                dk_scr[...] += jnp.dot(ds.astype(BF16), jnp.where(masks[hh], qs, jnp.zeros_like(qs)),
                                       preferred_element_type=F32)
                dc_scr[hh] -= _fold_lanes(ds, jnp.add)

        def later(qb, carry):
            tile(qb, False)
            return carry

        tile(kt, True)
        lax.fori_loop(kt + 1, nq, later, 0)
        dk_ref[0] = dk_scr[...].astype(BF16)
        dv_ref[0] = dv_scr[...].astype(BF16)
        dcs = [jnp.sum(dc_scr[hh], axis=-1, keepdims=True) for hh in range(2)]
        dc_ref[0, 0] = jnp.where(lane == 2 * p, dcs[0], jnp.where(lane == 2 * p + 1, dcs[1], 0.0))

    full = lambda col: pl.BlockSpec((1, S, LANES), col)
    tile_spec = lambda col: pl.BlockSpec((1, tk, LANES), col)
    return hosted_call(
        body, rider, name=name, grid=(B, npair, nq),
        in_specs=[full(lambda b, p, t: (b, 0, QKV0 + p)),
                  tile_spec(lambda b, p, t: (b, t, QKV0 + npair + p)),
                  tile_spec(lambda b, p, t: (b, t, QKV0 + 2 * npair + p)),
                  full(lambda b, p, t: (b, 0, npair + p)),
                  pl.BlockSpec((1, 1, 8, S), lambda b, p, t: (b, p, 0, 0)),
                  tile_spec(lambda b, p, t: (b, t, 0))],
        out_specs=[tile_spec(lambda b, p, t: (b, t, p)), tile_spec(lambda b, p, t: (b, t, p)),
                   pl.BlockSpec((1, 1, tk, LANES), lambda b, p, t: (b, p, t, 0))],
        out_shape=[jax.ShapeDtypeStruct((B, S, FOX_W), BF16)] * 2
        + [jax.ShapeDtypeStruct((B, npair, S, LANES), F32)],
        scratch_shapes=[pltpu.VMEM((tk, LANES), F32), pltpu.VMEM((tk, LANES), F32),
                        pltpu.VMEM((2, tk, LANES), F32)],
        args=(z, z, z, dcat, stats, c_col),
    )


def xattn_fwd(qm, kv, *, name, tq=512):
    B, S, D = qm.shape
    M = kv.shape[1]
    tq = min(tq, S)
    inv = 1.0 / math.sqrt(MEM_HEAD_DIM)

    nq = S // tq

    def body(q_ref, kv_ref, o_ref, ot_ref):
        for h in range(MEM_HEADS):
            c0 = h * MEM_HEAD_DIM
            qh = q_ref[0, :, c0:c0 + MEM_HEAD_DIM]
            kh = kv_ref[0, :, c0:c0 + MEM_HEAD_DIM]
            vh = kv_ref[0, :, D + c0:D + c0 + MEM_HEAD_DIM]
            s = lax.dot_general(qh, kh, NT, preferred_element_type=F32) * inv
            e = jnp.exp(s - jnp.max(s, axis=-1, keepdims=True))
            o = jnp.dot(e.astype(BF16), vh, preferred_element_type=F32) / jnp.sum(e, axis=-1, keepdims=True)
            o_ref[0, :, c0:c0 + MEM_HEAD_DIM] = o.astype(BF16)
            ot_ref[c0:c0 + MEM_HEAD_DIM, :] = o.T.astype(BF16)

    return _call(
        body, name=name, grid=(B, nq),
        in_specs=[pl.BlockSpec((1, tq, D), lambda b, i: (b, i, 0)),
                  pl.BlockSpec((1, M, 2 * D), lambda b, i: (b, 0, 0))],
        out_specs=[pl.BlockSpec((1, tq, D), lambda b, i: (b, i, 0)),
                   pl.BlockSpec((D, tq), lambda b, i: (0, b * nq + i))],
        out_shape=[jax.ShapeDtypeStruct((B, S, D), BF16), jax.ShapeDtypeStruct((D, B * S), BF16)],
        compiler_params=_params(("parallel", "parallel")),
    )(qm, kv)


def xattn_bwd(qm, kv, do, *, name, tq=512):
    B, S, D = qm.shape
    M = kv.shape[1]
    tq = min(tq, S)
    inv = 1.0 / math.sqrt(MEM_HEAD_DIM)

    def body(q_ref, kv_ref, do_ref, dq_ref, dkv_ref):
        @pl.when(pl.program_id(1) == 0)
        def _():
            dkv_ref[...] = jnp.zeros_like(dkv_ref)

        for h in range(MEM_HEADS):
            c0 = h * MEM_HEAD_DIM
            qh = q_ref[0, :, c0:c0 + MEM_HEAD_DIM]
            kh = kv_ref[0, :, c0:c0 + MEM_HEAD_DIM]
            vh = kv_ref[0, :, D + c0:D + c0 + MEM_HEAD_DIM]
            doh = do_ref[0, :, c0:c0 + MEM_HEAD_DIM]
            s = lax.dot_general(qh, kh, NT, preferred_element_type=F32) * inv
            e = jnp.exp(s - jnp.max(s, axis=-1, keepdims=True))
            pr = e / jnp.sum(e, axis=-1, keepdims=True)
            dp = lax.dot_general(doh, vh, NT, preferred_element_type=F32)
            ds = pr * (dp - jnp.sum(pr * dp, axis=-1, keepdims=True))
            ds_b = ds.astype(BF16)
            dq_ref[0, :, c0:c0 + MEM_HEAD_DIM] = (jnp.dot(ds_b, kh, preferred_element_type=F32) * inv).astype(BF16)
            dkv_ref[0, :, c0:c0 + MEM_HEAD_DIM] += lax.dot_general(ds_b, qh, TN, preferred_element_type=F32) * inv
            dkv_ref[0, :, D + c0:D + c0 + MEM_HEAD_DIM] += lax.dot_general(
                pr.astype(BF16), doh, TN, preferred_element_type=F32)

    row = pl.BlockSpec((1, tq, D), lambda b, i: (b, i, 0))
    kvs = pl.BlockSpec((1, M, 2 * D), lambda b, i: (b, 0, 0))
    return _call(
        body, name=name, grid=(B, S // tq), in_specs=[row, kvs, row], out_specs=[row, kvs],
        out_shape=[jax.ShapeDtypeStruct((B, S, D), BF16), jax.ShapeDtypeStruct((B, M, 2 * D), F32)],
        compiler_params=_params(("parallel", "arbitrary")),
    )(qm, kv, do)


SWIGLU_TN = 2816


def _chunks(n, w=256):
    return [(c0, min(w, n - c0)) for c0 in range(0, n, w)]


def mm_swiglu_fwd(hf, w_gu, *, name, tm=256):
    T, D = hf.shape
    Fh = w_gu.shape[1] // 2
    tm, tn = min(tm, T), SWIGLU_TN
    nj = Fh // tn
    assert Fh % tn == 0 and T % tm == 0

    def body(a_ref, bg_ref, bu_ref, g_ref, u_ref, o_ref, ot_ref):
        a = a_ref[...]
        for c0, cw in _chunks(tn):
            cols = pl.ds(c0, cw)
            g = jnp.dot(a, bg_ref[:, cols], preferred_element_type=F32)
            u = jnp.dot(a, bu_ref[:, cols], preferred_element_type=F32)
            act = g * _sigmoid(g) * u
            g_ref[:, cols] = g.astype(BF16)
            u_ref[:, cols] = u.astype(BF16)
            o_ref[:, cols] = act.astype(BF16)
            ot_ref[cols, :] = act.T.astype(BF16)

    tile = pl.BlockSpec((tm, tn), lambda i, j: (i, j))
    return _call(
        body, name=name, grid=(T // tm, nj),
        in_specs=[pl.BlockSpec((tm, D), lambda i, j: (i, 0)), pl.BlockSpec((D, tn), lambda i, j: (0, j)),
                  pl.BlockSpec((D, tn), lambda i, j: (0, nj + j))],
        out_specs=[tile, tile, tile, pl.BlockSpec((tn, tm), lambda i, j: (j, i))],
        out_shape=[jax.ShapeDtypeStruct((T, Fh), BF16)] * 3 + [jax.ShapeDtypeStruct((Fh, T), BF16)],
        compiler_params=_params(("parallel", "parallel"), VMEM_SWIGLU),
    )(hf, w_gu, w_gu)


def mm_swiglu_bwd(dx, w_down, g, u, *, name, tm=256):
    T, D = dx.shape
    Fh = w_down.shape[0]
    tm, tn = min(tm, T), SWIGLU_TN
    assert Fh % tn == 0 and T % tm == 0

    def body(a_ref, b_ref, g_ref, u_ref, dg_ref, du_ref):
        a = a_ref[...].astype(BF16)
        for c0, cw in _chunks(tn):
            cols = pl.ds(c0, cw)
            d = lax.dot_general(a, b_ref[cols, :], NT, preferred_element_type=F32)
            gv = g_ref[:, cols].astype(F32)
            uv = u_ref[:, cols].astype(F32)
            sg = _sigmoid(gv)
            dg_ref[:, cols] = (d * uv * (sg * (1.0 + gv * (1.0 - sg)))).astype(BF16)
            du_ref[:, cols] = (d * gv * sg).astype(BF16)

    tile = pl.BlockSpec((tm, tn), lambda i, j: (i, j))
    return _call(
        body, name=name, grid=(T // tm, Fh // tn),
        in_specs=[pl.BlockSpec((tm, D), lambda i, j: (i, 0)), pl.BlockSpec((tn, D), lambda i, j: (j, 0)), tile, tile],
        out_specs=[tile, tile],
        out_shape=[jax.ShapeDtypeStruct((T, Fh), BF16)] * 2,
        compiler_params=_params(("parallel", "parallel"), VMEM_SWIGLU),
    )(dx, w_down, g, u)


LATE_MID = ("w_out", "w_mq", "w_mo")
LATE_KV = ("w_mkv",)
LATE_FFN = ("w_gu", "w_down")
LATE = LATE_MID + LATE_KV + LATE_FFN
RS_GROUPS = (("w_gu", "w_down"), ("w_out", "w_mq", "w_mkv", "w_mo"), ("w_in",))


def pair_sums(names, g42, got):
    return {n: pair_sum(g, o, name="rs_pair_sum_" + n) for n, g, o in zip(names, g42, got)}


def local_step(x, mem, target, sp, first_shards, late_shards):
    B, S, D = x.shape
    T = B * S
    M = mem.shape[1]
    row = lambda v: v.reshape(1, -1).astype(F32)
    g_mix, g_x, g_mem, g_ffn, g_final = (row(sp[k]) for k in ("g_mix", "g_x", "g_mem", "g_ffn", "g_final"))
    conv_b, ln_g, ln_b = row(sp["conv_b"]), row(sp["ln_g"]), row(sp["ln_b"])
    b_f = jnp.pad(row(sp["b_f"]), ((0, 0), (0, LANES - FOX_HEADS)))
    n_ug, n_main = 2 * CONV_CH, 2 * CONV_CH + 3 * FOX_W

    x2d = x.reshape(T, D)
    h, h_t, partly = rmsnorm_fwd(x2d, g_mix, name="rms_mix", rider=AllGatherStage1(first_shards))
    w_in8, cw8 = run_rider(AllGatherStage2(partly), name="ag_first_stage2")
    w_in_full = _full_from_gathered("w_in", w_in8)
    conv_w = cw8.transpose(1, 0, 2).reshape(HALO, -1)
    w_main, w_ug, w_qkv = w_in_full[:, :n_main], w_in_full[:, :n_ug], w_in_full[:, n_ug:n_main]
    w_f = jnp.pad(w_in_full[:, n_main:], ((0, 0), (0, LANES - FOX_HEADS)))
    z = matmul(h, w_main, out_dtype=BF16, tn=n_main, name="mm_in")
    z3 = z.reshape(B, S, n_main)
    n_mid, n_kv = len(LATE_MID), len(LATE_MID) + len(LATE_KV)
    (conv_out, conv_t, conv_y), partly_mid = conv_branch_fwd(z3, conv_w, conv_b, ln_g, ln_b, name="conv_fwd",
                                                     rider=AllGatherStage1(late_shards[:n_mid]))
    (f_raw, c_col, c_row), rode = fgate_fwd(
        h.reshape(B, S, D), w_f, b_f, name="fgate_fwd",
        rider=Riders(AllGatherStage1(late_shards[n_mid:n_kv]), AllGatherStage2(partly_mid)))
    partly_kv, full_mid = rode[:n_kv - n_mid], rode[n_kv - n_mid:]
    (att, lse, att_t), rode = fox_fwd(
        z3, c_col, c_row, name="fox_fwd",
        rider=Riders(AllGatherStage1(late_shards[n_kv:]), AllGatherStage2(partly_kv)))
    partly_ffn, full_kv = rode[:len(LATE_FFN)], rode[len(LATE_FFN):]
    wf = {n: _full_from_gathered(n, blk) for n, blk in zip(LATE_MID + LATE_KV, full_mid + full_kv)}
    (x1, hx, hx_t), full_ffn = matmul(
        [conv_out.reshape(T, CONV_CH), att.reshape(T, FOX_W)], [wf["w_out"], wf["w_out"]], b_blk=[0, 1],
        res=x2d, tn=D, name="mm_out", post=rms_fwd_epilogue(g_x), rider=AllGatherStage2(partly_ffn))
    wf.update({n: _full_from_gathered(n, blk) for n, blk in zip(LATE_FFN, full_ffn)})
    qm = matmul(hx, wf["w_mq"], out_dtype=BF16, tn=D, name="mm_mq")
    mem2d = mem.reshape(B * M, D)
    mem_n, mem_n_t = rmsnorm_fwd(mem2d, g_mem, name="rms_mem")
    kv = matmul(mem_n, wf["w_mkv"], out_dtype=BF16, tn=2 * D, name="mm_mkv").reshape(B, M, 2 * D)
    o, o_t = xattn_fwd(qm.reshape(B, S, D), kv, name="xattn_fwd")
    o = o.reshape(T, D)
    x2, hf, hf_t = matmul(o, wf["w_mo"], res=x1, tn=D, name="mm_mo", post=rms_fwd_epilogue(g_ffn))
    gate, up, act, act_t = mm_swiglu_fwd(hf, wf["w_gu"], name="mm_gu")
    dx3, dg_final, loss = matmul(act, wf["w_down"], res=x2, tn=D, name="mm_down",
                                 post=loss_epilogue(g_final, target.reshape(T, D)))
    gw = {}
    gw["w_down"] = matmul(act_t, dx3, out_dtype=BF16, tm=1408, tn=512, name="dw_down")
    dgate, dup = mm_swiglu_bwd(dx3, wf["w_down"], gate, up, name="dx_down")
    gw["w_gu"] = [matmul(hf_t, dgate, out_dtype=BF16, tn=1408, name="dw_gate"),
                  matmul(hf_t, dup, out_dtype=BF16, tn=1408, name="dw_up")]
    g42 = [_shards_from_full(n, gw[n]) for n in RS_GROUPS[0]]
    (dx2, dg_ffn), got = matmul([dgate, dup], [wf["w_gu"], wf["w_gu"]], b_blk=[0, 1], tb=True, tm=256, tn=D,
                                name="dx_gu", post=rms_bwd_epilogue(x2, g_ffn, dx3), rider=SiblingExchange(g42))
    parts = pair_sums(RS_GROUPS[0], g42, got)
    gw["w_mo"] = matmul(o_t, dx2, out_dtype=BF16, tn=D, name="dw_mo")
    do = matmul(dx2, wf["w_mo"], tb=True, out_dtype=BF16, tn=D, name="dx_mo")
    dqm, dkv = xattn_bwd(qm.reshape(B, S, D), kv, do.reshape(B, S, D), name="xattn_bwd")
    dqm = dqm.reshape(T, D)
    dkv = dkv.reshape(B * M, 2 * D)
    gw["w_mq"] = matmul(hx_t, dqm, out_dtype=BF16, tn=D, name="dw_mq")
    dx1, dg_x = matmul(dqm, wf["w_mq"], tb=True, tn=D, name="dx_mq", post=rms_bwd_epilogue(x1, g_x, dx2))
    gw["w_mkv"] = matmul(mem_n_t, dkv, out_dtype=BF16, tn=D, name="dw_mkv")
    _, dg_mem = matmul(dkv, wf["w_mkv"], tb=True, tn=D, name="dx_mkv", post=rms_bwd_epilogue(mem2d, g_mem, None))
    gw["w_out"] = jnp.concatenate([matmul(conv_t, dx1, out_dtype=BF16, tn=D, name="dw_out_conv"),
                                   matmul(att_t, dx1, out_dtype=BF16, tn=D, name="dw_out_att")], axis=0)
    g42 = [_shards_from_full(n, gw[n]) for n in RS_GROUPS[1]]
    dcat, got = matmul(dx1, wf["w_out"], tb=True, out_dtype=BF16, tn=D, name="dx_out", rider=SiblingExchange(g42))
    dcat = dcat.reshape(B, S, D)
    parts.update(pair_sums(RS_GROUPS[1], g42, got))
    dy, dconv_w, dvec = conv_branch_bwd_a(z3, conv_y, dcat, ln_g, ln_b, name="conv_bwd_a")
    dug = conv_branch_bwd_b(z3, dy, conv_w, name="conv_bwd_b")
    gots = {}
    (dq, stats), got = fox_bwd_dq(z3, dcat, lse, c_col, c_row, name="fox_bwd_dq",
                                  rider=ChipExchange([parts[n] for n in RS_GROUPS[0]]))
    gots.update(zip(RS_GROUPS[0], got))
    (dk, dv, dc), got = fox_bwd_dkdv(z3, dcat, stats, c_col, name="fox_bwd_dkdv",
                                     rider=ChipExchange([parts[n] for n in RS_GROUPS[1]]))
    gots.update(zip(RS_GROUPS[1], got))
    df, db_f = fgate_bwd(dc, f_raw, b_f, name="fgate_bwd")
    dug2 = dug.reshape(T, n_ug)
    dqkv = jnp.concatenate([dq, dk, dv], axis=-1).reshape(T, 3 * FOX_W)
    df2 = df.reshape(T, LANES)
    dw_in = [matmul(h_t, dug2, out_dtype=BF16, tn=n_ug, name="dw_in_ug"),
             matmul(h_t, dqkv, out_dtype=BF16, tn=3 * FOX_W, name="dw_in_qkv"),
             matmul(h_t, df2, out_dtype=BF16, name="dw_f")[:, :FOX_HEADS]]
    g42 = [_shards_from_full("w_in", dw_in)]
    parts.update(pair_sums(RS_GROUPS[2], g42, run_rider(SiblingExchange(g42), name="rs_sibling_in")))
    (dx, dg_mix), (gots["w_in"],) = matmul(
        [dug2, dqkv, df2], [w_ug, w_qkv, w_f], tb=True, tn=D, name="dx_in",
        post=rms_bwd_epilogue(x2d, g_mix, dx1, out_dtype=F32), rider=ChipExchange([parts["w_in"]]))
    gs = dict(g_mix=dg_mix, b_f=db_f[:, :FOX_HEADS], conv_w=dconv_w[:CONV_K], conv_b=dvec[0:1],
              ln_g=dvec[1:2], ln_b=dvec[2:3], g_x=dg_x, g_mem=dg_mem, g_ffn=dg_ffn, g_final=dg_final)
    return loss, dx.reshape(B, S, D), gs, {n: (parts[n], gots[n]) for n in BIG}


def _me():
    return lax.axis_index("x"), lax.axis_index("y"), lax.axis_index("c")


def _any_specs(n):
    return [pl.BlockSpec(memory_space=pl.ANY)] * n


def all_gather(xs, *, name):
    n = len(xs)

    def body(*refs):
        x_refs, out_refs = refs[:n], refs[n:2 * n]
        send_sems, recv_sems, local_sems = refs[2 * n:]
        x, y, c = _me()
        me, sibling = (x, y, c), (x, y, 1 - c)
        chips = [(1 - x, y), (x, 1 - y), (1 - x, 1 - y)]

        def slot(a, px, py, pc):
            return out_refs[a].at[4 * px + 2 * py + pc]

        def copy(a, k, block, to, own=False):
            return pltpu.make_async_remote_copy(
                src_ref=x_refs[a] if own else slot(a, *block), dst_ref=slot(a, *block),
                send_sem=send_sems.at[k, a], recv_sem=recv_sems.at[k, a], device_id=to, device_id_type=MESH)

        mine = [pltpu.make_async_copy(x_refs[a], slot(a, *me), local_sems.at[a]) for a in range(n)]
        first = [copy(a, 0, me, sibling, own=True) for a in range(n)]
        first += [copy(a, 1 + j, me, (*chip, c), own=True) for j, chip in enumerate(chips) for a in range(n)]
        for cp in mine + first:
            cp.start()
        passed = []
        for j, chip in enumerate(chips):
            for a in range(n):
                copy(a, 1 + j, (*chip, c), me).wait_recv()
                passed.append(copy(a, 4 + j, (*chip, c), sibling))
                passed[-1].start()
        for a in range(n):
            copy(a, 0, sibling, me).wait_recv()
            for j, chip in enumerate(chips):
                copy(a, 4 + j, (*chip, 1 - c), me).wait_recv()
        for cp in first + passed:
            cp.wait_send()
        for cp in mine:
            cp.wait()

    return _call(
        body, name=name, in_specs=_any_specs(n), out_specs=_any_specs(n),
        out_shape=[jax.ShapeDtypeStruct((N_DEV,) + v.shape, v.dtype) for v in xs],
        scratch_shapes=[pltpu.SemaphoreType.DMA((7, n)), pltpu.SemaphoreType.DMA((7, n)),
                        pltpu.SemaphoreType.DMA((n,))],
    )(*xs)


SIBLING_BARRIER = 1
CHIPS_BARRIER = 2
GATHER_BARRIER = 3


class SiblingExchange:
    collective_id = SIBLING_BARRIER

    def __init__(self, gs):
        n = len(gs)
        self.n, self.inputs = n, list(gs)
        self.out_shape = [jax.ShapeDtypeStruct((4,) + g.shape[2:], g.dtype) for g in gs]
        self.scratch = [pltpu.SemaphoreType.DMA((n,)), pltpu.SemaphoreType.DMA((n,))]

    @staticmethod
    def barrier_peers():
        x, y, c = _me()
        return [(x, y, 1 - c)]

    def _copies(self, g_refs, out_refs, sems):
        send_sems, recv_sems = sems
        x, y, c = _me()
        return [pltpu.make_async_remote_copy(
            src_ref=g_refs[a].at[:, 1 - c], dst_ref=out_refs[a], send_sem=send_sems.at[a],
            recv_sem=recv_sems.at[a], device_id=(x, y, 1 - c), device_id_type=MESH) for a in range(self.n)]

    def start(self, in_refs, out_refs, sems):
        for cp in self._copies(in_refs, out_refs, sems):
            cp.start()

    def finish(self, in_refs, out_refs, sems):
        for cp in self._copies(in_refs, out_refs, sems):
            cp.wait()


def run_rider(rider, *, name):
    return hosted_call(None, rider, name=name, grid=(), in_specs=[], out_specs=[], out_shape=[],
                       scratch_shapes=[], args=[])[1]


class ChipExchange:
    collective_id = CHIPS_BARRIER

    @staticmethod
    def barrier_peers():
        x, y, c = _me()
        return [(1 - x, y, c), (x, 1 - y, c), (1 - x, 1 - y, c)]

    def __init__(self, ps):
        n = len(ps)
        self.n, self.inputs = n, list(ps)
        self.out_shape = [jax.ShapeDtypeStruct(p.shape, p.dtype) for p in ps]
        self.scratch = [pltpu.SemaphoreType.DMA((3, n)), pltpu.SemaphoreType.DMA((3, n))]

    def _copies(self, p_refs, out_refs, sems, outgoing):
        send_sems, recv_sems = sems
        x, y, c = _me()
        my_chip = 2 * x + y
        cps = []
        for k in range(3):
            px, py = x ^ ((k + 1) >> 1), y ^ ((k + 1) & 1)
            src, dst = (2 * px + py, my_chip) if outgoing else (my_chip, 2 * px + py)
            for a in range(self.n):
                cps.append(pltpu.make_async_remote_copy(
                    src_ref=p_refs[a].at[src], dst_ref=out_refs[a].at[dst], send_sem=send_sems.at[k, a],
                    recv_sem=recv_sems.at[k, a], device_id=(px, py, c), device_id_type=MESH))
        return cps

    def start(self, in_refs, out_refs, sems):
        for cp in self._copies(in_refs, out_refs, sems, True):
            cp.start()

    def finish(self, in_refs, out_refs, sems):
        for cp in self._copies(in_refs, out_refs, sems, False):
            cp.wait_recv()
        for cp in self._copies(in_refs, out_refs, sems, True):
            cp.wait_send()


class AllGatherStage1:
    collective_id = GATHER_BARRIER

    @staticmethod
    def barrier_peers():
        x, y, c = _me()
        return [(x, y, 1 - c), (1 - x, y, c), (x, 1 - y, c), (1 - x, 1 - y, c)]

    def __init__(self, xs):
        n = len(xs)
        self.n, self.inputs = n, list(xs)
        self.out_shape = [jax.ShapeDtypeStruct((N_DEV,) + v.shape, v.dtype) for v in xs]
        self.scratch = [pltpu.SemaphoreType.DMA((4, n)), pltpu.SemaphoreType.DMA((4, n)),
                        pltpu.SemaphoreType.DMA((n,))]

    def _copies(self, x_refs, out_refs, sems, kind):
        send_sems, recv_sems, local_sems = sems
        x, y, c = _me()
        slot = lambda a, d: out_refs[a].at[4 * d[0] + 2 * d[1] + d[2]]
        if kind == "local":
            return [pltpu.make_async_copy(x_refs[a], slot(a, (x, y, c)), local_sems.at[a]) for a in range(self.n)]
        cps = []
        for k, peer in enumerate([(x, y, 1 - c), (1 - x, y, c), (x, 1 - y, c), (1 - x, 1 - y, c)]):
            for a in range(self.n):
                cps.append(pltpu.make_async_remote_copy(
                    src_ref=x_refs[a], dst_ref=slot(a, (x, y, c) if kind == "out" else peer),
                    send_sem=send_sems.at[k, a], recv_sem=recv_sems.at[k, a], device_id=peer, device_id_type=MESH))
        return cps

    def start(self, in_refs, out_refs, sems):
        for cp in self._copies(in_refs, out_refs, sems, "local") + self._copies(in_refs, out_refs, sems, "out"):
            cp.start()

    def finish(self, in_refs, out_refs, sems):
        for cp in self._copies(in_refs, out_refs, sems, "in"):
            cp.wait_recv()
        for cp in self._copies(in_refs, out_refs, sems, "out"):
            cp.wait_send()
        for cp in self._copies(in_refs, out_refs, sems, "local"):
            cp.wait()


class AllGatherStage2:
    collective_id = SIBLING_BARRIER

    @staticmethod
    def barrier_peers():
        x, y, c = _me()
        return [(x, y, 1 - c)]

    def __init__(self, outs):
        n = len(outs)
        self.n, self.inputs = n, list(outs)
        self.out_shape = [jax.ShapeDtypeStruct(o.shape, o.dtype) for o in outs]
        self.scratch = [pltpu.SemaphoreType.DMA((3, n)), pltpu.SemaphoreType.DMA((3, n))]
        self.aliases = {a: a for a in range(n)}

    def _copies(self, out_refs, sems, outgoing):
        send_sems, recv_sems = sems
        x, y, c = _me()
        cps = []
        for k, (px, py) in enumerate([(1 - x, y), (x, 1 - y), (1 - x, 1 - y)]):
            for a in range(self.n):
                cps.append(pltpu.make_async_remote_copy(
                    src_ref=out_refs[a].at[4 * px + 2 * py + c],
                    dst_ref=out_refs[a].at[4 * px + 2 * py + (c if outgoing else 1 - c)],
                    send_sem=send_sems.at[k, a], recv_sem=recv_sems.at[k, a], device_id=(x, y, 1 - c),
                    device_id_type=MESH))
        return cps

    def start(self, in_refs, out_refs, sems):
        for cp in self._copies(out_refs, sems, True):
            cp.start()

    def finish(self, in_refs, out_refs, sems):
        for cp in self._copies(out_refs, sems, False):
            cp.wait_recv()
        for cp in self._copies(out_refs, sems, True):
            cp.wait_send()


class Riders:
    def __init__(self, *riders):
        self.riders = riders
        self.collective_id = riders[0].collective_id
        self.barrier_peers = riders[0].barrier_peers
        self.inputs = [v for r in riders for v in r.inputs]
        self.out_shape = [s for r in riders for s in r.out_shape]
        self.scratch = [s for r in riders for s in r.scratch]
        self.aliases, i0, o0 = {}, 0, 0
        for r in riders:
            self.aliases.update({i0 + i: o0 + o for i, o in getattr(r, "aliases", {}).items()})
            i0, o0 = i0 + len(r.inputs), o0 + len(r.out_shape)

    def _split(self, in_refs, out_refs, sems):
        i0 = o0 = s0 = 0
        for r in self.riders:
            ni, no, ns = len(r.inputs), len(r.out_shape), len(r.scratch)
            yield r, in_refs[i0:i0 + ni], out_refs[o0:o0 + no], sems[s0:s0 + ns]
            i0, o0, s0 = i0 + ni, o0 + no, s0 + ns

    def start(self, in_refs, out_refs, sems):
        for r, i, o, s in self._split(in_refs, out_refs, sems):
            r.start(i, o, s)

    def finish(self, in_refs, out_refs, sems):
        for r, i, o, s in self._split(in_refs, out_refs, sems):
            r.finish(i, o, s)


def _peer_barrier(peers):
    barrier = pltpu.get_barrier_semaphore()
    for peer in peers:
        pl.semaphore_signal(barrier, inc=1, device_id=peer, device_id_type=MESH)
    pl.semaphore_wait(barrier, len(peers))


def hosted_call(body, rider, *, name, grid, in_specs, out_specs, out_shape, scratch_shapes, args, vmem=None):
    n_in, n_out, n_scr = len(in_specs), len(out_specs), len(scratch_shapes)
    r_in, r_out = (len(rider.inputs), len(rider.out_shape)) if rider is not None else (0, 0)
    own_barrier = getattr(rider, "collective_id", None) is not None

    def wrapped(*refs):
        ins, refs = refs[:n_in], refs[n_in:]
        rins, refs = refs[:r_in], refs[r_in:]
        outs, refs = refs[:n_out], refs[n_out:]
        routs, refs = refs[:r_out], refs[r_out:]
        scr, rscr = refs[:n_scr], refs[n_scr:]
        ids = [pl.program_id(d) for d in range(len(grid))]
        first = functools.reduce(jnp.logical_and, [i == 0 for i in ids], True)
        last = functools.reduce(jnp.logical_and, [i == g - 1 for i, g in zip(ids, grid)], True)

        def begin():
            if own_barrier:
                _peer_barrier(rider.barrier_peers())
            rider.start(rins, routs, rscr)

        if rider is not None and grid:
            pl.when(first)(begin)
        elif rider is not None:
            begin()
        if body is not None:
            body(*ins, *outs, *scr)
        if rider is not None and grid:
            pl.when(last)(lambda: rider.finish(rins, routs, rscr))
        elif rider is not None:
            rider.finish(rins, routs, rscr)

    kw = dict(grid=grid) if grid else {}
    aliases = getattr(rider, "aliases", {})
    if aliases:
        kw["input_output_aliases"] = {n_in + i: n_out + o for i, o in aliases.items()}
    if grid or vmem is not None or own_barrier:
        kw["compiler_params"] = _params(("arbitrary",) * len(grid) if grid else None, vmem,
                                        rider.collective_id if own_barrier else None)
    res = _call(
        wrapped, name=name, in_specs=list(in_specs) + _any_specs(r_in), out_specs=list(out_specs) + _any_specs(r_out),
        out_shape=list(out_shape) + (rider.out_shape if rider is not None else []),
        scratch_shapes=list(scratch_shapes) + (rider.scratch if rider is not None else []), **kw,
    )(*args, *(rider.inputs if rider is not None else []))
    return list(res[:n_out]), list(res[n_out:])


def _pick_rows(r, target=256):
    best = None
    for d in range(16, min(r, target) + 1, 16):
        if r % d == 0:
            best = d
    return r if best is None else best


def pair_sum(g, got, *, name):
    _, _, R, C = g.shape
    tr = _pick_rows(R)

    def body(g_ref, got_ref, o_ref):
        mine = jnp.where(lax.axis_index("c") == 0, g_ref[:, 0], g_ref[:, 1])
        o_ref[...] = (mine.astype(F32) + got_ref[...].astype(F32)).astype(o_ref.dtype)

    return _call(
        body, name=name, grid=(R // tr,),
        in_specs=[pl.BlockSpec((4, 2, tr, C), lambda i: (0, 0, i, 0)), pl.BlockSpec((4, tr, C), lambda i: (0, i, 0))],
        out_specs=pl.BlockSpec((4, tr, C), lambda i: (0, i, 0)),
        out_shape=jax.ShapeDtypeStruct((4, R, C), g.dtype),
        compiler_params=_params(("parallel",)),
    )(g, got)


def chip_sum_adamw(p, got, w, m, v, *, name):
    _, R, C = p.shape
    assert w.shape == (1, R, C), (name, w.shape, p.shape)
    tr = _pick_rows(R)

    def body(p_ref, got_ref, w_ref, m_ref, v_ref, g_ref, d_ref, mo_ref, vo_ref):
        my_chip = 2 * lax.axis_index("x") + lax.axis_index("y")
        g = jnp.zeros((tr, C), F32)
        for j in range(4):
            g = g + jnp.where(my_chip == j, p_ref[j], got_ref[j]).astype(F32)
        g_ref[0] = g
        d_ref[0], mo_ref[0], vo_ref[0] = _adamw_math(w_ref[0], g, m_ref[0], v_ref[0])

    part = pl.BlockSpec((4, tr, C), lambda i: (0, i, 0))
    spec = pl.BlockSpec((1, tr, C), lambda i: (0, i, 0))
    return _call(
        body, name=name, grid=(R // tr,), in_specs=[part, part, spec, spec, spec], out_specs=[spec] * 4,
        out_shape=[jax.ShapeDtypeStruct((1, R, C), F32)] * 4,
        compiler_params=_params(("parallel",)),
    )(p, got, w, m, v)


def rows_sum(g8, *, name):
    _, R, C = g8.shape

    def body(g_ref, o_ref):
        acc = g_ref[0]
        for j in range(1, N_DEV):
            acc = acc + g_ref[j]
        o_ref[...] = acc

    return _call(body, name=name, out_shape=jax.ShapeDtypeStruct((R, C), F32))(g8)


def _adamw_math(w, g, m, v):
    m = ADAM_B1 * m + (1.0 - ADAM_B1) * g
    v = ADAM_B2 * v + (1.0 - ADAM_B2) * (g * g)
    m_hat = m / (1.0 - ADAM_B1 ** ADAM_STEP)
    v_hat = v / (1.0 - ADAM_B2 ** ADAM_STEP)
    delta = -ADAM_LR * (m_hat / (jnp.sqrt(v_hat) + ADAM_EPS) + ADAM_WD * w)
    return delta, m, v


def to_bf16(xs, *, name):
    def body(*refs):
        for x_ref, o_ref in zip(refs[:len(xs)], refs[len(xs):]):
            o_ref[...] = x_ref[...].astype(BF16)

    total = sum(_nbytes(v.shape, F32) + _nbytes(v.shape, BF16) for v in xs)
    return _call(body, name=name, out_shape=[jax.ShapeDtypeStruct(v.shape, BF16) for v in xs],
                 compiler_params=_params(vmem=2 * total + (4 << 20)))(*xs)


def adamw_small(wgmv, *, name):
    n = len(wgmv)

    def body(*refs):
        ins, outs = refs[:4 * n], refs[4 * n:]
        for a in range(n):
            w_ref, g_ref, m_ref, v_ref = ins[4 * a:4 * a + 4]
            d, mn, vn = _adamw_math(w_ref[...], g_ref[...], m_ref[...], v_ref[...])
            outs[3 * a][...] = d
            outs[3 * a + 1][...] = mn
            outs[3 * a + 2][...] = vn

    flat = [t for tup in wgmv for t in tup]
    res = _call(
        body, name=name,
        out_shape=[jax.ShapeDtypeStruct(tup[0].shape, F32) for tup in wgmv for _ in range(3)],
    )(*flat)
    return [tuple(res[3 * a:3 * a + 3]) for a in range(n)]


BIG = ("w_in", "w_out", "w_mq", "w_mkv", "w_mo", "w_gu", "w_down")
COL_SHARDED = ("w_in", "w_mkv", "w_gu")
SMALL = ("g_mix", "b_f", "conv_w", "conv_b", "ln_g", "ln_b", "g_x", "g_mem", "g_ffn", "g_final")


def _full_from_gathered(n, blk):
    _, rr, cc = blk.shape
    if n in COL_SHARDED:
        return join_columns(blk, name="join_" + n)
    return blk.reshape(N_DEV * rr, cc)


def join_columns(blk, *, name, tr=256):
    n, R, w = blk.shape
    tr = min(tr, R)

    def body(b_ref, o_ref):
        for k in range(n):
            o_ref[:, pl.ds(k * w, w)] = b_ref[k]

    return _call(
        body, name=name, grid=(R // tr,),
        in_specs=[pl.BlockSpec((n, tr, w), lambda r: (0, r, 0))],
        out_specs=pl.BlockSpec((tr, n * w), lambda r: (r, 0)),
        out_shape=jax.ShapeDtypeStruct((R, n * w), blk.dtype),
        compiler_params=_params(("parallel",)),
    )(blk)


def _shards_from_full(n, g):
    pieces = g if isinstance(g, list) else [g]
    rr, cc = pieces[0].shape[0], sum(p.shape[1] for p in pieces)
    if n in COL_SHARDED:
        return split_columns(pieces, name="shards_" + n).reshape(4, 2, rr, cc // N_DEV)
    return pieces[0].reshape(4, 2, rr // N_DEV, cc)


def split_columns(pieces, *, name, tr=256):
    R = pieces[0].shape[0]
    w = sum(p.shape[1] for p in pieces) // N_DEV
    tr = min(tr, R)
    moves, c0 = [], 0
    for i, p in enumerate(pieces):
        for k in range(N_DEV):
            lo, hi = max(k * w, c0), min((k + 1) * w, c0 + p.shape[1])
            if lo < hi:
                moves.append((k, i, lo - c0, hi - c0, lo - k * w))
        c0 += p.shape[1]

    def body(*refs):
        o_ref = refs[-1]
        for k, i, lo, hi, off in moves:
            o_ref[k, :, pl.ds(off, hi - lo)] = refs[i][:, pl.ds(lo, hi - lo)]

    return _call(
        body, name=name, grid=(R // tr,),
        in_specs=[pl.BlockSpec((tr, p.shape[1]), lambda r: (r, 0)) for p in pieces],
        out_specs=pl.BlockSpec((N_DEV, tr, w), lambda r: (0, r, 0)),
        out_shape=jax.ShapeDtypeStruct((N_DEV, R, w), pieces[0].dtype),
        compiler_params=_params(("parallel",)),
    )(*pieces)


def _small_layout():
    sizes = dict(g_mix=1024, b_f=8, conv_w=CONV_K * CONV_CH, conv_b=512, ln_g=512, ln_b=512, g_x=1024,
                 g_mem=1024, g_ffn=1024, g_final=1024, loss=1)
    lay, r0 = {}, 0
    for n, sz in sizes.items():
        r = -(-sz // LANES)
        lay[n] = (r0, r, sz)
        r0 += r
    return lay, -(-r0 // 8) * 8


def kernel(x, mem, g_mix, w_in, b_f, conv_w, conv_b, ln_g, ln_b, w_out, g_x, g_mem, w_mq, w_mkv, w_mo, g_ffn, w_gu, w_down, g_final, loss_target, m_g_mix, m_w_in, m_b_f, m_conv_w, m_conv_b, m_ln_g, m_ln_b, m_w_out, m_g_x, m_g_mem, m_w_mq, m_w_mkv, m_w_mo, m_g_ffn, m_w_gu, m_w_down, m_g_final, v_g_mix, v_w_in, v_b_f, v_conv_w, v_conv_b, v_ln_g, v_ln_b, v_w_out, v_g_x, v_g_mem, v_w_mq, v_w_mkv, v_w_mo, v_g_ffn, v_w_gu, v_w_down, v_g_final):
    names = ["g_mix", "w_in", "b_f", "conv_w", "conv_b", "ln_g", "ln_b", "w_out", "g_x", "g_mem", "w_mq",
             "w_mkv", "w_mo", "g_ffn", "w_gu", "w_down", "g_final"]
    W = dict(zip(names, [g_mix, w_in, b_f, conv_w, conv_b, ln_g, ln_b, w_out, g_x, g_mem, w_mq, w_mkv, w_mo,
                         g_ffn, w_gu, w_down, g_final]))
    Mo = dict(zip(names, [m_g_mix, m_w_in, m_b_f, m_conv_w, m_conv_b, m_ln_g, m_ln_b, m_w_out, m_g_x, m_g_mem,
                          m_w_mq, m_w_mkv, m_w_mo, m_g_ffn, m_w_gu, m_w_down, m_g_final]))
    Vo = dict(zip(names, [v_g_mix, v_w_in, v_b_f, v_conv_w, v_conv_b, v_ln_g, v_ln_b, v_w_out, v_g_x, v_g_mem,
                          v_w_mq, v_w_mkv, v_w_mo, v_g_ffn, v_w_gu, v_w_down, v_g_final]))
    dev = 4 * lax.axis_index("x") + 2 * lax.axis_index("y") + lax.axis_index("c")

    two = lambda a: a.reshape(-1, a.shape[-1])
    cw_shard = jnp.pad(two(conv_w), ((0, HALO - CONV_K), (0, 0)))
    sp = dict(g_mix=g_mix, b_f=b_f, conv_b=conv_b, ln_g=ln_g, ln_b=ln_b, g_x=g_x, g_mem=g_mem,
              g_ffn=g_ffn, g_final=g_final)
    shards = to_bf16([two(W[n]) for n in ("w_in",) + LATE], name="cast_shards")
    loss_blk, grad_x, gs, reduced = local_step(x, mem, loss_target, sp, [shards[0], cw_shard], shards[1:])

    lay, rs = _small_layout()
    small = {**{n: gs[n] for n in SMALL}, "loss": loss_blk[:, :1]}
    parts = []
    for n, (r0, r, sz) in lay.items():
        flat = small[n].reshape(-1).astype(F32)
        parts.append(jnp.pad(flat, (0, r * LANES - sz)).reshape(r, LANES))
    spack = jnp.concatenate(parts, axis=0)
    spack = jnp.pad(spack, ((0, rs - spack.shape[0]), (0, 0)))
    ssum = rows_sum(all_gather([spack], name="ag_small")[0], name="small_sum")
    gsmall = {n: ssum[r0:r0 + r].reshape(-1)[:sz] for n, (r0, r, sz) in lay.items()}
    loss = gsmall["loss"].reshape(())

    grads, delta, new_m, new_v = {}, {}, {}, {}
    for n in BIG:
        p, o = reduced[n]
        grads[n], delta[n], new_m[n], new_v[n] = chip_sum_adamw(p, o, W[n], Mo[n], Vo[n], name="adamw_" + n)
    for n in SMALL:
        if n == "conv_w":
            full = gsmall[n].reshape(CONV_K, CONV_CH)
            ncol = conv_w.shape[-1]
            grads[n] = lax.dynamic_slice(full, (0, dev * ncol), (CONV_K, ncol)).reshape(conv_w.shape)
        else:
            grads[n] = gsmall[n].reshape(W[n].shape)
    upd = adamw_small([(two(W[n]), two(grads[n]), two(Mo[n]), two(Vo[n])) for n in SMALL], name="adamw_small")
    for n, (d, mn, vn) in zip(SMALL, upd):
        shp = W[n].shape
        delta[n], new_m[n], new_v[n] = d.reshape(shp), mn.reshape(shp), vn.reshape(shp)
    return (loss, grad_x, *[grads[n] for n in names], *[delta[n] for n in names],
            *[new_m[n] for n in names], *[new_v[n] for n in names])
```

```python
import functools
import math

import jax
import jax.numpy as jnp
from jax import lax
from jax.experimental import pallas as pl
from jax.experimental.pallas import tpu as pltpu

F32 = jnp.float32
BF16 = jnp.bfloat16
EPS = 1e-6
N_DEV = 8
CONV_CH = 512
CONV_K = 31
FOX_HEADS = 8
FOX_HEAD_DIM = 64
FOX_W = 512
MEM_HEADS = 4
MEM_HEAD_DIM = 256
HALO = 32
LANES = 128
ADAM_LR, ADAM_B1, ADAM_B2, ADAM_EPS, ADAM_WD, ADAM_STEP = 0.001, 0.9, 0.999, 1e-08, 0.01, 10
NEG = -1e30
VMEM_CAP = 60 * 1024 * 1024
VMEM_FOX_BWD = 56 << 20
VMEM_SWIGLU = 48 << 20
MESH = pl.DeviceIdType.MESH


def _call(body, n_prefetch=0, **kw):
    kw["out_shape"] = jax.tree.map(lambda s: pltpu.HBM(s.shape, s.dtype), kw["out_shape"])
    call = pl.pallas_call(body, **kw)
    return lambda *args: call(*args[:n_prefetch],
                              *[pltpu.with_memory_space_constraint(a, pltpu.HBM) for a in args[n_prefetch:]])


def _params(sem=None, vmem=None, collective_id=None):
    kw = {} if collective_id is None else {"collective_id": collective_id}
    if sem is not None:
        kw["dimension_semantics"] = sem
    if vmem is not None:
        kw["vmem_limit_bytes"] = int(min(VMEM_CAP, vmem))
    return pltpu.CompilerParams(**kw)


def _nbytes(shape, dtype):
    return math.prod(shape) * jnp.dtype(dtype).itemsize


def _pick(n, target):
    best = None
    for d in range(LANES, min(n, target) + 1, LANES):
        if n % d == 0:
            best = d
    return n if best is None else best


class RowEpilogue:
    def __init__(self, fn, ins, outs):
        self.fn, self.ins, self.outs = fn, list(ins), list(outs)


def matmul(a, b, *, tb=False, out_dtype=None, res=None, tm=512, tn=512, name, rider=None, b_blk=None, post=None):
    a_list = list(a) if isinstance(a, (list, tuple)) else [a]
    b_list = list(b) if isinstance(b, (list, tuple)) else [b]
    n = len(a_list)
    assert len(b_list) == n
    M = a_list[0].shape[0]
    N = b_list[0].shape[0] if tb else b_list[0].shape[1]
    tm, tn = _pick(M, tm), _pick(N, tn)
    assert M % tm == 0 and N % tn == 0, (name, M, N, tm, tn)
    dn = (((1,), (1 if tb else 0,)), ((), ()))

    n_res = int(res is not None)
    n_pin = len(post.ins) if post is not None else 0

    def body(*refs):
        acc = None
        for a_ref, b_ref in zip(refs[:n], refs[n:2 * n]):
            p = lax.dot_general(a_ref[...].astype(BF16), b_ref[...].astype(BF16), dn, preferred_element_type=F32)
            acc = p if acc is None else acc + p
        if res is not None:
            acc = acc + refs[2 * n][...].astype(F32)
        if post is None:
            refs[-1][...] = acc.astype(out_dtype)
            return
        first_in = 2 * n + n_res
        vals = post.fn(acc, *[r[...] for r in refs[first_in:first_in + n_pin]])
        for (dtype, kind), o_ref, val in zip(post.outs, refs[first_in + n_pin:], vals):
            if kind in ("row", "rowT"):
                o_ref[...] = val.astype(dtype)
            else:
                @pl.when(pl.program_id(0) == 0)
                def _(o_ref=o_ref):
                    o_ref[...] = jnp.zeros_like(o_ref)

                o_ref[...] += jnp.broadcast_to(val, o_ref.shape).astype(dtype)

    o_spec = pl.BlockSpec((tm, tn), lambda i, j: (i, j))
    in_specs, est = [], 2 * _nbytes((tm, tn), out_dtype or F32) + 2 * _nbytes((tm, tn), F32)
    for av in a_list:
        assert av.shape[0] == M
        in_specs.append(pl.BlockSpec((tm, av.shape[1]), lambda i, j: (i, 0)))
        est += (2 * jnp.dtype(av.dtype).itemsize + (av.dtype != BF16) * 2) * tm * av.shape[1]
    for idx, (av, bv) in enumerate(zip(a_list, b_list)):
        K = av.shape[1]
        kb = 0 if b_blk is None else b_blk[idx]
        assert bv.shape[0 if tb else 1] == N and bv.shape[1 if tb else 0] >= (kb + 1) * K, (name, av.shape, bv.shape)
        assert b_blk is not None or bv.shape[1 if tb else 0] == K, (name, av.shape, bv.shape)
        in_specs.append(pl.BlockSpec((tn, K), lambda i, j, kb=kb: (j, kb)) if tb
                        else pl.BlockSpec((K, tn), lambda i, j, kb=kb: (kb, j)))
        est += (2 * jnp.dtype(bv.dtype).itemsize + (bv.dtype != BF16) * 2) * tn * K
    args = a_list + b_list
    if res is not None:
        in_specs.append(o_spec)
        args.append(res)
        est += 2 * _nbytes((tm, tn), res.dtype)
    if post is None:
        out_specs, out_shape = [o_spec], [jax.ShapeDtypeStruct((M, N), out_dtype)]
    else:
        assert tn == N, (name, tn, N)
        row = pl.BlockSpec((tm, N), lambda i, j: (i, 0))
        for arr, kind in post.ins:
            in_specs.append(row if kind == "row" else pl.BlockSpec((1, N), lambda i, j: (0, 0)))
            args.append(arr)
            est += 2 * _nbytes((tm, N), arr.dtype) * (kind == "row")
        specs = {"row": (row, (M, N)), "rowT": (pl.BlockSpec((N, tm), lambda i, j: (0, i)), (N, M)),
                 "vec": (pl.BlockSpec((1, N), lambda i, j: (0, 0)), (1, N)),
                 "lanes": (pl.BlockSpec((1, LANES), lambda i, j: (0, 0)), (1, LANES))}
        out_specs = [specs[kind][0] for _, kind in post.outs]
        out_shape = [jax.ShapeDtypeStruct(specs[kind][1], dtype) for dtype, kind in post.outs]
        est += sum(2 * _nbytes((tm, N), dtype) + _nbytes((tm, N), F32) for dtype, kind in post.outs if kind[:3] == "row")
    outs, rode = hosted_call(
        body, rider, name=name, grid=(M // tm, N // tn), in_specs=in_specs, out_specs=out_specs,
        out_shape=out_shape, scratch_shapes=[], args=args, vmem=est + (8 << 20),
    )
    result = outs[0] if post is None else outs
    return result if rider is None else (result, rode)


def _rms_scale(x):
    return lax.rsqrt(jnp.mean(x * x, axis=-1, keepdims=True) + EPS)


def rmsnorm_fwd(x, g, *, name, tm=512, rider=None):
    T, D = x.shape
    tm = min(tm, T)

    def body(x_ref, g_ref, o_ref, ot_ref):
        xv = x_ref[...]
        h = xv * _rms_scale(xv) * g_ref[...]
        o_ref[...] = h.astype(BF16)
        ot_ref[...] = h.T.astype(BF16)

    (h, h_t), rode = hosted_call(
        body, rider, name=name, grid=(T // tm,),
        in_specs=[pl.BlockSpec((tm, D), lambda i: (i, 0)), pl.BlockSpec((1, D), lambda i: (0, 0))],
        out_specs=[pl.BlockSpec((tm, D), lambda i: (i, 0)), pl.BlockSpec((D, tm), lambda i: (0, i))],
        out_shape=[jax.ShapeDtypeStruct((T, D), BF16), jax.ShapeDtypeStruct((D, T), BF16)],
        scratch_shapes=[], args=(x, g),
    )
    return (h, h_t) if rider is None else (h, h_t, rode)


def _rms_bwd_math(xv, gv, dh):
    r = _rms_scale(xv)
    xh = xv * r
    dg = jnp.sum(dh * xh, axis=0, keepdims=True)
    dxh = dh * gv
    dx = r * (dxh - xh * jnp.mean(dxh * xh, axis=-1, keepdims=True))
    return dx, dg


def rms_fwd_epilogue(g):
    def fn(acc, gv):
        h = acc * _rms_scale(acc) * gv
        return acc, h, h.T
    return RowEpilogue(fn, [(g, "vec")], [(F32, "row"), (BF16, "row"), (BF16, "rowT")])


def rms_bwd_epilogue(x, g, dres, out_dtype=BF16):
    def fn(acc, xv, gv, *dr):
        dx, dg = _rms_bwd_math(xv, gv, acc)
        return (dx + dr[0].astype(F32) if dr else dx), dg
    ins = [(x, "row"), (g, "vec")] + ([(dres, "row")] if dres is not None else [])
    return RowEpilogue(fn, ins, [(out_dtype, "row"), (F32, "vec")])


def loss_epilogue(g, target):
    def fn(acc, gv, tv):
        e = acc * _rms_scale(acc) * gv - tv
        part = 0.5 * jnp.sum(jnp.mean(e * e, axis=-1, keepdims=True), axis=0, keepdims=True)
        dx, dg = _rms_bwd_math(acc, gv, e * (1.0 / acc.shape[-1]))
        return dx, dg, part
    return RowEpilogue(fn, [(g, "vec"), (target, "row")], [(BF16, "row"), (F32, "vec"), (F32, "lanes")])


def _sigmoid(v):
    return 0.5 * jnp.tanh(0.5 * v) + 0.5


def _glu(blk):
    u = blk[:, :CONV_CH].astype(F32)
    gt = blk[:, CONV_CH:].astype(F32)
    return u * _sigmoid(gt)


def _fill_causal_ext(ext, cur_ref, halo_ref, s, ts):
    ext[pl.ds(HALO, ts), :] = _glu(cur_ref[0])
    hal = _glu(halo_ref[0])
    ext[pl.ds(0, HALO), :] = jnp.where(s > 0, hal, 0.0)


SUBLANES = 8


def _make_shifted(ext, sh):
    n = ext.shape[0]
    full = ext[...]
    for r in range(1, SUBLANES):
        sh[r - 1] = pltpu.roll(full, n - r, 0)


def _tap(ext, sh, off, ts):
    r = off % SUBLANES
    return ext[pl.ds(off, ts), :] if r == 0 else sh[r - 1, pl.ds(off - r, ts), :]


def _causal_conv(ext, sh, w_ref, ts):
    acc = jnp.zeros((ts, CONV_CH), F32)
    for j in range(CONV_K):
        acc = acc + _tap(ext, sh, HALO - (CONV_K - 1) + j, ts) * w_ref[pl.ds(j, 1), :]
    return acc


def _ln_stats(y):
    mu = jnp.mean(y, axis=-1, keepdims=True)
    yc = y - mu
    rstd = lax.rsqrt(jnp.mean(yc * yc, axis=-1, keepdims=True) + EPS)
    return yc * rstd, rstd


def _conv_specs(ts, S):
    nh = ts // HALO
    cur = pl.BlockSpec((1, ts, 2 * CONV_CH), lambda b, s: (b, s, 0))
    halo = pl.BlockSpec((1, HALO, 2 * CONV_CH), lambda b, s: (b, jnp.maximum(s * nh - 1, 0), 0))
    w = pl.BlockSpec((HALO, CONV_CH), lambda b, s: (0, 0))
    vec = pl.BlockSpec((1, CONV_CH), lambda b, s: (0, 0))
    return cur, halo, w, vec


def conv_branch_fwd(ug, conv_w, conv_b, ln_g, ln_b, *, name, ts=256, rider=None):
    B, S, _ = ug.shape
    ts = min(ts, S)
    ns = S // ts
    cur, halo, w, vec = _conv_specs(ts, S)

    def body(cur_ref, halo_ref, w_ref, cb_ref, lg_ref, lb_ref, o_ref, ot_ref, y_ref, ext, sh):
        _fill_causal_ext(ext, cur_ref, halo_ref, pl.program_id(1), ts)
        _make_shifted(ext, sh)
        y = _causal_conv(ext, sh, w_ref, ts) + cb_ref[...]
        y_ref[0] = y
        yh, _ = _ln_stats(y)
        ln = yh * lg_ref[...] + lb_ref[...]
        out = ln * _sigmoid(ln)
        o_ref[0] = out.astype(BF16)
        ot_ref[...] = out.T.astype(BF16)

    return hosted_call(
        body, rider, name=name, grid=(B, ns), in_specs=[cur, halo, w, vec, vec, vec],
        out_specs=[pl.BlockSpec((1, ts, CONV_CH), lambda b, s: (b, s, 0)),
                   pl.BlockSpec((CONV_CH, ts), lambda b, s: (0, b * ns + s)),
                   pl.BlockSpec((1, ts, CONV_CH), lambda b, s: (b, s, 0))],
        out_shape=[jax.ShapeDtypeStruct((B, S, CONV_CH), BF16), jax.ShapeDtypeStruct((CONV_CH, B * S), BF16),
                   jax.ShapeDtypeStruct((B, S, CONV_CH), F32)],
        scratch_shapes=[pltpu.VMEM((ts + HALO, CONV_CH), F32),
                        pltpu.VMEM((SUBLANES - 1, ts + HALO, CONV_CH), F32)],
        args=(ug, ug, conv_w, conv_b, ln_g, ln_b),
    )


def conv_branch_bwd_a(ug, y, dcat, ln_g, ln_b, *, name, ts=256):
    B, S, _ = ug.shape
    ts = min(ts, S)
    cur, halo, _, vec = _conv_specs(ts, S)
    tile = pl.BlockSpec((1, ts, CONV_CH), lambda b, s: (b, s, 0))

    def body(cur_ref, halo_ref, y_ref, d_ref, lg_ref, lb_ref, dy_ref, dw_ref, dv_ref, ext, sh):
        _fill_causal_ext(ext, cur_ref, halo_ref, pl.program_id(1), ts)
        _make_shifted(ext, sh)
        yh, rstd = _ln_stats(y_ref[0])
        lg = lg_ref[...]
        ln = yh * lg + lb_ref[...]
        sg = _sigmoid(ln)
        dln = d_ref[0].astype(F32) * (sg * (1.0 + ln * (1.0 - sg)))
        dyh = dln * lg
        dy = rstd * (dyh - jnp.mean(dyh, axis=-1, keepdims=True)
                     - yh * jnp.mean(dyh * yh, axis=-1, keepdims=True))
        dy_ref[0] = dy

        @pl.when((pl.program_id(0) == 0) & (pl.program_id(1) == 0))
        def _():
            dw_ref[...] = jnp.zeros_like(dw_ref)
            dv_ref[...] = jnp.zeros_like(dv_ref)

        dv_ref[pl.ds(0, 1), :] += jnp.sum(dy, axis=0, keepdims=True)
        dv_ref[pl.ds(1, 1), :] += jnp.sum(dln * yh, axis=0, keepdims=True)
        dv_ref[pl.ds(2, 1), :] += jnp.sum(dln, axis=0, keepdims=True)
        for j in range(CONV_K):
            tap = _tap(ext, sh, HALO - (CONV_K - 1) + j, ts)
            dw_ref[pl.ds(j, 1), :] += jnp.sum(dy * tap, axis=0, keepdims=True)

    return _call(
        body, name=name, grid=(B, S // ts),
        in_specs=[cur, halo, tile, tile, vec, vec],
        out_specs=[tile,
                   pl.BlockSpec((HALO, CONV_CH), lambda b, s: (0, 0)),
                   pl.BlockSpec((8, CONV_CH), lambda b, s: (0, 0))],
        out_shape=[jax.ShapeDtypeStruct((B, S, CONV_CH), F32),
                   jax.ShapeDtypeStruct((HALO, CONV_CH), F32),
                   jax.ShapeDtypeStruct((8, CONV_CH), F32)],
        scratch_shapes=[pltpu.VMEM((ts + HALO, CONV_CH), F32),
                        pltpu.VMEM((SUBLANES - 1, ts + HALO, CONV_CH), F32)],
        compiler_params=_params(("arbitrary", "arbitrary")),
    )(ug, ug, y, dcat, ln_g, ln_b)


def conv_branch_bwd_b(ug, dy, conv_w, *, name, ts=256):
    B, S, _ = ug.shape
    ts = min(ts, S)
    nh, n_halo = ts // HALO, S // HALO

    def body(cur_ref, dy_ref, nxt_ref, w_ref, o_ref, ext, sh):
        last = pl.program_id(1) == pl.num_programs(1) - 1
        ext[pl.ds(0, ts), :] = dy_ref[0]
        ext[pl.ds(ts, HALO), :] = jnp.where(last, 0.0, nxt_ref[0])
        _make_shifted(ext, sh)
        da = jnp.zeros((ts, CONV_CH), F32)
        for j in range(CONV_K):
            da = da + _tap(ext, sh, CONV_K - 1 - j, ts) * w_ref[pl.ds(j, 1), :]
        blk = cur_ref[0]
        u = blk[:, :CONV_CH].astype(F32)
        sg = _sigmoid(blk[:, CONV_CH:].astype(F32))
        o_ref[0, :, :CONV_CH] = (da * sg).astype(BF16)
        o_ref[0, :, CONV_CH:] = (da * u * sg * (1.0 - sg)).astype(BF16)

    return _call(
        body, name=name, grid=(B, S // ts),
        in_specs=[pl.BlockSpec((1, ts, 2 * CONV_CH), lambda b, s: (b, s, 0)),
                  pl.BlockSpec((1, ts, CONV_CH), lambda b, s: (b, s, 0)),
                  pl.BlockSpec((1, HALO, CONV_CH), lambda b, s: (b, jnp.minimum((s + 1) * nh, n_halo - 1), 0)),
                  pl.BlockSpec((HALO, CONV_CH), lambda b, s: (0, 0))],
        out_specs=pl.BlockSpec((1, ts, 2 * CONV_CH), lambda b, s: (b, s, 0)),
        out_shape=jax.ShapeDtypeStruct((B, S, 2 * CONV_CH), BF16),
        scratch_shapes=[pltpu.VMEM((ts + HALO, CONV_CH), F32),
                        pltpu.VMEM((SUBLANES - 1, ts + HALO, CONV_CH), F32)],
        compiler_params=_params(("parallel", "parallel")),
    )(ug, dy, dy, conv_w)


def _tri(n, lower):
    r = lax.broadcasted_iota(jnp.int32, (n, n), 0)
    c = lax.broadcasted_iota(jnp.int32, (n, n), 1)
    return ((r >= c) if lower else (r <= c)).astype(F32)


def _dot_hi(a, b, dn):
    return lax.dot_general(a, b, dn, precision=lax.Precision.HIGHEST, preferred_element_type=F32)


NN = (((1,), (0,)), ((), ()))
NT = (((1,), (1,)), ((), ()))
TN = (((0,), (0,)), ((), ()))


def _log_sigmoid(v):
    e = jnp.exp(-jnp.abs(v))
    log1p_e = jnp.where(e < 1e-3, e * (1.0 - 0.5 * e), jnp.log(1.0 + e))
    return jnp.minimum(v, 0.0) - log1p_e


def fgate_fwd(h, w_f, b_f, *, name, ts=256, rider=None):
    B, S, D = h.shape
    ts = min(ts, S)

    def body(h_ref, w_ref, b_ref, f_ref, cc_ref, cr_ref, carry):
        @pl.when(pl.program_id(1) == 0)
        def _():
            carry[...] = jnp.zeros_like(carry)

        f = jnp.dot(h_ref[0], w_ref[...], preferred_element_type=F32)
        f_ref[0] = f
        logf = _log_sigmoid(f + b_ref[...])
        c = _dot_hi(_tri(ts, True), logf, NN) + carry[pl.ds(0, 1), :]
        cc_ref[0] = c
        carry[pl.ds(0, 1), :] = c[ts - 1:ts, :]
        cr_ref[0] = c.T

    return hosted_call(
        body, rider, name=name, grid=(B, S // ts),
        in_specs=[pl.BlockSpec((1, ts, D), lambda b, s: (b, s, 0)),
                  pl.BlockSpec((D, LANES), lambda b, s: (0, 0)),
                  pl.BlockSpec((1, LANES), lambda b, s: (0, 0))],
        out_specs=[pl.BlockSpec((1, ts, LANES), lambda b, s: (b, s, 0)),
                   pl.BlockSpec((1, ts, LANES), lambda b, s: (b, s, 0)),
                   pl.BlockSpec((1, LANES, ts), lambda b, s: (b, 0, s))],
        out_shape=[jax.ShapeDtypeStruct((B, S, LANES), F32), jax.ShapeDtypeStruct((B, S, LANES), F32),
                   jax.ShapeDtypeStruct((B, LANES, S), F32)],
        scratch_shapes=[pltpu.VMEM((8, LANES), F32)],
        args=(h, w_f, b_f),
    )


def fgate_bwd(dc, f, b_f, *, name, ts=256):
    B, S, _ = f.shape
    P = dc.shape[1]
    ts = min(ts, S)
    ns = S // ts

    def body(dc_ref, f_ref, b_ref, df_ref, db_ref, carry):
        @pl.when(pl.program_id(1) == 0)
        def _():
            carry[...] = jnp.zeros_like(carry)

        @pl.when((pl.program_id(0) == 0) & (pl.program_id(1) == 0))
        def _():
            db_ref[...] = jnp.zeros_like(db_ref)

        dc_t = dc_ref[0, 0]
        for j in range(1, P):
            dc_t = dc_t + dc_ref[0, j]
        dlogf = _dot_hi(_tri(ts, False), dc_t, NN) + carry[pl.ds(0, 1), :]
        carry[pl.ds(0, 1), :] = dlogf[0:1, :]
        df = dlogf * _sigmoid(-(f_ref[0] + b_ref[...]))
        df_ref[0] = df.astype(BF16)
        db_ref[...] += jnp.sum(df, axis=0, keepdims=True)

    return _call(
        body, name=name, grid=(B, ns),
        in_specs=[pl.BlockSpec((1, P, ts, LANES), lambda b, s: (b, 0, ns - 1 - s, 0)),
                  pl.BlockSpec((1, ts, LANES), lambda b, s: (b, ns - 1 - s, 0)),
                  pl.BlockSpec((1, LANES), lambda b, s: (0, 0))],
        out_specs=[pl.BlockSpec((1, ts, LANES), lambda b, s: (b, ns - 1 - s, 0)),
                   pl.BlockSpec((1, LANES), lambda b, s: (0, 0))],
        out_shape=[jax.ShapeDtypeStruct((B, S, LANES), BF16), jax.ShapeDtypeStruct((1, LANES), F32)],
        scratch_shapes=[pltpu.VMEM((8, LANES), F32)],
        compiler_params=_params(("arbitrary", "arbitrary")),
    )(dc, f, b_f)


def _lane_pick(tile, idx):
    lane = lax.broadcasted_iota(jnp.int32, tile.shape, 1)
    return jnp.sum(jnp.where(lane == idx, tile, 0.0), axis=-1, keepdims=True)


FOX_T = 512


def _fox_heads(q, cc_ref, p):
    lane = lax.broadcasted_iota(jnp.int32, q.shape, 1)
    qs = q * (1.0 / math.sqrt(FOX_HEAD_DIM))
    qhs = [jnp.where((lane < FOX_HEAD_DIM) == (hh == 0), qs, jnp.zeros_like(qs)) for hh in range(2)]
    crefs = [_lane_pick(cc_ref[0, pl.ds(0, 1), :], 2 * p + hh) for hh in range(2)]
    return qhs, crefs


def _fold_lanes(x, op):
    out = x[:, :LANES]
    for j in range(1, x.shape[1] // LANES):
        out = op(out, x[:, j * LANES:(j + 1) * LANES])
    return out


def _causal(t, transposed):
    r = lax.broadcasted_iota(jnp.int32, (t, t), 0)
    c = lax.broadcasted_iota(jnp.int32, (t, t), 1)
    return (r <= c) if transposed else (c <= r)


QKV0 = 8


def fox_fwd(z, c_col, c_row, *, name, rider=None):
    B, S, _ = z.shape
    assert S % FOX_T == 0
    tq, nq = FOX_T, S // FOX_T
    npair = FOX_HEADS // 2

    def body(q_ref, k_ref, v_ref, cc_ref, cr_ref, o_ref, l_ref, ot_ref, s_scr, m_scr, acc_scr):
        p, qi = pl.program_id(1), pl.program_id(2)
        qhs, crefs = _fox_heads(q_ref[0], cc_ref, p)
        lane = lax.broadcasted_iota(jnp.int32, (tq, LANES), 1)
        first = lane < FOX_HEAD_DIM
        for hh in range(2):
            m_scr[hh] = jnp.full((tq, LANES), NEG, F32)
            acc_scr[hh] = jnp.zeros((tq, LANES), F32)

        def logits(kb, diagonal):
            k0 = pl.multiple_of(kb * tq, tq)
            k = k_ref[0, pl.ds(k0, tq), :]
            for hh in range(2):
                s = lax.dot_general(qhs[hh], k, NT, preferred_element_type=F32)
                s = s + (crefs[hh] - cr_ref[0, pl.ds(2 * p + hh, 1), pl.ds(k0, tq)])
                if diagonal:
                    s = jnp.where(_causal(tq, False), s, NEG)
                s_scr[hh, kb] = s
                m_scr[hh] = jnp.maximum(m_scr[hh], _fold_lanes(s, jnp.maximum))

        def sweep1(kb, carry):
            logits(kb, False)
            return carry

        lax.fori_loop(0, qi, sweep1, 0)
        logits(qi, True)
        ms = [jnp.max(m_scr[hh], axis=-1, keepdims=True) for hh in range(2)]
        mbs = [jnp.broadcast_to(ms[hh], (tq, tq)) for hh in range(2)]

        for hh in range(2):
            m_scr[hh] = jnp.zeros((tq, LANES), F32)

        def weigh(kb, carry):
            k0 = pl.multiple_of(kb * tq, tq)
            v = v_ref[0, pl.ds(k0, tq), :]
            for hh in range(2):
                pr = jnp.exp(s_scr[hh, kb] - mbs[hh])
                m_scr[hh] += _fold_lanes(pr, jnp.add)
                acc_scr[hh] += jnp.dot(pr.astype(BF16), v, preferred_element_type=F32)
            return carry

        lax.fori_loop(0, qi + 1, weigh, 0)
        accs = [acc_scr[hh] for hh in range(2)]
        ls = [jnp.sum(m_scr[hh], axis=-1, keepdims=True) for hh in range(2)]
        out = jnp.where(first, accs[0] / ls[0], accs[1] / ls[1])
        o_ref[0] = out.astype(BF16)
        ot_ref[...] = out.T.astype(BF16)
        l_ref[0, 0] = jnp.where(first, ms[0] + jnp.log(ls[0]), ms[1] + jnp.log(ls[1]))

    return hosted_call(
        body, rider, name=name, grid=(B, npair, nq),
        in_specs=[pl.BlockSpec((1, tq, LANES), lambda b, p, i: (b, i, QKV0 + p)),
                  pl.BlockSpec((1, S, LANES), lambda b, p, i: (b, 0, QKV0 + npair + p)),
                  pl.BlockSpec((1, S, LANES), lambda b, p, i: (b, 0, QKV0 + 2 * npair + p)),
                  pl.BlockSpec((1, tq, LANES), lambda b, p, i: (b, i, 0)),
                  pl.BlockSpec((1, 8, S), lambda b, p, i: (b, 0, 0))],
        out_specs=[pl.BlockSpec((1, tq, LANES), lambda b, p, i: (b, i, p)),
                   pl.BlockSpec((1, 1, tq, LANES), lambda b, p, i: (b, p, i, 0)),
                   pl.BlockSpec((LANES, tq), lambda b, p, i: (p, b * nq + i))],
        out_shape=[jax.ShapeDtypeStruct((B, S, FOX_W), BF16),
                   jax.ShapeDtypeStruct((B, npair, S, LANES), F32),
                   jax.ShapeDtypeStruct((FOX_W, B * S), BF16)],
        scratch_shapes=[pltpu.VMEM((2, nq, tq, tq), F32), pltpu.VMEM((2, tq, LANES), F32),
                        pltpu.VMEM((2, tq, LANES), F32)],
        args=(z, z, z, c_col, c_row),
    )


def fox_bwd_dq(z, dcat, lse, c_col, c_row, *, name, rider=None):
    B, S, _ = z.shape
    tq, nq = FOX_T, S // FOX_T
    npair = FOX_HEADS // 2

    def body(q_ref, k_ref, v_ref, do_ref, l_ref, cc_ref, cr_ref, dq_ref, st_ref, p_scr, dp_scr, dl_scr):
        p, qi = pl.program_id(1), pl.program_id(2)
        qhs, crefs = _fox_heads(q_ref[0], cc_ref, p)
        lane = lax.broadcasted_iota(jnp.int32, (tq, LANES), 1)
        do_b = do_ref[0].astype(BF16)
        dohs = [jnp.where((lane < FOX_HEAD_DIM) == (hh == 0), do_b, jnp.zeros_like(do_b)) for hh in range(2)]
        lses = [_lane_pick(l_ref[0, 0], hh * FOX_HEAD_DIM) for hh in range(2)]
        lbs = [jnp.broadcast_to(lses[hh], (tq, tq)) for hh in range(2)]
        for hh in range(2):
            dl_scr[hh] = jnp.zeros((tq, LANES), F32)

        def probs(kb, diagonal):
            k0 = pl.multiple_of(kb * tq, tq)
            k = k_ref[0, pl.ds(k0, tq), :]
            v = v_ref[0, pl.ds(k0, tq), :]
            for hh in range(2):
                s = lax.dot_general(qhs[hh], k, NT, preferred_element_type=F32)
                s = s + (crefs[hh] - cr_ref[0, pl.ds(2 * p + hh, 1), pl.ds(k0, tq)])
                pr = jnp.exp(s - lbs[hh])
                if diagonal:
                    pr = jnp.where(_causal(tq, False), pr, 0.0)
                dp = lax.dot_general(dohs[hh], v, NT, preferred_element_type=F32)
                pdp = pr * dp
                dl_scr[hh] += _fold_lanes(pdp, jnp.add)
                p_scr[hh, kb] = pr
                dp_scr[hh, kb] = dp

        def first_pass(kb, carry):
            probs(kb, False)
            return carry

        lax.fori_loop(0, qi, first_pass, 0)
        probs(qi, True)

        dls = [jnp.sum(dl_scr[hh], axis=-1, keepdims=True) for hh in range(2)]
        dlbs = [jnp.broadcast_to(dls[hh], (tq, tq)) for hh in range(2)]

        def second_pass(kb, dq):
            k0 = pl.multiple_of(kb * tq, tq)
            k = k_ref[0, pl.ds(k0, tq), :]
            for hh in range(2):
                ds = p_scr[hh, kb] * (dp_scr[hh, kb] - dlbs[hh])
                kh = jnp.where((lane < FOX_HEAD_DIM) == (hh == 0), k, jnp.zeros_like(k))
                dq = dq + jnp.dot(ds.astype(BF16), kh, preferred_element_type=F32)
            return dq

        dq = lax.fori_loop(0, qi + 1, second_pass, jnp.zeros((tq, LANES), F32))
        dq_ref[0] = (dq * (1.0 / math.sqrt(FOX_HEAD_DIM))).astype(BF16)
        cols = jnp.zeros((tq, LANES), F32)
        for j, col in enumerate([crefs[0] - lses[0], crefs[1] - lses[1], dls[0], dls[1]]):
            cols = jnp.where(lane == j, col, cols)
        st_ref[0, 0] = cols.T[:8]

    return hosted_call(
        body, rider, name=name, grid=(B, npair, nq),
        in_specs=[pl.BlockSpec((1, tq, LANES), lambda b, p, i: (b, i, QKV0 + p)),
                  pl.BlockSpec((1, S, LANES), lambda b, p, i: (b, 0, QKV0 + npair + p)),
                  pl.BlockSpec((1, S, LANES), lambda b, p, i: (b, 0, QKV0 + 2 * npair + p)),
                  pl.BlockSpec((1, tq, LANES), lambda b, p, i: (b, i, npair + p)),
                  pl.BlockSpec((1, 1, tq, LANES), lambda b, p, i: (b, p, i, 0)),
                  pl.BlockSpec((1, tq, LANES), lambda b, p, i: (b, i, 0)),
                  pl.BlockSpec((1, 8, S), lambda b, p, i: (b, 0, 0))],
        out_specs=[pl.BlockSpec((1, tq, LANES), lambda b, p, i: (b, i, p)),
                   pl.BlockSpec((1, 1, 8, tq), lambda b, p, i: (b, p, 0, i))],
        out_shape=[jax.ShapeDtypeStruct((B, S, FOX_W), BF16), jax.ShapeDtypeStruct((B, npair, 8, S), F32)],
        scratch_shapes=[pltpu.VMEM((2, nq, tq, tq), F32), pltpu.VMEM((2, nq, tq, tq), F32),
                        pltpu.VMEM((2, tq, LANES), F32)],
        args=(z, z, z, dcat, lse, c_col, c_row), vmem=VMEM_FOX_BWD,
    )


def fox_bwd_dkdv(z, dcat, stats, c_col, *, name, rider=None):
    B, S, _ = z.shape
    tk, nq = FOX_T, S // FOX_T
    npair = FOX_HEADS // 2
    inv = 1.0 / math.sqrt(FOX_HEAD_DIM)

    def body(q_ref, k_ref, v_ref, do_ref, st_ref, cc_ref, dk_ref, dv_ref, dc_ref, dk_scr, dv_scr, dc_scr):
        p, kt = pl.program_id(1), pl.program_id(2)
        lane = lax.broadcasted_iota(jnp.int32, (tk, LANES), 1)
        masks = [(lane < FOX_HEAD_DIM) == (hh == 0) for hh in range(2)]
        k = k_ref[0]
        v = v_ref[0]
        khs = [jnp.where(masks[hh], k, jnp.zeros_like(k)) for hh in range(2)]
        vhs = [jnp.where(masks[hh], v, jnp.zeros_like(v)) for hh in range(2)]
        ccbs = [jnp.broadcast_to(_lane_pick(cc_ref[0], 2 * p + hh), (tk, tk)) for hh in range(2)]
        dk_scr[...] = jnp.zeros_like(dk_scr)
        dv_scr[...] = jnp.zeros_like(dv_scr)
        dc_scr[...] = jnp.zeros_like(dc_scr)

        def tile(qb, diagonal):
            q0 = pl.multiple_of(qb * tk, tk)
            qs = q_ref[0, pl.ds(q0, tk), :] * inv
            do_b = do_ref[0, pl.ds(q0, tk), :].astype(BF16)
            for hh in range(2):
                st = lax.dot_general(khs[hh], qs, NT, preferred_element_type=F32)
                pr = jnp.exp(st - ccbs[hh] + st_ref[0, 0, pl.ds(hh, 1), pl.ds(q0, tk)])
                if diagonal:
                    pr = jnp.where(_causal(tk, True), pr, 0.0)
                dp = lax.dot_general(vhs[hh], do_b, NT, preferred_element_type=F32)
                ds = pr * (dp - st_ref[0, 0, pl.ds(2 + hh, 1), pl.ds(q0, tk)])
                dv_scr[...] += jnp.dot(pr.astype(BF16), jnp.where(masks[hh], do_b, jnp.zeros_like(do_b)),
                                       preferred_element_type=F32)
                dk_scr[...] += jnp.dot(ds.astype(BF16), jnp.where(masks[hh], qs, jnp.zeros_like(qs)),
                                       preferred_element_type=F32)
                dc_scr[hh] -= _fold_lanes(ds, jnp.add)

        def later(qb, carry):
            tile(qb, False)
            return carry

        tile(kt, True)
        lax.fori_loop(kt + 1, nq, later, 0)
        dk_ref[0] = dk_scr[...].astype(BF16)
        dv_ref[0] = dv_scr[...].astype(BF16)
        dcs = [jnp.sum(dc_scr[hh], axis=-1, keepdims=True) for hh in range(2)]
        dc_ref[0, 0] = jnp.where(lane == 2 * p, dcs[0], jnp.where(lane == 2 * p + 1, dcs[1], 0.0))

    full = lambda col: pl.BlockSpec((1, S, LANES), col)
    tile_spec = lambda col: pl.BlockSpec((1, tk, LANES), col)
    return hosted_call(
        body, rider, name=name, grid=(B, npair, nq),
        in_specs=[full(lambda b, p, t: (b, 0, QKV0 + p)),
                  tile_spec(lambda b, p, t: (b, t, QKV0 + npair + p)),
                  tile_spec(lambda b, p, t: (b, t, QKV0 + 2 * npair + p)),
                  full(lambda b, p, t: (b, 0, npair + p)),
                  pl.BlockSpec((1, 1, 8, S), lambda b, p, t: (b, p, 0, 0)),
                  tile_spec(lambda b, p, t: (b, t, 0))],
        out_specs=[tile_spec(lambda b, p, t: (b, t, p)), tile_spec(lambda b, p, t: (b, t, p)),
                   pl.BlockSpec((1, 1, tk, LANES), lambda b, p, t: (b, p, t, 0))],
        out_shape=[jax.ShapeDtypeStruct((B, S, FOX_W), BF16)] * 2
        + [jax.ShapeDtypeStruct((B, npair, S, LANES), F32)],
        scratch_shapes=[pltpu.VMEM((tk, LANES), F32), pltpu.VMEM((tk, LANES), F32),
                        pltpu.VMEM((2, tk, LANES), F32)],
        args=(z, z, z, dcat, stats, c_col),
    )


def xattn_fwd(qm, kv, *, name, tq=512):
    B, S, D = qm.shape
    M = kv.shape[1]
    tq = min(tq, S)
    inv = 1.0 / math.sqrt(MEM_HEAD_DIM)

    nq = S // tq

    def body(q_ref, kv_ref, o_ref, ot_ref):
        for h in range(MEM_HEADS):
            c0 = h * MEM_HEAD_DIM
            qh = q_ref[0, :, c0:c0 + MEM_HEAD_DIM]
            kh = kv_ref[0, :, c0:c0 + MEM_HEAD_DIM]
            vh = kv_ref[0, :, D + c0:D + c0 + MEM_HEAD_DIM]
            s = lax.dot_general(qh, kh, NT, preferred_element_type=F32) * inv
            e = jnp.exp(s - jnp.max(s, axis=-1, keepdims=True))
            o = jnp.dot(e.astype(BF16), vh, preferred_element_type=F32) / jnp.sum(e, axis=-1, keepdims=True)
            o_ref[0, :, c0:c0 + MEM_HEAD_DIM] = o.astype(BF16)
            ot_ref[c0:c0 + MEM_HEAD_DIM, :] = o.T.astype(BF16)

    return _call(
        body, name=name, grid=(B, nq),
        in_specs=[pl.BlockSpec((1, tq, D), lambda b, i: (b, i, 0)),
                  pl.BlockSpec((1, M, 2 * D), lambda b, i: (b, 0, 0))],
        out_specs=[pl.BlockSpec((1, tq, D), lambda b, i: (b, i, 0)),
                   pl.BlockSpec((D, tq), lambda b, i: (0, b * nq + i))],
        out_shape=[jax.ShapeDtypeStruct((B, S, D), BF16), jax.ShapeDtypeStruct((D, B * S), BF16)],
        compiler_params=_params(("parallel", "parallel")),
    )(qm, kv)


def xattn_bwd(qm, kv, do, *, name, tq=512):
    B, S, D = qm.shape
    M = kv.shape[1]
    tq = min(tq, S)
    inv = 1.0 / math.sqrt(MEM_HEAD_DIM)

    def body(q_ref, kv_ref, do_ref, dq_ref, dkv_ref):
        @pl.when(pl.program_id(1) == 0)
        def _():
            dkv_ref[...] = jnp.zeros_like(dkv_ref)

        for h in range(MEM_HEADS):
            c0 = h * MEM_HEAD_DIM
            qh = q_ref[0, :, c0:c0 + MEM_HEAD_DIM]
            kh = kv_ref[0, :, c0:c0 + MEM_HEAD_DIM]
            vh = kv_ref[0, :, D + c0:D + c0 + MEM_HEAD_DIM]
            doh = do_ref[0, :, c0:c0 + MEM_HEAD_DIM]
            s = lax.dot_general(qh, kh, NT, preferred_element_type=F32) * inv
            e = jnp.exp(s - jnp.max(s, axis=-1, keepdims=True))
            pr = e / jnp.sum(e, axis=-1, keepdims=True)
            dp = lax.dot_general(doh, vh, NT, preferred_element_type=F32)
            ds = pr * (dp - jnp.sum(pr * dp, axis=-1, keepdims=True))
            ds_b = ds.astype(BF16)
            dq_ref[0, :, c0:c0 + MEM_HEAD_DIM] = (jnp.dot(ds_b, kh, preferred_element_type=F32) * inv).astype(BF16)
            dkv_ref[0, :, c0:c0 + MEM_HEAD_DIM] += lax.dot_general(ds_b, qh, TN, preferred_element_type=F32) * inv
            dkv_ref[0, :, D + c0:D + c0 + MEM_HEAD_DIM] += lax.dot_general(
                pr.astype(BF16), doh, TN, preferred_element_type=F32)

    row = pl.BlockSpec((1, tq, D), lambda b, i: (b, i, 0))
    kvs = pl.BlockSpec((1, M, 2 * D), lambda b, i: (b, 0, 0))
    return _call(
        body, name=name, grid=(B, S // tq), in_specs=[row, kvs, row], out_specs=[row, kvs],
        out_shape=[jax.ShapeDtypeStruct((B, S, D), BF16), jax.ShapeDtypeStruct((B, M, 2 * D), F32)],
        compiler_params=_params(("parallel", "arbitrary")),
    )(qm, kv, do)


SWIGLU_TN = 2816


def _chunks(n, w=256):
    return [(c0, min(w, n - c0)) for c0 in range(0, n, w)]


def mm_swiglu_fwd(hf, w_gu, *, name, tm=256):
    T, D = hf.shape
    Fh = w_gu.shape[1] // 2
    tm, tn = min(tm, T), SWIGLU_TN
    nj = Fh // tn
    assert Fh % tn == 0 and T % tm == 0

    def body(a_ref, bg_ref, bu_ref, g_ref, u_ref, o_ref, ot_ref):
        a = a_ref[...]
        for c0, cw in _chunks(tn):
            cols = pl.ds(c0, cw)
            g = jnp.dot(a, bg_ref[:, cols], preferred_element_type=F32)
            u = jnp.dot(a, bu_ref[:, cols], preferred_element_type=F32)
            act = g * _sigmoid(g) * u
            g_ref[:, cols] = g.astype(BF16)
            u_ref[:, cols] = u.astype(BF16)
            o_ref[:, cols] = act.astype(BF16)
            ot_ref[cols, :] = act.T.astype(BF16)

    tile = pl.BlockSpec((tm, tn), lambda i, j: (i, j))
    return _call(
        body, name=name, grid=(T // tm, nj),
        in_specs=[pl.BlockSpec((tm, D), lambda i, j: (i, 0)), pl.BlockSpec((D, tn), lambda i, j: (0, j)),
                  pl.BlockSpec((D, tn), lambda i, j: (0, nj + j))],
        out_specs=[tile, tile, tile, pl.BlockSpec((tn, tm), lambda i, j: (j, i))],
        out_shape=[jax.ShapeDtypeStruct((T, Fh), BF16)] * 3 + [jax.ShapeDtypeStruct((Fh, T), BF16)],
        compiler_params=_params(("parallel", "parallel"), VMEM_SWIGLU),
    )(hf, w_gu, w_gu)


def mm_swiglu_bwd(dx, w_down, g, u, *, name, tm=256):
    T, D = dx.shape
    Fh = w_down.shape[0]
    tm, tn = min(tm, T), SWIGLU_TN
    assert Fh % tn == 0 and T % tm == 0

    def body(a_ref, b_ref, g_ref, u_ref, dg_ref, du_ref):
        a = a_ref[...].astype(BF16)
        for c0, cw in _chunks(tn):
            cols = pl.ds(c0, cw)
            d = lax.dot_general(a, b_ref[cols, :], NT, preferred_element_type=F32)
            gv = g_ref[:, cols].astype(F32)
            uv = u_ref[:, cols].astype(F32)
            sg = _sigmoid(gv)
            dg_ref[:, cols] = (d * uv * (sg * (1.0 + gv * (1.0 - sg)))).astype(BF16)
            du_ref[:, cols] = (d * gv * sg).astype(BF16)

    tile = pl.BlockSpec((tm, tn), lambda i, j: (i, j))
    return _call(
        body, name=name, grid=(T // tm, Fh // tn),
        in_specs=[pl.BlockSpec((tm, D), lambda i, j: (i, 0)), pl.BlockSpec((tn, D), lambda i, j: (j, 0)), tile, tile],
        out_specs=[tile, tile],
        out_shape=[jax.ShapeDtypeStruct((T, Fh), BF16)] * 2,
        compiler_params=_params(("parallel", "parallel"), VMEM_SWIGLU),
    )(dx, w_down, g, u)


LATE_MID = ("w_out", "w_mq", "w_mo")
LATE_KV = ("w_mkv",)
LATE_FFN = ("w_gu", "w_down")
LATE = LATE_MID + LATE_KV + LATE_FFN
RS_GROUPS = (("w_gu", "w_down"), ("w_out", "w_mq", "w_mkv", "w_mo"), ("w_in",))


def pair_sums(names, g42, got):
    return {n: pair_sum(g, o, name="rs_pair_sum_" + n) for n, g, o in zip(names, g42, got)}


def local_step(x, mem, target, sp, first_shards, late_shards):
    B, S, D = x.shape
    T = B * S
    M = mem.shape[1]
    row = lambda v: v.reshape(1, -1).astype(F32)
    g_mix, g_x, g_mem, g_ffn, g_final = (row(sp[k]) for k in ("g_mix", "g_x", "g_mem", "g_ffn", "g_final"))
    conv_b, ln_g, ln_b = row(sp["conv_b"]), row(sp["ln_g"]), row(sp["ln_b"])
    b_f = jnp.pad(row(sp["b_f"]), ((0, 0), (0, LANES - FOX_HEADS)))
    n_ug, n_main = 2 * CONV_CH, 2 * CONV_CH + 3 * FOX_W

    x2d = x.reshape(T, D)
    h, h_t, partly = rmsnorm_fwd(x2d, g_mix, name="rms_mix", rider=AllGatherStage1(first_shards))
    w_in8, cw8 = run_rider(AllGatherStage2(partly), name="ag_first_stage2")
    w_in_full = _full_from_gathered("w_in", w_in8)
    conv_w = cw8.transpose(1, 0, 2).reshape(HALO, -1)
    w_main, w_ug, w_qkv = w_in_full[:, :n_main], w_in_full[:, :n_ug], w_in_full[:, n_ug:n_main]
    w_f = jnp.pad(w_in_full[:, n_main:], ((0, 0), (0, LANES - FOX_HEADS)))
    z = matmul(h, w_main, out_dtype=BF16, tn=n_main, name="mm_in")
    z3 = z.reshape(B, S, n_main)
    n_mid, n_kv = len(LATE_MID), len(LATE_MID) + len(LATE_KV)
    (conv_out, conv_t, conv_y), partly_mid = conv_branch_fwd(z3, conv_w, conv_b, ln_g, ln_b, name="conv_fwd",
                                                     rider=AllGatherStage1(late_shards[:n_mid]))
    (f_raw, c_col, c_row), rode = fgate_fwd(
        h.reshape(B, S, D), w_f, b_f, name="fgate_fwd",
        rider=Riders(AllGatherStage1(late_shards[n_mid:n_kv]), AllGatherStage2(partly_mid)))
    partly_kv, full_mid = rode[:n_kv - n_mid], rode[n_kv - n_mid:]
    (att, lse, att_t), rode = fox_fwd(
        z3, c_col, c_row, name="fox_fwd",
        rider=Riders(AllGatherStage1(late_shards[n_kv:]), AllGatherStage2(partly_kv)))
    partly_ffn, full_kv = rode[:len(LATE_FFN)], rode[len(LATE_FFN):]
    wf = {n: _full_from_gathered(n, blk) for n, blk in zip(LATE_MID + LATE_KV, full_mid + full_kv)}
    (x1, hx, hx_t), full_ffn = matmul(
        [conv_out.reshape(T, CONV_CH), att.reshape(T, FOX_W)], [wf["w_out"], wf["w_out"]], b_blk=[0, 1],
        res=x2d, tn=D, name="mm_out", post=rms_fwd_epilogue(g_x), rider=AllGatherStage2(partly_ffn))
    wf.update({n: _full_from_gathered(n, blk) for n, blk in zip(LATE_FFN, full_ffn)})
    qm = matmul(hx, wf["w_mq"], out_dtype=BF16, tn=D, name="mm_mq")
    mem2d = mem.reshape(B * M, D)
    mem_n, mem_n_t = rmsnorm_fwd(mem2d, g_mem, name="rms_mem")
    kv = matmul(mem_n, wf["w_mkv"], out_dtype=BF16, tn=2 * D, name="mm_mkv").reshape(B, M, 2 * D)
    o, o_t = xattn_fwd(qm.reshape(B, S, D), kv, name="xattn_fwd")
    o = o.reshape(T, D)
    x2, hf, hf_t = matmul(o, wf["w_mo"], res=x1, tn=D, name="mm_mo", post=rms_fwd_epilogue(g_ffn))
    gate, up, act, act_t = mm_swiglu_fwd(hf, wf["w_gu"], name="mm_gu")
    dx3, dg_final, loss = matmul(act, wf["w_down"], res=x2, tn=D, name="mm_down",
                                 post=loss_epilogue(g_final, target.reshape(T, D)))
    gw = {}
    gw["w_down"] = matmul(act_t, dx3, out_dtype=BF16, tm=1408, tn=512, name="dw_down")
    dgate, dup = mm_swiglu_bwd(dx3, wf["w_down"], gate, up, name="dx_down")
    gw["w_gu"] = [matmul(hf_t, dgate, out_dtype=BF16, tn=1408, name="dw_gate"),
                  matmul(hf_t, dup, out_dtype=BF16, tn=1408, name="dw_up")]
    g42 = [_shards_from_full(n, gw[n]) for n in RS_GROUPS[0]]
    (dx2, dg_ffn), got = matmul([dgate, dup], [wf["w_gu"], wf["w_gu"]], b_blk=[0, 1], tb=True, tm=256, tn=D,
                                name="dx_gu", post=rms_bwd_epilogue(x2, g_ffn, dx3), rider=SiblingExchange(g42))
    parts = pair_sums(RS_GROUPS[0], g42, got)
    gw["w_mo"] = matmul(o_t, dx2, out_dtype=BF16, tn=D, name="dw_mo")
    do = matmul(dx2, wf["w_mo"], tb=True, out_dtype=BF16, tn=D, name="dx_mo")
    dqm, dkv = xattn_bwd(qm.reshape(B, S, D), kv, do.reshape(B, S, D), name="xattn_bwd")
    dqm = dqm.reshape(T, D)
    dkv = dkv.reshape(B * M, 2 * D)
    gw["w_mq"] = matmul(hx_t, dqm, out_dtype=BF16, tn=D, name="dw_mq")
    dx1, dg_x = matmul(dqm, wf["w_mq"], tb=True, tn=D, name="dx_mq", post=rms_bwd_epilogue(x1, g_x, dx2))
    gw["w_mkv"] = matmul(mem_n_t, dkv, out_dtype=BF16, tn=D, name="dw_mkv")
    _, dg_mem = matmul(dkv, wf["w_mkv"], tb=True, tn=D, name="dx_mkv", post=rms_bwd_epilogue(mem2d, g_mem, None))
    gw["w_out"] = jnp.concatenate([matmul(conv_t, dx1, out_dtype=BF16, tn=D, name="dw_out_conv"),
                                   matmul(att_t, dx1, out_dtype=BF16, tn=D, name="dw_out_att")], axis=0)
    g42 = [_shards_from_full(n, gw[n]) for n in RS_GROUPS[1]]
    dcat, got = matmul(dx1, wf["w_out"], tb=True, out_dtype=BF16, tn=D, name="dx_out", rider=SiblingExchange(g42))
    dcat = dcat.reshape(B, S, D)
    parts.update(pair_sums(RS_GROUPS[1], g42, got))
    dy, dconv_w, dvec = conv_branch_bwd_a(z3, conv_y, dcat, ln_g, ln_b, name="conv_bwd_a")
    dug = conv_branch_bwd_b(z3, dy, conv_w, name="conv_bwd_b")
    gots = {}
    (dq, stats), got = fox_bwd_dq(z3, dcat, lse, c_col, c_row, name="fox_bwd_dq",
                                  rider=ChipExchange([parts[n] for n in RS_GROUPS[0]]))
    gots.update(zip(RS_GROUPS[0], got))
    (dk, dv, dc), got = fox_bwd_dkdv(z3, dcat, stats, c_col, name="fox_bwd_dkdv",
                                     rider=ChipExchange([parts[n] for n in RS_GROUPS[1]]))
    gots.update(zip(RS_GROUPS[1], got))
    df, db_f = fgate_bwd(dc, f_raw, b_f, name="fgate_bwd")
    dug2 = dug.reshape(T, n_ug)
    dqkv = jnp.concatenate([dq, dk, dv], axis=-1).reshape(T, 3 * FOX_W)
    df2 = df.reshape(T, LANES)
    dw_in = [matmul(h_t, dug2, out_dtype=BF16, tn=n_ug, name="dw_in_ug"),
             matmul(h_t, dqkv, out_dtype=BF16, tn=3 * FOX_W, name="dw_in_qkv"),
             matmul(h_t, df2, out_dtype=BF16, name="dw_f")[:, :FOX_HEADS]]
    g42 = [_shards_from_full("w_in", dw_in)]
    parts.update(pair_sums(RS_GROUPS[2], g42, run_rider(SiblingExchange(g42), name="rs_sibling_in")))
    (dx, dg_mix), (gots["w_in"],) = matmul(
        [dug2, dqkv, df2], [w_ug, w_qkv, w_f], tb=True, tn=D, name="dx_in",
        post=rms_bwd_epilogue(x2d, g_mix, dx1, out_dtype=F32), rider=ChipExchange([parts["w_in"]]))
    gs = dict(g_mix=dg_mix, b_f=db_f[:, :FOX_HEADS], conv_w=dconv_w[:CONV_K], conv_b=dvec[0:1],
              ln_g=dvec[1:2], ln_b=dvec[2:3], g_x=dg_x, g_mem=dg_mem, g_ffn=dg_ffn, g_final=dg_final)
    return loss, dx.reshape(B, S, D), gs, {n: (parts[n], gots[n]) for n in BIG}


def _me():
    return lax.axis_index("x"), lax.axis_index("y"), lax.axis_index("c")


def _any_specs(n):
    return [pl.BlockSpec(memory_space=pl.ANY)] * n


def all_gather(xs, *, name):
    n = len(xs)

    def body(*refs):
        x_refs, out_refs = refs[:n], refs[n:2 * n]
        send_sems, recv_sems, local_sems = refs[2 * n:]
        x, y, c = _me()
        me, sibling = (x, y, c), (x, y, 1 - c)
        chips = [(1 - x, y), (x, 1 - y), (1 - x, 1 - y)]

        def slot(a, px, py, pc):
            return out_refs[a].at[4 * px + 2 * py + pc]

        def copy(a, k, block, to, own=False):
            return pltpu.make_async_remote_copy(
                src_ref=x_refs[a] if own else slot(a, *block), dst_ref=slot(a, *block),
                send_sem=send_sems.at[k, a], recv_sem=recv_sems.at[k, a], device_id=to, device_id_type=MESH)

        mine = [pltpu.make_async_copy(x_refs[a], slot(a, *me), local_sems.at[a]) for a in range(n)]
        first = [copy(a, 0, me, sibling, own=True) for a in range(n)]
        first += [copy(a, 1 + j, me, (*chip, c), own=True) for j, chip in enumerate(chips) for a in range(n)]
        for cp in mine + first:
            cp.start()
        passed = []
        for j, chip in enumerate(chips):
            for a in range(n):
                copy(a, 1 + j, (*chip, c), me).wait_recv()
                passed.append(copy(a, 4 + j, (*chip, c), sibling))
                passed[-1].start()
        for a in range(n):
            copy(a, 0, sibling, me).wait_recv()
            for j, chip in enumerate(chips):
                copy(a, 4 + j, (*chip, 1 - c), me).wait_recv()
        for cp in first + passed:
            cp.wait_send()
        for cp in mine:
            cp.wait()

    return _call(
        body, name=name, in_specs=_any_specs(n), out_specs=_any_specs(n),
        out_shape=[jax.ShapeDtypeStruct((N_DEV,) + v.shape, v.dtype) for v in xs],
        scratch_shapes=[pltpu.SemaphoreType.DMA((7, n)), pltpu.SemaphoreType.DMA((7, n)),
                        pltpu.SemaphoreType.DMA((n,))],
    )(*xs)


SIBLING_BARRIER = 1
CHIPS_BARRIER = 2
GATHER_BARRIER = 3


class SiblingExchange:
    collective_id = SIBLING_BARRIER

    def __init__(self, gs):
        n = len(gs)
        self.n, self.inputs = n, list(gs)
        self.out_shape = [jax.ShapeDtypeStruct((4,) + g.shape[2:], g.dtype) for g in gs]
        self.scratch = [pltpu.SemaphoreType.DMA((n,)), pltpu.SemaphoreType.DMA((n,))]

    @staticmethod
    def barrier_peers():
        x, y, c = _me()
        return [(x, y, 1 - c)]

    def _copies(self, g_refs, out_refs, sems):
        send_sems, recv_sems = sems
        x, y, c = _me()
        return [pltpu.make_async_remote_copy(
            src_ref=g_refs[a].at[:, 1 - c], dst_ref=out_refs[a], send_sem=send_sems.at[a],
            recv_sem=recv_sems.at[a], device_id=(x, y, 1 - c), device_id_type=MESH) for a in range(self.n)]

    def start(self, in_refs, out_refs, sems):
        for cp in self._copies(in_refs, out_refs, sems):
            cp.start()

    def finish(self, in_refs, out_refs, sems):
        for cp in self._copies(in_refs, out_refs, sems):
            cp.wait()


def run_rider(rider, *, name):
    return hosted_call(None, rider, name=name, grid=(), in_specs=[], out_specs=[], out_shape=[],
                       scratch_shapes=[], args=[])[1]


class ChipExchange:
    collective_id = CHIPS_BARRIER

    @staticmethod
    def barrier_peers():
        x, y, c = _me()
        return [(1 - x, y, c), (x, 1 - y, c), (1 - x, 1 - y, c)]

    def __init__(self, ps):
        n = len(ps)
        self.n, self.inputs = n, list(ps)
        self.out_shape = [jax.ShapeDtypeStruct(p.shape, p.dtype) for p in ps]
        self.scratch = [pltpu.SemaphoreType.DMA((3, n)), pltpu.SemaphoreType.DMA((3, n))]

    def _copies(self, p_refs, out_refs, sems, outgoing):
        send_sems, recv_sems = sems
        x, y, c = _me()
        my_chip = 2 * x + y
        cps = []
        for k in range(3):
            px, py = x ^ ((k + 1) >> 1), y ^ ((k + 1) & 1)
            src, dst = (2 * px + py, my_chip) if outgoing else (my_chip, 2 * px + py)
            for a in range(self.n):
                cps.append(pltpu.make_async_remote_copy(
                    src_ref=p_refs[a].at[src], dst_ref=out_refs[a].at[dst], send_sem=send_sems.at[k, a],
                    recv_sem=recv_sems.at[k, a], device_id=(px, py, c), device_id_type=MESH))
        return cps

    def start(self, in_refs, out_refs, sems):
        for cp in self._copies(in_refs, out_refs, sems, True):
            cp.start()

    def finish(self, in_refs, out_refs, sems):
        for cp in self._copies(in_refs, out_refs, sems, False):
            cp.wait_recv()
        for cp in self._copies(in_refs, out_refs, sems, True):
            cp.wait_send()


class AllGatherStage1:
    collective_id = GATHER_BARRIER

    @staticmethod
    def barrier_peers():
        x, y, c = _me()
        return [(x, y, 1 - c), (1 - x, y, c), (x, 1 - y, c), (1 - x, 1 - y, c)]

    def __init__(self, xs):
        n = len(xs)
        self.n, self.inputs = n, list(xs)
        self.out_shape = [jax.ShapeDtypeStruct((N_DEV,) + v.shape, v.dtype) for v in xs]
        self.scratch = [pltpu.SemaphoreType.DMA((4, n)), pltpu.SemaphoreType.DMA((4, n)),
                        pltpu.SemaphoreType.DMA((n,))]

    def _copies(self, x_refs, out_refs, sems, kind):
        send_sems, recv_sems, local_sems = sems
        x, y, c = _me()
        slot = lambda a, d: out_refs[a].at[4 * d[0] + 2 * d[1] + d[2]]
        if kind == "local":
            return [pltpu.make_async_copy(x_refs[a], slot(a, (x, y, c)), local_sems.at[a]) for a in range(self.n)]
        cps = []
        for k, peer in enumerate([(x, y, 1 - c), (1 - x, y, c), (x, 1 - y, c), (1 - x, 1 - y, c)]):
            for a in range(self.n):
                cps.append(pltpu.make_async_remote_copy(
                    src_ref=x_refs[a], dst_ref=slot(a, (x, y, c) if kind == "out" else peer),
                    send_sem=send_sems.at[k, a], recv_sem=recv_sems.at[k, a], device_id=peer, device_id_type=MESH))
        return cps

    def start(self, in_refs, out_refs, sems):
        for cp in self._copies(in_refs, out_refs, sems, "local") + self._copies(in_refs, out_refs, sems, "out"):
            cp.start()

    def finish(self, in_refs, out_refs, sems):
        for cp in self._copies(in_refs, out_refs, sems, "in"):
            cp.wait_recv()
        for cp in self._copies(in_refs, out_refs, sems, "out"):
            cp.wait_send()
        for cp in self._copies(in_refs, out_refs, sems, "local"):
            cp.wait()


class AllGatherStage2:
    collective_id = SIBLING_BARRIER

    @staticmethod
    def barrier_peers():
        x, y, c = _me()
        return [(x, y, 1 - c)]

    def __init__(self, outs):
        n = len(outs)
        self.n, self.inputs = n, list(outs)
        self.out_shape = [jax.ShapeDtypeStruct(o.shape, o.dtype) for o in outs]
        self.scratch = [pltpu.SemaphoreType.DMA((3, n)), pltpu.SemaphoreType.DMA((3, n))]
        self.aliases = {a: a for a in range(n)}

    def _copies(self, out_refs, sems, outgoing):
        send_sems, recv_sems = sems
        x, y, c = _me()
        cps = []
        for k, (px, py) in enumerate([(1 - x, y), (x, 1 - y), (1 - x, 1 - y)]):
            for a in range(self.n):
                cps.append(pltpu.make_async_remote_copy(
                    src_ref=out_refs[a].at[4 * px + 2 * py + c],
                    dst_ref=out_refs[a].at[4 * px + 2 * py + (c if outgoing else 1 - c)],
                    send_sem=send_sems.at[k, a], recv_sem=recv_sems.at[k, a], device_id=(x, y, 1 - c),
                    device_id_type=MESH))
        return cps

    def start(self, in_refs, out_refs, sems):
        for cp in self._copies(out_refs, sems, True):
            cp.start()

    def finish(self, in_refs, out_refs, sems):
        for cp in self._copies(out_refs, sems, False):
            cp.wait_recv()
        for cp in self._copies(out_refs, sems, True):
            cp.wait_send()


class Riders:
    def __init__(self, *riders):
        self.riders = riders
        self.collective_id = riders[0].collective_id
        self.barrier_peers = riders[0].barrier_peers
        self.inputs = [v for r in riders for v in r.inputs]
        self.out_shape = [s for r in riders for s in r.out_shape]
        self.scratch = [s for r in riders for s in r.scratch]
        self.aliases, i0, o0 = {}, 0, 0
        for r in riders:
            self.aliases.update({i0 + i: o0 + o for i, o in getattr(r, "aliases", {}).items()})
            i0, o0 = i0 + len(r.inputs), o0 + len(r.out_shape)

    def _split(self, in_refs, out_refs, sems):
        i0 = o0 = s0 = 0
        for r in self.riders:
            ni, no, ns = len(r.inputs), len(r.out_shape), len(r.scratch)
            yield r, in_refs[i0:i0 + ni], out_refs[o0:o0 + no], sems[s0:s0 + ns]
            i0, o0, s0 = i0 + ni, o0 + no, s0 + ns

    def start(self, in_refs, out_refs, sems):
        for r, i, o, s in self._split(in_refs, out_refs, sems):
            r.start(i, o, s)

    def finish(self, in_refs, out_refs, sems):
        for r, i, o, s in self._split(in_refs, out_refs, sems):
            r.finish(i, o, s)


def _peer_barrier(peers):
    barrier = pltpu.get_barrier_semaphore()
    for peer in peers:
        pl.semaphore_signal(barrier, inc=1, device_id=peer, device_id_type=MESH)
    pl.semaphore_wait(barrier, len(peers))


def hosted_call(body, rider, *, name, grid, in_specs, out_specs, out_shape, scratch_shapes, args, vmem=None):
    n_in, n_out, n_scr = len(in_specs), len(out_specs), len(scratch_shapes)
    r_in, r_out = (len(rider.inputs), len(rider.out_shape)) if rider is not None else (0, 0)
    own_barrier = getattr(rider, "collective_id", None) is not None

    def wrapped(*refs):
        ins, refs = refs[:n_in], refs[n_in:]
        rins, refs = refs[:r_in], refs[r_in:]
        outs, refs = refs[:n_out], refs[n_out:]
        routs, refs = refs[:r_out], refs[r_out:]
        scr, rscr = refs[:n_scr], refs[n_scr:]
        ids = [pl.program_id(d) for d in range(len(grid))]
        first = functools.reduce(jnp.logical_and, [i == 0 for i in ids], True)
        last = functools.reduce(jnp.logical_and, [i == g - 1 for i, g in zip(ids, grid)], True)

        def begin():
            if own_barrier:
                _peer_barrier(rider.barrier_peers())
            rider.start(rins, routs, rscr)

        if rider is not None and grid:
            pl.when(first)(begin)
        elif rider is not None:
            begin()
        if body is not None:
            body(*ins, *outs, *scr)
        if rider is not None and grid:
            pl.when(last)(lambda: rider.finish(rins, routs, rscr))
        elif rider is not None:
            rider.finish(rins, routs, rscr)

    kw = dict(grid=grid) if grid else {}
    aliases = getattr(rider, "aliases", {})
    if aliases:
        kw["input_output_aliases"] = {n_in + i: n_out + o for i, o in aliases.items()}
    if grid or vmem is not None or own_barrier:
        kw["compiler_params"] = _params(("arbitrary",) * len(grid) if grid else None, vmem,
                                        rider.collective_id if own_barrier else None)
    res = _call(
        wrapped, name=name, in_specs=list(in_specs) + _any_specs(r_in), out_specs=list(out_specs) + _any_specs(r_out),
        out_shape=list(out_shape) + (rider.out_shape if rider is not None else []),
        scratch_shapes=list(scratch_shapes) + (rider.scratch if rider is not None else []), **kw,
    )(*args, *(rider.inputs if rider is not None else []))
    return list(res[:n_out]), list(res[n_out:])


def _pick_rows(r, target=256):
    best = None
    for d in range(16, min(r, target) + 1, 16):
        if r % d == 0:
            best = d
    return r if best is None else best


def pair_sum(g, got, *, name):
    _, _, R, C = g.shape
    tr = _pick_rows(R)

    def body(c_ref, g_ref, got_ref, o_ref):
        o_ref[...] = (g_ref[:, 0].astype(F32) + got_ref[...].astype(F32)).astype(o_ref.dtype)

    return _call(
        body, name=name, n_prefetch=1,
        grid_spec=pltpu.PrefetchScalarGridSpec(
            num_scalar_prefetch=1, grid=(R // tr,),
            in_specs=[pl.BlockSpec((4, 1, tr, C), lambda i, c: (0, c[0], i, 0)),
                      pl.BlockSpec((4, tr, C), lambda i, c: (0, i, 0))],
            out_specs=pl.BlockSpec((4, tr, C), lambda i, c: (0, i, 0))),
        out_shape=jax.ShapeDtypeStruct((4, R, C), g.dtype),
        compiler_params=_params(("parallel",)),
    )(lax.axis_index("c").astype(jnp.int32).reshape(1), g, got)


def chip_sum_adamw(p, got, w, m, v, *, name):
    _, R, C = p.shape
    assert w.shape == (1, R, C), (name, w.shape, p.shape)
    tr = _pick_rows(R)

    def body(chip_ref, p_ref, got_ref, w_ref, m_ref, v_ref, g_ref, d_ref, mo_ref, vo_ref):
        my_chip = chip_ref[0]
        g = jnp.zeros((tr, C), F32)
        for j in range(4):
            g = g + jnp.where(my_chip == j, p_ref[0], got_ref[j]).astype(F32)
        g_ref[0] = g
        d_ref[0], mo_ref[0], vo_ref[0] = _adamw_math(w_ref[0], g, m_ref[0], v_ref[0])

    spec = pl.BlockSpec((1, tr, C), lambda i, chip: (0, i, 0))
    return _call(
        body, name=name, n_prefetch=1,
        grid_spec=pltpu.PrefetchScalarGridSpec(
            num_scalar_prefetch=1, grid=(R // tr,),
            in_specs=[pl.BlockSpec((1, tr, C), lambda i, chip: (chip[0], i, 0)),
                      pl.BlockSpec((4, tr, C), lambda i, chip: (0, i, 0)), spec, spec, spec],
            out_specs=[spec] * 4),
        out_shape=[jax.ShapeDtypeStruct((1, R, C), F32)] * 4,
        compiler_params=_params(("parallel",)),
    )((2 * lax.axis_index("x") + lax.axis_index("y")).astype(jnp.int32).reshape(1), p, got, w, m, v)


def rows_sum(g8, *, name):
    _, R, C = g8.shape

    def body(g_ref, o_ref):
        acc = g_ref[0]
        for j in range(1, N_DEV):
            acc = acc + g_ref[j]
        o_ref[...] = acc

    return _call(body, name=name, out_shape=jax.ShapeDtypeStruct((R, C), F32))(g8)


def _adamw_math(w, g, m, v):
    m = ADAM_B1 * m + (1.0 - ADAM_B1) * g
    v = ADAM_B2 * v + (1.0 - ADAM_B2) * (g * g)
    m_hat = m / (1.0 - ADAM_B1 ** ADAM_STEP)
    v_hat = v / (1.0 - ADAM_B2 ** ADAM_STEP)
    delta = -ADAM_LR * (m_hat / (jnp.sqrt(v_hat) + ADAM_EPS) + ADAM_WD * w)
    return delta, m, v


def to_bf16(xs, *, name):
    def body(*refs):
        for x_ref, o_ref in zip(refs[:len(xs)], refs[len(xs):]):
            o_ref[...] = x_ref[...].astype(BF16)

    total = sum(_nbytes(v.shape, F32) + _nbytes(v.shape, BF16) for v in xs)
    return _call(body, name=name, out_shape=[jax.ShapeDtypeStruct(v.shape, BF16) for v in xs],
                 compiler_params=_params(vmem=2 * total + (4 << 20)))(*xs)


def adamw_small(wgmv, *, name):
    n = len(wgmv)

    def body(*refs):
        ins, outs = refs[:4 * n], refs[4 * n:]
        for a in range(n):
            w_ref, g_ref, m_ref, v_ref = ins[4 * a:4 * a + 4]
            d, mn, vn = _adamw_math(w_ref[...], g_ref[...], m_ref[...], v_ref[...])
            outs[3 * a][...] = d
            outs[3 * a + 1][...] = mn
            outs[3 * a + 2][...] = vn

    flat = [t for tup in wgmv for t in tup]
    res = _call(
        body, name=name,
        out_shape=[jax.ShapeDtypeStruct(tup[0].shape, F32) for tup in wgmv for _ in range(3)],
    )(*flat)
    return [tuple(res[3 * a:3 * a + 3]) for a in range(n)]


BIG = ("w_in", "w_out", "w_mq", "w_mkv", "w_mo", "w_gu", "w_down")
COL_SHARDED = ("w_in", "w_mkv", "w_gu")
SMALL = ("g_mix", "b_f", "conv_w", "conv_b", "ln_g", "ln_b", "g_x", "g_mem", "g_ffn", "g_final")


def _full_from_gathered(n, blk):
    _, rr, cc = blk.shape
    if n in COL_SHARDED:
        return join_columns(blk, name="join_" + n)
    return blk.reshape(N_DEV * rr, cc)


def join_columns(blk, *, name, tr=256):
    n, R, w = blk.shape
    tr = min(tr, R)

    def body(b_ref, o_ref):
        for k in range(n):
            o_ref[:, pl.ds(k * w, w)] = b_ref[k]

    return _call(
        body, name=name, grid=(R // tr,),
        in_specs=[pl.BlockSpec((n, tr, w), lambda r: (0, r, 0))],
        out_specs=pl.BlockSpec((tr, n * w), lambda r: (r, 0)),
        out_shape=jax.ShapeDtypeStruct((R, n * w), blk.dtype),
        compiler_params=_params(("parallel",)),
    )(blk)


def _shards_from_full(n, g):
    pieces = g if isinstance(g, list) else [g]
    rr, cc = pieces[0].shape[0], sum(p.shape[1] for p in pieces)
    if n in COL_SHARDED:
        return split_columns(pieces, name="shards_" + n).reshape(4, 2, rr, cc // N_DEV)
    return pieces[0].reshape(4, 2, rr // N_DEV, cc)


def split_columns(pieces, *, name, tr=256):
    R = pieces[0].shape[0]
    w = sum(p.shape[1] for p in pieces) // N_DEV
    tr = min(tr, R)
    moves, c0 = [], 0
    for i, p in enumerate(pieces):
        for k in range(N_DEV):
            lo, hi = max(k * w, c0), min((k + 1) * w, c0 + p.shape[1])
            if lo < hi:
                moves.append((k, i, lo - c0, hi - c0, lo - k * w))
        c0 += p.shape[1]

    def body(*refs):
        o_ref = refs[-1]
        for k, i, lo, hi, off in moves:
            o_ref[k, :, pl.ds(off, hi - lo)] = refs[i][:, pl.ds(lo, hi - lo)]

    return _call(
        body, name=name, grid=(R // tr,),
        in_specs=[pl.BlockSpec((tr, p.shape[1]), lambda r: (r, 0)) for p in pieces],
        out_specs=pl.BlockSpec((N_DEV, tr, w), lambda r: (0, r, 0)),
        out_shape=jax.ShapeDtypeStruct((N_DEV, R, w), pieces[0].dtype),
        compiler_params=_params(("parallel",)),
    )(*pieces)


def _small_layout():
    sizes = dict(g_mix=1024, b_f=8, conv_w=CONV_K * CONV_CH, conv_b=512, ln_g=512, ln_b=512, g_x=1024,
                 g_mem=1024, g_ffn=1024, g_final=1024, loss=1)
    lay, r0 = {}, 0
    for n, sz in sizes.items():
        r = -(-sz // LANES)
        lay[n] = (r0, r, sz)
        r0 += r
    return lay, -(-r0 // 8) * 8


def kernel(x, mem, g_mix, w_in, b_f, conv_w, conv_b, ln_g, ln_b, w_out, g_x, g_mem, w_mq, w_mkv, w_mo, g_ffn, w_gu, w_down, g_final, loss_target, m_g_mix, m_w_in, m_b_f, m_conv_w, m_conv_b, m_ln_g, m_ln_b, m_w_out, m_g_x, m_g_mem, m_w_mq, m_w_mkv, m_w_mo, m_g_ffn, m_w_gu, m_w_down, m_g_final, v_g_mix, v_w_in, v_b_f, v_conv_w, v_conv_b, v_ln_g, v_ln_b, v_w_out, v_g_x, v_g_mem, v_w_mq, v_w_mkv, v_w_mo, v_g_ffn, v_w_gu, v_w_down, v_g_final):
    names = ["g_mix", "w_in", "b_f", "conv_w", "conv_b", "ln_g", "ln_b", "w_out", "g_x", "g_mem", "w_mq",
             "w_mkv", "w_mo", "g_ffn", "w_gu", "w_down", "g_final"]
    W = dict(zip(names, [g_mix, w_in, b_f, conv_w, conv_b, ln_g, ln_b, w_out, g_x, g_mem, w_mq, w_mkv, w_mo,
                         g_ffn, w_gu, w_down, g_final]))
    Mo = dict(zip(names, [m_g_mix, m_w_in, m_b_f, m_conv_w, m_conv_b, m_ln_g, m_ln_b, m_w_out, m_g_x, m_g_mem,
                          m_w_mq, m_w_mkv, m_w_mo, m_g_ffn, m_w_gu, m_w_down, m_g_final]))
    Vo = dict(zip(names, [v_g_mix, v_w_in, v_b_f, v_conv_w, v_conv_b, v_ln_g, v_ln_b, v_w_out, v_g_x, v_g_mem,
                          v_w_mq, v_w_mkv, v_w_mo, v_g_ffn, v_w_gu, v_w_down, v_g_final]))
    dev = 4 * lax.axis_index("x") + 2 * lax.axis_index("y") + lax.axis_index("c")

    two = lambda a: a.reshape(-1, a.shape[-1])
    cw_shard = jnp.pad(two(conv_w), ((0, HALO - CONV_K), (0, 0)))
    sp = dict(g_mix=g_mix, b_f=b_f, conv_b=conv_b, ln_g=ln_g, ln_b=ln_b, g_x=g_x, g_mem=g_mem,
              g_ffn=g_ffn, g_final=g_final)
    shards = to_bf16([two(W[n]) for n in ("w_in",) + LATE], name="cast_shards")
    loss_blk, grad_x, gs, reduced = local_step(x, mem, loss_target, sp, [shards[0], cw_shard], shards[1:])

    lay, rs = _small_layout()
    small = {**{n: gs[n] for n in SMALL}, "loss": loss_blk[:, :1]}
    parts = []
    for n, (r0, r, sz) in lay.items():
        flat = small[n].reshape(-1).astype(F32)
        parts.append(jnp.pad(flat, (0, r * LANES - sz)).reshape(r, LANES))
    spack = jnp.concatenate(parts, axis=0)
    spack = jnp.pad(spack, ((0, rs - spack.shape[0]), (0, 0)))
    ssum = rows_sum(all_gather([spack], name="ag_small")[0], name="small_sum")
    gsmall = {n: ssum[r0:r0 + r].reshape(-1)[:sz] for n, (r0, r, sz) in lay.items()}
    loss = gsmall["loss"].reshape(())

    grads, delta, new_m, new_v = {}, {}, {}, {}
    for n in BIG:
        p, o = reduced[n]
        grads[n], delta[n], new_m[n], new_v[n] = chip_sum_adamw(p, o, W[n], Mo[n], Vo[n], name="adamw_" + n)
    for n in SMALL:
        if n == "conv_w":
            full = gsmall[n].reshape(CONV_K, CONV_CH)
            ncol = conv_w.shape[-1]
            grads[n] = lax.dynamic_slice(full, (0, dev * ncol), (CONV_K, ncol)).reshape(conv_w.shape)
        else:
            grads[n] = gsmall[n].reshape(W[n].shape)
    upd = adamw_small([(two(W[n]), two(grads[n]), two(Mo[n]), two(Vo[n])) for n in SMALL], name="adamw_small")
    for n, (d, mn, vn) in zip(SMALL, upd):
        shp = W[n].shape
        delta[n], new_m[n], new_v[n] = d.reshape(shp), mn.reshape(shp), vn.reshape(shp)
    return (loss, grad_x, *[grads[n] for n in names], *[delta[n] for n in names],
            *[new_m[n] for n in names], *[new_v[n] for n in names])
```

```python
import functools
import math

import jax
import jax.numpy as jnp
from jax import lax
from jax.experimental import pallas as pl
from jax.experimental.pallas import tpu as pltpu

F32 = jnp.float32
BF16 = jnp.bfloat16
EPS = 1e-6
N_DEV = 8
CONV_CH = 512
CONV_K = 31
FOX_HEADS = 8
FOX_HEAD_DIM = 64
FOX_W = 512
MEM_HEADS = 4
MEM_HEAD_DIM = 256
HALO = 32
LANES = 128
ADAM_LR, ADAM_B1, ADAM_B2, ADAM_EPS, ADAM_WD, ADAM_STEP = 0.001, 0.9, 0.999, 1e-08, 0.01, 10
NEG = -1e30
VMEM_CAP = 60 * 1024 * 1024
VMEM_FOX_BWD = 56 << 20
VMEM_SWIGLU = 48 << 20
MESH = pl.DeviceIdType.MESH


def _call(body, n_prefetch=0, **kw):
    kw["out_shape"] = jax.tree.map(lambda s: pltpu.HBM(s.shape, s.dtype), kw["out_shape"])
    call = pl.pallas_call(body, **kw)
    return lambda *args: call(*args[:n_prefetch],
                              *[pltpu.with_memory_space_constraint(a, pltpu.HBM) for a in args[n_prefetch:]])


def _params(sem=None, vmem=None, collective_id=None):
    kw = {} if collective_id is None else {"collective_id": collective_id}
    if sem is not None:
        kw["dimension_semantics"] = sem
    if vmem is not None:
        kw["vmem_limit_bytes"] = int(min(VMEM_CAP, vmem))
    return pltpu.CompilerParams(**kw)


def _nbytes(shape, dtype):
    return math.prod(shape) * jnp.dtype(dtype).itemsize


def _pick(n, target):
    best = None
    for d in range(LANES, min(n, target) + 1, LANES):
        if n % d == 0:
            best = d
    return n if best is None else best


class RowEpilogue:
    def __init__(self, fn, ins, outs):
        self.fn, self.ins, self.outs = fn, list(ins), list(outs)


def matmul(a, b, *, tb=False, out_dtype=None, res=None, tm=512, tn=512, name, rider=None, b_blk=None, post=None):
    a_list = list(a) if isinstance(a, (list, tuple)) else [a]
    b_list = list(b) if isinstance(b, (list, tuple)) else [b]
    n = len(a_list)
    assert len(b_list) == n
    M = a_list[0].shape[0]
    N = b_list[0].shape[0] if tb else b_list[0].shape[1]
    tm, tn = _pick(M, tm), _pick(N, tn)
    assert M % tm == 0 and N % tn == 0, (name, M, N, tm, tn)
    dn = (((1,), (1 if tb else 0,)), ((), ()))

    n_res = int(res is not None)
    n_pin = len(post.ins) if post is not None else 0

    def body(*refs):
        acc = None
        for a_ref, b_ref in zip(refs[:n], refs[n:2 * n]):
            p = lax.dot_general(a_ref[...].astype(BF16), b_ref[...].astype(BF16), dn, preferred_element_type=F32)
            acc = p if acc is None else acc + p
        if res is not None:
            acc = acc + refs[2 * n][...].astype(F32)
        if post is None:
            refs[-1][...] = acc.astype(out_dtype)
            return
        first_in = 2 * n + n_res
        vals = post.fn(acc, *[r[...] for r in refs[first_in:first_in + n_pin]])
        for (dtype, kind), o_ref, val in zip(post.outs, refs[first_in + n_pin:], vals):
            if kind in ("row", "rowT"):
                o_ref[...] = val.astype(dtype)
            else:
                @pl.when(pl.program_id(0) == 0)
                def _(o_ref=o_ref):
                    o_ref[...] = jnp.zeros_like(o_ref)

                o_ref[...] += jnp.broadcast_to(val, o_ref.shape).astype(dtype)

    o_spec = pl.BlockSpec((tm, tn), lambda i, j: (i, j))
    in_specs, est = [], 2 * _nbytes((tm, tn), out_dtype or F32) + 2 * _nbytes((tm, tn), F32)
    for av in a_list:
        assert av.shape[0] == M
        in_specs.append(pl.BlockSpec((tm, av.shape[1]), lambda i, j: (i, 0)))
        est += (2 * jnp.dtype(av.dtype).itemsize + (av.dtype != BF16) * 2) * tm * av.shape[1]
    for idx, (av, bv) in enumerate(zip(a_list, b_list)):
        K = av.shape[1]
        kb = 0 if b_blk is None else b_blk[idx]
        assert bv.shape[0 if tb else 1] == N and bv.shape[1 if tb else 0] >= (kb + 1) * K, (name, av.shape, bv.shape)
        assert b_blk is not None or bv.shape[1 if tb else 0] == K, (name, av.shape, bv.shape)
        in_specs.append(pl.BlockSpec((tn, K), lambda i, j, kb=kb: (j, kb)) if tb
                        else pl.BlockSpec((K, tn), lambda i, j, kb=kb: (kb, j)))
        est += (2 * jnp.dtype(bv.dtype).itemsize + (bv.dtype != BF16) * 2) * tn * K
    args = a_list + b_list
    if res is not None:
        in_specs.append(o_spec)
        args.append(res)
        est += 2 * _nbytes((tm, tn), res.dtype)
    if post is None:
        out_specs, out_shape = [o_spec], [jax.ShapeDtypeStruct((M, N), out_dtype)]
    else:
        assert tn == N, (name, tn, N)
        row = pl.BlockSpec((tm, N), lambda i, j: (i, 0))
        for arr, kind in post.ins:
            in_specs.append(row if kind == "row" else pl.BlockSpec((1, N), lambda i, j: (0, 0)))
            args.append(arr)
            est += 2 * _nbytes((tm, N), arr.dtype) * (kind == "row")
        specs = {"row": (row, (M, N)), "rowT": (pl.BlockSpec((N, tm), lambda i, j: (0, i)), (N, M)),
                 "vec": (pl.BlockSpec((1, N), lambda i, j: (0, 0)), (1, N)),
                 "lanes": (pl.BlockSpec((1, LANES), lambda i, j: (0, 0)), (1, LANES))}
        out_specs = [specs[kind][0] for _, kind in post.outs]
        out_shape = [jax.ShapeDtypeStruct(specs[kind][1], dtype) for dtype, kind in post.outs]
        est += sum(2 * _nbytes((tm, N), dtype) + _nbytes((tm, N), F32) for dtype, kind in post.outs if kind[:3] == "row")
    outs, rode = hosted_call(
        body, rider, name=name, grid=(M // tm, N // tn), in_specs=in_specs, out_specs=out_specs,
        out_shape=out_shape, scratch_shapes=[], args=args, vmem=est + (8 << 20),
    )
    result = outs[0] if post is None else outs
    return result if rider is None else (result, rode)


def _rms_scale(x):
    return lax.rsqrt(jnp.mean(x * x, axis=-1, keepdims=True) + EPS)


def rmsnorm_fwd(x, g, *, name, tm=512, rider=None):
    T, D = x.shape
    tm = min(tm, T)

    def body(x_ref, g_ref, o_ref, ot_ref):
        xv = x_ref[...]
        h = xv * _rms_scale(xv) * g_ref[...]
        o_ref[...] = h.astype(BF16)
        ot_ref[...] = h.T.astype(BF16)

    (h, h_t), rode = hosted_call(
        body, rider, name=name, grid=(T // tm,),
        in_specs=[pl.BlockSpec((tm, D), lambda i: (i, 0)), pl.BlockSpec((1, D), lambda i: (0, 0))],
        out_specs=[pl.BlockSpec((tm, D), lambda i: (i, 0)), pl.BlockSpec((D, tm), lambda i: (0, i))],
        out_shape=[jax.ShapeDtypeStruct((T, D), BF16), jax.ShapeDtypeStruct((D, T), BF16)],
        scratch_shapes=[], args=(x, g),
    )
    return (h, h_t) if rider is None else (h, h_t, rode)


def _rms_bwd_math(xv, gv, dh):
    r = _rms_scale(xv)
    xh = xv * r
    dg = jnp.sum(dh * xh, axis=0, keepdims=True)
    dxh = dh * gv
    dx = r * (dxh - xh * jnp.mean(dxh * xh, axis=-1, keepdims=True))
    return dx, dg


def rms_fwd_epilogue(g):
    def fn(acc, gv):
        h = acc * _rms_scale(acc) * gv
        return acc, h, h.T
    return RowEpilogue(fn, [(g, "vec")], [(F32, "row"), (BF16, "row"), (BF16, "rowT")])


def rms_bwd_epilogue(x, g, dres, out_dtype=BF16):
    def fn(acc, xv, gv, *dr):
        dx, dg = _rms_bwd_math(xv, gv, acc)
        return (dx + dr[0].astype(F32) if dr else dx), dg
    ins = [(x, "row"), (g, "vec")] + ([(dres, "row")] if dres is not None else [])
    return RowEpilogue(fn, ins, [(out_dtype, "row"), (F32, "vec")])


def loss_epilogue(g, target):
    def fn(acc, gv, tv):
        e = acc * _rms_scale(acc) * gv - tv
        part = 0.5 * jnp.sum(jnp.mean(e * e, axis=-1, keepdims=True), axis=0, keepdims=True)
        dx, dg = _rms_bwd_math(acc, gv, e * (1.0 / acc.shape[-1]))
        return dx, dg, part
    return RowEpilogue(fn, [(g, "vec"), (target, "row")], [(BF16, "row"), (F32, "vec"), (F32, "lanes")])


def _sigmoid(v):
    return 0.5 * jnp.tanh(0.5 * v) + 0.5


def _glu(blk):
    u = blk[:, :CONV_CH].astype(F32)
    gt = blk[:, CONV_CH:].astype(F32)
    return u * _sigmoid(gt)


def _fill_causal_ext(ext, cur_ref, halo_ref, s, ts):
    ext[pl.ds(HALO, ts), :] = _glu(cur_ref[0])
    hal = _glu(halo_ref[0])
    ext[pl.ds(0, HALO), :] = jnp.where(s > 0, hal, 0.0)


SUBLANES = 8


def _make_shifted(ext, sh):
    n = ext.shape[0]
    full = ext[...]
    for r in range(1, SUBLANES):
        sh[r - 1] = pltpu.roll(full, n - r, 0)


def _tap(ext, sh, off, ts):
    r = off % SUBLANES
    return ext[pl.ds(off, ts), :] if r == 0 else sh[r - 1, pl.ds(off - r, ts), :]


def _causal_conv(ext, sh, w_ref, ts):
    acc = jnp.zeros((ts, CONV_CH), F32)
    for j in range(CONV_K):
        acc = acc + _tap(ext, sh, HALO - (CONV_K - 1) + j, ts) * w_ref[pl.ds(j, 1), :]
    return acc


def _ln_stats(y):
    mu = jnp.mean(y, axis=-1, keepdims=True)
    yc = y - mu
    rstd = lax.rsqrt(jnp.mean(yc * yc, axis=-1, keepdims=True) + EPS)
    return yc * rstd, rstd


def _conv_specs(ts, S):
    nh = ts // HALO
    cur = pl.BlockSpec((1, ts, 2 * CONV_CH), lambda b, s: (b, s, 0))
    halo = pl.BlockSpec((1, HALO, 2 * CONV_CH), lambda b, s: (b, jnp.maximum(s * nh - 1, 0), 0))
    w = pl.BlockSpec((HALO, CONV_CH), lambda b, s: (0, 0))
    vec = pl.BlockSpec((1, CONV_CH), lambda b, s: (0, 0))
    return cur, halo, w, vec


def conv_branch_fwd(ug, conv_w, conv_b, ln_g, ln_b, *, name, ts=512, rider=None):
    B, S, _ = ug.shape
    ts = min(ts, S)
    ns = S // ts
    cur, halo, w, vec = _conv_specs(ts, S)

    def body(cur_ref, halo_ref, w_ref, cb_ref, lg_ref, lb_ref, o_ref, ot_ref, y_ref, ext, sh):
        _fill_causal_ext(ext, cur_ref, halo_ref, pl.program_id(1), ts)
        _make_shifted(ext, sh)
        y = _causal_conv(ext, sh, w_ref, ts) + cb_ref[...]
        y_ref[0] = y
        yh, _ = _ln_stats(y)
        ln = yh * lg_ref[...] + lb_ref[...]
        out = ln * _sigmoid(ln)
        o_ref[0] = out.astype(BF16)
        ot_ref[...] = out.T.astype(BF16)

    return hosted_call(
        body, rider, name=name, grid=(B, ns), in_specs=[cur, halo, w, vec, vec, vec],
        out_specs=[pl.BlockSpec((1, ts, CONV_CH), lambda b, s: (b, s, 0)),
                   pl.BlockSpec((CONV_CH, ts), lambda b, s: (0, b * ns + s)),
                   pl.BlockSpec((1, ts, CONV_CH), lambda b, s: (b, s, 0))],
        out_shape=[jax.ShapeDtypeStruct((B, S, CONV_CH), BF16), jax.ShapeDtypeStruct((CONV_CH, B * S), BF16),
                   jax.ShapeDtypeStruct((B, S, CONV_CH), F32)],
        scratch_shapes=[pltpu.VMEM((ts + HALO, CONV_CH), F32),
                        pltpu.VMEM((SUBLANES - 1, ts + HALO, CONV_CH), F32)],
        args=(ug, ug, conv_w, conv_b, ln_g, ln_b),
    )


def conv_branch_bwd_a(ug, y, dcat, ln_g, ln_b, *, name, ts=512):
    B, S, _ = ug.shape
    ts = min(ts, S)
    cur, halo, _, vec = _conv_specs(ts, S)
    tile = pl.BlockSpec((1, ts, CONV_CH), lambda b, s: (b, s, 0))

    def body(cur_ref, halo_ref, y_ref, d_ref, lg_ref, lb_ref, dy_ref, dw_ref, dv_ref, ext, sh):
        _fill_causal_ext(ext, cur_ref, halo_ref, pl.program_id(1), ts)
        _make_shifted(ext, sh)
        yh, rstd = _ln_stats(y_ref[0])
        lg = lg_ref[...]
        ln = yh * lg + lb_ref[...]
        sg = _sigmoid(ln)
        dln = d_ref[0].astype(F32) * (sg * (1.0 + ln * (1.0 - sg)))
        dyh = dln * lg
        dy = rstd * (dyh - jnp.mean(dyh, axis=-1, keepdims=True)
                     - yh * jnp.mean(dyh * yh, axis=-1, keepdims=True))
        dy_ref[0] = dy

        @pl.when((pl.program_id(0) == 0) & (pl.program_id(1) == 0))
        def _():
            dw_ref[...] = jnp.zeros_like(dw_ref)
            dv_ref[...] = jnp.zeros_like(dv_ref)

        dv_ref[pl.ds(0, 1), :] += jnp.sum(dy, axis=0, keepdims=True)
        dv_ref[pl.ds(1, 1), :] += jnp.sum(dln * yh, axis=0, keepdims=True)
        dv_ref[pl.ds(2, 1), :] += jnp.sum(dln, axis=0, keepdims=True)
        for j in range(CONV_K):
            tap = _tap(ext, sh, HALO - (CONV_K - 1) + j, ts)
            dw_ref[pl.ds(j, 1), :] += jnp.sum(dy * tap, axis=0, keepdims=True)

    return _call(
        body, name=name, grid=(B, S // ts),
        in_specs=[cur, halo, tile, tile, vec, vec],
        out_specs=[tile,
                   pl.BlockSpec((HALO, CONV_CH), lambda b, s: (0, 0)),
                   pl.BlockSpec((8, CONV_CH), lambda b, s: (0, 0))],
        out_shape=[jax.ShapeDtypeStruct((B, S, CONV_CH), F32),
                   jax.ShapeDtypeStruct((HALO, CONV_CH), F32),
                   jax.ShapeDtypeStruct((8, CONV_CH), F32)],
        scratch_shapes=[pltpu.VMEM((ts + HALO, CONV_CH), F32),
                        pltpu.VMEM((SUBLANES - 1, ts + HALO, CONV_CH), F32)],
        compiler_params=_params(("arbitrary", "arbitrary")),
    )(ug, ug, y, dcat, ln_g, ln_b)


def conv_branch_bwd_b(ug, dy, conv_w, *, name, ts=512):
    B, S, _ = ug.shape
    ts = min(ts, S)
    nh, n_halo = ts // HALO, S // HALO

    def body(cur_ref, dy_ref, nxt_ref, w_ref, o_ref, ext, sh):
        last = pl.program_id(1) == pl.num_programs(1) - 1
        ext[pl.ds(0, ts), :] = dy_ref[0]
        ext[pl.ds(ts, HALO), :] = jnp.where(last, 0.0, nxt_ref[0])
        _make_shifted(ext, sh)
        da = jnp.zeros((ts, CONV_CH), F32)
        for j in range(CONV_K):
            da = da + _tap(ext, sh, CONV_K - 1 - j, ts) * w_ref[pl.ds(j, 1), :]
        blk = cur_ref[0]
        u = blk[:, :CONV_CH].astype(F32)
        sg = _sigmoid(blk[:, CONV_CH:].astype(F32))
        o_ref[0, :, :CONV_CH] = (da * sg).astype(BF16)
        o_ref[0, :, CONV_CH:] = (da * u * sg * (1.0 - sg)).astype(BF16)

    return _call(
        body, name=name, grid=(B, S // ts),
        in_specs=[pl.BlockSpec((1, ts, 2 * CONV_CH), lambda b, s: (b, s, 0)),
                  pl.BlockSpec((1, ts, CONV_CH), lambda b, s: (b, s, 0)),
                  pl.BlockSpec((1, HALO, CONV_CH), lambda b, s: (b, jnp.minimum((s + 1) * nh, n_halo - 1), 0)),
                  pl.BlockSpec((HALO, CONV_CH), lambda b, s: (0, 0))],
        out_specs=pl.BlockSpec((1, ts, 2 * CONV_CH), lambda b, s: (b, s, 0)),
        out_shape=jax.ShapeDtypeStruct((B, S, 2 * CONV_CH), BF16),
        scratch_shapes=[pltpu.VMEM((ts + HALO, CONV_CH), F32),
                        pltpu.VMEM((SUBLANES - 1, ts + HALO, CONV_CH), F32)],
        compiler_params=_params(("parallel", "parallel")),
    )(ug, dy, dy, conv_w)


def _tri(n, lower):
    r = lax.broadcasted_iota(jnp.int32, (n, n), 0)
    c = lax.broadcasted_iota(jnp.int32, (n, n), 1)
    return ((r >= c) if lower else (r <= c)).astype(F32)


def _dot_hi(a, b, dn):
    return lax.dot_general(a, b, dn, precision=lax.Precision.HIGHEST, preferred_element_type=F32)


NN = (((1,), (0,)), ((), ()))
NT = (((1,), (1,)), ((), ()))
TN = (((0,), (0,)), ((), ()))


def _log_sigmoid(v):
    e = jnp.exp(-jnp.abs(v))
    log1p_e = jnp.where(e < 1e-3, e * (1.0 - 0.5 * e), jnp.log(1.0 + e))
    return jnp.minimum(v, 0.0) - log1p_e


def fgate_fwd(h, w_f, b_f, *, name, ts=256, rider=None):
    B, S, D = h.shape
    ts = min(ts, S)

    def body(h_ref, w_ref, b_ref, f_ref, cc_ref, cr_ref, carry):
        @pl.when(pl.program_id(1) == 0)
        def _():
            carry[...] = jnp.zeros_like(carry)

        f = jnp.dot(h_ref[0], w_ref[...], preferred_element_type=F32)
        f_ref[0] = f
        logf = _log_sigmoid(f + b_ref[...])
        c = _dot_hi(_tri(ts, True), logf, NN) + carry[pl.ds(0, 1), :]
        cc_ref[0] = c
        carry[pl.ds(0, 1), :] = c[ts - 1:ts, :]
        cr_ref[0] = c.T

    return hosted_call(
        body, rider, name=name, grid=(B, S // ts),
        in_specs=[pl.BlockSpec((1, ts, D), lambda b, s: (b, s, 0)),
                  pl.BlockSpec((D, LANES), lambda b, s: (0, 0)),
                  pl.BlockSpec((1, LANES), lambda b, s: (0, 0))],
        out_specs=[pl.BlockSpec((1, ts, LANES), lambda b, s: (b, s, 0)),
                   pl.BlockSpec((1, ts, LANES), lambda b, s: (b, s, 0)),
                   pl.BlockSpec((1, LANES, ts), lambda b, s: (b, 0, s))],
        out_shape=[jax.ShapeDtypeStruct((B, S, LANES), F32), jax.ShapeDtypeStruct((B, S, LANES), F32),
                   jax.ShapeDtypeStruct((B, LANES, S), F32)],
        scratch_shapes=[pltpu.VMEM((8, LANES), F32)],
        args=(h, w_f, b_f),
    )


def fgate_bwd(dc, f, b_f, *, name, ts=256):
    B, S, _ = f.shape
    P = dc.shape[1]
    ts = min(ts, S)
    ns = S // ts

    def body(dc_ref, f_ref, b_ref, df_ref, db_ref, carry):
        @pl.when(pl.program_id(1) == 0)
        def _():
            carry[...] = jnp.zeros_like(carry)

        @pl.when((pl.program_id(0) == 0) & (pl.program_id(1) == 0))
        def _():
            db_ref[...] = jnp.zeros_like(db_ref)

        dc_t = dc_ref[0, 0]
        for j in range(1, P):
            dc_t = dc_t + dc_ref[0, j]
        dlogf = _dot_hi(_tri(ts, False), dc_t, NN) + carry[pl.ds(0, 1), :]
        carry[pl.ds(0, 1), :] = dlogf[0:1, :]
        df = dlogf * _sigmoid(-(f_ref[0] + b_ref[...]))
        df_ref[0] = df.astype(BF16)
        db_ref[...] += jnp.sum(df, axis=0, keepdims=True)

    return _call(
        body, name=name, grid=(B, ns),
        in_specs=[pl.BlockSpec((1, P, ts, LANES), lambda b, s: (b, 0, ns - 1 - s, 0)),
                  pl.BlockSpec((1, ts, LANES), lambda b, s: (b, ns - 1 - s, 0)),
                  pl.BlockSpec((1, LANES), lambda b, s: (0, 0))],
        out_specs=[pl.BlockSpec((1, ts, LANES), lambda b, s: (b, ns - 1 - s, 0)),
                   pl.BlockSpec((1, LANES), lambda b, s: (0, 0))],
        out_shape=[jax.ShapeDtypeStruct((B, S, LANES), BF16), jax.ShapeDtypeStruct((1, LANES), F32)],
        scratch_shapes=[pltpu.VMEM((8, LANES), F32)],
        compiler_params=_params(("arbitrary", "arbitrary")),
    )(dc, f, b_f)


def _lane_pick(tile, idx):
    lane = lax.broadcasted_iota(jnp.int32, tile.shape, 1)
    return jnp.sum(jnp.where(lane == idx, tile, 0.0), axis=-1, keepdims=True)


FOX_T = 512


def _fox_heads(q, cc_ref, p):
    lane = lax.broadcasted_iota(jnp.int32, q.shape, 1)
    qs = q * (1.0 / math.sqrt(FOX_HEAD_DIM))
    qhs = [jnp.where((lane < FOX_HEAD_DIM) == (hh == 0), qs, jnp.zeros_like(qs)) for hh in range(2)]
    crefs = [_lane_pick(cc_ref[0, pl.ds(0, 1), :], 2 * p + hh) for hh in range(2)]
    return qhs, crefs


def _fold_lanes(x, op):
    out = x[:, :LANES]
    for j in range(1, x.shape[1] // LANES):
        out = op(out, x[:, j * LANES:(j + 1) * LANES])
    return out


def _causal(t, transposed):
    r = lax.broadcasted_iota(jnp.int32, (t, t), 0)
    c = lax.broadcasted_iota(jnp.int32, (t, t), 1)
    return (r <= c) if transposed else (c <= r)


QKV0 = 8


def fox_fwd(z, c_col, c_row, *, name, rider=None):
    B, S, _ = z.shape
    assert S % FOX_T == 0
    tq, nq = FOX_T, S // FOX_T
    npair = FOX_HEADS // 2

    def body(q_ref, k_ref, v_ref, cc_ref, cr_ref, o_ref, l_ref, ot_ref, s_scr, m_scr, acc_scr):
        p, qi = pl.program_id(1), pl.program_id(2)
        qhs, crefs = _fox_heads(q_ref[0], cc_ref, p)
        lane = lax.broadcasted_iota(jnp.int32, (tq, LANES), 1)
        first = lane < FOX_HEAD_DIM
        for hh in range(2):
            m_scr[hh] = jnp.full((tq, LANES), NEG, F32)
            acc_scr[hh] = jnp.zeros((tq, LANES), F32)

        def logits(kb, diagonal):
            k0 = pl.multiple_of(kb * tq, tq)
            k = k_ref[0, pl.ds(k0, tq), :]
            for hh in range(2):
                s = lax.dot_general(qhs[hh], k, NT, preferred_element_type=F32)
                s = s + (crefs[hh] - cr_ref[0, pl.ds(2 * p + hh, 1), pl.ds(k0, tq)])
                if diagonal:
                    s = jnp.where(_causal(tq, False), s, NEG)
                s_scr[hh, kb] = s
                m_scr[hh] = jnp.maximum(m_scr[hh], _fold_lanes(s, jnp.maximum))

        def sweep1(kb, carry):
            logits(kb, False)
            return carry

        lax.fori_loop(0, qi, sweep1, 0)
        logits(qi, True)
        ms = [jnp.max(m_scr[hh], axis=-1, keepdims=True) for hh in range(2)]
        mbs = [jnp.broadcast_to(ms[hh], (tq, tq)) for hh in range(2)]

        for hh in range(2):
            m_scr[hh] = jnp.zeros((tq, LANES), F32)

        def weigh(kb, carry):
            k0 = pl.multiple_of(kb * tq, tq)
            v = v_ref[0, pl.ds(k0, tq), :]
            for hh in range(2):
                pr = jnp.exp(s_scr[hh, kb] - mbs[hh])
                m_scr[hh] += _fold_lanes(pr, jnp.add)
                acc_scr[hh] += jnp.dot(pr.astype(BF16), v, preferred_element_type=F32)
            return carry

        lax.fori_loop(0, qi + 1, weigh, 0)
        accs = [acc_scr[hh] for hh in range(2)]
        ls = [jnp.sum(m_scr[hh], axis=-1, keepdims=True) for hh in range(2)]
        out = jnp.where(first, accs[0] / ls[0], accs[1] / ls[1])
        o_ref[0] = out.astype(BF16)
        ot_ref[...] = out.T.astype(BF16)
        l_ref[0, 0] = jnp.where(first, ms[0] + jnp.log(ls[0]), ms[1] + jnp.log(ls[1]))

    return hosted_call(
        body, rider, name=name, grid=(B, npair, nq),
        in_specs=[pl.BlockSpec((1, tq, LANES), lambda b, p, i: (b, i, QKV0 + p)),
                  pl.BlockSpec((1, S, LANES), lambda b, p, i: (b, 0, QKV0 + npair + p)),
                  pl.BlockSpec((1, S, LANES), lambda b, p, i: (b, 0, QKV0 + 2 * npair + p)),
                  pl.BlockSpec((1, tq, LANES), lambda b, p, i: (b, i, 0)),
                  pl.BlockSpec((1, 8, S), lambda b, p, i: (b, 0, 0))],
        out_specs=[pl.BlockSpec((1, tq, LANES), lambda b, p, i: (b, i, p)),
                   pl.BlockSpec((1, 1, tq, LANES), lambda b, p, i: (b, p, i, 0)),
                   pl.BlockSpec((LANES, tq), lambda b, p, i: (p, b * nq + i))],
        out_shape=[jax.ShapeDtypeStruct((B, S, FOX_W), BF16),
                   jax.ShapeDtypeStruct((B, npair, S, LANES), F32),
                   jax.ShapeDtypeStruct((FOX_W, B * S), BF16)],
        scratch_shapes=[pltpu.VMEM((2, nq, tq, tq), F32), pltpu.VMEM((2, tq, LANES), F32),
                        pltpu.VMEM((2, tq, LANES), F32)],
        args=(z, z, z, c_col, c_row),
    )


def fox_bwd_dq(z, dcat, lse, c_col, c_row, *, name, rider=None):
    B, S, _ = z.shape
    tq, nq = FOX_T, S // FOX_T
    npair = FOX_HEADS // 2

    def body(q_ref, k_ref, v_ref, do_ref, l_ref, cc_ref, cr_ref, dq_ref, st_ref, p_scr, dp_scr, dl_scr):
        p, qi = pl.program_id(1), pl.program_id(2)
        qhs, crefs = _fox_heads(q_ref[0], cc_ref, p)
        lane = lax.broadcasted_iota(jnp.int32, (tq, LANES), 1)
        do_b = do_ref[0].astype(BF16)
        dohs = [jnp.where((lane < FOX_HEAD_DIM) == (hh == 0), do_b, jnp.zeros_like(do_b)) for hh in range(2)]
        lses = [_lane_pick(l_ref[0, 0], hh * FOX_HEAD_DIM) for hh in range(2)]
        lbs = [jnp.broadcast_to(lses[hh], (tq, tq)) for hh in range(2)]
        for hh in range(2):
            dl_scr[hh] = jnp.zeros((tq, LANES), F32)

        def probs(kb, diagonal):
            k0 = pl.multiple_of(kb * tq, tq)
            k = k_ref[0, pl.ds(k0, tq), :]
            v = v_ref[0, pl.ds(k0, tq), :]
            for hh in range(2):
                s = lax.dot_general(qhs[hh], k, NT, preferred_element_type=F32)
                s = s + (crefs[hh] - cr_ref[0, pl.ds(2 * p + hh, 1), pl.ds(k0, tq)])
                pr = jnp.exp(s - lbs[hh])
                if diagonal:
                    pr = jnp.where(_causal(tq, False), pr, 0.0)
                dp = lax.dot_general(dohs[hh], v, NT, preferred_element_type=F32)
                pdp = pr * dp
                dl_scr[hh] += _fold_lanes(pdp, jnp.add)
                p_scr[hh, kb] = pr
                dp_scr[hh, kb] = dp

        def first_pass(kb, carry):
            probs(kb, False)
            return carry

        lax.fori_loop(0, qi, first_pass, 0)
        probs(qi, True)

        dls = [jnp.sum(dl_scr[hh], axis=-1, keepdims=True) for hh in range(2)]
        dlbs = [jnp.broadcast_to(dls[hh], (tq, tq)) for hh in range(2)]

        def second_pass(kb, dq):
            k0 = pl.multiple_of(kb * tq, tq)
            k = k_ref[0, pl.ds(k0, tq), :]
            for hh in range(2):
                ds = p_scr[hh, kb] * (dp_scr[hh, kb] - dlbs[hh])
                kh = jnp.where((lane < FOX_HEAD_DIM) == (hh == 0), k, jnp.zeros_like(k))
                dq = dq + jnp.dot(ds.astype(BF16), kh, preferred_element_type=F32)
            return dq

        dq = lax.fori_loop(0, qi + 1, second_pass, jnp.zeros((tq, LANES), F32))
        dq_ref[0] = (dq * (1.0 / math.sqrt(FOX_HEAD_DIM))).astype(BF16)
        cols = jnp.zeros((tq, LANES), F32)
        for j, col in enumerate([crefs[0] - lses[0], crefs[1] - lses[1], dls[0], dls[1]]):
            cols = jnp.where(lane == j, col, cols)
        st_ref[0, 0] = cols.T[:8]

    return hosted_call(
        body, rider, name=name, grid=(B, npair, nq),
        in_specs=[pl.BlockSpec((1, tq, LANES), lambda b, p, i: (b, i, QKV0 + p)),
                  pl.BlockSpec((1, S, LANES), lambda b, p, i: (b, 0, QKV0 + npair + p)),
                  pl.BlockSpec((1, S, LANES), lambda b, p, i: (b, 0, QKV0 + 2 * npair + p)),
                  pl.BlockSpec((1, tq, LANES), lambda b, p, i: (b, i, npair + p)),
                  pl.BlockSpec((1, 1, tq, LANES), lambda b, p, i: (b, p, i, 0)),
                  pl.BlockSpec((1, tq, LANES), lambda b, p, i: (b, i, 0)),
                  pl.BlockSpec((1, 8, S), lambda b, p, i: (b, 0, 0))],
        out_specs=[pl.BlockSpec((1, tq, LANES), lambda b, p, i: (b, i, p)),
                   pl.BlockSpec((1, 1, 8, tq), lambda b, p, i: (b, p, 0, i))],
        out_shape=[jax.ShapeDtypeStruct((B, S, FOX_W), BF16), jax.ShapeDtypeStruct((B, npair, 8, S), F32)],
        scratch_shapes=[pltpu.VMEM((2, nq, tq, tq), F32), pltpu.VMEM((2, nq, tq, tq), F32),
                        pltpu.VMEM((2, tq, LANES), F32)],
        args=(z, z, z, dcat, lse, c_col, c_row), vmem=VMEM_FOX_BWD,
    )


def fox_bwd_dkdv(z, dcat, stats, c_col, *, name, rider=None):
    B, S, _ = z.shape
    tk, nq = FOX_T, S // FOX_T
    npair = FOX_HEADS // 2
    inv = 1.0 / math.sqrt(FOX_HEAD_DIM)

    def body(q_ref, k_ref, v_ref, do_ref, st_ref, cc_ref, dk_ref, dv_ref, dc_ref, dk_scr, dv_scr, dc_scr):
        p, kt = pl.program_id(1), pl.program_id(2)
        lane = lax.broadcasted_iota(jnp.int32, (tk, LANES), 1)
        masks = [(lane < FOX_HEAD_DIM) == (hh == 0) for hh in range(2)]
        k = k_ref[0]
        v = v_ref[0]
        khs = [jnp.where(masks[hh], k, jnp.zeros_like(k)) for hh in range(2)]
        vhs = [jnp.where(masks[hh], v, jnp.zeros_like(v)) for hh in range(2)]
        ccbs = [jnp.broadcast_to(_lane_pick(cc_ref[0], 2 * p + hh), (tk, tk)) for hh in range(2)]
        dk_scr[...] = jnp.zeros_like(dk_scr)
        dv_scr[...] = jnp.zeros_like(dv_scr)
        dc_scr[...] = jnp.zeros_like(dc_scr)

        def tile(qb, diagonal):
            q0 = pl.multiple_of(qb * tk, tk)
            qs = q_ref[0, pl.ds(q0, tk), :] * inv
            do_b = do_ref[0, pl.ds(q0, tk), :].astype(BF16)
            for hh in range(2):
                st = lax.dot_general(khs[hh], qs, NT, preferred_element_type=F32)
                pr = jnp.exp(st - ccbs[hh] + st_ref[0, 0, pl.ds(hh, 1), pl.ds(q0, tk)])
                if diagonal:
                    pr = jnp.where(_causal(tk, True), pr, 0.0)
                dp = lax.dot_general(vhs[hh], do_b, NT, preferred_element_type=F32)
                ds = pr * (dp - st_ref[0, 0, pl.ds(2 + hh, 1), pl.ds(q0, tk)])
                dv_scr[...] += jnp.dot(pr.astype(BF16), jnp.where(masks[hh], do_b, jnp.zeros_like(do_b)),
                                       preferred_element_type=F32)
                dk_scr[...] += jnp.dot(ds.astype(BF16), jnp.where(masks[hh], qs, jnp.zeros_like(qs)),
                                       preferred_element_type=F32)
                dc_scr[hh] -= _fold_lanes(ds, jnp.add)

        def later(qb, carry):
            tile(qb, False)
            return carry

        tile(kt, True)
        lax.fori_loop(kt + 1, nq, later, 0)
        dk_ref[0] = dk_scr[...].astype(BF16)
        dv_ref[0] = dv_scr[...].astype(BF16)
        dcs = [jnp.sum(dc_scr[hh], axis=-1, keepdims=True) for hh in range(2)]
        dc_ref[0, 0] = jnp.where(lane == 2 * p, dcs[0], jnp.where(lane == 2 * p + 1, dcs[1], 0.0))

    full = lambda col: pl.BlockSpec((1, S, LANES), col)
    tile_spec = lambda col: pl.BlockSpec((1, tk, LANES), col)
    return hosted_call(
        body, rider, name=name, grid=(B, npair, nq),
        in_specs=[full(lambda b, p, t: (b, 0, QKV0 + p)),
                  tile_spec(lambda b, p, t: (b, t, QKV0 + npair + p)),
                  tile_spec(lambda b, p, t: (b, t, QKV0 + 2 * npair + p)),
                  full(lambda b, p, t: (b, 0, npair + p)),
                  pl.BlockSpec((1, 1, 8, S), lambda b, p, t: (b, p, 0, 0)),
                  tile_spec(lambda b, p, t: (b, t, 0))],
        out_specs=[tile_spec(lambda b, p, t: (b, t, p)), tile_spec(lambda b, p, t: (b, t, p)),
                   pl.BlockSpec((1, 1, tk, LANES), lambda b, p, t: (b, p, t, 0))],
        out_shape=[jax.ShapeDtypeStruct((B, S, FOX_W), BF16)] * 2
        + [jax.ShapeDtypeStruct((B, npair, S, LANES), F32)],
        scratch_shapes=[pltpu.VMEM((tk, LANES), F32), pltpu.VMEM((tk, LANES), F32),
                        pltpu.VMEM((2, tk, LANES), F32)],
        args=(z, z, z, dcat, stats, c_col),
    )


def xattn_fwd(qm, kv, *, name, tq=512):
    B, S, D = qm.shape
    M = kv.shape[1]
    tq = min(tq, S)
    inv = 1.0 / math.sqrt(MEM_HEAD_DIM)

    nq = S // tq

    def body(q_ref, kv_ref, o_ref, ot_ref):
        for h in range(MEM_HEADS):
            c0 = h * MEM_HEAD_DIM
            qh = q_ref[0, :, c0:c0 + MEM_HEAD_DIM]
            kh = kv_ref[0, :, c0:c0 + MEM_HEAD_DIM]
            vh = kv_ref[0, :, D + c0:D + c0 + MEM_HEAD_DIM]
            s = lax.dot_general(qh, kh, NT, preferred_element_type=F32) * inv
            e = jnp.exp(s - jnp.max(s, axis=-1, keepdims=True))
            o = jnp.dot(e.astype(BF16), vh, preferred_element_type=F32) / jnp.sum(e, axis=-1, keepdims=True)
            o_ref[0, :, c0:c0 + MEM_HEAD_DIM] = o.astype(BF16)
            ot_ref[c0:c0 + MEM_HEAD_DIM, :] = o.T.astype(BF16)

    return _call(
        body, name=name, grid=(B, nq),
        in_specs=[pl.BlockSpec((1, tq, D), lambda b, i: (b, i, 0)),
                  pl.BlockSpec((1, M, 2 * D), lambda b, i: (b, 0, 0))],
        out_specs=[pl.BlockSpec((1, tq, D), lambda b, i: (b, i, 0)),
                   pl.BlockSpec((D, tq), lambda b, i: (0, b * nq + i))],
        out_shape=[jax.ShapeDtypeStruct((B, S, D), BF16), jax.ShapeDtypeStruct((D, B * S), BF16)],
        compiler_params=_params(("parallel", "parallel")),
    )(qm, kv)


def xattn_bwd(qm, kv, do, *, name, tq=512):
    B, S, D = qm.shape
    M = kv.shape[1]
    tq = min(tq, S)
    inv = 1.0 / math.sqrt(MEM_HEAD_DIM)

    def body(q_ref, kv_ref, do_ref, dq_ref, dkv_ref):
        @pl.when(pl.program_id(1) == 0)
        def _():
            dkv_ref[...] = jnp.zeros_like(dkv_ref)

        for h in range(MEM_HEADS):
            c0 = h * MEM_HEAD_DIM
            qh = q_ref[0, :, c0:c0 + MEM_HEAD_DIM]
            kh = kv_ref[0, :, c0:c0 + MEM_HEAD_DIM]
            vh = kv_ref[0, :, D + c0:D + c0 + MEM_HEAD_DIM]
            doh = do_ref[0, :, c0:c0 + MEM_HEAD_DIM]
            s = lax.dot_general(qh, kh, NT, preferred_element_type=F32) * inv
            e = jnp.exp(s - jnp.max(s, axis=-1, keepdims=True))
            pr = e / jnp.sum(e, axis=-1, keepdims=True)
            dp = lax.dot_general(doh, vh, NT, preferred_element_type=F32)
            ds = pr * (dp - jnp.sum(pr * dp, axis=-1, keepdims=True))
            ds_b = ds.astype(BF16)
            dq_ref[0, :, c0:c0 + MEM_HEAD_DIM] = (jnp.dot(ds_b, kh, preferred_element_type=F32) * inv).astype(BF16)
            dkv_ref[0, :, c0:c0 + MEM_HEAD_DIM] += lax.dot_general(ds_b, qh, TN, preferred_element_type=F32) * inv
            dkv_ref[0, :, D + c0:D + c0 + MEM_HEAD_DIM] += lax.dot_general(
                pr.astype(BF16), doh, TN, preferred_element_type=F32)

    row = pl.BlockSpec((1, tq, D), lambda b, i: (b, i, 0))
    kvs = pl.BlockSpec((1, M, 2 * D), lambda b, i: (b, 0, 0))
    return _call(
        body, name=name, grid=(B, S // tq), in_specs=[row, kvs, row], out_specs=[row, kvs],
        out_shape=[jax.ShapeDtypeStruct((B, S, D), BF16), jax.ShapeDtypeStruct((B, M, 2 * D), F32)],
        compiler_params=_params(("parallel", "arbitrary")),
    )(qm, kv, do)


SWIGLU_TN = 2816


def _chunks(n, w=256):
    return [(c0, min(w, n - c0)) for c0 in range(0, n, w)]


def mm_swiglu_fwd(hf, w_gu, *, name, tm=256):
    T, D = hf.shape
    Fh = w_gu.shape[1] // 2
    tm, tn = min(tm, T), SWIGLU_TN
    nj = Fh // tn
    assert Fh % tn == 0 and T % tm == 0

    def body(a_ref, bg_ref, bu_ref, g_ref, u_ref, o_ref, ot_ref):
        a = a_ref[...]
        for c0, cw in _chunks(tn):
            cols = pl.ds(c0, cw)
            g = jnp.dot(a, bg_ref[:, cols], preferred_element_type=F32)
            u = jnp.dot(a, bu_ref[:, cols], preferred_element_type=F32)
            act = g * _sigmoid(g) * u
            g_ref[:, cols] = g.astype(BF16)
            u_ref[:, cols] = u.astype(BF16)
            o_ref[:, cols] = act.astype(BF16)
            ot_ref[cols, :] = act.T.astype(BF16)

    tile = pl.BlockSpec((tm, tn), lambda i, j: (i, j))
    return _call(
        body, name=name, grid=(T // tm, nj),
        in_specs=[pl.BlockSpec((tm, D), lambda i, j: (i, 0)), pl.BlockSpec((D, tn), lambda i, j: (0, j)),
                  pl.BlockSpec((D, tn), lambda i, j: (0, nj + j))],
        out_specs=[tile, tile, tile, pl.BlockSpec((tn, tm), lambda i, j: (j, i))],
        out_shape=[jax.ShapeDtypeStruct((T, Fh), BF16)] * 3 + [jax.ShapeDtypeStruct((Fh, T), BF16)],
        compiler_params=_params(("parallel", "parallel"), VMEM_SWIGLU),
    )(hf, w_gu, w_gu)


def mm_swiglu_bwd(dx, w_down, g, u, *, name, tm=256):
    T, D = dx.shape
    Fh = w_down.shape[0]
    tm, tn = min(tm, T), SWIGLU_TN
    assert Fh % tn == 0 and T % tm == 0

    def body(a_ref, b_ref, g_ref, u_ref, dg_ref, du_ref):
        a = a_ref[...].astype(BF16)
        for c0, cw in _chunks(tn):
            cols = pl.ds(c0, cw)
            d = lax.dot_general(a, b_ref[cols, :], NT, preferred_element_type=F32)
            gv = g_ref[:, cols].astype(F32)
            uv = u_ref[:, cols].astype(F32)
            sg = _sigmoid(gv)
            dg_ref[:, cols] = (d * uv * (sg * (1.0 + gv * (1.0 - sg)))).astype(BF16)
            du_ref[:, cols] = (d * gv * sg).astype(BF16)

    tile = pl.BlockSpec((tm, tn), lambda i, j: (i, j))
    return _call(
        body, name=name, grid=(T // tm, Fh // tn),
        in_specs=[pl.BlockSpec((tm, D), lambda i, j: (i, 0)), pl.BlockSpec((tn, D), lambda i, j: (j, 0)), tile, tile],
        out_specs=[tile, tile],
        out_shape=[jax.ShapeDtypeStruct((T, Fh), BF16)] * 2,
        compiler_params=_params(("parallel", "parallel"), VMEM_SWIGLU),
    )(dx, w_down, g, u)


LATE_MID = ("w_out", "w_mq", "w_mo")
LATE_KV = ("w_mkv",)
LATE_FFN = ("w_gu", "w_down")
LATE = LATE_MID + LATE_KV + LATE_FFN
RS_GROUPS = (("w_gu", "w_down"), ("w_out", "w_mq", "w_mkv", "w_mo"), ("w_in",))


def pair_sums(names, g42, got):
    return {n: pair_sum(g, o, name="rs_pair_sum_" + n) for n, g, o in zip(names, g42, got)}


def local_step(x, mem, target, sp, first_shards, late_shards):
    B, S, D = x.shape
    T = B * S
    M = mem.shape[1]
    row = lambda v: v.reshape(1, -1).astype(F32)
    g_mix, g_x, g_mem, g_ffn, g_final = (row(sp[k]) for k in ("g_mix", "g_x", "g_mem", "g_ffn", "g_final"))
    conv_b, ln_g, ln_b = row(sp["conv_b"]), row(sp["ln_g"]), row(sp["ln_b"])
    b_f = jnp.pad(row(sp["b_f"]), ((0, 0), (0, LANES - FOX_HEADS)))
    n_ug, n_main = 2 * CONV_CH, 2 * CONV_CH + 3 * FOX_W

    x2d = x.reshape(T, D)
    h, h_t, partly = rmsnorm_fwd(x2d, g_mix, name="rms_mix", rider=AllGatherStage1(first_shards))
    w_in8, cw8 = run_rider(AllGatherStage2(partly), name="ag_first_stage2")
    w_in_full = _full_from_gathered("w_in", w_in8)
    conv_w = cw8.transpose(1, 0, 2).reshape(HALO, -1)
    w_main, w_ug, w_qkv = w_in_full[:, :n_main], w_in_full[:, :n_ug], w_in_full[:, n_ug:n_main]
    w_f = jnp.pad(w_in_full[:, n_main:], ((0, 0), (0, LANES - FOX_HEADS)))
    z = matmul(h, w_main, out_dtype=BF16, tn=n_main, name="mm_in")
    z3 = z.reshape(B, S, n_main)
    n_mid, n_kv = len(LATE_MID), len(LATE_MID) + len(LATE_KV)
    (conv_out, conv_t, conv_y), partly_mid = conv_branch_fwd(z3, conv_w, conv_b, ln_g, ln_b, name="conv_fwd",
                                                     rider=AllGatherStage1(late_shards[:n_mid]))
    (f_raw, c_col, c_row), rode = fgate_fwd(
        h.reshape(B, S, D), w_f, b_f, name="fgate_fwd",
        rider=Riders(AllGatherStage1(late_shards[n_mid:n_kv]), AllGatherStage2(partly_mid)))
    partly_kv, full_mid = rode[:n_kv - n_mid], rode[n_kv - n_mid:]
    (att, lse, att_t), rode = fox_fwd(
        z3, c_col, c_row, name="fox_fwd",
        rider=Riders(AllGatherStage1(late_shards[n_kv:]), AllGatherStage2(partly_kv)))
    partly_ffn, full_kv = rode[:len(LATE_FFN)], rode[len(LATE_FFN):]
    wf = {n: _full_from_gathered(n, blk) for n, blk in zip(LATE_MID + LATE_KV, full_mid + full_kv)}
    (x1, hx, hx_t), full_ffn = matmul(
        [conv_out.reshape(T, CONV_CH), att.reshape(T, FOX_W)], [wf["w_out"], wf["w_out"]], b_blk=[0, 1],
        res=x2d, tn=D, name="mm_out", post=rms_fwd_epilogue(g_x), rider=AllGatherStage2(partly_ffn))
    wf.update({n: _full_from_gathered(n, blk) for n, blk in zip(LATE_FFN, full_ffn)})
    qm = matmul(hx, wf["w_mq"], out_dtype=BF16, tn=D, name="mm_mq")
    mem2d = mem.reshape(B * M, D)
    mem_n, mem_n_t = rmsnorm_fwd(mem2d, g_mem, name="rms_mem")
    kv = matmul(mem_n, wf["w_mkv"], out_dtype=BF16, tn=2 * D, name="mm_mkv").reshape(B, M, 2 * D)
    o, o_t = xattn_fwd(qm.reshape(B, S, D), kv, name="xattn_fwd")
    o = o.reshape(T, D)
    x2, hf, hf_t = matmul(o, wf["w_mo"], res=x1, tn=D, name="mm_mo", post=rms_fwd_epilogue(g_ffn))
    gate, up, act, act_t = mm_swiglu_fwd(hf, wf["w_gu"], name="mm_gu")
    dx3, dg_final, loss = matmul(act, wf["w_down"], res=x2, tn=D, name="mm_down",
                                 post=loss_epilogue(g_final, target.reshape(T, D)))
    gw = {}
    gw["w_down"] = matmul(act_t, dx3, out_dtype=BF16, tm=1408, tn=512, name="dw_down")
    dgate, dup = mm_swiglu_bwd(dx3, wf["w_down"], gate, up, name="dx_down")
    gw["w_gu"] = [matmul(hf_t, dgate, out_dtype=BF16, tn=1408, name="dw_gate"),
                  matmul(hf_t, dup, out_dtype=BF16, tn=1408, name="dw_up")]
    g42 = [_shards_from_full(n, gw[n]) for n in RS_GROUPS[0]]
    (dx2, dg_ffn), got = matmul([dgate, dup], [wf["w_gu"], wf["w_gu"]], b_blk=[0, 1], tb=True, tm=256, tn=D,
                                name="dx_gu", post=rms_bwd_epilogue(x2, g_ffn, dx3), rider=SiblingExchange(g42))
    parts = pair_sums(RS_GROUPS[0], g42, got)
    gw["w_mo"] = matmul(o_t, dx2, out_dtype=BF16, tn=D, name="dw_mo")
    do = matmul(dx2, wf["w_mo"], tb=True, out_dtype=BF16, tn=D, name="dx_mo")
    dqm, dkv = xattn_bwd(qm.reshape(B, S, D), kv, do.reshape(B, S, D), name="xattn_bwd")
    dqm = dqm.reshape(T, D)
    dkv = dkv.reshape(B * M, 2 * D)
    gw["w_mq"] = matmul(hx_t, dqm, out_dtype=BF16, tn=D, name="dw_mq")
    dx1, dg_x = matmul(dqm, wf["w_mq"], tb=True, tn=D, name="dx_mq", post=rms_bwd_epilogue(x1, g_x, dx2))
    gw["w_mkv"] = matmul(mem_n_t, dkv, out_dtype=BF16, tn=D, name="dw_mkv")
    _, dg_mem = matmul(dkv, wf["w_mkv"], tb=True, tn=D, name="dx_mkv", post=rms_bwd_epilogue(mem2d, g_mem, None))
    gw["w_out"] = jnp.concatenate([matmul(conv_t, dx1, out_dtype=BF16, tn=D, name="dw_out_conv"),
                                   matmul(att_t, dx1, out_dtype=BF16, tn=D, name="dw_out_att")], axis=0)
    g42 = [_shards_from_full(n, gw[n]) for n in RS_GROUPS[1]]
    dcat, got = matmul(dx1, wf["w_out"], tb=True, out_dtype=BF16, tn=D, name="dx_out", rider=SiblingExchange(g42))
    dcat = dcat.reshape(B, S, D)
    parts.update(pair_sums(RS_GROUPS[1], g42, got))
    dy, dconv_w, dvec = conv_branch_bwd_a(z3, conv_y, dcat, ln_g, ln_b, name="conv_bwd_a")
    dug = conv_branch_bwd_b(z3, dy, conv_w, name="conv_bwd_b")
    gots = {}
    (dq, stats), got = fox_bwd_dq(z3, dcat, lse, c_col, c_row, name="fox_bwd_dq",
                                  rider=ChipExchange([parts[n] for n in RS_GROUPS[0]]))
    gots.update(zip(RS_GROUPS[0], got))
    (dk, dv, dc), got = fox_bwd_dkdv(z3, dcat, stats, c_col, name="fox_bwd_dkdv",
                                     rider=ChipExchange([parts[n] for n in RS_GROUPS[1]]))
    gots.update(zip(RS_GROUPS[1], got))
    df, db_f = fgate_bwd(dc, f_raw, b_f, name="fgate_bwd")
    dug2 = dug.reshape(T, n_ug)
    dqkv = jnp.concatenate([dq, dk, dv], axis=-1).reshape(T, 3 * FOX_W)
    df2 = df.reshape(T, LANES)
    dw_in = [matmul(h_t, dug2, out_dtype=BF16, tn=n_ug, name="dw_in_ug"),
             matmul(h_t, dqkv, out_dtype=BF16, tn=3 * FOX_W, name="dw_in_qkv"),
             matmul(h_t, df2, out_dtype=BF16, name="dw_f")[:, :FOX_HEADS]]
    g42 = [_shards_from_full("w_in", dw_in)]
    parts.update(pair_sums(RS_GROUPS[2], g42, run_rider(SiblingExchange(g42), name="rs_sibling_in")))
    (dx, dg_mix), (gots["w_in"],) = matmul(
        [dug2, dqkv, df2], [w_ug, w_qkv, w_f], tb=True, tn=D, name="dx_in",
        post=rms_bwd_epilogue(x2d, g_mix, dx1, out_dtype=F32), rider=ChipExchange([parts["w_in"]]))
    gs = dict(g_mix=dg_mix, b_f=db_f[:, :FOX_HEADS], conv_w=dconv_w[:CONV_K], conv_b=dvec[0:1],
              ln_g=dvec[1:2], ln_b=dvec[2:3], g_x=dg_x, g_mem=dg_mem, g_ffn=dg_ffn, g_final=dg_final)
    return loss, dx.reshape(B, S, D), gs, {n: (parts[n], gots[n]) for n in BIG}


def _me():
    return lax.axis_index("x"), lax.axis_index("y"), lax.axis_index("c")


def _any_specs(n):
    return [pl.BlockSpec(memory_space=pl.ANY)] * n


def all_gather(xs, *, name):
    n = len(xs)

    def body(*refs):
        x_refs, out_refs = refs[:n], refs[n:2 * n]
        send_sems, recv_sems, local_sems = refs[2 * n:]
        x, y, c = _me()
        me, sibling = (x, y, c), (x, y, 1 - c)
        chips = [(1 - x, y), (x, 1 - y), (1 - x, 1 - y)]

        def slot(a, px, py, pc):
            return out_refs[a].at[4 * px + 2 * py + pc]

        def copy(a, k, block, to, own=False):
            return pltpu.make_async_remote_copy(
                src_ref=x_refs[a] if own else slot(a, *block), dst_ref=slot(a, *block),
                send_sem=send_sems.at[k, a], recv_sem=recv_sems.at[k, a], device_id=to, device_id_type=MESH)

        mine = [pltpu.make_async_copy(x_refs[a], slot(a, *me), local_sems.at[a]) for a in range(n)]
        first = [copy(a, 0, me, sibling, own=True) for a in range(n)]
        first += [copy(a, 1 + j, me, (*chip, c), own=True) for j, chip in enumerate(chips) for a in range(n)]
        for cp in mine + first:
            cp.start()
        passed = []
        for j, chip in enumerate(chips):
            for a in range(n):
                copy(a, 1 + j, (*chip, c), me).wait_recv()
                passed.append(copy(a, 4 + j, (*chip, c), sibling))
                passed[-1].start()
        for a in range(n):
            copy(a, 0, sibling, me).wait_recv()
            for j, chip in enumerate(chips):
                copy(a, 4 + j, (*chip, 1 - c), me).wait_recv()
        for cp in first + passed:
            cp.wait_send()
        for cp in mine:
            cp.wait()

    return _call(
        body, name=name, in_specs=_any_specs(n), out_specs=_any_specs(n),
        out_shape=[jax.ShapeDtypeStruct((N_DEV,) + v.shape, v.dtype) for v in xs],
        scratch_shapes=[pltpu.SemaphoreType.DMA((7, n)), pltpu.SemaphoreType.DMA((7, n)),
                        pltpu.SemaphoreType.DMA((n,))],
    )(*xs)


SIBLING_BARRIER = 1
CHIPS_BARRIER = 2
GATHER_BARRIER = 3


class SiblingExchange:
    collective_id = SIBLING_BARRIER

    def __init__(self, gs):
        n = len(gs)
        self.n, self.inputs = n, list(gs)
        self.out_shape = [jax.ShapeDtypeStruct((4,) + g.shape[2:], g.dtype) for g in gs]
        self.scratch = [pltpu.SemaphoreType.DMA((n,)), pltpu.SemaphoreType.DMA((n,))]

    @staticmethod
    def barrier_peers():
        x, y, c = _me()
        return [(x, y, 1 - c)]

    def _copies(self, g_refs, out_refs, sems):
        send_sems, recv_sems = sems
        x, y, c = _me()
        return [pltpu.make_async_remote_copy(
            src_ref=g_refs[a].at[:, 1 - c], dst_ref=out_refs[a], send_sem=send_sems.at[a],
            recv_sem=recv_sems.at[a], device_id=(x, y, 1 - c), device_id_type=MESH) for a in range(self.n)]

    def start(self, in_refs, out_refs, sems):
        for cp in self._copies(in_refs, out_refs, sems):
            cp.start()

    def finish(self, in_refs, out_refs, sems):
        for cp in self._copies(in_refs, out_refs, sems):
            cp.wait()


def run_rider(rider, *, name):
    return hosted_call(None, rider, name=name, grid=(), in_specs=[], out_specs=[], out_shape=[],
                       scratch_shapes=[], args=[])[1]


class ChipExchange:
    collective_id = CHIPS_BARRIER

    @staticmethod
    def barrier_peers():
        x, y, c = _me()
        return [(1 - x, y, c), (x, 1 - y, c), (1 - x, 1 - y, c)]

    def __init__(self, ps):
        n = len(ps)
        self.n, self.inputs = n, list(ps)
        self.out_shape = [jax.ShapeDtypeStruct(p.shape, p.dtype) for p in ps]
        self.scratch = [pltpu.SemaphoreType.DMA((3, n)), pltpu.SemaphoreType.DMA((3, n))]

    def _copies(self, p_refs, out_refs, sems, outgoing):
        send_sems, recv_sems = sems
        x, y, c = _me()
        my_chip = 2 * x + y
        cps = []
        for k in range(3):
            px, py = x ^ ((k + 1) >> 1), y ^ ((k + 1) & 1)
            src, dst = (2 * px + py, my_chip) if outgoing else (my_chip, 2 * px + py)
            for a in range(self.n):
                cps.append(pltpu.make_async_remote_copy(
                    src_ref=p_refs[a].at[src], dst_ref=out_refs[a].at[dst], send_sem=send_sems.at[k, a],
                    recv_sem=recv_sems.at[k, a], device_id=(px, py, c), device_id_type=MESH))
        return cps

    def start(self, in_refs, out_refs, sems):
        for cp in self._copies(in_refs, out_refs, sems, True):
            cp.start()

    def finish(self, in_refs, out_refs, sems):
        for cp in self._copies(in_refs, out_refs, sems, False):
            cp.wait_recv()
        for cp in self._copies(in_refs, out_refs, sems, True):
            cp.wait_send()


class AllGatherStage1:
    collective_id = GATHER_BARRIER

    @staticmethod
    def barrier_peers():
        x, y, c = _me()
        return [(x, y, 1 - c), (1 - x, y, c), (x, 1 - y, c), (1 - x, 1 - y, c)]

    def __init__(self, xs):
        n = len(xs)
        self.n, self.inputs = n, list(xs)
        self.out_shape = [jax.ShapeDtypeStruct((N_DEV,) + v.shape, v.dtype) for v in xs]
        self.scratch = [pltpu.SemaphoreType.DMA((4, n)), pltpu.SemaphoreType.DMA((4, n)),
                        pltpu.SemaphoreType.DMA((n,))]

    def _copies(self, x_refs, out_refs, sems, kind):
        send_sems, recv_sems, local_sems = sems
        x, y, c = _me()
        slot = lambda a, d: out_refs[a].at[4 * d[0] + 2 * d[1] + d[2]]
        if kind == "local":
            return [pltpu.make_async_copy(x_refs[a], slot(a, (x, y, c)), local_sems.at[a]) for a in range(self.n)]
        cps = []
        for k, peer in enumerate([(x, y, 1 - c), (1 - x, y, c), (x, 1 - y, c), (1 - x, 1 - y, c)]):
            for a in range(self.n):
                cps.append(pltpu.make_async_remote_copy(
                    src_ref=x_refs[a], dst_ref=slot(a, (x, y, c) if kind == "out" else peer),
                    send_sem=send_sems.at[k, a], recv_sem=recv_sems.at[k, a], device_id=peer, device_id_type=MESH))
        return cps

    def start(self, in_refs, out_refs, sems):
        for cp in self._copies(in_refs, out_refs, sems, "local") + self._copies(in_refs, out_refs, sems, "out"):
            cp.start()

    def finish(self, in_refs, out_refs, sems):
        for cp in self._copies(in_refs, out_refs, sems, "in"):
            cp.wait_recv()
        for cp in self._copies(in_refs, out_refs, sems, "out"):
            cp.wait_send()
        for cp in self._copies(in_refs, out_refs, sems, "local"):
            cp.wait()


class AllGatherStage2:
    collective_id = SIBLING_BARRIER

    @staticmethod
    def barrier_peers():
        x, y, c = _me()
        return [(x, y, 1 - c)]

    def __init__(self, outs):
        n = len(outs)
        self.n, self.inputs = n, list(outs)
        self.out_shape = [jax.ShapeDtypeStruct(o.shape, o.dtype) for o in outs]
        self.scratch = [pltpu.SemaphoreType.DMA((3, n)), pltpu.SemaphoreType.DMA((3, n))]
        self.aliases = {a: a for a in range(n)}

    def _copies(self, out_refs, sems, outgoing):
        send_sems, recv_sems = sems
        x, y, c = _me()
        cps = []
        for k, (px, py) in enumerate([(1 - x, y), (x, 1 - y), (1 - x, 1 - y)]):
            for a in range(self.n):
                cps.append(pltpu.make_async_remote_copy(
                    src_ref=out_refs[a].at[4 * px + 2 * py + c],
                    dst_ref=out_refs[a].at[4 * px + 2 * py + (c if outgoing else 1 - c)],
                    send_sem=send_sems.at[k, a], recv_sem=recv_sems.at[k, a], device_id=(x, y, 1 - c),
                    device_id_type=MESH))
        return cps

    def start(self, in_refs, out_refs, sems):
        for cp in self._copies(out_refs, sems, True):
            cp.start()

    def finish(self, in_refs, out_refs, sems):
        for cp in self._copies(out_refs, sems, False):
            cp.wait_recv()
        for cp in self._copies(out_refs, sems, True):
            cp.wait_send()


class Riders:
    def __init__(self, *riders):
        self.riders = riders
        self.collective_id = riders[0].collective_id
        self.barrier_peers = riders[0].barrier_peers
        self.inputs = [v for r in riders for v in r.inputs]
        self.out_shape = [s for r in riders for s in r.out_shape]
        self.scratch = [s for r in riders for s in r.scratch]
        self.aliases, i0, o0 = {}, 0, 0
        for r in riders:
            self.aliases.update({i0 + i: o0 + o for i, o in getattr(r, "aliases", {}).items()})
            i0, o0 = i0 + len(r.inputs), o0 + len(r.out_shape)

    def _split(self, in_refs, out_refs, sems):
        i0 = o0 = s0 = 0
        for r in self.riders:
            ni, no, ns = len(r.inputs), len(r.out_shape), len(r.scratch)
            yield r, in_refs[i0:i0 + ni], out_refs[o0:o0 + no], sems[s0:s0 + ns]
            i0, o0, s0 = i0 + ni, o0 + no, s0 + ns

    def start(self, in_refs, out_refs, sems):
        for r, i, o, s in self._split(in_refs, out_refs, sems):
            r.start(i, o, s)

    def finish(self, in_refs, out_refs, sems):
        for r, i, o, s in self._split(in_refs, out_refs, sems):
            r.finish(i, o, s)


def _peer_barrier(peers):
    barrier = pltpu.get_barrier_semaphore()
    for peer in peers:
        pl.semaphore_signal(barrier, inc=1, device_id=peer, device_id_type=MESH)
    pl.semaphore_wait(barrier, len(peers))


def hosted_call(body, rider, *, name, grid, in_specs, out_specs, out_shape, scratch_shapes, args, vmem=None):
    n_in, n_out, n_scr = len(in_specs), len(out_specs), len(scratch_shapes)
    r_in, r_out = (len(rider.inputs), len(rider.out_shape)) if rider is not None else (0, 0)
    own_barrier = getattr(rider, "collective_id", None) is not None

    def wrapped(*refs):
        ins, refs = refs[:n_in], refs[n_in:]
        rins, refs = refs[:r_in], refs[r_in:]
        outs, refs = refs[:n_out], refs[n_out:]
        routs, refs = refs[:r_out], refs[r_out:]
        scr, rscr = refs[:n_scr], refs[n_scr:]
        ids = [pl.program_id(d) for d in range(len(grid))]
        first = functools.reduce(jnp.logical_and, [i == 0 for i in ids], True)
        last = functools.reduce(jnp.logical_and, [i == g - 1 for i, g in zip(ids, grid)], True)

        def begin():
            if own_barrier:
                _peer_barrier(rider.barrier_peers())
            rider.start(rins, routs, rscr)

        if rider is not None and grid:
            pl.when(first)(begin)
        elif rider is not None:
            begin()
        if body is not None:
            body(*ins, *outs, *scr)
        if rider is not None and grid:
            pl.when(last)(lambda: rider.finish(rins, routs, rscr))
        elif rider is not None:
            rider.finish(rins, routs, rscr)

    kw = dict(grid=grid) if grid else {}
    aliases = getattr(rider, "aliases", {})
    if aliases:
        kw["input_output_aliases"] = {n_in + i: n_out + o for i, o in aliases.items()}
    if grid or vmem is not None or own_barrier:
        kw["compiler_params"] = _params(("arbitrary",) * len(grid) if grid else None, vmem,
                                        rider.collective_id if own_barrier else None)
    res = _call(
        wrapped, name=name, in_specs=list(in_specs) + _any_specs(r_in), out_specs=list(out_specs) + _any_specs(r_out),
        out_shape=list(out_shape) + (rider.out_shape if rider is not None else []),
        scratch_shapes=list(scratch_shapes) + (rider.scratch if rider is not None else []), **kw,
    )(*args, *(rider.inputs if rider is not None else []))
    return list(res[:n_out]), list(res[n_out:])


def _pick_rows(r, target=256):
    best = None
    for d in range(16, min(r, target) + 1, 16):
        if r % d == 0:
            best = d
    return r if best is None else best


def pair_sum(g, got, *, name):
    _, _, R, C = g.shape
    tr = _pick_rows(R)

    def body(c_ref, g_ref, got_ref, o_ref):
        o_ref[...] = (g_ref[:, 0].astype(F32) + got_ref[...].astype(F32)).astype(o_ref.dtype)

    return _call(
        body, name=name, n_prefetch=1,
        grid_spec=pltpu.PrefetchScalarGridSpec(
            num_scalar_prefetch=1, grid=(R // tr,),
            in_specs=[pl.BlockSpec((4, 1, tr, C), lambda i, c: (0, c[0], i, 0)),
                      pl.BlockSpec((4, tr, C), lambda i, c: (0, i, 0))],
            out_specs=pl.BlockSpec((4, tr, C), lambda i, c: (0, i, 0))),
        out_shape=jax.ShapeDtypeStruct((4, R, C), g.dtype),
        compiler_params=_params(("parallel",)),
    )(lax.axis_index("c").astype(jnp.int32).reshape(1), g, got)


def chip_sum_adamw(p, got, w, m, v, *, name):
    _, R, C = p.shape
    assert w.shape == (1, R, C), (name, w.shape, p.shape)
    tr = _pick_rows(R)

    def body(chip_ref, p_ref, got_ref, w_ref, m_ref, v_ref, g_ref, d_ref, mo_ref, vo_ref):
        my_chip = chip_ref[0]
        g = jnp.zeros((tr, C), F32)
        for j in range(4):
            g = g + jnp.where(my_chip == j, p_ref[0], got_ref[j]).astype(F32)
        g_ref[0] = g
        d_ref[0], mo_ref[0], vo_ref[0] = _adamw_math(w_ref[0], g, m_ref[0], v_ref[0])

    spec = pl.BlockSpec((1, tr, C), lambda i, chip: (0, i, 0))
    return _call(
        body, name=name, n_prefetch=1,
        grid_spec=pltpu.PrefetchScalarGridSpec(
            num_scalar_prefetch=1, grid=(R // tr,),
            in_specs=[pl.BlockSpec((1, tr, C), lambda i, chip: (chip[0], i, 0)),
                      pl.BlockSpec((4, tr, C), lambda i, chip: (0, i, 0)), spec, spec, spec],
            out_specs=[spec] * 4),
        out_shape=[jax.ShapeDtypeStruct((1, R, C), F32)] * 4,
        compiler_params=_params(("parallel",)),
    )((2 * lax.axis_index("x") + lax.axis_index("y")).astype(jnp.int32).reshape(1), p, got, w, m, v)


def rows_sum(g8, *, name):
    _, R, C = g8.shape

    def body(g_ref, o_ref):
        acc = g_ref[0]
        for j in range(1, N_DEV):
            acc = acc + g_ref[j]
        o_ref[...] = acc

    return _call(body, name=name, out_shape=jax.ShapeDtypeStruct((R, C), F32))(g8)


def _adamw_math(w, g, m, v):
    m = ADAM_B1 * m + (1.0 - ADAM_B1) * g
    v = ADAM_B2 * v + (1.0 - ADAM_B2) * (g * g)
    m_hat = m / (1.0 - ADAM_B1 ** ADAM_STEP)
    v_hat = v / (1.0 - ADAM_B2 ** ADAM_STEP)
    delta = -ADAM_LR * (m_hat / (jnp.sqrt(v_hat) + ADAM_EPS) + ADAM_WD * w)
    return delta, m, v


def to_bf16(xs, *, name):
    def body(*refs):
        for x_ref, o_ref in zip(refs[:len(xs)], refs[len(xs):]):
            o_ref[...] = x_ref[...].astype(BF16)

    total = sum(_nbytes(v.shape, F32) + _nbytes(v.shape, BF16) for v in xs)
    return _call(body, name=name, out_shape=[jax.ShapeDtypeStruct(v.shape, BF16) for v in xs],
                 compiler_params=_params(vmem=2 * total + (4 << 20)))(*xs)


def adamw_small(wgmv, *, name):
    n = len(wgmv)

    def body(*refs):
        ins, outs = refs[:4 * n], refs[4 * n:]
        for a in range(n):
            w_ref, g_ref, m_ref, v_ref = ins[4 * a:4 * a + 4]
            d, mn, vn = _adamw_math(w_ref[...], g_ref[...], m_ref[...], v_ref[...])
            outs[3 * a][...] = d
            outs[3 * a + 1][...] = mn
            outs[3 * a + 2][...] = vn

    flat = [t for tup in wgmv for t in tup]
    res = _call(
        body, name=name,
        out_shape=[jax.ShapeDtypeStruct(tup[0].shape, F32) for tup in wgmv for _ in range(3)],
    )(*flat)
    return [tuple(res[3 * a:3 * a + 3]) for a in range(n)]


BIG = ("w_in", "w_out", "w_mq", "w_mkv", "w_mo", "w_gu", "w_down")
COL_SHARDED = ("w_in", "w_mkv", "w_gu")
SMALL = ("g_mix", "b_f", "conv_w", "conv_b", "ln_g", "ln_b", "g_x", "g_mem", "g_ffn", "g_final")


def _full_from_gathered(n, blk):
    _, rr, cc = blk.shape
    if n in COL_SHARDED:
        return join_columns(blk, name="join_" + n)
    return blk.reshape(N_DEV * rr, cc)


def join_columns(blk, *, name, tr=256):
    n, R, w = blk.shape
    tr = min(tr, R)

    def body(b_ref, o_ref):
        for k in range(n):
            o_ref[:, pl.ds(k * w, w)] = b_ref[k]

    return _call(
        body, name=name, grid=(R // tr,),
        in_specs=[pl.BlockSpec((n, tr, w), lambda r: (0, r, 0))],
        out_specs=pl.BlockSpec((tr, n * w), lambda r: (r, 0)),
        out_shape=jax.ShapeDtypeStruct((R, n * w), blk.dtype),
        compiler_params=_params(("parallel",)),
    )(blk)


def _shards_from_full(n, g):
    pieces = g if isinstance(g, list) else [g]
    rr, cc = pieces[0].shape[0], sum(p.shape[1] for p in pieces)
    if n in COL_SHARDED:
        return split_columns(pieces, name="shards_" + n).reshape(4, 2, rr, cc // N_DEV)
    return pieces[0].reshape(4, 2, rr // N_DEV, cc)


def split_columns(pieces, *, name, tr=256):
    R = pieces[0].shape[0]
    w = sum(p.shape[1] for p in pieces) // N_DEV
    tr = min(tr, R)
    moves, c0 = [], 0
    for i, p in enumerate(pieces):
        for k in range(N_DEV):
            lo, hi = max(k * w, c0), min((k + 1) * w, c0 + p.shape[1])
            if lo < hi:
                moves.append((k, i, lo - c0, hi - c0, lo - k * w))
        c0 += p.shape[1]

    def body(*refs):
        o_ref = refs[-1]
        for k, i, lo, hi, off in moves:
            o_ref[k, :, pl.ds(off, hi - lo)] = refs[i][:, pl.ds(lo, hi - lo)]

    return _call(
        body, name=name, grid=(R // tr,),
        in_specs=[pl.BlockSpec((tr, p.shape[1]), lambda r: (r, 0)) for p in pieces],
        out_specs=pl.BlockSpec((N_DEV, tr, w), lambda r: (0, r, 0)),
        out_shape=jax.ShapeDtypeStruct((N_DEV, R, w), pieces[0].dtype),
        compiler_params=_params(("parallel",)),
    )(*pieces)


def _small_layout():
    sizes = dict(g_mix=1024, b_f=8, conv_w=CONV_K * CONV_CH, conv_b=512, ln_g=512, ln_b=512, g_x=1024,
                 g_mem=1024, g_ffn=1024, g_final=1024, loss=1)
    lay, r0 = {}, 0
    for n, sz in sizes.items():
        r = -(-sz // LANES)
        lay[n] = (r0, r, sz)
        r0 += r
    return lay, -(-r0 // 8) * 8


def kernel(x, mem, g_mix, w_in, b_f, conv_w, conv_b, ln_g, ln_b, w_out, g_x, g_mem, w_mq, w_mkv, w_mo, g_ffn, w_gu, w_down, g_final, loss_target, m_g_mix, m_w_in, m_b_f, m_conv_w, m_conv_b, m_ln_g, m_ln_b, m_w_out, m_g_x, m_g_mem, m_w_mq, m_w_mkv, m_w_mo, m_g_ffn, m_w_gu, m_w_down, m_g_final, v_g_mix, v_w_in, v_b_f, v_conv_w, v_conv_b, v_ln_g, v_ln_b, v_w_out, v_g_x, v_g_mem, v_w_mq, v_w_mkv, v_w_mo, v_g_ffn, v_w_gu, v_w_down, v_g_final):
    names = ["g_mix", "w_in", "b_f", "conv_w", "conv_b", "ln_g", "ln_b", "w_out", "g_x", "g_mem", "w_mq",
             "w_mkv", "w_mo", "g_ffn", "w_gu", "w_down", "g_final"]
    W = dict(zip(names, [g_mix, w_in, b_f, conv_w, conv_b, ln_g, ln_b, w_out, g_x, g_mem, w_mq, w_mkv, w_mo,
                         g_ffn, w_gu, w_down, g_final]))
    Mo = dict(zip(names, [m_g_mix, m_w_in, m_b_f, m_conv_w, m_conv_b, m_ln_g, m_ln_b, m_w_out, m_g_x, m_g_mem,
                          m_w_mq, m_w_mkv, m_w_mo, m_g_ffn, m_w_gu, m_w_down, m_g_final]))
    Vo = dict(zip(names, [v_g_mix, v_w_in, v_b_f, v_conv_w, v_conv_b, v_ln_g, v_ln_b, v_w_out, v_g_x, v_g_mem,
                          v_w_mq, v_w_mkv, v_w_mo, v_g_ffn, v_w_gu, v_w_down, v_g_final]))
    dev = 4 * lax.axis_index("x") + 2 * lax.axis_index("y") + lax.axis_index("c")

    two = lambda a: a.reshape(-1, a.shape[-1])
    cw_shard = jnp.pad(two(conv_w), ((0, HALO - CONV_K), (0, 0)))
    sp = dict(g_mix=g_mix, b_f=b_f, conv_b=conv_b, ln_g=ln_g, ln_b=ln_b, g_x=g_x, g_mem=g_mem,
              g_ffn=g_ffn, g_final=g_final)
    shards = to_bf16([two(W[n]) for n in ("w_in",) + LATE], name="cast_shards")
    loss_blk, grad_x, gs, reduced = local_step(x, mem, loss_target, sp, [shards[0], cw_shard], shards[1:])

    lay, rs = _small_layout()
    small = {**{n: gs[n] for n in SMALL}, "loss": loss_blk[:, :1]}
    parts = []
    for n, (r0, r, sz) in lay.items():
        flat = small[n].reshape(-1).astype(F32)
        parts.append(jnp.pad(flat, (0, r * LANES - sz)).reshape(r, LANES))
    spack = jnp.concatenate(parts, axis=0)
    spack = jnp.pad(spack, ((0, rs - spack.shape[0]), (0, 0)))
    ssum = rows_sum(all_gather([spack], name="ag_small")[0], name="small_sum")
    gsmall = {n: ssum[r0:r0 + r].reshape(-1)[:sz] for n, (r0, r, sz) in lay.items()}
    loss = gsmall["loss"].reshape(())

    grads, delta, new_m, new_v = {}, {}, {}, {}
    for n in BIG:
        p, o = reduced[n]
        grads[n], delta[n], new_m[n], new_v[n] = chip_sum_adamw(p, o, W[n], Mo[n], Vo[n], name="adamw_" + n)
    for n in SMALL:
        if n == "conv_w":
            full = gsmall[n].reshape(CONV_K, CONV_CH)
            ncol = conv_w.shape[-1]
            grads[n] = lax.dynamic_slice(full, (0, dev * ncol), (CONV_K, ncol)).reshape(conv_w.shape)
        else:
            grads[n] = gsmall[n].reshape(W[n].shape)
    upd = adamw_small([(two(W[n]), two(grads[n]), two(Mo[n]), two(Vo[n])) for n in SMALL], name="adamw_small")
    for n, (d, mn, vn) in zip(SMALL, upd):
        shp = W[n].shape
        delta[n], new_m[n], new_v[n] = d.reshape(shp), mn.reshape(shp), vn.reshape(shp)
    return (loss, grad_x, *[grads[n] for n in names], *[delta[n] for n in names],
            *[new_m[n] for n in names], *[new_v[n] for n in names])
```

```python
import functools
import math

import jax
import jax.numpy as jnp
from jax import lax
from jax.experimental import pallas as pl
from jax.experimental.pallas import tpu as pltpu

F32 = jnp.float32
BF16 = jnp.bfloat16
EPS = 1e-6
N_DEV = 8
CONV_CH = 512
CONV_K = 31
FOX_HEADS = 8
FOX_HEAD_DIM = 64
FOX_W = 512
MEM_HEADS = 4
MEM_HEAD_DIM = 256
HALO = 32
LANES = 128
ADAM_LR, ADAM_B1, ADAM_B2, ADAM_EPS, ADAM_WD, ADAM_STEP = 0.001, 0.9, 0.999, 1e-08, 0.01, 10
NEG = -1e30
VMEM_CAP = 60 * 1024 * 1024
VMEM_FOX_BWD = 56 << 20
VMEM_SWIGLU = 48 << 20
MESH = pl.DeviceIdType.MESH


def _call(body, n_prefetch=0, **kw):
    kw["out_shape"] = jax.tree.map(lambda s: pltpu.HBM(s.shape, s.dtype), kw["out_shape"])
    call = pl.pallas_call(body, **kw)
    return lambda *args: call(*args[:n_prefetch],
                              *[pltpu.with_memory_space_constraint(a, pltpu.HBM) for a in args[n_prefetch:]])


def _params(sem=None, vmem=None, collective_id=None):
    kw = {} if collective_id is None else {"collective_id": collective_id}
    if sem is not None:
        kw["dimension_semantics"] = sem
    if vmem is not None:
        kw["vmem_limit_bytes"] = int(min(VMEM_CAP, vmem))
    return pltpu.CompilerParams(**kw)


def _nbytes(shape, dtype):
    return math.prod(shape) * jnp.dtype(dtype).itemsize


def _pick(n, target):
    best = None
    for d in range(LANES, min(n, target) + 1, LANES):
        if n % d == 0:
            best = d
    return n if best is None else best


class RowEpilogue:
    def __init__(self, fn, ins, outs):
        self.fn, self.ins, self.outs = fn, list(ins), list(outs)


def matmul(a, b, *, tb=False, out_dtype=None, res=None, tm=512, tn=512, name, rider=None, b_blk=None, post=None):
    a_list = list(a) if isinstance(a, (list, tuple)) else [a]
    b_list = list(b) if isinstance(b, (list, tuple)) else [b]
    n = len(a_list)
    assert len(b_list) == n
    M = a_list[0].shape[0]
    N = b_list[0].shape[0] if tb else b_list[0].shape[1]
    tm, tn = _pick(M, tm), _pick(N, tn)
    assert M % tm == 0 and N % tn == 0, (name, M, N, tm, tn)
    dn = (((1,), (1 if tb else 0,)), ((), ()))

    n_res = int(res is not None)
    n_pin = len(post.ins) if post is not None else 0

    def body(*refs):
        acc = None
        for a_ref, b_ref in zip(refs[:n], refs[n:2 * n]):
            p = lax.dot_general(a_ref[...].astype(BF16), b_ref[...].astype(BF16), dn, preferred_element_type=F32)
            acc = p if acc is None else acc + p
        if res is not None:
            acc = acc + refs[2 * n][...].astype(F32)
        if post is None:
            refs[-1][...] = acc.astype(out_dtype)
            return
        first_in = 2 * n + n_res
        vals = post.fn(acc, *[r[...] for r in refs[first_in:first_in + n_pin]])
        for (dtype, kind), o_ref, val in zip(post.outs, refs[first_in + n_pin:], vals):
            if kind in ("row", "rowT"):
                o_ref[...] = val.astype(dtype)
            else:
                @pl.when(pl.program_id(0) == 0)
                def _(o_ref=o_ref):
                    o_ref[...] = jnp.zeros_like(o_ref)

                o_ref[...] += jnp.broadcast_to(val, o_ref.shape).astype(dtype)

    o_spec = pl.BlockSpec((tm, tn), lambda i, j: (i, j))
    in_specs, est = [], 2 * _nbytes((tm, tn), out_dtype or F32) + 2 * _nbytes((tm, tn), F32)
    for av in a_list:
        assert av.shape[0] == M
        in_specs.append(pl.BlockSpec((tm, av.shape[1]), lambda i, j: (i, 0)))
        est += (2 * jnp.dtype(av.dtype).itemsize + (av.dtype != BF16) * 2) * tm * av.shape[1]
    for idx, (av, bv) in enumerate(zip(a_list, b_list)):
        K = av.shape[1]
        kb = 0 if b_blk is None else b_blk[idx]
        assert bv.shape[0 if tb else 1] == N and bv.shape[1 if tb else 0] >= (kb + 1) * K, (name, av.shape, bv.shape)
        assert b_blk is not None or bv.shape[1 if tb else 0] == K, (name, av.shape, bv.shape)
        in_specs.append(pl.BlockSpec((tn, K), lambda i, j, kb=kb: (j, kb)) if tb
                        else pl.BlockSpec((K, tn), lambda i, j, kb=kb: (kb, j)))
        est += (2 * jnp.dtype(bv.dtype).itemsize + (bv.dtype != BF16) * 2) * tn * K
    args = a_list + b_list
    if res is not None:
        in_specs.append(o_spec)
        args.append(res)
        est += 2 * _nbytes((tm, tn), res.dtype)
    if post is None:
        out_specs, out_shape = [o_spec], [jax.ShapeDtypeStruct((M, N), out_dtype)]
    else:
        assert tn == N, (name, tn, N)
        row = pl.BlockSpec((tm, N), lambda i, j: (i, 0))
        for arr, kind in post.ins:
            in_specs.append(row if kind == "row" else pl.BlockSpec((1, N), lambda i, j: (0, 0)))
            args.append(arr)
            est += 2 * _nbytes((tm, N), arr.dtype) * (kind == "row")
        specs = {"row": (row, (M, N)), "rowT": (pl.BlockSpec((N, tm), lambda i, j: (0, i)), (N, M)),
                 "vec": (pl.BlockSpec((1, N), lambda i, j: (0, 0)), (1, N)),
                 "lanes": (pl.BlockSpec((1, LANES), lambda i, j: (0, 0)), (1, LANES))}
        out_specs = [specs[kind][0] for _, kind in post.outs]
        out_shape = [jax.ShapeDtypeStruct(specs[kind][1], dtype) for dtype, kind in post.outs]
        est += sum(2 * _nbytes((tm, N), dtype) + _nbytes((tm, N), F32) for dtype, kind in post.outs if kind[:3] == "row")
    outs, rode = hosted_call(
        body, rider, name=name, grid=(M // tm, N // tn), in_specs=in_specs, out_specs=out_specs,
        out_shape=out_shape, scratch_shapes=[], args=args, vmem=est + (8 << 20),
    )
    result = outs[0] if post is None else outs
    return result if rider is None else (result, rode)


def _rms_scale(x):
    return lax.rsqrt(jnp.mean(x * x, axis=-1, keepdims=True) + EPS)


def rmsnorm_fwd(x, g, *, name, tm=512, rider=None):
    T, D = x.shape
    tm = min(tm, T)

    def body(x_ref, g_ref, o_ref, ot_ref):
        xv = x_ref[...]
        h = xv * _rms_scale(xv) * g_ref[...]
        o_ref[...] = h.astype(BF16)
        ot_ref[...] = h.T.astype(BF16)

    (h, h_t), rode = hosted_call(
        body, rider, name=name, grid=(T // tm,),
        in_specs=[pl.BlockSpec((tm, D), lambda i: (i, 0)), pl.BlockSpec((1, D), lambda i: (0, 0))],
        out_specs=[pl.BlockSpec((tm, D), lambda i: (i, 0)), pl.BlockSpec((D, tm), lambda i: (0, i))],
        out_shape=[jax.ShapeDtypeStruct((T, D), BF16), jax.ShapeDtypeStruct((D, T), BF16)],
        scratch_shapes=[], args=(x, g),
    )
    return (h, h_t) if rider is None else (h, h_t, rode)


def _rms_bwd_math(xv, gv, dh):
    r = _rms_scale(xv)
    xh = xv * r
    dg = jnp.sum(dh * xh, axis=0, keepdims=True)
    dxh = dh * gv
    dx = r * (dxh - xh * jnp.mean(dxh * xh, axis=-1, keepdims=True))
    return dx, dg


def rms_fwd_epilogue(g):
    def fn(acc, gv):
        h = acc * _rms_scale(acc) * gv
        return acc, h, h.T
    return RowEpilogue(fn, [(g, "vec")], [(F32, "row"), (BF16, "row"), (BF16, "rowT")])


def rms_bwd_epilogue(x, g, dres, out_dtype=BF16):
    def fn(acc, xv, gv, *dr):
        dx, dg = _rms_bwd_math(xv, gv, acc)
        return (dx + dr[0].astype(F32) if dr else dx), dg
    ins = [(x, "row"), (g, "vec")] + ([(dres, "row")] if dres is not None else [])
    return RowEpilogue(fn, ins, [(out_dtype, "row"), (F32, "vec")])


def loss_epilogue(g, target):
    def fn(acc, gv, tv):
        e = acc * _rms_scale(acc) * gv - tv
        part = 0.5 * jnp.sum(jnp.mean(e * e, axis=-1, keepdims=True), axis=0, keepdims=True)
        dx, dg = _rms_bwd_math(acc, gv, e * (1.0 / acc.shape[-1]))
        return dx, dg, part
    return RowEpilogue(fn, [(g, "vec"), (target, "row")], [(BF16, "row"), (F32, "vec"), (F32, "lanes")])


def _sigmoid(v):
    return 0.5 * jnp.tanh(0.5 * v) + 0.5


def _glu(blk):
    u = blk[:, :CONV_CH].astype(F32)
    gt = blk[:, CONV_CH:].astype(F32)
    return u * _sigmoid(gt)


def _fill_causal_ext(ext, cur_ref, halo_ref, s, ts):
    ext[pl.ds(HALO, ts), :] = _glu(cur_ref[0])
    hal = _glu(halo_ref[0])
    ext[pl.ds(0, HALO), :] = jnp.where(s > 0, hal, 0.0)


SUBLANES = 8


def _make_shifted(ext, sh):
    n = ext.shape[0]
    full = ext[...]
    for r in range(1, SUBLANES):
        sh[r - 1] = pltpu.roll(full, n - r, 0)


def _tap(ext, sh, off, ts):
    r = off % SUBLANES
    return ext[pl.ds(off, ts), :] if r == 0 else sh[r - 1, pl.ds(off - r, ts), :]


def _causal_conv(ext, sh, w_ref, ts):
    acc = jnp.zeros((ts, CONV_CH), F32)
    for j in range(CONV_K):
        acc = acc + _tap(ext, sh, HALO - (CONV_K - 1) + j, ts) * w_ref[pl.ds(j, 1), :]
    return acc


def _ln_stats(y):
    mu = jnp.mean(y, axis=-1, keepdims=True)
    yc = y - mu
    rstd = lax.rsqrt(jnp.mean(yc * yc, axis=-1, keepdims=True) + EPS)
    return yc * rstd, rstd


def _conv_specs(ts, S):
    nh = ts // HALO
    cur = pl.BlockSpec((1, ts, 2 * CONV_CH), lambda b, s: (b, s, 0))
    halo = pl.BlockSpec((1, HALO, 2 * CONV_CH), lambda b, s: (b, jnp.maximum(s * nh - 1, 0), 0))
    w = pl.BlockSpec((HALO, CONV_CH), lambda b, s: (0, 0))
    vec = pl.BlockSpec((1, CONV_CH), lambda b, s: (0, 0))
    return cur, halo, w, vec


def conv_branch_fwd(ug, conv_w, conv_b, ln_g, ln_b, *, name, ts=256, rider=None):
    B, S, _ = ug.shape
    ts = min(ts, S)
    ns = S // ts
    cur, halo, w, vec = _conv_specs(ts, S)

    def body(cur_ref, halo_ref, w_ref, cb_ref, lg_ref, lb_ref, o_ref, ot_ref, y_ref, ext, sh):
        _fill_causal_ext(ext, cur_ref, halo_ref, pl.program_id(1), ts)
        _make_shifted(ext, sh)
        y = _causal_conv(ext, sh, w_ref, ts) + cb_ref[...]
        y_ref[0] = y
        yh, _ = _ln_stats(y)
        ln = yh * lg_ref[...] + lb_ref[...]
        out = ln * _sigmoid(ln)
        o_ref[0] = out.astype(BF16)
        ot_ref[...] = out.T.astype(BF16)

    return hosted_call(
        body, rider, name=name, grid=(B, ns), in_specs=[cur, halo, w, vec, vec, vec],
        out_specs=[pl.BlockSpec((1, ts, CONV_CH), lambda b, s: (b, s, 0)),
                   pl.BlockSpec((CONV_CH, ts), lambda b, s: (0, b * ns + s)),
                   pl.BlockSpec((1, ts, CONV_CH), lambda b, s: (b, s, 0))],
        out_shape=[jax.ShapeDtypeStruct((B, S, CONV_CH), BF16), jax.ShapeDtypeStruct((CONV_CH, B * S), BF16),
                   jax.ShapeDtypeStruct((B, S, CONV_CH), F32)],
        scratch_shapes=[pltpu.VMEM((ts + HALO, CONV_CH), F32),
                        pltpu.VMEM((SUBLANES - 1, ts + HALO, CONV_CH), F32)],
        args=(ug, ug, conv_w, conv_b, ln_g, ln_b),
    )


def conv_branch_bwd_a(ug, y, dcat, ln_g, ln_b, *, name, ts=256):
    B, S, _ = ug.shape
    ts = min(ts, S)
    cur, halo, _, vec = _conv_specs(ts, S)
    tile = pl.BlockSpec((1, ts, CONV_CH), lambda b, s: (b, s, 0))

    def body(cur_ref, halo_ref, y_ref, d_ref, lg_ref, lb_ref, dy_ref, dw_ref, dv_ref, ext, sh):
        _fill_causal_ext(ext, cur_ref, halo_ref, pl.program_id(1), ts)
        _make_shifted(ext, sh)
        yh, rstd = _ln_stats(y_ref[0])
        lg = lg_ref[...]
        ln = yh * lg + lb_ref[...]
        sg = _sigmoid(ln)
        dln = d_ref[0].astype(F32) * (sg * (1.0 + ln * (1.0 - sg)))
        dyh = dln * lg
        dy = rstd * (dyh - jnp.mean(dyh, axis=-1, keepdims=True)
                     - yh * jnp.mean(dyh * yh, axis=-1, keepdims=True))
        dy_ref[0] = dy

        @pl.when((pl.program_id(0) == 0) & (pl.program_id(1) == 0))
        def _():
            dw_ref[...] = jnp.zeros_like(dw_ref)
            dv_ref[...] = jnp.zeros_like(dv_ref)

        dv_ref[pl.ds(0, 1), :] += jnp.sum(dy, axis=0, keepdims=True)
        dv_ref[pl.ds(1, 1), :] += jnp.sum(dln * yh, axis=0, keepdims=True)
        dv_ref[pl.ds(2, 1), :] += jnp.sum(dln, axis=0, keepdims=True)
        for j in range(CONV_K):
            tap = _tap(ext, sh, HALO - (CONV_K - 1) + j, ts)
            dw_ref[pl.ds(j, 1), :] += jnp.sum(dy * tap, axis=0, keepdims=True)

    return _call(
        body, name=name, grid=(B, S // ts),
        in_specs=[cur, halo, tile, tile, vec, vec],
        out_specs=[tile,
                   pl.BlockSpec((HALO, CONV_CH), lambda b, s: (0, 0)),
                   pl.BlockSpec((8, CONV_CH), lambda b, s: (0, 0))],
        out_shape=[jax.ShapeDtypeStruct((B, S, CONV_CH), F32),
                   jax.ShapeDtypeStruct((HALO, CONV_CH), F32),
                   jax.ShapeDtypeStruct((8, CONV_CH), F32)],
        scratch_shapes=[pltpu.VMEM((ts + HALO, CONV_CH), F32),
                        pltpu.VMEM((SUBLANES - 1, ts + HALO, CONV_CH), F32)],
        compiler_params=_params(("arbitrary", "arbitrary")),
    )(ug, ug, y, dcat, ln_g, ln_b)


def conv_branch_bwd_b(ug, dy, conv_w, *, name, ts=256):
    B, S, _ = ug.shape
    ts = min(ts, S)
    nh, n_halo = ts // HALO, S // HALO

    def body(cur_ref, dy_ref, nxt_ref, w_ref, o_ref, ext, sh):
        last = pl.program_id(1) == pl.num_programs(1) - 1
        ext[pl.ds(0, ts), :] = dy_ref[0]
        ext[pl.ds(ts, HALO), :] = jnp.where(last, 0.0, nxt_ref[0])
        _make_shifted(ext, sh)
        da = jnp.zeros((ts, CONV_CH), F32)
        for j in range(CONV_K):
            da = da + _tap(ext, sh, CONV_K - 1 - j, ts) * w_ref[pl.ds(j, 1), :]
        blk = cur_ref[0]
        u = blk[:, :CONV_CH].astype(F32)
        sg = _sigmoid(blk[:, CONV_CH:].astype(F32))
        o_ref[0, :, :CONV_CH] = (da * sg).astype(BF16)
        o_ref[0, :, CONV_CH:] = (da * u * sg * (1.0 - sg)).astype(BF16)

    return _call(
        body, name=name, grid=(B, S // ts),
        in_specs=[pl.BlockSpec((1, ts, 2 * CONV_CH), lambda b, s: (b, s, 0)),
                  pl.BlockSpec((1, ts, CONV_CH), lambda b, s: (b, s, 0)),
                  pl.BlockSpec((1, HALO, CONV_CH), lambda b, s: (b, jnp.minimum((s + 1) * nh, n_halo - 1), 0)),
                  pl.BlockSpec((HALO, CONV_CH), lambda b, s: (0, 0))],
        out_specs=pl.BlockSpec((1, ts, 2 * CONV_CH), lambda b, s: (b, s, 0)),
        out_shape=jax.ShapeDtypeStruct((B, S, 2 * CONV_CH), BF16),
        scratch_shapes=[pltpu.VMEM((ts + HALO, CONV_CH), F32),
                        pltpu.VMEM((SUBLANES - 1, ts + HALO, CONV_CH), F32)],
        compiler_params=_params(("parallel", "parallel")),
    )(ug, dy, dy, conv_w)


def _tri(n, lower):
    r = lax.broadcasted_iota(jnp.int32, (n, n), 0)
    c = lax.broadcasted_iota(jnp.int32, (n, n), 1)
    return ((r >= c) if lower else (r <= c)).astype(F32)


def _dot_hi(a, b, dn):
    return lax.dot_general(a, b, dn, precision=lax.Precision.HIGHEST, preferred_element_type=F32)


NN = (((1,), (0,)), ((), ()))
NT = (((1,), (1,)), ((), ()))
TN = (((0,), (0,)), ((), ()))


def _log_sigmoid(v):
    e = jnp.exp(-jnp.abs(v))
    log1p_e = jnp.where(e < 1e-3, e * (1.0 - 0.5 * e), jnp.log(1.0 + e))
    return jnp.minimum(v, 0.0) - log1p_e


def fgate_fwd(h, w_f, b_f, *, name, ts=256, rider=None):
    B, S, D = h.shape
    ts = min(ts, S)

    def body(h_ref, w_ref, b_ref, f_ref, cc_ref, cr_ref, carry):
        @pl.when(pl.program_id(1) == 0)
        def _():
            carry[...] = jnp.zeros_like(carry)

        f = jnp.dot(h_ref[0], w_ref[...], preferred_element_type=F32)
        f_ref[0] = f
        logf = _log_sigmoid(f + b_ref[...])
        c = _dot_hi(_tri(ts, True), logf, NN) + carry[pl.ds(0, 1), :]
        cc_ref[0] = c
        carry[pl.ds(0, 1), :] = c[ts - 1:ts, :]
        cr_ref[0] = c.T

    return hosted_call(
        body, rider, name=name, grid=(B, S // ts),
        in_specs=[pl.BlockSpec((1, ts, D), lambda b, s: (b, s, 0)),
                  pl.BlockSpec((D, LANES), lambda b, s: (0, 0)),
                  pl.BlockSpec((1, LANES), lambda b, s: (0, 0))],
        out_specs=[pl.BlockSpec((1, ts, LANES), lambda b, s: (b, s, 0)),
                   pl.BlockSpec((1, ts, LANES), lambda b, s: (b, s, 0)),
                   pl.BlockSpec((1, LANES, ts), lambda b, s: (b, 0, s))],
        out_shape=[jax.ShapeDtypeStruct((B, S, LANES), F32), jax.ShapeDtypeStruct((B, S, LANES), F32),
                   jax.ShapeDtypeStruct((B, LANES, S), F32)],
        scratch_shapes=[pltpu.VMEM((8, LANES), F32)],
        args=(h, w_f, b_f),
    )


def fgate_bwd(dc, f, b_f, *, name, ts=256):
    B, S, _ = f.shape
    P = dc.shape[1]
    ts = min(ts, S)
    ns = S // ts

    def body(dc_ref, f_ref, b_ref, df_ref, db_ref, carry):
        @pl.when(pl.program_id(1) == 0)
        def _():
            carry[...] = jnp.zeros_like(carry)

        @pl.when((pl.program_id(0) == 0) & (pl.program_id(1) == 0))
        def _():
            db_ref[...] = jnp.zeros_like(db_ref)

        dc_t = dc_ref[0, 0]
        for j in range(1, P):
            dc_t = dc_t + dc_ref[0, j]
        dlogf = _dot_hi(_tri(ts, False), dc_t, NN) + carry[pl.ds(0, 1), :]
        carry[pl.ds(0, 1), :] = dlogf[0:1, :]
        df = dlogf * _sigmoid(-(f_ref[0] + b_ref[...]))
        df_ref[0] = df.astype(BF16)
        db_ref[...] += jnp.sum(df, axis=0, keepdims=True)

    return _call(
        body, name=name, grid=(B, ns),
        in_specs=[pl.BlockSpec((1, P, ts, LANES), lambda b, s: (b, 0, ns - 1 - s, 0)),
                  pl.BlockSpec((1, ts, LANES), lambda b, s: (b, ns - 1 - s, 0)),
                  pl.BlockSpec((1, LANES), lambda b, s: (0, 0))],
        out_specs=[pl.BlockSpec((1, ts, LANES), lambda b, s: (b, ns - 1 - s, 0)),
                   pl.BlockSpec((1, LANES), lambda b, s: (0, 0))],
        out_shape=[jax.ShapeDtypeStruct((B, S, LANES), BF16), jax.ShapeDtypeStruct((1, LANES), F32)],
        scratch_shapes=[pltpu.VMEM((8, LANES), F32)],
        compiler_params=_params(("arbitrary", "arbitrary")),
    )(dc, f, b_f)


def _lane_pick(tile, idx):
    lane = lax.broadcasted_iota(jnp.int32, tile.shape, 1)
    return jnp.sum(jnp.where(lane == idx, tile, 0.0), axis=-1, keepdims=True)


FOX_T = 512


def _fox_heads(q, cc_ref, p):
    lane = lax.broadcasted_iota(jnp.int32, q.shape, 1)
    qs = q * (1.0 / math.sqrt(FOX_HEAD_DIM))
    qhs = [jnp.where((lane < FOX_HEAD_DIM) == (hh == 0), qs, jnp.zeros_like(qs)) for hh in range(2)]
    crefs = [_lane_pick(cc_ref[0, pl.ds(0, 1), :], 2 * p + hh) for hh in range(2)]
    return qhs, crefs


def _fold_lanes(x, op):
    out = x[:, :LANES]
    for j in range(1, x.shape[1] // LANES):
        out = op(out, x[:, j * LANES:(j + 1) * LANES])
    return out


def _causal(t, transposed):
    r = lax.broadcasted_iota(jnp.int32, (t, t), 0)
    c = lax.broadcasted_iota(jnp.int32, (t, t), 1)
    return (r <= c) if transposed else (c <= r)


QKV0 = 8


def fox_fwd(z, c_col, c_row, *, name, rider=None):
    B, S, _ = z.shape
    assert S % FOX_T == 0
    tq, nq = FOX_T, S // FOX_T
    npair = FOX_HEADS // 2

    def body(q_ref, k_ref, v_ref, cc_ref, cr_ref, o_ref, l_ref, ot_ref, s_scr, m_scr, acc_scr):
        p, qi = pl.program_id(1), pl.program_id(2)
        qhs, crefs = _fox_heads(q_ref[0], cc_ref, p)
        lane = lax.broadcasted_iota(jnp.int32, (tq, LANES), 1)
        first = lane < FOX_HEAD_DIM
        for hh in range(2):
            m_scr[hh] = jnp.full((tq, LANES), NEG, F32)
            acc_scr[hh] = jnp.zeros((tq, LANES), F32)

        def logits(kb, diagonal):
            k0 = pl.multiple_of(kb * tq, tq)
            k = k_ref[0, pl.ds(k0, tq), :]
            for hh in range(2):
                s = lax.dot_general(qhs[hh], k, NT, preferred_element_type=F32)
                s = s + (crefs[hh] - cr_ref[0, pl.ds(2 * p + hh, 1), pl.ds(k0, tq)])
                if diagonal:
                    s = jnp.where(_causal(tq, False), s, NEG)
                s_scr[hh, kb] = s
                m_scr[hh] = jnp.maximum(m_scr[hh], _fold_lanes(s, jnp.maximum))

        def sweep1(kb, carry):
            logits(kb, False)
            return carry

        lax.fori_loop(0, qi, sweep1, 0)
        logits(qi, True)
        ms = [jnp.max(m_scr[hh], axis=-1, keepdims=True) for hh in range(2)]
        mbs = [jnp.broadcast_to(ms[hh], (tq, tq)) for hh in range(2)]

        for hh in range(2):
            m_scr[hh] = jnp.zeros((tq, LANES), F32)

        def weigh(kb, carry):
            k0 = pl.multiple_of(kb * tq, tq)
            v = v_ref[0, pl.ds(k0, tq), :]
            for hh in range(2):
                pr = jnp.exp(s_scr[hh, kb] - mbs[hh])
                m_scr[hh] += _fold_lanes(pr, jnp.add)
                acc_scr[hh] += jnp.dot(pr.astype(BF16), v, preferred_element_type=F32)
            return carry

        lax.fori_loop(0, qi + 1, weigh, 0)
        accs = [acc_scr[hh] for hh in range(2)]
        ls = [jnp.sum(m_scr[hh], axis=-1, keepdims=True) for hh in range(2)]
        out = jnp.where(first, accs[0] / ls[0], accs[1] / ls[1])
        o_ref[0] = out.astype(BF16)
        ot_ref[...] = out.T.astype(BF16)
        l_ref[0, 0] = jnp.where(first, ms[0] + jnp.log(ls[0]), ms[1] + jnp.log(ls[1]))

    return hosted_call(
        body, rider, name=name, grid=(B, npair, nq),
        in_specs=[pl.BlockSpec((1, tq, LANES), lambda b, p, i: (b, i, QKV0 + p)),
                  pl.BlockSpec((1, S, LANES), lambda b, p, i: (b, 0, QKV0 + npair + p)),
                  pl.BlockSpec((1, S, LANES), lambda b, p, i: (b, 0, QKV0 + 2 * npair + p)),
                  pl.BlockSpec((1, tq, LANES), lambda b, p, i: (b, i, 0)),
                  pl.BlockSpec((1, 8, S), lambda b, p, i: (b, 0, 0))],
        out_specs=[pl.BlockSpec((1, tq, LANES), lambda b, p, i: (b, i, p)),
                   pl.BlockSpec((1, 1, tq, LANES), lambda b, p, i: (b, p, i, 0)),
                   pl.BlockSpec((LANES, tq), lambda b, p, i: (p, b * nq + i))],
        out_shape=[jax.ShapeDtypeStruct((B, S, FOX_W), BF16),
                   jax.ShapeDtypeStruct((B, npair, S, LANES), F32),
                   jax.ShapeDtypeStruct((FOX_W, B * S), BF16)],
        scratch_shapes=[pltpu.VMEM((2, nq, tq, tq), F32), pltpu.VMEM((2, tq, LANES), F32),
                        pltpu.VMEM((2, tq, LANES), F32)],
        args=(z, z, z, c_col, c_row),
    )


def fox_bwd_dq(z, dcat, lse, c_col, c_row, *, name, rider=None):
    B, S, _ = z.shape
    tq, nq = FOX_T, S // FOX_T
    npair = FOX_HEADS // 2

    def body(q_ref, k_ref, v_ref, do_ref, l_ref, cc_ref, cr_ref, dq_ref, st_ref, p_scr, dp_scr, dl_scr):
        p, qi = pl.program_id(1), pl.program_id(2)
        qhs, crefs = _fox_heads(q_ref[0], cc_ref, p)
        lane = lax.broadcasted_iota(jnp.int32, (tq, LANES), 1)
        do_b = do_ref[0].astype(BF16)
        dohs = [jnp.where((lane < FOX_HEAD_DIM) == (hh == 0), do_b, jnp.zeros_like(do_b)) for hh in range(2)]
        lses = [_lane_pick(l_ref[0, 0], hh * FOX_HEAD_DIM) for hh in range(2)]
        lbs = [jnp.broadcast_to(lses[hh], (tq, tq)) for hh in range(2)]
        for hh in range(2):
            dl_scr[hh] = jnp.zeros((tq, LANES), F32)

        def probs(kb, diagonal):
            k0 = pl.multiple_of(kb * tq, tq)
            k = k_ref[0, pl.ds(k0, tq), :]
            v = v_ref[0, pl.ds(k0, tq), :]
            for hh in range(2):
                s = lax.dot_general(qhs[hh], k, NT, preferred_element_type=F32)
                s = s + (crefs[hh] - cr_ref[0, pl.ds(2 * p + hh, 1), pl.ds(k0, tq)])
                pr = jnp.exp(s - lbs[hh])
                if diagonal:
                    pr = jnp.where(_causal(tq, False), pr, 0.0)
                dp = lax.dot_general(dohs[hh], v, NT, preferred_element_type=F32)
                pdp = pr * dp
                dl_scr[hh] += _fold_lanes(pdp, jnp.add)
                p_scr[hh, kb] = pr
                dp_scr[hh, kb] = dp

        def first_pass(kb, carry):
            probs(kb, False)
            return carry

        lax.fori_loop(0, qi, first_pass, 0)
        probs(qi, True)

        dls = [jnp.sum(dl_scr[hh], axis=-1, keepdims=True) for hh in range(2)]
        dlbs = [jnp.broadcast_to(dls[hh], (tq, tq)) for hh in range(2)]

        def second_pass(kb, dq):
            k0 = pl.multiple_of(kb * tq, tq)
            k = k_ref[0, pl.ds(k0, tq), :]
            for hh in range(2):
                ds = p_scr[hh, kb] * (dp_scr[hh, kb] - dlbs[hh])
                kh = jnp.where((lane < FOX_HEAD_DIM) == (hh == 0), k, jnp.zeros_like(k))
                dq = dq + jnp.dot(ds.astype(BF16), kh, preferred_element_type=F32)
            return dq

        dq = lax.fori_loop(0, qi + 1, second_pass, jnp.zeros((tq, LANES), F32))
        dq_ref[0] = (dq * (1.0 / math.sqrt(FOX_HEAD_DIM))).astype(BF16)
        cols = jnp.zeros((tq, LANES), F32)
        for j, col in enumerate([crefs[0] - lses[0], crefs[1] - lses[1], dls[0], dls[1]]):
            cols = jnp.where(lane == j, col, cols)
        st_ref[0, 0] = cols.T[:8]

    return hosted_call(
        body, rider, name=name, grid=(B, npair, nq),
        in_specs=[pl.BlockSpec((1, tq, LANES), lambda b, p, i: (b, i, QKV0 + p)),
                  pl.BlockSpec((1, S, LANES), lambda b, p, i: (b, 0, QKV0 + npair + p)),
                  pl.BlockSpec((1, S, LANES), lambda b, p, i: (b, 0, QKV0 + 2 * npair + p)),
                  pl.BlockSpec((1, tq, LANES), lambda b, p, i: (b, i, npair + p)),
                  pl.BlockSpec((1, 1, tq, LANES), lambda b, p, i: (b, p, i, 0)),
                  pl.BlockSpec((1, tq, LANES), lambda b, p, i: (b, i, 0)),
                  pl.BlockSpec((1, 8, S), lambda b, p, i: (b, 0, 0))],
        out_specs=[pl.BlockSpec((1, tq, LANES), lambda b, p, i: (b, i, p)),
                   pl.BlockSpec((1, 1, 8, tq), lambda b, p, i: (b, p, 0, i))],
        out_shape=[jax.ShapeDtypeStruct((B, S, FOX_W), BF16), jax.ShapeDtypeStruct((B, npair, 8, S), F32)],
        scratch_shapes=[pltpu.VMEM((2, nq, tq, tq), F32), pltpu.VMEM((2, nq, tq, tq), F32),
                        pltpu.VMEM((2, tq, LANES), F32)],
        args=(z, z, z, dcat, lse, c_col, c_row), vmem=VMEM_FOX_BWD,
    )


def fox_bwd_dkdv(z, dcat, stats, c_col, dq, *, name, rider=None):
    B, S, _ = z.shape
    tk, nq = FOX_T, S // FOX_T
    npair = FOX_HEADS // 2
    inv = 1.0 / math.sqrt(FOX_HEAD_DIM)

    def body(q_ref, k_ref, v_ref, do_ref, st_ref, cc_ref, dq_ref, o_ref, dc_ref, dk_scr, dv_scr, dc_scr):
        kt, p = pl.program_id(1), pl.program_id(2)
        lane = lax.broadcasted_iota(jnp.int32, (tk, LANES), 1)
        masks = [(lane < FOX_HEAD_DIM) == (hh == 0) for hh in range(2)]
        k = k_ref[0]
        v = v_ref[0]
        khs = [jnp.where(masks[hh], k, jnp.zeros_like(k)) for hh in range(2)]
        vhs = [jnp.where(masks[hh], v, jnp.zeros_like(v)) for hh in range(2)]
        ccbs = [jnp.broadcast_to(_lane_pick(cc_ref[0], 2 * p + hh), (tk, tk)) for hh in range(2)]
        dk_scr[...] = jnp.zeros_like(dk_scr)
        dv_scr[...] = jnp.zeros_like(dv_scr)
        dc_scr[...] = jnp.zeros_like(dc_scr)

        def tile(qb, diagonal):
            q0 = pl.multiple_of(qb * tk, tk)
            qs = q_ref[0, pl.ds(q0, tk), :] * inv
            do_b = do_ref[0, pl.ds(q0, tk), :].astype(BF16)
            for hh in range(2):
                st = lax.dot_general(khs[hh], qs, NT, preferred_element_type=F32)
                pr = jnp.exp(st - ccbs[hh] + st_ref[0, 0, pl.ds(hh, 1), pl.ds(q0, tk)])
                if diagonal:
                    pr = jnp.where(_causal(tk, True), pr, 0.0)
                dp = lax.dot_general(vhs[hh], do_b, NT, preferred_element_type=F32)
                ds = pr * (dp - st_ref[0, 0, pl.ds(2 + hh, 1), pl.ds(q0, tk)])
                dv_scr[...] += jnp.dot(pr.astype(BF16), jnp.where(masks[hh], do_b, jnp.zeros_like(do_b)),
                                       preferred_element_type=F32)
                dk_scr[...] += jnp.dot(ds.astype(BF16), jnp.where(masks[hh], qs, jnp.zeros_like(qs)),
                                       preferred_element_type=F32)
                dc_scr[hh] -= _fold_lanes(ds, jnp.add)

        def later(qb, carry):
            tile(qb, False)
            return carry

        tile(kt, True)
        lax.fori_loop(kt + 1, nq, later, 0)
        @pl.when(p == 0)
        def _():
            o_ref[0, :, :FOX_W] = dq_ref[0]

        for pair in range(npair):
            def put(pair=pair):
                o_ref[0, :, FOX_W + pair * LANES:FOX_W + (pair + 1) * LANES] = dk_scr[...].astype(BF16)
                o_ref[0, :, 2 * FOX_W + pair * LANES:2 * FOX_W + (pair + 1) * LANES] = dv_scr[...].astype(BF16)
            pl.when(p == pair)(put)
        dcs = [jnp.sum(dc_scr[hh], axis=-1, keepdims=True) for hh in range(2)]
        dc_ref[0, 0] = jnp.where(lane == 2 * p, dcs[0], jnp.where(lane == 2 * p + 1, dcs[1], 0.0))

    full = lambda col: pl.BlockSpec((1, S, LANES), col)
    tile_spec = lambda col: pl.BlockSpec((1, tk, LANES), col)
    return hosted_call(
        body, rider, name=name, grid=(B, nq, npair),
        in_specs=[full(lambda b, t, p: (b, 0, QKV0 + p)),
                  tile_spec(lambda b, t, p: (b, t, QKV0 + npair + p)),
                  tile_spec(lambda b, t, p: (b, t, QKV0 + 2 * npair + p)),
                  full(lambda b, t, p: (b, 0, npair + p)),
                  pl.BlockSpec((1, 1, 8, S), lambda b, t, p: (b, p, 0, 0)),
                  tile_spec(lambda b, t, p: (b, t, 0)),
                  pl.BlockSpec((1, tk, FOX_W), lambda b, t, p: (b, t, 0))],
        out_specs=[pl.BlockSpec((1, tk, 3 * FOX_W), lambda b, t, p: (b, t, 0)),
                   pl.BlockSpec((1, 1, tk, LANES), lambda b, t, p: (b, p, t, 0))],
        out_shape=[jax.ShapeDtypeStruct((B, S, 3 * FOX_W), BF16),
                   jax.ShapeDtypeStruct((B, npair, S, LANES), F32)],
        scratch_shapes=[pltpu.VMEM((tk, LANES), F32), pltpu.VMEM((tk, LANES), F32),
                        pltpu.VMEM((2, tk, LANES), F32)],
        args=(z, z, z, dcat, stats, c_col, dq),
    )


def xattn_fwd(qm, kv, *, name, tq=512):
    B, S, D = qm.shape
    M = kv.shape[1]
    tq = min(tq, S)
    inv = 1.0 / math.sqrt(MEM_HEAD_DIM)

    nq = S // tq

    def body(q_ref, kv_ref, o_ref, ot_ref):
        for h in range(MEM_HEADS):
            c0 = h * MEM_HEAD_DIM
            qh = q_ref[0, :, c0:c0 + MEM_HEAD_DIM]
            kh = kv_ref[0, :, c0:c0 + MEM_HEAD_DIM]
            vh = kv_ref[0, :, D + c0:D + c0 + MEM_HEAD_DIM]
            s = lax.dot_general(qh, kh, NT, preferred_element_type=F32) * inv
            e = jnp.exp(s - jnp.max(s, axis=-1, keepdims=True))
            o = jnp.dot(e.astype(BF16), vh, preferred_element_type=F32) / jnp.sum(e, axis=-1, keepdims=True)
            o_ref[0, :, c0:c0 + MEM_HEAD_DIM] = o.astype(BF16)
            ot_ref[c0:c0 + MEM_HEAD_DIM, :] = o.T.astype(BF16)

    return _call(
        body, name=name, grid=(B, nq),
        in_specs=[pl.BlockSpec((1, tq, D), lambda b, i: (b, i, 0)),
                  pl.BlockSpec((1, M, 2 * D), lambda b, i: (b, 0, 0))],
        out_specs=[pl.BlockSpec((1, tq, D), lambda b, i: (b, i, 0)),
                   pl.BlockSpec((D, tq), lambda b, i: (0, b * nq + i))],
        out_shape=[jax.ShapeDtypeStruct((B, S, D), BF16), jax.ShapeDtypeStruct((D, B * S), BF16)],
        compiler_params=_params(("parallel", "parallel")),
    )(qm, kv)


def xattn_bwd(qm, kv, do, *, name, tq=512):
    B, S, D = qm.shape
    M = kv.shape[1]
    tq = min(tq, S)
    inv = 1.0 / math.sqrt(MEM_HEAD_DIM)

    def body(q_ref, kv_ref, do_ref, dq_ref, dkv_ref):
        @pl.when(pl.program_id(1) == 0)
        def _():
            dkv_ref[...] = jnp.zeros_like(dkv_ref)

        for h in range(MEM_HEADS):
            c0 = h * MEM_HEAD_DIM
            qh = q_ref[0, :, c0:c0 + MEM_HEAD_DIM]
            kh = kv_ref[0, :, c0:c0 + MEM_HEAD_DIM]
            vh = kv_ref[0, :, D + c0:D + c0 + MEM_HEAD_DIM]
            doh = do_ref[0, :, c0:c0 + MEM_HEAD_DIM]
            s = lax.dot_general(qh, kh, NT, preferred_element_type=F32) * inv
            e = jnp.exp(s - jnp.max(s, axis=-1, keepdims=True))
            pr = e / jnp.sum(e, axis=-1, keepdims=True)
            dp = lax.dot_general(doh, vh, NT, preferred_element_type=F32)
            ds = pr * (dp - jnp.sum(pr * dp, axis=-1, keepdims=True))
            ds_b = ds.astype(BF16)
            dq_ref[0, :, c0:c0 + MEM_HEAD_DIM] = (jnp.dot(ds_b, kh, preferred_element_type=F32) * inv).astype(BF16)
            dkv_ref[0, :, c0:c0 + MEM_HEAD_DIM] += lax.dot_general(ds_b, qh, TN, preferred_element_type=F32) * inv
            dkv_ref[0, :, D + c0:D + c0 + MEM_HEAD_DIM] += lax.dot_general(
                pr.astype(BF16), doh, TN, preferred_element_type=F32)

    row = pl.BlockSpec((1, tq, D), lambda b, i: (b, i, 0))
    kvs = pl.BlockSpec((1, M, 2 * D), lambda b, i: (b, 0, 0))
    return _call(
        body, name=name, grid=(B, S // tq), in_specs=[row, kvs, row], out_specs=[row, kvs],
        out_shape=[jax.ShapeDtypeStruct((B, S, D), BF16), jax.ShapeDtypeStruct((B, M, 2 * D), F32)],
        compiler_params=_params(("parallel", "arbitrary")),
    )(qm, kv, do)


SWIGLU_TN = 2816


def _chunks(n, w=256):
    return [(c0, min(w, n - c0)) for c0 in range(0, n, w)]


def mm_swiglu_fwd(hf, w_gu, *, name, tm=256):
    T, D = hf.shape
    Fh = w_gu.shape[1] // 2
    tm, tn = min(tm, T), SWIGLU_TN
    nj = Fh // tn
    assert Fh % tn == 0 and T % tm == 0

    def body(a_ref, bg_ref, bu_ref, g_ref, u_ref, o_ref, ot_ref):
        a = a_ref[...]
        for c0, cw in _chunks(tn):
            cols = pl.ds(c0, cw)
            g = jnp.dot(a, bg_ref[:, cols], preferred_element_type=F32)
            u = jnp.dot(a, bu_ref[:, cols], preferred_element_type=F32)
            act = g * _sigmoid(g) * u
            g_ref[:, cols] = g.astype(BF16)
            u_ref[:, cols] = u.astype(BF16)
            o_ref[:, cols] = act.astype(BF16)
            ot_ref[cols, :] = act.T.astype(BF16)

    tile = pl.BlockSpec((tm, tn), lambda i, j: (i, j))
    return _call(
        body, name=name, grid=(T // tm, nj),
        in_specs=[pl.BlockSpec((tm, D), lambda i, j: (i, 0)), pl.BlockSpec((D, tn), lambda i, j: (0, j)),
                  pl.BlockSpec((D, tn), lambda i, j: (0, nj + j))],
        out_specs=[tile, tile, tile, pl.BlockSpec((tn, tm), lambda i, j: (j, i))],
        out_shape=[jax.ShapeDtypeStruct((T, Fh), BF16)] * 3 + [jax.ShapeDtypeStruct((Fh, T), BF16)],
        compiler_params=_params(("parallel", "parallel"), VMEM_SWIGLU),
    )(hf, w_gu, w_gu)


def mm_swiglu_bwd(dx, w_down, g, u, *, name, tm=256):
    T, D = dx.shape
    Fh = w_down.shape[0]
    tm, tn = min(tm, T), SWIGLU_TN
    assert Fh % tn == 0 and T % tm == 0

    def body(a_ref, b_ref, g_ref, u_ref, dg_ref, du_ref):
        a = a_ref[...].astype(BF16)
        for c0, cw in _chunks(tn):
            cols = pl.ds(c0, cw)
            d = lax.dot_general(a, b_ref[cols, :], NT, preferred_element_type=F32)
            gv = g_ref[:, cols].astype(F32)
            uv = u_ref[:, cols].astype(F32)
            sg = _sigmoid(gv)
            dg_ref[:, cols] = (d * uv * (sg * (1.0 + gv * (1.0 - sg)))).astype(BF16)
            du_ref[:, cols] = (d * gv * sg).astype(BF16)

    tile = pl.BlockSpec((tm, tn), lambda i, j: (i, j))
    return _call(
        body, name=name, grid=(T // tm, Fh // tn),
        in_specs=[pl.BlockSpec((tm, D), lambda i, j: (i, 0)), pl.BlockSpec((tn, D), lambda i, j: (j, 0)), tile, tile],
        out_specs=[tile, tile],
        out_shape=[jax.ShapeDtypeStruct((T, Fh), BF16)] * 2,
        compiler_params=_params(("parallel", "parallel"), VMEM_SWIGLU),
    )(dx, w_down, g, u)


LATE_MID = ("w_out", "w_mq", "w_mo")
LATE_KV = ("w_mkv",)
LATE_FFN = ("w_gu", "w_down")
LATE = LATE_MID + LATE_KV + LATE_FFN
RS_GROUPS = (("w_gu", "w_down"), ("w_out", "w_mq", "w_mkv", "w_mo"), ("w_in",))


def pair_sums(names, g42, got):
    return {n: pair_sum(g, o, name="rs_pair_sum_" + n) for n, g, o in zip(names, g42, got)}


def local_step(x, mem, target, sp, first_shards, late_shards):
    B, S, D = x.shape
    T = B * S
    M = mem.shape[1]
    row = lambda v: v.reshape(1, -1).astype(F32)
    g_mix, g_x, g_mem, g_ffn, g_final = (row(sp[k]) for k in ("g_mix", "g_x", "g_mem", "g_ffn", "g_final"))
    conv_b, ln_g, ln_b = row(sp["conv_b"]), row(sp["ln_g"]), row(sp["ln_b"])
    b_f = jnp.pad(row(sp["b_f"]), ((0, 0), (0, LANES - FOX_HEADS)))
    n_ug, n_main = 2 * CONV_CH, 2 * CONV_CH + 3 * FOX_W

    x2d = x.reshape(T, D)
    h, h_t, partly = rmsnorm_fwd(x2d, g_mix, name="rms_mix", rider=AllGatherStage1(first_shards))
    w_in8, cw8 = run_rider(AllGatherStage2(partly), name="ag_first_stage2")
    w_in_full = _full_from_gathered("w_in", w_in8)
    conv_w = cw8.transpose(1, 0, 2).reshape(HALO, -1)
    w_main, w_ug, w_qkv = w_in_full[:, :n_main], w_in_full[:, :n_ug], w_in_full[:, n_ug:n_main]
    w_f = jnp.pad(w_in_full[:, n_main:], ((0, 0), (0, LANES - FOX_HEADS)))
    z = matmul(h, w_main, out_dtype=BF16, tn=n_main, name="mm_in")
    z3 = z.reshape(B, S, n_main)
    n_mid, n_kv = len(LATE_MID), len(LATE_MID) + len(LATE_KV)
    (conv_out, conv_t, conv_y), partly_mid = conv_branch_fwd(z3, conv_w, conv_b, ln_g, ln_b, name="conv_fwd",
                                                     rider=AllGatherStage1(late_shards[:n_mid]))
    (f_raw, c_col, c_row), rode = fgate_fwd(
        h.reshape(B, S, D), w_f, b_f, name="fgate_fwd",
        rider=Riders(AllGatherStage1(late_shards[n_mid:n_kv]), AllGatherStage2(partly_mid)))
    partly_kv, full_mid = rode[:n_kv - n_mid], rode[n_kv - n_mid:]
    (att, lse, att_t), rode = fox_fwd(
        z3, c_col, c_row, name="fox_fwd",
        rider=Riders(AllGatherStage1(late_shards[n_kv:]), AllGatherStage2(partly_kv)))
    partly_ffn, full_kv = rode[:len(LATE_FFN)], rode[len(LATE_FFN):]
    wf = {n: _full_from_gathered(n, blk) for n, blk in zip(LATE_MID + LATE_KV, full_mid + full_kv)}
    (x1, hx, hx_t), full_ffn = matmul(
        [conv_out.reshape(T, CONV_CH), att.reshape(T, FOX_W)], [wf["w_out"], wf["w_out"]], b_blk=[0, 1],
        res=x2d, tn=D, name="mm_out", post=rms_fwd_epilogue(g_x), rider=AllGatherStage2(partly_ffn))
    wf.update({n: _full_from_gathered(n, blk) for n, blk in zip(LATE_FFN, full_ffn)})
    qm = matmul(hx, wf["w_mq"], out_dtype=BF16, tn=D, name="mm_mq")
    mem2d = mem.reshape(B * M, D)
    mem_n, mem_n_t = rmsnorm_fwd(mem2d, g_mem, name="rms_mem")
    kv = matmul(mem_n, wf["w_mkv"], out_dtype=BF16, tn=2 * D, name="mm_mkv").reshape(B, M, 2 * D)
    o, o_t = xattn_fwd(qm.reshape(B, S, D), kv, name="xattn_fwd")
    o = o.reshape(T, D)
    x2, hf, hf_t = matmul(o, wf["w_mo"], res=x1, tn=D, name="mm_mo", post=rms_fwd_epilogue(g_ffn))
    gate, up, act, act_t = mm_swiglu_fwd(hf, wf["w_gu"], name="mm_gu")
    dx3, dg_final, loss = matmul(act, wf["w_down"], res=x2, tn=D, name="mm_down",
                                 post=loss_epilogue(g_final, target.reshape(T, D)))
    gw = {}
    gw["w_down"] = matmul(act_t, dx3, out_dtype=BF16, tm=1408, tn=512, name="dw_down")
    dgate, dup = mm_swiglu_bwd(dx3, wf["w_down"], gate, up, name="dx_down")
    gw["w_gu"] = [matmul(hf_t, dgate, out_dtype=BF16, tn=1408, name="dw_gate"),
                  matmul(hf_t, dup, out_dtype=BF16, tn=1408, name="dw_up")]
    g42 = [_shards_from_full(n, gw[n]) for n in RS_GROUPS[0]]
    (dx2, dg_ffn), got = matmul([dgate, dup], [wf["w_gu"], wf["w_gu"]], b_blk=[0, 1], tb=True, tm=256, tn=D,
                                name="dx_gu", post=rms_bwd_epilogue(x2, g_ffn, dx3), rider=SiblingExchange(g42))
    parts = pair_sums(RS_GROUPS[0], g42, got)
    gw["w_mo"] = matmul(o_t, dx2, out_dtype=BF16, tn=D, name="dw_mo")
    do = matmul(dx2, wf["w_mo"], tb=True, out_dtype=BF16, tn=D, name="dx_mo")
    dqm, dkv = xattn_bwd(qm.reshape(B, S, D), kv, do.reshape(B, S, D), name="xattn_bwd")
    dqm = dqm.reshape(T, D)
    dkv = dkv.reshape(B * M, 2 * D)
    gw["w_mq"] = matmul(hx_t, dqm, out_dtype=BF16, tn=D, name="dw_mq")
    dx1, dg_x = matmul(dqm, wf["w_mq"], tb=True, tn=D, name="dx_mq", post=rms_bwd_epilogue(x1, g_x, dx2))
    gw["w_mkv"] = matmul(mem_n_t, dkv, out_dtype=BF16, tn=D, name="dw_mkv")
    _, dg_mem = matmul(dkv, wf["w_mkv"], tb=True, tn=D, name="dx_mkv", post=rms_bwd_epilogue(mem2d, g_mem, None))
    gw["w_out"] = jnp.concatenate([matmul(conv_t, dx1, out_dtype=BF16, tn=D, name="dw_out_conv"),
                                   matmul(att_t, dx1, out_dtype=BF16, tn=D, name="dw_out_att")], axis=0)
    g42 = [_shards_from_full(n, gw[n]) for n in RS_GROUPS[1]]
    dcat, got = matmul(dx1, wf["w_out"], tb=True, out_dtype=BF16, tn=D, name="dx_out", rider=SiblingExchange(g42))
    dcat = dcat.reshape(B, S, D)
    parts.update(pair_sums(RS_GROUPS[1], g42, got))
    dy, dconv_w, dvec = conv_branch_bwd_a(z3, conv_y, dcat, ln_g, ln_b, name="conv_bwd_a")
    dug = conv_branch_bwd_b(z3, dy, conv_w, name="conv_bwd_b")
    gots = {}
    (dq, stats), got = fox_bwd_dq(z3, dcat, lse, c_col, c_row, name="fox_bwd_dq",
                                  rider=ChipExchange([parts[n] for n in RS_GROUPS[0]]))
    gots.update(zip(RS_GROUPS[0], got))
    (dqkv, dc), got = fox_bwd_dkdv(z3, dcat, stats, c_col, dq, name="fox_bwd_dkdv",
                                   rider=ChipExchange([parts[n] for n in RS_GROUPS[1]]))
    gots.update(zip(RS_GROUPS[1], got))
    df, db_f = fgate_bwd(dc, f_raw, b_f, name="fgate_bwd")
    dug2 = dug.reshape(T, n_ug)
    dqkv = dqkv.reshape(T, 3 * FOX_W)
    df2 = df.reshape(T, LANES)
    dw_in = [matmul(h_t, dug2, out_dtype=BF16, tn=n_ug, name="dw_in_ug"),
             matmul(h_t, dqkv, out_dtype=BF16, tn=3 * FOX_W, name="dw_in_qkv"),
             matmul(h_t, df2, out_dtype=BF16, name="dw_f")[:, :FOX_HEADS]]
    g42 = [_shards_from_full("w_in", dw_in)]
    parts.update(pair_sums(RS_GROUPS[2], g42, run_rider(SiblingExchange(g42), name="rs_sibling_in")))
    (dx, dg_mix), (gots["w_in"],) = matmul(
        [dug2, dqkv, df2], [w_ug, w_qkv, w_f], tb=True, tn=D, name="dx_in",
        post=rms_bwd_epilogue(x2d, g_mix, dx1, out_dtype=F32), rider=ChipExchange([parts["w_in"]]))
    gs = dict(g_mix=dg_mix, b_f=db_f[:, :FOX_HEADS], conv_w=dconv_w[:CONV_K], conv_b=dvec[0:1],
              ln_g=dvec[1:2], ln_b=dvec[2:3], g_x=dg_x, g_mem=dg_mem, g_ffn=dg_ffn, g_final=dg_final)
    return loss, dx.reshape(B, S, D), gs, {n: (parts[n], gots[n]) for n in BIG}


def _me():
    return lax.axis_index("x"), lax.axis_index("y"), lax.axis_index("c")


def _any_specs(n):
    return [pl.BlockSpec(memory_space=pl.ANY)] * n


def all_gather(xs, *, name):
    n = len(xs)

    def body(*refs):
        x_refs, out_refs = refs[:n], refs[n:2 * n]
        send_sems, recv_sems, local_sems = refs[2 * n:]
        x, y, c = _me()
        me, sibling = (x, y, c), (x, y, 1 - c)
        chips = [(1 - x, y), (x, 1 - y), (1 - x, 1 - y)]

        def slot(a, px, py, pc):
            return out_refs[a].at[4 * px + 2 * py + pc]

        def copy(a, k, block, to, own=False):
            return pltpu.make_async_remote_copy(
                src_ref=x_refs[a] if own else slot(a, *block), dst_ref=slot(a, *block),
                send_sem=send_sems.at[k, a], recv_sem=recv_sems.at[k, a], device_id=to, device_id_type=MESH)

        mine = [pltpu.make_async_copy(x_refs[a], slot(a, *me), local_sems.at[a]) for a in range(n)]
        first = [copy(a, 0, me, sibling, own=True) for a in range(n)]
        first += [copy(a, 1 + j, me, (*chip, c), own=True) for j, chip in enumerate(chips) for a in range(n)]
        for cp in mine + first:
            cp.start()
        passed = []
        for j, chip in enumerate(chips):
            for a in range(n):
                copy(a, 1 + j, (*chip, c), me).wait_recv()
                passed.append(copy(a, 4 + j, (*chip, c), sibling))
                passed[-1].start()
        for a in range(n):
            copy(a, 0, sibling, me).wait_recv()
            for j, chip in enumerate(chips):
                copy(a, 4 + j, (*chip, 1 - c), me).wait_recv()
        for cp in first + passed:
            cp.wait_send()
        for cp in mine:
            cp.wait()

    return _call(
        body, name=name, in_specs=_any_specs(n), out_specs=_any_specs(n),
        out_shape=[jax.ShapeDtypeStruct((N_DEV,) + v.shape, v.dtype) for v in xs],
        scratch_shapes=[pltpu.SemaphoreType.DMA((7, n)), pltpu.SemaphoreType.DMA((7, n)),
                        pltpu.SemaphoreType.DMA((n,))],
    )(*xs)


SIBLING_BARRIER = 1
CHIPS_BARRIER = 2
GATHER_BARRIER = 3


class SiblingExchange:
    collective_id = SIBLING_BARRIER

    def __init__(self, gs):
        n = len(gs)
        self.n, self.inputs = n, list(gs)
        self.out_shape = [jax.ShapeDtypeStruct((4,) + g.shape[2:], g.dtype) for g in gs]
        self.scratch = [pltpu.SemaphoreType.DMA((n,)), pltpu.SemaphoreType.DMA((n,))]

    @staticmethod
    def barrier_peers():
        x, y, c = _me()
        return [(x, y, 1 - c)]

    def _copies(self, g_refs, out_refs, sems):
        send_sems, recv_sems = sems
        x, y, c = _me()
        return [pltpu.make_async_remote_copy(
            src_ref=g_refs[a].at[:, 1 - c], dst_ref=out_refs[a], send_sem=send_sems.at[a],
            recv_sem=recv_sems.at[a], device_id=(x, y, 1 - c), device_id_type=MESH) for a in range(self.n)]

    def start(self, in_refs, out_refs, sems):
        for cp in self._copies(in_refs, out_refs, sems):
            cp.start()

    def finish(self, in_refs, out_refs, sems):
        for cp in self._copies(in_refs, out_refs, sems):
            cp.wait()


def run_rider(rider, *, name):
    return hosted_call(None, rider, name=name, grid=(), in_specs=[], out_specs=[], out_shape=[],
                       scratch_shapes=[], args=[])[1]


class ChipExchange:
    collective_id = CHIPS_BARRIER

    @staticmethod
    def barrier_peers():
        x, y, c = _me()
        return [(1 - x, y, c), (x, 1 - y, c), (1 - x, 1 - y, c)]

    def __init__(self, ps):
        n = len(ps)
        self.n, self.inputs = n, list(ps)
        self.out_shape = [jax.ShapeDtypeStruct(p.shape, p.dtype) for p in ps]
        self.scratch = [pltpu.SemaphoreType.DMA((3, n)), pltpu.SemaphoreType.DMA((3, n))]

    def _copies(self, p_refs, out_refs, sems, outgoing):
        send_sems, recv_sems = sems
        x, y, c = _me()
        my_chip = 2 * x + y
        cps = []
        for k in range(3):
            px, py = x ^ ((k + 1) >> 1), y ^ ((k + 1) & 1)
            src, dst = (2 * px + py, my_chip) if outgoing else (my_chip, 2 * px + py)
            for a in range(self.n):
                cps.append(pltpu.make_async_remote_copy(
                    src_ref=p_refs[a].at[src], dst_ref=out_refs[a].at[dst], send_sem=send_sems.at[k, a],
                    recv_sem=recv_sems.at[k, a], device_id=(px, py, c), device_id_type=MESH))
        return cps

    def start(self, in_refs, out_refs, sems):
        for cp in self._copies(in_refs, out_refs, sems, True):
            cp.start()

    def finish(self, in_refs, out_refs, sems):
        for cp in self._copies(in_refs, out_refs, sems, False):
            cp.wait_recv()
        for cp in self._copies(in_refs, out_refs, sems, True):
            cp.wait_send()


class AllGatherStage1:
    collective_id = GATHER_BARRIER

    @staticmethod
    def barrier_peers():
        x, y, c = _me()
        return [(x, y, 1 - c), (1 - x, y, c), (x, 1 - y, c), (1 - x, 1 - y, c)]

    def __init__(self, xs):
        n = len(xs)
        self.n, self.inputs = n, list(xs)
        self.out_shape = [jax.ShapeDtypeStruct((N_DEV,) + v.shape, v.dtype) for v in xs]
        self.scratch = [pltpu.SemaphoreType.DMA((4, n)), pltpu.SemaphoreType.DMA((4, n)),
                        pltpu.SemaphoreType.DMA((n,))]

    def _copies(self, x_refs, out_refs, sems, kind):
        send_sems, recv_sems, local_sems = sems
        x, y, c = _me()
        slot = lambda a, d: out_refs[a].at[4 * d[0] + 2 * d[1] + d[2]]
        if kind == "local":
            return [pltpu.make_async_copy(x_refs[a], slot(a, (x, y, c)), local_sems.at[a]) for a in range(self.n)]
        cps = []
        for k, peer in enumerate([(x, y, 1 - c), (1 - x, y, c), (x, 1 - y, c), (1 - x, 1 - y, c)]):
            for a in range(self.n):
                cps.append(pltpu.make_async_remote_copy(
                    src_ref=x_refs[a], dst_ref=slot(a, (x, y, c) if kind == "out" else peer),
                    send_sem=send_sems.at[k, a], recv_sem=recv_sems.at[k, a], device_id=peer, device_id_type=MESH))
        return cps

    def start(self, in_refs, out_refs, sems):
        for cp in self._copies(in_refs, out_refs, sems, "local") + self._copies(in_refs, out_refs, sems, "out"):
            cp.start()

    def finish(self, in_refs, out_refs, sems):
        for cp in self._copies(in_refs, out_refs, sems, "in"):
            cp.wait_recv()
        for cp in self._copies(in_refs, out_refs, sems, "out"):
            cp.wait_send()
        for cp in self._copies(in_refs, out_refs, sems, "local"):
            cp.wait()


class AllGatherStage2:
    collective_id = SIBLING_BARRIER

    @staticmethod
    def barrier_peers():
        x, y, c = _me()
        return [(x, y, 1 - c)]

    def __init__(self, outs):
        n = len(outs)
        self.n, self.inputs = n, list(outs)
        self.out_shape = [jax.ShapeDtypeStruct(o.shape, o.dtype) for o in outs]
        self.scratch = [pltpu.SemaphoreType.DMA((3, n)), pltpu.SemaphoreType.DMA((3, n))]
        self.aliases = {a: a for a in range(n)}

    def _copies(self, out_refs, sems, outgoing):
        send_sems, recv_sems = sems
        x, y, c = _me()
        cps = []
        for k, (px, py) in enumerate([(1 - x, y), (x, 1 - y), (1 - x, 1 - y)]):
            for a in range(self.n):
                cps.append(pltpu.make_async_remote_copy(
                    src_ref=out_refs[a].at[4 * px + 2 * py + c],
                    dst_ref=out_refs[a].at[4 * px + 2 * py + (c if outgoing else 1 - c)],
                    send_sem=send_sems.at[k, a], recv_sem=recv_sems.at[k, a], device_id=(x, y, 1 - c),
                    device_id_type=MESH))
        return cps

    def start(self, in_refs, out_refs, sems):
        for cp in self._copies(out_refs, sems, True):
            cp.start()

    def finish(self, in_refs, out_refs, sems):
        for cp in self._copies(out_refs, sems, False):
            cp.wait_recv()
        for cp in self._copies(out_refs, sems, True):
            cp.wait_send()


class Riders:
    def __init__(self, *riders):
        self.riders = riders
        self.collective_id = riders[0].collective_id
        self.barrier_peers = riders[0].barrier_peers
        self.inputs = [v for r in riders for v in r.inputs]
        self.out_shape = [s for r in riders for s in r.out_shape]
        self.scratch = [s for r in riders for s in r.scratch]
        self.aliases, i0, o0 = {}, 0, 0
        for r in riders:
            self.aliases.update({i0 + i: o0 + o for i, o in getattr(r, "aliases", {}).items()})
            i0, o0 = i0 + len(r.inputs), o0 + len(r.out_shape)

    def _split(self, in_refs, out_refs, sems):
        i0 = o0 = s0 = 0
        for r in self.riders:
            ni, no, ns = len(r.inputs), len(r.out_shape), len(r.scratch)
            yield r, in_refs[i0:i0 + ni], out_refs[o0:o0 + no], sems[s0:s0 + ns]
            i0, o0, s0 = i0 + ni, o0 + no, s0 + ns

    def start(self, in_refs, out_refs, sems):
        for r, i, o, s in self._split(in_refs, out_refs, sems):
            r.start(i, o, s)

    def finish(self, in_refs, out_refs, sems):
        for r, i, o, s in self._split(in_refs, out_refs, sems):
            r.finish(i, o, s)


def _peer_barrier(peers):
    barrier = pltpu.get_barrier_semaphore()
    for peer in peers:
        pl.semaphore_signal(barrier, inc=1, device_id=peer, device_id_type=MESH)
    pl.semaphore_wait(barrier, len(peers))


def hosted_call(body, rider, *, name, grid, in_specs, out_specs, out_shape, scratch_shapes, args, vmem=None):
    n_in, n_out, n_scr = len(in_specs), len(out_specs), len(scratch_shapes)
    r_in, r_out = (len(rider.inputs), len(rider.out_shape)) if rider is not None else (0, 0)
    own_barrier = getattr(rider, "collective_id", None) is not None

    def wrapped(*refs):
        ins, refs = refs[:n_in], refs[n_in:]
        rins, refs = refs[:r_in], refs[r_in:]
        outs, refs = refs[:n_out], refs[n_out:]
        routs, refs = refs[:r_out], refs[r_out:]
        scr, rscr = refs[:n_scr], refs[n_scr:]
        ids = [pl.program_id(d) for d in range(len(grid))]
        first = functools.reduce(jnp.logical_and, [i == 0 for i in ids], True)
        last = functools.reduce(jnp.logical_and, [i == g - 1 for i, g in zip(ids, grid)], True)

        def begin():
            if own_barrier:
                _peer_barrier(rider.barrier_peers())
            rider.start(rins, routs, rscr)

        if rider is not None and grid:
            pl.when(first)(begin)
        elif rider is not None:
            begin()
        if body is not None:
            body(*ins, *outs, *scr)
        if rider is not None and grid:
            pl.when(last)(lambda: rider.finish(rins, routs, rscr))
        elif rider is not None:
            rider.finish(rins, routs, rscr)

    kw = dict(grid=grid) if grid else {}
    aliases = getattr(rider, "aliases", {})
    if aliases:
        kw["input_output_aliases"] = {n_in + i: n_out + o for i, o in aliases.items()}
    if grid or vmem is not None or own_barrier:
        kw["compiler_params"] = _params(("arbitrary",) * len(grid) if grid else None, vmem,
                                        rider.collective_id if own_barrier else None)
    res = _call(
        wrapped, name=name, in_specs=list(in_specs) + _any_specs(r_in), out_specs=list(out_specs) + _any_specs(r_out),
        out_shape=list(out_shape) + (rider.out_shape if rider is not None else []),
        scratch_shapes=list(scratch_shapes) + (rider.scratch if rider is not None else []), **kw,
    )(*args, *(rider.inputs if rider is not None else []))
    return list(res[:n_out]), list(res[n_out:])


def _pick_rows(r, target=256):
    best = None
    for d in range(16, min(r, target) + 1, 16):
        if r % d == 0:
            best = d
    return r if best is None else best


def pair_sum(g, got, *, name):
    _, _, R, C = g.shape
    tr = _pick_rows(R)

    def body(c_ref, g_ref, got_ref, o_ref):
        o_ref[...] = (g_ref[:, 0].astype(F32) + got_ref[...].astype(F32)).astype(o_ref.dtype)

    return _call(
        body, name=name, n_prefetch=1,
        grid_spec=pltpu.PrefetchScalarGridSpec(
            num_scalar_prefetch=1, grid=(R // tr,),
            in_specs=[pl.BlockSpec((4, 1, tr, C), lambda i, c: (0, c[0], i, 0)),
                      pl.BlockSpec((4, tr, C), lambda i, c: (0, i, 0))],
            out_specs=pl.BlockSpec((4, tr, C), lambda i, c: (0, i, 0))),
        out_shape=jax.ShapeDtypeStruct((4, R, C), g.dtype),
        compiler_params=_params(("parallel",)),
    )(lax.axis_index("c").astype(jnp.int32).reshape(1), g, got)


def chip_sum_adamw(p, got, w, m, v, *, name):
    _, R, C = p.shape
    assert w.shape == (1, R, C), (name, w.shape, p.shape)
    tr = _pick_rows(R)

    def body(chip_ref, p_ref, got_ref, w_ref, m_ref, v_ref, g_ref, d_ref, mo_ref, vo_ref):
        my_chip = chip_ref[0]
        g = jnp.zeros((tr, C), F32)
        for j in range(4):
            g = g + jnp.where(my_chip == j, p_ref[0], got_ref[j]).astype(F32)
        g_ref[0] = g
        d_ref[0], mo_ref[0], vo_ref[0] = _adamw_math(w_ref[0], g, m_ref[0], v_ref[0])

    spec = pl.BlockSpec((1, tr, C), lambda i, chip: (0, i, 0))
    return _call(
        body, name=name, n_prefetch=1,
        grid_spec=pltpu.PrefetchScalarGridSpec(
            num_scalar_prefetch=1, grid=(R // tr,),
            in_specs=[pl.BlockSpec((1, tr, C), lambda i, chip: (chip[0], i, 0)),
                      pl.BlockSpec((4, tr, C), lambda i, chip: (0, i, 0)), spec, spec, spec],
            out_specs=[spec] * 4),
        out_shape=[jax.ShapeDtypeStruct((1, R, C), F32)] * 4,
        compiler_params=_params(("parallel",)),
    )((2 * lax.axis_index("x") + lax.axis_index("y")).astype(jnp.int32).reshape(1), p, got, w, m, v)


def rows_sum(g8, *, name):
    _, R, C = g8.shape

    def body(g_ref, o_ref):
        acc = g_ref[0]
        for j in range(1, N_DEV):
            acc = acc + g_ref[j]
        o_ref[...] = acc

    return _call(body, name=name, out_shape=jax.ShapeDtypeStruct((R, C), F32))(g8)


def _adamw_math(w, g, m, v):
    m = ADAM_B1 * m + (1.0 - ADAM_B1) * g
    v = ADAM_B2 * v + (1.0 - ADAM_B2) * (g * g)
    m_hat = m / (1.0 - ADAM_B1 ** ADAM_STEP)
    v_hat = v / (1.0 - ADAM_B2 ** ADAM_STEP)
    delta = -ADAM_LR * (m_hat / (jnp.sqrt(v_hat) + ADAM_EPS) + ADAM_WD * w)
    return delta, m, v


def to_bf16(xs, *, name):
    def body(*refs):
        for x_ref, o_ref in zip(refs[:len(xs)], refs[len(xs):]):
            o_ref[...] = x_ref[...].astype(BF16)

    total = sum(_nbytes(v.shape, F32) + _nbytes(v.shape, BF16) for v in xs)
    return _call(body, name=name, out_shape=[jax.ShapeDtypeStruct(v.shape, BF16) for v in xs],
                 compiler_params=_params(vmem=2 * total + (4 << 20)))(*xs)


def adamw_small(wgmv, *, name):
    n = len(wgmv)

    def body(*refs):
        ins, outs = refs[:4 * n], refs[4 * n:]
        for a in range(n):
            w_ref, g_ref, m_ref, v_ref = ins[4 * a:4 * a + 4]
            d, mn, vn = _adamw_math(w_ref[...], g_ref[...], m_ref[...], v_ref[...])
            outs[3 * a][...] = d
            outs[3 * a + 1][...] = mn
            outs[3 * a + 2][...] = vn

    flat = [t for tup in wgmv for t in tup]
    res = _call(
        body, name=name,
        out_shape=[jax.ShapeDtypeStruct(tup[0].shape, F32) for tup in wgmv for _ in range(3)],
    )(*flat)
    return [tuple(res[3 * a:3 * a + 3]) for a in range(n)]


BIG = ("w_in", "w_out", "w_mq", "w_mkv", "w_mo", "w_gu", "w_down")
COL_SHARDED = ("w_in", "w_mkv", "w_gu")
SMALL = ("g_mix", "b_f", "conv_w", "conv_b", "ln_g", "ln_b", "g_x", "g_mem", "g_ffn", "g_final")


def _full_from_gathered(n, blk):
    _, rr, cc = blk.shape
    if n in COL_SHARDED:
        return join_columns(blk, name="join_" + n)
    return blk.reshape(N_DEV * rr, cc)


def join_columns(blk, *, name, tr=256):
    n, R, w = blk.shape
    tr = min(tr, R)

    def body(b_ref, o_ref):
        for k in range(n):
            o_ref[:, pl.ds(k * w, w)] = b_ref[k]

    return _call(
        body, name=name, grid=(R // tr,),
        in_specs=[pl.BlockSpec((n, tr, w), lambda r: (0, r, 0))],
        out_specs=pl.BlockSpec((tr, n * w), lambda r: (r, 0)),
        out_shape=jax.ShapeDtypeStruct((R, n * w), blk.dtype),
        compiler_params=_params(("parallel",)),
    )(blk)


def _shards_from_full(n, g):
    pieces = g if isinstance(g, list) else [g]
    rr, cc = pieces[0].shape[0], sum(p.shape[1] for p in pieces)
    if n in COL_SHARDED:
        return split_columns(pieces, name="shards_" + n).reshape(4, 2, rr, cc // N_DEV)
    return pieces[0].reshape(4, 2, rr // N_DEV, cc)


def split_columns(pieces, *, name, tr=256):
    R = pieces[0].shape[0]
    w = sum(p.shape[1] for p in pieces) // N_DEV
    tr = min(tr, R)
    moves, c0 = [], 0
    for i, p in enumerate(pieces):
        for k in range(N_DEV):
            lo, hi = max(k * w, c0), min((k + 1) * w, c0 + p.shape[1])
            if lo < hi:
                moves.append((k, i, lo - c0, hi - c0, lo - k * w))
        c0 += p.shape[1]

    def body(*refs):
        o_ref = refs[-1]
        for k, i, lo, hi, off in moves:
            o_ref[k, :, pl.ds(off, hi - lo)] = refs[i][:, pl.ds(lo, hi - lo)]

    return _call(
        body, name=name, grid=(R // tr,),
        in_specs=[pl.BlockSpec((tr, p.shape[1]), lambda r: (r, 0)) for p in pieces],
        out_specs=pl.BlockSpec((N_DEV, tr, w), lambda r: (0, r, 0)),
        out_shape=jax.ShapeDtypeStruct((N_DEV, R, w), pieces[0].dtype),
        compiler_params=_params(("parallel",)),
    )(*pieces)


def _small_layout():
    sizes = dict(g_mix=1024, b_f=8, conv_w=CONV_K * CONV_CH, conv_b=512, ln_g=512, ln_b=512, g_x=1024,
                 g_mem=1024, g_ffn=1024, g_final=1024, loss=1)
    lay, r0 = {}, 0
    for n, sz in sizes.items():
        r = -(-sz // LANES)
        lay[n] = (r0, r, sz)
        r0 += r
    return lay, -(-r0 // 8) * 8


def kernel(x, mem, g_mix, w_in, b_f, conv_w, conv_b, ln_g, ln_b, w_out, g_x, g_mem, w_mq, w_mkv, w_mo, g_ffn, w_gu, w_down, g_final, loss_target, m_g_mix, m_w_in, m_b_f, m_conv_w, m_conv_b, m_ln_g, m_ln_b, m_w_out, m_g_x, m_g_mem, m_w_mq, m_w_mkv, m_w_mo, m_g_ffn, m_w_gu, m_w_down, m_g_final, v_g_mix, v_w_in, v_b_f, v_conv_w, v_conv_b, v_ln_g, v_ln_b, v_w_out, v_g_x, v_g_mem, v_w_mq, v_w_mkv, v_w_mo, v_g_ffn, v_w_gu, v_w_down, v_g_final):
    names = ["g_mix", "w_in", "b_f", "conv_w", "conv_b", "ln_g", "ln_b", "w_out", "g_x", "g_mem", "w_mq",
             "w_mkv", "w_mo", "g_ffn", "w_gu", "w_down", "g_final"]
    W = dict(zip(names, [g_mix, w_in, b_f, conv_w, conv_b, ln_g, ln_b, w_out, g_x, g_mem, w_mq, w_mkv, w_mo,
                         g_ffn, w_gu, w_down, g_final]))
    Mo = dict(zip(names, [m_g_mix, m_w_in, m_b_f, m_conv_w, m_conv_b, m_ln_g, m_ln_b, m_w_out, m_g_x, m_g_mem,
                          m_w_mq, m_w_mkv, m_w_mo, m_g_ffn, m_w_gu, m_w_down, m_g_final]))
    Vo = dict(zip(names, [v_g_mix, v_w_in, v_b_f, v_conv_w, v_conv_b, v_ln_g, v_ln_b, v_w_out, v_g_x, v_g_mem,
                          v_w_mq, v_w_mkv, v_w_mo, v_g_ffn, v_w_gu, v_w_down, v_g_final]))
    dev = 4 * lax.axis_index("x") + 2 * lax.axis_index("y") + lax.axis_index("c")

    two = lambda a: a.reshape(-1, a.shape[-1])
    cw_shard = jnp.pad(two(conv_w), ((0, HALO - CONV_K), (0, 0)))
    sp = dict(g_mix=g_mix, b_f=b_f, conv_b=conv_b, ln_g=ln_g, ln_b=ln_b, g_x=g_x, g_mem=g_mem,
              g_ffn=g_ffn, g_final=g_final)
    shards = to_bf16([two(W[n]) for n in ("w_in",) + LATE], name="cast_shards")
    loss_blk, grad_x, gs, reduced = local_step(x, mem, loss_target, sp, [shards[0], cw_shard], shards[1:])

    lay, rs = _small_layout()
    small = {**{n: gs[n] for n in SMALL}, "loss": loss_blk[:, :1]}
    parts = []
    for n, (r0, r, sz) in lay.items():
        flat = small[n].reshape(-1).astype(F32)
        parts.append(jnp.pad(flat, (0, r * LANES - sz)).reshape(r, LANES))
    spack = jnp.concatenate(parts, axis=0)
    spack = jnp.pad(spack, ((0, rs - spack.shape[0]), (0, 0)))
    ssum = rows_sum(all_gather([spack], name="ag_small")[0], name="small_sum")
    gsmall = {n: ssum[r0:r0 + r].reshape(-1)[:sz] for n, (r0, r, sz) in lay.items()}
    loss = gsmall["loss"].reshape(())

    grads, delta, new_m, new_v = {}, {}, {}, {}
    for n in BIG:
        p, o = reduced[n]
        grads[n], delta[n], new_m[n], new_v[n] = chip_sum_adamw(p, o, W[n], Mo[n], Vo[n], name="adamw_" + n)
    for n in SMALL:
        if n == "conv_w":
            full = gsmall[n].reshape(CONV_K, CONV_CH)
            ncol = conv_w.shape[-1]
            grads[n] = lax.dynamic_slice(full, (0, dev * ncol), (CONV_K, ncol)).reshape(conv_w.shape)
        else:
            grads[n] = gsmall[n].reshape(W[n].shape)
    upd = adamw_small([(two(W[n]), two(grads[n]), two(Mo[n]), two(Vo[n])) for n in SMALL], name="adamw_small")
    for n, (d, mn, vn) in zip(SMALL, upd):
        shp = W[n].shape
        delta[n], new_m[n], new_v[n] = d.reshape(shp), mn.reshape(shp), vn.reshape(shp)
    return (loss, grad_x, *[grads[n] for n in names], *[delta[n] for n in names],
            *[new_m[n] for n in names], *[new_v[n] for n in names])
```
